```python
import jax, jax.numpy as jnp
from jax import lax
import numpy as np

D_MODEL = 1024
BATCH = 8
SEQ = 8192
DEPTH = 2

CHUNK = 64
EPS = 1e-6
HEAD_DIM = 64
H_A = D_MODEL // (2 * HEAD_DIM)
A_PREV_CHUNKS = 8
MAX_REL_DIST = 256
H_B = D_MODEL // (2 * HEAD_DIM)
H_B_KV = H_B // 4
B_WINDOW = 128
B_PREV_CHUNKS = B_WINDOW // CHUNK
ATTN_PROJ = 3 * H_A * HEAD_DIM + H_B * HEAD_DIM + 2 * H_B_KV * HEAD_DIM
D_INNER = 2 * D_MODEL
SSM_HEAD_DIM = 64
SSM_HEADS = D_INNER // SSM_HEAD_DIM
SSM_GROUPS = 4
SSM_STATE = 128
SSM_CONV = 4
SSD_CHUNK = 64
SSM_CONV_CH = D_INNER + 2 * SSM_GROUPS * SSM_STATE
SSM_PROJ = D_INNER + SSM_CONV_CH + SSM_HEADS
D_FF = ((8 * D_MODEL // 3 + 127) // 128) * 128
FFN_CONV = 3

kernel_name = "chunk_causal_hybrid_attn_ssd_convffn"


def rmsnorm(x, g):
    xf = x.astype(jnp.float32)
    y = xf * lax.rsqrt(jnp.mean(xf * xf, axis=-1, keepdims=True) + EPS)
    return (y * g.astype(jnp.float32)).astype(x.dtype)


def causal_dwconv(x, w, b):
    k = w.shape[0]
    y = lax.conv_general_dilated(
        x, w[:, None, :].astype(x.dtype), window_strides=(1,), padding=[(k - 1, 0)],
        dimension_numbers=("NWC", "WIO", "NWC"), feature_group_count=x.shape[-1])
    return y + b.astype(x.dtype)


def band_offsets(n_prev):
    band = (n_prev + 1) * CHUNK
    q_off = jnp.arange(CHUNK, dtype=jnp.int32)
    k_off = jnp.arange(band, dtype=jnp.int32) - n_prev * CHUNK
    return q_off[:, None] - k_off[None, :], k_off


def band_attention(q, k, v, n_prev, bias, sinks):
    b, s, hq, d = q.shape
    hkv = k.shape[2]
    grp = hq // hkv
    nc = s // CHUNK
    band = (n_prev + 1) * CHUNK
    pad = n_prev * CHUNK
    kp = jnp.pad(k, ((0, 0), (pad, 0), (0, 0), (0, 0)))
    vp = jnp.pad(v, ((0, 0), (pad, 0), (0, 0), (0, 0)))
    qc = jnp.moveaxis(q.reshape(b, nc, CHUNK, hkv, grp, d), 1, 0)
    _, k_off = band_offsets(n_prev)
    scale = d ** -0.5

    def one_chunk(args):
        c, qb = args
        start = c * CHUNK
        kb = lax.dynamic_slice_in_dim(kp, start, band, axis=1)
        vb = lax.dynamic_slice_in_dim(vp, start, band, axis=1)
        sc = jnp.einsum("bqkgd,bskd->bkgqs", qb, kb).astype(jnp.float32) * scale + bias
        valid = (start + k_off) >= 0
        sc = jnp.where(valid, sc, -jnp.inf)
        if sinks is None:
            p = jax.nn.softmax(sc, axis=-1)
        else:
            snk = sinks.astype(jnp.float32)[None, :, :, None, None]
            m = jnp.maximum(jnp.max(sc, axis=-1, keepdims=True), snk)
            e = jnp.exp(sc - m)
            p = e / (jnp.sum(e, axis=-1, keepdims=True) + jnp.exp(snk - m))
        return jnp.einsum("bkgqs,bskd->bqkgd", p.astype(vb.dtype), vb)

    out = lax.map(one_chunk, (jnp.arange(nc, dtype=jnp.int32), qc))
    return jnp.moveaxis(out, 0, 1).reshape(b, s, hq * d)


def attn_layer(h, w_in, w_out, relpos_table, q_norm_a, k_norm_a, q_norm_b, k_norm_b, sinks):
    b, s, _ = h.shape
    da, db, dkv = H_A * HEAD_DIM, H_B * HEAD_DIM, H_B_KV * HEAD_DIM
    cuts = [da, 2 * da, 3 * da, 3 * da + db, 3 * da + db + dkv]
    qa, ka, va, qb, kb, vb = jnp.split(h @ w_in, cuts, axis=-1)
    heads = lambda t, n: t.reshape(b, s, n, HEAD_DIM)
    qa = rmsnorm(heads(qa, H_A), q_norm_a)
    ka = rmsnorm(heads(ka, H_A), k_norm_a)
    rel_a, _ = band_offsets(A_PREV_CHUNKS)
    idx = jnp.clip(rel_a, -MAX_REL_DIST, MAX_REL_DIST) + MAX_REL_DIST
    bias_a = relpos_table.astype(jnp.float32)[:, idx][:, None]
    oa = band_attention(qa, ka, heads(va, H_A), A_PREV_CHUNKS, bias_a, None)
    qb = rmsnorm(heads(qb, H_B), q_norm_b)
    kb = rmsnorm(heads(kb, H_B_KV), k_norm_b)
    rel_b, _ = band_offsets(B_PREV_CHUNKS)
    slopes = 2.0 ** (-8.0 * jnp.arange(1, H_B + 1, dtype=jnp.float32) / H_B)
    bias_b = (-slopes[:, None, None] * jnp.abs(rel_b).astype(jnp.float32)).reshape(
        H_B_KV, H_B // H_B_KV, CHUNK, (B_PREV_CHUNKS + 1) * CHUNK)
    ob = band_attention(qb, kb, heads(vb, H_B_KV), B_PREV_CHUNKS, bias_b,
                        sinks.reshape(H_B_KV, H_B // H_B_KV))
    return jnp.concatenate([oa, ob], axis=-1) @ w_out


def ssd_scan(x, dt, a, bm, cm):
    b, s, h, p = x.shape
    g, n = bm.shape[2], bm.shape[3]
    r = h // g
    L = SSD_CHUNK
    nc = s // L

    def to_chunks(t):
        return jnp.moveaxis(t.reshape((b, nc, L) + t.shape[2:]), 1, 0)

    xc = to_chunks(x.reshape(b, s, g, r, p))
    dtc = to_chunks(dt.reshape(b, s, g, r))
    bc, cc = to_chunks(bm), to_chunks(cm)
    a = a.reshape(g, r)
    causal = jnp.tril(jnp.ones((L, L), dtype=bool))[None, :, :, None, None]

    def step(state, inp):
        xk, dtk, bk, ck = inp
        acs = jnp.cumsum(dtk * a, axis=1)
        seg = acs[:, :, None] - acs[:, None, :]
        decay = jnp.exp(jnp.where(causal, seg, -jnp.inf))
        cb = jnp.einsum("blgn,bsgn->bgls", ck, bk)
        y_intra = jnp.einsum("bgls,blsgr,bsgrp->blgrp", cb, decay, xk * dtk[..., None])
        y_state = jnp.einsum("blgn,bgrpn->blgrp", ck, state) * jnp.exp(acs)[..., None]
        last = acs[:, -1]
        w_in = jnp.exp(last[:, None] - acs) * dtk
        new_state = state * jnp.exp(last)[..., None, None] + jnp.einsum(
            "bsgn,bsgr,bsgrp->bgrpn", bk, w_in, xk)
        return new_state, y_intra + y_state

    state0 = jnp.zeros((b, g, r, p, n), jnp.float32)
    _, ys = lax.scan(step, state0, (xc, dtc, bc, cc))
    return jnp.moveaxis(ys, 0, 1).reshape(b, s, h, p)


def ssm_layer(h, w_in, conv_w, conv_b, dt_bias, a_log, d_skip, norm_w, w_out):
    b, s, _ = h.shape
    z, xbc, dt = jnp.split(h @ w_in, [D_INNER, D_INNER + SSM_CONV_CH], axis=-1)
    xbc = jax.nn.silu(causal_dwconv(xbc, conv_w, conv_b))
    xs, bm, cm = jnp.split(xbc, [D_INNER, D_INNER + SSM_GROUPS * SSM_STATE], axis=-1)
    xs = xs.reshape(b, s, SSM_HEADS, SSM_HEAD_DIM).astype(jnp.float32)
    bm = bm.reshape(b, s, SSM_GROUPS, SSM_STATE).astype(jnp.float32)
    cm = cm.reshape(b, s, SSM_GROUPS, SSM_STATE).astype(jnp.float32)
    dt = jax.nn.softplus(dt.astype(jnp.float32) + dt_bias.astype(jnp.float32))
    a = -jnp.exp(a_log.astype(jnp.float32))
    y = ssd_scan(xs, dt, a, bm, cm) + d_skip.astype(jnp.float32)[:, None] * xs
    y = y.reshape(b, s, D_INNER) * jax.nn.silu(z.astype(jnp.float32))
    yg = y.reshape(b, s, SSM_GROUPS, D_INNER // SSM_GROUPS)
    yg = yg * lax.rsqrt(jnp.mean(yg * yg, axis=-1, keepdims=True) + EPS)
    y = (yg.reshape(b, s, D_INNER) * norm_w.astype(jnp.float32)).astype(h.dtype)
    return y @ w_out


def conv_ffn(h, w_in, conv_w, conv_b, w_out):
    gate, up = jnp.split(h @ w_in, [D_FF], axis=-1)
    gate = causal_dwconv(gate, conv_w, conv_b)
    return (jax.nn.silu(gate) * up) @ w_out


def _fwd_setup_inputs(seed: int = 0) -> dict:
    key = jax.random.key(seed)
    ks = jax.random.split(key, 24)
    n_even, n_odd = (DEPTH + 1) // 2, DEPTH // 2
    nrm = lambda k, shape, scale: jax.random.normal(k, shape, jnp.float32) * scale
    dt0 = jnp.exp(jax.random.uniform(ks[15], (n_odd, SSM_HEADS), jnp.float32,
                                     np.log(1e-3), np.log(1e-1)))
    return {
        "x": nrm(ks[0], (BATCH, SEQ, D_MODEL), 1.0),
        "norm_mix": 1.0 + nrm(ks[1], (DEPTH, D_MODEL), 0.05),
        "norm_ffn": 1.0 + nrm(ks[2], (DEPTH, D_MODEL), 0.05),
        "attn_w_in": nrm(ks[3], (n_even, D_MODEL, ATTN_PROJ), D_MODEL ** -0.5),
        "attn_w_out": nrm(ks[4], (n_even, (H_A + H_B) * HEAD_DIM, D_MODEL), ((H_A + H_B) * HEAD_DIM) ** -0.5),
        "relpos_table": nrm(ks[5], (n_even, H_A, 2 * MAX_REL_DIST + 1), 0.2),
        "q_norm_a": 1.0 + nrm(ks[6], (n_even, HEAD_DIM), 0.05),
        "k_norm_a": 1.0 + nrm(ks[7], (n_even, HEAD_DIM), 0.05),
        "q_norm_b": 1.0 + nrm(ks[8], (n_even, HEAD_DIM), 0.05),
        "k_norm_b": 1.0 + nrm(ks[9], (n_even, HEAD_DIM), 0.05),
        "sinks": nrm(ks[10], (n_even, H_B), 0.5),
        "ssm_w_in": nrm(ks[11], (n_odd, D_MODEL, SSM_PROJ), D_MODEL ** -0.5),
        "ssm_conv_w": nrm(ks[12], (n_odd, SSM_CONV, SSM_CONV_CH), SSM_CONV ** -0.5),
        "ssm_conv_b": nrm(ks[13], (n_odd, SSM_CONV_CH), 0.02),
        "ssm_dt_bias": dt0 + jnp.log(-jnp.expm1(-dt0)),
        "ssm_a_log": jnp.log(jax.random.uniform(ks[14], (n_odd, SSM_HEADS), jnp.float32, 1.0, 16.0)),
        "ssm_d": 1.0 + nrm(ks[16], (n_odd, SSM_HEADS), 0.1),
        "ssm_norm": 1.0 + nrm(ks[17], (n_odd, D_INNER), 0.05),
        "ssm_w_out": nrm(ks[18], (n_odd, D_INNER, D_MODEL), D_INNER ** -0.5),
        "ffn_w_in": nrm(ks[19], (DEPTH, D_MODEL, 2 * D_FF), D_MODEL ** -0.5),
        "ffn_conv_w": nrm(ks[20], (DEPTH, FFN_CONV, D_FF), FFN_CONV ** -0.5),
        "ffn_conv_b": nrm(ks[21], (DEPTH, D_FF), 0.02),
        "ffn_w_out": nrm(ks[22], (DEPTH, D_FF, D_MODEL), D_FF ** -0.5),
    }


def _fwd_reference(x, norm_mix, norm_ffn, attn_w_in, attn_w_out, relpos_table, q_norm_a, k_norm_a,
              q_norm_b, k_norm_b, sinks, ssm_w_in, ssm_conv_w, ssm_conv_b, ssm_dt_bias, ssm_a_log,
              ssm_d, ssm_norm, ssm_w_out, ffn_w_in, ffn_conv_w, ffn_conv_b, ffn_w_out):
    for layer in range(DEPTH):
        i = layer // 2
        h = rmsnorm(x, norm_mix[layer])
        if layer % 2 == 0:
            mix = attn_layer(h, attn_w_in[i], attn_w_out[i], relpos_table[i], q_norm_a[i],
                             k_norm_a[i], q_norm_b[i], k_norm_b[i], sinks[i])
        else:
            mix = ssm_layer(h, ssm_w_in[i], ssm_conv_w[i], ssm_conv_b[i], ssm_dt_bias[i],
                            ssm_a_log[i], ssm_d[i], ssm_norm[i], ssm_w_out[i])
        x = x + mix
        h = rmsnorm(x, norm_ffn[layer])
        x = x + conv_ffn(h, ffn_w_in[layer], ffn_conv_w[layer], ffn_conv_b[layer], ffn_w_out[layer])
    return x


import jax as _jax
import jax.numpy as _jnp

TWIN_FORMAT = 'train_step'
FWD_PARAMS = ['x', 'norm_mix', 'norm_ffn', 'attn_w_in', 'attn_w_out', 'relpos_table', 'q_norm_a', 'k_norm_a', 'q_norm_b', 'k_norm_b', 'sinks', 'ssm_w_in', 'ssm_conv_w', 'ssm_conv_b', 'ssm_dt_bias', 'ssm_a_log', 'ssm_d', 'ssm_norm', 'ssm_w_out', 'ffn_w_in', 'ffn_conv_w', 'ffn_conv_b', 'ffn_w_out']
TWIN_WEIGHTS = ['norm_mix', 'norm_ffn', 'attn_w_in', 'attn_w_out', 'relpos_table', 'q_norm_a', 'k_norm_a', 'q_norm_b', 'k_norm_b', 'sinks', 'ssm_w_in', 'ssm_conv_w', 'ssm_conv_b', 'ssm_dt_bias', 'ssm_a_log', 'ssm_d', 'ssm_norm', 'ssm_w_out', 'ffn_w_in', 'ffn_conv_w', 'ffn_conv_b', 'ffn_w_out']
TWIN_DIFF_INPUT = 'x'
TWIN_INPUTS = ['x', 'norm_mix', 'norm_ffn', 'attn_w_in', 'attn_w_out', 'relpos_table', 'q_norm_a', 'k_norm_a', 'q_norm_b', 'k_norm_b', 'sinks', 'ssm_w_in', 'ssm_conv_w', 'ssm_conv_b', 'ssm_dt_bias', 'ssm_a_log', 'ssm_d', 'ssm_norm', 'ssm_w_out', 'ffn_w_in', 'ffn_conv_w', 'ffn_conv_b', 'ffn_w_out', 'loss_target', 'm_norm_mix', 'm_norm_ffn', 'm_attn_w_in', 'm_attn_w_out', 'm_relpos_table', 'm_q_norm_a', 'm_k_norm_a', 'm_q_norm_b', 'm_k_norm_b', 'm_sinks', 'm_ssm_w_in', 'm_ssm_conv_w', 'm_ssm_conv_b', 'm_ssm_dt_bias', 'm_ssm_a_log', 'm_ssm_d', 'm_ssm_norm', 'm_ssm_w_out', 'm_ffn_w_in', 'm_ffn_conv_w', 'm_ffn_conv_b', 'm_ffn_w_out', 'v_norm_mix', 'v_norm_ffn', 'v_attn_w_in', 'v_attn_w_out', 'v_relpos_table', 'v_q_norm_a', 'v_k_norm_a', 'v_q_norm_b', 'v_k_norm_b', 'v_sinks', 'v_ssm_w_in', 'v_ssm_conv_w', 'v_ssm_conv_b', 'v_ssm_dt_bias', 'v_ssm_a_log', 'v_ssm_d', 'v_ssm_norm', 'v_ssm_w_out', 'v_ffn_w_in', 'v_ffn_conv_w', 'v_ffn_conv_b', 'v_ffn_w_out']
TWIN_OUTPUTS = ['loss', 'grad_x', 'grad_norm_mix', 'grad_norm_ffn', 'grad_attn_w_in', 'grad_attn_w_out', 'grad_relpos_table', 'grad_q_norm_a', 'grad_k_norm_a', 'grad_q_norm_b', 'grad_k_norm_b', 'grad_sinks', 'grad_ssm_w_in', 'grad_ssm_conv_w', 'grad_ssm_conv_b', 'grad_ssm_dt_bias', 'grad_ssm_a_log', 'grad_ssm_d', 'grad_ssm_norm', 'grad_ssm_w_out', 'grad_ffn_w_in', 'grad_ffn_conv_w', 'grad_ffn_conv_b', 'grad_ffn_w_out', 'delta_norm_mix', 'delta_norm_ffn', 'delta_attn_w_in', 'delta_attn_w_out', 'delta_relpos_table', 'delta_q_norm_a', 'delta_k_norm_a', 'delta_q_norm_b', 'delta_k_norm_b', 'delta_sinks', 'delta_ssm_w_in', 'delta_ssm_conv_w', 'delta_ssm_conv_b', 'delta_ssm_dt_bias', 'delta_ssm_a_log', 'delta_ssm_d', 'delta_ssm_norm', 'delta_ssm_w_out', 'delta_ffn_w_in', 'delta_ffn_conv_w', 'delta_ffn_conv_b', 'delta_ffn_w_out', 'new_m_norm_mix', 'new_m_norm_ffn', 'new_m_attn_w_in', 'new_m_attn_w_out', 'new_m_relpos_table', 'new_m_q_norm_a', 'new_m_k_norm_a', 'new_m_q_norm_b', 'new_m_k_norm_b', 'new_m_sinks', 'new_m_ssm_w_in', 'new_m_ssm_conv_w', 'new_m_ssm_conv_b', 'new_m_ssm_dt_bias', 'new_m_ssm_a_log', 'new_m_ssm_d', 'new_m_ssm_norm', 'new_m_ssm_w_out', 'new_m_ffn_w_in', 'new_m_ffn_conv_w', 'new_m_ffn_conv_b', 'new_m_ffn_w_out', 'new_v_norm_mix', 'new_v_norm_ffn', 'new_v_attn_w_in', 'new_v_attn_w_out', 'new_v_relpos_table', 'new_v_q_norm_a', 'new_v_k_norm_a', 'new_v_q_norm_b', 'new_v_k_norm_b', 'new_v_sinks', 'new_v_ssm_w_in', 'new_v_ssm_conv_w', 'new_v_ssm_conv_b', 'new_v_ssm_dt_bias', 'new_v_ssm_a_log', 'new_v_ssm_d', 'new_v_ssm_norm', 'new_v_ssm_w_out', 'new_v_ffn_w_in', 'new_v_ffn_conv_w', 'new_v_ffn_conv_b', 'new_v_ffn_w_out']
TWIN_LEAF_KINDS = {'loss': 'loss', 'grad_x': 'grad_x', 'grad_norm_mix': 'grad_w', 'grad_norm_ffn': 'grad_w', 'grad_attn_w_in': 'grad_w', 'grad_attn_w_out': 'grad_w', 'grad_relpos_table': 'grad_w', 'grad_q_norm_a': 'grad_w', 'grad_k_norm_a': 'grad_w', 'grad_q_norm_b': 'grad_w', 'grad_k_norm_b': 'grad_w', 'grad_sinks': 'grad_w', 'grad_ssm_w_in': 'grad_w', 'grad_ssm_conv_w': 'grad_w', 'grad_ssm_conv_b': 'grad_w', 'grad_ssm_dt_bias': 'grad_w', 'grad_ssm_a_log': 'grad_w', 'grad_ssm_d': 'grad_w', 'grad_ssm_norm': 'grad_w', 'grad_ssm_w_out': 'grad_w', 'grad_ffn_w_in': 'grad_w', 'grad_ffn_conv_w': 'grad_w', 'grad_ffn_conv_b': 'grad_w', 'grad_ffn_w_out': 'grad_w', 'delta_norm_mix': 'delta_w', 'delta_norm_ffn': 'delta_w', 'delta_attn_w_in': 'delta_w', 'delta_attn_w_out': 'delta_w', 'delta_relpos_table': 'delta_w', 'delta_q_norm_a': 'delta_w', 'delta_k_norm_a': 'delta_w', 'delta_q_norm_b': 'delta_w', 'delta_k_norm_b': 'delta_w', 'delta_sinks': 'delta_w', 'delta_ssm_w_in': 'delta_w', 'delta_ssm_conv_w': 'delta_w', 'delta_ssm_conv_b': 'delta_w', 'delta_ssm_dt_bias': 'delta_w', 'delta_ssm_a_log': 'delta_w', 'delta_ssm_d': 'delta_w', 'delta_ssm_norm': 'delta_w', 'delta_ssm_w_out': 'delta_w', 'delta_ffn_w_in': 'delta_w', 'delta_ffn_conv_w': 'delta_w', 'delta_ffn_conv_b': 'delta_w', 'delta_ffn_w_out': 'delta_w', 'new_m_norm_mix': 'new_m', 'new_m_norm_ffn': 'new_m', 'new_m_attn_w_in': 'new_m', 'new_m_attn_w_out': 'new_m', 'new_m_relpos_table': 'new_m', 'new_m_q_norm_a': 'new_m', 'new_m_k_norm_a': 'new_m', 'new_m_q_norm_b': 'new_m', 'new_m_k_norm_b': 'new_m', 'new_m_sinks': 'new_m', 'new_m_ssm_w_in': 'new_m', 'new_m_ssm_conv_w': 'new_m', 'new_m_ssm_conv_b': 'new_m', 'new_m_ssm_dt_bias': 'new_m', 'new_m_ssm_a_log': 'new_m', 'new_m_ssm_d': 'new_m', 'new_m_ssm_norm': 'new_m', 'new_m_ssm_w_out': 'new_m', 'new_m_ffn_w_in': 'new_m', 'new_m_ffn_conv_w': 'new_m', 'new_m_ffn_conv_b': 'new_m', 'new_m_ffn_w_out': 'new_m', 'new_v_norm_mix': 'new_v', 'new_v_norm_ffn': 'new_v', 'new_v_attn_w_in': 'new_v', 'new_v_attn_w_out': 'new_v', 'new_v_relpos_table': 'new_v', 'new_v_q_norm_a': 'new_v', 'new_v_k_norm_a': 'new_v', 'new_v_q_norm_b': 'new_v', 'new_v_k_norm_b': 'new_v', 'new_v_sinks': 'new_v', 'new_v_ssm_w_in': 'new_v', 'new_v_ssm_conv_w': 'new_v', 'new_v_ssm_conv_b': 'new_v', 'new_v_ssm_dt_bias': 'new_v', 'new_v_ssm_a_log': 'new_v', 'new_v_ssm_d': 'new_v', 'new_v_ssm_norm': 'new_v', 'new_v_ssm_w_out': 'new_v', 'new_v_ffn_w_in': 'new_v', 'new_v_ffn_conv_w': 'new_v', 'new_v_ffn_conv_b': 'new_v', 'new_v_ffn_w_out': 'new_v'}


def _forward(args):
    return _fwd_reference(*[args[k] for k in FWD_PARAMS])


def _output_shape():
    out = _jax.eval_shape(lambda: _forward(_fwd_setup_inputs(0)))
    return out.shape, out.dtype

N_MICROBATCH = 1
ADAM_LR = 0.001
ADAM_B1 = 0.9
ADAM_B2 = 0.999
ADAM_EPS = 1e-08
ADAM_WD = 0.01
ADAM_STEP = 10
PER_EXAMPLE_BATCH_AXIS = {'x': 0, 'loss_target': 0}
SHARED_INPUTS = []
_WEIGHT_DTYPES = {'norm_mix': _jnp.float32, 'norm_ffn': _jnp.float32, 'attn_w_in': _jnp.float32, 'attn_w_out': _jnp.float32, 'relpos_table': _jnp.float32, 'q_norm_a': _jnp.float32, 'k_norm_a': _jnp.float32, 'q_norm_b': _jnp.float32, 'k_norm_b': _jnp.float32, 'sinks': _jnp.float32, 'ssm_w_in': _jnp.float32, 'ssm_conv_w': _jnp.float32, 'ssm_conv_b': _jnp.float32, 'ssm_dt_bias': _jnp.float32, 'ssm_a_log': _jnp.float32, 'ssm_d': _jnp.float32, 'ssm_norm': _jnp.float32, 'ssm_w_out': _jnp.float32, 'ffn_w_in': _jnp.float32, 'ffn_conv_w': _jnp.float32, 'ffn_conv_b': _jnp.float32, 'ffn_w_out': _jnp.float32}
MOMENT_SCALE = {'norm_mix': 1.090699e+00, 'norm_ffn': 5.121005e+01, 'attn_w_in': 3.019537e-01, 'attn_w_out': 3.020363e-01, 'relpos_table': 6.494546e-02, 'q_norm_a': 1.255061e+00, 'k_norm_a': 1.263492e+00, 'q_norm_b': 1.398375e+01, 'k_norm_b': 1.368901e+01, 'sinks': 2.676275e+01, 'ssm_w_in': 4.758373e-01, 'ssm_conv_w': 1.439023e+00, 'ssm_conv_b': 4.392523e+00, 'ssm_dt_bias': 2.996578e+00, 'ssm_a_log': 6.789992e+00, 'ssm_d': 7.605344e+00, 'ssm_norm': 4.534847e+01, 'ssm_w_out': 2.660146e+00, 'ffn_w_in': 4.926555e-01, 'ffn_conv_w': 5.451683e+00, 'ffn_conv_b': 6.713819e+00, 'ffn_w_out': 6.629333e-01}


def _to_microbatches(a, axis):
    t = _jnp.moveaxis(a, axis, 0)
    t = t.reshape((N_MICROBATCH, t.shape[0] // N_MICROBATCH) + t.shape[1:])
    return _jnp.moveaxis(t, 1, axis + 1)


def setup_inputs(seed: int = 0) -> dict:
    inp = _fwd_setup_inputs(seed)
    key = _jax.random.fold_in(_jax.random.key(seed), 7919)
    shape, _ = _output_shape()
    out = dict(inp)
    out["loss_target"] = _jax.random.normal(_jax.random.fold_in(key, 0), shape, _jnp.float32)
    for i, name in enumerate(TWIN_WEIGHTS):
        w = inp[name].astype(_jnp.float32)
        if MOMENT_SCALE is None:
            s = _jnp.sqrt(_jnp.mean(_jnp.square(w)) + 1e-30)
        else:
            s = MOMENT_SCALE[name]
        km, kv = _jax.random.split(_jax.random.fold_in(key, i + 1))
        out[name] = w
        out["m_" + name] = s * _jax.random.normal(km, w.shape, _jnp.float32)
        out["v_" + name] = (s * s) * _jax.random.uniform(kv, w.shape, _jnp.float32, 0.5, 1.5)
    if N_MICROBATCH > 1:
        for name, axis in PER_EXAMPLE_BATCH_AXIS.items():
            out[name] = _to_microbatches(out[name], axis)
    return {'x': out['x'], 'norm_mix': out['norm_mix'], 'norm_ffn': out['norm_ffn'], 'attn_w_in': out['attn_w_in'], 'attn_w_out': out['attn_w_out'], 'relpos_table': out['relpos_table'], 'q_norm_a': out['q_norm_a'], 'k_norm_a': out['k_norm_a'], 'q_norm_b': out['q_norm_b'], 'k_norm_b': out['k_norm_b'], 'sinks': out['sinks'], 'ssm_w_in': out['ssm_w_in'], 'ssm_conv_w': out['ssm_conv_w'], 'ssm_conv_b': out['ssm_conv_b'], 'ssm_dt_bias': out['ssm_dt_bias'], 'ssm_a_log': out['ssm_a_log'], 'ssm_d': out['ssm_d'], 'ssm_norm': out['ssm_norm'], 'ssm_w_out': out['ssm_w_out'], 'ffn_w_in': out['ffn_w_in'], 'ffn_conv_w': out['ffn_conv_w'], 'ffn_conv_b': out['ffn_conv_b'], 'ffn_w_out': out['ffn_w_out'], 'loss_target': out['loss_target'], 'm_norm_mix': out['m_norm_mix'], 'm_norm_ffn': out['m_norm_ffn'], 'm_attn_w_in': out['m_attn_w_in'], 'm_attn_w_out': out['m_attn_w_out'], 'm_relpos_table': out['m_relpos_table'], 'm_q_norm_a': out['m_q_norm_a'], 'm_k_norm_a': out['m_k_norm_a'], 'm_q_norm_b': out['m_q_norm_b'], 'm_k_norm_b': out['m_k_norm_b'], 'm_sinks': out['m_sinks'], 'm_ssm_w_in': out['m_ssm_w_in'], 'm_ssm_conv_w': out['m_ssm_conv_w'], 'm_ssm_conv_b': out['m_ssm_conv_b'], 'm_ssm_dt_bias': out['m_ssm_dt_bias'], 'm_ssm_a_log': out['m_ssm_a_log'], 'm_ssm_d': out['m_ssm_d'], 'm_ssm_norm': out['m_ssm_norm'], 'm_ssm_w_out': out['m_ssm_w_out'], 'm_ffn_w_in': out['m_ffn_w_in'], 'm_ffn_conv_w': out['m_ffn_conv_w'], 'm_ffn_conv_b': out['m_ffn_conv_b'], 'm_ffn_w_out': out['m_ffn_w_out'], 'v_norm_mix': out['v_norm_mix'], 'v_norm_ffn': out['v_norm_ffn'], 'v_attn_w_in': out['v_attn_w_in'], 'v_attn_w_out': out['v_attn_w_out'], 'v_relpos_table': out['v_relpos_table'], 'v_q_norm_a': out['v_q_norm_a'], 'v_k_norm_a': out['v_k_norm_a'], 'v_q_norm_b': out['v_q_norm_b'], 'v_k_norm_b': out['v_k_norm_b'], 'v_sinks': out['v_sinks'], 'v_ssm_w_in': out['v_ssm_w_in'], 'v_ssm_conv_w': out['v_ssm_conv_w'], 'v_ssm_conv_b': out['v_ssm_conv_b'], 'v_ssm_dt_bias': out['v_ssm_dt_bias'], 'v_ssm_a_log': out['v_ssm_a_log'], 'v_ssm_d': out['v_ssm_d'], 'v_ssm_norm': out['v_ssm_norm'], 'v_ssm_w_out': out['v_ssm_w_out'], 'v_ffn_w_in': out['v_ffn_w_in'], 'v_ffn_conv_w': out['v_ffn_conv_w'], 'v_ffn_conv_b': out['v_ffn_conv_b'], 'v_ffn_w_out': out['v_ffn_w_out']}


def _loss(weights, diff, rest, loss_target):
    with _jax.named_scope("forward"):
        args = {**rest, TWIN_DIFF_INPUT: diff, **{k: w.astype(_WEIGHT_DTYPES[k]) for k, w in weights.items()}}
        y = _forward(args)
    with _jax.named_scope("loss_head"):
        err = _jnp.square(y.astype(_jnp.float32) - loss_target)
        return 0.5 * _jnp.sum(_jnp.mean(err, axis=-1)) if err.ndim else 0.5 * err


def _adamw(w, g, m, v):
    m = ADAM_B1 * m + (1.0 - ADAM_B1) * g
    v = ADAM_B2 * v + (1.0 - ADAM_B2) * _jnp.square(g)
    m_hat = m / (1.0 - ADAM_B1 ** ADAM_STEP)
    v_hat = v / (1.0 - ADAM_B2 ** ADAM_STEP)
    delta = -ADAM_LR * (m_hat / (_jnp.sqrt(v_hat) + ADAM_EPS) + ADAM_WD * w)
    return delta, m, v


def reference(x, norm_mix, norm_ffn, attn_w_in, attn_w_out, relpos_table, q_norm_a, k_norm_a, q_norm_b, k_norm_b, sinks, ssm_w_in, ssm_conv_w, ssm_conv_b, ssm_dt_bias, ssm_a_log, ssm_d, ssm_norm, ssm_w_out, ffn_w_in, ffn_conv_w, ffn_conv_b, ffn_w_out, loss_target, m_norm_mix, m_norm_ffn, m_attn_w_in, m_attn_w_out, m_relpos_table, m_q_norm_a, m_k_norm_a, m_q_norm_b, m_k_norm_b, m_sinks, m_ssm_w_in, m_ssm_conv_w, m_ssm_conv_b, m_ssm_dt_bias, m_ssm_a_log, m_ssm_d, m_ssm_norm, m_ssm_w_out, m_ffn_w_in, m_ffn_conv_w, m_ffn_conv_b, m_ffn_w_out, v_norm_mix, v_norm_ffn, v_attn_w_in, v_attn_w_out, v_relpos_table, v_q_norm_a, v_k_norm_a, v_q_norm_b, v_k_norm_b, v_sinks, v_ssm_w_in, v_ssm_conv_w, v_ssm_conv_b, v_ssm_dt_bias, v_ssm_a_log, v_ssm_d, v_ssm_norm, v_ssm_w_out, v_ffn_w_in, v_ffn_conv_w, v_ffn_conv_b, v_ffn_w_out):
    given = dict(x=x, norm_mix=norm_mix, norm_ffn=norm_ffn, attn_w_in=attn_w_in, attn_w_out=attn_w_out, relpos_table=relpos_table, q_norm_a=q_norm_a, k_norm_a=k_norm_a, q_norm_b=q_norm_b, k_norm_b=k_norm_b, sinks=sinks, ssm_w_in=ssm_w_in, ssm_conv_w=ssm_conv_w, ssm_conv_b=ssm_conv_b, ssm_dt_bias=ssm_dt_bias, ssm_a_log=ssm_a_log, ssm_d=ssm_d, ssm_norm=ssm_norm, ssm_w_out=ssm_w_out, ffn_w_in=ffn_w_in, ffn_conv_w=ffn_conv_w, ffn_conv_b=ffn_conv_b, ffn_w_out=ffn_w_out, loss_target=loss_target, m_norm_mix=m_norm_mix, m_norm_ffn=m_norm_ffn, m_attn_w_in=m_attn_w_in, m_attn_w_out=m_attn_w_out, m_relpos_table=m_relpos_table, m_q_norm_a=m_q_norm_a, m_k_norm_a=m_k_norm_a, m_q_norm_b=m_q_norm_b, m_k_norm_b=m_k_norm_b, m_sinks=m_sinks, m_ssm_w_in=m_ssm_w_in, m_ssm_conv_w=m_ssm_conv_w, m_ssm_conv_b=m_ssm_conv_b, m_ssm_dt_bias=m_ssm_dt_bias, m_ssm_a_log=m_ssm_a_log, m_ssm_d=m_ssm_d, m_ssm_norm=m_ssm_norm, m_ssm_w_out=m_ssm_w_out, m_ffn_w_in=m_ffn_w_in, m_ffn_conv_w=m_ffn_conv_w, m_ffn_conv_b=m_ffn_conv_b, m_ffn_w_out=m_ffn_w_out, v_norm_mix=v_norm_mix, v_norm_ffn=v_norm_ffn, v_attn_w_in=v_attn_w_in, v_attn_w_out=v_attn_w_out, v_relpos_table=v_relpos_table, v_q_norm_a=v_q_norm_a, v_k_norm_a=v_k_norm_a, v_q_norm_b=v_q_norm_b, v_k_norm_b=v_k_norm_b, v_sinks=v_sinks, v_ssm_w_in=v_ssm_w_in, v_ssm_conv_w=v_ssm_conv_w, v_ssm_conv_b=v_ssm_conv_b, v_ssm_dt_bias=v_ssm_dt_bias, v_ssm_a_log=v_ssm_a_log, v_ssm_d=v_ssm_d, v_ssm_norm=v_ssm_norm, v_ssm_w_out=v_ssm_w_out, v_ffn_w_in=v_ffn_w_in, v_ffn_conv_w=v_ffn_conv_w, v_ffn_conv_b=v_ffn_conv_b, v_ffn_w_out=v_ffn_w_out)
    weights = {n: given[n] for n in TWIN_WEIGHTS}
    shared = {n: given[n] for n in SHARED_INPUTS}
    per_example = {n: given[n] for n in ['x']}
    grad_fn = _jax.value_and_grad(_loss, argnums=(0, 1))

    def one_microbatch(ex, loss_target):
        ex = dict(ex)
        diff = ex.pop(TWIN_DIFF_INPUT)
        return grad_fn(weights, diff, {**shared, **ex}, loss_target)

    if N_MICROBATCH == 1:
        loss, (grad_w, grad_x) = one_microbatch(per_example, given["loss_target"])
    else:
        def body(carry, xs):
            loss_sum, grad_sum = carry
            l_k, (gw_k, gx_k) = one_microbatch(xs[0], xs[1])
            with _jax.named_scope("update"):
                return (loss_sum + l_k, _jax.tree.map(_jnp.add, grad_sum, gw_k)), gx_k

        init = (_jnp.zeros((), _jnp.float32), _jax.tree.map(_jnp.zeros_like, weights))
        (loss, grad_w), grad_x = _jax.lax.scan(body, init, (per_example, given["loss_target"]))
    with _jax.named_scope("update"):
        delta_w, new_m, new_v = {}, {}, {}
        for n in TWIN_WEIGHTS:
            delta_w[n], new_m[n], new_v[n] = _adamw(weights[n], grad_w[n], given["m_" + n], given["v_" + n])
    return (loss, grad_x, *[grad_w[n] for n in TWIN_WEIGHTS], *[delta_w[n] for n in TWIN_WEIGHTS],
            *[new_m[n] for n in TWIN_WEIGHTS], *[new_v[n] for n in TWIN_WEIGHTS])
```

```python
import jax
import jax.numpy as jnp
from jax import lax
from jax.experimental import pallas as pl
from jax.experimental.pallas import tpu as pltpu

F32 = jnp.float32
BF16 = jnp.bfloat16
HI = lax.Precision.HIGHEST
MESH = pl.DeviceIdType.MESH
NEG = -1e30

N_DEV = 8
D_MODEL = 1024
EPS = 1e-6
CHUNK = 64
HEAD_DIM = 64
N_HEADS = 8
A_PREV = 8
B_PREV = 2
MAX_REL = 256
TQ = 2 * CHUNK
PAD_A = A_PREV * CHUNK
PAD_B = B_PREV * CHUNK
REL_W = PAD_A + TQ
D_INNER = 2048
SSM_HEADS = 32
SSM_GROUPS = 4
SSM_STATE = 128
XBC = D_INNER + 2 * SSM_GROUPS * SSM_STATE
ZX = D_INNER + XBC
D_FF = 2816
SSD_L = 128
LANES = 128
VMEM_LIMIT = 48 << 20

ADAM_LR, ADAM_B1, ADAM_B2, ADAM_EPS, ADAM_WD, ADAM_STEP = 0.001, 0.9, 0.999, 1e-08, 0.01, 10


def _cp(sem=None):
    return pltpu.CompilerParams(dimension_semantics=sem, vmem_limit_bytes=VMEM_LIMIT)


def _dot(a, b, ca=1, cb=0, prec=None):
    return lax.dot_general(a, b, (((ca,), (cb,)), ((), ())), preferred_element_type=F32, precision=prec)


def _pick(n, cands):
    for c in cands:
        if n % c == 0:
            return c
    return n


def _lo_mask():
    return lax.broadcasted_iota(jnp.int32, (1, LANES), 1) < HEAD_DIM


def _mm(a, b, name, out_dtype=F32, res=None, trans_a=False):
    if trans_a:
        kdim, m = a.shape
    else:
        m, kdim = a.shape
    n = b.shape[1]
    assert b.shape[0] == kdim, (a.shape, b.shape)
    tm = _pick(m, (512, 1408, 256, 128))
    tn = _pick(n, (1024, 1152, 1408, 1280, 768, 512, 256, 128))
    if trans_a:
        tk = _pick(kdim, (512, 256, 128))
    else:
        tk = kdim if kdim <= 2048 else _pick(kdim, (1408, 1536, 1152, 1024, 512))
    nk = kdim // tk

    def body(*refs):
        if res is None:
            a_ref, b_ref, o_ref, acc = refs
        else:
            a_ref, b_ref, r_ref, o_ref, acc = refs
        k = pl.program_id(2)

        @pl.when(k == 0)
        def _():
            acc[...] = jnp.zeros_like(acc)

        acc[...] += _dot(a_ref[...], b_ref[...], 0 if trans_a else 1, 0)

        @pl.when(k == nk - 1)
        def _():
            r = acc[...]
            if res is not None:
                r = r + r_ref[...]
            o_ref[...] = r.astype(out_dtype)

    a_spec = pl.BlockSpec((tk, tm), lambda i, j, k: (k, i)) if trans_a else pl.BlockSpec((tm, tk), lambda i, j, k: (i, k))
    in_specs = [a_spec, pl.BlockSpec((tk, tn), lambda i, j, k: (k, j))]
    args = [a, b]
    if res is not None:
        in_specs.append(pl.BlockSpec((tm, tn), lambda i, j, k: (i, j)))
        args.append(res)
    return pl.pallas_call(
        body, name=name, grid=(m // tm, n // tn, nk), in_specs=in_specs,
        out_specs=pl.BlockSpec((tm, tn), lambda i, j, k: (i, j)),
        out_shape=jax.ShapeDtypeStruct((m, n), out_dtype),
        scratch_shapes=[pltpu.VMEM((tm, tn), F32)],
        compiler_params=_cp(("parallel", "parallel", "arbitrary")),
    )(*args)


def _rms_fwd(x, g, name):
    t, d = x.shape
    tm = _pick(t, (512, 256, 128))

    def body(x_ref, g_ref, h_ref):
        xv = x_ref[...]
        r = lax.rsqrt(jnp.mean(xv * xv, axis=-1, keepdims=True) + EPS)
        h_ref[...] = (xv * r * g_ref[...]).astype(BF16)

    return pl.pallas_call(
        body, name=name, grid=(t // tm,),
        in_specs=[pl.BlockSpec((tm, d), lambda i: (i, 0)), pl.BlockSpec((1, d), lambda i: (0, 0))],
        out_specs=pl.BlockSpec((tm, d), lambda i: (i, 0)),
        out_shape=jax.ShapeDtypeStruct((t, d), BF16), compiler_params=_cp(("parallel",)),
    )(x, g)


def _rms_bwd(x, g, dh, dres, name):
    t, d = x.shape
    tm = _pick(t, (512, 256, 128))

    def body(x_ref, g_ref, dh_ref, dr_ref, dx_ref, dxb_ref, dg_ref):
        i = pl.program_id(0)
        xv = x_ref[...]
        r = lax.rsqrt(jnp.mean(xv * xv, axis=-1, keepdims=True) + EPS)
        xh = xv * r
        dhv = dh_ref[...]
        dxh = dhv * g_ref[...]
        dx = dr_ref[...] + r * (dxh - xh * jnp.mean(dxh * xh, axis=-1, keepdims=True))
        dx_ref[...] = dx
        dxb_ref[...] = dx.astype(BF16)

        @pl.when(i == 0)
        def _():
            dg_ref[...] = jnp.zeros_like(dg_ref)

        dg_ref[...] += jnp.sum(dhv * xh, axis=0, keepdims=True)

    row = pl.BlockSpec((tm, d), lambda i: (i, 0))
    vec = pl.BlockSpec((1, d), lambda i: (0, 0))
    return pl.pallas_call(
        body, name=name, grid=(t // tm,), in_specs=[row, vec, row, row], out_specs=[row, row, vec],
        out_shape=[jax.ShapeDtypeStruct((t, d), F32), jax.ShapeDtypeStruct((t, d), BF16), jax.ShapeDtypeStruct((1, d), F32)],
        compiler_params=_cp(("arbitrary",)),
    )(x, g, dh, dres)


def _head_rms(xs, w, lo):
    sq = xs * xs
    s0 = jnp.sum(jnp.where(lo, sq, 0.0), axis=-1, keepdims=True)
    s1 = jnp.sum(jnp.where(lo, 0.0, sq), axis=-1, keepdims=True)
    r = jnp.where(lo, lax.rsqrt(s0 * (1.0 / HEAD_DIM) + EPS), lax.rsqrt(s1 * (1.0 / HEAD_DIM) + EPS))
    return xs * r, r


def _head_rms_bwd(xs, w, dy, lo):
    xh, r = _head_rms(xs, w, lo)
    dxh = dy * w
    t = dxh * xh
    m0 = jnp.sum(jnp.where(lo, t, 0.0), axis=-1, keepdims=True)
    m1 = jnp.sum(jnp.where(lo, 0.0, t), axis=-1, keepdims=True)
    mm = jnp.where(lo, m0, m1) * (1.0 / HEAD_DIM)
    return r * (dxh - xh * mm), dy * xh


_QSCALE = HEAD_DIM ** -0.5


def _headnorm_fwd(proj, ws, name):
    t = proj.shape[0]
    tm = _pick(t, (256, 128))

    def body(p_ref, w_ref, qa_ref, ka_ref, va_ref, qb_ref, kb_ref, vb_ref):
        lo = _lo_mask()
        for s in range(4):
            c = LANES * s
            xh, _ = _head_rms(p_ref[:, c:c + LANES], None, lo)
            qa_ref[:, c:c + LANES] = (xh * w_ref[0:1, :] * _QSCALE).astype(BF16)
            xh, _ = _head_rms(p_ref[:, 512 + c:512 + c + LANES], None, lo)
            ka_ref[:, c:c + LANES] = (xh * w_ref[1:2, :]).astype(BF16)
            xh, _ = _head_rms(p_ref[:, 1536 + c:1536 + c + LANES], None, lo)
            qb_ref[:, c:c + LANES] = (xh * w_ref[2:3, :] * _QSCALE).astype(BF16)
        va_ref[...] = p_ref[:, 1024:1536].astype(BF16)
        xh, _ = _head_rms(p_ref[:, 2048:2176], None, lo)
        kb_ref[...] = (xh * w_ref[3:4, :]).astype(BF16)
        vb_ref[...] = p_ref[:, 2176:2304].astype(BF16)

    wide = pl.BlockSpec((tm, 512), lambda i: (i, 0))
    narrow = pl.BlockSpec((tm, LANES), lambda i: (i, 0))
    sd = lambda n: jax.ShapeDtypeStruct((t, n), BF16)
    return pl.pallas_call(
        body, name=name, grid=(t // tm,),
        in_specs=[pl.BlockSpec((tm, 2304), lambda i: (i, 0)), pl.BlockSpec((4, LANES), lambda i: (0, 0))],
        out_specs=[wide, wide, wide, wide, narrow, narrow],
        out_shape=[sd(512), sd(512), sd(512), sd(512), sd(LANES), sd(LANES)],
        compiler_params=_cp(("parallel",)),
    )(proj, ws)


def _headnorm_bwd(proj, ws, dqa, dkpa, dvpa, dqb, dkpb, dvpb, name):
    t = proj.shape[0]
    tm = TQ
    offa, offb = PAD_A // tm, PAD_B // tm

    def body(p_ref, w_ref, dqa_ref, dka_ref, dva_ref, dqb_ref, dkb_ref, dvb_ref, dp_ref, dw_ref):
        i = pl.program_id(0)
        lo = _lo_mask()

        @pl.when(i == 0)
        def _():
            dw_ref[...] = jnp.zeros_like(dw_ref)

        acc = [jnp.zeros((1, LANES), F32) for _ in range(4)]
        for s in range(4):
            c = LANES * s
            dx, dwl = _head_rms_bwd(p_ref[:, c:c + LANES], w_ref[0:1, :], dqa_ref[:, c:c + LANES] * _QSCALE, lo)
            dp_ref[:, c:c + LANES] = dx.astype(BF16)
            acc[0] += jnp.sum(dwl, axis=0, keepdims=True)
            dx, dwl = _head_rms_bwd(p_ref[:, 512 + c:512 + c + LANES], w_ref[1:2, :], dka_ref[:, c:c + LANES], lo)
            dp_ref[:, 512 + c:512 + c + LANES] = dx.astype(BF16)
            acc[1] += jnp.sum(dwl, axis=0, keepdims=True)
            dx, dwl = _head_rms_bwd(p_ref[:, 1536 + c:1536 + c + LANES], w_ref[2:3, :], dqb_ref[:, c:c + LANES] * _QSCALE, lo)
            dp_ref[:, 1536 + c:1536 + c + LANES] = dx.astype(BF16)
            acc[2] += jnp.sum(dwl, axis=0, keepdims=True)
        dp_ref[:, 1024:1536] = dva_ref[...].astype(BF16)

        def group_sum(ref):
            s0 = ref[:, 0:128] + ref[:, 128:256]
            s1 = ref[:, 256:384] + ref[:, 384:512]
            s0 = s0 + pltpu.roll(s0, HEAD_DIM, 1)
            s1 = s1 + pltpu.roll(s1, HEAD_DIM, 1)
            return jnp.where(lo, s0, s1)

        dx, dwl = _head_rms_bwd(p_ref[:, 2048:2176], w_ref[3:4, :], group_sum(dkb_ref), lo)
        dp_ref[:, 2048:2176] = dx.astype(BF16)
        acc[3] += jnp.sum(dwl, axis=0, keepdims=True)
        dp_ref[:, 2176:2304] = group_sum(dvb_ref).astype(BF16)
        for n in range(4):
            dw_ref[n:n + 1, :] += acc[n]

    wide = pl.BlockSpec((tm, 512), lambda i: (i, 0))
    pa = pl.BlockSpec((tm, 512), lambda i: (i + offa, 0))
    pb = pl.BlockSpec((tm, 512), lambda i: (i + offb, 0))
    return pl.pallas_call(
        body, name=name, grid=(t // tm,),
        in_specs=[pl.BlockSpec((tm, 2304), lambda i: (i, 0)), pl.BlockSpec((4, LANES), lambda i: (0, 0)),
                  wide, pa, pa, wide, pb, pb],
        out_specs=[pl.BlockSpec((tm, 2304), lambda i: (i, 0)), pl.BlockSpec((4, LANES), lambda i: (0, 0))],
        out_shape=[jax.ShapeDtypeStruct((t, 2304), BF16), jax.ShapeDtypeStruct((4, LANES), F32)],
        compiler_params=_cp(("arbitrary",)),
    )(proj, ws, dqa, dkpa, dvpa, dqb, dkpb, dvpb)


def _rel_onehot(q):
    r_io = lax.broadcasted_iota(jnp.int32, (REL_W, REL_W), 0)
    j_io = lax.broadcasted_iota(jnp.int32, (REL_W, REL_W), 1)
    idx = jnp.clip(q - (j_io - PAD_A), -MAX_REL, MAX_REL) + MAX_REL
    return (r_io == idx).astype(F32)


def _relpos_fwd(table, name):
    def body(t_ref, o_ref):
        def step(q, c):
            o_ref[q] = _dot(t_ref[...], _rel_onehot(q), 1, 0, HI)
            return c

        lax.fori_loop(0, TQ, step, 0)

    return pl.pallas_call(
        body, name=name, out_shape=jax.ShapeDtypeStruct((TQ, N_HEADS, REL_W), F32),
        in_specs=[pl.BlockSpec(memory_space=pltpu.VMEM)], out_specs=pl.BlockSpec(memory_space=pltpu.VMEM),
        compiler_params=_cp(),
    )(table)


def _relpos_bwd(dbias_t, name):
    def body(d_ref, o_ref):
        o_ref[...] = jnp.zeros_like(o_ref)

        def step(q, c):
            o_ref[...] += _dot(d_ref[q], _rel_onehot(q), 1, 1, HI)
            return c

        lax.fori_loop(0, TQ, step, 0)

    return pl.pallas_call(
        body, name=name, out_shape=jax.ShapeDtypeStruct((N_HEADS, REL_W), F32),
        in_specs=[pl.BlockSpec(memory_space=pltpu.VMEM)], out_specs=pl.BlockSpec(memory_space=pltpu.VMEM),
        compiler_params=_cp(),
    )(dbias_t)


def _attn_probs(qe, kw, bias, kvalid, snk):
    s = _dot(qe, kw, 1, 1) + bias
    s = jnp.where(kvalid, s, NEG)
    m = jnp.maximum(jnp.max(s, axis=-1, keepdims=True), snk)
    p = jnp.exp(s - m)
    inv = 1.0 / (jnp.sum(p, axis=-1, keepdims=True) + jnp.exp(snk - m))
    return p * inv, jnp.exp(snk - m) * inv


def _attn_fwd(q, kp, vp, bias, sinks, pad, name):
    t, hd = q.shape
    w = pad + TQ

    def body(sink_ref, q_ref, k_ref, v_ref, b_ref, o_ref):
        hp, i = pl.program_id(0), pl.program_id(1)
        start = pl.multiple_of(i * TQ, TQ)
        qv = q_ref[...]
        kw = k_ref[pl.ds(start, w), :]
        vw = v_ref[pl.ds(start, w), :]
        lo = _lo_mask()
        kvalid = (start + lax.broadcasted_iota(jnp.int32, (1, w), 1)) >= pad
        outs = []
        for e in range(2):
            sel = lo if e == 0 else jnp.logical_not(lo)
            qe = jnp.where(sel, qv, jnp.zeros_like(qv))
            p, _ = _attn_probs(qe, kw, b_ref[e], kvalid, sink_ref[2 * hp + e])
            outs.append(_dot(p.astype(BF16), vw, 1, 0))
        o_ref[...] = jnp.where(lo, outs[0], outs[1]).astype(BF16)

    full = pl.BlockSpec((t + pad, LANES), lambda h, i: (0, h))
    tile = pl.BlockSpec((TQ, LANES), lambda h, i: (i, h))
    return pl.pallas_call(
        body, name=name, grid=(hd // LANES, t // TQ),
        in_specs=[pl.BlockSpec(memory_space=pltpu.SMEM), tile, full, full, pl.BlockSpec((2, TQ, w), lambda h, i: (h, 0, 0))],
        out_specs=tile, out_shape=jax.ShapeDtypeStruct((t, hd), BF16),
        compiler_params=_cp(("parallel", "arbitrary")),
    )(sinks, q, kp, vp, bias)


def _attn_bwd(q, kp, vp, bias, sinks, do, col_off, pad, name):
    t, hd = q.shape
    w = pad + TQ
    nhp = hd // LANES

    def body(sink_ref, q_ref, k_ref, v_ref, b_ref, do_ref, dq_ref, dk_ref, dv_ref, db_ref, ds_ref):
        hp, i = pl.program_id(0), pl.program_id(1)
        start = pl.multiple_of(i * TQ, TQ)

        @pl.when(i == 0)
        def _():
            dk_ref[...] = jnp.zeros_like(dk_ref)
            dv_ref[...] = jnp.zeros_like(dv_ref)
            db_ref[...] = jnp.zeros_like(db_ref)
            ds_ref[...] = jnp.zeros_like(ds_ref)

        qv = q_ref[...]
        dov = do_ref[...]
        kw = k_ref[pl.ds(start, w), :]
        vw = v_ref[pl.ds(start, w), :]
        lo = _lo_mask()
        kvalid = (start + lax.broadcasted_iota(jnp.int32, (1, w), 1)) >= pad
        row8 = lax.broadcasted_iota(jnp.int32, (8, LANES), 0)
        dqs, dkw, dvw = [], None, None
        for e in range(2):
            sel = lo if e == 0 else jnp.logical_not(lo)
            qe = jnp.where(sel, qv, jnp.zeros_like(qv))
            doe = jnp.where(sel, dov, jnp.zeros_like(dov))
            p, psink = _attn_probs(qe, kw, b_ref[e], kvalid, sink_ref[2 * hp + e])
            dp = _dot(doe, vw, 1, 1)
            delta = jnp.sum(p * dp, axis=-1, keepdims=True)
            ds = p * (dp - delta)
            db_ref[e] += ds
            dsnk = jnp.sum(-psink * delta, axis=0, keepdims=True)
            ds_ref[0] += jnp.where(row8 == e, dsnk, 0.0)
            dsb = ds.astype(BF16)
            dqs.append(_dot(dsb, kw, 1, 0))
            dk_e = _dot(dsb, qe, 0, 0)
            dv_e = _dot(p.astype(BF16), doe, 0, 0)
            dkw = dk_e if dkw is None else dkw + dk_e
            dvw = dv_e if dvw is None else dvw + dv_e
        dq_ref[...] = jnp.where(lo, dqs[0], dqs[1])
        dk_ref[pl.ds(start, w), :] += dkw
        dv_ref[pl.ds(start, w), :] += dvw

    full = pl.BlockSpec((t + pad, LANES), lambda h, i: (0, h))
    tile = pl.BlockSpec((TQ, LANES), lambda h, i: (i, h))
    btile = pl.BlockSpec((2, TQ, w), lambda h, i: (h, 0, 0))
    return pl.pallas_call(
        body, name=name, grid=(nhp, t // TQ),
        in_specs=[pl.BlockSpec(memory_space=pltpu.SMEM), tile, full, full, btile,
                  pl.BlockSpec((TQ, LANES), lambda h, i: (i, h + col_off))],
        out_specs=[tile, full, full, btile, pl.BlockSpec((1, 8, LANES), lambda h, i: (h, 0, 0))],
        out_shape=[jax.ShapeDtypeStruct((t, hd), F32), jax.ShapeDtypeStruct((t + pad, hd), F32),
                   jax.ShapeDtypeStruct((t + pad, hd), F32), jax.ShapeDtypeStruct((N_HEADS, TQ, w), F32),
                   jax.ShapeDtypeStruct((nhp, 8, LANES), F32)],
        compiler_params=_cp(("parallel", "arbitrary")),
    )(sinks, q, kp, vp, bias, do)


def _halo_prev(tm):
    return lambda i: jnp.maximum(i * (tm // 8) - 1, 0)


def _halo_next(tm, t):
    return lambda i: jnp.minimum((i + 1) * (tm // 8), t // 8 - 1)


def _taps_prev(tile, halo, ktaps, first):
    tm = tile.shape[0]
    ext = jnp.concatenate([jnp.where(first, 0.0, halo), tile], axis=0)
    return [tile] + [pltpu.roll(ext, s, 0)[8:8 + tm] for s in range(1, ktaps)]


def _taps_next(tile, halo, ktaps, last):
    tm = tile.shape[0]
    ext = jnp.concatenate([tile, jnp.where(last, 0.0, halo)], axis=0)
    return [tile] + [pltpu.roll(ext, tm + 8 - s, 0)[0:tm] for s in range(1, ktaps)]


def _conv_apply(taps, w_ref, ktaps):
    out = taps[0] * w_ref[ktaps - 1:ktaps, :]
    for s in range(1, ktaps):
        out = out + taps[s] * w_ref[ktaps - 1 - s:ktaps - s, :]
    return out


def _silu_grad(x):
    sg = jax.nn.sigmoid(x)
    return x * sg, sg * (1.0 + x * (1.0 - sg))


FFN_TM = 128


def _ffn_mid_fwd(gu, w8, b, name):
    t = gu.shape[0]
    f = D_FF
    tm = FFN_TM

    def body(g_ref, u_ref, h_ref, w_ref, b_ref, a_ref):
        first = pl.program_id(0) == 0
        gc = _conv_apply(_taps_prev(g_ref[...], h_ref[...], 3, first), w_ref, 3) + b_ref[...]
        a_ref[...] = (gc * jax.nn.sigmoid(gc) * u_ref[...]).astype(BF16)

    hp = _halo_prev(tm)
    return pl.pallas_call(
        body, name=name, grid=(t // tm,),
        in_specs=[pl.BlockSpec((tm, f), lambda i: (i, 0)), pl.BlockSpec((tm, f), lambda i: (i, 1)),
                  pl.BlockSpec((8, f), lambda i: (hp(i), 0)), pl.BlockSpec((8, f), lambda i: (0, 0)),
                  pl.BlockSpec((1, f), lambda i: (0, 0))],
        out_specs=pl.BlockSpec((tm, f), lambda i: (i, 0)), out_shape=jax.ShapeDtypeStruct((t, f), BF16),
        compiler_params=_cp(("parallel",)),
    )(gu, gu, gu, w8, b)


def _ffn_mid_bwd1(gu, da, w8, b, name):
    t = gu.shape[0]
    f = D_FF
    tm = FFN_TM

    def body(g_ref, u_ref, h_ref, da_ref, w_ref, b_ref, dgc_ref, dup_ref, dw_ref, db_ref):
        i = pl.program_id(0)

        @pl.when(i == 0)
        def _():
            dw_ref[...] = jnp.zeros_like(dw_ref)
            db_ref[...] = jnp.zeros_like(db_ref)

        taps = _taps_prev(g_ref[...], h_ref[...], 3, i == 0)
        gc = _conv_apply(taps, w_ref, 3) + b_ref[...]
        act, dact = _silu_grad(gc)
        dav = da_ref[...]
        dup_ref[...] = (dav * act).astype(BF16)
        dgc = dav * u_ref[...] * dact
        dgc_ref[...] = dgc
        db_ref[...] += jnp.sum(dgc, axis=0, keepdims=True)
        for s in range(3):
            dw_ref[2 - s:3 - s, :] += jnp.sum(dgc * taps[s], axis=0, keepdims=True)

    hp = _halo_prev(tm)
    row = pl.BlockSpec((tm, f), lambda i: (i, 0))
    return pl.pallas_call(
        body, name=name, grid=(t // tm,),
        in_specs=[row, pl.BlockSpec((tm, f), lambda i: (i, 1)), pl.BlockSpec((8, f), lambda i: (hp(i), 0)), row,
                  pl.BlockSpec((8, f), lambda i: (0, 0)), pl.BlockSpec((1, f), lambda i: (0, 0))],
        out_specs=[row, row, pl.BlockSpec((8, f), lambda i: (0, 0)), pl.BlockSpec((1, f), lambda i: (0, 0))],
        out_shape=[jax.ShapeDtypeStruct((t, f), F32), jax.ShapeDtypeStruct((t, f), BF16),
                   jax.ShapeDtypeStruct((8, f), F32), jax.ShapeDtypeStruct((1, f), F32)],
        compiler_params=_cp(("arbitrary",)),
    )(gu, gu, gu, da, w8, b)


def _conv_bwd_data(dc, w8, ktaps, name):
    t, c = dc.shape
    tm = 256
    tc = _pick(c, (1408, 1024))
    nt = t // tm

    def body(d_ref, h_ref, w_ref, o_ref):
        last = pl.program_id(0) == nt - 1
        o_ref[...] = _conv_apply(_taps_next(d_ref[...], h_ref[...], ktaps, last), w_ref, ktaps).astype(BF16)

    hn = _halo_next(tm, t)
    return pl.pallas_call(
        body, name=name, grid=(nt, c // tc),
        in_specs=[pl.BlockSpec((tm, tc), lambda i, j: (i, j)), pl.BlockSpec((8, tc), lambda i, j: (hn(i), j)),
                  pl.BlockSpec((8, tc), lambda i, j: (0, j))],
        out_specs=pl.BlockSpec((tm, tc), lambda i, j: (i, j)), out_shape=jax.ShapeDtypeStruct((t, c), BF16),
        compiler_params=_cp(("parallel", "parallel")),
    )(dc, dc, w8)


PRE_TM = 256
PRE_TC = 1024


def _ssm_pre_fwd(zx, w8, b, name):
    t = zx.shape[0]
    tm, tc = PRE_TM, PRE_TC
    off = D_INNER // tc

    def body(x_ref, h_ref, w_ref, b_ref, o_ref):
        first = pl.program_id(0) == 0
        c = _conv_apply(_taps_prev(x_ref[...], h_ref[...], 4, first), w_ref, 4) + b_ref[...]
        o_ref[...] = c * jax.nn.sigmoid(c)

    hp = _halo_prev(tm)
    return pl.pallas_call(
        body, name=name, grid=(t // tm, XBC // tc),
        in_specs=[pl.BlockSpec((tm, tc), lambda i, j: (i, j + off)), pl.BlockSpec((8, tc), lambda i, j: (hp(i), j + off)),
                  pl.BlockSpec((8, tc), lambda i, j: (0, j)), pl.BlockSpec((1, tc), lambda i, j: (0, j))],
        out_specs=pl.BlockSpec((tm, tc), lambda i, j: (i, j)), out_shape=jax.ShapeDtypeStruct((t, XBC), F32),
        compiler_params=_cp(("parallel", "parallel")),
    )(zx, zx, w8, b)


def _ssm_pre_bwd1(zx, dxbc, w8, b, name):
    t = zx.shape[0]
    tm, tc = PRE_TM, PRE_TC
    off = D_INNER // tc

    def body(x_ref, h_ref, d_ref, w_ref, b_ref, dc_ref, dw_ref, db_ref):
        i = pl.program_id(1)

        @pl.when(i == 0)
        def _():
            dw_ref[...] = jnp.zeros_like(dw_ref)
            db_ref[...] = jnp.zeros_like(db_ref)

        taps = _taps_prev(x_ref[...], h_ref[...], 4, i == 0)
        c = _conv_apply(taps, w_ref, 4) + b_ref[...]
        _, dact = _silu_grad(c)
        dc = d_ref[...] * dact
        dc_ref[...] = dc
        db_ref[...] += jnp.sum(dc, axis=0, keepdims=True)
        for s in range(4):
            dw_ref[3 - s:4 - s, :] += jnp.sum(dc * taps[s], axis=0, keepdims=True)

    hp = _halo_prev(tm)
    return pl.pallas_call(
        body, name=name, grid=(XBC // tc, t // tm),
        in_specs=[pl.BlockSpec((tm, tc), lambda j, i: (i, j + off)), pl.BlockSpec((8, tc), lambda j, i: (hp(i), j + off)),
                  pl.BlockSpec((tm, tc), lambda j, i: (i, j)),
                  pl.BlockSpec((8, tc), lambda j, i: (0, j)), pl.BlockSpec((1, tc), lambda j, i: (0, j))],
        out_specs=[pl.BlockSpec((tm, tc), lambda j, i: (i, j)), pl.BlockSpec((8, tc), lambda j, i: (0, j)),
                   pl.BlockSpec((1, tc), lambda j, i: (0, j))],
        out_shape=[jax.ShapeDtypeStruct((t, XBC), F32), jax.ShapeDtypeStruct((8, XBC), F32),
                   jax.ShapeDtypeStruct((1, XBC), F32)],
        compiler_params=_cp(("parallel", "arbitrary")),
    )(zx, zx, dxbc, w8, b)


def _head_lanes():
    return lax.broadcasted_iota(jnp.int32, (1, LANES), 1) < SSM_HEADS


def _dt_fwd(dtraw, bias, name):
    t = dtraw.shape[0]
    tm = _pick(t, (1024, 512, 256, 128))

    def body(x_ref, b_ref, o_ref):
        v = x_ref[...] + b_ref[...]
        sp = jnp.maximum(v, 0.0) + jnp.log(1.0 + jnp.exp(-jnp.abs(v)))
        o_ref[...] = jnp.where(_head_lanes(), sp, 0.0)

    row = pl.BlockSpec((tm, LANES), lambda i: (i, 0))
    return pl.pallas_call(
        body, name=name, grid=(t // tm,), in_specs=[row, pl.BlockSpec((1, LANES), lambda i: (0, 0))], out_specs=row,
        out_shape=jax.ShapeDtypeStruct((t, LANES), F32), compiler_params=_cp(("parallel",)),
    )(dtraw, bias)


def _dt_bwd(dtraw, bias, ddt, name):
    t = dtraw.shape[0]
    tm = _pick(t, (1024, 512, 256, 128))

    def body(x_ref, b_ref, d_ref, o_ref, db_ref):
        @pl.when(pl.program_id(0) == 0)
        def _():
            db_ref[...] = jnp.zeros_like(db_ref)

        g = jnp.where(_head_lanes(), d_ref[...] * jax.nn.sigmoid(x_ref[...] + b_ref[...]), 0.0)
        o_ref[...] = g.astype(BF16)
        db_ref[...] += jnp.sum(g, axis=0, keepdims=True)

    row = pl.BlockSpec((tm, LANES), lambda i: (i, 0))
    vec = pl.BlockSpec((1, LANES), lambda i: (0, 0))
    return pl.pallas_call(
        body, name=name, grid=(t // tm,), in_specs=[row, vec, row], out_specs=[row, vec],
        out_shape=[jax.ShapeDtypeStruct((t, LANES), BF16), jax.ShapeDtypeStruct((1, LANES), F32)],
        compiler_params=_cp(("arbitrary",)),
    )(dtraw, bias, ddt)


GROUP_W = D_INNER // SSM_GROUPS
POST_TM = 512


def _ssm_post_fwd(y, xbc, zx, dexp, nw, name):
    t = y.shape[0]
    tm = _pick(t, (POST_TM, 256, 128))

    def body(y_ref, x_ref, z_ref, d_ref, w_ref, o_ref):
        zv = z_ref[...]
        y3 = (y_ref[...] + d_ref[...] * x_ref[...]) * (zv * jax.nn.sigmoid(zv))
        r = lax.rsqrt(jnp.mean(y3 * y3, axis=-1, keepdims=True) + EPS)
        o_ref[...] = (y3 * r * w_ref[...]).astype(BF16)

    blk = pl.BlockSpec((tm, GROUP_W), lambda i, g: (i, g))
    vec = pl.BlockSpec((1, GROUP_W), lambda i, g: (0, g))
    return pl.pallas_call(
        body, name=name, grid=(t // tm, SSM_GROUPS), in_specs=[blk, blk, blk, vec, vec], out_specs=blk,
        out_shape=jax.ShapeDtypeStruct((t, D_INNER), BF16), compiler_params=_cp(("parallel", "parallel")),
    )(y, xbc, zx, dexp, nw)


def _ssm_post_bwd(dy4, y, xbc, zx, dexp, nw, name):
    t = y.shape[0]
    tm = _pick(t, (POST_TM, 256, 128))

    def body(g_ref, y_ref, x_ref, z_ref, d_ref, w_ref, dy_ref, dxs_ref, dz_ref, dd_ref, dw_ref):
        @pl.when(pl.program_id(1) == 0)
        def _():
            dd_ref[...] = jnp.zeros_like(dd_ref)
            dw_ref[...] = jnp.zeros_like(dw_ref)

        zv = z_ref[...]
        xv = x_ref[...]
        act, dact = _silu_grad(zv)
        y2 = y_ref[...] + d_ref[...] * xv
        y3 = y2 * act
        r = lax.rsqrt(jnp.mean(y3 * y3, axis=-1, keepdims=True) + EPS)
        y3n = y3 * r
        gv = g_ref[...]
        dyn = gv * w_ref[...]
        dy3 = r * (dyn - y3n * jnp.mean(dyn * y3n, axis=-1, keepdims=True))
        dy2 = dy3 * act
        dy_ref[...] = dy2
        dxs_ref[...] = dy2 * d_ref[...]
        dz_ref[...] = (dy3 * y2 * dact).astype(BF16)
        dd_ref[...] += jnp.sum(dy2 * xv, axis=0, keepdims=True)
        dw_ref[...] += jnp.sum(gv * y3n, axis=0, keepdims=True)

    blk = pl.BlockSpec((tm, GROUP_W), lambda g, i: (i, g))
    vec = pl.BlockSpec((1, GROUP_W), lambda g, i: (0, g))
    return pl.pallas_call(
        body, name=name, grid=(SSM_GROUPS, t // tm), in_specs=[blk, blk, blk, blk, vec, vec],
        out_specs=[blk, blk, blk, vec, vec],
        out_shape=[jax.ShapeDtypeStruct((t, D_INNER), F32), jax.ShapeDtypeStruct((t, D_INNER), F32),
                   jax.ShapeDtypeStruct((t, D_INNER), BF16), jax.ShapeDtypeStruct((1, D_INNER), F32),
                   jax.ShapeDtypeStruct((1, D_INNER), F32)],
        compiler_params=_cp(("parallel", "arbitrary")),
    )(dy4, y, xbc, zx, dexp, nw)


def _ssd_common(dt, alog):
    ll = dt.shape[0]
    a_neg = -jnp.exp(alog)
    a = dt * a_neg
    ri = lax.broadcasted_iota(jnp.int32, (ll, ll), 0)
    ci = lax.broadcasted_iota(jnp.int32, (ll, ll), 1)
    tril = ri >= ci
    acs = _dot(tril.astype(F32), a, 1, 0, HI)
    return a_neg, tril, acs, acs.T


def _pair_terms(acs, acs_t, dt, h0, lo):
    ll = acs.shape[0]
    cols = [acs[:, h0 + e:h0 + e + 1] for e in range(2)]
    rows = [acs_t[h0 + e:h0 + e + 1, :] for e in range(2)]
    dtc = [dt[:, h0 + e:h0 + e + 1] for e in range(2)]
    lasts = [c[ll - 1:ll, :] for c in cols]
    dtx = jnp.where(lo, dtc[0], dtc[1])
    eac = jnp.where(lo, jnp.exp(cols[0]), jnp.exp(cols[1]))
    fdec = jnp.where(lo, jnp.exp(lasts[0] - cols[0]), jnp.exp(lasts[1] - cols[1]))
    elast = jnp.where(lo, jnp.exp(lasts[0]), jnp.exp(lasts[1]))
    return cols, rows, dtx, eac, fdec, elast


def _decay(col, row, tril):
    return jnp.where(tril, jnp.exp(jnp.minimum(col - row, 0.0)), 0.0)


def _ssd_fwd(xbc, dt, alog, name):
    t = xbc.shape[0]
    ll = SSD_L
    nc = t // ll

    def body(x_ref, dt_ref, al_ref, y_ref, sp_ref, st_ref):
        @pl.when(pl.program_id(0) == 0)
        def _():
            st_ref[...] = jnp.zeros_like(st_ref)

        dtv = dt_ref[...]
        _, tril, acs, acs_t = _ssd_common(dtv, al_ref[...])
        lo = _lo_mask()
        sp_ref[0] = st_ref[...]
        for g in range(SSM_GROUPS):
            bg = x_ref[:, D_INNER + SSM_STATE * g:D_INNER + SSM_STATE * (g + 1)].astype(BF16)
            cg = x_ref[:, D_INNER + 512 + SSM_STATE * g:D_INNER + 512 + SSM_STATE * (g + 1)].astype(BF16)
            gm = _dot(cg, bg, 1, 1)
            for pp in range(4):
                h0 = 8 * g + 2 * pp
                c0 = HEAD_DIM * h0
                x2 = x_ref[:, c0:c0 + LANES]
                cols, rows, dtx, eac, fdec, elast = _pair_terms(acs, acs_t, dtv, h0, lo)
                u = (x2 * dtx).astype(BF16)
                ys = [_dot((gm * _decay(cols[e], rows[e], tril)).astype(BF16), u, 1, 0) for e in range(2)]
                s2 = st_ref[:, c0:c0 + LANES]
                y_ref[:, c0:c0 + LANES] = jnp.where(lo, ys[0], ys[1]) + _dot(cg, s2.astype(BF16), 1, 0) * eac
                xw = (x2 * (fdec * dtx)).astype(BF16)
                st_ref[:, c0:c0 + LANES] = s2 * elast + _dot(bg, xw, 0, 0)

    return pl.pallas_call(
        body, name=name, grid=(nc,),
        in_specs=[pl.BlockSpec((ll, XBC), lambda c: (c, 0)), pl.BlockSpec((ll, LANES), lambda c: (c, 0)),
                  pl.BlockSpec((1, LANES), lambda c: (0, 0))],
        out_specs=[pl.BlockSpec((ll, D_INNER), lambda c: (c, 0)), pl.BlockSpec((1, SSM_STATE, D_INNER), lambda c: (c, 0, 0))],
        out_shape=[jax.ShapeDtypeStruct((t, D_INNER), F32), jax.ShapeDtypeStruct((nc, SSM_STATE, D_INNER), F32)],
        scratch_shapes=[pltpu.VMEM((SSM_STATE, D_INNER), F32)],
        compiler_params=_cp(("arbitrary",)),
    )(xbc, dt, alog)


def _ssd_bwd(xbc, dt, alog, sprev, dy, dskip, name):
    t = xbc.shape[0]
    ll = SSD_L
    nc = t // ll

    def body(x_ref, dt_ref, al_ref, sp_ref, dy_ref, dk_ref, dx_ref, ddt_ref, dal_ref, ds_ref, colt_ref):
        @pl.when(pl.program_id(0) == 0)
        def _():
            ds_ref[...] = jnp.zeros_like(ds_ref)
            dal_ref[...] = jnp.zeros_like(dal_ref)

        dtv = dt_ref[...]
        a_neg, tril, acs, acs_t = _ssd_common(dtv, al_ref[...])
        lo = _lo_mask()
        hi = jnp.logical_not(lo)
        lane = lax.broadcasted_iota(jnp.int32, (1, LANES), 1)
        colt_ref[...] = jnp.zeros_like(colt_ref)
        rowterm = jnp.zeros((ll, LANES), F32)
        ddt_u = jnp.zeros((ll, LANES), F32)
        dlast = jnp.zeros((1, LANES), F32)

        def halves(v):
            return (jnp.sum(jnp.where(lo, v, 0.0), axis=-1, keepdims=True),
                    jnp.sum(jnp.where(hi, v, 0.0), axis=-1, keepdims=True))

        for g in range(SSM_GROUPS):
            cb0 = D_INNER + SSM_STATE * g
            cc0 = D_INNER + 512 + SSM_STATE * g
            bg = x_ref[:, cb0:cb0 + SSM_STATE].astype(BF16)
            cg = x_ref[:, cc0:cc0 + SSM_STATE].astype(BF16)
            gm = _dot(cg, bg, 1, 1)
            dgm = jnp.zeros((ll, ll), F32)
            dc_st = jnp.zeros((ll, SSM_STATE), F32)
            db_st = jnp.zeros((ll, SSM_STATE), F32)
            for pp in range(4):
                h0 = 8 * g + 2 * pp
                c0 = HEAD_DIM * h0
                x2 = x_ref[:, c0:c0 + LANES]
                cols, rows, dtx, eac, fdec, elast = _pair_terms(acs, acs_t, dtv, h0, lo)
                u32 = x2 * dtx
                u = u32.astype(BF16)
                dy2 = dy_ref[:, c0:c0 + LANES]
                dyb = dy2.astype(BF16)
                sp2 = sp_ref[0, :, c0:c0 + LANES]
                spb = sp2.astype(BF16)
                ds2 = ds_ref[:, c0:c0 + LANES]
                dsb = ds2.astype(BF16)
                xw = (x2 * (fdec * dtx)).astype(BF16)
                du_st = _dot(bg, dsb, 1, 0) * fdec
                yst = _dot(cg, spb, 1, 0) * eac
                dye = (dy2 * eac).astype(BF16)
                dc_st = dc_st + _dot(dye, spb, 1, 1)
                db_st = db_st + _dot(xw, dsb, 1, 1)
                ds_ref[:, c0:c0 + LANES] = ds2 * elast + _dot(cg, dye, 0, 0)
                dus, rsum, csum = [], [], []
                for e in range(2):
                    dec = _decay(cols[e], rows[e], tril)
                    wm = gm * dec
                    dye_m = jnp.where(lo if e == 0 else hi, dyb, jnp.zeros_like(dyb))
                    dum = _dot(dye_m, u, 1, 1)
                    dgm = dgm + dum * dec
                    tm_ = dum * wm
                    rsum.append(jnp.sum(tm_, axis=1, keepdims=True))
                    csum.append(jnp.sum(tm_, axis=0, keepdims=True))
                    dus.append(_dot(wm.astype(BF16), dyb, 0, 0))
                du = jnp.where(lo, dus[0], dus[1]) + du_st
                dx_ref[:, c0:c0 + LANES] = du * dtx + dk_ref[:, c0:c0 + LANES]
                ddtu = halves(du * x2)
                rst = halves(dy2 * yst)
                qst = halves(du_st * u32)
                sst = halves(jnp.sum(ds2 * sp2, axis=0, keepdims=True))
                for e in range(2):
                    oh = lane == (h0 + e)
                    rowterm = rowterm + jnp.where(oh, rsum[e] + rst[e] - qst[e], 0.0)
                    ddt_u = ddt_u + jnp.where(oh, ddtu[e], 0.0)
                    last_e = cols[e][ll - 1:ll, :]
                    dl = jnp.exp(last_e) * sst[e] + jnp.sum(qst[e], axis=0, keepdims=True)
                    dlast = dlast + jnp.where(oh, dl, 0.0)
                    colt_ref[h0 + e:h0 + e + 1, :] = csum[e]
            dgb = dgm.astype(BF16)
            dx_ref[:, cc0:cc0 + SSM_STATE] = _dot(dgb, bg, 1, 0) + dc_st
            dx_ref[:, cb0:cb0 + SSM_STATE] = _dot(dgb, cg, 0, 0) + db_st
        row_io = lax.broadcasted_iota(jnp.int32, (ll, LANES), 0)
        dacs = rowterm - colt_ref[...].T + jnp.where(row_io == ll - 1, dlast, 0.0)
        da = _dot(jnp.logical_not(tril).astype(F32) + jnp.where(
            lax.broadcasted_iota(jnp.int32, (ll, ll), 0) == lax.broadcasted_iota(jnp.int32, (ll, ll), 1), 1.0, 0.0),
            dacs, 1, 0, HI)
        ddt_ref[...] = da * a_neg + ddt_u
        dal_ref[...] += jnp.sum(da * dtv, axis=0, keepdims=True) * a_neg

    rev = lambda c: nc - 1 - c
    return pl.pallas_call(
        body, name=name, grid=(nc,),
        in_specs=[pl.BlockSpec((ll, XBC), lambda c: (rev(c), 0)), pl.BlockSpec((ll, LANES), lambda c: (rev(c), 0)),
                  pl.BlockSpec((1, LANES), lambda c: (0, 0)),
                  pl.BlockSpec((1, SSM_STATE, D_INNER), lambda c: (rev(c), 0, 0)),
                  pl.BlockSpec((ll, D_INNER), lambda c: (rev(c), 0)), pl.BlockSpec((ll, D_INNER), lambda c: (rev(c), 0))],
        out_specs=[pl.BlockSpec((ll, XBC), lambda c: (rev(c), 0)), pl.BlockSpec((ll, LANES), lambda c: (rev(c), 0)),
                   pl.BlockSpec((1, LANES), lambda c: (0, 0))],
        out_shape=[jax.ShapeDtypeStruct((t, XBC), F32), jax.ShapeDtypeStruct((t, LANES), F32),
                   jax.ShapeDtypeStruct((1, LANES), F32)],
        scratch_shapes=[pltpu.VMEM((SSM_STATE, D_INNER), F32), pltpu.VMEM((LANES, ll), F32)],
        compiler_params=_cp(("arbitrary",)),
    )(xbc, dt, alog, sprev, dy, dskip)


def _loss(y, target, name):
    t, d = y.shape
    tm = _pick(t, (512, 256, 128))

    def body(y_ref, t_ref, dy_ref, dyb_ref, acc_ref):
        @pl.when(pl.program_id(0) == 0)
        def _():
            acc_ref[...] = jnp.zeros_like(acc_ref)

        err = y_ref[...] - t_ref[...]
        dy = err * (1.0 / d)
        dy_ref[...] = dy
        dyb_ref[...] = dy.astype(BF16)
        acc_ref[...] += jnp.sum(err * err, axis=0, keepdims=True)

    row = pl.BlockSpec((tm, d), lambda i: (i, 0))
    vec = pl.BlockSpec((1, d), lambda i: (0, 0))
    return pl.pallas_call(
        body, name=name, grid=(t // tm,), in_specs=[row, row], out_specs=[row, row, vec],
        out_shape=[jax.ShapeDtypeStruct((t, d), F32), jax.ShapeDtypeStruct((t, d), BF16), jax.ShapeDtypeStruct((1, d), F32)],
        compiler_params=_cp(("arbitrary",)),
    )(y, target)


ADAM_TR = 512


def _adamw(parts, w, m, v, name):
    r = w.shape[0]
    tr = ADAM_TR
    c1 = 1.0 - ADAM_B1 ** ADAM_STEP
    c2 = 1.0 - ADAM_B2 ** ADAM_STEP

    def body(p_ref, w_ref, m_ref, v_ref, g_ref, d_ref, mo_ref, vo_ref):
        g = p_ref[0].astype(F32)
        for k in range(1, N_DEV):
            g = g + p_ref[k].astype(F32)
        mn = ADAM_B1 * m_ref[...] + (1.0 - ADAM_B1) * g
        vn = ADAM_B2 * v_ref[...] + (1.0 - ADAM_B2) * (g * g)
        g_ref[...] = g
        mo_ref[...] = mn
        vo_ref[...] = vn
        d_ref[...] = -ADAM_LR * ((mn / c1) / (jnp.sqrt(vn / c2) + ADAM_EPS) + ADAM_WD * w_ref[...])

    row = pl.BlockSpec((tr, LANES), lambda i: (i, 0))
    sd = jax.ShapeDtypeStruct((r, LANES), F32)
    return pl.pallas_call(
        body, name=name, grid=(r // tr,),
        in_specs=[pl.BlockSpec((N_DEV, tr, LANES), lambda i: (0, i, 0)), row, row, row],
        out_specs=[row, row, row, row], out_shape=[sd, sd, sd, sd], compiler_params=_cp(("parallel",)),
    )(parts, w, m, v)


def _peers():
    mx, my, mc = lax.axis_index("x"), lax.axis_index("y"), lax.axis_index("c")
    me = 4 * mx + 2 * my + mc
    out = []
    for k in range(1, N_DEV):
        px = 1 - mx if k & 4 else mx
        py = 1 - my if k & 2 else my
        pc = 1 - mc if k & 1 else mc
        out.append(((px, py, pc), 4 * px + 2 * py + pc))
    return me, out


def _exchange(x, scatter, name):
    r = x.shape[-2]

    def body(x_ref, o_ref, send_sems, recv_sems, local_sem):
        me, peers = _peers()
        mine = x_ref.at[me] if scatter else x_ref
        local = pltpu.make_async_copy(mine, o_ref.at[me], local_sem)
        local.start()
        sends = []
        for k, (dev, idx) in enumerate(peers):
            cp = pltpu.make_async_remote_copy(
                src_ref=x_ref.at[idx] if scatter else x_ref, dst_ref=o_ref.at[me],
                send_sem=send_sems.at[k], recv_sem=recv_sems.at[k], device_id=dev, device_id_type=MESH)
            cp.start()
            sends.append(cp)
        for k, (dev, idx) in enumerate(peers):
            pltpu.make_async_remote_copy(
                src_ref=mine, dst_ref=o_ref.at[idx], send_sem=send_sems.at[k], recv_sem=recv_sems.at[k],
                device_id=dev, device_id_type=MESH).wait_recv()
        for cp in sends:
            cp.wait_send()
        local.wait()

    return pl.pallas_call(
        body, name=name, out_shape=jax.ShapeDtypeStruct((N_DEV, r, LANES), x.dtype),
        in_specs=[pl.BlockSpec(memory_space=pl.ANY)], out_specs=pl.BlockSpec(memory_space=pl.ANY),
        scratch_shapes=[pltpu.SemaphoreType.DMA((N_DEV - 1,)), pltpu.SemaphoreType.DMA((N_DEV - 1,)),
                        pltpu.SemaphoreType.DMA],
    )(x)


def _pack(arrs, dtype, lead=()):
    nl = len(lead)
    flat = jnp.concatenate([a.astype(dtype).reshape(lead + (-1,)) for a in arrs], axis=nl)
    n = flat.shape[-1]
    rows = -(-n // (LANES * ADAM_TR)) * ADAM_TR
    flat = jnp.pad(flat, [(0, 0)] * nl + [(0, rows * LANES - n)])
    return flat.reshape(lead + (rows, LANES))


def _unpack(flat, shapes, lead=()):
    nl = len(lead)
    flat = flat.reshape(lead + (-1,))
    out, o = [], 0
    for s in shapes:
        n = 1
        for d in s:
            n *= d
        out.append(lax.slice_in_dim(flat, o, o + n, axis=nl).reshape(lead + tuple(s)))
        o += n
    return out


def _join(g, ax):
    s = g.shape[1:]
    return jnp.moveaxis(g, 0, ax).reshape(s[:ax] + (N_DEV * s[ax],) + s[ax + 1:])


def _split(full, ax):
    s = full.shape
    return jnp.moveaxis(full.reshape(s[:ax] + (N_DEV, s[ax] // N_DEV) + s[ax + 1:]), ax, 0)


_WEIGHTS = ['norm_mix', 'norm_ffn', 'attn_w_in', 'attn_w_out', 'relpos_table', 'q_norm_a', 'k_norm_a', 'q_norm_b',
            'k_norm_b', 'sinks', 'ssm_w_in', 'ssm_conv_w', 'ssm_conv_b', 'ssm_dt_bias', 'ssm_a_log', 'ssm_d', 'ssm_norm',
            'ssm_w_out', 'ffn_w_in', 'ffn_conv_w', 'ffn_conv_b', 'ffn_w_out']
_SHARD_AX = {'attn_w_in': 2, 'attn_w_out': 1, 'ssm_w_in': 2, 'ssm_conv_w': 2, 'ssm_conv_b': 1, 'ssm_norm': 1,
             'ssm_w_out': 1, 'ffn_w_in': 2, 'ffn_conv_w': 2, 'ffn_w_out': 1}
_BIG = ['attn_w_in', 'attn_w_out', 'ssm_w_in', 'ssm_w_out', 'ffn_w_in', 'ffn_w_out']
_SMALL = ['ssm_conv_w', 'ssm_conv_b', 'ssm_norm', 'ffn_conv_w']
_SHARDED = _BIG + _SMALL
_REPL = [n for n in _WEIGHTS if n not in _SHARD_AX]


def _rows8(w):
    return jnp.pad(w, ((0, 8 - w.shape[0]), (0, 0)))


def _lanes128(v):
    return jnp.pad(v, (0, LANES - v.shape[0])).reshape(1, LANES)


def _band_mask(n_prev, pad):
    cq = jnp.arange(TQ)[:, None] // CHUNK
    ck = jnp.arange(pad + TQ)[None, :] // CHUNK
    return (ck >= cq) & (ck <= cq + n_prev)


def _pad_rows(a, pad):
    return jnp.pad(a, ((pad, 0), (0, 0)))


def _kv_expand(a):
    return jnp.concatenate([a[:, :HEAD_DIM]] * 4 + [a[:, HEAD_DIM:]] * 4, axis=1)


def _ffn_fwd(xin, g, w_in, w8, cb, w_out, tag):
    h = _rms_fwd(xin, g, f"rms_ffn{tag}")
    gu = _mm(h, w_in, f"mm_ffn_in{tag}")
    a = _ffn_mid_fwd(gu, w8, cb, f"ffn_mid{tag}")
    xout = _mm(a, w_out, f"mm_ffn_out{tag}", res=xin)
    return xout, (h, gu, a)


def _ffn_bwd(dx, dxb, xin, g, w_in_t, w8, cb, w_out_t, saved, tag):
    h, gu, a = saved
    da = _mm(dxb, w_out_t, f"mm_ffn_da{tag}")
    dw_out = _mm(a, dxb, f"mm_ffn_dwout{tag}", trans_a=True)
    dgc, dup, dw8, dcb = _ffn_mid_bwd1(gu, da, w8, cb, f"ffn_mid_bwd{tag}")
    dgate = _conv_bwd_data(dgc, w8, 3, f"ffn_conv_bwd{tag}")
    dh = _mm(dgate, w_in_t[:D_FF], f"mm_ffn_dh_g{tag}")
    dh = _mm(dup, w_in_t[D_FF:], f"mm_ffn_dh_u{tag}", res=dh)
    dw_in = jnp.concatenate([_mm(h, dgate, f"mm_ffn_dwin_g{tag}", trans_a=True),
                             _mm(h, dup, f"mm_ffn_dwin_u{tag}", trans_a=True)], axis=1)
    dxp, dxpb, dg = _rms_bwd(xin, g, dh, dx, f"rms_ffn_bwd{tag}")
    return dxp, dxpb, dg, dw_in, dw8[:3], dcb, dw_out


def kernel(x, norm_mix, norm_ffn, attn_w_in, attn_w_out, relpos_table, q_norm_a, k_norm_a, q_norm_b, k_norm_b, sinks, ssm_w_in, ssm_conv_w, ssm_conv_b, ssm_dt_bias, ssm_a_log, ssm_d, ssm_norm, ssm_w_out, ffn_w_in, ffn_conv_w, ffn_conv_b, ffn_w_out, loss_target, m_norm_mix, m_norm_ffn, m_attn_w_in, m_attn_w_out, m_relpos_table, m_q_norm_a, m_k_norm_a, m_q_norm_b, m_k_norm_b, m_sinks, m_ssm_w_in, m_ssm_conv_w, m_ssm_conv_b, m_ssm_dt_bias, m_ssm_a_log, m_ssm_d, m_ssm_norm, m_ssm_w_out, m_ffn_w_in, m_ffn_conv_w, m_ffn_conv_b, m_ffn_w_out, v_norm_mix, v_norm_ffn, v_attn_w_in, v_attn_w_out, v_relpos_table, v_q_norm_a, v_k_norm_a, v_q_norm_b, v_k_norm_b, v_sinks, v_ssm_w_in, v_ssm_conv_w, v_ssm_conv_b, v_ssm_dt_bias, v_ssm_a_log, v_ssm_d, v_ssm_norm, v_ssm_w_out, v_ffn_w_in, v_ffn_conv_w, v_ffn_conv_b, v_ffn_w_out):
    w = dict(norm_mix=norm_mix, norm_ffn=norm_ffn, attn_w_in=attn_w_in, attn_w_out=attn_w_out, relpos_table=relpos_table,
             q_norm_a=q_norm_a, k_norm_a=k_norm_a, q_norm_b=q_norm_b, k_norm_b=k_norm_b, sinks=sinks, ssm_w_in=ssm_w_in,
             ssm_conv_w=ssm_conv_w, ssm_conv_b=ssm_conv_b, ssm_dt_bias=ssm_dt_bias, ssm_a_log=ssm_a_log, ssm_d=ssm_d,
             ssm_norm=ssm_norm, ssm_w_out=ssm_w_out, ffn_w_in=ffn_w_in, ffn_conv_w=ffn_conv_w, ffn_conv_b=ffn_conv_b,
             ffn_w_out=ffn_w_out)
    mom = dict(norm_mix=m_norm_mix, norm_ffn=m_norm_ffn, attn_w_in=m_attn_w_in, attn_w_out=m_attn_w_out,
               relpos_table=m_relpos_table, q_norm_a=m_q_norm_a, k_norm_a=m_k_norm_a, q_norm_b=m_q_norm_b,
               k_norm_b=m_k_norm_b, sinks=m_sinks, ssm_w_in=m_ssm_w_in, ssm_conv_w=m_ssm_conv_w, ssm_conv_b=m_ssm_conv_b,
               ssm_dt_bias=m_ssm_dt_bias, ssm_a_log=m_ssm_a_log, ssm_d=m_ssm_d, ssm_norm=m_ssm_norm, ssm_w_out=m_ssm_w_out,
               ffn_w_in=m_ffn_w_in, ffn_conv_w=m_ffn_conv_w, ffn_conv_b=m_ffn_conv_b, ffn_w_out=m_ffn_w_out)
    var = dict(norm_mix=v_norm_mix, norm_ffn=v_norm_ffn, attn_w_in=v_attn_w_in, attn_w_out=v_attn_w_out,
               relpos_table=v_relpos_table, q_norm_a=v_q_norm_a, k_norm_a=v_k_norm_a, q_norm_b=v_q_norm_b,
               k_norm_b=v_k_norm_b, sinks=v_sinks, ssm_w_in=v_ssm_w_in, ssm_conv_w=v_ssm_conv_w, ssm_conv_b=v_ssm_conv_b,
               ssm_dt_bias=v_ssm_dt_bias, ssm_a_log=v_ssm_a_log, ssm_d=v_ssm_d, ssm_norm=v_ssm_norm, ssm_w_out=v_ssm_w_out,
               ffn_w_in=v_ffn_w_in, ffn_conv_w=v_ffn_conv_w, ffn_conv_b=v_ffn_conv_b, ffn_w_out=v_ffn_w_out)

    big = _exchange(_pack([w[n] for n in _BIG], BF16), False, "gather_big")
    small = _exchange(_pack([w[n] for n in _SMALL], F32), False, "gather_small")
    full = {}
    for names, buf in ((_BIG, big), (_SMALL, small)):
        for n, g in zip(names, _unpack(buf, [w[n].shape for n in names], lead=(N_DEV,))):
            full[n] = _join(g, _SHARD_AX[n])
    w_attn_in = full['attn_w_in'][0]
    w_attn_out = full['attn_w_out'][0]
    w_ssm_main = full['ssm_w_in'][0][:, :ZX]
    w_ssm_dt = jnp.pad(full['ssm_w_in'][0][:, ZX:], ((0, 0), (0, LANES - SSM_HEADS)))
    w_ssm_out = full['ssm_w_out'][0]
    w_ffn_in = full['ffn_w_in']
    w_ffn_out = full['ffn_w_out']
    ssm_cw8 = _rows8(full['ssm_conv_w'][0])
    ssm_cb = full['ssm_conv_b']
    ssm_nw = full['ssm_norm']
    ffn_cw8 = [_rows8(full['ffn_conv_w'][l]) for l in range(2)]
    ffn_cb = [ffn_conv_b[l:l + 1] for l in range(2)]

    x0 = x[0]
    target = loss_target[0]
    t = x0.shape[0]

    g_mix0, g_mix1 = norm_mix[0:1], norm_mix[1:2]
    g_ffn0, g_ffn1 = norm_ffn[0:1], norm_ffn[1:2]
    h0 = _rms_fwd(x0, g_mix0, "rms_mix0")
    proj = _mm(h0, w_attn_in, "mm_attn_in")
    hn_w = jnp.concatenate([jnp.tile(v, (1, 2)) for v in (q_norm_a, k_norm_a, q_norm_b, k_norm_b)], axis=0)
    qa, ka, va, qb, kb, vb = _headnorm_fwd(proj, hn_w, "headnorm")
    table = jnp.pad(relpos_table[0], ((0, 0), (0, REL_W - (2 * MAX_REL + 1))))
    bias_a = jnp.where(_band_mask(A_PREV, PAD_A)[None], jnp.transpose(_relpos_fwd(table, "relpos_bias"), (1, 0, 2)), NEG)
    rel_b = jnp.arange(TQ)[:, None] - (jnp.arange(PAD_B + TQ)[None, :] - PAD_B)
    slopes = 2.0 ** (-8.0 * jnp.arange(1, N_HEADS + 1, dtype=F32) / N_HEADS)
    bias_b = jnp.where(_band_mask(B_PREV, PAD_B)[None], -slopes[:, None, None] * jnp.abs(rel_b).astype(F32)[None], NEG)
    no_sinks = jnp.full((N_HEADS,), NEG, F32)
    kpa, vpa = _pad_rows(ka, PAD_A), _pad_rows(va, PAD_A)
    kpb, vpb = _pad_rows(_kv_expand(kb), PAD_B), _pad_rows(_kv_expand(vb), PAD_B)
    oa = _attn_fwd(qa, kpa, vpa, bias_a, no_sinks, PAD_A, "attn_a")
    ob = _attn_fwd(qb, kpb, vpb, bias_b, sinks[0], PAD_B, "attn_b")
    x1 = _mm(oa, w_attn_out[:512], "mm_attn_out_a", res=x0)
    x1 = _mm(ob, w_attn_out[512:], "mm_attn_out_b", res=x1)
    x2, ffn0_saved = _ffn_fwd(x1, g_ffn0, w_ffn_in[0], ffn_cw8[0], ffn_cb[0], w_ffn_out[0], "0")

    h2 = _rms_fwd(x2, g_mix1, "rms_mix1")
    zx = _mm(h2, w_ssm_main, "mm_ssm_in")
    dtraw = _mm(h2, w_ssm_dt, "mm_ssm_dt")
    dt_bias = _lanes128(ssm_dt_bias[0])
    alog = _lanes128(ssm_a_log[0])
    dexp = jnp.repeat(ssm_d[0], HEAD_DIM).reshape(1, D_INNER)
    xbc = _ssm_pre_fwd(zx, ssm_cw8, ssm_cb, "ssm_pre")
    dt = _dt_fwd(dtraw, dt_bias, "ssm_dt")
    y, sprev = _ssd_fwd(xbc, dt, alog, "ssd_fwd")
    y4 = _ssm_post_fwd(y, xbc, zx, dexp, ssm_nw, "ssm_post")
    x3 = _mm(y4, w_ssm_out, "mm_ssm_out", res=x2)
    x4, ffn1_saved = _ffn_fwd(x3, g_ffn1, w_ffn_in[1], ffn_cw8[1], ffn_cb[1], w_ffn_out[1], "1")

    dx4, dx4b, sq = _loss(x4, target, "loss")
    loss = lax.psum(0.5 * jnp.sum(sq) / D_MODEL, ("x", "y", "c"))

    grads = {}
    w_ffn_in_t = jnp.transpose(w_ffn_in, (0, 2, 1))
    w_ffn_out_t = jnp.transpose(w_ffn_out, (0, 2, 1))
    dx3, dx3b, dg_ffn1, dwin1, dcw1, dcb1, dwout1 = _ffn_bwd(
        dx4, dx4b, x3, g_ffn1, w_ffn_in_t[1], ffn_cw8[1], ffn_cb[1], w_ffn_out_t[1], ffn1_saved, "1")

    dy4 = _mm(dx3b, w_ssm_out.T, "mm_ssm_dy")
    grads['ssm_w_out'] = _mm(y4, dx3b, "mm_ssm_dwout", trans_a=True)[None]
    dyv, dskip, dz, dd_lane, dnw = _ssm_post_bwd(dy4, y, xbc, zx, dexp, ssm_nw, "ssm_post_bwd")
    dxbc, ddt, dalog = _ssd_bwd(xbc, dt, alog, sprev, dyv, dskip, "ssd_bwd")
    dc, dcw_s, dcb_s = _ssm_pre_bwd1(zx, dxbc, ssm_cw8, ssm_cb, "ssm_pre_bwd")
    ddtraw, ddtb = _dt_bwd(dtraw, dt_bias, ddt, "ssm_dt_bwd")
    dxr = _conv_bwd_data(dc, ssm_cw8, 4, "ssm_conv_bwd")
    w_main_t = w_ssm_main.T
    dh2 = _mm(dz, w_main_t[:D_INNER], "mm_ssm_dh_z")
    dh2 = _mm(dxr, w_main_t[D_INNER:], "mm_ssm_dh_x", res=dh2)
    dh2 = _mm(ddtraw, w_ssm_dt.T, "mm_ssm_dh_dt", res=dh2)
    grads['ssm_w_in'] = jnp.concatenate([
        _mm(h2, dz, "mm_ssm_dwin_z", trans_a=True), _mm(h2, dxr, "mm_ssm_dwin_x", trans_a=True),
        _mm(h2, ddtraw, "mm_ssm_dwin_dt", trans_a=True)[:, :SSM_HEADS]], axis=1)[None]
    dx2, dx2b, dg_mix1 = _rms_bwd(x2, g_mix1, dh2, dx3, "rms_mix1_bwd")
    grads['ssm_conv_w'] = dcw_s[:4][None]
    grads['ssm_conv_b'] = dcb_s
    grads['ssm_norm'] = dnw
    grads['ssm_dt_bias'] = ddtb[:, :SSM_HEADS]
    grads['ssm_a_log'] = dalog[:, :SSM_HEADS]
    grads['ssm_d'] = jnp.sum(dd_lane.reshape(SSM_HEADS, HEAD_DIM), axis=1)[None]

    dx1, dx1b, dg_ffn0, dwin0, dcw0, dcb0, dwout0 = _ffn_bwd(
        dx2, dx2b, x1, g_ffn0, w_ffn_in_t[0], ffn_cw8[0], ffn_cb[0], w_ffn_out_t[0], ffn0_saved, "0")
    grads['ffn_w_in'] = jnp.stack([dwin0, dwin1])
    grads['ffn_conv_w'] = jnp.stack([dcw0, dcw1])
    grads['ffn_conv_b'] = jnp.concatenate([dcb0, dcb1], axis=0)
    grads['ffn_w_out'] = jnp.stack([dwout0, dwout1])
    grads['norm_ffn'] = jnp.concatenate([dg_ffn0, dg_ffn1], axis=0)

    do = _mm(dx1b, w_attn_out.T, "mm_attn_do", out_dtype=BF16)
    grads['attn_w_out'] = jnp.concatenate([_mm(oa, dx1b, "mm_attn_dwout_a", trans_a=True),
                                           _mm(ob, dx1b, "mm_attn_dwout_b", trans_a=True)], axis=0)[None]
    dqa, dkpa, dvpa, dbias_a, _ = _attn_bwd(qa, kpa, vpa, bias_a, no_sinks, do, 0, PAD_A, "attn_a_bwd")
    dqb, dkpb, dvpb, _, dsink = _attn_bwd(qb, kpb, vpb, bias_b, sinks[0], do, 4, PAD_B, "attn_b_bwd")
    grads['relpos_table'] = _relpos_bwd(jnp.transpose(dbias_a, (1, 0, 2)), "relpos_bwd")[None, :, :2 * MAX_REL + 1]
    grads['sinks'] = dsink[:, :2, 0].reshape(1, N_HEADS)
    dproj, dhn = _headnorm_bwd(proj, hn_w, dqa, dkpa, dvpa, dqb, dkpb, dvpb, "headnorm_bwd")
    dhn = dhn[:, :HEAD_DIM] + dhn[:, HEAD_DIM:]
    for k, n in enumerate(('q_norm_a', 'k_norm_a', 'q_norm_b', 'k_norm_b')):
        grads[n] = dhn[k:k + 1]
    dh0 = _mm(dproj, w_attn_in.T, "mm_attn_dh")
    grads['attn_w_in'] = _mm(h0, dproj, "mm_attn_dwin", trans_a=True)[None]
    dx0, _, dg_mix0 = _rms_bwd(x0, g_mix0, dh0, dx1, "rms_mix0_bwd")
    grads['norm_mix'] = jnp.concatenate([dg_mix0, dg_mix1], axis=0)

    sh_shapes = [w[n].shape for n in _SHARDED]
    parts = _exchange(_pack([_split(grads[n], _SHARD_AX[n]) for n in _SHARDED], BF16, lead=(N_DEV,)), True, "scatter_grads")
    outs_sh = _adamw(parts, _pack([w[n] for n in _SHARDED], F32), _pack([mom[n] for n in _SHARDED], F32),
                     _pack([var[n] for n in _SHARDED], F32), "adamw_sharded")
    rp_shapes = [w[n].shape for n in _REPL]
    parts_r = _exchange(_pack([grads[n] for n in _REPL], F32), False, "gather_small_grads")
    outs_rp = _adamw(parts_r, _pack([w[n] for n in _REPL], F32), _pack([mom[n] for n in _REPL], F32),
                     _pack([var[n] for n in _REPL], F32), "adamw_replicated")
    res = [{}, {}, {}, {}]
    for names, shapes, outs in ((_SHARDED, sh_shapes, outs_sh), (_REPL, rp_shapes, outs_rp)):
        for kind, flat in enumerate(outs):
            for n, a in zip(names, _unpack(flat, shapes)):
                res[kind][n] = a
    return (loss, dx0[None], *[res[0][n] for n in _WEIGHTS], *[res[1][n] for n in _WEIGHTS],
            *[res[2][n] for n in _WEIGHTS], *[res[3][n] for n in _WEIGHTS])
```

```python
import jax
import jax.numpy as jnp
from jax import lax
from jax.experimental import pallas as pl
from jax.experimental.pallas import tpu as pltpu

F32 = jnp.float32
BF16 = jnp.bfloat16
HI = lax.Precision.HIGHEST
MESH = pl.DeviceIdType.MESH
NEG = -1e30

N_DEV = 8
D_MODEL = 1024
EPS = 1e-6
CHUNK = 64
HEAD_DIM = 64
N_HEADS = 8
A_PREV = 8
B_PREV = 2
MAX_REL = 256
TQ = 2 * CHUNK
PAD_A = A_PREV * CHUNK
PAD_B = B_PREV * CHUNK
REL_W = PAD_A + TQ
D_INNER = 2048
SSM_HEADS = 32
SSM_GROUPS = 4
SSM_STATE = 128
XBC = D_INNER + 2 * SSM_GROUPS * SSM_STATE
ZX = D_INNER + XBC
D_FF = 2816
SSD_L = 128
LANES = 128
VMEM_LIMIT = 48 << 20

ADAM_LR, ADAM_B1, ADAM_B2, ADAM_EPS, ADAM_WD, ADAM_STEP = 0.001, 0.9, 0.999, 1e-08, 0.01, 10


def _cp(sem=None):
    return pltpu.CompilerParams(dimension_semantics=sem, vmem_limit_bytes=VMEM_LIMIT)


def _dot(a, b, ca=1, cb=0, prec=None):
    return lax.dot_general(a, b, (((ca,), (cb,)), ((), ())), preferred_element_type=F32, precision=prec)


def _pick(n, cands):
    for c in cands:
        if n % c == 0:
            return c
    return n


def _lo_mask():
    return lax.broadcasted_iota(jnp.int32, (1, LANES), 1) < HEAD_DIM


def _mm(a, b, name, out_dtype=F32, res=None, trans_a=False):
    if trans_a:
        kdim, m = a.shape
    else:
        m, kdim = a.shape
    n = b.shape[1]
    assert b.shape[0] == kdim, (a.shape, b.shape)
    tm = _pick(m, (512, 1408, 256, 128))
    tn = _pick(n, (1024, 1152, 1408, 1280, 768, 512, 256, 128))
    if trans_a:
        tk = _pick(kdim, (512, 256, 128))
    else:
        tk = kdim if kdim <= 2048 else _pick(kdim, (1408, 1536, 1152, 1024, 512))
    nk = kdim // tk

    def body(*refs):
        if res is None:
            a_ref, b_ref, o_ref, acc = refs
        else:
            a_ref, b_ref, r_ref, o_ref, acc = refs
        k = pl.program_id(2)

        @pl.when(k == 0)
        def _():
            acc[...] = jnp.zeros_like(acc)

        acc[...] += _dot(a_ref[...], b_ref[...], 0 if trans_a else 1, 0)

        @pl.when(k == nk - 1)
        def _():
            r = acc[...]
            if res is not None:
                r = r + r_ref[...]
            o_ref[...] = r.astype(out_dtype)

    a_spec = pl.BlockSpec((tk, tm), lambda i, j, k: (k, i)) if trans_a else pl.BlockSpec((tm, tk), lambda i, j, k: (i, k))
    in_specs = [a_spec, pl.BlockSpec((tk, tn), lambda i, j, k: (k, j))]
    args = [a, b]
    if res is not None:
        in_specs.append(pl.BlockSpec((tm, tn), lambda i, j, k: (i, j)))
        args.append(res)
    return pl.pallas_call(
        body, name=name, grid=(m // tm, n // tn, nk), in_specs=in_specs,
        out_specs=pl.BlockSpec((tm, tn), lambda i, j, k: (i, j)),
        out_shape=jax.ShapeDtypeStruct((m, n), out_dtype),
        scratch_shapes=[pltpu.VMEM((tm, tn), F32)],
        compiler_params=_cp(("parallel", "parallel", "arbitrary")),
    )(*args)


def _rms_fwd(x, g, name):
    t, d = x.shape
    tm = _pick(t, (512, 256, 128))

    def body(x_ref, g_ref, h_ref):
        xv = x_ref[...]
        r = lax.rsqrt(jnp.mean(xv * xv, axis=-1, keepdims=True) + EPS)
        h_ref[...] = (xv * r * g_ref[...]).astype(BF16)

    return pl.pallas_call(
        body, name=name, grid=(t // tm,),
        in_specs=[pl.BlockSpec((tm, d), lambda i: (i, 0)), pl.BlockSpec((1, d), lambda i: (0, 0))],
        out_specs=pl.BlockSpec((tm, d), lambda i: (i, 0)),
        out_shape=jax.ShapeDtypeStruct((t, d), BF16), compiler_params=_cp(("parallel",)),
    )(x, g)


def _rms_bwd(x, g, dh, dres, name):
    t, d = x.shape
    tm = _pick(t, (512, 256, 128))

    def body(x_ref, g_ref, dh_ref, dr_ref, dx_ref, dxb_ref, dg_ref):
        i = pl.program_id(0)
        xv = x_ref[...]
        r = lax.rsqrt(jnp.mean(xv * xv, axis=-1, keepdims=True) + EPS)
        xh = xv * r
        dhv = dh_ref[...]
        dxh = dhv * g_ref[...]
        dx = dr_ref[...] + r * (dxh - xh * jnp.mean(dxh * xh, axis=-1, keepdims=True))
        dx_ref[...] = dx
        dxb_ref[...] = dx.astype(BF16)

        @pl.when(i == 0)
        def _():
            dg_ref[...] = jnp.zeros_like(dg_ref)

        dg_ref[...] += jnp.sum(dhv * xh, axis=0, keepdims=True)

    row = pl.BlockSpec((tm, d), lambda i: (i, 0))
    vec = pl.BlockSpec((1, d), lambda i: (0, 0))
    return pl.pallas_call(
        body, name=name, grid=(t // tm,), in_specs=[row, vec, row, row], out_specs=[row, row, vec],
        out_shape=[jax.ShapeDtypeStruct((t, d), F32), jax.ShapeDtypeStruct((t, d), BF16), jax.ShapeDtypeStruct((1, d), F32)],
        compiler_params=_cp(("arbitrary",)),
    )(x, g, dh, dres)


def _head_rms(xs, w, lo):
    sq = xs * xs
    s0 = jnp.sum(jnp.where(lo, sq, 0.0), axis=-1, keepdims=True)
    s1 = jnp.sum(jnp.where(lo, 0.0, sq), axis=-1, keepdims=True)
    r = jnp.where(lo, lax.rsqrt(s0 * (1.0 / HEAD_DIM) + EPS), lax.rsqrt(s1 * (1.0 / HEAD_DIM) + EPS))
    return xs * r, r


def _head_rms_bwd(xs, w, dy, lo):
    xh, r = _head_rms(xs, w, lo)
    dxh = dy * w
    t = dxh * xh
    m0 = jnp.sum(jnp.where(lo, t, 0.0), axis=-1, keepdims=True)
    m1 = jnp.sum(jnp.where(lo, 0.0, t), axis=-1, keepdims=True)
    mm = jnp.where(lo, m0, m1) * (1.0 / HEAD_DIM)
    return r * (dxh - xh * mm), dy * xh


_QSCALE = HEAD_DIM ** -0.5


def _headnorm_fwd(proj, ws, name):
    t = proj.shape[0]
    tm = _pick(t, (256, 128))

    def body(p_ref, w_ref, qa_ref, ka_ref, va_ref, qb_ref, kb_ref, vb_ref):
        lo = _lo_mask()
        for s in range(4):
            c = LANES * s
            xh, _ = _head_rms(p_ref[:, c:c + LANES], None, lo)
            qa_ref[:, c:c + LANES] = (xh * w_ref[0:1, :] * _QSCALE).astype(BF16)
            xh, _ = _head_rms(p_ref[:, 512 + c:512 + c + LANES], None, lo)
            ka_ref[:, c:c + LANES] = (xh * w_ref[1:2, :]).astype(BF16)
            xh, _ = _head_rms(p_ref[:, 1536 + c:1536 + c + LANES], None, lo)
            qb_ref[:, c:c + LANES] = (xh * w_ref[2:3, :] * _QSCALE).astype(BF16)
        va_ref[...] = p_ref[:, 1024:1536].astype(BF16)
        xh, _ = _head_rms(p_ref[:, 2048:2176], None, lo)
        kb_ref[...] = (xh * w_ref[3:4, :]).astype(BF16)
        vb_ref[...] = p_ref[:, 2176:2304].astype(BF16)

    wide = pl.BlockSpec((tm, 512), lambda i: (i, 0))
    narrow = pl.BlockSpec((tm, LANES), lambda i: (i, 0))
    sd = lambda n: jax.ShapeDtypeStruct((t, n), BF16)
    return pl.pallas_call(
        body, name=name, grid=(t // tm,),
        in_specs=[pl.BlockSpec((tm, 2304), lambda i: (i, 0)), pl.BlockSpec((4, LANES), lambda i: (0, 0))],
        out_specs=[wide, wide, wide, wide, narrow, narrow],
        out_shape=[sd(512), sd(512), sd(512), sd(512), sd(LANES), sd(LANES)],
        compiler_params=_cp(("parallel",)),
    )(proj, ws)


def _headnorm_bwd(proj, ws, dqa, dkpa, dvpa, dqb, dkpb, dvpb, name):
    t = proj.shape[0]
    tm = TQ
    offa, offb = PAD_A // tm, PAD_B // tm

    def body(p_ref, w_ref, dqa_ref, dka_ref, dva_ref, dqb_ref, dkb_ref, dvb_ref, dp_ref, dw_ref):
        i = pl.program_id(0)
        lo = _lo_mask()

        @pl.when(i == 0)
        def _():
            dw_ref[...] = jnp.zeros_like(dw_ref)

        acc = [jnp.zeros((1, LANES), F32) for _ in range(4)]
        for s in range(4):
            c = LANES * s
            dx, dwl = _head_rms_bwd(p_ref[:, c:c + LANES], w_ref[0:1, :], dqa_ref[:, c:c + LANES] * _QSCALE, lo)
            dp_ref[:, c:c + LANES] = dx.astype(BF16)
            acc[0] += jnp.sum(dwl, axis=0, keepdims=True)
            dx, dwl = _head_rms_bwd(p_ref[:, 512 + c:512 + c + LANES], w_ref[1:2, :], dka_ref[:, c:c + LANES], lo)
            dp_ref[:, 512 + c:512 + c + LANES] = dx.astype(BF16)
            acc[1] += jnp.sum(dwl, axis=0, keepdims=True)
            dx, dwl = _head_rms_bwd(p_ref[:, 1536 + c:1536 + c + LANES], w_ref[2:3, :], dqb_ref[:, c:c + LANES] * _QSCALE, lo)
            dp_ref[:, 1536 + c:1536 + c + LANES] = dx.astype(BF16)
            acc[2] += jnp.sum(dwl, axis=0, keepdims=True)
        dp_ref[:, 1024:1536] = dva_ref[...].astype(BF16)

        def group_sum(ref):
            s0 = ref[:, 0:128] + ref[:, 128:256]
            s1 = ref[:, 256:384] + ref[:, 384:512]
            s0 = s0 + pltpu.roll(s0, HEAD_DIM, 1)
            s1 = s1 + pltpu.roll(s1, HEAD_DIM, 1)
            return jnp.where(lo, s0, s1)

        dx, dwl = _head_rms_bwd(p_ref[:, 2048:2176], w_ref[3:4, :], group_sum(dkb_ref), lo)
        dp_ref[:, 2048:2176] = dx.astype(BF16)
        acc[3] += jnp.sum(dwl, axis=0, keepdims=True)
        dp_ref[:, 2176:2304] = group_sum(dvb_ref).astype(BF16)
        for n in range(4):
            dw_ref[n:n + 1, :] += acc[n]

    wide = pl.BlockSpec((tm, 512), lambda i: (i, 0))
    pa = pl.BlockSpec((tm, 512), lambda i: (i + offa, 0))
    pb = pl.BlockSpec((tm, 512), lambda i: (i + offb, 0))
    return pl.pallas_call(
        body, name=name, grid=(t // tm,),
        in_specs=[pl.BlockSpec((tm, 2304), lambda i: (i, 0)), pl.BlockSpec((4, LANES), lambda i: (0, 0)),
                  wide, pa, pa, wide, pb, pb],
        out_specs=[pl.BlockSpec((tm, 2304), lambda i: (i, 0)), pl.BlockSpec((4, LANES), lambda i: (0, 0))],
        out_shape=[jax.ShapeDtypeStruct((t, 2304), BF16), jax.ShapeDtypeStruct((4, LANES), F32)],
        compiler_params=_cp(("arbitrary",)),
    )(proj, ws, dqa, dkpa, dvpa, dqb, dkpb, dvpb)


ROLL_W = 1024


def _rel_onehot():
    r_io = lax.broadcasted_iota(jnp.int32, (REL_W, ROLL_W), 0)
    m_io = lax.broadcasted_iota(jnp.int32, (REL_W, ROLL_W), 1)
    return (r_io == jnp.clip(REL_W - 1 - m_io, -MAX_REL, MAX_REL) + MAX_REL).astype(F32)


def _relpos_fwd(table, name):
    def body(t_ref, o_ref):
        rr = _dot(t_ref[...], _rel_onehot(), 1, 0, HI)

        def step(q, c):
            o_ref[q] = pltpu.roll(rr, (ROLL_W - (TQ - 1) + q) % ROLL_W, 1)[:, :REL_W]
            return c

        lax.fori_loop(0, TQ, step, 0)

    return pl.pallas_call(
        body, name=name, out_shape=jax.ShapeDtypeStruct((TQ, N_HEADS, REL_W), F32),
        in_specs=[pl.BlockSpec(memory_space=pltpu.VMEM)], out_specs=pl.BlockSpec(memory_space=pltpu.VMEM),
        compiler_params=_cp(),
    )(table)


def _relpos_bwd(dbias_t, name):
    def body(d_ref, o_ref):
        def step(q, acc):
            row = jnp.concatenate([d_ref[q], jnp.zeros((N_HEADS, ROLL_W - REL_W), F32)], axis=1)
            return acc + pltpu.roll(row, TQ - 1 - q, 1)

        drr = lax.fori_loop(0, TQ, step, jnp.zeros((N_HEADS, ROLL_W), F32))
        o_ref[...] = _dot(drr, _rel_onehot(), 1, 1, HI)

    return pl.pallas_call(
        body, name=name, out_shape=jax.ShapeDtypeStruct((N_HEADS, REL_W), F32),
        in_specs=[pl.BlockSpec(memory_space=pltpu.VMEM)], out_specs=pl.BlockSpec(memory_space=pltpu.VMEM),
        compiler_params=_cp(),
    )(dbias_t)


def _attn_probs(qe, kw, bias, kvalid, snk):
    s = _dot(qe, kw, 1, 1) + bias
    s = jnp.where(kvalid, s, NEG)
    m = jnp.maximum(jnp.max(s, axis=-1, keepdims=True), snk)
    p = jnp.exp(s - m)
    inv = 1.0 / (jnp.sum(p, axis=-1, keepdims=True) + jnp.exp(snk - m))
    return p * inv, jnp.exp(snk - m) * inv


def _attn_fwd(q, kp, vp, bias, sinks, pad, name):
    t, hd = q.shape
    w = pad + TQ

    def body(sink_ref, q_ref, k_ref, v_ref, b_ref, o_ref):
        hp, i = pl.program_id(0), pl.program_id(1)
        start = pl.multiple_of(i * TQ, TQ)
        qv = q_ref[...]
        kw = k_ref[pl.ds(start, w), :]
        vw = v_ref[pl.ds(start, w), :]
        lo = _lo_mask()
        kvalid = (start + lax.broadcasted_iota(jnp.int32, (1, w), 1)) >= pad
        outs = []
        for e in range(2):
            sel = lo if e == 0 else jnp.logical_not(lo)
            qe = jnp.where(sel, qv, jnp.zeros_like(qv))
            p, _ = _attn_probs(qe, kw, b_ref[e], kvalid, sink_ref[2 * hp + e])
            outs.append(_dot(p.astype(BF16), vw, 1, 0))
        o_ref[...] = jnp.where(lo, outs[0], outs[1]).astype(BF16)

    full = pl.BlockSpec((t + pad, LANES), lambda h, i: (0, h))
    tile = pl.BlockSpec((TQ, LANES), lambda h, i: (i, h))
    return pl.pallas_call(
        body, name=name, grid=(hd // LANES, t // TQ),
        in_specs=[pl.BlockSpec(memory_space=pltpu.SMEM), tile, full, full, pl.BlockSpec((2, TQ, w), lambda h, i: (h, 0, 0))],
        out_specs=tile, out_shape=jax.ShapeDtypeStruct((t, hd), BF16),
        compiler_params=_cp(("parallel", "arbitrary")),
    )(sinks, q, kp, vp, bias)


def _attn_bwd(q, kp, vp, bias, sinks, do, col_off, pad, name):
    t, hd = q.shape
    w = pad + TQ
    nhp = hd // LANES

    def body(sink_ref, q_ref, k_ref, v_ref, b_ref, do_ref, dq_ref, dk_ref, dv_ref, db_ref, ds_ref):
        hp, i = pl.program_id(0), pl.program_id(1)
        start = pl.multiple_of(i * TQ, TQ)

        @pl.when(i == 0)
        def _():
            dk_ref[...] = jnp.zeros_like(dk_ref)
            dv_ref[...] = jnp.zeros_like(dv_ref)
            db_ref[...] = jnp.zeros_like(db_ref)
            ds_ref[...] = jnp.zeros_like(ds_ref)

        qv = q_ref[...]
        dov = do_ref[...]
        kw = k_ref[pl.ds(start, w), :]
        vw = v_ref[pl.ds(start, w), :]
        lo = _lo_mask()
        kvalid = (start + lax.broadcasted_iota(jnp.int32, (1, w), 1)) >= pad
        row8 = lax.broadcasted_iota(jnp.int32, (8, LANES), 0)
        dqs, dkw, dvw = [], None, None
        for e in range(2):
            sel = lo if e == 0 else jnp.logical_not(lo)
            qe = jnp.where(sel, qv, jnp.zeros_like(qv))
            doe = jnp.where(sel, dov, jnp.zeros_like(dov))
            p, psink = _attn_probs(qe, kw, b_ref[e], kvalid, sink_ref[2 * hp + e])
            dp = _dot(doe, vw, 1, 1)
            delta = jnp.sum(p * dp, axis=-1, keepdims=True)
            ds = p * (dp - delta)
            db_ref[e] += ds
            dsnk = jnp.sum(-psink * delta, axis=0, keepdims=True)
            ds_ref[0] += jnp.where(row8 == e, dsnk, 0.0)
            dsb = ds.astype(BF16)
            dqs.append(_dot(dsb, kw, 1, 0))
            dk_e = _dot(dsb, qe, 0, 0)
            dv_e = _dot(p.astype(BF16), doe, 0, 0)
            dkw = dk_e if dkw is None else dkw + dk_e
            dvw = dv_e if dvw is None else dvw + dv_e
        dq_ref[...] = jnp.where(lo, dqs[0], dqs[1])
        dk_ref[pl.ds(start, w), :] += dkw
        dv_ref[pl.ds(start, w), :] += dvw

    full = pl.BlockSpec((t + pad, LANES), lambda h, i: (0, h))
    tile = pl.BlockSpec((TQ, LANES), lambda h, i: (i, h))
    btile = pl.BlockSpec((2, TQ, w), lambda h, i: (h, 0, 0))
    return pl.pallas_call(
        body, name=name, grid=(nhp, t // TQ),
        in_specs=[pl.BlockSpec(memory_space=pltpu.SMEM), tile, full, full, btile,
                  pl.BlockSpec((TQ, LANES), lambda h, i: (i, h + col_off))],
        out_specs=[tile, full, full, btile, pl.BlockSpec((1, 8, LANES), lambda h, i: (h, 0, 0))],
        out_shape=[jax.ShapeDtypeStruct((t, hd), F32), jax.ShapeDtypeStruct((t + pad, hd), F32),
                   jax.ShapeDtypeStruct((t + pad, hd), F32), jax.ShapeDtypeStruct((N_HEADS, TQ, w), F32),
                   jax.ShapeDtypeStruct((nhp, 8, LANES), F32)],
        compiler_params=_cp(("parallel", "arbitrary")),
    )(sinks, q, kp, vp, bias, do)


def _halo_prev(tm):
    return lambda i: jnp.maximum(i * (tm // 8) - 1, 0)


def _halo_next(tm, t):
    return lambda i: jnp.minimum((i + 1) * (tm // 8), t // 8 - 1)


def _taps_prev(tile, halo, ktaps, first):
    tm = tile.shape[0]
    ext = jnp.concatenate([jnp.where(first, 0.0, halo), tile], axis=0)
    return [tile] + [pltpu.roll(ext, s, 0)[8:8 + tm] for s in range(1, ktaps)]


def _taps_next(tile, halo, ktaps, last):
    tm = tile.shape[0]
    ext = jnp.concatenate([tile, jnp.where(last, 0.0, halo)], axis=0)
    return [tile] + [pltpu.roll(ext, tm + 8 - s, 0)[0:tm] for s in range(1, ktaps)]


def _conv_apply(taps, w_ref, ktaps):
    out = taps[0] * w_ref[ktaps - 1:ktaps, :]
    for s in range(1, ktaps):
        out = out + taps[s] * w_ref[ktaps - 1 - s:ktaps - s, :]
    return out


def _silu_grad(x):
    sg = jax.nn.sigmoid(x)
    return x * sg, sg * (1.0 + x * (1.0 - sg))


FFN_TM = 128


def _ffn_mid_fwd(gu, w8, b, name):
    t = gu.shape[0]
    f = D_FF
    tm = FFN_TM

    def body(g_ref, u_ref, h_ref, w_ref, b_ref, a_ref):
        first = pl.program_id(0) == 0
        gc = _conv_apply(_taps_prev(g_ref[...], h_ref[...], 3, first), w_ref, 3) + b_ref[...]
        a_ref[...] = (gc * jax.nn.sigmoid(gc) * u_ref[...]).astype(BF16)

    hp = _halo_prev(tm)
    return pl.pallas_call(
        body, name=name, grid=(t // tm,),
        in_specs=[pl.BlockSpec((tm, f), lambda i: (i, 0)), pl.BlockSpec((tm, f), lambda i: (i, 1)),
                  pl.BlockSpec((8, f), lambda i: (hp(i), 0)), pl.BlockSpec((8, f), lambda i: (0, 0)),
                  pl.BlockSpec((1, f), lambda i: (0, 0))],
        out_specs=pl.BlockSpec((tm, f), lambda i: (i, 0)), out_shape=jax.ShapeDtypeStruct((t, f), BF16),
        compiler_params=_cp(("parallel",)),
    )(gu, gu, gu, w8, b)


def _ffn_mid_bwd1(gu, da, w8, b, name):
    t = gu.shape[0]
    f = D_FF
    tm = FFN_TM

    def body(g_ref, u_ref, h_ref, da_ref, w_ref, b_ref, dgc_ref, dup_ref, dw_ref, db_ref):
        i = pl.program_id(0)

        @pl.when(i == 0)
        def _():
            dw_ref[...] = jnp.zeros_like(dw_ref)
            db_ref[...] = jnp.zeros_like(db_ref)

        taps = _taps_prev(g_ref[...], h_ref[...], 3, i == 0)
        gc = _conv_apply(taps, w_ref, 3) + b_ref[...]
        act, dact = _silu_grad(gc)
        dav = da_ref[...]
        dup_ref[...] = (dav * act).astype(BF16)
        dgc = dav * u_ref[...] * dact
        dgc_ref[...] = dgc
        db_ref[...] += jnp.sum(dgc, axis=0, keepdims=True)
        for s in range(3):
            dw_ref[2 - s:3 - s, :] += jnp.sum(dgc * taps[s], axis=0, keepdims=True)

    hp = _halo_prev(tm)
    row = pl.BlockSpec((tm, f), lambda i: (i, 0))
    return pl.pallas_call(
        body, name=name, grid=(t // tm,),
        in_specs=[row, pl.BlockSpec((tm, f), lambda i: (i, 1)), pl.BlockSpec((8, f), lambda i: (hp(i), 0)), row,
                  pl.BlockSpec((8, f), lambda i: (0, 0)), pl.BlockSpec((1, f), lambda i: (0, 0))],
        out_specs=[row, row, pl.BlockSpec((8, f), lambda i: (0, 0)), pl.BlockSpec((1, f), lambda i: (0, 0))],
        out_shape=[jax.ShapeDtypeStruct((t, f), F32), jax.ShapeDtypeStruct((t, f), BF16),
                   jax.ShapeDtypeStruct((8, f), F32), jax.ShapeDtypeStruct((1, f), F32)],
        compiler_params=_cp(("arbitrary",)),
    )(gu, gu, gu, da, w8, b)


def _conv_bwd_data(dc, w8, ktaps, name):
    t, c = dc.shape
    tm = 256
    tc = _pick(c, (1408, 1024))
    nt = t // tm

    def body(d_ref, h_ref, w_ref, o_ref):
        last = pl.program_id(0) == nt - 1
        o_ref[...] = _conv_apply(_taps_next(d_ref[...], h_ref[...], ktaps, last), w_ref, ktaps).astype(BF16)

    hn = _halo_next(tm, t)
    return pl.pallas_call(
        body, name=name, grid=(nt, c // tc),
        in_specs=[pl.BlockSpec((tm, tc), lambda i, j: (i, j)), pl.BlockSpec((8, tc), lambda i, j: (hn(i), j)),
                  pl.BlockSpec((8, tc), lambda i, j: (0, j))],
        out_specs=pl.BlockSpec((tm, tc), lambda i, j: (i, j)), out_shape=jax.ShapeDtypeStruct((t, c), BF16),
        compiler_params=_cp(("parallel", "parallel")),
    )(dc, dc, w8)


PRE_TM = 256
PRE_TC = 1024


def _ssm_pre_fwd(zx, w8, b, name):
    t = zx.shape[0]
    tm, tc = PRE_TM, PRE_TC
    off = D_INNER // tc

    def body(x_ref, h_ref, w_ref, b_ref, o_ref):
        first = pl.program_id(0) == 0
        c = _conv_apply(_taps_prev(x_ref[...], h_ref[...], 4, first), w_ref, 4) + b_ref[...]
        o_ref[...] = c * jax.nn.sigmoid(c)

    hp = _halo_prev(tm)
    return pl.pallas_call(
        body, name=name, grid=(t // tm, XBC // tc),
        in_specs=[pl.BlockSpec((tm, tc), lambda i, j: (i, j + off)), pl.BlockSpec((8, tc), lambda i, j: (hp(i), j + off)),
                  pl.BlockSpec((8, tc), lambda i, j: (0, j)), pl.BlockSpec((1, tc), lambda i, j: (0, j))],
        out_specs=pl.BlockSpec((tm, tc), lambda i, j: (i, j)), out_shape=jax.ShapeDtypeStruct((t, XBC), F32),
        compiler_params=_cp(("parallel", "parallel")),
    )(zx, zx, w8, b)


def _ssm_pre_bwd1(zx, dxbc, w8, b, name):
    t = zx.shape[0]
    tm, tc = PRE_TM, PRE_TC
    off = D_INNER // tc

    def body(x_ref, h_ref, d_ref, w_ref, b_ref, dc_ref, dw_ref, db_ref):
        i = pl.program_id(1)

        @pl.when(i == 0)
        def _():
            dw_ref[...] = jnp.zeros_like(dw_ref)
            db_ref[...] = jnp.zeros_like(db_ref)

        taps = _taps_prev(x_ref[...], h_ref[...], 4, i == 0)
        c = _conv_apply(taps, w_ref, 4) + b_ref[...]
        _, dact = _silu_grad(c)
        dc = d_ref[...] * dact
        dc_ref[...] = dc
        db_ref[...] += jnp.sum(dc, axis=0, keepdims=True)
        for s in range(4):
            dw_ref[3 - s:4 - s, :] += jnp.sum(dc * taps[s], axis=0, keepdims=True)

    hp = _halo_prev(tm)
    return pl.pallas_call(
        body, name=name, grid=(XBC // tc, t // tm),
        in_specs=[pl.BlockSpec((tm, tc), lambda j, i: (i, j + off)), pl.BlockSpec((8, tc), lambda j, i: (hp(i), j + off)),
                  pl.BlockSpec((tm, tc), lambda j, i: (i, j)),
                  pl.BlockSpec((8, tc), lambda j, i: (0, j)), pl.BlockSpec((1, tc), lambda j, i: (0, j))],
        out_specs=[pl.BlockSpec((tm, tc), lambda j, i: (i, j)), pl.BlockSpec((8, tc), lambda j, i: (0, j)),
                   pl.BlockSpec((1, tc), lambda j, i: (0, j))],
        out_shape=[jax.ShapeDtypeStruct((t, XBC), F32), jax.ShapeDtypeStruct((8, XBC), F32),
                   jax.ShapeDtypeStruct((1, XBC), F32)],
        compiler_params=_cp(("parallel", "arbitrary")),
    )(zx, zx, dxbc, w8, b)


def _head_lanes():
    return lax.broadcasted_iota(jnp.int32, (1, LANES), 1) < SSM_HEADS


def _dt_fwd(dtraw, bias, name):
    t = dtraw.shape[0]
    tm = _pick(t, (1024, 512, 256, 128))

    def body(x_ref, b_ref, o_ref):
        v = x_ref[...] + b_ref[...]
        sp = jnp.maximum(v, 0.0) + jnp.log(1.0 + jnp.exp(-jnp.abs(v)))
        o_ref[...] = jnp.where(_head_lanes(), sp, 0.0)

    row = pl.BlockSpec((tm, LANES), lambda i: (i, 0))
    return pl.pallas_call(
        body, name=name, grid=(t // tm,), in_specs=[row, pl.BlockSpec((1, LANES), lambda i: (0, 0))], out_specs=row,
        out_shape=jax.ShapeDtypeStruct((t, LANES), F32), compiler_params=_cp(("parallel",)),
    )(dtraw, bias)


def _dt_bwd(dtraw, bias, ddt, name):
    t = dtraw.shape[0]
    tm = _pick(t, (1024, 512, 256, 128))

    def body(x_ref, b_ref, d_ref, o_ref, db_ref):
        @pl.when(pl.program_id(0) == 0)
        def _():
            db_ref[...] = jnp.zeros_like(db_ref)

        g = jnp.where(_head_lanes(), d_ref[...] * jax.nn.sigmoid(x_ref[...] + b_ref[...]), 0.0)
        o_ref[...] = g.astype(BF16)
        db_ref[...] += jnp.sum(g, axis=0, keepdims=True)

    row = pl.BlockSpec((tm, LANES), lambda i: (i, 0))
    vec = pl.BlockSpec((1, LANES), lambda i: (0, 0))
    return pl.pallas_call(
        body, name=name, grid=(t // tm,), in_specs=[row, vec, row], out_specs=[row, vec],
        out_shape=[jax.ShapeDtypeStruct((t, LANES), BF16), jax.ShapeDtypeStruct((1, LANES), F32)],
        compiler_params=_cp(("arbitrary",)),
    )(dtraw, bias, ddt)


GROUP_W = D_INNER // SSM_GROUPS
POST_TM = 512


def _ssm_post_fwd(y, xbc, zx, dexp, nw, name):
    t = y.shape[0]
    tm = _pick(t, (POST_TM, 256, 128))

    def body(y_ref, x_ref, z_ref, d_ref, w_ref, o_ref):
        zv = z_ref[...]
        y3 = (y_ref[...] + d_ref[...] * x_ref[...]) * (zv * jax.nn.sigmoid(zv))
        r = lax.rsqrt(jnp.mean(y3 * y3, axis=-1, keepdims=True) + EPS)
        o_ref[...] = (y3 * r * w_ref[...]).astype(BF16)

    blk = pl.BlockSpec((tm, GROUP_W), lambda i, g: (i, g))
    vec = pl.BlockSpec((1, GROUP_W), lambda i, g: (0, g))
    return pl.pallas_call(
        body, name=name, grid=(t // tm, SSM_GROUPS), in_specs=[blk, blk, blk, vec, vec], out_specs=blk,
        out_shape=jax.ShapeDtypeStruct((t, D_INNER), BF16), compiler_params=_cp(("parallel", "parallel")),
    )(y, xbc, zx, dexp, nw)


def _ssm_post_bwd(dy4, y, xbc, zx, dexp, nw, name):
    t = y.shape[0]
    tm = _pick(t, (POST_TM, 256, 128))

    def body(g_ref, y_ref, x_ref, z_ref, d_ref, w_ref, dy_ref, dxs_ref, dz_ref, dd_ref, dw_ref):
        @pl.when(pl.program_id(1) == 0)
        def _():
            dd_ref[...] = jnp.zeros_like(dd_ref)
            dw_ref[...] = jnp.zeros_like(dw_ref)

        zv = z_ref[...]
        xv = x_ref[...]
        act, dact = _silu_grad(zv)
        y2 = y_ref[...] + d_ref[...] * xv
        y3 = y2 * act
        r = lax.rsqrt(jnp.mean(y3 * y3, axis=-1, keepdims=True) + EPS)
        y3n = y3 * r
        gv = g_ref[...]
        dyn = gv * w_ref[...]
        dy3 = r * (dyn - y3n * jnp.mean(dyn * y3n, axis=-1, keepdims=True))
        dy2 = dy3 * act
        dy_ref[...] = dy2
        dxs_ref[...] = dy2 * d_ref[...]
        dz_ref[...] = (dy3 * y2 * dact).astype(BF16)
        dd_ref[...] += jnp.sum(dy2 * xv, axis=0, keepdims=True)
        dw_ref[...] += jnp.sum(gv * y3n, axis=0, keepdims=True)

    blk = pl.BlockSpec((tm, GROUP_W), lambda g, i: (i, g))
    vec = pl.BlockSpec((1, GROUP_W), lambda g, i: (0, g))
    return pl.pallas_call(
        body, name=name, grid=(SSM_GROUPS, t // tm), in_specs=[blk, blk, blk, blk, vec, vec],
        out_specs=[blk, blk, blk, vec, vec],
        out_shape=[jax.ShapeDtypeStruct((t, D_INNER), F32), jax.ShapeDtypeStruct((t, D_INNER), F32),
                   jax.ShapeDtypeStruct((t, D_INNER), BF16), jax.ShapeDtypeStruct((1, D_INNER), F32),
                   jax.ShapeDtypeStruct((1, D_INNER), F32)],
        compiler_params=_cp(("parallel", "arbitrary")),
    )(dy4, y, xbc, zx, dexp, nw)


def _ssd_common(dt, alog):
    ll = dt.shape[0]
    a_neg = -jnp.exp(alog)
    a = dt * a_neg
    ri = lax.broadcasted_iota(jnp.int32, (ll, ll), 0)
    ci = lax.broadcasted_iota(jnp.int32, (ll, ll), 1)
    tril = ri >= ci
    acs = _dot(tril.astype(F32), a, 1, 0, HI)
    return a_neg, tril, acs, acs.T


def _pair_terms(acs, acs_t, dt, h0, lo):
    ll = acs.shape[0]
    cols = [acs[:, h0 + e:h0 + e + 1] for e in range(2)]
    rows = [acs_t[h0 + e:h0 + e + 1, :] for e in range(2)]
    dtc = [dt[:, h0 + e:h0 + e + 1] for e in range(2)]
    lasts = [c[ll - 1:ll, :] for c in cols]
    dtx = jnp.where(lo, dtc[0], dtc[1])
    eac = jnp.where(lo, jnp.exp(cols[0]), jnp.exp(cols[1]))
    fdec = jnp.where(lo, jnp.exp(lasts[0] - cols[0]), jnp.exp(lasts[1] - cols[1]))
    elast = jnp.where(lo, jnp.exp(lasts[0]), jnp.exp(lasts[1]))
    return cols, rows, dtx, eac, fdec, elast


def _decay(col, row, tril):
    return jnp.where(tril, jnp.exp(jnp.minimum(col - row, 0.0)), 0.0)


def _ssd_fwd(xbc, dt, alog, name):
    t = xbc.shape[0]
    ll = SSD_L
    nc = t // ll

    def body(x_ref, dt_ref, al_ref, y_ref, sp_ref, st_ref):
        @pl.when(pl.program_id(0) == 0)
        def _():
            st_ref[...] = jnp.zeros_like(st_ref)

        dtv = dt_ref[...]
        _, tril, acs, acs_t = _ssd_common(dtv, al_ref[...])
        lo = _lo_mask()
        sp_ref[0] = st_ref[...]
        for g in range(SSM_GROUPS):
            bg = x_ref[:, D_INNER + SSM_STATE * g:D_INNER + SSM_STATE * (g + 1)].astype(BF16)
            cg = x_ref[:, D_INNER + 512 + SSM_STATE * g:D_INNER + 512 + SSM_STATE * (g + 1)].astype(BF16)
            gm = _dot(cg, bg, 1, 1)
            for pp in range(4):
                h0 = 8 * g + 2 * pp
                c0 = HEAD_DIM * h0
                x2 = x_ref[:, c0:c0 + LANES]
                cols, rows, dtx, eac, fdec, elast = _pair_terms(acs, acs_t, dtv, h0, lo)
                u = (x2 * dtx).astype(BF16)
                ys = [_dot((gm * _decay(cols[e], rows[e], tril)).astype(BF16), u, 1, 0) for e in range(2)]
                s2 = st_ref[:, c0:c0 + LANES]
                y_ref[:, c0:c0 + LANES] = jnp.where(lo, ys[0], ys[1]) + _dot(cg, s2.astype(BF16), 1, 0) * eac
                xw = (x2 * (fdec * dtx)).astype(BF16)
                st_ref[:, c0:c0 + LANES] = s2 * elast + _dot(bg, xw, 0, 0)

    return pl.pallas_call(
        body, name=name, grid=(nc,),
        in_specs=[pl.BlockSpec((ll, XBC), lambda c: (c, 0)), pl.BlockSpec((ll, LANES), lambda c: (c, 0)),
                  pl.BlockSpec((1, LANES), lambda c: (0, 0))],
        out_specs=[pl.BlockSpec((ll, D_INNER), lambda c: (c, 0)), pl.BlockSpec((1, SSM_STATE, D_INNER), lambda c: (c, 0, 0))],
        out_shape=[jax.ShapeDtypeStruct((t, D_INNER), F32), jax.ShapeDtypeStruct((nc, SSM_STATE, D_INNER), F32)],
        scratch_shapes=[pltpu.VMEM((SSM_STATE, D_INNER), F32)],
        compiler_params=_cp(("arbitrary",)),
    )(xbc, dt, alog)


def _ssd_bwd(xbc, dt, alog, sprev, dy, dskip, name):
    t = xbc.shape[0]
    ll = SSD_L
    nc = t // ll

    def body(x_ref, dt_ref, al_ref, sp_ref, dy_ref, dk_ref, dx_ref, ddt_ref, dal_ref, ds_ref, colt_ref):
        @pl.when(pl.program_id(0) == 0)
        def _():
            ds_ref[...] = jnp.zeros_like(ds_ref)
            dal_ref[...] = jnp.zeros_like(dal_ref)

        dtv = dt_ref[...]
        a_neg, tril, acs, acs_t = _ssd_common(dtv, al_ref[...])
        lo = _lo_mask()
        hi = jnp.logical_not(lo)
        lane = lax.broadcasted_iota(jnp.int32, (1, LANES), 1)
        colt_ref[...] = jnp.zeros_like(colt_ref)
        rowterm = jnp.zeros((ll, LANES), F32)
        ddt_u = jnp.zeros((ll, LANES), F32)
        dlast = jnp.zeros((1, LANES), F32)

        def halves(v):
            return (jnp.sum(jnp.where(lo, v, 0.0), axis=-1, keepdims=True),
                    jnp.sum(jnp.where(hi, v, 0.0), axis=-1, keepdims=True))

        for g in range(SSM_GROUPS):
            cb0 = D_INNER + SSM_STATE * g
            cc0 = D_INNER + 512 + SSM_STATE * g
            bg = x_ref[:, cb0:cb0 + SSM_STATE].astype(BF16)
            cg = x_ref[:, cc0:cc0 + SSM_STATE].astype(BF16)
            gm = _dot(cg, bg, 1, 1)
            dgm = jnp.zeros((ll, ll), F32)
            dc_st = jnp.zeros((ll, SSM_STATE), F32)
            db_st = jnp.zeros((ll, SSM_STATE), F32)
            for pp in range(4):
                h0 = 8 * g + 2 * pp
                c0 = HEAD_DIM * h0
                x2 = x_ref[:, c0:c0 + LANES]
                cols, rows, dtx, eac, fdec, elast = _pair_terms(acs, acs_t, dtv, h0, lo)
                u32 = x2 * dtx
                u = u32.astype(BF16)
                dy2 = dy_ref[:, c0:c0 + LANES]
                dyb = dy2.astype(BF16)
                sp2 = sp_ref[0, :, c0:c0 + LANES]
                spb = sp2.astype(BF16)
                ds2 = ds_ref[:, c0:c0 + LANES]
                dsb = ds2.astype(BF16)
                xw = (x2 * (fdec * dtx)).astype(BF16)
                du_st = _dot(bg, dsb, 1, 0) * fdec
                yst = _dot(cg, spb, 1, 0) * eac
                dye = (dy2 * eac).astype(BF16)
                dc_st = dc_st + _dot(dye, spb, 1, 1)
                db_st = db_st + _dot(xw, dsb, 1, 1)
                ds_ref[:, c0:c0 + LANES] = ds2 * elast + _dot(cg, dye, 0, 0)
                dus, rsum, csum = [], [], []
                for e in range(2):
                    dec = _decay(cols[e], rows[e], tril)
                    wm = gm * dec
                    dye_m = jnp.where(lo if e == 0 else hi, dyb, jnp.zeros_like(dyb))
                    dum = _dot(dye_m, u, 1, 1)
                    dgm = dgm + dum * dec
                    tm_ = dum * wm
                    rsum.append(jnp.sum(tm_, axis=1, keepdims=True))
                    csum.append(jnp.sum(tm_, axis=0, keepdims=True))
                    dus.append(_dot(wm.astype(BF16), dyb, 0, 0))
                du = jnp.where(lo, dus[0], dus[1]) + du_st
                dx_ref[:, c0:c0 + LANES] = du * dtx + dk_ref[:, c0:c0 + LANES]
                ddtu = halves(du * x2)
                rst = halves(dy2 * yst)
                qst = halves(du_st * u32)
                sst = halves(jnp.sum(ds2 * sp2, axis=0, keepdims=True))
                for e in range(2):
                    oh = lane == (h0 + e)
                    rowterm = rowterm + jnp.where(oh, rsum[e] + rst[e] - qst[e], 0.0)
                    ddt_u = ddt_u + jnp.where(oh, ddtu[e], 0.0)
                    last_e = cols[e][ll - 1:ll, :]
                    dl = jnp.exp(last_e) * sst[e] + jnp.sum(qst[e], axis=0, keepdims=True)
                    dlast = dlast + jnp.where(oh, dl, 0.0)
                    colt_ref[h0 + e:h0 + e + 1, :] = csum[e]
            dgb = dgm.astype(BF16)
            dx_ref[:, cc0:cc0 + SSM_STATE] = _dot(dgb, bg, 1, 0) + dc_st
            dx_ref[:, cb0:cb0 + SSM_STATE] = _dot(dgb, cg, 0, 0) + db_st
        row_io = lax.broadcasted_iota(jnp.int32, (ll, LANES), 0)
        dacs = rowterm - colt_ref[...].T + jnp.where(row_io == ll - 1, dlast, 0.0)
        da = _dot(jnp.logical_not(tril).astype(F32) + jnp.where(
            lax.broadcasted_iota(jnp.int32, (ll, ll), 0) == lax.broadcasted_iota(jnp.int32, (ll, ll), 1), 1.0, 0.0),
            dacs, 1, 0, HI)
        ddt_ref[...] = da * a_neg + ddt_u
        dal_ref[...] += jnp.sum(da * dtv, axis=0, keepdims=True) * a_neg

    rev = lambda c: nc - 1 - c
    return pl.pallas_call(
        body, name=name, grid=(nc,),
        in_specs=[pl.BlockSpec((ll, XBC), lambda c: (rev(c), 0)), pl.BlockSpec((ll, LANES), lambda c: (rev(c), 0)),
                  pl.BlockSpec((1, LANES), lambda c: (0, 0)),
                  pl.BlockSpec((1, SSM_STATE, D_INNER), lambda c: (rev(c), 0, 0)),
                  pl.BlockSpec((ll, D_INNER), lambda c: (rev(c), 0)), pl.BlockSpec((ll, D_INNER), lambda c: (rev(c), 0))],
        out_specs=[pl.BlockSpec((ll, XBC), lambda c: (rev(c), 0)), pl.BlockSpec((ll, LANES), lambda c: (rev(c), 0)),
                   pl.BlockSpec((1, LANES), lambda c: (0, 0))],
        out_shape=[jax.ShapeDtypeStruct((t, XBC), F32), jax.ShapeDtypeStruct((t, LANES), F32),
                   jax.ShapeDtypeStruct((1, LANES), F32)],
        scratch_shapes=[pltpu.VMEM((SSM_STATE, D_INNER), F32), pltpu.VMEM((LANES, ll), F32)],
        compiler_params=_cp(("arbitrary",)),
    )(xbc, dt, alog, sprev, dy, dskip)


def _loss(y, target, name):
    t, d = y.shape
    tm = _pick(t, (512, 256, 128))

    def body(y_ref, t_ref, dy_ref, dyb_ref, acc_ref):
        @pl.when(pl.program_id(0) == 0)
        def _():
            acc_ref[...] = jnp.zeros_like(acc_ref)

        err = y_ref[...] - t_ref[...]
        dy = err * (1.0 / d)
        dy_ref[...] = dy
        dyb_ref[...] = dy.astype(BF16)
        acc_ref[...] += jnp.sum(err * err, axis=0, keepdims=True)

    row = pl.BlockSpec((tm, d), lambda i: (i, 0))
    vec = pl.BlockSpec((1, d), lambda i: (0, 0))
    return pl.pallas_call(
        body, name=name, grid=(t // tm,), in_specs=[row, row], out_specs=[row, row, vec],
        out_shape=[jax.ShapeDtypeStruct((t, d), F32), jax.ShapeDtypeStruct((t, d), BF16), jax.ShapeDtypeStruct((1, d), F32)],
        compiler_params=_cp(("arbitrary",)),
    )(y, target)


ADAM_TR = 512


def _adamw(parts, w, m, v, tr, name):
    r, c = w.shape
    c1 = 1.0 - ADAM_B1 ** ADAM_STEP
    c2 = 1.0 - ADAM_B2 ** ADAM_STEP

    def body(p_ref, w_ref, m_ref, v_ref, g_ref, d_ref, mo_ref, vo_ref):
        g = p_ref[0].astype(F32)
        for k in range(1, N_DEV):
            g = g + p_ref[k].astype(F32)
        mn = ADAM_B1 * m_ref[...] + (1.0 - ADAM_B1) * g
        vn = ADAM_B2 * v_ref[...] + (1.0 - ADAM_B2) * (g * g)
        g_ref[...] = g
        mo_ref[...] = mn
        vo_ref[...] = vn
        d_ref[...] = -ADAM_LR * ((mn / c1) / (jnp.sqrt(vn / c2) + ADAM_EPS) + ADAM_WD * w_ref[...])

    row = pl.BlockSpec((tr, c), lambda i: (i, 0))
    sd = jax.ShapeDtypeStruct((r, c), F32)
    return pl.pallas_call(
        body, name=name, grid=(r // tr,),
        in_specs=[pl.BlockSpec((N_DEV, tr, c), lambda i: (0, i, 0)), row, row, row],
        out_specs=[row, row, row, row], out_shape=[sd, sd, sd, sd], compiler_params=_cp(("parallel",)),
    )(parts, w, m, v)


def _peers():
    mx, my, mc = lax.axis_index("x"), lax.axis_index("y"), lax.axis_index("c")
    me = 4 * mx + 2 * my + mc
    out = []
    for k in range(1, N_DEV):
        px = 1 - mx if k & 4 else mx
        py = 1 - my if k & 2 else my
        pc = 1 - mc if k & 1 else mc
        out.append(((px, py, pc), 4 * px + 2 * py + pc))
    return me, out


def _exchange(arrs, scatters, name):
    n = len(arrs)
    np_ = N_DEV - 1

    def body(*refs):
        x_refs, o_refs = refs[:n], refs[n:2 * n]
        send_sems, recv_sems, local_sems = refs[2 * n:]
        me, peers = _peers()
        started = []
        for a in range(n):
            mine = x_refs[a].at[me] if scatters[a] else x_refs[a]
            cp = pltpu.make_async_copy(mine, o_refs[a].at[me], local_sems.at[a])
            cp.start()
            started.append(cp)
        sends = []
        for k, (dev, idx) in enumerate(peers):
            for a in range(n):
                cp = pltpu.make_async_remote_copy(
                    src_ref=x_refs[a].at[idx] if scatters[a] else x_refs[a], dst_ref=o_refs[a].at[me],
                    send_sem=send_sems.at[a * np_ + k], recv_sem=recv_sems.at[a * np_ + k], device_id=dev, device_id_type=MESH)
                cp.start()
                sends.append(cp)
        for k, (dev, idx) in enumerate(peers):
            for a in range(n):
                mine = x_refs[a].at[me] if scatters[a] else x_refs[a]
                pltpu.make_async_remote_copy(
                    src_ref=mine, dst_ref=o_refs[a].at[idx], send_sem=send_sems.at[a * np_ + k],
                    recv_sem=recv_sems.at[a * np_ + k], device_id=dev, device_id_type=MESH).wait_recv()
        for cp in sends:
            cp.wait_send()
        for cp in started:
            cp.wait()

    out_shape = [jax.ShapeDtypeStruct(x.shape if sc else (N_DEV,) + x.shape, x.dtype) for x, sc in zip(arrs, scatters)]
    anyspace = pl.BlockSpec(memory_space=pl.ANY)
    return pl.pallas_call(
        body, name=name, out_shape=out_shape, in_specs=[anyspace] * n, out_specs=[anyspace] * n,
        scratch_shapes=[pltpu.SemaphoreType.DMA((np_ * n,)), pltpu.SemaphoreType.DMA((np_ * n,)),
                        pltpu.SemaphoreType.DMA((n,))],
    )(*arrs)


def _pack(arrs, dtype, lead=()):
    nl = len(lead)
    flat = jnp.concatenate([a.astype(dtype).reshape(lead + (-1,)) for a in arrs], axis=nl)
    n = flat.shape[-1]
    rows = -(-n // (LANES * ADAM_TR)) * ADAM_TR
    flat = jnp.pad(flat, [(0, 0)] * nl + [(0, rows * LANES - n)])
    return flat.reshape(lead + (rows, LANES))


def _unpack(flat, shapes, lead=()):
    nl = len(lead)
    flat = flat.reshape(lead + (-1,))
    out, o = [], 0
    for s in shapes:
        n = 1
        for d in s:
            n *= d
        out.append(lax.slice_in_dim(flat, o, o + n, axis=nl).reshape(lead + tuple(s)))
        o += n
    return out


def _join(g, ax):
    return jnp.concatenate([g[d] for d in range(N_DEV)], axis=ax)


def _split(full, ax):
    n = full.shape[ax] // N_DEV
    return jnp.stack([lax.slice_in_dim(full, d * n, (d + 1) * n, axis=ax) for d in range(N_DEV)])


def _rows2d(a, lead=0):
    return a.reshape(a.shape[:lead] + (-1, a.shape[-1]))


_WEIGHTS = ['norm_mix', 'norm_ffn', 'attn_w_in', 'attn_w_out', 'relpos_table', 'q_norm_a', 'k_norm_a', 'q_norm_b',
            'k_norm_b', 'sinks', 'ssm_w_in', 'ssm_conv_w', 'ssm_conv_b', 'ssm_dt_bias', 'ssm_a_log', 'ssm_d', 'ssm_norm',
            'ssm_w_out', 'ffn_w_in', 'ffn_conv_w', 'ffn_conv_b', 'ffn_w_out']
_SHARD_AX = {'attn_w_in': 2, 'attn_w_out': 1, 'ssm_w_in': 2, 'ssm_conv_w': 2, 'ssm_conv_b': 1, 'ssm_norm': 1,
             'ssm_w_out': 1, 'ffn_w_in': 2, 'ffn_conv_w': 2, 'ffn_w_out': 1}
_BIG = ['attn_w_in', 'attn_w_out', 'ssm_w_in', 'ssm_w_out', 'ffn_w_in', 'ffn_w_out']
_SMALL = ['ssm_conv_w', 'ssm_conv_b', 'ssm_norm', 'ffn_conv_w']
_SHARDED = _BIG + _SMALL
_REPL = [n for n in _WEIGHTS if n not in _SHARD_AX]


def _rows8(w):
    return jnp.pad(w, ((0, 8 - w.shape[0]), (0, 0)))


def _lanes128(v):
    return jnp.pad(v, (0, LANES - v.shape[0])).reshape(1, LANES)


def _band_mask(n_prev, pad):
    cq = jnp.arange(TQ)[:, None] // CHUNK
    ck = jnp.arange(pad + TQ)[None, :] // CHUNK
    return (ck >= cq) & (ck <= cq + n_prev)


def _pad_rows(a, pad):
    return jnp.pad(a, ((pad, 0), (0, 0)))


def _kv_expand(a):
    return jnp.concatenate([a[:, :HEAD_DIM]] * 4 + [a[:, HEAD_DIM:]] * 4, axis=1)


def _ffn_fwd(xin, g, w_in, w8, cb, w_out, tag):
    h = _rms_fwd(xin, g, f"rms_ffn{tag}")
    gu = _mm(h, w_in, f"mm_ffn_in{tag}")
    a = _ffn_mid_fwd(gu, w8, cb, f"ffn_mid{tag}")
    xout = _mm(a, w_out, f"mm_ffn_out{tag}", res=xin)
    return xout, (h, gu, a)


def _ffn_bwd(dx, dxb, xin, g, w_in_t, w8, cb, w_out_t, saved, tag):
    h, gu, a = saved
    da = _mm(dxb, w_out_t, f"mm_ffn_da{tag}")
    dw_out = _mm(a, dxb, f"mm_ffn_dwout{tag}", trans_a=True)
    dgc, dup, dw8, dcb = _ffn_mid_bwd1(gu, da, w8, cb, f"ffn_mid_bwd{tag}")
    dgate = _conv_bwd_data(dgc, w8, 3, f"ffn_conv_bwd{tag}")
    dh = _mm(dgate, w_in_t[:D_FF], f"mm_ffn_dh_g{tag}")
    dh = _mm(dup, w_in_t[D_FF:], f"mm_ffn_dh_u{tag}", res=dh)
    dw_in = jnp.concatenate([_mm(h, dgate, f"mm_ffn_dwin_g{tag}", trans_a=True),
                             _mm(h, dup, f"mm_ffn_dwin_u{tag}", trans_a=True)], axis=1)
    dxp, dxpb, dg = _rms_bwd(xin, g, dh, dx, f"rms_ffn_bwd{tag}")
    return dxp, dxpb, dg, dw_in, dw8[:3], dcb, dw_out


def kernel(x, norm_mix, norm_ffn, attn_w_in, attn_w_out, relpos_table, q_norm_a, k_norm_a, q_norm_b, k_norm_b, sinks, ssm_w_in, ssm_conv_w, ssm_conv_b, ssm_dt_bias, ssm_a_log, ssm_d, ssm_norm, ssm_w_out, ffn_w_in, ffn_conv_w, ffn_conv_b, ffn_w_out, loss_target, m_norm_mix, m_norm_ffn, m_attn_w_in, m_attn_w_out, m_relpos_table, m_q_norm_a, m_k_norm_a, m_q_norm_b, m_k_norm_b, m_sinks, m_ssm_w_in, m_ssm_conv_w, m_ssm_conv_b, m_ssm_dt_bias, m_ssm_a_log, m_ssm_d, m_ssm_norm, m_ssm_w_out, m_ffn_w_in, m_ffn_conv_w, m_ffn_conv_b, m_ffn_w_out, v_norm_mix, v_norm_ffn, v_attn_w_in, v_attn_w_out, v_relpos_table, v_q_norm_a, v_k_norm_a, v_q_norm_b, v_k_norm_b, v_sinks, v_ssm_w_in, v_ssm_conv_w, v_ssm_conv_b, v_ssm_dt_bias, v_ssm_a_log, v_ssm_d, v_ssm_norm, v_ssm_w_out, v_ffn_w_in, v_ffn_conv_w, v_ffn_conv_b, v_ffn_w_out):
    w = dict(norm_mix=norm_mix, norm_ffn=norm_ffn, attn_w_in=attn_w_in, attn_w_out=attn_w_out, relpos_table=relpos_table,
             q_norm_a=q_norm_a, k_norm_a=k_norm_a, q_norm_b=q_norm_b, k_norm_b=k_norm_b, sinks=sinks, ssm_w_in=ssm_w_in,
             ssm_conv_w=ssm_conv_w, ssm_conv_b=ssm_conv_b, ssm_dt_bias=ssm_dt_bias, ssm_a_log=ssm_a_log, ssm_d=ssm_d,
             ssm_norm=ssm_norm, ssm_w_out=ssm_w_out, ffn_w_in=ffn_w_in, ffn_conv_w=ffn_conv_w, ffn_conv_b=ffn_conv_b,
             ffn_w_out=ffn_w_out)
    mom = dict(norm_mix=m_norm_mix, norm_ffn=m_norm_ffn, attn_w_in=m_attn_w_in, attn_w_out=m_attn_w_out,
               relpos_table=m_relpos_table, q_norm_a=m_q_norm_a, k_norm_a=m_k_norm_a, q_norm_b=m_q_norm_b,
               k_norm_b=m_k_norm_b, sinks=m_sinks, ssm_w_in=m_ssm_w_in, ssm_conv_w=m_ssm_conv_w, ssm_conv_b=m_ssm_conv_b,
               ssm_dt_bias=m_ssm_dt_bias, ssm_a_log=m_ssm_a_log, ssm_d=m_ssm_d, ssm_norm=m_ssm_norm, ssm_w_out=m_ssm_w_out,
               ffn_w_in=m_ffn_w_in, ffn_conv_w=m_ffn_conv_w, ffn_conv_b=m_ffn_conv_b, ffn_w_out=m_ffn_w_out)
    var = dict(norm_mix=v_norm_mix, norm_ffn=v_norm_ffn, attn_w_in=v_attn_w_in, attn_w_out=v_attn_w_out,
               relpos_table=v_relpos_table, q_norm_a=v_q_norm_a, k_norm_a=v_k_norm_a, q_norm_b=v_q_norm_b,
               k_norm_b=v_k_norm_b, sinks=v_sinks, ssm_w_in=v_ssm_w_in, ssm_conv_w=v_ssm_conv_w, ssm_conv_b=v_ssm_conv_b,
               ssm_dt_bias=v_ssm_dt_bias, ssm_a_log=v_ssm_a_log, ssm_d=v_ssm_d, ssm_norm=v_ssm_norm, ssm_w_out=v_ssm_w_out,
               ffn_w_in=v_ffn_w_in, ffn_conv_w=v_ffn_conv_w, ffn_conv_b=v_ffn_conv_b, ffn_w_out=v_ffn_w_out)

    gathered = _exchange([w[n].astype(BF16) for n in _BIG] + [_pack([w[n] for n in _SMALL], F32)],
                         [False] * (len(_BIG) + 1), "gather_weights")
    full = {n: _join(g, _SHARD_AX[n]) for n, g in zip(_BIG, gathered)}
    for n, g in zip(_SMALL, _unpack(gathered[-1], [w[n].shape for n in _SMALL], lead=(N_DEV,))):
        full[n] = _join(g, _SHARD_AX[n])
    w_attn_in = full['attn_w_in'][0]
    w_attn_out = full['attn_w_out'][0]
    w_ssm_main = full['ssm_w_in'][0][:, :ZX]
    w_ssm_dt = jnp.pad(full['ssm_w_in'][0][:, ZX:], ((0, 0), (0, LANES - SSM_HEADS)))
    w_ssm_out = full['ssm_w_out'][0]
    w_ffn_in = full['ffn_w_in']
    w_ffn_out = full['ffn_w_out']
    ssm_cw8 = _rows8(full['ssm_conv_w'][0])
    ssm_cb = full['ssm_conv_b']
    ssm_nw = full['ssm_norm']
    ffn_cw8 = [_rows8(full['ffn_conv_w'][l]) for l in range(2)]
    ffn_cb = [ffn_conv_b[l:l + 1] for l in range(2)]

    x0 = x[0]
    target = loss_target[0]
    t = x0.shape[0]

    g_mix0, g_mix1 = norm_mix[0:1], norm_mix[1:2]
    g_ffn0, g_ffn1 = norm_ffn[0:1], norm_ffn[1:2]
    h0 = _rms_fwd(x0, g_mix0, "rms_mix0")
    proj = _mm(h0, w_attn_in, "mm_attn_in")
    hn_w = jnp.concatenate([jnp.tile(v, (1, 2)) for v in (q_norm_a, k_norm_a, q_norm_b, k_norm_b)], axis=0)
    qa, ka, va, qb, kb, vb = _headnorm_fwd(proj, hn_w, "headnorm")
    table = jnp.pad(relpos_table[0], ((0, 0), (0, REL_W - (2 * MAX_REL + 1))))
    bias_a = jnp.where(_band_mask(A_PREV, PAD_A)[None], jnp.transpose(_relpos_fwd(table, "relpos_bias"), (1, 0, 2)), NEG)
    rel_b = jnp.arange(TQ)[:, None] - (jnp.arange(PAD_B + TQ)[None, :] - PAD_B)
    slopes = 2.0 ** (-8.0 * jnp.arange(1, N_HEADS + 1, dtype=F32) / N_HEADS)
    bias_b = jnp.where(_band_mask(B_PREV, PAD_B)[None], -slopes[:, None, None] * jnp.abs(rel_b).astype(F32)[None], NEG)
    no_sinks = jnp.full((N_HEADS,), NEG, F32)
    kpa, vpa = _pad_rows(ka, PAD_A), _pad_rows(va, PAD_A)
    kpb, vpb = _pad_rows(_kv_expand(kb), PAD_B), _pad_rows(_kv_expand(vb), PAD_B)
    oa = _attn_fwd(qa, kpa, vpa, bias_a, no_sinks, PAD_A, "attn_a")
    ob = _attn_fwd(qb, kpb, vpb, bias_b, sinks[0], PAD_B, "attn_b")
    x1 = _mm(oa, w_attn_out[:512], "mm_attn_out_a", res=x0)
    x1 = _mm(ob, w_attn_out[512:], "mm_attn_out_b", res=x1)
    x2, ffn0_saved = _ffn_fwd(x1, g_ffn0, w_ffn_in[0], ffn_cw8[0], ffn_cb[0], w_ffn_out[0], "0")

    h2 = _rms_fwd(x2, g_mix1, "rms_mix1")
    zx = _mm(h2, w_ssm_main, "mm_ssm_in")
    dtraw = _mm(h2, w_ssm_dt, "mm_ssm_dt")
    dt_bias = _lanes128(ssm_dt_bias[0])
    alog = _lanes128(ssm_a_log[0])
    dexp = jnp.repeat(ssm_d[0], HEAD_DIM).reshape(1, D_INNER)
    xbc = _ssm_pre_fwd(zx, ssm_cw8, ssm_cb, "ssm_pre")
    dt = _dt_fwd(dtraw, dt_bias, "ssm_dt")
    y, sprev = _ssd_fwd(xbc, dt, alog, "ssd_fwd")
    y4 = _ssm_post_fwd(y, xbc, zx, dexp, ssm_nw, "ssm_post")
    x3 = _mm(y4, w_ssm_out, "mm_ssm_out", res=x2)
    x4, ffn1_saved = _ffn_fwd(x3, g_ffn1, w_ffn_in[1], ffn_cw8[1], ffn_cb[1], w_ffn_out[1], "1")

    dx4, dx4b, sq = _loss(x4, target, "loss")
    loss = lax.psum(0.5 * jnp.sum(sq) / D_MODEL, ("x", "y", "c"))

    grads = {}
    w_ffn_in_t = jnp.transpose(w_ffn_in, (0, 2, 1))
    w_ffn_out_t = jnp.transpose(w_ffn_out, (0, 2, 1))
    dx3, dx3b, dg_ffn1, dwin1, dcw1, dcb1, dwout1 = _ffn_bwd(
        dx4, dx4b, x3, g_ffn1, w_ffn_in_t[1], ffn_cw8[1], ffn_cb[1], w_ffn_out_t[1], ffn1_saved, "1")

    dy4 = _mm(dx3b, w_ssm_out.T, "mm_ssm_dy")
    grads['ssm_w_out'] = _mm(y4, dx3b, "mm_ssm_dwout", trans_a=True)[None]
    dyv, dskip, dz, dd_lane, dnw = _ssm_post_bwd(dy4, y, xbc, zx, dexp, ssm_nw, "ssm_post_bwd")
    dxbc, ddt, dalog = _ssd_bwd(xbc, dt, alog, sprev, dyv, dskip, "ssd_bwd")
    dc, dcw_s, dcb_s = _ssm_pre_bwd1(zx, dxbc, ssm_cw8, ssm_cb, "ssm_pre_bwd")
    ddtraw, ddtb = _dt_bwd(dtraw, dt_bias, ddt, "ssm_dt_bwd")
    dxr = _conv_bwd_data(dc, ssm_cw8, 4, "ssm_conv_bwd")
    w_main_t = w_ssm_main.T
    dh2 = _mm(dz, w_main_t[:D_INNER], "mm_ssm_dh_z")
    dh2 = _mm(dxr, w_main_t[D_INNER:], "mm_ssm_dh_x", res=dh2)
    dh2 = _mm(ddtraw, w_ssm_dt.T, "mm_ssm_dh_dt", res=dh2)
    grads['ssm_w_in'] = jnp.concatenate([
        _mm(h2, dz, "mm_ssm_dwin_z", trans_a=True), _mm(h2, dxr, "mm_ssm_dwin_x", trans_a=True),
        _mm(h2, ddtraw, "mm_ssm_dwin_dt", trans_a=True)[:, :SSM_HEADS]], axis=1)[None]
    dx2, dx2b, dg_mix1 = _rms_bwd(x2, g_mix1, dh2, dx3, "rms_mix1_bwd")
    grads['ssm_conv_w'] = dcw_s[:4][None]
    grads['ssm_conv_b'] = dcb_s
    grads['ssm_norm'] = dnw
    grads['ssm_dt_bias'] = ddtb[:, :SSM_HEADS]
    grads['ssm_a_log'] = dalog[:, :SSM_HEADS]
    grads['ssm_d'] = jnp.sum(dd_lane.reshape(SSM_HEADS, HEAD_DIM), axis=1)[None]

    dx1, dx1b, dg_ffn0, dwin0, dcw0, dcb0, dwout0 = _ffn_bwd(
        dx2, dx2b, x1, g_ffn0, w_ffn_in_t[0], ffn_cw8[0], ffn_cb[0], w_ffn_out_t[0], ffn0_saved, "0")
    grads['ffn_w_in'] = jnp.stack([dwin0, dwin1])
    grads['ffn_conv_w'] = jnp.stack([dcw0, dcw1])
    grads['ffn_conv_b'] = jnp.concatenate([dcb0, dcb1], axis=0)
    grads['ffn_w_out'] = jnp.stack([dwout0, dwout1])
    grads['norm_ffn'] = jnp.concatenate([dg_ffn0, dg_ffn1], axis=0)

    do = _mm(dx1b, w_attn_out.T, "mm_attn_do", out_dtype=BF16)
    grads['attn_w_out'] = jnp.concatenate([_mm(oa, dx1b, "mm_attn_dwout_a", trans_a=True),
                                           _mm(ob, dx1b, "mm_attn_dwout_b", trans_a=True)], axis=0)[None]
    dqa, dkpa, dvpa, dbias_a, _ = _attn_bwd(qa, kpa, vpa, bias_a, no_sinks, do, 0, PAD_A, "attn_a_bwd")
    dqb, dkpb, dvpb, _, dsink = _attn_bwd(qb, kpb, vpb, bias_b, sinks[0], do, 4, PAD_B, "attn_b_bwd")
    grads['relpos_table'] = _relpos_bwd(jnp.transpose(dbias_a, (1, 0, 2)), "relpos_bwd")[None, :, :2 * MAX_REL + 1]
    grads['sinks'] = dsink[:, :2, 0].reshape(1, N_HEADS)
    dproj, dhn = _headnorm_bwd(proj, hn_w, dqa, dkpa, dvpa, dqb, dkpb, dvpb, "headnorm_bwd")
    dhn = dhn[:, :HEAD_DIM] + dhn[:, HEAD_DIM:]
    for k, n in enumerate(('q_norm_a', 'k_norm_a', 'q_norm_b', 'k_norm_b')):
        grads[n] = dhn[k:k + 1]
    dh0 = _mm(dproj, w_attn_in.T, "mm_attn_dh")
    grads['attn_w_in'] = _mm(h0, dproj, "mm_attn_dwin", trans_a=True)[None]
    dx0, _, dg_mix0 = _rms_bwd(x0, g_mix0, dh0, dx1, "rms_mix0_bwd")
    grads['norm_mix'] = jnp.concatenate([dg_mix0, dg_mix1], axis=0)

    sm_shapes = [w[n].shape for n in _SMALL]
    rp_shapes = [w[n].shape for n in _REPL]
    recv = _exchange(
        [_split(grads[n], _SHARD_AX[n]).astype(BF16) for n in _BIG]
        + [_pack([_split(grads[n], _SHARD_AX[n]) for n in _SMALL], F32, lead=(N_DEV,)), _pack([grads[n] for n in _REPL], F32)],
        [True] * (len(_BIG) + 1) + [False], "exchange_grads")
    res = [{}, {}, {}, {}]
    for n, parts in zip(_BIG, recv):
        rows = _rows2d(w[n]).shape[0]
        outs = _adamw(_rows2d(parts, 1), _rows2d(w[n]), _rows2d(mom[n]), _rows2d(var[n]), _pick(rows, (256, 128, 64)), f"adamw_{n}")
        for kind, a in enumerate(outs):
            res[kind][n] = a.reshape(w[n].shape)
    for names, shapes, parts in ((_SMALL, sm_shapes, recv[-2]), (_REPL, rp_shapes, recv[-1])):
        outs = _adamw(parts, _pack([w[n] for n in names], F32), _pack([mom[n] for n in names], F32),
                      _pack([var[n] for n in names], F32), ADAM_TR, "adamw_" + ("small" if names is _SMALL else "replicated"))
        for kind, flat in enumerate(outs):
            for n, a in zip(names, _unpack(flat, shapes)):
                res[kind][n] = a
    return (loss, dx0[None], *[res[0][n] for n in _WEIGHTS], *[res[1][n] for n in _WEIGHTS],
            *[res[2][n] for n in _WEIGHTS], *[res[3][n] for n in _WEIGHTS])
```

```python
import jax
import jax.numpy as jnp
from jax import lax
from jax.experimental import pallas as pl
from jax.experimental.pallas import tpu as pltpu

F32 = jnp.float32
BF16 = jnp.bfloat16
HI = lax.Precision.HIGHEST
MESH = pl.DeviceIdType.MESH
NEG = -1e30

N_DEV = 8
D_MODEL = 1024
EPS = 1e-6
CHUNK = 64
HEAD_DIM = 64
N_HEADS = 8
A_PREV = 8
B_PREV = 2
MAX_REL = 256
TQ = 2 * CHUNK
PAD_A = A_PREV * CHUNK
PAD_B = B_PREV * CHUNK
REL_W = PAD_A + TQ
D_INNER = 2048
SSM_HEADS = 32
SSM_GROUPS = 4
SSM_STATE = 128
XBC = D_INNER + 2 * SSM_GROUPS * SSM_STATE
ZX = D_INNER + XBC
D_FF = 2816
SSD_L = 128
LANES = 128
VMEM_LIMIT = 48 << 20

ADAM_LR, ADAM_B1, ADAM_B2, ADAM_EPS, ADAM_WD, ADAM_STEP = 0.001, 0.9, 0.999, 1e-08, 0.01, 10


def _cp(sem=None):
    return pltpu.CompilerParams(dimension_semantics=sem, vmem_limit_bytes=VMEM_LIMIT)


def _dot(a, b, ca=1, cb=0, prec=None):
    return lax.dot_general(a, b, (((ca,), (cb,)), ((), ())), preferred_element_type=F32, precision=prec)


def _pick(n, cands):
    for c in cands:
        if n % c == 0:
            return c
    return n


def _lo_mask():
    return lax.broadcasted_iota(jnp.int32, (1, LANES), 1) < HEAD_DIM


def _mm(a, b, name, out_dtype=F32, res=None, trans_a=False):
    if trans_a:
        kdim, m = a.shape
    else:
        m, kdim = a.shape
    n = b.shape[1]
    assert b.shape[0] == kdim, (a.shape, b.shape)
    tm = _pick(m, (512, 1408, 256, 128))
    tn = _pick(n, (1024, 1152, 1408, 1280, 768, 512, 256, 128))
    if trans_a:
        tk = _pick(kdim, (512, 256, 128))
    else:
        tk = kdim if kdim <= 2048 else _pick(kdim, (1408, 1536, 1152, 1024, 512))
    nk = kdim // tk

    def body(*refs):
        if res is None:
            a_ref, b_ref, o_ref, acc = refs
        else:
            a_ref, b_ref, r_ref, o_ref, acc = refs
        k = pl.program_id(2)

        @pl.when(k == 0)
        def _():
            acc[...] = jnp.zeros_like(acc)

        acc[...] += _dot(a_ref[...], b_ref[...], 0 if trans_a else 1, 0)

        @pl.when(k == nk - 1)
        def _():
            r = acc[...]
            if res is not None:
                r = r + r_ref[...]
            o_ref[...] = r.astype(out_dtype)

    a_spec = pl.BlockSpec((tk, tm), lambda i, j, k: (k, i)) if trans_a else pl.BlockSpec((tm, tk), lambda i, j, k: (i, k))
    in_specs = [a_spec, pl.BlockSpec((tk, tn), lambda i, j, k: (k, j))]
    args = [a, b]
    if res is not None:
        in_specs.append(pl.BlockSpec((tm, tn), lambda i, j, k: (i, j)))
        args.append(res)
    return pl.pallas_call(
        body, name=name, grid=(m // tm, n // tn, nk), in_specs=in_specs,
        out_specs=pl.BlockSpec((tm, tn), lambda i, j, k: (i, j)),
        out_shape=jax.ShapeDtypeStruct((m, n), out_dtype),
        scratch_shapes=[pltpu.VMEM((tm, tn), F32)],
        compiler_params=_cp(("parallel", "parallel", "arbitrary")),
    )(*args)


def _rms_fwd(x, g, name):
    t, d = x.shape
    tm = _pick(t, (512, 256, 128))

    def body(x_ref, g_ref, h_ref):
        xv = x_ref[...]
        r = lax.rsqrt(jnp.mean(xv * xv, axis=-1, keepdims=True) + EPS)
        h_ref[...] = (xv * r * g_ref[...]).astype(BF16)

    return pl.pallas_call(
        body, name=name, grid=(t // tm,),
        in_specs=[pl.BlockSpec((tm, d), lambda i: (i, 0)), pl.BlockSpec((1, d), lambda i: (0, 0))],
        out_specs=pl.BlockSpec((tm, d), lambda i: (i, 0)),
        out_shape=jax.ShapeDtypeStruct((t, d), BF16), compiler_params=_cp(("parallel",)),
    )(x, g)


def _rms_bwd(x, g, dh, dres, name):
    t, d = x.shape
    tm = _pick(t, (512, 256, 128))

    def body(x_ref, g_ref, dh_ref, dr_ref, dx_ref, dxb_ref, dg_ref):
        i = pl.program_id(0)
        xv = x_ref[...]
        r = lax.rsqrt(jnp.mean(xv * xv, axis=-1, keepdims=True) + EPS)
        xh = xv * r
        dhv = dh_ref[...]
        dxh = dhv * g_ref[...]
        dx = dr_ref[...] + r * (dxh - xh * jnp.mean(dxh * xh, axis=-1, keepdims=True))
        dx_ref[...] = dx
        dxb_ref[...] = dx.astype(BF16)

        @pl.when(i == 0)
        def _():
            dg_ref[...] = jnp.zeros_like(dg_ref)

        dg_ref[...] += jnp.sum(dhv * xh, axis=0, keepdims=True)

    row = pl.BlockSpec((tm, d), lambda i: (i, 0))
    vec = pl.BlockSpec((1, d), lambda i: (0, 0))
    return pl.pallas_call(
        body, name=name, grid=(t // tm,), in_specs=[row, vec, row, row], out_specs=[row, row, vec],
        out_shape=[jax.ShapeDtypeStruct((t, d), F32), jax.ShapeDtypeStruct((t, d), BF16), jax.ShapeDtypeStruct((1, d), F32)],
        compiler_params=_cp(("arbitrary",)),
    )(x, g, dh, dres)


def _head_rms(xs, w, lo):
    sq = xs * xs
    s0 = jnp.sum(jnp.where(lo, sq, 0.0), axis=-1, keepdims=True)
    s1 = jnp.sum(jnp.where(lo, 0.0, sq), axis=-1, keepdims=True)
    r = jnp.where(lo, lax.rsqrt(s0 * (1.0 / HEAD_DIM) + EPS), lax.rsqrt(s1 * (1.0 / HEAD_DIM) + EPS))
    return xs * r, r


def _head_rms_bwd(xs, w, dy, lo):
    xh, r = _head_rms(xs, w, lo)
    dxh = dy * w
    t = dxh * xh
    m0 = jnp.sum(jnp.where(lo, t, 0.0), axis=-1, keepdims=True)
    m1 = jnp.sum(jnp.where(lo, 0.0, t), axis=-1, keepdims=True)
    mm = jnp.where(lo, m0, m1) * (1.0 / HEAD_DIM)
    return r * (dxh - xh * mm), dy * xh


_QSCALE = HEAD_DIM ** -0.5


def _headnorm_fwd(proj, ws, name):
    t = proj.shape[0]
    tm = _pick(t, (256, 128))

    def body(p_ref, w_ref, qa_ref, ka_ref, va_ref, qb_ref, kb_ref, vb_ref):
        lo = _lo_mask()
        for s in range(4):
            c = LANES * s
            xh, _ = _head_rms(p_ref[:, c:c + LANES], None, lo)
            qa_ref[:, c:c + LANES] = (xh * w_ref[0:1, :] * _QSCALE).astype(BF16)
            xh, _ = _head_rms(p_ref[:, 512 + c:512 + c + LANES], None, lo)
            ka_ref[:, c:c + LANES] = (xh * w_ref[1:2, :]).astype(BF16)
            xh, _ = _head_rms(p_ref[:, 1536 + c:1536 + c + LANES], None, lo)
            qb_ref[:, c:c + LANES] = (xh * w_ref[2:3, :] * _QSCALE).astype(BF16)
        va_ref[...] = p_ref[:, 1024:1536].astype(BF16)
        xh, _ = _head_rms(p_ref[:, 2048:2176], None, lo)
        kb_ref[...] = (xh * w_ref[3:4, :]).astype(BF16)
        vb_ref[...] = p_ref[:, 2176:2304].astype(BF16)

    wide = pl.BlockSpec((tm, 512), lambda i: (i, 0))
    narrow = pl.BlockSpec((tm, LANES), lambda i: (i, 0))
    sd = lambda n: jax.ShapeDtypeStruct((t, n), BF16)
    return pl.pallas_call(
        body, name=name, grid=(t // tm,),
        in_specs=[pl.BlockSpec((tm, 2304), lambda i: (i, 0)), pl.BlockSpec((4, LANES), lambda i: (0, 0))],
        out_specs=[wide, wide, wide, wide, narrow, narrow],
        out_shape=[sd(512), sd(512), sd(512), sd(512), sd(LANES), sd(LANES)],
        compiler_params=_cp(("parallel",)),
    )(proj, ws)


def _headnorm_bwd(proj, ws, dqa, dkpa, dvpa, dqb, dkpb, dvpb, name):
    t = proj.shape[0]
    tm = TQ
    offa, offb = PAD_A // tm, PAD_B // tm

    def body(p_ref, w_ref, dqa_ref, dka_ref, dva_ref, dqb_ref, dkb_ref, dvb_ref, dp_ref, dw_ref):
        i = pl.program_id(0)
        lo = _lo_mask()

        @pl.when(i == 0)
        def _():
            dw_ref[...] = jnp.zeros_like(dw_ref)

        acc = [jnp.zeros((1, LANES), F32) for _ in range(4)]
        for s in range(4):
            c = LANES * s
            dx, dwl = _head_rms_bwd(p_ref[:, c:c + LANES], w_ref[0:1, :], dqa_ref[:, c:c + LANES] * _QSCALE, lo)
            dp_ref[:, c:c + LANES] = dx.astype(BF16)
            acc[0] += jnp.sum(dwl, axis=0, keepdims=True)
            dx, dwl = _head_rms_bwd(p_ref[:, 512 + c:512 + c + LANES], w_ref[1:2, :], dka_ref[:, c:c + LANES], lo)
            dp_ref[:, 512 + c:512 + c + LANES] = dx.astype(BF16)
            acc[1] += jnp.sum(dwl, axis=0, keepdims=True)
            dx, dwl = _head_rms_bwd(p_ref[:, 1536 + c:1536 + c + LANES], w_ref[2:3, :], dqb_ref[:, c:c + LANES] * _QSCALE, lo)
            dp_ref[:, 1536 + c:1536 + c + LANES] = dx.astype(BF16)
            acc[2] += jnp.sum(dwl, axis=0, keepdims=True)
        dp_ref[:, 1024:1536] = dva_ref[...].astype(BF16)

        def group_sum(ref):
            s0 = ref[:, 0:128] + ref[:, 128:256]
            s1 = ref[:, 256:384] + ref[:, 384:512]
            s0 = s0 + pltpu.roll(s0, HEAD_DIM, 1)
            s1 = s1 + pltpu.roll(s1, HEAD_DIM, 1)
            return jnp.where(lo, s0, s1)

        dx, dwl = _head_rms_bwd(p_ref[:, 2048:2176], w_ref[3:4, :], group_sum(dkb_ref), lo)
        dp_ref[:, 2048:2176] = dx.astype(BF16)
        acc[3] += jnp.sum(dwl, axis=0, keepdims=True)
        dp_ref[:, 2176:2304] = group_sum(dvb_ref).astype(BF16)
        for n in range(4):
            dw_ref[n:n + 1, :] += acc[n]

    wide = pl.BlockSpec((tm, 512), lambda i: (i, 0))
    pa = pl.BlockSpec((tm, 512), lambda i: (i + offa, 0))
    pb = pl.BlockSpec((tm, 512), lambda i: (i + offb, 0))
    return pl.pallas_call(
        body, name=name, grid=(t // tm,),
        in_specs=[pl.BlockSpec((tm, 2304), lambda i: (i, 0)), pl.BlockSpec((4, LANES), lambda i: (0, 0)),
                  wide, pa, pa, wide, pb, pb],
        out_specs=[pl.BlockSpec((tm, 2304), lambda i: (i, 0)), pl.BlockSpec((4, LANES), lambda i: (0, 0))],
        out_shape=[jax.ShapeDtypeStruct((t, 2304), BF16), jax.ShapeDtypeStruct((4, LANES), F32)],
        compiler_params=_cp(("arbitrary",)),
    )(proj, ws, dqa, dkpa, dvpa, dqb, dkpb, dvpb)


ROLL_W = 1024


def _rel_onehot():
    r_io = lax.broadcasted_iota(jnp.int32, (REL_W, ROLL_W), 0)
    m_io = lax.broadcasted_iota(jnp.int32, (REL_W, ROLL_W), 1)
    return (r_io == jnp.clip(REL_W - 1 - m_io, -MAX_REL, MAX_REL) + MAX_REL).astype(F32)


def _relpos_fwd(table, name):
    def body(t_ref, o_ref):
        rr = _dot(t_ref[...], _rel_onehot(), 1, 0, HI)

        def step(q, c):
            o_ref[q] = pltpu.roll(rr, (ROLL_W - (TQ - 1) + q) % ROLL_W, 1)[:, :REL_W]
            return c

        lax.fori_loop(0, TQ, step, 0)

    return pl.pallas_call(
        body, name=name, out_shape=jax.ShapeDtypeStruct((TQ, N_HEADS, REL_W), F32),
        in_specs=[pl.BlockSpec(memory_space=pltpu.VMEM)], out_specs=pl.BlockSpec(memory_space=pltpu.VMEM),
        compiler_params=_cp(),
    )(table)


def _relpos_bwd(dbias_t, name):
    def body(d_ref, o_ref):
        def step(q, acc):
            row = jnp.concatenate([d_ref[q], jnp.zeros((N_HEADS, ROLL_W - REL_W), F32)], axis=1)
            return acc + pltpu.roll(row, TQ - 1 - q, 1)

        drr = lax.fori_loop(0, TQ, step, jnp.zeros((N_HEADS, ROLL_W), F32))
        o_ref[...] = _dot(drr, _rel_onehot(), 1, 1, HI)

    return pl.pallas_call(
        body, name=name, out_shape=jax.ShapeDtypeStruct((N_HEADS, REL_W), F32),
        in_specs=[pl.BlockSpec(memory_space=pltpu.VMEM)], out_specs=pl.BlockSpec(memory_space=pltpu.VMEM),
        compiler_params=_cp(),
    )(dbias_t)


def _attn_probs(qe, kw, bias, kvalid, snk):
    s = _dot(qe, kw, 1, 1) + bias
    s = jnp.where(kvalid, s, NEG)
    m = jnp.maximum(jnp.max(s, axis=-1, keepdims=True), snk)
    p = jnp.exp(s - m)
    inv = 1.0 / (jnp.sum(p, axis=-1, keepdims=True) + jnp.exp(snk - m))
    return p * inv, jnp.exp(snk - m) * inv


def _attn_fwd(q, kp, vp, bias, sinks, pad, name, comm=None):
    t, hd = q.shape
    w = pad + TQ

    def body(sink_ref, q_ref, k_ref, v_ref, b_ref, o_ref):
        hp, i = pl.program_id(0), pl.program_id(1)
        start = pl.multiple_of(i * TQ, TQ)
        qv = q_ref[...]
        kw = k_ref[pl.ds(start, w), :]
        vw = v_ref[pl.ds(start, w), :]
        lo = _lo_mask()
        kvalid = (start + lax.broadcasted_iota(jnp.int32, (1, w), 1)) >= pad
        outs = []
        for e in range(2):
            sel = lo if e == 0 else jnp.logical_not(lo)
            qe = jnp.where(sel, qv, jnp.zeros_like(qv))
            p, _ = _attn_probs(qe, kw, b_ref[e], kvalid, sink_ref[2 * hp + e])
            outs.append(_dot(p.astype(BF16), vw, 1, 0))
        o_ref[...] = jnp.where(lo, outs[0], outs[1]).astype(BF16)

    full = pl.BlockSpec((t + pad, LANES), lambda h, i: (0, h))
    tile = pl.BlockSpec((TQ, LANES), lambda h, i: (i, h))
    (o,), got = _call(
        body, name=name, grid=(hd // LANES, t // TQ),
        in_specs=[pl.BlockSpec(memory_space=pltpu.SMEM), tile, full, full, pl.BlockSpec((2, TQ, w), lambda h, i: (h, 0, 0))],
        out_specs=[tile], out_shape=[jax.ShapeDtypeStruct((t, hd), BF16)],
        args=(sinks, q, kp, vp, bias), sem=("parallel", "arbitrary"), comm=comm)
    return o, got


def _attn_bwd(q, kp, vp, bias, sinks, do, col_off, pad, name, comm=None):
    t, hd = q.shape
    w = pad + TQ
    nhp = hd // LANES

    def body(sink_ref, q_ref, k_ref, v_ref, b_ref, do_ref, dq_ref, dk_ref, dv_ref, db_ref, ds_ref):
        hp, i = pl.program_id(0), pl.program_id(1)
        start = pl.multiple_of(i * TQ, TQ)

        @pl.when(i == 0)
        def _():
            dk_ref[...] = jnp.zeros_like(dk_ref)
            dv_ref[...] = jnp.zeros_like(dv_ref)
            db_ref[...] = jnp.zeros_like(db_ref)
            ds_ref[...] = jnp.zeros_like(ds_ref)

        qv = q_ref[...]
        dov = do_ref[...]
        kw = k_ref[pl.ds(start, w), :]
        vw = v_ref[pl.ds(start, w), :]
        lo = _lo_mask()
        kvalid = (start + lax.broadcasted_iota(jnp.int32, (1, w), 1)) >= pad
        row8 = lax.broadcasted_iota(jnp.int32, (8, LANES), 0)
        dqs, dkw, dvw = [], None, None
        for e in range(2):
            sel = lo if e == 0 else jnp.logical_not(lo)
            qe = jnp.where(sel, qv, jnp.zeros_like(qv))
            doe = jnp.where(sel, dov, jnp.zeros_like(dov))
            p, psink = _attn_probs(qe, kw, b_ref[e], kvalid, sink_ref[2 * hp + e])
            dp = _dot(doe, vw, 1, 1)
            delta = jnp.sum(p * dp, axis=-1, keepdims=True)
            ds = p * (dp - delta)
            db_ref[e] += ds
            dsnk = jnp.sum(-psink * delta, axis=0, keepdims=True)
            ds_ref[0] += jnp.where(row8 == e, dsnk, 0.0)
            dsb = ds.astype(BF16)
            dqs.append(_dot(dsb, kw, 1, 0))
            dk_e = _dot(dsb, qe, 0, 0)
            dv_e = _dot(p.astype(BF16), doe, 0, 0)
            dkw = dk_e if dkw is None else dkw + dk_e
            dvw = dv_e if dvw is None else dvw + dv_e
        dq_ref[...] = jnp.where(lo, dqs[0], dqs[1])
        dk_ref[pl.ds(start, w), :] += dkw
        dv_ref[pl.ds(start, w), :] += dvw

    full = pl.BlockSpec((t + pad, LANES), lambda h, i: (0, h))
    tile = pl.BlockSpec((TQ, LANES), lambda h, i: (i, h))
    btile = pl.BlockSpec((2, TQ, w), lambda h, i: (h, 0, 0))
    return _call(
        body, name=name, grid=(nhp, t // TQ),
        in_specs=[pl.BlockSpec(memory_space=pltpu.SMEM), tile, full, full, btile,
                  pl.BlockSpec((TQ, LANES), lambda h, i: (i, h + col_off))],
        out_specs=[tile, full, full, btile, pl.BlockSpec((1, 8, LANES), lambda h, i: (h, 0, 0))],
        out_shape=[jax.ShapeDtypeStruct((t, hd), F32), jax.ShapeDtypeStruct((t + pad, hd), F32),
                   jax.ShapeDtypeStruct((t + pad, hd), F32), jax.ShapeDtypeStruct((N_HEADS, TQ, w), F32),
                   jax.ShapeDtypeStruct((nhp, 8, LANES), F32)],
        args=(sinks, q, kp, vp, bias, do), sem=("parallel", "arbitrary"), comm=comm)


def _halo_prev(tm):
    return lambda i: jnp.maximum(i * (tm // 8) - 1, 0)


def _halo_next(tm, t):
    return lambda i: jnp.minimum((i + 1) * (tm // 8), t // 8 - 1)


def _taps_prev(tile, halo, ktaps, first):
    tm = tile.shape[0]
    ext = jnp.concatenate([jnp.where(first, 0.0, halo), tile], axis=0)
    return [tile] + [pltpu.roll(ext, s, 0)[8:8 + tm] for s in range(1, ktaps)]


def _taps_next(tile, halo, ktaps, last):
    tm = tile.shape[0]
    ext = jnp.concatenate([tile, jnp.where(last, 0.0, halo)], axis=0)
    return [tile] + [pltpu.roll(ext, tm + 8 - s, 0)[0:tm] for s in range(1, ktaps)]


def _conv_apply(taps, w_ref, ktaps):
    out = taps[0] * w_ref[ktaps - 1:ktaps, :]
    for s in range(1, ktaps):
        out = out + taps[s] * w_ref[ktaps - 1 - s:ktaps - s, :]
    return out


def _silu_grad(x):
    sg = jax.nn.sigmoid(x)
    return x * sg, sg * (1.0 + x * (1.0 - sg))


FFN_TM = 128


def _ffn_mid_fwd(gu, w8, b, name):
    t = gu.shape[0]
    f = D_FF
    tm = FFN_TM

    def body(g_ref, u_ref, h_ref, w_ref, b_ref, a_ref):
        first = pl.program_id(0) == 0
        gc = _conv_apply(_taps_prev(g_ref[...], h_ref[...], 3, first), w_ref, 3) + b_ref[...]
        a_ref[...] = (gc * jax.nn.sigmoid(gc) * u_ref[...]).astype(BF16)

    hp = _halo_prev(tm)
    return pl.pallas_call(
        body, name=name, grid=(t // tm,),
        in_specs=[pl.BlockSpec((tm, f), lambda i: (i, 0)), pl.BlockSpec((tm, f), lambda i: (i, 1)),
                  pl.BlockSpec((8, f), lambda i: (hp(i), 0)), pl.BlockSpec((8, f), lambda i: (0, 0)),
                  pl.BlockSpec((1, f), lambda i: (0, 0))],
        out_specs=pl.BlockSpec((tm, f), lambda i: (i, 0)), out_shape=jax.ShapeDtypeStruct((t, f), BF16),
        compiler_params=_cp(("parallel",)),
    )(gu, gu, gu, w8, b)


def _ffn_mid_bwd1(gu, da, w8, b, name):
    t = gu.shape[0]
    f = D_FF
    tm = FFN_TM

    def body(g_ref, u_ref, h_ref, da_ref, w_ref, b_ref, dgc_ref, dup_ref, dw_ref, db_ref):
        i = pl.program_id(0)

        @pl.when(i == 0)
        def _():
            dw_ref[...] = jnp.zeros_like(dw_ref)
            db_ref[...] = jnp.zeros_like(db_ref)

        taps = _taps_prev(g_ref[...], h_ref[...], 3, i == 0)
        gc = _conv_apply(taps, w_ref, 3) + b_ref[...]
        act, dact = _silu_grad(gc)
        dav = da_ref[...]
        dup_ref[...] = (dav * act).astype(BF16)
        dgc = dav * u_ref[...] * dact
        dgc_ref[...] = dgc
        db_ref[...] += jnp.sum(dgc, axis=0, keepdims=True)
        for s in range(3):
            dw_ref[2 - s:3 - s, :] += jnp.sum(dgc * taps[s], axis=0, keepdims=True)

    hp = _halo_prev(tm)
    row = pl.BlockSpec((tm, f), lambda i: (i, 0))
    return pl.pallas_call(
        body, name=name, grid=(t // tm,),
        in_specs=[row, pl.BlockSpec((tm, f), lambda i: (i, 1)), pl.BlockSpec((8, f), lambda i: (hp(i), 0)), row,
                  pl.BlockSpec((8, f), lambda i: (0, 0)), pl.BlockSpec((1, f), lambda i: (0, 0))],
        out_specs=[row, row, pl.BlockSpec((8, f), lambda i: (0, 0)), pl.BlockSpec((1, f), lambda i: (0, 0))],
        out_shape=[jax.ShapeDtypeStruct((t, f), F32), jax.ShapeDtypeStruct((t, f), BF16),
                   jax.ShapeDtypeStruct((8, f), F32), jax.ShapeDtypeStruct((1, f), F32)],
        compiler_params=_cp(("arbitrary",)),
    )(gu, gu, gu, da, w8, b)


def _conv_bwd_data(dc, w8, ktaps, name):
    t, c = dc.shape
    tm = 256
    tc = _pick(c, (1408, 1024))
    nt = t // tm

    def body(d_ref, h_ref, w_ref, o_ref):
        last = pl.program_id(0) == nt - 1
        o_ref[...] = _conv_apply(_taps_next(d_ref[...], h_ref[...], ktaps, last), w_ref, ktaps).astype(BF16)

    hn = _halo_next(tm, t)
    return pl.pallas_call(
        body, name=name, grid=(nt, c // tc),
        in_specs=[pl.BlockSpec((tm, tc), lambda i, j: (i, j)), pl.BlockSpec((8, tc), lambda i, j: (hn(i), j)),
                  pl.BlockSpec((8, tc), lambda i, j: (0, j))],
        out_specs=pl.BlockSpec((tm, tc), lambda i, j: (i, j)), out_shape=jax.ShapeDtypeStruct((t, c), BF16),
        compiler_params=_cp(("parallel", "parallel")),
    )(dc, dc, w8)


PRE_TM = 256
PRE_TC = 1024


def _ssm_pre_fwd(zx, w8, b, name):
    t = zx.shape[0]
    tm, tc = PRE_TM, PRE_TC
    off = D_INNER // tc

    def body(x_ref, h_ref, w_ref, b_ref, o_ref):
        first = pl.program_id(0) == 0
        c = _conv_apply(_taps_prev(x_ref[...], h_ref[...], 4, first), w_ref, 4) + b_ref[...]
        o_ref[...] = c * jax.nn.sigmoid(c)

    hp = _halo_prev(tm)
    return pl.pallas_call(
        body, name=name, grid=(t // tm, XBC // tc),
        in_specs=[pl.BlockSpec((tm, tc), lambda i, j: (i, j + off)), pl.BlockSpec((8, tc), lambda i, j: (hp(i), j + off)),
                  pl.BlockSpec((8, tc), lambda i, j: (0, j)), pl.BlockSpec((1, tc), lambda i, j: (0, j))],
        out_specs=pl.BlockSpec((tm, tc), lambda i, j: (i, j)), out_shape=jax.ShapeDtypeStruct((t, XBC), F32),
        compiler_params=_cp(("parallel", "parallel")),
    )(zx, zx, w8, b)


def _ssm_pre_bwd1(zx, dxbc, w8, b, name):
    t = zx.shape[0]
    tm, tc = PRE_TM, PRE_TC
    off = D_INNER // tc

    def body(x_ref, h_ref, d_ref, w_ref, b_ref, dc_ref, dw_ref, db_ref):
        i = pl.program_id(1)

        @pl.when(i == 0)
        def _():
            dw_ref[...] = jnp.zeros_like(dw_ref)
            db_ref[...] = jnp.zeros_like(db_ref)

        taps = _taps_prev(x_ref[...], h_ref[...], 4, i == 0)
        c = _conv_apply(taps, w_ref, 4) + b_ref[...]
        _, dact = _silu_grad(c)
        dc = d_ref[...] * dact
        dc_ref[...] = dc
        db_ref[...] += jnp.sum(dc, axis=0, keepdims=True)
        for s in range(4):
            dw_ref[3 - s:4 - s, :] += jnp.sum(dc * taps[s], axis=0, keepdims=True)

    hp = _halo_prev(tm)
    return pl.pallas_call(
        body, name=name, grid=(XBC // tc, t // tm),
        in_specs=[pl.BlockSpec((tm, tc), lambda j, i: (i, j + off)), pl.BlockSpec((8, tc), lambda j, i: (hp(i), j + off)),
                  pl.BlockSpec((tm, tc), lambda j, i: (i, j)),
                  pl.BlockSpec((8, tc), lambda j, i: (0, j)), pl.BlockSpec((1, tc), lambda j, i: (0, j))],
        out_specs=[pl.BlockSpec((tm, tc), lambda j, i: (i, j)), pl.BlockSpec((8, tc), lambda j, i: (0, j)),
                   pl.BlockSpec((1, tc), lambda j, i: (0, j))],
        out_shape=[jax.ShapeDtypeStruct((t, XBC), F32), jax.ShapeDtypeStruct((8, XBC), F32),
                   jax.ShapeDtypeStruct((1, XBC), F32)],
        compiler_params=_cp(("parallel", "arbitrary")),
    )(zx, zx, dxbc, w8, b)


def _head_lanes():
    return lax.broadcasted_iota(jnp.int32, (1, LANES), 1) < SSM_HEADS


def _dt_fwd(dtraw, bias, name):
    t = dtraw.shape[0]
    tm = _pick(t, (1024, 512, 256, 128))

    def body(x_ref, b_ref, o_ref):
        v = x_ref[...] + b_ref[...]
        sp = jnp.maximum(v, 0.0) + jnp.log(1.0 + jnp.exp(-jnp.abs(v)))
        o_ref[...] = jnp.where(_head_lanes(), sp, 0.0)

    row = pl.BlockSpec((tm, LANES), lambda i: (i, 0))
    return pl.pallas_call(
        body, name=name, grid=(t // tm,), in_specs=[row, pl.BlockSpec((1, LANES), lambda i: (0, 0))], out_specs=row,
        out_shape=jax.ShapeDtypeStruct((t, LANES), F32), compiler_params=_cp(("parallel",)),
    )(dtraw, bias)


def _dt_bwd(dtraw, bias, ddt, name):
    t = dtraw.shape[0]
    tm = _pick(t, (1024, 512, 256, 128))

    def body(x_ref, b_ref, d_ref, o_ref, db_ref):
        @pl.when(pl.program_id(0) == 0)
        def _():
            db_ref[...] = jnp.zeros_like(db_ref)

        g = jnp.where(_head_lanes(), d_ref[...] * jax.nn.sigmoid(x_ref[...] + b_ref[...]), 0.0)
        o_ref[...] = g.astype(BF16)
        db_ref[...] += jnp.sum(g, axis=0, keepdims=True)

    row = pl.BlockSpec((tm, LANES), lambda i: (i, 0))
    vec = pl.BlockSpec((1, LANES), lambda i: (0, 0))
    return pl.pallas_call(
        body, name=name, grid=(t // tm,), in_specs=[row, vec, row], out_specs=[row, vec],
        out_shape=[jax.ShapeDtypeStruct((t, LANES), BF16), jax.ShapeDtypeStruct((1, LANES), F32)],
        compiler_params=_cp(("arbitrary",)),
    )(dtraw, bias, ddt)


GROUP_W = D_INNER // SSM_GROUPS
POST_TM = 512


def _ssm_post_fwd(y, xbc, zx, dexp, nw, name):
    t = y.shape[0]
    tm = _pick(t, (POST_TM, 256, 128))

    def body(y_ref, x_ref, z_ref, d_ref, w_ref, o_ref):
        zv = z_ref[...]
        y3 = (y_ref[...] + d_ref[...] * x_ref[...]) * (zv * jax.nn.sigmoid(zv))
        r = lax.rsqrt(jnp.mean(y3 * y3, axis=-1, keepdims=True) + EPS)
        o_ref[...] = (y3 * r * w_ref[...]).astype(BF16)

    blk = pl.BlockSpec((tm, GROUP_W), lambda i, g: (i, g))
    vec = pl.BlockSpec((1, GROUP_W), lambda i, g: (0, g))
    return pl.pallas_call(
        body, name=name, grid=(t // tm, SSM_GROUPS), in_specs=[blk, blk, blk, vec, vec], out_specs=blk,
        out_shape=jax.ShapeDtypeStruct((t, D_INNER), BF16), compiler_params=_cp(("parallel", "parallel")),
    )(y, xbc, zx, dexp, nw)


def _ssm_post_bwd(dy4, y, xbc, zx, dexp, nw, name):
    t = y.shape[0]
    tm = _pick(t, (POST_TM, 256, 128))

    def body(g_ref, y_ref, x_ref, z_ref, d_ref, w_ref, dy_ref, dxs_ref, dz_ref, dd_ref, dw_ref):
        @pl.when(pl.program_id(1) == 0)
        def _():
            dd_ref[...] = jnp.zeros_like(dd_ref)
            dw_ref[...] = jnp.zeros_like(dw_ref)

        zv = z_ref[...]
        xv = x_ref[...]
        act, dact = _silu_grad(zv)
        y2 = y_ref[...] + d_ref[...] * xv
        y3 = y2 * act
        r = lax.rsqrt(jnp.mean(y3 * y3, axis=-1, keepdims=True) + EPS)
        y3n = y3 * r
        gv = g_ref[...]
        dyn = gv * w_ref[...]
        dy3 = r * (dyn - y3n * jnp.mean(dyn * y3n, axis=-1, keepdims=True))
        dy2 = dy3 * act
        dy_ref[...] = dy2
        dxs_ref[...] = dy2 * d_ref[...]
        dz_ref[...] = (dy3 * y2 * dact).astype(BF16)
        dd_ref[...] += jnp.sum(dy2 * xv, axis=0, keepdims=True)
        dw_ref[...] += jnp.sum(gv * y3n, axis=0, keepdims=True)

    blk = pl.BlockSpec((tm, GROUP_W), lambda g, i: (i, g))
    vec = pl.BlockSpec((1, GROUP_W), lambda g, i: (0, g))
    return pl.pallas_call(
        body, name=name, grid=(SSM_GROUPS, t // tm), in_specs=[blk, blk, blk, blk, vec, vec],
        out_specs=[blk, blk, blk, vec, vec],
        out_shape=[jax.ShapeDtypeStruct((t, D_INNER), F32), jax.ShapeDtypeStruct((t, D_INNER), F32),
                   jax.ShapeDtypeStruct((t, D_INNER), BF16), jax.ShapeDtypeStruct((1, D_INNER), F32),
                   jax.ShapeDtypeStruct((1, D_INNER), F32)],
        compiler_params=_cp(("parallel", "arbitrary")),
    )(dy4, y, xbc, zx, dexp, nw)


def _ssd_common(dt, alog):
    ll = dt.shape[0]
    a_neg = -jnp.exp(alog)
    a = dt * a_neg
    ri = lax.broadcasted_iota(jnp.int32, (ll, ll), 0)
    ci = lax.broadcasted_iota(jnp.int32, (ll, ll), 1)
    tril = ri >= ci
    acs = _dot(tril.astype(F32), a, 1, 0, HI)
    return a_neg, tril, acs, acs.T


def _pair_terms(acs, acs_t, dt, h0, lo):
    ll = acs.shape[0]
    cols = [acs[:, h0 + e:h0 + e + 1] for e in range(2)]
    rows = [acs_t[h0 + e:h0 + e + 1, :] for e in range(2)]
    dtc = [dt[:, h0 + e:h0 + e + 1] for e in range(2)]
    lasts = [c[ll - 1:ll, :] for c in cols]
    dtx = jnp.where(lo, dtc[0], dtc[1])
    eac = jnp.where(lo, jnp.exp(cols[0]), jnp.exp(cols[1]))
    fdec = jnp.where(lo, jnp.exp(lasts[0] - cols[0]), jnp.exp(lasts[1] - cols[1]))
    elast = jnp.where(lo, jnp.exp(lasts[0]), jnp.exp(lasts[1]))
    return cols, rows, dtx, eac, fdec, elast


def _decay(col, row, tril):
    return jnp.where(tril, jnp.exp(jnp.minimum(col - row, 0.0)), 0.0)


def _ssd_fwd(xbc, dt, alog, name, comm=None):
    t = xbc.shape[0]
    ll = SSD_L
    nc = t // ll

    def body(x_ref, dt_ref, al_ref, y_ref, sp_ref, st_ref):
        @pl.when(pl.program_id(0) == 0)
        def _():
            st_ref[...] = jnp.zeros_like(st_ref)

        dtv = dt_ref[...]
        _, tril, acs, acs_t = _ssd_common(dtv, al_ref[...])
        lo = _lo_mask()
        sp_ref[0] = st_ref[...]
        for g in range(SSM_GROUPS):
            bg = x_ref[:, D_INNER + SSM_STATE * g:D_INNER + SSM_STATE * (g + 1)].astype(BF16)
            cg = x_ref[:, D_INNER + 512 + SSM_STATE * g:D_INNER + 512 + SSM_STATE * (g + 1)].astype(BF16)
            gm = _dot(cg, bg, 1, 1)
            for pp in range(4):
                h0 = 8 * g + 2 * pp
                c0 = HEAD_DIM * h0
                x2 = x_ref[:, c0:c0 + LANES]
                cols, rows, dtx, eac, fdec, elast = _pair_terms(acs, acs_t, dtv, h0, lo)
                u = (x2 * dtx).astype(BF16)
                ys = [_dot((gm * _decay(cols[e], rows[e], tril)).astype(BF16), u, 1, 0) for e in range(2)]
                s2 = st_ref[:, c0:c0 + LANES]
                y_ref[:, c0:c0 + LANES] = jnp.where(lo, ys[0], ys[1]) + _dot(cg, s2.astype(BF16), 1, 0) * eac
                xw = (x2 * (fdec * dtx)).astype(BF16)
                st_ref[:, c0:c0 + LANES] = s2 * elast + _dot(bg, xw, 0, 0)

    return _call(
        body, name=name, grid=(nc,),
        in_specs=[pl.BlockSpec((ll, XBC), lambda c: (c, 0)), pl.BlockSpec((ll, LANES), lambda c: (c, 0)),
                  pl.BlockSpec((1, LANES), lambda c: (0, 0))],
        out_specs=[pl.BlockSpec((ll, D_INNER), lambda c: (c, 0)), pl.BlockSpec((1, SSM_STATE, D_INNER), lambda c: (c, 0, 0))],
        out_shape=[jax.ShapeDtypeStruct((t, D_INNER), F32), jax.ShapeDtypeStruct((nc, SSM_STATE, D_INNER), F32)],
        scratch_shapes=[pltpu.VMEM((SSM_STATE, D_INNER), F32)],
        args=(xbc, dt, alog), sem=("arbitrary",), comm=comm)


def _ssd_bwd(xbc, dt, alog, sprev, dy, dskip, name, comm=None):
    t = xbc.shape[0]
    ll = SSD_L
    nc = t // ll

    def body(x_ref, dt_ref, al_ref, sp_ref, dy_ref, dk_ref, dx_ref, ddt_ref, dal_ref, ds_ref, colt_ref):
        @pl.when(pl.program_id(0) == 0)
        def _():
            ds_ref[...] = jnp.zeros_like(ds_ref)
            dal_ref[...] = jnp.zeros_like(dal_ref)

        dtv = dt_ref[...]
        a_neg, tril, acs, acs_t = _ssd_common(dtv, al_ref[...])
        lo = _lo_mask()
        hi = jnp.logical_not(lo)
        lane = lax.broadcasted_iota(jnp.int32, (1, LANES), 1)
        colt_ref[...] = jnp.zeros_like(colt_ref)
        rowterm = jnp.zeros((ll, LANES), F32)
        ddt_u = jnp.zeros((ll, LANES), F32)
        dlast = jnp.zeros((1, LANES), F32)

        def halves(v):
            return (jnp.sum(jnp.where(lo, v, 0.0), axis=-1, keepdims=True),
                    jnp.sum(jnp.where(hi, v, 0.0), axis=-1, keepdims=True))

        for g in range(SSM_GROUPS):
            cb0 = D_INNER + SSM_STATE * g
            cc0 = D_INNER + 512 + SSM_STATE * g
            bg = x_ref[:, cb0:cb0 + SSM_STATE].astype(BF16)
            cg = x_ref[:, cc0:cc0 + SSM_STATE].astype(BF16)
            gm = _dot(cg, bg, 1, 1)
            dgm = jnp.zeros((ll, ll), F32)
            dc_st = jnp.zeros((ll, SSM_STATE), F32)
            db_st = jnp.zeros((ll, SSM_STATE), F32)
            for pp in range(4):
                h0 = 8 * g + 2 * pp
                c0 = HEAD_DIM * h0
                x2 = x_ref[:, c0:c0 + LANES]
                cols, rows, dtx, eac, fdec, elast = _pair_terms(acs, acs_t, dtv, h0, lo)
                u32 = x2 * dtx
                u = u32.astype(BF16)
                dy2 = dy_ref[:, c0:c0 + LANES]
                dyb = dy2.astype(BF16)
                sp2 = sp_ref[0, :, c0:c0 + LANES]
                spb = sp2.astype(BF16)
                ds2 = ds_ref[:, c0:c0 + LANES]
                dsb = ds2.astype(BF16)
                xw = (x2 * (fdec * dtx)).astype(BF16)
                du_st = _dot(bg, dsb, 1, 0) * fdec
                yst = _dot(cg, spb, 1, 0) * eac
                dye = (dy2 * eac).astype(BF16)
                dc_st = dc_st + _dot(dye, spb, 1, 1)
                db_st = db_st + _dot(xw, dsb, 1, 1)
                ds_ref[:, c0:c0 + LANES] = ds2 * elast + _dot(cg, dye, 0, 0)
                dus, rsum, csum = [], [], []
                for e in range(2):
                    dec = _decay(cols[e], rows[e], tril)
                    wm = gm * dec
                    dye_m = jnp.where(lo if e == 0 else hi, dyb, jnp.zeros_like(dyb))
                    dum = _dot(dye_m, u, 1, 1)
                    dgm = dgm + dum * dec
                    tm_ = dum * wm
                    rsum.append(jnp.sum(tm_, axis=1, keepdims=True))
                    csum.append(jnp.sum(tm_, axis=0, keepdims=True))
                    dus.append(_dot(wm.astype(BF16), dyb, 0, 0))
                du = jnp.where(lo, dus[0], dus[1]) + du_st
                dx_ref[:, c0:c0 + LANES] = du * dtx + dk_ref[:, c0:c0 + LANES]
                ddtu = halves(du * x2)
                rst = halves(dy2 * yst)
                qst = halves(du_st * u32)
                sst = halves(jnp.sum(ds2 * sp2, axis=0, keepdims=True))
                for e in range(2):
                    oh = lane == (h0 + e)
                    rowterm = rowterm + jnp.where(oh, rsum[e] + rst[e] - qst[e], 0.0)
                    ddt_u = ddt_u + jnp.where(oh, ddtu[e], 0.0)
                    last_e = cols[e][ll - 1:ll, :]
                    dl = jnp.exp(last_e) * sst[e] + jnp.sum(qst[e], axis=0, keepdims=True)
                    dlast = dlast + jnp.where(oh, dl, 0.0)
                    colt_ref[h0 + e:h0 + e + 1, :] = csum[e]
            dgb = dgm.astype(BF16)
            dx_ref[:, cc0:cc0 + SSM_STATE] = _dot(dgb, bg, 1, 0) + dc_st
            dx_ref[:, cb0:cb0 + SSM_STATE] = _dot(dgb, cg, 0, 0) + db_st
        row_io = lax.broadcasted_iota(jnp.int32, (ll, LANES), 0)
        dacs = rowterm - colt_ref[...].T + jnp.where(row_io == ll - 1, dlast, 0.0)
        da = _dot(jnp.logical_not(tril).astype(F32) + jnp.where(
            lax.broadcasted_iota(jnp.int32, (ll, ll), 0) == lax.broadcasted_iota(jnp.int32, (ll, ll), 1), 1.0, 0.0),
            dacs, 1, 0, HI)
        ddt_ref[...] = da * a_neg + ddt_u
        dal_ref[...] += jnp.sum(da * dtv, axis=0, keepdims=True) * a_neg

    rev = lambda c: nc - 1 - c
    return _call(
        body, name=name, grid=(nc,),
        in_specs=[pl.BlockSpec((ll, XBC), lambda c: (rev(c), 0)), pl.BlockSpec((ll, LANES), lambda c: (rev(c), 0)),
                  pl.BlockSpec((1, LANES), lambda c: (0, 0)),
                  pl.BlockSpec((1, SSM_STATE, D_INNER), lambda c: (rev(c), 0, 0)),
                  pl.BlockSpec((ll, D_INNER), lambda c: (rev(c), 0)), pl.BlockSpec((ll, D_INNER), lambda c: (rev(c), 0))],
        out_specs=[pl.BlockSpec((ll, XBC), lambda c: (rev(c), 0)), pl.BlockSpec((ll, LANES), lambda c: (rev(c), 0)),
                   pl.BlockSpec((1, LANES), lambda c: (0, 0))],
        out_shape=[jax.ShapeDtypeStruct((t, XBC), F32), jax.ShapeDtypeStruct((t, LANES), F32),
                   jax.ShapeDtypeStruct((1, LANES), F32)],
        scratch_shapes=[pltpu.VMEM((SSM_STATE, D_INNER), F32), pltpu.VMEM((LANES, ll), F32)],
        args=(xbc, dt, alog, sprev, dy, dskip), sem=("arbitrary",), comm=comm)


def _loss(y, target, name):
    t, d = y.shape
    tm = _pick(t, (512, 256, 128))

    def body(y_ref, t_ref, dy_ref, dyb_ref, acc_ref):
        @pl.when(pl.program_id(0) == 0)
        def _():
            acc_ref[...] = jnp.zeros_like(acc_ref)

        err = y_ref[...] - t_ref[...]
        dy = err * (1.0 / d)
        dy_ref[...] = dy
        dyb_ref[...] = dy.astype(BF16)
        acc_ref[...] += jnp.sum(err * err, axis=0, keepdims=True)

    row = pl.BlockSpec((tm, d), lambda i: (i, 0))
    vec = pl.BlockSpec((1, d), lambda i: (0, 0))
    return pl.pallas_call(
        body, name=name, grid=(t // tm,), in_specs=[row, row], out_specs=[row, row, vec],
        out_shape=[jax.ShapeDtypeStruct((t, d), F32), jax.ShapeDtypeStruct((t, d), BF16), jax.ShapeDtypeStruct((1, d), F32)],
        compiler_params=_cp(("arbitrary",)),
    )(y, target)


ADAM_TR = 512


def _adamw(parts, w, m, v, tr, name, comm=None):
    r, c = w.shape
    c1 = 1.0 - ADAM_B1 ** ADAM_STEP
    c2 = 1.0 - ADAM_B2 ** ADAM_STEP

    def body(p_ref, w_ref, m_ref, v_ref, g_ref, d_ref, mo_ref, vo_ref):
        g = p_ref[0].astype(F32)
        for k in range(1, N_DEV):
            g = g + p_ref[k].astype(F32)
        mn = ADAM_B1 * m_ref[...] + (1.0 - ADAM_B1) * g
        vn = ADAM_B2 * v_ref[...] + (1.0 - ADAM_B2) * (g * g)
        g_ref[...] = g
        mo_ref[...] = mn
        vo_ref[...] = vn
        d_ref[...] = -ADAM_LR * ((mn / c1) / (jnp.sqrt(vn / c2) + ADAM_EPS) + ADAM_WD * w_ref[...])

    row = pl.BlockSpec((tr, c), lambda i: (i, 0))
    sd = jax.ShapeDtypeStruct((r, c), F32)
    return _call(
        body, name=name, grid=(r // tr,),
        in_specs=[pl.BlockSpec((N_DEV, tr, c), lambda i: (0, i, 0)), row, row, row],
        out_specs=[row, row, row, row], out_shape=[sd, sd, sd, sd], args=(parts, w, m, v), sem=("parallel",), comm=comm)


def _peers():
    mx, my, mc = lax.axis_index("x"), lax.axis_index("y"), lax.axis_index("c")
    me = 4 * mx + 2 * my + mc
    out = []
    for k in range(1, N_DEV):
        px = 1 - mx if k & 4 else mx
        py = 1 - my if k & 2 else my
        pc = 1 - mc if k & 1 else mc
        out.append(((px, py, pc), 4 * px + 2 * py + pc))
    return me, out


class _Comm:
    def __init__(self, arrs, scatters):
        self.arrs, self.scatters, self.n = list(arrs), list(scatters), len(arrs)
        self.specs = [pl.BlockSpec(memory_space=pl.ANY)] * self.n
        self.out_shape = [jax.ShapeDtypeStruct(x.shape if sc else (N_DEV,) + x.shape, x.dtype)
                          for x, sc in zip(self.arrs, self.scatters)]
        np_ = N_DEV - 1
        self.scratch = [pltpu.SemaphoreType.DMA((np_ * self.n,)), pltpu.SemaphoreType.DMA((np_ * self.n,)),
                        pltpu.SemaphoreType.DMA((self.n,))]

    def _copies(self, x_refs, o_refs, sems):
        send_sems, recv_sems, local_sems = sems
        me, peers = _peers()
        np_ = N_DEV - 1
        local, sends, recvs = [], [], []
        for a in range(self.n):
            mine = x_refs[a].at[me] if self.scatters[a] else x_refs[a]
            local.append(pltpu.make_async_copy(mine, o_refs[a].at[me], local_sems.at[a]))
        for k, (dev, idx) in enumerate(peers):
            for a in range(self.n):
                mine = x_refs[a].at[me] if self.scatters[a] else x_refs[a]
                sends.append(pltpu.make_async_remote_copy(
                    src_ref=x_refs[a].at[idx] if self.scatters[a] else x_refs[a], dst_ref=o_refs[a].at[me],
                    send_sem=send_sems.at[a * np_ + k], recv_sem=recv_sems.at[a * np_ + k], device_id=dev, device_id_type=MESH))
                recvs.append(pltpu.make_async_remote_copy(
                    src_ref=mine, dst_ref=o_refs[a].at[idx], send_sem=send_sems.at[a * np_ + k],
                    recv_sem=recv_sems.at[a * np_ + k], device_id=dev, device_id_type=MESH))
        return local, sends, recvs

    def start(self, x_refs, o_refs, sems):
        local, sends, _ = self._copies(x_refs, o_refs, sems)
        for cp in local + sends:
            cp.start()

    def wait(self, x_refs, o_refs, sems):
        local, sends, recvs = self._copies(x_refs, o_refs, sems)
        for cp in recvs:
            cp.wait_recv()
        for cp in sends:
            cp.wait_send()
        for cp in local:
            cp.wait()


def _call(body, *, name, grid, in_specs, out_specs, out_shape, args, scratch_shapes=(), sem=None, comm=None):
    if comm is None:
        outs = pl.pallas_call(
            body, name=name, grid=grid, in_specs=list(in_specs), out_specs=list(out_specs), out_shape=list(out_shape),
            scratch_shapes=list(scratch_shapes), compiler_params=_cp(sem),
        )(*args)
        return list(outs), []
    n_in, n_out, nc = len(in_specs), len(out_specs), comm.n
    nsteps = 1
    for g in grid:
        nsteps *= g

    def carrier(*refs):
        ins, cin = refs[:n_in], refs[n_in:n_in + nc]
        outs, cout = refs[n_in + nc:n_in + nc + n_out], refs[n_in + nc + n_out:n_in + 2 * nc + n_out]
        rest = refs[n_in + 2 * nc + n_out:]
        scratch, sems = rest[:len(rest) - 3], rest[len(rest) - 3:]
        if nsteps == 1:
            comm.start(cin, cout, sems)
            body(*ins, *outs, *scratch)
            comm.wait(cin, cout, sems)
            return
        step = 0
        for d, g in enumerate(grid):
            step = step * g + pl.program_id(d)

        @pl.when(step == 0)
        def _():
            comm.start(cin, cout, sems)

        body(*ins, *outs, *scratch)

        @pl.when(step == nsteps - 1)
        def _():
            comm.wait(cin, cout, sems)

    outs = pl.pallas_call(
        carrier, name=name, grid=grid, in_specs=list(in_specs) + comm.specs, out_specs=list(out_specs) + comm.specs,
        out_shape=list(out_shape) + comm.out_shape, scratch_shapes=list(scratch_shapes) + comm.scratch,
        compiler_params=_cp(("arbitrary",) * len(grid) if grid else None),
    )(*args, *comm.arrs)
    return list(outs[:n_out]), list(outs[n_out:])


def _exchange(arrs, scatters, name):
    return _call(lambda *refs: None, name=name, grid=(), in_specs=[], out_specs=[], out_shape=[], args=[],
                 comm=_Comm(arrs, scatters))[1]


def _pack(arrs, dtype, lead=()):
    nl = len(lead)
    flat = jnp.concatenate([a.astype(dtype).reshape(lead + (-1,)) for a in arrs], axis=nl)
    n = flat.shape[-1]
    rows = -(-n // (LANES * ADAM_TR)) * ADAM_TR
    flat = jnp.pad(flat, [(0, 0)] * nl + [(0, rows * LANES - n)])
    return flat.reshape(lead + (rows, LANES))


def _unpack(flat, shapes, lead=()):
    nl = len(lead)
    flat = flat.reshape(lead + (-1,))
    out, o = [], 0
    for s in shapes:
        n = 1
        for d in s:
            n *= d
        out.append(lax.slice_in_dim(flat, o, o + n, axis=nl).reshape(lead + tuple(s)))
        o += n
    return out


def _join(g, ax):
    return jnp.concatenate([g[d] for d in range(N_DEV)], axis=ax)


def _split(full, ax):
    n = full.shape[ax] // N_DEV
    return jnp.stack([lax.slice_in_dim(full, d * n, (d + 1) * n, axis=ax) for d in range(N_DEV)])


_WEIGHTS = ['norm_mix', 'norm_ffn', 'attn_w_in', 'attn_w_out', 'relpos_table', 'q_norm_a', 'k_norm_a', 'q_norm_b',
            'k_norm_b', 'sinks', 'ssm_w_in', 'ssm_conv_w', 'ssm_conv_b', 'ssm_dt_bias', 'ssm_a_log', 'ssm_d', 'ssm_norm',
            'ssm_w_out', 'ffn_w_in', 'ffn_conv_w', 'ffn_conv_b', 'ffn_w_out']
_SHARD_AX = {'attn_w_in': 2, 'attn_w_out': 1, 'ssm_w_in': 2, 'ssm_conv_w': 2, 'ssm_conv_b': 1, 'ssm_norm': 1,
             'ssm_w_out': 1, 'ffn_w_in': 2, 'ffn_conv_w': 2, 'ffn_w_out': 1}
_BIG = ['attn_w_in', 'attn_w_out', 'ssm_w_in', 'ssm_w_out', 'ffn_w_in', 'ffn_w_out']
_SMALL = ['ssm_conv_w', 'ssm_conv_b', 'ssm_norm', 'ffn_conv_w']
_AX2 = {n: _SHARD_AX[n] - 1 for n in _BIG}
_REPL = [n for n in _WEIGHTS if n not in _SHARD_AX]


def _rows8(w):
    return jnp.pad(w, ((0, 8 - w.shape[0]), (0, 0)))


def _lanes128(v):
    return jnp.pad(v, (0, LANES - v.shape[0])).reshape(1, LANES)


def _band_mask(n_prev, pad):
    cq = jnp.arange(TQ)[:, None] // CHUNK
    ck = jnp.arange(pad + TQ)[None, :] // CHUNK
    return (ck >= cq) & (ck <= cq + n_prev)


def _pad_rows(a, pad):
    return jnp.pad(a, ((pad, 0), (0, 0)))


def _kv_expand(a):
    return jnp.concatenate([a[:, :HEAD_DIM]] * 4 + [a[:, HEAD_DIM:]] * 4, axis=1)


def _ffn_fwd(xin, g, w_in, w8, cb, w_out, tag):
    h = _rms_fwd(xin, g, f"rms_ffn{tag}")
    gu = _mm(h, w_in, f"mm_ffn_in{tag}")
    a = _ffn_mid_fwd(gu, w8, cb, f"ffn_mid{tag}")
    xout = _mm(a, w_out, f"mm_ffn_out{tag}", res=xin)
    return xout, (h, gu, a)


def _ffn_bwd(dx, dxb, xin, g, w_in_t, w8, cb, w_out_t, saved, tag):
    h, gu, a = saved
    da = _mm(dxb, w_out_t, f"mm_ffn_da{tag}")
    dw_out = _mm(a, dxb, f"mm_ffn_dwout{tag}", trans_a=True)
    dgc, dup, dw8, dcb = _ffn_mid_bwd1(gu, da, w8, cb, f"ffn_mid_bwd{tag}")
    dgate = _conv_bwd_data(dgc, w8, 3, f"ffn_conv_bwd{tag}")
    dh = _mm(dgate, w_in_t[:D_FF], f"mm_ffn_dh_g{tag}")
    dh = _mm(dup, w_in_t[D_FF:], f"mm_ffn_dh_u{tag}", res=dh)
    dw_in = jnp.concatenate([_mm(h, dgate, f"mm_ffn_dwin_g{tag}", trans_a=True),
                             _mm(h, dup, f"mm_ffn_dwin_u{tag}", trans_a=True)], axis=1)
    dxp, dxpb, dg = _rms_bwd(xin, g, dh, dx, f"rms_ffn_bwd{tag}")
    return dxp, dxpb, dg, dw_in, dw8[:3], dcb, dw_out


def kernel(x, norm_mix, norm_ffn, attn_w_in, attn_w_out, relpos_table, q_norm_a, k_norm_a, q_norm_b, k_norm_b, sinks, ssm_w_in, ssm_conv_w, ssm_conv_b, ssm_dt_bias, ssm_a_log, ssm_d, ssm_norm, ssm_w_out, ffn_w_in, ffn_conv_w, ffn_conv_b, ffn_w_out, loss_target, m_norm_mix, m_norm_ffn, m_attn_w_in, m_attn_w_out, m_relpos_table, m_q_norm_a, m_k_norm_a, m_q_norm_b, m_k_norm_b, m_sinks, m_ssm_w_in, m_ssm_conv_w, m_ssm_conv_b, m_ssm_dt_bias, m_ssm_a_log, m_ssm_d, m_ssm_norm, m_ssm_w_out, m_ffn_w_in, m_ffn_conv_w, m_ffn_conv_b, m_ffn_w_out, v_norm_mix, v_norm_ffn, v_attn_w_in, v_attn_w_out, v_relpos_table, v_q_norm_a, v_k_norm_a, v_q_norm_b, v_k_norm_b, v_sinks, v_ssm_w_in, v_ssm_conv_w, v_ssm_conv_b, v_ssm_dt_bias, v_ssm_a_log, v_ssm_d, v_ssm_norm, v_ssm_w_out, v_ffn_w_in, v_ffn_conv_w, v_ffn_conv_b, v_ffn_w_out):
    w = dict(norm_mix=norm_mix, norm_ffn=norm_ffn, attn_w_in=attn_w_in, attn_w_out=attn_w_out, relpos_table=relpos_table,
             q_norm_a=q_norm_a, k_norm_a=k_norm_a, q_norm_b=q_norm_b, k_norm_b=k_norm_b, sinks=sinks, ssm_w_in=ssm_w_in,
             ssm_conv_w=ssm_conv_w, ssm_conv_b=ssm_conv_b, ssm_dt_bias=ssm_dt_bias, ssm_a_log=ssm_a_log, ssm_d=ssm_d,
             ssm_norm=ssm_norm, ssm_w_out=ssm_w_out, ffn_w_in=ffn_w_in, ffn_conv_w=ffn_conv_w, ffn_conv_b=ffn_conv_b,
             ffn_w_out=ffn_w_out)
    mom = dict(norm_mix=m_norm_mix, norm_ffn=m_norm_ffn, attn_w_in=m_attn_w_in, attn_w_out=m_attn_w_out,
               relpos_table=m_relpos_table, q_norm_a=m_q_norm_a, k_norm_a=m_k_norm_a, q_norm_b=m_q_norm_b,
               k_norm_b=m_k_norm_b, sinks=m_sinks, ssm_w_in=m_ssm_w_in, ssm_conv_w=m_ssm_conv_w, ssm_conv_b=m_ssm_conv_b,
               ssm_dt_bias=m_ssm_dt_bias, ssm_a_log=m_ssm_a_log, ssm_d=m_ssm_d, ssm_norm=m_ssm_norm, ssm_w_out=m_ssm_w_out,
               ffn_w_in=m_ffn_w_in, ffn_conv_w=m_ffn_conv_w, ffn_conv_b=m_ffn_conv_b, ffn_w_out=m_ffn_w_out)
    var = dict(norm_mix=v_norm_mix, norm_ffn=v_norm_ffn, attn_w_in=v_attn_w_in, attn_w_out=v_attn_w_out,
               relpos_table=v_relpos_table, q_norm_a=v_q_norm_a, k_norm_a=v_k_norm_a, q_norm_b=v_q_norm_b,
               k_norm_b=v_k_norm_b, sinks=v_sinks, ssm_w_in=v_ssm_w_in, ssm_conv_w=v_ssm_conv_w, ssm_conv_b=v_ssm_conv_b,
               ssm_dt_bias=v_ssm_dt_bias, ssm_a_log=v_ssm_a_log, ssm_d=v_ssm_d, ssm_norm=v_ssm_norm, ssm_w_out=v_ssm_w_out,
               ffn_w_in=v_ffn_w_in, ffn_conv_w=v_ffn_conv_w, ffn_conv_b=v_ffn_conv_b, ffn_w_out=v_ffn_w_out)

    def piece(n, l):
        return w[n][l].astype(BF16)

    def gather_of(names_layers):
        return _Comm([piece(n, l) for n, l in names_layers], [False] * len(names_layers))

    def joined(got, names_layers):
        return [_join(g, _AX2[n]) for g, (n, _) in zip(got, names_layers)]

    first = [('attn_w_in', 0), ('attn_w_out', 0)]
    got = _exchange([piece(n, l) for n, l in first] + [_pack([w[n] for n in _SMALL], F32)], [False] * 3, "gather_attn")
    w_attn_in, w_attn_out = joined(got[:2], first)
    full = {}
    for n, g in zip(_SMALL, _unpack(got[2], [w[n].shape for n in _SMALL], lead=(N_DEV,))):
        full[n] = _join(g, _SHARD_AX[n])
    ssm_cw8 = _rows8(full['ssm_conv_w'][0])
    ssm_cb = full['ssm_conv_b']
    ssm_nw = full['ssm_norm']
    ffn_cw8 = [_rows8(full['ffn_conv_w'][l]) for l in range(2)]
    ffn_cb = [ffn_conv_b[l:l + 1] for l in range(2)]

    x0 = x[0]
    target = loss_target[0]
    t = x0.shape[0]

    g_mix0, g_mix1 = norm_mix[0:1], norm_mix[1:2]
    g_ffn0, g_ffn1 = norm_ffn[0:1], norm_ffn[1:2]
    h0 = _rms_fwd(x0, g_mix0, "rms_mix0")
    proj = _mm(h0, w_attn_in, "mm_attn_in")
    hn_w = jnp.concatenate([jnp.tile(v, (1, 2)) for v in (q_norm_a, k_norm_a, q_norm_b, k_norm_b)], axis=0)
    qa, ka, va, qb, kb, vb = _headnorm_fwd(proj, hn_w, "headnorm")
    table = jnp.pad(relpos_table[0], ((0, 0), (0, REL_W - (2 * MAX_REL + 1))))
    bias_a = jnp.where(_band_mask(A_PREV, PAD_A)[None], jnp.transpose(_relpos_fwd(table, "relpos_bias"), (1, 0, 2)), NEG)
    rel_b = jnp.arange(TQ)[:, None] - (jnp.arange(PAD_B + TQ)[None, :] - PAD_B)
    slopes = 2.0 ** (-8.0 * jnp.arange(1, N_HEADS + 1, dtype=F32) / N_HEADS)
    bias_b = jnp.where(_band_mask(B_PREV, PAD_B)[None], -slopes[:, None, None] * jnp.abs(rel_b).astype(F32)[None], NEG)
    no_sinks = jnp.full((N_HEADS,), NEG, F32)
    kpa, vpa = _pad_rows(ka, PAD_A), _pad_rows(va, PAD_A)
    kpb, vpb = _pad_rows(_kv_expand(kb), PAD_B), _pad_rows(_kv_expand(vb), PAD_B)
    ffn0_w, ssm_w, ffn1_w = [('ffn_w_in', 0), ('ffn_w_out', 0)], [('ssm_w_in', 0), ('ssm_w_out', 0)], [('ffn_w_in', 1), ('ffn_w_out', 1)]
    oa, got = _attn_fwd(qa, kpa, vpa, bias_a, no_sinks, PAD_A, "attn_a", comm=gather_of(ffn0_w))
    w_ffn_in0, w_ffn_out0 = joined(got, ffn0_w)
    ob, got = _attn_fwd(qb, kpb, vpb, bias_b, sinks[0], PAD_B, "attn_b", comm=gather_of(ssm_w))
    w_ssm_in, w_ssm_out = joined(got, ssm_w)
    w_ssm_main = w_ssm_in[:, :ZX]
    w_ssm_dt = jnp.pad(w_ssm_in[:, ZX:], ((0, 0), (0, LANES - SSM_HEADS)))
    x1 = _mm(oa, w_attn_out[:512], "mm_attn_out_a", res=x0)
    x1 = _mm(ob, w_attn_out[512:], "mm_attn_out_b", res=x1)
    x2, ffn0_saved = _ffn_fwd(x1, g_ffn0, w_ffn_in0, ffn_cw8[0], ffn_cb[0], w_ffn_out0, "0")

    h2 = _rms_fwd(x2, g_mix1, "rms_mix1")
    zx = _mm(h2, w_ssm_main, "mm_ssm_in")
    dtraw = _mm(h2, w_ssm_dt, "mm_ssm_dt")
    dt_bias = _lanes128(ssm_dt_bias[0])
    alog = _lanes128(ssm_a_log[0])
    dexp = jnp.repeat(ssm_d[0], HEAD_DIM).reshape(1, D_INNER)
    xbc = _ssm_pre_fwd(zx, ssm_cw8, ssm_cb, "ssm_pre")
    dt = _dt_fwd(dtraw, dt_bias, "ssm_dt")
    (y, sprev), got = _ssd_fwd(xbc, dt, alog, "ssd_fwd", comm=gather_of(ffn1_w))
    w_ffn_in1, w_ffn_out1 = joined(got, ffn1_w)
    y4 = _ssm_post_fwd(y, xbc, zx, dexp, ssm_nw, "ssm_post")
    x3 = _mm(y4, w_ssm_out, "mm_ssm_out", res=x2)
    x4, ffn1_saved = _ffn_fwd(x3, g_ffn1, w_ffn_in1, ffn_cw8[1], ffn_cb[1], w_ffn_out1, "1")

    dx4, dx4b, sq = _loss(x4, target, "loss")
    loss = lax.psum(0.5 * jnp.sum(sq) / D_MODEL, ("x", "y", "c"))

    grads = {}

    def scatter_of(pieces):
        return _Comm([_split(g, _AX2[n]).astype(BF16) for n, g in pieces], [True] * len(pieces))

    dx3, dx3b, dg_ffn1, dwin1, dcw1, dcb1, dwout1 = _ffn_bwd(
        dx4, dx4b, x3, g_ffn1, w_ffn_in1.T, ffn_cw8[1], ffn_cb[1], w_ffn_out1.T, ffn1_saved, "1")

    dy4 = _mm(dx3b, w_ssm_out.T, "mm_ssm_dy")
    dw_ssm_out = _mm(y4, dx3b, "mm_ssm_dwout", trans_a=True)
    dyv, dskip, dz, dd_lane, dnw = _ssm_post_bwd(dy4, y, xbc, zx, dexp, ssm_nw, "ssm_post_bwd")
    (dxbc, ddt, dalog), parts_ffn1 = _ssd_bwd(xbc, dt, alog, sprev, dyv, dskip, "ssd_bwd",
                                              comm=scatter_of([('ffn_w_in', dwin1), ('ffn_w_out', dwout1)]))
    dc, dcw_s, dcb_s = _ssm_pre_bwd1(zx, dxbc, ssm_cw8, ssm_cb, "ssm_pre_bwd")
    ddtraw, ddtb = _dt_bwd(dtraw, dt_bias, ddt, "ssm_dt_bwd")
    dxr = _conv_bwd_data(dc, ssm_cw8, 4, "ssm_conv_bwd")
    w_main_t = w_ssm_main.T
    dh2 = _mm(dz, w_main_t[:D_INNER], "mm_ssm_dh_z")
    dh2 = _mm(dxr, w_main_t[D_INNER:], "mm_ssm_dh_x", res=dh2)
    dh2 = _mm(ddtraw, w_ssm_dt.T, "mm_ssm_dh_dt", res=dh2)
    dw_ssm_in = jnp.concatenate([
        _mm(h2, dz, "mm_ssm_dwin_z", trans_a=True), _mm(h2, dxr, "mm_ssm_dwin_x", trans_a=True),
        _mm(h2, ddtraw, "mm_ssm_dwin_dt", trans_a=True)[:, :SSM_HEADS]], axis=1)
    dx2, dx2b, dg_mix1 = _rms_bwd(x2, g_mix1, dh2, dx3, "rms_mix1_bwd")
    grads['ssm_conv_w'] = dcw_s[:4][None]
    grads['ssm_conv_b'] = dcb_s
    grads['ssm_norm'] = dnw
    grads['ssm_dt_bias'] = ddtb[:, :SSM_HEADS]
    grads['ssm_a_log'] = dalog[:, :SSM_HEADS]
    grads['ssm_d'] = jnp.sum(dd_lane.reshape(SSM_HEADS, HEAD_DIM), axis=1)[None]

    dx1, dx1b, dg_ffn0, dwin0, dcw0, dcb0, dwout0 = _ffn_bwd(
        dx2, dx2b, x1, g_ffn0, w_ffn_in0.T, ffn_cw8[0], ffn_cb[0], w_ffn_out0.T, ffn0_saved, "0")
    grads['ffn_conv_w'] = jnp.stack([dcw0, dcw1])
    grads['ffn_conv_b'] = jnp.concatenate([dcb0, dcb1], axis=0)
    grads['norm_ffn'] = jnp.concatenate([dg_ffn0, dg_ffn1], axis=0)

    do = _mm(dx1b, w_attn_out.T, "mm_attn_do", out_dtype=BF16)
    dw_attn_out = jnp.concatenate([_mm(oa, dx1b, "mm_attn_dwout_a", trans_a=True),
                                   _mm(ob, dx1b, "mm_attn_dwout_b", trans_a=True)], axis=0)
    (dqa, dkpa, dvpa, dbias_a, _), parts_ssm = _attn_bwd(
        qa, kpa, vpa, bias_a, no_sinks, do, 0, PAD_A, "attn_a_bwd",
        comm=scatter_of([('ssm_w_in', dw_ssm_in), ('ssm_w_out', dw_ssm_out)]))
    (dqb, dkpb, dvpb, _, dsink), parts_ffn0 = _attn_bwd(
        qb, kpb, vpb, bias_b, sinks[0], do, 4, PAD_B, "attn_b_bwd",
        comm=scatter_of([('ffn_w_in', dwin0), ('ffn_w_out', dwout0)]))
    grads['relpos_table'] = _relpos_bwd(jnp.transpose(dbias_a, (1, 0, 2)), "relpos_bwd")[None, :, :2 * MAX_REL + 1]
    grads['sinks'] = dsink[:, :2, 0].reshape(1, N_HEADS)
    dproj, dhn = _headnorm_bwd(proj, hn_w, dqa, dkpa, dvpa, dqb, dkpb, dvpb, "headnorm_bwd")
    dhn = dhn[:, :HEAD_DIM] + dhn[:, HEAD_DIM:]
    for k, n in enumerate(('q_norm_a', 'k_norm_a', 'q_norm_b', 'k_norm_b')):
        grads[n] = dhn[k:k + 1]
    dh0 = _mm(dproj, w_attn_in.T, "mm_attn_dh")
    dw_attn_in = _mm(h0, dproj, "mm_attn_dwin", trans_a=True)
    dx0, _, dg_mix0 = _rms_bwd(x0, g_mix0, dh0, dx1, "rms_mix0_bwd")
    grads['norm_mix'] = jnp.concatenate([dg_mix0, dg_mix1], axis=0)

    def adam_piece(n, l, parts, comm=None):
        rows = w[n][l].shape[0]
        return _adamw(parts, w[n][l], mom[n][l], var[n][l], _pick(rows, (256, 128, 64)), f"adamw_{n}{l}", comm=comm)

    sm_shapes = [w[n].shape for n in _SMALL]
    rp_shapes = [w[n].shape for n in _REPL]
    last = _Comm(
        [_split(dw_attn_in, 1).astype(BF16), _split(dw_attn_out, 0).astype(BF16),
         _pack([_split(grads[n], _SHARD_AX[n]) for n in _SMALL], F32, lead=(N_DEV,)), _pack([grads[n] for n in _REPL], F32)],
        [True, True, True, False])
    res = [{}, {}, {}, {}]
    by_piece = {}
    by_piece['ffn_w_in', 1], recv = adam_piece('ffn_w_in', 1, parts_ffn1[0], comm=last)
    by_piece['ffn_w_out', 1], _ = adam_piece('ffn_w_out', 1, parts_ffn1[1])
    by_piece['ssm_w_in', 0], _ = adam_piece('ssm_w_in', 0, parts_ssm[0])
    by_piece['ssm_w_out', 0], _ = adam_piece('ssm_w_out', 0, parts_ssm[1])
    by_piece['ffn_w_in', 0], _ = adam_piece('ffn_w_in', 0, parts_ffn0[0])
    by_piece['ffn_w_out', 0], _ = adam_piece('ffn_w_out', 0, parts_ffn0[1])
    by_piece['attn_w_in', 0], _ = adam_piece('attn_w_in', 0, recv[0])
    by_piece['attn_w_out', 0], _ = adam_piece('attn_w_out', 0, recv[1])
    for n in _BIG:
        for kind in range(4):
            res[kind][n] = jnp.stack([by_piece[n, l][kind] for l in range(w[n].shape[0])])
    for names, shapes, parts in ((_SMALL, sm_shapes, recv[2]), (_REPL, rp_shapes, recv[3])):
        outs, _ = _adamw(parts, _pack([w[n] for n in names], F32), _pack([mom[n] for n in names], F32),
                         _pack([var[n] for n in names], F32), ADAM_TR, "adamw_" + ("small" if names is _SMALL else "replicated"))
        for kind, flat in enumerate(outs):
            for n, a in zip(names, _unpack(flat, shapes)):
                res[kind][n] = a
    return (loss, dx0[None], *[res[0][n] for n in _WEIGHTS], *[res[1][n] for n in _WEIGHTS],
            *[res[2][n] for n in _WEIGHTS], *[res[3][n] for n in _WEIGHTS])
```

```python
import jax
import jax.numpy as jnp
from jax import lax
from jax.experimental import pallas as pl
from jax.experimental.pallas import tpu as pltpu

F32 = jnp.float32
BF16 = jnp.bfloat16
HI = lax.Precision.HIGHEST
MESH = pl.DeviceIdType.MESH
NEG = -1e30

N_DEV = 8
D_MODEL = 1024
EPS = 1e-6
CHUNK = 64
HEAD_DIM = 64
N_HEADS = 8
A_PREV = 8
B_PREV = 2
MAX_REL = 256
TQ = 2 * CHUNK
ATT_SUB = 2
PAD_A = A_PREV * CHUNK
PAD_B = B_PREV * CHUNK
REL_W = PAD_A + TQ
D_INNER = 2048
SSM_HEADS = 32
SSM_GROUPS = 4
SSM_STATE = 128
XBC = D_INNER + 2 * SSM_GROUPS * SSM_STATE
ZX = D_INNER + XBC
D_FF = 2816
SSD_L = 128
LANES = 128
VMEM_LIMIT = 56 << 20

ADAM_LR, ADAM_B1, ADAM_B2, ADAM_EPS, ADAM_WD, ADAM_STEP = 0.001, 0.9, 0.999, 1e-08, 0.01, 10


def _cp(sem=None):
    return pltpu.CompilerParams(dimension_semantics=sem, vmem_limit_bytes=VMEM_LIMIT)


def _dot(a, b, ca=1, cb=0, prec=None):
    return lax.dot_general(a, b, (((ca,), (cb,)), ((), ())), preferred_element_type=F32, precision=prec)


def _pick(n, cands):
    for c in cands:
        if n % c == 0:
            return c
    return n


def _lo_mask():
    return lax.broadcasted_iota(jnp.int32, (1, LANES), 1) < HEAD_DIM


def _mm(a, b, name, out_dtype=F32, res=None, trans_a=False):
    n = b.shape[1]
    tn = _pick(n, (1408, 1536, 1152, 1024, 512, 256, 128))
    if trans_a:
        kdim, m = a.shape
        assert b.shape[0] == kdim and res is None and out_dtype == F32, (a.shape, b.shape)
        tk = _pick(kdim, (512, 256, 128))

        def body_t(a_ref, b_ref, o_ref):
            @pl.when(pl.program_id(0) == 0)
            def _():
                o_ref[...] = jnp.zeros_like(o_ref)

            av = a_ref[...]
            for c in range(0, n, tn):
                o_ref[:, c:c + tn] += _dot(av, b_ref[:, c:c + tn], 0, 0)

        return pl.pallas_call(
            body_t, name=name, grid=(kdim // tk,),
            in_specs=[pl.BlockSpec((tk, m), lambda k: (k, 0)), pl.BlockSpec((tk, n), lambda k: (k, 0))],
            out_specs=pl.BlockSpec((m, n), lambda k: (0, 0)), out_shape=jax.ShapeDtypeStruct((m, n), F32),
            compiler_params=_cp(("arbitrary",)),
        )(a, b)

    m, kdim = a.shape
    assert b.shape[0] == kdim, (a.shape, b.shape)
    tm = _pick(m, (256, 128) if n > 2304 else (512, 256, 128))

    def body(*refs):
        if res is None:
            a_ref, b_ref, o_ref = refs
        else:
            a_ref, b_ref, r_ref, o_ref = refs
        av = a_ref[...]
        for c in range(0, n, tn):
            r = _dot(av, b_ref[:, c:c + tn], 1, 0)
            if res is not None:
                r = r + r_ref[:, c:c + tn]
            o_ref[:, c:c + tn] = r.astype(out_dtype)

    in_specs = [pl.BlockSpec((tm, kdim), lambda i: (i, 0)), pl.BlockSpec((kdim, n), lambda i: (0, 0))]
    args = [a, b]
    if res is not None:
        in_specs.append(pl.BlockSpec((tm, n), lambda i: (i, 0)))
        args.append(res)
    return pl.pallas_call(
        body, name=name, grid=(m // tm,), in_specs=in_specs, out_specs=pl.BlockSpec((tm, n), lambda i: (i, 0)),
        out_shape=jax.ShapeDtypeStruct((m, n), out_dtype), compiler_params=_cp(("parallel",)),
    )(*args)


def _rms_fwd(x, g, name):
    t, d = x.shape
    tm = _pick(t, (512, 256, 128))

    def body(x_ref, g_ref, h_ref):
        xv = x_ref[...]
        r = lax.rsqrt(jnp.mean(xv * xv, axis=-1, keepdims=True) + EPS)
        h_ref[...] = (xv * r * g_ref[...]).astype(BF16)

    return pl.pallas_call(
        body, name=name, grid=(t // tm,),
        in_specs=[pl.BlockSpec((tm, d), lambda i: (i, 0)), pl.BlockSpec((1, d), lambda i: (0, 0))],
        out_specs=pl.BlockSpec((tm, d), lambda i: (i, 0)),
        out_shape=jax.ShapeDtypeStruct((t, d), BF16), compiler_params=_cp(("parallel",)),
    )(x, g)


def _rms_bwd(x, g, dh, dres, name):
    t, d = x.shape
    tm = _pick(t, (512, 256, 128))

    def body(x_ref, g_ref, dh_ref, dr_ref, dx_ref, dxb_ref, dg_ref):
        i = pl.program_id(0)
        xv = x_ref[...]
        r = lax.rsqrt(jnp.mean(xv * xv, axis=-1, keepdims=True) + EPS)
        xh = xv * r
        dhv = dh_ref[...]
        dxh = dhv * g_ref[...]
        dx = dr_ref[...] + r * (dxh - xh * jnp.mean(dxh * xh, axis=-1, keepdims=True))
        dx_ref[...] = dx
        dxb_ref[...] = dx.astype(BF16)

        @pl.when(i == 0)
        def _():
            dg_ref[...] = jnp.zeros_like(dg_ref)

        dg_ref[...] += jnp.sum(dhv * xh, axis=0, keepdims=True)

    row = pl.BlockSpec((tm, d), lambda i: (i, 0))
    vec = pl.BlockSpec((1, d), lambda i: (0, 0))
    return pl.pallas_call(
        body, name=name, grid=(t // tm,), in_specs=[row, vec, row, row], out_specs=[row, row, vec],
        out_shape=[jax.ShapeDtypeStruct((t, d), F32), jax.ShapeDtypeStruct((t, d), BF16), jax.ShapeDtypeStruct((1, d), F32)],
        compiler_params=_cp(("arbitrary",)),
    )(x, g, dh, dres)


def _head_rms(xs, w, lo):
    sq = xs * xs
    s0 = jnp.sum(jnp.where(lo, sq, 0.0), axis=-1, keepdims=True)
    s1 = jnp.sum(jnp.where(lo, 0.0, sq), axis=-1, keepdims=True)
    r = jnp.where(lo, lax.rsqrt(s0 * (1.0 / HEAD_DIM) + EPS), lax.rsqrt(s1 * (1.0 / HEAD_DIM) + EPS))
    return xs * r, r


def _head_rms_bwd(xs, w, dy, lo):
    xh, r = _head_rms(xs, w, lo)
    dxh = dy * w
    t = dxh * xh
    m0 = jnp.sum(jnp.where(lo, t, 0.0), axis=-1, keepdims=True)
    m1 = jnp.sum(jnp.where(lo, 0.0, t), axis=-1, keepdims=True)
    mm = jnp.where(lo, m0, m1) * (1.0 / HEAD_DIM)
    return r * (dxh - xh * mm), dy * xh


_QSCALE = HEAD_DIM ** -0.5


def _headnorm_fwd(proj, ws, name):
    t = proj.shape[0]
    tm = _pick(t, (256, 128))

    def body(p_ref, w_ref, qa_ref, ka_ref, va_ref, qb_ref, kb_ref, vb_ref):
        lo = _lo_mask()
        for s in range(4):
            c = LANES * s
            xh, _ = _head_rms(p_ref[:, c:c + LANES], None, lo)
            qa_ref[:, c:c + LANES] = (xh * w_ref[0:1, :] * _QSCALE).astype(BF16)
            xh, _ = _head_rms(p_ref[:, 512 + c:512 + c + LANES], None, lo)
            ka_ref[:, c:c + LANES] = (xh * w_ref[1:2, :]).astype(BF16)
            xh, _ = _head_rms(p_ref[:, 1536 + c:1536 + c + LANES], None, lo)
            qb_ref[:, c:c + LANES] = (xh * w_ref[2:3, :] * _QSCALE).astype(BF16)
        va_ref[...] = p_ref[:, 1024:1536].astype(BF16)
        xh, _ = _head_rms(p_ref[:, 2048:2176], None, lo)
        kb_ref[...] = (xh * w_ref[3:4, :]).astype(BF16)
        vb_ref[...] = p_ref[:, 2176:2304].astype(BF16)

    wide = pl.BlockSpec((tm, 512), lambda i: (i, 0))
    narrow = pl.BlockSpec((tm, LANES), lambda i: (i, 0))
    sd = lambda n: jax.ShapeDtypeStruct((t, n), BF16)
    return pl.pallas_call(
        body, name=name, grid=(t // tm,),
        in_specs=[pl.BlockSpec((tm, 2304), lambda i: (i, 0)), pl.BlockSpec((4, LANES), lambda i: (0, 0))],
        out_specs=[wide, wide, wide, wide, narrow, narrow],
        out_shape=[sd(512), sd(512), sd(512), sd(512), sd(LANES), sd(LANES)],
        compiler_params=_cp(("parallel",)),
    )(proj, ws)


def _headnorm_bwd(proj, ws, dqa, dkpa, dvpa, dqb, dkpb, dvpb, name):
    t = proj.shape[0]
    tm = TQ
    offa, offb = PAD_A // tm, PAD_B // tm

    def body(p_ref, w_ref, dqa_ref, dka_ref, dva_ref, dqb_ref, dkb_ref, dvb_ref, dp_ref, dw_ref):
        i = pl.program_id(0)
        lo = _lo_mask()

        @pl.when(i == 0)
        def _():
            dw_ref[...] = jnp.zeros_like(dw_ref)

        acc = [jnp.zeros((1, LANES), F32) for _ in range(4)]
        for s in range(4):
            c = LANES * s
            dx, dwl = _head_rms_bwd(p_ref[:, c:c + LANES], w_ref[0:1, :], dqa_ref[:, c:c + LANES] * _QSCALE, lo)
            dp_ref[:, c:c + LANES] = dx.astype(BF16)
            acc[0] += jnp.sum(dwl, axis=0, keepdims=True)
            dx, dwl = _head_rms_bwd(p_ref[:, 512 + c:512 + c + LANES], w_ref[1:2, :], dka_ref[:, c:c + LANES], lo)
            dp_ref[:, 512 + c:512 + c + LANES] = dx.astype(BF16)
            acc[1] += jnp.sum(dwl, axis=0, keepdims=True)
            dx, dwl = _head_rms_bwd(p_ref[:, 1536 + c:1536 + c + LANES], w_ref[2:3, :], dqb_ref[:, c:c + LANES] * _QSCALE, lo)
            dp_ref[:, 1536 + c:1536 + c + LANES] = dx.astype(BF16)
            acc[2] += jnp.sum(dwl, axis=0, keepdims=True)
        dp_ref[:, 1024:1536] = dva_ref[...].astype(BF16)

        def group_sum(ref):
            s0 = ref[:, 0:128] + ref[:, 128:256]
            s1 = ref[:, 256:384] + ref[:, 384:512]
            s0 = s0 + pltpu.roll(s0, HEAD_DIM, 1)
            s1 = s1 + pltpu.roll(s1, HEAD_DIM, 1)
            return jnp.where(lo, s0, s1)

        dx, dwl = _head_rms_bwd(p_ref[:, 2048:2176], w_ref[3:4, :], group_sum(dkb_ref), lo)
        dp_ref[:, 2048:2176] = dx.astype(BF16)
        acc[3] += jnp.sum(dwl, axis=0, keepdims=True)
        dp_ref[:, 2176:2304] = group_sum(dvb_ref).astype(BF16)
        for n in range(4):
            dw_ref[n:n + 1, :] += acc[n]

    wide = pl.BlockSpec((tm, 512), lambda i: (i, 0))
    pa = pl.BlockSpec((tm, 512), lambda i: (i + offa, 0))
    pb = pl.BlockSpec((tm, 512), lambda i: (i + offb, 0))
    return pl.pallas_call(
        body, name=name, grid=(t // tm,),
        in_specs=[pl.BlockSpec((tm, 2304), lambda i: (i, 0)), pl.BlockSpec((4, LANES), lambda i: (0, 0)),
                  wide, pa, pa, wide, pb, pb],
        out_specs=[pl.BlockSpec((tm, 2304), lambda i: (i, 0)), pl.BlockSpec((4, LANES), lambda i: (0, 0))],
        out_shape=[jax.ShapeDtypeStruct((t, 2304), BF16), jax.ShapeDtypeStruct((4, LANES), F32)],
        compiler_params=_cp(("arbitrary",)),
    )(proj, ws, dqa, dkpa, dvpa, dqb, dkpb, dvpb)


ROLL_W = 1024


def _rel_onehot():
    r_io = lax.broadcasted_iota(jnp.int32, (REL_W, ROLL_W), 0)
    m_io = lax.broadcasted_iota(jnp.int32, (REL_W, ROLL_W), 1)
    return (r_io == jnp.clip(REL_W - 1 - m_io, -MAX_REL, MAX_REL) + MAX_REL).astype(F32)


def _relpos_fwd(table, name):
    def body(t_ref, o_ref):
        rr = _dot(t_ref[...], _rel_onehot(), 1, 0, HI)

        def step(q, c):
            o_ref[q] = pltpu.roll(rr, (ROLL_W - (TQ - 1) + q) % ROLL_W, 1)[:, :REL_W]
            return c

        lax.fori_loop(0, TQ, step, 0)

    return pl.pallas_call(
        body, name=name, out_shape=jax.ShapeDtypeStruct((TQ, N_HEADS, REL_W), F32),
        in_specs=[pl.BlockSpec(memory_space=pltpu.VMEM)], out_specs=pl.BlockSpec(memory_space=pltpu.VMEM),
        compiler_params=_cp(),
    )(table)


def _relpos_bwd(dbias_t, name):
    def body(d_ref, o_ref):
        def step(q, acc):
            row = jnp.concatenate([d_ref[q], jnp.zeros((N_HEADS, ROLL_W - REL_W), F32)], axis=1)
            return acc + pltpu.roll(row, TQ - 1 - q, 1)

        drr = lax.fori_loop(0, TQ, step, jnp.zeros((N_HEADS, ROLL_W), F32))
        o_ref[...] = _dot(drr, _rel_onehot(), 1, 1, HI)

    return pl.pallas_call(
        body, name=name, out_shape=jax.ShapeDtypeStruct((N_HEADS, REL_W), F32),
        in_specs=[pl.BlockSpec(memory_space=pltpu.VMEM)], out_specs=pl.BlockSpec(memory_space=pltpu.VMEM),
        compiler_params=_cp(),
    )(dbias_t)


def _attn_probs(qe, kw, bias, kvalid, snk):
    s = _dot(qe, kw, 1, 1) + bias
    s = jnp.where(kvalid, s, NEG)
    m = jnp.maximum(jnp.max(s, axis=-1, keepdims=True), snk)
    p = jnp.exp(s - m)
    inv = 1.0 / (jnp.sum(p, axis=-1, keepdims=True) + jnp.exp(snk - m))
    return p * inv, jnp.exp(snk - m) * inv


def _attn_fwd(q, kp, vp, bias, sinks, pad, name, comm=None):
    t, hd = q.shape
    w = pad + TQ

    def body(sink_ref, q_ref, k_ref, v_ref, b_ref, o_ref):
        hp, i = pl.program_id(0), pl.program_id(1)
        lo = _lo_mask()
        for j in range(ATT_SUB):
            start = pl.multiple_of((i * ATT_SUB + j) * TQ, TQ)
            qv = q_ref[TQ * j:TQ * (j + 1), :]
            kw = k_ref[pl.ds(start, w), :]
            vw = v_ref[pl.ds(start, w), :]
            kvalid = (start + lax.broadcasted_iota(jnp.int32, (1, w), 1)) >= pad
            outs = []
            for e in range(2):
                sel = lo if e == 0 else jnp.logical_not(lo)
                qe = jnp.where(sel, qv, jnp.zeros_like(qv))
                p, _ = _attn_probs(qe, kw, b_ref[e], kvalid, sink_ref[2 * hp + e])
                outs.append(_dot(p.astype(BF16), vw, 1, 0))
            o_ref[TQ * j:TQ * (j + 1), :] = jnp.where(lo, outs[0], outs[1]).astype(BF16)

    full = pl.BlockSpec((t + pad, LANES), lambda h, i: (0, h))
    tile = pl.BlockSpec((ATT_SUB * TQ, LANES), lambda h, i: (i, h))
    (o,), got = _call(
        body, name=name, grid=(hd // LANES, t // (ATT_SUB * TQ)),
        in_specs=[pl.BlockSpec(memory_space=pltpu.SMEM), tile, full, full, pl.BlockSpec((2, TQ, w), lambda h, i: (h, 0, 0))],
        out_specs=[tile], out_shape=[jax.ShapeDtypeStruct((t, hd), BF16)],
        args=(sinks, q, kp, vp, bias), sem=("parallel", "arbitrary"), comm=comm)
    return o, got


def _attn_bwd(q, kp, vp, bias, sinks, do, col_off, pad, name, comm=None):
    t, hd = q.shape
    w = pad + TQ
    nhp = hd // LANES

    def body(sink_ref, q_ref, k_ref, v_ref, b_ref, do_ref, dq_ref, dk_ref, dv_ref, db_ref, ds_ref):
        hp, i = pl.program_id(0), pl.program_id(1)

        @pl.when(i == 0)
        def _():
            dk_ref[...] = jnp.zeros_like(dk_ref)
            dv_ref[...] = jnp.zeros_like(dv_ref)
            db_ref[...] = jnp.zeros_like(db_ref)
            ds_ref[...] = jnp.zeros_like(ds_ref)

        lo = _lo_mask()
        row8 = lax.broadcasted_iota(jnp.int32, (8, LANES), 0)
        dbias = [None, None]
        dsink = jnp.zeros((8, LANES), F32)
        for j in range(ATT_SUB):
            start = pl.multiple_of((i * ATT_SUB + j) * TQ, TQ)
            qv = q_ref[TQ * j:TQ * (j + 1), :]
            dov = do_ref[TQ * j:TQ * (j + 1), :]
            kw = k_ref[pl.ds(start, w), :]
            vw = v_ref[pl.ds(start, w), :]
            kvalid = (start + lax.broadcasted_iota(jnp.int32, (1, w), 1)) >= pad
            dqs, dkw, dvw = [], None, None
            for e in range(2):
                sel = lo if e == 0 else jnp.logical_not(lo)
                qe = jnp.where(sel, qv, jnp.zeros_like(qv))
                doe = jnp.where(sel, dov, jnp.zeros_like(dov))
                p, psink = _attn_probs(qe, kw, b_ref[e], kvalid, sink_ref[2 * hp + e])
                dp = _dot(doe, vw, 1, 1)
                delta = jnp.sum(p * dp, axis=-1, keepdims=True)
                ds = p * (dp - delta)
                dbias[e] = ds if dbias[e] is None else dbias[e] + ds
                dsink = dsink + jnp.where(row8 == e, jnp.sum(-psink * delta, axis=0, keepdims=True), 0.0)
                dsb = ds.astype(BF16)
                dqs.append(_dot(dsb, kw, 1, 0))
                dk_e = _dot(dsb, qe, 0, 0)
                dv_e = _dot(p.astype(BF16), doe, 0, 0)
                dkw = dk_e if dkw is None else dkw + dk_e
                dvw = dv_e if dvw is None else dvw + dv_e
            dq_ref[TQ * j:TQ * (j + 1), :] = jnp.where(lo, dqs[0], dqs[1])
            dk_ref[pl.ds(start, w), :] += dkw
            dv_ref[pl.ds(start, w), :] += dvw
        for e in range(2):
            db_ref[e] += dbias[e]
        ds_ref[0] += dsink

    full = pl.BlockSpec((t + pad, LANES), lambda h, i: (0, h))
    tile = pl.BlockSpec((ATT_SUB * TQ, LANES), lambda h, i: (i, h))
    btile = pl.BlockSpec((2, TQ, w), lambda h, i: (h, 0, 0))
    return _call(
        body, name=name, grid=(nhp, t // (ATT_SUB * TQ)),
        in_specs=[pl.BlockSpec(memory_space=pltpu.SMEM), tile, full, full, btile,
                  pl.BlockSpec((ATT_SUB * TQ, LANES), lambda h, i: (i, h + col_off))],
        out_specs=[tile, full, full, btile, pl.BlockSpec((1, 8, LANES), lambda h, i: (h, 0, 0))],
        out_shape=[jax.ShapeDtypeStruct((t, hd), F32), jax.ShapeDtypeStruct((t + pad, hd), F32),
                   jax.ShapeDtypeStruct((t + pad, hd), F32), jax.ShapeDtypeStruct((N_HEADS, TQ, w), F32),
                   jax.ShapeDtypeStruct((nhp, 8, LANES), F32)],
        args=(sinks, q, kp, vp, bias, do), sem=("parallel", "arbitrary"), comm=comm)


def _halo_prev(tm):
    return lambda i: jnp.maximum(i * (tm // 8) - 1, 0)


def _halo_next(tm, t):
    return lambda i: jnp.minimum((i + 1) * (tm // 8), t // 8 - 1)


def _taps_prev(tile, halo, ktaps, first):
    tm = tile.shape[0]
    ext = jnp.concatenate([jnp.where(first, 0.0, halo), tile], axis=0)
    return [tile] + [pltpu.roll(ext, s, 0)[8:8 + tm] for s in range(1, ktaps)]


def _taps_next(tile, halo, ktaps, last):
    tm = tile.shape[0]
    ext = jnp.concatenate([tile, jnp.where(last, 0.0, halo)], axis=0)
    return [tile] + [pltpu.roll(ext, tm + 8 - s, 0)[0:tm] for s in range(1, ktaps)]


def _conv_apply(taps, w_ref, ktaps):
    out = taps[0] * w_ref[ktaps - 1:ktaps, :]
    for s in range(1, ktaps):
        out = out + taps[s] * w_ref[ktaps - 1 - s:ktaps - s, :]
    return out


def _silu_grad(x):
    sg = jax.nn.sigmoid(x)
    return x * sg, sg * (1.0 + x * (1.0 - sg))


FFN_TM = 128


def _ffn_mid_fwd(gu, w8, b, name):
    t = gu.shape[0]
    f = D_FF
    tm = FFN_TM

    def body(g_ref, u_ref, h_ref, w_ref, b_ref, a_ref):
        first = pl.program_id(0) == 0
        gc = _conv_apply(_taps_prev(g_ref[...], h_ref[...], 3, first), w_ref, 3) + b_ref[...]
        a_ref[...] = (gc * jax.nn.sigmoid(gc) * u_ref[...]).astype(BF16)

    hp = _halo_prev(tm)
    return pl.pallas_call(
        body, name=name, grid=(t // tm,),
        in_specs=[pl.BlockSpec((tm, f), lambda i: (i, 0)), pl.BlockSpec((tm, f), lambda i: (i, 1)),
                  pl.BlockSpec((8, f), lambda i: (hp(i), 0)), pl.BlockSpec((8, f), lambda i: (0, 0)),
                  pl.BlockSpec((1, f), lambda i: (0, 0))],
        out_specs=pl.BlockSpec((tm, f), lambda i: (i, 0)), out_shape=jax.ShapeDtypeStruct((t, f), BF16),
        compiler_params=_cp(("parallel",)),
    )(gu, gu, gu, w8, b)


def _ffn_mid_bwd1(gu, da, w8, b, name):
    t = gu.shape[0]
    f = D_FF
    tm = FFN_TM

    def body(g_ref, u_ref, h_ref, da_ref, w_ref, b_ref, dgc_ref, dup_ref, dw_ref, db_ref):
        i = pl.program_id(0)

        @pl.when(i == 0)
        def _():
            dw_ref[...] = jnp.zeros_like(dw_ref)
            db_ref[...] = jnp.zeros_like(db_ref)

        taps = _taps_prev(g_ref[...], h_ref[...], 3, i == 0)
        gc = _conv_apply(taps, w_ref, 3) + b_ref[...]
        act, dact = _silu_grad(gc)
        dav = da_ref[...]
        dup_ref[...] = (dav * act).astype(BF16)
        dgc = dav * u_ref[...] * dact
        dgc_ref[...] = dgc
        db_ref[...] += jnp.sum(dgc, axis=0, keepdims=True)
        for s in range(3):
            dw_ref[2 - s:3 - s, :] += jnp.sum(dgc * taps[s], axis=0, keepdims=True)

    hp = _halo_prev(tm)
    row = pl.BlockSpec((tm, f), lambda i: (i, 0))
    return pl.pallas_call(
        body, name=name, grid=(t // tm,),
        in_specs=[row, pl.BlockSpec((tm, f), lambda i: (i, 1)), pl.BlockSpec((8, f), lambda i: (hp(i), 0)), row,
                  pl.BlockSpec((8, f), lambda i: (0, 0)), pl.BlockSpec((1, f), lambda i: (0, 0))],
        out_specs=[row, row, pl.BlockSpec((8, f), lambda i: (0, 0)), pl.BlockSpec((1, f), lambda i: (0, 0))],
        out_shape=[jax.ShapeDtypeStruct((t, f), F32), jax.ShapeDtypeStruct((t, f), BF16),
                   jax.ShapeDtypeStruct((8, f), F32), jax.ShapeDtypeStruct((1, f), F32)],
        compiler_params=_cp(("arbitrary",)),
    )(gu, gu, gu, da, w8, b)


def _conv_bwd_data(dc, w8, ktaps, name):
    t, c = dc.shape
    tm = 256
    tc = _pick(c, (1408, 1024))
    nt = t // tm

    def body(d_ref, h_ref, w_ref, o_ref):
        last = pl.program_id(0) == nt - 1
        o_ref[...] = _conv_apply(_taps_next(d_ref[...], h_ref[...], ktaps, last), w_ref, ktaps).astype(BF16)

    hn = _halo_next(tm, t)
    return pl.pallas_call(
        body, name=name, grid=(nt, c // tc),
        in_specs=[pl.BlockSpec((tm, tc), lambda i, j: (i, j)), pl.BlockSpec((8, tc), lambda i, j: (hn(i), j)),
                  pl.BlockSpec((8, tc), lambda i, j: (0, j))],
        out_specs=pl.BlockSpec((tm, tc), lambda i, j: (i, j)), out_shape=jax.ShapeDtypeStruct((t, c), BF16),
        compiler_params=_cp(("parallel", "parallel")),
    )(dc, dc, w8)


PRE_TM = 256
PRE_TC = 1024


def _ssm_pre_fwd(zx, w8, b, name):
    t = zx.shape[0]
    tm, tc = PRE_TM, PRE_TC
    off = D_INNER // tc

    def body(x_ref, h_ref, w_ref, b_ref, o_ref):
        first = pl.program_id(0) == 0
        c = _conv_apply(_taps_prev(x_ref[...], h_ref[...], 4, first), w_ref, 4) + b_ref[...]
        o_ref[...] = c * jax.nn.sigmoid(c)

    hp = _halo_prev(tm)
    return pl.pallas_call(
        body, name=name, grid=(t // tm, XBC // tc),
        in_specs=[pl.BlockSpec((tm, tc), lambda i, j: (i, j + off)), pl.BlockSpec((8, tc), lambda i, j: (hp(i), j + off)),
                  pl.BlockSpec((8, tc), lambda i, j: (0, j)), pl.BlockSpec((1, tc), lambda i, j: (0, j))],
        out_specs=pl.BlockSpec((tm, tc), lambda i, j: (i, j)), out_shape=jax.ShapeDtypeStruct((t, XBC), F32),
        compiler_params=_cp(("parallel", "parallel")),
    )(zx, zx, w8, b)


def _ssm_pre_bwd1(zx, dxbc, w8, b, name):
    t = zx.shape[0]
    tm, tc = PRE_TM, PRE_TC
    off = D_INNER // tc

    def body(x_ref, h_ref, d_ref, w_ref, b_ref, dc_ref, dw_ref, db_ref):
        i = pl.program_id(1)

        @pl.when(i == 0)
        def _():
            dw_ref[...] = jnp.zeros_like(dw_ref)
            db_ref[...] = jnp.zeros_like(db_ref)

        taps = _taps_prev(x_ref[...], h_ref[...], 4, i == 0)
        c = _conv_apply(taps, w_ref, 4) + b_ref[...]
        _, dact = _silu_grad(c)
        dc = d_ref[...] * dact
        dc_ref[...] = dc
        db_ref[...] += jnp.sum(dc, axis=0, keepdims=True)
        for s in range(4):
            dw_ref[3 - s:4 - s, :] += jnp.sum(dc * taps[s], axis=0, keepdims=True)

    hp = _halo_prev(tm)
    return pl.pallas_call(
        body, name=name, grid=(XBC // tc, t // tm),
        in_specs=[pl.BlockSpec((tm, tc), lambda j, i: (i, j + off)), pl.BlockSpec((8, tc), lambda j, i: (hp(i), j + off)),
                  pl.BlockSpec((tm, tc), lambda j, i: (i, j)),
                  pl.BlockSpec((8, tc), lambda j, i: (0, j)), pl.BlockSpec((1, tc), lambda j, i: (0, j))],
        out_specs=[pl.BlockSpec((tm, tc), lambda j, i: (i, j)), pl.BlockSpec((8, tc), lambda j, i: (0, j)),
                   pl.BlockSpec((1, tc), lambda j, i: (0, j))],
        out_shape=[jax.ShapeDtypeStruct((t, XBC), F32), jax.ShapeDtypeStruct((8, XBC), F32),
                   jax.ShapeDtypeStruct((1, XBC), F32)],
        compiler_params=_cp(("parallel", "arbitrary")),
    )(zx, zx, dxbc, w8, b)


def _head_lanes():
    return lax.broadcasted_iota(jnp.int32, (1, LANES), 1) < SSM_HEADS


def _dt_fwd(dtraw, bias, name):
    t = dtraw.shape[0]
    tm = _pick(t, (1024, 512, 256, 128))

    def body(x_ref, b_ref, o_ref):
        v = x_ref[...] + b_ref[...]
        sp = jnp.maximum(v, 0.0) + jnp.log(1.0 + jnp.exp(-jnp.abs(v)))
        o_ref[...] = jnp.where(_head_lanes(), sp, 0.0)

    row = pl.BlockSpec((tm, LANES), lambda i: (i, 0))
    return pl.pallas_call(
        body, name=name, grid=(t // tm,), in_specs=[row, pl.BlockSpec((1, LANES), lambda i: (0, 0))], out_specs=row,
        out_shape=jax.ShapeDtypeStruct((t, LANES), F32), compiler_params=_cp(("parallel",)),
    )(dtraw, bias)


def _dt_bwd(dtraw, bias, ddt, name):
    t = dtraw.shape[0]
    tm = _pick(t, (1024, 512, 256, 128))

    def body(x_ref, b_ref, d_ref, o_ref, db_ref):
        @pl.when(pl.program_id(0) == 0)
        def _():
            db_ref[...] = jnp.zeros_like(db_ref)

        g = jnp.where(_head_lanes(), d_ref[...] * jax.nn.sigmoid(x_ref[...] + b_ref[...]), 0.0)
        o_ref[...] = g.astype(BF16)
        db_ref[...] += jnp.sum(g, axis=0, keepdims=True)

    row = pl.BlockSpec((tm, LANES), lambda i: (i, 0))
    vec = pl.BlockSpec((1, LANES), lambda i: (0, 0))
    return pl.pallas_call(
        body, name=name, grid=(t // tm,), in_specs=[row, vec, row], out_specs=[row, vec],
        out_shape=[jax.ShapeDtypeStruct((t, LANES), BF16), jax.ShapeDtypeStruct((1, LANES), F32)],
        compiler_params=_cp(("arbitrary",)),
    )(dtraw, bias, ddt)


GROUP_W = D_INNER // SSM_GROUPS
POST_TM = 512


def _ssm_post_fwd(y, xbc, zx, dexp, nw, name):
    t = y.shape[0]
    tm = _pick(t, (POST_TM, 256, 128))

    def body(y_ref, x_ref, z_ref, d_ref, w_ref, o_ref):
        zv = z_ref[...]
        y3 = (y_ref[...] + d_ref[...] * x_ref[...]) * (zv * jax.nn.sigmoid(zv))
        r = lax.rsqrt(jnp.mean(y3 * y3, axis=-1, keepdims=True) + EPS)
        o_ref[...] = (y3 * r * w_ref[...]).astype(BF16)

    blk = pl.BlockSpec((tm, GROUP_W), lambda i, g: (i, g))
    vec = pl.BlockSpec((1, GROUP_W), lambda i, g: (0, g))
    return pl.pallas_call(
        body, name=name, grid=(t // tm, SSM_GROUPS), in_specs=[blk, blk, blk, vec, vec], out_specs=blk,
        out_shape=jax.ShapeDtypeStruct((t, D_INNER), BF16), compiler_params=_cp(("parallel", "parallel")),
    )(y, xbc, zx, dexp, nw)


def _ssm_post_bwd(dy4, y, xbc, zx, dexp, nw, name):
    t = y.shape[0]
    tm = _pick(t, (POST_TM, 256, 128))

    def body(g_ref, y_ref, x_ref, z_ref, d_ref, w_ref, dy_ref, dxs_ref, dz_ref, dd_ref, dw_ref):
        @pl.when(pl.program_id(1) == 0)
        def _():
            dd_ref[...] = jnp.zeros_like(dd_ref)
            dw_ref[...] = jnp.zeros_like(dw_ref)

        zv = z_ref[...]
        xv = x_ref[...]
        act, dact = _silu_grad(zv)
        y2 = y_ref[...] + d_ref[...] * xv
        y3 = y2 * act
        r = lax.rsqrt(jnp.mean(y3 * y3, axis=-1, keepdims=True) + EPS)
        y3n = y3 * r
        gv = g_ref[...]
        dyn = gv * w_ref[...]
        dy3 = r * (dyn - y3n * jnp.mean(dyn * y3n, axis=-1, keepdims=True))
        dy2 = dy3 * act
        dy_ref[...] = dy2
        dxs_ref[...] = dy2 * d_ref[...]
        dz_ref[...] = (dy3 * y2 * dact).astype(BF16)
        dd_ref[...] += jnp.sum(dy2 * xv, axis=0, keepdims=True)
        dw_ref[...] += jnp.sum(gv * y3n, axis=0, keepdims=True)

    blk = pl.BlockSpec((tm, GROUP_W), lambda g, i: (i, g))
    vec = pl.BlockSpec((1, GROUP_W), lambda g, i: (0, g))
    return pl.pallas_call(
        body, name=name, grid=(SSM_GROUPS, t // tm), in_specs=[blk, blk, blk, blk, vec, vec],
        out_specs=[blk, blk, blk, vec, vec],
        out_shape=[jax.ShapeDtypeStruct((t, D_INNER), F32), jax.ShapeDtypeStruct((t, D_INNER), F32),
                   jax.ShapeDtypeStruct((t, D_INNER), BF16), jax.ShapeDtypeStruct((1, D_INNER), F32),
                   jax.ShapeDtypeStruct((1, D_INNER), F32)],
        compiler_params=_cp(("parallel", "arbitrary")),
    )(dy4, y, xbc, zx, dexp, nw)


def _ssd_common(dt, alog):
    ll = dt.shape[0]
    a_neg = -jnp.exp(alog)
    a = dt * a_neg
    ri = lax.broadcasted_iota(jnp.int32, (ll, ll), 0)
    ci = lax.broadcasted_iota(jnp.int32, (ll, ll), 1)
    tril = ri >= ci
    acs = _dot(tril.astype(F32), a, 1, 0, HI)
    return a_neg, tril, acs, acs.T


def _pair_terms(acs, acs_t, dt, h0, lo):
    ll = acs.shape[0]
    cols = [acs[:, h0 + e:h0 + e + 1] for e in range(2)]
    rows = [acs_t[h0 + e:h0 + e + 1, :] for e in range(2)]
    dtc = [dt[:, h0 + e:h0 + e + 1] for e in range(2)]
    lasts = [c[ll - 1:ll, :] for c in cols]
    dtx = jnp.where(lo, dtc[0], dtc[1])
    eac = jnp.where(lo, jnp.exp(cols[0]), jnp.exp(cols[1]))
    fdec = jnp.where(lo, jnp.exp(lasts[0] - cols[0]), jnp.exp(lasts[1] - cols[1]))
    elast = jnp.where(lo, jnp.exp(lasts[0]), jnp.exp(lasts[1]))
    return cols, rows, dtx, eac, fdec, elast


def _decay(col, row, tril):
    return jnp.where(tril, jnp.exp(jnp.minimum(col - row, 0.0)), 0.0)


def _two_heads_rows(v, lo):
    z = jnp.zeros_like(v)
    return jnp.concatenate([jnp.where(lo, v, z), jnp.where(lo, z, v)], axis=0)


def _two_heads_cols(ms):
    return jnp.concatenate(ms, axis=1)


def _ssd_fwd(xbc, dt, alog, name, comm=None):
    t = xbc.shape[0]
    ll = SSD_L
    nc = t // ll

    def body(x_ref, dt_ref, al_ref, y_ref, sp_ref, st_ref):
        @pl.when(pl.program_id(0) == 0)
        def _():
            st_ref[...] = jnp.zeros_like(st_ref)

        dtv = dt_ref[...]
        _, tril, acs, acs_t = _ssd_common(dtv, al_ref[...])
        lo = _lo_mask()
        sp_ref[0] = st_ref[...]
        for g in range(SSM_GROUPS):
            bg = x_ref[:, D_INNER + SSM_STATE * g:D_INNER + SSM_STATE * (g + 1)].astype(BF16)
            cg = x_ref[:, D_INNER + 512 + SSM_STATE * g:D_INNER + 512 + SSM_STATE * (g + 1)].astype(BF16)
            gm = _dot(cg, bg, 1, 1)
            g0 = GROUP_W * g
            terms = [_pair_terms(acs, acs_t, dtv, 8 * g + 2 * pp, lo) for pp in range(4)]
            dtx, eac, fdec, elast = [jnp.concatenate([tt[k] for tt in terms], axis=1) for k in (2, 3, 4, 5)]
            xg = x_ref[:, g0:g0 + GROUP_W]
            ug = (xg * dtx).astype(BF16)
            sg = st_ref[:, g0:g0 + GROUP_W]
            yst = _dot(cg, sg.astype(BF16), 1, 0) * eac
            st_ref[:, g0:g0 + GROUP_W] = sg * elast + _dot(bg, (xg * (fdec * dtx)).astype(BF16), 0, 0)
            for pp in range(4):
                cols, rows = terms[pp][0], terms[pp][1]
                sl = slice(LANES * pp, LANES * (pp + 1))
                y_in = _dot(_two_heads_cols([(gm * _decay(cols[e], rows[e], tril)).astype(BF16) for e in range(2)]),
                            _two_heads_rows(ug[:, sl], lo), 1, 0)
                y_ref[:, g0 + LANES * pp:g0 + LANES * (pp + 1)] = y_in + yst[:, sl]

    return _call(
        body, name=name, grid=(nc,),
        in_specs=[pl.BlockSpec((ll, XBC), lambda c: (c, 0)), pl.BlockSpec((ll, LANES), lambda c: (c, 0)),
                  pl.BlockSpec((1, LANES), lambda c: (0, 0))],
        out_specs=[pl.BlockSpec((ll, D_INNER), lambda c: (c, 0)), pl.BlockSpec((1, SSM_STATE, D_INNER), lambda c: (c, 0, 0))],
        out_shape=[jax.ShapeDtypeStruct((t, D_INNER), F32), jax.ShapeDtypeStruct((nc, SSM_STATE, D_INNER), F32)],
        scratch_shapes=[pltpu.VMEM((SSM_STATE, D_INNER), F32)],
        args=(xbc, dt, alog), sem=("arbitrary",), comm=comm)


def _ssd_bwd(xbc, dt, alog, sprev, dy, dskip, name, comm=None):
    t = xbc.shape[0]
    ll = SSD_L
    nc = t // ll

    def body(x_ref, dt_ref, al_ref, sp_ref, dy_ref, dk_ref, dx_ref, ddt_ref, dal_ref, ds_ref, colt_ref):
        @pl.when(pl.program_id(0) == 0)
        def _():
            ds_ref[...] = jnp.zeros_like(ds_ref)
            dal_ref[...] = jnp.zeros_like(dal_ref)

        dtv = dt_ref[...]
        a_neg, tril, acs, acs_t = _ssd_common(dtv, al_ref[...])
        lo = _lo_mask()
        hi = jnp.logical_not(lo)
        lane = lax.broadcasted_iota(jnp.int32, (1, LANES), 1)
        colt_ref[...] = jnp.zeros_like(colt_ref)
        rowterm = jnp.zeros((ll, LANES), F32)
        ddt_u = jnp.zeros((ll, LANES), F32)
        dlast = jnp.zeros((1, LANES), F32)

        def halves(v):
            return (jnp.sum(jnp.where(lo, v, 0.0), axis=-1, keepdims=True),
                    jnp.sum(jnp.where(hi, v, 0.0), axis=-1, keepdims=True))

        for g in range(SSM_GROUPS):
            cb0 = D_INNER + SSM_STATE * g
            cc0 = D_INNER + 512 + SSM_STATE * g
            bg = x_ref[:, cb0:cb0 + SSM_STATE].astype(BF16)
            cg = x_ref[:, cc0:cc0 + SSM_STATE].astype(BF16)
            gm = _dot(cg, bg, 1, 1)
            g0 = GROUP_W * g
            terms = [_pair_terms(acs, acs_t, dtv, 8 * g + 2 * pp, lo) for pp in range(4)]
            dtx, eac, fdec, elast = [jnp.concatenate([tt[k] for tt in terms], axis=1) for k in (2, 3, 4, 5)]
            xg = x_ref[:, g0:g0 + GROUP_W]
            u32 = xg * dtx
            ug = u32.astype(BF16)
            dyg = dy_ref[:, g0:g0 + GROUP_W]
            dyb = dyg.astype(BF16)
            spg = sp_ref[0, :, g0:g0 + GROUP_W]
            spb = spg.astype(BF16)
            dsg = ds_ref[:, g0:g0 + GROUP_W]
            dsb = dsg.astype(BF16)
            du_st = _dot(bg, dsb, 1, 0) * fdec
            yst = _dot(cg, spb, 1, 0) * eac
            dye = (dyg * eac).astype(BF16)
            dc_st = _dot(dye, spb, 1, 1)
            db_st = _dot((xg * (fdec * dtx)).astype(BF16), dsb, 1, 1)
            ds_ref[:, g0:g0 + GROUP_W] = dsg * elast + _dot(cg, dye, 0, 0)
            qst_el = du_st * u32
            rq_el = dyg * yst - qst_el
            q_row = jnp.sum(qst_el, axis=0, keepdims=True)
            s_row = jnp.sum(dsg * spg, axis=0, keepdims=True)
            dgm = jnp.zeros((ll, ll), F32)
            for pp in range(4):
                h0 = 8 * g + 2 * pp
                cols, rows = terms[pp][0], terms[pp][1]
                sl = slice(LANES * pp, LANES * (pp + 1))
                decs = [_decay(cols[e], rows[e], tril) for e in range(2)]
                wms = [gm * d for d in decs]
                dum2 = _dot(dyb[:, sl], _two_heads_rows(ug[:, sl], lo), 1, 1)
                du = _dot(jnp.concatenate([wm.astype(BF16) for wm in wms], axis=0),
                          _two_heads_rows(dyb[:, sl], lo), 0, 0) + du_st[:, sl]
                dx_ref[:, g0 + LANES * pp:g0 + LANES * (pp + 1)] = du * dtx[:, sl] + dk_ref[:, g0 + LANES * pp:g0 + LANES * (pp + 1)]
                ddtu = halves(du * xg[:, sl])
                rq = halves(rq_el[:, sl])
                qs = halves(q_row[:, sl])
                ss = halves(s_row[:, sl])
                for e in range(2):
                    dum = dum2[:, ll * e:ll * (e + 1)]
                    dgm = dgm + dum * decs[e]
                    tm_ = dum * wms[e]
                    oh = lane == (h0 + e)
                    rowterm = rowterm + jnp.where(oh, jnp.sum(tm_, axis=1, keepdims=True) + rq[e], 0.0)
                    ddt_u = ddt_u + jnp.where(oh, ddtu[e], 0.0)
                    dlast = dlast + jnp.where(oh, jnp.exp(cols[e][ll - 1:ll, :]) * ss[e] + qs[e], 0.0)
                    colt_ref[h0 + e:h0 + e + 1, :] = jnp.sum(tm_, axis=0, keepdims=True)
            dgb = dgm.astype(BF16)
            dx_ref[:, cc0:cc0 + SSM_STATE] = _dot(dgb, bg, 1, 0) + dc_st
            dx_ref[:, cb0:cb0 + SSM_STATE] = _dot(dgb, cg, 0, 0) + db_st
        row_io = lax.broadcasted_iota(jnp.int32, (ll, LANES), 0)
        dacs = rowterm - colt_ref[...].T + jnp.where(row_io == ll - 1, dlast, 0.0)
        da = _dot(jnp.logical_not(tril).astype(F32) + jnp.where(
            lax.broadcasted_iota(jnp.int32, (ll, ll), 0) == lax.broadcasted_iota(jnp.int32, (ll, ll), 1), 1.0, 0.0),
            dacs, 1, 0, HI)
        ddt_ref[...] = da * a_neg + ddt_u
        dal_ref[...] += jnp.sum(da * dtv, axis=0, keepdims=True) * a_neg

    rev = lambda c: nc - 1 - c
    return _call(
        body, name=name, grid=(nc,),
        in_specs=[pl.BlockSpec((ll, XBC), lambda c: (rev(c), 0)), pl.BlockSpec((ll, LANES), lambda c: (rev(c), 0)),
                  pl.BlockSpec((1, LANES), lambda c: (0, 0)),
                  pl.BlockSpec((1, SSM_STATE, D_INNER), lambda c: (rev(c), 0, 0)),
                  pl.BlockSpec((ll, D_INNER), lambda c: (rev(c), 0)), pl.BlockSpec((ll, D_INNER), lambda c: (rev(c), 0))],
        out_specs=[pl.BlockSpec((ll, XBC), lambda c: (rev(c), 0)), pl.BlockSpec((ll, LANES), lambda c: (rev(c), 0)),
                   pl.BlockSpec((1, LANES), lambda c: (0, 0))],
        out_shape=[jax.ShapeDtypeStruct((t, XBC), F32), jax.ShapeDtypeStruct((t, LANES), F32),
                   jax.ShapeDtypeStruct((1, LANES), F32)],
        scratch_shapes=[pltpu.VMEM((SSM_STATE, D_INNER), F32), pltpu.VMEM((LANES, ll), F32)],
        args=(xbc, dt, alog, sprev, dy, dskip), sem=("arbitrary",), comm=comm)


def _loss(y, target, name):
    t, d = y.shape
    tm = _pick(t, (512, 256, 128))

    def body(y_ref, t_ref, dy_ref, dyb_ref, acc_ref):
        @pl.when(pl.program_id(0) == 0)
        def _():
            acc_ref[...] = jnp.zeros_like(acc_ref)

        err = y_ref[...] - t_ref[...]
        dy = err * (1.0 / d)
        dy_ref[...] = dy
        dyb_ref[...] = dy.astype(BF16)
        acc_ref[...] += jnp.sum(err * err, axis=0, keepdims=True)

    row = pl.BlockSpec((tm, d), lambda i: (i, 0))
    vec = pl.BlockSpec((1, d), lambda i: (0, 0))
    return pl.pallas_call(
        body, name=name, grid=(t // tm,), in_specs=[row, row], out_specs=[row, row, vec],
        out_shape=[jax.ShapeDtypeStruct((t, d), F32), jax.ShapeDtypeStruct((t, d), BF16), jax.ShapeDtypeStruct((1, d), F32)],
        compiler_params=_cp(("arbitrary",)),
    )(y, target)


ADAM_TR = 512


def _adamw(parts, w, m, v, tr, name, comm=None):
    r, c = w.shape
    c1 = 1.0 - ADAM_B1 ** ADAM_STEP
    c2 = 1.0 - ADAM_B2 ** ADAM_STEP

    def body(p_ref, w_ref, m_ref, v_ref, g_ref, d_ref, mo_ref, vo_ref):
        g = p_ref[0].astype(F32)
        for k in range(1, N_DEV):
            g = g + p_ref[k].astype(F32)
        mn = ADAM_B1 * m_ref[...] + (1.0 - ADAM_B1) * g
        vn = ADAM_B2 * v_ref[...] + (1.0 - ADAM_B2) * (g * g)
        g_ref[...] = g
        mo_ref[...] = mn
        vo_ref[...] = vn
        d_ref[...] = -ADAM_LR * ((mn / c1) / (jnp.sqrt(vn / c2) + ADAM_EPS) + ADAM_WD * w_ref[...])

    row = pl.BlockSpec((tr, c), lambda i: (i, 0))
    sd = jax.ShapeDtypeStruct((r, c), F32)
    return _call(
        body, name=name, grid=(r // tr,),
        in_specs=[pl.BlockSpec((N_DEV, tr, c), lambda i: (0, i, 0)), row, row, row],
        out_specs=[row, row, row, row], out_shape=[sd, sd, sd, sd], args=(parts, w, m, v), sem=("parallel",), comm=comm)


def _peers():
    mx, my, mc = lax.axis_index("x"), lax.axis_index("y"), lax.axis_index("c")
    me = 4 * mx + 2 * my + mc
    out = []
    for k in range(1, N_DEV):
        px = 1 - mx if k & 4 else mx
        py = 1 - my if k & 2 else my
        pc = 1 - mc if k & 1 else mc
        out.append(((px, py, pc), 4 * px + 2 * py + pc))
    return me, out


class _Comm:
    def __init__(self, arrs, scatters):
        self.arrs, self.scatters, self.n = list(arrs), list(scatters), len(arrs)
        self.specs = [pl.BlockSpec(memory_space=pl.ANY)] * self.n
        self.out_shape = [jax.ShapeDtypeStruct(x.shape if sc else (N_DEV,) + x.shape, x.dtype)
                          for x, sc in zip(self.arrs, self.scatters)]
        np_ = N_DEV - 1
        self.scratch = [pltpu.SemaphoreType.DMA((np_ * self.n,)), pltpu.SemaphoreType.DMA((np_ * self.n,)),
                        pltpu.SemaphoreType.DMA((self.n,))]

    def _copies(self, x_refs, o_refs, sems):
        send_sems, recv_sems, local_sems = sems
        me, peers = _peers()
        np_ = N_DEV - 1
        local, sends, recvs = [], [], []
        for a in range(self.n):
            mine = x_refs[a].at[me] if self.scatters[a] else x_refs[a]
            local.append(pltpu.make_async_copy(mine, o_refs[a].at[me], local_sems.at[a]))
        for k, (dev, idx) in enumerate(peers):
            for a in range(self.n):
                mine = x_refs[a].at[me] if self.scatters[a] else x_refs[a]
                sends.append(pltpu.make_async_remote_copy(
                    src_ref=x_refs[a].at[idx] if self.scatters[a] else x_refs[a], dst_ref=o_refs[a].at[me],
                    send_sem=send_sems.at[a * np_ + k], recv_sem=recv_sems.at[a * np_ + k], device_id=dev, device_id_type=MESH))
                recvs.append(pltpu.make_async_remote_copy(
                    src_ref=mine, dst_ref=o_refs[a].at[idx], send_sem=send_sems.at[a * np_ + k],
                    recv_sem=recv_sems.at[a * np_ + k], device_id=dev, device_id_type=MESH))
        return local, sends, recvs

    def start(self, x_refs, o_refs, sems):
        local, sends, _ = self._copies(x_refs, o_refs, sems)
        for cp in local + sends:
            cp.start()

    def wait(self, x_refs, o_refs, sems):
        local, sends, recvs = self._copies(x_refs, o_refs, sems)
        for cp in recvs:
            cp.wait_recv()
        for cp in sends:
            cp.wait_send()
        for cp in local:
            cp.wait()


def _call(body, *, name, grid, in_specs, out_specs, out_shape, args, scratch_shapes=(), sem=None, comm=None):
    if comm is None:
        outs = pl.pallas_call(
            body, name=name, grid=grid, in_specs=list(in_specs), out_specs=list(out_specs), out_shape=list(out_shape),
            scratch_shapes=list(scratch_shapes), compiler_params=_cp(sem),
        )(*args)
        return list(outs), []
    n_in, n_out, nc = len(in_specs), len(out_specs), comm.n
    nsteps = 1
    for g in grid:
        nsteps *= g

    def carrier(*refs):
        ins, cin = refs[:n_in], refs[n_in:n_in + nc]
        outs, cout = refs[n_in + nc:n_in + nc + n_out], refs[n_in + nc + n_out:n_in + 2 * nc + n_out]
        rest = refs[n_in + 2 * nc + n_out:]
        scratch, sems = rest[:len(rest) - 3], rest[len(rest) - 3:]
        if nsteps == 1:
            comm.start(cin, cout, sems)
            body(*ins, *outs, *scratch)
            comm.wait(cin, cout, sems)
            return
        step = 0
        for d, g in enumerate(grid):
            step = step * g + pl.program_id(d)

        @pl.when(step == 0)
        def _():
            comm.start(cin, cout, sems)

        body(*ins, *outs, *scratch)

        @pl.when(step == nsteps - 1)
        def _():
            comm.wait(cin, cout, sems)

    outs = pl.pallas_call(
        carrier, name=name, grid=grid, in_specs=list(in_specs) + comm.specs, out_specs=list(out_specs) + comm.specs,
        out_shape=list(out_shape) + comm.out_shape, scratch_shapes=list(scratch_shapes) + comm.scratch,
        compiler_params=_cp(("arbitrary",) * len(grid) if grid else None),
    )(*args, *comm.arrs)
    return list(outs[:n_out]), list(outs[n_out:])


def _exchange(arrs, scatters, name):
    return _call(lambda *refs: None, name=name, grid=(), in_specs=[], out_specs=[], out_shape=[], args=[],
                 comm=_Comm(arrs, scatters))[1]


def _pack(arrs, dtype, lead=()):
    nl = len(lead)
    flat = jnp.concatenate([a.astype(dtype).reshape(lead + (-1,)) for a in arrs], axis=nl)
    n = flat.shape[-1]
    rows = -(-n // (LANES * ADAM_TR)) * ADAM_TR
    flat = jnp.pad(flat, [(0, 0)] * nl + [(0, rows * LANES - n)])
    return flat.reshape(lead + (rows, LANES))


def _unpack(flat, shapes, lead=()):
    nl = len(lead)
    flat = flat.reshape(lead + (-1,))
    out, o = [], 0
    for s in shapes:
        n = 1
        for d in s:
            n *= d
        out.append(lax.slice_in_dim(flat, o, o + n, axis=nl).reshape(lead + tuple(s)))
        o += n
    return out


def _join(g, ax):
    return jnp.concatenate([g[d] for d in range(N_DEV)], axis=ax)


def _split(full, ax):
    n = full.shape[ax] // N_DEV
    return jnp.stack([lax.slice_in_dim(full, d * n, (d + 1) * n, axis=ax) for d in range(N_DEV)])


_WEIGHTS = ['norm_mix', 'norm_ffn', 'attn_w_in', 'attn_w_out', 'relpos_table', 'q_norm_a', 'k_norm_a', 'q_norm_b',
            'k_norm_b', 'sinks', 'ssm_w_in', 'ssm_conv_w', 'ssm_conv_b', 'ssm_dt_bias', 'ssm_a_log', 'ssm_d', 'ssm_norm',
            'ssm_w_out', 'ffn_w_in', 'ffn_conv_w', 'ffn_conv_b', 'ffn_w_out']
_SHARD_AX = {'attn_w_in': 2, 'attn_w_out': 1, 'ssm_w_in': 2, 'ssm_conv_w': 2, 'ssm_conv_b': 1, 'ssm_norm': 1,
             'ssm_w_out': 1, 'ffn_w_in': 2, 'ffn_conv_w': 2, 'ffn_w_out': 1}
_BIG = ['attn_w_in', 'attn_w_out', 'ssm_w_in', 'ssm_w_out', 'ffn_w_in', 'ffn_w_out']
_SMALL = ['ssm_conv_w', 'ssm_conv_b', 'ssm_norm', 'ffn_conv_w']
_AX2 = {n: _SHARD_AX[n] - 1 for n in _BIG}
_REPL = [n for n in _WEIGHTS if n not in _SHARD_AX]


def _rows8(w):
    return jnp.pad(w, ((0, 8 - w.shape[0]), (0, 0)))


def _lanes128(v):
    return jnp.pad(v, (0, LANES - v.shape[0])).reshape(1, LANES)


def _band_mask(n_prev, pad):
    cq = jnp.arange(TQ)[:, None] // CHUNK
    ck = jnp.arange(pad + TQ)[None, :] // CHUNK
    return (ck >= cq) & (ck <= cq + n_prev)


def _pad_rows(a, pad):
    return jnp.pad(a, ((pad, 0), (0, 0)))


def _kv_expand(a):
    return jnp.concatenate([a[:, :HEAD_DIM]] * 4 + [a[:, HEAD_DIM:]] * 4, axis=1)


def _ffn_fwd(xin, g, w_in, w8, cb, w_out, tag):
    h = _rms_fwd(xin, g, f"rms_ffn{tag}")
    gu = _mm(h, w_in, f"mm_ffn_in{tag}")
    a = _ffn_mid_fwd(gu, w8, cb, f"ffn_mid{tag}")
    xout = _mm(a, w_out, f"mm_ffn_out{tag}", res=xin)
    return xout, (h, gu, a)


def _ffn_bwd(dx, dxb, xin, g, w_in_t, w8, cb, w_out_t, saved, tag):
    h, gu, a = saved
    da = _mm(dxb, w_out_t, f"mm_ffn_da{tag}")
    dw_out = _mm(a, dxb, f"mm_ffn_dwout{tag}", trans_a=True)
    dgc, dup, dw8, dcb = _ffn_mid_bwd1(gu, da, w8, cb, f"ffn_mid_bwd{tag}")
    dgate = _conv_bwd_data(dgc, w8, 3, f"ffn_conv_bwd{tag}")
    dh = _mm(dgate, w_in_t[:D_FF], f"mm_ffn_dh_g{tag}")
    dh = _mm(dup, w_in_t[D_FF:], f"mm_ffn_dh_u{tag}", res=dh)
    dw_in = jnp.concatenate([_mm(h, dgate, f"mm_ffn_dwin_g{tag}", trans_a=True),
                             _mm(h, dup, f"mm_ffn_dwin_u{tag}", trans_a=True)], axis=1)
    dxp, dxpb, dg = _rms_bwd(xin, g, dh, dx, f"rms_ffn_bwd{tag}")
    return dxp, dxpb, dg, dw_in, dw8[:3], dcb, dw_out


def kernel(x, norm_mix, norm_ffn, attn_w_in, attn_w_out, relpos_table, q_norm_a, k_norm_a, q_norm_b, k_norm_b, sinks, ssm_w_in, ssm_conv_w, ssm_conv_b, ssm_dt_bias, ssm_a_log, ssm_d, ssm_norm, ssm_w_out, ffn_w_in, ffn_conv_w, ffn_conv_b, ffn_w_out, loss_target, m_norm_mix, m_norm_ffn, m_attn_w_in, m_attn_w_out, m_relpos_table, m_q_norm_a, m_k_norm_a, m_q_norm_b, m_k_norm_b, m_sinks, m_ssm_w_in, m_ssm_conv_w, m_ssm_conv_b, m_ssm_dt_bias, m_ssm_a_log, m_ssm_d, m_ssm_norm, m_ssm_w_out, m_ffn_w_in, m_ffn_conv_w, m_ffn_conv_b, m_ffn_w_out, v_norm_mix, v_norm_ffn, v_attn_w_in, v_attn_w_out, v_relpos_table, v_q_norm_a, v_k_norm_a, v_q_norm_b, v_k_norm_b, v_sinks, v_ssm_w_in, v_ssm_conv_w, v_ssm_conv_b, v_ssm_dt_bias, v_ssm_a_log, v_ssm_d, v_ssm_norm, v_ssm_w_out, v_ffn_w_in, v_ffn_conv_w, v_ffn_conv_b, v_ffn_w_out):
    w = dict(norm_mix=norm_mix, norm_ffn=norm_ffn, attn_w_in=attn_w_in, attn_w_out=attn_w_out, relpos_table=relpos_table,
             q_norm_a=q_norm_a, k_norm_a=k_norm_a, q_norm_b=q_norm_b, k_norm_b=k_norm_b, sinks=sinks, ssm_w_in=ssm_w_in,
             ssm_conv_w=ssm_conv_w, ssm_conv_b=ssm_conv_b, ssm_dt_bias=ssm_dt_bias, ssm_a_log=ssm_a_log, ssm_d=ssm_d,
             ssm_norm=ssm_norm, ssm_w_out=ssm_w_out, ffn_w_in=ffn_w_in, ffn_conv_w=ffn_conv_w, ffn_conv_b=ffn_conv_b,
             ffn_w_out=ffn_w_out)
    mom = dict(norm_mix=m_norm_mix, norm_ffn=m_norm_ffn, attn_w_in=m_attn_w_in, attn_w_out=m_attn_w_out,
               relpos_table=m_relpos_table, q_norm_a=m_q_norm_a, k_norm_a=m_k_norm_a, q_norm_b=m_q_norm_b,
               k_norm_b=m_k_norm_b, sinks=m_sinks, ssm_w_in=m_ssm_w_in, ssm_conv_w=m_ssm_conv_w, ssm_conv_b=m_ssm_conv_b,
               ssm_dt_bias=m_ssm_dt_bias, ssm_a_log=m_ssm_a_log, ssm_d=m_ssm_d, ssm_norm=m_ssm_norm, ssm_w_out=m_ssm_w_out,
               ffn_w_in=m_ffn_w_in, ffn_conv_w=m_ffn_conv_w, ffn_conv_b=m_ffn_conv_b, ffn_w_out=m_ffn_w_out)
    var = dict(norm_mix=v_norm_mix, norm_ffn=v_norm_ffn, attn_w_in=v_attn_w_in, attn_w_out=v_attn_w_out,
               relpos_table=v_relpos_table, q_norm_a=v_q_norm_a, k_norm_a=v_k_norm_a, q_norm_b=v_q_norm_b,
               k_norm_b=v_k_norm_b, sinks=v_sinks, ssm_w_in=v_ssm_w_in, ssm_conv_w=v_ssm_conv_w, ssm_conv_b=v_ssm_conv_b,
               ssm_dt_bias=v_ssm_dt_bias, ssm_a_log=v_ssm_a_log, ssm_d=v_ssm_d, ssm_norm=v_ssm_norm, ssm_w_out=v_ssm_w_out,
               ffn_w_in=v_ffn_w_in, ffn_conv_w=v_ffn_conv_w, ffn_conv_b=v_ffn_conv_b, ffn_w_out=v_ffn_w_out)

    def piece(n, l):
        return w[n][l].astype(BF16)

    def gather_of(names_layers):
        return _Comm([piece(n, l) for n, l in names_layers], [False] * len(names_layers))

    def joined(got, names_layers):
        return [_join(g, _AX2[n]) for g, (n, _) in zip(got, names_layers)]

    first = [('attn_w_in', 0), ('attn_w_out', 0)]
    got = _exchange([piece(n, l) for n, l in first] + [_pack([w[n] for n in _SMALL], F32)], [False] * 3, "gather_attn")
    w_attn_in, w_attn_out = joined(got[:2], first)
    full = {}
    for n, g in zip(_SMALL, _unpack(got[2], [w[n].shape for n in _SMALL], lead=(N_DEV,))):
        full[n] = _join(g, _SHARD_AX[n])
    ssm_cw8 = _rows8(full['ssm_conv_w'][0])
    ssm_cb = full['ssm_conv_b']
    ssm_nw = full['ssm_norm']
    ffn_cw8 = [_rows8(full['ffn_conv_w'][l]) for l in range(2)]
    ffn_cb = [ffn_conv_b[l:l + 1] for l in range(2)]

    x0 = x[0]
    target = loss_target[0]
    t = x0.shape[0]

    g_mix0, g_mix1 = norm_mix[0:1], norm_mix[1:2]
    g_ffn0, g_ffn1 = norm_ffn[0:1], norm_ffn[1:2]
    h0 = _rms_fwd(x0, g_mix0, "rms_mix0")
    proj = _mm(h0, w_attn_in, "mm_attn_in")
    hn_w = jnp.concatenate([jnp.tile(v, (1, 2)) for v in (q_norm_a, k_norm_a, q_norm_b, k_norm_b)], axis=0)
    qa, ka, va, qb, kb, vb = _headnorm_fwd(proj, hn_w, "headnorm")
    table = jnp.pad(relpos_table[0], ((0, 0), (0, REL_W - (2 * MAX_REL + 1))))
    bias_a = jnp.where(_band_mask(A_PREV, PAD_A)[None], jnp.transpose(_relpos_fwd(table, "relpos_bias"), (1, 0, 2)), NEG)
    rel_b = jnp.arange(TQ)[:, None] - (jnp.arange(PAD_B + TQ)[None, :] - PAD_B)
    slopes = 2.0 ** (-8.0 * jnp.arange(1, N_HEADS + 1, dtype=F32) / N_HEADS)
    bias_b = jnp.where(_band_mask(B_PREV, PAD_B)[None], -slopes[:, None, None] * jnp.abs(rel_b).astype(F32)[None], NEG)
    no_sinks = jnp.full((N_HEADS,), NEG, F32)
    kpa, vpa = _pad_rows(ka, PAD_A), _pad_rows(va, PAD_A)
    kpb, vpb = _pad_rows(_kv_expand(kb), PAD_B), _pad_rows(_kv_expand(vb), PAD_B)
    ffn0_w, ssm_w, ffn1_w = [('ffn_w_in', 0), ('ffn_w_out', 0)], [('ssm_w_in', 0), ('ssm_w_out', 0)], [('ffn_w_in', 1), ('ffn_w_out', 1)]
    oa, got = _attn_fwd(qa, kpa, vpa, bias_a, no_sinks, PAD_A, "attn_a", comm=gather_of(ffn0_w))
    w_ffn_in0, w_ffn_out0 = joined(got, ffn0_w)
    ob, got = _attn_fwd(qb, kpb, vpb, bias_b, sinks[0], PAD_B, "attn_b", comm=gather_of(ssm_w))
    w_ssm_in, w_ssm_out = joined(got, ssm_w)
    w_ssm_main = w_ssm_in[:, :ZX]
    w_ssm_dt = jnp.pad(w_ssm_in[:, ZX:], ((0, 0), (0, LANES - SSM_HEADS)))
    x1 = _mm(oa, w_attn_out[:512], "mm_attn_out_a", res=x0)
    x1 = _mm(ob, w_attn_out[512:], "mm_attn_out_b", res=x1)
    x2, ffn0_saved = _ffn_fwd(x1, g_ffn0, w_ffn_in0, ffn_cw8[0], ffn_cb[0], w_ffn_out0, "0")

    h2 = _rms_fwd(x2, g_mix1, "rms_mix1")
    zx = _mm(h2, w_ssm_main, "mm_ssm_in")
    dtraw = _mm(h2, w_ssm_dt, "mm_ssm_dt")
    dt_bias = _lanes128(ssm_dt_bias[0])
    alog = _lanes128(ssm_a_log[0])
    dexp = jnp.repeat(ssm_d[0], HEAD_DIM).reshape(1, D_INNER)
    xbc = _ssm_pre_fwd(zx, ssm_cw8, ssm_cb, "ssm_pre")
    dt = _dt_fwd(dtraw, dt_bias, "ssm_dt")
    (y, sprev), got = _ssd_fwd(xbc, dt, alog, "ssd_fwd", comm=gather_of(ffn1_w))
    w_ffn_in1, w_ffn_out1 = joined(got, ffn1_w)
    y4 = _ssm_post_fwd(y, xbc, zx, dexp, ssm_nw, "ssm_post")
    x3 = _mm(y4, w_ssm_out, "mm_ssm_out", res=x2)
    x4, ffn1_saved = _ffn_fwd(x3, g_ffn1, w_ffn_in1, ffn_cw8[1], ffn_cb[1], w_ffn_out1, "1")

    dx4, dx4b, sq = _loss(x4, target, "loss")
    loss = lax.psum(0.5 * jnp.sum(sq) / D_MODEL, ("x", "y", "c"))

    grads = {}

    def scatter_of(pieces):
        return _Comm([_split(g, _AX2[n]).astype(BF16) for n, g in pieces], [True] * len(pieces))

    dx3, dx3b, dg_ffn1, dwin1, dcw1, dcb1, dwout1 = _ffn_bwd(
        dx4, dx4b, x3, g_ffn1, w_ffn_in1.T, ffn_cw8[1], ffn_cb[1], w_ffn_out1.T, ffn1_saved, "1")

    dy4 = _mm(dx3b, w_ssm_out.T, "mm_ssm_dy")
    dw_ssm_out = _mm(y4, dx3b, "mm_ssm_dwout", trans_a=True)
    dyv, dskip, dz, dd_lane, dnw = _ssm_post_bwd(dy4, y, xbc, zx, dexp, ssm_nw, "ssm_post_bwd")
    (dxbc, ddt, dalog), parts_ffn1 = _ssd_bwd(xbc, dt, alog, sprev, dyv, dskip, "ssd_bwd",
                                              comm=scatter_of([('ffn_w_in', dwin1), ('ffn_w_out', dwout1)]))
    dc, dcw_s, dcb_s = _ssm_pre_bwd1(zx, dxbc, ssm_cw8, ssm_cb, "ssm_pre_bwd")
    ddtraw, ddtb = _dt_bwd(dtraw, dt_bias, ddt, "ssm_dt_bwd")
    dxr = _conv_bwd_data(dc, ssm_cw8, 4, "ssm_conv_bwd")
    w_main_t = w_ssm_main.T
    dh2 = _mm(dz, w_main_t[:D_INNER], "mm_ssm_dh_z")
    dh2 = _mm(dxr, w_main_t[D_INNER:], "mm_ssm_dh_x", res=dh2)
    dh2 = _mm(ddtraw, w_ssm_dt.T, "mm_ssm_dh_dt", res=dh2)
    dw_ssm_in = jnp.concatenate([
        _mm(h2, dz, "mm_ssm_dwin_z", trans_a=True), _mm(h2, dxr, "mm_ssm_dwin_x", trans_a=True),
        _mm(h2, ddtraw, "mm_ssm_dwin_dt", trans_a=True)[:, :SSM_HEADS]], axis=1)
    dx2, dx2b, dg_mix1 = _rms_bwd(x2, g_mix1, dh2, dx3, "rms_mix1_bwd")
    grads['ssm_conv_w'] = dcw_s[:4][None]
    grads['ssm_conv_b'] = dcb_s
    grads['ssm_norm'] = dnw
    grads['ssm_dt_bias'] = ddtb[:, :SSM_HEADS]
    grads['ssm_a_log'] = dalog[:, :SSM_HEADS]
    grads['ssm_d'] = jnp.sum(dd_lane.reshape(SSM_HEADS, HEAD_DIM), axis=1)[None]

    dx1, dx1b, dg_ffn0, dwin0, dcw0, dcb0, dwout0 = _ffn_bwd(
        dx2, dx2b, x1, g_ffn0, w_ffn_in0.T, ffn_cw8[0], ffn_cb[0], w_ffn_out0.T, ffn0_saved, "0")
    grads['ffn_conv_w'] = jnp.stack([dcw0, dcw1])
    grads['ffn_conv_b'] = jnp.concatenate([dcb0, dcb1], axis=0)
    grads['norm_ffn'] = jnp.concatenate([dg_ffn0, dg_ffn1], axis=0)

    do = _mm(dx1b, w_attn_out.T, "mm_attn_do", out_dtype=BF16)
    dw_attn_out = jnp.concatenate([_mm(oa, dx1b, "mm_attn_dwout_a", trans_a=True),
                                   _mm(ob, dx1b, "mm_attn_dwout_b", trans_a=True)], axis=0)
    (dqa, dkpa, dvpa, dbias_a, _), parts_ssm = _attn_bwd(
        qa, kpa, vpa, bias_a, no_sinks, do, 0, PAD_A, "attn_a_bwd",
        comm=scatter_of([('ssm_w_in', dw_ssm_in), ('ssm_w_out', dw_ssm_out)]))
    (dqb, dkpb, dvpb, _, dsink), parts_ffn0 = _attn_bwd(
        qb, kpb, vpb, bias_b, sinks[0], do, 4, PAD_B, "attn_b_bwd",
        comm=scatter_of([('ffn_w_in', dwin0), ('ffn_w_out', dwout0)]))
    grads['relpos_table'] = _relpos_bwd(jnp.transpose(dbias_a, (1, 0, 2)), "relpos_bwd")[None, :, :2 * MAX_REL + 1]
    grads['sinks'] = dsink[:, :2, 0].reshape(1, N_HEADS)
    dproj, dhn = _headnorm_bwd(proj, hn_w, dqa, dkpa, dvpa, dqb, dkpb, dvpb, "headnorm_bwd")
    dhn = dhn[:, :HEAD_DIM] + dhn[:, HEAD_DIM:]
    for k, n in enumerate(('q_norm_a', 'k_norm_a', 'q_norm_b', 'k_norm_b')):
        grads[n] = dhn[k:k + 1]
    dh0 = _mm(dproj, w_attn_in.T, "mm_attn_dh")
    dw_attn_in = _mm(h0, dproj, "mm_attn_dwin", trans_a=True)
    dx0, _, dg_mix0 = _rms_bwd(x0, g_mix0, dh0, dx1, "rms_mix0_bwd")
    grads['norm_mix'] = jnp.concatenate([dg_mix0, dg_mix1], axis=0)

    def adam_piece(n, l, parts, comm=None):
        rows = w[n][l].shape[0]
        return _adamw(parts, w[n][l], mom[n][l], var[n][l], _pick(rows, (256, 128, 64)), f"adamw_{n}{l}", comm=comm)

    sm_shapes = [w[n].shape for n in _SMALL]
    rp_shapes = [w[n].shape for n in _REPL]
    last = _Comm(
        [_split(dw_attn_in, 1).astype(BF16), _split(dw_attn_out, 0).astype(BF16),
         _pack([_split(grads[n], _SHARD_AX[n]) for n in _SMALL], F32, lead=(N_DEV,)), _pack([grads[n] for n in _REPL], F32)],
        [True, True, True, False])
    res = [{}, {}, {}, {}]
    by_piece = {}
    by_piece['ffn_w_in', 1], recv = adam_piece('ffn_w_in', 1, parts_ffn1[0], comm=last)
    by_piece['ffn_w_out', 1], _ = adam_piece('ffn_w_out', 1, parts_ffn1[1])
    by_piece['ssm_w_in', 0], _ = adam_piece('ssm_w_in', 0, parts_ssm[0])
    by_piece['ssm_w_out', 0], _ = adam_piece('ssm_w_out', 0, parts_ssm[1])
    by_piece['ffn_w_in', 0], _ = adam_piece('ffn_w_in', 0, parts_ffn0[0])
    by_piece['ffn_w_out', 0], _ = adam_piece('ffn_w_out', 0, parts_ffn0[1])
    by_piece['attn_w_in', 0], _ = adam_piece('attn_w_in', 0, recv[0])
    by_piece['attn_w_out', 0], _ = adam_piece('attn_w_out', 0, recv[1])
    for n in _BIG:
        for kind in range(4):
            res[kind][n] = jnp.stack([by_piece[n, l][kind] for l in range(w[n].shape[0])])
    for names, shapes, parts in ((_SMALL, sm_shapes, recv[2]), (_REPL, rp_shapes, recv[3])):
        outs, _ = _adamw(parts, _pack([w[n] for n in names], F32), _pack([mom[n] for n in names], F32),
                         _pack([var[n] for n in names], F32), ADAM_TR, "adamw_" + ("small" if names is _SMALL else "replicated"))
        for kind, flat in enumerate(outs):
            for n, a in zip(names, _unpack(flat, shapes)):
                res[kind][n] = a
    return (loss, dx0[None], *[res[0][n] for n in _WEIGHTS], *[res[1][n] for n in _WEIGHTS],
            *[res[2][n] for n in _WEIGHTS], *[res[3][n] for n in _WEIGHTS])
```

```python
import jax
import jax.numpy as jnp
from jax import lax
from jax.experimental import pallas as pl
from jax.experimental.pallas import tpu as pltpu

F32 = jnp.float32
BF16 = jnp.bfloat16
HI = lax.Precision.HIGHEST
MESH = pl.DeviceIdType.MESH
NEG = -1e30

N_DEV = 8
D_MODEL = 1024
EPS = 1e-6
CHUNK = 64
HEAD_DIM = 64
N_HEADS = 8
A_PREV = 8
B_PREV = 2
MAX_REL = 256
TQ = 2 * CHUNK
ATT_SUB = 2
PAD_A = A_PREV * CHUNK
PAD_B = B_PREV * CHUNK
REL_W = PAD_A + TQ
D_INNER = 2048
SSM_HEADS = 32
SSM_GROUPS = 4
SSM_STATE = 128
XBC = D_INNER + 2 * SSM_GROUPS * SSM_STATE
ZX = D_INNER + XBC
D_FF = 2816
SSD_L = 128
LANES = 128
VMEM_LIMIT = 56 << 20

ADAM_LR, ADAM_B1, ADAM_B2, ADAM_EPS, ADAM_WD, ADAM_STEP = 0.001, 0.9, 0.999, 1e-08, 0.01, 10


def _cp(sem=None):
    return pltpu.CompilerParams(dimension_semantics=sem, vmem_limit_bytes=VMEM_LIMIT)


def _dot(a, b, ca=1, cb=0, prec=None):
    return lax.dot_general(a, b, (((ca,), (cb,)), ((), ())), preferred_element_type=F32, precision=prec)


def _pick(n, cands):
    for c in cands:
        if n % c == 0:
            return c
    return n


def _lo_mask():
    return lax.broadcasted_iota(jnp.int32, (1, LANES), 1) < HEAD_DIM


def _mm(a, b, name, out_dtype=F32, res=None, trans_a=False):
    n = b.shape[1]
    tn = _pick(n, (1408, 1536, 1152, 1024, 512, 256, 128))
    if trans_a:
        kdim, m = a.shape
        assert b.shape[0] == kdim and res is None and out_dtype == F32, (a.shape, b.shape)
        tk = _pick(kdim, (512, 256, 128))

        def body_t(a_ref, b_ref, o_ref):
            @pl.when(pl.program_id(0) == 0)
            def _():
                o_ref[...] = jnp.zeros_like(o_ref)

            av = a_ref[...]
            for c in range(0, n, tn):
                o_ref[:, c:c + tn] += _dot(av, b_ref[:, c:c + tn], 0, 0)

        return pl.pallas_call(
            body_t, name=name, grid=(kdim // tk,),
            in_specs=[pl.BlockSpec((tk, m), lambda k: (k, 0)), pl.BlockSpec((tk, n), lambda k: (k, 0))],
            out_specs=pl.BlockSpec((m, n), lambda k: (0, 0)), out_shape=jax.ShapeDtypeStruct((m, n), F32),
            compiler_params=_cp(("arbitrary",)),
        )(a, b)

    m, kdim = a.shape
    assert b.shape[0] == kdim, (a.shape, b.shape)
    tm = _pick(m, (256, 128) if n > 2304 else (512, 256, 128))

    def body(*refs):
        if res is None:
            a_ref, b_ref, o_ref = refs
        else:
            a_ref, b_ref, r_ref, o_ref = refs
        av = a_ref[...]
        for c in range(0, n, tn):
            r = _dot(av, b_ref[:, c:c + tn], 1, 0)
            if res is not None:
                r = r + r_ref[:, c:c + tn]
            o_ref[:, c:c + tn] = r.astype(out_dtype)

    in_specs = [pl.BlockSpec((tm, kdim), lambda i: (i, 0)), pl.BlockSpec((kdim, n), lambda i: (0, 0))]
    args = [a, b]
    if res is not None:
        in_specs.append(pl.BlockSpec((tm, n), lambda i: (i, 0)))
        args.append(res)
    return pl.pallas_call(
        body, name=name, grid=(m // tm,), in_specs=in_specs, out_specs=pl.BlockSpec((tm, n), lambda i: (i, 0)),
        out_shape=jax.ShapeDtypeStruct((m, n), out_dtype), compiler_params=_cp(("parallel",)),
    )(*args)


def _rms_fwd(x, g, name):
    t, d = x.shape
    tm = _pick(t, (512, 256, 128))

    def body(x_ref, g_ref, h_ref):
        xv = x_ref[...]
        r = lax.rsqrt(jnp.mean(xv * xv, axis=-1, keepdims=True) + EPS)
        h_ref[...] = (xv * r * g_ref[...]).astype(BF16)

    return pl.pallas_call(
        body, name=name, grid=(t // tm,),
        in_specs=[pl.BlockSpec((tm, d), lambda i: (i, 0)), pl.BlockSpec((1, d), lambda i: (0, 0))],
        out_specs=pl.BlockSpec((tm, d), lambda i: (i, 0)),
        out_shape=jax.ShapeDtypeStruct((t, d), BF16), compiler_params=_cp(("parallel",)),
    )(x, g)


def _rms_bwd(x, g, dh, dres, name, comm=None):
    t, d = x.shape
    tm = _pick(t, (512, 256, 128))

    def body(x_ref, g_ref, dh_ref, dr_ref, dx_ref, dxb_ref, dg_ref):
        i = pl.program_id(0)
        xv = x_ref[...]
        r = lax.rsqrt(jnp.mean(xv * xv, axis=-1, keepdims=True) + EPS)
        xh = xv * r
        dhv = dh_ref[...]
        dxh = dhv * g_ref[...]
        dx = dr_ref[...] + r * (dxh - xh * jnp.mean(dxh * xh, axis=-1, keepdims=True))
        dx_ref[...] = dx
        dxb_ref[...] = dx.astype(BF16)

        @pl.when(i == 0)
        def _():
            dg_ref[...] = jnp.zeros_like(dg_ref)

        dg_ref[...] += jnp.sum(dhv * xh, axis=0, keepdims=True)

    row = pl.BlockSpec((tm, d), lambda i: (i, 0))
    vec = pl.BlockSpec((1, d), lambda i: (0, 0))
    outs, got = _call(
        body, name=name, grid=(t // tm,), in_specs=[row, vec, row, row], out_specs=[row, row, vec],
        out_shape=[jax.ShapeDtypeStruct((t, d), F32), jax.ShapeDtypeStruct((t, d), BF16), jax.ShapeDtypeStruct((1, d), F32)],
        args=(x, g, dh, dres), sem=("arbitrary",), comm=comm)
    return (*outs, got) if comm is not None else tuple(outs)


def _head_rms(xs, w, lo):
    sq = xs * xs
    s0 = jnp.sum(jnp.where(lo, sq, 0.0), axis=-1, keepdims=True)
    s1 = jnp.sum(jnp.where(lo, 0.0, sq), axis=-1, keepdims=True)
    r = jnp.where(lo, lax.rsqrt(s0 * (1.0 / HEAD_DIM) + EPS), lax.rsqrt(s1 * (1.0 / HEAD_DIM) + EPS))
    return xs * r, r


def _head_rms_bwd(xs, w, dy, lo):
    xh, r = _head_rms(xs, w, lo)
    dxh = dy * w
    t = dxh * xh
    m0 = jnp.sum(jnp.where(lo, t, 0.0), axis=-1, keepdims=True)
    m1 = jnp.sum(jnp.where(lo, 0.0, t), axis=-1, keepdims=True)
    mm = jnp.where(lo, m0, m1) * (1.0 / HEAD_DIM)
    return r * (dxh - xh * mm), dy * xh


_QSCALE = HEAD_DIM ** -0.5


def _headnorm_fwd(proj, ws, name):
    t = proj.shape[0]
    tm = _pick(t, (256, 128))

    def body(p_ref, w_ref, qa_ref, ka_ref, va_ref, qb_ref, kb_ref, vb_ref):
        lo = _lo_mask()
        for s in range(4):
            c = LANES * s
            xh, _ = _head_rms(p_ref[:, c:c + LANES], None, lo)
            qa_ref[:, c:c + LANES] = (xh * w_ref[0:1, :] * _QSCALE).astype(BF16)
            xh, _ = _head_rms(p_ref[:, 512 + c:512 + c + LANES], None, lo)
            ka_ref[:, c:c + LANES] = (xh * w_ref[1:2, :]).astype(BF16)
            xh, _ = _head_rms(p_ref[:, 1536 + c:1536 + c + LANES], None, lo)
            qb_ref[:, c:c + LANES] = (xh * w_ref[2:3, :] * _QSCALE).astype(BF16)
        va_ref[...] = p_ref[:, 1024:1536].astype(BF16)
        xh, _ = _head_rms(p_ref[:, 2048:2176], None, lo)
        kb_ref[...] = (xh * w_ref[3:4, :]).astype(BF16)
        vb_ref[...] = p_ref[:, 2176:2304].astype(BF16)

    wide = pl.BlockSpec((tm, 512), lambda i: (i, 0))
    narrow = pl.BlockSpec((tm, LANES), lambda i: (i, 0))
    sd = lambda n: jax.ShapeDtypeStruct((t, n), BF16)
    return pl.pallas_call(
        body, name=name, grid=(t // tm,),
        in_specs=[pl.BlockSpec((tm, 2304), lambda i: (i, 0)), pl.BlockSpec((4, LANES), lambda i: (0, 0))],
        out_specs=[wide, wide, wide, wide, narrow, narrow],
        out_shape=[sd(512), sd(512), sd(512), sd(512), sd(LANES), sd(LANES)],
        compiler_params=_cp(("parallel",)),
    )(proj, ws)


def _headnorm_bwd(proj, ws, dqa, dkpa, dvpa, dqb, dkpb, dvpb, name):
    t = proj.shape[0]
    tm = TQ
    offa, offb = PAD_A // tm, PAD_B // tm

    def body(p_ref, w_ref, dqa_ref, dka_ref, dva_ref, dqb_ref, dkb_ref, dvb_ref, dp_ref, dw_ref):
        i = pl.program_id(0)
        lo = _lo_mask()

        @pl.when(i == 0)
        def _():
            dw_ref[...] = jnp.zeros_like(dw_ref)

        acc = [jnp.zeros((1, LANES), F32) for _ in range(4)]
        for s in range(4):
            c = LANES * s
            dx, dwl = _head_rms_bwd(p_ref[:, c:c + LANES], w_ref[0:1, :], dqa_ref[:, c:c + LANES] * _QSCALE, lo)
            dp_ref[:, c:c + LANES] = dx.astype(BF16)
            acc[0] += jnp.sum(dwl, axis=0, keepdims=True)
            dx, dwl = _head_rms_bwd(p_ref[:, 512 + c:512 + c + LANES], w_ref[1:2, :], dka_ref[:, c:c + LANES], lo)
            dp_ref[:, 512 + c:512 + c + LANES] = dx.astype(BF16)
            acc[1] += jnp.sum(dwl, axis=0, keepdims=True)
            dx, dwl = _head_rms_bwd(p_ref[:, 1536 + c:1536 + c + LANES], w_ref[2:3, :], dqb_ref[:, c:c + LANES] * _QSCALE, lo)
            dp_ref[:, 1536 + c:1536 + c + LANES] = dx.astype(BF16)
            acc[2] += jnp.sum(dwl, axis=0, keepdims=True)
        dp_ref[:, 1024:1536] = dva_ref[...].astype(BF16)

        def group_sum(ref):
            s0 = ref[:, 0:128] + ref[:, 128:256]
            s1 = ref[:, 256:384] + ref[:, 384:512]
            s0 = s0 + pltpu.roll(s0, HEAD_DIM, 1)
            s1 = s1 + pltpu.roll(s1, HEAD_DIM, 1)
            return jnp.where(lo, s0, s1)

        dx, dwl = _head_rms_bwd(p_ref[:, 2048:2176], w_ref[3:4, :], group_sum(dkb_ref), lo)
        dp_ref[:, 2048:2176] = dx.astype(BF16)
        acc[3] += jnp.sum(dwl, axis=0, keepdims=True)
        dp_ref[:, 2176:2304] = group_sum(dvb_ref).astype(BF16)
        for n in range(4):
            dw_ref[n:n + 1, :] += acc[n]

    wide = pl.BlockSpec((tm, 512), lambda i: (i, 0))
    pa = pl.BlockSpec((tm, 512), lambda i: (i + offa, 0))
    pb = pl.BlockSpec((tm, 512), lambda i: (i + offb, 0))
    return pl.pallas_call(
        body, name=name, grid=(t // tm,),
        in_specs=[pl.BlockSpec((tm, 2304), lambda i: (i, 0)), pl.BlockSpec((4, LANES), lambda i: (0, 0)),
                  wide, pa, pa, wide, pb, pb],
        out_specs=[pl.BlockSpec((tm, 2304), lambda i: (i, 0)), pl.BlockSpec((4, LANES), lambda i: (0, 0))],
        out_shape=[jax.ShapeDtypeStruct((t, 2304), BF16), jax.ShapeDtypeStruct((4, LANES), F32)],
        compiler_params=_cp(("arbitrary",)),
    )(proj, ws, dqa, dkpa, dvpa, dqb, dkpb, dvpb)


ROLL_W = 1024


def _rel_onehot():
    r_io = lax.broadcasted_iota(jnp.int32, (REL_W, ROLL_W), 0)
    m_io = lax.broadcasted_iota(jnp.int32, (REL_W, ROLL_W), 1)
    return (r_io == jnp.clip(REL_W - 1 - m_io, -MAX_REL, MAX_REL) + MAX_REL).astype(F32)


def _relpos_fwd(table, name):
    def body(t_ref, o_ref):
        rr = _dot(t_ref[...], _rel_onehot(), 1, 0, HI)

        def step(q, c):
            o_ref[q] = pltpu.roll(rr, (ROLL_W - (TQ - 1) + q) % ROLL_W, 1)[:, :REL_W]
            return c

        lax.fori_loop(0, TQ, step, 0)

    return pl.pallas_call(
        body, name=name, out_shape=jax.ShapeDtypeStruct((TQ, N_HEADS, REL_W), F32),
        in_specs=[pl.BlockSpec(memory_space=pltpu.VMEM)], out_specs=pl.BlockSpec(memory_space=pltpu.VMEM),
        compiler_params=_cp(),
    )(table)


def _relpos_bwd(dbias_t, name):
    def body(d_ref, o_ref):
        def step(q, acc):
            row = jnp.concatenate([d_ref[q], jnp.zeros((N_HEADS, ROLL_W - REL_W), F32)], axis=1)
            return acc + pltpu.roll(row, TQ - 1 - q, 1)

        drr = lax.fori_loop(0, TQ, step, jnp.zeros((N_HEADS, ROLL_W), F32))
        o_ref[...] = _dot(drr, _rel_onehot(), 1, 1, HI)

    return pl.pallas_call(
        body, name=name, out_shape=jax.ShapeDtypeStruct((N_HEADS, REL_W), F32),
        in_specs=[pl.BlockSpec(memory_space=pltpu.VMEM)], out_specs=pl.BlockSpec(memory_space=pltpu.VMEM),
        compiler_params=_cp(),
    )(dbias_t)


def _attn_probs(qe, kw, bias, kvalid, snk):
    s = _dot(qe, kw, 1, 1) + bias
    s = jnp.where(kvalid, s, NEG)
    m = jnp.maximum(jnp.max(s, axis=-1, keepdims=True), snk)
    p = jnp.exp(s - m)
    inv = 1.0 / (jnp.sum(p, axis=-1, keepdims=True) + jnp.exp(snk - m))
    return p * inv, jnp.exp(snk - m) * inv


def _attn_fwd(q, kp, vp, bias, sinks, pad, name, comm=None):
    t, hd = q.shape
    w = pad + TQ

    def body(sink_ref, q_ref, k_ref, v_ref, b_ref, o_ref):
        hp, i = pl.program_id(0), pl.program_id(1)
        lo = _lo_mask()
        for j in range(ATT_SUB):
            start = pl.multiple_of((i * ATT_SUB + j) * TQ, TQ)
            qv = q_ref[TQ * j:TQ * (j + 1), :]
            kw = k_ref[pl.ds(start, w), :]
            vw = v_ref[pl.ds(start, w), :]
            kvalid = (start + lax.broadcasted_iota(jnp.int32, (1, w), 1)) >= pad
            outs = []
            for e in range(2):
                sel = lo if e == 0 else jnp.logical_not(lo)
                qe = jnp.where(sel, qv, jnp.zeros_like(qv))
                p, _ = _attn_probs(qe, kw, b_ref[e], kvalid, sink_ref[2 * hp + e])
                outs.append(_dot(p.astype(BF16), vw, 1, 0))
            o_ref[TQ * j:TQ * (j + 1), :] = jnp.where(lo, outs[0], outs[1]).astype(BF16)

    full = pl.BlockSpec((t + pad, LANES), lambda h, i: (0, h))
    tile = pl.BlockSpec((ATT_SUB * TQ, LANES), lambda h, i: (i, h))
    (o,), got = _call(
        body, name=name, grid=(hd // LANES, t // (ATT_SUB * TQ)),
        in_specs=[pl.BlockSpec(memory_space=pltpu.SMEM), tile, full, full, pl.BlockSpec((2, TQ, w), lambda h, i: (h, 0, 0))],
        out_specs=[tile], out_shape=[jax.ShapeDtypeStruct((t, hd), BF16)],
        args=(sinks, q, kp, vp, bias), sem=("parallel", "arbitrary"), comm=comm)
    return o, got


def _attn_bwd(q, kp, vp, bias, sinks, do, col_off, pad, name, comm=None):
    t, hd = q.shape
    w = pad + TQ
    nhp = hd // LANES

    def body(sink_ref, q_ref, k_ref, v_ref, b_ref, do_ref, dq_ref, dk_ref, dv_ref, db_ref, ds_ref):
        hp, i = pl.program_id(0), pl.program_id(1)

        @pl.when(i == 0)
        def _():
            dk_ref[...] = jnp.zeros_like(dk_ref)
            dv_ref[...] = jnp.zeros_like(dv_ref)
            db_ref[...] = jnp.zeros_like(db_ref)
            ds_ref[...] = jnp.zeros_like(ds_ref)

        lo = _lo_mask()
        row8 = lax.broadcasted_iota(jnp.int32, (8, LANES), 0)
        dbias = [None, None]
        dsink = jnp.zeros((8, LANES), F32)
        for j in range(ATT_SUB):
            start = pl.multiple_of((i * ATT_SUB + j) * TQ, TQ)
            qv = q_ref[TQ * j:TQ * (j + 1), :]
            dov = do_ref[TQ * j:TQ * (j + 1), :]
            kw = k_ref[pl.ds(start, w), :]
            vw = v_ref[pl.ds(start, w), :]
            kvalid = (start + lax.broadcasted_iota(jnp.int32, (1, w), 1)) >= pad
            dqs, dkw, dvw = [], None, None
            for e in range(2):
                sel = lo if e == 0 else jnp.logical_not(lo)
                qe = jnp.where(sel, qv, jnp.zeros_like(qv))
                doe = jnp.where(sel, dov, jnp.zeros_like(dov))
                p, psink = _attn_probs(qe, kw, b_ref[e], kvalid, sink_ref[2 * hp + e])
                dp = _dot(doe, vw, 1, 1)
                delta = jnp.sum(p * dp, axis=-1, keepdims=True)
                ds = p * (dp - delta)
                dbias[e] = ds if dbias[e] is None else dbias[e] + ds
                dsink = dsink + jnp.where(row8 == e, jnp.sum(-psink * delta, axis=0, keepdims=True), 0.0)
                dsb = ds.astype(BF16)
                dqs.append(_dot(dsb, kw, 1, 0))
                dk_e = _dot(dsb, qe, 0, 0)
                dv_e = _dot(p.astype(BF16), doe, 0, 0)
                dkw = dk_e if dkw is None else dkw + dk_e
                dvw = dv_e if dvw is None else dvw + dv_e
            dq_ref[TQ * j:TQ * (j + 1), :] = jnp.where(lo, dqs[0], dqs[1])
            dk_ref[pl.ds(start, w), :] += dkw
            dv_ref[pl.ds(start, w), :] += dvw
        for e in range(2):
            db_ref[e] += dbias[e]
        ds_ref[0] += dsink

    full = pl.BlockSpec((t + pad, LANES), lambda h, i: (0, h))
    tile = pl.BlockSpec((ATT_SUB * TQ, LANES), lambda h, i: (i, h))
    btile = pl.BlockSpec((2, TQ, w), lambda h, i: (h, 0, 0))
    return _call(
        body, name=name, grid=(nhp, t // (ATT_SUB * TQ)),
        in_specs=[pl.BlockSpec(memory_space=pltpu.SMEM), tile, full, full, btile,
                  pl.BlockSpec((ATT_SUB * TQ, LANES), lambda h, i: (i, h + col_off))],
        out_specs=[tile, full, full, btile, pl.BlockSpec((1, 8, LANES), lambda h, i: (h, 0, 0))],
        out_shape=[jax.ShapeDtypeStruct((t, hd), F32), jax.ShapeDtypeStruct((t + pad, hd), F32),
                   jax.ShapeDtypeStruct((t + pad, hd), F32), jax.ShapeDtypeStruct((N_HEADS, TQ, w), F32),
                   jax.ShapeDtypeStruct((nhp, 8, LANES), F32)],
        args=(sinks, q, kp, vp, bias, do), sem=("parallel", "arbitrary"), comm=comm)


def _halo_prev(tm):
    return lambda i: jnp.maximum(i * (tm // 8) - 1, 0)


def _halo_next(tm, t):
    return lambda i: jnp.minimum((i + 1) * (tm // 8), t // 8 - 1)


def _taps_prev(tile, halo, ktaps, first):
    tm = tile.shape[0]
    ext = jnp.concatenate([jnp.where(first, 0.0, halo), tile], axis=0)
    return [tile] + [pltpu.roll(ext, s, 0)[8:8 + tm] for s in range(1, ktaps)]


def _taps_next(tile, halo, ktaps, last):
    tm = tile.shape[0]
    ext = jnp.concatenate([tile, jnp.where(last, 0.0, halo)], axis=0)
    return [tile] + [pltpu.roll(ext, tm + 8 - s, 0)[0:tm] for s in range(1, ktaps)]


def _conv_apply(taps, w_ref, ktaps):
    out = taps[0] * w_ref[ktaps - 1:ktaps, :]
    for s in range(1, ktaps):
        out = out + taps[s] * w_ref[ktaps - 1 - s:ktaps - s, :]
    return out


def _silu_grad(x):
    sg = jax.nn.sigmoid(x)
    return x * sg, sg * (1.0 + x * (1.0 - sg))


FFN_TM = 128
FFN_HALO = 16


def _ffn_mid_fwd(gu, w8, b, name):
    t = gu.shape[0]
    f = D_FF
    tm, hr = FFN_TM, FFN_HALO

    def body(g_ref, u_ref, h_ref, w_ref, b_ref, a_ref):
        first = pl.program_id(0) == 0
        ext = jnp.concatenate([jnp.where(first, 0.0, h_ref[...].astype(F32)), g_ref[...].astype(F32)], axis=0)
        taps = [ext[hr:]] + [pltpu.roll(ext, s, 0)[hr:] for s in (1, 2)]
        gc = _conv_apply(taps, w_ref, 3) + b_ref[...]
        a_ref[...] = (gc * jax.nn.sigmoid(gc) * u_ref[...].astype(F32)).astype(BF16)

    return pl.pallas_call(
        body, name=name, grid=(t // tm,),
        in_specs=[pl.BlockSpec((tm, f), lambda i: (i, 0)), pl.BlockSpec((tm, f), lambda i: (i, 1)),
                  pl.BlockSpec((hr, f), lambda i: (jnp.maximum(i * (tm // hr) - 1, 0), 0)),
                  pl.BlockSpec((8, f), lambda i: (0, 0)), pl.BlockSpec((1, f), lambda i: (0, 0))],
        out_specs=pl.BlockSpec((tm, f), lambda i: (i, 0)), out_shape=jax.ShapeDtypeStruct((t, f), BF16),
        compiler_params=_cp(("parallel",)),
    )(gu, gu, gu, w8, b)


FFN_BT = 256
FFN_BC = 1408


def _ffn_mid_bwd(gu, dxb, w_out_t, w8, b, name):
    t, d = dxb.shape
    f = D_FF
    tm, hr = FFN_BT, FFN_HALO
    nt = t // tm
    n = tm + hr

    def body(g_ref, u_ref, gp_ref, gn_ref, un_ref, dx_ref, dxn_ref, wo_ref, w_ref, b_ref, dg_ref, du_ref, dw_ref, db_ref):
        i = pl.program_id(0)
        first, last = i == 0, i == nt - 1

        @pl.when(first)
        def _():
            dw_ref[...] = jnp.zeros_like(dw_ref)
            db_ref[...] = jnp.zeros_like(db_ref)

        dxe = jnp.concatenate([dx_ref[...], dxn_ref[...]], axis=0)
        row = lax.broadcasted_iota(jnp.int32, (n, 1), 0)
        keep = (row < tm) | jnp.logical_not(last)
        for c in range(0, f, FFN_BC):
            cs = slice(c, c + FFN_BC)
            ext = jnp.concatenate([jnp.where(first, 0.0, gp_ref[:, cs].astype(F32)), g_ref[:, cs].astype(F32),
                                   gn_ref[:, cs].astype(F32)], axis=0)
            taps = [ext[hr:]] + [pltpu.roll(ext, s, 0)[hr:] for s in (1, 2)]
            gc = b_ref[:, cs] + taps[0] * w_ref[2:3, cs] + taps[1] * w_ref[1:2, cs] + taps[2] * w_ref[0:1, cs]
            act, dact = _silu_grad(gc)
            da = _dot(dxe, wo_ref[:, cs], 1, 0)
            up = jnp.concatenate([u_ref[:, cs], un_ref[:, cs]], axis=0).astype(F32)
            dgc = jnp.where(keep, da * up * dact, 0.0)
            du_ref[:, cs] = (da[:tm] * act[:tm]).astype(BF16)
            dg_ref[:, cs] = (dgc[:tm] * w_ref[2:3, cs] + pltpu.roll(dgc, n - 1, 0)[:tm] * w_ref[1:2, cs]
                             + pltpu.roll(dgc, n - 2, 0)[:tm] * w_ref[0:1, cs]).astype(BF16)
            db_ref[:, cs] += jnp.sum(dgc[:tm], axis=0, keepdims=True)
            for s in range(3):
                dw_ref[2 - s:3 - s, cs] += jnp.sum(dgc[:tm] * taps[s][:tm], axis=0, keepdims=True)

    r = tm // hr
    prev = lambda i: jnp.maximum(i * r - 1, 0)
    nxt_blk = lambda i: jnp.minimum((i + 1) * r, t // hr - 1)
    row_f = pl.BlockSpec((tm, f), lambda i: (i, 0))
    return pl.pallas_call(
        body, name=name, grid=(nt,),
        in_specs=[row_f, pl.BlockSpec((tm, f), lambda i: (i, 1)),
                  pl.BlockSpec((hr, f), lambda i: (prev(i), 0)), pl.BlockSpec((hr, f), lambda i: (nxt_blk(i), 0)),
                  pl.BlockSpec((hr, f), lambda i: (nxt_blk(i), 1)),
                  pl.BlockSpec((tm, d), lambda i: (i, 0)), pl.BlockSpec((hr, d), lambda i: (nxt_blk(i), 0)),
                  pl.BlockSpec((d, f), lambda i: (0, 0)),
                  pl.BlockSpec((8, f), lambda i: (0, 0)), pl.BlockSpec((1, f), lambda i: (0, 0))],
        out_specs=[row_f, row_f, pl.BlockSpec((8, f), lambda i: (0, 0)), pl.BlockSpec((1, f), lambda i: (0, 0))],
        out_shape=[jax.ShapeDtypeStruct((t, f), BF16), jax.ShapeDtypeStruct((t, f), BF16),
                   jax.ShapeDtypeStruct((8, f), F32), jax.ShapeDtypeStruct((1, f), F32)],
        compiler_params=_cp(("arbitrary",)),
    )(gu, gu, gu, gu, gu, dxb, dxb, w_out_t, w8, b)


def _conv_bwd_data(dc, w8, ktaps, name):
    t, c = dc.shape
    tm = 256
    tc = _pick(c, (1408, 1024))
    nt = t // tm

    def body(d_ref, h_ref, w_ref, o_ref):
        last = pl.program_id(0) == nt - 1
        o_ref[...] = _conv_apply(_taps_next(d_ref[...], h_ref[...], ktaps, last), w_ref, ktaps).astype(BF16)

    hn = _halo_next(tm, t)
    return pl.pallas_call(
        body, name=name, grid=(nt, c // tc),
        in_specs=[pl.BlockSpec((tm, tc), lambda i, j: (i, j)), pl.BlockSpec((8, tc), lambda i, j: (hn(i), j)),
                  pl.BlockSpec((8, tc), lambda i, j: (0, j))],
        out_specs=pl.BlockSpec((tm, tc), lambda i, j: (i, j)), out_shape=jax.ShapeDtypeStruct((t, c), BF16),
        compiler_params=_cp(("parallel", "parallel")),
    )(dc, dc, w8)


PRE_TM = 256
PRE_TC = 1024


def _ssm_pre_fwd(zx, w8, b, name):
    t = zx.shape[0]
    tm, tc = PRE_TM, PRE_TC
    off = D_INNER // tc

    def body(x_ref, h_ref, w_ref, b_ref, o_ref):
        first = pl.program_id(0) == 0
        c = _conv_apply(_taps_prev(x_ref[...], h_ref[...], 4, first), w_ref, 4) + b_ref[...]
        o_ref[...] = c * jax.nn.sigmoid(c)

    hp = _halo_prev(tm)
    return pl.pallas_call(
        body, name=name, grid=(t // tm, XBC // tc),
        in_specs=[pl.BlockSpec((tm, tc), lambda i, j: (i, j + off)), pl.BlockSpec((8, tc), lambda i, j: (hp(i), j + off)),
                  pl.BlockSpec((8, tc), lambda i, j: (0, j)), pl.BlockSpec((1, tc), lambda i, j: (0, j))],
        out_specs=pl.BlockSpec((tm, tc), lambda i, j: (i, j)), out_shape=jax.ShapeDtypeStruct((t, XBC), F32),
        compiler_params=_cp(("parallel", "parallel")),
    )(zx, zx, w8, b)


def _ssm_pre_bwd1(zx, dxbc, w8, b, name):
    t = zx.shape[0]
    tm, tc = PRE_TM, PRE_TC
    off = D_INNER // tc

    def body(x_ref, h_ref, d_ref, w_ref, b_ref, dc_ref, dw_ref, db_ref):
        i = pl.program_id(1)

        @pl.when(i == 0)
        def _():
            dw_ref[...] = jnp.zeros_like(dw_ref)
            db_ref[...] = jnp.zeros_like(db_ref)

        taps = _taps_prev(x_ref[...], h_ref[...], 4, i == 0)
        c = _conv_apply(taps, w_ref, 4) + b_ref[...]
        _, dact = _silu_grad(c)
        dc = d_ref[...] * dact
        dc_ref[...] = dc
        db_ref[...] += jnp.sum(dc, axis=0, keepdims=True)
        for s in range(4):
            dw_ref[3 - s:4 - s, :] += jnp.sum(dc * taps[s], axis=0, keepdims=True)

    hp = _halo_prev(tm)
    return pl.pallas_call(
        body, name=name, grid=(XBC // tc, t // tm),
        in_specs=[pl.BlockSpec((tm, tc), lambda j, i: (i, j + off)), pl.BlockSpec((8, tc), lambda j, i: (hp(i), j + off)),
                  pl.BlockSpec((tm, tc), lambda j, i: (i, j)),
                  pl.BlockSpec((8, tc), lambda j, i: (0, j)), pl.BlockSpec((1, tc), lambda j, i: (0, j))],
        out_specs=[pl.BlockSpec((tm, tc), lambda j, i: (i, j)), pl.BlockSpec((8, tc), lambda j, i: (0, j)),
                   pl.BlockSpec((1, tc), lambda j, i: (0, j))],
        out_shape=[jax.ShapeDtypeStruct((t, XBC), F32), jax.ShapeDtypeStruct((8, XBC), F32),
                   jax.ShapeDtypeStruct((1, XBC), F32)],
        compiler_params=_cp(("parallel", "arbitrary")),
    )(zx, zx, dxbc, w8, b)


def _head_lanes():
    return lax.broadcasted_iota(jnp.int32, (1, LANES), 1) < SSM_HEADS


def _dt_fwd(dtraw, bias, name):
    t = dtraw.shape[0]
    tm = _pick(t, (1024, 512, 256, 128))

    def body(x_ref, b_ref, o_ref):
        v = x_ref[...] + b_ref[...]
        sp = jnp.maximum(v, 0.0) + jnp.log(1.0 + jnp.exp(-jnp.abs(v)))
        o_ref[...] = jnp.where(_head_lanes(), sp, 0.0)

    row = pl.BlockSpec((tm, LANES), lambda i: (i, 0))
    return pl.pallas_call(
        body, name=name, grid=(t // tm,), in_specs=[row, pl.BlockSpec((1, LANES), lambda i: (0, 0))], out_specs=row,
        out_shape=jax.ShapeDtypeStruct((t, LANES), F32), compiler_params=_cp(("parallel",)),
    )(dtraw, bias)


def _dt_bwd(dtraw, bias, ddt, name):
    t = dtraw.shape[0]
    tm = _pick(t, (1024, 512, 256, 128))

    def body(x_ref, b_ref, d_ref, o_ref, db_ref):
        @pl.when(pl.program_id(0) == 0)
        def _():
            db_ref[...] = jnp.zeros_like(db_ref)

        g = jnp.where(_head_lanes(), d_ref[...] * jax.nn.sigmoid(x_ref[...] + b_ref[...]), 0.0)
        o_ref[...] = g.astype(BF16)
        db_ref[...] += jnp.sum(g, axis=0, keepdims=True)

    row = pl.BlockSpec((tm, LANES), lambda i: (i, 0))
    vec = pl.BlockSpec((1, LANES), lambda i: (0, 0))
    return pl.pallas_call(
        body, name=name, grid=(t // tm,), in_specs=[row, vec, row], out_specs=[row, vec],
        out_shape=[jax.ShapeDtypeStruct((t, LANES), BF16), jax.ShapeDtypeStruct((1, LANES), F32)],
        compiler_params=_cp(("arbitrary",)),
    )(dtraw, bias, ddt)


GROUP_W = D_INNER // SSM_GROUPS
POST_TM = 512


def _ssm_post_fwd(y, xbc, zx, dexp, nw, name):
    t = y.shape[0]
    tm = _pick(t, (POST_TM, 256, 128))

    def body(y_ref, x_ref, z_ref, d_ref, w_ref, o_ref):
        zv = z_ref[...]
        y3 = (y_ref[...] + d_ref[...] * x_ref[...]) * (zv * jax.nn.sigmoid(zv))
        r = lax.rsqrt(jnp.mean(y3 * y3, axis=-1, keepdims=True) + EPS)
        o_ref[...] = (y3 * r * w_ref[...]).astype(BF16)

    blk = pl.BlockSpec((tm, GROUP_W), lambda i, g: (i, g))
    vec = pl.BlockSpec((1, GROUP_W), lambda i, g: (0, g))
    return pl.pallas_call(
        body, name=name, grid=(t // tm, SSM_GROUPS), in_specs=[blk, blk, blk, vec, vec], out_specs=blk,
        out_shape=jax.ShapeDtypeStruct((t, D_INNER), BF16), compiler_params=_cp(("parallel", "parallel")),
    )(y, xbc, zx, dexp, nw)


def _ssm_post_bwd(dy4, y, xbc, zx, dexp, nw, name):
    t = y.shape[0]
    tm = _pick(t, (POST_TM, 256, 128))

    def body(g_ref, y_ref, x_ref, z_ref, d_ref, w_ref, dy_ref, dxs_ref, dz_ref, dd_ref, dw_ref):
        @pl.when(pl.program_id(1) == 0)
        def _():
            dd_ref[...] = jnp.zeros_like(dd_ref)
            dw_ref[...] = jnp.zeros_like(dw_ref)

        zv = z_ref[...]
        xv = x_ref[...]
        act, dact = _silu_grad(zv)
        y2 = y_ref[...] + d_ref[...] * xv
        y3 = y2 * act
        r = lax.rsqrt(jnp.mean(y3 * y3, axis=-1, keepdims=True) + EPS)
        y3n = y3 * r
        gv = g_ref[...]
        dyn = gv * w_ref[...]
        dy3 = r * (dyn - y3n * jnp.mean(dyn * y3n, axis=-1, keepdims=True))
        dy2 = dy3 * act
        dy_ref[...] = dy2
        dxs_ref[...] = dy2 * d_ref[...]
        dz_ref[...] = (dy3 * y2 * dact).astype(BF16)
        dd_ref[...] += jnp.sum(dy2 * xv, axis=0, keepdims=True)
        dw_ref[...] += jnp.sum(gv * y3n, axis=0, keepdims=True)

    blk = pl.BlockSpec((tm, GROUP_W), lambda g, i: (i, g))
    vec = pl.BlockSpec((1, GROUP_W), lambda g, i: (0, g))
    return pl.pallas_call(
        body, name=name, grid=(SSM_GROUPS, t // tm), in_specs=[blk, blk, blk, blk, vec, vec],
        out_specs=[blk, blk, blk, vec, vec],
        out_shape=[jax.ShapeDtypeStruct((t, D_INNER), F32), jax.ShapeDtypeStruct((t, D_INNER), F32),
                   jax.ShapeDtypeStruct((t, D_INNER), BF16), jax.ShapeDtypeStruct((1, D_INNER), F32),
                   jax.ShapeDtypeStruct((1, D_INNER), F32)],
        compiler_params=_cp(("parallel", "arbitrary")),
    )(dy4, y, xbc, zx, dexp, nw)


def _ssd_common(dt, alog):
    ll = dt.shape[0]
    a_neg = -jnp.exp(alog)
    a = dt * a_neg
    ri = lax.broadcasted_iota(jnp.int32, (ll, ll), 0)
    ci = lax.broadcasted_iota(jnp.int32, (ll, ll), 1)
    tril = ri >= ci
    acs = _dot(tril.astype(F32), a, 1, 0, HI)
    return a_neg, tril, acs, acs.T


def _pair_terms(acs, acs_t, dt, h0, lo):
    ll = acs.shape[0]
    cols = [acs[:, h0 + e:h0 + e + 1] for e in range(2)]
    rows = [acs_t[h0 + e:h0 + e + 1, :] for e in range(2)]
    dtc = [dt[:, h0 + e:h0 + e + 1] for e in range(2)]
    lasts = [c[ll - 1:ll, :] for c in cols]
    dtx = jnp.where(lo, dtc[0], dtc[1])
    eac = jnp.where(lo, jnp.exp(cols[0]), jnp.exp(cols[1]))
    fdec = jnp.where(lo, jnp.exp(lasts[0] - cols[0]), jnp.exp(lasts[1] - cols[1]))
    elast = jnp.where(lo, jnp.exp(lasts[0]), jnp.exp(lasts[1]))
    return cols, rows, dtx, eac, fdec, elast


def _decay(col, row, tril):
    return jnp.where(tril, jnp.exp(jnp.minimum(col - row, 0.0)), 0.0)


def _two_heads_rows(v, lo):
    z = jnp.zeros_like(v)
    return jnp.concatenate([jnp.where(lo, v, z), jnp.where(lo, z, v)], axis=0)


def _two_heads_cols(ms):
    return jnp.concatenate(ms, axis=1)


def _ssd_fwd(xbc, dt, alog, name, comm=None):
    t = xbc.shape[0]
    ll = SSD_L
    nc = t // ll

    def body(x_ref, dt_ref, al_ref, y_ref, sp_ref, st_ref):
        @pl.when(pl.program_id(0) == 0)
        def _():
            st_ref[...] = jnp.zeros_like(st_ref)

        dtv = dt_ref[...]
        _, tril, acs, acs_t = _ssd_common(dtv, al_ref[...])
        lo = _lo_mask()
        sp_ref[0] = st_ref[...]
        for g in range(SSM_GROUPS):
            bg = x_ref[:, D_INNER + SSM_STATE * g:D_INNER + SSM_STATE * (g + 1)].astype(BF16)
            cg = x_ref[:, D_INNER + 512 + SSM_STATE * g:D_INNER + 512 + SSM_STATE * (g + 1)].astype(BF16)
            gm = _dot(cg, bg, 1, 1)
            g0 = GROUP_W * g
            terms = [_pair_terms(acs, acs_t, dtv, 8 * g + 2 * pp, lo) for pp in range(4)]
            dtx, eac, fdec, elast = [jnp.concatenate([tt[k] for tt in terms], axis=1) for k in (2, 3, 4, 5)]
            xg = x_ref[:, g0:g0 + GROUP_W]
            ug = (xg * dtx).astype(BF16)
            sg = st_ref[:, g0:g0 + GROUP_W]
            yst = _dot(cg, sg.astype(BF16), 1, 0) * eac
            st_ref[:, g0:g0 + GROUP_W] = sg * elast + _dot(bg, (xg * (fdec * dtx)).astype(BF16), 0, 0)
            for pp in range(4):
                cols, rows = terms[pp][0], terms[pp][1]
                sl = slice(LANES * pp, LANES * (pp + 1))
                y_in = _dot(_two_heads_cols([(gm * _decay(cols[e], rows[e], tril)).astype(BF16) for e in range(2)]),
                            _two_heads_rows(ug[:, sl], lo), 1, 0)
                y_ref[:, g0 + LANES * pp:g0 + LANES * (pp + 1)] = y_in + yst[:, sl]

    return _call(
        body, name=name, grid=(nc,),
        in_specs=[pl.BlockSpec((ll, XBC), lambda c: (c, 0)), pl.BlockSpec((ll, LANES), lambda c: (c, 0)),
                  pl.BlockSpec((1, LANES), lambda c: (0, 0))],
        out_specs=[pl.BlockSpec((ll, D_INNER), lambda c: (c, 0)), pl.BlockSpec((1, SSM_STATE, D_INNER), lambda c: (c, 0, 0))],
        out_shape=[jax.ShapeDtypeStruct((t, D_INNER), F32), jax.ShapeDtypeStruct((nc, SSM_STATE, D_INNER), F32)],
        scratch_shapes=[pltpu.VMEM((SSM_STATE, D_INNER), F32)],
        args=(xbc, dt, alog), sem=("arbitrary",), comm=comm)


def _ssd_bwd(xbc, dt, alog, sprev, dy, dskip, name, comm=None):
    t = xbc.shape[0]
    ll = SSD_L
    nc = t // ll

    def body(x_ref, dt_ref, al_ref, sp_ref, dy_ref, dk_ref, dx_ref, ddt_ref, dal_ref, ds_ref, colt_ref):
        @pl.when(pl.program_id(0) == 0)
        def _():
            ds_ref[...] = jnp.zeros_like(ds_ref)
            dal_ref[...] = jnp.zeros_like(dal_ref)

        dtv = dt_ref[...]
        a_neg, tril, acs, acs_t = _ssd_common(dtv, al_ref[...])
        lo = _lo_mask()
        hi = jnp.logical_not(lo)
        lane = lax.broadcasted_iota(jnp.int32, (1, LANES), 1)
        colt_ref[...] = jnp.zeros_like(colt_ref)
        rowterm = jnp.zeros((ll, LANES), F32)
        ddt_u = jnp.zeros((ll, LANES), F32)
        dlast = jnp.zeros((1, LANES), F32)

        def halves(v):
            return (jnp.sum(jnp.where(lo, v, 0.0), axis=-1, keepdims=True),
                    jnp.sum(jnp.where(hi, v, 0.0), axis=-1, keepdims=True))

        for g in range(SSM_GROUPS):
            cb0 = D_INNER + SSM_STATE * g
            cc0 = D_INNER + 512 + SSM_STATE * g
            bg = x_ref[:, cb0:cb0 + SSM_STATE].astype(BF16)
            cg = x_ref[:, cc0:cc0 + SSM_STATE].astype(BF16)
            gm = _dot(cg, bg, 1, 1)
            g0 = GROUP_W * g
            terms = [_pair_terms(acs, acs_t, dtv, 8 * g + 2 * pp, lo) for pp in range(4)]
            dtx, eac, fdec, elast = [jnp.concatenate([tt[k] for tt in terms], axis=1) for k in (2, 3, 4, 5)]
            xg = x_ref[:, g0:g0 + GROUP_W]
            u32 = xg * dtx
            ug = u32.astype(BF16)
            dyg = dy_ref[:, g0:g0 + GROUP_W]
            dyb = dyg.astype(BF16)
            spg = sp_ref[0, :, g0:g0 + GROUP_W]
            spb = spg.astype(BF16)
            dsg = ds_ref[:, g0:g0 + GROUP_W]
            dsb = dsg.astype(BF16)
            du_st = _dot(bg, dsb, 1, 0) * fdec
            yst = _dot(cg, spb, 1, 0) * eac
            dye = (dyg * eac).astype(BF16)
            dc_st = _dot(dye, spb, 1, 1)
            db_st = _dot((xg * (fdec * dtx)).astype(BF16), dsb, 1, 1)
            ds_ref[:, g0:g0 + GROUP_W] = dsg * elast + _dot(cg, dye, 0, 0)
            qst_el = du_st * u32
            rq_el = dyg * yst - qst_el
            q_row = jnp.sum(qst_el, axis=0, keepdims=True)
            s_row = jnp.sum(dsg * spg, axis=0, keepdims=True)
            dgm = jnp.zeros((ll, ll), F32)
            for pp in range(4):
                h0 = 8 * g + 2 * pp
                cols, rows = terms[pp][0], terms[pp][1]
                sl = slice(LANES * pp, LANES * (pp + 1))
                decs = [_decay(cols[e], rows[e], tril) for e in range(2)]
                wms = [gm * d for d in decs]
                dum2 = _dot(dyb[:, sl], _two_heads_rows(ug[:, sl], lo), 1, 1)
                du = _dot(jnp.concatenate([wm.astype(BF16) for wm in wms], axis=0),
                          _two_heads_rows(dyb[:, sl], lo), 0, 0) + du_st[:, sl]
                dx_ref[:, g0 + LANES * pp:g0 + LANES * (pp + 1)] = du * dtx[:, sl] + dk_ref[:, g0 + LANES * pp:g0 + LANES * (pp + 1)]
                ddtu = halves(du * xg[:, sl])
                rq = halves(rq_el[:, sl])
                qs = halves(q_row[:, sl])
                ss = halves(s_row[:, sl])
                for e in range(2):
                    dum = dum2[:, ll * e:ll * (e + 1)]
                    dgm = dgm + dum * decs[e]
                    tm_ = dum * wms[e]
                    oh = lane == (h0 + e)
                    rowterm = rowterm + jnp.where(oh, jnp.sum(tm_, axis=1, keepdims=True) + rq[e], 0.0)
                    ddt_u = ddt_u + jnp.where(oh, ddtu[e], 0.0)
                    dlast = dlast + jnp.where(oh, jnp.exp(cols[e][ll - 1:ll, :]) * ss[e] + qs[e], 0.0)
                    colt_ref[h0 + e:h0 + e + 1, :] = jnp.sum(tm_, axis=0, keepdims=True)
            dgb = dgm.astype(BF16)
            dx_ref[:, cc0:cc0 + SSM_STATE] = _dot(dgb, bg, 1, 0) + dc_st
            dx_ref[:, cb0:cb0 + SSM_STATE] = _dot(dgb, cg, 0, 0) + db_st
        row_io = lax.broadcasted_iota(jnp.int32, (ll, LANES), 0)
        dacs = rowterm - colt_ref[...].T + jnp.where(row_io == ll - 1, dlast, 0.0)
        da = _dot(jnp.logical_not(tril).astype(F32) + jnp.where(
            lax.broadcasted_iota(jnp.int32, (ll, ll), 0) == lax.broadcasted_iota(jnp.int32, (ll, ll), 1), 1.0, 0.0),
            dacs, 1, 0, HI)
        ddt_ref[...] = da * a_neg + ddt_u
        dal_ref[...] += jnp.sum(da * dtv, axis=0, keepdims=True) * a_neg

    rev = lambda c: nc - 1 - c
    return _call(
        body, name=name, grid=(nc,),
        in_specs=[pl.BlockSpec((ll, XBC), lambda c: (rev(c), 0)), pl.BlockSpec((ll, LANES), lambda c: (rev(c), 0)),
                  pl.BlockSpec((1, LANES), lambda c: (0, 0)),
                  pl.BlockSpec((1, SSM_STATE, D_INNER), lambda c: (rev(c), 0, 0)),
                  pl.BlockSpec((ll, D_INNER), lambda c: (rev(c), 0)), pl.BlockSpec((ll, D_INNER), lambda c: (rev(c), 0))],
        out_specs=[pl.BlockSpec((ll, XBC), lambda c: (rev(c), 0)), pl.BlockSpec((ll, LANES), lambda c: (rev(c), 0)),
                   pl.BlockSpec((1, LANES), lambda c: (0, 0))],
        out_shape=[jax.ShapeDtypeStruct((t, XBC), F32), jax.ShapeDtypeStruct((t, LANES), F32),
                   jax.ShapeDtypeStruct((1, LANES), F32)],
        scratch_shapes=[pltpu.VMEM((SSM_STATE, D_INNER), F32), pltpu.VMEM((LANES, ll), F32)],
        args=(xbc, dt, alog, sprev, dy, dskip), sem=("arbitrary",), comm=comm)


def _loss(y, target, name):
    t, d = y.shape
    tm = _pick(t, (512, 256, 128))

    def body(y_ref, t_ref, dy_ref, dyb_ref, acc_ref):
        @pl.when(pl.program_id(0) == 0)
        def _():
            acc_ref[...] = jnp.zeros_like(acc_ref)

        err = y_ref[...] - t_ref[...]
        dy = err * (1.0 / d)
        dy_ref[...] = dy
        dyb_ref[...] = dy.astype(BF16)
        acc_ref[...] += jnp.sum(err * err, axis=0, keepdims=True)

    row = pl.BlockSpec((tm, d), lambda i: (i, 0))
    vec = pl.BlockSpec((1, d), lambda i: (0, 0))
    return pl.pallas_call(
        body, name=name, grid=(t // tm,), in_specs=[row, row], out_specs=[row, row, vec],
        out_shape=[jax.ShapeDtypeStruct((t, d), F32), jax.ShapeDtypeStruct((t, d), BF16), jax.ShapeDtypeStruct((1, d), F32)],
        compiler_params=_cp(("arbitrary",)),
    )(y, target)


ADAM_TR = 512


def _adamw(parts, w, m, v, tr, name, comm=None):
    r, c = w.shape
    c1 = 1.0 - ADAM_B1 ** ADAM_STEP
    c2 = 1.0 - ADAM_B2 ** ADAM_STEP

    def body(p_ref, w_ref, m_ref, v_ref, g_ref, d_ref, mo_ref, vo_ref):
        g = p_ref[0].astype(F32)
        for k in range(1, N_DEV):
            g = g + p_ref[k].astype(F32)
        mn = ADAM_B1 * m_ref[...] + (1.0 - ADAM_B1) * g
        vn = ADAM_B2 * v_ref[...] + (1.0 - ADAM_B2) * (g * g)
        g_ref[...] = g
        mo_ref[...] = mn
        vo_ref[...] = vn
        d_ref[...] = -ADAM_LR * ((mn / c1) / (jnp.sqrt(vn / c2) + ADAM_EPS) + ADAM_WD * w_ref[...])

    row = pl.BlockSpec((tr, c), lambda i: (i, 0))
    sd = jax.ShapeDtypeStruct((r, c), F32)
    return _call(
        body, name=name, grid=(r // tr,),
        in_specs=[pl.BlockSpec((N_DEV, tr, c), lambda i: (0, i, 0)), row, row, row],
        out_specs=[row, row, row, row], out_shape=[sd, sd, sd, sd], args=(parts, w, m, v), sem=("parallel",), comm=comm)


def _peers():
    mx, my, mc = lax.axis_index("x"), lax.axis_index("y"), lax.axis_index("c")
    me = 4 * mx + 2 * my + mc
    out = []
    for k in range(1, N_DEV):
        px = 1 - mx if k & 4 else mx
        py = 1 - my if k & 2 else my
        pc = 1 - mc if k & 1 else mc
        out.append(((px, py, pc), 4 * px + 2 * py + pc))
    return me, out


class _Comm:
    def __init__(self, arrs, scatters):
        self.arrs, self.scatters, self.n = list(arrs), list(scatters), len(arrs)
        self.specs = [pl.BlockSpec(memory_space=pl.ANY)] * self.n
        self.out_shape = [jax.ShapeDtypeStruct(x.shape if sc else (N_DEV,) + x.shape, x.dtype)
                          for x, sc in zip(self.arrs, self.scatters)]
        np_ = N_DEV - 1
        self.scratch = [pltpu.SemaphoreType.DMA((np_ * self.n,)), pltpu.SemaphoreType.DMA((np_ * self.n,)),
                        pltpu.SemaphoreType.DMA((self.n,))]

    def _copies(self, x_refs, o_refs, sems):
        send_sems, recv_sems, local_sems = sems
        me, peers = _peers()
        np_ = N_DEV - 1
        local, sends, recvs = [], [], []
        for a in range(self.n):
            mine = x_refs[a].at[me] if self.scatters[a] else x_refs[a]
            local.append(pltpu.make_async_copy(mine, o_refs[a].at[me], local_sems.at[a]))
        for k, (dev, idx) in enumerate(peers):
            for a in range(self.n):
                mine = x_refs[a].at[me] if self.scatters[a] else x_refs[a]
                sends.append(pltpu.make_async_remote_copy(
                    src_ref=x_refs[a].at[idx] if self.scatters[a] else x_refs[a], dst_ref=o_refs[a].at[me],
                    send_sem=send_sems.at[a * np_ + k], recv_sem=recv_sems.at[a * np_ + k], device_id=dev, device_id_type=MESH))
                recvs.append(pltpu.make_async_remote_copy(
                    src_ref=mine, dst_ref=o_refs[a].at[idx], send_sem=send_sems.at[a * np_ + k],
                    recv_sem=recv_sems.at[a * np_ + k], device_id=dev, device_id_type=MESH))
        return local, sends, recvs

    def start(self, x_refs, o_refs, sems):
        local, sends, _ = self._copies(x_refs, o_refs, sems)
        for cp in local + sends:
            cp.start()

    def wait(self, x_refs, o_refs, sems):
        local, sends, recvs = self._copies(x_refs, o_refs, sems)
        for cp in recvs:
            cp.wait_recv()
        for cp in sends:
            cp.wait_send()
        for cp in local:
            cp.wait()


def _call(body, *, name, grid, in_specs, out_specs, out_shape, args, scratch_shapes=(), sem=None, comm=None):
    if comm is None:
        outs = pl.pallas_call(
            body, name=name, grid=grid, in_specs=list(in_specs), out_specs=list(out_specs), out_shape=list(out_shape),
            scratch_shapes=list(scratch_shapes), compiler_params=_cp(sem),
        )(*args)
        return list(outs), []
    n_in, n_out, nc = len(in_specs), len(out_specs), comm.n
    nsteps = 1
    for g in grid:
        nsteps *= g

    def carrier(*refs):
        ins, cin = refs[:n_in], refs[n_in:n_in + nc]
        outs, cout = refs[n_in + nc:n_in + nc + n_out], refs[n_in + nc + n_out:n_in + 2 * nc + n_out]
        rest = refs[n_in + 2 * nc + n_out:]
        scratch, sems = rest[:len(rest) - 3], rest[len(rest) - 3:]
        if nsteps == 1:
            comm.start(cin, cout, sems)
            body(*ins, *outs, *scratch)
            comm.wait(cin, cout, sems)
            return
        step = 0
        for d, g in enumerate(grid):
            step = step * g + pl.program_id(d)

        @pl.when(step == 0)
        def _():
            comm.start(cin, cout, sems)

        body(*ins, *outs, *scratch)

        @pl.when(step == nsteps - 1)
        def _():
            comm.wait(cin, cout, sems)

    outs = pl.pallas_call(
        carrier, name=name, grid=grid, in_specs=list(in_specs) + comm.specs, out_specs=list(out_specs) + comm.specs,
        out_shape=list(out_shape) + comm.out_shape, scratch_shapes=list(scratch_shapes) + comm.scratch,
        compiler_params=_cp(("arbitrary",) * len(grid) if grid else None),
    )(*args, *comm.arrs)
    return list(outs[:n_out]), list(outs[n_out:])


def _exchange(arrs, scatters, name):
    return _call(lambda *refs: None, name=name, grid=(), in_specs=[], out_specs=[], out_shape=[], args=[],
                 comm=_Comm(arrs, scatters))[1]


def _pack(arrs, dtype, lead=()):
    nl = len(lead)
    flat = jnp.concatenate([a.astype(dtype).reshape(lead + (-1,)) for a in arrs], axis=nl)
    n = flat.shape[-1]
    rows = -(-n // (LANES * ADAM_TR)) * ADAM_TR
    flat = jnp.pad(flat, [(0, 0)] * nl + [(0, rows * LANES - n)])
    return flat.reshape(lead + (rows, LANES))


def _unpack(flat, shapes, lead=()):
    nl = len(lead)
    flat = flat.reshape(lead + (-1,))
    out, o = [], 0
    for s in shapes:
        n = 1
        for d in s:
            n *= d
        out.append(lax.slice_in_dim(flat, o, o + n, axis=nl).reshape(lead + tuple(s)))
        o += n
    return out


def _join(g, ax):
    return jnp.concatenate([g[d] for d in range(N_DEV)], axis=ax)


def _split(full, ax):
    n = full.shape[ax] // N_DEV
    return jnp.stack([lax.slice_in_dim(full, d * n, (d + 1) * n, axis=ax) for d in range(N_DEV)])


_WEIGHTS = ['norm_mix', 'norm_ffn', 'attn_w_in', 'attn_w_out', 'relpos_table', 'q_norm_a', 'k_norm_a', 'q_norm_b',
            'k_norm_b', 'sinks', 'ssm_w_in', 'ssm_conv_w', 'ssm_conv_b', 'ssm_dt_bias', 'ssm_a_log', 'ssm_d', 'ssm_norm',
            'ssm_w_out', 'ffn_w_in', 'ffn_conv_w', 'ffn_conv_b', 'ffn_w_out']
_SHARD_AX = {'attn_w_in': 2, 'attn_w_out': 1, 'ssm_w_in': 2, 'ssm_conv_w': 2, 'ssm_conv_b': 1, 'ssm_norm': 1,
             'ssm_w_out': 1, 'ffn_w_in': 2, 'ffn_conv_w': 2, 'ffn_w_out': 1}
_BIG = ['attn_w_in', 'attn_w_out', 'ssm_w_in', 'ssm_w_out', 'ffn_w_in', 'ffn_w_out']
_SMALL = ['ssm_conv_w', 'ssm_conv_b', 'ssm_norm', 'ffn_conv_w']
_AX2 = {n: _SHARD_AX[n] - 1 for n in _BIG}
_REPL = [n for n in _WEIGHTS if n not in _SHARD_AX]


def _rows8(w):
    return jnp.pad(w, ((0, 8 - w.shape[0]), (0, 0)))


def _lanes128(v):
    return jnp.pad(v, (0, LANES - v.shape[0])).reshape(1, LANES)


def _band_mask(n_prev, pad):
    cq = jnp.arange(TQ)[:, None] // CHUNK
    ck = jnp.arange(pad + TQ)[None, :] // CHUNK
    return (ck >= cq) & (ck <= cq + n_prev)


def _pad_rows(a, pad):
    return jnp.pad(a, ((pad, 0), (0, 0)))


def _kv_expand(a):
    return jnp.concatenate([a[:, :HEAD_DIM]] * 4 + [a[:, HEAD_DIM:]] * 4, axis=1)


def _ffn_fwd(xin, g, w_in, w8, cb, w_out, tag):
    h = _rms_fwd(xin, g, f"rms_ffn{tag}")
    gu = _mm(h, w_in, f"mm_ffn_in{tag}", out_dtype=BF16)
    a = _ffn_mid_fwd(gu, w8, cb, f"ffn_mid{tag}")
    xout = _mm(a, w_out, f"mm_ffn_out{tag}", res=xin)
    return xout, (h, gu, a)


def _ffn_bwd(dx, dxb, xin, g, w_in_t, w8, cb, w_out_t, saved, tag):
    h, gu, a = saved
    dw_out = _mm(a, dxb, f"mm_ffn_dwout{tag}", trans_a=True)
    dgate, dup, dw8, dcb = _ffn_mid_bwd(gu, dxb, w_out_t, w8, cb, f"ffn_mid_bwd{tag}")
    dh = _mm(dgate, w_in_t[:D_FF], f"mm_ffn_dh_g{tag}")
    dh = _mm(dup, w_in_t[D_FF:], f"mm_ffn_dh_u{tag}", res=dh)
    dw_in = jnp.concatenate([_mm(h, dgate, f"mm_ffn_dwin_g{tag}", trans_a=True),
                             _mm(h, dup, f"mm_ffn_dwin_u{tag}", trans_a=True)], axis=1)
    dxp, dxpb, dg = _rms_bwd(xin, g, dh, dx, f"rms_ffn_bwd{tag}")
    return dxp, dxpb, dg, dw_in, dw8[:3], dcb, dw_out


def kernel(x, norm_mix, norm_ffn, attn_w_in, attn_w_out, relpos_table, q_norm_a, k_norm_a, q_norm_b, k_norm_b, sinks, ssm_w_in, ssm_conv_w, ssm_conv_b, ssm_dt_bias, ssm_a_log, ssm_d, ssm_norm, ssm_w_out, ffn_w_in, ffn_conv_w, ffn_conv_b, ffn_w_out, loss_target, m_norm_mix, m_norm_ffn, m_attn_w_in, m_attn_w_out, m_relpos_table, m_q_norm_a, m_k_norm_a, m_q_norm_b, m_k_norm_b, m_sinks, m_ssm_w_in, m_ssm_conv_w, m_ssm_conv_b, m_ssm_dt_bias, m_ssm_a_log, m_ssm_d, m_ssm_norm, m_ssm_w_out, m_ffn_w_in, m_ffn_conv_w, m_ffn_conv_b, m_ffn_w_out, v_norm_mix, v_norm_ffn, v_attn_w_in, v_attn_w_out, v_relpos_table, v_q_norm_a, v_k_norm_a, v_q_norm_b, v_k_norm_b, v_sinks, v_ssm_w_in, v_ssm_conv_w, v_ssm_conv_b, v_ssm_dt_bias, v_ssm_a_log, v_ssm_d, v_ssm_norm, v_ssm_w_out, v_ffn_w_in, v_ffn_conv_w, v_ffn_conv_b, v_ffn_w_out):
    w = dict(norm_mix=norm_mix, norm_ffn=norm_ffn, attn_w_in=attn_w_in, attn_w_out=attn_w_out, relpos_table=relpos_table,
             q_norm_a=q_norm_a, k_norm_a=k_norm_a, q_norm_b=q_norm_b, k_norm_b=k_norm_b, sinks=sinks, ssm_w_in=ssm_w_in,
             ssm_conv_w=ssm_conv_w, ssm_conv_b=ssm_conv_b, ssm_dt_bias=ssm_dt_bias, ssm_a_log=ssm_a_log, ssm_d=ssm_d,
             ssm_norm=ssm_norm, ssm_w_out=ssm_w_out, ffn_w_in=ffn_w_in, ffn_conv_w=ffn_conv_w, ffn_conv_b=ffn_conv_b,
             ffn_w_out=ffn_w_out)
    mom = dict(norm_mix=m_norm_mix, norm_ffn=m_norm_ffn, attn_w_in=m_attn_w_in, attn_w_out=m_attn_w_out,
               relpos_table=m_relpos_table, q_norm_a=m_q_norm_a, k_norm_a=m_k_norm_a, q_norm_b=m_q_norm_b,
               k_norm_b=m_k_norm_b, sinks=m_sinks, ssm_w_in=m_ssm_w_in, ssm_conv_w=m_ssm_conv_w, ssm_conv_b=m_ssm_conv_b,
               ssm_dt_bias=m_ssm_dt_bias, ssm_a_log=m_ssm_a_log, ssm_d=m_ssm_d, ssm_norm=m_ssm_norm, ssm_w_out=m_ssm_w_out,
               ffn_w_in=m_ffn_w_in, ffn_conv_w=m_ffn_conv_w, ffn_conv_b=m_ffn_conv_b, ffn_w_out=m_ffn_w_out)
    var = dict(norm_mix=v_norm_mix, norm_ffn=v_norm_ffn, attn_w_in=v_attn_w_in, attn_w_out=v_attn_w_out,
               relpos_table=v_relpos_table, q_norm_a=v_q_norm_a, k_norm_a=v_k_norm_a, q_norm_b=v_q_norm_b,
               k_norm_b=v_k_norm_b, sinks=v_sinks, ssm_w_in=v_ssm_w_in, ssm_conv_w=v_ssm_conv_w, ssm_conv_b=v_ssm_conv_b,
               ssm_dt_bias=v_ssm_dt_bias, ssm_a_log=v_ssm_a_log, ssm_d=v_ssm_d, ssm_norm=v_ssm_norm, ssm_w_out=v_ssm_w_out,
               ffn_w_in=v_ffn_w_in, ffn_conv_w=v_ffn_conv_w, ffn_conv_b=v_ffn_conv_b, ffn_w_out=v_ffn_w_out)

    def piece(n, l):
        return w[n][l].astype(BF16)

    def gather_of(names_layers):
        return _Comm([piece(n, l) for n, l in names_layers], [False] * len(names_layers))

    def joined(got, names_layers):
        return [_join(g, _AX2[n]) for g, (n, _) in zip(got, names_layers)]

    first = [('attn_w_in', 0), ('attn_w_out', 0)]
    got = _exchange([piece(n, l) for n, l in first] + [_pack([w[n] for n in _SMALL], F32)], [False] * 3, "gather_attn")
    w_attn_in, w_attn_out = joined(got[:2], first)
    full = {}
    for n, g in zip(_SMALL, _unpack(got[2], [w[n].shape for n in _SMALL], lead=(N_DEV,))):
        full[n] = _join(g, _SHARD_AX[n])
    ssm_cw8 = _rows8(full['ssm_conv_w'][0])
    ssm_cb = full['ssm_conv_b']
    ssm_nw = full['ssm_norm']
    ffn_cw8 = [_rows8(full['ffn_conv_w'][l]) for l in range(2)]
    ffn_cb = [ffn_conv_b[l:l + 1] for l in range(2)]

    x0 = x[0]
    target = loss_target[0]
    t = x0.shape[0]

    g_mix0, g_mix1 = norm_mix[0:1], norm_mix[1:2]
    g_ffn0, g_ffn1 = norm_ffn[0:1], norm_ffn[1:2]
    h0 = _rms_fwd(x0, g_mix0, "rms_mix0")
    proj = _mm(h0, w_attn_in, "mm_attn_in")
    hn_w = jnp.concatenate([jnp.tile(v, (1, 2)) for v in (q_norm_a, k_norm_a, q_norm_b, k_norm_b)], axis=0)
    qa, ka, va, qb, kb, vb = _headnorm_fwd(proj, hn_w, "headnorm")
    table = jnp.pad(relpos_table[0], ((0, 0), (0, REL_W - (2 * MAX_REL + 1))))
    bias_a = jnp.where(_band_mask(A_PREV, PAD_A)[None], jnp.transpose(_relpos_fwd(table, "relpos_bias"), (1, 0, 2)), NEG)
    rel_b = jnp.arange(TQ)[:, None] - (jnp.arange(PAD_B + TQ)[None, :] - PAD_B)
    slopes = 2.0 ** (-8.0 * jnp.arange(1, N_HEADS + 1, dtype=F32) / N_HEADS)
    bias_b = jnp.where(_band_mask(B_PREV, PAD_B)[None], -slopes[:, None, None] * jnp.abs(rel_b).astype(F32)[None], NEG)
    no_sinks = jnp.full((N_HEADS,), NEG, F32)
    kpa, vpa = _pad_rows(ka, PAD_A), _pad_rows(va, PAD_A)
    kpb, vpb = _pad_rows(_kv_expand(kb), PAD_B), _pad_rows(_kv_expand(vb), PAD_B)
    ffn0_w, ssm_w, ffn1_w = [('ffn_w_in', 0), ('ffn_w_out', 0)], [('ssm_w_in', 0), ('ssm_w_out', 0)], [('ffn_w_in', 1), ('ffn_w_out', 1)]
    oa, got = _attn_fwd(qa, kpa, vpa, bias_a, no_sinks, PAD_A, "attn_a", comm=gather_of(ffn0_w))
    w_ffn_in0, w_ffn_out0 = joined(got, ffn0_w)
    ob, got = _attn_fwd(qb, kpb, vpb, bias_b, sinks[0], PAD_B, "attn_b", comm=gather_of(ssm_w))
    w_ssm_in, w_ssm_out = joined(got, ssm_w)
    w_ssm_main = w_ssm_in[:, :ZX]
    w_ssm_dt = jnp.pad(w_ssm_in[:, ZX:], ((0, 0), (0, LANES - SSM_HEADS)))
    x1 = _mm(oa, w_attn_out[:512], "mm_attn_out_a", res=x0)
    x1 = _mm(ob, w_attn_out[512:], "mm_attn_out_b", res=x1)
    x2, ffn0_saved = _ffn_fwd(x1, g_ffn0, w_ffn_in0, ffn_cw8[0], ffn_cb[0], w_ffn_out0, "0")

    h2 = _rms_fwd(x2, g_mix1, "rms_mix1")
    zx = _mm(h2, w_ssm_main, "mm_ssm_in")
    dtraw = _mm(h2, w_ssm_dt, "mm_ssm_dt")
    dt_bias = _lanes128(ssm_dt_bias[0])
    alog = _lanes128(ssm_a_log[0])
    dexp = jnp.repeat(ssm_d[0], HEAD_DIM).reshape(1, D_INNER)
    xbc = _ssm_pre_fwd(zx, ssm_cw8, ssm_cb, "ssm_pre")
    dt = _dt_fwd(dtraw, dt_bias, "ssm_dt")
    (y, sprev), got = _ssd_fwd(xbc, dt, alog, "ssd_fwd", comm=gather_of(ffn1_w))
    w_ffn_in1, w_ffn_out1 = joined(got, ffn1_w)
    y4 = _ssm_post_fwd(y, xbc, zx, dexp, ssm_nw, "ssm_post")
    x3 = _mm(y4, w_ssm_out, "mm_ssm_out", res=x2)
    x4, ffn1_saved = _ffn_fwd(x3, g_ffn1, w_ffn_in1, ffn_cw8[1], ffn_cb[1], w_ffn_out1, "1")

    dx4, dx4b, sq = _loss(x4, target, "loss")
    loss = lax.psum(0.5 * jnp.sum(sq) / D_MODEL, ("x", "y", "c"))

    grads = {}

    def scatter_of(pieces):
        return _Comm([_split(g, _AX2[n]).astype(BF16) for n, g in pieces], [True] * len(pieces))

    dx3, dx3b, dg_ffn1, dwin1, dcw1, dcb1, dwout1 = _ffn_bwd(
        dx4, dx4b, x3, g_ffn1, w_ffn_in1.T, ffn_cw8[1], ffn_cb[1], w_ffn_out1.T, ffn1_saved, "1")

    dy4 = _mm(dx3b, w_ssm_out.T, "mm_ssm_dy")
    dw_ssm_out = _mm(y4, dx3b, "mm_ssm_dwout", trans_a=True)
    dyv, dskip, dz, dd_lane, dnw = _ssm_post_bwd(dy4, y, xbc, zx, dexp, ssm_nw, "ssm_post_bwd")
    (dxbc, ddt, dalog), parts_ffn1 = _ssd_bwd(xbc, dt, alog, sprev, dyv, dskip, "ssd_bwd",
                                              comm=scatter_of([('ffn_w_in', dwin1), ('ffn_w_out', dwout1)]))
    dc, dcw_s, dcb_s = _ssm_pre_bwd1(zx, dxbc, ssm_cw8, ssm_cb, "ssm_pre_bwd")
    ddtraw, ddtb = _dt_bwd(dtraw, dt_bias, ddt, "ssm_dt_bwd")
    dxr = _conv_bwd_data(dc, ssm_cw8, 4, "ssm_conv_bwd")
    w_main_t = w_ssm_main.T
    dh2 = _mm(dz, w_main_t[:D_INNER], "mm_ssm_dh_z")
    dh2 = _mm(dxr, w_main_t[D_INNER:], "mm_ssm_dh_x", res=dh2)
    dh2 = _mm(ddtraw, w_ssm_dt.T, "mm_ssm_dh_dt", res=dh2)
    dw_ssm_in = jnp.concatenate([
        _mm(h2, dz, "mm_ssm_dwin_z", trans_a=True), _mm(h2, dxr, "mm_ssm_dwin_x", trans_a=True),
        _mm(h2, ddtraw, "mm_ssm_dwin_dt", trans_a=True)[:, :SSM_HEADS]], axis=1)
    dx2, dx2b, dg_mix1 = _rms_bwd(x2, g_mix1, dh2, dx3, "rms_mix1_bwd")
    grads['ssm_conv_w'] = dcw_s[:4][None]
    grads['ssm_conv_b'] = dcb_s
    grads['ssm_norm'] = dnw
    grads['ssm_dt_bias'] = ddtb[:, :SSM_HEADS]
    grads['ssm_a_log'] = dalog[:, :SSM_HEADS]
    grads['ssm_d'] = jnp.sum(dd_lane.reshape(SSM_HEADS, HEAD_DIM), axis=1)[None]

    dx1, dx1b, dg_ffn0, dwin0, dcw0, dcb0, dwout0 = _ffn_bwd(
        dx2, dx2b, x1, g_ffn0, w_ffn_in0.T, ffn_cw8[0], ffn_cb[0], w_ffn_out0.T, ffn0_saved, "0")
    grads['ffn_conv_w'] = jnp.stack([dcw0, dcw1])
    grads['ffn_conv_b'] = jnp.concatenate([dcb0, dcb1], axis=0)
    grads['norm_ffn'] = jnp.concatenate([dg_ffn0, dg_ffn1], axis=0)

    do = _mm(dx1b, w_attn_out.T, "mm_attn_do", out_dtype=BF16)
    dw_attn_out = jnp.concatenate([_mm(oa, dx1b, "mm_attn_dwout_a", trans_a=True),
                                   _mm(ob, dx1b, "mm_attn_dwout_b", trans_a=True)], axis=0)
    (dqa, dkpa, dvpa, dbias_a, _), parts_ssm = _attn_bwd(
        qa, kpa, vpa, bias_a, no_sinks, do, 0, PAD_A, "attn_a_bwd",
        comm=scatter_of([('ssm_w_in', dw_ssm_in), ('ssm_w_out', dw_ssm_out), ('attn_w_out', dw_attn_out)]))
    (dqb, dkpb, dvpb, _, dsink), parts_ffn0 = _attn_bwd(
        qb, kpb, vpb, bias_b, sinks[0], do, 4, PAD_B, "attn_b_bwd",
        comm=scatter_of([('ffn_w_in', dwin0), ('ffn_w_out', dwout0)]))
    grads['relpos_table'] = _relpos_bwd(jnp.transpose(dbias_a, (1, 0, 2)), "relpos_bwd")[None, :, :2 * MAX_REL + 1]
    grads['sinks'] = dsink[:, :2, 0].reshape(1, N_HEADS)
    dproj, dhn = _headnorm_bwd(proj, hn_w, dqa, dkpa, dvpa, dqb, dkpb, dvpb, "headnorm_bwd")
    dhn = dhn[:, :HEAD_DIM] + dhn[:, HEAD_DIM:]
    for k, n in enumerate(('q_norm_a', 'k_norm_a', 'q_norm_b', 'k_norm_b')):
        grads[n] = dhn[k:k + 1]
    dh0 = _mm(dproj, w_attn_in.T, "mm_attn_dh")
    dw_attn_in = _mm(h0, dproj, "mm_attn_dwin", trans_a=True)
    dx0, _, dg_mix0, parts_attn_in = _rms_bwd(x0, g_mix0, dh0, dx1, "rms_mix0_bwd", comm=scatter_of([('attn_w_in', dw_attn_in)]))
    grads['norm_mix'] = jnp.concatenate([dg_mix0, dg_mix1], axis=0)

    def adam_piece(n, l, parts):
        rows = w[n][l].shape[0]
        return _adamw(parts, w[n][l], mom[n][l], var[n][l], _pick(rows, (256, 128, 64)), f"adamw_{n}{l}")[0]

    sm_shapes = [w[n].shape for n in _SMALL]
    rp_shapes = [w[n].shape for n in _REPL]
    recv = _exchange(
        [_pack([_split(grads[n], _SHARD_AX[n]) for n in _SMALL], F32, lead=(N_DEV,)), _pack([grads[n] for n in _REPL], F32)],
        [True, False], "exchange_small")
    res = [{}, {}, {}, {}]
    by_piece = {
        ('ffn_w_in', 1): adam_piece('ffn_w_in', 1, parts_ffn1[0]), ('ffn_w_out', 1): adam_piece('ffn_w_out', 1, parts_ffn1[1]),
        ('ssm_w_in', 0): adam_piece('ssm_w_in', 0, parts_ssm[0]), ('ssm_w_out', 0): adam_piece('ssm_w_out', 0, parts_ssm[1]),
        ('attn_w_out', 0): adam_piece('attn_w_out', 0, parts_ssm[2]),
        ('ffn_w_in', 0): adam_piece('ffn_w_in', 0, parts_ffn0[0]), ('ffn_w_out', 0): adam_piece('ffn_w_out', 0, parts_ffn0[1]),
        ('attn_w_in', 0): adam_piece('attn_w_in', 0, parts_attn_in[0]),
    }
    for n in _BIG:
        for kind in range(4):
            res[kind][n] = jnp.stack([by_piece[n, l][kind] for l in range(w[n].shape[0])])
    for names, shapes, parts in ((_SMALL, sm_shapes, recv[0]), (_REPL, rp_shapes, recv[1])):
        outs, _ = _adamw(parts, _pack([w[n] for n in names], F32), _pack([mom[n] for n in names], F32),
                         _pack([var[n] for n in names], F32), ADAM_TR, "adamw_" + ("small" if names is _SMALL else "replicated"))
        for kind, flat in enumerate(outs):
            for n, a in zip(names, _unpack(flat, shapes)):
                res[kind][n] = a
    return (loss, dx0[None], *[res[0][n] for n in _WEIGHTS], *[res[1][n] for n in _WEIGHTS],
            *[res[2][n] for n in _WEIGHTS], *[res[3][n] for n in _WEIGHTS])
```

```python
import jax
import jax.numpy as jnp
from jax import lax
from jax.experimental import pallas as pl
from jax.experimental.pallas import tpu as pltpu

F32 = jnp.float32
BF16 = jnp.bfloat16
HI = lax.Precision.HIGHEST
MESH = pl.DeviceIdType.MESH
NEG = -1e30

N_DEV = 8
D_MODEL = 1024
EPS = 1e-6
CHUNK = 64
HEAD_DIM = 64
N_HEADS = 8
A_PREV = 8
B_PREV = 2
MAX_REL = 256
TQ = 2 * CHUNK
ATT_SUB = 4
PAD_A = A_PREV * CHUNK
PAD_B = B_PREV * CHUNK
REL_W = PAD_A + TQ
D_INNER = 2048
SSM_HEADS = 32
SSM_GROUPS = 4
SSM_STATE = 128
XBC = D_INNER + 2 * SSM_GROUPS * SSM_STATE
ZX = D_INNER + XBC
D_FF = 2816
SSD_L = 128
LANES = 128
VMEM_LIMIT = 56 << 20

ADAM_LR, ADAM_B1, ADAM_B2, ADAM_EPS, ADAM_WD, ADAM_STEP = 0.001, 0.9, 0.999, 1e-08, 0.01, 10


def _cp(sem=None):
    return pltpu.CompilerParams(dimension_semantics=sem, vmem_limit_bytes=VMEM_LIMIT)


def _dot(a, b, ca=1, cb=0, prec=None):
    return lax.dot_general(a, b, (((ca,), (cb,)), ((), ())), preferred_element_type=F32, precision=prec)


def _pick(n, cands):
    for c in cands:
        if n % c == 0:
            return c
    return n


def _lo_mask():
    return lax.broadcasted_iota(jnp.int32, (1, LANES), 1) < HEAD_DIM


def _mm(a, b, name, out_dtype=F32, res=None, trans_a=False):
    n = b.shape[1]
    tn = _pick(n, (1408, 1536, 1152, 1024, 512, 256, 128))
    if trans_a:
        kdim, m = a.shape
        assert b.shape[0] == kdim and res is None and out_dtype == F32, (a.shape, b.shape)
        tk = _pick(kdim, (512, 256, 128))

        def body_t(a_ref, b_ref, o_ref):
            @pl.when(pl.program_id(0) == 0)
            def _():
                o_ref[...] = jnp.zeros_like(o_ref)

            av = a_ref[...]
            for c in range(0, n, tn):
                o_ref[:, c:c + tn] += _dot(av, b_ref[:, c:c + tn], 0, 0)

        return pl.pallas_call(
            body_t, name=name, grid=(kdim // tk,),
            in_specs=[pl.BlockSpec((tk, m), lambda k: (k, 0)), pl.BlockSpec((tk, n), lambda k: (k, 0))],
            out_specs=pl.BlockSpec((m, n), lambda k: (0, 0)), out_shape=jax.ShapeDtypeStruct((m, n), F32),
            compiler_params=_cp(("arbitrary",)),
        )(a, b)

    m, kdim = a.shape
    assert b.shape[0] == kdim, (a.shape, b.shape)
    tm = _pick(m, (256, 128) if n > 2304 else (512, 256, 128))

    def body(*refs):
        if res is None:
            a_ref, b_ref, o_ref = refs
        else:
            a_ref, b_ref, r_ref, o_ref = refs
        av = a_ref[...]
        for c in range(0, n, tn):
            r = _dot(av, b_ref[:, c:c + tn], 1, 0)
            if res is not None:
                r = r + r_ref[:, c:c + tn]
            o_ref[:, c:c + tn] = r.astype(out_dtype)

    in_specs = [pl.BlockSpec((tm, kdim), lambda i: (i, 0)), pl.BlockSpec((kdim, n), lambda i: (0, 0))]
    args = [a, b]
    if res is not None:
        in_specs.append(pl.BlockSpec((tm, n), lambda i: (i, 0)))
        args.append(res)
    return pl.pallas_call(
        body, name=name, grid=(m // tm,), in_specs=in_specs, out_specs=pl.BlockSpec((tm, n), lambda i: (i, 0)),
        out_shape=jax.ShapeDtypeStruct((m, n), out_dtype), compiler_params=_cp(("parallel",)),
    )(*args)


def _rms_mm(x, g, b, name, out_dtype):
    t, d = x.shape
    n = b.shape[1]
    tn = _pick(n, (1408, 1536, 1152, 1024, 512, 256, 128))
    tm = _pick(t, (256, 128))

    def body(x_ref, g_ref, b_ref, o_ref, h_ref):
        xv = x_ref[...]
        r = lax.rsqrt(jnp.mean(xv * xv, axis=-1, keepdims=True) + EPS)
        h = (xv * r * g_ref[...]).astype(BF16)
        h_ref[...] = h
        for c in range(0, n, tn):
            o_ref[:, c:c + tn] = _dot(h, b_ref[:, c:c + tn], 1, 0).astype(out_dtype)

    row = pl.BlockSpec((tm, d), lambda i: (i, 0))
    return pl.pallas_call(
        body, name=name, grid=(t // tm,),
        in_specs=[row, pl.BlockSpec((1, d), lambda i: (0, 0)), pl.BlockSpec((d, n), lambda i: (0, 0))],
        out_specs=[pl.BlockSpec((tm, n), lambda i: (i, 0)), row],
        out_shape=[jax.ShapeDtypeStruct((t, n), out_dtype), jax.ShapeDtypeStruct((t, d), BF16)],
        compiler_params=_cp(("parallel",)),
    )(x, g, b)


def _mm_rms_bwd(a, b, dh_prev, x, g, dres, name, comm=None):
    t, d = x.shape
    kdim = a.shape[1]
    tm = _pick(t, (256, 128))

    def body(*refs):
        if dh_prev is None:
            a_ref, b_ref, x_ref, g_ref, dr_ref, dx_ref, dxb_ref, dg_ref = refs
            dhv = _dot(a_ref[...], b_ref[...], 1, 0)
        else:
            a_ref, b_ref, p_ref, x_ref, g_ref, dr_ref, dx_ref, dxb_ref, dg_ref = refs
            dhv = _dot(a_ref[...], b_ref[...], 1, 0) + p_ref[...]
        xv = x_ref[...]
        r = lax.rsqrt(jnp.mean(xv * xv, axis=-1, keepdims=True) + EPS)
        xh = xv * r
        dxh = dhv * g_ref[...]
        dx = dr_ref[...] + r * (dxh - xh * jnp.mean(dxh * xh, axis=-1, keepdims=True))
        dx_ref[...] = dx
        dxb_ref[...] = dx.astype(BF16)

        @pl.when(pl.program_id(0) == 0)
        def _():
            dg_ref[...] = jnp.zeros_like(dg_ref)

        dg_ref[...] += jnp.sum(dhv * xh, axis=0, keepdims=True)

    row = pl.BlockSpec((tm, d), lambda i: (i, 0))
    vec = pl.BlockSpec((1, d), lambda i: (0, 0))
    in_specs = [pl.BlockSpec((tm, kdim), lambda i: (i, 0)), pl.BlockSpec((kdim, d), lambda i: (0, 0))]
    args = [a, b]
    if dh_prev is not None:
        in_specs.append(row)
        args.append(dh_prev)
    outs, got = _call(
        body, name=name, grid=(t // tm,), in_specs=in_specs + [row, vec, row], out_specs=[row, row, vec],
        out_shape=[jax.ShapeDtypeStruct((t, d), F32), jax.ShapeDtypeStruct((t, d), BF16), jax.ShapeDtypeStruct((1, d), F32)],
        args=(*args, x, g, dres), sem=("arbitrary",), comm=comm)
    return (*outs, got) if comm is not None else tuple(outs)


def _mm_loss(a, b, res, target, name):
    t, kdim = a.shape
    d = b.shape[1]
    tm = _pick(t, (512, 256, 128))

    def body(a_ref, b_ref, r_ref, t_ref, dy_ref, dyb_ref, acc_ref):
        @pl.when(pl.program_id(0) == 0)
        def _():
            acc_ref[...] = jnp.zeros_like(acc_ref)

        err = _dot(a_ref[...], b_ref[...], 1, 0) + r_ref[...] - t_ref[...]
        dy = err * (1.0 / d)
        dy_ref[...] = dy
        dyb_ref[...] = dy.astype(BF16)
        acc_ref[...] += jnp.sum(err * err, axis=0, keepdims=True)

    row = pl.BlockSpec((tm, d), lambda i: (i, 0))
    vec = pl.BlockSpec((1, d), lambda i: (0, 0))
    return pl.pallas_call(
        body, name=name, grid=(t // tm,),
        in_specs=[pl.BlockSpec((tm, kdim), lambda i: (i, 0)), pl.BlockSpec((kdim, d), lambda i: (0, 0)), row, row],
        out_specs=[row, row, vec],
        out_shape=[jax.ShapeDtypeStruct((t, d), F32), jax.ShapeDtypeStruct((t, d), BF16), jax.ShapeDtypeStruct((1, d), F32)],
        compiler_params=_cp(("arbitrary",)),
    )(a, b, res, target)


def _head_rms(xs, w, lo):
    sq = xs * xs
    s0 = jnp.sum(jnp.where(lo, sq, 0.0), axis=-1, keepdims=True)
    s1 = jnp.sum(jnp.where(lo, 0.0, sq), axis=-1, keepdims=True)
    r = jnp.where(lo, lax.rsqrt(s0 * (1.0 / HEAD_DIM) + EPS), lax.rsqrt(s1 * (1.0 / HEAD_DIM) + EPS))
    return xs * r, r


def _head_rms_bwd(xs, w, dy, lo):
    xh, r = _head_rms(xs, w, lo)
    dxh = dy * w
    t = dxh * xh
    m0 = jnp.sum(jnp.where(lo, t, 0.0), axis=-1, keepdims=True)
    m1 = jnp.sum(jnp.where(lo, 0.0, t), axis=-1, keepdims=True)
    mm = jnp.where(lo, m0, m1) * (1.0 / HEAD_DIM)
    return r * (dxh - xh * mm), dy * xh


_QSCALE = HEAD_DIM ** -0.5


def _headnorm_fwd(proj, ws, name):
    t = proj.shape[0]
    tm = _pick(t, (256, 128))

    def body(p_ref, w_ref, qa_ref, ka_ref, va_ref, qb_ref, kb_ref, vb_ref):
        lo = _lo_mask()
        for s in range(4):
            c = LANES * s
            xh, _ = _head_rms(p_ref[:, c:c + LANES], None, lo)
            qa_ref[:, c:c + LANES] = (xh * w_ref[0:1, :] * _QSCALE).astype(BF16)
            xh, _ = _head_rms(p_ref[:, 512 + c:512 + c + LANES], None, lo)
            ka_ref[:, c:c + LANES] = (xh * w_ref[1:2, :]).astype(BF16)
            xh, _ = _head_rms(p_ref[:, 1536 + c:1536 + c + LANES], None, lo)
            qb_ref[:, c:c + LANES] = (xh * w_ref[2:3, :] * _QSCALE).astype(BF16)
        va_ref[...] = p_ref[:, 1024:1536].astype(BF16)
        xh, _ = _head_rms(p_ref[:, 2048:2176], None, lo)
        kb_ref[...] = (xh * w_ref[3:4, :]).astype(BF16)
        vb_ref[...] = p_ref[:, 2176:2304].astype(BF16)

    wide = pl.BlockSpec((tm, 512), lambda i: (i, 0))
    narrow = pl.BlockSpec((tm, LANES), lambda i: (i, 0))
    sd = lambda n: jax.ShapeDtypeStruct((t, n), BF16)
    return pl.pallas_call(
        body, name=name, grid=(t // tm,),
        in_specs=[pl.BlockSpec((tm, 2304), lambda i: (i, 0)), pl.BlockSpec((4, LANES), lambda i: (0, 0))],
        out_specs=[wide, wide, wide, wide, narrow, narrow],
        out_shape=[sd(512), sd(512), sd(512), sd(512), sd(LANES), sd(LANES)],
        compiler_params=_cp(("parallel",)),
    )(proj, ws)


def _headnorm_bwd(proj, ws, dqa, dkpa, dvpa, dqb, dkpb, dvpb, name):
    t = proj.shape[0]
    tm = TQ
    offa, offb = PAD_A // tm, PAD_B // tm

    def body(p_ref, w_ref, dqa_ref, dka_ref, dva_ref, dqb_ref, dkb_ref, dvb_ref, dp_ref, dw_ref):
        i = pl.program_id(0)
        lo = _lo_mask()

        @pl.when(i == 0)
        def _():
            dw_ref[...] = jnp.zeros_like(dw_ref)

        acc = [jnp.zeros((1, LANES), F32) for _ in range(4)]
        for s in range(4):
            c = LANES * s
            dx, dwl = _head_rms_bwd(p_ref[:, c:c + LANES], w_ref[0:1, :], dqa_ref[:, c:c + LANES] * _QSCALE, lo)
            dp_ref[:, c:c + LANES] = dx.astype(BF16)
            acc[0] += jnp.sum(dwl, axis=0, keepdims=True)
            dx, dwl = _head_rms_bwd(p_ref[:, 512 + c:512 + c + LANES], w_ref[1:2, :], dka_ref[:, c:c + LANES], lo)
            dp_ref[:, 512 + c:512 + c + LANES] = dx.astype(BF16)
            acc[1] += jnp.sum(dwl, axis=0, keepdims=True)
            dx, dwl = _head_rms_bwd(p_ref[:, 1536 + c:1536 + c + LANES], w_ref[2:3, :], dqb_ref[:, c:c + LANES] * _QSCALE, lo)
            dp_ref[:, 1536 + c:1536 + c + LANES] = dx.astype(BF16)
            acc[2] += jnp.sum(dwl, axis=0, keepdims=True)
        dp_ref[:, 1024:1536] = dva_ref[...].astype(BF16)

        def group_sum(ref):
            s0 = ref[:, 0:128] + ref[:, 128:256]
            s1 = ref[:, 256:384] + ref[:, 384:512]
            s0 = s0 + pltpu.roll(s0, HEAD_DIM, 1)
            s1 = s1 + pltpu.roll(s1, HEAD_DIM, 1)
            return jnp.where(lo, s0, s1)

        dx, dwl = _head_rms_bwd(p_ref[:, 2048:2176], w_ref[3:4, :], group_sum(dkb_ref), lo)
        dp_ref[:, 2048:2176] = dx.astype(BF16)
        acc[3] += jnp.sum(dwl, axis=0, keepdims=True)
        dp_ref[:, 2176:2304] = group_sum(dvb_ref).astype(BF16)
        for n in range(4):
            dw_ref[n:n + 1, :] += acc[n]

    wide = pl.BlockSpec((tm, 512), lambda i: (i, 0))
    pa = pl.BlockSpec((tm, 512), lambda i: (i + offa, 0))
    pb = pl.BlockSpec((tm, 512), lambda i: (i + offb, 0))
    return pl.pallas_call(
        body, name=name, grid=(t // tm,),
        in_specs=[pl.BlockSpec((tm, 2304), lambda i: (i, 0)), pl.BlockSpec((4, LANES), lambda i: (0, 0)),
                  wide, pa, pa, wide, pb, pb],
        out_specs=[pl.BlockSpec((tm, 2304), lambda i: (i, 0)), pl.BlockSpec((4, LANES), lambda i: (0, 0))],
        out_shape=[jax.ShapeDtypeStruct((t, 2304), BF16), jax.ShapeDtypeStruct((4, LANES), F32)],
        compiler_params=_cp(("arbitrary",)),
    )(proj, ws, dqa, dkpa, dvpa, dqb, dkpb, dvpb)


ROLL_W = 1024


def _rel_onehot():
    r_io = lax.broadcasted_iota(jnp.int32, (REL_W, ROLL_W), 0)
    m_io = lax.broadcasted_iota(jnp.int32, (REL_W, ROLL_W), 1)
    return (r_io == jnp.clip(REL_W - 1 - m_io, -MAX_REL, MAX_REL) + MAX_REL).astype(F32)


def _relpos_fwd(table, name):
    def body(t_ref, o_ref):
        rr = _dot(t_ref[...], _rel_onehot(), 1, 0, HI)

        def step(q, c):
            o_ref[q] = pltpu.roll(rr, (ROLL_W - (TQ - 1) + q) % ROLL_W, 1)[:, :REL_W]
            return c

        lax.fori_loop(0, TQ, step, 0)

    return pl.pallas_call(
        body, name=name, out_shape=jax.ShapeDtypeStruct((TQ, N_HEADS, REL_W), F32),
        in_specs=[pl.BlockSpec(memory_space=pltpu.VMEM)], out_specs=pl.BlockSpec(memory_space=pltpu.VMEM),
        compiler_params=_cp(),
    )(table)


def _relpos_bwd(dbias_t, name):
    def body(d_ref, o_ref):
        def step(q, acc):
            row = jnp.concatenate([d_ref[q], jnp.zeros((N_HEADS, ROLL_W - REL_W), F32)], axis=1)
            return acc + pltpu.roll(row, TQ - 1 - q, 1)

        drr = lax.fori_loop(0, TQ, step, jnp.zeros((N_HEADS, ROLL_W), F32))
        o_ref[...] = _dot(drr, _rel_onehot(), 1, 1, HI)

    return pl.pallas_call(
        body, name=name, out_shape=jax.ShapeDtypeStruct((N_HEADS, REL_W), F32),
        in_specs=[pl.BlockSpec(memory_space=pltpu.VMEM)], out_specs=pl.BlockSpec(memory_space=pltpu.VMEM),
        compiler_params=_cp(),
    )(dbias_t)


def _attn_probs(qe, kw, bias, kvalid, snk):
    s = _dot(qe, kw, 1, 1) + bias
    s = jnp.where(kvalid, s, NEG)
    m = jnp.maximum(jnp.max(s, axis=-1, keepdims=True), snk)
    p = jnp.exp(s - m)
    inv = 1.0 / (jnp.sum(p, axis=-1, keepdims=True) + jnp.exp(snk - m))
    return p * inv, jnp.exp(snk - m) * inv


def _attn_fwd(q, kp, vp, bias, sinks, pad, name, comm=None):
    t, hd = q.shape
    w = pad + TQ

    def body(sink_ref, q_ref, k_ref, v_ref, b_ref, o_ref):
        hp, i = pl.program_id(0), pl.program_id(1)
        lo = _lo_mask()
        for j in range(ATT_SUB):
            start = pl.multiple_of((i * ATT_SUB + j) * TQ, TQ)
            qv = q_ref[TQ * j:TQ * (j + 1), :]
            kw = k_ref[pl.ds(start, w), :]
            vw = v_ref[pl.ds(start, w), :]
            kvalid = (start + lax.broadcasted_iota(jnp.int32, (1, w), 1)) >= pad
            outs = []
            for e in range(2):
                sel = lo if e == 0 else jnp.logical_not(lo)
                qe = jnp.where(sel, qv, jnp.zeros_like(qv))
                p, _ = _attn_probs(qe, kw, b_ref[e], kvalid, sink_ref[2 * hp + e])
                outs.append(_dot(p.astype(BF16), vw, 1, 0))
            o_ref[TQ * j:TQ * (j + 1), :] = jnp.where(lo, outs[0], outs[1]).astype(BF16)

    full = pl.BlockSpec((t + pad, LANES), lambda h, i: (0, h))
    tile = pl.BlockSpec((ATT_SUB * TQ, LANES), lambda h, i: (i, h))
    (o,), got = _call(
        body, name=name, grid=(hd // LANES, t // (ATT_SUB * TQ)),
        in_specs=[pl.BlockSpec(memory_space=pltpu.SMEM), tile, full, full, pl.BlockSpec((2, TQ, w), lambda h, i: (h, 0, 0))],
        out_specs=[tile], out_shape=[jax.ShapeDtypeStruct((t, hd), BF16)],
        args=(sinks, q, kp, vp, bias), sem=("parallel", "arbitrary"), comm=comm)
    return o, got


def _attn_bwd(q, kp, vp, bias, sinks, do, col_off, pad, name, comm=None):
    t, hd = q.shape
    w = pad + TQ
    nhp = hd // LANES

    def body(sink_ref, q_ref, k_ref, v_ref, b_ref, do_ref, dq_ref, dk_ref, dv_ref, db_ref, ds_ref):
        hp, i = pl.program_id(0), pl.program_id(1)

        @pl.when(i == 0)
        def _():
            dk_ref[...] = jnp.zeros_like(dk_ref)
            dv_ref[...] = jnp.zeros_like(dv_ref)
            db_ref[...] = jnp.zeros_like(db_ref)
            ds_ref[...] = jnp.zeros_like(ds_ref)

        lo = _lo_mask()
        row8 = lax.broadcasted_iota(jnp.int32, (8, LANES), 0)
        dbias = [None, None]
        dsink = jnp.zeros((8, LANES), F32)
        for j in range(ATT_SUB):
            start = pl.multiple_of((i * ATT_SUB + j) * TQ, TQ)
            qv = q_ref[TQ * j:TQ * (j + 1), :]
            dov = do_ref[TQ * j:TQ * (j + 1), :]
            kw = k_ref[pl.ds(start, w), :]
            vw = v_ref[pl.ds(start, w), :]
            kvalid = (start + lax.broadcasted_iota(jnp.int32, (1, w), 1)) >= pad
            dqs, dkw, dvw = [], None, None
            for e in range(2):
                sel = lo if e == 0 else jnp.logical_not(lo)
                qe = jnp.where(sel, qv, jnp.zeros_like(qv))
                doe = jnp.where(sel, dov, jnp.zeros_like(dov))
                p, psink = _attn_probs(qe, kw, b_ref[e], kvalid, sink_ref[2 * hp + e])
                dp = _dot(doe, vw, 1, 1)
                delta = jnp.sum(p * dp, axis=-1, keepdims=True)
                ds = p * (dp - delta)
                dbias[e] = ds if dbias[e] is None else dbias[e] + ds
                dsink = dsink + jnp.where(row8 == e, jnp.sum(-psink * delta, axis=0, keepdims=True), 0.0)
                dsb = ds.astype(BF16)
                dqs.append(_dot(dsb, kw, 1, 0))
                dk_e = _dot(dsb, qe, 0, 0)
                dv_e = _dot(p.astype(BF16), doe, 0, 0)
                dkw = dk_e if dkw is None else dkw + dk_e
                dvw = dv_e if dvw is None else dvw + dv_e
            dq_ref[TQ * j:TQ * (j + 1), :] = jnp.where(lo, dqs[0], dqs[1])
            dk_ref[pl.ds(start, w), :] += dkw
            dv_ref[pl.ds(start, w), :] += dvw
        for e in range(2):
            db_ref[e] += dbias[e]
        ds_ref[0] += dsink

    full = pl.BlockSpec((t + pad, LANES), lambda h, i: (0, h))
    tile = pl.BlockSpec((ATT_SUB * TQ, LANES), lambda h, i: (i, h))
    btile = pl.BlockSpec((2, TQ, w), lambda h, i: (h, 0, 0))
    return _call(
        body, name=name, grid=(nhp, t // (ATT_SUB * TQ)),
        in_specs=[pl.BlockSpec(memory_space=pltpu.SMEM), tile, full, full, btile,
                  pl.BlockSpec((ATT_SUB * TQ, LANES), lambda h, i: (i, h + col_off))],
        out_specs=[tile, full, full, btile, pl.BlockSpec((1, 8, LANES), lambda h, i: (h, 0, 0))],
        out_shape=[jax.ShapeDtypeStruct((t, hd), F32), jax.ShapeDtypeStruct((t + pad, hd), F32),
                   jax.ShapeDtypeStruct((t + pad, hd), F32), jax.ShapeDtypeStruct((N_HEADS, TQ, w), F32),
                   jax.ShapeDtypeStruct((nhp, 8, LANES), F32)],
        args=(sinks, q, kp, vp, bias, do), sem=("parallel", "arbitrary"), comm=comm)


def _halo_prev(tm):
    return lambda i: jnp.maximum(i * (tm // 8) - 1, 0)


def _halo_next(tm, t):
    return lambda i: jnp.minimum((i + 1) * (tm // 8), t // 8 - 1)


def _taps_prev(tile, halo, ktaps, first):
    tm = tile.shape[0]
    ext = jnp.concatenate([jnp.where(first, 0.0, halo), tile], axis=0)
    return [tile] + [pltpu.roll(ext, s, 0)[8:8 + tm] for s in range(1, ktaps)]


def _conv_apply(taps, w_ref, ktaps):
    out = taps[0] * w_ref[ktaps - 1:ktaps, :]
    for s in range(1, ktaps):
        out = out + taps[s] * w_ref[ktaps - 1 - s:ktaps - s, :]
    return out


def _silu_grad(x):
    sg = jax.nn.sigmoid(x)
    return x * sg, sg * (1.0 + x * (1.0 - sg))


FFN_TM = 128
FFN_HALO = 16


def _ffn_mid_fwd(gu, w8, b, name):
    t = gu.shape[0]
    f = D_FF
    tm, hr = FFN_TM, FFN_HALO

    def body(g_ref, u_ref, h_ref, w_ref, b_ref, a_ref):
        first = pl.program_id(0) == 0
        ext = jnp.concatenate([jnp.where(first, 0.0, h_ref[...].astype(F32)), g_ref[...].astype(F32)], axis=0)
        taps = [ext[hr:]] + [pltpu.roll(ext, s, 0)[hr:] for s in (1, 2)]
        gc = _conv_apply(taps, w_ref, 3) + b_ref[...]
        a_ref[...] = (gc * jax.nn.sigmoid(gc) * u_ref[...].astype(F32)).astype(BF16)

    return pl.pallas_call(
        body, name=name, grid=(t // tm,),
        in_specs=[pl.BlockSpec((tm, f), lambda i: (i, 0)), pl.BlockSpec((tm, f), lambda i: (i, 1)),
                  pl.BlockSpec((hr, f), lambda i: (jnp.maximum(i * (tm // hr) - 1, 0), 0)),
                  pl.BlockSpec((8, f), lambda i: (0, 0)), pl.BlockSpec((1, f), lambda i: (0, 0))],
        out_specs=pl.BlockSpec((tm, f), lambda i: (i, 0)), out_shape=jax.ShapeDtypeStruct((t, f), BF16),
        compiler_params=_cp(("parallel",)),
    )(gu, gu, gu, w8, b)


FFN_BT = 256
FFN_BC = 1408


def _ffn_mid_bwd(gu, dxb, w_out_t, w8, b, name):
    t, d = dxb.shape
    f = D_FF
    tm, hr = FFN_BT, FFN_HALO
    nt = t // tm
    n = tm + hr

    def body(g_ref, u_ref, gp_ref, gn_ref, un_ref, dx_ref, dxn_ref, wo_ref, w_ref, b_ref, dg_ref, du_ref, dw_ref, db_ref):
        i = pl.program_id(0)
        first, last = i == 0, i == nt - 1

        @pl.when(first)
        def _():
            dw_ref[...] = jnp.zeros_like(dw_ref)
            db_ref[...] = jnp.zeros_like(db_ref)

        dxe = jnp.concatenate([dx_ref[...], dxn_ref[...]], axis=0)
        row = lax.broadcasted_iota(jnp.int32, (n, 1), 0)
        keep = (row < tm) | jnp.logical_not(last)
        for c in range(0, f, FFN_BC):
            cs = slice(c, c + FFN_BC)
            ext = jnp.concatenate([jnp.where(first, 0.0, gp_ref[:, cs].astype(F32)), g_ref[:, cs].astype(F32),
                                   gn_ref[:, cs].astype(F32)], axis=0)
            taps = [ext[hr:]] + [pltpu.roll(ext, s, 0)[hr:] for s in (1, 2)]
            gc = b_ref[:, cs] + taps[0] * w_ref[2:3, cs] + taps[1] * w_ref[1:2, cs] + taps[2] * w_ref[0:1, cs]
            act, dact = _silu_grad(gc)
            da = _dot(dxe, wo_ref[:, cs], 1, 0)
            up = jnp.concatenate([u_ref[:, cs], un_ref[:, cs]], axis=0).astype(F32)
            dgc = jnp.where(keep, da * up * dact, 0.0)
            du_ref[:, cs] = (da[:tm] * act[:tm]).astype(BF16)
            dg_ref[:, cs] = (dgc[:tm] * w_ref[2:3, cs] + pltpu.roll(dgc, n - 1, 0)[:tm] * w_ref[1:2, cs]
                             + pltpu.roll(dgc, n - 2, 0)[:tm] * w_ref[0:1, cs]).astype(BF16)
            db_ref[:, cs] += jnp.sum(dgc[:tm], axis=0, keepdims=True)
            for s in range(3):
                dw_ref[2 - s:3 - s, cs] += jnp.sum(dgc[:tm] * taps[s][:tm], axis=0, keepdims=True)

    r = tm // hr
    prev = lambda i: jnp.maximum(i * r - 1, 0)
    nxt_blk = lambda i: jnp.minimum((i + 1) * r, t // hr - 1)
    row_f = pl.BlockSpec((tm, f), lambda i: (i, 0))
    return pl.pallas_call(
        body, name=name, grid=(nt,),
        in_specs=[row_f, pl.BlockSpec((tm, f), lambda i: (i, 1)),
                  pl.BlockSpec((hr, f), lambda i: (prev(i), 0)), pl.BlockSpec((hr, f), lambda i: (nxt_blk(i), 0)),
                  pl.BlockSpec((hr, f), lambda i: (nxt_blk(i), 1)),
                  pl.BlockSpec((tm, d), lambda i: (i, 0)), pl.BlockSpec((hr, d), lambda i: (nxt_blk(i), 0)),
                  pl.BlockSpec((d, f), lambda i: (0, 0)),
                  pl.BlockSpec((8, f), lambda i: (0, 0)), pl.BlockSpec((1, f), lambda i: (0, 0))],
        out_specs=[row_f, row_f, pl.BlockSpec((8, f), lambda i: (0, 0)), pl.BlockSpec((1, f), lambda i: (0, 0))],
        out_shape=[jax.ShapeDtypeStruct((t, f), BF16), jax.ShapeDtypeStruct((t, f), BF16),
                   jax.ShapeDtypeStruct((8, f), F32), jax.ShapeDtypeStruct((1, f), F32)],
        compiler_params=_cp(("arbitrary",)),
    )(gu, gu, gu, gu, gu, dxb, dxb, w_out_t, w8, b)


PRE_TM = 256
PRE_TC = 1024


def _ssm_pre_fwd(zx, w8, b, name):
    t = zx.shape[0]
    tm, tc = PRE_TM, PRE_TC
    off = D_INNER // tc

    def body(x_ref, h_ref, w_ref, b_ref, o_ref):
        first = pl.program_id(0) == 0
        c = _conv_apply(_taps_prev(x_ref[...], h_ref[...], 4, first), w_ref, 4) + b_ref[...]
        o_ref[...] = c * jax.nn.sigmoid(c)

    hp = _halo_prev(tm)
    return pl.pallas_call(
        body, name=name, grid=(t // tm, XBC // tc),
        in_specs=[pl.BlockSpec((tm, tc), lambda i, j: (i, j + off)), pl.BlockSpec((8, tc), lambda i, j: (hp(i), j + off)),
                  pl.BlockSpec((8, tc), lambda i, j: (0, j)), pl.BlockSpec((1, tc), lambda i, j: (0, j))],
        out_specs=pl.BlockSpec((tm, tc), lambda i, j: (i, j)), out_shape=jax.ShapeDtypeStruct((t, XBC), F32),
        compiler_params=_cp(("parallel", "parallel")),
    )(zx, zx, w8, b)


def _ssm_pre_bwd(zx, dxbc, w8, b, name):
    t = zx.shape[0]
    tm, tc = PRE_TM, PRE_TC
    off = D_INNER // tc
    nt = t // tm
    n = tm + 8

    def body(x_ref, xp_ref, xn_ref, d_ref, dn_ref, w_ref, b_ref, o_ref, dw_ref, db_ref):
        i = pl.program_id(1)
        first, last = i == 0, i == nt - 1

        @pl.when(first)
        def _():
            dw_ref[...] = jnp.zeros_like(dw_ref)
            db_ref[...] = jnp.zeros_like(db_ref)

        ext = jnp.concatenate([jnp.where(first, 0.0, xp_ref[...]), x_ref[...], xn_ref[...]], axis=0)
        taps = [ext[8:8 + n]] + [pltpu.roll(ext, s, 0)[8:8 + n] for s in (1, 2, 3)]
        c = _conv_apply(taps, w_ref, 4) + b_ref[...]
        _, dact = _silu_grad(c)
        row = lax.broadcasted_iota(jnp.int32, (n, 1), 0)
        dc = jnp.where((row < tm) | jnp.logical_not(last), jnp.concatenate([d_ref[...], dn_ref[...]], axis=0) * dact, 0.0)
        nxt = [dc[:tm]] + [pltpu.roll(dc, n - s, 0)[:tm] for s in (1, 2, 3)]
        o_ref[...] = _conv_apply(nxt, w_ref, 4).astype(BF16)
        db_ref[...] += jnp.sum(dc[:tm], axis=0, keepdims=True)
        for s in range(4):
            dw_ref[3 - s:4 - s, :] += jnp.sum(dc[:tm] * taps[s][:tm], axis=0, keepdims=True)

    hp = _halo_prev(tm)
    hn = _halo_next(tm, t)
    return pl.pallas_call(
        body, name=name, grid=(XBC // tc, nt),
        in_specs=[pl.BlockSpec((tm, tc), lambda j, i: (i, j + off)), pl.BlockSpec((8, tc), lambda j, i: (hp(i), j + off)),
                  pl.BlockSpec((8, tc), lambda j, i: (hn(i), j + off)),
                  pl.BlockSpec((tm, tc), lambda j, i: (i, j)), pl.BlockSpec((8, tc), lambda j, i: (hn(i), j)),
                  pl.BlockSpec((8, tc), lambda j, i: (0, j)), pl.BlockSpec((1, tc), lambda j, i: (0, j))],
        out_specs=[pl.BlockSpec((tm, tc), lambda j, i: (i, j)), pl.BlockSpec((8, tc), lambda j, i: (0, j)),
                   pl.BlockSpec((1, tc), lambda j, i: (0, j))],
        out_shape=[jax.ShapeDtypeStruct((t, XBC), BF16), jax.ShapeDtypeStruct((8, XBC), F32),
                   jax.ShapeDtypeStruct((1, XBC), F32)],
        compiler_params=_cp(("parallel", "arbitrary")),
    )(zx, zx, zx, dxbc, dxbc, w8, b)


def _head_lanes():
    return lax.broadcasted_iota(jnp.int32, (1, LANES), 1) < SSM_HEADS


def _dt_fwd(dtraw, bias, name):
    t = dtraw.shape[0]
    tm = _pick(t, (1024, 512, 256, 128))

    def body(x_ref, b_ref, o_ref):
        v = x_ref[...] + b_ref[...]
        sp = jnp.maximum(v, 0.0) + jnp.log(1.0 + jnp.exp(-jnp.abs(v)))
        o_ref[...] = jnp.where(_head_lanes(), sp, 0.0)

    row = pl.BlockSpec((tm, LANES), lambda i: (i, 0))
    return pl.pallas_call(
        body, name=name, grid=(t // tm,), in_specs=[row, pl.BlockSpec((1, LANES), lambda i: (0, 0))], out_specs=row,
        out_shape=jax.ShapeDtypeStruct((t, LANES), F32), compiler_params=_cp(("parallel",)),
    )(dtraw, bias)


def _dt_bwd(dtraw, bias, ddt, name):
    t = dtraw.shape[0]
    tm = _pick(t, (1024, 512, 256, 128))

    def body(x_ref, b_ref, d_ref, o_ref, db_ref):
        @pl.when(pl.program_id(0) == 0)
        def _():
            db_ref[...] = jnp.zeros_like(db_ref)

        g = jnp.where(_head_lanes(), d_ref[...] * jax.nn.sigmoid(x_ref[...] + b_ref[...]), 0.0)
        o_ref[...] = g.astype(BF16)
        db_ref[...] += jnp.sum(g, axis=0, keepdims=True)

    row = pl.BlockSpec((tm, LANES), lambda i: (i, 0))
    vec = pl.BlockSpec((1, LANES), lambda i: (0, 0))
    return pl.pallas_call(
        body, name=name, grid=(t // tm,), in_specs=[row, vec, row], out_specs=[row, vec],
        out_shape=[jax.ShapeDtypeStruct((t, LANES), BF16), jax.ShapeDtypeStruct((1, LANES), F32)],
        compiler_params=_cp(("arbitrary",)),
    )(dtraw, bias, ddt)


GROUP_W = D_INNER // SSM_GROUPS
POST_TM = 512


def _ssm_post_fwd(y, xbc, zx, dexp, nw, name):
    t = y.shape[0]
    tm = _pick(t, (POST_TM, 256, 128))

    def body(y_ref, x_ref, z_ref, d_ref, w_ref, o_ref):
        zv = z_ref[...]
        y3 = (y_ref[...] + d_ref[...] * x_ref[...]) * (zv * jax.nn.sigmoid(zv))
        r = lax.rsqrt(jnp.mean(y3 * y3, axis=-1, keepdims=True) + EPS)
        o_ref[...] = (y3 * r * w_ref[...]).astype(BF16)

    blk = pl.BlockSpec((tm, GROUP_W), lambda i, g: (i, g))
    vec = pl.BlockSpec((1, GROUP_W), lambda i, g: (0, g))
    return pl.pallas_call(
        body, name=name, grid=(t // tm, SSM_GROUPS), in_specs=[blk, blk, blk, vec, vec], out_specs=blk,
        out_shape=jax.ShapeDtypeStruct((t, D_INNER), BF16), compiler_params=_cp(("parallel", "parallel")),
    )(y, xbc, zx, dexp, nw)


def _ssm_post_bwd(dy4, y, xbc, zx, dexp, nw, name):
    t = y.shape[0]
    tm = _pick(t, (POST_TM, 256, 128))

    def body(g_ref, y_ref, x_ref, z_ref, d_ref, w_ref, dy_ref, dxs_ref, dz_ref, dd_ref, dw_ref):
        @pl.when(pl.program_id(1) == 0)
        def _():
            dd_ref[...] = jnp.zeros_like(dd_ref)
            dw_ref[...] = jnp.zeros_like(dw_ref)

        zv = z_ref[...]
        xv = x_ref[...]
        act, dact = _silu_grad(zv)
        y2 = y_ref[...] + d_ref[...] * xv
        y3 = y2 * act
        r = lax.rsqrt(jnp.mean(y3 * y3, axis=-1, keepdims=True) + EPS)
        y3n = y3 * r
        gv = g_ref[...]
        dyn = gv * w_ref[...]
        dy3 = r * (dyn - y3n * jnp.mean(dyn * y3n, axis=-1, keepdims=True))
        dy2 = dy3 * act
        dy_ref[...] = dy2
        dxs_ref[...] = dy2 * d_ref[...]
        dz_ref[...] = (dy3 * y2 * dact).astype(BF16)
        dd_ref[...] += jnp.sum(dy2 * xv, axis=0, keepdims=True)
        dw_ref[...] += jnp.sum(gv * y3n, axis=0, keepdims=True)

    blk = pl.BlockSpec((tm, GROUP_W), lambda g, i: (i, g))
    vec = pl.BlockSpec((1, GROUP_W), lambda g, i: (0, g))
    return pl.pallas_call(
        body, name=name, grid=(SSM_GROUPS, t // tm), in_specs=[blk, blk, blk, blk, vec, vec],
        out_specs=[blk, blk, blk, vec, vec],
        out_shape=[jax.ShapeDtypeStruct((t, D_INNER), F32), jax.ShapeDtypeStruct((t, D_INNER), F32),
                   jax.ShapeDtypeStruct((t, D_INNER), BF16), jax.ShapeDtypeStruct((1, D_INNER), F32),
                   jax.ShapeDtypeStruct((1, D_INNER), F32)],
        compiler_params=_cp(("parallel", "arbitrary")),
    )(dy4, y, xbc, zx, dexp, nw)


def _ssd_common(dt, alog):
    ll = dt.shape[0]
    a_neg = -jnp.exp(alog)
    a = dt * a_neg
    ri = lax.broadcasted_iota(jnp.int32, (ll, ll), 0)
    ci = lax.broadcasted_iota(jnp.int32, (ll, ll), 1)
    tril = ri >= ci
    acs = _dot(tril.astype(F32), a, 1, 0, HI)
    return a_neg, tril, acs, acs.T


def _pair_terms(acs, acs_t, dt, h0, lo):
    ll = acs.shape[0]
    cols = [acs[:, h0 + e:h0 + e + 1] for e in range(2)]
    rows = [acs_t[h0 + e:h0 + e + 1, :] for e in range(2)]
    dtc = [dt[:, h0 + e:h0 + e + 1] for e in range(2)]
    lasts = [c[ll - 1:ll, :] for c in cols]
    dtx = jnp.where(lo, dtc[0], dtc[1])
    eac = jnp.where(lo, jnp.exp(cols[0]), jnp.exp(cols[1]))
    fdec = jnp.where(lo, jnp.exp(lasts[0] - cols[0]), jnp.exp(lasts[1] - cols[1]))
    elast = jnp.where(lo, jnp.exp(lasts[0]), jnp.exp(lasts[1]))
    return cols, rows, dtx, eac, fdec, elast


def _decay(col, row, tril):
    return jnp.where(tril, jnp.exp(jnp.minimum(col - row, 0.0)), 0.0)


def _two_heads_rows(v, lo):
    z = jnp.zeros_like(v)
    return jnp.concatenate([jnp.where(lo, v, z), jnp.where(lo, z, v)], axis=0)


def _two_heads_cols(ms):
    return jnp.concatenate(ms, axis=1)


def _ssd_fwd(xbc, dt, alog, name, comm=None):
    t = xbc.shape[0]
    ll = SSD_L
    nc = t // ll

    def body(x_ref, dt_ref, al_ref, y_ref, sp_ref, st_ref):
        @pl.when(pl.program_id(0) == 0)
        def _():
            st_ref[...] = jnp.zeros_like(st_ref)

        dtv = dt_ref[...]
        _, tril, acs, acs_t = _ssd_common(dtv, al_ref[...])
        lo = _lo_mask()
        sp_ref[0] = st_ref[...]
        for g in range(SSM_GROUPS):
            bg = x_ref[:, D_INNER + SSM_STATE * g:D_INNER + SSM_STATE * (g + 1)].astype(BF16)
            cg = x_ref[:, D_INNER + 512 + SSM_STATE * g:D_INNER + 512 + SSM_STATE * (g + 1)].astype(BF16)
            gm = _dot(cg, bg, 1, 1)
            g0 = GROUP_W * g
            terms = [_pair_terms(acs, acs_t, dtv, 8 * g + 2 * pp, lo) for pp in range(4)]
            dtx, eac, fdec, elast = [jnp.concatenate([tt[k] for tt in terms], axis=1) for k in (2, 3, 4, 5)]
            xg = x_ref[:, g0:g0 + GROUP_W]
            ug = (xg * dtx).astype(BF16)
            sg = st_ref[:, g0:g0 + GROUP_W]
            yst = _dot(cg, sg.astype(BF16), 1, 0) * eac
            st_ref[:, g0:g0 + GROUP_W] = sg * elast + _dot(bg, (xg * (fdec * dtx)).astype(BF16), 0, 0)
            for pp in range(4):
                cols, rows = terms[pp][0], terms[pp][1]
                sl = slice(LANES * pp, LANES * (pp + 1))
                y_in = _dot(_two_heads_cols([(gm * _decay(cols[e], rows[e], tril)).astype(BF16) for e in range(2)]),
                            _two_heads_rows(ug[:, sl], lo), 1, 0)
                y_ref[:, g0 + LANES * pp:g0 + LANES * (pp + 1)] = y_in + yst[:, sl]

    return _call(
        body, name=name, grid=(nc,),
        in_specs=[pl.BlockSpec((ll, XBC), lambda c: (c, 0)), pl.BlockSpec((ll, LANES), lambda c: (c, 0)),
                  pl.BlockSpec((1, LANES), lambda c: (0, 0))],
        out_specs=[pl.BlockSpec((ll, D_INNER), lambda c: (c, 0)), pl.BlockSpec((1, SSM_STATE, D_INNER), lambda c: (c, 0, 0))],
        out_shape=[jax.ShapeDtypeStruct((t, D_INNER), F32), jax.ShapeDtypeStruct((nc, SSM_STATE, D_INNER), F32)],
        scratch_shapes=[pltpu.VMEM((SSM_STATE, D_INNER), F32)],
        args=(xbc, dt, alog), sem=("arbitrary",), comm=comm)


def _ssd_bwd(xbc, dt, alog, sprev, dy, dskip, name, comm=None):
    t = xbc.shape[0]
    ll = SSD_L
    nc = t // ll

    def body(x_ref, dt_ref, al_ref, sp_ref, dy_ref, dk_ref, dx_ref, ddt_ref, dal_ref, ds_ref, colt_ref):
        @pl.when(pl.program_id(0) == 0)
        def _():
            ds_ref[...] = jnp.zeros_like(ds_ref)
            dal_ref[...] = jnp.zeros_like(dal_ref)

        dtv = dt_ref[...]
        a_neg, tril, acs, acs_t = _ssd_common(dtv, al_ref[...])
        lo = _lo_mask()
        hi = jnp.logical_not(lo)
        lane = lax.broadcasted_iota(jnp.int32, (1, LANES), 1)
        colt_ref[...] = jnp.zeros_like(colt_ref)
        rowterm = jnp.zeros((ll, LANES), F32)
        ddt_u = jnp.zeros((ll, LANES), F32)
        dlast = jnp.zeros((1, LANES), F32)

        def halves(v):
            return (jnp.sum(jnp.where(lo, v, 0.0), axis=-1, keepdims=True),
                    jnp.sum(jnp.where(hi, v, 0.0), axis=-1, keepdims=True))

        for g in range(SSM_GROUPS):
            cb0 = D_INNER + SSM_STATE * g
            cc0 = D_INNER + 512 + SSM_STATE * g
            bg = x_ref[:, cb0:cb0 + SSM_STATE].astype(BF16)
            cg = x_ref[:, cc0:cc0 + SSM_STATE].astype(BF16)
            gm = _dot(cg, bg, 1, 1)
            g0 = GROUP_W * g
            terms = [_pair_terms(acs, acs_t, dtv, 8 * g + 2 * pp, lo) for pp in range(4)]
            dtx, eac, fdec, elast = [jnp.concatenate([tt[k] for tt in terms], axis=1) for k in (2, 3, 4, 5)]
            xg = x_ref[:, g0:g0 + GROUP_W]
            u32 = xg * dtx
            ug = u32.astype(BF16)
            dyg = dy_ref[:, g0:g0 + GROUP_W]
            dyb = dyg.astype(BF16)
            spg = sp_ref[0, :, g0:g0 + GROUP_W]
            spb = spg.astype(BF16)
            dsg = ds_ref[:, g0:g0 + GROUP_W]
            dsb = dsg.astype(BF16)
            du_st = _dot(bg, dsb, 1, 0) * fdec
            yst = _dot(cg, spb, 1, 0) * eac
            dye = (dyg * eac).astype(BF16)
            dc_st = _dot(dye, spb, 1, 1)
            db_st = _dot((xg * (fdec * dtx)).astype(BF16), dsb, 1, 1)
            ds_ref[:, g0:g0 + GROUP_W] = dsg * elast + _dot(cg, dye, 0, 0)
            qst_el = du_st * u32
            rq_el = dyg * yst - qst_el
            q_row = jnp.sum(qst_el, axis=0, keepdims=True)
            s_row = jnp.sum(dsg * spg, axis=0, keepdims=True)
            dgm = jnp.zeros((ll, ll), F32)
            for pp in range(4):
                h0 = 8 * g + 2 * pp
                cols, rows = terms[pp][0], terms[pp][1]
                sl = slice(LANES * pp, LANES * (pp + 1))
                decs = [_decay(cols[e], rows[e], tril) for e in range(2)]
                wms = [gm * d for d in decs]
                dum2 = _dot(dyb[:, sl], _two_heads_rows(ug[:, sl], lo), 1, 1)
                du = _dot(jnp.concatenate([wm.astype(BF16) for wm in wms], axis=0),
                          _two_heads_rows(dyb[:, sl], lo), 0, 0) + du_st[:, sl]
                dx_ref[:, g0 + LANES * pp:g0 + LANES * (pp + 1)] = du * dtx[:, sl] + dk_ref[:, g0 + LANES * pp:g0 + LANES * (pp + 1)]
                ddtu = halves(du * xg[:, sl])
                rq = halves(rq_el[:, sl])
                qs = halves(q_row[:, sl])
                ss = halves(s_row[:, sl])
                for e in range(2):
                    dum = dum2[:, ll * e:ll * (e + 1)]
                    dgm = dgm + dum * decs[e]
                    tm_ = dum * wms[e]
                    oh = lane == (h0 + e)
                    rowterm = rowterm + jnp.where(oh, jnp.sum(tm_, axis=1, keepdims=True) + rq[e], 0.0)
                    ddt_u = ddt_u + jnp.where(oh, ddtu[e], 0.0)
                    dlast = dlast + jnp.where(oh, jnp.exp(cols[e][ll - 1:ll, :]) * ss[e] + qs[e], 0.0)
                    colt_ref[h0 + e:h0 + e + 1, :] = jnp.sum(tm_, axis=0, keepdims=True)
            dgb = dgm.astype(BF16)
            dx_ref[:, cc0:cc0 + SSM_STATE] = _dot(dgb, bg, 1, 0) + dc_st
            dx_ref[:, cb0:cb0 + SSM_STATE] = _dot(dgb, cg, 0, 0) + db_st
        row_io = lax.broadcasted_iota(jnp.int32, (ll, LANES), 0)
        dacs = rowterm - colt_ref[...].T + jnp.where(row_io == ll - 1, dlast, 0.0)
        da = _dot(jnp.logical_not(tril).astype(F32) + jnp.where(
            lax.broadcasted_iota(jnp.int32, (ll, ll), 0) == lax.broadcasted_iota(jnp.int32, (ll, ll), 1), 1.0, 0.0),
            dacs, 1, 0, HI)
        ddt_ref[...] = da * a_neg + ddt_u
        dal_ref[...] += jnp.sum(da * dtv, axis=0, keepdims=True) * a_neg

    rev = lambda c: nc - 1 - c
    return _call(
        body, name=name, grid=(nc,),
        in_specs=[pl.BlockSpec((ll, XBC), lambda c: (rev(c), 0)), pl.BlockSpec((ll, LANES), lambda c: (rev(c), 0)),
                  pl.BlockSpec((1, LANES), lambda c: (0, 0)),
                  pl.BlockSpec((1, SSM_STATE, D_INNER), lambda c: (rev(c), 0, 0)),
                  pl.BlockSpec((ll, D_INNER), lambda c: (rev(c), 0)), pl.BlockSpec((ll, D_INNER), lambda c: (rev(c), 0))],
        out_specs=[pl.BlockSpec((ll, XBC), lambda c: (rev(c), 0)), pl.BlockSpec((ll, LANES), lambda c: (rev(c), 0)),
                   pl.BlockSpec((1, LANES), lambda c: (0, 0))],
        out_shape=[jax.ShapeDtypeStruct((t, XBC), F32), jax.ShapeDtypeStruct((t, LANES), F32),
                   jax.ShapeDtypeStruct((1, LANES), F32)],
        scratch_shapes=[pltpu.VMEM((SSM_STATE, D_INNER), F32), pltpu.VMEM((LANES, ll), F32)],
        args=(xbc, dt, alog, sprev, dy, dskip), sem=("arbitrary",), comm=comm)


ADAM_TR = 512


def _adamw(parts, w, m, v, tr, name, comm=None):
    r, c = w.shape
    c1 = 1.0 - ADAM_B1 ** ADAM_STEP
    c2 = 1.0 - ADAM_B2 ** ADAM_STEP

    def body(p_ref, w_ref, m_ref, v_ref, g_ref, d_ref, mo_ref, vo_ref):
        g = p_ref[0].astype(F32)
        for k in range(1, N_DEV):
            g = g + p_ref[k].astype(F32)
        mn = ADAM_B1 * m_ref[...] + (1.0 - ADAM_B1) * g
        vn = ADAM_B2 * v_ref[...] + (1.0 - ADAM_B2) * (g * g)
        g_ref[...] = g
        mo_ref[...] = mn
        vo_ref[...] = vn
        d_ref[...] = -ADAM_LR * ((mn / c1) / (jnp.sqrt(vn / c2) + ADAM_EPS) + ADAM_WD * w_ref[...])

    row = pl.BlockSpec((tr, c), lambda i: (i, 0))
    sd = jax.ShapeDtypeStruct((r, c), F32)
    return _call(
        body, name=name, grid=(r // tr,),
        in_specs=[pl.BlockSpec((N_DEV, tr, c), lambda i: (0, i, 0)), row, row, row],
        out_specs=[row, row, row, row], out_shape=[sd, sd, sd, sd], args=(parts, w, m, v), sem=("parallel",), comm=comm)


def _peers():
    mx, my, mc = lax.axis_index("x"), lax.axis_index("y"), lax.axis_index("c")
    me = 4 * mx + 2 * my + mc
    out = []
    for k in range(1, N_DEV):
        px = 1 - mx if k & 4 else mx
        py = 1 - my if k & 2 else my
        pc = 1 - mc if k & 1 else mc
        out.append(((px, py, pc), 4 * px + 2 * py + pc))
    return me, out


class _Comm:
    def __init__(self, arrs, scatters):
        self.arrs, self.scatters, self.n = list(arrs), list(scatters), len(arrs)
        self.specs = [pl.BlockSpec(memory_space=pl.ANY)] * self.n
        self.out_shape = [jax.ShapeDtypeStruct(x.shape if sc else (N_DEV,) + x.shape, x.dtype)
                          for x, sc in zip(self.arrs, self.scatters)]
        np_ = N_DEV - 1
        self.scratch = [pltpu.SemaphoreType.DMA((np_ * self.n,)), pltpu.SemaphoreType.DMA((np_ * self.n,)),
                        pltpu.SemaphoreType.DMA((self.n,))]

    def _copies(self, x_refs, o_refs, sems):
        send_sems, recv_sems, local_sems = sems
        me, peers = _peers()
        np_ = N_DEV - 1
        local, sends, recvs = [], [], []
        for a in range(self.n):
            mine = x_refs[a].at[me] if self.scatters[a] else x_refs[a]
            local.append(pltpu.make_async_copy(mine, o_refs[a].at[me], local_sems.at[a]))
        for k, (dev, idx) in enumerate(peers):
            for a in range(self.n):
                mine = x_refs[a].at[me] if self.scatters[a] else x_refs[a]
                sends.append(pltpu.make_async_remote_copy(
                    src_ref=x_refs[a].at[idx] if self.scatters[a] else x_refs[a], dst_ref=o_refs[a].at[me],
                    send_sem=send_sems.at[a * np_ + k], recv_sem=recv_sems.at[a * np_ + k], device_id=dev, device_id_type=MESH))
                recvs.append(pltpu.make_async_remote_copy(
                    src_ref=mine, dst_ref=o_refs[a].at[idx], send_sem=send_sems.at[a * np_ + k],
                    recv_sem=recv_sems.at[a * np_ + k], device_id=dev, device_id_type=MESH))
        return local, sends, recvs

    def start(self, x_refs, o_refs, sems):
        local, sends, _ = self._copies(x_refs, o_refs, sems)
        for cp in local + sends:
            cp.start()

    def wait(self, x_refs, o_refs, sems):
        local, sends, recvs = self._copies(x_refs, o_refs, sems)
        for cp in recvs:
            cp.wait_recv()
        for cp in sends:
            cp.wait_send()
        for cp in local:
            cp.wait()


def _call(body, *, name, grid, in_specs, out_specs, out_shape, args, scratch_shapes=(), sem=None, comm=None):
    if comm is None:
        outs = pl.pallas_call(
            body, name=name, grid=grid, in_specs=list(in_specs), out_specs=list(out_specs), out_shape=list(out_shape),
            scratch_shapes=list(scratch_shapes), compiler_params=_cp(sem),
        )(*args)
        return list(outs), []
    n_in, n_out, nc = len(in_specs), len(out_specs), comm.n
    nsteps = 1
    for g in grid:
        nsteps *= g

    def carrier(*refs):
        ins, cin = refs[:n_in], refs[n_in:n_in + nc]
        outs, cout = refs[n_in + nc:n_in + nc + n_out], refs[n_in + nc + n_out:n_in + 2 * nc + n_out]
        rest = refs[n_in + 2 * nc + n_out:]
        scratch, sems = rest[:len(rest) - 3], rest[len(rest) - 3:]
        if nsteps == 1:
            comm.start(cin, cout, sems)
            body(*ins, *outs, *scratch)
            comm.wait(cin, cout, sems)
            return
        step = 0
        for d, g in enumerate(grid):
            step = step * g + pl.program_id(d)

        @pl.when(step == 0)
        def _():
            comm.start(cin, cout, sems)

        body(*ins, *outs, *scratch)

        @pl.when(step == nsteps - 1)
        def _():
            comm.wait(cin, cout, sems)

    outs = pl.pallas_call(
        carrier, name=name, grid=grid, in_specs=list(in_specs) + comm.specs, out_specs=list(out_specs) + comm.specs,
        out_shape=list(out_shape) + comm.out_shape, scratch_shapes=list(scratch_shapes) + comm.scratch,
        compiler_params=_cp(("arbitrary",) * len(grid) if grid else None),
    )(*args, *comm.arrs)
    return list(outs[:n_out]), list(outs[n_out:])


def _exchange(arrs, scatters, name):
    return _call(lambda *refs: None, name=name, grid=(), in_specs=[], out_specs=[], out_shape=[], args=[],
                 comm=_Comm(arrs, scatters))[1]


def _pack(arrs, dtype, lead=()):
    nl = len(lead)
    flat = jnp.concatenate([a.astype(dtype).reshape(lead + (-1,)) for a in arrs], axis=nl)
    n = flat.shape[-1]
    rows = -(-n // (LANES * ADAM_TR)) * ADAM_TR
    flat = jnp.pad(flat, [(0, 0)] * nl + [(0, rows * LANES - n)])
    return flat.reshape(lead + (rows, LANES))


def _unpack(flat, shapes, lead=()):
    nl = len(lead)
    flat = flat.reshape(lead + (-1,))
    out, o = [], 0
    for s in shapes:
        n = 1
        for d in s:
            n *= d
        out.append(lax.slice_in_dim(flat, o, o + n, axis=nl).reshape(lead + tuple(s)))
        o += n
    return out


def _join(g, ax):
    return jnp.concatenate([g[d] for d in range(N_DEV)], axis=ax)


def _split(full, ax):
    n = full.shape[ax] // N_DEV
    return jnp.stack([lax.slice_in_dim(full, d * n, (d + 1) * n, axis=ax) for d in range(N_DEV)])


_WEIGHTS = ['norm_mix', 'norm_ffn', 'attn_w_in', 'attn_w_out', 'relpos_table', 'q_norm_a', 'k_norm_a', 'q_norm_b',
            'k_norm_b', 'sinks', 'ssm_w_in', 'ssm_conv_w', 'ssm_conv_b', 'ssm_dt_bias', 'ssm_a_log', 'ssm_d', 'ssm_norm',
            'ssm_w_out', 'ffn_w_in', 'ffn_conv_w', 'ffn_conv_b', 'ffn_w_out']
_SHARD_AX = {'attn_w_in': 2, 'attn_w_out': 1, 'ssm_w_in': 2, 'ssm_conv_w': 2, 'ssm_conv_b': 1, 'ssm_norm': 1,
             'ssm_w_out': 1, 'ffn_w_in': 2, 'ffn_conv_w': 2, 'ffn_w_out': 1}
_BIG = ['attn_w_in', 'attn_w_out', 'ssm_w_in', 'ssm_w_out', 'ffn_w_in', 'ffn_w_out']
_SMALL = ['ssm_conv_w', 'ssm_conv_b', 'ssm_norm', 'ffn_conv_w']
_AX2 = {n: _SHARD_AX[n] - 1 for n in _BIG}
_REPL = [n for n in _WEIGHTS if n not in _SHARD_AX]


def _rows8(w):
    return jnp.pad(w, ((0, 8 - w.shape[0]), (0, 0)))


def _lanes128(v):
    return jnp.pad(v, (0, LANES - v.shape[0])).reshape(1, LANES)


def _band_mask(n_prev, pad):
    cq = jnp.arange(TQ)[:, None] // CHUNK
    ck = jnp.arange(pad + TQ)[None, :] // CHUNK
    return (ck >= cq) & (ck <= cq + n_prev)


def _pad_rows(a, pad):
    return jnp.pad(a, ((pad, 0), (0, 0)))


def _kv_expand(a):
    return jnp.concatenate([a[:, :HEAD_DIM]] * 4 + [a[:, HEAD_DIM:]] * 4, axis=1)


def _ffn_fwd(xin, g, w_in, w8, cb, tag):
    gu, h = _rms_mm(xin, g, w_in, f"mm_ffn_in{tag}", BF16)
    a = _ffn_mid_fwd(gu, w8, cb, f"ffn_mid{tag}")
    return a, (h, gu, a)


def _ffn_bwd(dx, dxb, xin, g, w_in_t, w8, cb, w_out_t, saved, tag):
    h, gu, a = saved
    dw_out = _mm(a, dxb, f"mm_ffn_dwout{tag}", trans_a=True)
    dgate, dup, dw8, dcb = _ffn_mid_bwd(gu, dxb, w_out_t, w8, cb, f"ffn_mid_bwd{tag}")
    dh = _mm(dgate, w_in_t[:D_FF], f"mm_ffn_dh_g{tag}")
    dw_in = jnp.concatenate([_mm(h, dgate, f"mm_ffn_dwin_g{tag}", trans_a=True),
                             _mm(h, dup, f"mm_ffn_dwin_u{tag}", trans_a=True)], axis=1)
    dxp, dxpb, dg = _mm_rms_bwd(dup, w_in_t[D_FF:], dh, xin, g, dx, f"mm_ffn_dh_u{tag}")
    return dxp, dxpb, dg, dw_in, dw8[:3], dcb, dw_out


def kernel(x, norm_mix, norm_ffn, attn_w_in, attn_w_out, relpos_table, q_norm_a, k_norm_a, q_norm_b, k_norm_b, sinks, ssm_w_in, ssm_conv_w, ssm_conv_b, ssm_dt_bias, ssm_a_log, ssm_d, ssm_norm, ssm_w_out, ffn_w_in, ffn_conv_w, ffn_conv_b, ffn_w_out, loss_target, m_norm_mix, m_norm_ffn, m_attn_w_in, m_attn_w_out, m_relpos_table, m_q_norm_a, m_k_norm_a, m_q_norm_b, m_k_norm_b, m_sinks, m_ssm_w_in, m_ssm_conv_w, m_ssm_conv_b, m_ssm_dt_bias, m_ssm_a_log, m_ssm_d, m_ssm_norm, m_ssm_w_out, m_ffn_w_in, m_ffn_conv_w, m_ffn_conv_b, m_ffn_w_out, v_norm_mix, v_norm_ffn, v_attn_w_in, v_attn_w_out, v_relpos_table, v_q_norm_a, v_k_norm_a, v_q_norm_b, v_k_norm_b, v_sinks, v_ssm_w_in, v_ssm_conv_w, v_ssm_conv_b, v_ssm_dt_bias, v_ssm_a_log, v_ssm_d, v_ssm_norm, v_ssm_w_out, v_ffn_w_in, v_ffn_conv_w, v_ffn_conv_b, v_ffn_w_out):
    w = dict(norm_mix=norm_mix, norm_ffn=norm_ffn, attn_w_in=attn_w_in, attn_w_out=attn_w_out, relpos_table=relpos_table,
             q_norm_a=q_norm_a, k_norm_a=k_norm_a, q_norm_b=q_norm_b, k_norm_b=k_norm_b, sinks=sinks, ssm_w_in=ssm_w_in,
             ssm_conv_w=ssm_conv_w, ssm_conv_b=ssm_conv_b, ssm_dt_bias=ssm_dt_bias, ssm_a_log=ssm_a_log, ssm_d=ssm_d,
             ssm_norm=ssm_norm, ssm_w_out=ssm_w_out, ffn_w_in=ffn_w_in, ffn_conv_w=ffn_conv_w, ffn_conv_b=ffn_conv_b,
             ffn_w_out=ffn_w_out)
    mom = dict(norm_mix=m_norm_mix, norm_ffn=m_norm_ffn, attn_w_in=m_attn_w_in, attn_w_out=m_attn_w_out,
               relpos_table=m_relpos_table, q_norm_a=m_q_norm_a, k_norm_a=m_k_norm_a, q_norm_b=m_q_norm_b,
               k_norm_b=m_k_norm_b, sinks=m_sinks, ssm_w_in=m_ssm_w_in, ssm_conv_w=m_ssm_conv_w, ssm_conv_b=m_ssm_conv_b,
               ssm_dt_bias=m_ssm_dt_bias, ssm_a_log=m_ssm_a_log, ssm_d=m_ssm_d, ssm_norm=m_ssm_norm, ssm_w_out=m_ssm_w_out,
               ffn_w_in=m_ffn_w_in, ffn_conv_w=m_ffn_conv_w, ffn_conv_b=m_ffn_conv_b, ffn_w_out=m_ffn_w_out)
    var = dict(norm_mix=v_norm_mix, norm_ffn=v_norm_ffn, attn_w_in=v_attn_w_in, attn_w_out=v_attn_w_out,
               relpos_table=v_relpos_table, q_norm_a=v_q_norm_a, k_norm_a=v_k_norm_a, q_norm_b=v_q_norm_b,
               k_norm_b=v_k_norm_b, sinks=v_sinks, ssm_w_in=v_ssm_w_in, ssm_conv_w=v_ssm_conv_w, ssm_conv_b=v_ssm_conv_b,
               ssm_dt_bias=v_ssm_dt_bias, ssm_a_log=v_ssm_a_log, ssm_d=v_ssm_d, ssm_norm=v_ssm_norm, ssm_w_out=v_ssm_w_out,
               ffn_w_in=v_ffn_w_in, ffn_conv_w=v_ffn_conv_w, ffn_conv_b=v_ffn_conv_b, ffn_w_out=v_ffn_w_out)

    def piece(n, l):
        return w[n][l].astype(BF16)

    def gather_of(names_layers):
        return _Comm([piece(n, l) for n, l in names_layers], [False] * len(names_layers))

    def joined(got, names_layers):
        return [_join(g, _AX2[n]) for g, (n, _) in zip(got, names_layers)]

    first = [('attn_w_in', 0), ('attn_w_out', 0)]
    got = _exchange([piece(n, l) for n, l in first] + [_pack([w[n] for n in _SMALL], F32)], [False] * 3, "gather_attn")
    w_attn_in, w_attn_out = joined(got[:2], first)
    full = {}
    for n, g in zip(_SMALL, _unpack(got[2], [w[n].shape for n in _SMALL], lead=(N_DEV,))):
        full[n] = _join(g, _SHARD_AX[n])
    ssm_cw8 = _rows8(full['ssm_conv_w'][0])
    ssm_cb = full['ssm_conv_b']
    ssm_nw = full['ssm_norm']
    ffn_cw8 = [_rows8(full['ffn_conv_w'][l]) for l in range(2)]
    ffn_cb = [ffn_conv_b[l:l + 1] for l in range(2)]

    x0 = x[0]
    target = loss_target[0]
    t = x0.shape[0]

    g_mix0, g_mix1 = norm_mix[0:1], norm_mix[1:2]
    g_ffn0, g_ffn1 = norm_ffn[0:1], norm_ffn[1:2]
    proj, h0 = _rms_mm(x0, g_mix0, w_attn_in, "mm_attn_in", F32)
    hn_w = jnp.concatenate([jnp.tile(v, (1, 2)) for v in (q_norm_a, k_norm_a, q_norm_b, k_norm_b)], axis=0)
    qa, ka, va, qb, kb, vb = _headnorm_fwd(proj, hn_w, "headnorm")
    table = jnp.pad(relpos_table[0], ((0, 0), (0, REL_W - (2 * MAX_REL + 1))))
    bias_a = jnp.where(_band_mask(A_PREV, PAD_A)[None], jnp.transpose(_relpos_fwd(table, "relpos_bias"), (1, 0, 2)), NEG)
    rel_b = jnp.arange(TQ)[:, None] - (jnp.arange(PAD_B + TQ)[None, :] - PAD_B)
    slopes = 2.0 ** (-8.0 * jnp.arange(1, N_HEADS + 1, dtype=F32) / N_HEADS)
    bias_b = jnp.where(_band_mask(B_PREV, PAD_B)[None], -slopes[:, None, None] * jnp.abs(rel_b).astype(F32)[None], NEG)
    no_sinks = jnp.full((N_HEADS,), NEG, F32)
    kpa, vpa = _pad_rows(ka, PAD_A), _pad_rows(va, PAD_A)
    kpb, vpb = _pad_rows(_kv_expand(kb), PAD_B), _pad_rows(_kv_expand(vb), PAD_B)
    ffn0_w, ssm_w, ffn1_w = [('ffn_w_in', 0), ('ffn_w_out', 0)], [('ssm_w_in', 0), ('ssm_w_out', 0)], [('ffn_w_in', 1), ('ffn_w_out', 1)]
    oa, got = _attn_fwd(qa, kpa, vpa, bias_a, no_sinks, PAD_A, "attn_a", comm=gather_of(ffn0_w))
    w_ffn_in0, w_ffn_out0 = joined(got, ffn0_w)
    ob, got = _attn_fwd(qb, kpb, vpb, bias_b, sinks[0], PAD_B, "attn_b", comm=gather_of(ssm_w))
    w_ssm_in, w_ssm_out = joined(got, ssm_w)
    w_ssm_main = w_ssm_in[:, :ZX]
    w_ssm_dt = jnp.pad(w_ssm_in[:, ZX:], ((0, 0), (0, LANES - SSM_HEADS)))
    x1 = _mm(oa, w_attn_out[:512], "mm_attn_out_a", res=x0)
    x1 = _mm(ob, w_attn_out[512:], "mm_attn_out_b", res=x1)
    a0, ffn0_saved = _ffn_fwd(x1, g_ffn0, w_ffn_in0, ffn_cw8[0], ffn_cb[0], "0")
    x2 = _mm(a0, w_ffn_out0, "mm_ffn_out0", res=x1)

    zx, h2 = _rms_mm(x2, g_mix1, w_ssm_main, "mm_ssm_in", F32)
    dtraw = _mm(h2, w_ssm_dt, "mm_ssm_dt")
    dt_bias = _lanes128(ssm_dt_bias[0])
    alog = _lanes128(ssm_a_log[0])
    dexp = jnp.repeat(ssm_d[0], HEAD_DIM).reshape(1, D_INNER)
    xbc = _ssm_pre_fwd(zx, ssm_cw8, ssm_cb, "ssm_pre")
    dt = _dt_fwd(dtraw, dt_bias, "ssm_dt")
    (y, sprev), got = _ssd_fwd(xbc, dt, alog, "ssd_fwd", comm=gather_of(ffn1_w))
    w_ffn_in1, w_ffn_out1 = joined(got, ffn1_w)
    y4 = _ssm_post_fwd(y, xbc, zx, dexp, ssm_nw, "ssm_post")
    x3 = _mm(y4, w_ssm_out, "mm_ssm_out", res=x2)
    a1, ffn1_saved = _ffn_fwd(x3, g_ffn1, w_ffn_in1, ffn_cw8[1], ffn_cb[1], "1")

    dx4, dx4b, sq = _mm_loss(a1, w_ffn_out1, x3, target, "mm_ffn_out1_loss")
    loss = lax.psum(0.5 * jnp.sum(sq) / D_MODEL, ("x", "y", "c"))

    grads = {}

    def scatter_of(pieces):
        return _Comm([_split(g, _AX2[n]).astype(BF16) for n, g in pieces], [True] * len(pieces))

    dx3, dx3b, dg_ffn1, dwin1, dcw1, dcb1, dwout1 = _ffn_bwd(
        dx4, dx4b, x3, g_ffn1, w_ffn_in1.T, ffn_cw8[1], ffn_cb[1], w_ffn_out1.T, ffn1_saved, "1")

    dy4 = _mm(dx3b, w_ssm_out.T, "mm_ssm_dy")
    dw_ssm_out = _mm(y4, dx3b, "mm_ssm_dwout", trans_a=True)
    dyv, dskip, dz, dd_lane, dnw = _ssm_post_bwd(dy4, y, xbc, zx, dexp, ssm_nw, "ssm_post_bwd")
    (dxbc, ddt, dalog), parts_ffn1 = _ssd_bwd(xbc, dt, alog, sprev, dyv, dskip, "ssd_bwd",
                                              comm=scatter_of([('ffn_w_in', dwin1), ('ffn_w_out', dwout1)]))
    dxr, dcw_s, dcb_s = _ssm_pre_bwd(zx, dxbc, ssm_cw8, ssm_cb, "ssm_pre_bwd")
    ddtraw, ddtb = _dt_bwd(dtraw, dt_bias, ddt, "ssm_dt_bwd")
    w_main_t = w_ssm_main.T
    dh2 = _mm(dz, w_main_t[:D_INNER], "mm_ssm_dh_z")
    dh2 = _mm(dxr, w_main_t[D_INNER:], "mm_ssm_dh_x", res=dh2)
    dw_ssm_in = jnp.concatenate([
        _mm(h2, dz, "mm_ssm_dwin_z", trans_a=True), _mm(h2, dxr, "mm_ssm_dwin_x", trans_a=True),
        _mm(h2, ddtraw, "mm_ssm_dwin_dt", trans_a=True)[:, :SSM_HEADS]], axis=1)
    dx2, dx2b, dg_mix1 = _mm_rms_bwd(ddtraw, w_ssm_dt.T, dh2, x2, g_mix1, dx3, "mm_ssm_dh_dt")
    grads['ssm_conv_w'] = dcw_s[:4][None]
    grads['ssm_conv_b'] = dcb_s
    grads['ssm_norm'] = dnw
    grads['ssm_dt_bias'] = ddtb[:, :SSM_HEADS]
    grads['ssm_a_log'] = dalog[:, :SSM_HEADS]
    grads['ssm_d'] = jnp.sum(dd_lane.reshape(SSM_HEADS, HEAD_DIM), axis=1)[None]

    dx1, dx1b, dg_ffn0, dwin0, dcw0, dcb0, dwout0 = _ffn_bwd(
        dx2, dx2b, x1, g_ffn0, w_ffn_in0.T, ffn_cw8[0], ffn_cb[0], w_ffn_out0.T, ffn0_saved, "0")
    grads['ffn_conv_w'] = jnp.stack([dcw0, dcw1])
    grads['ffn_conv_b'] = jnp.concatenate([dcb0, dcb1], axis=0)
    grads['norm_ffn'] = jnp.concatenate([dg_ffn0, dg_ffn1], axis=0)

    do = _mm(dx1b, w_attn_out.T, "mm_attn_do", out_dtype=BF16)
    dw_attn_out = jnp.concatenate([_mm(oa, dx1b, "mm_attn_dwout_a", trans_a=True),
                                   _mm(ob, dx1b, "mm_attn_dwout_b", trans_a=True)], axis=0)
    (dqa, dkpa, dvpa, dbias_a, _), parts_ssm = _attn_bwd(
        qa, kpa, vpa, bias_a, no_sinks, do, 0, PAD_A, "attn_a_bwd",
        comm=scatter_of([('ssm_w_in', dw_ssm_in), ('ssm_w_out', dw_ssm_out), ('attn_w_out', dw_attn_out)]))
    (dqb, dkpb, dvpb, _, dsink), parts_ffn0 = _attn_bwd(
        qb, kpb, vpb, bias_b, sinks[0], do, 4, PAD_B, "attn_b_bwd",
        comm=scatter_of([('ffn_w_in', dwin0), ('ffn_w_out', dwout0)]))
    grads['relpos_table'] = _relpos_bwd(jnp.transpose(dbias_a, (1, 0, 2)), "relpos_bwd")[None, :, :2 * MAX_REL + 1]
    grads['sinks'] = dsink[:, :2, 0].reshape(1, N_HEADS)
    dproj, dhn = _headnorm_bwd(proj, hn_w, dqa, dkpa, dvpa, dqb, dkpb, dvpb, "headnorm_bwd")
    dhn = dhn[:, :HEAD_DIM] + dhn[:, HEAD_DIM:]
    for k, n in enumerate(('q_norm_a', 'k_norm_a', 'q_norm_b', 'k_norm_b')):
        grads[n] = dhn[k:k + 1]
    dw_attn_in = _mm(h0, dproj, "mm_attn_dwin", trans_a=True)
    dx0, _, dg_mix0, parts_attn_in = _mm_rms_bwd(dproj, w_attn_in.T, None, x0, g_mix0, dx1, "mm_attn_dh",
                                                 comm=scatter_of([('attn_w_in', dw_attn_in)]))
    grads['norm_mix'] = jnp.concatenate([dg_mix0, dg_mix1], axis=0)

    def adam_piece(n, l, parts):
        rows = w[n][l].shape[0]
        return _adamw(parts, w[n][l], mom[n][l], var[n][l], _pick(rows, (256, 128, 64)), f"adamw_{n}{l}")[0]

    sm_shapes = [w[n].shape for n in _SMALL]
    rp_shapes = [w[n].shape for n in _REPL]
    recv = _exchange(
        [_pack([_split(grads[n], _SHARD_AX[n]) for n in _SMALL], F32, lead=(N_DEV,)), _pack([grads[n] for n in _REPL], F32)],
        [True, False], "exchange_small")
    res = [{}, {}, {}, {}]
    by_piece = {
        ('ffn_w_in', 1): adam_piece('ffn_w_in', 1, parts_ffn1[0]), ('ffn_w_out', 1): adam_piece('ffn_w_out', 1, parts_ffn1[1]),
        ('ssm_w_in', 0): adam_piece('ssm_w_in', 0, parts_ssm[0]), ('ssm_w_out', 0): adam_piece('ssm_w_out', 0, parts_ssm[1]),
        ('attn_w_out', 0): adam_piece('attn_w_out', 0, parts_ssm[2]),
        ('ffn_w_in', 0): adam_piece('ffn_w_in', 0, parts_ffn0[0]), ('ffn_w_out', 0): adam_piece('ffn_w_out', 0, parts_ffn0[1]),
        ('attn_w_in', 0): adam_piece('attn_w_in', 0, parts_attn_in[0]),
    }
    for n in _BIG:
        for kind in range(4):
            res[kind][n] = jnp.stack([by_piece[n, l][kind] for l in range(w[n].shape[0])])
    for names, shapes, parts in ((_SMALL, sm_shapes, recv[0]), (_REPL, rp_shapes, recv[1])):
        outs, _ = _adamw(parts, _pack([w[n] for n in names], F32), _pack([mom[n] for n in names], F32),
                         _pack([var[n] for n in names], F32), ADAM_TR, "adamw_" + ("small" if names is _SMALL else "replicated"))
        for kind, flat in enumerate(outs):
            for n, a in zip(names, _unpack(flat, shapes)):
                res[kind][n] = a
    return (loss, dx0[None], *[res[0][n] for n in _WEIGHTS], *[res[1][n] for n in _WEIGHTS],
            *[res[2][n] for n in _WEIGHTS], *[res[3][n] for n in _WEIGHTS])
```

```python
import jax
import jax.numpy as jnp
from jax import lax
from jax.experimental import pallas as pl
from jax.experimental.pallas import tpu as pltpu

F32 = jnp.float32
BF16 = jnp.bfloat16
HI = lax.Precision.HIGHEST
MESH = pl.DeviceIdType.MESH
NEG = -1e30

N_DEV = 8
D_MODEL = 1024
EPS = 1e-6
CHUNK = 64
HEAD_DIM = 64
N_HEADS = 8
A_PREV = 8
B_PREV = 2
MAX_REL = 256
TQ = 2 * CHUNK
ATT_SUB = 4
PAD_A = A_PREV * CHUNK
PAD_B = B_PREV * CHUNK
REL_W = PAD_A + TQ
D_INNER = 2048
SSM_HEADS = 32
SSM_GROUPS = 4
SSM_STATE = 128
XBC = D_INNER + 2 * SSM_GROUPS * SSM_STATE
ZX = D_INNER + XBC
D_FF = 2816
SSD_L = 128
LANES = 128
VMEM_LIMIT = 56 << 20

ADAM_LR, ADAM_B1, ADAM_B2, ADAM_EPS, ADAM_WD, ADAM_STEP = 0.001, 0.9, 0.999, 1e-08, 0.01, 10


def _cp(sem=None):
    return pltpu.CompilerParams(dimension_semantics=sem, vmem_limit_bytes=VMEM_LIMIT)


def _dot(a, b, ca=1, cb=0, prec=None):
    return lax.dot_general(a, b, (((ca,), (cb,)), ((), ())), preferred_element_type=F32, precision=prec)


def _pick(n, cands):
    for c in cands:
        if n % c == 0:
            return c
    return n


def _lo_mask():
    return lax.broadcasted_iota(jnp.int32, (1, LANES), 1) < HEAD_DIM


def _mm(a, b, name, out_dtype=F32, res=None, trans_a=False):
    n = b.shape[1]
    tn = _pick(n, (1408, 1536, 1152, 1024, 512, 256, 128))
    if trans_a:
        kdim, m = a.shape
        assert b.shape[0] == kdim and res is None and out_dtype == F32, (a.shape, b.shape)
        tk = _pick(kdim, (512, 256, 128))

        def body_t(a_ref, b_ref, o_ref):
            @pl.when(pl.program_id(0) == 0)
            def _():
                o_ref[...] = jnp.zeros_like(o_ref)

            av = a_ref[...]
            for c in range(0, n, tn):
                o_ref[:, c:c + tn] += _dot(av, b_ref[:, c:c + tn], 0, 0)

        return pl.pallas_call(
            body_t, name=name, grid=(kdim // tk,),
            in_specs=[pl.BlockSpec((tk, m), lambda k: (k, 0)), pl.BlockSpec((tk, n), lambda k: (k, 0))],
            out_specs=pl.BlockSpec((m, n), lambda k: (0, 0)), out_shape=jax.ShapeDtypeStruct((m, n), F32),
            compiler_params=_cp(("arbitrary",)),
        )(a, b)

    m, kdim = a.shape
    assert b.shape[0] == kdim, (a.shape, b.shape)
    tm = _pick(m, (256, 128) if n > 2304 else (512, 256, 128))

    def body(*refs):
        if res is None:
            a_ref, b_ref, o_ref = refs
        else:
            a_ref, b_ref, r_ref, o_ref = refs
        av = a_ref[...]
        for c in range(0, n, tn):
            r = _dot(av, b_ref[:, c:c + tn], 1, 0)
            if res is not None:
                r = r + r_ref[:, c:c + tn]
            o_ref[:, c:c + tn] = r.astype(out_dtype)

    in_specs = [pl.BlockSpec((tm, kdim), lambda i: (i, 0)), pl.BlockSpec((kdim, n), lambda i: (0, 0))]
    args = [a, b]
    if res is not None:
        in_specs.append(pl.BlockSpec((tm, n), lambda i: (i, 0)))
        args.append(res)
    return pl.pallas_call(
        body, name=name, grid=(m // tm,), in_specs=in_specs, out_specs=pl.BlockSpec((tm, n), lambda i: (i, 0)),
        out_shape=jax.ShapeDtypeStruct((m, n), out_dtype), compiler_params=_cp(("parallel",)),
    )(*args)


def _rms_mm(x, g, b, name, out_dtype):
    t, d = x.shape
    n = b.shape[1]
    tn = _pick(n, (1408, 1536, 1152, 1024, 512, 256, 128))
    tm = _pick(t, (256, 128))

    def body(x_ref, g_ref, b_ref, o_ref, h_ref):
        xv = x_ref[...]
        r = lax.rsqrt(jnp.mean(xv * xv, axis=-1, keepdims=True) + EPS)
        h = (xv * r * g_ref[...]).astype(BF16)
        h_ref[...] = h
        for c in range(0, n, tn):
            o_ref[:, c:c + tn] = _dot(h, b_ref[:, c:c + tn], 1, 0).astype(out_dtype)

    row = pl.BlockSpec((tm, d), lambda i: (i, 0))
    return pl.pallas_call(
        body, name=name, grid=(t // tm,),
        in_specs=[row, pl.BlockSpec((1, d), lambda i: (0, 0)), pl.BlockSpec((d, n), lambda i: (0, 0))],
        out_specs=[pl.BlockSpec((tm, n), lambda i: (i, 0)), row],
        out_shape=[jax.ShapeDtypeStruct((t, n), out_dtype), jax.ShapeDtypeStruct((t, d), BF16)],
        compiler_params=_cp(("parallel",)),
    )(x, g, b)


def _mm_rms_bwd(a, b, dh_prev, x, g, dres, name, comm=None):
    t, d = x.shape
    kdim = a.shape[1]
    tm = _pick(t, (256, 128))

    def body(*refs):
        if dh_prev is None:
            a_ref, b_ref, x_ref, g_ref, dr_ref, dx_ref, dxb_ref, dg_ref = refs
            dhv = _dot(a_ref[...], b_ref[...], 1, 0)
        else:
            a_ref, b_ref, p_ref, x_ref, g_ref, dr_ref, dx_ref, dxb_ref, dg_ref = refs
            dhv = _dot(a_ref[...], b_ref[...], 1, 0) + p_ref[...]
        xv = x_ref[...]
        r = lax.rsqrt(jnp.mean(xv * xv, axis=-1, keepdims=True) + EPS)
        xh = xv * r
        dxh = dhv * g_ref[...]
        dx = dr_ref[...] + r * (dxh - xh * jnp.mean(dxh * xh, axis=-1, keepdims=True))
        dx_ref[...] = dx
        dxb_ref[...] = dx.astype(BF16)

        @pl.when(pl.program_id(0) == 0)
        def _():
            dg_ref[...] = jnp.zeros_like(dg_ref)

        dg_ref[...] += jnp.sum(dhv * xh, axis=0, keepdims=True)

    row = pl.BlockSpec((tm, d), lambda i: (i, 0))
    vec = pl.BlockSpec((1, d), lambda i: (0, 0))
    in_specs = [pl.BlockSpec((tm, kdim), lambda i: (i, 0)), pl.BlockSpec((kdim, d), lambda i: (0, 0))]
    args = [a, b]
    if dh_prev is not None:
        in_specs.append(row)
        args.append(dh_prev)
    outs, got = _call(
        body, name=name, grid=(t // tm,), in_specs=in_specs + [row, vec, row], out_specs=[row, row, vec],
        out_shape=[jax.ShapeDtypeStruct((t, d), F32), jax.ShapeDtypeStruct((t, d), BF16), jax.ShapeDtypeStruct((1, d), F32)],
        args=(*args, x, g, dres), sem=("arbitrary",), comm=comm)
    return (*outs, got) if comm is not None else tuple(outs)


def _mm_loss(a, b, res, target, name):
    t, kdim = a.shape
    d = b.shape[1]
    tm = _pick(t, (512, 256, 128))

    def body(a_ref, b_ref, r_ref, t_ref, dy_ref, dyb_ref, acc_ref):
        @pl.when(pl.program_id(0) == 0)
        def _():
            acc_ref[...] = jnp.zeros_like(acc_ref)

        err = _dot(a_ref[...], b_ref[...], 1, 0) + r_ref[...] - t_ref[...]
        dy = err * (1.0 / d)
        dy_ref[...] = dy
        dyb_ref[...] = dy.astype(BF16)
        acc_ref[...] += jnp.sum(err * err, axis=0, keepdims=True)

    row = pl.BlockSpec((tm, d), lambda i: (i, 0))
    vec = pl.BlockSpec((1, d), lambda i: (0, 0))
    return pl.pallas_call(
        body, name=name, grid=(t // tm,),
        in_specs=[pl.BlockSpec((tm, kdim), lambda i: (i, 0)), pl.BlockSpec((kdim, d), lambda i: (0, 0)), row, row],
        out_specs=[row, row, vec],
        out_shape=[jax.ShapeDtypeStruct((t, d), F32), jax.ShapeDtypeStruct((t, d), BF16), jax.ShapeDtypeStruct((1, d), F32)],
        compiler_params=_cp(("arbitrary",)),
    )(a, b, res, target)


def _head_rms(xs, w, lo):
    sq = xs * xs
    s0 = jnp.sum(jnp.where(lo, sq, 0.0), axis=-1, keepdims=True)
    s1 = jnp.sum(jnp.where(lo, 0.0, sq), axis=-1, keepdims=True)
    r = jnp.where(lo, lax.rsqrt(s0 * (1.0 / HEAD_DIM) + EPS), lax.rsqrt(s1 * (1.0 / HEAD_DIM) + EPS))
    return xs * r, r


def _head_rms_bwd(xs, w, dy, lo):
    xh, r = _head_rms(xs, w, lo)
    dxh = dy * w
    t = dxh * xh
    m0 = jnp.sum(jnp.where(lo, t, 0.0), axis=-1, keepdims=True)
    m1 = jnp.sum(jnp.where(lo, 0.0, t), axis=-1, keepdims=True)
    mm = jnp.where(lo, m0, m1) * (1.0 / HEAD_DIM)
    return r * (dxh - xh * mm), dy * xh


_QSCALE = HEAD_DIM ** -0.5


def _headnorm_fwd(proj, ws, name):
    t = proj.shape[0]
    tm = TQ
    lead = PAD_A // tm
    leadb = PAD_B // tm

    def body(p_ref, w_ref, qa_ref, ka_ref, va_ref, qb_ref, kb_ref, vb_ref):
        data = pl.program_id(0) >= lead
        lo = _lo_mask()

        def put(ref, c, val):
            ref[:, c:c + val.shape[1]] = jnp.where(data, val, 0.0).astype(BF16)

        def per_query_head(slab):
            other = pltpu.roll(slab, HEAD_DIM, 1)
            e0, e1 = jnp.where(lo, slab, other), jnp.where(lo, other, slab)
            return jnp.concatenate([e0, e0, e1, e1], axis=1)

        for s in range(4):
            c = LANES * s
            xh, _ = _head_rms(p_ref[:, c:c + LANES], None, lo)
            qa_ref[:, c:c + LANES] = (xh * w_ref[0:1, :] * _QSCALE).astype(BF16)
            xh, _ = _head_rms(p_ref[:, 512 + c:512 + c + LANES], None, lo)
            put(ka_ref, c, xh * w_ref[1:2, :])
            xh, _ = _head_rms(p_ref[:, 1536 + c:1536 + c + LANES], None, lo)
            qb_ref[:, c:c + LANES] = (xh * w_ref[2:3, :] * _QSCALE).astype(BF16)
        put(va_ref, 0, p_ref[:, 1024:1536])
        xh, _ = _head_rms(p_ref[:, 2048:2176], None, lo)
        put(kb_ref, 0, per_query_head(xh * w_ref[3:4, :]))
        put(vb_ref, 0, per_query_head(p_ref[:, 2176:2304]))

    src = lambda i: jnp.maximum(i - lead, 0)
    wide = pl.BlockSpec((tm, 512), lambda i: (src(i), 0))
    pad_a = pl.BlockSpec((tm, 512), lambda i: (i, 0))
    pad_b = pl.BlockSpec((tm, 512), lambda i: (jnp.maximum(i - lead + leadb, 0), 0))
    sd = lambda rows: jax.ShapeDtypeStruct((rows, 512), BF16)
    return pl.pallas_call(
        body, name=name, grid=(t // tm + lead,),
        in_specs=[pl.BlockSpec((tm, 2304), lambda i: (src(i), 0)), pl.BlockSpec((4, LANES), lambda i: (0, 0))],
        out_specs=[wide, pad_a, pad_a, wide, pad_b, pad_b],
        out_shape=[sd(t), sd(t + PAD_A), sd(t + PAD_A), sd(t), sd(t + PAD_B), sd(t + PAD_B)],
        compiler_params=_cp(("arbitrary",)),
    )(proj, ws)


def _headnorm_bwd(proj, ws, dqa, dkpa, dvpa, dqb, dkpb, dvpb, name):
    t = proj.shape[0]
    tm = TQ
    offa, offb = PAD_A // tm, PAD_B // tm

    def body(p_ref, w_ref, dqa_ref, dka_ref, dva_ref, dqb_ref, dkb_ref, dvb_ref, dp_ref, dw_ref):
        i = pl.program_id(0)
        lo = _lo_mask()

        @pl.when(i == 0)
        def _():
            dw_ref[...] = jnp.zeros_like(dw_ref)

        acc = [jnp.zeros((1, LANES), F32) for _ in range(4)]
        for s in range(4):
            c = LANES * s
            dx, dwl = _head_rms_bwd(p_ref[:, c:c + LANES], w_ref[0:1, :], dqa_ref[:, c:c + LANES] * _QSCALE, lo)
            dp_ref[:, c:c + LANES] = dx.astype(BF16)
            acc[0] += jnp.sum(dwl, axis=0, keepdims=True)
            dx, dwl = _head_rms_bwd(p_ref[:, 512 + c:512 + c + LANES], w_ref[1:2, :], dka_ref[:, c:c + LANES], lo)
            dp_ref[:, 512 + c:512 + c + LANES] = dx.astype(BF16)
            acc[1] += jnp.sum(dwl, axis=0, keepdims=True)
            dx, dwl = _head_rms_bwd(p_ref[:, 1536 + c:1536 + c + LANES], w_ref[2:3, :], dqb_ref[:, c:c + LANES] * _QSCALE, lo)
            dp_ref[:, 1536 + c:1536 + c + LANES] = dx.astype(BF16)
            acc[2] += jnp.sum(dwl, axis=0, keepdims=True)
        dp_ref[:, 1024:1536] = dva_ref[...].astype(BF16)

        def group_sum(ref):
            s0 = ref[:, 0:128] + ref[:, 128:256]
            s1 = ref[:, 256:384] + ref[:, 384:512]
            s0 = s0 + pltpu.roll(s0, HEAD_DIM, 1)
            s1 = s1 + pltpu.roll(s1, HEAD_DIM, 1)
            return jnp.where(lo, s0, s1)

        dx, dwl = _head_rms_bwd(p_ref[:, 2048:2176], w_ref[3:4, :], group_sum(dkb_ref), lo)
        dp_ref[:, 2048:2176] = dx.astype(BF16)
        acc[3] += jnp.sum(dwl, axis=0, keepdims=True)
        dp_ref[:, 2176:2304] = group_sum(dvb_ref).astype(BF16)
        for n in range(4):
            dw_ref[n:n + 1, :] += acc[n]

    wide = pl.BlockSpec((tm, 512), lambda i: (i, 0))
    pa = pl.BlockSpec((tm, 512), lambda i: (i + offa, 0))
    pb = pl.BlockSpec((tm, 512), lambda i: (i + offb, 0))
    return pl.pallas_call(
        body, name=name, grid=(t // tm,),
        in_specs=[pl.BlockSpec((tm, 2304), lambda i: (i, 0)), pl.BlockSpec((4, LANES), lambda i: (0, 0)),
                  wide, pa, pa, wide, pb, pb],
        out_specs=[pl.BlockSpec((tm, 2304), lambda i: (i, 0)), pl.BlockSpec((4, LANES), lambda i: (0, 0))],
        out_shape=[jax.ShapeDtypeStruct((t, 2304), BF16), jax.ShapeDtypeStruct((4, LANES), F32)],
        compiler_params=_cp(("arbitrary",)),
    )(proj, ws, dqa, dkpa, dvpa, dqb, dkpb, dvpb)


ROLL_W = 1024


def _rel_onehot():
    r_io = lax.broadcasted_iota(jnp.int32, (REL_W, ROLL_W), 0)
    m_io = lax.broadcasted_iota(jnp.int32, (REL_W, ROLL_W), 1)
    return (r_io == jnp.clip(REL_W - 1 - m_io, -MAX_REL, MAX_REL) + MAX_REL).astype(F32)


def _relpos_fwd(table, name):
    def body(t_ref, o_ref):
        rr = _dot(t_ref[...], _rel_onehot(), 1, 0, HI)

        def step(q, c):
            o_ref[q] = pltpu.roll(rr, (ROLL_W - (TQ - 1) + q) % ROLL_W, 1)[:, :REL_W]
            return c

        lax.fori_loop(0, TQ, step, 0)

    return pl.pallas_call(
        body, name=name, out_shape=jax.ShapeDtypeStruct((TQ, N_HEADS, REL_W), F32),
        in_specs=[pl.BlockSpec(memory_space=pltpu.VMEM)], out_specs=pl.BlockSpec(memory_space=pltpu.VMEM),
        compiler_params=_cp(),
    )(table)


def _relpos_bwd(dbias_t, name):
    def body(d_ref, o_ref):
        def step(q, acc):
            row = jnp.concatenate([d_ref[q], jnp.zeros((N_HEADS, ROLL_W - REL_W), F32)], axis=1)
            return acc + pltpu.roll(row, TQ - 1 - q, 1)

        drr = lax.fori_loop(0, TQ, step, jnp.zeros((N_HEADS, ROLL_W), F32))
        o_ref[...] = _dot(drr, _rel_onehot(), 1, 1, HI)

    return pl.pallas_call(
        body, name=name, out_shape=jax.ShapeDtypeStruct((N_HEADS, REL_W), F32),
        in_specs=[pl.BlockSpec(memory_space=pltpu.VMEM)], out_specs=pl.BlockSpec(memory_space=pltpu.VMEM),
        compiler_params=_cp(),
    )(dbias_t)


def _attn_probs(qe, kw, bias, kvalid, snk):
    s = _dot(qe, kw, 1, 1) + bias
    s = jnp.where(kvalid, s, NEG)
    m = jnp.maximum(jnp.max(s, axis=-1, keepdims=True), snk)
    p = jnp.exp(s - m)
    inv = 1.0 / (jnp.sum(p, axis=-1, keepdims=True) + jnp.exp(snk - m))
    return p * inv, jnp.exp(snk - m) * inv


def _attn_fwd(q, kp, vp, bias, sinks, pad, name, comm=None):
    t, hd = q.shape
    w = pad + TQ

    def body(sink_ref, q_ref, k_ref, v_ref, b_ref, o_ref):
        hp, i = pl.program_id(0), pl.program_id(1)
        lo = _lo_mask()
        for j in range(ATT_SUB):
            start = pl.multiple_of((i * ATT_SUB + j) * TQ, TQ)
            qv = q_ref[TQ * j:TQ * (j + 1), :]
            kw = k_ref[pl.ds(start, w), :]
            vw = v_ref[pl.ds(start, w), :]
            kvalid = (start + lax.broadcasted_iota(jnp.int32, (1, w), 1)) >= pad
            outs = []
            for e in range(2):
                sel = lo if e == 0 else jnp.logical_not(lo)
                qe = jnp.where(sel, qv, jnp.zeros_like(qv))
                p, _ = _attn_probs(qe, kw, b_ref[e], kvalid, sink_ref[2 * hp + e])
                outs.append(_dot(p.astype(BF16), vw, 1, 0))
            o_ref[TQ * j:TQ * (j + 1), :] = jnp.where(lo, outs[0], outs[1]).astype(BF16)

    full = pl.BlockSpec((t + pad, LANES), lambda h, i: (0, h))
    tile = pl.BlockSpec((ATT_SUB * TQ, LANES), lambda h, i: (i, h))
    (o,), got = _call(
        body, name=name, grid=(hd // LANES, t // (ATT_SUB * TQ)),
        in_specs=[pl.BlockSpec(memory_space=pltpu.SMEM), tile, full, full, pl.BlockSpec((2, TQ, w), lambda h, i: (h, 0, 0))],
        out_specs=[tile], out_shape=[jax.ShapeDtypeStruct((t, hd), BF16)],
        args=(sinks, q, kp, vp, bias), sem=("parallel", "arbitrary"), comm=comm)
    return o, got


def _attn_bwd(q, kp, vp, bias, sinks, do, col_off, pad, name, comm=None):
    t, hd = q.shape
    w = pad + TQ
    nhp = hd // LANES

    def body(sink_ref, q_ref, k_ref, v_ref, b_ref, do_ref, dq_ref, dk_ref, dv_ref, db_ref, ds_ref):
        hp, i = pl.program_id(0), pl.program_id(1)

        @pl.when(i == 0)
        def _():
            dk_ref[...] = jnp.zeros_like(dk_ref)
            dv_ref[...] = jnp.zeros_like(dv_ref)
            db_ref[...] = jnp.zeros_like(db_ref)
            ds_ref[...] = jnp.zeros_like(ds_ref)

        lo = _lo_mask()
        row8 = lax.broadcasted_iota(jnp.int32, (8, LANES), 0)
        dbias = [None, None]
        dsink = jnp.zeros((8, LANES), F32)
        for j in range(ATT_SUB):
            start = pl.multiple_of((i * ATT_SUB + j) * TQ, TQ)
            qv = q_ref[TQ * j:TQ * (j + 1), :]
            dov = do_ref[TQ * j:TQ * (j + 1), :]
            kw = k_ref[pl.ds(start, w), :]
            vw = v_ref[pl.ds(start, w), :]
            kvalid = (start + lax.broadcasted_iota(jnp.int32, (1, w), 1)) >= pad
            dqs, dkw, dvw = [], None, None
            for e in range(2):
                sel = lo if e == 0 else jnp.logical_not(lo)
                qe = jnp.where(sel, qv, jnp.zeros_like(qv))
                doe = jnp.where(sel, dov, jnp.zeros_like(dov))
                p, psink = _attn_probs(qe, kw, b_ref[e], kvalid, sink_ref[2 * hp + e])
                dp = _dot(doe, vw, 1, 1)
                delta = jnp.sum(p * dp, axis=-1, keepdims=True)
                ds = p * (dp - delta)
                dbias[e] = ds if dbias[e] is None else dbias[e] + ds
                dsink = dsink + jnp.where(row8 == e, jnp.sum(-psink * delta, axis=0, keepdims=True), 0.0)
                dsb = ds.astype(BF16)
                dqs.append(_dot(dsb, kw, 1, 0))
                dk_e = _dot(dsb, qe, 0, 0)
                dv_e = _dot(p.astype(BF16), doe, 0, 0)
                dkw = dk_e if dkw is None else dkw + dk_e
                dvw = dv_e if dvw is None else dvw + dv_e
            dq_ref[TQ * j:TQ * (j + 1), :] = jnp.where(lo, dqs[0], dqs[1])
            dk_ref[pl.ds(start, w), :] += dkw
            dv_ref[pl.ds(start, w), :] += dvw
        for e in range(2):
            db_ref[e] += dbias[e]
        ds_ref[0] += dsink

    full = pl.BlockSpec((t + pad, LANES), lambda h, i: (0, h))
    tile = pl.BlockSpec((ATT_SUB * TQ, LANES), lambda h, i: (i, h))
    btile = pl.BlockSpec((2, TQ, w), lambda h, i: (h, 0, 0))
    return _call(
        body, name=name, grid=(nhp, t // (ATT_SUB * TQ)),
        in_specs=[pl.BlockSpec(memory_space=pltpu.SMEM), tile, full, full, btile,
                  pl.BlockSpec((ATT_SUB * TQ, LANES), lambda h, i: (i, h + col_off))],
        out_specs=[tile, full, full, btile, pl.BlockSpec((1, 8, LANES), lambda h, i: (h, 0, 0))],
        out_shape=[jax.ShapeDtypeStruct((t, hd), F32), jax.ShapeDtypeStruct((t + pad, hd), F32),
                   jax.ShapeDtypeStruct((t + pad, hd), F32), jax.ShapeDtypeStruct((N_HEADS, TQ, w), F32),
                   jax.ShapeDtypeStruct((nhp, 8, LANES), F32)],
        args=(sinks, q, kp, vp, bias, do), sem=("parallel", "arbitrary"), comm=comm)


def _halo_prev(tm):
    return lambda i: jnp.maximum(i * (tm // 8) - 1, 0)


def _halo_next(tm, t):
    return lambda i: jnp.minimum((i + 1) * (tm // 8), t // 8 - 1)


def _taps_prev(tile, halo, ktaps, first):
    tm = tile.shape[0]
    ext = jnp.concatenate([jnp.where(first, 0.0, halo), tile], axis=0)
    return [tile] + [pltpu.roll(ext, s, 0)[8:8 + tm] for s in range(1, ktaps)]


def _conv_apply(taps, w_ref, ktaps):
    out = taps[0] * w_ref[ktaps - 1:ktaps, :]
    for s in range(1, ktaps):
        out = out + taps[s] * w_ref[ktaps - 1 - s:ktaps - s, :]
    return out


def _silu_grad(x):
    sg = jax.nn.sigmoid(x)
    return x * sg, sg * (1.0 + x * (1.0 - sg))


FFN_TM = 128
FFN_HALO = 16


def _ffn_mid_fwd(gu, w8, b, name):
    t = gu.shape[0]
    f = D_FF
    tm, hr = FFN_TM, FFN_HALO

    def body(g_ref, u_ref, h_ref, w_ref, b_ref, a_ref):
        first = pl.program_id(0) == 0
        ext = jnp.concatenate([jnp.where(first, 0.0, h_ref[...].astype(F32)), g_ref[...].astype(F32)], axis=0)
        taps = [ext[hr:]] + [pltpu.roll(ext, s, 0)[hr:] for s in (1, 2)]
        gc = _conv_apply(taps, w_ref, 3) + b_ref[...]
        a_ref[...] = (gc * jax.nn.sigmoid(gc) * u_ref[...].astype(F32)).astype(BF16)

    return pl.pallas_call(
        body, name=name, grid=(t // tm,),
        in_specs=[pl.BlockSpec((tm, f), lambda i: (i, 0)), pl.BlockSpec((tm, f), lambda i: (i, 1)),
                  pl.BlockSpec((hr, f), lambda i: (jnp.maximum(i * (tm // hr) - 1, 0), 0)),
                  pl.BlockSpec((8, f), lambda i: (0, 0)), pl.BlockSpec((1, f), lambda i: (0, 0))],
        out_specs=pl.BlockSpec((tm, f), lambda i: (i, 0)), out_shape=jax.ShapeDtypeStruct((t, f), BF16),
        compiler_params=_cp(("parallel",)),
    )(gu, gu, gu, w8, b)


FFN_BT = 256
FFN_BC = 1408


def _ffn_mid_bwd(gu, dxb, w_out_t, w8, b, name):
    t, d = dxb.shape
    f = D_FF
    tm, hr = FFN_BT, FFN_HALO
    nt = t // tm
    n = tm + hr

    def body(g_ref, u_ref, gp_ref, gn_ref, un_ref, dx_ref, dxn_ref, wo_ref, w_ref, b_ref, dg_ref, du_ref, dw_ref, db_ref):
        i = pl.program_id(0)
        first, last = i == 0, i == nt - 1

        @pl.when(first)
        def _():
            dw_ref[...] = jnp.zeros_like(dw_ref)
            db_ref[...] = jnp.zeros_like(db_ref)

        dxe = jnp.concatenate([dx_ref[...], dxn_ref[...]], axis=0)
        row = lax.broadcasted_iota(jnp.int32, (n, 1), 0)
        keep = (row < tm) | jnp.logical_not(last)
        for c in range(0, f, FFN_BC):
            cs = slice(c, c + FFN_BC)
            ext = jnp.concatenate([jnp.where(first, 0.0, gp_ref[:, cs].astype(F32)), g_ref[:, cs].astype(F32),
                                   gn_ref[:, cs].astype(F32)], axis=0)
            taps = [ext[hr:]] + [pltpu.roll(ext, s, 0)[hr:] for s in (1, 2)]
            gc = b_ref[:, cs] + taps[0] * w_ref[2:3, cs] + taps[1] * w_ref[1:2, cs] + taps[2] * w_ref[0:1, cs]
            act, dact = _silu_grad(gc)
            da = _dot(dxe, wo_ref[:, cs], 1, 0)
            up = jnp.concatenate([u_ref[:, cs], un_ref[:, cs]], axis=0).astype(F32)
            dgc = jnp.where(keep, da * up * dact, 0.0)
            du_ref[:, cs] = (da[:tm] * act[:tm]).astype(BF16)
            dg_ref[:, cs] = (dgc[:tm] * w_ref[2:3, cs] + pltpu.roll(dgc, n - 1, 0)[:tm] * w_ref[1:2, cs]
                             + pltpu.roll(dgc, n - 2, 0)[:tm] * w_ref[0:1, cs]).astype(BF16)
            db_ref[:, cs] += jnp.sum(dgc[:tm], axis=0, keepdims=True)
            for s in range(3):
                dw_ref[2 - s:3 - s, cs] += jnp.sum(dgc[:tm] * taps[s][:tm], axis=0, keepdims=True)

    r = tm // hr
    prev = lambda i: jnp.maximum(i * r - 1, 0)
    nxt_blk = lambda i: jnp.minimum((i + 1) * r, t // hr - 1)
    row_f = pl.BlockSpec((tm, f), lambda i: (i, 0))
    return pl.pallas_call(
        body, name=name, grid=(nt,),
        in_specs=[row_f, pl.BlockSpec((tm, f), lambda i: (i, 1)),
                  pl.BlockSpec((hr, f), lambda i: (prev(i), 0)), pl.BlockSpec((hr, f), lambda i: (nxt_blk(i), 0)),
                  pl.BlockSpec((hr, f), lambda i: (nxt_blk(i), 1)),
                  pl.BlockSpec((tm, d), lambda i: (i, 0)), pl.BlockSpec((hr, d), lambda i: (nxt_blk(i), 0)),
                  pl.BlockSpec((d, f), lambda i: (0, 0)),
                  pl.BlockSpec((8, f), lambda i: (0, 0)), pl.BlockSpec((1, f), lambda i: (0, 0))],
        out_specs=[row_f, row_f, pl.BlockSpec((8, f), lambda i: (0, 0)), pl.BlockSpec((1, f), lambda i: (0, 0))],
        out_shape=[jax.ShapeDtypeStruct((t, f), BF16), jax.ShapeDtypeStruct((t, f), BF16),
                   jax.ShapeDtypeStruct((8, f), F32), jax.ShapeDtypeStruct((1, f), F32)],
        compiler_params=_cp(("arbitrary",)),
    )(gu, gu, gu, gu, gu, dxb, dxb, w_out_t, w8, b)


PRE_TM = 256
PRE_TC = 1024


def _ssm_pre_fwd(zx, w8, b, name):
    t = zx.shape[0]
    tm, tc = PRE_TM, PRE_TC
    off = D_INNER // tc

    def body(x_ref, h_ref, w_ref, b_ref, o_ref):
        first = pl.program_id(0) == 0
        c = _conv_apply(_taps_prev(x_ref[...], h_ref[...], 4, first), w_ref, 4) + b_ref[...]
        o_ref[...] = c * jax.nn.sigmoid(c)

    hp = _halo_prev(tm)
    return pl.pallas_call(
        body, name=name, grid=(t // tm, XBC // tc),
        in_specs=[pl.BlockSpec((tm, tc), lambda i, j: (i, j + off)), pl.BlockSpec((8, tc), lambda i, j: (hp(i), j + off)),
                  pl.BlockSpec((8, tc), lambda i, j: (0, j)), pl.BlockSpec((1, tc), lambda i, j: (0, j))],
        out_specs=pl.BlockSpec((tm, tc), lambda i, j: (i, j)), out_shape=jax.ShapeDtypeStruct((t, XBC), F32),
        compiler_params=_cp(("parallel", "parallel")),
    )(zx, zx, w8, b)


def _ssm_pre_bwd(zx, dxbc, w8, b, name):
    t = zx.shape[0]
    tm, tc = PRE_TM, PRE_TC
    off = D_INNER // tc
    nt = t // tm
    n = tm + 8

    def body(x_ref, xp_ref, xn_ref, d_ref, dn_ref, w_ref, b_ref, o_ref, dw_ref, db_ref):
        i = pl.program_id(1)
        first, last = i == 0, i == nt - 1

        @pl.when(first)
        def _():
            dw_ref[...] = jnp.zeros_like(dw_ref)
            db_ref[...] = jnp.zeros_like(db_ref)

        ext = jnp.concatenate([jnp.where(first, 0.0, xp_ref[...]), x_ref[...], xn_ref[...]], axis=0)
        taps = [ext[8:8 + n]] + [pltpu.roll(ext, s, 0)[8:8 + n] for s in (1, 2, 3)]
        c = _conv_apply(taps, w_ref, 4) + b_ref[...]
        _, dact = _silu_grad(c)
        row = lax.broadcasted_iota(jnp.int32, (n, 1), 0)
        dc = jnp.where((row < tm) | jnp.logical_not(last), jnp.concatenate([d_ref[...], dn_ref[...]], axis=0) * dact, 0.0)
        nxt = [dc[:tm]] + [pltpu.roll(dc, n - s, 0)[:tm] for s in (1, 2, 3)]
        o_ref[...] = _conv_apply(nxt, w_ref, 4).astype(BF16)
        db_ref[...] += jnp.sum(dc[:tm], axis=0, keepdims=True)
        for s in range(4):
            dw_ref[3 - s:4 - s, :] += jnp.sum(dc[:tm] * taps[s][:tm], axis=0, keepdims=True)

    hp = _halo_prev(tm)
    hn = _halo_next(tm, t)
    return pl.pallas_call(
        body, name=name, grid=(XBC // tc, nt),
        in_specs=[pl.BlockSpec((tm, tc), lambda j, i: (i, j + off)), pl.BlockSpec((8, tc), lambda j, i: (hp(i), j + off)),
                  pl.BlockSpec((8, tc), lambda j, i: (hn(i), j + off)),
                  pl.BlockSpec((tm, tc), lambda j, i: (i, j)), pl.BlockSpec((8, tc), lambda j, i: (hn(i), j)),
                  pl.BlockSpec((8, tc), lambda j, i: (0, j)), pl.BlockSpec((1, tc), lambda j, i: (0, j))],
        out_specs=[pl.BlockSpec((tm, tc), lambda j, i: (i, j)), pl.BlockSpec((8, tc), lambda j, i: (0, j)),
                   pl.BlockSpec((1, tc), lambda j, i: (0, j))],
        out_shape=[jax.ShapeDtypeStruct((t, XBC), BF16), jax.ShapeDtypeStruct((8, XBC), F32),
                   jax.ShapeDtypeStruct((1, XBC), F32)],
        compiler_params=_cp(("parallel", "arbitrary")),
    )(zx, zx, zx, dxbc, dxbc, w8, b)


def _head_lanes():
    return lax.broadcasted_iota(jnp.int32, (1, LANES), 1) < SSM_HEADS


def _dt_fwd(dtraw, bias, name):
    t = dtraw.shape[0]
    tm = _pick(t, (1024, 512, 256, 128))

    def body(x_ref, b_ref, o_ref):
        v = x_ref[...] + b_ref[...]
        sp = jnp.maximum(v, 0.0) + jnp.log(1.0 + jnp.exp(-jnp.abs(v)))
        o_ref[...] = jnp.where(_head_lanes(), sp, 0.0)

    row = pl.BlockSpec((tm, LANES), lambda i: (i, 0))
    return pl.pallas_call(
        body, name=name, grid=(t // tm,), in_specs=[row, pl.BlockSpec((1, LANES), lambda i: (0, 0))], out_specs=row,
        out_shape=jax.ShapeDtypeStruct((t, LANES), F32), compiler_params=_cp(("parallel",)),
    )(dtraw, bias)


def _dt_bwd(dtraw, bias, ddt, name):
    t = dtraw.shape[0]
    tm = _pick(t, (1024, 512, 256, 128))

    def body(x_ref, b_ref, d_ref, o_ref, db_ref):
        @pl.when(pl.program_id(0) == 0)
        def _():
            db_ref[...] = jnp.zeros_like(db_ref)

        g = jnp.where(_head_lanes(), d_ref[...] * jax.nn.sigmoid(x_ref[...] + b_ref[...]), 0.0)
        o_ref[...] = g.astype(BF16)
        db_ref[...] += jnp.sum(g, axis=0, keepdims=True)

    row = pl.BlockSpec((tm, LANES), lambda i: (i, 0))
    vec = pl.BlockSpec((1, LANES), lambda i: (0, 0))
    return pl.pallas_call(
        body, name=name, grid=(t // tm,), in_specs=[row, vec, row], out_specs=[row, vec],
        out_shape=[jax.ShapeDtypeStruct((t, LANES), BF16), jax.ShapeDtypeStruct((1, LANES), F32)],
        compiler_params=_cp(("arbitrary",)),
    )(dtraw, bias, ddt)


GROUP_W = D_INNER // SSM_GROUPS
POST_TM = 512


def _ssm_post_fwd(y, xbc, zx, dexp, nw, name):
    t = y.shape[0]
    tm = _pick(t, (POST_TM, 256, 128))

    def body(y_ref, x_ref, z_ref, d_ref, w_ref, o_ref):
        zv = z_ref[...]
        y3 = (y_ref[...] + d_ref[...] * x_ref[...]) * (zv * jax.nn.sigmoid(zv))
        r = lax.rsqrt(jnp.mean(y3 * y3, axis=-1, keepdims=True) + EPS)
        o_ref[...] = (y3 * r * w_ref[...]).astype(BF16)

    blk = pl.BlockSpec((tm, GROUP_W), lambda i, g: (i, g))
    vec = pl.BlockSpec((1, GROUP_W), lambda i, g: (0, g))
    return pl.pallas_call(
        body, name=name, grid=(t // tm, SSM_GROUPS), in_specs=[blk, blk, blk, vec, vec], out_specs=blk,
        out_shape=jax.ShapeDtypeStruct((t, D_INNER), BF16), compiler_params=_cp(("parallel", "parallel")),
    )(y, xbc, zx, dexp, nw)


def _ssm_post_bwd(dy4, y, xbc, zx, dexp, nw, name):
    t = y.shape[0]
    tm = _pick(t, (POST_TM, 256, 128))

    def body(g_ref, y_ref, x_ref, z_ref, d_ref, w_ref, dy_ref, dxs_ref, dz_ref, dd_ref, dw_ref):
        @pl.when(pl.program_id(1) == 0)
        def _():
            dd_ref[...] = jnp.zeros_like(dd_ref)
            dw_ref[...] = jnp.zeros_like(dw_ref)

        zv = z_ref[...]
        xv = x_ref[...]
        act, dact = _silu_grad(zv)
        y2 = y_ref[...] + d_ref[...] * xv
        y3 = y2 * act
        r = lax.rsqrt(jnp.mean(y3 * y3, axis=-1, keepdims=True) + EPS)
        y3n = y3 * r
        gv = g_ref[...]
        dyn = gv * w_ref[...]
        dy3 = r * (dyn - y3n * jnp.mean(dyn * y3n, axis=-1, keepdims=True))
        dy2 = dy3 * act
        dy_ref[...] = dy2
        dxs_ref[...] = dy2 * d_ref[...]
        dz_ref[...] = (dy3 * y2 * dact).astype(BF16)
        dd_ref[...] += jnp.sum(dy2 * xv, axis=0, keepdims=True)
        dw_ref[...] += jnp.sum(gv * y3n, axis=0, keepdims=True)

    blk = pl.BlockSpec((tm, GROUP_W), lambda g, i: (i, g))
    vec = pl.BlockSpec((1, GROUP_W), lambda g, i: (0, g))
    return pl.pallas_call(
        body, name=name, grid=(SSM_GROUPS, t // tm), in_specs=[blk, blk, blk, blk, vec, vec],
        out_specs=[blk, blk, blk, vec, vec],
        out_shape=[jax.ShapeDtypeStruct((t, D_INNER), F32), jax.ShapeDtypeStruct((t, D_INNER), F32),
                   jax.ShapeDtypeStruct((t, D_INNER), BF16), jax.ShapeDtypeStruct((1, D_INNER), F32),
                   jax.ShapeDtypeStruct((1, D_INNER), F32)],
        compiler_params=_cp(("parallel", "arbitrary")),
    )(dy4, y, xbc, zx, dexp, nw)


def _ssd_common(dt, alog):
    ll = dt.shape[0]
    a_neg = -jnp.exp(alog)
    a = dt * a_neg
    ri = lax.broadcasted_iota(jnp.int32, (ll, ll), 0)
    ci = lax.broadcasted_iota(jnp.int32, (ll, ll), 1)
    tril = ri >= ci
    acs = _dot(tril.astype(F32), a, 1, 0, HI)
    return a_neg, tril, acs, acs.T


def _pair_terms(acs, acs_t, dt, h0, lo):
    ll = acs.shape[0]
    cols = [acs[:, h0 + e:h0 + e + 1] for e in range(2)]
    rows = [acs_t[h0 + e:h0 + e + 1, :] for e in range(2)]
    dtc = [dt[:, h0 + e:h0 + e + 1] for e in range(2)]
    lasts = [c[ll - 1:ll, :] for c in cols]
    dtx = jnp.where(lo, dtc[0], dtc[1])
    eac = jnp.where(lo, jnp.exp(cols[0]), jnp.exp(cols[1]))
    fdec = jnp.where(lo, jnp.exp(lasts[0] - cols[0]), jnp.exp(lasts[1] - cols[1]))
    elast = jnp.where(lo, jnp.exp(lasts[0]), jnp.exp(lasts[1]))
    return cols, rows, dtx, eac, fdec, elast


def _decay(col, row, tril):
    return jnp.where(tril, jnp.exp(jnp.minimum(col - row, 0.0)), 0.0)


def _two_heads_rows(v, lo):
    z = jnp.zeros_like(v)
    return jnp.concatenate([jnp.where(lo, v, z), jnp.where(lo, z, v)], axis=0)


def _two_heads_cols(ms):
    return jnp.concatenate(ms, axis=1)


def _ssd_fwd(xbc, dt, alog, name, comm=None):
    t = xbc.shape[0]
    ll = SSD_L
    nc = t // ll

    def body(x_ref, dt_ref, al_ref, y_ref, sp_ref, st_ref):
        @pl.when(pl.program_id(0) == 0)
        def _():
            st_ref[...] = jnp.zeros_like(st_ref)

        dtv = dt_ref[...]
        _, tril, acs, acs_t = _ssd_common(dtv, al_ref[...])
        lo = _lo_mask()
        sp_ref[0] = st_ref[...]
        for g in range(SSM_GROUPS):
            bg = x_ref[:, D_INNER + SSM_STATE * g:D_INNER + SSM_STATE * (g + 1)].astype(BF16)
            cg = x_ref[:, D_INNER + 512 + SSM_STATE * g:D_INNER + 512 + SSM_STATE * (g + 1)].astype(BF16)
            gm = _dot(cg, bg, 1, 1)
            g0 = GROUP_W * g
            terms = [_pair_terms(acs, acs_t, dtv, 8 * g + 2 * pp, lo) for pp in range(4)]
            dtx, eac, fdec, elast = [jnp.concatenate([tt[k] for tt in terms], axis=1) for k in (2, 3, 4, 5)]
            xg = x_ref[:, g0:g0 + GROUP_W]
            ug = (xg * dtx).astype(BF16)
            sg = st_ref[:, g0:g0 + GROUP_W]
            yst = _dot(cg, sg.astype(BF16), 1, 0) * eac
            st_ref[:, g0:g0 + GROUP_W] = sg * elast + _dot(bg, (xg * (fdec * dtx)).astype(BF16), 0, 0)
            for pp in range(4):
                cols, rows = terms[pp][0], terms[pp][1]
                sl = slice(LANES * pp, LANES * (pp + 1))
                y_in = _dot(_two_heads_cols([(gm * _decay(cols[e], rows[e], tril)).astype(BF16) for e in range(2)]),
                            _two_heads_rows(ug[:, sl], lo), 1, 0)
                y_ref[:, g0 + LANES * pp:g0 + LANES * (pp + 1)] = y_in + yst[:, sl]

    return _call(
        body, name=name, grid=(nc,),
        in_specs=[pl.BlockSpec((ll, XBC), lambda c: (c, 0)), pl.BlockSpec((ll, LANES), lambda c: (c, 0)),
                  pl.BlockSpec((1, LANES), lambda c: (0, 0))],
        out_specs=[pl.BlockSpec((ll, D_INNER), lambda c: (c, 0)), pl.BlockSpec((1, SSM_STATE, D_INNER), lambda c: (c, 0, 0))],
        out_shape=[jax.ShapeDtypeStruct((t, D_INNER), F32), jax.ShapeDtypeStruct((nc, SSM_STATE, D_INNER), F32)],
        scratch_shapes=[pltpu.VMEM((SSM_STATE, D_INNER), F32)],
        args=(xbc, dt, alog), sem=("arbitrary",), comm=comm)


def _ssd_bwd(xbc, dt, alog, sprev, dy, dskip, name, comm=None):
    t = xbc.shape[0]
    ll = SSD_L
    nc = t // ll

    def body(x_ref, dt_ref, al_ref, sp_ref, dy_ref, dk_ref, dx_ref, ddt_ref, dal_ref, ds_ref, colt_ref):
        @pl.when(pl.program_id(0) == 0)
        def _():
            ds_ref[...] = jnp.zeros_like(ds_ref)
            dal_ref[...] = jnp.zeros_like(dal_ref)

        dtv = dt_ref[...]
        a_neg, tril, acs, acs_t = _ssd_common(dtv, al_ref[...])
        lo = _lo_mask()
        hi = jnp.logical_not(lo)
        lane = lax.broadcasted_iota(jnp.int32, (1, LANES), 1)
        colt_ref[...] = jnp.zeros_like(colt_ref)
        rowterm = jnp.zeros((ll, LANES), F32)
        ddt_u = jnp.zeros((ll, LANES), F32)
        dlast = jnp.zeros((1, LANES), F32)

        def halves(v):
            return (jnp.sum(jnp.where(lo, v, 0.0), axis=-1, keepdims=True),
                    jnp.sum(jnp.where(hi, v, 0.0), axis=-1, keepdims=True))

        for g in range(SSM_GROUPS):
            cb0 = D_INNER + SSM_STATE * g
            cc0 = D_INNER + 512 + SSM_STATE * g
            bg = x_ref[:, cb0:cb0 + SSM_STATE].astype(BF16)
            cg = x_ref[:, cc0:cc0 + SSM_STATE].astype(BF16)
            gm = _dot(cg, bg, 1, 1)
            g0 = GROUP_W * g
            terms = [_pair_terms(acs, acs_t, dtv, 8 * g + 2 * pp, lo) for pp in range(4)]
            dtx, eac, fdec, elast = [jnp.concatenate([tt[k] for tt in terms], axis=1) for k in (2, 3, 4, 5)]
            xg = x_ref[:, g0:g0 + GROUP_W]
            u32 = xg * dtx
            ug = u32.astype(BF16)
            dyg = dy_ref[:, g0:g0 + GROUP_W]
            dyb = dyg.astype(BF16)
            spg = sp_ref[0, :, g0:g0 + GROUP_W]
            spb = spg.astype(BF16)
            dsg = ds_ref[:, g0:g0 + GROUP_W]
            dsb = dsg.astype(BF16)
            du_st = _dot(bg, dsb, 1, 0) * fdec
            yst = _dot(cg, spb, 1, 0) * eac
            dye = (dyg * eac).astype(BF16)
            dc_st = _dot(dye, spb, 1, 1)
            db_st = _dot((xg * (fdec * dtx)).astype(BF16), dsb, 1, 1)
            ds_ref[:, g0:g0 + GROUP_W] = dsg * elast + _dot(cg, dye, 0, 0)
            qst_el = du_st * u32
            rq_el = dyg * yst - qst_el
            q_row = jnp.sum(qst_el, axis=0, keepdims=True)
            s_row = jnp.sum(dsg * spg, axis=0, keepdims=True)
            dgm = jnp.zeros((ll, ll), F32)
            for pp in range(4):
                h0 = 8 * g + 2 * pp
                cols, rows = terms[pp][0], terms[pp][1]
                sl = slice(LANES * pp, LANES * (pp + 1))
                decs = [_decay(cols[e], rows[e], tril) for e in range(2)]
                wms = [gm * d for d in decs]
                dum2 = _dot(dyb[:, sl], _two_heads_rows(ug[:, sl], lo), 1, 1)
                du = _dot(jnp.concatenate([wm.astype(BF16) for wm in wms], axis=0),
                          _two_heads_rows(dyb[:, sl], lo), 0, 0) + du_st[:, sl]
                dx_ref[:, g0 + LANES * pp:g0 + LANES * (pp + 1)] = du * dtx[:, sl] + dk_ref[:, g0 + LANES * pp:g0 + LANES * (pp + 1)]
                ddtu = halves(du * xg[:, sl])
                rq = halves(rq_el[:, sl])
                qs = halves(q_row[:, sl])
                ss = halves(s_row[:, sl])
                for e in range(2):
                    dum = dum2[:, ll * e:ll * (e + 1)]
                    dgm = dgm + dum * decs[e]
                    tm_ = dum * wms[e]
                    oh = lane == (h0 + e)
                    rowterm = rowterm + jnp.where(oh, jnp.sum(tm_, axis=1, keepdims=True) + rq[e], 0.0)
                    ddt_u = ddt_u + jnp.where(oh, ddtu[e], 0.0)
                    dlast = dlast + jnp.where(oh, jnp.exp(cols[e][ll - 1:ll, :]) * ss[e] + qs[e], 0.0)
                    colt_ref[h0 + e:h0 + e + 1, :] = jnp.sum(tm_, axis=0, keepdims=True)
            dgb = dgm.astype(BF16)
            dx_ref[:, cc0:cc0 + SSM_STATE] = _dot(dgb, bg, 1, 0) + dc_st
            dx_ref[:, cb0:cb0 + SSM_STATE] = _dot(dgb, cg, 0, 0) + db_st
        row_io = lax.broadcasted_iota(jnp.int32, (ll, LANES), 0)
        dacs = rowterm - colt_ref[...].T + jnp.where(row_io == ll - 1, dlast, 0.0)
        da = _dot(jnp.logical_not(tril).astype(F32) + jnp.where(
            lax.broadcasted_iota(jnp.int32, (ll, ll), 0) == lax.broadcasted_iota(jnp.int32, (ll, ll), 1), 1.0, 0.0),
            dacs, 1, 0, HI)
        ddt_ref[...] = da * a_neg + ddt_u
        dal_ref[...] += jnp.sum(da * dtv, axis=0, keepdims=True) * a_neg

    rev = lambda c: nc - 1 - c
    return _call(
        body, name=name, grid=(nc,),
        in_specs=[pl.BlockSpec((ll, XBC), lambda c: (rev(c), 0)), pl.BlockSpec((ll, LANES), lambda c: (rev(c), 0)),
                  pl.BlockSpec((1, LANES), lambda c: (0, 0)),
                  pl.BlockSpec((1, SSM_STATE, D_INNER), lambda c: (rev(c), 0, 0)),
                  pl.BlockSpec((ll, D_INNER), lambda c: (rev(c), 0)), pl.BlockSpec((ll, D_INNER), lambda c: (rev(c), 0))],
        out_specs=[pl.BlockSpec((ll, XBC), lambda c: (rev(c), 0)), pl.BlockSpec((ll, LANES), lambda c: (rev(c), 0)),
                   pl.BlockSpec((1, LANES), lambda c: (0, 0))],
        out_shape=[jax.ShapeDtypeStruct((t, XBC), F32), jax.ShapeDtypeStruct((t, LANES), F32),
                   jax.ShapeDtypeStruct((1, LANES), F32)],
        scratch_shapes=[pltpu.VMEM((SSM_STATE, D_INNER), F32), pltpu.VMEM((LANES, ll), F32)],
        args=(xbc, dt, alog, sprev, dy, dskip), sem=("arbitrary",), comm=comm)


ADAM_TR = 512


def _adamw(parts, w, m, v, tr, name, comm=None):
    r, c = w.shape
    c1 = 1.0 - ADAM_B1 ** ADAM_STEP
    c2 = 1.0 - ADAM_B2 ** ADAM_STEP

    def body(p_ref, w_ref, m_ref, v_ref, g_ref, d_ref, mo_ref, vo_ref):
        g = p_ref[0].astype(F32)
        for k in range(1, N_DEV):
            g = g + p_ref[k].astype(F32)
        mn = ADAM_B1 * m_ref[...] + (1.0 - ADAM_B1) * g
        vn = ADAM_B2 * v_ref[...] + (1.0 - ADAM_B2) * (g * g)
        g_ref[...] = g
        mo_ref[...] = mn
        vo_ref[...] = vn
        d_ref[...] = -ADAM_LR * ((mn / c1) / (jnp.sqrt(vn / c2) + ADAM_EPS) + ADAM_WD * w_ref[...])

    row = pl.BlockSpec((tr, c), lambda i: (i, 0))
    sd = jax.ShapeDtypeStruct((r, c), F32)
    return _call(
        body, name=name, grid=(r // tr,),
        in_specs=[pl.BlockSpec((N_DEV, tr, c), lambda i: (0, i, 0)), row, row, row],
        out_specs=[row, row, row, row], out_shape=[sd, sd, sd, sd], args=(parts, w, m, v), sem=("parallel",), comm=comm)


def _peers():
    mx, my, mc = lax.axis_index("x"), lax.axis_index("y"), lax.axis_index("c")
    me = 4 * mx + 2 * my + mc
    out = []
    for k in range(1, N_DEV):
        px = 1 - mx if k & 4 else mx
        py = 1 - my if k & 2 else my
        pc = 1 - mc if k & 1 else mc
        out.append(((px, py, pc), 4 * px + 2 * py + pc))
    return me, out


class _Comm:
    def __init__(self, arrs, scatters):
        self.arrs, self.scatters, self.n = list(arrs), list(scatters), len(arrs)
        self.specs = [pl.BlockSpec(memory_space=pl.ANY)] * self.n
        self.out_shape = [jax.ShapeDtypeStruct(x.shape if sc else (N_DEV,) + x.shape, x.dtype)
                          for x, sc in zip(self.arrs, self.scatters)]
        np_ = N_DEV - 1
        self.scratch = [pltpu.SemaphoreType.DMA((np_ * self.n,)), pltpu.SemaphoreType.DMA((np_ * self.n,)),
                        pltpu.SemaphoreType.DMA((self.n,))]

    def _copies(self, x_refs, o_refs, sems):
        send_sems, recv_sems, local_sems = sems
        me, peers = _peers()
        np_ = N_DEV - 1
        local, sends, recvs = [], [], []
        for a in range(self.n):
            mine = x_refs[a].at[me] if self.scatters[a] else x_refs[a]
            local.append(pltpu.make_async_copy(mine, o_refs[a].at[me], local_sems.at[a]))
        for k, (dev, idx) in enumerate(peers):
            for a in range(self.n):
                mine = x_refs[a].at[me] if self.scatters[a] else x_refs[a]
                sends.append(pltpu.make_async_remote_copy(
                    src_ref=x_refs[a].at[idx] if self.scatters[a] else x_refs[a], dst_ref=o_refs[a].at[me],
                    send_sem=send_sems.at[a * np_ + k], recv_sem=recv_sems.at[a * np_ + k], device_id=dev, device_id_type=MESH))
                recvs.append(pltpu.make_async_remote_copy(
                    src_ref=mine, dst_ref=o_refs[a].at[idx], send_sem=send_sems.at[a * np_ + k],
                    recv_sem=recv_sems.at[a * np_ + k], device_id=dev, device_id_type=MESH))
        return local, sends, recvs

    def start(self, x_refs, o_refs, sems):
        local, sends, _ = self._copies(x_refs, o_refs, sems)
        for cp in local + sends:
            cp.start()

    def wait(self, x_refs, o_refs, sems):
        local, sends, recvs = self._copies(x_refs, o_refs, sems)
        for cp in recvs:
            cp.wait_recv()
        for cp in sends:
            cp.wait_send()
        for cp in local:
            cp.wait()


class _Gather2(_Comm):
    def __init__(self, arrs):
        super().__init__(arrs, [False] * len(arrs))

    def _plan(self, x_refs, o_refs, sems):
        send_sems, recv_sems, local_sems = sems
        mx, my, mc = lax.axis_index("x"), lax.axis_index("y"), lax.axis_index("c")
        slot = lambda px, py, pc: 4 * px + 2 * py + pc
        sib = (mx, my, 1 - mc)
        chips = [(1 - mx, my), (mx, 1 - my), (1 - mx, 1 - my)]
        np_ = N_DEV - 1
        local, first, passed, arrive_first, arrive_rest = [], [], [], [], []

        def copy(a, k, src, block, to):
            return pltpu.make_async_remote_copy(
                src_ref=src, dst_ref=o_refs[a].at[block], send_sem=send_sems.at[a * np_ + k], recv_sem=recv_sems.at[a * np_ + k],
                device_id=to, device_id_type=MESH)

        for a in range(self.n):
            me = slot(mx, my, mc)
            local.append(pltpu.make_async_copy(x_refs[a], o_refs[a].at[me], local_sems.at[a]))
            first.append(copy(a, 0, x_refs[a], me, sib))
            arrive_rest.append(copy(a, 0, x_refs[a], slot(*sib), sib))
            for j, (cx, cy) in enumerate(chips):
                first.append(copy(a, 1 + j, x_refs[a], me, (cx, cy, mc)))
                arrive_first.append(copy(a, 1 + j, x_refs[a], slot(cx, cy, mc), (cx, cy, mc)))
                passed.append(copy(a, 4 + j, o_refs[a].at[slot(cx, cy, mc)], slot(cx, cy, mc), sib))
                arrive_rest.append(copy(a, 4 + j, x_refs[a], slot(cx, cy, 1 - mc), sib))
        return local, first, passed, arrive_first, arrive_rest

    def start(self, x_refs, o_refs, sems):
        local, first, _, _, _ = self._plan(x_refs, o_refs, sems)
        for cp in local + first:
            cp.start()

    def wait(self, x_refs, o_refs, sems):
        local, first, passed, arrive_first, arrive_rest = self._plan(x_refs, o_refs, sems)
        for arrived, onward in zip(arrive_first, passed):
            arrived.wait_recv()
            onward.start()
        for cp in arrive_rest:
            cp.wait_recv()
        for cp in first + passed:
            cp.wait_send()
        for cp in local:
            cp.wait()


def _call(body, *, name, grid, in_specs, out_specs, out_shape, args, scratch_shapes=(), sem=None, comm=None):
    if comm is None:
        outs = pl.pallas_call(
            body, name=name, grid=grid, in_specs=list(in_specs), out_specs=list(out_specs), out_shape=list(out_shape),
            scratch_shapes=list(scratch_shapes), compiler_params=_cp(sem),
        )(*args)
        return list(outs), []
    n_in, n_out, nc = len(in_specs), len(out_specs), comm.n
    nsteps = 1
    for g in grid:
        nsteps *= g

    def carrier(*refs):
        ins, cin = refs[:n_in], refs[n_in:n_in + nc]
        outs, cout = refs[n_in + nc:n_in + nc + n_out], refs[n_in + nc + n_out:n_in + 2 * nc + n_out]
        rest = refs[n_in + 2 * nc + n_out:]
        scratch, sems = rest[:len(rest) - 3], rest[len(rest) - 3:]
        if nsteps == 1:
            comm.start(cin, cout, sems)
            body(*ins, *outs, *scratch)
            comm.wait(cin, cout, sems)
            return
        step = 0
        for d, g in enumerate(grid):
            step = step * g + pl.program_id(d)

        @pl.when(step == 0)
        def _():
            comm.start(cin, cout, sems)

        body(*ins, *outs, *scratch)

        @pl.when(step == nsteps - 1)
        def _():
            comm.wait(cin, cout, sems)

    outs = pl.pallas_call(
        carrier, name=name, grid=grid, in_specs=list(in_specs) + comm.specs, out_specs=list(out_specs) + comm.specs,
        out_shape=list(out_shape) + comm.out_shape, scratch_shapes=list(scratch_shapes) + comm.scratch,
        compiler_params=_cp(("arbitrary",) * len(grid) if grid else None),
    )(*args, *comm.arrs)
    return list(outs[:n_out]), list(outs[n_out:])


def _exchange(comm, name):
    return _call(lambda *refs: None, name=name, grid=(), in_specs=[], out_specs=[], out_shape=[], args=[], comm=comm)[1]


def _pack(arrs, dtype, lead=()):
    nl = len(lead)
    flat = jnp.concatenate([a.astype(dtype).reshape(lead + (-1,)) for a in arrs], axis=nl)
    n = flat.shape[-1]
    rows = -(-n // (LANES * ADAM_TR)) * ADAM_TR
    flat = jnp.pad(flat, [(0, 0)] * nl + [(0, rows * LANES - n)])
    return flat.reshape(lead + (rows, LANES))


def _unpack(flat, shapes, lead=()):
    nl = len(lead)
    flat = flat.reshape(lead + (-1,))
    out, o = [], 0
    for s in shapes:
        n = 1
        for d in s:
            n *= d
        out.append(lax.slice_in_dim(flat, o, o + n, axis=nl).reshape(lead + tuple(s)))
        o += n
    return out


def _join(g, ax):
    return jnp.concatenate([g[d] for d in range(N_DEV)], axis=ax)


def _split(full, ax):
    n = full.shape[ax] // N_DEV
    return jnp.stack([lax.slice_in_dim(full, d * n, (d + 1) * n, axis=ax) for d in range(N_DEV)])


_WEIGHTS = ['norm_mix', 'norm_ffn', 'attn_w_in', 'attn_w_out', 'relpos_table', 'q_norm_a', 'k_norm_a', 'q_norm_b',
            'k_norm_b', 'sinks', 'ssm_w_in', 'ssm_conv_w', 'ssm_conv_b', 'ssm_dt_bias', 'ssm_a_log', 'ssm_d', 'ssm_norm',
            'ssm_w_out', 'ffn_w_in', 'ffn_conv_w', 'ffn_conv_b', 'ffn_w_out']
_SHARD_AX = {'attn_w_in': 2, 'attn_w_out': 1, 'ssm_w_in': 2, 'ssm_conv_w': 2, 'ssm_conv_b': 1, 'ssm_norm': 1,
             'ssm_w_out': 1, 'ffn_w_in': 2, 'ffn_conv_w': 2, 'ffn_w_out': 1}
_BIG = ['attn_w_in', 'attn_w_out', 'ssm_w_in', 'ssm_w_out', 'ffn_w_in', 'ffn_w_out']
_SMALL = ['ssm_conv_w', 'ssm_conv_b', 'ssm_norm', 'ffn_conv_w']
_AX2 = {n: _SHARD_AX[n] - 1 for n in _BIG}
_REPL = [n for n in _WEIGHTS if n not in _SHARD_AX]


def _rows8(w):
    return jnp.pad(w, ((0, 8 - w.shape[0]), (0, 0)))


def _lanes128(v):
    return jnp.pad(v, (0, LANES - v.shape[0])).reshape(1, LANES)


def _band_mask(n_prev, pad):
    cq = jnp.arange(TQ)[:, None] // CHUNK
    ck = jnp.arange(pad + TQ)[None, :] // CHUNK
    return (ck >= cq) & (ck <= cq + n_prev)


def _ffn_fwd(xin, g, w_in, w8, cb, tag):
    gu, h = _rms_mm(xin, g, w_in, f"mm_ffn_in{tag}", BF16)
    a = _ffn_mid_fwd(gu, w8, cb, f"ffn_mid{tag}")
    return a, (h, gu, a)


def _ffn_bwd(dx, dxb, xin, g, w_in_t, w8, cb, w_out_t, saved, tag):
    h, gu, a = saved
    dw_out = _mm(a, dxb, f"mm_ffn_dwout{tag}", trans_a=True)
    dgate, dup, dw8, dcb = _ffn_mid_bwd(gu, dxb, w_out_t, w8, cb, f"ffn_mid_bwd{tag}")
    dh = _mm(dgate, w_in_t[:D_FF], f"mm_ffn_dh_g{tag}")
    dw_in = jnp.concatenate([_mm(h, dgate, f"mm_ffn_dwin_g{tag}", trans_a=True),
                             _mm(h, dup, f"mm_ffn_dwin_u{tag}", trans_a=True)], axis=1)
    dxp, dxpb, dg = _mm_rms_bwd(dup, w_in_t[D_FF:], dh, xin, g, dx, f"mm_ffn_dh_u{tag}")
    return dxp, dxpb, dg, dw_in, dw8[:3], dcb, dw_out


def kernel(x, norm_mix, norm_ffn, attn_w_in, attn_w_out, relpos_table, q_norm_a, k_norm_a, q_norm_b, k_norm_b, sinks, ssm_w_in, ssm_conv_w, ssm_conv_b, ssm_dt_bias, ssm_a_log, ssm_d, ssm_norm, ssm_w_out, ffn_w_in, ffn_conv_w, ffn_conv_b, ffn_w_out, loss_target, m_norm_mix, m_norm_ffn, m_attn_w_in, m_attn_w_out, m_relpos_table, m_q_norm_a, m_k_norm_a, m_q_norm_b, m_k_norm_b, m_sinks, m_ssm_w_in, m_ssm_conv_w, m_ssm_conv_b, m_ssm_dt_bias, m_ssm_a_log, m_ssm_d, m_ssm_norm, m_ssm_w_out, m_ffn_w_in, m_ffn_conv_w, m_ffn_conv_b, m_ffn_w_out, v_norm_mix, v_norm_ffn, v_attn_w_in, v_attn_w_out, v_relpos_table, v_q_norm_a, v_k_norm_a, v_q_norm_b, v_k_norm_b, v_sinks, v_ssm_w_in, v_ssm_conv_w, v_ssm_conv_b, v_ssm_dt_bias, v_ssm_a_log, v_ssm_d, v_ssm_norm, v_ssm_w_out, v_ffn_w_in, v_ffn_conv_w, v_ffn_conv_b, v_ffn_w_out):
    w = dict(norm_mix=norm_mix, norm_ffn=norm_ffn, attn_w_in=attn_w_in, attn_w_out=attn_w_out, relpos_table=relpos_table,
             q_norm_a=q_norm_a, k_norm_a=k_norm_a, q_norm_b=q_norm_b, k_norm_b=k_norm_b, sinks=sinks, ssm_w_in=ssm_w_in,
             ssm_conv_w=ssm_conv_w, ssm_conv_b=ssm_conv_b, ssm_dt_bias=ssm_dt_bias, ssm_a_log=ssm_a_log, ssm_d=ssm_d,
             ssm_norm=ssm_norm, ssm_w_out=ssm_w_out, ffn_w_in=ffn_w_in, ffn_conv_w=ffn_conv_w, ffn_conv_b=ffn_conv_b,
             ffn_w_out=ffn_w_out)
    mom = dict(norm_mix=m_norm_mix, norm_ffn=m_norm_ffn, attn_w_in=m_attn_w_in, attn_w_out=m_attn_w_out,
               relpos_table=m_relpos_table, q_norm_a=m_q_norm_a, k_norm_a=m_k_norm_a, q_norm_b=m_q_norm_b,
               k_norm_b=m_k_norm_b, sinks=m_sinks, ssm_w_in=m_ssm_w_in, ssm_conv_w=m_ssm_conv_w, ssm_conv_b=m_ssm_conv_b,
               ssm_dt_bias=m_ssm_dt_bias, ssm_a_log=m_ssm_a_log, ssm_d=m_ssm_d, ssm_norm=m_ssm_norm, ssm_w_out=m_ssm_w_out,
               ffn_w_in=m_ffn_w_in, ffn_conv_w=m_ffn_conv_w, ffn_conv_b=m_ffn_conv_b, ffn_w_out=m_ffn_w_out)
    var = dict(norm_mix=v_norm_mix, norm_ffn=v_norm_ffn, attn_w_in=v_attn_w_in, attn_w_out=v_attn_w_out,
               relpos_table=v_relpos_table, q_norm_a=v_q_norm_a, k_norm_a=v_k_norm_a, q_norm_b=v_q_norm_b,
               k_norm_b=v_k_norm_b, sinks=v_sinks, ssm_w_in=v_ssm_w_in, ssm_conv_w=v_ssm_conv_w, ssm_conv_b=v_ssm_conv_b,
               ssm_dt_bias=v_ssm_dt_bias, ssm_a_log=v_ssm_a_log, ssm_d=v_ssm_d, ssm_norm=v_ssm_norm, ssm_w_out=v_ssm_w_out,
               ffn_w_in=v_ffn_w_in, ffn_conv_w=v_ffn_conv_w, ffn_conv_b=v_ffn_conv_b, ffn_w_out=v_ffn_w_out)

    def piece(n, l):
        return w[n][l].astype(BF16)

    def gather_of(names_layers):
        return _Gather2([piece(n, l) for n, l in names_layers])

    def joined(got, names_layers):
        return [_join(g, _AX2[n]) for g, (n, _) in zip(got, names_layers)]

    first = [('attn_w_in', 0), ('attn_w_out', 0)]
    got = _exchange(_Gather2([piece(n, l) for n, l in first] + [_pack([w[n] for n in _SMALL], F32)]), "gather_attn")
    w_attn_in, w_attn_out = joined(got[:2], first)
    full = {}
    for n, g in zip(_SMALL, _unpack(got[2], [w[n].shape for n in _SMALL], lead=(N_DEV,))):
        full[n] = _join(g, _SHARD_AX[n])
    ssm_cw8 = _rows8(full['ssm_conv_w'][0])
    ssm_cb = full['ssm_conv_b']
    ssm_nw = full['ssm_norm']
    ffn_cw8 = [_rows8(full['ffn_conv_w'][l]) for l in range(2)]
    ffn_cb = [ffn_conv_b[l:l + 1] for l in range(2)]

    x0 = x[0]
    target = loss_target[0]
    t = x0.shape[0]

    g_mix0, g_mix1 = norm_mix[0:1], norm_mix[1:2]
    g_ffn0, g_ffn1 = norm_ffn[0:1], norm_ffn[1:2]
    proj, h0 = _rms_mm(x0, g_mix0, w_attn_in, "mm_attn_in", F32)
    hn_w = jnp.concatenate([jnp.tile(v, (1, 2)) for v in (q_norm_a, k_norm_a, q_norm_b, k_norm_b)], axis=0)
    qa, kpa, vpa, qb, kpb, vpb = _headnorm_fwd(proj, hn_w, "headnorm")
    table = jnp.pad(relpos_table[0], ((0, 0), (0, REL_W - (2 * MAX_REL + 1))))
    bias_a = jnp.where(_band_mask(A_PREV, PAD_A)[None], jnp.transpose(_relpos_fwd(table, "relpos_bias"), (1, 0, 2)), NEG)
    rel_b = jnp.arange(TQ)[:, None] - (jnp.arange(PAD_B + TQ)[None, :] - PAD_B)
    slopes = 2.0 ** (-8.0 * jnp.arange(1, N_HEADS + 1, dtype=F32) / N_HEADS)
    bias_b = jnp.where(_band_mask(B_PREV, PAD_B)[None], -slopes[:, None, None] * jnp.abs(rel_b).astype(F32)[None], NEG)
    no_sinks = jnp.full((N_HEADS,), NEG, F32)
    ffn0_w, ssm_w, ffn1_w = [('ffn_w_in', 0), ('ffn_w_out', 0)], [('ssm_w_in', 0), ('ssm_w_out', 0)], [('ffn_w_in', 1), ('ffn_w_out', 1)]
    oa, got = _attn_fwd(qa, kpa, vpa, bias_a, no_sinks, PAD_A, "attn_a", comm=gather_of(ffn0_w + ssm_w))
    w_ffn_in0, w_ffn_out0, w_ssm_in, w_ssm_out = joined(got, ffn0_w + ssm_w)
    ob, got = _attn_fwd(qb, kpb, vpb, bias_b, sinks[0], PAD_B, "attn_b", comm=gather_of(ffn1_w))
    w_ffn_in1, w_ffn_out1 = joined(got, ffn1_w)
    w_ssm_main = w_ssm_in[:, :ZX]
    w_ssm_dt = jnp.pad(w_ssm_in[:, ZX:], ((0, 0), (0, LANES - SSM_HEADS)))
    x1 = _mm(oa, w_attn_out[:512], "mm_attn_out_a", res=x0)
    x1 = _mm(ob, w_attn_out[512:], "mm_attn_out_b", res=x1)
    a0, ffn0_saved = _ffn_fwd(x1, g_ffn0, w_ffn_in0, ffn_cw8[0], ffn_cb[0], "0")
    x2 = _mm(a0, w_ffn_out0, "mm_ffn_out0", res=x1)

    zx, h2 = _rms_mm(x2, g_mix1, w_ssm_main, "mm_ssm_in", F32)
    dtraw = _mm(h2, w_ssm_dt, "mm_ssm_dt")
    dt_bias = _lanes128(ssm_dt_bias[0])
    alog = _lanes128(ssm_a_log[0])
    dexp = jnp.repeat(ssm_d[0], HEAD_DIM).reshape(1, D_INNER)
    xbc = _ssm_pre_fwd(zx, ssm_cw8, ssm_cb, "ssm_pre")
    dt = _dt_fwd(dtraw, dt_bias, "ssm_dt")
    (y, sprev), _ = _ssd_fwd(xbc, dt, alog, "ssd_fwd")
    y4 = _ssm_post_fwd(y, xbc, zx, dexp, ssm_nw, "ssm_post")
    x3 = _mm(y4, w_ssm_out, "mm_ssm_out", res=x2)
    a1, ffn1_saved = _ffn_fwd(x3, g_ffn1, w_ffn_in1, ffn_cw8[1], ffn_cb[1], "1")

    dx4, dx4b, sq = _mm_loss(a1, w_ffn_out1, x3, target, "mm_ffn_out1_loss")
    loss = lax.psum(0.5 * jnp.sum(sq) / D_MODEL, ("x", "y", "c"))

    grads = {}

    def scatter_of(pieces):
        return _Comm([_split(g, _AX2[n]).astype(BF16) for n, g in pieces], [True] * len(pieces))

    dx3, dx3b, dg_ffn1, dwin1, dcw1, dcb1, dwout1 = _ffn_bwd(
        dx4, dx4b, x3, g_ffn1, w_ffn_in1.T, ffn_cw8[1], ffn_cb[1], w_ffn_out1.T, ffn1_saved, "1")

    dy4 = _mm(dx3b, w_ssm_out.T, "mm_ssm_dy")
    dw_ssm_out = _mm(y4, dx3b, "mm_ssm_dwout", trans_a=True)
    dyv, dskip, dz, dd_lane, dnw = _ssm_post_bwd(dy4, y, xbc, zx, dexp, ssm_nw, "ssm_post_bwd")
    (dxbc, ddt, dalog), parts_ffn1 = _ssd_bwd(xbc, dt, alog, sprev, dyv, dskip, "ssd_bwd",
                                              comm=scatter_of([('ffn_w_in', dwin1), ('ffn_w_out', dwout1)]))
    dxr, dcw_s, dcb_s = _ssm_pre_bwd(zx, dxbc, ssm_cw8, ssm_cb, "ssm_pre_bwd")
    ddtraw, ddtb = _dt_bwd(dtraw, dt_bias, ddt, "ssm_dt_bwd")
    w_main_t = w_ssm_main.T
    dh2 = _mm(dz, w_main_t[:D_INNER], "mm_ssm_dh_z")
    dh2 = _mm(dxr, w_main_t[D_INNER:], "mm_ssm_dh_x", res=dh2)
    dw_ssm_in = jnp.concatenate([
        _mm(h2, dz, "mm_ssm_dwin_z", trans_a=True), _mm(h2, dxr, "mm_ssm_dwin_x", trans_a=True),
        _mm(h2, ddtraw, "mm_ssm_dwin_dt", trans_a=True)[:, :SSM_HEADS]], axis=1)
    dx2, dx2b, dg_mix1 = _mm_rms_bwd(ddtraw, w_ssm_dt.T, dh2, x2, g_mix1, dx3, "mm_ssm_dh_dt")
    grads['ssm_conv_w'] = dcw_s[:4][None]
    grads['ssm_conv_b'] = dcb_s
    grads['ssm_norm'] = dnw
    grads['ssm_dt_bias'] = ddtb[:, :SSM_HEADS]
    grads['ssm_a_log'] = dalog[:, :SSM_HEADS]
    grads['ssm_d'] = jnp.sum(dd_lane.reshape(SSM_HEADS, HEAD_DIM), axis=1)[None]

    dx1, dx1b, dg_ffn0, dwin0, dcw0, dcb0, dwout0 = _ffn_bwd(
        dx2, dx2b, x1, g_ffn0, w_ffn_in0.T, ffn_cw8[0], ffn_cb[0], w_ffn_out0.T, ffn0_saved, "0")
    grads['ffn_conv_w'] = jnp.stack([dcw0, dcw1])
    grads['ffn_conv_b'] = jnp.concatenate([dcb0, dcb1], axis=0)
    grads['norm_ffn'] = jnp.concatenate([dg_ffn0, dg_ffn1], axis=0)

    do = _mm(dx1b, w_attn_out.T, "mm_attn_do", out_dtype=BF16)
    dw_attn_out = jnp.concatenate([_mm(oa, dx1b, "mm_attn_dwout_a", trans_a=True),
                                   _mm(ob, dx1b, "mm_attn_dwout_b", trans_a=True)], axis=0)
    (dqa, dkpa, dvpa, dbias_a, _), parts_ssm = _attn_bwd(
        qa, kpa, vpa, bias_a, no_sinks, do, 0, PAD_A, "attn_a_bwd",
        comm=scatter_of([('ssm_w_in', dw_ssm_in), ('ssm_w_out', dw_ssm_out), ('attn_w_out', dw_attn_out)]))
    (dqb, dkpb, dvpb, _, dsink), parts_ffn0 = _attn_bwd(
        qb, kpb, vpb, bias_b, sinks[0], do, 4, PAD_B, "attn_b_bwd",
        comm=scatter_of([('ffn_w_in', dwin0), ('ffn_w_out', dwout0)]))
    grads['relpos_table'] = _relpos_bwd(jnp.transpose(dbias_a, (1, 0, 2)), "relpos_bwd")[None, :, :2 * MAX_REL + 1]
    grads['sinks'] = dsink[:, :2, 0].reshape(1, N_HEADS)
    dproj, dhn = _headnorm_bwd(proj, hn_w, dqa, dkpa, dvpa, dqb, dkpb, dvpb, "headnorm_bwd")
    dhn = dhn[:, :HEAD_DIM] + dhn[:, HEAD_DIM:]
    for k, n in enumerate(('q_norm_a', 'k_norm_a', 'q_norm_b', 'k_norm_b')):
        grads[n] = dhn[k:k + 1]
    dw_attn_in = _mm(h0, dproj, "mm_attn_dwin", trans_a=True)
    dx0, _, dg_mix0, parts_attn_in = _mm_rms_bwd(dproj, w_attn_in.T, None, x0, g_mix0, dx1, "mm_attn_dh",
                                                 comm=scatter_of([('attn_w_in', dw_attn_in)]))
    grads['norm_mix'] = jnp.concatenate([dg_mix0, dg_mix1], axis=0)

    def adam_piece(n, l, parts):
        rows = w[n][l].shape[0]
        return _adamw(parts, w[n][l], mom[n][l], var[n][l], _pick(rows, (256, 128, 64)), f"adamw_{n}{l}")[0]

    sm_shapes = [w[n].shape for n in _SMALL]
    rp_shapes = [w[n].shape for n in _REPL]
    recv = _exchange(_Comm(
        [_pack([_split(grads[n], _SHARD_AX[n]) for n in _SMALL], F32, lead=(N_DEV,)), _pack([grads[n] for n in _REPL], F32)],
        [True, False]), "exchange_small")
    res = [{}, {}, {}, {}]
    by_piece = {
        ('ffn_w_in', 1): adam_piece('ffn_w_in', 1, parts_ffn1[0]), ('ffn_w_out', 1): adam_piece('ffn_w_out', 1, parts_ffn1[1]),
        ('ssm_w_in', 0): adam_piece('ssm_w_in', 0, parts_ssm[0]), ('ssm_w_out', 0): adam_piece('ssm_w_out', 0, parts_ssm[1]),
        ('attn_w_out', 0): adam_piece('attn_w_out', 0, parts_ssm[2]),
        ('ffn_w_in', 0): adam_piece('ffn_w_in', 0, parts_ffn0[0]), ('ffn_w_out', 0): adam_piece('ffn_w_out', 0, parts_ffn0[1]),
        ('attn_w_in', 0): adam_piece('attn_w_in', 0, parts_attn_in[0]),
    }
    for n in _BIG:
        for kind in range(4):
            res[kind][n] = jnp.stack([by_piece[n, l][kind] for l in range(w[n].shape[0])])
    for names, shapes, parts in ((_SMALL, sm_shapes, recv[0]), (_REPL, rp_shapes, recv[1])):
        outs, _ = _adamw(parts, _pack([w[n] for n in names], F32), _pack([mom[n] for n in names], F32),
                         _pack([var[n] for n in names], F32), ADAM_TR, "adamw_" + ("small" if names is _SMALL else "replicated"))
        for kind, flat in enumerate(outs):
            for n, a in zip(names, _unpack(flat, shapes)):
                res[kind][n] = a
    return (loss, dx0[None], *[res[0][n] for n in _WEIGHTS], *[res[1][n] for n in _WEIGHTS],
            *[res[2][n] for n in _WEIGHTS], *[res[3][n] for n in _WEIGHTS])
```

```python
import jax
import jax.numpy as jnp
from jax import lax
from jax.experimental import pallas as pl
from jax.experimental.pallas import tpu as pltpu

F32 = jnp.float32
BF16 = jnp.bfloat16
HI = lax.Precision.HIGHEST
MESH = pl.DeviceIdType.MESH
NEG = -1e30

N_DEV = 8
D_MODEL = 1024
EPS = 1e-6
CHUNK = 64
HEAD_DIM = 64
N_HEADS = 8
A_PREV = 8
B_PREV = 2
MAX_REL = 256
TQ = 2 * CHUNK
ATT_SUB = 4
PAD_A = A_PREV * CHUNK
PAD_B = B_PREV * CHUNK
REL_W = PAD_A + TQ
D_INNER = 2048
SSM_HEADS = 32
SSM_GROUPS = 4
SSM_STATE = 128
XBC = D_INNER + 2 * SSM_GROUPS * SSM_STATE
ZX = D_INNER + XBC
D_FF = 2816
SSD_L = 128
LANES = 128
VMEM_LIMIT = 56 << 20

ADAM_LR, ADAM_B1, ADAM_B2, ADAM_EPS, ADAM_WD, ADAM_STEP = 0.001, 0.9, 0.999, 1e-08, 0.01, 10


def _cp(sem=None):
    return pltpu.CompilerParams(dimension_semantics=sem, vmem_limit_bytes=VMEM_LIMIT)


def _dot(a, b, ca=1, cb=0, prec=None):
    return lax.dot_general(a, b, (((ca,), (cb,)), ((), ())), preferred_element_type=F32, precision=prec)


def _pick(n, cands):
    for c in cands:
        if n % c == 0:
            return c
    return n


def _lo_mask():
    return lax.broadcasted_iota(jnp.int32, (1, LANES), 1) < HEAD_DIM


_TN_CHUNKS = (1408, 1536, 1152, 1024, 512, 256, 128)


def _mm_tn(a, b, name):
    kdim, m = a.shape
    n = b.shape[1]
    assert b.shape[0] == kdim, (a.shape, b.shape)
    tn = _pick(n, _TN_CHUNKS)
    tk = _pick(kdim, (512, 256, 128))
    nk = kdim // tk

    def body(a_ref, b_ref, o_ref, acc):
        k = pl.program_id(0)

        @pl.when(k == 0)
        def _():
            acc[...] = jnp.zeros_like(acc)

        av = a_ref[...]
        for c in range(0, n, tn):
            acc[:, c:c + tn] += _dot(av, b_ref[:, c:c + tn], 0, 0)

        @pl.when(k == nk - 1)
        def _():
            o_ref[...] = acc[...].astype(BF16)

    return pl.pallas_call(
        body, name=name, grid=(nk,),
        in_specs=[pl.BlockSpec((tk, m), lambda k: (k, 0)), pl.BlockSpec((tk, n), lambda k: (k, 0))],
        out_specs=pl.BlockSpec((m, n), lambda k: (0, 0)), out_shape=jax.ShapeDtypeStruct((m, n), BF16),
        scratch_shapes=[pltpu.VMEM((m, n), F32)], compiler_params=_cp(("arbitrary",)),
    )(a, b)


def _mm(a, b, name, out_dtype=F32, res=None, trans_b=False, b_rows=None):
    m, kdim = a.shape
    if b_rows is None:
        b_rows = (0, b.shape[0])
    off, rows = b_rows
    n = rows if trans_b else b.shape[1]
    assert (b.shape[1] if trans_b else rows) == kdim and off % rows == 0, (a.shape, b.shape, b_rows)
    tn = _pick(n, _TN_CHUNKS)
    tm = _pick(m, (256, 128) if n > 2304 else (512, 256, 128))

    def body(*refs):
        if res is None:
            a_ref, b_ref, o_ref = refs
        else:
            a_ref, b_ref, r_ref, o_ref = refs
        av = a_ref[...]
        for c in range(0, n, tn):
            r = _dot(av, b_ref[c:c + tn, :], 1, 1) if trans_b else _dot(av, b_ref[:, c:c + tn], 1, 0)
            if res is not None:
                r = r + r_ref[:, c:c + tn]
            o_ref[:, c:c + tn] = r.astype(out_dtype)

    in_specs = [pl.BlockSpec((tm, kdim), lambda i: (i, 0)), pl.BlockSpec((rows, b.shape[1]), lambda i: (off // rows, 0))]
    args = [a, b]
    if res is not None:
        in_specs.append(pl.BlockSpec((tm, n), lambda i: (i, 0)))
        args.append(res)
    return pl.pallas_call(
        body, name=name, grid=(m // tm,), in_specs=in_specs, out_specs=pl.BlockSpec((tm, n), lambda i: (i, 0)),
        out_shape=jax.ShapeDtypeStruct((m, n), out_dtype), compiler_params=_cp(("parallel",)),
    )(*args)


def _rms_mm(x, g, bt, n, name, out_dtype):
    t, d = x.shape
    tn = _pick(n, _TN_CHUNKS)
    tm = _pick(t, (256, 128))

    def body(x_ref, g_ref, b_ref, o_ref, h_ref):
        xv = x_ref[...]
        r = lax.rsqrt(jnp.mean(xv * xv, axis=-1, keepdims=True) + EPS)
        h = (xv * r * g_ref[...]).astype(BF16)
        h_ref[...] = h
        for c in range(0, n, tn):
            o_ref[:, c:c + tn] = _dot(h, b_ref[c:c + tn, :], 1, 1).astype(out_dtype)

    row = pl.BlockSpec((tm, d), lambda i: (i, 0))
    return pl.pallas_call(
        body, name=name, grid=(t // tm,),
        in_specs=[row, pl.BlockSpec((1, d), lambda i: (0, 0)), pl.BlockSpec(bt.shape, lambda i: (0, 0))],
        out_specs=[pl.BlockSpec((tm, n), lambda i: (i, 0)), row],
        out_shape=[jax.ShapeDtypeStruct((t, n), out_dtype), jax.ShapeDtypeStruct((t, d), BF16)],
        compiler_params=_cp(("parallel",)),
    )(x, g, bt)


def _mm_rms_bwd(a, b, b_off, dh_prev, x, g, dres, name, comm=None):
    t, d = x.shape
    kdim = a.shape[1]
    assert b_off % kdim == 0 and b.shape[1] == d, (a.shape, b.shape, b_off)
    tm = _pick(t, (256, 128))

    def body(*refs):
        if dh_prev is None:
            a_ref, b_ref, x_ref, g_ref, dr_ref, dx_ref, dxb_ref, dg_ref = refs
            dhv = _dot(a_ref[...], b_ref[...], 1, 0)
        else:
            a_ref, b_ref, p_ref, x_ref, g_ref, dr_ref, dx_ref, dxb_ref, dg_ref = refs
            dhv = _dot(a_ref[...], b_ref[...], 1, 0) + p_ref[...]
        xv = x_ref[...]
        r = lax.rsqrt(jnp.mean(xv * xv, axis=-1, keepdims=True) + EPS)
        xh = xv * r
        dxh = dhv * g_ref[...]
        dx = dr_ref[...] + r * (dxh - xh * jnp.mean(dxh * xh, axis=-1, keepdims=True))
        dx_ref[...] = dx
        dxb_ref[...] = dx.astype(BF16)

        @pl.when(pl.program_id(0) == 0)
        def _():
            dg_ref[...] = jnp.zeros_like(dg_ref)

        dg_ref[...] += jnp.sum(dhv * xh, axis=0, keepdims=True)

    row = pl.BlockSpec((tm, d), lambda i: (i, 0))
    vec = pl.BlockSpec((1, d), lambda i: (0, 0))
    in_specs = [pl.BlockSpec((tm, kdim), lambda i: (i, 0)), pl.BlockSpec((kdim, d), lambda i: (b_off // kdim, 0))]
    args = [a, b]
    if dh_prev is not None:
        in_specs.append(row)
        args.append(dh_prev)
    outs, got = _call(
        body, name=name, grid=(t // tm,), in_specs=in_specs + [row, vec, row], out_specs=[row, row, vec],
        out_shape=[jax.ShapeDtypeStruct((t, d), F32), jax.ShapeDtypeStruct((t, d), BF16), jax.ShapeDtypeStruct((1, d), F32)],
        args=(*args, x, g, dres), sem=("arbitrary",), comm=comm)
    return (*outs, got) if comm is not None else tuple(outs)


def _mm_loss(a, b, res, target, name):
    t, kdim = a.shape
    d = b.shape[1]
    tm = _pick(t, (512, 256, 128))

    def body(a_ref, b_ref, r_ref, t_ref, dy_ref, dyb_ref, acc_ref):
        @pl.when(pl.program_id(0) == 0)
        def _():
            acc_ref[...] = jnp.zeros_like(acc_ref)

        err = _dot(a_ref[...], b_ref[...], 1, 0) + r_ref[...] - t_ref[...]
        dy = err * (1.0 / d)
        dy_ref[...] = dy
        dyb_ref[...] = dy.astype(BF16)
        acc_ref[...] += jnp.sum(err * err, axis=0, keepdims=True)

    row = pl.BlockSpec((tm, d), lambda i: (i, 0))
    vec = pl.BlockSpec((1, d), lambda i: (0, 0))
    return pl.pallas_call(
        body, name=name, grid=(t // tm,),
        in_specs=[pl.BlockSpec((tm, kdim), lambda i: (i, 0)), pl.BlockSpec((kdim, d), lambda i: (0, 0)), row, row],
        out_specs=[row, row, vec],
        out_shape=[jax.ShapeDtypeStruct((t, d), F32), jax.ShapeDtypeStruct((t, d), BF16), jax.ShapeDtypeStruct((1, d), F32)],
        compiler_params=_cp(("arbitrary",)),
    )(a, b, res, target)


def _head_rms(xs, w, lo):
    sq = xs * xs
    s0 = jnp.sum(jnp.where(lo, sq, 0.0), axis=-1, keepdims=True)
    s1 = jnp.sum(jnp.where(lo, 0.0, sq), axis=-1, keepdims=True)
    r = jnp.where(lo, lax.rsqrt(s0 * (1.0 / HEAD_DIM) + EPS), lax.rsqrt(s1 * (1.0 / HEAD_DIM) + EPS))
    return xs * r, r


def _head_rms_bwd(xs, w, dy, lo):
    xh, r = _head_rms(xs, w, lo)
    dxh = dy * w
    t = dxh * xh
    m0 = jnp.sum(jnp.where(lo, t, 0.0), axis=-1, keepdims=True)
    m1 = jnp.sum(jnp.where(lo, 0.0, t), axis=-1, keepdims=True)
    mm = jnp.where(lo, m0, m1) * (1.0 / HEAD_DIM)
    return r * (dxh - xh * mm), dy * xh


_QSCALE = HEAD_DIM ** -0.5


def _headnorm_fwd(proj, ws, name):
    t = proj.shape[0]
    tm = TQ
    lead = PAD_A // tm
    leadb = PAD_B // tm

    def body(p_ref, w_ref, qa_ref, ka_ref, va_ref, qb_ref, kb_ref, vb_ref):
        data = pl.program_id(0) >= lead
        lo = _lo_mask()

        def put(ref, c, val):
            ref[:, c:c + val.shape[1]] = jnp.where(data, val, 0.0).astype(BF16)

        def per_query_head(slab):
            other = pltpu.roll(slab, HEAD_DIM, 1)
            e0, e1 = jnp.where(lo, slab, other), jnp.where(lo, other, slab)
            return jnp.concatenate([e0, e0, e1, e1], axis=1)

        for s in range(4):
            c = LANES * s
            xh, _ = _head_rms(p_ref[:, c:c + LANES], None, lo)
            qa_ref[:, c:c + LANES] = (xh * w_ref[0:1, :] * _QSCALE).astype(BF16)
            xh, _ = _head_rms(p_ref[:, 512 + c:512 + c + LANES], None, lo)
            put(ka_ref, c, xh * w_ref[1:2, :])
            xh, _ = _head_rms(p_ref[:, 1536 + c:1536 + c + LANES], None, lo)
            qb_ref[:, c:c + LANES] = (xh * w_ref[2:3, :] * _QSCALE).astype(BF16)
        put(va_ref, 0, p_ref[:, 1024:1536])
        xh, _ = _head_rms(p_ref[:, 2048:2176], None, lo)
        put(kb_ref, 0, per_query_head(xh * w_ref[3:4, :]))
        put(vb_ref, 0, per_query_head(p_ref[:, 2176:2304]))

    src = lambda i: jnp.maximum(i - lead, 0)
    wide = pl.BlockSpec((tm, 512), lambda i: (src(i), 0))
    pad_a = pl.BlockSpec((tm, 512), lambda i: (i, 0))
    pad_b = pl.BlockSpec((tm, 512), lambda i: (jnp.maximum(i - lead + leadb, 0), 0))
    sd = lambda rows: jax.ShapeDtypeStruct((rows, 512), BF16)
    return pl.pallas_call(
        body, name=name, grid=(t // tm + lead,),
        in_specs=[pl.BlockSpec((tm, 2304), lambda i: (src(i), 0)), pl.BlockSpec((4, LANES), lambda i: (0, 0))],
        out_specs=[wide, pad_a, pad_a, wide, pad_b, pad_b],
        out_shape=[sd(t), sd(t + PAD_A), sd(t + PAD_A), sd(t), sd(t + PAD_B), sd(t + PAD_B)],
        compiler_params=_cp(("arbitrary",)),
    )(proj, ws)


def _headnorm_bwd(proj, ws, dqa, dkpa, dvpa, dqb, dkpb, dvpb, name):
    t = proj.shape[0]
    tm = TQ
    offa, offb = PAD_A // tm, PAD_B // tm

    def body(p_ref, w_ref, dqa_ref, dka_ref, dva_ref, dqb_ref, dkb_ref, dvb_ref, dp_ref, dw_ref):
        i = pl.program_id(0)
        lo = _lo_mask()

        @pl.when(i == 0)
        def _():
            dw_ref[...] = jnp.zeros_like(dw_ref)

        acc = [jnp.zeros((1, LANES), F32) for _ in range(4)]
        for s in range(4):
            c = LANES * s
            dx, dwl = _head_rms_bwd(p_ref[:, c:c + LANES], w_ref[0:1, :], dqa_ref[:, c:c + LANES] * _QSCALE, lo)
            dp_ref[:, c:c + LANES] = dx.astype(BF16)
            acc[0] += jnp.sum(dwl, axis=0, keepdims=True)
            dx, dwl = _head_rms_bwd(p_ref[:, 512 + c:512 + c + LANES], w_ref[1:2, :], dka_ref[:, c:c + LANES], lo)
            dp_ref[:, 512 + c:512 + c + LANES] = dx.astype(BF16)
            acc[1] += jnp.sum(dwl, axis=0, keepdims=True)
            dx, dwl = _head_rms_bwd(p_ref[:, 1536 + c:1536 + c + LANES], w_ref[2:3, :], dqb_ref[:, c:c + LANES] * _QSCALE, lo)
            dp_ref[:, 1536 + c:1536 + c + LANES] = dx.astype(BF16)
            acc[2] += jnp.sum(dwl, axis=0, keepdims=True)
        dp_ref[:, 1024:1536] = dva_ref[...].astype(BF16)

        def group_sum(ref):
            s0 = ref[:, 0:128] + ref[:, 128:256]
            s1 = ref[:, 256:384] + ref[:, 384:512]
            s0 = s0 + pltpu.roll(s0, HEAD_DIM, 1)
            s1 = s1 + pltpu.roll(s1, HEAD_DIM, 1)
            return jnp.where(lo, s0, s1)

        dx, dwl = _head_rms_bwd(p_ref[:, 2048:2176], w_ref[3:4, :], group_sum(dkb_ref), lo)
        dp_ref[:, 2048:2176] = dx.astype(BF16)
        acc[3] += jnp.sum(dwl, axis=0, keepdims=True)
        dp_ref[:, 2176:2304] = group_sum(dvb_ref).astype(BF16)
        for n in range(4):
            dw_ref[n:n + 1, :] += acc[n]

    wide = pl.BlockSpec((tm, 512), lambda i: (i, 0))
    pa = pl.BlockSpec((tm, 512), lambda i: (i + offa, 0))
    pb = pl.BlockSpec((tm, 512), lambda i: (i + offb, 0))
    return pl.pallas_call(
        body, name=name, grid=(t // tm,),
        in_specs=[pl.BlockSpec((tm, 2304), lambda i: (i, 0)), pl.BlockSpec((4, LANES), lambda i: (0, 0)),
                  wide, pa, pa, wide, pb, pb],
        out_specs=[pl.BlockSpec((tm, 2304), lambda i: (i, 0)), pl.BlockSpec((4, LANES), lambda i: (0, 0))],
        out_shape=[jax.ShapeDtypeStruct((t, 2304), BF16), jax.ShapeDtypeStruct((4, LANES), F32)],
        compiler_params=_cp(("arbitrary",)),
    )(proj, ws, dqa, dkpa, dvpa, dqb, dkpb, dvpb)


ROLL_W = 1024


def _rel_onehot():
    r_io = lax.broadcasted_iota(jnp.int32, (REL_W, ROLL_W), 0)
    m_io = lax.broadcasted_iota(jnp.int32, (REL_W, ROLL_W), 1)
    return (r_io == jnp.clip(REL_W - 1 - m_io, -MAX_REL, MAX_REL) + MAX_REL).astype(F32)


def _relpos_fwd(table, name):
    def body(t_ref, o_ref):
        rr = _dot(t_ref[...], _rel_onehot(), 1, 0, HI)

        def step(q, c):
            o_ref[q] = pltpu.roll(rr, (ROLL_W - (TQ - 1) + q) % ROLL_W, 1)[:, :REL_W]
            return c

        lax.fori_loop(0, TQ, step, 0)

    return pl.pallas_call(
        body, name=name, out_shape=jax.ShapeDtypeStruct((TQ, N_HEADS, REL_W), F32),
        in_specs=[pl.BlockSpec(memory_space=pltpu.VMEM)], out_specs=pl.BlockSpec(memory_space=pltpu.VMEM),
        compiler_params=_cp(),
    )(table)


def _relpos_bwd(dbias_t, name):
    def body(d_ref, o_ref):
        def step(q, acc):
            row = jnp.concatenate([d_ref[q], jnp.zeros((N_HEADS, ROLL_W - REL_W), F32)], axis=1)
            return acc + pltpu.roll(row, TQ - 1 - q, 1)

        drr = lax.fori_loop(0, TQ, step, jnp.zeros((N_HEADS, ROLL_W), F32))
        o_ref[...] = _dot(drr, _rel_onehot(), 1, 1, HI)

    return pl.pallas_call(
        body, name=name, out_shape=jax.ShapeDtypeStruct((N_HEADS, REL_W), F32),
        in_specs=[pl.BlockSpec(memory_space=pltpu.VMEM)], out_specs=pl.BlockSpec(memory_space=pltpu.VMEM),
        compiler_params=_cp(),
    )(dbias_t)


def _attn_probs(qe, kw, bias, kvalid, snk):
    s = _dot(qe, kw, 1, 1) + bias
    s = jnp.where(kvalid, s, NEG)
    m = jnp.maximum(jnp.max(s, axis=-1, keepdims=True), snk)
    p = jnp.exp(s - m)
    inv = 1.0 / (jnp.sum(p, axis=-1, keepdims=True) + jnp.exp(snk - m))
    return p * inv, jnp.exp(snk - m) * inv


def _attn_fwd(q, kp, vp, bias, sinks, pad, name, comm=None):
    t, hd = q.shape
    w = pad + TQ

    def body(sink_ref, q_ref, k_ref, v_ref, b_ref, o_ref):
        hp, i = pl.program_id(0), pl.program_id(1)
        lo = _lo_mask()
        for j in range(ATT_SUB):
            start = pl.multiple_of((i * ATT_SUB + j) * TQ, TQ)
            qv = q_ref[TQ * j:TQ * (j + 1), :]
            kw = k_ref[pl.ds(start, w), :]
            vw = v_ref[pl.ds(start, w), :]
            kvalid = (start + lax.broadcasted_iota(jnp.int32, (1, w), 1)) >= pad
            outs = []
            for e in range(2):
                sel = lo if e == 0 else jnp.logical_not(lo)
                qe = jnp.where(sel, qv, jnp.zeros_like(qv))
                p, _ = _attn_probs(qe, kw, b_ref[e], kvalid, sink_ref[2 * hp + e])
                outs.append(_dot(p.astype(BF16), vw, 1, 0))
            o_ref[TQ * j:TQ * (j + 1), :] = jnp.where(lo, outs[0], outs[1]).astype(BF16)

    full = pl.BlockSpec((t + pad, LANES), lambda h, i: (0, h))
    tile = pl.BlockSpec((ATT_SUB * TQ, LANES), lambda h, i: (i, h))
    (o,), got = _call(
        body, name=name, grid=(hd // LANES, t // (ATT_SUB * TQ)),
        in_specs=[pl.BlockSpec(memory_space=pltpu.SMEM), tile, full, full, pl.BlockSpec((2, TQ, w), lambda h, i: (h, 0, 0))],
        out_specs=[tile], out_shape=[jax.ShapeDtypeStruct((t, hd), BF16)],
        args=(sinks, q, kp, vp, bias), sem=("parallel", "arbitrary"), comm=comm)
    return o, got


def _attn_bwd(q, kp, vp, bias, sinks, do, col_off, pad, name, comm=None):
    t, hd = q.shape
    w = pad + TQ
    nhp = hd // LANES

    def body(sink_ref, q_ref, k_ref, v_ref, b_ref, do_ref, dq_ref, dk_ref, dv_ref, db_ref, ds_ref):
        hp, i = pl.program_id(0), pl.program_id(1)

        @pl.when(i == 0)
        def _():
            dk_ref[...] = jnp.zeros_like(dk_ref)
            dv_ref[...] = jnp.zeros_like(dv_ref)
            db_ref[...] = jnp.zeros_like(db_ref)
            ds_ref[...] = jnp.zeros_like(ds_ref)

        lo = _lo_mask()
        row8 = lax.broadcasted_iota(jnp.int32, (8, LANES), 0)
        dbias = [None, None]
        dsink = jnp.zeros((8, LANES), F32)
        for j in range(ATT_SUB):
            start = pl.multiple_of((i * ATT_SUB + j) * TQ, TQ)
            qv = q_ref[TQ * j:TQ * (j + 1), :]
            dov = do_ref[TQ * j:TQ * (j + 1), :]
            kw = k_ref[pl.ds(start, w), :]
            vw = v_ref[pl.ds(start, w), :]
            kvalid = (start + lax.broadcasted_iota(jnp.int32, (1, w), 1)) >= pad
            dqs, dkw, dvw = [], None, None
            for e in range(2):
                sel = lo if e == 0 else jnp.logical_not(lo)
                qe = jnp.where(sel, qv, jnp.zeros_like(qv))
                doe = jnp.where(sel, dov, jnp.zeros_like(dov))
                p, psink = _attn_probs(qe, kw, b_ref[e], kvalid, sink_ref[2 * hp + e])
                dp = _dot(doe, vw, 1, 1)
                delta = jnp.sum(p * dp, axis=-1, keepdims=True)
                ds = p * (dp - delta)
                dbias[e] = ds if dbias[e] is None else dbias[e] + ds
                dsink = dsink + jnp.where(row8 == e, jnp.sum(-psink * delta, axis=0, keepdims=True), 0.0)
                dsb = ds.astype(BF16)
                dqs.append(_dot(dsb, kw, 1, 0))
                dk_e = _dot(dsb, qe, 0, 0)
                dv_e = _dot(p.astype(BF16), doe, 0, 0)
                dkw = dk_e if dkw is None else dkw + dk_e
                dvw = dv_e if dvw is None else dvw + dv_e
            dq_ref[TQ * j:TQ * (j + 1), :] = jnp.where(lo, dqs[0], dqs[1])
            dk_ref[pl.ds(start, w), :] += dkw
            dv_ref[pl.ds(start, w), :] += dvw
        for e in range(2):
            db_ref[e] += dbias[e]
        ds_ref[0] += dsink

    full = pl.BlockSpec((t + pad, LANES), lambda h, i: (0, h))
    tile = pl.BlockSpec((ATT_SUB * TQ, LANES), lambda h, i: (i, h))
    btile = pl.BlockSpec((2, TQ, w), lambda h, i: (h, 0, 0))
    return _call(
        body, name=name, grid=(nhp, t // (ATT_SUB * TQ)),
        in_specs=[pl.BlockSpec(memory_space=pltpu.SMEM), tile, full, full, btile,
                  pl.BlockSpec((ATT_SUB * TQ, LANES), lambda h, i: (i, h + col_off))],
        out_specs=[tile, full, full, btile, pl.BlockSpec((1, 8, LANES), lambda h, i: (h, 0, 0))],
        out_shape=[jax.ShapeDtypeStruct((t, hd), F32), jax.ShapeDtypeStruct((t + pad, hd), F32),
                   jax.ShapeDtypeStruct((t + pad, hd), F32), jax.ShapeDtypeStruct((N_HEADS, TQ, w), F32),
                   jax.ShapeDtypeStruct((nhp, 8, LANES), F32)],
        args=(sinks, q, kp, vp, bias, do), sem=("parallel", "arbitrary"), comm=comm)


def _halo_prev(tm):
    return lambda i: jnp.maximum(i * (tm // 8) - 1, 0)


def _halo_next(tm, t):
    return lambda i: jnp.minimum((i + 1) * (tm // 8), t // 8 - 1)


def _taps_prev(tile, halo, ktaps, first):
    tm = tile.shape[0]
    ext = jnp.concatenate([jnp.where(first, 0.0, halo), tile], axis=0)
    return [tile] + [pltpu.roll(ext, s, 0)[8:8 + tm] for s in range(1, ktaps)]


def _conv_apply(taps, w_ref, ktaps):
    out = taps[0] * w_ref[ktaps - 1:ktaps, :]
    for s in range(1, ktaps):
        out = out + taps[s] * w_ref[ktaps - 1 - s:ktaps - s, :]
    return out


def _sigmoid(x):
    return jax.nn.sigmoid(x)


def _silu_grad(x):
    sg = _sigmoid(x)
    return x * sg, sg * (1.0 + x * (1.0 - sg))


FFN_TM = 128
FFN_HALO = 16


def _ffn_mid_fwd(gu, w8, b, name):
    t = gu.shape[0]
    f = D_FF
    tm, hr = FFN_TM, FFN_HALO

    def body(g_ref, u_ref, h_ref, w_ref, b_ref, a_ref):
        first = pl.program_id(0) == 0
        ext = jnp.concatenate([jnp.where(first, 0.0, h_ref[...].astype(F32)), g_ref[...].astype(F32)], axis=0)
        taps = [ext[hr:]] + [pltpu.roll(ext, s, 0)[hr:] for s in (1, 2)]
        gc = _conv_apply(taps, w_ref, 3) + b_ref[...]
        a_ref[...] = (gc * _sigmoid(gc) * u_ref[...].astype(F32)).astype(BF16)

    return pl.pallas_call(
        body, name=name, grid=(t // tm,),
        in_specs=[pl.BlockSpec((tm, f), lambda i: (i, 0)), pl.BlockSpec((tm, f), lambda i: (i, 1)),
                  pl.BlockSpec((hr, f), lambda i: (jnp.maximum(i * (tm // hr) - 1, 0), 0)),
                  pl.BlockSpec((8, f), lambda i: (0, 0)), pl.BlockSpec((1, f), lambda i: (0, 0))],
        out_specs=pl.BlockSpec((tm, f), lambda i: (i, 0)), out_shape=jax.ShapeDtypeStruct((t, f), BF16),
        compiler_params=_cp(("parallel",)),
    )(gu, gu, gu, w8, b)


FFN_BT = 256
FFN_BC = 1408


def _ffn_mid_bwd(gu, dxb, w_out, w8, b, name):
    t, d = dxb.shape
    f = D_FF
    tm, hr = FFN_BT, FFN_HALO
    nt = t // tm
    n = tm + hr

    def body(g_ref, u_ref, gp_ref, gn_ref, un_ref, dx_ref, dxn_ref, wo_ref, w_ref, b_ref, dg_ref, du_ref, dw_ref, db_ref):
        i = pl.program_id(0)
        first, last = i == 0, i == nt - 1

        @pl.when(first)
        def _():
            dw_ref[...] = jnp.zeros_like(dw_ref)
            db_ref[...] = jnp.zeros_like(db_ref)

        dxe = jnp.concatenate([dx_ref[...], dxn_ref[...]], axis=0)
        row = lax.broadcasted_iota(jnp.int32, (n, 1), 0)
        keep = (row < tm) | jnp.logical_not(last)
        for c in range(0, f, FFN_BC):
            cs = slice(c, c + FFN_BC)
            ext = jnp.concatenate([jnp.where(first, 0.0, gp_ref[:, cs].astype(F32)), g_ref[:, cs].astype(F32),
                                   gn_ref[:, cs].astype(F32)], axis=0)
            taps = [ext[hr:]] + [pltpu.roll(ext, s, 0)[hr:] for s in (1, 2)]
            gc = b_ref[:, cs] + taps[0] * w_ref[2:3, cs] + taps[1] * w_ref[1:2, cs] + taps[2] * w_ref[0:1, cs]
            act, dact = _silu_grad(gc)
            da = _dot(dxe, wo_ref[cs, :], 1, 1)
            up = jnp.concatenate([u_ref[:, cs], un_ref[:, cs]], axis=0).astype(F32)
            dgc = jnp.where(keep, da * up * dact, 0.0)
            du_ref[:, cs] = (da[:tm] * act[:tm]).astype(BF16)
            dg_ref[:, cs] = (dgc[:tm] * w_ref[2:3, cs] + pltpu.roll(dgc, n - 1, 0)[:tm] * w_ref[1:2, cs]
                             + pltpu.roll(dgc, n - 2, 0)[:tm] * w_ref[0:1, cs]).astype(BF16)
            db_ref[:, cs] += jnp.sum(dgc[:tm], axis=0, keepdims=True)
            for s in range(3):
                dw_ref[2 - s:3 - s, cs] += jnp.sum(dgc[:tm] * taps[s][:tm], axis=0, keepdims=True)

    r = tm // hr
    prev = lambda i: jnp.maximum(i * r - 1, 0)
    nxt_blk = lambda i: jnp.minimum((i + 1) * r, t // hr - 1)
    row_f = pl.BlockSpec((tm, f), lambda i: (i, 0))
    return pl.pallas_call(
        body, name=name, grid=(nt,),
        in_specs=[row_f, pl.BlockSpec((tm, f), lambda i: (i, 1)),
                  pl.BlockSpec((hr, f), lambda i: (prev(i), 0)), pl.BlockSpec((hr, f), lambda i: (nxt_blk(i), 0)),
                  pl.BlockSpec((hr, f), lambda i: (nxt_blk(i), 1)),
                  pl.BlockSpec((tm, d), lambda i: (i, 0)), pl.BlockSpec((hr, d), lambda i: (nxt_blk(i), 0)),
                  pl.BlockSpec((f, d), lambda i: (0, 0)),
                  pl.BlockSpec((8, f), lambda i: (0, 0)), pl.BlockSpec((1, f), lambda i: (0, 0))],
        out_specs=[row_f, row_f, pl.BlockSpec((8, f), lambda i: (0, 0)), pl.BlockSpec((1, f), lambda i: (0, 0))],
        out_shape=[jax.ShapeDtypeStruct((t, f), BF16), jax.ShapeDtypeStruct((t, f), BF16),
                   jax.ShapeDtypeStruct((8, f), F32), jax.ShapeDtypeStruct((1, f), F32)],
        compiler_params=_cp(("arbitrary",)),
    )(gu, gu, gu, gu, gu, dxb, dxb, w_out, w8, b)


PRE_TM = 256
PRE_TC = 1024


def _ssm_pre_fwd(zx, w8, b, name):
    t = zx.shape[0]
    tm, tc = PRE_TM, PRE_TC
    off = D_INNER // tc

    def body(x_ref, h_ref, w_ref, b_ref, o_ref):
        first = pl.program_id(0) == 0
        c = _conv_apply(_taps_prev(x_ref[...], h_ref[...], 4, first), w_ref, 4) + b_ref[...]
        o_ref[...] = c * _sigmoid(c)

    hp = _halo_prev(tm)
    return pl.pallas_call(
        body, name=name, grid=(t // tm, XBC // tc),
        in_specs=[pl.BlockSpec((tm, tc), lambda i, j: (i, j + off)), pl.BlockSpec((8, tc), lambda i, j: (hp(i), j + off)),
                  pl.BlockSpec((8, tc), lambda i, j: (0, j)), pl.BlockSpec((1, tc), lambda i, j: (0, j))],
        out_specs=pl.BlockSpec((tm, tc), lambda i, j: (i, j)), out_shape=jax.ShapeDtypeStruct((t, XBC), F32),
        compiler_params=_cp(("parallel", "parallel")),
    )(zx, zx, w8, b)


def _ssm_pre_bwd(zx, dxbc, w8, b, name):
    t = zx.shape[0]
    tm, tc = PRE_TM, PRE_TC
    off = D_INNER // tc
    nt = t // tm
    n = tm + 8

    def body(x_ref, xp_ref, xn_ref, d_ref, dn_ref, w_ref, b_ref, o_ref, dw_ref, db_ref):
        i = pl.program_id(1)
        first, last = i == 0, i == nt - 1

        @pl.when(first)
        def _():
            dw_ref[...] = jnp.zeros_like(dw_ref)
            db_ref[...] = jnp.zeros_like(db_ref)

        ext = jnp.concatenate([jnp.where(first, 0.0, xp_ref[...]), x_ref[...], xn_ref[...]], axis=0)
        taps = [ext[8:8 + n]] + [pltpu.roll(ext, s, 0)[8:8 + n] for s in (1, 2, 3)]
        c = _conv_apply(taps, w_ref, 4) + b_ref[...]
        _, dact = _silu_grad(c)
        row = lax.broadcasted_iota(jnp.int32, (n, 1), 0)
        dc = jnp.where((row < tm) | jnp.logical_not(last), jnp.concatenate([d_ref[...], dn_ref[...]], axis=0) * dact, 0.0)
        nxt = [dc[:tm]] + [pltpu.roll(dc, n - s, 0)[:tm] for s in (1, 2, 3)]
        o_ref[...] = _conv_apply(nxt, w_ref, 4).astype(BF16)
        db_ref[...] += jnp.sum(dc[:tm], axis=0, keepdims=True)
        for s in range(4):
            dw_ref[3 - s:4 - s, :] += jnp.sum(dc[:tm] * taps[s][:tm], axis=0, keepdims=True)

    hp = _halo_prev(tm)
    hn = _halo_next(tm, t)
    return pl.pallas_call(
        body, name=name, grid=(XBC // tc, nt),
        in_specs=[pl.BlockSpec((tm, tc), lambda j, i: (i, j + off)), pl.BlockSpec((8, tc), lambda j, i: (hp(i), j + off)),
                  pl.BlockSpec((8, tc), lambda j, i: (hn(i), j + off)),
                  pl.BlockSpec((tm, tc), lambda j, i: (i, j)), pl.BlockSpec((8, tc), lambda j, i: (hn(i), j)),
                  pl.BlockSpec((8, tc), lambda j, i: (0, j)), pl.BlockSpec((1, tc), lambda j, i: (0, j))],
        out_specs=[pl.BlockSpec((tm, tc), lambda j, i: (i, j)), pl.BlockSpec((8, tc), lambda j, i: (0, j)),
                   pl.BlockSpec((1, tc), lambda j, i: (0, j))],
        out_shape=[jax.ShapeDtypeStruct((t, XBC), BF16), jax.ShapeDtypeStruct((8, XBC), F32),
                   jax.ShapeDtypeStruct((1, XBC), F32)],
        compiler_params=_cp(("parallel", "arbitrary")),
    )(zx, zx, zx, dxbc, dxbc, w8, b)


def _head_lanes():
    return lax.broadcasted_iota(jnp.int32, (1, LANES), 1) < SSM_HEADS


def _dt_fwd(dtraw, bias, name):
    t = dtraw.shape[0]
    tm = _pick(t, (1024, 512, 256, 128))

    def body(x_ref, b_ref, o_ref):
        v = x_ref[...] + b_ref[...]
        sp = jnp.maximum(v, 0.0) + jnp.log(1.0 + jnp.exp(-jnp.abs(v)))
        o_ref[...] = jnp.where(_head_lanes(), sp, 0.0)

    row = pl.BlockSpec((tm, LANES), lambda i: (i, 0))
    return pl.pallas_call(
        body, name=name, grid=(t // tm,), in_specs=[row, pl.BlockSpec((1, LANES), lambda i: (0, 0))], out_specs=row,
        out_shape=jax.ShapeDtypeStruct((t, LANES), F32), compiler_params=_cp(("parallel",)),
    )(dtraw, bias)


def _dt_bwd(dtraw, bias, ddt, name):
    t = dtraw.shape[0]
    tm = _pick(t, (1024, 512, 256, 128))

    def body(x_ref, b_ref, d_ref, o_ref, db_ref):
        @pl.when(pl.program_id(0) == 0)
        def _():
            db_ref[...] = jnp.zeros_like(db_ref)

        g = jnp.where(_head_lanes(), d_ref[...] * _sigmoid(x_ref[...] + b_ref[...]), 0.0)
        o_ref[...] = g.astype(BF16)
        db_ref[...] += jnp.sum(g, axis=0, keepdims=True)

    row = pl.BlockSpec((tm, LANES), lambda i: (i, 0))
    vec = pl.BlockSpec((1, LANES), lambda i: (0, 0))
    return pl.pallas_call(
        body, name=name, grid=(t // tm,), in_specs=[row, vec, row], out_specs=[row, vec],
        out_shape=[jax.ShapeDtypeStruct((t, LANES), BF16), jax.ShapeDtypeStruct((1, LANES), F32)],
        compiler_params=_cp(("arbitrary",)),
    )(dtraw, bias, ddt)


GROUP_W = D_INNER // SSM_GROUPS
POST_TM = 512


def _ssm_post_fwd(y, xbc, zx, dexp, nw, name):
    t = y.shape[0]
    tm = _pick(t, (POST_TM, 256, 128))

    def body(y_ref, x_ref, z_ref, d_ref, w_ref, o_ref):
        zv = z_ref[...]
        y3 = (y_ref[...] + d_ref[...] * x_ref[...]) * (zv * _sigmoid(zv))
        r = lax.rsqrt(jnp.mean(y3 * y3, axis=-1, keepdims=True) + EPS)
        o_ref[...] = (y3 * r * w_ref[...]).astype(BF16)

    blk = pl.BlockSpec((tm, GROUP_W), lambda i, g: (i, g))
    vec = pl.BlockSpec((1, GROUP_W), lambda i, g: (0, g))
    return pl.pallas_call(
        body, name=name, grid=(t // tm, SSM_GROUPS), in_specs=[blk, blk, blk, vec, vec], out_specs=blk,
        out_shape=jax.ShapeDtypeStruct((t, D_INNER), BF16), compiler_params=_cp(("parallel", "parallel")),
    )(y, xbc, zx, dexp, nw)


def _ssm_post_bwd(dy4, y, xbc, zx, dexp, nw, name):
    t = y.shape[0]
    tm = _pick(t, (POST_TM, 256, 128))

    def body(g_ref, y_ref, x_ref, z_ref, d_ref, w_ref, dy_ref, dxs_ref, dz_ref, dd_ref, dw_ref):
        @pl.when(pl.program_id(1) == 0)
        def _():
            dd_ref[...] = jnp.zeros_like(dd_ref)
            dw_ref[...] = jnp.zeros_like(dw_ref)

        zv = z_ref[...]
        xv = x_ref[...]
        act, dact = _silu_grad(zv)
        y2 = y_ref[...] + d_ref[...] * xv
        y3 = y2 * act
        r = lax.rsqrt(jnp.mean(y3 * y3, axis=-1, keepdims=True) + EPS)
        y3n = y3 * r
        gv = g_ref[...]
        dyn = gv * w_ref[...]
        dy3 = r * (dyn - y3n * jnp.mean(dyn * y3n, axis=-1, keepdims=True))
        dy2 = dy3 * act
        dy_ref[...] = dy2
        dxs_ref[...] = dy2 * d_ref[...]
        dz_ref[...] = (dy3 * y2 * dact).astype(BF16)
        dd_ref[...] += jnp.sum(dy2 * xv, axis=0, keepdims=True)
        dw_ref[...] += jnp.sum(gv * y3n, axis=0, keepdims=True)

    blk = pl.BlockSpec((tm, GROUP_W), lambda g, i: (i, g))
    vec = pl.BlockSpec((1, GROUP_W), lambda g, i: (0, g))
    return pl.pallas_call(
        body, name=name, grid=(SSM_GROUPS, t // tm), in_specs=[blk, blk, blk, blk, vec, vec],
        out_specs=[blk, blk, blk, vec, vec],
        out_shape=[jax.ShapeDtypeStruct((t, D_INNER), F32), jax.ShapeDtypeStruct((t, D_INNER), F32),
                   jax.ShapeDtypeStruct((t, D_INNER), BF16), jax.ShapeDtypeStruct((1, D_INNER), F32),
                   jax.ShapeDtypeStruct((1, D_INNER), F32)],
        compiler_params=_cp(("parallel", "arbitrary")),
    )(dy4, y, xbc, zx, dexp, nw)


def _ssd_common(dt, alog):
    ll = dt.shape[0]
    a_neg = -jnp.exp(alog)
    a = dt * a_neg
    ri = lax.broadcasted_iota(jnp.int32, (ll, ll), 0)
    ci = lax.broadcasted_iota(jnp.int32, (ll, ll), 1)
    tril = ri >= ci
    acs = _dot(tril.astype(F32), a, 1, 0, HI)
    return a_neg, tril, acs, acs.T


def _pair_terms(acs, acs_t, dt, h0, lo):
    ll = acs.shape[0]
    cols = [acs[:, h0 + e:h0 + e + 1] for e in range(2)]
    rows = [acs_t[h0 + e:h0 + e + 1, :] for e in range(2)]
    dtc = [dt[:, h0 + e:h0 + e + 1] for e in range(2)]
    lasts = [c[ll - 1:ll, :] for c in cols]
    dtx = jnp.where(lo, dtc[0], dtc[1])
    eac = jnp.where(lo, jnp.exp(cols[0]), jnp.exp(cols[1]))
    fdec = jnp.where(lo, jnp.exp(lasts[0] - cols[0]), jnp.exp(lasts[1] - cols[1]))
    elast = jnp.where(lo, jnp.exp(lasts[0]), jnp.exp(lasts[1]))
    return cols, rows, dtx, eac, fdec, elast


def _decay(col, row, tril):
    return jnp.where(tril, jnp.exp(jnp.minimum(col - row, 0.0)), 0.0)


def _two_heads_rows(v, lo):
    z = jnp.zeros_like(v)
    return jnp.concatenate([jnp.where(lo, v, z), jnp.where(lo, z, v)], axis=0)


def _two_heads_cols(ms):
    return jnp.concatenate(ms, axis=1)


def _ssd_fwd(xbc, dt, alog, name, comm=None):
    t = xbc.shape[0]
    ll = SSD_L
    nc = t // ll

    def body(x_ref, dt_ref, al_ref, y_ref, sp_ref, st_ref):
        @pl.when(pl.program_id(0) == 0)
        def _():
            st_ref[...] = jnp.zeros_like(st_ref)

        dtv = dt_ref[...]
        _, tril, acs, acs_t = _ssd_common(dtv, al_ref[...])
        lo = _lo_mask()
        sp_ref[0] = st_ref[...]
        for g in range(SSM_GROUPS):
            bg = x_ref[:, D_INNER + SSM_STATE * g:D_INNER + SSM_STATE * (g + 1)].astype(BF16)
            cg = x_ref[:, D_INNER + 512 + SSM_STATE * g:D_INNER + 512 + SSM_STATE * (g + 1)].astype(BF16)
            gm = _dot(cg, bg, 1, 1)
            g0 = GROUP_W * g
            terms = [_pair_terms(acs, acs_t, dtv, 8 * g + 2 * pp, lo) for pp in range(4)]
            dtx, eac, fdec, elast = [jnp.concatenate([tt[k] for tt in terms], axis=1) for k in (2, 3, 4, 5)]
            xg = x_ref[:, g0:g0 + GROUP_W]
            ug = (xg * dtx).astype(BF16)
            sg = st_ref[:, g0:g0 + GROUP_W]
            yst = _dot(cg, sg.astype(BF16), 1, 0) * eac
            st_ref[:, g0:g0 + GROUP_W] = sg * elast + _dot(bg, (xg * (fdec * dtx)).astype(BF16), 0, 0)
            for pp in range(4):
                cols, rows = terms[pp][0], terms[pp][1]
                sl = slice(LANES * pp, LANES * (pp + 1))
                y_in = _dot(_two_heads_cols([(gm * _decay(cols[e], rows[e], tril)).astype(BF16) for e in range(2)]),
                            _two_heads_rows(ug[:, sl], lo), 1, 0)
                y_ref[:, g0 + LANES * pp:g0 + LANES * (pp + 1)] = y_in + yst[:, sl]

    return _call(
        body, name=name, grid=(nc,),
        in_specs=[pl.BlockSpec((ll, XBC), lambda c: (c, 0)), pl.BlockSpec((ll, LANES), lambda c: (c, 0)),
                  pl.BlockSpec((1, LANES), lambda c: (0, 0))],
        out_specs=[pl.BlockSpec((ll, D_INNER), lambda c: (c, 0)), pl.BlockSpec((1, SSM_STATE, D_INNER), lambda c: (c, 0, 0))],
        out_shape=[jax.ShapeDtypeStruct((t, D_INNER), F32), jax.ShapeDtypeStruct((nc, SSM_STATE, D_INNER), F32)],
        scratch_shapes=[pltpu.VMEM((SSM_STATE, D_INNER), F32)],
        args=(xbc, dt, alog), sem=("arbitrary",), comm=comm)


def _ssd_bwd(xbc, dt, alog, sprev, dy, dskip, name, comm=None):
    t = xbc.shape[0]
    ll = SSD_L
    nc = t // ll

    def body(x_ref, dt_ref, al_ref, sp_ref, dy_ref, dk_ref, dx_ref, ddt_ref, dal_ref, ds_ref, colt_ref):
        @pl.when(pl.program_id(0) == 0)
        def _():
            ds_ref[...] = jnp.zeros_like(ds_ref)
            dal_ref[...] = jnp.zeros_like(dal_ref)

        dtv = dt_ref[...]
        a_neg, tril, acs, acs_t = _ssd_common(dtv, al_ref[...])
        lo = _lo_mask()
        hi = jnp.logical_not(lo)
        lane = lax.broadcasted_iota(jnp.int32, (1, LANES), 1)
        colt_ref[...] = jnp.zeros_like(colt_ref)
        rowterm = jnp.zeros((ll, LANES), F32)
        ddt_u = jnp.zeros((ll, LANES), F32)
        dlast = jnp.zeros((1, LANES), F32)

        def halves(v):
            return (jnp.sum(jnp.where(lo, v, 0.0), axis=-1, keepdims=True),
                    jnp.sum(jnp.where(hi, v, 0.0), axis=-1, keepdims=True))

        for g in range(SSM_GROUPS):
            cb0 = D_INNER + SSM_STATE * g
            cc0 = D_INNER + 512 + SSM_STATE * g
            bg = x_ref[:, cb0:cb0 + SSM_STATE].astype(BF16)
            cg = x_ref[:, cc0:cc0 + SSM_STATE].astype(BF16)
            gm = _dot(cg, bg, 1, 1)
            g0 = GROUP_W * g
            terms = [_pair_terms(acs, acs_t, dtv, 8 * g + 2 * pp, lo) for pp in range(4)]
            dtx, eac, fdec, elast = [jnp.concatenate([tt[k] for tt in terms], axis=1) for k in (2, 3, 4, 5)]
            xg = x_ref[:, g0:g0 + GROUP_W]
            u32 = xg * dtx
            ug = u32.astype(BF16)
            dyg = dy_ref[:, g0:g0 + GROUP_W]
            dyb = dyg.astype(BF16)
            spg = sp_ref[0, :, g0:g0 + GROUP_W]
            spb = spg.astype(BF16)
            dsg = ds_ref[:, g0:g0 + GROUP_W]
            dsb = dsg.astype(BF16)
            du_st = _dot(bg, dsb, 1, 0) * fdec
            yst = _dot(cg, spb, 1, 0) * eac
            dye = (dyg * eac).astype(BF16)
            dc_st = _dot(dye, spb, 1, 1)
            db_st = _dot((xg * (fdec * dtx)).astype(BF16), dsb, 1, 1)
            ds_ref[:, g0:g0 + GROUP_W] = dsg * elast + _dot(cg, dye, 0, 0)
            qst_el = du_st * u32
            rq_el = dyg * yst - qst_el
            q_row = jnp.sum(qst_el, axis=0, keepdims=True)
            s_row = jnp.sum(dsg * spg, axis=0, keepdims=True)
            dgm = jnp.zeros((ll, ll), F32)
            for pp in range(4):
                h0 = 8 * g + 2 * pp
                cols, rows = terms[pp][0], terms[pp][1]
                sl = slice(LANES * pp, LANES * (pp + 1))
                decs = [_decay(cols[e], rows[e], tril) for e in range(2)]
                wms = [gm * d for d in decs]
                dum2 = _dot(dyb[:, sl], _two_heads_rows(ug[:, sl], lo), 1, 1)
                du = _dot(jnp.concatenate([wm.astype(BF16) for wm in wms], axis=0),
                          _two_heads_rows(dyb[:, sl], lo), 0, 0) + du_st[:, sl]
                dx_ref[:, g0 + LANES * pp:g0 + LANES * (pp + 1)] = du * dtx[:, sl] + dk_ref[:, g0 + LANES * pp:g0 + LANES * (pp + 1)]
                ddtu = halves(du * xg[:, sl])
                rq = halves(rq_el[:, sl])
                qs = halves(q_row[:, sl])
                ss = halves(s_row[:, sl])
                for e in range(2):
                    dum = dum2[:, ll * e:ll * (e + 1)]
                    dgm = dgm + dum * decs[e]
                    tm_ = dum * wms[e]
                    oh = lane == (h0 + e)
                    rowterm = rowterm + jnp.where(oh, jnp.sum(tm_, axis=1, keepdims=True) + rq[e], 0.0)
                    ddt_u = ddt_u + jnp.where(oh, ddtu[e], 0.0)
                    dlast = dlast + jnp.where(oh, jnp.exp(cols[e][ll - 1:ll, :]) * ss[e] + qs[e], 0.0)
                    colt_ref[h0 + e:h0 + e + 1, :] = jnp.sum(tm_, axis=0, keepdims=True)
            dgb = dgm.astype(BF16)
            dx_ref[:, cc0:cc0 + SSM_STATE] = _dot(dgb, bg, 1, 0) + dc_st
            dx_ref[:, cb0:cb0 + SSM_STATE] = _dot(dgb, cg, 0, 0) + db_st
        row_io = lax.broadcasted_iota(jnp.int32, (ll, LANES), 0)
        dacs = rowterm - colt_ref[...].T + jnp.where(row_io == ll - 1, dlast, 0.0)
        da = _dot(jnp.logical_not(tril).astype(F32) + jnp.where(
            lax.broadcasted_iota(jnp.int32, (ll, ll), 0) == lax.broadcasted_iota(jnp.int32, (ll, ll), 1), 1.0, 0.0),
            dacs, 1, 0, HI)
        ddt_ref[...] = da * a_neg + ddt_u
        dal_ref[...] += jnp.sum(da * dtv, axis=0, keepdims=True) * a_neg

    rev = lambda c: nc - 1 - c
    return _call(
        body, name=name, grid=(nc,),
        in_specs=[pl.BlockSpec((ll, XBC), lambda c: (rev(c), 0)), pl.BlockSpec((ll, LANES), lambda c: (rev(c), 0)),
                  pl.BlockSpec((1, LANES), lambda c: (0, 0)),
                  pl.BlockSpec((1, SSM_STATE, D_INNER), lambda c: (rev(c), 0, 0)),
                  pl.BlockSpec((ll, D_INNER), lambda c: (rev(c), 0)), pl.BlockSpec((ll, D_INNER), lambda c: (rev(c), 0))],
        out_specs=[pl.BlockSpec((ll, XBC), lambda c: (rev(c), 0)), pl.BlockSpec((ll, LANES), lambda c: (rev(c), 0)),
                   pl.BlockSpec((1, LANES), lambda c: (0, 0))],
        out_shape=[jax.ShapeDtypeStruct((t, XBC), F32), jax.ShapeDtypeStruct((t, LANES), F32),
                   jax.ShapeDtypeStruct((1, LANES), F32)],
        scratch_shapes=[pltpu.VMEM((SSM_STATE, D_INNER), F32), pltpu.VMEM((LANES, ll), F32)],
        args=(xbc, dt, alog, sprev, dy, dskip), sem=("arbitrary",), comm=comm)


ADAM_TR = 512


def _sum_parts(parts, name):
    nparts, r, c = parts.shape
    tc = _pick(c, (256, 128))

    def body(p_ref, o_ref):
        g = p_ref[0].astype(F32)
        for k in range(1, nparts):
            g = g + p_ref[k].astype(F32)
        o_ref[...] = g

    return pl.pallas_call(
        body, name=name, grid=(c // tc,), in_specs=[pl.BlockSpec((nparts, r, tc), lambda j: (0, 0, j))],
        out_specs=pl.BlockSpec((r, tc), lambda j: (0, j)), out_shape=jax.ShapeDtypeStruct((r, c), F32),
        compiler_params=_cp(("parallel",)),
    )(parts)


def _adamw(parts, w, m, v, name):
    nl, r, c = w.shape
    assert len(parts) == nl
    tr = _pick(r, (256, 128, 64))
    c1 = 1.0 - ADAM_B1 ** ADAM_STEP
    c2 = 1.0 - ADAM_B2 ** ADAM_STEP

    def body(*refs):
        p_refs = refs[:nl]
        w_ref, m_ref, v_ref, g_ref, d_ref, mo_ref, vo_ref = refs[nl:]
        g = None
        for l, p_ref in enumerate(p_refs):
            s = p_ref[0].astype(F32)
            for k in range(1, p_ref.shape[0]):
                s = s + p_ref[k].astype(F32)
            g = s if g is None else jnp.where(pl.program_id(0) == l, s, g)
        mn = ADAM_B1 * m_ref[0] + (1.0 - ADAM_B1) * g
        vn = ADAM_B2 * v_ref[0] + (1.0 - ADAM_B2) * (g * g)
        g_ref[0] = g
        mo_ref[0] = mn
        vo_ref[0] = vn
        d_ref[0] = -ADAM_LR * ((mn / c1) / (jnp.sqrt(vn / c2) + ADAM_EPS) + ADAM_WD * w_ref[0])

    row = pl.BlockSpec((1, tr, c), lambda l, i: (l, i, 0))
    sd = jax.ShapeDtypeStruct((nl, r, c), F32)
    return pl.pallas_call(
        body, name=name, grid=(nl, r // tr),
        in_specs=[pl.BlockSpec((p.shape[0], tr, c), lambda l, i: (0, i, 0)) for p in parts] + [row, row, row],
        out_specs=[row, row, row, row], out_shape=[sd, sd, sd, sd], compiler_params=_cp(("parallel", "parallel")),
    )(*parts, w, m, v)


def _peers():
    mx, my, mc = lax.axis_index("x"), lax.axis_index("y"), lax.axis_index("c")
    me = 4 * mx + 2 * my + mc
    out = []
    for k in range(1, N_DEV):
        px = 1 - mx if k & 4 else mx
        py = 1 - my if k & 2 else my
        pc = 1 - mc if k & 1 else mc
        out.append(((px, py, pc), 4 * px + 2 * py + pc))
    return me, out


class _Comm:
    def __init__(self, arrs, scatters):
        self.arrs, self.scatters, self.n = list(arrs), list(scatters), len(arrs)
        self.specs = [pl.BlockSpec(memory_space=pl.ANY)] * self.n
        self.out_shape = [jax.ShapeDtypeStruct(x.shape if sc else (N_DEV,) + x.shape, x.dtype)
                          for x, sc in zip(self.arrs, self.scatters)]
        np_ = N_DEV - 1
        self.scratch = [pltpu.SemaphoreType.DMA((np_ * self.n,)), pltpu.SemaphoreType.DMA((np_ * self.n,)),
                        pltpu.SemaphoreType.DMA((self.n,))]

    def _copies(self, x_refs, o_refs, sems):
        send_sems, recv_sems, local_sems = sems
        me, peers = _peers()
        np_ = N_DEV - 1
        local, sends, recvs = [], [], []
        for a in range(self.n):
            mine = x_refs[a].at[me] if self.scatters[a] else x_refs[a]
            local.append(pltpu.make_async_copy(mine, o_refs[a].at[me], local_sems.at[a]))
        for k, (dev, idx) in enumerate(peers):
            for a in range(self.n):
                mine = x_refs[a].at[me] if self.scatters[a] else x_refs[a]
                sends.append(pltpu.make_async_remote_copy(
                    src_ref=x_refs[a].at[idx] if self.scatters[a] else x_refs[a], dst_ref=o_refs[a].at[me],
                    send_sem=send_sems.at[a * np_ + k], recv_sem=recv_sems.at[a * np_ + k], device_id=dev, device_id_type=MESH))
                recvs.append(pltpu.make_async_remote_copy(
                    src_ref=mine, dst_ref=o_refs[a].at[idx], send_sem=send_sems.at[a * np_ + k],
                    recv_sem=recv_sems.at[a * np_ + k], device_id=dev, device_id_type=MESH))
        return local, sends, recvs

    def start(self, x_refs, o_refs, sems):
        local, sends, _ = self._copies(x_refs, o_refs, sems)
        for cp in local + sends:
            cp.start()

    def wait(self, x_refs, o_refs, sems):
        local, sends, recvs = self._copies(x_refs, o_refs, sems)
        for cp in recvs:
            cp.wait_recv()
        for cp in sends:
            cp.wait_send()
        for cp in local:
            cp.wait()


class _Gather2(_Comm):
    def __init__(self, arrs):
        super().__init__(arrs, [False] * len(arrs))

    def _plan(self, x_refs, o_refs, sems):
        send_sems, recv_sems, local_sems = sems
        mx, my, mc = lax.axis_index("x"), lax.axis_index("y"), lax.axis_index("c")
        slot = lambda px, py, pc: 4 * px + 2 * py + pc
        sib = (mx, my, 1 - mc)
        chips = [(1 - mx, my), (mx, 1 - my), (1 - mx, 1 - my)]
        np_ = N_DEV - 1
        local, first, passed, arrive_first, arrive_rest = [], [], [], [], []

        def copy(a, k, src, block, to):
            return pltpu.make_async_remote_copy(
                src_ref=src, dst_ref=o_refs[a].at[block], send_sem=send_sems.at[a * np_ + k], recv_sem=recv_sems.at[a * np_ + k],
                device_id=to, device_id_type=MESH)

        for a in range(self.n):
            me = slot(mx, my, mc)
            local.append(pltpu.make_async_copy(x_refs[a], o_refs[a].at[me], local_sems.at[a]))
            first.append(copy(a, 0, x_refs[a], me, sib))
            arrive_rest.append(copy(a, 0, x_refs[a], slot(*sib), sib))
            for j, (cx, cy) in enumerate(chips):
                first.append(copy(a, 1 + j, x_refs[a], me, (cx, cy, mc)))
                arrive_first.append(copy(a, 1 + j, x_refs[a], slot(cx, cy, mc), (cx, cy, mc)))
                passed.append(copy(a, 4 + j, o_refs[a].at[slot(cx, cy, mc)], slot(cx, cy, mc), sib))
                arrive_rest.append(copy(a, 4 + j, x_refs[a], slot(cx, cy, 1 - mc), sib))
        return local, first, passed, arrive_first, arrive_rest

    def start(self, x_refs, o_refs, sems):
        local, first, _, _, _ = self._plan(x_refs, o_refs, sems)
        for cp in local + first:
            cp.start()

    def wait(self, x_refs, o_refs, sems):
        local, first, passed, arrive_first, arrive_rest = self._plan(x_refs, o_refs, sems)
        for arrived, onward in zip(arrive_first, passed):
            arrived.wait_recv()
            onward.start()
        for cp in arrive_rest:
            cp.wait_recv()
        for cp in first + passed:
            cp.wait_send()
        for cp in local:
            cp.wait()


def _call(body, *, name, grid, in_specs, out_specs, out_shape, args, scratch_shapes=(), sem=None, comm=None):
    if comm is None:
        outs = pl.pallas_call(
            body, name=name, grid=grid, in_specs=list(in_specs), out_specs=list(out_specs), out_shape=list(out_shape),
            scratch_shapes=list(scratch_shapes), compiler_params=_cp(sem),
        )(*args)
        return list(outs), []
    n_in, n_out, nc = len(in_specs), len(out_specs), comm.n
    nsteps = 1
    for g in grid:
        nsteps *= g

    def carrier(*refs):
        ins, cin = refs[:n_in], refs[n_in:n_in + nc]
        outs, cout = refs[n_in + nc:n_in + nc + n_out], refs[n_in + nc + n_out:n_in + 2 * nc + n_out]
        rest = refs[n_in + 2 * nc + n_out:]
        scratch, sems = rest[:len(rest) - 3], rest[len(rest) - 3:]
        if nsteps == 1:
            comm.start(cin, cout, sems)
            body(*ins, *outs, *scratch)
            comm.wait(cin, cout, sems)
            return
        step = 0
        for d, g in enumerate(grid):
            step = step * g + pl.program_id(d)

        @pl.when(step == 0)
        def _():
            comm.start(cin, cout, sems)

        body(*ins, *outs, *scratch)

        @pl.when(step == nsteps - 1)
        def _():
            comm.wait(cin, cout, sems)

    outs = pl.pallas_call(
        carrier, name=name, grid=grid, in_specs=list(in_specs) + comm.specs, out_specs=list(out_specs) + comm.specs,
        out_shape=list(out_shape) + comm.out_shape, scratch_shapes=list(scratch_shapes) + comm.scratch,
        compiler_params=_cp(("arbitrary",) * len(grid) if grid else None),
    )(*args, *comm.arrs)
    return list(outs[:n_out]), list(outs[n_out:])


def _exchange(comm, name):
    return _call(lambda *refs: None, name=name, grid=(), in_specs=[], out_specs=[], out_shape=[], args=[], comm=comm)[1]


def _pack(arrs, dtype, lead=()):
    nl = len(lead)
    flat = jnp.concatenate([a.astype(dtype).reshape(lead + (-1,)) for a in arrs], axis=nl)
    n = flat.shape[-1]
    rows = -(-n // (LANES * ADAM_TR)) * ADAM_TR
    flat = jnp.pad(flat, [(0, 0)] * nl + [(0, rows * LANES - n)])
    return flat.reshape(lead + (rows, LANES))


def _unpack(flat, shapes, lead=()):
    nl = len(lead)
    flat = flat.reshape(lead + (-1,))
    out, o = [], 0
    for s in shapes:
        n = 1
        for d in s:
            n *= d
        out.append(lax.slice_in_dim(flat, o, o + n, axis=nl).reshape(lead + tuple(s)))
        o += n
    return out


def _join(g, ax):
    return jnp.concatenate([g[d] for d in range(N_DEV)], axis=ax)


def _split(full, ax):
    n = full.shape[ax] // N_DEV
    return jnp.stack([lax.slice_in_dim(full, d * n, (d + 1) * n, axis=ax) for d in range(N_DEV)])


_WEIGHTS = ['norm_mix', 'norm_ffn', 'attn_w_in', 'attn_w_out', 'relpos_table', 'q_norm_a', 'k_norm_a', 'q_norm_b',
            'k_norm_b', 'sinks', 'ssm_w_in', 'ssm_conv_w', 'ssm_conv_b', 'ssm_dt_bias', 'ssm_a_log', 'ssm_d', 'ssm_norm',
            'ssm_w_out', 'ffn_w_in', 'ffn_conv_w', 'ffn_conv_b', 'ffn_w_out']
_SHARD_AX = {'attn_w_in': 2, 'attn_w_out': 1, 'ssm_w_in': 2, 'ssm_conv_w': 2, 'ssm_conv_b': 1, 'ssm_norm': 1,
             'ssm_w_out': 1, 'ffn_w_in': 2, 'ffn_conv_w': 2, 'ffn_w_out': 1}
_BIG = ['attn_w_in', 'attn_w_out', 'ssm_w_in', 'ssm_w_out', 'ffn_w_in', 'ffn_w_out']
_SMALL = ['ssm_conv_w', 'ssm_conv_b', 'ssm_norm', 'ffn_conv_w']
_AX2 = {n: _SHARD_AX[n] - 1 for n in _BIG}
_REPL = [n for n in _WEIGHTS if n not in _SHARD_AX]


def _rows8(w):
    return jnp.pad(w, ((0, 8 - w.shape[0]), (0, 0)))


def _lanes128(v):
    return jnp.pad(v, (0, LANES - v.shape[0])).reshape(1, LANES)


def _band_mask(n_prev, pad):
    cq = jnp.arange(TQ)[:, None] // CHUNK
    ck = jnp.arange(pad + TQ)[None, :] // CHUNK
    return (ck >= cq) & (ck <= cq + n_prev)


def _ffn_fwd(xin, g, w_in_t, w8, cb, tag):
    gu, h = _rms_mm(xin, g, w_in_t, 2 * D_FF, f"mm_ffn_in{tag}", BF16)
    a = _ffn_mid_fwd(gu, w8, cb, f"ffn_mid{tag}")
    return a, (h, gu, a)


def _ffn_bwd(dx, dxb, xin, g, w_in_t, w8, cb, w_out, saved, tag):
    h, gu, a = saved
    dw_out = _mm_tn(a, dxb, f"mm_ffn_dwout{tag}")
    dgate, dup, dw8, dcb = _ffn_mid_bwd(gu, dxb, w_out, w8, cb, f"ffn_mid_bwd{tag}")
    dh = _mm(dgate, w_in_t, f"mm_ffn_dh_g{tag}", b_rows=(0, D_FF))
    dw_in_t = jnp.concatenate([_mm_tn(dgate, h, f"mm_ffn_dwin_g{tag}"), _mm_tn(dup, h, f"mm_ffn_dwin_u{tag}")], axis=0)
    dxp, dxpb, dg = _mm_rms_bwd(dup, w_in_t, D_FF, dh, xin, g, dx, f"mm_ffn_dh_u{tag}")
    return dxp, dxpb, dg, dw_in_t, dw8[:3], dcb, dw_out


def kernel(x, norm_mix, norm_ffn, attn_w_in, attn_w_out, relpos_table, q_norm_a, k_norm_a, q_norm_b, k_norm_b, sinks, ssm_w_in, ssm_conv_w, ssm_conv_b, ssm_dt_bias, ssm_a_log, ssm_d, ssm_norm, ssm_w_out, ffn_w_in, ffn_conv_w, ffn_conv_b, ffn_w_out, loss_target, m_norm_mix, m_norm_ffn, m_attn_w_in, m_attn_w_out, m_relpos_table, m_q_norm_a, m_k_norm_a, m_q_norm_b, m_k_norm_b, m_sinks, m_ssm_w_in, m_ssm_conv_w, m_ssm_conv_b, m_ssm_dt_bias, m_ssm_a_log, m_ssm_d, m_ssm_norm, m_ssm_w_out, m_ffn_w_in, m_ffn_conv_w, m_ffn_conv_b, m_ffn_w_out, v_norm_mix, v_norm_ffn, v_attn_w_in, v_attn_w_out, v_relpos_table, v_q_norm_a, v_k_norm_a, v_q_norm_b, v_k_norm_b, v_sinks, v_ssm_w_in, v_ssm_conv_w, v_ssm_conv_b, v_ssm_dt_bias, v_ssm_a_log, v_ssm_d, v_ssm_norm, v_ssm_w_out, v_ffn_w_in, v_ffn_conv_w, v_ffn_conv_b, v_ffn_w_out):
    w = dict(norm_mix=norm_mix, norm_ffn=norm_ffn, attn_w_in=attn_w_in, attn_w_out=attn_w_out, relpos_table=relpos_table,
             q_norm_a=q_norm_a, k_norm_a=k_norm_a, q_norm_b=q_norm_b, k_norm_b=k_norm_b, sinks=sinks, ssm_w_in=ssm_w_in,
             ssm_conv_w=ssm_conv_w, ssm_conv_b=ssm_conv_b, ssm_dt_bias=ssm_dt_bias, ssm_a_log=ssm_a_log, ssm_d=ssm_d,
             ssm_norm=ssm_norm, ssm_w_out=ssm_w_out, ffn_w_in=ffn_w_in, ffn_conv_w=ffn_conv_w, ffn_conv_b=ffn_conv_b,
             ffn_w_out=ffn_w_out)
    mom = dict(norm_mix=m_norm_mix, norm_ffn=m_norm_ffn, attn_w_in=m_attn_w_in, attn_w_out=m_attn_w_out,
               relpos_table=m_relpos_table, q_norm_a=m_q_norm_a, k_norm_a=m_k_norm_a, q_norm_b=m_q_norm_b,
               k_norm_b=m_k_norm_b, sinks=m_sinks, ssm_w_in=m_ssm_w_in, ssm_conv_w=m_ssm_conv_w, ssm_conv_b=m_ssm_conv_b,
               ssm_dt_bias=m_ssm_dt_bias, ssm_a_log=m_ssm_a_log, ssm_d=m_ssm_d, ssm_norm=m_ssm_norm, ssm_w_out=m_ssm_w_out,
               ffn_w_in=m_ffn_w_in, ffn_conv_w=m_ffn_conv_w, ffn_conv_b=m_ffn_conv_b, ffn_w_out=m_ffn_w_out)
    var = dict(norm_mix=v_norm_mix, norm_ffn=v_norm_ffn, attn_w_in=v_attn_w_in, attn_w_out=v_attn_w_out,
               relpos_table=v_relpos_table, q_norm_a=v_q_norm_a, k_norm_a=v_k_norm_a, q_norm_b=v_q_norm_b,
               k_norm_b=v_k_norm_b, sinks=v_sinks, ssm_w_in=v_ssm_w_in, ssm_conv_w=v_ssm_conv_w, ssm_conv_b=v_ssm_conv_b,
               ssm_dt_bias=v_ssm_dt_bias, ssm_a_log=v_ssm_a_log, ssm_d=v_ssm_d, ssm_norm=v_ssm_norm, ssm_w_out=v_ssm_w_out,
               ffn_w_in=v_ffn_w_in, ffn_conv_w=v_ffn_conv_w, ffn_conv_b=v_ffn_conv_b, ffn_w_out=v_ffn_w_out)

    def piece(n, l):
        return (w[n][l].T if _AX2[n] == 1 else w[n][l]).astype(BF16)

    def gather_of(names_layers):
        return _Gather2([piece(n, l) for n, l in names_layers])

    def joined(got):
        return [g.reshape(-1, D_MODEL) for g in got]

    first = [('attn_w_in', 0), ('attn_w_out', 0)]
    got = _exchange(_Gather2([piece(n, l) for n, l in first] + [_pack([w[n] for n in _SMALL], F32)]), "gather_attn")
    wt_attn_in, w_attn_out = joined(got[:2])
    full = {}
    for n, g in zip(_SMALL, _unpack(got[2], [w[n].shape for n in _SMALL], lead=(N_DEV,))):
        full[n] = _join(g, _SHARD_AX[n])
    ssm_cw8 = _rows8(full['ssm_conv_w'][0])
    ssm_cb = full['ssm_conv_b']
    ssm_nw = full['ssm_norm']
    ffn_cw8 = [_rows8(full['ffn_conv_w'][l]) for l in range(2)]
    ffn_cb = [ffn_conv_b[l:l + 1] for l in range(2)]

    x0 = x[0]
    target = loss_target[0]
    t = x0.shape[0]

    g_mix0, g_mix1 = norm_mix[0:1], norm_mix[1:2]
    g_ffn0, g_ffn1 = norm_ffn[0:1], norm_ffn[1:2]
    proj, h0 = _rms_mm(x0, g_mix0, wt_attn_in, 2304, "mm_attn_in", F32)
    hn_w = jnp.concatenate([jnp.tile(v, (1, 2)) for v in (q_norm_a, k_norm_a, q_norm_b, k_norm_b)], axis=0)
    qa, kpa, vpa, qb, kpb, vpb = _headnorm_fwd(proj, hn_w, "headnorm")
    table = jnp.pad(relpos_table[0], ((0, 0), (0, REL_W - (2 * MAX_REL + 1))))
    bias_a = jnp.where(_band_mask(A_PREV, PAD_A)[None], jnp.transpose(_relpos_fwd(table, "relpos_bias"), (1, 0, 2)), NEG)
    rel_b = jnp.arange(TQ)[:, None] - (jnp.arange(PAD_B + TQ)[None, :] - PAD_B)
    slopes = 2.0 ** (-8.0 * jnp.arange(1, N_HEADS + 1, dtype=F32) / N_HEADS)
    bias_b = jnp.where(_band_mask(B_PREV, PAD_B)[None], -slopes[:, None, None] * jnp.abs(rel_b).astype(F32)[None], NEG)
    no_sinks = jnp.full((N_HEADS,), NEG, F32)
    ffn0_w, ssm_w, ffn1_w = [('ffn_w_in', 0), ('ffn_w_out', 0)], [('ssm_w_in', 0), ('ssm_w_out', 0)], [('ffn_w_in', 1), ('ffn_w_out', 1)]
    oa, got = _attn_fwd(qa, kpa, vpa, bias_a, no_sinks, PAD_A, "attn_a", comm=gather_of(ffn0_w + ssm_w))
    wt_ffn_in0, w_ffn_out0, wt_ssm_in, w_ssm_out = joined(got)
    ob, got = _attn_fwd(qb, kpb, vpb, bias_b, sinks[0], PAD_B, "attn_b", comm=gather_of(ffn1_w))
    wt_ffn_in1, w_ffn_out1 = joined(got)
    wt_ssm_dt = jnp.pad(wt_ssm_in[ZX:], ((0, LANES - SSM_HEADS), (0, 0)))
    x1 = _mm(oa, w_attn_out, "mm_attn_out_a", res=x0, b_rows=(0, 512))
    x1 = _mm(ob, w_attn_out, "mm_attn_out_b", res=x1, b_rows=(512, 512))
    a0, ffn0_saved = _ffn_fwd(x1, g_ffn0, wt_ffn_in0, ffn_cw8[0], ffn_cb[0], "0")
    x2 = _mm(a0, w_ffn_out0, "mm_ffn_out0", res=x1)

    zx, h2 = _rms_mm(x2, g_mix1, wt_ssm_in, ZX, "mm_ssm_in", F32)
    dtraw = _mm(h2, wt_ssm_dt, "mm_ssm_dt", trans_b=True)
    dt_bias = _lanes128(ssm_dt_bias[0])
    alog = _lanes128(ssm_a_log[0])
    dexp = jnp.repeat(ssm_d[0], HEAD_DIM).reshape(1, D_INNER)
    xbc = _ssm_pre_fwd(zx, ssm_cw8, ssm_cb, "ssm_pre")
    dt = _dt_fwd(dtraw, dt_bias, "ssm_dt")
    (y, sprev), _ = _ssd_fwd(xbc, dt, alog, "ssd_fwd")
    y4 = _ssm_post_fwd(y, xbc, zx, dexp, ssm_nw, "ssm_post")
    x3 = _mm(y4, w_ssm_out, "mm_ssm_out", res=x2)
    a1, ffn1_saved = _ffn_fwd(x3, g_ffn1, wt_ffn_in1, ffn_cw8[1], ffn_cb[1], "1")

    dx4, dx4b, sq = _mm_loss(a1, w_ffn_out1, x3, target, "mm_ffn_out1_loss")
    loss = lax.psum(0.5 * jnp.sum(sq) / D_MODEL, ("x", "y", "c"))

    grads = {}

    def scatter_of(grads_2d):
        return _Comm([g.reshape(N_DEV, -1, D_MODEL) for g in grads_2d], [True] * len(grads_2d))

    dx3, dx3b, dg_ffn1, dwtin1, dcw1, dcb1, dwout1 = _ffn_bwd(
        dx4, dx4b, x3, g_ffn1, wt_ffn_in1, ffn_cw8[1], ffn_cb[1], w_ffn_out1, ffn1_saved, "1")

    dy4 = _mm(dx3b, w_ssm_out, "mm_ssm_dy", trans_b=True)
    dw_ssm_out = _mm_tn(y4, dx3b, "mm_ssm_dwout")
    dyv, dskip, dz, dd_lane, dnw = _ssm_post_bwd(dy4, y, xbc, zx, dexp, ssm_nw, "ssm_post_bwd")
    (dxbc, ddt, dalog), parts_ffn1 = _ssd_bwd(xbc, dt, alog, sprev, dyv, dskip, "ssd_bwd", comm=scatter_of([dwtin1, dwout1]))
    dxr, dcw_s, dcb_s = _ssm_pre_bwd(zx, dxbc, ssm_cw8, ssm_cb, "ssm_pre_bwd")
    ddtraw, ddtb = _dt_bwd(dtraw, dt_bias, ddt, "ssm_dt_bwd")
    dh2 = _mm(dz, wt_ssm_in, "mm_ssm_dh_z", b_rows=(0, D_INNER))
    dh2 = _mm(dxr, wt_ssm_in[D_INNER:ZX], "mm_ssm_dh_x", res=dh2)
    dwt_ssm_in = jnp.concatenate([
        _mm_tn(dz, h2, "mm_ssm_dwin_z"), _mm_tn(dxr, h2, "mm_ssm_dwin_x"),
        _mm_tn(ddtraw, h2, "mm_ssm_dwin_dt")[:SSM_HEADS]], axis=0)
    dx2, dx2b, dg_mix1 = _mm_rms_bwd(ddtraw, wt_ssm_dt, 0, dh2, x2, g_mix1, dx3, "mm_ssm_dh_dt")
    grads['ssm_conv_w'] = dcw_s[:4][None]
    grads['ssm_conv_b'] = dcb_s
    grads['ssm_norm'] = dnw
    grads['ssm_dt_bias'] = ddtb[:, :SSM_HEADS]
    grads['ssm_a_log'] = dalog[:, :SSM_HEADS]
    grads['ssm_d'] = jnp.sum(dd_lane.reshape(SSM_HEADS, HEAD_DIM), axis=1)[None]

    dx1, dx1b, dg_ffn0, dwtin0, dcw0, dcb0, dwout0 = _ffn_bwd(
        dx2, dx2b, x1, g_ffn0, wt_ffn_in0, ffn_cw8[0], ffn_cb[0], w_ffn_out0, ffn0_saved, "0")
    grads['ffn_conv_w'] = jnp.stack([dcw0, dcw1])
    grads['ffn_conv_b'] = jnp.concatenate([dcb0, dcb1], axis=0)
    grads['norm_ffn'] = jnp.concatenate([dg_ffn0, dg_ffn1], axis=0)

    do = _mm(dx1b, w_attn_out, "mm_attn_do", out_dtype=BF16, trans_b=True)
    dw_attn_out = jnp.concatenate([_mm_tn(oa, dx1b, "mm_attn_dwout_a"), _mm_tn(ob, dx1b, "mm_attn_dwout_b")], axis=0)
    (dqa, dkpa, dvpa, dbias_a, _), parts_ssm = _attn_bwd(
        qa, kpa, vpa, bias_a, no_sinks, do, 0, PAD_A, "attn_a_bwd", comm=scatter_of([dwt_ssm_in, dw_ssm_out, dw_attn_out]))
    (dqb, dkpb, dvpb, _, dsink), parts_ffn0 = _attn_bwd(
        qb, kpb, vpb, bias_b, sinks[0], do, 4, PAD_B, "attn_b_bwd", comm=scatter_of([dwtin0, dwout0]))
    grads['relpos_table'] = _relpos_bwd(jnp.transpose(dbias_a, (1, 0, 2)), "relpos_bwd")[None, :, :2 * MAX_REL + 1]
    grads['sinks'] = dsink[:, :2, 0].reshape(1, N_HEADS)
    dproj, dhn = _headnorm_bwd(proj, hn_w, dqa, dkpa, dvpa, dqb, dkpb, dvpb, "headnorm_bwd")
    dhn = dhn[:, :HEAD_DIM] + dhn[:, HEAD_DIM:]
    for k, n in enumerate(('q_norm_a', 'k_norm_a', 'q_norm_b', 'k_norm_b')):
        grads[n] = dhn[k:k + 1]
    dwt_attn_in = _mm_tn(dproj, h0, "mm_attn_dwin")
    dx0, _, dg_mix0, parts_attn_in = _mm_rms_bwd(dproj, wt_attn_in, 0, None, x0, g_mix0, dx1, "mm_attn_dh",
                                                 comm=scatter_of([dwt_attn_in]))
    grads['norm_mix'] = jnp.concatenate([dg_mix0, dg_mix1], axis=0)

    def summed_t(parts, name):
        return _sum_parts(parts, name).T[None]

    sm_shapes = [w[n].shape for n in _SMALL]
    rp_shapes = [w[n].shape for n in _REPL]
    recv = _exchange(_Comm(
        [_pack([_split(grads[n], _SHARD_AX[n]) for n in _SMALL], F32, lead=(N_DEV,)), _pack([grads[n] for n in _REPL], F32)],
        [True, False]), "exchange_small")
    big_parts = {
        'attn_w_in': [summed_t(parts_attn_in[0], "sum_attn_w_in")], 'attn_w_out': [parts_ssm[2]],
        'ssm_w_in': [summed_t(parts_ssm[0], "sum_ssm_w_in")], 'ssm_w_out': [parts_ssm[1]],
        'ffn_w_in': [summed_t(parts_ffn0[0], "sum_ffn_w_in0"), summed_t(parts_ffn1[0], "sum_ffn_w_in1")],
        'ffn_w_out': [parts_ffn0[1], parts_ffn1[1]],
    }
    res = [{}, {}, {}, {}]
    for n in _BIG:
        for kind, a in enumerate(_adamw(big_parts[n], w[n], mom[n], var[n], f"adamw_{n}")):
            res[kind][n] = a
    for names, shapes, parts in ((_SMALL, sm_shapes, recv[0]), (_REPL, rp_shapes, recv[1])):
        outs = _adamw([parts], _pack([w[n] for n in names], F32)[None], _pack([mom[n] for n in names], F32)[None],
                      _pack([var[n] for n in names], F32)[None], "adamw_" + ("small" if names is _SMALL else "replicated"))
        for kind, flat in enumerate(outs):
            for n, a in zip(names, _unpack(flat[0], shapes)):
                res[kind][n] = a
    return (loss, dx0[None], *[res[0][n] for n in _WEIGHTS], *[res[1][n] for n in _WEIGHTS],
            *[res[2][n] for n in _WEIGHTS], *[res[3][n] for n in _WEIGHTS])
```

```python
import jax
import jax.numpy as jnp
from jax import lax
from jax.experimental import pallas as pl
from jax.experimental.pallas import tpu as pltpu

F32 = jnp.float32
BF16 = jnp.bfloat16
HI = lax.Precision.HIGHEST
MESH = pl.DeviceIdType.MESH
NEG = -1e30

N_DEV = 8
D_MODEL = 1024
EPS = 1e-6
CHUNK = 64
HEAD_DIM = 64
N_HEADS = 8
A_PREV = 8
B_PREV = 2
MAX_REL = 256
TQ = 2 * CHUNK
ATT_SUB = 4
PAD_A = A_PREV * CHUNK
PAD_B = B_PREV * CHUNK
REL_W = PAD_A + TQ
D_INNER = 2048
SSM_HEADS = 32
SSM_GROUPS = 4
SSM_STATE = 128
XBC = D_INNER + 2 * SSM_GROUPS * SSM_STATE
ZX = D_INNER + XBC
D_FF = 2816
SSD_L = 128
LANES = 128
VMEM_LIMIT = 56 << 20

ADAM_LR, ADAM_B1, ADAM_B2, ADAM_EPS, ADAM_WD, ADAM_STEP = 0.001, 0.9, 0.999, 1e-08, 0.01, 10


def _cp(sem=None):
    return pltpu.CompilerParams(dimension_semantics=sem, vmem_limit_bytes=VMEM_LIMIT)


def _dot(a, b, ca=1, cb=0, prec=None):
    return lax.dot_general(a, b, (((ca,), (cb,)), ((), ())), preferred_element_type=F32, precision=prec)


def _pick(n, cands):
    for c in cands:
        if n % c == 0:
            return c
    return n


def _lo_mask():
    return lax.broadcasted_iota(jnp.int32, (1, LANES), 1) < HEAD_DIM


_TN_CHUNKS = (1408, 1536, 1152, 1024, 512, 256, 128)


TN_MAX_ROWS = 3072


def _mm_tn(a, b, name):
    kdim, m = a.shape
    n = b.shape[1]
    assert b.shape[0] == kdim, (a.shape, b.shape)
    mb = m if m <= TN_MAX_ROWS else m // 2
    tn = _pick(n, _TN_CHUNKS)
    tk = _pick(kdim, (512, 256, 128))
    nk = kdim // tk

    def body(a_ref, b_ref, o_ref, acc):
        k = pl.program_id(1)

        @pl.when(k == 0)
        def _():
            acc[...] = jnp.zeros_like(acc)

        av = a_ref[...]
        for c in range(0, n, tn):
            acc[:, c:c + tn] += _dot(av, b_ref[:, c:c + tn], 0, 0)

        @pl.when(k == nk - 1)
        def _():
            o_ref[...] = acc[...].astype(BF16)

    return pl.pallas_call(
        body, name=name, grid=(m // mb, nk),
        in_specs=[pl.BlockSpec((tk, mb), lambda j, k: (k, j)), pl.BlockSpec((tk, n), lambda j, k: (k, 0))],
        out_specs=pl.BlockSpec((mb, n), lambda j, k: (j, 0)), out_shape=jax.ShapeDtypeStruct((m, n), BF16),
        scratch_shapes=[pltpu.VMEM((mb, n), F32)], compiler_params=_cp(("parallel", "arbitrary")),
    )(a, b)


def _mm(a, b, name, out_dtype=F32, res=None, trans_b=False, b_rows=None):
    m, kdim = a.shape
    if b_rows is None:
        b_rows = (0, b.shape[0])
    off, rows = b_rows
    n = rows if trans_b else b.shape[1]
    assert (b.shape[1] if trans_b else rows) == kdim and off % rows == 0, (a.shape, b.shape, b_rows)
    tn = _pick(n, _TN_CHUNKS)
    tm = _pick(m, (256, 128) if n > 2304 else (512, 256, 128))

    def body(*refs):
        if res is None:
            a_ref, b_ref, o_ref = refs
        else:
            a_ref, b_ref, r_ref, o_ref = refs
        av = a_ref[...]
        for c in range(0, n, tn):
            r = _dot(av, b_ref[c:c + tn, :], 1, 1) if trans_b else _dot(av, b_ref[:, c:c + tn], 1, 0)
            if res is not None:
                r = r + r_ref[:, c:c + tn]
            o_ref[:, c:c + tn] = r.astype(out_dtype)

    in_specs = [pl.BlockSpec((tm, kdim), lambda i: (i, 0)), pl.BlockSpec((rows, b.shape[1]), lambda i: (off // rows, 0))]
    args = [a, b]
    if res is not None:
        in_specs.append(pl.BlockSpec((tm, n), lambda i: (i, 0)))
        args.append(res)
    return pl.pallas_call(
        body, name=name, grid=(m // tm,), in_specs=in_specs, out_specs=pl.BlockSpec((tm, n), lambda i: (i, 0)),
        out_shape=jax.ShapeDtypeStruct((m, n), out_dtype), compiler_params=_cp(("parallel",)),
    )(*args)


def _rms_mm(x, g, bt, n, name, out_dtype):
    t, d = x.shape
    tn = _pick(n, _TN_CHUNKS)
    tm = _pick(t, (256, 128))

    def body(x_ref, g_ref, b_ref, o_ref, h_ref):
        xv = x_ref[...]
        r = lax.rsqrt(jnp.mean(xv * xv, axis=-1, keepdims=True) + EPS)
        h = (xv * r * g_ref[...]).astype(BF16)
        h_ref[...] = h
        for c in range(0, n, tn):
            o_ref[:, c:c + tn] = _dot(h, b_ref[c:c + tn, :], 1, 1).astype(out_dtype)

    row = pl.BlockSpec((tm, d), lambda i: (i, 0))
    return pl.pallas_call(
        body, name=name, grid=(t // tm,),
        in_specs=[row, pl.BlockSpec((1, d), lambda i: (0, 0)), pl.BlockSpec(bt.shape, lambda i: (0, 0))],
        out_specs=[pl.BlockSpec((tm, n), lambda i: (i, 0)), row],
        out_shape=[jax.ShapeDtypeStruct((t, n), out_dtype), jax.ShapeDtypeStruct((t, d), BF16)],
        compiler_params=_cp(("parallel",)),
    )(x, g, bt)


def _mm_rms_bwd(a, b, b_off, dh_prev, x, g, dres, name, comm=None):
    t, d = x.shape
    kdim = a.shape[1]
    assert b_off % kdim == 0 and b.shape[1] == d, (a.shape, b.shape, b_off)
    tm = _pick(t, (256, 128))

    def body(*refs):
        if dh_prev is None:
            a_ref, b_ref, x_ref, g_ref, dr_ref, dx_ref, dxb_ref, dg_ref = refs
            dhv = _dot(a_ref[...], b_ref[...], 1, 0)
        else:
            a_ref, b_ref, p_ref, x_ref, g_ref, dr_ref, dx_ref, dxb_ref, dg_ref = refs
            dhv = _dot(a_ref[...], b_ref[...], 1, 0) + p_ref[...]
        xv = x_ref[...]
        r = lax.rsqrt(jnp.mean(xv * xv, axis=-1, keepdims=True) + EPS)
        xh = xv * r
        dxh = dhv * g_ref[...]
        dx = dr_ref[...] + r * (dxh - xh * jnp.mean(dxh * xh, axis=-1, keepdims=True))
        dx_ref[...] = dx
        dxb_ref[...] = dx.astype(BF16)

        @pl.when(pl.program_id(0) == 0)
        def _():
            dg_ref[...] = jnp.zeros_like(dg_ref)

        dg_ref[...] += jnp.sum(dhv * xh, axis=0, keepdims=True)

    row = pl.BlockSpec((tm, d), lambda i: (i, 0))
    vec = pl.BlockSpec((1, d), lambda i: (0, 0))
    in_specs = [pl.BlockSpec((tm, kdim), lambda i: (i, 0)), pl.BlockSpec((kdim, d), lambda i: (b_off // kdim, 0))]
    args = [a, b]
    if dh_prev is not None:
        in_specs.append(row)
        args.append(dh_prev)
    outs, got = _call(
        body, name=name, grid=(t // tm,), in_specs=in_specs + [row, vec, row], out_specs=[row, row, vec],
        out_shape=[jax.ShapeDtypeStruct((t, d), F32), jax.ShapeDtypeStruct((t, d), BF16), jax.ShapeDtypeStruct((1, d), F32)],
        args=(*args, x, g, dres), sem=("arbitrary",), comm=comm)
    return (*outs, got) if comm is not None else tuple(outs)


def _mm_loss(a, b, res, target, name):
    t, kdim = a.shape
    d = b.shape[1]
    tm = _pick(t, (512, 256, 128))

    def body(a_ref, b_ref, r_ref, t_ref, dy_ref, dyb_ref, acc_ref):
        @pl.when(pl.program_id(0) == 0)
        def _():
            acc_ref[...] = jnp.zeros_like(acc_ref)

        err = _dot(a_ref[...], b_ref[...], 1, 0) + r_ref[...] - t_ref[...]
        dy = err * (1.0 / d)
        dy_ref[...] = dy
        dyb_ref[...] = dy.astype(BF16)
        acc_ref[...] += jnp.sum(err * err, axis=0, keepdims=True)

    row = pl.BlockSpec((tm, d), lambda i: (i, 0))
    vec = pl.BlockSpec((1, d), lambda i: (0, 0))
    return pl.pallas_call(
        body, name=name, grid=(t // tm,),
        in_specs=[pl.BlockSpec((tm, kdim), lambda i: (i, 0)), pl.BlockSpec((kdim, d), lambda i: (0, 0)), row, row],
        out_specs=[row, row, vec],
        out_shape=[jax.ShapeDtypeStruct((t, d), F32), jax.ShapeDtypeStruct((t, d), BF16), jax.ShapeDtypeStruct((1, d), F32)],
        compiler_params=_cp(("arbitrary",)),
    )(a, b, res, target)


def _head_rms(xs, w, lo):
    sq = xs * xs
    s0 = jnp.sum(jnp.where(lo, sq, 0.0), axis=-1, keepdims=True)
    s1 = jnp.sum(jnp.where(lo, 0.0, sq), axis=-1, keepdims=True)
    r = jnp.where(lo, lax.rsqrt(s0 * (1.0 / HEAD_DIM) + EPS), lax.rsqrt(s1 * (1.0 / HEAD_DIM) + EPS))
    return xs * r, r


def _head_rms_bwd(xs, w, dy, lo):
    xh, r = _head_rms(xs, w, lo)
    dxh = dy * w
    t = dxh * xh
    m0 = jnp.sum(jnp.where(lo, t, 0.0), axis=-1, keepdims=True)
    m1 = jnp.sum(jnp.where(lo, 0.0, t), axis=-1, keepdims=True)
    mm = jnp.where(lo, m0, m1) * (1.0 / HEAD_DIM)
    return r * (dxh - xh * mm), dy * xh


_QSCALE = HEAD_DIM ** -0.5


def _headnorm_fwd(proj, ws, name):
    t = proj.shape[0]
    tm = TQ
    lead = PAD_A // tm
    leadb = PAD_B // tm

    def body(p_ref, w_ref, qa_ref, ka_ref, va_ref, qb_ref, kb_ref, vb_ref):
        data = pl.program_id(0) >= lead
        lo = _lo_mask()

        def put(ref, c, val):
            ref[:, c:c + val.shape[1]] = jnp.where(data, val, 0.0).astype(BF16)

        def per_query_head(slab):
            other = pltpu.roll(slab, HEAD_DIM, 1)
            e0, e1 = jnp.where(lo, slab, other), jnp.where(lo, other, slab)
            return jnp.concatenate([e0, e0, e1, e1], axis=1)

        for s in range(4):
            c = LANES * s
            xh, _ = _head_rms(p_ref[:, c:c + LANES], None, lo)
            qa_ref[:, c:c + LANES] = (xh * w_ref[0:1, :] * _QSCALE).astype(BF16)
            xh, _ = _head_rms(p_ref[:, 512 + c:512 + c + LANES], None, lo)
            put(ka_ref, c, xh * w_ref[1:2, :])
            xh, _ = _head_rms(p_ref[:, 1536 + c:1536 + c + LANES], None, lo)
            qb_ref[:, c:c + LANES] = (xh * w_ref[2:3, :] * _QSCALE).astype(BF16)
        put(va_ref, 0, p_ref[:, 1024:1536])
        xh, _ = _head_rms(p_ref[:, 2048:2176], None, lo)
        put(kb_ref, 0, per_query_head(xh * w_ref[3:4, :]))
        put(vb_ref, 0, per_query_head(p_ref[:, 2176:2304]))

    src = lambda i: jnp.maximum(i - lead, 0)
    wide = pl.BlockSpec((tm, 512), lambda i: (src(i), 0))
    pad_a = pl.BlockSpec((tm, 512), lambda i: (i, 0))
    pad_b = pl.BlockSpec((tm, 512), lambda i: (jnp.maximum(i - lead + leadb, 0), 0))
    sd = lambda rows: jax.ShapeDtypeStruct((rows, 512), BF16)
    return pl.pallas_call(
        body, name=name, grid=(t // tm + lead,),
        in_specs=[pl.BlockSpec((tm, 2304), lambda i: (src(i), 0)), pl.BlockSpec((4, LANES), lambda i: (0, 0))],
        out_specs=[wide, pad_a, pad_a, wide, pad_b, pad_b],
        out_shape=[sd(t), sd(t + PAD_A), sd(t + PAD_A), sd(t), sd(t + PAD_B), sd(t + PAD_B)],
        compiler_params=_cp(("arbitrary",)),
    )(proj, ws)


def _headnorm_bwd(proj, ws, dqa, dkpa, dvpa, dqb, dkpb, dvpb, name):
    t = proj.shape[0]
    tm = TQ
    offa, offb = PAD_A // tm, PAD_B // tm

    def body(p_ref, w_ref, dqa_ref, dka_ref, dva_ref, dqb_ref, dkb_ref, dvb_ref, dp_ref, dw_ref):
        i = pl.program_id(0)
        lo = _lo_mask()

        @pl.when(i == 0)
        def _():
            dw_ref[...] = jnp.zeros_like(dw_ref)

        acc = [jnp.zeros((1, LANES), F32) for _ in range(4)]
        for s in range(4):
            c = LANES * s
            dx, dwl = _head_rms_bwd(p_ref[:, c:c + LANES], w_ref[0:1, :], dqa_ref[:, c:c + LANES] * _QSCALE, lo)
            dp_ref[:, c:c + LANES] = dx.astype(BF16)
            acc[0] += jnp.sum(dwl, axis=0, keepdims=True)
            dx, dwl = _head_rms_bwd(p_ref[:, 512 + c:512 + c + LANES], w_ref[1:2, :], dka_ref[:, c:c + LANES], lo)
            dp_ref[:, 512 + c:512 + c + LANES] = dx.astype(BF16)
            acc[1] += jnp.sum(dwl, axis=0, keepdims=True)
            dx, dwl = _head_rms_bwd(p_ref[:, 1536 + c:1536 + c + LANES], w_ref[2:3, :], dqb_ref[:, c:c + LANES] * _QSCALE, lo)
            dp_ref[:, 1536 + c:1536 + c + LANES] = dx.astype(BF16)
            acc[2] += jnp.sum(dwl, axis=0, keepdims=True)
        dp_ref[:, 1024:1536] = dva_ref[...].astype(BF16)

        def group_sum(ref):
            s0 = ref[:, 0:128] + ref[:, 128:256]
            s1 = ref[:, 256:384] + ref[:, 384:512]
            s0 = s0 + pltpu.roll(s0, HEAD_DIM, 1)
            s1 = s1 + pltpu.roll(s1, HEAD_DIM, 1)
            return jnp.where(lo, s0, s1)

        dx, dwl = _head_rms_bwd(p_ref[:, 2048:2176], w_ref[3:4, :], group_sum(dkb_ref), lo)
        dp_ref[:, 2048:2176] = dx.astype(BF16)
        acc[3] += jnp.sum(dwl, axis=0, keepdims=True)
        dp_ref[:, 2176:2304] = group_sum(dvb_ref).astype(BF16)
        for n in range(4):
            dw_ref[n:n + 1, :] += acc[n]

    wide = pl.BlockSpec((tm, 512), lambda i: (i, 0))
    pa = pl.BlockSpec((tm, 512), lambda i: (i + offa, 0))
    pb = pl.BlockSpec((tm, 512), lambda i: (i + offb, 0))
    return pl.pallas_call(
        body, name=name, grid=(t // tm,),
        in_specs=[pl.BlockSpec((tm, 2304), lambda i: (i, 0)), pl.BlockSpec((4, LANES), lambda i: (0, 0)),
                  wide, pa, pa, wide, pb, pb],
        out_specs=[pl.BlockSpec((tm, 2304), lambda i: (i, 0)), pl.BlockSpec((4, LANES), lambda i: (0, 0))],
        out_shape=[jax.ShapeDtypeStruct((t, 2304), BF16), jax.ShapeDtypeStruct((4, LANES), F32)],
        compiler_params=_cp(("arbitrary",)),
    )(proj, ws, dqa, dkpa, dvpa, dqb, dkpb, dvpb)


ROLL_W = 1024


def _rel_onehot():
    r_io = lax.broadcasted_iota(jnp.int32, (REL_W, ROLL_W), 0)
    m_io = lax.broadcasted_iota(jnp.int32, (REL_W, ROLL_W), 1)
    return (r_io == jnp.clip(REL_W - 1 - m_io, -MAX_REL, MAX_REL) + MAX_REL).astype(F32)


def _relpos_fwd(table, name):
    def body(t_ref, o_ref):
        rr = _dot(t_ref[...], _rel_onehot(), 1, 0, HI)

        def step(q, c):
            o_ref[q] = pltpu.roll(rr, (ROLL_W - (TQ - 1) + q) % ROLL_W, 1)[:, :REL_W]
            return c

        lax.fori_loop(0, TQ, step, 0)

    return pl.pallas_call(
        body, name=name, out_shape=jax.ShapeDtypeStruct((TQ, N_HEADS, REL_W), F32),
        in_specs=[pl.BlockSpec(memory_space=pltpu.VMEM)], out_specs=pl.BlockSpec(memory_space=pltpu.VMEM),
        compiler_params=_cp(),
    )(table)


def _relpos_bwd(dbias_t, name):
    def body(d_ref, o_ref):
        def step(q, acc):
            row = jnp.concatenate([d_ref[q], jnp.zeros((N_HEADS, ROLL_W - REL_W), F32)], axis=1)
            return acc + pltpu.roll(row, TQ - 1 - q, 1)

        drr = lax.fori_loop(0, TQ, step, jnp.zeros((N_HEADS, ROLL_W), F32))
        o_ref[...] = _dot(drr, _rel_onehot(), 1, 1, HI)

    return pl.pallas_call(
        body, name=name, out_shape=jax.ShapeDtypeStruct((N_HEADS, REL_W), F32),
        in_specs=[pl.BlockSpec(memory_space=pltpu.VMEM)], out_specs=pl.BlockSpec(memory_space=pltpu.VMEM),
        compiler_params=_cp(),
    )(dbias_t)


def _attn_probs(qe, kw, bias, kvalid, snk):
    s = _dot(qe, kw, 1, 1) + bias
    s = jnp.where(kvalid, s, NEG)
    m = jnp.maximum(jnp.max(s, axis=-1, keepdims=True), snk)
    p = jnp.exp(s - m)
    inv = 1.0 / (jnp.sum(p, axis=-1, keepdims=True) + jnp.exp(snk - m))
    return p * inv, jnp.exp(snk - m) * inv


def _attn_fwd(q, kp, vp, bias, sinks, pad, name, comm=None):
    t, hd = q.shape
    w = pad + TQ

    def body(sink_ref, q_ref, k_ref, v_ref, b_ref, o_ref):
        hp, i = pl.program_id(0), pl.program_id(1)
        lo = _lo_mask()
        for j in range(ATT_SUB):
            start = pl.multiple_of((i * ATT_SUB + j) * TQ, TQ)
            qv = q_ref[TQ * j:TQ * (j + 1), :]
            kw = k_ref[pl.ds(start, w), :]
            vw = v_ref[pl.ds(start, w), :]
            kvalid = (start + lax.broadcasted_iota(jnp.int32, (1, w), 1)) >= pad
            outs = []
            for e in range(2):
                sel = lo if e == 0 else jnp.logical_not(lo)
                qe = jnp.where(sel, qv, jnp.zeros_like(qv))
                p, _ = _attn_probs(qe, kw, b_ref[e], kvalid, sink_ref[2 * hp + e])
                outs.append(_dot(p.astype(BF16), vw, 1, 0))
            o_ref[TQ * j:TQ * (j + 1), :] = jnp.where(lo, outs[0], outs[1]).astype(BF16)

    full = pl.BlockSpec((t + pad, LANES), lambda h, i: (0, h))
    tile = pl.BlockSpec((ATT_SUB * TQ, LANES), lambda h, i: (i, h))
    (o,), got = _call(
        body, name=name, grid=(hd // LANES, t // (ATT_SUB * TQ)),
        in_specs=[pl.BlockSpec(memory_space=pltpu.SMEM), tile, full, full, pl.BlockSpec((2, TQ, w), lambda h, i: (h, 0, 0))],
        out_specs=[tile], out_shape=[jax.ShapeDtypeStruct((t, hd), BF16)],
        args=(sinks, q, kp, vp, bias), sem=("parallel", "arbitrary"), comm=comm)
    return o, got


def _attn_bwd(q, kp, vp, bias, sinks, do, col_off, pad, name, comm=None):
    t, hd = q.shape
    w = pad + TQ
    nhp = hd // LANES

    def body(sink_ref, q_ref, k_ref, v_ref, b_ref, do_ref, dq_ref, dk_ref, dv_ref, db_ref, ds_ref):
        hp, i = pl.program_id(0), pl.program_id(1)

        @pl.when(i == 0)
        def _():
            dk_ref[...] = jnp.zeros_like(dk_ref)
            dv_ref[...] = jnp.zeros_like(dv_ref)
            db_ref[...] = jnp.zeros_like(db_ref)
            ds_ref[...] = jnp.zeros_like(ds_ref)

        lo = _lo_mask()
        row8 = lax.broadcasted_iota(jnp.int32, (8, LANES), 0)
        dbias = [None, None]
        dsink = jnp.zeros((8, LANES), F32)
        for j in range(ATT_SUB):
            start = pl.multiple_of((i * ATT_SUB + j) * TQ, TQ)
            qv = q_ref[TQ * j:TQ * (j + 1), :]
            dov = do_ref[TQ * j:TQ * (j + 1), :]
            kw = k_ref[pl.ds(start, w), :]
            vw = v_ref[pl.ds(start, w), :]
            kvalid = (start + lax.broadcasted_iota(jnp.int32, (1, w), 1)) >= pad
            dqs, dkw, dvw = [], None, None
            for e in range(2):
                sel = lo if e == 0 else jnp.logical_not(lo)
                qe = jnp.where(sel, qv, jnp.zeros_like(qv))
                doe = jnp.where(sel, dov, jnp.zeros_like(dov))
                p, psink = _attn_probs(qe, kw, b_ref[e], kvalid, sink_ref[2 * hp + e])
                dp = _dot(doe, vw, 1, 1)
                delta = jnp.sum(p * dp, axis=-1, keepdims=True)
                ds = p * (dp - delta)
                dbias[e] = ds if dbias[e] is None else dbias[e] + ds
                dsink = dsink + jnp.where(row8 == e, jnp.sum(-psink * delta, axis=0, keepdims=True), 0.0)
                dsb = ds.astype(BF16)
                dqs.append(_dot(dsb, kw, 1, 0))
                dk_e = _dot(dsb, qe, 0, 0)
                dv_e = _dot(p.astype(BF16), doe, 0, 0)
                dkw = dk_e if dkw is None else dkw + dk_e
                dvw = dv_e if dvw is None else dvw + dv_e
            dq_ref[TQ * j:TQ * (j + 1), :] = jnp.where(lo, dqs[0], dqs[1])
            dk_ref[pl.ds(start, w), :] += dkw
            dv_ref[pl.ds(start, w), :] += dvw
        for e in range(2):
            db_ref[e] += dbias[e]
        ds_ref[0] += dsink

    full = pl.BlockSpec((t + pad, LANES), lambda h, i: (0, h))
    tile = pl.BlockSpec((ATT_SUB * TQ, LANES), lambda h, i: (i, h))
    btile = pl.BlockSpec((2, TQ, w), lambda h, i: (h, 0, 0))
    return _call(
        body, name=name, grid=(nhp, t // (ATT_SUB * TQ)),
        in_specs=[pl.BlockSpec(memory_space=pltpu.SMEM), tile, full, full, btile,
                  pl.BlockSpec((ATT_SUB * TQ, LANES), lambda h, i: (i, h + col_off))],
        out_specs=[tile, full, full, btile, pl.BlockSpec((1, 8, LANES), lambda h, i: (h, 0, 0))],
        out_shape=[jax.ShapeDtypeStruct((t, hd), F32), jax.ShapeDtypeStruct((t + pad, hd), F32),
                   jax.ShapeDtypeStruct((t + pad, hd), F32), jax.ShapeDtypeStruct((N_HEADS, TQ, w), F32),
                   jax.ShapeDtypeStruct((nhp, 8, LANES), F32)],
        args=(sinks, q, kp, vp, bias, do), sem=("parallel", "arbitrary"), comm=comm)


def _halo_prev(tm):
    return lambda i: jnp.maximum(i * (tm // 8) - 1, 0)


def _halo_next(tm, t):
    return lambda i: jnp.minimum((i + 1) * (tm // 8), t // 8 - 1)


def _taps_prev(tile, halo, ktaps, first):
    tm = tile.shape[0]
    ext = jnp.concatenate([jnp.where(first, 0.0, halo), tile], axis=0)
    return [tile] + [pltpu.roll(ext, s, 0)[8:8 + tm] for s in range(1, ktaps)]


def _conv_apply(taps, w_ref, ktaps):
    out = taps[0] * w_ref[ktaps - 1:ktaps, :]
    for s in range(1, ktaps):
        out = out + taps[s] * w_ref[ktaps - 1 - s:ktaps - s, :]
    return out


def _sigmoid(x):
    return jax.nn.sigmoid(x)


def _silu_grad(x):
    sg = _sigmoid(x)
    return x * sg, sg * (1.0 + x * (1.0 - sg))


FFN_TM = 128
FFN_HALO = 16


def _ffn_mid_fwd(gu, w8, b, name):
    t = gu.shape[0]
    f = D_FF
    tm, hr = FFN_TM, FFN_HALO

    def body(g_ref, u_ref, h_ref, w_ref, b_ref, a_ref):
        first = pl.program_id(0) == 0
        ext = jnp.concatenate([jnp.where(first, 0.0, h_ref[...].astype(F32)), g_ref[...].astype(F32)], axis=0)
        taps = [ext[hr:]] + [pltpu.roll(ext, s, 0)[hr:] for s in (1, 2)]
        gc = _conv_apply(taps, w_ref, 3) + b_ref[...]
        a_ref[...] = (gc * _sigmoid(gc) * u_ref[...].astype(F32)).astype(BF16)

    return pl.pallas_call(
        body, name=name, grid=(t // tm,),
        in_specs=[pl.BlockSpec((tm, f), lambda i: (i, 0)), pl.BlockSpec((tm, f), lambda i: (i, 1)),
                  pl.BlockSpec((hr, f), lambda i: (jnp.maximum(i * (tm // hr) - 1, 0), 0)),
                  pl.BlockSpec((8, f), lambda i: (0, 0)), pl.BlockSpec((1, f), lambda i: (0, 0))],
        out_specs=pl.BlockSpec((tm, f), lambda i: (i, 0)), out_shape=jax.ShapeDtypeStruct((t, f), BF16),
        compiler_params=_cp(("parallel",)),
    )(gu, gu, gu, w8, b)


FFN_BT = 256
FFN_BC = 1408


def _ffn_mid_bwd(gu, dxb, w_out, w8, b, name):
    t, d = dxb.shape
    f = D_FF
    tm, hr = FFN_BT, FFN_HALO
    nt = t // tm
    n = tm + hr

    def body(g_ref, u_ref, gp_ref, gn_ref, un_ref, dx_ref, dxn_ref, wo_ref, w_ref, b_ref, dgu_ref, dw_ref, db_ref):
        i = pl.program_id(0)
        first, last = i == 0, i == nt - 1

        @pl.when(first)
        def _():
            dw_ref[...] = jnp.zeros_like(dw_ref)
            db_ref[...] = jnp.zeros_like(db_ref)

        dxe = jnp.concatenate([dx_ref[...], dxn_ref[...]], axis=0)
        row = lax.broadcasted_iota(jnp.int32, (n, 1), 0)
        keep = (row < tm) | jnp.logical_not(last)
        for c in range(0, f, FFN_BC):
            cs = slice(c, c + FFN_BC)
            ext = jnp.concatenate([jnp.where(first, 0.0, gp_ref[:, cs].astype(F32)), g_ref[:, cs].astype(F32),
                                   gn_ref[:, cs].astype(F32)], axis=0)
            taps = [ext[hr:]] + [pltpu.roll(ext, s, 0)[hr:] for s in (1, 2)]
            gc = b_ref[:, cs] + taps[0] * w_ref[2:3, cs] + taps[1] * w_ref[1:2, cs] + taps[2] * w_ref[0:1, cs]
            act, dact = _silu_grad(gc)
            da = _dot(dxe, wo_ref[cs, :], 1, 1)
            up = jnp.concatenate([u_ref[:, cs], un_ref[:, cs]], axis=0).astype(F32)
            dgc = jnp.where(keep, da * up * dact, 0.0)
            dgu_ref[:, f + c:f + c + FFN_BC] = (da[:tm] * act[:tm]).astype(BF16)
            dgu_ref[:, cs] = (dgc[:tm] * w_ref[2:3, cs] + pltpu.roll(dgc, n - 1, 0)[:tm] * w_ref[1:2, cs]
                              + pltpu.roll(dgc, n - 2, 0)[:tm] * w_ref[0:1, cs]).astype(BF16)
            db_ref[:, cs] += jnp.sum(dgc[:tm], axis=0, keepdims=True)
            for s in range(3):
                dw_ref[2 - s:3 - s, cs] += jnp.sum(dgc[:tm] * taps[s][:tm], axis=0, keepdims=True)

    r = tm // hr
    prev = lambda i: jnp.maximum(i * r - 1, 0)
    nxt_blk = lambda i: jnp.minimum((i + 1) * r, t // hr - 1)
    row_f = pl.BlockSpec((tm, f), lambda i: (i, 0))
    return pl.pallas_call(
        body, name=name, grid=(nt,),
        in_specs=[row_f, pl.BlockSpec((tm, f), lambda i: (i, 1)),
                  pl.BlockSpec((hr, f), lambda i: (prev(i), 0)), pl.BlockSpec((hr, f), lambda i: (nxt_blk(i), 0)),
                  pl.BlockSpec((hr, f), lambda i: (nxt_blk(i), 1)),
                  pl.BlockSpec((tm, d), lambda i: (i, 0)), pl.BlockSpec((hr, d), lambda i: (nxt_blk(i), 0)),
                  pl.BlockSpec((f, d), lambda i: (0, 0)),
                  pl.BlockSpec((8, f), lambda i: (0, 0)), pl.BlockSpec((1, f), lambda i: (0, 0))],
        out_specs=[pl.BlockSpec((tm, 2 * f), lambda i: (i, 0)), pl.BlockSpec((8, f), lambda i: (0, 0)),
                   pl.BlockSpec((1, f), lambda i: (0, 0))],
        out_shape=[jax.ShapeDtypeStruct((t, 2 * f), BF16), jax.ShapeDtypeStruct((8, f), F32), jax.ShapeDtypeStruct((1, f), F32)],
        compiler_params=_cp(("arbitrary",)),
    )(gu, gu, gu, gu, gu, dxb, dxb, w_out, w8, b)


PRE_TM = 256
PRE_TC = 1024


def _ssm_pre_fwd(zx, w8, b, name):
    t = zx.shape[0]
    tm, tc = PRE_TM, PRE_TC
    off = D_INNER // tc

    def body(x_ref, h_ref, w_ref, b_ref, o_ref):
        first = pl.program_id(0) == 0
        c = _conv_apply(_taps_prev(x_ref[...], h_ref[...], 4, first), w_ref, 4) + b_ref[...]
        o_ref[...] = c * _sigmoid(c)

    hp = _halo_prev(tm)
    return pl.pallas_call(
        body, name=name, grid=(t // tm, XBC // tc),
        in_specs=[pl.BlockSpec((tm, tc), lambda i, j: (i, j + off)), pl.BlockSpec((8, tc), lambda i, j: (hp(i), j + off)),
                  pl.BlockSpec((8, tc), lambda i, j: (0, j)), pl.BlockSpec((1, tc), lambda i, j: (0, j))],
        out_specs=pl.BlockSpec((tm, tc), lambda i, j: (i, j)), out_shape=jax.ShapeDtypeStruct((t, XBC), F32),
        compiler_params=_cp(("parallel", "parallel")),
    )(zx, zx, w8, b)


def _ssm_pre_bwd(zx, dxbc, w8, b, name):
    t = zx.shape[0]
    tm, tc = PRE_TM, PRE_TC
    off = D_INNER // tc
    nt = t // tm
    n = tm + 8

    def body(x_ref, xp_ref, xn_ref, d_ref, dn_ref, w_ref, b_ref, o_ref, dw_ref, db_ref):
        i = pl.program_id(1)
        first, last = i == 0, i == nt - 1

        @pl.when(first)
        def _():
            dw_ref[...] = jnp.zeros_like(dw_ref)
            db_ref[...] = jnp.zeros_like(db_ref)

        ext = jnp.concatenate([jnp.where(first, 0.0, xp_ref[...]), x_ref[...], xn_ref[...]], axis=0)
        taps = [ext[8:8 + n]] + [pltpu.roll(ext, s, 0)[8:8 + n] for s in (1, 2, 3)]
        c = _conv_apply(taps, w_ref, 4) + b_ref[...]
        _, dact = _silu_grad(c)
        row = lax.broadcasted_iota(jnp.int32, (n, 1), 0)
        dc = jnp.where((row < tm) | jnp.logical_not(last), jnp.concatenate([d_ref[...], dn_ref[...]], axis=0) * dact, 0.0)
        nxt = [dc[:tm]] + [pltpu.roll(dc, n - s, 0)[:tm] for s in (1, 2, 3)]
        o_ref[...] = _conv_apply(nxt, w_ref, 4).astype(BF16)
        db_ref[...] += jnp.sum(dc[:tm], axis=0, keepdims=True)
        for s in range(4):
            dw_ref[3 - s:4 - s, :] += jnp.sum(dc[:tm] * taps[s][:tm], axis=0, keepdims=True)

    hp = _halo_prev(tm)
    hn = _halo_next(tm, t)
    return pl.pallas_call(
        body, name=name, grid=(XBC // tc, nt),
        in_specs=[pl.BlockSpec((tm, tc), lambda j, i: (i, j + off)), pl.BlockSpec((8, tc), lambda j, i: (hp(i), j + off)),
                  pl.BlockSpec((8, tc), lambda j, i: (hn(i), j + off)),
                  pl.BlockSpec((tm, tc), lambda j, i: (i, j)), pl.BlockSpec((8, tc), lambda j, i: (hn(i), j)),
                  pl.BlockSpec((8, tc), lambda j, i: (0, j)), pl.BlockSpec((1, tc), lambda j, i: (0, j))],
        out_specs=[pl.BlockSpec((tm, tc), lambda j, i: (i, j)), pl.BlockSpec((8, tc), lambda j, i: (0, j)),
                   pl.BlockSpec((1, tc), lambda j, i: (0, j))],
        out_shape=[jax.ShapeDtypeStruct((t, XBC), BF16), jax.ShapeDtypeStruct((8, XBC), F32),
                   jax.ShapeDtypeStruct((1, XBC), F32)],
        compiler_params=_cp(("parallel", "arbitrary")),
    )(zx, zx, zx, dxbc, dxbc, w8, b)


def _head_lanes():
    return lax.broadcasted_iota(jnp.int32, (1, LANES), 1) < SSM_HEADS


def _dt_fwd(dtraw, bias, name):
    t = dtraw.shape[0]
    tm = _pick(t, (1024, 512, 256, 128))

    def body(x_ref, b_ref, o_ref):
        v = x_ref[...] + b_ref[...]
        sp = jnp.maximum(v, 0.0) + jnp.log(1.0 + jnp.exp(-jnp.abs(v)))
        o_ref[...] = jnp.where(_head_lanes(), sp, 0.0)

    row = pl.BlockSpec((tm, LANES), lambda i: (i, 0))
    return pl.pallas_call(
        body, name=name, grid=(t // tm,), in_specs=[row, pl.BlockSpec((1, LANES), lambda i: (0, 0))], out_specs=row,
        out_shape=jax.ShapeDtypeStruct((t, LANES), F32), compiler_params=_cp(("parallel",)),
    )(dtraw, bias)


def _dt_bwd(dtraw, bias, ddt, name):
    t = dtraw.shape[0]
    tm = _pick(t, (1024, 512, 256, 128))

    def body(x_ref, b_ref, d_ref, o_ref, db_ref):
        @pl.when(pl.program_id(0) == 0)
        def _():
            db_ref[...] = jnp.zeros_like(db_ref)

        g = jnp.where(_head_lanes(), d_ref[...] * _sigmoid(x_ref[...] + b_ref[...]), 0.0)
        o_ref[...] = g.astype(BF16)
        db_ref[...] += jnp.sum(g, axis=0, keepdims=True)

    row = pl.BlockSpec((tm, LANES), lambda i: (i, 0))
    vec = pl.BlockSpec((1, LANES), lambda i: (0, 0))
    return pl.pallas_call(
        body, name=name, grid=(t // tm,), in_specs=[row, vec, row], out_specs=[row, vec],
        out_shape=[jax.ShapeDtypeStruct((t, LANES), BF16), jax.ShapeDtypeStruct((1, LANES), F32)],
        compiler_params=_cp(("arbitrary",)),
    )(dtraw, bias, ddt)


GROUP_W = D_INNER // SSM_GROUPS
POST_TM = 512


def _ssm_post_fwd(y, xbc, zx, dexp, nw, name):
    t = y.shape[0]
    tm = _pick(t, (POST_TM, 256, 128))

    def body(y_ref, x_ref, z_ref, d_ref, w_ref, o_ref):
        zv = z_ref[...]
        y3 = (y_ref[...] + d_ref[...] * x_ref[...]) * (zv * _sigmoid(zv))
        r = lax.rsqrt(jnp.mean(y3 * y3, axis=-1, keepdims=True) + EPS)
        o_ref[...] = (y3 * r * w_ref[...]).astype(BF16)

    blk = pl.BlockSpec((tm, GROUP_W), lambda i, g: (i, g))
    vec = pl.BlockSpec((1, GROUP_W), lambda i, g: (0, g))
    return pl.pallas_call(
        body, name=name, grid=(t // tm, SSM_GROUPS), in_specs=[blk, blk, blk, vec, vec], out_specs=blk,
        out_shape=jax.ShapeDtypeStruct((t, D_INNER), BF16), compiler_params=_cp(("parallel", "parallel")),
    )(y, xbc, zx, dexp, nw)


def _ssm_post_bwd(dy4, y, xbc, zx, dexp, nw, name):
    t = y.shape[0]
    tm = _pick(t, (POST_TM, 256, 128))

    def body(g_ref, y_ref, x_ref, z_ref, d_ref, w_ref, dy_ref, dxs_ref, dz_ref, dd_ref, dw_ref):
        @pl.when(pl.program_id(1) == 0)
        def _():
            dd_ref[...] = jnp.zeros_like(dd_ref)
            dw_ref[...] = jnp.zeros_like(dw_ref)

        zv = z_ref[...]
        xv = x_ref[...]
        act, dact = _silu_grad(zv)
        y2 = y_ref[...] + d_ref[...] * xv
        y3 = y2 * act
        r = lax.rsqrt(jnp.mean(y3 * y3, axis=-1, keepdims=True) + EPS)
        y3n = y3 * r
        gv = g_ref[...]
        dyn = gv * w_ref[...]
        dy3 = r * (dyn - y3n * jnp.mean(dyn * y3n, axis=-1, keepdims=True))
        dy2 = dy3 * act
        dy_ref[...] = dy2
        dxs_ref[...] = dy2 * d_ref[...]
        dz_ref[...] = (dy3 * y2 * dact).astype(BF16)
        dd_ref[...] += jnp.sum(dy2 * xv, axis=0, keepdims=True)
        dw_ref[...] += jnp.sum(gv * y3n, axis=0, keepdims=True)

    blk = pl.BlockSpec((tm, GROUP_W), lambda g, i: (i, g))
    vec = pl.BlockSpec((1, GROUP_W), lambda g, i: (0, g))
    return pl.pallas_call(
        body, name=name, grid=(SSM_GROUPS, t // tm), in_specs=[blk, blk, blk, blk, vec, vec],
        out_specs=[blk, blk, blk, vec, vec],
        out_shape=[jax.ShapeDtypeStruct((t, D_INNER), F32), jax.ShapeDtypeStruct((t, D_INNER), F32),
                   jax.ShapeDtypeStruct((t, D_INNER), BF16), jax.ShapeDtypeStruct((1, D_INNER), F32),
                   jax.ShapeDtypeStruct((1, D_INNER), F32)],
        compiler_params=_cp(("parallel", "arbitrary")),
    )(dy4, y, xbc, zx, dexp, nw)


def _ssd_common(dt, alog):
    ll = dt.shape[0]
    a_neg = -jnp.exp(alog)
    a = dt * a_neg
    ri = lax.broadcasted_iota(jnp.int32, (ll, ll), 0)
    ci = lax.broadcasted_iota(jnp.int32, (ll, ll), 1)
    tril = ri >= ci
    acs = _dot(tril.astype(F32), a, 1, 0, HI)
    return a_neg, tril, acs, acs.T


def _pair_terms(acs, acs_t, dt, h0, lo):
    ll = acs.shape[0]
    cols = [acs[:, h0 + e:h0 + e + 1] for e in range(2)]
    rows = [acs_t[h0 + e:h0 + e + 1, :] for e in range(2)]
    dtc = [dt[:, h0 + e:h0 + e + 1] for e in range(2)]
    lasts = [c[ll - 1:ll, :] for c in cols]
    dtx = jnp.where(lo, dtc[0], dtc[1])
    eac = jnp.where(lo, jnp.exp(cols[0]), jnp.exp(cols[1]))
    fdec = jnp.where(lo, jnp.exp(lasts[0] - cols[0]), jnp.exp(lasts[1] - cols[1]))
    elast = jnp.where(lo, jnp.exp(lasts[0]), jnp.exp(lasts[1]))
    return cols, rows, dtx, eac, fdec, elast


def _decay(col, row, tril):
    return jnp.where(tril, jnp.exp(jnp.minimum(col - row, 0.0)), 0.0)


def _two_heads_rows(v, lo):
    z = jnp.zeros_like(v)
    return jnp.concatenate([jnp.where(lo, v, z), jnp.where(lo, z, v)], axis=0)


def _two_heads_cols(ms):
    return jnp.concatenate(ms, axis=1)


def _ssd_fwd(xbc, dt, alog, name, comm=None):
    t = xbc.shape[0]
    ll = SSD_L
    nc = t // ll

    def body(x_ref, dt_ref, al_ref, y_ref, sp_ref, st_ref):
        @pl.when(pl.program_id(0) == 0)
        def _():
            st_ref[...] = jnp.zeros_like(st_ref)

        dtv = dt_ref[...]
        _, tril, acs, acs_t = _ssd_common(dtv, al_ref[...])
        lo = _lo_mask()
        sp_ref[0] = st_ref[...]
        for g in range(SSM_GROUPS):
            bg = x_ref[:, D_INNER + SSM_STATE * g:D_INNER + SSM_STATE * (g + 1)].astype(BF16)
            cg = x_ref[:, D_INNER + 512 + SSM_STATE * g:D_INNER + 512 + SSM_STATE * (g + 1)].astype(BF16)
            gm = _dot(cg, bg, 1, 1)
            g0 = GROUP_W * g
            terms = [_pair_terms(acs, acs_t, dtv, 8 * g + 2 * pp, lo) for pp in range(4)]
            dtx, eac, fdec, elast = [jnp.concatenate([tt[k] for tt in terms], axis=1) for k in (2, 3, 4, 5)]
            xg = x_ref[:, g0:g0 + GROUP_W]
            ug = (xg * dtx).astype(BF16)
            sg = st_ref[:, g0:g0 + GROUP_W]
            yst = _dot(cg, sg.astype(BF16), 1, 0) * eac
            st_ref[:, g0:g0 + GROUP_W] = sg * elast + _dot(bg, (xg * (fdec * dtx)).astype(BF16), 0, 0)
            for pp in range(4):
                cols, rows = terms[pp][0], terms[pp][1]
                sl = slice(LANES * pp, LANES * (pp + 1))
                y_in = _dot(_two_heads_cols([(gm * _decay(cols[e], rows[e], tril)).astype(BF16) for e in range(2)]),
                            _two_heads_rows(ug[:, sl], lo), 1, 0)
                y_ref[:, g0 + LANES * pp:g0 + LANES * (pp + 1)] = y_in + yst[:, sl]

    return _call(
        body, name=name, grid=(nc,),
        in_specs=[pl.BlockSpec((ll, XBC), lambda c: (c, 0)), pl.BlockSpec((ll, LANES), lambda c: (c, 0)),
                  pl.BlockSpec((1, LANES), lambda c: (0, 0))],
        out_specs=[pl.BlockSpec((ll, D_INNER), lambda c: (c, 0)), pl.BlockSpec((1, SSM_STATE, D_INNER), lambda c: (c, 0, 0))],
        out_shape=[jax.ShapeDtypeStruct((t, D_INNER), F32), jax.ShapeDtypeStruct((nc, SSM_STATE, D_INNER), F32)],
        scratch_shapes=[pltpu.VMEM((SSM_STATE, D_INNER), F32)],
        args=(xbc, dt, alog), sem=("arbitrary",), comm=comm)


def _ssd_bwd(xbc, dt, alog, sprev, dy, dskip, name, comm=None):
    t = xbc.shape[0]
    ll = SSD_L
    nc = t // ll

    def body(x_ref, dt_ref, al_ref, sp_ref, dy_ref, dk_ref, dx_ref, ddt_ref, dal_ref, ds_ref, colt_ref):
        @pl.when(pl.program_id(0) == 0)
        def _():
            ds_ref[...] = jnp.zeros_like(ds_ref)
            dal_ref[...] = jnp.zeros_like(dal_ref)

        dtv = dt_ref[...]
        a_neg, tril, acs, acs_t = _ssd_common(dtv, al_ref[...])
        lo = _lo_mask()
        hi = jnp.logical_not(lo)
        lane = lax.broadcasted_iota(jnp.int32, (1, LANES), 1)
        colt_ref[...] = jnp.zeros_like(colt_ref)
        rowterm = jnp.zeros((ll, LANES), F32)
        ddt_u = jnp.zeros((ll, LANES), F32)
        dlast = jnp.zeros((1, LANES), F32)

        def halves(v):
            return (jnp.sum(jnp.where(lo, v, 0.0), axis=-1, keepdims=True),
                    jnp.sum(jnp.where(hi, v, 0.0), axis=-1, keepdims=True))

        for g in range(SSM_GROUPS):
            cb0 = D_INNER + SSM_STATE * g
            cc0 = D_INNER + 512 + SSM_STATE * g
            bg = x_ref[:, cb0:cb0 + SSM_STATE].astype(BF16)
            cg = x_ref[:, cc0:cc0 + SSM_STATE].astype(BF16)
            gm = _dot(cg, bg, 1, 1)
            g0 = GROUP_W * g
            terms = [_pair_terms(acs, acs_t, dtv, 8 * g + 2 * pp, lo) for pp in range(4)]
            dtx, eac, fdec, elast = [jnp.concatenate([tt[k] for tt in terms], axis=1) for k in (2, 3, 4, 5)]
            xg = x_ref[:, g0:g0 + GROUP_W]
            u32 = xg * dtx
            ug = u32.astype(BF16)
            dyg = dy_ref[:, g0:g0 + GROUP_W]
            dyb = dyg.astype(BF16)
            spg = sp_ref[0, :, g0:g0 + GROUP_W]
            spb = spg.astype(BF16)
            dsg = ds_ref[:, g0:g0 + GROUP_W]
            dsb = dsg.astype(BF16)
            du_st = _dot(bg, dsb, 1, 0) * fdec
            yst = _dot(cg, spb, 1, 0) * eac
            dye = (dyg * eac).astype(BF16)
            dc_st = _dot(dye, spb, 1, 1)
            db_st = _dot((xg * (fdec * dtx)).astype(BF16), dsb, 1, 1)
            ds_ref[:, g0:g0 + GROUP_W] = dsg * elast + _dot(cg, dye, 0, 0)
            qst_el = du_st * u32
            rq_el = dyg * yst - qst_el
            q_row = jnp.sum(qst_el, axis=0, keepdims=True)
            s_row = jnp.sum(dsg * spg, axis=0, keepdims=True)
            dgm = jnp.zeros((ll, ll), F32)
            for pp in range(4):
                h0 = 8 * g + 2 * pp
                cols, rows = terms[pp][0], terms[pp][1]
                sl = slice(LANES * pp, LANES * (pp + 1))
                decs = [_decay(cols[e], rows[e], tril) for e in range(2)]
                wms = [gm * d for d in decs]
                dum2 = _dot(dyb[:, sl], _two_heads_rows(ug[:, sl], lo), 1, 1)
                du = _dot(jnp.concatenate([wm.astype(BF16) for wm in wms], axis=0),
                          _two_heads_rows(dyb[:, sl], lo), 0, 0) + du_st[:, sl]
                dx_ref[:, g0 + LANES * pp:g0 + LANES * (pp + 1)] = du * dtx[:, sl] + dk_ref[:, g0 + LANES * pp:g0 + LANES * (pp + 1)]
                ddtu = halves(du * xg[:, sl])
                rq = halves(rq_el[:, sl])
                qs = halves(q_row[:, sl])
                ss = halves(s_row[:, sl])
                for e in range(2):
                    dum = dum2[:, ll * e:ll * (e + 1)]
                    dgm = dgm + dum * decs[e]
                    tm_ = dum * wms[e]
                    oh = lane == (h0 + e)
                    rowterm = rowterm + jnp.where(oh, jnp.sum(tm_, axis=1, keepdims=True) + rq[e], 0.0)
                    ddt_u = ddt_u + jnp.where(oh, ddtu[e], 0.0)
                    dlast = dlast + jnp.where(oh, jnp.exp(cols[e][ll - 1:ll, :]) * ss[e] + qs[e], 0.0)
                    colt_ref[h0 + e:h0 + e + 1, :] = jnp.sum(tm_, axis=0, keepdims=True)
            dgb = dgm.astype(BF16)
            dx_ref[:, cc0:cc0 + SSM_STATE] = _dot(dgb, bg, 1, 0) + dc_st
            dx_ref[:, cb0:cb0 + SSM_STATE] = _dot(dgb, cg, 0, 0) + db_st
        row_io = lax.broadcasted_iota(jnp.int32, (ll, LANES), 0)
        dacs = rowterm - colt_ref[...].T + jnp.where(row_io == ll - 1, dlast, 0.0)
        da = _dot(jnp.logical_not(tril).astype(F32) + jnp.where(
            lax.broadcasted_iota(jnp.int32, (ll, ll), 0) == lax.broadcasted_iota(jnp.int32, (ll, ll), 1), 1.0, 0.0),
            dacs, 1, 0, HI)
        ddt_ref[...] = da * a_neg + ddt_u
        dal_ref[...] += jnp.sum(da * dtv, axis=0, keepdims=True) * a_neg

    rev = lambda c: nc - 1 - c
    return _call(
        body, name=name, grid=(nc,),
        in_specs=[pl.BlockSpec((ll, XBC), lambda c: (rev(c), 0)), pl.BlockSpec((ll, LANES), lambda c: (rev(c), 0)),
                  pl.BlockSpec((1, LANES), lambda c: (0, 0)),
                  pl.BlockSpec((1, SSM_STATE, D_INNER), lambda c: (rev(c), 0, 0)),
                  pl.BlockSpec((ll, D_INNER), lambda c: (rev(c), 0)), pl.BlockSpec((ll, D_INNER), lambda c: (rev(c), 0))],
        out_specs=[pl.BlockSpec((ll, XBC), lambda c: (rev(c), 0)), pl.BlockSpec((ll, LANES), lambda c: (rev(c), 0)),
                   pl.BlockSpec((1, LANES), lambda c: (0, 0))],
        out_shape=[jax.ShapeDtypeStruct((t, XBC), F32), jax.ShapeDtypeStruct((t, LANES), F32),
                   jax.ShapeDtypeStruct((1, LANES), F32)],
        scratch_shapes=[pltpu.VMEM((SSM_STATE, D_INNER), F32), pltpu.VMEM((LANES, ll), F32)],
        args=(xbc, dt, alog, sprev, dy, dskip), sem=("arbitrary",), comm=comm)


ADAM_TR = 512


def _sum_parts(parts, name):
    nparts, r, c = parts.shape
    tc = _pick(c, (256, 128))

    def body(p_ref, o_ref):
        g = p_ref[0].astype(F32)
        for k in range(1, nparts):
            g = g + p_ref[k].astype(F32)
        o_ref[...] = g

    return pl.pallas_call(
        body, name=name, grid=(c // tc,), in_specs=[pl.BlockSpec((nparts, r, tc), lambda j: (0, 0, j))],
        out_specs=pl.BlockSpec((r, tc), lambda j: (0, j)), out_shape=jax.ShapeDtypeStruct((r, c), F32),
        compiler_params=_cp(("parallel",)),
    )(parts)


def _adamw(parts, w, m, v, name):
    nl, r, c = w.shape
    assert len(parts) == nl
    tr = _pick(r, (256, 128, 64))
    c1 = 1.0 - ADAM_B1 ** ADAM_STEP
    c2 = 1.0 - ADAM_B2 ** ADAM_STEP

    def body(*refs):
        p_refs = refs[:nl]
        w_ref, m_ref, v_ref, g_ref, d_ref, mo_ref, vo_ref = refs[nl:]
        g = None
        for l, p_ref in enumerate(p_refs):
            s = p_ref[0].astype(F32)
            for k in range(1, p_ref.shape[0]):
                s = s + p_ref[k].astype(F32)
            g = s if g is None else jnp.where(pl.program_id(0) == l, s, g)
        mn = ADAM_B1 * m_ref[0] + (1.0 - ADAM_B1) * g
        vn = ADAM_B2 * v_ref[0] + (1.0 - ADAM_B2) * (g * g)
        g_ref[0] = g
        mo_ref[0] = mn
        vo_ref[0] = vn
        d_ref[0] = -ADAM_LR * ((mn / c1) / (jnp.sqrt(vn / c2) + ADAM_EPS) + ADAM_WD * w_ref[0])

    row = pl.BlockSpec((1, tr, c), lambda l, i: (l, i, 0))
    sd = jax.ShapeDtypeStruct((nl, r, c), F32)
    return pl.pallas_call(
        body, name=name, grid=(nl, r // tr),
        in_specs=[pl.BlockSpec((p.shape[0], tr, c), lambda l, i: (0, i, 0)) for p in parts] + [row, row, row],
        out_specs=[row, row, row, row], out_shape=[sd, sd, sd, sd], compiler_params=_cp(("parallel", "parallel")),
    )(*parts, w, m, v)


def _peers():
    mx, my, mc = lax.axis_index("x"), lax.axis_index("y"), lax.axis_index("c")
    me = 4 * mx + 2 * my + mc
    out = []
    for k in range(1, N_DEV):
        px = 1 - mx if k & 4 else mx
        py = 1 - my if k & 2 else my
        pc = 1 - mc if k & 1 else mc
        out.append(((px, py, pc), 4 * px + 2 * py + pc))
    return me, out


class _Comm:
    def __init__(self, arrs, scatters):
        self.arrs, self.scatters, self.n = list(arrs), list(scatters), len(arrs)
        self.specs = [pl.BlockSpec(memory_space=pl.ANY)] * self.n
        self.out_shape = [jax.ShapeDtypeStruct(x.shape if sc else (N_DEV,) + x.shape, x.dtype)
                          for x, sc in zip(self.arrs, self.scatters)]
        np_ = N_DEV - 1
        self.scratch = [pltpu.SemaphoreType.DMA((np_ * self.n,)), pltpu.SemaphoreType.DMA((np_ * self.n,)),
                        pltpu.SemaphoreType.DMA((self.n,))]

    def _copies(self, x_refs, o_refs, sems):
        send_sems, recv_sems, local_sems = sems
        me, peers = _peers()
        np_ = N_DEV - 1
        local, sends, recvs = [], [], []
        for a in range(self.n):
            mine = x_refs[a].at[me] if self.scatters[a] else x_refs[a]
            local.append(pltpu.make_async_copy(mine, o_refs[a].at[me], local_sems.at[a]))
        for k, (dev, idx) in enumerate(peers):
            for a in range(self.n):
                mine = x_refs[a].at[me] if self.scatters[a] else x_refs[a]
                sends.append(pltpu.make_async_remote_copy(
                    src_ref=x_refs[a].at[idx] if self.scatters[a] else x_refs[a], dst_ref=o_refs[a].at[me],
                    send_sem=send_sems.at[a * np_ + k], recv_sem=recv_sems.at[a * np_ + k], device_id=dev, device_id_type=MESH))
                recvs.append(pltpu.make_async_remote_copy(
                    src_ref=mine, dst_ref=o_refs[a].at[idx], send_sem=send_sems.at[a * np_ + k],
                    recv_sem=recv_sems.at[a * np_ + k], device_id=dev, device_id_type=MESH))
        return local, sends, recvs

    def start(self, x_refs, o_refs, sems):
        local, sends, _ = self._copies(x_refs, o_refs, sems)
        for cp in local + sends:
            cp.start()

    def wait(self, x_refs, o_refs, sems):
        local, sends, recvs = self._copies(x_refs, o_refs, sems)
        for cp in recvs:
            cp.wait_recv()
        for cp in sends:
            cp.wait_send()
        for cp in local:
            cp.wait()


class _Gather2(_Comm):
    def __init__(self, arrs):
        super().__init__(arrs, [False] * len(arrs))

    def _plan(self, x_refs, o_refs, sems):
        send_sems, recv_sems, local_sems = sems
        mx, my, mc = lax.axis_index("x"), lax.axis_index("y"), lax.axis_index("c")
        slot = lambda px, py, pc: 4 * px + 2 * py + pc
        sib = (mx, my, 1 - mc)
        chips = [(1 - mx, my), (mx, 1 - my), (1 - mx, 1 - my)]
        np_ = N_DEV - 1
        local, first, passed, arrive_first, arrive_rest = [], [], [], [], []

        def copy(a, k, src, block, to):
            return pltpu.make_async_remote_copy(
                src_ref=src, dst_ref=o_refs[a].at[block], send_sem=send_sems.at[a * np_ + k], recv_sem=recv_sems.at[a * np_ + k],
                device_id=to, device_id_type=MESH)

        for a in range(self.n):
            me = slot(mx, my, mc)
            local.append(pltpu.make_async_copy(x_refs[a], o_refs[a].at[me], local_sems.at[a]))
            first.append(copy(a, 0, x_refs[a], me, sib))
            arrive_rest.append(copy(a, 0, x_refs[a], slot(*sib), sib))
            for j, (cx, cy) in enumerate(chips):
                first.append(copy(a, 1 + j, x_refs[a], me, (cx, cy, mc)))
                arrive_first.append(copy(a, 1 + j, x_refs[a], slot(cx, cy, mc), (cx, cy, mc)))
                passed.append(copy(a, 4 + j, o_refs[a].at[slot(cx, cy, mc)], slot(cx, cy, mc), sib))
                arrive_rest.append(copy(a, 4 + j, x_refs[a], slot(cx, cy, 1 - mc), sib))
        return local, first, passed, arrive_first, arrive_rest

    def start(self, x_refs, o_refs, sems):
        local, first, _, _, _ = self._plan(x_refs, o_refs, sems)
        for cp in local + first:
            cp.start()

    def wait(self, x_refs, o_refs, sems):
        local, first, passed, arrive_first, arrive_rest = self._plan(x_refs, o_refs, sems)
        for arrived, onward in zip(arrive_first, passed):
            arrived.wait_recv()
            onward.start()
        for cp in arrive_rest:
            cp.wait_recv()
        for cp in first + passed:
            cp.wait_send()
        for cp in local:
            cp.wait()


def _call(body, *, name, grid, in_specs, out_specs, out_shape, args, scratch_shapes=(), sem=None, comm=None):
    if comm is None:
        outs = pl.pallas_call(
            body, name=name, grid=grid, in_specs=list(in_specs), out_specs=list(out_specs), out_shape=list(out_shape),
            scratch_shapes=list(scratch_shapes), compiler_params=_cp(sem),
        )(*args)
        return list(outs), []
    n_in, n_out, nc = len(in_specs), len(out_specs), comm.n
    nsteps = 1
    for g in grid:
        nsteps *= g

    def carrier(*refs):
        ins, cin = refs[:n_in], refs[n_in:n_in + nc]
        outs, cout = refs[n_in + nc:n_in + nc + n_out], refs[n_in + nc + n_out:n_in + 2 * nc + n_out]
        rest = refs[n_in + 2 * nc + n_out:]
        scratch, sems = rest[:len(rest) - 3], rest[len(rest) - 3:]
        if nsteps == 1:
            comm.start(cin, cout, sems)
            body(*ins, *outs, *scratch)
            comm.wait(cin, cout, sems)
            return
        step = 0
        for d, g in enumerate(grid):
            step = step * g + pl.program_id(d)

        @pl.when(step == 0)
        def _():
            comm.start(cin, cout, sems)

        body(*ins, *outs, *scratch)

        @pl.when(step == nsteps - 1)
        def _():
            comm.wait(cin, cout, sems)

    outs = pl.pallas_call(
        carrier, name=name, grid=grid, in_specs=list(in_specs) + comm.specs, out_specs=list(out_specs) + comm.specs,
        out_shape=list(out_shape) + comm.out_shape, scratch_shapes=list(scratch_shapes) + comm.scratch,
        compiler_params=_cp(("arbitrary",) * len(grid) if grid else None),
    )(*args, *comm.arrs)
    return list(outs[:n_out]), list(outs[n_out:])


def _exchange(comm, name):
    return _call(lambda *refs: None, name=name, grid=(), in_specs=[], out_specs=[], out_shape=[], args=[], comm=comm)[1]


def _pack(arrs, dtype, lead=()):
    nl = len(lead)
    flat = jnp.concatenate([a.astype(dtype).reshape(lead + (-1,)) for a in arrs], axis=nl)
    n = flat.shape[-1]
    rows = -(-n // (LANES * ADAM_TR)) * ADAM_TR
    flat = jnp.pad(flat, [(0, 0)] * nl + [(0, rows * LANES - n)])
    return flat.reshape(lead + (rows, LANES))


def _unpack(flat, shapes, lead=()):
    nl = len(lead)
    flat = flat.reshape(lead + (-1,))
    out, o = [], 0
    for s in shapes:
        n = 1
        for d in s:
            n *= d
        out.append(lax.slice_in_dim(flat, o, o + n, axis=nl).reshape(lead + tuple(s)))
        o += n
    return out


def _join(g, ax):
    return jnp.concatenate([g[d] for d in range(N_DEV)], axis=ax)


def _split(full, ax):
    n = full.shape[ax] // N_DEV
    return jnp.stack([lax.slice_in_dim(full, d * n, (d + 1) * n, axis=ax) for d in range(N_DEV)])


_WEIGHTS = ['norm_mix', 'norm_ffn', 'attn_w_in', 'attn_w_out', 'relpos_table', 'q_norm_a', 'k_norm_a', 'q_norm_b',
            'k_norm_b', 'sinks', 'ssm_w_in', 'ssm_conv_w', 'ssm_conv_b', 'ssm_dt_bias', 'ssm_a_log', 'ssm_d', 'ssm_norm',
            'ssm_w_out', 'ffn_w_in', 'ffn_conv_w', 'ffn_conv_b', 'ffn_w_out']
_SHARD_AX = {'attn_w_in': 2, 'attn_w_out': 1, 'ssm_w_in': 2, 'ssm_conv_w': 2, 'ssm_conv_b': 1, 'ssm_norm': 1,
             'ssm_w_out': 1, 'ffn_w_in': 2, 'ffn_conv_w': 2, 'ffn_w_out': 1}
_BIG = ['attn_w_in', 'attn_w_out', 'ssm_w_in', 'ssm_w_out', 'ffn_w_in', 'ffn_w_out']
_SMALL = ['ssm_conv_w', 'ssm_conv_b', 'ssm_norm', 'ffn_conv_w']
_AX2 = {n: _SHARD_AX[n] - 1 for n in _BIG}
_REPL = [n for n in _WEIGHTS if n not in _SHARD_AX]


def _rows8(w):
    return jnp.pad(w, ((0, 8 - w.shape[0]), (0, 0)))


def _lanes128(v):
    return jnp.pad(v, (0, LANES - v.shape[0])).reshape(1, LANES)


def _band_mask(n_prev, pad):
    cq = jnp.arange(TQ)[:, None] // CHUNK
    ck = jnp.arange(pad + TQ)[None, :] // CHUNK
    return (ck >= cq) & (ck <= cq + n_prev)


def _ffn_fwd(xin, g, w_in_t, w8, cb, tag):
    gu, h = _rms_mm(xin, g, w_in_t, 2 * D_FF, f"mm_ffn_in{tag}", BF16)
    a = _ffn_mid_fwd(gu, w8, cb, f"ffn_mid{tag}")
    return a, (h, gu, a)


def _ffn_bwd(dx, dxb, xin, g, w_in_t, w8, cb, w_out, saved, tag):
    h, gu, a = saved
    dw_out = _mm_tn(a, dxb, f"mm_ffn_dwout{tag}")
    dgu, dw8, dcb = _ffn_mid_bwd(gu, dxb, w_out, w8, cb, f"ffn_mid_bwd{tag}")
    dw_in_t = _mm_tn(dgu, h, f"mm_ffn_dwin{tag}")
    dxp, dxpb, dg = _mm_rms_bwd(dgu, w_in_t, 0, None, xin, g, dx, f"mm_ffn_dh{tag}")
    return dxp, dxpb, dg, dw_in_t, dw8[:3], dcb, dw_out


def kernel(x, norm_mix, norm_ffn, attn_w_in, attn_w_out, relpos_table, q_norm_a, k_norm_a, q_norm_b, k_norm_b, sinks, ssm_w_in, ssm_conv_w, ssm_conv_b, ssm_dt_bias, ssm_a_log, ssm_d, ssm_norm, ssm_w_out, ffn_w_in, ffn_conv_w, ffn_conv_b, ffn_w_out, loss_target, m_norm_mix, m_norm_ffn, m_attn_w_in, m_attn_w_out, m_relpos_table, m_q_norm_a, m_k_norm_a, m_q_norm_b, m_k_norm_b, m_sinks, m_ssm_w_in, m_ssm_conv_w, m_ssm_conv_b, m_ssm_dt_bias, m_ssm_a_log, m_ssm_d, m_ssm_norm, m_ssm_w_out, m_ffn_w_in, m_ffn_conv_w, m_ffn_conv_b, m_ffn_w_out, v_norm_mix, v_norm_ffn, v_attn_w_in, v_attn_w_out, v_relpos_table, v_q_norm_a, v_k_norm_a, v_q_norm_b, v_k_norm_b, v_sinks, v_ssm_w_in, v_ssm_conv_w, v_ssm_conv_b, v_ssm_dt_bias, v_ssm_a_log, v_ssm_d, v_ssm_norm, v_ssm_w_out, v_ffn_w_in, v_ffn_conv_w, v_ffn_conv_b, v_ffn_w_out):
    w = dict(norm_mix=norm_mix, norm_ffn=norm_ffn, attn_w_in=attn_w_in, attn_w_out=attn_w_out, relpos_table=relpos_table,
             q_norm_a=q_norm_a, k_norm_a=k_norm_a, q_norm_b=q_norm_b, k_norm_b=k_norm_b, sinks=sinks, ssm_w_in=ssm_w_in,
             ssm_conv_w=ssm_conv_w, ssm_conv_b=ssm_conv_b, ssm_dt_bias=ssm_dt_bias, ssm_a_log=ssm_a_log, ssm_d=ssm_d,
             ssm_norm=ssm_norm, ssm_w_out=ssm_w_out, ffn_w_in=ffn_w_in, ffn_conv_w=ffn_conv_w, ffn_conv_b=ffn_conv_b,
             ffn_w_out=ffn_w_out)
    mom = dict(norm_mix=m_norm_mix, norm_ffn=m_norm_ffn, attn_w_in=m_attn_w_in, attn_w_out=m_attn_w_out,
               relpos_table=m_relpos_table, q_norm_a=m_q_norm_a, k_norm_a=m_k_norm_a, q_norm_b=m_q_norm_b,
               k_norm_b=m_k_norm_b, sinks=m_sinks, ssm_w_in=m_ssm_w_in, ssm_conv_w=m_ssm_conv_w, ssm_conv_b=m_ssm_conv_b,
               ssm_dt_bias=m_ssm_dt_bias, ssm_a_log=m_ssm_a_log, ssm_d=m_ssm_d, ssm_norm=m_ssm_norm, ssm_w_out=m_ssm_w_out,
               ffn_w_in=m_ffn_w_in, ffn_conv_w=m_ffn_conv_w, ffn_conv_b=m_ffn_conv_b, ffn_w_out=m_ffn_w_out)
    var = dict(norm_mix=v_norm_mix, norm_ffn=v_norm_ffn, attn_w_in=v_attn_w_in, attn_w_out=v_attn_w_out,
               relpos_table=v_relpos_table, q_norm_a=v_q_norm_a, k_norm_a=v_k_norm_a, q_norm_b=v_q_norm_b,
               k_norm_b=v_k_norm_b, sinks=v_sinks, ssm_w_in=v_ssm_w_in, ssm_conv_w=v_ssm_conv_w, ssm_conv_b=v_ssm_conv_b,
               ssm_dt_bias=v_ssm_dt_bias, ssm_a_log=v_ssm_a_log, ssm_d=v_ssm_d, ssm_norm=v_ssm_norm, ssm_w_out=v_ssm_w_out,
               ffn_w_in=v_ffn_w_in, ffn_conv_w=v_ffn_conv_w, ffn_conv_b=v_ffn_conv_b, ffn_w_out=v_ffn_w_out)

    def piece(n, l):
        return (w[n][l].T if _AX2[n] == 1 else w[n][l]).astype(BF16)

    def gather_of(names_layers):
        return _Gather2([piece(n, l) for n, l in names_layers])

    def joined(got):
        return [g.reshape(-1, D_MODEL) for g in got]

    first = [('attn_w_in', 0), ('attn_w_out', 0)]
    got = _exchange(_Gather2([piece(n, l) for n, l in first] + [_pack([w[n] for n in _SMALL], F32)]), "gather_attn")
    wt_attn_in, w_attn_out = joined(got[:2])
    full = {}
    for n, g in zip(_SMALL, _unpack(got[2], [w[n].shape for n in _SMALL], lead=(N_DEV,))):
        full[n] = _join(g, _SHARD_AX[n])
    ssm_cw8 = _rows8(full['ssm_conv_w'][0])
    ssm_cb = full['ssm_conv_b']
    ssm_nw = full['ssm_norm']
    ffn_cw8 = [_rows8(full['ffn_conv_w'][l]) for l in range(2)]
    ffn_cb = [ffn_conv_b[l:l + 1] for l in range(2)]

    x0 = x[0]
    target = loss_target[0]
    t = x0.shape[0]

    g_mix0, g_mix1 = norm_mix[0:1], norm_mix[1:2]
    g_ffn0, g_ffn1 = norm_ffn[0:1], norm_ffn[1:2]
    proj, h0 = _rms_mm(x0, g_mix0, wt_attn_in, 2304, "mm_attn_in", F32)
    hn_w = jnp.concatenate([jnp.tile(v, (1, 2)) for v in (q_norm_a, k_norm_a, q_norm_b, k_norm_b)], axis=0)
    qa, kpa, vpa, qb, kpb, vpb = _headnorm_fwd(proj, hn_w, "headnorm")
    table = jnp.pad(relpos_table[0], ((0, 0), (0, REL_W - (2 * MAX_REL + 1))))
    bias_a = jnp.where(_band_mask(A_PREV, PAD_A)[None], jnp.transpose(_relpos_fwd(table, "relpos_bias"), (1, 0, 2)), NEG)
    rel_b = jnp.arange(TQ)[:, None] - (jnp.arange(PAD_B + TQ)[None, :] - PAD_B)
    slopes = 2.0 ** (-8.0 * jnp.arange(1, N_HEADS + 1, dtype=F32) / N_HEADS)
    bias_b = jnp.where(_band_mask(B_PREV, PAD_B)[None], -slopes[:, None, None] * jnp.abs(rel_b).astype(F32)[None], NEG)
    no_sinks = jnp.full((N_HEADS,), NEG, F32)
    ffn0_w, ssm_w, ffn1_w = [('ffn_w_in', 0), ('ffn_w_out', 0)], [('ssm_w_in', 0), ('ssm_w_out', 0)], [('ffn_w_in', 1), ('ffn_w_out', 1)]
    oa, got = _attn_fwd(qa, kpa, vpa, bias_a, no_sinks, PAD_A, "attn_a", comm=gather_of(ffn0_w + ssm_w))
    wt_ffn_in0, w_ffn_out0, wt_ssm_in, w_ssm_out = joined(got)
    ob, got = _attn_fwd(qb, kpb, vpb, bias_b, sinks[0], PAD_B, "attn_b", comm=gather_of(ffn1_w))
    wt_ffn_in1, w_ffn_out1 = joined(got)
    wt_ssm_dt = jnp.pad(wt_ssm_in[ZX:], ((0, LANES - SSM_HEADS), (0, 0)))
    x1 = _mm(oa, w_attn_out, "mm_attn_out_a", res=x0, b_rows=(0, 512))
    x1 = _mm(ob, w_attn_out, "mm_attn_out_b", res=x1, b_rows=(512, 512))
    a0, ffn0_saved = _ffn_fwd(x1, g_ffn0, wt_ffn_in0, ffn_cw8[0], ffn_cb[0], "0")
    x2 = _mm(a0, w_ffn_out0, "mm_ffn_out0", res=x1)

    zx, h2 = _rms_mm(x2, g_mix1, wt_ssm_in, ZX, "mm_ssm_in", F32)
    dtraw = _mm(h2, wt_ssm_dt, "mm_ssm_dt", trans_b=True)
    dt_bias = _lanes128(ssm_dt_bias[0])
    alog = _lanes128(ssm_a_log[0])
    dexp = jnp.repeat(ssm_d[0], HEAD_DIM).reshape(1, D_INNER)
    xbc = _ssm_pre_fwd(zx, ssm_cw8, ssm_cb, "ssm_pre")
    dt = _dt_fwd(dtraw, dt_bias, "ssm_dt")
    (y, sprev), _ = _ssd_fwd(xbc, dt, alog, "ssd_fwd")
    y4 = _ssm_post_fwd(y, xbc, zx, dexp, ssm_nw, "ssm_post")
    x3 = _mm(y4, w_ssm_out, "mm_ssm_out", res=x2)
    a1, ffn1_saved = _ffn_fwd(x3, g_ffn1, wt_ffn_in1, ffn_cw8[1], ffn_cb[1], "1")

    dx4, dx4b, sq = _mm_loss(a1, w_ffn_out1, x3, target, "mm_ffn_out1_loss")
    loss = lax.psum(0.5 * jnp.sum(sq) / D_MODEL, ("x", "y", "c"))

    grads = {}

    def scatter_of(grads_2d):
        return _Comm([g.reshape(N_DEV, -1, D_MODEL) for g in grads_2d], [True] * len(grads_2d))

    dx3, dx3b, dg_ffn1, dwtin1, dcw1, dcb1, dwout1 = _ffn_bwd(
        dx4, dx4b, x3, g_ffn1, wt_ffn_in1, ffn_cw8[1], ffn_cb[1], w_ffn_out1, ffn1_saved, "1")

    dy4 = _mm(dx3b, w_ssm_out, "mm_ssm_dy", trans_b=True)
    dw_ssm_out = _mm_tn(y4, dx3b, "mm_ssm_dwout")
    dyv, dskip, dz, dd_lane, dnw = _ssm_post_bwd(dy4, y, xbc, zx, dexp, ssm_nw, "ssm_post_bwd")
    (dxbc, ddt, dalog), parts_ffn1 = _ssd_bwd(xbc, dt, alog, sprev, dyv, dskip, "ssd_bwd", comm=scatter_of([dwtin1, dwout1]))
    dxr, dcw_s, dcb_s = _ssm_pre_bwd(zx, dxbc, ssm_cw8, ssm_cb, "ssm_pre_bwd")
    ddtraw, ddtb = _dt_bwd(dtraw, dt_bias, ddt, "ssm_dt_bwd")
    dh2 = _mm(dz, wt_ssm_in, "mm_ssm_dh_z", b_rows=(0, D_INNER))
    dh2 = _mm(dxr, wt_ssm_in[D_INNER:ZX], "mm_ssm_dh_x", res=dh2)
    dwt_ssm_in = jnp.concatenate([
        _mm_tn(dz, h2, "mm_ssm_dwin_z"), _mm_tn(dxr, h2, "mm_ssm_dwin_x"),
        _mm_tn(ddtraw, h2, "mm_ssm_dwin_dt")[:SSM_HEADS]], axis=0)
    dx2, dx2b, dg_mix1 = _mm_rms_bwd(ddtraw, wt_ssm_dt, 0, dh2, x2, g_mix1, dx3, "mm_ssm_dh_dt")
    grads['ssm_conv_w'] = dcw_s[:4][None]
    grads['ssm_conv_b'] = dcb_s
    grads['ssm_norm'] = dnw
    grads['ssm_dt_bias'] = ddtb[:, :SSM_HEADS]
    grads['ssm_a_log'] = dalog[:, :SSM_HEADS]
    grads['ssm_d'] = jnp.sum(dd_lane.reshape(SSM_HEADS, HEAD_DIM), axis=1)[None]

    dx1, dx1b, dg_ffn0, dwtin0, dcw0, dcb0, dwout0 = _ffn_bwd(
        dx2, dx2b, x1, g_ffn0, wt_ffn_in0, ffn_cw8[0], ffn_cb[0], w_ffn_out0, ffn0_saved, "0")
    grads['ffn_conv_w'] = jnp.stack([dcw0, dcw1])
    grads['ffn_conv_b'] = jnp.concatenate([dcb0, dcb1], axis=0)
    grads['norm_ffn'] = jnp.concatenate([dg_ffn0, dg_ffn1], axis=0)

    do = _mm(dx1b, w_attn_out, "mm_attn_do", out_dtype=BF16, trans_b=True)
    dw_attn_out = jnp.concatenate([_mm_tn(oa, dx1b, "mm_attn_dwout_a"), _mm_tn(ob, dx1b, "mm_attn_dwout_b")], axis=0)
    (dqa, dkpa, dvpa, dbias_a, _), parts_ssm = _attn_bwd(
        qa, kpa, vpa, bias_a, no_sinks, do, 0, PAD_A, "attn_a_bwd", comm=scatter_of([dwt_ssm_in, dw_ssm_out, dw_attn_out]))
    (dqb, dkpb, dvpb, _, dsink), parts_ffn0 = _attn_bwd(
        qb, kpb, vpb, bias_b, sinks[0], do, 4, PAD_B, "attn_b_bwd", comm=scatter_of([dwtin0, dwout0]))
    grads['relpos_table'] = _relpos_bwd(jnp.transpose(dbias_a, (1, 0, 2)), "relpos_bwd")[None, :, :2 * MAX_REL + 1]
    grads['sinks'] = dsink[:, :2, 0].reshape(1, N_HEADS)
    dproj, dhn = _headnorm_bwd(proj, hn_w, dqa, dkpa, dvpa, dqb, dkpb, dvpb, "headnorm_bwd")
    dhn = dhn[:, :HEAD_DIM] + dhn[:, HEAD_DIM:]
    for k, n in enumerate(('q_norm_a', 'k_norm_a', 'q_norm_b', 'k_norm_b')):
        grads[n] = dhn[k:k + 1]
    dwt_attn_in = _mm_tn(dproj, h0, "mm_attn_dwin")
    dx0, _, dg_mix0, parts_attn_in = _mm_rms_bwd(dproj, wt_attn_in, 0, None, x0, g_mix0, dx1, "mm_attn_dh",
                                                 comm=scatter_of([dwt_attn_in]))
    grads['norm_mix'] = jnp.concatenate([dg_mix0, dg_mix1], axis=0)

    def summed_t(parts, name):
        return _sum_parts(parts, name).T[None]

    sm_shapes = [w[n].shape for n in _SMALL]
    rp_shapes = [w[n].shape for n in _REPL]
    recv = _exchange(_Comm(
        [_pack([_split(grads[n], _SHARD_AX[n]) for n in _SMALL], F32, lead=(N_DEV,)), _pack([grads[n] for n in _REPL], F32)],
        [True, False]), "exchange_small")
    big_parts = {
        'attn_w_in': [summed_t(parts_attn_in[0], "sum_attn_w_in")], 'attn_w_out': [parts_ssm[2]],
        'ssm_w_in': [summed_t(parts_ssm[0], "sum_ssm_w_in")], 'ssm_w_out': [parts_ssm[1]],
        'ffn_w_in': [summed_t(parts_ffn0[0], "sum_ffn_w_in0"), summed_t(parts_ffn1[0], "sum_ffn_w_in1")],
        'ffn_w_out': [parts_ffn0[1], parts_ffn1[1]],
    }
    res = [{}, {}, {}, {}]
    for n in _BIG:
        for kind, a in enumerate(_adamw(big_parts[n], w[n], mom[n], var[n], f"adamw_{n}")):
            res[kind][n] = a
    for names, shapes, parts in ((_SMALL, sm_shapes, recv[0]), (_REPL, rp_shapes, recv[1])):
        outs = _adamw([parts], _pack([w[n] for n in names], F32)[None], _pack([mom[n] for n in names], F32)[None],
                      _pack([var[n] for n in names], F32)[None], "adamw_" + ("small" if names is _SMALL else "replicated"))
        for kind, flat in enumerate(outs):
            for n, a in zip(names, _unpack(flat[0], shapes)):
                res[kind][n] = a
    return (loss, dx0[None], *[res[0][n] for n in _WEIGHTS], *[res[1][n] for n in _WEIGHTS],
            *[res[2][n] for n in _WEIGHTS], *[res[3][n] for n in _WEIGHTS])
```

```python
import jax
import jax.numpy as jnp
from jax import lax
from jax.experimental import pallas as pl
from jax.experimental.pallas import tpu as pltpu

F32 = jnp.float32
BF16 = jnp.bfloat16
HI = lax.Precision.HIGHEST
MESH = pl.DeviceIdType.MESH
NEG = -1e30

N_DEV = 8
D_MODEL = 1024
EPS = 1e-6
CHUNK = 64
HEAD_DIM = 64
N_HEADS = 8
A_PREV = 8
B_PREV = 2
MAX_REL = 256
TQ = 2 * CHUNK
ATT_SUB = 4
PAD_A = A_PREV * CHUNK
PAD_B = B_PREV * CHUNK
REL_W = PAD_A + TQ
D_INNER = 2048
SSM_HEADS = 32
SSM_GROUPS = 4
SSM_STATE = 128
XBC = D_INNER + 2 * SSM_GROUPS * SSM_STATE
ZX = D_INNER + XBC
D_FF = 2816
SSD_L = 128
LANES = 128
VMEM_LIMIT = 56 << 20

ADAM_LR, ADAM_B1, ADAM_B2, ADAM_EPS, ADAM_WD, ADAM_STEP = 0.001, 0.9, 0.999, 1e-08, 0.01, 10


def _cp(sem=None):
    return pltpu.CompilerParams(dimension_semantics=sem, vmem_limit_bytes=VMEM_LIMIT)


def _dot(a, b, ca=1, cb=0, prec=None):
    return lax.dot_general(a, b, (((ca,), (cb,)), ((), ())), preferred_element_type=F32, precision=prec)


def _pick(n, cands):
    for c in cands:
        if n % c == 0:
            return c
    return n


def _lo_mask():
    return lax.broadcasted_iota(jnp.int32, (1, LANES), 1) < HEAD_DIM


_TN_CHUNKS = (1408, 1536, 1152, 1024, 512, 256, 128)


TN_MAX_ROWS = 3072


def _mm_tn(a, b, name):
    kdim, m = a.shape
    n = b.shape[1]
    assert b.shape[0] == kdim, (a.shape, b.shape)
    mb = m if m <= TN_MAX_ROWS else m // 2
    tn = _pick(n, _TN_CHUNKS)
    tk = _pick(kdim, (512, 256, 128))
    nk = kdim // tk

    def body(a_ref, b_ref, o_ref, acc):
        k = pl.program_id(1)

        @pl.when(k == 0)
        def _():
            acc[...] = jnp.zeros_like(acc)

        av = a_ref[...]
        for c in range(0, n, tn):
            acc[:, c:c + tn] += _dot(av, b_ref[:, c:c + tn], 0, 0)

        @pl.when(k == nk - 1)
        def _():
            o_ref[...] = acc[...].astype(BF16)

    return pl.pallas_call(
        body, name=name, grid=(m // mb, nk),
        in_specs=[pl.BlockSpec((tk, mb), lambda j, k: (k, j)), pl.BlockSpec((tk, n), lambda j, k: (k, 0))],
        out_specs=pl.BlockSpec((mb, n), lambda j, k: (j, 0)), out_shape=jax.ShapeDtypeStruct((m, n), BF16),
        scratch_shapes=[pltpu.VMEM((mb, n), F32)], compiler_params=_cp(("parallel", "arbitrary")),
    )(a, b)


def _mm(a, b, name, out_dtype=F32, res=None, trans_b=False, b_rows=None):
    m, kdim = a.shape
    if b_rows is None:
        b_rows = (0, b.shape[0])
    off, rows = b_rows
    n = rows if trans_b else b.shape[1]
    assert (b.shape[1] if trans_b else rows) == kdim and off % rows == 0, (a.shape, b.shape, b_rows)
    tn = _pick(n, _TN_CHUNKS)
    tm = _pick(m, (256, 128) if n > 2304 else (512, 256, 128))

    def body(*refs):
        if res is None:
            a_ref, b_ref, o_ref = refs
        else:
            a_ref, b_ref, r_ref, o_ref = refs
        av = a_ref[...]
        for c in range(0, n, tn):
            r = _dot(av, b_ref[c:c + tn, :], 1, 1) if trans_b else _dot(av, b_ref[:, c:c + tn], 1, 0)
            if res is not None:
                r = r + r_ref[:, c:c + tn]
            o_ref[:, c:c + tn] = r.astype(out_dtype)

    in_specs = [pl.BlockSpec((tm, kdim), lambda i: (i, 0)), pl.BlockSpec((rows, b.shape[1]), lambda i: (off // rows, 0))]
    args = [a, b]
    if res is not None:
        in_specs.append(pl.BlockSpec((tm, n), lambda i: (i, 0)))
        args.append(res)
    return pl.pallas_call(
        body, name=name, grid=(m // tm,), in_specs=in_specs, out_specs=pl.BlockSpec((tm, n), lambda i: (i, 0)),
        out_shape=jax.ShapeDtypeStruct((m, n), out_dtype), compiler_params=_cp(("parallel",)),
    )(*args)


def _rms_mm(x, g, bt, n, name, out_dtype):
    t, d = x.shape
    tn = _pick(n, _TN_CHUNKS)
    tm = _pick(t, (256, 128))

    def body(x_ref, g_ref, b_ref, o_ref, h_ref):
        xv = x_ref[...]
        r = lax.rsqrt(jnp.mean(xv * xv, axis=-1, keepdims=True) + EPS)
        h = (xv * r * g_ref[...]).astype(BF16)
        h_ref[...] = h
        for c in range(0, n, tn):
            o_ref[:, c:c + tn] = _dot(h, b_ref[c:c + tn, :], 1, 1).astype(out_dtype)

    row = pl.BlockSpec((tm, d), lambda i: (i, 0))
    return pl.pallas_call(
        body, name=name, grid=(t // tm,),
        in_specs=[row, pl.BlockSpec((1, d), lambda i: (0, 0)), pl.BlockSpec(bt.shape, lambda i: (0, 0))],
        out_specs=[pl.BlockSpec((tm, n), lambda i: (i, 0)), row],
        out_shape=[jax.ShapeDtypeStruct((t, n), out_dtype), jax.ShapeDtypeStruct((t, d), BF16)],
        compiler_params=_cp(("parallel",)),
    )(x, g, bt)


def _mm_rms_bwd(a, b, b_off, dh_prev, x, g, dres, name, comm=None):
    t, d = x.shape
    kdim = a.shape[1]
    assert b_off % kdim == 0 and b.shape[1] == d, (a.shape, b.shape, b_off)
    tm = _pick(t, (256, 128))

    def body(*refs):
        if dh_prev is None:
            a_ref, b_ref, x_ref, g_ref, dr_ref, dx_ref, dxb_ref, dg_ref = refs
            dhv = _dot(a_ref[...], b_ref[...], 1, 0)
        else:
            a_ref, b_ref, p_ref, x_ref, g_ref, dr_ref, dx_ref, dxb_ref, dg_ref = refs
            dhv = _dot(a_ref[...], b_ref[...], 1, 0) + p_ref[...]
        xv = x_ref[...]
        r = lax.rsqrt(jnp.mean(xv * xv, axis=-1, keepdims=True) + EPS)
        xh = xv * r
        dxh = dhv * g_ref[...]
        dx = dr_ref[...] + r * (dxh - xh * jnp.mean(dxh * xh, axis=-1, keepdims=True))
        dx_ref[...] = dx
        dxb_ref[...] = dx.astype(BF16)

        @pl.when(pl.program_id(0) == 0)
        def _():
            dg_ref[...] = jnp.zeros_like(dg_ref)

        dg_ref[...] += jnp.sum(dhv * xh, axis=0, keepdims=True)

    row = pl.BlockSpec((tm, d), lambda i: (i, 0))
    vec = pl.BlockSpec((1, d), lambda i: (0, 0))
    in_specs = [pl.BlockSpec((tm, kdim), lambda i: (i, 0)), pl.BlockSpec((kdim, d), lambda i: (b_off // kdim, 0))]
    args = [a, b]
    if dh_prev is not None:
        in_specs.append(row)
        args.append(dh_prev)
    outs, got = _call(
        body, name=name, grid=(t // tm,), in_specs=in_specs + [row, vec, row], out_specs=[row, row, vec],
        out_shape=[jax.ShapeDtypeStruct((t, d), F32), jax.ShapeDtypeStruct((t, d), BF16), jax.ShapeDtypeStruct((1, d), F32)],
        args=(*args, x, g, dres), sem=("arbitrary",), comm=comm)
    return (*outs, got) if comm is not None else tuple(outs)


def _mm_loss(a, b, res, target, name):
    t, kdim = a.shape
    d = b.shape[1]
    tm = _pick(t, (512, 256, 128))

    def body(a_ref, b_ref, r_ref, t_ref, dy_ref, dyb_ref, acc_ref):
        @pl.when(pl.program_id(0) == 0)
        def _():
            acc_ref[...] = jnp.zeros_like(acc_ref)

        err = _dot(a_ref[...], b_ref[...], 1, 0) + r_ref[...] - t_ref[...]
        dy = err * (1.0 / d)
        dy_ref[...] = dy
        dyb_ref[...] = dy.astype(BF16)
        acc_ref[...] += jnp.sum(err * err, axis=0, keepdims=True)

    row = pl.BlockSpec((tm, d), lambda i: (i, 0))
    vec = pl.BlockSpec((1, d), lambda i: (0, 0))
    return pl.pallas_call(
        body, name=name, grid=(t // tm,),
        in_specs=[pl.BlockSpec((tm, kdim), lambda i: (i, 0)), pl.BlockSpec((kdim, d), lambda i: (0, 0)), row, row],
        out_specs=[row, row, vec],
        out_shape=[jax.ShapeDtypeStruct((t, d), F32), jax.ShapeDtypeStruct((t, d), BF16), jax.ShapeDtypeStruct((1, d), F32)],
        compiler_params=_cp(("arbitrary",)),
    )(a, b, res, target)


def _head_sums(v):
    ri = lax.broadcasted_iota(jnp.int32, (LANES, LANES), 0) // HEAD_DIM
    ci = lax.broadcasted_iota(jnp.int32, (LANES, LANES), 1) // HEAD_DIM
    ones = (ri == ci).astype(BF16)
    hi = v.astype(BF16)
    lo_part = (v - hi.astype(F32)).astype(BF16)
    return _dot(hi, ones, 1, 0) + _dot(lo_part, ones, 1, 0)


def _head_rms(xs, w, lo):
    r = lax.rsqrt(_head_sums(xs * xs) * (1.0 / HEAD_DIM) + EPS)
    return xs * r, r


def _head_rms_bwd(xs, w, dy, lo):
    xh, r = _head_rms(xs, w, lo)
    dxh = dy * w
    mm = _head_sums(dxh * xh) * (1.0 / HEAD_DIM)
    return r * (dxh - xh * mm), dy * xh


_QSCALE = HEAD_DIM ** -0.5


def _headnorm_fwd(proj, ws, name):
    t = proj.shape[0]
    tm = TQ
    lead = PAD_A // tm
    leadb = PAD_B // tm

    def body(p_ref, w_ref, qa_ref, ka_ref, va_ref, qb_ref, kb_ref, vb_ref):
        data = pl.program_id(0) >= lead
        lo = _lo_mask()

        def put(ref, c, val):
            ref[:, c:c + val.shape[1]] = jnp.where(data, val, 0.0).astype(BF16)

        def per_query_head(slab):
            other = pltpu.roll(slab, HEAD_DIM, 1)
            e0, e1 = jnp.where(lo, slab, other), jnp.where(lo, other, slab)
            return jnp.concatenate([e0, e0, e1, e1], axis=1)

        for s in range(4):
            c = LANES * s
            xh, _ = _head_rms(p_ref[:, c:c + LANES], None, lo)
            qa_ref[:, c:c + LANES] = (xh * w_ref[0:1, :] * _QSCALE).astype(BF16)
            xh, _ = _head_rms(p_ref[:, 512 + c:512 + c + LANES], None, lo)
            put(ka_ref, c, xh * w_ref[1:2, :])
            xh, _ = _head_rms(p_ref[:, 1536 + c:1536 + c + LANES], None, lo)
            qb_ref[:, c:c + LANES] = (xh * w_ref[2:3, :] * _QSCALE).astype(BF16)
        put(va_ref, 0, p_ref[:, 1024:1536])
        xh, _ = _head_rms(p_ref[:, 2048:2176], None, lo)
        put(kb_ref, 0, per_query_head(xh * w_ref[3:4, :]))
        put(vb_ref, 0, per_query_head(p_ref[:, 2176:2304]))

    src = lambda i: jnp.maximum(i - lead, 0)
    wide = pl.BlockSpec((tm, 512), lambda i: (src(i), 0))
    pad_a = pl.BlockSpec((tm, 512), lambda i: (i, 0))
    pad_b = pl.BlockSpec((tm, 512), lambda i: (jnp.maximum(i - lead + leadb, 0), 0))
    sd = lambda rows: jax.ShapeDtypeStruct((rows, 512), BF16)
    return pl.pallas_call(
        body, name=name, grid=(t // tm + lead,),
        in_specs=[pl.BlockSpec((tm, 2304), lambda i: (src(i), 0)), pl.BlockSpec((4, LANES), lambda i: (0, 0))],
        out_specs=[wide, pad_a, pad_a, wide, pad_b, pad_b],
        out_shape=[sd(t), sd(t + PAD_A), sd(t + PAD_A), sd(t), sd(t + PAD_B), sd(t + PAD_B)],
        compiler_params=_cp(("arbitrary",)),
    )(proj, ws)


def _headnorm_bwd(proj, ws, dqa, dkpa, dvpa, dqb, dkpb, dvpb, name):
    t = proj.shape[0]
    tm = TQ
    offa, offb = PAD_A // tm, PAD_B // tm

    def body(p_ref, w_ref, dqa_ref, dka_ref, dva_ref, dqb_ref, dkb_ref, dvb_ref, dp_ref, dw_ref):
        i = pl.program_id(0)
        lo = _lo_mask()

        @pl.when(i == 0)
        def _():
            dw_ref[...] = jnp.zeros_like(dw_ref)

        acc = [jnp.zeros((1, LANES), F32) for _ in range(4)]
        for s in range(4):
            c = LANES * s
            dx, dwl = _head_rms_bwd(p_ref[:, c:c + LANES], w_ref[0:1, :], dqa_ref[:, c:c + LANES] * _QSCALE, lo)
            dp_ref[:, c:c + LANES] = dx.astype(BF16)
            acc[0] += jnp.sum(dwl, axis=0, keepdims=True)
            dx, dwl = _head_rms_bwd(p_ref[:, 512 + c:512 + c + LANES], w_ref[1:2, :], dka_ref[:, c:c + LANES], lo)
            dp_ref[:, 512 + c:512 + c + LANES] = dx.astype(BF16)
            acc[1] += jnp.sum(dwl, axis=0, keepdims=True)
            dx, dwl = _head_rms_bwd(p_ref[:, 1536 + c:1536 + c + LANES], w_ref[2:3, :], dqb_ref[:, c:c + LANES] * _QSCALE, lo)
            dp_ref[:, 1536 + c:1536 + c + LANES] = dx.astype(BF16)
            acc[2] += jnp.sum(dwl, axis=0, keepdims=True)
        dp_ref[:, 1024:1536] = dva_ref[...].astype(BF16)

        def group_sum(ref):
            s0 = ref[:, 0:128] + ref[:, 128:256]
            s1 = ref[:, 256:384] + ref[:, 384:512]
            s0 = s0 + pltpu.roll(s0, HEAD_DIM, 1)
            s1 = s1 + pltpu.roll(s1, HEAD_DIM, 1)
            return jnp.where(lo, s0, s1)

        dx, dwl = _head_rms_bwd(p_ref[:, 2048:2176], w_ref[3:4, :], group_sum(dkb_ref), lo)
        dp_ref[:, 2048:2176] = dx.astype(BF16)
        acc[3] += jnp.sum(dwl, axis=0, keepdims=True)
        dp_ref[:, 2176:2304] = group_sum(dvb_ref).astype(BF16)
        for n in range(4):
            dw_ref[n:n + 1, :] += acc[n]

    wide = pl.BlockSpec((tm, 512), lambda i: (i, 0))
    pa = pl.BlockSpec((tm, 512), lambda i: (i + offa, 0))
    pb = pl.BlockSpec((tm, 512), lambda i: (i + offb, 0))
    return pl.pallas_call(
        body, name=name, grid=(t // tm,),
        in_specs=[pl.BlockSpec((tm, 2304), lambda i: (i, 0)), pl.BlockSpec((4, LANES), lambda i: (0, 0)),
                  wide, pa, pa, wide, pb, pb],
        out_specs=[pl.BlockSpec((tm, 2304), lambda i: (i, 0)), pl.BlockSpec((4, LANES), lambda i: (0, 0))],
        out_shape=[jax.ShapeDtypeStruct((t, 2304), BF16), jax.ShapeDtypeStruct((4, LANES), F32)],
        compiler_params=_cp(("arbitrary",)),
    )(proj, ws, dqa, dkpa, dvpa, dqb, dkpb, dvpb)


ROLL_W = 1024


def _rel_onehot():
    r_io = lax.broadcasted_iota(jnp.int32, (REL_W, ROLL_W), 0)
    m_io = lax.broadcasted_iota(jnp.int32, (REL_W, ROLL_W), 1)
    return (r_io == jnp.clip(REL_W - 1 - m_io, -MAX_REL, MAX_REL) + MAX_REL).astype(F32)


def _relpos_fwd(table, name):
    def body(t_ref, o_ref):
        rr = _dot(t_ref[...], _rel_onehot(), 1, 0, HI)

        def step(q, c):
            o_ref[q] = pltpu.roll(rr, (ROLL_W - (TQ - 1) + q) % ROLL_W, 1)[:, :REL_W]
            return c

        lax.fori_loop(0, TQ, step, 0)

    return pl.pallas_call(
        body, name=name, out_shape=jax.ShapeDtypeStruct((TQ, N_HEADS, REL_W), F32),
        in_specs=[pl.BlockSpec(memory_space=pltpu.VMEM)], out_specs=pl.BlockSpec(memory_space=pltpu.VMEM),
        compiler_params=_cp(),
    )(table)


def _relpos_bwd(dbias_t, name):
    def body(d_ref, o_ref):
        def step(q, acc):
            row = jnp.concatenate([d_ref[q], jnp.zeros((N_HEADS, ROLL_W - REL_W), F32)], axis=1)
            return acc + pltpu.roll(row, TQ - 1 - q, 1)

        drr = lax.fori_loop(0, TQ, step, jnp.zeros((N_HEADS, ROLL_W), F32))
        o_ref[...] = _dot(drr, _rel_onehot(), 1, 1, HI)

    return pl.pallas_call(
        body, name=name, out_shape=jax.ShapeDtypeStruct((N_HEADS, REL_W), F32),
        in_specs=[pl.BlockSpec(memory_space=pltpu.VMEM)], out_specs=pl.BlockSpec(memory_space=pltpu.VMEM),
        compiler_params=_cp(),
    )(dbias_t)


def _attn_scores(qe, kw, bias, kvalid):
    return jnp.where(kvalid, _dot(qe, kw, 1, 1) + bias, NEG)


def _stat_cols(stats, e):
    return stats[:, 64 * e:64 * e + 1], stats[:, 64 * e + 32:64 * e + 33]


def _attn_fwd(q, kp, vp, bias, sinks, pad, name, comm=None):
    t, hd = q.shape
    w = pad + TQ

    def body(sink_ref, q_ref, k_ref, v_ref, b_ref, o_ref, st_ref):
        hp, i = pl.program_id(0), pl.program_id(1)
        lo = _lo_mask()
        lane = lax.broadcasted_iota(jnp.int32, (1, LANES), 1)
        for j in range(ATT_SUB):
            start = pl.multiple_of((i * ATT_SUB + j) * TQ, TQ)
            qv = q_ref[TQ * j:TQ * (j + 1), :]
            kw = k_ref[pl.ds(start, w), :]
            vw = v_ref[pl.ds(start, w), :]
            kvalid = (start + lax.broadcasted_iota(jnp.int32, (1, w), 1)) >= pad
            outs, ms, ls = [], [], []
            for e in range(2):
                sel = lo if e == 0 else jnp.logical_not(lo)
                qe = jnp.where(sel, qv, jnp.zeros_like(qv))
                snk = sink_ref[2 * hp + e]
                s = _attn_scores(qe, kw, b_ref[e], kvalid)
                m = jnp.maximum(jnp.max(s, axis=-1, keepdims=True), snk)
                acc = _dot(jnp.exp(s - m).astype(BF16), jnp.where(sel, vw, jnp.ones_like(vw)), 1, 0)
                denom = acc[:, 64 * (1 - e):64 * (1 - e) + 1] + jnp.exp(snk - m)
                outs.append(acc * (1.0 / denom))
                ms.append(m)
                ls.append(denom)
            o_ref[TQ * j:TQ * (j + 1), :] = jnp.where(lo, outs[0], outs[1]).astype(BF16)
            st_ref[TQ * j:TQ * (j + 1), :] = jnp.where(lane < 32, ms[0], jnp.where(lane < 64, ls[0],
                                                                                 jnp.where(lane < 96, ms[1], ls[1])))

    full = pl.BlockSpec((t + pad, LANES), lambda h, i: (0, h))
    tile = pl.BlockSpec((ATT_SUB * TQ, LANES), lambda h, i: (i, h))
    (o, stats), got = _call(
        body, name=name, grid=(hd // LANES, t // (ATT_SUB * TQ)),
        in_specs=[pl.BlockSpec(memory_space=pltpu.SMEM), tile, full, full, pl.BlockSpec((2, TQ, w), lambda h, i: (h, 0, 0))],
        out_specs=[tile, tile], out_shape=[jax.ShapeDtypeStruct((t, hd), BF16), jax.ShapeDtypeStruct((t, hd), F32)],
        args=(sinks, q, kp, vp, bias), sem=("parallel", "arbitrary"), comm=comm)
    return o, stats, got


def _attn_bwd(q, kp, vp, bias, sinks, do, stats, col_off, pad, name, comm=None):
    t, hd = q.shape
    w = pad + TQ
    nhp = hd // LANES

    def body(sink_ref, q_ref, k_ref, v_ref, b_ref, do_ref, st_ref, dq_ref, dk_ref, dv_ref, db_ref, ds_ref):
        hp, i = pl.program_id(0), pl.program_id(1)

        @pl.when(i == 0)
        def _():
            dk_ref[...] = jnp.zeros_like(dk_ref)
            dv_ref[...] = jnp.zeros_like(dv_ref)
            db_ref[...] = jnp.zeros_like(db_ref)
            ds_ref[...] = jnp.zeros_like(ds_ref)

        lo = _lo_mask()
        row8 = lax.broadcasted_iota(jnp.int32, (8, LANES), 0)
        dbias = [None, None]
        dsink = jnp.zeros((8, LANES), F32)
        for j in range(ATT_SUB):
            start = pl.multiple_of((i * ATT_SUB + j) * TQ, TQ)
            qv = q_ref[TQ * j:TQ * (j + 1), :]
            dov = do_ref[TQ * j:TQ * (j + 1), :]
            kw = k_ref[pl.ds(start, w), :]
            vw = v_ref[pl.ds(start, w), :]
            kvalid = (start + lax.broadcasted_iota(jnp.int32, (1, w), 1)) >= pad
            stats = st_ref[TQ * j:TQ * (j + 1), :]
            dqs, dkw, dvw = [], None, None
            for e in range(2):
                sel = lo if e == 0 else jnp.logical_not(lo)
                qe = jnp.where(sel, qv, jnp.zeros_like(qv))
                doe = jnp.where(sel, dov, jnp.zeros_like(dov))
                m, denom = _stat_cols(stats, e)
                inv = 1.0 / denom
                p = jnp.exp(_attn_scores(qe, kw, b_ref[e], kvalid) - m) * inv
                psink = jnp.exp(sink_ref[2 * hp + e] - m) * inv
                dp = _dot(doe, vw, 1, 1)
                delta = jnp.sum(p * dp, axis=-1, keepdims=True)
                ds = p * (dp - delta)
                dbias[e] = ds if dbias[e] is None else dbias[e] + ds
                dsink = dsink + jnp.where(row8 == e, jnp.sum(-psink * delta, axis=0, keepdims=True), 0.0)
                dsb = ds.astype(BF16)
                dqs.append(_dot(dsb, kw, 1, 0))
                dk_e = _dot(dsb, qe, 0, 0)
                dv_e = _dot(p.astype(BF16), doe, 0, 0)
                dkw = dk_e if dkw is None else dkw + dk_e
                dvw = dv_e if dvw is None else dvw + dv_e
            dq_ref[TQ * j:TQ * (j + 1), :] = jnp.where(lo, dqs[0], dqs[1])
            dk_ref[pl.ds(start, w), :] += dkw
            dv_ref[pl.ds(start, w), :] += dvw
        for e in range(2):
            db_ref[e] += dbias[e]
        ds_ref[0] += dsink

    full = pl.BlockSpec((t + pad, LANES), lambda h, i: (0, h))
    tile = pl.BlockSpec((ATT_SUB * TQ, LANES), lambda h, i: (i, h))
    btile = pl.BlockSpec((2, TQ, w), lambda h, i: (h, 0, 0))
    return _call(
        body, name=name, grid=(nhp, t // (ATT_SUB * TQ)),
        in_specs=[pl.BlockSpec(memory_space=pltpu.SMEM), tile, full, full, btile,
                  pl.BlockSpec((ATT_SUB * TQ, LANES), lambda h, i: (i, h + col_off)), tile],
        out_specs=[tile, full, full, btile, pl.BlockSpec((1, 8, LANES), lambda h, i: (h, 0, 0))],
        out_shape=[jax.ShapeDtypeStruct((t, hd), F32), jax.ShapeDtypeStruct((t + pad, hd), F32),
                   jax.ShapeDtypeStruct((t + pad, hd), F32), jax.ShapeDtypeStruct((N_HEADS, TQ, w), F32),
                   jax.ShapeDtypeStruct((nhp, 8, LANES), F32)],
        args=(sinks, q, kp, vp, bias, do, stats), sem=("parallel", "arbitrary"), comm=comm)


def _halo_prev(tm):
    return lambda i: jnp.maximum(i * (tm // 8) - 1, 0)


def _halo_next(tm, t):
    return lambda i: jnp.minimum((i + 1) * (tm // 8), t // 8 - 1)


def _taps_prev(tile, halo, ktaps, first):
    tm = tile.shape[0]
    ext = jnp.concatenate([jnp.where(first, 0.0, halo), tile], axis=0)
    return [tile] + [pltpu.roll(ext, s, 0)[8:8 + tm] for s in range(1, ktaps)]


def _conv_apply(taps, w_ref, ktaps):
    out = taps[0] * w_ref[ktaps - 1:ktaps, :]
    for s in range(1, ktaps):
        out = out + taps[s] * w_ref[ktaps - 1 - s:ktaps - s, :]
    return out


def _sigmoid(x):
    return jax.nn.sigmoid(x)


def _silu_grad(x):
    sg = _sigmoid(x)
    return x * sg, sg * (1.0 + x * (1.0 - sg))


FFN_TM = 128
FFN_HALO = 16


def _ffn_mid_fwd(gu, w8, b, name):
    t = gu.shape[0]
    f = D_FF
    tm, hr = FFN_TM, FFN_HALO

    def body(g_ref, u_ref, h_ref, w_ref, b_ref, a_ref):
        first = pl.program_id(0) == 0
        ext = jnp.concatenate([jnp.where(first, 0.0, h_ref[...].astype(F32)), g_ref[...].astype(F32)], axis=0)
        taps = [ext[hr:]] + [pltpu.roll(ext, s, 0)[hr:] for s in (1, 2)]
        gc = _conv_apply(taps, w_ref, 3) + b_ref[...]
        a_ref[...] = (gc * _sigmoid(gc) * u_ref[...].astype(F32)).astype(BF16)

    return pl.pallas_call(
        body, name=name, grid=(t // tm,),
        in_specs=[pl.BlockSpec((tm, f), lambda i: (i, 0)), pl.BlockSpec((tm, f), lambda i: (i, 1)),
                  pl.BlockSpec((hr, f), lambda i: (jnp.maximum(i * (tm // hr) - 1, 0), 0)),
                  pl.BlockSpec((8, f), lambda i: (0, 0)), pl.BlockSpec((1, f), lambda i: (0, 0))],
        out_specs=pl.BlockSpec((tm, f), lambda i: (i, 0)), out_shape=jax.ShapeDtypeStruct((t, f), BF16),
        compiler_params=_cp(("parallel",)),
    )(gu, gu, gu, w8, b)


FFN_BT = 256
FFN_BC = 1408


def _ffn_mid_bwd(gu, dxb, w_out, w8, b, name):
    t, d = dxb.shape
    f = D_FF
    tm, hr = FFN_BT, FFN_HALO
    nt = t // tm
    n = tm + hr

    def body(g_ref, u_ref, gp_ref, gn_ref, un_ref, dx_ref, dxn_ref, wo_ref, w_ref, b_ref, dgu_ref, dw_ref, db_ref):
        i = pl.program_id(0)
        first, last = i == 0, i == nt - 1

        @pl.when(first)
        def _():
            dw_ref[...] = jnp.zeros_like(dw_ref)
            db_ref[...] = jnp.zeros_like(db_ref)

        dxe = jnp.concatenate([dx_ref[...], dxn_ref[...]], axis=0)
        row = lax.broadcasted_iota(jnp.int32, (n, 1), 0)
        keep = (row < tm) | jnp.logical_not(last)
        for c in range(0, f, FFN_BC):
            cs = slice(c, c + FFN_BC)
            ext = jnp.concatenate([jnp.where(first, 0.0, gp_ref[:, cs].astype(F32)), g_ref[:, cs].astype(F32),
                                   gn_ref[:, cs].astype(F32)], axis=0)
            taps = [ext[hr:]] + [pltpu.roll(ext, s, 0)[hr:] for s in (1, 2)]
            gc = b_ref[:, cs] + taps[0] * w_ref[2:3, cs] + taps[1] * w_ref[1:2, cs] + taps[2] * w_ref[0:1, cs]
            act, dact = _silu_grad(gc)
            da = _dot(dxe, wo_ref[cs, :], 1, 1)
            up = jnp.concatenate([u_ref[:, cs], un_ref[:, cs]], axis=0).astype(F32)
            dgc = jnp.where(keep, da * up * dact, 0.0)
            dgu_ref[:, f + c:f + c + FFN_BC] = (da[:tm] * act[:tm]).astype(BF16)
            dgu_ref[:, cs] = (dgc[:tm] * w_ref[2:3, cs] + pltpu.roll(dgc, n - 1, 0)[:tm] * w_ref[1:2, cs]
                              + pltpu.roll(dgc, n - 2, 0)[:tm] * w_ref[0:1, cs]).astype(BF16)
            db_ref[:, cs] += jnp.sum(dgc[:tm], axis=0, keepdims=True)
            for s in range(3):
                dw_ref[2 - s:3 - s, cs] += jnp.sum(dgc[:tm] * taps[s][:tm], axis=0, keepdims=True)

    r = tm // hr
    prev = lambda i: jnp.maximum(i * r - 1, 0)
    nxt_blk = lambda i: jnp.minimum((i + 1) * r, t // hr - 1)
    row_f = pl.BlockSpec((tm, f), lambda i: (i, 0))
    return pl.pallas_call(
        body, name=name, grid=(nt,),
        in_specs=[row_f, pl.BlockSpec((tm, f), lambda i: (i, 1)),
                  pl.BlockSpec((hr, f), lambda i: (prev(i), 0)), pl.BlockSpec((hr, f), lambda i: (nxt_blk(i), 0)),
                  pl.BlockSpec((hr, f), lambda i: (nxt_blk(i), 1)),
                  pl.BlockSpec((tm, d), lambda i: (i, 0)), pl.BlockSpec((hr, d), lambda i: (nxt_blk(i), 0)),
                  pl.BlockSpec((f, d), lambda i: (0, 0)),
                  pl.BlockSpec((8, f), lambda i: (0, 0)), pl.BlockSpec((1, f), lambda i: (0, 0))],
        out_specs=[pl.BlockSpec((tm, 2 * f), lambda i: (i, 0)), pl.BlockSpec((8, f), lambda i: (0, 0)),
                   pl.BlockSpec((1, f), lambda i: (0, 0))],
        out_shape=[jax.ShapeDtypeStruct((t, 2 * f), BF16), jax.ShapeDtypeStruct((8, f), F32), jax.ShapeDtypeStruct((1, f), F32)],
        compiler_params=_cp(("arbitrary",)),
    )(gu, gu, gu, gu, gu, dxb, dxb, w_out, w8, b)


PRE_TM = 256
PRE_TC = 1024


def _ssm_pre_fwd(zx, w8, b, name):
    t = zx.shape[0]
    tm, tc = PRE_TM, PRE_TC
    off = D_INNER // tc

    def body(x_ref, h_ref, w_ref, b_ref, o_ref):
        first = pl.program_id(0) == 0
        c = _conv_apply(_taps_prev(x_ref[...], h_ref[...], 4, first), w_ref, 4) + b_ref[...]
        o_ref[...] = c * _sigmoid(c)

    hp = _halo_prev(tm)
    return pl.pallas_call(
        body, name=name, grid=(t // tm, XBC // tc),
        in_specs=[pl.BlockSpec((tm, tc), lambda i, j: (i, j + off)), pl.BlockSpec((8, tc), lambda i, j: (hp(i), j + off)),
                  pl.BlockSpec((8, tc), lambda i, j: (0, j)), pl.BlockSpec((1, tc), lambda i, j: (0, j))],
        out_specs=pl.BlockSpec((tm, tc), lambda i, j: (i, j)), out_shape=jax.ShapeDtypeStruct((t, XBC), F32),
        compiler_params=_cp(("parallel", "parallel")),
    )(zx, zx, w8, b)


def _ssm_pre_bwd(zx, dxbc, w8, b, name):
    t = zx.shape[0]
    tm, tc = PRE_TM, PRE_TC
    off = D_INNER // tc
    nt = t // tm
    n = tm + 8

    def body(x_ref, xp_ref, xn_ref, d_ref, dn_ref, w_ref, b_ref, o_ref, dw_ref, db_ref):
        i = pl.program_id(1)
        first, last = i == 0, i == nt - 1

        @pl.when(first)
        def _():
            dw_ref[...] = jnp.zeros_like(dw_ref)
            db_ref[...] = jnp.zeros_like(db_ref)

        ext = jnp.concatenate([jnp.where(first, 0.0, xp_ref[...]), x_ref[...], xn_ref[...]], axis=0)
        taps = [ext[8:8 + n]] + [pltpu.roll(ext, s, 0)[8:8 + n] for s in (1, 2, 3)]
        c = _conv_apply(taps, w_ref, 4) + b_ref[...]
        _, dact = _silu_grad(c)
        row = lax.broadcasted_iota(jnp.int32, (n, 1), 0)
        dc = jnp.where((row < tm) | jnp.logical_not(last), jnp.concatenate([d_ref[...], dn_ref[...]], axis=0) * dact, 0.0)
        nxt = [dc[:tm]] + [pltpu.roll(dc, n - s, 0)[:tm] for s in (1, 2, 3)]
        o_ref[...] = _conv_apply(nxt, w_ref, 4).astype(BF16)
        db_ref[...] += jnp.sum(dc[:tm], axis=0, keepdims=True)
        for s in range(4):
            dw_ref[3 - s:4 - s, :] += jnp.sum(dc[:tm] * taps[s][:tm], axis=0, keepdims=True)

    hp = _halo_prev(tm)
    hn = _halo_next(tm, t)
    return pl.pallas_call(
        body, name=name, grid=(XBC // tc, nt),
        in_specs=[pl.BlockSpec((tm, tc), lambda j, i: (i, j + off)), pl.BlockSpec((8, tc), lambda j, i: (hp(i), j + off)),
                  pl.BlockSpec((8, tc), lambda j, i: (hn(i), j + off)),
                  pl.BlockSpec((tm, tc), lambda j, i: (i, j)), pl.BlockSpec((8, tc), lambda j, i: (hn(i), j)),
                  pl.BlockSpec((8, tc), lambda j, i: (0, j)), pl.BlockSpec((1, tc), lambda j, i: (0, j))],
        out_specs=[pl.BlockSpec((tm, tc), lambda j, i: (i, j)), pl.BlockSpec((8, tc), lambda j, i: (0, j)),
                   pl.BlockSpec((1, tc), lambda j, i: (0, j))],
        out_shape=[jax.ShapeDtypeStruct((t, XBC), BF16), jax.ShapeDtypeStruct((8, XBC), F32),
                   jax.ShapeDtypeStruct((1, XBC), F32)],
        compiler_params=_cp(("parallel", "arbitrary")),
    )(zx, zx, zx, dxbc, dxbc, w8, b)


def _head_lanes():
    return lax.broadcasted_iota(jnp.int32, (1, LANES), 1) < SSM_HEADS


def _dt_fwd(dtraw, bias, name):
    t = dtraw.shape[0]
    tm = _pick(t, (1024, 512, 256, 128))

    def body(x_ref, b_ref, o_ref):
        v = x_ref[...] + b_ref[...]
        sp = jnp.maximum(v, 0.0) + jnp.log(1.0 + jnp.exp(-jnp.abs(v)))
        o_ref[...] = jnp.where(_head_lanes(), sp, 0.0)

    row = pl.BlockSpec((tm, LANES), lambda i: (i, 0))
    return pl.pallas_call(
        body, name=name, grid=(t // tm,), in_specs=[row, pl.BlockSpec((1, LANES), lambda i: (0, 0))], out_specs=row,
        out_shape=jax.ShapeDtypeStruct((t, LANES), F32), compiler_params=_cp(("parallel",)),
    )(dtraw, bias)


def _dt_bwd(dtraw, bias, ddt, name):
    t = dtraw.shape[0]
    tm = _pick(t, (1024, 512, 256, 128))

    def body(x_ref, b_ref, d_ref, o_ref, db_ref):
        @pl.when(pl.program_id(0) == 0)
        def _():
            db_ref[...] = jnp.zeros_like(db_ref)

        g = jnp.where(_head_lanes(), d_ref[...] * _sigmoid(x_ref[...] + b_ref[...]), 0.0)
        o_ref[...] = g.astype(BF16)
        db_ref[...] += jnp.sum(g, axis=0, keepdims=True)

    row = pl.BlockSpec((tm, LANES), lambda i: (i, 0))
    vec = pl.BlockSpec((1, LANES), lambda i: (0, 0))
    return pl.pallas_call(
        body, name=name, grid=(t // tm,), in_specs=[row, vec, row], out_specs=[row, vec],
        out_shape=[jax.ShapeDtypeStruct((t, LANES), BF16), jax.ShapeDtypeStruct((1, LANES), F32)],
        compiler_params=_cp(("arbitrary",)),
    )(dtraw, bias, ddt)


GROUP_W = D_INNER // SSM_GROUPS
POST_TM = 512


def _ssm_post_fwd(y, xbc, zx, dexp, nw, name):
    t = y.shape[0]
    tm = _pick(t, (POST_TM, 256, 128))

    def body(y_ref, x_ref, z_ref, d_ref, w_ref, o_ref):
        zv = z_ref[...]
        y3 = (y_ref[...] + d_ref[...] * x_ref[...]) * (zv * _sigmoid(zv))
        r = lax.rsqrt(jnp.mean(y3 * y3, axis=-1, keepdims=True) + EPS)
        o_ref[...] = (y3 * r * w_ref[...]).astype(BF16)

    blk = pl.BlockSpec((tm, GROUP_W), lambda i, g: (i, g))
    vec = pl.BlockSpec((1, GROUP_W), lambda i, g: (0, g))
    return pl.pallas_call(
        body, name=name, grid=(t // tm, SSM_GROUPS), in_specs=[blk, blk, blk, vec, vec], out_specs=blk,
        out_shape=jax.ShapeDtypeStruct((t, D_INNER), BF16), compiler_params=_cp(("parallel", "parallel")),
    )(y, xbc, zx, dexp, nw)


def _ssm_post_bwd(dy4, y, xbc, zx, dexp, nw, name):
    t = y.shape[0]
    tm = _pick(t, (POST_TM, 256, 128))

    def body(g_ref, y_ref, x_ref, z_ref, d_ref, w_ref, dy_ref, dxs_ref, dz_ref, dd_ref, dw_ref):
        @pl.when(pl.program_id(1) == 0)
        def _():
            dd_ref[...] = jnp.zeros_like(dd_ref)
            dw_ref[...] = jnp.zeros_like(dw_ref)

        zv = z_ref[...]
        xv = x_ref[...]
        act, dact = _silu_grad(zv)
        y2 = y_ref[...] + d_ref[...] * xv
        y3 = y2 * act
        r = lax.rsqrt(jnp.mean(y3 * y3, axis=-1, keepdims=True) + EPS)
        y3n = y3 * r
        gv = g_ref[...]
        dyn = gv * w_ref[...]
        dy3 = r * (dyn - y3n * jnp.mean(dyn * y3n, axis=-1, keepdims=True))
        dy2 = dy3 * act
        dy_ref[...] = dy2
        dxs_ref[...] = dy2 * d_ref[...]
        dz_ref[...] = (dy3 * y2 * dact).astype(BF16)
        dd_ref[...] += jnp.sum(dy2 * xv, axis=0, keepdims=True)
        dw_ref[...] += jnp.sum(gv * y3n, axis=0, keepdims=True)

    blk = pl.BlockSpec((tm, GROUP_W), lambda g, i: (i, g))
    vec = pl.BlockSpec((1, GROUP_W), lambda g, i: (0, g))
    return pl.pallas_call(
        body, name=name, grid=(SSM_GROUPS, t // tm), in_specs=[blk, blk, blk, blk, vec, vec],
        out_specs=[blk, blk, blk, vec, vec],
        out_shape=[jax.ShapeDtypeStruct((t, D_INNER), F32), jax.ShapeDtypeStruct((t, D_INNER), F32),
                   jax.ShapeDtypeStruct((t, D_INNER), BF16), jax.ShapeDtypeStruct((1, D_INNER), F32),
                   jax.ShapeDtypeStruct((1, D_INNER), F32)],
        compiler_params=_cp(("parallel", "arbitrary")),
    )(dy4, y, xbc, zx, dexp, nw)


def _ssd_common(dt, alog):
    ll = dt.shape[0]
    a_neg = -jnp.exp(alog)
    a = dt * a_neg
    ri = lax.broadcasted_iota(jnp.int32, (ll, ll), 0)
    ci = lax.broadcasted_iota(jnp.int32, (ll, ll), 1)
    tril = ri >= ci
    acs = _dot(tril.astype(F32), a, 1, 0, HI)
    return a_neg, tril, acs, acs.T


def _pair_terms(acs, acs_t, dt, h0, lo):
    ll = acs.shape[0]
    cols = [acs[:, h0 + e:h0 + e + 1] for e in range(2)]
    rows = [acs_t[h0 + e:h0 + e + 1, :] for e in range(2)]
    dtc = [dt[:, h0 + e:h0 + e + 1] for e in range(2)]
    lasts = [c[ll - 1:ll, :] for c in cols]
    dtx = jnp.where(lo, dtc[0], dtc[1])
    eac = jnp.where(lo, jnp.exp(cols[0]), jnp.exp(cols[1]))
    fdec = jnp.where(lo, jnp.exp(lasts[0] - cols[0]), jnp.exp(lasts[1] - cols[1]))
    elast = jnp.where(lo, jnp.exp(lasts[0]), jnp.exp(lasts[1]))
    return cols, rows, dtx, eac, fdec, elast


def _decay(col, row, tril):
    return jnp.where(tril, jnp.exp(jnp.minimum(col - row, 0.0)), 0.0)


def _two_heads_rows(v, lo):
    z = jnp.zeros_like(v)
    return jnp.concatenate([jnp.where(lo, v, z), jnp.where(lo, z, v)], axis=0)


def _two_heads_cols(ms):
    return jnp.concatenate(ms, axis=1)


def _ssd_fwd(xbc, dt, alog, name, comm=None):
    t = xbc.shape[0]
    ll = SSD_L
    nc = t // ll

    def body(x_ref, dt_ref, al_ref, y_ref, sp_ref, st_ref):
        @pl.when(pl.program_id(0) == 0)
        def _():
            st_ref[...] = jnp.zeros_like(st_ref)

        dtv = dt_ref[...]
        _, tril, acs, acs_t = _ssd_common(dtv, al_ref[...])
        lo = _lo_mask()
        sp_ref[0] = st_ref[...]
        for g in range(SSM_GROUPS):
            bg = x_ref[:, D_INNER + SSM_STATE * g:D_INNER + SSM_STATE * (g + 1)].astype(BF16)
            cg = x_ref[:, D_INNER + 512 + SSM_STATE * g:D_INNER + 512 + SSM_STATE * (g + 1)].astype(BF16)
            gm = _dot(cg, bg, 1, 1)
            g0 = GROUP_W * g
            terms = [_pair_terms(acs, acs_t, dtv, 8 * g + 2 * pp, lo) for pp in range(4)]
            dtx, eac, fdec, elast = [jnp.concatenate([tt[k] for tt in terms], axis=1) for k in (2, 3, 4, 5)]
            xg = x_ref[:, g0:g0 + GROUP_W]
            ug = (xg * dtx).astype(BF16)
            sg = st_ref[:, g0:g0 + GROUP_W]
            yst = _dot(cg, sg.astype(BF16), 1, 0) * eac
            st_ref[:, g0:g0 + GROUP_W] = sg * elast + _dot(bg, (xg * (fdec * dtx)).astype(BF16), 0, 0)
            for pp in range(4):
                cols, rows = terms[pp][0], terms[pp][1]
                sl = slice(LANES * pp, LANES * (pp + 1))
                y_in = _dot(_two_heads_cols([(gm * _decay(cols[e], rows[e], tril)).astype(BF16) for e in range(2)]),
                            _two_heads_rows(ug[:, sl], lo), 1, 0)
                y_ref[:, g0 + LANES * pp:g0 + LANES * (pp + 1)] = y_in + yst[:, sl]

    return _call(
        body, name=name, grid=(nc,),
        in_specs=[pl.BlockSpec((ll, XBC), lambda c: (c, 0)), pl.BlockSpec((ll, LANES), lambda c: (c, 0)),
                  pl.BlockSpec((1, LANES), lambda c: (0, 0))],
        out_specs=[pl.BlockSpec((ll, D_INNER), lambda c: (c, 0)), pl.BlockSpec((1, SSM_STATE, D_INNER), lambda c: (c, 0, 0))],
        out_shape=[jax.ShapeDtypeStruct((t, D_INNER), F32), jax.ShapeDtypeStruct((nc, SSM_STATE, D_INNER), F32)],
        scratch_shapes=[pltpu.VMEM((SSM_STATE, D_INNER), F32)],
        args=(xbc, dt, alog), sem=("arbitrary",), comm=comm)


def _ssd_bwd(xbc, dt, alog, sprev, dy, dskip, name, comm=None):
    t = xbc.shape[0]
    ll = SSD_L
    nc = t // ll

    def body(x_ref, dt_ref, al_ref, sp_ref, dy_ref, dk_ref, dx_ref, ddt_ref, dal_ref, ds_ref, colt_ref):
        @pl.when(pl.program_id(0) == 0)
        def _():
            ds_ref[...] = jnp.zeros_like(ds_ref)
            dal_ref[...] = jnp.zeros_like(dal_ref)

        dtv = dt_ref[...]
        a_neg, tril, acs, acs_t = _ssd_common(dtv, al_ref[...])
        lo = _lo_mask()
        hi = jnp.logical_not(lo)
        lane = lax.broadcasted_iota(jnp.int32, (1, LANES), 1)
        colt_ref[...] = jnp.zeros_like(colt_ref)
        rowterm = jnp.zeros((ll, LANES), F32)
        ddt_u = jnp.zeros((ll, LANES), F32)
        dlast = jnp.zeros((1, LANES), F32)

        def halves(v):
            return (jnp.sum(jnp.where(lo, v, 0.0), axis=-1, keepdims=True),
                    jnp.sum(jnp.where(hi, v, 0.0), axis=-1, keepdims=True))

        for g in range(SSM_GROUPS):
            cb0 = D_INNER + SSM_STATE * g
            cc0 = D_INNER + 512 + SSM_STATE * g
            bg = x_ref[:, cb0:cb0 + SSM_STATE].astype(BF16)
            cg = x_ref[:, cc0:cc0 + SSM_STATE].astype(BF16)
            gm = _dot(cg, bg, 1, 1)
            g0 = GROUP_W * g
            terms = [_pair_terms(acs, acs_t, dtv, 8 * g + 2 * pp, lo) for pp in range(4)]
            dtx, eac, fdec, elast = [jnp.concatenate([tt[k] for tt in terms], axis=1) for k in (2, 3, 4, 5)]
            xg = x_ref[:, g0:g0 + GROUP_W]
            u32 = xg * dtx
            ug = u32.astype(BF16)
            dyg = dy_ref[:, g0:g0 + GROUP_W]
            dyb = dyg.astype(BF16)
            spg = sp_ref[0, :, g0:g0 + GROUP_W]
            spb = spg.astype(BF16)
            dsg = ds_ref[:, g0:g0 + GROUP_W]
            dsb = dsg.astype(BF16)
            du_st = _dot(bg, dsb, 1, 0) * fdec
            yst = _dot(cg, spb, 1, 0) * eac
            dye = (dyg * eac).astype(BF16)
            dc_st = _dot(dye, spb, 1, 1)
            db_st = _dot((xg * (fdec * dtx)).astype(BF16), dsb, 1, 1)
            ds_ref[:, g0:g0 + GROUP_W] = dsg * elast + _dot(cg, dye, 0, 0)
            qst_el = du_st * u32
            rq_el = dyg * yst - qst_el
            q_row = jnp.sum(qst_el, axis=0, keepdims=True)
            s_row = jnp.sum(dsg * spg, axis=0, keepdims=True)
            dgm = jnp.zeros((ll, ll), F32)
            for pp in range(4):
                h0 = 8 * g + 2 * pp
                cols, rows = terms[pp][0], terms[pp][1]
                sl = slice(LANES * pp, LANES * (pp + 1))
                decs = [_decay(cols[e], rows[e], tril) for e in range(2)]
                wms = [gm * d for d in decs]
                dum2 = _dot(dyb[:, sl], _two_heads_rows(ug[:, sl], lo), 1, 1)
                du = _dot(jnp.concatenate([wm.astype(BF16) for wm in wms], axis=0),
                          _two_heads_rows(dyb[:, sl], lo), 0, 0) + du_st[:, sl]
                dx_ref[:, g0 + LANES * pp:g0 + LANES * (pp + 1)] = du * dtx[:, sl] + dk_ref[:, g0 + LANES * pp:g0 + LANES * (pp + 1)]
                ddtu = halves(du * xg[:, sl])
                rq = halves(rq_el[:, sl])
                qs = halves(q_row[:, sl])
                ss = halves(s_row[:, sl])
                for e in range(2):
                    dum = dum2[:, ll * e:ll * (e + 1)]
                    dgm = dgm + dum * decs[e]
                    tm_ = dum * wms[e]
                    oh = lane == (h0 + e)
                    rowterm = rowterm + jnp.where(oh, jnp.sum(tm_, axis=1, keepdims=True) + rq[e], 0.0)
                    ddt_u = ddt_u + jnp.where(oh, ddtu[e], 0.0)
                    dlast = dlast + jnp.where(oh, jnp.exp(cols[e][ll - 1:ll, :]) * ss[e] + qs[e], 0.0)
                    colt_ref[h0 + e:h0 + e + 1, :] = jnp.sum(tm_, axis=0, keepdims=True)
            dgb = dgm.astype(BF16)
            dx_ref[:, cc0:cc0 + SSM_STATE] = _dot(dgb, bg, 1, 0) + dc_st
            dx_ref[:, cb0:cb0 + SSM_STATE] = _dot(dgb, cg, 0, 0) + db_st
        row_io = lax.broadcasted_iota(jnp.int32, (ll, LANES), 0)
        dacs = rowterm - colt_ref[...].T + jnp.where(row_io == ll - 1, dlast, 0.0)
        da = _dot(jnp.logical_not(tril).astype(F32) + jnp.where(
            lax.broadcasted_iota(jnp.int32, (ll, ll), 0) == lax.broadcasted_iota(jnp.int32, (ll, ll), 1), 1.0, 0.0),
            dacs, 1, 0, HI)
        ddt_ref[...] = da * a_neg + ddt_u
        dal_ref[...] += jnp.sum(da * dtv, axis=0, keepdims=True) * a_neg

    rev = lambda c: nc - 1 - c
    return _call(
        body, name=name, grid=(nc,),
        in_specs=[pl.BlockSpec((ll, XBC), lambda c: (rev(c), 0)), pl.BlockSpec((ll, LANES), lambda c: (rev(c), 0)),
                  pl.BlockSpec((1, LANES), lambda c: (0, 0)),
                  pl.BlockSpec((1, SSM_STATE, D_INNER), lambda c: (rev(c), 0, 0)),
                  pl.BlockSpec((ll, D_INNER), lambda c: (rev(c), 0)), pl.BlockSpec((ll, D_INNER), lambda c: (rev(c), 0))],
        out_specs=[pl.BlockSpec((ll, XBC), lambda c: (rev(c), 0)), pl.BlockSpec((ll, LANES), lambda c: (rev(c), 0)),
                   pl.BlockSpec((1, LANES), lambda c: (0, 0))],
        out_shape=[jax.ShapeDtypeStruct((t, XBC), F32), jax.ShapeDtypeStruct((t, LANES), F32),
                   jax.ShapeDtypeStruct((1, LANES), F32)],
        scratch_shapes=[pltpu.VMEM((SSM_STATE, D_INNER), F32), pltpu.VMEM((LANES, ll), F32)],
        args=(xbc, dt, alog, sprev, dy, dskip), sem=("arbitrary",), comm=comm)


ADAM_TR = 512


def _sum_parts(parts, name):
    nparts, r, c = parts.shape
    tc = _pick(c, (256, 128))

    def body(p_ref, o_ref):
        g = p_ref[0].astype(F32)
        for k in range(1, nparts):
            g = g + p_ref[k].astype(F32)
        o_ref[...] = g

    return pl.pallas_call(
        body, name=name, grid=(c // tc,), in_specs=[pl.BlockSpec((nparts, r, tc), lambda j: (0, 0, j))],
        out_specs=pl.BlockSpec((r, tc), lambda j: (0, j)), out_shape=jax.ShapeDtypeStruct((r, c), F32),
        compiler_params=_cp(("parallel",)),
    )(parts)


def _adamw(parts, w, m, v, name):
    nl, r, c = w.shape
    assert len(parts) == nl
    tr = _pick(r, (256, 128, 64))
    c1 = 1.0 - ADAM_B1 ** ADAM_STEP
    c2 = 1.0 - ADAM_B2 ** ADAM_STEP

    def body(*refs):
        p_refs = refs[:nl]
        w_ref, m_ref, v_ref, g_ref, d_ref, mo_ref, vo_ref = refs[nl:]
        g = None
        for l, p_ref in enumerate(p_refs):
            s = p_ref[0].astype(F32)
            for k in range(1, p_ref.shape[0]):
                s = s + p_ref[k].astype(F32)
            g = s if g is None else jnp.where(pl.program_id(0) == l, s, g)
        mn = ADAM_B1 * m_ref[0] + (1.0 - ADAM_B1) * g
        vn = ADAM_B2 * v_ref[0] + (1.0 - ADAM_B2) * (g * g)
        g_ref[0] = g
        mo_ref[0] = mn
        vo_ref[0] = vn
        d_ref[0] = -ADAM_LR * ((mn / c1) / (jnp.sqrt(vn / c2) + ADAM_EPS) + ADAM_WD * w_ref[0])

    row = pl.BlockSpec((1, tr, c), lambda l, i: (l, i, 0))
    sd = jax.ShapeDtypeStruct((nl, r, c), F32)
    return pl.pallas_call(
        body, name=name, grid=(nl, r // tr),
        in_specs=[pl.BlockSpec((p.shape[0], tr, c), lambda l, i: (0, i, 0)) for p in parts] + [row, row, row],
        out_specs=[row, row, row, row], out_shape=[sd, sd, sd, sd], compiler_params=_cp(("parallel", "parallel")),
    )(*parts, w, m, v)


def _peers():
    mx, my, mc = lax.axis_index("x"), lax.axis_index("y"), lax.axis_index("c")
    me = 4 * mx + 2 * my + mc
    out = []
    for k in range(1, N_DEV):
        px = 1 - mx if k & 4 else mx
        py = 1 - my if k & 2 else my
        pc = 1 - mc if k & 1 else mc
        out.append(((px, py, pc), 4 * px + 2 * py + pc))
    return me, out


class _Comm:
    def __init__(self, arrs, scatters):
        self.arrs, self.scatters, self.n = list(arrs), list(scatters), len(arrs)
        self.specs = [pl.BlockSpec(memory_space=pl.ANY)] * self.n
        self.out_shape = [jax.ShapeDtypeStruct(x.shape if sc else (N_DEV,) + x.shape, x.dtype)
                          for x, sc in zip(self.arrs, self.scatters)]
        np_ = N_DEV - 1
        self.scratch = [pltpu.SemaphoreType.DMA((np_ * self.n,)), pltpu.SemaphoreType.DMA((np_ * self.n,)),
                        pltpu.SemaphoreType.DMA((self.n,))]

    def _copies(self, x_refs, o_refs, sems):
        send_sems, recv_sems, local_sems = sems
        me, peers = _peers()
        np_ = N_DEV - 1
        local, sends, recvs = [], [], []
        for a in range(self.n):
            mine = x_refs[a].at[me] if self.scatters[a] else x_refs[a]
            local.append(pltpu.make_async_copy(mine, o_refs[a].at[me], local_sems.at[a]))
        for k, (dev, idx) in enumerate(peers):
            for a in range(self.n):
                mine = x_refs[a].at[me] if self.scatters[a] else x_refs[a]
                sends.append(pltpu.make_async_remote_copy(
                    src_ref=x_refs[a].at[idx] if self.scatters[a] else x_refs[a], dst_ref=o_refs[a].at[me],
                    send_sem=send_sems.at[a * np_ + k], recv_sem=recv_sems.at[a * np_ + k], device_id=dev, device_id_type=MESH))
                recvs.append(pltpu.make_async_remote_copy(
                    src_ref=mine, dst_ref=o_refs[a].at[idx], send_sem=send_sems.at[a * np_ + k],
                    recv_sem=recv_sems.at[a * np_ + k], device_id=dev, device_id_type=MESH))
        return local, sends, recvs

    def start(self, x_refs, o_refs, sems):
        local, sends, _ = self._copies(x_refs, o_refs, sems)
        for cp in local + sends:
            cp.start()

    def wait(self, x_refs, o_refs, sems):
        local, sends, recvs = self._copies(x_refs, o_refs, sems)
        for cp in recvs:
            cp.wait_recv()
        for cp in sends:
            cp.wait_send()
        for cp in local:
            cp.wait()


class _Gather2(_Comm):
    def __init__(self, arrs):
        super().__init__(arrs, [False] * len(arrs))

    def _plan(self, x_refs, o_refs, sems):
        send_sems, recv_sems, local_sems = sems
        mx, my, mc = lax.axis_index("x"), lax.axis_index("y"), lax.axis_index("c")
        slot = lambda px, py, pc: 4 * px + 2 * py + pc
        sib = (mx, my, 1 - mc)
        chips = [(1 - mx, my), (mx, 1 - my), (1 - mx, 1 - my)]
        np_ = N_DEV - 1
        local, first, passed, arrive_first, arrive_rest = [], [], [], [], []

        def copy(a, k, src, block, to):
            return pltpu.make_async_remote_copy(
                src_ref=src, dst_ref=o_refs[a].at[block], send_sem=send_sems.at[a * np_ + k], recv_sem=recv_sems.at[a * np_ + k],
                device_id=to, device_id_type=MESH)

        for a in range(self.n):
            me = slot(mx, my, mc)
            local.append(pltpu.make_async_copy(x_refs[a], o_refs[a].at[me], local_sems.at[a]))
            first.append(copy(a, 0, x_refs[a], me, sib))
            arrive_rest.append(copy(a, 0, x_refs[a], slot(*sib), sib))
            for j, (cx, cy) in enumerate(chips):
                first.append(copy(a, 1 + j, x_refs[a], me, (cx, cy, mc)))
                arrive_first.append(copy(a, 1 + j, x_refs[a], slot(cx, cy, mc), (cx, cy, mc)))
                passed.append(copy(a, 4 + j, o_refs[a].at[slot(cx, cy, mc)], slot(cx, cy, mc), sib))
                arrive_rest.append(copy(a, 4 + j, x_refs[a], slot(cx, cy, 1 - mc), sib))
        return local, first, passed, arrive_first, arrive_rest

    def start(self, x_refs, o_refs, sems):
        local, first, _, _, _ = self._plan(x_refs, o_refs, sems)
        for cp in local + first:
            cp.start()

    def wait(self, x_refs, o_refs, sems):
        local, first, passed, arrive_first, arrive_rest = self._plan(x_refs, o_refs, sems)
        for arrived, onward in zip(arrive_first, passed):
            arrived.wait_recv()
            onward.start()
        for cp in arrive_rest:
            cp.wait_recv()
        for cp in first + passed:
            cp.wait_send()
        for cp in local:
            cp.wait()


def _call(body, *, name, grid, in_specs, out_specs, out_shape, args, scratch_shapes=(), sem=None, comm=None):
    if comm is None:
        outs = pl.pallas_call(
            body, name=name, grid=grid, in_specs=list(in_specs), out_specs=list(out_specs), out_shape=list(out_shape),
            scratch_shapes=list(scratch_shapes), compiler_params=_cp(sem),
        )(*args)
        return list(outs), []
    n_in, n_out, nc = len(in_specs), len(out_specs), comm.n
    nsteps = 1
    for g in grid:
        nsteps *= g

    def carrier(*refs):
        ins, cin = refs[:n_in], refs[n_in:n_in + nc]
        outs, cout = refs[n_in + nc:n_in + nc + n_out], refs[n_in + nc + n_out:n_in + 2 * nc + n_out]
        rest = refs[n_in + 2 * nc + n_out:]
        scratch, sems = rest[:len(rest) - 3], rest[len(rest) - 3:]
        if nsteps == 1:
            comm.start(cin, cout, sems)
            body(*ins, *outs, *scratch)
            comm.wait(cin, cout, sems)
            return
        step = 0
        for d, g in enumerate(grid):
            step = step * g + pl.program_id(d)

        @pl.when(step == 0)
        def _():
            comm.start(cin, cout, sems)

        body(*ins, *outs, *scratch)

        @pl.when(step == nsteps - 1)
        def _():
            comm.wait(cin, cout, sems)

    outs = pl.pallas_call(
        carrier, name=name, grid=grid, in_specs=list(in_specs) + comm.specs, out_specs=list(out_specs) + comm.specs,
        out_shape=list(out_shape) + comm.out_shape, scratch_shapes=list(scratch_shapes) + comm.scratch,
        compiler_params=_cp(("arbitrary",) * len(grid) if grid else None),
    )(*args, *comm.arrs)
    return list(outs[:n_out]), list(outs[n_out:])


def _exchange(comm, name):
    return _call(lambda *refs: None, name=name, grid=(), in_specs=[], out_specs=[], out_shape=[], args=[], comm=comm)[1]


def _pack(arrs, dtype, lead=()):
    nl = len(lead)
    flat = jnp.concatenate([a.astype(dtype).reshape(lead + (-1,)) for a in arrs], axis=nl)
    n = flat.shape[-1]
    rows = -(-n // (LANES * ADAM_TR)) * ADAM_TR
    flat = jnp.pad(flat, [(0, 0)] * nl + [(0, rows * LANES - n)])
    return flat.reshape(lead + (rows, LANES))


def _unpack(flat, shapes, lead=()):
    nl = len(lead)
    flat = flat.reshape(lead + (-1,))
    out, o = [], 0
    for s in shapes:
        n = 1
        for d in s:
            n *= d
        out.append(lax.slice_in_dim(flat, o, o + n, axis=nl).reshape(lead + tuple(s)))
        o += n
    return out


def _join(g, ax):
    return jnp.concatenate([g[d] for d in range(N_DEV)], axis=ax)


def _split(full, ax):
    n = full.shape[ax] // N_DEV
    return jnp.stack([lax.slice_in_dim(full, d * n, (d + 1) * n, axis=ax) for d in range(N_DEV)])


_WEIGHTS = ['norm_mix', 'norm_ffn', 'attn_w_in', 'attn_w_out', 'relpos_table', 'q_norm_a', 'k_norm_a', 'q_norm_b',
            'k_norm_b', 'sinks', 'ssm_w_in', 'ssm_conv_w', 'ssm_conv_b', 'ssm_dt_bias', 'ssm_a_log', 'ssm_d', 'ssm_norm',
            'ssm_w_out', 'ffn_w_in', 'ffn_conv_w', 'ffn_conv_b', 'ffn_w_out']
_SHARD_AX = {'attn_w_in': 2, 'attn_w_out': 1, 'ssm_w_in': 2, 'ssm_conv_w': 2, 'ssm_conv_b': 1, 'ssm_norm': 1,
             'ssm_w_out': 1, 'ffn_w_in': 2, 'ffn_conv_w': 2, 'ffn_w_out': 1}
_BIG = ['attn_w_in', 'attn_w_out', 'ssm_w_in', 'ssm_w_out', 'ffn_w_in', 'ffn_w_out']
_SMALL = ['ssm_conv_w', 'ssm_conv_b', 'ssm_norm', 'ffn_conv_w']
_AX2 = {n: _SHARD_AX[n] - 1 for n in _BIG}
_REPL = [n for n in _WEIGHTS if n not in _SHARD_AX]


def _rows8(w):
    return jnp.pad(w, ((0, 8 - w.shape[0]), (0, 0)))


def _lanes128(v):
    return jnp.pad(v, (0, LANES - v.shape[0])).reshape(1, LANES)


def _band_mask(n_prev, pad):
    cq = jnp.arange(TQ)[:, None] // CHUNK
    ck = jnp.arange(pad + TQ)[None, :] // CHUNK
    return (ck >= cq) & (ck <= cq + n_prev)


def _ffn_fwd(xin, g, w_in_t, w8, cb, tag):
    gu, h = _rms_mm(xin, g, w_in_t, 2 * D_FF, f"mm_ffn_in{tag}", BF16)
    a = _ffn_mid_fwd(gu, w8, cb, f"ffn_mid{tag}")
    return a, (h, gu, a)


def _ffn_bwd(dx, dxb, xin, g, w_in_t, w8, cb, w_out, saved, tag):
    h, gu, a = saved
    dw_out = _mm_tn(a, dxb, f"mm_ffn_dwout{tag}")
    dgu, dw8, dcb = _ffn_mid_bwd(gu, dxb, w_out, w8, cb, f"ffn_mid_bwd{tag}")
    dw_in_t = _mm_tn(dgu, h, f"mm_ffn_dwin{tag}")
    dxp, dxpb, dg = _mm_rms_bwd(dgu, w_in_t, 0, None, xin, g, dx, f"mm_ffn_dh{tag}")
    return dxp, dxpb, dg, dw_in_t, dw8[:3], dcb, dw_out


def kernel(x, norm_mix, norm_ffn, attn_w_in, attn_w_out, relpos_table, q_norm_a, k_norm_a, q_norm_b, k_norm_b, sinks, ssm_w_in, ssm_conv_w, ssm_conv_b, ssm_dt_bias, ssm_a_log, ssm_d, ssm_norm, ssm_w_out, ffn_w_in, ffn_conv_w, ffn_conv_b, ffn_w_out, loss_target, m_norm_mix, m_norm_ffn, m_attn_w_in, m_attn_w_out, m_relpos_table, m_q_norm_a, m_k_norm_a, m_q_norm_b, m_k_norm_b, m_sinks, m_ssm_w_in, m_ssm_conv_w, m_ssm_conv_b, m_ssm_dt_bias, m_ssm_a_log, m_ssm_d, m_ssm_norm, m_ssm_w_out, m_ffn_w_in, m_ffn_conv_w, m_ffn_conv_b, m_ffn_w_out, v_norm_mix, v_norm_ffn, v_attn_w_in, v_attn_w_out, v_relpos_table, v_q_norm_a, v_k_norm_a, v_q_norm_b, v_k_norm_b, v_sinks, v_ssm_w_in, v_ssm_conv_w, v_ssm_conv_b, v_ssm_dt_bias, v_ssm_a_log, v_ssm_d, v_ssm_norm, v_ssm_w_out, v_ffn_w_in, v_ffn_conv_w, v_ffn_conv_b, v_ffn_w_out):
    w = dict(norm_mix=norm_mix, norm_ffn=norm_ffn, attn_w_in=attn_w_in, attn_w_out=attn_w_out, relpos_table=relpos_table,
             q_norm_a=q_norm_a, k_norm_a=k_norm_a, q_norm_b=q_norm_b, k_norm_b=k_norm_b, sinks=sinks, ssm_w_in=ssm_w_in,
             ssm_conv_w=ssm_conv_w, ssm_conv_b=ssm_conv_b, ssm_dt_bias=ssm_dt_bias, ssm_a_log=ssm_a_log, ssm_d=ssm_d,
             ssm_norm=ssm_norm, ssm_w_out=ssm_w_out, ffn_w_in=ffn_w_in, ffn_conv_w=ffn_conv_w, ffn_conv_b=ffn_conv_b,
             ffn_w_out=ffn_w_out)
    mom = dict(norm_mix=m_norm_mix, norm_ffn=m_norm_ffn, attn_w_in=m_attn_w_in, attn_w_out=m_attn_w_out,
               relpos_table=m_relpos_table, q_norm_a=m_q_norm_a, k_norm_a=m_k_norm_a, q_norm_b=m_q_norm_b,
               k_norm_b=m_k_norm_b, sinks=m_sinks, ssm_w_in=m_ssm_w_in, ssm_conv_w=m_ssm_conv_w, ssm_conv_b=m_ssm_conv_b,
               ssm_dt_bias=m_ssm_dt_bias, ssm_a_log=m_ssm_a_log, ssm_d=m_ssm_d, ssm_norm=m_ssm_norm, ssm_w_out=m_ssm_w_out,
               ffn_w_in=m_ffn_w_in, ffn_conv_w=m_ffn_conv_w, ffn_conv_b=m_ffn_conv_b, ffn_w_out=m_ffn_w_out)
    var = dict(norm_mix=v_norm_mix, norm_ffn=v_norm_ffn, attn_w_in=v_attn_w_in, attn_w_out=v_attn_w_out,
               relpos_table=v_relpos_table, q_norm_a=v_q_norm_a, k_norm_a=v_k_norm_a, q_norm_b=v_q_norm_b,
               k_norm_b=v_k_norm_b, sinks=v_sinks, ssm_w_in=v_ssm_w_in, ssm_conv_w=v_ssm_conv_w, ssm_conv_b=v_ssm_conv_b,
               ssm_dt_bias=v_ssm_dt_bias, ssm_a_log=v_ssm_a_log, ssm_d=v_ssm_d, ssm_norm=v_ssm_norm, ssm_w_out=v_ssm_w_out,
               ffn_w_in=v_ffn_w_in, ffn_conv_w=v_ffn_conv_w, ffn_conv_b=v_ffn_conv_b, ffn_w_out=v_ffn_w_out)

    def piece(n, l):
        return (w[n][l].T if _AX2[n] == 1 else w[n][l]).astype(BF16)

    def gather_of(names_layers):
        return _Gather2([piece(n, l) for n, l in names_layers])

    def joined(got):
        return [g.reshape(-1, D_MODEL) for g in got]

    first = [('attn_w_in', 0), ('attn_w_out', 0)]
    got = _exchange(_Gather2([piece(n, l) for n, l in first] + [_pack([w[n] for n in _SMALL], F32)]), "gather_attn")
    wt_attn_in, w_attn_out = joined(got[:2])
    full = {}
    for n, g in zip(_SMALL, _unpack(got[2], [w[n].shape for n in _SMALL], lead=(N_DEV,))):
        full[n] = _join(g, _SHARD_AX[n])
    ssm_cw8 = _rows8(full['ssm_conv_w'][0])
    ssm_cb = full['ssm_conv_b']
    ssm_nw = full['ssm_norm']
    ffn_cw8 = [_rows8(full['ffn_conv_w'][l]) for l in range(2)]
    ffn_cb = [ffn_conv_b[l:l + 1] for l in range(2)]

    x0 = x[0]
    target = loss_target[0]
    t = x0.shape[0]

    g_mix0, g_mix1 = norm_mix[0:1], norm_mix[1:2]
    g_ffn0, g_ffn1 = norm_ffn[0:1], norm_ffn[1:2]
    proj, h0 = _rms_mm(x0, g_mix0, wt_attn_in, 2304, "mm_attn_in", F32)
    hn_w = jnp.concatenate([jnp.tile(v, (1, 2)) for v in (q_norm_a, k_norm_a, q_norm_b, k_norm_b)], axis=0)
    qa, kpa, vpa, qb, kpb, vpb = _headnorm_fwd(proj, hn_w, "headnorm")
    table = jnp.pad(relpos_table[0], ((0, 0), (0, REL_W - (2 * MAX_REL + 1))))
    bias_a = jnp.where(_band_mask(A_PREV, PAD_A)[None], jnp.transpose(_relpos_fwd(table, "relpos_bias"), (1, 0, 2)), NEG)
    rel_b = jnp.arange(TQ)[:, None] - (jnp.arange(PAD_B + TQ)[None, :] - PAD_B)
    slopes = 2.0 ** (-8.0 * jnp.arange(1, N_HEADS + 1, dtype=F32) / N_HEADS)
    bias_b = jnp.where(_band_mask(B_PREV, PAD_B)[None], -slopes[:, None, None] * jnp.abs(rel_b).astype(F32)[None], NEG)
    no_sinks = jnp.full((N_HEADS,), NEG, F32)
    ffn0_w, ssm_w, ffn1_w = [('ffn_w_in', 0), ('ffn_w_out', 0)], [('ssm_w_in', 0), ('ssm_w_out', 0)], [('ffn_w_in', 1), ('ffn_w_out', 1)]
    oa, stats_a, got = _attn_fwd(qa, kpa, vpa, bias_a, no_sinks, PAD_A, "attn_a", comm=gather_of(ffn0_w + ssm_w))
    wt_ffn_in0, w_ffn_out0, wt_ssm_in, w_ssm_out = joined(got)
    ob, stats_b, got = _attn_fwd(qb, kpb, vpb, bias_b, sinks[0], PAD_B, "attn_b", comm=gather_of(ffn1_w))
    wt_ffn_in1, w_ffn_out1 = joined(got)
    wt_ssm_dt = jnp.pad(wt_ssm_in[ZX:], ((0, LANES - SSM_HEADS), (0, 0)))
    x1 = _mm(oa, w_attn_out, "mm_attn_out_a", res=x0, b_rows=(0, 512))
    x1 = _mm(ob, w_attn_out, "mm_attn_out_b", res=x1, b_rows=(512, 512))
    a0, ffn0_saved = _ffn_fwd(x1, g_ffn0, wt_ffn_in0, ffn_cw8[0], ffn_cb[0], "0")
    x2 = _mm(a0, w_ffn_out0, "mm_ffn_out0", res=x1)

    zx, h2 = _rms_mm(x2, g_mix1, wt_ssm_in, ZX, "mm_ssm_in", F32)
    dtraw = _mm(h2, wt_ssm_dt, "mm_ssm_dt", trans_b=True)
    dt_bias = _lanes128(ssm_dt_bias[0])
    alog = _lanes128(ssm_a_log[0])
    dexp = jnp.repeat(ssm_d[0], HEAD_DIM).reshape(1, D_INNER)
    xbc = _ssm_pre_fwd(zx, ssm_cw8, ssm_cb, "ssm_pre")
    dt = _dt_fwd(dtraw, dt_bias, "ssm_dt")
    (y, sprev), _ = _ssd_fwd(xbc, dt, alog, "ssd_fwd")
    y4 = _ssm_post_fwd(y, xbc, zx, dexp, ssm_nw, "ssm_post")
    x3 = _mm(y4, w_ssm_out, "mm_ssm_out", res=x2)
    a1, ffn1_saved = _ffn_fwd(x3, g_ffn1, wt_ffn_in1, ffn_cw8[1], ffn_cb[1], "1")

    dx4, dx4b, sq = _mm_loss(a1, w_ffn_out1, x3, target, "mm_ffn_out1_loss")
    loss = lax.psum(0.5 * jnp.sum(sq) / D_MODEL, ("x", "y", "c"))

    grads = {}

    def scatter_of(grads_2d):
        return _Comm([g.reshape(N_DEV, -1, D_MODEL) for g in grads_2d], [True] * len(grads_2d))

    dx3, dx3b, dg_ffn1, dwtin1, dcw1, dcb1, dwout1 = _ffn_bwd(
        dx4, dx4b, x3, g_ffn1, wt_ffn_in1, ffn_cw8[1], ffn_cb[1], w_ffn_out1, ffn1_saved, "1")

    dy4 = _mm(dx3b, w_ssm_out, "mm_ssm_dy", trans_b=True)
    dw_ssm_out = _mm_tn(y4, dx3b, "mm_ssm_dwout")
    dyv, dskip, dz, dd_lane, dnw = _ssm_post_bwd(dy4, y, xbc, zx, dexp, ssm_nw, "ssm_post_bwd")
    (dxbc, ddt, dalog), parts_ffn1 = _ssd_bwd(xbc, dt, alog, sprev, dyv, dskip, "ssd_bwd", comm=scatter_of([dwtin1, dwout1]))
    dxr, dcw_s, dcb_s = _ssm_pre_bwd(zx, dxbc, ssm_cw8, ssm_cb, "ssm_pre_bwd")
    ddtraw, ddtb = _dt_bwd(dtraw, dt_bias, ddt, "ssm_dt_bwd")
    dh2 = _mm(dz, wt_ssm_in, "mm_ssm_dh_z", b_rows=(0, D_INNER))
    dh2 = _mm(dxr, wt_ssm_in[D_INNER:ZX], "mm_ssm_dh_x", res=dh2)
    dwt_ssm_in = jnp.concatenate([
        _mm_tn(dz, h2, "mm_ssm_dwin_z"), _mm_tn(dxr, h2, "mm_ssm_dwin_x"),
        _mm_tn(ddtraw, h2, "mm_ssm_dwin_dt")[:SSM_HEADS]], axis=0)
    dx2, dx2b, dg_mix1 = _mm_rms_bwd(ddtraw, wt_ssm_dt, 0, dh2, x2, g_mix1, dx3, "mm_ssm_dh_dt")
    grads['ssm_conv_w'] = dcw_s[:4][None]
    grads['ssm_conv_b'] = dcb_s
    grads['ssm_norm'] = dnw
    grads['ssm_dt_bias'] = ddtb[:, :SSM_HEADS]
    grads['ssm_a_log'] = dalog[:, :SSM_HEADS]
    grads['ssm_d'] = jnp.sum(dd_lane.reshape(SSM_HEADS, HEAD_DIM), axis=1)[None]

    dx1, dx1b, dg_ffn0, dwtin0, dcw0, dcb0, dwout0 = _ffn_bwd(
        dx2, dx2b, x1, g_ffn0, wt_ffn_in0, ffn_cw8[0], ffn_cb[0], w_ffn_out0, ffn0_saved, "0")
    grads['ffn_conv_w'] = jnp.stack([dcw0, dcw1])
    grads['ffn_conv_b'] = jnp.concatenate([dcb0, dcb1], axis=0)
    grads['norm_ffn'] = jnp.concatenate([dg_ffn0, dg_ffn1], axis=0)

    do = _mm(dx1b, w_attn_out, "mm_attn_do", out_dtype=BF16, trans_b=True)
    dw_attn_out = jnp.concatenate([_mm_tn(oa, dx1b, "mm_attn_dwout_a"), _mm_tn(ob, dx1b, "mm_attn_dwout_b")], axis=0)
    (dqa, dkpa, dvpa, dbias_a, _), parts_ssm = _attn_bwd(
        qa, kpa, vpa, bias_a, no_sinks, do, stats_a, 0, PAD_A, "attn_a_bwd",
        comm=scatter_of([dwt_ssm_in, dw_ssm_out, dw_attn_out]))
    (dqb, dkpb, dvpb, _, dsink), parts_ffn0 = _attn_bwd(
        qb, kpb, vpb, bias_b, sinks[0], do, stats_b, 4, PAD_B, "attn_b_bwd", comm=scatter_of([dwtin0, dwout0]))
    grads['relpos_table'] = _relpos_bwd(jnp.transpose(dbias_a, (1, 0, 2)), "relpos_bwd")[None, :, :2 * MAX_REL + 1]
    grads['sinks'] = dsink[:, :2, 0].reshape(1, N_HEADS)
    dproj, dhn = _headnorm_bwd(proj, hn_w, dqa, dkpa, dvpa, dqb, dkpb, dvpb, "headnorm_bwd")
    dhn = dhn[:, :HEAD_DIM] + dhn[:, HEAD_DIM:]
    for k, n in enumerate(('q_norm_a', 'k_norm_a', 'q_norm_b', 'k_norm_b')):
        grads[n] = dhn[k:k + 1]
    dwt_attn_in = _mm_tn(dproj, h0, "mm_attn_dwin")
    dx0, _, dg_mix0, parts_attn_in = _mm_rms_bwd(dproj, wt_attn_in, 0, None, x0, g_mix0, dx1, "mm_attn_dh",
                                                 comm=scatter_of([dwt_attn_in]))
    grads['norm_mix'] = jnp.concatenate([dg_mix0, dg_mix1], axis=0)

    def summed_t(parts, name):
        return _sum_parts(parts, name).T[None]

    sm_shapes = [w[n].shape for n in _SMALL]
    rp_shapes = [w[n].shape for n in _REPL]
    recv = _exchange(_Comm(
        [_pack([_split(grads[n], _SHARD_AX[n]) for n in _SMALL], F32, lead=(N_DEV,)), _pack([grads[n] for n in _REPL], F32)],
        [True, False]), "exchange_small")
    big_parts = {
        'attn_w_in': [summed_t(parts_attn_in[0], "sum_attn_w_in")], 'attn_w_out': [parts_ssm[2]],
        'ssm_w_in': [summed_t(parts_ssm[0], "sum_ssm_w_in")], 'ssm_w_out': [parts_ssm[1]],
        'ffn_w_in': [summed_t(parts_ffn0[0], "sum_ffn_w_in0"), summed_t(parts_ffn1[0], "sum_ffn_w_in1")],
        'ffn_w_out': [parts_ffn0[1], parts_ffn1[1]],
    }
    res = [{}, {}, {}, {}]
    for n in _BIG:
        for kind, a in enumerate(_adamw(big_parts[n], w[n], mom[n], var[n], f"adamw_{n}")):
            res[kind][n] = a
    for names, shapes, parts in ((_SMALL, sm_shapes, recv[0]), (_REPL, rp_shapes, recv[1])):
        outs = _adamw([parts], _pack([w[n] for n in names], F32)[None], _pack([mom[n] for n in names], F32)[None],
                      _pack([var[n] for n in names], F32)[None], "adamw_" + ("small" if names is _SMALL else "replicated"))
        for kind, flat in enumerate(outs):
            for n, a in zip(names, _unpack(flat[0], shapes)):
                res[kind][n] = a
    return (loss, dx0[None], *[res[0][n] for n in _WEIGHTS], *[res[1][n] for n in _WEIGHTS],
            *[res[2][n] for n in _WEIGHTS], *[res[3][n] for n in _WEIGHTS])
```

```python
import jax
import jax.numpy as jnp
from jax import lax
from jax.experimental import pallas as pl
from jax.experimental.pallas import tpu as pltpu

F32 = jnp.float32
BF16 = jnp.bfloat16
HI = lax.Precision.HIGHEST
MESH = pl.DeviceIdType.MESH
NEG = -1e30

N_DEV = 8
D_MODEL = 1024
EPS = 1e-6
CHUNK = 64
HEAD_DIM = 64
N_HEADS = 8
A_PREV = 8
B_PREV = 2
MAX_REL = 256
TQ = 2 * CHUNK
ATT_SUB = 4
PAD_A = A_PREV * CHUNK
PAD_B = B_PREV * CHUNK
REL_W = PAD_A + TQ
D_INNER = 2048
SSM_HEADS = 32
SSM_GROUPS = 4
SSM_STATE = 128
XBC = D_INNER + 2 * SSM_GROUPS * SSM_STATE
ZX = D_INNER + XBC
D_FF = 2816
SSD_L = 128
LANES = 128
VMEM_LIMIT = 56 << 20

ADAM_LR, ADAM_B1, ADAM_B2, ADAM_EPS, ADAM_WD, ADAM_STEP = 0.001, 0.9, 0.999, 1e-08, 0.01, 10


def _cp(sem=None):
    return pltpu.CompilerParams(dimension_semantics=sem, vmem_limit_bytes=VMEM_LIMIT)


def _dot(a, b, ca=1, cb=0, prec=None):
    return lax.dot_general(a, b, (((ca,), (cb,)), ((), ())), preferred_element_type=F32, precision=prec)


def _pick(n, cands):
    for c in cands:
        if n % c == 0:
            return c
    return n


def _lo_mask():
    return lax.broadcasted_iota(jnp.int32, (1, LANES), 1) < HEAD_DIM


_TN_CHUNKS = (1408, 1536, 1152, 1024, 512, 256, 128)


TN_MAX_ROWS = 3072


def _mm_tn(a, b, name):
    kdim, m = a.shape
    n = b.shape[1]
    assert b.shape[0] == kdim, (a.shape, b.shape)
    mb = m if m <= TN_MAX_ROWS else m // 2
    tn = _pick(n, _TN_CHUNKS)
    tk = _pick(kdim, (512, 256, 128))
    nk = kdim // tk

    def body(a_ref, b_ref, o_ref, acc):
        k = pl.program_id(1)

        @pl.when(k == 0)
        def _():
            acc[...] = jnp.zeros_like(acc)

        av = a_ref[...]
        for c in range(0, n, tn):
            acc[:, c:c + tn] += _dot(av, b_ref[:, c:c + tn], 0, 0)

        @pl.when(k == nk - 1)
        def _():
            o_ref[...] = acc[...].astype(BF16)

    return pl.pallas_call(
        body, name=name, grid=(m // mb, nk),
        in_specs=[pl.BlockSpec((tk, mb), lambda j, k: (k, j)), pl.BlockSpec((tk, n), lambda j, k: (k, 0))],
        out_specs=pl.BlockSpec((mb, n), lambda j, k: (j, 0)), out_shape=jax.ShapeDtypeStruct((m, n), BF16),
        scratch_shapes=[pltpu.VMEM((mb, n), F32)], compiler_params=_cp(("parallel", "arbitrary")),
    )(a, b)


def _mm(a, b, name, out_dtype=F32, res=None, trans_b=False, b_rows=None):
    m, kdim = a.shape
    if b_rows is None:
        b_rows = (0, b.shape[0])
    off, rows = b_rows
    n = rows if trans_b else b.shape[1]
    assert (b.shape[1] if trans_b else rows) == kdim and off % rows == 0, (a.shape, b.shape, b_rows)
    tn = _pick(n, _TN_CHUNKS)
    tm = _pick(m, (256, 128) if n > 2304 else (512, 256, 128))

    def body(*refs):
        if res is None:
            a_ref, b_ref, o_ref = refs
        else:
            a_ref, b_ref, r_ref, o_ref = refs
        av = a_ref[...]
        for c in range(0, n, tn):
            r = _dot(av, b_ref[c:c + tn, :], 1, 1) if trans_b else _dot(av, b_ref[:, c:c + tn], 1, 0)
            if res is not None:
                r = r + r_ref[:, c:c + tn]
            o_ref[:, c:c + tn] = r.astype(out_dtype)

    in_specs = [pl.BlockSpec((tm, kdim), lambda i: (i, 0)), pl.BlockSpec((rows, b.shape[1]), lambda i: (off // rows, 0))]
    args = [a, b]
    if res is not None:
        in_specs.append(pl.BlockSpec((tm, n), lambda i: (i, 0)))
        args.append(res)
    return pl.pallas_call(
        body, name=name, grid=(m // tm,), in_specs=in_specs, out_specs=pl.BlockSpec((tm, n), lambda i: (i, 0)),
        out_shape=jax.ShapeDtypeStruct((m, n), out_dtype), compiler_params=_cp(("parallel",)),
    )(*args)


def _rms_mm(x, g, bt, n, name, out_dtype):
    t, d = x.shape
    tn = _pick(n, _TN_CHUNKS)
    tm = _pick(t, (256, 128))

    def body(x_ref, g_ref, b_ref, o_ref, h_ref):
        xv = x_ref[...]
        r = lax.rsqrt(jnp.mean(xv * xv, axis=-1, keepdims=True) + EPS)
        h = (xv * r * g_ref[...]).astype(BF16)
        h_ref[...] = h
        for c in range(0, n, tn):
            o_ref[:, c:c + tn] = _dot(h, b_ref[c:c + tn, :], 1, 1).astype(out_dtype)

    row = pl.BlockSpec((tm, d), lambda i: (i, 0))
    return pl.pallas_call(
        body, name=name, grid=(t // tm,),
        in_specs=[row, pl.BlockSpec((1, d), lambda i: (0, 0)), pl.BlockSpec(bt.shape, lambda i: (0, 0))],
        out_specs=[pl.BlockSpec((tm, n), lambda i: (i, 0)), row],
        out_shape=[jax.ShapeDtypeStruct((t, n), out_dtype), jax.ShapeDtypeStruct((t, d), BF16)],
        compiler_params=_cp(("parallel",)),
    )(x, g, bt)


def _mm_rms_bwd(a, b, b_off, dh_prev, x, g, dres, name, comm=None):
    t, d = x.shape
    kdim = a.shape[1]
    assert b_off % kdim == 0 and b.shape[1] == d, (a.shape, b.shape, b_off)
    tm = _pick(t, (256, 128))

    def body(*refs):
        if dh_prev is None:
            a_ref, b_ref, x_ref, g_ref, dr_ref, dx_ref, dxb_ref, dg_ref = refs
            dhv = _dot(a_ref[...], b_ref[...], 1, 0)
        else:
            a_ref, b_ref, p_ref, x_ref, g_ref, dr_ref, dx_ref, dxb_ref, dg_ref = refs
            dhv = _dot(a_ref[...], b_ref[...], 1, 0) + p_ref[...]
        xv = x_ref[...]
        r = lax.rsqrt(jnp.mean(xv * xv, axis=-1, keepdims=True) + EPS)
        xh = xv * r
        dxh = dhv * g_ref[...]
        dx = dr_ref[...] + r * (dxh - xh * jnp.mean(dxh * xh, axis=-1, keepdims=True))
        dx_ref[...] = dx
        dxb_ref[...] = dx.astype(BF16)

        @pl.when(pl.program_id(0) == 0)
        def _():
            dg_ref[...] = jnp.zeros_like(dg_ref)

        dg_ref[...] += jnp.sum(dhv * xh, axis=0, keepdims=True)

    row = pl.BlockSpec((tm, d), lambda i: (i, 0))
    vec = pl.BlockSpec((1, d), lambda i: (0, 0))
    in_specs = [pl.BlockSpec((tm, kdim), lambda i: (i, 0)), pl.BlockSpec((kdim, d), lambda i: (b_off // kdim, 0))]
    args = [a, b]
    if dh_prev is not None:
        in_specs.append(row)
        args.append(dh_prev)
    outs, got = _call(
        body, name=name, grid=(t // tm,), in_specs=in_specs + [row, vec, row], out_specs=[row, row, vec],
        out_shape=[jax.ShapeDtypeStruct((t, d), F32), jax.ShapeDtypeStruct((t, d), BF16), jax.ShapeDtypeStruct((1, d), F32)],
        args=(*args, x, g, dres), sem=("arbitrary",), comm=comm)
    return (*outs, got) if comm is not None else tuple(outs)


def _mm_loss(a, b, res, target, name):
    t, kdim = a.shape
    d = b.shape[1]
    tm = _pick(t, (512, 256, 128))

    def body(a_ref, b_ref, r_ref, t_ref, dy_ref, dyb_ref, acc_ref):
        @pl.when(pl.program_id(0) == 0)
        def _():
            acc_ref[...] = jnp.zeros_like(acc_ref)

        err = _dot(a_ref[...], b_ref[...], 1, 0) + r_ref[...] - t_ref[...]
        dy = err * (1.0 / d)
        dy_ref[...] = dy
        dyb_ref[...] = dy.astype(BF16)
        acc_ref[...] += jnp.sum(err * err, axis=0, keepdims=True)

    row = pl.BlockSpec((tm, d), lambda i: (i, 0))
    vec = pl.BlockSpec((1, d), lambda i: (0, 0))
    return pl.pallas_call(
        body, name=name, grid=(t // tm,),
        in_specs=[pl.BlockSpec((tm, kdim), lambda i: (i, 0)), pl.BlockSpec((kdim, d), lambda i: (0, 0)), row, row],
        out_specs=[row, row, vec],
        out_shape=[jax.ShapeDtypeStruct((t, d), F32), jax.ShapeDtypeStruct((t, d), BF16), jax.ShapeDtypeStruct((1, d), F32)],
        compiler_params=_cp(("arbitrary",)),
    )(a, b, res, target)


def _head_sums(v):
    ri = lax.broadcasted_iota(jnp.int32, (LANES, LANES), 0) // HEAD_DIM
    ci = lax.broadcasted_iota(jnp.int32, (LANES, LANES), 1) // HEAD_DIM
    ones = (ri == ci).astype(BF16)
    hi = v.astype(BF16)
    lo_part = (v - hi.astype(F32)).astype(BF16)
    return _dot(hi, ones, 1, 0) + _dot(lo_part, ones, 1, 0)


def _head_rms(xs, w, lo):
    r = lax.rsqrt(_head_sums(xs * xs) * (1.0 / HEAD_DIM) + EPS)
    return xs * r, r


def _head_rms_bwd(xs, w, dy, lo):
    xh, r = _head_rms(xs, w, lo)
    dxh = dy * w
    mm = _head_sums(dxh * xh) * (1.0 / HEAD_DIM)
    return r * (dxh - xh * mm), dy * xh


_QSCALE = HEAD_DIM ** -0.5


def _headnorm_fwd(proj, ws, name):
    t = proj.shape[0]
    tm = TQ
    lead = PAD_A // tm
    leadb = PAD_B // tm

    def body(p_ref, w_ref, qa_ref, ka_ref, va_ref, qb_ref, kb_ref, vb_ref):
        data = pl.program_id(0) >= lead
        lo = _lo_mask()

        def put(ref, c, val):
            ref[:, c:c + val.shape[1]] = jnp.where(data, val, 0.0).astype(BF16)

        def per_query_head(slab):
            other = pltpu.roll(slab, HEAD_DIM, 1)
            e0, e1 = jnp.where(lo, slab, other), jnp.where(lo, other, slab)
            return jnp.concatenate([e0, e0, e1, e1], axis=1)

        for s in range(4):
            c = LANES * s
            xh, _ = _head_rms(p_ref[:, c:c + LANES], None, lo)
            qa_ref[:, c:c + LANES] = (xh * w_ref[0:1, :] * _QSCALE).astype(BF16)
            xh, _ = _head_rms(p_ref[:, 512 + c:512 + c + LANES], None, lo)
            put(ka_ref, c, xh * w_ref[1:2, :])
            xh, _ = _head_rms(p_ref[:, 1536 + c:1536 + c + LANES], None, lo)
            qb_ref[:, c:c + LANES] = (xh * w_ref[2:3, :] * _QSCALE).astype(BF16)
        put(va_ref, 0, p_ref[:, 1024:1536])
        xh, _ = _head_rms(p_ref[:, 2048:2176], None, lo)
        put(kb_ref, 0, per_query_head(xh * w_ref[3:4, :]))
        put(vb_ref, 0, per_query_head(p_ref[:, 2176:2304]))

    src = lambda i: jnp.maximum(i - lead, 0)
    wide = pl.BlockSpec((tm, 512), lambda i: (src(i), 0))
    pad_a = pl.BlockSpec((tm, 512), lambda i: (i, 0))
    pad_b = pl.BlockSpec((tm, 512), lambda i: (jnp.maximum(i - lead + leadb, 0), 0))
    sd = lambda rows: jax.ShapeDtypeStruct((rows, 512), BF16)
    return pl.pallas_call(
        body, name=name, grid=(t // tm + lead,),
        in_specs=[pl.BlockSpec((tm, 2304), lambda i: (src(i), 0)), pl.BlockSpec((4, LANES), lambda i: (0, 0))],
        out_specs=[wide, pad_a, pad_a, wide, pad_b, pad_b],
        out_shape=[sd(t), sd(t + PAD_A), sd(t + PAD_A), sd(t), sd(t + PAD_B), sd(t + PAD_B)],
        compiler_params=_cp(("arbitrary",)),
    )(proj, ws)


def _headnorm_bwd(proj, ws, dqa, dkpa, dvpa, dqb, dkpb, dvpb, name):
    t = proj.shape[0]
    tm = TQ
    offa, offb = PAD_A // tm, PAD_B // tm

    def body(p_ref, w_ref, dqa_ref, dka_ref, dva_ref, dqb_ref, dkb_ref, dvb_ref, dp_ref, dw_ref):
        i = pl.program_id(0)
        lo = _lo_mask()

        @pl.when(i == 0)
        def _():
            dw_ref[...] = jnp.zeros_like(dw_ref)

        acc = [jnp.zeros((1, LANES), F32) for _ in range(4)]
        for s in range(4):
            c = LANES * s
            dx, dwl = _head_rms_bwd(p_ref[:, c:c + LANES], w_ref[0:1, :], dqa_ref[:, c:c + LANES] * _QSCALE, lo)
            dp_ref[:, c:c + LANES] = dx.astype(BF16)
            acc[0] += jnp.sum(dwl, axis=0, keepdims=True)
            dx, dwl = _head_rms_bwd(p_ref[:, 512 + c:512 + c + LANES], w_ref[1:2, :], dka_ref[:, c:c + LANES], lo)
            dp_ref[:, 512 + c:512 + c + LANES] = dx.astype(BF16)
            acc[1] += jnp.sum(dwl, axis=0, keepdims=True)
            dx, dwl = _head_rms_bwd(p_ref[:, 1536 + c:1536 + c + LANES], w_ref[2:3, :], dqb_ref[:, c:c + LANES] * _QSCALE, lo)
            dp_ref[:, 1536 + c:1536 + c + LANES] = dx.astype(BF16)
            acc[2] += jnp.sum(dwl, axis=0, keepdims=True)
        dp_ref[:, 1024:1536] = dva_ref[...].astype(BF16)

        def group_sum(ref):
            s0 = ref[:, 0:128] + ref[:, 128:256]
            s1 = ref[:, 256:384] + ref[:, 384:512]
            s0 = s0 + pltpu.roll(s0, HEAD_DIM, 1)
            s1 = s1 + pltpu.roll(s1, HEAD_DIM, 1)
            return jnp.where(lo, s0, s1)

        dx, dwl = _head_rms_bwd(p_ref[:, 2048:2176], w_ref[3:4, :], group_sum(dkb_ref), lo)
        dp_ref[:, 2048:2176] = dx.astype(BF16)
        acc[3] += jnp.sum(dwl, axis=0, keepdims=True)
        dp_ref[:, 2176:2304] = group_sum(dvb_ref).astype(BF16)
        for n in range(4):
            dw_ref[n:n + 1, :] += acc[n]

    wide = pl.BlockSpec((tm, 512), lambda i: (i, 0))
    pa = pl.BlockSpec((tm, 512), lambda i: (i + offa, 0))
    pb = pl.BlockSpec((tm, 512), lambda i: (i + offb, 0))
    return pl.pallas_call(
        body, name=name, grid=(t // tm,),
        in_specs=[pl.BlockSpec((tm, 2304), lambda i: (i, 0)), pl.BlockSpec((4, LANES), lambda i: (0, 0)),
                  wide, pa, pa, wide, pb, pb],
        out_specs=[pl.BlockSpec((tm, 2304), lambda i: (i, 0)), pl.BlockSpec((4, LANES), lambda i: (0, 0))],
        out_shape=[jax.ShapeDtypeStruct((t, 2304), BF16), jax.ShapeDtypeStruct((4, LANES), F32)],
        compiler_params=_cp(("arbitrary",)),
    )(proj, ws, dqa, dkpa, dvpa, dqb, dkpb, dvpb)


ROLL_W = 1024


def _rel_onehot():
    r_io = lax.broadcasted_iota(jnp.int32, (REL_W, ROLL_W), 0)
    m_io = lax.broadcasted_iota(jnp.int32, (REL_W, ROLL_W), 1)
    return (r_io == jnp.clip(REL_W - 1 - m_io, -MAX_REL, MAX_REL) + MAX_REL).astype(F32)


def _relpos_fwd(table, name):
    def body(t_ref, o_ref):
        rr = _dot(t_ref[...], _rel_onehot(), 1, 0, HI)

        def step(q, c):
            o_ref[q] = pltpu.roll(rr, (ROLL_W - (TQ - 1) + q) % ROLL_W, 1)[:, :REL_W]
            return c

        lax.fori_loop(0, TQ, step, 0)

    return pl.pallas_call(
        body, name=name, out_shape=jax.ShapeDtypeStruct((TQ, N_HEADS, REL_W), F32),
        in_specs=[pl.BlockSpec(memory_space=pltpu.VMEM)], out_specs=pl.BlockSpec(memory_space=pltpu.VMEM),
        compiler_params=_cp(),
    )(table)


def _relpos_bwd(dbias_t, name):
    def body(d_ref, o_ref):
        def step(q, acc):
            row = jnp.concatenate([d_ref[q], jnp.zeros((N_HEADS, ROLL_W - REL_W), F32)], axis=1)
            return acc + pltpu.roll(row, TQ - 1 - q, 1)

        drr = lax.fori_loop(0, TQ, step, jnp.zeros((N_HEADS, ROLL_W), F32))
        o_ref[...] = _dot(drr, _rel_onehot(), 1, 1, HI)

    return pl.pallas_call(
        body, name=name, out_shape=jax.ShapeDtypeStruct((N_HEADS, REL_W), F32),
        in_specs=[pl.BlockSpec(memory_space=pltpu.VMEM)], out_specs=pl.BlockSpec(memory_space=pltpu.VMEM),
        compiler_params=_cp(),
    )(dbias_t)


def _attn_scores(qe, kw, bias, kvalid):
    return jnp.where(kvalid, _dot(qe, kw, 1, 1) + bias, NEG)


def _stat_cols(stats, e):
    return stats[:, 64 * e:64 * e + 1], stats[:, 64 * e + 32:64 * e + 33]


def _attn_fwd(q, kp, vp, bias, sinks, pad, name, comm=None):
    t, hd = q.shape
    w = pad + TQ

    def body(sink_ref, q_ref, k_ref, v_ref, b_ref, o_ref, st_ref):
        hp, i = pl.program_id(0), pl.program_id(1)
        lo = _lo_mask()
        lane = lax.broadcasted_iota(jnp.int32, (1, LANES), 1)
        for j in range(ATT_SUB):
            start = pl.multiple_of((i * ATT_SUB + j) * TQ, TQ)
            qv = q_ref[TQ * j:TQ * (j + 1), :]
            kw = k_ref[pl.ds(start, w), :]
            vw = v_ref[pl.ds(start, w), :]
            kvalid = (start + lax.broadcasted_iota(jnp.int32, (1, w), 1)) >= pad
            outs, ms, ls = [], [], []
            for e in range(2):
                sel = lo if e == 0 else jnp.logical_not(lo)
                qe = jnp.where(sel, qv, jnp.zeros_like(qv))
                snk = sink_ref[2 * hp + e]
                s = _attn_scores(qe, kw, b_ref[e], kvalid)
                m = jnp.maximum(jnp.max(s, axis=-1, keepdims=True), snk)
                acc = _dot(jnp.exp(s - m).astype(BF16), jnp.where(sel, vw, jnp.ones_like(vw)), 1, 0)
                denom = acc[:, 64 * (1 - e):64 * (1 - e) + 1] + jnp.exp(snk - m)
                outs.append(acc * (1.0 / denom))
                ms.append(m)
                ls.append(denom)
            o_ref[TQ * j:TQ * (j + 1), :] = jnp.where(lo, outs[0], outs[1]).astype(BF16)
            st_ref[TQ * j:TQ * (j + 1), :] = jnp.where(lane < 32, ms[0], jnp.where(lane < 64, ls[0],
                                                                                 jnp.where(lane < 96, ms[1], ls[1])))

    full = pl.BlockSpec((t + pad, LANES), lambda h, i: (0, h))
    tile = pl.BlockSpec((ATT_SUB * TQ, LANES), lambda h, i: (i, h))
    (o, stats), got = _call(
        body, name=name, grid=(hd // LANES, t // (ATT_SUB * TQ)),
        in_specs=[pl.BlockSpec(memory_space=pltpu.SMEM), tile, full, full, pl.BlockSpec((2, TQ, w), lambda h, i: (h, 0, 0))],
        out_specs=[tile, tile], out_shape=[jax.ShapeDtypeStruct((t, hd), BF16), jax.ShapeDtypeStruct((t, hd), F32)],
        args=(sinks, q, kp, vp, bias), sem=("parallel", "arbitrary"), comm=comm)
    return o, stats, got


def _attn_bwd(q, kp, vp, bias, sinks, do, stats, col_off, pad, name, comm=None):
    t, hd = q.shape
    w = pad + TQ
    nhp = hd // LANES

    def body(sink_ref, q_ref, k_ref, v_ref, b_ref, do_ref, st_ref, dq_ref, dk_ref, dv_ref, db_ref, ds_ref):
        hp, i = pl.program_id(0), pl.program_id(1)

        @pl.when(i == 0)
        def _():
            dk_ref[...] = jnp.zeros_like(dk_ref)
            dv_ref[...] = jnp.zeros_like(dv_ref)
            db_ref[...] = jnp.zeros_like(db_ref)
            ds_ref[...] = jnp.zeros_like(ds_ref)

        lo = _lo_mask()
        row8 = lax.broadcasted_iota(jnp.int32, (8, LANES), 0)
        dbias = [None, None]
        dsink = jnp.zeros((8, LANES), F32)
        for j in range(ATT_SUB):
            start = pl.multiple_of((i * ATT_SUB + j) * TQ, TQ)
            qv = q_ref[TQ * j:TQ * (j + 1), :]
            dov = do_ref[TQ * j:TQ * (j + 1), :]
            kw = k_ref[pl.ds(start, w), :]
            vw = v_ref[pl.ds(start, w), :]
            kvalid = (start + lax.broadcasted_iota(jnp.int32, (1, w), 1)) >= pad
            stats = st_ref[TQ * j:TQ * (j + 1), :]
            dqs, dkw, dvw = [], None, None
            for e in range(2):
                sel = lo if e == 0 else jnp.logical_not(lo)
                qe = jnp.where(sel, qv, jnp.zeros_like(qv))
                doe = jnp.where(sel, dov, jnp.zeros_like(dov))
                m, denom = _stat_cols(stats, e)
                inv = 1.0 / denom
                p = jnp.exp(_attn_scores(qe, kw, b_ref[e], kvalid) - m) * inv
                psink = jnp.exp(sink_ref[2 * hp + e] - m) * inv
                dp = _dot(doe, vw, 1, 1)
                delta = jnp.sum(p * dp, axis=-1, keepdims=True)
                ds = p * (dp - delta)
                dbias[e] = ds if dbias[e] is None else dbias[e] + ds
                dsink = dsink + jnp.where(row8 == e, jnp.sum(-psink * delta, axis=0, keepdims=True), 0.0)
                dsb = ds.astype(BF16)
                dqs.append(_dot(dsb, kw, 1, 0))
                dk_e = _dot(dsb, qe, 0, 0)
                dv_e = _dot(p.astype(BF16), doe, 0, 0)
                dkw = dk_e if dkw is None else dkw + dk_e
                dvw = dv_e if dvw is None else dvw + dv_e
            dq_ref[TQ * j:TQ * (j + 1), :] = jnp.where(lo, dqs[0], dqs[1])
            dk_ref[pl.ds(start, w), :] += dkw
            dv_ref[pl.ds(start, w), :] += dvw
        for e in range(2):
            db_ref[e] += dbias[e]
        ds_ref[0] += dsink

    full = pl.BlockSpec((t + pad, LANES), lambda h, i: (0, h))
    tile = pl.BlockSpec((ATT_SUB * TQ, LANES), lambda h, i: (i, h))
    btile = pl.BlockSpec((2, TQ, w), lambda h, i: (h, 0, 0))
    return _call(
        body, name=name, grid=(nhp, t // (ATT_SUB * TQ)),
        in_specs=[pl.BlockSpec(memory_space=pltpu.SMEM), tile, full, full, btile,
                  pl.BlockSpec((ATT_SUB * TQ, LANES), lambda h, i: (i, h + col_off)), tile],
        out_specs=[tile, full, full, btile, pl.BlockSpec((1, 8, LANES), lambda h, i: (h, 0, 0))],
        out_shape=[jax.ShapeDtypeStruct((t, hd), F32), jax.ShapeDtypeStruct((t + pad, hd), F32),
                   jax.ShapeDtypeStruct((t + pad, hd), F32), jax.ShapeDtypeStruct((N_HEADS, TQ, w), F32),
                   jax.ShapeDtypeStruct((nhp, 8, LANES), F32)],
        args=(sinks, q, kp, vp, bias, do, stats), sem=("parallel", "arbitrary"), comm=comm)


def _halo_prev(tm):
    return lambda i: jnp.maximum(i * (tm // 8) - 1, 0)


def _halo_next(tm, t):
    return lambda i: jnp.minimum((i + 1) * (tm // 8), t // 8 - 1)


def _taps_prev(tile, halo, ktaps, first):
    tm = tile.shape[0]
    ext = jnp.concatenate([jnp.where(first, 0.0, halo), tile], axis=0)
    return [tile] + [pltpu.roll(ext, s, 0)[8:8 + tm] for s in range(1, ktaps)]


def _conv_apply(taps, w_ref, ktaps):
    out = taps[0] * w_ref[ktaps - 1:ktaps, :]
    for s in range(1, ktaps):
        out = out + taps[s] * w_ref[ktaps - 1 - s:ktaps - s, :]
    return out


def _sigmoid(x):
    return jax.nn.sigmoid(x)


def _silu_grad(x):
    sg = _sigmoid(x)
    return x * sg, sg * (1.0 + x * (1.0 - sg))


FFN_TM = 128
FFN_HALO = 16


def _ffn_mid_fwd(gu, w8, b, name):
    t = gu.shape[0]
    f = D_FF
    tm, hr = FFN_TM, FFN_HALO

    def body(g_ref, u_ref, h_ref, w_ref, b_ref, a_ref):
        first = pl.program_id(0) == 0
        ext = jnp.concatenate([jnp.where(first, 0.0, h_ref[...].astype(F32)), g_ref[...].astype(F32)], axis=0)
        taps = [ext[hr:]] + [pltpu.roll(ext, s, 0)[hr:] for s in (1, 2)]
        gc = _conv_apply(taps, w_ref, 3) + b_ref[...]
        a_ref[...] = (gc * _sigmoid(gc) * u_ref[...].astype(F32)).astype(BF16)

    return pl.pallas_call(
        body, name=name, grid=(t // tm,),
        in_specs=[pl.BlockSpec((tm, f), lambda i: (i, 0)), pl.BlockSpec((tm, f), lambda i: (i, 1)),
                  pl.BlockSpec((hr, f), lambda i: (jnp.maximum(i * (tm // hr) - 1, 0), 0)),
                  pl.BlockSpec((8, f), lambda i: (0, 0)), pl.BlockSpec((1, f), lambda i: (0, 0))],
        out_specs=pl.BlockSpec((tm, f), lambda i: (i, 0)), out_shape=jax.ShapeDtypeStruct((t, f), BF16),
        compiler_params=_cp(("parallel",)),
    )(gu, gu, gu, w8, b)


FFN_BT = 256
FFN_BC = 1408


def _ffn_mid_bwd(gu, dxb, w_out, w8, b, name):
    t, d = dxb.shape
    f = D_FF
    tm, hr = FFN_BT, FFN_HALO
    nt = t // tm
    n = tm + hr

    def body(g_ref, u_ref, gp_ref, gn_ref, un_ref, dx_ref, dxn_ref, wo_ref, w_ref, b_ref, dgu_ref, dw_ref, db_ref):
        i = pl.program_id(0)
        first, last = i == 0, i == nt - 1

        @pl.when(first)
        def _():
            dw_ref[...] = jnp.zeros_like(dw_ref)
            db_ref[...] = jnp.zeros_like(db_ref)

        dxe = jnp.concatenate([dx_ref[...], dxn_ref[...]], axis=0)
        row = lax.broadcasted_iota(jnp.int32, (n, 1), 0)
        keep = (row < tm) | jnp.logical_not(last)
        for c in range(0, f, FFN_BC):
            cs = slice(c, c + FFN_BC)
            ext = jnp.concatenate([jnp.where(first, 0.0, gp_ref[:, cs].astype(F32)), g_ref[:, cs].astype(F32),
                                   gn_ref[:, cs].astype(F32)], axis=0)
            taps = [ext[hr:]] + [pltpu.roll(ext, s, 0)[hr:] for s in (1, 2)]
            gc = b_ref[:, cs] + taps[0] * w_ref[2:3, cs] + taps[1] * w_ref[1:2, cs] + taps[2] * w_ref[0:1, cs]
            act, dact = _silu_grad(gc)
            da = _dot(dxe, wo_ref[cs, :], 1, 1)
            up = jnp.concatenate([u_ref[:, cs], un_ref[:, cs]], axis=0).astype(F32)
            dgc = jnp.where(keep, da * up * dact, 0.0)
            dgu_ref[:, f + c:f + c + FFN_BC] = (da[:tm] * act[:tm]).astype(BF16)
            dgu_ref[:, cs] = (dgc[:tm] * w_ref[2:3, cs] + pltpu.roll(dgc, n - 1, 0)[:tm] * w_ref[1:2, cs]
                              + pltpu.roll(dgc, n - 2, 0)[:tm] * w_ref[0:1, cs]).astype(BF16)
            db_ref[:, cs] += jnp.sum(dgc[:tm], axis=0, keepdims=True)
            for s in range(3):
                dw_ref[2 - s:3 - s, cs] += jnp.sum(dgc[:tm] * taps[s][:tm], axis=0, keepdims=True)

    r = tm // hr
    prev = lambda i: jnp.maximum(i * r - 1, 0)
    nxt_blk = lambda i: jnp.minimum((i + 1) * r, t // hr - 1)
    row_f = pl.BlockSpec((tm, f), lambda i: (i, 0))
    return pl.pallas_call(
        body, name=name, grid=(nt,),
        in_specs=[row_f, pl.BlockSpec((tm, f), lambda i: (i, 1)),
                  pl.BlockSpec((hr, f), lambda i: (prev(i), 0)), pl.BlockSpec((hr, f), lambda i: (nxt_blk(i), 0)),
                  pl.BlockSpec((hr, f), lambda i: (nxt_blk(i), 1)),
                  pl.BlockSpec((tm, d), lambda i: (i, 0)), pl.BlockSpec((hr, d), lambda i: (nxt_blk(i), 0)),
                  pl.BlockSpec((f, d), lambda i: (0, 0)),
                  pl.BlockSpec((8, f), lambda i: (0, 0)), pl.BlockSpec((1, f), lambda i: (0, 0))],
        out_specs=[pl.BlockSpec((tm, 2 * f), lambda i: (i, 0)), pl.BlockSpec((8, f), lambda i: (0, 0)),
                   pl.BlockSpec((1, f), lambda i: (0, 0))],
        out_shape=[jax.ShapeDtypeStruct((t, 2 * f), BF16), jax.ShapeDtypeStruct((8, f), F32), jax.ShapeDtypeStruct((1, f), F32)],
        compiler_params=_cp(("arbitrary",)),
    )(gu, gu, gu, gu, gu, dxb, dxb, w_out, w8, b)


PRE_TM = 256
PRE_TC = 1024


def _ssm_pre_fwd(zx, w8, b, name):
    t = zx.shape[0]
    tm, tc = PRE_TM, PRE_TC
    off = D_INNER // tc

    def body(x_ref, h_ref, w_ref, b_ref, o_ref):
        first = pl.program_id(0) == 0
        c = _conv_apply(_taps_prev(x_ref[...], h_ref[...], 4, first), w_ref, 4) + b_ref[...]
        o_ref[...] = c * _sigmoid(c)

    hp = _halo_prev(tm)
    return pl.pallas_call(
        body, name=name, grid=(t // tm, XBC // tc),
        in_specs=[pl.BlockSpec((tm, tc), lambda i, j: (i, j + off)), pl.BlockSpec((8, tc), lambda i, j: (hp(i), j + off)),
                  pl.BlockSpec((8, tc), lambda i, j: (0, j)), pl.BlockSpec((1, tc), lambda i, j: (0, j))],
        out_specs=pl.BlockSpec((tm, tc), lambda i, j: (i, j)), out_shape=jax.ShapeDtypeStruct((t, XBC), F32),
        compiler_params=_cp(("parallel", "parallel")),
    )(zx, zx, w8, b)


def _ssm_pre_bwd(zx, dxbc, w8, b, name):
    t = zx.shape[0]
    tm, tc = PRE_TM, PRE_TC
    off = D_INNER // tc
    nt = t // tm
    n = tm + 8

    def body(x_ref, xp_ref, xn_ref, d_ref, dn_ref, w_ref, b_ref, o_ref, dw_ref, db_ref):
        i = pl.program_id(1)
        first, last = i == 0, i == nt - 1

        @pl.when(first)
        def _():
            dw_ref[...] = jnp.zeros_like(dw_ref)
            db_ref[...] = jnp.zeros_like(db_ref)

        ext = jnp.concatenate([jnp.where(first, 0.0, xp_ref[...]), x_ref[...], xn_ref[...]], axis=0)
        taps = [ext[8:8 + n]] + [pltpu.roll(ext, s, 0)[8:8 + n] for s in (1, 2, 3)]
        c = _conv_apply(taps, w_ref, 4) + b_ref[...]
        _, dact = _silu_grad(c)
        row = lax.broadcasted_iota(jnp.int32, (n, 1), 0)
        dc = jnp.where((row < tm) | jnp.logical_not(last), jnp.concatenate([d_ref[...], dn_ref[...]], axis=0) * dact, 0.0)
        nxt = [dc[:tm]] + [pltpu.roll(dc, n - s, 0)[:tm] for s in (1, 2, 3)]
        o_ref[...] = _conv_apply(nxt, w_ref, 4).astype(BF16)
        db_ref[...] += jnp.sum(dc[:tm], axis=0, keepdims=True)
        for s in range(4):
            dw_ref[3 - s:4 - s, :] += jnp.sum(dc[:tm] * taps[s][:tm], axis=0, keepdims=True)

    hp = _halo_prev(tm)
    hn = _halo_next(tm, t)
    return pl.pallas_call(
        body, name=name, grid=(XBC // tc, nt),
        in_specs=[pl.BlockSpec((tm, tc), lambda j, i: (i, j + off)), pl.BlockSpec((8, tc), lambda j, i: (hp(i), j + off)),
                  pl.BlockSpec((8, tc), lambda j, i: (hn(i), j + off)),
                  pl.BlockSpec((tm, tc), lambda j, i: (i, j)), pl.BlockSpec((8, tc), lambda j, i: (hn(i), j)),
                  pl.BlockSpec((8, tc), lambda j, i: (0, j)), pl.BlockSpec((1, tc), lambda j, i: (0, j))],
        out_specs=[pl.BlockSpec((tm, tc), lambda j, i: (i, j)), pl.BlockSpec((8, tc), lambda j, i: (0, j)),
                   pl.BlockSpec((1, tc), lambda j, i: (0, j))],
        out_shape=[jax.ShapeDtypeStruct((t, XBC), BF16), jax.ShapeDtypeStruct((8, XBC), F32),
                   jax.ShapeDtypeStruct((1, XBC), F32)],
        compiler_params=_cp(("parallel", "arbitrary")),
    )(zx, zx, zx, dxbc, dxbc, w8, b)


def _head_lanes():
    return lax.broadcasted_iota(jnp.int32, (1, LANES), 1) < SSM_HEADS


def _dt_fwd(dtraw, bias, name):
    t = dtraw.shape[0]
    tm = _pick(t, (1024, 512, 256, 128))

    def body(x_ref, b_ref, o_ref):
        v = x_ref[...] + b_ref[...]
        sp = jnp.maximum(v, 0.0) + jnp.log(1.0 + jnp.exp(-jnp.abs(v)))
        o_ref[...] = jnp.where(_head_lanes(), sp, 0.0)

    row = pl.BlockSpec((tm, LANES), lambda i: (i, 0))
    return pl.pallas_call(
        body, name=name, grid=(t // tm,), in_specs=[row, pl.BlockSpec((1, LANES), lambda i: (0, 0))], out_specs=row,
        out_shape=jax.ShapeDtypeStruct((t, LANES), F32), compiler_params=_cp(("parallel",)),
    )(dtraw, bias)


def _dt_bwd(dtraw, bias, ddt, name):
    t = dtraw.shape[0]
    tm = _pick(t, (1024, 512, 256, 128))

    def body(x_ref, b_ref, d_ref, o_ref, db_ref):
        @pl.when(pl.program_id(0) == 0)
        def _():
            db_ref[...] = jnp.zeros_like(db_ref)

        g = jnp.where(_head_lanes(), d_ref[...] * _sigmoid(x_ref[...] + b_ref[...]), 0.0)
        o_ref[...] = g.astype(BF16)
        db_ref[...] += jnp.sum(g, axis=0, keepdims=True)

    row = pl.BlockSpec((tm, LANES), lambda i: (i, 0))
    vec = pl.BlockSpec((1, LANES), lambda i: (0, 0))
    return pl.pallas_call(
        body, name=name, grid=(t // tm,), in_specs=[row, vec, row], out_specs=[row, vec],
        out_shape=[jax.ShapeDtypeStruct((t, LANES), BF16), jax.ShapeDtypeStruct((1, LANES), F32)],
        compiler_params=_cp(("arbitrary",)),
    )(dtraw, bias, ddt)


GROUP_W = D_INNER // SSM_GROUPS


def _ssd_common(dt, alog):
    ll = dt.shape[0]
    a_neg = -jnp.exp(alog)
    a = dt * a_neg
    ri = lax.broadcasted_iota(jnp.int32, (ll, ll), 0)
    ci = lax.broadcasted_iota(jnp.int32, (ll, ll), 1)
    tril = ri >= ci
    acs = _dot(tril.astype(F32), a, 1, 0, HI)
    return a_neg, tril, acs, acs.T


def _pair_terms(acs, acs_t, dt, h0, lo):
    ll = acs.shape[0]
    cols = [acs[:, h0 + e:h0 + e + 1] for e in range(2)]
    rows = [acs_t[h0 + e:h0 + e + 1, :] for e in range(2)]
    dtc = [dt[:, h0 + e:h0 + e + 1] for e in range(2)]
    lasts = [c[ll - 1:ll, :] for c in cols]
    dtx = jnp.where(lo, dtc[0], dtc[1])
    eac = jnp.where(lo, jnp.exp(cols[0]), jnp.exp(cols[1]))
    fdec = jnp.where(lo, jnp.exp(lasts[0] - cols[0]), jnp.exp(lasts[1] - cols[1]))
    elast = jnp.where(lo, jnp.exp(lasts[0]), jnp.exp(lasts[1]))
    return cols, rows, dtx, eac, fdec, elast


def _decay(col, row, tril):
    return jnp.where(tril, jnp.exp(jnp.minimum(col - row, 0.0)), 0.0)


def _two_heads_rows(v, lo):
    z = jnp.zeros_like(v)
    return jnp.concatenate([jnp.where(lo, v, z), jnp.where(lo, z, v)], axis=0)


def _two_heads_cols(ms):
    return jnp.concatenate(ms, axis=1)


def _z_group(z_refs, g):
    return z_refs[g // 2][:, GROUP_W * (g % 2):GROUP_W * (g % 2 + 1)]


def _ssd_fwd(xbc, dt, alog, zx, dexp, nw, name, comm=None):
    t = xbc.shape[0]
    ll = SSD_L
    nc = t // ll

    def body(x_ref, dt_ref, al_ref, z0_ref, z1_ref, d_ref, w_ref, y_ref, sp_ref, y4_ref, st_ref):
        @pl.when(pl.program_id(0) == 0)
        def _():
            st_ref[...] = jnp.zeros_like(st_ref)

        dtv = dt_ref[...]
        _, tril, acs, acs_t = _ssd_common(dtv, al_ref[...])
        lo = _lo_mask()
        sp_ref[0] = st_ref[...]
        for g in range(SSM_GROUPS):
            bg = x_ref[:, D_INNER + SSM_STATE * g:D_INNER + SSM_STATE * (g + 1)].astype(BF16)
            cg = x_ref[:, D_INNER + 512 + SSM_STATE * g:D_INNER + 512 + SSM_STATE * (g + 1)].astype(BF16)
            gm = _dot(cg, bg, 1, 1)
            g0 = GROUP_W * g
            terms = [_pair_terms(acs, acs_t, dtv, 8 * g + 2 * pp, lo) for pp in range(4)]
            dtx, eac, fdec, elast = [jnp.concatenate([tt[k] for tt in terms], axis=1) for k in (2, 3, 4, 5)]
            xg = x_ref[:, g0:g0 + GROUP_W]
            ug = (xg * dtx).astype(BF16)
            sg = st_ref[:, g0:g0 + GROUP_W]
            yst = _dot(cg, sg.astype(BF16), 1, 0) * eac
            st_ref[:, g0:g0 + GROUP_W] = sg * elast + _dot(bg, (xg * (fdec * dtx)).astype(BF16), 0, 0)
            ys = []
            for pp in range(4):
                cols, rows = terms[pp][0], terms[pp][1]
                sl = slice(LANES * pp, LANES * (pp + 1))
                y_in = _dot(_two_heads_cols([(gm * _decay(cols[e], rows[e], tril)).astype(BF16) for e in range(2)]),
                            _two_heads_rows(ug[:, sl], lo), 1, 0)
                ys.append(y_in + yst[:, sl])
            yg = jnp.concatenate(ys, axis=1)
            y_ref[:, g0:g0 + GROUP_W] = yg
            zg = _z_group((z0_ref, z1_ref), g)
            y3 = (yg + d_ref[:, g0:g0 + GROUP_W] * xg) * (zg * _sigmoid(zg))
            r = lax.rsqrt(jnp.mean(y3 * y3, axis=-1, keepdims=True) + EPS)
            y4_ref[:, g0:g0 + GROUP_W] = (y3 * r * w_ref[:, g0:g0 + GROUP_W]).astype(BF16)

    zblk = lambda j: pl.BlockSpec((ll, 1024), lambda c: (c, j))
    vec = pl.BlockSpec((1, D_INNER), lambda c: (0, 0))
    row = pl.BlockSpec((ll, D_INNER), lambda c: (c, 0))
    return _call(
        body, name=name, grid=(nc,),
        in_specs=[pl.BlockSpec((ll, XBC), lambda c: (c, 0)), pl.BlockSpec((ll, LANES), lambda c: (c, 0)),
                  pl.BlockSpec((1, LANES), lambda c: (0, 0)), zblk(0), zblk(1), vec, vec],
        out_specs=[row, pl.BlockSpec((1, SSM_STATE, D_INNER), lambda c: (c, 0, 0)), row],
        out_shape=[jax.ShapeDtypeStruct((t, D_INNER), F32), jax.ShapeDtypeStruct((nc, SSM_STATE, D_INNER), F32),
                   jax.ShapeDtypeStruct((t, D_INNER), BF16)],
        scratch_shapes=[pltpu.VMEM((SSM_STATE, D_INNER), F32)],
        args=(xbc, dt, alog, zx, zx, dexp, nw), sem=("arbitrary",), comm=comm)


def _ssd_bwd(xbc, dt, alog, sprev, dy4, y, zx, dexp, nw, name, comm=None):
    t = xbc.shape[0]
    ll = SSD_L
    nc = t // ll

    def body(x_ref, dt_ref, al_ref, sp_ref, g4_ref, y_ref, z0_ref, z1_ref, d_ref, w_ref,
             dx_ref, ddt_ref, dal_ref, dz_ref, dd_ref, dnw_ref, ds_ref, colt_ref):
        @pl.when(pl.program_id(0) == 0)
        def _():
            ds_ref[...] = jnp.zeros_like(ds_ref)
            dal_ref[...] = jnp.zeros_like(dal_ref)
            dd_ref[...] = jnp.zeros_like(dd_ref)
            dnw_ref[...] = jnp.zeros_like(dnw_ref)

        dtv = dt_ref[...]
        a_neg, tril, acs, acs_t = _ssd_common(dtv, al_ref[...])
        lo = _lo_mask()
        hi = jnp.logical_not(lo)
        lane = lax.broadcasted_iota(jnp.int32, (1, LANES), 1)
        colt_ref[...] = jnp.zeros_like(colt_ref)
        rowterm = jnp.zeros((ll, LANES), F32)
        ddt_u = jnp.zeros((ll, LANES), F32)
        dlast = jnp.zeros((1, LANES), F32)

        def halves(v):
            return (jnp.sum(jnp.where(lo, v, 0.0), axis=-1, keepdims=True),
                    jnp.sum(jnp.where(hi, v, 0.0), axis=-1, keepdims=True))

        for g in range(SSM_GROUPS):
            cb0 = D_INNER + SSM_STATE * g
            cc0 = D_INNER + 512 + SSM_STATE * g
            bg = x_ref[:, cb0:cb0 + SSM_STATE].astype(BF16)
            cg = x_ref[:, cc0:cc0 + SSM_STATE].astype(BF16)
            gm = _dot(cg, bg, 1, 1)
            g0 = GROUP_W * g
            terms = [_pair_terms(acs, acs_t, dtv, 8 * g + 2 * pp, lo) for pp in range(4)]
            dtx, eac, fdec, elast = [jnp.concatenate([tt[k] for tt in terms], axis=1) for k in (2, 3, 4, 5)]
            xg = x_ref[:, g0:g0 + GROUP_W]
            u32 = xg * dtx
            ug = u32.astype(BF16)
            zg = _z_group((z0_ref, z1_ref), g)
            dg = d_ref[:, g0:g0 + GROUP_W]
            act, dact = _silu_grad(zg)
            y2 = y_ref[:, g0:g0 + GROUP_W] + dg * xg
            y3 = y2 * act
            rn = lax.rsqrt(jnp.mean(y3 * y3, axis=-1, keepdims=True) + EPS)
            y3n = y3 * rn
            gv = g4_ref[:, g0:g0 + GROUP_W]
            dyn = gv * w_ref[:, g0:g0 + GROUP_W]
            dy3 = rn * (dyn - y3n * jnp.mean(dyn * y3n, axis=-1, keepdims=True))
            dyg = dy3 * act
            dskip = dyg * dg
            dz_ref[:, g0:g0 + GROUP_W] = (dy3 * y2 * dact).astype(BF16)
            dd_ref[:, g0:g0 + GROUP_W] += jnp.sum(dyg * xg, axis=0, keepdims=True)
            dnw_ref[:, g0:g0 + GROUP_W] += jnp.sum(gv * y3n, axis=0, keepdims=True)
            dyb = dyg.astype(BF16)
            spg = sp_ref[0, :, g0:g0 + GROUP_W]
            spb = spg.astype(BF16)
            dsg = ds_ref[:, g0:g0 + GROUP_W]
            dsb = dsg.astype(BF16)
            du_st = _dot(bg, dsb, 1, 0) * fdec
            yst = _dot(cg, spb, 1, 0) * eac
            dye = (dyg * eac).astype(BF16)
            dc_st = _dot(dye, spb, 1, 1)
            db_st = _dot((xg * (fdec * dtx)).astype(BF16), dsb, 1, 1)
            ds_ref[:, g0:g0 + GROUP_W] = dsg * elast + _dot(cg, dye, 0, 0)
            qst_el = du_st * u32
            rq_el = dyg * yst - qst_el
            q_row = jnp.sum(qst_el, axis=0, keepdims=True)
            s_row = jnp.sum(dsg * spg, axis=0, keepdims=True)
            dgm = jnp.zeros((ll, ll), F32)
            for pp in range(4):
                h0 = 8 * g + 2 * pp
                cols, rows = terms[pp][0], terms[pp][1]
                sl = slice(LANES * pp, LANES * (pp + 1))
                decs = [_decay(cols[e], rows[e], tril) for e in range(2)]
                wms = [gm * d for d in decs]
                dum2 = _dot(dyb[:, sl], _two_heads_rows(ug[:, sl], lo), 1, 1)
                du = _dot(jnp.concatenate([wm.astype(BF16) for wm in wms], axis=0),
                          _two_heads_rows(dyb[:, sl], lo), 0, 0) + du_st[:, sl]
                dx_ref[:, g0 + LANES * pp:g0 + LANES * (pp + 1)] = du * dtx[:, sl] + dskip[:, sl]
                ddtu = halves(du * xg[:, sl])
                rq = halves(rq_el[:, sl])
                qs = halves(q_row[:, sl])
                ss = halves(s_row[:, sl])
                for e in range(2):
                    dum = dum2[:, ll * e:ll * (e + 1)]
                    dgm = dgm + dum * decs[e]
                    tm_ = dum * wms[e]
                    oh = lane == (h0 + e)
                    rowterm = rowterm + jnp.where(oh, jnp.sum(tm_, axis=1, keepdims=True) + rq[e], 0.0)
                    ddt_u = ddt_u + jnp.where(oh, ddtu[e], 0.0)
                    dlast = dlast + jnp.where(oh, jnp.exp(cols[e][ll - 1:ll, :]) * ss[e] + qs[e], 0.0)
                    colt_ref[h0 + e:h0 + e + 1, :] = jnp.sum(tm_, axis=0, keepdims=True)
            dgb = dgm.astype(BF16)
            dx_ref[:, cc0:cc0 + SSM_STATE] = _dot(dgb, bg, 1, 0) + dc_st
            dx_ref[:, cb0:cb0 + SSM_STATE] = _dot(dgb, cg, 0, 0) + db_st
        row_io = lax.broadcasted_iota(jnp.int32, (ll, LANES), 0)
        dacs = rowterm - colt_ref[...].T + jnp.where(row_io == ll - 1, dlast, 0.0)
        da = _dot(jnp.logical_not(tril).astype(F32) + jnp.where(
            lax.broadcasted_iota(jnp.int32, (ll, ll), 0) == lax.broadcasted_iota(jnp.int32, (ll, ll), 1), 1.0, 0.0),
            dacs, 1, 0, HI)
        ddt_ref[...] = da * a_neg + ddt_u
        dal_ref[...] += jnp.sum(da * dtv, axis=0, keepdims=True) * a_neg

    rev = lambda c: nc - 1 - c
    row = pl.BlockSpec((ll, D_INNER), lambda c: (rev(c), 0))
    vec = pl.BlockSpec((1, D_INNER), lambda c: (0, 0))
    zblk = lambda j: pl.BlockSpec((ll, 1024), lambda c: (rev(c), j))
    return _call(
        body, name=name, grid=(nc,),
        in_specs=[pl.BlockSpec((ll, XBC), lambda c: (rev(c), 0)), pl.BlockSpec((ll, LANES), lambda c: (rev(c), 0)),
                  pl.BlockSpec((1, LANES), lambda c: (0, 0)),
                  pl.BlockSpec((1, SSM_STATE, D_INNER), lambda c: (rev(c), 0, 0)), row, row, zblk(0), zblk(1), vec, vec],
        out_specs=[pl.BlockSpec((ll, XBC), lambda c: (rev(c), 0)), pl.BlockSpec((ll, LANES), lambda c: (rev(c), 0)),
                   pl.BlockSpec((1, LANES), lambda c: (0, 0)), row, vec, vec],
        out_shape=[jax.ShapeDtypeStruct((t, XBC), F32), jax.ShapeDtypeStruct((t, LANES), F32),
                   jax.ShapeDtypeStruct((1, LANES), F32), jax.ShapeDtypeStruct((t, D_INNER), BF16),
                   jax.ShapeDtypeStruct((1, D_INNER), F32), jax.ShapeDtypeStruct((1, D_INNER), F32)],
        scratch_shapes=[pltpu.VMEM((SSM_STATE, D_INNER), F32), pltpu.VMEM((LANES, ll), F32)],
        args=(xbc, dt, alog, sprev, dy4, y, zx, zx, dexp, nw), sem=("arbitrary",), comm=comm)


ADAM_TR = 512


def _sum_parts(parts, name):
    nparts, r, c = parts.shape
    tc = _pick(c, (256, 128))

    def body(p_ref, o_ref):
        g = p_ref[0].astype(F32)
        for k in range(1, nparts):
            g = g + p_ref[k].astype(F32)
        o_ref[...] = g

    return pl.pallas_call(
        body, name=name, grid=(c // tc,), in_specs=[pl.BlockSpec((nparts, r, tc), lambda j: (0, 0, j))],
        out_specs=pl.BlockSpec((r, tc), lambda j: (0, j)), out_shape=jax.ShapeDtypeStruct((r, c), F32),
        compiler_params=_cp(("parallel",)),
    )(parts)


def _adamw(parts, w, m, v, name):
    nl, r, c = w.shape
    assert len(parts) == nl
    tr = _pick(r, (256, 128, 64))
    c1 = 1.0 - ADAM_B1 ** ADAM_STEP
    c2 = 1.0 - ADAM_B2 ** ADAM_STEP

    def body(*refs):
        p_refs = refs[:nl]
        w_ref, m_ref, v_ref, g_ref, d_ref, mo_ref, vo_ref = refs[nl:]
        g = None
        for l, p_ref in enumerate(p_refs):
            s = p_ref[0].astype(F32)
            for k in range(1, p_ref.shape[0]):
                s = s + p_ref[k].astype(F32)
            g = s if g is None else jnp.where(pl.program_id(0) == l, s, g)
        mn = ADAM_B1 * m_ref[0] + (1.0 - ADAM_B1) * g
        vn = ADAM_B2 * v_ref[0] + (1.0 - ADAM_B2) * (g * g)
        g_ref[0] = g
        mo_ref[0] = mn
        vo_ref[0] = vn
        d_ref[0] = -ADAM_LR * ((mn / c1) / (jnp.sqrt(vn / c2) + ADAM_EPS) + ADAM_WD * w_ref[0])

    row = pl.BlockSpec((1, tr, c), lambda l, i: (l, i, 0))
    sd = jax.ShapeDtypeStruct((nl, r, c), F32)
    return pl.pallas_call(
        body, name=name, grid=(nl, r // tr),
        in_specs=[pl.BlockSpec((p.shape[0], tr, c), lambda l, i: (0, i, 0)) for p in parts] + [row, row, row],
        out_specs=[row, row, row, row], out_shape=[sd, sd, sd, sd], compiler_params=_cp(("parallel", "parallel")),
    )(*parts, w, m, v)


def _peers():
    mx, my, mc = lax.axis_index("x"), lax.axis_index("y"), lax.axis_index("c")
    me = 4 * mx + 2 * my + mc
    out = []
    for k in range(1, N_DEV):
        px = 1 - mx if k & 4 else mx
        py = 1 - my if k & 2 else my
        pc = 1 - mc if k & 1 else mc
        out.append(((px, py, pc), 4 * px + 2 * py + pc))
    return me, out


class _Comm:
    def __init__(self, arrs, scatters):
        self.arrs, self.scatters, self.n = list(arrs), list(scatters), len(arrs)
        self.specs = [pl.BlockSpec(memory_space=pl.ANY)] * self.n
        self.out_shape = [jax.ShapeDtypeStruct(x.shape if sc else (N_DEV,) + x.shape, x.dtype)
                          for x, sc in zip(self.arrs, self.scatters)]
        np_ = N_DEV - 1
        self.scratch = [pltpu.SemaphoreType.DMA((np_ * self.n,)), pltpu.SemaphoreType.DMA((np_ * self.n,)),
                        pltpu.SemaphoreType.DMA((self.n,))]

    def _copies(self, x_refs, o_refs, sems):
        send_sems, recv_sems, local_sems = sems
        me, peers = _peers()
        np_ = N_DEV - 1
        local, sends, recvs = [], [], []
        for a in range(self.n):
            mine = x_refs[a].at[me] if self.scatters[a] else x_refs[a]
            local.append(pltpu.make_async_copy(mine, o_refs[a].at[me], local_sems.at[a]))
        for k, (dev, idx) in enumerate(peers):
            for a in range(self.n):
                mine = x_refs[a].at[me] if self.scatters[a] else x_refs[a]
                sends.append(pltpu.make_async_remote_copy(
                    src_ref=x_refs[a].at[idx] if self.scatters[a] else x_refs[a], dst_ref=o_refs[a].at[me],
                    send_sem=send_sems.at[a * np_ + k], recv_sem=recv_sems.at[a * np_ + k], device_id=dev, device_id_type=MESH))
                recvs.append(pltpu.make_async_remote_copy(
                    src_ref=mine, dst_ref=o_refs[a].at[idx], send_sem=send_sems.at[a * np_ + k],
                    recv_sem=recv_sems.at[a * np_ + k], device_id=dev, device_id_type=MESH))
        return local, sends, recvs

    def start(self, x_refs, o_refs, sems):
        local, sends, _ = self._copies(x_refs, o_refs, sems)
        for cp in local + sends:
            cp.start()

    def wait(self, x_refs, o_refs, sems):
        local, sends, recvs = self._copies(x_refs, o_refs, sems)
        for cp in recvs:
            cp.wait_recv()
        for cp in sends:
            cp.wait_send()
        for cp in local:
            cp.wait()


class _Gather2(_Comm):
    def __init__(self, arrs):
        super().__init__(arrs, [False] * len(arrs))

    def _plan(self, x_refs, o_refs, sems):
        send_sems, recv_sems, local_sems = sems
        mx, my, mc = lax.axis_index("x"), lax.axis_index("y"), lax.axis_index("c")
        slot = lambda px, py, pc: 4 * px + 2 * py + pc
        sib = (mx, my, 1 - mc)
        chips = [(1 - mx, my), (mx, 1 - my), (1 - mx, 1 - my)]
        np_ = N_DEV - 1
        local, first, passed, arrive_first, arrive_rest = [], [], [], [], []

        def copy(a, k, src, block, to):
            return pltpu.make_async_remote_copy(
                src_ref=src, dst_ref=o_refs[a].at[block], send_sem=send_sems.at[a * np_ + k], recv_sem=recv_sems.at[a * np_ + k],
                device_id=to, device_id_type=MESH)

        for a in range(self.n):
            me = slot(mx, my, mc)
            local.append(pltpu.make_async_copy(x_refs[a], o_refs[a].at[me], local_sems.at[a]))
            first.append(copy(a, 0, x_refs[a], me, sib))
            arrive_rest.append(copy(a, 0, x_refs[a], slot(*sib), sib))
            for j, (cx, cy) in enumerate(chips):
                first.append(copy(a, 1 + j, x_refs[a], me, (cx, cy, mc)))
                arrive_first.append(copy(a, 1 + j, x_refs[a], slot(cx, cy, mc), (cx, cy, mc)))
                passed.append(copy(a, 4 + j, o_refs[a].at[slot(cx, cy, mc)], slot(cx, cy, mc), sib))
                arrive_rest.append(copy(a, 4 + j, x_refs[a], slot(cx, cy, 1 - mc), sib))
        return local, first, passed, arrive_first, arrive_rest

    def start(self, x_refs, o_refs, sems):
        local, first, _, _, _ = self._plan(x_refs, o_refs, sems)
        for cp in local + first:
            cp.start()

    def wait(self, x_refs, o_refs, sems):
        local, first, passed, arrive_first, arrive_rest = self._plan(x_refs, o_refs, sems)
        for arrived, onward in zip(arrive_first, passed):
            arrived.wait_recv()
            onward.start()
        for cp in arrive_rest:
            cp.wait_recv()
        for cp in first + passed:
            cp.wait_send()
        for cp in local:
            cp.wait()


def _call(body, *, name, grid, in_specs, out_specs, out_shape, args, scratch_shapes=(), sem=None, comm=None):
    if comm is None:
        outs = pl.pallas_call(
            body, name=name, grid=grid, in_specs=list(in_specs), out_specs=list(out_specs), out_shape=list(out_shape),
            scratch_shapes=list(scratch_shapes), compiler_params=_cp(sem),
        )(*args)
        return list(outs), []
    n_in, n_out, nc = len(in_specs), len(out_specs), comm.n
    nsteps = 1
    for g in grid:
        nsteps *= g

    def carrier(*refs):
        ins, cin = refs[:n_in], refs[n_in:n_in + nc]
        outs, cout = refs[n_in + nc:n_in + nc + n_out], refs[n_in + nc + n_out:n_in + 2 * nc + n_out]
        rest = refs[n_in + 2 * nc + n_out:]
        scratch, sems = rest[:len(rest) - 3], rest[len(rest) - 3:]
        if nsteps == 1:
            comm.start(cin, cout, sems)
            body(*ins, *outs, *scratch)
            comm.wait(cin, cout, sems)
            return
        step = 0
        for d, g in enumerate(grid):
            step = step * g + pl.program_id(d)

        @pl.when(step == 0)
        def _():
            comm.start(cin, cout, sems)

        body(*ins, *outs, *scratch)

        @pl.when(step == nsteps - 1)
        def _():
            comm.wait(cin, cout, sems)

    outs = pl.pallas_call(
        carrier, name=name, grid=grid, in_specs=list(in_specs) + comm.specs, out_specs=list(out_specs) + comm.specs,
        out_shape=list(out_shape) + comm.out_shape, scratch_shapes=list(scratch_shapes) + comm.scratch,
        compiler_params=_cp(("arbitrary",) * len(grid) if grid else None),
    )(*args, *comm.arrs)
    return list(outs[:n_out]), list(outs[n_out:])


def _exchange(comm, name):
    return _call(lambda *refs: None, name=name, grid=(), in_specs=[], out_specs=[], out_shape=[], args=[], comm=comm)[1]


def _pack(arrs, dtype, lead=()):
    nl = len(lead)
    flat = jnp.concatenate([a.astype(dtype).reshape(lead + (-1,)) for a in arrs], axis=nl)
    n = flat.shape[-1]
    rows = -(-n // (LANES * ADAM_TR)) * ADAM_TR
    flat = jnp.pad(flat, [(0, 0)] * nl + [(0, rows * LANES - n)])
    return flat.reshape(lead + (rows, LANES))


def _unpack(flat, shapes, lead=()):
    nl = len(lead)
    flat = flat.reshape(lead + (-1,))
    out, o = [], 0
    for s in shapes:
        n = 1
        for d in s:
            n *= d
        out.append(lax.slice_in_dim(flat, o, o + n, axis=nl).reshape(lead + tuple(s)))
        o += n
    return out


def _join(g, ax):
    return jnp.concatenate([g[d] for d in range(N_DEV)], axis=ax)


def _split(full, ax):
    n = full.shape[ax] // N_DEV
    return jnp.stack([lax.slice_in_dim(full, d * n, (d + 1) * n, axis=ax) for d in range(N_DEV)])


_WEIGHTS = ['norm_mix', 'norm_ffn', 'attn_w_in', 'attn_w_out', 'relpos_table', 'q_norm_a', 'k_norm_a', 'q_norm_b',
            'k_norm_b', 'sinks', 'ssm_w_in', 'ssm_conv_w', 'ssm_conv_b', 'ssm_dt_bias', 'ssm_a_log', 'ssm_d', 'ssm_norm',
            'ssm_w_out', 'ffn_w_in', 'ffn_conv_w', 'ffn_conv_b', 'ffn_w_out']
_SHARD_AX = {'attn_w_in': 2, 'attn_w_out': 1, 'ssm_w_in': 2, 'ssm_conv_w': 2, 'ssm_conv_b': 1, 'ssm_norm': 1,
             'ssm_w_out': 1, 'ffn_w_in': 2, 'ffn_conv_w': 2, 'ffn_w_out': 1}
_BIG = ['attn_w_in', 'attn_w_out', 'ssm_w_in', 'ssm_w_out', 'ffn_w_in', 'ffn_w_out']
_SMALL = ['ssm_conv_w', 'ssm_conv_b', 'ssm_norm', 'ffn_conv_w']
_AX2 = {n: _SHARD_AX[n] - 1 for n in _BIG}
_REPL = [n for n in _WEIGHTS if n not in _SHARD_AX]


def _rows8(w):
    return jnp.pad(w, ((0, 8 - w.shape[0]), (0, 0)))


def _lanes128(v):
    return jnp.pad(v, (0, LANES - v.shape[0])).reshape(1, LANES)


def _band_mask(n_prev, pad):
    cq = jnp.arange(TQ)[:, None] // CHUNK
    ck = jnp.arange(pad + TQ)[None, :] // CHUNK
    return (ck >= cq) & (ck <= cq + n_prev)


def _ffn_fwd(xin, g, w_in_t, w8, cb, tag):
    gu, h = _rms_mm(xin, g, w_in_t, 2 * D_FF, f"mm_ffn_in{tag}", BF16)
    a = _ffn_mid_fwd(gu, w8, cb, f"ffn_mid{tag}")
    return a, (h, gu, a)


def _ffn_bwd(dx, dxb, xin, g, w_in_t, w8, cb, w_out, saved, tag):
    h, gu, a = saved
    dw_out = _mm_tn(a, dxb, f"mm_ffn_dwout{tag}")
    dgu, dw8, dcb = _ffn_mid_bwd(gu, dxb, w_out, w8, cb, f"ffn_mid_bwd{tag}")
    dw_in_t = _mm_tn(dgu, h, f"mm_ffn_dwin{tag}")
    dxp, dxpb, dg = _mm_rms_bwd(dgu, w_in_t, 0, None, xin, g, dx, f"mm_ffn_dh{tag}")
    return dxp, dxpb, dg, dw_in_t, dw8[:3], dcb, dw_out


def kernel(x, norm_mix, norm_ffn, attn_w_in, attn_w_out, relpos_table, q_norm_a, k_norm_a, q_norm_b, k_norm_b, sinks, ssm_w_in, ssm_conv_w, ssm_conv_b, ssm_dt_bias, ssm_a_log, ssm_d, ssm_norm, ssm_w_out, ffn_w_in, ffn_conv_w, ffn_conv_b, ffn_w_out, loss_target, m_norm_mix, m_norm_ffn, m_attn_w_in, m_attn_w_out, m_relpos_table, m_q_norm_a, m_k_norm_a, m_q_norm_b, m_k_norm_b, m_sinks, m_ssm_w_in, m_ssm_conv_w, m_ssm_conv_b, m_ssm_dt_bias, m_ssm_a_log, m_ssm_d, m_ssm_norm, m_ssm_w_out, m_ffn_w_in, m_ffn_conv_w, m_ffn_conv_b, m_ffn_w_out, v_norm_mix, v_norm_ffn, v_attn_w_in, v_attn_w_out, v_relpos_table, v_q_norm_a, v_k_norm_a, v_q_norm_b, v_k_norm_b, v_sinks, v_ssm_w_in, v_ssm_conv_w, v_ssm_conv_b, v_ssm_dt_bias, v_ssm_a_log, v_ssm_d, v_ssm_norm, v_ssm_w_out, v_ffn_w_in, v_ffn_conv_w, v_ffn_conv_b, v_ffn_w_out):
    w = dict(norm_mix=norm_mix, norm_ffn=norm_ffn, attn_w_in=attn_w_in, attn_w_out=attn_w_out, relpos_table=relpos_table,
             q_norm_a=q_norm_a, k_norm_a=k_norm_a, q_norm_b=q_norm_b, k_norm_b=k_norm_b, sinks=sinks, ssm_w_in=ssm_w_in,
             ssm_conv_w=ssm_conv_w, ssm_conv_b=ssm_conv_b, ssm_dt_bias=ssm_dt_bias, ssm_a_log=ssm_a_log, ssm_d=ssm_d,
             ssm_norm=ssm_norm, ssm_w_out=ssm_w_out, ffn_w_in=ffn_w_in, ffn_conv_w=ffn_conv_w, ffn_conv_b=ffn_conv_b,
             ffn_w_out=ffn_w_out)
    mom = dict(norm_mix=m_norm_mix, norm_ffn=m_norm_ffn, attn_w_in=m_attn_w_in, attn_w_out=m_attn_w_out,
               relpos_table=m_relpos_table, q_norm_a=m_q_norm_a, k_norm_a=m_k_norm_a, q_norm_b=m_q_norm_b,
               k_norm_b=m_k_norm_b, sinks=m_sinks, ssm_w_in=m_ssm_w_in, ssm_conv_w=m_ssm_conv_w, ssm_conv_b=m_ssm_conv_b,
               ssm_dt_bias=m_ssm_dt_bias, ssm_a_log=m_ssm_a_log, ssm_d=m_ssm_d, ssm_norm=m_ssm_norm, ssm_w_out=m_ssm_w_out,
               ffn_w_in=m_ffn_w_in, ffn_conv_w=m_ffn_conv_w, ffn_conv_b=m_ffn_conv_b, ffn_w_out=m_ffn_w_out)
    var = dict(norm_mix=v_norm_mix, norm_ffn=v_norm_ffn, attn_w_in=v_attn_w_in, attn_w_out=v_attn_w_out,
               relpos_table=v_relpos_table, q_norm_a=v_q_norm_a, k_norm_a=v_k_norm_a, q_norm_b=v_q_norm_b,
               k_norm_b=v_k_norm_b, sinks=v_sinks, ssm_w_in=v_ssm_w_in, ssm_conv_w=v_ssm_conv_w, ssm_conv_b=v_ssm_conv_b,
               ssm_dt_bias=v_ssm_dt_bias, ssm_a_log=v_ssm_a_log, ssm_d=v_ssm_d, ssm_norm=v_ssm_norm, ssm_w_out=v_ssm_w_out,
               ffn_w_in=v_ffn_w_in, ffn_conv_w=v_ffn_conv_w, ffn_conv_b=v_ffn_conv_b, ffn_w_out=v_ffn_w_out)

    def piece(n, l):
        return (w[n][l].T if _AX2[n] == 1 else w[n][l]).astype(BF16)

    def gather_of(names_layers):
        return _Gather2([piece(n, l) for n, l in names_layers])

    def joined(got):
        return [g.reshape(-1, D_MODEL) for g in got]

    first = [('attn_w_in', 0), ('attn_w_out', 0)]
    got = _exchange(_Gather2([piece(n, l) for n, l in first] + [_pack([w[n] for n in _SMALL], F32)]), "gather_attn")
    wt_attn_in, w_attn_out = joined(got[:2])
    full = {}
    for n, g in zip(_SMALL, _unpack(got[2], [w[n].shape for n in _SMALL], lead=(N_DEV,))):
        full[n] = _join(g, _SHARD_AX[n])
    ssm_cw8 = _rows8(full['ssm_conv_w'][0])
    ssm_cb = full['ssm_conv_b']
    ssm_nw = full['ssm_norm']
    ffn_cw8 = [_rows8(full['ffn_conv_w'][l]) for l in range(2)]
    ffn_cb = [ffn_conv_b[l:l + 1] for l in range(2)]

    x0 = x[0]
    target = loss_target[0]
    t = x0.shape[0]

    g_mix0, g_mix1 = norm_mix[0:1], norm_mix[1:2]
    g_ffn0, g_ffn1 = norm_ffn[0:1], norm_ffn[1:2]
    proj, h0 = _rms_mm(x0, g_mix0, wt_attn_in, 2304, "mm_attn_in", F32)
    hn_w = jnp.concatenate([jnp.tile(v, (1, 2)) for v in (q_norm_a, k_norm_a, q_norm_b, k_norm_b)], axis=0)
    qa, kpa, vpa, qb, kpb, vpb = _headnorm_fwd(proj, hn_w, "headnorm")
    table = jnp.pad(relpos_table[0], ((0, 0), (0, REL_W - (2 * MAX_REL + 1))))
    bias_a = jnp.where(_band_mask(A_PREV, PAD_A)[None], jnp.transpose(_relpos_fwd(table, "relpos_bias"), (1, 0, 2)), NEG)
    rel_b = jnp.arange(TQ)[:, None] - (jnp.arange(PAD_B + TQ)[None, :] - PAD_B)
    slopes = 2.0 ** (-8.0 * jnp.arange(1, N_HEADS + 1, dtype=F32) / N_HEADS)
    bias_b = jnp.where(_band_mask(B_PREV, PAD_B)[None], -slopes[:, None, None] * jnp.abs(rel_b).astype(F32)[None], NEG)
    no_sinks = jnp.full((N_HEADS,), NEG, F32)
    ffn0_w, ssm_w, ffn1_w = [('ffn_w_in', 0), ('ffn_w_out', 0)], [('ssm_w_in', 0), ('ssm_w_out', 0)], [('ffn_w_in', 1), ('ffn_w_out', 1)]
    oa, stats_a, got = _attn_fwd(qa, kpa, vpa, bias_a, no_sinks, PAD_A, "attn_a", comm=gather_of(ffn0_w + ssm_w))
    wt_ffn_in0, w_ffn_out0, wt_ssm_in, w_ssm_out = joined(got)
    ob, stats_b, got = _attn_fwd(qb, kpb, vpb, bias_b, sinks[0], PAD_B, "attn_b", comm=gather_of(ffn1_w))
    wt_ffn_in1, w_ffn_out1 = joined(got)
    wt_ssm_dt = jnp.pad(wt_ssm_in[ZX:], ((0, LANES - SSM_HEADS), (0, 0)))
    x1 = _mm(oa, w_attn_out, "mm_attn_out_a", res=x0, b_rows=(0, 512))
    x1 = _mm(ob, w_attn_out, "mm_attn_out_b", res=x1, b_rows=(512, 512))
    a0, ffn0_saved = _ffn_fwd(x1, g_ffn0, wt_ffn_in0, ffn_cw8[0], ffn_cb[0], "0")
    x2 = _mm(a0, w_ffn_out0, "mm_ffn_out0", res=x1)

    zx, h2 = _rms_mm(x2, g_mix1, wt_ssm_in, ZX, "mm_ssm_in", F32)
    dtraw = _mm(h2, wt_ssm_dt, "mm_ssm_dt", trans_b=True)
    dt_bias = _lanes128(ssm_dt_bias[0])
    alog = _lanes128(ssm_a_log[0])
    dexp = jnp.repeat(ssm_d[0], HEAD_DIM).reshape(1, D_INNER)
    xbc = _ssm_pre_fwd(zx, ssm_cw8, ssm_cb, "ssm_pre")
    dt = _dt_fwd(dtraw, dt_bias, "ssm_dt")
    (y, sprev, y4), _ = _ssd_fwd(xbc, dt, alog, zx, dexp, ssm_nw, "ssd_fwd")
    x3 = _mm(y4, w_ssm_out, "mm_ssm_out", res=x2)
    a1, ffn1_saved = _ffn_fwd(x3, g_ffn1, wt_ffn_in1, ffn_cw8[1], ffn_cb[1], "1")

    dx4, dx4b, sq = _mm_loss(a1, w_ffn_out1, x3, target, "mm_ffn_out1_loss")
    loss = lax.psum(0.5 * jnp.sum(sq) / D_MODEL, ("x", "y", "c"))

    grads = {}

    def scatter_of(grads_2d):
        return _Comm([g.reshape(N_DEV, -1, D_MODEL) for g in grads_2d], [True] * len(grads_2d))

    dx3, dx3b, dg_ffn1, dwtin1, dcw1, dcb1, dwout1 = _ffn_bwd(
        dx4, dx4b, x3, g_ffn1, wt_ffn_in1, ffn_cw8[1], ffn_cb[1], w_ffn_out1, ffn1_saved, "1")

    dy4 = _mm(dx3b, w_ssm_out, "mm_ssm_dy", trans_b=True)
    dw_ssm_out = _mm_tn(y4, dx3b, "mm_ssm_dwout")
    (dxbc, ddt, dalog, dz, dd_lane, dnw), parts_ffn1 = _ssd_bwd(
        xbc, dt, alog, sprev, dy4, y, zx, dexp, ssm_nw, "ssd_bwd", comm=scatter_of([dwtin1, dwout1]))
    dxr, dcw_s, dcb_s = _ssm_pre_bwd(zx, dxbc, ssm_cw8, ssm_cb, "ssm_pre_bwd")
    ddtraw, ddtb = _dt_bwd(dtraw, dt_bias, ddt, "ssm_dt_bwd")
    dh2 = _mm(dz, wt_ssm_in, "mm_ssm_dh_z", b_rows=(0, D_INNER))
    dh2 = _mm(dxr, wt_ssm_in[D_INNER:ZX], "mm_ssm_dh_x", res=dh2)
    dwt_ssm_in = jnp.concatenate([
        _mm_tn(dz, h2, "mm_ssm_dwin_z"), _mm_tn(dxr, h2, "mm_ssm_dwin_x"),
        _mm_tn(ddtraw, h2, "mm_ssm_dwin_dt")[:SSM_HEADS]], axis=0)
    dx2, dx2b, dg_mix1 = _mm_rms_bwd(ddtraw, wt_ssm_dt, 0, dh2, x2, g_mix1, dx3, "mm_ssm_dh_dt")
    grads['ssm_conv_w'] = dcw_s[:4][None]
    grads['ssm_conv_b'] = dcb_s
    grads['ssm_norm'] = dnw
    grads['ssm_dt_bias'] = ddtb[:, :SSM_HEADS]
    grads['ssm_a_log'] = dalog[:, :SSM_HEADS]
    grads['ssm_d'] = jnp.sum(dd_lane.reshape(SSM_HEADS, HEAD_DIM), axis=1)[None]

    dx1, dx1b, dg_ffn0, dwtin0, dcw0, dcb0, dwout0 = _ffn_bwd(
        dx2, dx2b, x1, g_ffn0, wt_ffn_in0, ffn_cw8[0], ffn_cb[0], w_ffn_out0, ffn0_saved, "0")
    grads['ffn_conv_w'] = jnp.stack([dcw0, dcw1])
    grads['ffn_conv_b'] = jnp.concatenate([dcb0, dcb1], axis=0)
    grads['norm_ffn'] = jnp.concatenate([dg_ffn0, dg_ffn1], axis=0)

    do = _mm(dx1b, w_attn_out, "mm_attn_do", out_dtype=BF16, trans_b=True)
    dw_attn_out = jnp.concatenate([_mm_tn(oa, dx1b, "mm_attn_dwout_a"), _mm_tn(ob, dx1b, "mm_attn_dwout_b")], axis=0)
    (dqa, dkpa, dvpa, dbias_a, _), parts_ssm = _attn_bwd(
        qa, kpa, vpa, bias_a, no_sinks, do, stats_a, 0, PAD_A, "attn_a_bwd",
        comm=scatter_of([dwt_ssm_in, dw_ssm_out, dw_attn_out]))
    (dqb, dkpb, dvpb, _, dsink), parts_ffn0 = _attn_bwd(
        qb, kpb, vpb, bias_b, sinks[0], do, stats_b, 4, PAD_B, "attn_b_bwd", comm=scatter_of([dwtin0, dwout0]))
    grads['relpos_table'] = _relpos_bwd(jnp.transpose(dbias_a, (1, 0, 2)), "relpos_bwd")[None, :, :2 * MAX_REL + 1]
    grads['sinks'] = dsink[:, :2, 0].reshape(1, N_HEADS)
    dproj, dhn = _headnorm_bwd(proj, hn_w, dqa, dkpa, dvpa, dqb, dkpb, dvpb, "headnorm_bwd")
    dhn = dhn[:, :HEAD_DIM] + dhn[:, HEAD_DIM:]
    for k, n in enumerate(('q_norm_a', 'k_norm_a', 'q_norm_b', 'k_norm_b')):
        grads[n] = dhn[k:k + 1]
    dwt_attn_in = _mm_tn(dproj, h0, "mm_attn_dwin")
    dx0, _, dg_mix0, parts_attn_in = _mm_rms_bwd(dproj, wt_attn_in, 0, None, x0, g_mix0, dx1, "mm_attn_dh",
                                                 comm=scatter_of([dwt_attn_in]))
    grads['norm_mix'] = jnp.concatenate([dg_mix0, dg_mix1], axis=0)

    def summed_t(parts, name):
        return _sum_parts(parts, name).T[None]

    sm_shapes = [w[n].shape for n in _SMALL]
    rp_shapes = [w[n].shape for n in _REPL]
    recv = _exchange(_Comm(
        [_pack([_split(grads[n], _SHARD_AX[n]) for n in _SMALL], F32, lead=(N_DEV,)), _pack([grads[n] for n in _REPL], F32)],
        [True, False]), "exchange_small")
    big_parts = {
        'attn_w_in': [summed_t(parts_attn_in[0], "sum_attn_w_in")], 'attn_w_out': [parts_ssm[2]],
        'ssm_w_in': [summed_t(parts_ssm[0], "sum_ssm_w_in")], 'ssm_w_out': [parts_ssm[1]],
        'ffn_w_in': [summed_t(parts_ffn0[0], "sum_ffn_w_in0"), summed_t(parts_ffn1[0], "sum_ffn_w_in1")],
        'ffn_w_out': [parts_ffn0[1], parts_ffn1[1]],
    }
    res = [{}, {}, {}, {}]
    for n in _BIG:
        for kind, a in enumerate(_adamw(big_parts[n], w[n], mom[n], var[n], f"adamw_{n}")):
            res[kind][n] = a
    for names, shapes, parts in ((_SMALL, sm_shapes, recv[0]), (_REPL, rp_shapes, recv[1])):
        outs = _adamw([parts], _pack([w[n] for n in names], F32)[None], _pack([mom[n] for n in names], F32)[None],
                      _pack([var[n] for n in names], F32)[None], "adamw_" + ("small" if names is _SMALL else "replicated"))
        for kind, flat in enumerate(outs):
            for n, a in zip(names, _unpack(flat[0], shapes)):
                res[kind][n] = a
    return (loss, dx0[None], *[res[0][n] for n in _WEIGHTS], *[res[1][n] for n in _WEIGHTS],
            *[res[2][n] for n in _WEIGHTS], *[res[3][n] for n in _WEIGHTS])
```

```python
import jax
import jax.numpy as jnp
from jax import lax
from jax.experimental import pallas as pl
from jax.experimental.pallas import tpu as pltpu

F32 = jnp.float32
BF16 = jnp.bfloat16
HI = lax.Precision.HIGHEST
MESH = pl.DeviceIdType.MESH
NEG = -1e30

N_DEV = 8
D_MODEL = 1024
EPS = 1e-6
CHUNK = 64
HEAD_DIM = 64
N_HEADS = 8
A_PREV = 8
B_PREV = 2
MAX_REL = 256
TQ = 2 * CHUNK
ATT_SUB = 4
PAD_A = A_PREV * CHUNK
PAD_B = B_PREV * CHUNK
REL_W = PAD_A + TQ
D_INNER = 2048
SSM_HEADS = 32
SSM_GROUPS = 4
SSM_STATE = 128
XBC = D_INNER + 2 * SSM_GROUPS * SSM_STATE
ZX = D_INNER + XBC
D_FF = 2816
SSD_L = 128
LANES = 128
VMEM_LIMIT = 56 << 20

ADAM_LR, ADAM_B1, ADAM_B2, ADAM_EPS, ADAM_WD, ADAM_STEP = 0.001, 0.9, 0.999, 1e-08, 0.01, 10


def _cp(sem=None):
    return pltpu.CompilerParams(dimension_semantics=sem, vmem_limit_bytes=VMEM_LIMIT)


def _dot(a, b, ca=1, cb=0, prec=None):
    return lax.dot_general(a, b, (((ca,), (cb,)), ((), ())), preferred_element_type=F32, precision=prec)


def _pick(n, cands):
    for c in cands:
        if n % c == 0:
            return c
    return n


def _lo_mask():
    return lax.broadcasted_iota(jnp.int32, (1, LANES), 1) < HEAD_DIM


_TN_CHUNKS = (1408, 1536, 1152, 1024, 512, 256, 128)


TN_MAX_ROWS = 3072


def _mm_tn(a, b, name):
    kdim, m = a.shape
    n = b.shape[1]
    assert b.shape[0] == kdim, (a.shape, b.shape)
    mb = m if m <= TN_MAX_ROWS else m // 2
    tn = _pick(n, _TN_CHUNKS)
    tk = _pick(kdim, (512, 256, 128))
    nk = kdim // tk

    def body(a_ref, b_ref, o_ref, acc):
        k = pl.program_id(1)

        @pl.when(k == 0)
        def _():
            acc[...] = jnp.zeros_like(acc)

        av = a_ref[...]
        for c in range(0, n, tn):
            acc[:, c:c + tn] += _dot(av, b_ref[:, c:c + tn], 0, 0)

        @pl.when(k == nk - 1)
        def _():
            o_ref[...] = acc[...].astype(BF16)

    return pl.pallas_call(
        body, name=name, grid=(m // mb, nk),
        in_specs=[pl.BlockSpec((tk, mb), lambda j, k: (k, j)), pl.BlockSpec((tk, n), lambda j, k: (k, 0))],
        out_specs=pl.BlockSpec((mb, n), lambda j, k: (j, 0)), out_shape=jax.ShapeDtypeStruct((m, n), BF16),
        scratch_shapes=[pltpu.VMEM((mb, n), F32)], compiler_params=_cp(("parallel", "arbitrary")),
    )(a, b)


def _mm(a, b, name, out_dtype=F32, res=None, trans_b=False, b_rows=None):
    m, kdim = a.shape
    if b_rows is None:
        b_rows = (0, b.shape[0])
    off, rows = b_rows
    n = rows if trans_b else b.shape[1]
    assert (b.shape[1] if trans_b else rows) == kdim and off % rows == 0, (a.shape, b.shape, b_rows)
    tn = _pick(n, _TN_CHUNKS)
    tm = _pick(m, (256, 128) if n > 2304 else (512, 256, 128))

    def body(*refs):
        if res is None:
            a_ref, b_ref, o_ref = refs
        else:
            a_ref, b_ref, r_ref, o_ref = refs
        av = a_ref[...]
        for c in range(0, n, tn):
            r = _dot(av, b_ref[c:c + tn, :], 1, 1) if trans_b else _dot(av, b_ref[:, c:c + tn], 1, 0)
            if res is not None:
                r = r + r_ref[:, c:c + tn]
            o_ref[:, c:c + tn] = r.astype(out_dtype)

    in_specs = [pl.BlockSpec((tm, kdim), lambda i: (i, 0)), pl.BlockSpec((rows, b.shape[1]), lambda i: (off // rows, 0))]
    args = [a, b]
    if res is not None:
        in_specs.append(pl.BlockSpec((tm, n), lambda i: (i, 0)))
        args.append(res)
    return pl.pallas_call(
        body, name=name, grid=(m // tm,), in_specs=in_specs, out_specs=pl.BlockSpec((tm, n), lambda i: (i, 0)),
        out_shape=jax.ShapeDtypeStruct((m, n), out_dtype), compiler_params=_cp(("parallel",)),
    )(*args)


def _rms_mm(x, g, bt, n, name, out_dtype):
    t, d = x.shape
    tn = _pick(n, _TN_CHUNKS)
    tm = _pick(t, (256, 128))

    def body(x_ref, g_ref, b_ref, o_ref, h_ref):
        xv = x_ref[...]
        r = lax.rsqrt(jnp.mean(xv * xv, axis=-1, keepdims=True) + EPS)
        h = (xv * r * g_ref[...]).astype(BF16)
        h_ref[...] = h
        for c in range(0, n, tn):
            o_ref[:, c:c + tn] = _dot(h, b_ref[c:c + tn, :], 1, 1).astype(out_dtype)

    row = pl.BlockSpec((tm, d), lambda i: (i, 0))
    return pl.pallas_call(
        body, name=name, grid=(t // tm,),
        in_specs=[row, pl.BlockSpec((1, d), lambda i: (0, 0)), pl.BlockSpec(bt.shape, lambda i: (0, 0))],
        out_specs=[pl.BlockSpec((tm, n), lambda i: (i, 0)), row],
        out_shape=[jax.ShapeDtypeStruct((t, n), out_dtype), jax.ShapeDtypeStruct((t, d), BF16)],
        compiler_params=_cp(("parallel",)),
    )(x, g, bt)


def _mm_rms_bwd(a, b, b_off, dh_prev, x, g, dres, name, comm=None):
    t, d = x.shape
    kdim = a.shape[1]
    assert b_off % kdim == 0 and b.shape[1] == d, (a.shape, b.shape, b_off)
    tm = _pick(t, (256, 128))

    def body(*refs):
        if dh_prev is None:
            a_ref, b_ref, x_ref, g_ref, dr_ref, dx_ref, dxb_ref, dg_ref = refs
            dhv = _dot(a_ref[...], b_ref[...], 1, 0)
        else:
            a_ref, b_ref, p_ref, x_ref, g_ref, dr_ref, dx_ref, dxb_ref, dg_ref = refs
            dhv = _dot(a_ref[...], b_ref[...], 1, 0) + p_ref[...]
        xv = x_ref[...]
        r = lax.rsqrt(jnp.mean(xv * xv, axis=-1, keepdims=True) + EPS)
        xh = xv * r
        dxh = dhv * g_ref[...]
        dx = dr_ref[...] + r * (dxh - xh * jnp.mean(dxh * xh, axis=-1, keepdims=True))
        dx_ref[...] = dx
        dxb_ref[...] = dx.astype(BF16)

        @pl.when(pl.program_id(0) == 0)
        def _():
            dg_ref[...] = jnp.zeros_like(dg_ref)

        dg_ref[...] += jnp.sum(dhv * xh, axis=0, keepdims=True)

    row = pl.BlockSpec((tm, d), lambda i: (i, 0))
    vec = pl.BlockSpec((1, d), lambda i: (0, 0))
    in_specs = [pl.BlockSpec((tm, kdim), lambda i: (i, 0)), pl.BlockSpec((kdim, d), lambda i: (b_off // kdim, 0))]
    args = [a, b]
    if dh_prev is not None:
        in_specs.append(row)
        args.append(dh_prev)
    outs, got = _call(
        body, name=name, grid=(t // tm,), in_specs=in_specs + [row, vec, row], out_specs=[row, row, vec],
        out_shape=[jax.ShapeDtypeStruct((t, d), F32), jax.ShapeDtypeStruct((t, d), BF16), jax.ShapeDtypeStruct((1, d), F32)],
        args=(*args, x, g, dres), sem=("arbitrary",), comm=comm)
    return (*outs, got) if comm is not None else tuple(outs)


def _mm_loss(a, b, res, target, name):
    t, kdim = a.shape
    d = b.shape[1]
    tm = _pick(t, (512, 256, 128))

    def body(a_ref, b_ref, r_ref, t_ref, dy_ref, dyb_ref, acc_ref):
        @pl.when(pl.program_id(0) == 0)
        def _():
            acc_ref[...] = jnp.zeros_like(acc_ref)

        err = _dot(a_ref[...], b_ref[...], 1, 0) + r_ref[...] - t_ref[...]
        dy = err * (1.0 / d)
        dy_ref[...] = dy
        dyb_ref[...] = dy.astype(BF16)
        acc_ref[...] += jnp.sum(err * err, axis=0, keepdims=True)

    row = pl.BlockSpec((tm, d), lambda i: (i, 0))
    vec = pl.BlockSpec((1, d), lambda i: (0, 0))
    return pl.pallas_call(
        body, name=name, grid=(t // tm,),
        in_specs=[pl.BlockSpec((tm, kdim), lambda i: (i, 0)), pl.BlockSpec((kdim, d), lambda i: (0, 0)), row, row],
        out_specs=[row, row, vec],
        out_shape=[jax.ShapeDtypeStruct((t, d), F32), jax.ShapeDtypeStruct((t, d), BF16), jax.ShapeDtypeStruct((1, d), F32)],
        compiler_params=_cp(("arbitrary",)),
    )(a, b, res, target)


def _head_sums(v):
    ri = lax.broadcasted_iota(jnp.int32, (LANES, LANES), 0) // HEAD_DIM
    ci = lax.broadcasted_iota(jnp.int32, (LANES, LANES), 1) // HEAD_DIM
    ones = (ri == ci).astype(BF16)
    hi = v.astype(BF16)
    lo_part = (v - hi.astype(F32)).astype(BF16)
    return _dot(hi, ones, 1, 0) + _dot(lo_part, ones, 1, 0)


def _head_rms(xs, w, lo):
    r = lax.rsqrt(_head_sums(xs * xs) * (1.0 / HEAD_DIM) + EPS)
    return xs * r, r


def _head_rms_bwd(xs, w, dy, lo):
    xh, r = _head_rms(xs, w, lo)
    dxh = dy * w
    mm = _head_sums(dxh * xh) * (1.0 / HEAD_DIM)
    return r * (dxh - xh * mm), dy * xh


_QSCALE = HEAD_DIM ** -0.5


def _headnorm_fwd(proj, ws, name):
    t = proj.shape[0]
    tm = TQ
    lead = PAD_A // tm
    leadb = PAD_B // tm

    def body(p_ref, w_ref, qa_ref, ka_ref, va_ref, qb_ref, kb_ref, vb_ref):
        data = pl.program_id(0) >= lead
        lo = _lo_mask()

        def put(ref, c, val):
            ref[:, c:c + val.shape[1]] = jnp.where(data, val, 0.0).astype(BF16)

        def per_query_head(slab):
            other = pltpu.roll(slab, HEAD_DIM, 1)
            e0, e1 = jnp.where(lo, slab, other), jnp.where(lo, other, slab)
            return jnp.concatenate([e0, e0, e1, e1], axis=1)

        for s in range(4):
            c = LANES * s
            xh, _ = _head_rms(p_ref[:, c:c + LANES], None, lo)
            qa_ref[:, c:c + LANES] = (xh * w_ref[0:1, :] * _QSCALE).astype(BF16)
            xh, _ = _head_rms(p_ref[:, 512 + c:512 + c + LANES], None, lo)
            put(ka_ref, c, xh * w_ref[1:2, :])
            xh, _ = _head_rms(p_ref[:, 1536 + c:1536 + c + LANES], None, lo)
            qb_ref[:, c:c + LANES] = (xh * w_ref[2:3, :] * _QSCALE).astype(BF16)
        put(va_ref, 0, p_ref[:, 1024:1536])
        xh, _ = _head_rms(p_ref[:, 2048:2176], None, lo)
        put(kb_ref, 0, per_query_head(xh * w_ref[3:4, :]))
        put(vb_ref, 0, per_query_head(p_ref[:, 2176:2304]))

    src = lambda i: jnp.maximum(i - lead, 0)
    wide = pl.BlockSpec((tm, 512), lambda i: (src(i), 0))
    pad_a = pl.BlockSpec((tm, 512), lambda i: (i, 0))
    pad_b = pl.BlockSpec((tm, 512), lambda i: (jnp.maximum(i - lead + leadb, 0), 0))
    sd = lambda rows: jax.ShapeDtypeStruct((rows, 512), BF16)
    return pl.pallas_call(
        body, name=name, grid=(t // tm + lead,),
        in_specs=[pl.BlockSpec((tm, 2304), lambda i: (src(i), 0)), pl.BlockSpec((4, LANES), lambda i: (0, 0))],
        out_specs=[wide, pad_a, pad_a, wide, pad_b, pad_b],
        out_shape=[sd(t), sd(t + PAD_A), sd(t + PAD_A), sd(t), sd(t + PAD_B), sd(t + PAD_B)],
        compiler_params=_cp(("arbitrary",)),
    )(proj, ws)


def _headnorm_bwd(proj, ws, dqa, dkpa, dvpa, dqb, dkpb, dvpb, name):
    t = proj.shape[0]
    tm = TQ
    offa, offb = PAD_A // tm, PAD_B // tm

    def body(p_ref, w_ref, dqa_ref, dka_ref, dva_ref, dqb_ref, dkb_ref, dvb_ref, dp_ref, dw_ref):
        i = pl.program_id(0)
        lo = _lo_mask()

        @pl.when(i == 0)
        def _():
            dw_ref[...] = jnp.zeros_like(dw_ref)

        acc = [jnp.zeros((1, LANES), F32) for _ in range(4)]
        for s in range(4):
            c = LANES * s
            dx, dwl = _head_rms_bwd(p_ref[:, c:c + LANES], w_ref[0:1, :], dqa_ref[:, c:c + LANES] * _QSCALE, lo)
            dp_ref[:, c:c + LANES] = dx.astype(BF16)
            acc[0] += jnp.sum(dwl, axis=0, keepdims=True)
            dx, dwl = _head_rms_bwd(p_ref[:, 512 + c:512 + c + LANES], w_ref[1:2, :], dka_ref[:, c:c + LANES], lo)
            dp_ref[:, 512 + c:512 + c + LANES] = dx.astype(BF16)
            acc[1] += jnp.sum(dwl, axis=0, keepdims=True)
            dx, dwl = _head_rms_bwd(p_ref[:, 1536 + c:1536 + c + LANES], w_ref[2:3, :], dqb_ref[:, c:c + LANES] * _QSCALE, lo)
            dp_ref[:, 1536 + c:1536 + c + LANES] = dx.astype(BF16)
            acc[2] += jnp.sum(dwl, axis=0, keepdims=True)
        dp_ref[:, 1024:1536] = dva_ref[...].astype(BF16)

        def group_sum(ref):
            s0 = ref[:, 0:128] + ref[:, 128:256]
            s1 = ref[:, 256:384] + ref[:, 384:512]
            s0 = s0 + pltpu.roll(s0, HEAD_DIM, 1)
            s1 = s1 + pltpu.roll(s1, HEAD_DIM, 1)
            return jnp.where(lo, s0, s1)

        dx, dwl = _head_rms_bwd(p_ref[:, 2048:2176], w_ref[3:4, :], group_sum(dkb_ref), lo)
        dp_ref[:, 2048:2176] = dx.astype(BF16)
        acc[3] += jnp.sum(dwl, axis=0, keepdims=True)
        dp_ref[:, 2176:2304] = group_sum(dvb_ref).astype(BF16)
        for n in range(4):
            dw_ref[n:n + 1, :] += acc[n]

    wide = pl.BlockSpec((tm, 512), lambda i: (i, 0))
    pa = pl.BlockSpec((tm, 512), lambda i: (i + offa, 0))
    pb = pl.BlockSpec((tm, 512), lambda i: (i + offb, 0))
    return pl.pallas_call(
        body, name=name, grid=(t // tm,),
        in_specs=[pl.BlockSpec((tm, 2304), lambda i: (i, 0)), pl.BlockSpec((4, LANES), lambda i: (0, 0)),
                  wide, pa, pa, wide, pb, pb],
        out_specs=[pl.BlockSpec((tm, 2304), lambda i: (i, 0)), pl.BlockSpec((4, LANES), lambda i: (0, 0))],
        out_shape=[jax.ShapeDtypeStruct((t, 2304), BF16), jax.ShapeDtypeStruct((4, LANES), F32)],
        compiler_params=_cp(("arbitrary",)),
    )(proj, ws, dqa, dkpa, dvpa, dqb, dkpb, dvpb)


ROLL_W = 1024


def _rel_onehot():
    r_io = lax.broadcasted_iota(jnp.int32, (REL_W, ROLL_W), 0)
    m_io = lax.broadcasted_iota(jnp.int32, (REL_W, ROLL_W), 1)
    return (r_io == jnp.clip(REL_W - 1 - m_io, -MAX_REL, MAX_REL) + MAX_REL).astype(F32)


def _relpos_fwd(table, name):
    def body(t_ref, o_ref):
        rr = _dot(t_ref[...], _rel_onehot(), 1, 0, HI)

        def step(q, c):
            o_ref[q] = pltpu.roll(rr, (ROLL_W - (TQ - 1) + q) % ROLL_W, 1)[:, :REL_W]
            return c

        lax.fori_loop(0, TQ, step, 0)

    return pl.pallas_call(
        body, name=name, out_shape=jax.ShapeDtypeStruct((TQ, N_HEADS, REL_W), F32),
        in_specs=[pl.BlockSpec(memory_space=pltpu.VMEM)], out_specs=pl.BlockSpec(memory_space=pltpu.VMEM),
        compiler_params=_cp(),
    )(table)


def _relpos_bwd(dbias_t, name):
    def body(d_ref, o_ref):
        def step(q, acc):
            row = jnp.concatenate([d_ref[q], jnp.zeros((N_HEADS, ROLL_W - REL_W), F32)], axis=1)
            return acc + pltpu.roll(row, TQ - 1 - q, 1)

        drr = lax.fori_loop(0, TQ, step, jnp.zeros((N_HEADS, ROLL_W), F32))
        o_ref[...] = _dot(drr, _rel_onehot(), 1, 1, HI)

    return pl.pallas_call(
        body, name=name, out_shape=jax.ShapeDtypeStruct((N_HEADS, REL_W), F32),
        in_specs=[pl.BlockSpec(memory_space=pltpu.VMEM)], out_specs=pl.BlockSpec(memory_space=pltpu.VMEM),
        compiler_params=_cp(),
    )(dbias_t)


def _attn_scores(qe, kw, bias, kvalid):
    return jnp.where(kvalid, _dot(qe, kw, 1, 1) + bias, NEG)


def _stat_cols(stats, e):
    return stats[:, 64 * e:64 * e + 1], stats[:, 64 * e + 32:64 * e + 33]


def _attn_fwd(q, kp, vp, bias, sinks, pad, name, comm=None):
    t, hd = q.shape
    w = pad + TQ

    def body(sink_ref, q_ref, k_ref, v_ref, b_ref, o_ref, st_ref):
        hp, i = pl.program_id(0), pl.program_id(1)
        lo = _lo_mask()
        lane = lax.broadcasted_iota(jnp.int32, (1, LANES), 1)
        for j in range(ATT_SUB):
            start = pl.multiple_of((i * ATT_SUB + j) * TQ, TQ)
            qv = q_ref[TQ * j:TQ * (j + 1), :]
            kw = k_ref[pl.ds(start, w), :]
            vw = v_ref[pl.ds(start, w), :]
            kvalid = (start + lax.broadcasted_iota(jnp.int32, (1, w), 1)) >= pad
            outs, ms, ls = [], [], []
            for e in range(2):
                sel = lo if e == 0 else jnp.logical_not(lo)
                qe = jnp.where(sel, qv, jnp.zeros_like(qv))
                snk = sink_ref[2 * hp + e]
                s = _attn_scores(qe, kw, b_ref[e], kvalid)
                m = jnp.maximum(jnp.max(s, axis=-1, keepdims=True), snk)
                acc = _dot(jnp.exp(s - m).astype(BF16), jnp.where(sel, vw, jnp.ones_like(vw)), 1, 0)
                denom = acc[:, 64 * (1 - e):64 * (1 - e) + 1] + jnp.exp(snk - m)
                outs.append(acc * (1.0 / denom))
                ms.append(m)
                ls.append(denom)
            o_ref[TQ * j:TQ * (j + 1), :] = jnp.where(lo, outs[0], outs[1]).astype(BF16)
            st_ref[TQ * j:TQ * (j + 1), :] = jnp.where(lane < 32, ms[0], jnp.where(lane < 64, ls[0],
                                                                                 jnp.where(lane < 96, ms[1], ls[1])))

    full = pl.BlockSpec((t + pad, LANES), lambda h, i: (0, h))
    tile = pl.BlockSpec((ATT_SUB * TQ, LANES), lambda h, i: (i, h))
    (o, stats), got = _call(
        body, name=name, grid=(hd // LANES, t // (ATT_SUB * TQ)),
        in_specs=[pl.BlockSpec(memory_space=pltpu.SMEM), tile, full, full, pl.BlockSpec((2, TQ, w), lambda h, i: (h, 0, 0))],
        out_specs=[tile, tile], out_shape=[jax.ShapeDtypeStruct((t, hd), BF16), jax.ShapeDtypeStruct((t, hd), F32)],
        args=(sinks, q, kp, vp, bias), sem=("parallel", "arbitrary"), comm=comm)
    return o, stats, got


def _attn_bwd(q, kp, vp, bias, sinks, do, stats, col_off, pad, name, comm=None):
    t, hd = q.shape
    w = pad + TQ
    nhp = hd // LANES

    def body(sink_ref, q_ref, k_ref, v_ref, b_ref, do_ref, st_ref, dq_ref, dk_ref, dv_ref, db_ref, ds_ref):
        hp, i = pl.program_id(0), pl.program_id(1)

        @pl.when(i == 0)
        def _():
            dk_ref[...] = jnp.zeros_like(dk_ref)
            dv_ref[...] = jnp.zeros_like(dv_ref)
            db_ref[...] = jnp.zeros_like(db_ref)
            ds_ref[...] = jnp.zeros_like(ds_ref)

        lo = _lo_mask()
        row8 = lax.broadcasted_iota(jnp.int32, (8, LANES), 0)
        dbias = [None, None]
        dsink = jnp.zeros((8, LANES), F32)
        for j in range(ATT_SUB):
            start = pl.multiple_of((i * ATT_SUB + j) * TQ, TQ)
            qv = q_ref[TQ * j:TQ * (j + 1), :]
            dov = do_ref[TQ * j:TQ * (j + 1), :]
            kw = k_ref[pl.ds(start, w), :]
            vw = v_ref[pl.ds(start, w), :]
            kvalid = (start + lax.broadcasted_iota(jnp.int32, (1, w), 1)) >= pad
            stats = st_ref[TQ * j:TQ * (j + 1), :]
            dqs, dkw, dvw = [], None, None
            for e in range(2):
                sel = lo if e == 0 else jnp.logical_not(lo)
                qe = jnp.where(sel, qv, jnp.zeros_like(qv))
                doe = jnp.where(sel, dov, jnp.zeros_like(dov))
                m, denom = _stat_cols(stats, e)
                inv = 1.0 / denom
                p = jnp.exp(_attn_scores(qe, kw, b_ref[e], kvalid) - m) * inv
                psink = jnp.exp(sink_ref[2 * hp + e] - m) * inv
                dp = _dot(doe, vw, 1, 1)
                delta = jnp.sum(p * dp, axis=-1, keepdims=True)
                ds = p * (dp - delta)
                dbias[e] = ds if dbias[e] is None else dbias[e] + ds
                dsink = dsink + jnp.where(row8 == e, jnp.sum(-psink * delta, axis=0, keepdims=True), 0.0)
                dsb = ds.astype(BF16)
                dqs.append(_dot(dsb, kw, 1, 0))
                dk_e = _dot(dsb, qe, 0, 0)
                dv_e = _dot(p.astype(BF16), doe, 0, 0)
                dkw = dk_e if dkw is None else dkw + dk_e
                dvw = dv_e if dvw is None else dvw + dv_e
            dq_ref[TQ * j:TQ * (j + 1), :] = jnp.where(lo, dqs[0], dqs[1])
            dk_ref[pl.ds(start, w), :] += dkw
            dv_ref[pl.ds(start, w), :] += dvw
        for e in range(2):
            db_ref[e] += dbias[e]
        ds_ref[0] += dsink

    full = pl.BlockSpec((t + pad, LANES), lambda h, i: (0, h))
    tile = pl.BlockSpec((ATT_SUB * TQ, LANES), lambda h, i: (i, h))
    btile = pl.BlockSpec((2, TQ, w), lambda h, i: (h, 0, 0))
    return _call(
        body, name=name, grid=(nhp, t // (ATT_SUB * TQ)),
        in_specs=[pl.BlockSpec(memory_space=pltpu.SMEM), tile, full, full, btile,
                  pl.BlockSpec((ATT_SUB * TQ, LANES), lambda h, i: (i, h + col_off)), tile],
        out_specs=[tile, full, full, btile, pl.BlockSpec((1, 8, LANES), lambda h, i: (h, 0, 0))],
        out_shape=[jax.ShapeDtypeStruct((t, hd), F32), jax.ShapeDtypeStruct((t + pad, hd), F32),
                   jax.ShapeDtypeStruct((t + pad, hd), F32), jax.ShapeDtypeStruct((N_HEADS, TQ, w), F32),
                   jax.ShapeDtypeStruct((nhp, 8, LANES), F32)],
        args=(sinks, q, kp, vp, bias, do, stats), sem=("parallel", "arbitrary"), comm=comm)


def _halo_prev(tm):
    return lambda i: jnp.maximum(i * (tm // 8) - 1, 0)


def _halo_next(tm, t):
    return lambda i: jnp.minimum((i + 1) * (tm // 8), t // 8 - 1)


def _taps_prev(tile, halo, ktaps, first):
    tm = tile.shape[0]
    ext = jnp.concatenate([jnp.where(first, 0.0, halo), tile], axis=0)
    return [tile] + [pltpu.roll(ext, s, 0)[8:8 + tm] for s in range(1, ktaps)]


def _conv_apply(taps, w_ref, ktaps):
    out = taps[0] * w_ref[ktaps - 1:ktaps, :]
    for s in range(1, ktaps):
        out = out + taps[s] * w_ref[ktaps - 1 - s:ktaps - s, :]
    return out


def _sigmoid(x):
    return jax.nn.sigmoid(x)


def _silu_grad(x):
    sg = _sigmoid(x)
    return x * sg, sg * (1.0 + x * (1.0 - sg))


FFN_HALO = 16
FFN_BT = 256
FFN_BC = 1408


def _ffn_in_mid(x, g, wt, w8, b, name):
    t, d = x.shape
    f = D_FF
    tm = FFN_BT

    def body(x_ref, g_ref, b_ref, w_ref, cb_ref, gu_ref, h_ref, a_ref, halo_ref):
        @pl.when(pl.program_id(0) == 0)
        def _():
            halo_ref[...] = jnp.zeros_like(halo_ref)

        xv = x_ref[...]
        r = lax.rsqrt(jnp.mean(xv * xv, axis=-1, keepdims=True) + EPS)
        h = (xv * r * g_ref[...]).astype(BF16)
        h_ref[...] = h
        for c in range(0, f, FFN_BC):
            cs = slice(c, c + FFN_BC)
            gate = _dot(h, b_ref[c:c + FFN_BC, :], 1, 1).astype(BF16)
            up = _dot(h, b_ref[f + c:f + c + FFN_BC, :], 1, 1).astype(BF16)
            gu_ref[:, cs] = gate
            gu_ref[:, f + c:f + c + FFN_BC] = up
            gf = gate.astype(F32)
            ext = jnp.concatenate([halo_ref[:, cs], gf], axis=0)
            gc = (cb_ref[:, cs] + gf * w_ref[2:3, cs] + pltpu.roll(ext, 1, 0)[8:] * w_ref[1:2, cs]
                  + pltpu.roll(ext, 2, 0)[8:] * w_ref[0:1, cs])
            a_ref[:, cs] = (gc * _sigmoid(gc) * up.astype(F32)).astype(BF16)
            halo_ref[:, cs] = gf[tm - 8:]

    row = pl.BlockSpec((tm, d), lambda i: (i, 0))
    return pl.pallas_call(
        body, name=name, grid=(t // tm,),
        in_specs=[row, pl.BlockSpec((1, d), lambda i: (0, 0)), pl.BlockSpec((2 * f, d), lambda i: (0, 0)),
                  pl.BlockSpec((8, f), lambda i: (0, 0)), pl.BlockSpec((1, f), lambda i: (0, 0))],
        out_specs=[pl.BlockSpec((tm, 2 * f), lambda i: (i, 0)), row, pl.BlockSpec((tm, f), lambda i: (i, 0))],
        out_shape=[jax.ShapeDtypeStruct((t, 2 * f), BF16), jax.ShapeDtypeStruct((t, d), BF16), jax.ShapeDtypeStruct((t, f), BF16)],
        scratch_shapes=[pltpu.VMEM((8, f), F32)], compiler_params=_cp(("arbitrary",)),
    )(x, g, wt, w8, b)


def _ffn_mid_bwd(gu, dxb, w_out, w8, b, name):
    t, d = dxb.shape
    f = D_FF
    tm, hr = FFN_BT, FFN_HALO
    nt = t // tm
    n = tm + hr

    def body(g_ref, u_ref, gp_ref, gn_ref, un_ref, dx_ref, dxn_ref, wo_ref, w_ref, b_ref, dgu_ref, dw_ref, db_ref):
        i = pl.program_id(0)
        first, last = i == 0, i == nt - 1

        @pl.when(first)
        def _():
            dw_ref[...] = jnp.zeros_like(dw_ref)
            db_ref[...] = jnp.zeros_like(db_ref)

        dxe = jnp.concatenate([dx_ref[...], dxn_ref[...]], axis=0)
        row = lax.broadcasted_iota(jnp.int32, (n, 1), 0)
        keep = (row < tm) | jnp.logical_not(last)
        for c in range(0, f, FFN_BC):
            cs = slice(c, c + FFN_BC)
            ext = jnp.concatenate([jnp.where(first, 0.0, gp_ref[:, cs].astype(F32)), g_ref[:, cs].astype(F32),
                                   gn_ref[:, cs].astype(F32)], axis=0)
            taps = [ext[hr:]] + [pltpu.roll(ext, s, 0)[hr:] for s in (1, 2)]
            gc = b_ref[:, cs] + taps[0] * w_ref[2:3, cs] + taps[1] * w_ref[1:2, cs] + taps[2] * w_ref[0:1, cs]
            act, dact = _silu_grad(gc)
            da = _dot(dxe, wo_ref[cs, :], 1, 1)
            up = jnp.concatenate([u_ref[:, cs], un_ref[:, cs]], axis=0).astype(F32)
            dgc = jnp.where(keep, da * up * dact, 0.0)
            dgu_ref[:, f + c:f + c + FFN_BC] = (da[:tm] * act[:tm]).astype(BF16)
            dgu_ref[:, cs] = (dgc[:tm] * w_ref[2:3, cs] + pltpu.roll(dgc, n - 1, 0)[:tm] * w_ref[1:2, cs]
                              + pltpu.roll(dgc, n - 2, 0)[:tm] * w_ref[0:1, cs]).astype(BF16)
            db_ref[:, cs] += jnp.sum(dgc[:tm], axis=0, keepdims=True)
            for s in range(3):
                dw_ref[2 - s:3 - s, cs] += jnp.sum(dgc[:tm] * taps[s][:tm], axis=0, keepdims=True)

    r = tm // hr
    prev = lambda i: jnp.maximum(i * r - 1, 0)
    nxt_blk = lambda i: jnp.minimum((i + 1) * r, t // hr - 1)
    row_f = pl.BlockSpec((tm, f), lambda i: (i, 0))
    return pl.pallas_call(
        body, name=name, grid=(nt,),
        in_specs=[row_f, pl.BlockSpec((tm, f), lambda i: (i, 1)),
                  pl.BlockSpec((hr, f), lambda i: (prev(i), 0)), pl.BlockSpec((hr, f), lambda i: (nxt_blk(i), 0)),
                  pl.BlockSpec((hr, f), lambda i: (nxt_blk(i), 1)),
                  pl.BlockSpec((tm, d), lambda i: (i, 0)), pl.BlockSpec((hr, d), lambda i: (nxt_blk(i), 0)),
                  pl.BlockSpec((f, d), lambda i: (0, 0)),
                  pl.BlockSpec((8, f), lambda i: (0, 0)), pl.BlockSpec((1, f), lambda i: (0, 0))],
        out_specs=[pl.BlockSpec((tm, 2 * f), lambda i: (i, 0)), pl.BlockSpec((8, f), lambda i: (0, 0)),
                   pl.BlockSpec((1, f), lambda i: (0, 0))],
        out_shape=[jax.ShapeDtypeStruct((t, 2 * f), BF16), jax.ShapeDtypeStruct((8, f), F32), jax.ShapeDtypeStruct((1, f), F32)],
        compiler_params=_cp(("arbitrary",)),
    )(gu, gu, gu, gu, gu, dxb, dxb, w_out, w8, b)


PRE_TM = 256
PRE_TC = 1024


def _ssm_pre_fwd(zx, w8, b, name):
    t = zx.shape[0]
    tm, tc = PRE_TM, PRE_TC
    off = D_INNER // tc

    def body(x_ref, h_ref, w_ref, b_ref, o_ref):
        first = pl.program_id(0) == 0
        c = _conv_apply(_taps_prev(x_ref[...], h_ref[...], 4, first), w_ref, 4) + b_ref[...]
        o_ref[...] = c * _sigmoid(c)

    hp = _halo_prev(tm)
    return pl.pallas_call(
        body, name=name, grid=(t // tm, XBC // tc),
        in_specs=[pl.BlockSpec((tm, tc), lambda i, j: (i, j + off)), pl.BlockSpec((8, tc), lambda i, j: (hp(i), j + off)),
                  pl.BlockSpec((8, tc), lambda i, j: (0, j)), pl.BlockSpec((1, tc), lambda i, j: (0, j))],
        out_specs=pl.BlockSpec((tm, tc), lambda i, j: (i, j)), out_shape=jax.ShapeDtypeStruct((t, XBC), F32),
        compiler_params=_cp(("parallel", "parallel")),
    )(zx, zx, w8, b)


def _ssm_pre_bwd(zx, dxbc, w8, b, name):
    t = zx.shape[0]
    tm, tc = PRE_TM, PRE_TC
    off = D_INNER // tc
    nt = t // tm
    n = tm + 8

    def body(x_ref, xp_ref, xn_ref, d_ref, dn_ref, w_ref, b_ref, o_ref, dw_ref, db_ref):
        i = pl.program_id(1)
        first, last = i == 0, i == nt - 1

        @pl.when(first)
        def _():
            dw_ref[...] = jnp.zeros_like(dw_ref)
            db_ref[...] = jnp.zeros_like(db_ref)

        ext = jnp.concatenate([jnp.where(first, 0.0, xp_ref[...]), x_ref[...], xn_ref[...]], axis=0)
        taps = [ext[8:8 + n]] + [pltpu.roll(ext, s, 0)[8:8 + n] for s in (1, 2, 3)]
        c = _conv_apply(taps, w_ref, 4) + b_ref[...]
        _, dact = _silu_grad(c)
        row = lax.broadcasted_iota(jnp.int32, (n, 1), 0)
        dc = jnp.where((row < tm) | jnp.logical_not(last), jnp.concatenate([d_ref[...], dn_ref[...]], axis=0) * dact, 0.0)
        nxt = [dc[:tm]] + [pltpu.roll(dc, n - s, 0)[:tm] for s in (1, 2, 3)]
        o_ref[...] = _conv_apply(nxt, w_ref, 4).astype(BF16)
        db_ref[...] += jnp.sum(dc[:tm], axis=0, keepdims=True)
        for s in range(4):
            dw_ref[3 - s:4 - s, :] += jnp.sum(dc[:tm] * taps[s][:tm], axis=0, keepdims=True)

    hp = _halo_prev(tm)
    hn = _halo_next(tm, t)
    return pl.pallas_call(
        body, name=name, grid=(XBC // tc, nt),
        in_specs=[pl.BlockSpec((tm, tc), lambda j, i: (i, j + off)), pl.BlockSpec((8, tc), lambda j, i: (hp(i), j + off)),
                  pl.BlockSpec((8, tc), lambda j, i: (hn(i), j + off)),
                  pl.BlockSpec((tm, tc), lambda j, i: (i, j)), pl.BlockSpec((8, tc), lambda j, i: (hn(i), j)),
                  pl.BlockSpec((8, tc), lambda j, i: (0, j)), pl.BlockSpec((1, tc), lambda j, i: (0, j))],
        out_specs=[pl.BlockSpec((tm, tc), lambda j, i: (i, j)), pl.BlockSpec((8, tc), lambda j, i: (0, j)),
                   pl.BlockSpec((1, tc), lambda j, i: (0, j))],
        out_shape=[jax.ShapeDtypeStruct((t, XBC), BF16), jax.ShapeDtypeStruct((8, XBC), F32),
                   jax.ShapeDtypeStruct((1, XBC), F32)],
        compiler_params=_cp(("parallel", "arbitrary")),
    )(zx, zx, zx, dxbc, dxbc, w8, b)


def _head_lanes():
    return lax.broadcasted_iota(jnp.int32, (1, LANES), 1) < SSM_HEADS


def _dt_fwd(dtraw, bias, name):
    t = dtraw.shape[0]
    tm = _pick(t, (1024, 512, 256, 128))

    def body(x_ref, b_ref, o_ref):
        v = x_ref[...] + b_ref[...]
        sp = jnp.maximum(v, 0.0) + jnp.log(1.0 + jnp.exp(-jnp.abs(v)))
        o_ref[...] = jnp.where(_head_lanes(), sp, 0.0)

    row = pl.BlockSpec((tm, LANES), lambda i: (i, 0))
    return pl.pallas_call(
        body, name=name, grid=(t // tm,), in_specs=[row, pl.BlockSpec((1, LANES), lambda i: (0, 0))], out_specs=row,
        out_shape=jax.ShapeDtypeStruct((t, LANES), F32), compiler_params=_cp(("parallel",)),
    )(dtraw, bias)


def _dt_bwd(dtraw, bias, ddt, name):
    t = dtraw.shape[0]
    tm = _pick(t, (1024, 512, 256, 128))

    def body(x_ref, b_ref, d_ref, o_ref, db_ref):
        @pl.when(pl.program_id(0) == 0)
        def _():
            db_ref[...] = jnp.zeros_like(db_ref)

        g = jnp.where(_head_lanes(), d_ref[...] * _sigmoid(x_ref[...] + b_ref[...]), 0.0)
        o_ref[...] = g.astype(BF16)
        db_ref[...] += jnp.sum(g, axis=0, keepdims=True)

    row = pl.BlockSpec((tm, LANES), lambda i: (i, 0))
    vec = pl.BlockSpec((1, LANES), lambda i: (0, 0))
    return pl.pallas_call(
        body, name=name, grid=(t // tm,), in_specs=[row, vec, row], out_specs=[row, vec],
        out_shape=[jax.ShapeDtypeStruct((t, LANES), BF16), jax.ShapeDtypeStruct((1, LANES), F32)],
        compiler_params=_cp(("arbitrary",)),
    )(dtraw, bias, ddt)


GROUP_W = D_INNER // SSM_GROUPS


def _ssd_common(dt, alog):
    ll = dt.shape[0]
    a_neg = -jnp.exp(alog)
    a = dt * a_neg
    ri = lax.broadcasted_iota(jnp.int32, (ll, ll), 0)
    ci = lax.broadcasted_iota(jnp.int32, (ll, ll), 1)
    tril = ri >= ci
    acs = _dot(tril.astype(F32), a, 1, 0, HI)
    return a_neg, tril, acs, acs.T


def _pair_terms(acs, acs_t, dt, h0, lo):
    ll = acs.shape[0]
    cols = [acs[:, h0 + e:h0 + e + 1] for e in range(2)]
    rows = [acs_t[h0 + e:h0 + e + 1, :] for e in range(2)]
    dtc = [dt[:, h0 + e:h0 + e + 1] for e in range(2)]
    lasts = [c[ll - 1:ll, :] for c in cols]
    dtx = jnp.where(lo, dtc[0], dtc[1])
    eac = jnp.where(lo, jnp.exp(cols[0]), jnp.exp(cols[1]))
    fdec = jnp.where(lo, jnp.exp(lasts[0] - cols[0]), jnp.exp(lasts[1] - cols[1]))
    elast = jnp.where(lo, jnp.exp(lasts[0]), jnp.exp(lasts[1]))
    return cols, rows, dtx, eac, fdec, elast


def _decay(col, row, tril):
    return jnp.where(tril, jnp.exp(jnp.minimum(col - row, 0.0)), 0.0)


def _two_heads_rows(v, lo):
    z = jnp.zeros_like(v)
    return jnp.concatenate([jnp.where(lo, v, z), jnp.where(lo, z, v)], axis=0)


def _two_heads_cols(ms):
    return jnp.concatenate(ms, axis=1)


def _z_group(z_refs, g):
    return z_refs[g // 2][:, GROUP_W * (g % 2):GROUP_W * (g % 2 + 1)]


def _ssd_fwd(xbc, dt, alog, zx, dexp, nw, name, comm=None):
    t = xbc.shape[0]
    ll = SSD_L
    nc = t // ll

    def body(x_ref, dt_ref, al_ref, z0_ref, z1_ref, d_ref, w_ref, y_ref, sp_ref, y4_ref, st_ref):
        @pl.when(pl.program_id(0) == 0)
        def _():
            st_ref[...] = jnp.zeros_like(st_ref)

        dtv = dt_ref[...]
        _, tril, acs, acs_t = _ssd_common(dtv, al_ref[...])
        lo = _lo_mask()
        sp_ref[0] = st_ref[...]
        for g in range(SSM_GROUPS):
            bg = x_ref[:, D_INNER + SSM_STATE * g:D_INNER + SSM_STATE * (g + 1)].astype(BF16)
            cg = x_ref[:, D_INNER + 512 + SSM_STATE * g:D_INNER + 512 + SSM_STATE * (g + 1)].astype(BF16)
            gm = _dot(cg, bg, 1, 1)
            g0 = GROUP_W * g
            terms = [_pair_terms(acs, acs_t, dtv, 8 * g + 2 * pp, lo) for pp in range(4)]
            dtx, eac, fdec, elast = [jnp.concatenate([tt[k] for tt in terms], axis=1) for k in (2, 3, 4, 5)]
            xg = x_ref[:, g0:g0 + GROUP_W]
            ug = (xg * dtx).astype(BF16)
            sg = st_ref[:, g0:g0 + GROUP_W]
            yst = _dot(cg, sg.astype(BF16), 1, 0) * eac
            st_ref[:, g0:g0 + GROUP_W] = sg * elast + _dot(bg, (xg * (fdec * dtx)).astype(BF16), 0, 0)
            ys = []
            for pp in range(4):
                cols, rows = terms[pp][0], terms[pp][1]
                sl = slice(LANES * pp, LANES * (pp + 1))
                y_in = _dot(_two_heads_cols([(gm * _decay(cols[e], rows[e], tril)).astype(BF16) for e in range(2)]),
                            _two_heads_rows(ug[:, sl], lo), 1, 0)
                ys.append(y_in + yst[:, sl])
            yg = jnp.concatenate(ys, axis=1)
            y_ref[:, g0:g0 + GROUP_W] = yg
            zg = _z_group((z0_ref, z1_ref), g)
            y3 = (yg + d_ref[:, g0:g0 + GROUP_W] * xg) * (zg * _sigmoid(zg))
            r = lax.rsqrt(jnp.mean(y3 * y3, axis=-1, keepdims=True) + EPS)
            y4_ref[:, g0:g0 + GROUP_W] = (y3 * r * w_ref[:, g0:g0 + GROUP_W]).astype(BF16)

    zblk = lambda j: pl.BlockSpec((ll, 1024), lambda c: (c, j))
    vec = pl.BlockSpec((1, D_INNER), lambda c: (0, 0))
    row = pl.BlockSpec((ll, D_INNER), lambda c: (c, 0))
    return _call(
        body, name=name, grid=(nc,),
        in_specs=[pl.BlockSpec((ll, XBC), lambda c: (c, 0)), pl.BlockSpec((ll, LANES), lambda c: (c, 0)),
                  pl.BlockSpec((1, LANES), lambda c: (0, 0)), zblk(0), zblk(1), vec, vec],
        out_specs=[row, pl.BlockSpec((1, SSM_STATE, D_INNER), lambda c: (c, 0, 0)), row],
        out_shape=[jax.ShapeDtypeStruct((t, D_INNER), F32), jax.ShapeDtypeStruct((nc, SSM_STATE, D_INNER), F32),
                   jax.ShapeDtypeStruct((t, D_INNER), BF16)],
        scratch_shapes=[pltpu.VMEM((SSM_STATE, D_INNER), F32)],
        args=(xbc, dt, alog, zx, zx, dexp, nw), sem=("arbitrary",), comm=comm)


def _ssd_bwd(xbc, dt, alog, sprev, dy4, y, zx, dexp, nw, name, comm=None):
    t = xbc.shape[0]
    ll = SSD_L
    nc = t // ll

    def body(x_ref, dt_ref, al_ref, sp_ref, g4_ref, y_ref, z0_ref, z1_ref, d_ref, w_ref,
             dx_ref, ddt_ref, dal_ref, dz_ref, dd_ref, dnw_ref, ds_ref, colt_ref):
        @pl.when(pl.program_id(0) == 0)
        def _():
            ds_ref[...] = jnp.zeros_like(ds_ref)
            dal_ref[...] = jnp.zeros_like(dal_ref)
            dd_ref[...] = jnp.zeros_like(dd_ref)
            dnw_ref[...] = jnp.zeros_like(dnw_ref)

        dtv = dt_ref[...]
        a_neg, tril, acs, acs_t = _ssd_common(dtv, al_ref[...])
        lo = _lo_mask()
        hi = jnp.logical_not(lo)
        lane = lax.broadcasted_iota(jnp.int32, (1, LANES), 1)
        colt_ref[...] = jnp.zeros_like(colt_ref)
        rowterm = jnp.zeros((ll, LANES), F32)
        ddt_u = jnp.zeros((ll, LANES), F32)
        dlast = jnp.zeros((1, LANES), F32)

        def halves(v):
            return (jnp.sum(jnp.where(lo, v, 0.0), axis=-1, keepdims=True),
                    jnp.sum(jnp.where(hi, v, 0.0), axis=-1, keepdims=True))

        for g in range(SSM_GROUPS):
            cb0 = D_INNER + SSM_STATE * g
            cc0 = D_INNER + 512 + SSM_STATE * g
            bg = x_ref[:, cb0:cb0 + SSM_STATE].astype(BF16)
            cg = x_ref[:, cc0:cc0 + SSM_STATE].astype(BF16)
            gm = _dot(cg, bg, 1, 1)
            g0 = GROUP_W * g
            terms = [_pair_terms(acs, acs_t, dtv, 8 * g + 2 * pp, lo) for pp in range(4)]
            dtx, eac, fdec, elast = [jnp.concatenate([tt[k] for tt in terms], axis=1) for k in (2, 3, 4, 5)]
            xg = x_ref[:, g0:g0 + GROUP_W]
            u32 = xg * dtx
            ug = u32.astype(BF16)
            zg = _z_group((z0_ref, z1_ref), g)
            dg = d_ref[:, g0:g0 + GROUP_W]
            act, dact = _silu_grad(zg)
            y2 = y_ref[:, g0:g0 + GROUP_W] + dg * xg
            y3 = y2 * act
            rn = lax.rsqrt(jnp.mean(y3 * y3, axis=-1, keepdims=True) + EPS)
            y3n = y3 * rn
            gv = g4_ref[:, g0:g0 + GROUP_W]
            dyn = gv * w_ref[:, g0:g0 + GROUP_W]
            dy3 = rn * (dyn - y3n * jnp.mean(dyn * y3n, axis=-1, keepdims=True))
            dyg = dy3 * act
            dskip = dyg * dg
            dz_ref[:, g0:g0 + GROUP_W] = (dy3 * y2 * dact).astype(BF16)
            dd_ref[:, g0:g0 + GROUP_W] += jnp.sum(dyg * xg, axis=0, keepdims=True)
            dnw_ref[:, g0:g0 + GROUP_W] += jnp.sum(gv * y3n, axis=0, keepdims=True)
            dyb = dyg.astype(BF16)
            spg = sp_ref[0, :, g0:g0 + GROUP_W]
            spb = spg.astype(BF16)
            dsg = ds_ref[:, g0:g0 + GROUP_W]
            dsb = dsg.astype(BF16)
            du_st = _dot(bg, dsb, 1, 0) * fdec
            yst = _dot(cg, spb, 1, 0) * eac
            dye = (dyg * eac).astype(BF16)
            dc_st = _dot(dye, spb, 1, 1)
            db_st = _dot((xg * (fdec * dtx)).astype(BF16), dsb, 1, 1)
            ds_ref[:, g0:g0 + GROUP_W] = dsg * elast + _dot(cg, dye, 0, 0)
            qst_el = du_st * u32
            rq_el = dyg * yst - qst_el
            q_row = jnp.sum(qst_el, axis=0, keepdims=True)
            s_row = jnp.sum(dsg * spg, axis=0, keepdims=True)
            dgm = jnp.zeros((ll, ll), F32)
            for pp in range(4):
                h0 = 8 * g + 2 * pp
                cols, rows = terms[pp][0], terms[pp][1]
                sl = slice(LANES * pp, LANES * (pp + 1))
                decs = [_decay(cols[e], rows[e], tril) for e in range(2)]
                wms = [gm * d for d in decs]
                dum2 = _dot(dyb[:, sl], _two_heads_rows(ug[:, sl], lo), 1, 1)
                du = _dot(jnp.concatenate([wm.astype(BF16) for wm in wms], axis=0),
                          _two_heads_rows(dyb[:, sl], lo), 0, 0) + du_st[:, sl]
                dx_ref[:, g0 + LANES * pp:g0 + LANES * (pp + 1)] = du * dtx[:, sl] + dskip[:, sl]
                ddtu = halves(du * xg[:, sl])
                rq = halves(rq_el[:, sl])
                qs = halves(q_row[:, sl])
                ss = halves(s_row[:, sl])
                for e in range(2):
                    dum = dum2[:, ll * e:ll * (e + 1)]
                    dgm = dgm + dum * decs[e]
                    tm_ = dum * wms[e]
                    oh = lane == (h0 + e)
                    rowterm = rowterm + jnp.where(oh, jnp.sum(tm_, axis=1, keepdims=True) + rq[e], 0.0)
                    ddt_u = ddt_u + jnp.where(oh, ddtu[e], 0.0)
                    dlast = dlast + jnp.where(oh, jnp.exp(cols[e][ll - 1:ll, :]) * ss[e] + qs[e], 0.0)
                    colt_ref[h0 + e:h0 + e + 1, :] = jnp.sum(tm_, axis=0, keepdims=True)
            dgb = dgm.astype(BF16)
            dx_ref[:, cc0:cc0 + SSM_STATE] = _dot(dgb, bg, 1, 0) + dc_st
            dx_ref[:, cb0:cb0 + SSM_STATE] = _dot(dgb, cg, 0, 0) + db_st
        row_io = lax.broadcasted_iota(jnp.int32, (ll, LANES), 0)
        dacs = rowterm - colt_ref[...].T + jnp.where(row_io == ll - 1, dlast, 0.0)
        da = _dot(jnp.logical_not(tril).astype(F32) + jnp.where(
            lax.broadcasted_iota(jnp.int32, (ll, ll), 0) == lax.broadcasted_iota(jnp.int32, (ll, ll), 1), 1.0, 0.0),
            dacs, 1, 0, HI)
        ddt_ref[...] = da * a_neg + ddt_u
        dal_ref[...] += jnp.sum(da * dtv, axis=0, keepdims=True) * a_neg

    rev = lambda c: nc - 1 - c
    row = pl.BlockSpec((ll, D_INNER), lambda c: (rev(c), 0))
    vec = pl.BlockSpec((1, D_INNER), lambda c: (0, 0))
    zblk = lambda j: pl.BlockSpec((ll, 1024), lambda c: (rev(c), j))
    return _call(
        body, name=name, grid=(nc,),
        in_specs=[pl.BlockSpec((ll, XBC), lambda c: (rev(c), 0)), pl.BlockSpec((ll, LANES), lambda c: (rev(c), 0)),
                  pl.BlockSpec((1, LANES), lambda c: (0, 0)),
                  pl.BlockSpec((1, SSM_STATE, D_INNER), lambda c: (rev(c), 0, 0)), row, row, zblk(0), zblk(1), vec, vec],
        out_specs=[pl.BlockSpec((ll, XBC), lambda c: (rev(c), 0)), pl.BlockSpec((ll, LANES), lambda c: (rev(c), 0)),
                   pl.BlockSpec((1, LANES), lambda c: (0, 0)), row, vec, vec],
        out_shape=[jax.ShapeDtypeStruct((t, XBC), F32), jax.ShapeDtypeStruct((t, LANES), F32),
                   jax.ShapeDtypeStruct((1, LANES), F32), jax.ShapeDtypeStruct((t, D_INNER), BF16),
                   jax.ShapeDtypeStruct((1, D_INNER), F32), jax.ShapeDtypeStruct((1, D_INNER), F32)],
        scratch_shapes=[pltpu.VMEM((SSM_STATE, D_INNER), F32), pltpu.VMEM((LANES, ll), F32)],
        args=(xbc, dt, alog, sprev, dy4, y, zx, zx, dexp, nw), sem=("arbitrary",), comm=comm)


ADAM_TR = 512


def _sum_parts(parts, name):
    nparts, r, c = parts.shape
    tc = _pick(c, (256, 128))

    def body(p_ref, o_ref):
        g = p_ref[0].astype(F32)
        for k in range(1, nparts):
            g = g + p_ref[k].astype(F32)
        o_ref[...] = g

    return pl.pallas_call(
        body, name=name, grid=(c // tc,), in_specs=[pl.BlockSpec((nparts, r, tc), lambda j: (0, 0, j))],
        out_specs=pl.BlockSpec((r, tc), lambda j: (0, j)), out_shape=jax.ShapeDtypeStruct((r, c), F32),
        compiler_params=_cp(("parallel",)),
    )(parts)


def _adamw(parts, w, m, v, name):
    nl, r, c = w.shape
    assert len(parts) == nl
    tr = _pick(r, (256, 128, 64))
    c1 = 1.0 - ADAM_B1 ** ADAM_STEP
    c2 = 1.0 - ADAM_B2 ** ADAM_STEP

    def body(*refs):
        p_refs = refs[:nl]
        w_ref, m_ref, v_ref, g_ref, d_ref, mo_ref, vo_ref = refs[nl:]
        g = None
        for l, p_ref in enumerate(p_refs):
            s = p_ref[0].astype(F32)
            for k in range(1, p_ref.shape[0]):
                s = s + p_ref[k].astype(F32)
            g = s if g is None else jnp.where(pl.program_id(0) == l, s, g)
        mn = ADAM_B1 * m_ref[0] + (1.0 - ADAM_B1) * g
        vn = ADAM_B2 * v_ref[0] + (1.0 - ADAM_B2) * (g * g)
        g_ref[0] = g
        mo_ref[0] = mn
        vo_ref[0] = vn
        d_ref[0] = -ADAM_LR * ((mn / c1) / (jnp.sqrt(vn / c2) + ADAM_EPS) + ADAM_WD * w_ref[0])

    row = pl.BlockSpec((1, tr, c), lambda l, i: (l, i, 0))
    sd = jax.ShapeDtypeStruct((nl, r, c), F32)
    return pl.pallas_call(
        body, name=name, grid=(nl, r // tr),
        in_specs=[pl.BlockSpec((p.shape[0], tr, c), lambda l, i: (0, i, 0)) for p in parts] + [row, row, row],
        out_specs=[row, row, row, row], out_shape=[sd, sd, sd, sd], compiler_params=_cp(("parallel", "parallel")),
    )(*parts, w, m, v)


def _peers():
    mx, my, mc = lax.axis_index("x"), lax.axis_index("y"), lax.axis_index("c")
    me = 4 * mx + 2 * my + mc
    out = []
    for k in range(1, N_DEV):
        px = 1 - mx if k & 4 else mx
        py = 1 - my if k & 2 else my
        pc = 1 - mc if k & 1 else mc
        out.append(((px, py, pc), 4 * px + 2 * py + pc))
    return me, out


class _Comm:
    def __init__(self, arrs, scatters):
        self.arrs, self.scatters, self.n = list(arrs), list(scatters), len(arrs)
        self.specs = [pl.BlockSpec(memory_space=pl.ANY)] * self.n
        self.out_shape = [jax.ShapeDtypeStruct(x.shape if sc else (N_DEV,) + x.shape, x.dtype)
                          for x, sc in zip(self.arrs, self.scatters)]
        np_ = N_DEV - 1
        self.scratch = [pltpu.SemaphoreType.DMA((np_ * self.n,)), pltpu.SemaphoreType.DMA((np_ * self.n,)),
                        pltpu.SemaphoreType.DMA((self.n,))]

    def _copies(self, x_refs, o_refs, sems):
        send_sems, recv_sems, local_sems = sems
        me, peers = _peers()
        np_ = N_DEV - 1
        local, sends, recvs = [], [], []
        for a in range(self.n):
            mine = x_refs[a].at[me] if self.scatters[a] else x_refs[a]
            local.append(pltpu.make_async_copy(mine, o_refs[a].at[me], local_sems.at[a]))
        for k, (dev, idx) in enumerate(peers):
            for a in range(self.n):
                mine = x_refs[a].at[me] if self.scatters[a] else x_refs[a]
                sends.append(pltpu.make_async_remote_copy(
                    src_ref=x_refs[a].at[idx] if self.scatters[a] else x_refs[a], dst_ref=o_refs[a].at[me],
                    send_sem=send_sems.at[a * np_ + k], recv_sem=recv_sems.at[a * np_ + k], device_id=dev, device_id_type=MESH))
                recvs.append(pltpu.make_async_remote_copy(
                    src_ref=mine, dst_ref=o_refs[a].at[idx], send_sem=send_sems.at[a * np_ + k],
                    recv_sem=recv_sems.at[a * np_ + k], device_id=dev, device_id_type=MESH))
        return local, sends, recvs

    def start(self, x_refs, o_refs, sems):
        local, sends, _ = self._copies(x_refs, o_refs, sems)
        for cp in local + sends:
            cp.start()

    def wait(self, x_refs, o_refs, sems):
        local, sends, recvs = self._copies(x_refs, o_refs, sems)
        for cp in recvs:
            cp.wait_recv()
        for cp in sends:
            cp.wait_send()
        for cp in local:
            cp.wait()


class _Gather2(_Comm):
    def __init__(self, arrs):
        super().__init__(arrs, [False] * len(arrs))

    def _plan(self, x_refs, o_refs, sems):
        send_sems, recv_sems, local_sems = sems
        mx, my, mc = lax.axis_index("x"), lax.axis_index("y"), lax.axis_index("c")
        slot = lambda px, py, pc: 4 * px + 2 * py + pc
        sib = (mx, my, 1 - mc)
        chips = [(1 - mx, my), (mx, 1 - my), (1 - mx, 1 - my)]
        np_ = N_DEV - 1
        local, first, passed, arrive_first, arrive_rest = [], [], [], [], []

        def copy(a, k, src, block, to):
            return pltpu.make_async_remote_copy(
                src_ref=src, dst_ref=o_refs[a].at[block], send_sem=send_sems.at[a * np_ + k], recv_sem=recv_sems.at[a * np_ + k],
                device_id=to, device_id_type=MESH)

        for a in range(self.n):
            me = slot(mx, my, mc)
            local.append(pltpu.make_async_copy(x_refs[a], o_refs[a].at[me], local_sems.at[a]))
            first.append(copy(a, 0, x_refs[a], me, sib))
            arrive_rest.append(copy(a, 0, x_refs[a], slot(*sib), sib))
            for j, (cx, cy) in enumerate(chips):
                first.append(copy(a, 1 + j, x_refs[a], me, (cx, cy, mc)))
                arrive_first.append(copy(a, 1 + j, x_refs[a], slot(cx, cy, mc), (cx, cy, mc)))
                passed.append(copy(a, 4 + j, o_refs[a].at[slot(cx, cy, mc)], slot(cx, cy, mc), sib))
                arrive_rest.append(copy(a, 4 + j, x_refs[a], slot(cx, cy, 1 - mc), sib))
        return local, first, passed, arrive_first, arrive_rest

    def start(self, x_refs, o_refs, sems):
        local, first, _, _, _ = self._plan(x_refs, o_refs, sems)
        for cp in local + first:
            cp.start()

    def wait(self, x_refs, o_refs, sems):
        local, first, passed, arrive_first, arrive_rest = self._plan(x_refs, o_refs, sems)
        for arrived, onward in zip(arrive_first, passed):
            arrived.wait_recv()
            onward.start()
        for cp in arrive_rest:
            cp.wait_recv()
        for cp in first + passed:
            cp.wait_send()
        for cp in local:
            cp.wait()


def _call(body, *, name, grid, in_specs, out_specs, out_shape, args, scratch_shapes=(), sem=None, comm=None):
    if comm is None:
        outs = pl.pallas_call(
            body, name=name, grid=grid, in_specs=list(in_specs), out_specs=list(out_specs), out_shape=list(out_shape),
            scratch_shapes=list(scratch_shapes), compiler_params=_cp(sem),
        )(*args)
        return list(outs), []
    n_in, n_out, nc = len(in_specs), len(out_specs), comm.n
    nsteps = 1
    for g in grid:
        nsteps *= g

    def carrier(*refs):
        ins, cin = refs[:n_in], refs[n_in:n_in + nc]
        outs, cout = refs[n_in + nc:n_in + nc + n_out], refs[n_in + nc + n_out:n_in + 2 * nc + n_out]
        rest = refs[n_in + 2 * nc + n_out:]
        scratch, sems = rest[:len(rest) - 3], rest[len(rest) - 3:]
        if nsteps == 1:
            comm.start(cin, cout, sems)
            body(*ins, *outs, *scratch)
            comm.wait(cin, cout, sems)
            return
        step = 0
        for d, g in enumerate(grid):
            step = step * g + pl.program_id(d)

        @pl.when(step == 0)
        def _():
            comm.start(cin, cout, sems)

        body(*ins, *outs, *scratch)

        @pl.when(step == nsteps - 1)
        def _():
            comm.wait(cin, cout, sems)

    outs = pl.pallas_call(
        carrier, name=name, grid=grid, in_specs=list(in_specs) + comm.specs, out_specs=list(out_specs) + comm.specs,
        out_shape=list(out_shape) + comm.out_shape, scratch_shapes=list(scratch_shapes) + comm.scratch,
        compiler_params=_cp(("arbitrary",) * len(grid) if grid else None),
    )(*args, *comm.arrs)
    return list(outs[:n_out]), list(outs[n_out:])


def _exchange(comm, name):
    return _call(lambda *refs: None, name=name, grid=(), in_specs=[], out_specs=[], out_shape=[], args=[], comm=comm)[1]


def _pack(arrs, dtype, lead=()):
    nl = len(lead)
    flat = jnp.concatenate([a.astype(dtype).reshape(lead + (-1,)) for a in arrs], axis=nl)
    n = flat.shape[-1]
    rows = -(-n // (LANES * ADAM_TR)) * ADAM_TR
    flat = jnp.pad(flat, [(0, 0)] * nl + [(0, rows * LANES - n)])
    return flat.reshape(lead + (rows, LANES))


def _unpack(flat, shapes, lead=()):
    nl = len(lead)
    flat = flat.reshape(lead + (-1,))
    out, o = [], 0
    for s in shapes:
        n = 1
        for d in s:
            n *= d
        out.append(lax.slice_in_dim(flat, o, o + n, axis=nl).reshape(lead + tuple(s)))
        o += n
    return out


def _join(g, ax):
    return jnp.concatenate([g[d] for d in range(N_DEV)], axis=ax)


def _split(full, ax):
    n = full.shape[ax] // N_DEV
    return jnp.stack([lax.slice_in_dim(full, d * n, (d + 1) * n, axis=ax) for d in range(N_DEV)])


_WEIGHTS = ['norm_mix', 'norm_ffn', 'attn_w_in', 'attn_w_out', 'relpos_table', 'q_norm_a', 'k_norm_a', 'q_norm_b',
            'k_norm_b', 'sinks', 'ssm_w_in', 'ssm_conv_w', 'ssm_conv_b', 'ssm_dt_bias', 'ssm_a_log', 'ssm_d', 'ssm_norm',
            'ssm_w_out', 'ffn_w_in', 'ffn_conv_w', 'ffn_conv_b', 'ffn_w_out']
_SHARD_AX = {'attn_w_in': 2, 'attn_w_out': 1, 'ssm_w_in': 2, 'ssm_conv_w': 2, 'ssm_conv_b': 1, 'ssm_norm': 1,
             'ssm_w_out': 1, 'ffn_w_in': 2, 'ffn_conv_w': 2, 'ffn_w_out': 1}
_BIG = ['attn_w_in', 'attn_w_out', 'ssm_w_in', 'ssm_w_out', 'ffn_w_in', 'ffn_w_out']
_SMALL = ['ssm_conv_w', 'ssm_conv_b', 'ssm_norm', 'ffn_conv_w']
_AX2 = {n: _SHARD_AX[n] - 1 for n in _BIG}
_REPL = [n for n in _WEIGHTS if n not in _SHARD_AX]


def _rows8(w):
    return jnp.pad(w, ((0, 8 - w.shape[0]), (0, 0)))


def _lanes128(v):
    return jnp.pad(v, (0, LANES - v.shape[0])).reshape(1, LANES)


def _band_mask(n_prev, pad):
    cq = jnp.arange(TQ)[:, None] // CHUNK
    ck = jnp.arange(pad + TQ)[None, :] // CHUNK
    return (ck >= cq) & (ck <= cq + n_prev)


def _ffn_fwd(xin, g, w_in_t, w8, cb, tag):
    gu, h, a = _ffn_in_mid(xin, g, w_in_t, w8, cb, f"mm_ffn_in{tag}")
    return a, (h, gu, a)


def _ffn_bwd(dx, dxb, xin, g, w_in_t, w8, cb, w_out, saved, tag):
    h, gu, a = saved
    dw_out = _mm_tn(a, dxb, f"mm_ffn_dwout{tag}")
    dgu, dw8, dcb = _ffn_mid_bwd(gu, dxb, w_out, w8, cb, f"ffn_mid_bwd{tag}")
    dw_in_t = _mm_tn(dgu, h, f"mm_ffn_dwin{tag}")
    dxp, dxpb, dg = _mm_rms_bwd(dgu, w_in_t, 0, None, xin, g, dx, f"mm_ffn_dh{tag}")
    return dxp, dxpb, dg, dw_in_t, dw8[:3], dcb, dw_out


def kernel(x, norm_mix, norm_ffn, attn_w_in, attn_w_out, relpos_table, q_norm_a, k_norm_a, q_norm_b, k_norm_b, sinks, ssm_w_in, ssm_conv_w, ssm_conv_b, ssm_dt_bias, ssm_a_log, ssm_d, ssm_norm, ssm_w_out, ffn_w_in, ffn_conv_w, ffn_conv_b, ffn_w_out, loss_target, m_norm_mix, m_norm_ffn, m_attn_w_in, m_attn_w_out, m_relpos_table, m_q_norm_a, m_k_norm_a, m_q_norm_b, m_k_norm_b, m_sinks, m_ssm_w_in, m_ssm_conv_w, m_ssm_conv_b, m_ssm_dt_bias, m_ssm_a_log, m_ssm_d, m_ssm_norm, m_ssm_w_out, m_ffn_w_in, m_ffn_conv_w, m_ffn_conv_b, m_ffn_w_out, v_norm_mix, v_norm_ffn, v_attn_w_in, v_attn_w_out, v_relpos_table, v_q_norm_a, v_k_norm_a, v_q_norm_b, v_k_norm_b, v_sinks, v_ssm_w_in, v_ssm_conv_w, v_ssm_conv_b, v_ssm_dt_bias, v_ssm_a_log, v_ssm_d, v_ssm_norm, v_ssm_w_out, v_ffn_w_in, v_ffn_conv_w, v_ffn_conv_b, v_ffn_w_out):
    w = dict(norm_mix=norm_mix, norm_ffn=norm_ffn, attn_w_in=attn_w_in, attn_w_out=attn_w_out, relpos_table=relpos_table,
             q_norm_a=q_norm_a, k_norm_a=k_norm_a, q_norm_b=q_norm_b, k_norm_b=k_norm_b, sinks=sinks, ssm_w_in=ssm_w_in,
             ssm_conv_w=ssm_conv_w, ssm_conv_b=ssm_conv_b, ssm_dt_bias=ssm_dt_bias, ssm_a_log=ssm_a_log, ssm_d=ssm_d,
             ssm_norm=ssm_norm, ssm_w_out=ssm_w_out, ffn_w_in=ffn_w_in, ffn_conv_w=ffn_conv_w, ffn_conv_b=ffn_conv_b,
             ffn_w_out=ffn_w_out)
    mom = dict(norm_mix=m_norm_mix, norm_ffn=m_norm_ffn, attn_w_in=m_attn_w_in, attn_w_out=m_attn_w_out,
               relpos_table=m_relpos_table, q_norm_a=m_q_norm_a, k_norm_a=m_k_norm_a, q_norm_b=m_q_norm_b,
               k_norm_b=m_k_norm_b, sinks=m_sinks, ssm_w_in=m_ssm_w_in, ssm_conv_w=m_ssm_conv_w, ssm_conv_b=m_ssm_conv_b,
               ssm_dt_bias=m_ssm_dt_bias, ssm_a_log=m_ssm_a_log, ssm_d=m_ssm_d, ssm_norm=m_ssm_norm, ssm_w_out=m_ssm_w_out,
               ffn_w_in=m_ffn_w_in, ffn_conv_w=m_ffn_conv_w, ffn_conv_b=m_ffn_conv_b, ffn_w_out=m_ffn_w_out)
    var = dict(norm_mix=v_norm_mix, norm_ffn=v_norm_ffn, attn_w_in=v_attn_w_in, attn_w_out=v_attn_w_out,
               relpos_table=v_relpos_table, q_norm_a=v_q_norm_a, k_norm_a=v_k_norm_a, q_norm_b=v_q_norm_b,
               k_norm_b=v_k_norm_b, sinks=v_sinks, ssm_w_in=v_ssm_w_in, ssm_conv_w=v_ssm_conv_w, ssm_conv_b=v_ssm_conv_b,
               ssm_dt_bias=v_ssm_dt_bias, ssm_a_log=v_ssm_a_log, ssm_d=v_ssm_d, ssm_norm=v_ssm_norm, ssm_w_out=v_ssm_w_out,
               ffn_w_in=v_ffn_w_in, ffn_conv_w=v_ffn_conv_w, ffn_conv_b=v_ffn_conv_b, ffn_w_out=v_ffn_w_out)

    def piece(n, l):
        return (w[n][l].T if _AX2[n] == 1 else w[n][l]).astype(BF16)

    def gather_of(names_layers):
        return _Gather2([piece(n, l) for n, l in names_layers])

    def joined(got):
        return [g.reshape(-1, D_MODEL) for g in got]

    first = [('attn_w_in', 0), ('attn_w_out', 0)]
    got = _exchange(_Gather2([piece(n, l) for n, l in first] + [_pack([w[n] for n in _SMALL], F32)]), "gather_attn")
    wt_attn_in, w_attn_out = joined(got[:2])
    full = {}
    for n, g in zip(_SMALL, _unpack(got[2], [w[n].shape for n in _SMALL], lead=(N_DEV,))):
        full[n] = _join(g, _SHARD_AX[n])
    ssm_cw8 = _rows8(full['ssm_conv_w'][0])
    ssm_cb = full['ssm_conv_b']
    ssm_nw = full['ssm_norm']
    ffn_cw8 = [_rows8(full['ffn_conv_w'][l]) for l in range(2)]
    ffn_cb = [ffn_conv_b[l:l + 1] for l in range(2)]

    x0 = x[0]
    target = loss_target[0]
    t = x0.shape[0]

    g_mix0, g_mix1 = norm_mix[0:1], norm_mix[1:2]
    g_ffn0, g_ffn1 = norm_ffn[0:1], norm_ffn[1:2]
    proj, h0 = _rms_mm(x0, g_mix0, wt_attn_in, 2304, "mm_attn_in", F32)
    hn_w = jnp.concatenate([jnp.tile(v, (1, 2)) for v in (q_norm_a, k_norm_a, q_norm_b, k_norm_b)], axis=0)
    qa, kpa, vpa, qb, kpb, vpb = _headnorm_fwd(proj, hn_w, "headnorm")
    table = jnp.pad(relpos_table[0], ((0, 0), (0, REL_W - (2 * MAX_REL + 1))))
    bias_a = jnp.where(_band_mask(A_PREV, PAD_A)[None], jnp.transpose(_relpos_fwd(table, "relpos_bias"), (1, 0, 2)), NEG)
    rel_b = jnp.arange(TQ)[:, None] - (jnp.arange(PAD_B + TQ)[None, :] - PAD_B)
    slopes = 2.0 ** (-8.0 * jnp.arange(1, N_HEADS + 1, dtype=F32) / N_HEADS)
    bias_b = jnp.where(_band_mask(B_PREV, PAD_B)[None], -slopes[:, None, None] * jnp.abs(rel_b).astype(F32)[None], NEG)
    no_sinks = jnp.full((N_HEADS,), NEG, F32)
    ffn0_w, ssm_w, ffn1_w = [('ffn_w_in', 0), ('ffn_w_out', 0)], [('ssm_w_in', 0), ('ssm_w_out', 0)], [('ffn_w_in', 1), ('ffn_w_out', 1)]
    oa, stats_a, got = _attn_fwd(qa, kpa, vpa, bias_a, no_sinks, PAD_A, "attn_a", comm=gather_of(ffn0_w + ssm_w))
    wt_ffn_in0, w_ffn_out0, wt_ssm_in, w_ssm_out = joined(got)
    ob, stats_b, got = _attn_fwd(qb, kpb, vpb, bias_b, sinks[0], PAD_B, "attn_b", comm=gather_of(ffn1_w))
    wt_ffn_in1, w_ffn_out1 = joined(got)
    wt_ssm_dt = jnp.pad(wt_ssm_in[ZX:], ((0, LANES - SSM_HEADS), (0, 0)))
    x1 = _mm(oa, w_attn_out, "mm_attn_out_a", res=x0, b_rows=(0, 512))
    x1 = _mm(ob, w_attn_out, "mm_attn_out_b", res=x1, b_rows=(512, 512))
    a0, ffn0_saved = _ffn_fwd(x1, g_ffn0, wt_ffn_in0, ffn_cw8[0], ffn_cb[0], "0")
    x2 = _mm(a0, w_ffn_out0, "mm_ffn_out0", res=x1)

    zx, h2 = _rms_mm(x2, g_mix1, wt_ssm_in, ZX, "mm_ssm_in", F32)
    dtraw = _mm(h2, wt_ssm_dt, "mm_ssm_dt", trans_b=True)
    dt_bias = _lanes128(ssm_dt_bias[0])
    alog = _lanes128(ssm_a_log[0])
    dexp = jnp.repeat(ssm_d[0], HEAD_DIM).reshape(1, D_INNER)
    xbc = _ssm_pre_fwd(zx, ssm_cw8, ssm_cb, "ssm_pre")
    dt = _dt_fwd(dtraw, dt_bias, "ssm_dt")
    (y, sprev, y4), _ = _ssd_fwd(xbc, dt, alog, zx, dexp, ssm_nw, "ssd_fwd")
    x3 = _mm(y4, w_ssm_out, "mm_ssm_out", res=x2)
    a1, ffn1_saved = _ffn_fwd(x3, g_ffn1, wt_ffn_in1, ffn_cw8[1], ffn_cb[1], "1")

    dx4, dx4b, sq = _mm_loss(a1, w_ffn_out1, x3, target, "mm_ffn_out1_loss")
    loss = lax.psum(0.5 * jnp.sum(sq) / D_MODEL, ("x", "y", "c"))

    grads = {}

    def scatter_of(grads_2d):
        return _Comm([g.reshape(N_DEV, -1, D_MODEL) for g in grads_2d], [True] * len(grads_2d))

    dx3, dx3b, dg_ffn1, dwtin1, dcw1, dcb1, dwout1 = _ffn_bwd(
        dx4, dx4b, x3, g_ffn1, wt_ffn_in1, ffn_cw8[1], ffn_cb[1], w_ffn_out1, ffn1_saved, "1")

    dy4 = _mm(dx3b, w_ssm_out, "mm_ssm_dy", trans_b=True)
    dw_ssm_out = _mm_tn(y4, dx3b, "mm_ssm_dwout")
    (dxbc, ddt, dalog, dz, dd_lane, dnw), parts_ffn1 = _ssd_bwd(
        xbc, dt, alog, sprev, dy4, y, zx, dexp, ssm_nw, "ssd_bwd", comm=scatter_of([dwtin1, dwout1]))
    dxr, dcw_s, dcb_s = _ssm_pre_bwd(zx, dxbc, ssm_cw8, ssm_cb, "ssm_pre_bwd")
    ddtraw, ddtb = _dt_bwd(dtraw, dt_bias, ddt, "ssm_dt_bwd")
    dh2 = _mm(dz, wt_ssm_in, "mm_ssm_dh_z", b_rows=(0, D_INNER))
    dh2 = _mm(dxr, wt_ssm_in[D_INNER:ZX], "mm_ssm_dh_x", res=dh2)
    dwt_ssm_in = jnp.concatenate([
        _mm_tn(dz, h2, "mm_ssm_dwin_z"), _mm_tn(dxr, h2, "mm_ssm_dwin_x"),
        _mm_tn(ddtraw, h2, "mm_ssm_dwin_dt")[:SSM_HEADS]], axis=0)
    dx2, dx2b, dg_mix1 = _mm_rms_bwd(ddtraw, wt_ssm_dt, 0, dh2, x2, g_mix1, dx3, "mm_ssm_dh_dt")
    grads['ssm_conv_w'] = dcw_s[:4][None]
    grads['ssm_conv_b'] = dcb_s
    grads['ssm_norm'] = dnw
    grads['ssm_dt_bias'] = ddtb[:, :SSM_HEADS]
    grads['ssm_a_log'] = dalog[:, :SSM_HEADS]
    grads['ssm_d'] = jnp.sum(dd_lane.reshape(SSM_HEADS, HEAD_DIM), axis=1)[None]

    dx1, dx1b, dg_ffn0, dwtin0, dcw0, dcb0, dwout0 = _ffn_bwd(
        dx2, dx2b, x1, g_ffn0, wt_ffn_in0, ffn_cw8[0], ffn_cb[0], w_ffn_out0, ffn0_saved, "0")
    grads['ffn_conv_w'] = jnp.stack([dcw0, dcw1])
    grads['ffn_conv_b'] = jnp.concatenate([dcb0, dcb1], axis=0)
    grads['norm_ffn'] = jnp.concatenate([dg_ffn0, dg_ffn1], axis=0)

    do = _mm(dx1b, w_attn_out, "mm_attn_do", out_dtype=BF16, trans_b=True)
    dw_attn_out = jnp.concatenate([_mm_tn(oa, dx1b, "mm_attn_dwout_a"), _mm_tn(ob, dx1b, "mm_attn_dwout_b")], axis=0)
    (dqa, dkpa, dvpa, dbias_a, _), parts_ssm = _attn_bwd(
        qa, kpa, vpa, bias_a, no_sinks, do, stats_a, 0, PAD_A, "attn_a_bwd",
        comm=scatter_of([dwt_ssm_in, dw_ssm_out, dw_attn_out]))
    (dqb, dkpb, dvpb, _, dsink), parts_ffn0 = _attn_bwd(
        qb, kpb, vpb, bias_b, sinks[0], do, stats_b, 4, PAD_B, "attn_b_bwd", comm=scatter_of([dwtin0, dwout0]))
    grads['relpos_table'] = _relpos_bwd(jnp.transpose(dbias_a, (1, 0, 2)), "relpos_bwd")[None, :, :2 * MAX_REL + 1]
    grads['sinks'] = dsink[:, :2, 0].reshape(1, N_HEADS)
    dproj, dhn = _headnorm_bwd(proj, hn_w, dqa, dkpa, dvpa, dqb, dkpb, dvpb, "headnorm_bwd")
    dhn = dhn[:, :HEAD_DIM] + dhn[:, HEAD_DIM:]
    for k, n in enumerate(('q_norm_a', 'k_norm_a', 'q_norm_b', 'k_norm_b')):
        grads[n] = dhn[k:k + 1]
    dwt_attn_in = _mm_tn(dproj, h0, "mm_attn_dwin")
    dx0, _, dg_mix0, parts_attn_in = _mm_rms_bwd(dproj, wt_attn_in, 0, None, x0, g_mix0, dx1, "mm_attn_dh",
                                                 comm=scatter_of([dwt_attn_in]))
    grads['norm_mix'] = jnp.concatenate([dg_mix0, dg_mix1], axis=0)

    def summed_t(parts, name):
        return _sum_parts(parts, name).T[None]

    sm_shapes = [w[n].shape for n in _SMALL]
    rp_shapes = [w[n].shape for n in _REPL]
    recv = _exchange(_Comm(
        [_pack([_split(grads[n], _SHARD_AX[n]) for n in _SMALL], F32, lead=(N_DEV,)), _pack([grads[n] for n in _REPL], F32)],
        [True, False]), "exchange_small")
    big_parts = {
        'attn_w_in': [summed_t(parts_attn_in[0], "sum_attn_w_in")], 'attn_w_out': [parts_ssm[2]],
        'ssm_w_in': [summed_t(parts_ssm[0], "sum_ssm_w_in")], 'ssm_w_out': [parts_ssm[1]],
        'ffn_w_in': [summed_t(parts_ffn0[0], "sum_ffn_w_in0"), summed_t(parts_ffn1[0], "sum_ffn_w_in1")],
        'ffn_w_out': [parts_ffn0[1], parts_ffn1[1]],
    }
    res = [{}, {}, {}, {}]
    for n in _BIG:
        for kind, a in enumerate(_adamw(big_parts[n], w[n], mom[n], var[n], f"adamw_{n}")):
            res[kind][n] = a
    for names, shapes, parts in ((_SMALL, sm_shapes, recv[0]), (_REPL, rp_shapes, recv[1])):
        outs = _adamw([parts], _pack([w[n] for n in names], F32)[None], _pack([mom[n] for n in names], F32)[None],
                      _pack([var[n] for n in names], F32)[None], "adamw_" + ("small" if names is _SMALL else "replicated"))
        for kind, flat in enumerate(outs):
            for n, a in zip(names, _unpack(flat[0], shapes)):
                res[kind][n] = a
    return (loss, dx0[None], *[res[0][n] for n in _WEIGHTS], *[res[1][n] for n in _WEIGHTS],
            *[res[2][n] for n in _WEIGHTS], *[res[3][n] for n in _WEIGHTS])
```

```python
import jax
import jax.numpy as jnp
from jax import lax
from jax.experimental import pallas as pl
from jax.experimental.pallas import tpu as pltpu

F32 = jnp.float32
BF16 = jnp.bfloat16
HI = lax.Precision.HIGHEST
MESH = pl.DeviceIdType.MESH
NEG = -1e30

N_DEV = 8
D_MODEL = 1024
EPS = 1e-6
CHUNK = 64
HEAD_DIM = 64
N_HEADS = 8
A_PREV = 8
B_PREV = 2
MAX_REL = 256
TQ = 2 * CHUNK
ATT_SUB = 4
PAD_A = A_PREV * CHUNK
PAD_B = B_PREV * CHUNK
REL_W = PAD_A + TQ
D_INNER = 2048
SSM_HEADS = 32
SSM_GROUPS = 4
SSM_STATE = 128
XBC = D_INNER + 2 * SSM_GROUPS * SSM_STATE
ZX = D_INNER + XBC
D_FF = 2816
SSD_L = 128
LANES = 128
VMEM_LIMIT = 56 << 20

ADAM_LR, ADAM_B1, ADAM_B2, ADAM_EPS, ADAM_WD, ADAM_STEP = 0.001, 0.9, 0.999, 1e-08, 0.01, 10


def _cp(sem=None):
    return pltpu.CompilerParams(dimension_semantics=sem, vmem_limit_bytes=VMEM_LIMIT)


def _dot(a, b, ca=1, cb=0, prec=None):
    return lax.dot_general(a, b, (((ca,), (cb,)), ((), ())), preferred_element_type=F32, precision=prec)


def _pick(n, cands):
    for c in cands:
        if n % c == 0:
            return c
    return n


def _lo_mask():
    return lax.broadcasted_iota(jnp.int32, (1, LANES), 1) < HEAD_DIM


_TN_CHUNKS = (1408, 1536, 1152, 1024, 512, 256, 128)


TN_MAX_ROWS = 3072


def _mm_tn(a, b, name):
    kdim, m = a.shape
    n = b.shape[1]
    assert b.shape[0] == kdim, (a.shape, b.shape)
    mb = m if m <= TN_MAX_ROWS else m // 2
    tn = _pick(n, _TN_CHUNKS)
    tk = _pick(kdim, (512, 256, 128))
    nk = kdim // tk

    def body(a_ref, b_ref, o_ref, acc):
        k = pl.program_id(1)

        @pl.when(k == 0)
        def _():
            acc[...] = jnp.zeros_like(acc)

        av = a_ref[...]
        for c in range(0, n, tn):
            acc[:, c:c + tn] += _dot(av, b_ref[:, c:c + tn], 0, 0)

        @pl.when(k == nk - 1)
        def _():
            o_ref[...] = acc[...].astype(BF16)

    return pl.pallas_call(
        body, name=name, grid=(m // mb, nk),
        in_specs=[pl.BlockSpec((tk, mb), lambda j, k: (k, j)), pl.BlockSpec((tk, n), lambda j, k: (k, 0))],
        out_specs=pl.BlockSpec((mb, n), lambda j, k: (j, 0)), out_shape=jax.ShapeDtypeStruct((m, n), BF16),
        scratch_shapes=[pltpu.VMEM((mb, n), F32)], compiler_params=_cp(("parallel", "arbitrary")),
    )(a, b)


def _mm(a, b, name, out_dtype=F32, res=None, trans_b=False, b_rows=None):
    m, kdim = a.shape
    if b_rows is None:
        b_rows = (0, b.shape[0])
    off, rows = b_rows
    n = rows if trans_b else b.shape[1]
    assert (b.shape[1] if trans_b else rows) == kdim and off % rows == 0, (a.shape, b.shape, b_rows)
    tn = _pick(n, _TN_CHUNKS)
    tm = _pick(m, (256, 128) if n > 2304 else (512, 256, 128))

    def body(*refs):
        if res is None:
            a_ref, b_ref, o_ref = refs
        else:
            a_ref, b_ref, r_ref, o_ref = refs
        av = a_ref[...]
        for c in range(0, n, tn):
            r = _dot(av, b_ref[c:c + tn, :], 1, 1) if trans_b else _dot(av, b_ref[:, c:c + tn], 1, 0)
            if res is not None:
                r = r + r_ref[:, c:c + tn]
            o_ref[:, c:c + tn] = r.astype(out_dtype)

    in_specs = [pl.BlockSpec((tm, kdim), lambda i: (i, 0)), pl.BlockSpec((rows, b.shape[1]), lambda i: (off // rows, 0))]
    args = [a, b]
    if res is not None:
        in_specs.append(pl.BlockSpec((tm, n), lambda i: (i, 0)))
        args.append(res)
    return pl.pallas_call(
        body, name=name, grid=(m // tm,), in_specs=in_specs, out_specs=pl.BlockSpec((tm, n), lambda i: (i, 0)),
        out_shape=jax.ShapeDtypeStruct((m, n), out_dtype), compiler_params=_cp(("parallel",)),
    )(*args)


def _rms_mm(x, g, bt, n, name, out_dtype):
    t, d = x.shape
    tn = _pick(n, _TN_CHUNKS)
    tm = _pick(t, (256, 128))

    def body(x_ref, g_ref, b_ref, o_ref, h_ref):
        xv = x_ref[...]
        r = lax.rsqrt(jnp.mean(xv * xv, axis=-1, keepdims=True) + EPS)
        h = (xv * r * g_ref[...]).astype(BF16)
        h_ref[...] = h
        for c in range(0, n, tn):
            o_ref[:, c:c + tn] = _dot(h, b_ref[c:c + tn, :], 1, 1).astype(out_dtype)

    row = pl.BlockSpec((tm, d), lambda i: (i, 0))
    return pl.pallas_call(
        body, name=name, grid=(t // tm,),
        in_specs=[row, pl.BlockSpec((1, d), lambda i: (0, 0)), pl.BlockSpec(bt.shape, lambda i: (0, 0))],
        out_specs=[pl.BlockSpec((tm, n), lambda i: (i, 0)), row],
        out_shape=[jax.ShapeDtypeStruct((t, n), out_dtype), jax.ShapeDtypeStruct((t, d), BF16)],
        compiler_params=_cp(("parallel",)),
    )(x, g, bt)


def _mm_rms_bwd(a, b, b_off, dh_prev, x, g, dres, name, comm=None):
    t, d = x.shape
    kdim = a.shape[1]
    assert b_off % kdim == 0 and b.shape[1] == d, (a.shape, b.shape, b_off)
    tm = _pick(t, (256, 128))

    def body(*refs):
        if dh_prev is None:
            a_ref, b_ref, x_ref, g_ref, dr_ref, dx_ref, dxb_ref, dg_ref = refs
            dhv = _dot(a_ref[...], b_ref[...], 1, 0)
        else:
            a_ref, b_ref, p_ref, x_ref, g_ref, dr_ref, dx_ref, dxb_ref, dg_ref = refs
            dhv = _dot(a_ref[...], b_ref[...], 1, 0) + p_ref[...]
        xv = x_ref[...]
        r = lax.rsqrt(jnp.mean(xv * xv, axis=-1, keepdims=True) + EPS)
        xh = xv * r
        dxh = dhv * g_ref[...]
        dx = dr_ref[...] + r * (dxh - xh * jnp.mean(dxh * xh, axis=-1, keepdims=True))
        dx_ref[...] = dx
        dxb_ref[...] = dx.astype(BF16)

        @pl.when(pl.program_id(0) == 0)
        def _():
            dg_ref[...] = jnp.zeros_like(dg_ref)

        dg_ref[...] += jnp.sum(dhv * xh, axis=0, keepdims=True)

    row = pl.BlockSpec((tm, d), lambda i: (i, 0))
    vec = pl.BlockSpec((1, d), lambda i: (0, 0))
    in_specs = [pl.BlockSpec((tm, kdim), lambda i: (i, 0)), pl.BlockSpec((kdim, d), lambda i: (b_off // kdim, 0))]
    args = [a, b]
    if dh_prev is not None:
        in_specs.append(row)
        args.append(dh_prev)
    outs, got = _call(
        body, name=name, grid=(t // tm,), in_specs=in_specs + [row, vec, row], out_specs=[row, row, vec],
        out_shape=[jax.ShapeDtypeStruct((t, d), F32), jax.ShapeDtypeStruct((t, d), BF16), jax.ShapeDtypeStruct((1, d), F32)],
        args=(*args, x, g, dres), sem=("arbitrary",), comm=comm)
    return (*outs, got) if comm is not None else tuple(outs)


def _mm_loss(a, b, res, target, name):
    t, kdim = a.shape
    d = b.shape[1]
    tm = _pick(t, (512, 256, 128))

    def body(a_ref, b_ref, r_ref, t_ref, dy_ref, dyb_ref, acc_ref):
        @pl.when(pl.program_id(0) == 0)
        def _():
            acc_ref[...] = jnp.zeros_like(acc_ref)

        err = _dot(a_ref[...], b_ref[...], 1, 0) + r_ref[...] - t_ref[...]
        dy = err * (1.0 / d)
        dy_ref[...] = dy
        dyb_ref[...] = dy.astype(BF16)
        acc_ref[...] += jnp.sum(err * err, axis=0, keepdims=True)

    row = pl.BlockSpec((tm, d), lambda i: (i, 0))
    vec = pl.BlockSpec((1, d), lambda i: (0, 0))
    return pl.pallas_call(
        body, name=name, grid=(t // tm,),
        in_specs=[pl.BlockSpec((tm, kdim), lambda i: (i, 0)), pl.BlockSpec((kdim, d), lambda i: (0, 0)), row, row],
        out_specs=[row, row, vec],
        out_shape=[jax.ShapeDtypeStruct((t, d), F32), jax.ShapeDtypeStruct((t, d), BF16), jax.ShapeDtypeStruct((1, d), F32)],
        compiler_params=_cp(("arbitrary",)),
    )(a, b, res, target)


def _head_sums(v):
    ri = lax.broadcasted_iota(jnp.int32, (LANES, LANES), 0) // HEAD_DIM
    ci = lax.broadcasted_iota(jnp.int32, (LANES, LANES), 1) // HEAD_DIM
    ones = (ri == ci).astype(BF16)
    hi = v.astype(BF16)
    lo_part = (v - hi.astype(F32)).astype(BF16)
    return _dot(hi, ones, 1, 0) + _dot(lo_part, ones, 1, 0)


def _head_rms(xs, w, lo):
    r = lax.rsqrt(_head_sums(xs * xs) * (1.0 / HEAD_DIM) + EPS)
    return xs * r, r


def _head_rms_bwd(xs, w, dy, lo):
    xh, r = _head_rms(xs, w, lo)
    dxh = dy * w
    mm = _head_sums(dxh * xh) * (1.0 / HEAD_DIM)
    return r * (dxh - xh * mm), dy * xh


_QSCALE = HEAD_DIM ** -0.5


def _headnorm_fwd(proj, ws, name):
    t = proj.shape[0]
    tm = TQ
    lead = PAD_A // tm
    leadb = PAD_B // tm

    def body(p_ref, w_ref, qa_ref, ka_ref, va_ref, qb_ref, kb_ref, vb_ref):
        data = pl.program_id(0) >= lead
        lo = _lo_mask()

        def put(ref, c, val):
            ref[:, c:c + val.shape[1]] = jnp.where(data, val, 0.0).astype(BF16)

        def per_query_head(slab):
            other = pltpu.roll(slab, HEAD_DIM, 1)
            e0, e1 = jnp.where(lo, slab, other), jnp.where(lo, other, slab)
            return jnp.concatenate([e0, e0, e1, e1], axis=1)

        for s in range(4):
            c = LANES * s
            xh, _ = _head_rms(p_ref[:, c:c + LANES], None, lo)
            qa_ref[:, c:c + LANES] = (xh * w_ref[0:1, :] * _QSCALE).astype(BF16)
            xh, _ = _head_rms(p_ref[:, 512 + c:512 + c + LANES], None, lo)
            put(ka_ref, c, xh * w_ref[1:2, :])
            xh, _ = _head_rms(p_ref[:, 1536 + c:1536 + c + LANES], None, lo)
            qb_ref[:, c:c + LANES] = (xh * w_ref[2:3, :] * _QSCALE).astype(BF16)
        put(va_ref, 0, p_ref[:, 1024:1536])
        xh, _ = _head_rms(p_ref[:, 2048:2176], None, lo)
        put(kb_ref, 0, per_query_head(xh * w_ref[3:4, :]))
        put(vb_ref, 0, per_query_head(p_ref[:, 2176:2304]))

    src = lambda i: jnp.maximum(i - lead, 0)
    wide = pl.BlockSpec((tm, 512), lambda i: (src(i), 0))
    pad_a = pl.BlockSpec((tm, 512), lambda i: (i, 0))
    pad_b = pl.BlockSpec((tm, 512), lambda i: (jnp.maximum(i - lead + leadb, 0), 0))
    sd = lambda rows: jax.ShapeDtypeStruct((rows, 512), BF16)
    return pl.pallas_call(
        body, name=name, grid=(t // tm + lead,),
        in_specs=[pl.BlockSpec((tm, 2304), lambda i: (src(i), 0)), pl.BlockSpec((4, LANES), lambda i: (0, 0))],
        out_specs=[wide, pad_a, pad_a, wide, pad_b, pad_b],
        out_shape=[sd(t), sd(t + PAD_A), sd(t + PAD_A), sd(t), sd(t + PAD_B), sd(t + PAD_B)],
        compiler_params=_cp(("arbitrary",)),
    )(proj, ws)


def _headnorm_bwd(proj, ws, dqa, dkpa, dvpa, dqb, dkpb, dvpb, name):
    t = proj.shape[0]
    tm = TQ
    offa, offb = PAD_A // tm, PAD_B // tm

    def body(p_ref, w_ref, dqa_ref, dka_ref, dva_ref, dqb_ref, dkb_ref, dvb_ref, dp_ref, dw_ref):
        i = pl.program_id(0)
        lo = _lo_mask()

        @pl.when(i == 0)
        def _():
            dw_ref[...] = jnp.zeros_like(dw_ref)

        acc = [jnp.zeros((1, LANES), F32) for _ in range(4)]
        for s in range(4):
            c = LANES * s
            dx, dwl = _head_rms_bwd(p_ref[:, c:c + LANES], w_ref[0:1, :], dqa_ref[:, c:c + LANES] * _QSCALE, lo)
            dp_ref[:, c:c + LANES] = dx.astype(BF16)
            acc[0] += jnp.sum(dwl, axis=0, keepdims=True)
            dx, dwl = _head_rms_bwd(p_ref[:, 512 + c:512 + c + LANES], w_ref[1:2, :], dka_ref[:, c:c + LANES], lo)
            dp_ref[:, 512 + c:512 + c + LANES] = dx.astype(BF16)
            acc[1] += jnp.sum(dwl, axis=0, keepdims=True)
            dx, dwl = _head_rms_bwd(p_ref[:, 1536 + c:1536 + c + LANES], w_ref[2:3, :], dqb_ref[:, c:c + LANES] * _QSCALE, lo)
            dp_ref[:, 1536 + c:1536 + c + LANES] = dx.astype(BF16)
            acc[2] += jnp.sum(dwl, axis=0, keepdims=True)
        dp_ref[:, 1024:1536] = dva_ref[...].astype(BF16)

        def group_sum(ref):
            s0 = ref[:, 0:128] + ref[:, 128:256]
            s1 = ref[:, 256:384] + ref[:, 384:512]
            s0 = s0 + pltpu.roll(s0, HEAD_DIM, 1)
            s1 = s1 + pltpu.roll(s1, HEAD_DIM, 1)
            return jnp.where(lo, s0, s1)

        dx, dwl = _head_rms_bwd(p_ref[:, 2048:2176], w_ref[3:4, :], group_sum(dkb_ref), lo)
        dp_ref[:, 2048:2176] = dx.astype(BF16)
        acc[3] += jnp.sum(dwl, axis=0, keepdims=True)
        dp_ref[:, 2176:2304] = group_sum(dvb_ref).astype(BF16)
        for n in range(4):
            dw_ref[n:n + 1, :] += acc[n]

    wide = pl.BlockSpec((tm, 512), lambda i: (i, 0))
    pa = pl.BlockSpec((tm, 512), lambda i: (i + offa, 0))
    pb = pl.BlockSpec((tm, 512), lambda i: (i + offb, 0))
    return pl.pallas_call(
        body, name=name, grid=(t // tm,),
        in_specs=[pl.BlockSpec((tm, 2304), lambda i: (i, 0)), pl.BlockSpec((4, LANES), lambda i: (0, 0)),
                  wide, pa, pa, wide, pb, pb],
        out_specs=[pl.BlockSpec((tm, 2304), lambda i: (i, 0)), pl.BlockSpec((4, LANES), lambda i: (0, 0))],
        out_shape=[jax.ShapeDtypeStruct((t, 2304), BF16), jax.ShapeDtypeStruct((4, LANES), F32)],
        compiler_params=_cp(("arbitrary",)),
    )(proj, ws, dqa, dkpa, dvpa, dqb, dkpb, dvpb)


ROLL_W = 1024


def _rel_onehot():
    r_io = lax.broadcasted_iota(jnp.int32, (REL_W, ROLL_W), 0)
    m_io = lax.broadcasted_iota(jnp.int32, (REL_W, ROLL_W), 1)
    return (r_io == jnp.clip(REL_W - 1 - m_io, -MAX_REL, MAX_REL) + MAX_REL).astype(F32)


def _relpos_fwd(table, name):
    def body(t_ref, o_ref):
        rr = _dot(t_ref[...], _rel_onehot(), 1, 0, HI)

        def step(q, c):
            o_ref[q] = pltpu.roll(rr, (ROLL_W - (TQ - 1) + q) % ROLL_W, 1)[:, :REL_W]
            return c

        lax.fori_loop(0, TQ, step, 0)

    return pl.pallas_call(
        body, name=name, out_shape=jax.ShapeDtypeStruct((TQ, N_HEADS, REL_W), F32),
        in_specs=[pl.BlockSpec(memory_space=pltpu.VMEM)], out_specs=pl.BlockSpec(memory_space=pltpu.VMEM),
        compiler_params=_cp(),
    )(table)


def _relpos_bwd(dbias_t, name):
    def body(d_ref, o_ref):
        def step(q, acc):
            row = jnp.concatenate([d_ref[q], jnp.zeros((N_HEADS, ROLL_W - REL_W), F32)], axis=1)
            return acc + pltpu.roll(row, TQ - 1 - q, 1)

        drr = lax.fori_loop(0, TQ, step, jnp.zeros((N_HEADS, ROLL_W), F32))
        o_ref[...] = _dot(drr, _rel_onehot(), 1, 1, HI)

    return pl.pallas_call(
        body, name=name, out_shape=jax.ShapeDtypeStruct((N_HEADS, REL_W), F32),
        in_specs=[pl.BlockSpec(memory_space=pltpu.VMEM)], out_specs=pl.BlockSpec(memory_space=pltpu.VMEM),
        compiler_params=_cp(),
    )(dbias_t)


def _attn_scores(qe, kw, bias, kvalid):
    return jnp.where(kvalid, _dot(qe, kw, 1, 1) + bias, NEG)


def _stat_cols(stats, e):
    return stats[:, 64 * e:64 * e + 1], stats[:, 64 * e + 32:64 * e + 33]


def _attn_fwd(q, kp, vp, bias, sinks, pad, name, comm=None):
    t, hd = q.shape
    w = pad + TQ

    def body(sink_ref, q_ref, k_ref, v_ref, b_ref, o_ref, st_ref):
        hp, i = pl.program_id(0), pl.program_id(1)
        lo = _lo_mask()
        lane = lax.broadcasted_iota(jnp.int32, (1, LANES), 1)
        for j in range(ATT_SUB):
            start = pl.multiple_of((i * ATT_SUB + j) * TQ, TQ)
            qv = q_ref[TQ * j:TQ * (j + 1), :]
            kw = k_ref[pl.ds(start, w), :]
            vw = v_ref[pl.ds(start, w), :]
            kvalid = (start + lax.broadcasted_iota(jnp.int32, (1, w), 1)) >= pad
            outs, ms, ls = [], [], []
            for e in range(2):
                sel = lo if e == 0 else jnp.logical_not(lo)
                qe = jnp.where(sel, qv, jnp.zeros_like(qv))
                snk = sink_ref[2 * hp + e]
                s = _attn_scores(qe, kw, b_ref[e], kvalid)
                m = jnp.maximum(jnp.max(s, axis=-1, keepdims=True), snk)
                acc = _dot(jnp.exp(s - m).astype(BF16), jnp.where(sel, vw, jnp.ones_like(vw)), 1, 0)
                denom = acc[:, 64 * (1 - e):64 * (1 - e) + 1] + jnp.exp(snk - m)
                outs.append(acc * (1.0 / denom))
                ms.append(m)
                ls.append(denom)
            o_ref[TQ * j:TQ * (j + 1), :] = jnp.where(lo, outs[0], outs[1]).astype(BF16)
            st_ref[TQ * j:TQ * (j + 1), :] = jnp.where(lane < 32, ms[0], jnp.where(lane < 64, ls[0],
                                                                                 jnp.where(lane < 96, ms[1], ls[1])))

    full = pl.BlockSpec((t + pad, LANES), lambda h, i: (0, h))
    tile = pl.BlockSpec((ATT_SUB * TQ, LANES), lambda h, i: (i, h))
    (o, stats), got = _call(
        body, name=name, grid=(hd // LANES, t // (ATT_SUB * TQ)),
        in_specs=[pl.BlockSpec(memory_space=pltpu.SMEM), tile, full, full, pl.BlockSpec((2, TQ, w), lambda h, i: (h, 0, 0))],
        out_specs=[tile, tile], out_shape=[jax.ShapeDtypeStruct((t, hd), BF16), jax.ShapeDtypeStruct((t, hd), F32)],
        args=(sinks, q, kp, vp, bias), sem=("parallel", "arbitrary"), comm=comm)
    return o, stats, got


def _attn_bwd(q, kp, vp, bias, sinks, do, stats, col_off, pad, name, comm=None):
    t, hd = q.shape
    w = pad + TQ
    nhp = hd // LANES

    def body(sink_ref, q_ref, k_ref, v_ref, b_ref, do_ref, st_ref, dq_ref, dk_ref, dv_ref, db_ref, ds_ref):
        hp, i = pl.program_id(0), pl.program_id(1)

        @pl.when(i == 0)
        def _():
            dk_ref[...] = jnp.zeros_like(dk_ref)
            dv_ref[...] = jnp.zeros_like(dv_ref)
            db_ref[...] = jnp.zeros_like(db_ref)
            ds_ref[...] = jnp.zeros_like(ds_ref)

        lo = _lo_mask()
        row8 = lax.broadcasted_iota(jnp.int32, (8, LANES), 0)
        dbias = [None, None]
        dsink = jnp.zeros((8, LANES), F32)
        for j in range(ATT_SUB):
            start = pl.multiple_of((i * ATT_SUB + j) * TQ, TQ)
            qv = q_ref[TQ * j:TQ * (j + 1), :]
            dov = do_ref[TQ * j:TQ * (j + 1), :]
            kw = k_ref[pl.ds(start, w), :]
            vw = v_ref[pl.ds(start, w), :]
            kvalid = (start + lax.broadcasted_iota(jnp.int32, (1, w), 1)) >= pad
            stats = st_ref[TQ * j:TQ * (j + 1), :]
            dqs, dkw, dvw = [], None, None
            for e in range(2):
                sel = lo if e == 0 else jnp.logical_not(lo)
                qe = jnp.where(sel, qv, jnp.zeros_like(qv))
                doe = jnp.where(sel, dov, jnp.zeros_like(dov))
                m, denom = _stat_cols(stats, e)
                inv = 1.0 / denom
                p = jnp.exp(_attn_scores(qe, kw, b_ref[e], kvalid) - m) * inv
                psink = jnp.exp(sink_ref[2 * hp + e] - m) * inv
                dp = _dot(doe, vw, 1, 1)
                delta = jnp.sum(p * dp, axis=-1, keepdims=True)
                ds = p * (dp - delta)
                dbias[e] = ds if dbias[e] is None else dbias[e] + ds
                dsink = dsink + jnp.where(row8 == e, jnp.sum(-psink * delta, axis=0, keepdims=True), 0.0)
                dsb = ds.astype(BF16)
                dqs.append(_dot(dsb, kw, 1, 0))
                dk_e = _dot(dsb, qe, 0, 0)
                dv_e = _dot(p.astype(BF16), doe, 0, 0)
                dkw = dk_e if dkw is None else dkw + dk_e
                dvw = dv_e if dvw is None else dvw + dv_e
            dq_ref[TQ * j:TQ * (j + 1), :] = jnp.where(lo, dqs[0], dqs[1])
            dk_ref[pl.ds(start, w), :] += dkw
            dv_ref[pl.ds(start, w), :] += dvw
        for e in range(2):
            db_ref[e] += dbias[e]
        ds_ref[0] += dsink

    full = pl.BlockSpec((t + pad, LANES), lambda h, i: (0, h))
    tile = pl.BlockSpec((ATT_SUB * TQ, LANES), lambda h, i: (i, h))
    btile = pl.BlockSpec((2, TQ, w), lambda h, i: (h, 0, 0))
    return _call(
        body, name=name, grid=(nhp, t // (ATT_SUB * TQ)),
        in_specs=[pl.BlockSpec(memory_space=pltpu.SMEM), tile, full, full, btile,
                  pl.BlockSpec((ATT_SUB * TQ, LANES), lambda h, i: (i, h + col_off)), tile],
        out_specs=[tile, full, full, btile, pl.BlockSpec((1, 8, LANES), lambda h, i: (h, 0, 0))],
        out_shape=[jax.ShapeDtypeStruct((t, hd), F32), jax.ShapeDtypeStruct((t + pad, hd), F32),
                   jax.ShapeDtypeStruct((t + pad, hd), F32), jax.ShapeDtypeStruct((N_HEADS, TQ, w), F32),
                   jax.ShapeDtypeStruct((nhp, 8, LANES), F32)],
        args=(sinks, q, kp, vp, bias, do, stats), sem=("parallel", "arbitrary"), comm=comm)


def _halo_prev(tm):
    return lambda i: jnp.maximum(i * (tm // 8) - 1, 0)


def _halo_next(tm, t):
    return lambda i: jnp.minimum((i + 1) * (tm // 8), t // 8 - 1)


def _conv_apply(taps, w_ref, ktaps):
    out = taps[0] * w_ref[ktaps - 1:ktaps, :]
    for s in range(1, ktaps):
        out = out + taps[s] * w_ref[ktaps - 1 - s:ktaps - s, :]
    return out


def _sigmoid(x):
    return jax.nn.sigmoid(x)


def _silu_grad(x):
    sg = _sigmoid(x)
    return x * sg, sg * (1.0 + x * (1.0 - sg))


FFN_HALO = 16
FFN_BT = 256
FFN_BC = 1408


def _ffn_in_mid(x, g, wt, w8, b, name):
    t, d = x.shape
    f = D_FF
    tm = FFN_BT

    def body(x_ref, g_ref, b_ref, w_ref, cb_ref, gu_ref, h_ref, a_ref, halo_ref):
        @pl.when(pl.program_id(0) == 0)
        def _():
            halo_ref[...] = jnp.zeros_like(halo_ref)

        xv = x_ref[...]
        r = lax.rsqrt(jnp.mean(xv * xv, axis=-1, keepdims=True) + EPS)
        h = (xv * r * g_ref[...]).astype(BF16)
        h_ref[...] = h
        for c in range(0, f, FFN_BC):
            cs = slice(c, c + FFN_BC)
            gate = _dot(h, b_ref[c:c + FFN_BC, :], 1, 1).astype(BF16)
            up = _dot(h, b_ref[f + c:f + c + FFN_BC, :], 1, 1).astype(BF16)
            gu_ref[:, cs] = gate
            gu_ref[:, f + c:f + c + FFN_BC] = up
            gf = gate.astype(F32)
            ext = jnp.concatenate([halo_ref[:, cs], gf], axis=0)
            gc = (cb_ref[:, cs] + gf * w_ref[2:3, cs] + pltpu.roll(ext, 1, 0)[8:] * w_ref[1:2, cs]
                  + pltpu.roll(ext, 2, 0)[8:] * w_ref[0:1, cs])
            a_ref[:, cs] = (gc * _sigmoid(gc) * up.astype(F32)).astype(BF16)
            halo_ref[:, cs] = gf[tm - 8:]

    row = pl.BlockSpec((tm, d), lambda i: (i, 0))
    return pl.pallas_call(
        body, name=name, grid=(t // tm,),
        in_specs=[row, pl.BlockSpec((1, d), lambda i: (0, 0)), pl.BlockSpec((2 * f, d), lambda i: (0, 0)),
                  pl.BlockSpec((8, f), lambda i: (0, 0)), pl.BlockSpec((1, f), lambda i: (0, 0))],
        out_specs=[pl.BlockSpec((tm, 2 * f), lambda i: (i, 0)), row, pl.BlockSpec((tm, f), lambda i: (i, 0))],
        out_shape=[jax.ShapeDtypeStruct((t, 2 * f), BF16), jax.ShapeDtypeStruct((t, d), BF16), jax.ShapeDtypeStruct((t, f), BF16)],
        scratch_shapes=[pltpu.VMEM((8, f), F32)], compiler_params=_cp(("arbitrary",)),
    )(x, g, wt, w8, b)


def _ffn_mid_bwd(gu, dxb, w_out, w8, b, name):
    t, d = dxb.shape
    f = D_FF
    tm, hr = FFN_BT, FFN_HALO
    nt = t // tm
    n = tm + hr

    def body(g_ref, u_ref, gp_ref, gn_ref, un_ref, dx_ref, dxn_ref, wo_ref, w_ref, b_ref, dgu_ref, dw_ref, db_ref):
        i = pl.program_id(0)
        first, last = i == 0, i == nt - 1

        @pl.when(first)
        def _():
            dw_ref[...] = jnp.zeros_like(dw_ref)
            db_ref[...] = jnp.zeros_like(db_ref)

        dxe = jnp.concatenate([dx_ref[...], dxn_ref[...]], axis=0)
        row = lax.broadcasted_iota(jnp.int32, (n, 1), 0)
        keep = (row < tm) | jnp.logical_not(last)
        for c in range(0, f, FFN_BC):
            cs = slice(c, c + FFN_BC)
            ext = jnp.concatenate([jnp.where(first, 0.0, gp_ref[:, cs].astype(F32)), g_ref[:, cs].astype(F32),
                                   gn_ref[:, cs].astype(F32)], axis=0)
            taps = [ext[hr:]] + [pltpu.roll(ext, s, 0)[hr:] for s in (1, 2)]
            gc = b_ref[:, cs] + taps[0] * w_ref[2:3, cs] + taps[1] * w_ref[1:2, cs] + taps[2] * w_ref[0:1, cs]
            act, dact = _silu_grad(gc)
            da = _dot(dxe, wo_ref[cs, :], 1, 1)
            up = jnp.concatenate([u_ref[:, cs], un_ref[:, cs]], axis=0).astype(F32)
            dgc = jnp.where(keep, da * up * dact, 0.0)
            dgu_ref[:, f + c:f + c + FFN_BC] = (da[:tm] * act[:tm]).astype(BF16)
            dgu_ref[:, cs] = (dgc[:tm] * w_ref[2:3, cs] + pltpu.roll(dgc, n - 1, 0)[:tm] * w_ref[1:2, cs]
                              + pltpu.roll(dgc, n - 2, 0)[:tm] * w_ref[0:1, cs]).astype(BF16)
            db_ref[:, cs] += jnp.sum(dgc[:tm], axis=0, keepdims=True)
            for s in range(3):
                dw_ref[2 - s:3 - s, cs] += jnp.sum(dgc[:tm] * taps[s][:tm], axis=0, keepdims=True)

    r = tm // hr
    prev = lambda i: jnp.maximum(i * r - 1, 0)
    nxt_blk = lambda i: jnp.minimum((i + 1) * r, t // hr - 1)
    row_f = pl.BlockSpec((tm, f), lambda i: (i, 0))
    return pl.pallas_call(
        body, name=name, grid=(nt,),
        in_specs=[row_f, pl.BlockSpec((tm, f), lambda i: (i, 1)),
                  pl.BlockSpec((hr, f), lambda i: (prev(i), 0)), pl.BlockSpec((hr, f), lambda i: (nxt_blk(i), 0)),
                  pl.BlockSpec((hr, f), lambda i: (nxt_blk(i), 1)),
                  pl.BlockSpec((tm, d), lambda i: (i, 0)), pl.BlockSpec((hr, d), lambda i: (nxt_blk(i), 0)),
                  pl.BlockSpec((f, d), lambda i: (0, 0)),
                  pl.BlockSpec((8, f), lambda i: (0, 0)), pl.BlockSpec((1, f), lambda i: (0, 0))],
        out_specs=[pl.BlockSpec((tm, 2 * f), lambda i: (i, 0)), pl.BlockSpec((8, f), lambda i: (0, 0)),
                   pl.BlockSpec((1, f), lambda i: (0, 0))],
        out_shape=[jax.ShapeDtypeStruct((t, 2 * f), BF16), jax.ShapeDtypeStruct((8, f), F32), jax.ShapeDtypeStruct((1, f), F32)],
        compiler_params=_cp(("arbitrary",)),
    )(gu, gu, gu, gu, gu, dxb, dxb, w_out, w8, b)


PRE_TM = 256
PRE_TC = 1024


def _ssm_in_pre(x, g, wt, w8, b, name):
    t, d = x.shape
    tm, tc = PRE_TM, PRE_TC

    def body(x_ref, g_ref, b_ref, w_ref, cb_ref, zx_ref, h_ref, o_ref, halo_ref):
        @pl.when(pl.program_id(0) == 0)
        def _():
            halo_ref[...] = jnp.zeros_like(halo_ref)

        xv = x_ref[...]
        r = lax.rsqrt(jnp.mean(xv * xv, axis=-1, keepdims=True) + EPS)
        h = (xv * r * g_ref[...]).astype(BF16)
        h_ref[...] = h
        for c in range(0, ZX, tc):
            v = _dot(h, b_ref[c:c + tc, :], 1, 1)
            zx_ref[:, c:c + tc] = v
            if c >= D_INNER:
                cs = slice(c - D_INNER, c - D_INNER + tc)
                ext = jnp.concatenate([halo_ref[:, cs], v], axis=0)
                conv = cb_ref[:, cs] + v * w_ref[3:4, cs]
                for s in (1, 2, 3):
                    conv = conv + pltpu.roll(ext, s, 0)[8:] * w_ref[3 - s:4 - s, cs]
                o_ref[:, cs] = conv * _sigmoid(conv)
                halo_ref[:, cs] = v[tm - 8:]

    row = pl.BlockSpec((tm, d), lambda i: (i, 0))
    return pl.pallas_call(
        body, name=name, grid=(t // tm,),
        in_specs=[row, pl.BlockSpec((1, d), lambda i: (0, 0)), pl.BlockSpec(wt.shape, lambda i: (0, 0)),
                  pl.BlockSpec((8, XBC), lambda i: (0, 0)), pl.BlockSpec((1, XBC), lambda i: (0, 0))],
        out_specs=[pl.BlockSpec((tm, ZX), lambda i: (i, 0)), row, pl.BlockSpec((tm, XBC), lambda i: (i, 0))],
        out_shape=[jax.ShapeDtypeStruct((t, ZX), F32), jax.ShapeDtypeStruct((t, d), BF16), jax.ShapeDtypeStruct((t, XBC), F32)],
        scratch_shapes=[pltpu.VMEM((8, XBC), F32)], compiler_params=_cp(("arbitrary",)),
    )(x, g, wt, w8, b)


def _ssm_pre_bwd(zx, dxbc, w8, b, name):
    t = zx.shape[0]
    tm, tc = PRE_TM, PRE_TC
    off = D_INNER // tc
    nt = t // tm
    n = tm + 8

    def body(x_ref, xp_ref, xn_ref, d_ref, dn_ref, w_ref, b_ref, o_ref, dw_ref, db_ref):
        i = pl.program_id(1)
        first, last = i == 0, i == nt - 1

        @pl.when(first)
        def _():
            dw_ref[...] = jnp.zeros_like(dw_ref)
            db_ref[...] = jnp.zeros_like(db_ref)

        ext = jnp.concatenate([jnp.where(first, 0.0, xp_ref[...]), x_ref[...], xn_ref[...]], axis=0)
        taps = [ext[8:8 + n]] + [pltpu.roll(ext, s, 0)[8:8 + n] for s in (1, 2, 3)]
        c = _conv_apply(taps, w_ref, 4) + b_ref[...]
        _, dact = _silu_grad(c)
        row = lax.broadcasted_iota(jnp.int32, (n, 1), 0)
        dc = jnp.where((row < tm) | jnp.logical_not(last), jnp.concatenate([d_ref[...], dn_ref[...]], axis=0) * dact, 0.0)
        nxt = [dc[:tm]] + [pltpu.roll(dc, n - s, 0)[:tm] for s in (1, 2, 3)]
        o_ref[...] = _conv_apply(nxt, w_ref, 4).astype(BF16)
        db_ref[...] += jnp.sum(dc[:tm], axis=0, keepdims=True)
        for s in range(4):
            dw_ref[3 - s:4 - s, :] += jnp.sum(dc[:tm] * taps[s][:tm], axis=0, keepdims=True)

    hp = _halo_prev(tm)
    hn = _halo_next(tm, t)
    return pl.pallas_call(
        body, name=name, grid=(XBC // tc, nt),
        in_specs=[pl.BlockSpec((tm, tc), lambda j, i: (i, j + off)), pl.BlockSpec((8, tc), lambda j, i: (hp(i), j + off)),
                  pl.BlockSpec((8, tc), lambda j, i: (hn(i), j + off)),
                  pl.BlockSpec((tm, tc), lambda j, i: (i, j)), pl.BlockSpec((8, tc), lambda j, i: (hn(i), j)),
                  pl.BlockSpec((8, tc), lambda j, i: (0, j)), pl.BlockSpec((1, tc), lambda j, i: (0, j))],
        out_specs=[pl.BlockSpec((tm, tc), lambda j, i: (i, j)), pl.BlockSpec((8, tc), lambda j, i: (0, j)),
                   pl.BlockSpec((1, tc), lambda j, i: (0, j))],
        out_shape=[jax.ShapeDtypeStruct((t, XBC), BF16), jax.ShapeDtypeStruct((8, XBC), F32),
                   jax.ShapeDtypeStruct((1, XBC), F32)],
        compiler_params=_cp(("parallel", "arbitrary")),
    )(zx, zx, zx, dxbc, dxbc, w8, b)


def _head_lanes():
    return lax.broadcasted_iota(jnp.int32, (1, LANES), 1) < SSM_HEADS


def _dt_fwd(dtraw, bias, name):
    t = dtraw.shape[0]
    tm = _pick(t, (1024, 512, 256, 128))

    def body(x_ref, b_ref, o_ref):
        v = x_ref[...] + b_ref[...]
        sp = jnp.maximum(v, 0.0) + jnp.log(1.0 + jnp.exp(-jnp.abs(v)))
        o_ref[...] = jnp.where(_head_lanes(), sp, 0.0)

    row = pl.BlockSpec((tm, LANES), lambda i: (i, 0))
    return pl.pallas_call(
        body, name=name, grid=(t // tm,), in_specs=[row, pl.BlockSpec((1, LANES), lambda i: (0, 0))], out_specs=row,
        out_shape=jax.ShapeDtypeStruct((t, LANES), F32), compiler_params=_cp(("parallel",)),
    )(dtraw, bias)


def _dt_bwd(dtraw, bias, ddt, name):
    t = dtraw.shape[0]
    tm = _pick(t, (1024, 512, 256, 128))

    def body(x_ref, b_ref, d_ref, o_ref, db_ref):
        @pl.when(pl.program_id(0) == 0)
        def _():
            db_ref[...] = jnp.zeros_like(db_ref)

        g = jnp.where(_head_lanes(), d_ref[...] * _sigmoid(x_ref[...] + b_ref[...]), 0.0)
        o_ref[...] = g.astype(BF16)
        db_ref[...] += jnp.sum(g, axis=0, keepdims=True)

    row = pl.BlockSpec((tm, LANES), lambda i: (i, 0))
    vec = pl.BlockSpec((1, LANES), lambda i: (0, 0))
    return pl.pallas_call(
        body, name=name, grid=(t // tm,), in_specs=[row, vec, row], out_specs=[row, vec],
        out_shape=[jax.ShapeDtypeStruct((t, LANES), BF16), jax.ShapeDtypeStruct((1, LANES), F32)],
        compiler_params=_cp(("arbitrary",)),
    )(dtraw, bias, ddt)


GROUP_W = D_INNER // SSM_GROUPS


def _ssd_common(dt, alog):
    ll = dt.shape[0]
    a_neg = -jnp.exp(alog)
    a = dt * a_neg
    ri = lax.broadcasted_iota(jnp.int32, (ll, ll), 0)
    ci = lax.broadcasted_iota(jnp.int32, (ll, ll), 1)
    tril = ri >= ci
    acs = _dot(tril.astype(F32), a, 1, 0, HI)
    return a_neg, tril, acs, acs.T


def _pair_terms(acs, acs_t, dt, h0, lo):
    ll = acs.shape[0]
    cols = [acs[:, h0 + e:h0 + e + 1] for e in range(2)]
    rows = [acs_t[h0 + e:h0 + e + 1, :] for e in range(2)]
    dtc = [dt[:, h0 + e:h0 + e + 1] for e in range(2)]
    lasts = [c[ll - 1:ll, :] for c in cols]
    dtx = jnp.where(lo, dtc[0], dtc[1])
    eac = jnp.where(lo, jnp.exp(cols[0]), jnp.exp(cols[1]))
    fdec = jnp.where(lo, jnp.exp(lasts[0] - cols[0]), jnp.exp(lasts[1] - cols[1]))
    elast = jnp.where(lo, jnp.exp(lasts[0]), jnp.exp(lasts[1]))
    return cols, rows, dtx, eac, fdec, elast


def _decay(col, row, tril):
    return jnp.where(tril, jnp.exp(jnp.minimum(col - row, 0.0)), 0.0)


def _two_heads_rows(v, lo):
    z = jnp.zeros_like(v)
    return jnp.concatenate([jnp.where(lo, v, z), jnp.where(lo, z, v)], axis=0)


def _two_heads_cols(ms):
    return jnp.concatenate(ms, axis=1)


def _z_group(z_refs, g):
    return z_refs[g // 2][:, GROUP_W * (g % 2):GROUP_W * (g % 2 + 1)]


def _ssd_fwd(xbc, dt, alog, zx, dexp, nw, name, comm=None):
    t = xbc.shape[0]
    ll = SSD_L
    nc = t // ll

    def body(x_ref, dt_ref, al_ref, z0_ref, z1_ref, d_ref, w_ref, y_ref, sp_ref, y4_ref, st_ref):
        @pl.when(pl.program_id(0) == 0)
        def _():
            st_ref[...] = jnp.zeros_like(st_ref)

        dtv = dt_ref[...]
        _, tril, acs, acs_t = _ssd_common(dtv, al_ref[...])
        lo = _lo_mask()
        sp_ref[0] = st_ref[...]
        for g in range(SSM_GROUPS):
            bg = x_ref[:, D_INNER + SSM_STATE * g:D_INNER + SSM_STATE * (g + 1)].astype(BF16)
            cg = x_ref[:, D_INNER + 512 + SSM_STATE * g:D_INNER + 512 + SSM_STATE * (g + 1)].astype(BF16)
            gm = _dot(cg, bg, 1, 1)
            g0 = GROUP_W * g
            terms = [_pair_terms(acs, acs_t, dtv, 8 * g + 2 * pp, lo) for pp in range(4)]
            dtx, eac, fdec, elast = [jnp.concatenate([tt[k] for tt in terms], axis=1) for k in (2, 3, 4, 5)]
            xg = x_ref[:, g0:g0 + GROUP_W]
            ug = (xg * dtx).astype(BF16)
            sg = st_ref[:, g0:g0 + GROUP_W]
            yst = _dot(cg, sg.astype(BF16), 1, 0) * eac
            st_ref[:, g0:g0 + GROUP_W] = sg * elast + _dot(bg, (xg * (fdec * dtx)).astype(BF16), 0, 0)
            ys = []
            for pp in range(4):
                cols, rows = terms[pp][0], terms[pp][1]
                sl = slice(LANES * pp, LANES * (pp + 1))
                y_in = _dot(_two_heads_cols([(gm * _decay(cols[e], rows[e], tril)).astype(BF16) for e in range(2)]),
                            _two_heads_rows(ug[:, sl], lo), 1, 0)
                ys.append(y_in + yst[:, sl])
            yg = jnp.concatenate(ys, axis=1)
            y_ref[:, g0:g0 + GROUP_W] = yg
            zg = _z_group((z0_ref, z1_ref), g)
            y3 = (yg + d_ref[:, g0:g0 + GROUP_W] * xg) * (zg * _sigmoid(zg))
            r = lax.rsqrt(jnp.mean(y3 * y3, axis=-1, keepdims=True) + EPS)
            y4_ref[:, g0:g0 + GROUP_W] = (y3 * r * w_ref[:, g0:g0 + GROUP_W]).astype(BF16)

    zblk = lambda j: pl.BlockSpec((ll, 1024), lambda c: (c, j))
    vec = pl.BlockSpec((1, D_INNER), lambda c: (0, 0))
    row = pl.BlockSpec((ll, D_INNER), lambda c: (c, 0))
    return _call(
        body, name=name, grid=(nc,),
        in_specs=[pl.BlockSpec((ll, XBC), lambda c: (c, 0)), pl.BlockSpec((ll, LANES), lambda c: (c, 0)),
                  pl.BlockSpec((1, LANES), lambda c: (0, 0)), zblk(0), zblk(1), vec, vec],
        out_specs=[row, pl.BlockSpec((1, SSM_STATE, D_INNER), lambda c: (c, 0, 0)), row],
        out_shape=[jax.ShapeDtypeStruct((t, D_INNER), F32), jax.ShapeDtypeStruct((nc, SSM_STATE, D_INNER), F32),
                   jax.ShapeDtypeStruct((t, D_INNER), BF16)],
        scratch_shapes=[pltpu.VMEM((SSM_STATE, D_INNER), F32)],
        args=(xbc, dt, alog, zx, zx, dexp, nw), sem=("arbitrary",), comm=comm)


def _ssd_bwd(xbc, dt, alog, sprev, dy4, y, zx, dexp, nw, name, comm=None):
    t = xbc.shape[0]
    ll = SSD_L
    nc = t // ll

    def body(x_ref, dt_ref, al_ref, sp_ref, g4_ref, y_ref, z0_ref, z1_ref, d_ref, w_ref,
             dx_ref, ddt_ref, dal_ref, dz_ref, dd_ref, dnw_ref, ds_ref, colt_ref):
        @pl.when(pl.program_id(0) == 0)
        def _():
            ds_ref[...] = jnp.zeros_like(ds_ref)
            dal_ref[...] = jnp.zeros_like(dal_ref)
            dd_ref[...] = jnp.zeros_like(dd_ref)
            dnw_ref[...] = jnp.zeros_like(dnw_ref)

        dtv = dt_ref[...]
        a_neg, tril, acs, acs_t = _ssd_common(dtv, al_ref[...])
        lo = _lo_mask()
        hi = jnp.logical_not(lo)
        lane = lax.broadcasted_iota(jnp.int32, (1, LANES), 1)
        colt_ref[...] = jnp.zeros_like(colt_ref)
        rowterm = jnp.zeros((ll, LANES), F32)
        ddt_u = jnp.zeros((ll, LANES), F32)
        dlast = jnp.zeros((1, LANES), F32)

        def halves(v):
            return (jnp.sum(jnp.where(lo, v, 0.0), axis=-1, keepdims=True),
                    jnp.sum(jnp.where(hi, v, 0.0), axis=-1, keepdims=True))

        for g in range(SSM_GROUPS):
            cb0 = D_INNER + SSM_STATE * g
            cc0 = D_INNER + 512 + SSM_STATE * g
            bg = x_ref[:, cb0:cb0 + SSM_STATE].astype(BF16)
            cg = x_ref[:, cc0:cc0 + SSM_STATE].astype(BF16)
            gm = _dot(cg, bg, 1, 1)
            g0 = GROUP_W * g
            terms = [_pair_terms(acs, acs_t, dtv, 8 * g + 2 * pp, lo) for pp in range(4)]
            dtx, eac, fdec, elast = [jnp.concatenate([tt[k] for tt in terms], axis=1) for k in (2, 3, 4, 5)]
            xg = x_ref[:, g0:g0 + GROUP_W]
            u32 = xg * dtx
            ug = u32.astype(BF16)
            zg = _z_group((z0_ref, z1_ref), g)
            dg = d_ref[:, g0:g0 + GROUP_W]
            act, dact = _silu_grad(zg)
            y2 = y_ref[:, g0:g0 + GROUP_W] + dg * xg
            y3 = y2 * act
            rn = lax.rsqrt(jnp.mean(y3 * y3, axis=-1, keepdims=True) + EPS)
            y3n = y3 * rn
            gv = g4_ref[:, g0:g0 + GROUP_W]
            dyn = gv * w_ref[:, g0:g0 + GROUP_W]
            dy3 = rn * (dyn - y3n * jnp.mean(dyn * y3n, axis=-1, keepdims=True))
            dyg = dy3 * act
            dskip = dyg * dg
            dz_ref[:, g0:g0 + GROUP_W] = (dy3 * y2 * dact).astype(BF16)
            dd_ref[:, g0:g0 + GROUP_W] += jnp.sum(dyg * xg, axis=0, keepdims=True)
            dnw_ref[:, g0:g0 + GROUP_W] += jnp.sum(gv * y3n, axis=0, keepdims=True)
            dyb = dyg.astype(BF16)
            spg = sp_ref[0, :, g0:g0 + GROUP_W]
            spb = spg.astype(BF16)
            dsg = ds_ref[:, g0:g0 + GROUP_W]
            dsb = dsg.astype(BF16)
            du_st = _dot(bg, dsb, 1, 0) * fdec
            yst = _dot(cg, spb, 1, 0) * eac
            dye = (dyg * eac).astype(BF16)
            dc_st = _dot(dye, spb, 1, 1)
            db_st = _dot((xg * (fdec * dtx)).astype(BF16), dsb, 1, 1)
            ds_ref[:, g0:g0 + GROUP_W] = dsg * elast + _dot(cg, dye, 0, 0)
            qst_el = du_st * u32
            rq_el = dyg * yst - qst_el
            q_row = jnp.sum(qst_el, axis=0, keepdims=True)
            s_row = jnp.sum(dsg * spg, axis=0, keepdims=True)
            dgm = jnp.zeros((ll, ll), F32)
            for pp in range(4):
                h0 = 8 * g + 2 * pp
                cols, rows = terms[pp][0], terms[pp][1]
                sl = slice(LANES * pp, LANES * (pp + 1))
                decs = [_decay(cols[e], rows[e], tril) for e in range(2)]
                wms = [gm * d for d in decs]
                dum2 = _dot(dyb[:, sl], _two_heads_rows(ug[:, sl], lo), 1, 1)
                du = _dot(jnp.concatenate([wm.astype(BF16) for wm in wms], axis=0),
                          _two_heads_rows(dyb[:, sl], lo), 0, 0) + du_st[:, sl]
                dx_ref[:, g0 + LANES * pp:g0 + LANES * (pp + 1)] = du * dtx[:, sl] + dskip[:, sl]
                ddtu = halves(du * xg[:, sl])
                rq = halves(rq_el[:, sl])
                qs = halves(q_row[:, sl])
                ss = halves(s_row[:, sl])
                for e in range(2):
                    dum = dum2[:, ll * e:ll * (e + 1)]
                    dgm = dgm + dum * decs[e]
                    tm_ = dum * wms[e]
                    oh = lane == (h0 + e)
                    rowterm = rowterm + jnp.where(oh, jnp.sum(tm_, axis=1, keepdims=True) + rq[e], 0.0)
                    ddt_u = ddt_u + jnp.where(oh, ddtu[e], 0.0)
                    dlast = dlast + jnp.where(oh, jnp.exp(cols[e][ll - 1:ll, :]) * ss[e] + qs[e], 0.0)
                    colt_ref[h0 + e:h0 + e + 1, :] = jnp.sum(tm_, axis=0, keepdims=True)
            dgb = dgm.astype(BF16)
            dx_ref[:, cc0:cc0 + SSM_STATE] = _dot(dgb, bg, 1, 0) + dc_st
            dx_ref[:, cb0:cb0 + SSM_STATE] = _dot(dgb, cg, 0, 0) + db_st
        row_io = lax.broadcasted_iota(jnp.int32, (ll, LANES), 0)
        dacs = rowterm - colt_ref[...].T + jnp.where(row_io == ll - 1, dlast, 0.0)
        da = _dot(jnp.logical_not(tril).astype(F32) + jnp.where(
            lax.broadcasted_iota(jnp.int32, (ll, ll), 0) == lax.broadcasted_iota(jnp.int32, (ll, ll), 1), 1.0, 0.0),
            dacs, 1, 0, HI)
        ddt_ref[...] = da * a_neg + ddt_u
        dal_ref[...] += jnp.sum(da * dtv, axis=0, keepdims=True) * a_neg

    rev = lambda c: nc - 1 - c
    row = pl.BlockSpec((ll, D_INNER), lambda c: (rev(c), 0))
    vec = pl.BlockSpec((1, D_INNER), lambda c: (0, 0))
    zblk = lambda j: pl.BlockSpec((ll, 1024), lambda c: (rev(c), j))
    return _call(
        body, name=name, grid=(nc,),
        in_specs=[pl.BlockSpec((ll, XBC), lambda c: (rev(c), 0)), pl.BlockSpec((ll, LANES), lambda c: (rev(c), 0)),
                  pl.BlockSpec((1, LANES), lambda c: (0, 0)),
                  pl.BlockSpec((1, SSM_STATE, D_INNER), lambda c: (rev(c), 0, 0)), row, row, zblk(0), zblk(1), vec, vec],
        out_specs=[pl.BlockSpec((ll, XBC), lambda c: (rev(c), 0)), pl.BlockSpec((ll, LANES), lambda c: (rev(c), 0)),
                   pl.BlockSpec((1, LANES), lambda c: (0, 0)), row, vec, vec],
        out_shape=[jax.ShapeDtypeStruct((t, XBC), F32), jax.ShapeDtypeStruct((t, LANES), F32),
                   jax.ShapeDtypeStruct((1, LANES), F32), jax.ShapeDtypeStruct((t, D_INNER), BF16),
                   jax.ShapeDtypeStruct((1, D_INNER), F32), jax.ShapeDtypeStruct((1, D_INNER), F32)],
        scratch_shapes=[pltpu.VMEM((SSM_STATE, D_INNER), F32), pltpu.VMEM((LANES, ll), F32)],
        args=(xbc, dt, alog, sprev, dy4, y, zx, zx, dexp, nw), sem=("arbitrary",), comm=comm)


ADAM_TR = 512


def _sum_parts(parts, name):
    nparts, r, c = parts.shape
    tc = _pick(c, (256, 128))

    def body(p_ref, o_ref):
        g = p_ref[0].astype(F32)
        for k in range(1, nparts):
            g = g + p_ref[k].astype(F32)
        o_ref[...] = g

    return pl.pallas_call(
        body, name=name, grid=(c // tc,), in_specs=[pl.BlockSpec((nparts, r, tc), lambda j: (0, 0, j))],
        out_specs=pl.BlockSpec((r, tc), lambda j: (0, j)), out_shape=jax.ShapeDtypeStruct((r, c), F32),
        compiler_params=_cp(("parallel",)),
    )(parts)


def _adamw(parts, w, m, v, name):
    nl, r, c = w.shape
    assert len(parts) == nl
    tr = _pick(r, (256, 128, 64))
    c1 = 1.0 - ADAM_B1 ** ADAM_STEP
    c2 = 1.0 - ADAM_B2 ** ADAM_STEP

    def body(*refs):
        p_refs = refs[:nl]
        w_ref, m_ref, v_ref, g_ref, d_ref, mo_ref, vo_ref = refs[nl:]
        g = None
        for l, p_ref in enumerate(p_refs):
            s = p_ref[0].astype(F32)
            for k in range(1, p_ref.shape[0]):
                s = s + p_ref[k].astype(F32)
            g = s if g is None else jnp.where(pl.program_id(0) == l, s, g)
        mn = ADAM_B1 * m_ref[0] + (1.0 - ADAM_B1) * g
        vn = ADAM_B2 * v_ref[0] + (1.0 - ADAM_B2) * (g * g)
        g_ref[0] = g
        mo_ref[0] = mn
        vo_ref[0] = vn
        d_ref[0] = -ADAM_LR * ((mn / c1) / (jnp.sqrt(vn / c2) + ADAM_EPS) + ADAM_WD * w_ref[0])

    row = pl.BlockSpec((1, tr, c), lambda l, i: (l, i, 0))
    sd = jax.ShapeDtypeStruct((nl, r, c), F32)
    return pl.pallas_call(
        body, name=name, grid=(nl, r // tr),
        in_specs=[pl.BlockSpec((p.shape[0], tr, c), lambda l, i: (0, i, 0)) for p in parts] + [row, row, row],
        out_specs=[row, row, row, row], out_shape=[sd, sd, sd, sd], compiler_params=_cp(("parallel", "parallel")),
    )(*parts, w, m, v)


def _peers():
    mx, my, mc = lax.axis_index("x"), lax.axis_index("y"), lax.axis_index("c")
    me = 4 * mx + 2 * my + mc
    out = []
    for k in range(1, N_DEV):
        px = 1 - mx if k & 4 else mx
        py = 1 - my if k & 2 else my
        pc = 1 - mc if k & 1 else mc
        out.append(((px, py, pc), 4 * px + 2 * py + pc))
    return me, out


class _Comm:
    def __init__(self, arrs, scatters):
        self.arrs, self.scatters, self.n = list(arrs), list(scatters), len(arrs)
        self.specs = [pl.BlockSpec(memory_space=pl.ANY)] * self.n
        self.out_shape = [jax.ShapeDtypeStruct(x.shape if sc else (N_DEV,) + x.shape, x.dtype)
                          for x, sc in zip(self.arrs, self.scatters)]
        np_ = N_DEV - 1
        self.scratch = [pltpu.SemaphoreType.DMA((np_ * self.n,)), pltpu.SemaphoreType.DMA((np_ * self.n,)),
                        pltpu.SemaphoreType.DMA((self.n,))]

    def _copies(self, x_refs, o_refs, sems):
        send_sems, recv_sems, local_sems = sems
        me, peers = _peers()
        np_ = N_DEV - 1
        local, sends, recvs = [], [], []
        for a in range(self.n):
            mine = x_refs[a].at[me] if self.scatters[a] else x_refs[a]
            local.append(pltpu.make_async_copy(mine, o_refs[a].at[me], local_sems.at[a]))
        for k, (dev, idx) in enumerate(peers):
            for a in range(self.n):
                mine = x_refs[a].at[me] if self.scatters[a] else x_refs[a]
                sends.append(pltpu.make_async_remote_copy(
                    src_ref=x_refs[a].at[idx] if self.scatters[a] else x_refs[a], dst_ref=o_refs[a].at[me],
                    send_sem=send_sems.at[a * np_ + k], recv_sem=recv_sems.at[a * np_ + k], device_id=dev, device_id_type=MESH))
                recvs.append(pltpu.make_async_remote_copy(
                    src_ref=mine, dst_ref=o_refs[a].at[idx], send_sem=send_sems.at[a * np_ + k],
                    recv_sem=recv_sems.at[a * np_ + k], device_id=dev, device_id_type=MESH))
        return local, sends, recvs

    def start(self, x_refs, o_refs, sems):
        local, sends, _ = self._copies(x_refs, o_refs, sems)
        for cp in local + sends:
            cp.start()

    def wait(self, x_refs, o_refs, sems):
        local, sends, recvs = self._copies(x_refs, o_refs, sems)
        for cp in recvs:
            cp.wait_recv()
        for cp in sends:
            cp.wait_send()
        for cp in local:
            cp.wait()


class _Gather2(_Comm):
    def __init__(self, arrs):
        super().__init__(arrs, [False] * len(arrs))

    def _plan(self, x_refs, o_refs, sems):
        send_sems, recv_sems, local_sems = sems
        mx, my, mc = lax.axis_index("x"), lax.axis_index("y"), lax.axis_index("c")
        slot = lambda px, py, pc: 4 * px + 2 * py + pc
        sib = (mx, my, 1 - mc)
        chips = [(1 - mx, my), (mx, 1 - my), (1 - mx, 1 - my)]
        np_ = N_DEV - 1
        local, first, passed, arrive_first, arrive_rest = [], [], [], [], []

        def copy(a, k, src, block, to):
            return pltpu.make_async_remote_copy(
                src_ref=src, dst_ref=o_refs[a].at[block], send_sem=send_sems.at[a * np_ + k], recv_sem=recv_sems.at[a * np_ + k],
                device_id=to, device_id_type=MESH)

        for a in range(self.n):
            me = slot(mx, my, mc)
            local.append(pltpu.make_async_copy(x_refs[a], o_refs[a].at[me], local_sems.at[a]))
            first.append(copy(a, 0, x_refs[a], me, sib))
            arrive_rest.append(copy(a, 0, x_refs[a], slot(*sib), sib))
            for j, (cx, cy) in enumerate(chips):
                first.append(copy(a, 1 + j, x_refs[a], me, (cx, cy, mc)))
                arrive_first.append(copy(a, 1 + j, x_refs[a], slot(cx, cy, mc), (cx, cy, mc)))
                passed.append(copy(a, 4 + j, o_refs[a].at[slot(cx, cy, mc)], slot(cx, cy, mc), sib))
                arrive_rest.append(copy(a, 4 + j, x_refs[a], slot(cx, cy, 1 - mc), sib))
        return local, first, passed, arrive_first, arrive_rest

    def start(self, x_refs, o_refs, sems):
        local, first, _, _, _ = self._plan(x_refs, o_refs, sems)
        for cp in local + first:
            cp.start()

    def wait(self, x_refs, o_refs, sems):
        local, first, passed, arrive_first, arrive_rest = self._plan(x_refs, o_refs, sems)
        for arrived, onward in zip(arrive_first, passed):
            arrived.wait_recv()
            onward.start()
        for cp in arrive_rest:
            cp.wait_recv()
        for cp in first + passed:
            cp.wait_send()
        for cp in local:
            cp.wait()


def _call(body, *, name, grid, in_specs, out_specs, out_shape, args, scratch_shapes=(), sem=None, comm=None):
    if comm is None:
        outs = pl.pallas_call(
            body, name=name, grid=grid, in_specs=list(in_specs), out_specs=list(out_specs), out_shape=list(out_shape),
            scratch_shapes=list(scratch_shapes), compiler_params=_cp(sem),
        )(*args)
        return list(outs), []
    n_in, n_out, nc = len(in_specs), len(out_specs), comm.n
    nsteps = 1
    for g in grid:
        nsteps *= g

    def carrier(*refs):
        ins, cin = refs[:n_in], refs[n_in:n_in + nc]
        outs, cout = refs[n_in + nc:n_in + nc + n_out], refs[n_in + nc + n_out:n_in + 2 * nc + n_out]
        rest = refs[n_in + 2 * nc + n_out:]
        scratch, sems = rest[:len(rest) - 3], rest[len(rest) - 3:]
        if nsteps == 1:
            comm.start(cin, cout, sems)
            body(*ins, *outs, *scratch)
            comm.wait(cin, cout, sems)
            return
        step = 0
        for d, g in enumerate(grid):
            step = step * g + pl.program_id(d)

        @pl.when(step == 0)
        def _():
            comm.start(cin, cout, sems)

        body(*ins, *outs, *scratch)

        @pl.when(step == nsteps - 1)
        def _():
            comm.wait(cin, cout, sems)

    outs = pl.pallas_call(
        carrier, name=name, grid=grid, in_specs=list(in_specs) + comm.specs, out_specs=list(out_specs) + comm.specs,
        out_shape=list(out_shape) + comm.out_shape, scratch_shapes=list(scratch_shapes) + comm.scratch,
        compiler_params=_cp(("arbitrary",) * len(grid) if grid else None),
    )(*args, *comm.arrs)
    return list(outs[:n_out]), list(outs[n_out:])


def _exchange(comm, name):
    return _call(lambda *refs: None, name=name, grid=(), in_specs=[], out_specs=[], out_shape=[], args=[], comm=comm)[1]


def _pack(arrs, dtype, lead=()):
    nl = len(lead)
    flat = jnp.concatenate([a.astype(dtype).reshape(lead + (-1,)) for a in arrs], axis=nl)
    n = flat.shape[-1]
    rows = -(-n // (LANES * ADAM_TR)) * ADAM_TR
    flat = jnp.pad(flat, [(0, 0)] * nl + [(0, rows * LANES - n)])
    return flat.reshape(lead + (rows, LANES))


def _unpack(flat, shapes, lead=()):
    nl = len(lead)
    flat = flat.reshape(lead + (-1,))
    out, o = [], 0
    for s in shapes:
        n = 1
        for d in s:
            n *= d
        out.append(lax.slice_in_dim(flat, o, o + n, axis=nl).reshape(lead + tuple(s)))
        o += n
    return out


def _join(g, ax):
    return jnp.concatenate([g[d] for d in range(N_DEV)], axis=ax)


def _split(full, ax):
    n = full.shape[ax] // N_DEV
    return jnp.stack([lax.slice_in_dim(full, d * n, (d + 1) * n, axis=ax) for d in range(N_DEV)])


_WEIGHTS = ['norm_mix', 'norm_ffn', 'attn_w_in', 'attn_w_out', 'relpos_table', 'q_norm_a', 'k_norm_a', 'q_norm_b',
            'k_norm_b', 'sinks', 'ssm_w_in', 'ssm_conv_w', 'ssm_conv_b', 'ssm_dt_bias', 'ssm_a_log', 'ssm_d', 'ssm_norm',
            'ssm_w_out', 'ffn_w_in', 'ffn_conv_w', 'ffn_conv_b', 'ffn_w_out']
_SHARD_AX = {'attn_w_in': 2, 'attn_w_out': 1, 'ssm_w_in': 2, 'ssm_conv_w': 2, 'ssm_conv_b': 1, 'ssm_norm': 1,
             'ssm_w_out': 1, 'ffn_w_in': 2, 'ffn_conv_w': 2, 'ffn_w_out': 1}
_BIG = ['attn_w_in', 'attn_w_out', 'ssm_w_in', 'ssm_w_out', 'ffn_w_in', 'ffn_w_out']
_SMALL = ['ssm_conv_w', 'ssm_conv_b', 'ssm_norm', 'ffn_conv_w']
_AX2 = {n: _SHARD_AX[n] - 1 for n in _BIG}
_REPL = [n for n in _WEIGHTS if n not in _SHARD_AX]


def _rows8(w):
    return jnp.pad(w, ((0, 8 - w.shape[0]), (0, 0)))


def _lanes128(v):
    return jnp.pad(v, (0, LANES - v.shape[0])).reshape(1, LANES)


def _band_mask(n_prev, pad):
    cq = jnp.arange(TQ)[:, None] // CHUNK
    ck = jnp.arange(pad + TQ)[None, :] // CHUNK
    return (ck >= cq) & (ck <= cq + n_prev)


def _ffn_fwd(xin, g, w_in_t, w8, cb, tag):
    gu, h, a = _ffn_in_mid(xin, g, w_in_t, w8, cb, f"mm_ffn_in{tag}")
    return a, (h, gu, a)


def _ffn_bwd(dx, dxb, xin, g, w_in_t, w8, cb, w_out, saved, tag):
    h, gu, a = saved
    dw_out = _mm_tn(a, dxb, f"mm_ffn_dwout{tag}")
    dgu, dw8, dcb = _ffn_mid_bwd(gu, dxb, w_out, w8, cb, f"ffn_mid_bwd{tag}")
    dw_in_t = _mm_tn(dgu, h, f"mm_ffn_dwin{tag}")
    dxp, dxpb, dg = _mm_rms_bwd(dgu, w_in_t, 0, None, xin, g, dx, f"mm_ffn_dh{tag}")
    return dxp, dxpb, dg, dw_in_t, dw8[:3], dcb, dw_out


def kernel(x, norm_mix, norm_ffn, attn_w_in, attn_w_out, relpos_table, q_norm_a, k_norm_a, q_norm_b, k_norm_b, sinks, ssm_w_in, ssm_conv_w, ssm_conv_b, ssm_dt_bias, ssm_a_log, ssm_d, ssm_norm, ssm_w_out, ffn_w_in, ffn_conv_w, ffn_conv_b, ffn_w_out, loss_target, m_norm_mix, m_norm_ffn, m_attn_w_in, m_attn_w_out, m_relpos_table, m_q_norm_a, m_k_norm_a, m_q_norm_b, m_k_norm_b, m_sinks, m_ssm_w_in, m_ssm_conv_w, m_ssm_conv_b, m_ssm_dt_bias, m_ssm_a_log, m_ssm_d, m_ssm_norm, m_ssm_w_out, m_ffn_w_in, m_ffn_conv_w, m_ffn_conv_b, m_ffn_w_out, v_norm_mix, v_norm_ffn, v_attn_w_in, v_attn_w_out, v_relpos_table, v_q_norm_a, v_k_norm_a, v_q_norm_b, v_k_norm_b, v_sinks, v_ssm_w_in, v_ssm_conv_w, v_ssm_conv_b, v_ssm_dt_bias, v_ssm_a_log, v_ssm_d, v_ssm_norm, v_ssm_w_out, v_ffn_w_in, v_ffn_conv_w, v_ffn_conv_b, v_ffn_w_out):
    w = dict(norm_mix=norm_mix, norm_ffn=norm_ffn, attn_w_in=attn_w_in, attn_w_out=attn_w_out, relpos_table=relpos_table,
             q_norm_a=q_norm_a, k_norm_a=k_norm_a, q_norm_b=q_norm_b, k_norm_b=k_norm_b, sinks=sinks, ssm_w_in=ssm_w_in,
             ssm_conv_w=ssm_conv_w, ssm_conv_b=ssm_conv_b, ssm_dt_bias=ssm_dt_bias, ssm_a_log=ssm_a_log, ssm_d=ssm_d,
             ssm_norm=ssm_norm, ssm_w_out=ssm_w_out, ffn_w_in=ffn_w_in, ffn_conv_w=ffn_conv_w, ffn_conv_b=ffn_conv_b,
             ffn_w_out=ffn_w_out)
    mom = dict(norm_mix=m_norm_mix, norm_ffn=m_norm_ffn, attn_w_in=m_attn_w_in, attn_w_out=m_attn_w_out,
               relpos_table=m_relpos_table, q_norm_a=m_q_norm_a, k_norm_a=m_k_norm_a, q_norm_b=m_q_norm_b,
               k_norm_b=m_k_norm_b, sinks=m_sinks, ssm_w_in=m_ssm_w_in, ssm_conv_w=m_ssm_conv_w, ssm_conv_b=m_ssm_conv_b,
               ssm_dt_bias=m_ssm_dt_bias, ssm_a_log=m_ssm_a_log, ssm_d=m_ssm_d, ssm_norm=m_ssm_norm, ssm_w_out=m_ssm_w_out,
               ffn_w_in=m_ffn_w_in, ffn_conv_w=m_ffn_conv_w, ffn_conv_b=m_ffn_conv_b, ffn_w_out=m_ffn_w_out)
    var = dict(norm_mix=v_norm_mix, norm_ffn=v_norm_ffn, attn_w_in=v_attn_w_in, attn_w_out=v_attn_w_out,
               relpos_table=v_relpos_table, q_norm_a=v_q_norm_a, k_norm_a=v_k_norm_a, q_norm_b=v_q_norm_b,
               k_norm_b=v_k_norm_b, sinks=v_sinks, ssm_w_in=v_ssm_w_in, ssm_conv_w=v_ssm_conv_w, ssm_conv_b=v_ssm_conv_b,
               ssm_dt_bias=v_ssm_dt_bias, ssm_a_log=v_ssm_a_log, ssm_d=v_ssm_d, ssm_norm=v_ssm_norm, ssm_w_out=v_ssm_w_out,
               ffn_w_in=v_ffn_w_in, ffn_conv_w=v_ffn_conv_w, ffn_conv_b=v_ffn_conv_b, ffn_w_out=v_ffn_w_out)

    def piece(n, l):
        return (w[n][l].T if _AX2[n] == 1 else w[n][l]).astype(BF16)

    def gather_of(names_layers):
        return _Gather2([piece(n, l) for n, l in names_layers])

    def joined(got):
        return [g.reshape(-1, D_MODEL) for g in got]

    first = [('attn_w_in', 0), ('attn_w_out', 0)]
    got = _exchange(_Gather2([piece(n, l) for n, l in first] + [_pack([w[n] for n in _SMALL], F32)]), "gather_attn")
    wt_attn_in, w_attn_out = joined(got[:2])
    full = {}
    for n, g in zip(_SMALL, _unpack(got[2], [w[n].shape for n in _SMALL], lead=(N_DEV,))):
        full[n] = _join(g, _SHARD_AX[n])
    ssm_cw8 = _rows8(full['ssm_conv_w'][0])
    ssm_cb = full['ssm_conv_b']
    ssm_nw = full['ssm_norm']
    ffn_cw8 = [_rows8(full['ffn_conv_w'][l]) for l in range(2)]
    ffn_cb = [ffn_conv_b[l:l + 1] for l in range(2)]

    x0 = x[0]
    target = loss_target[0]
    t = x0.shape[0]

    g_mix0, g_mix1 = norm_mix[0:1], norm_mix[1:2]
    g_ffn0, g_ffn1 = norm_ffn[0:1], norm_ffn[1:2]
    proj, h0 = _rms_mm(x0, g_mix0, wt_attn_in, 2304, "mm_attn_in", F32)
    hn_w = jnp.concatenate([jnp.tile(v, (1, 2)) for v in (q_norm_a, k_norm_a, q_norm_b, k_norm_b)], axis=0)
    qa, kpa, vpa, qb, kpb, vpb = _headnorm_fwd(proj, hn_w, "headnorm")
    table = jnp.pad(relpos_table[0], ((0, 0), (0, REL_W - (2 * MAX_REL + 1))))
    bias_a = jnp.where(_band_mask(A_PREV, PAD_A)[None], jnp.transpose(_relpos_fwd(table, "relpos_bias"), (1, 0, 2)), NEG)
    rel_b = jnp.arange(TQ)[:, None] - (jnp.arange(PAD_B + TQ)[None, :] - PAD_B)
    slopes = 2.0 ** (-8.0 * jnp.arange(1, N_HEADS + 1, dtype=F32) / N_HEADS)
    bias_b = jnp.where(_band_mask(B_PREV, PAD_B)[None], -slopes[:, None, None] * jnp.abs(rel_b).astype(F32)[None], NEG)
    no_sinks = jnp.full((N_HEADS,), NEG, F32)
    ffn0_w, ssm_w, ffn1_w = [('ffn_w_in', 0), ('ffn_w_out', 0)], [('ssm_w_in', 0), ('ssm_w_out', 0)], [('ffn_w_in', 1), ('ffn_w_out', 1)]
    oa, stats_a, got = _attn_fwd(qa, kpa, vpa, bias_a, no_sinks, PAD_A, "attn_a", comm=gather_of(ffn0_w + ssm_w))
    wt_ffn_in0, w_ffn_out0, wt_ssm_in, w_ssm_out = joined(got)
    ob, stats_b, got = _attn_fwd(qb, kpb, vpb, bias_b, sinks[0], PAD_B, "attn_b", comm=gather_of(ffn1_w))
    wt_ffn_in1, w_ffn_out1 = joined(got)
    wt_ssm_dt = jnp.pad(wt_ssm_in[ZX:], ((0, LANES - SSM_HEADS), (0, 0)))
    x1 = _mm(oa, w_attn_out, "mm_attn_out_a", res=x0, b_rows=(0, 512))
    x1 = _mm(ob, w_attn_out, "mm_attn_out_b", res=x1, b_rows=(512, 512))
    a0, ffn0_saved = _ffn_fwd(x1, g_ffn0, wt_ffn_in0, ffn_cw8[0], ffn_cb[0], "0")
    x2 = _mm(a0, w_ffn_out0, "mm_ffn_out0", res=x1)

    zx, h2, xbc = _ssm_in_pre(x2, g_mix1, wt_ssm_in, ssm_cw8, ssm_cb, "mm_ssm_in")
    dtraw = _mm(h2, wt_ssm_dt, "mm_ssm_dt", trans_b=True)
    dt_bias = _lanes128(ssm_dt_bias[0])
    alog = _lanes128(ssm_a_log[0])
    dexp = jnp.repeat(ssm_d[0], HEAD_DIM).reshape(1, D_INNER)
    dt = _dt_fwd(dtraw, dt_bias, "ssm_dt")
    (y, sprev, y4), _ = _ssd_fwd(xbc, dt, alog, zx, dexp, ssm_nw, "ssd_fwd")
    x3 = _mm(y4, w_ssm_out, "mm_ssm_out", res=x2)
    a1, ffn1_saved = _ffn_fwd(x3, g_ffn1, wt_ffn_in1, ffn_cw8[1], ffn_cb[1], "1")

    dx4, dx4b, sq = _mm_loss(a1, w_ffn_out1, x3, target, "mm_ffn_out1_loss")
    loss = lax.psum(0.5 * jnp.sum(sq) / D_MODEL, ("x", "y", "c"))

    grads = {}

    def scatter_of(grads_2d):
        return _Comm([g.reshape(N_DEV, -1, D_MODEL) for g in grads_2d], [True] * len(grads_2d))

    dx3, dx3b, dg_ffn1, dwtin1, dcw1, dcb1, dwout1 = _ffn_bwd(
        dx4, dx4b, x3, g_ffn1, wt_ffn_in1, ffn_cw8[1], ffn_cb[1], w_ffn_out1, ffn1_saved, "1")

    dy4 = _mm(dx3b, w_ssm_out, "mm_ssm_dy", trans_b=True)
    dw_ssm_out = _mm_tn(y4, dx3b, "mm_ssm_dwout")
    (dxbc, ddt, dalog, dz, dd_lane, dnw), parts_ffn1 = _ssd_bwd(
        xbc, dt, alog, sprev, dy4, y, zx, dexp, ssm_nw, "ssd_bwd", comm=scatter_of([dwtin1, dwout1]))
    dxr, dcw_s, dcb_s = _ssm_pre_bwd(zx, dxbc, ssm_cw8, ssm_cb, "ssm_pre_bwd")
    ddtraw, ddtb = _dt_bwd(dtraw, dt_bias, ddt, "ssm_dt_bwd")
    dh2 = _mm(dz, wt_ssm_in, "mm_ssm_dh_z", b_rows=(0, D_INNER))
    dh2 = _mm(dxr, wt_ssm_in[D_INNER:ZX], "mm_ssm_dh_x", res=dh2)
    dwt_ssm_in = jnp.concatenate([
        _mm_tn(dz, h2, "mm_ssm_dwin_z"), _mm_tn(dxr, h2, "mm_ssm_dwin_x"),
        _mm_tn(ddtraw, h2, "mm_ssm_dwin_dt")[:SSM_HEADS]], axis=0)
    dx2, dx2b, dg_mix1 = _mm_rms_bwd(ddtraw, wt_ssm_dt, 0, dh2, x2, g_mix1, dx3, "mm_ssm_dh_dt")
    grads['ssm_conv_w'] = dcw_s[:4][None]
    grads['ssm_conv_b'] = dcb_s
    grads['ssm_norm'] = dnw
    grads['ssm_dt_bias'] = ddtb[:, :SSM_HEADS]
    grads['ssm_a_log'] = dalog[:, :SSM_HEADS]
    grads['ssm_d'] = jnp.sum(dd_lane.reshape(SSM_HEADS, HEAD_DIM), axis=1)[None]

    dx1, dx1b, dg_ffn0, dwtin0, dcw0, dcb0, dwout0 = _ffn_bwd(
        dx2, dx2b, x1, g_ffn0, wt_ffn_in0, ffn_cw8[0], ffn_cb[0], w_ffn_out0, ffn0_saved, "0")
    grads['ffn_conv_w'] = jnp.stack([dcw0, dcw1])
    grads['ffn_conv_b'] = jnp.concatenate([dcb0, dcb1], axis=0)
    grads['norm_ffn'] = jnp.concatenate([dg_ffn0, dg_ffn1], axis=0)

    do = _mm(dx1b, w_attn_out, "mm_attn_do", out_dtype=BF16, trans_b=True)
    dw_attn_out = jnp.concatenate([_mm_tn(oa, dx1b, "mm_attn_dwout_a"), _mm_tn(ob, dx1b, "mm_attn_dwout_b")], axis=0)
    (dqa, dkpa, dvpa, dbias_a, _), parts_ssm = _attn_bwd(
        qa, kpa, vpa, bias_a, no_sinks, do, stats_a, 0, PAD_A, "attn_a_bwd",
        comm=scatter_of([dwt_ssm_in, dw_ssm_out, dw_attn_out]))
    (dqb, dkpb, dvpb, _, dsink), parts_ffn0 = _attn_bwd(
        qb, kpb, vpb, bias_b, sinks[0], do, stats_b, 4, PAD_B, "attn_b_bwd", comm=scatter_of([dwtin0, dwout0]))
    grads['relpos_table'] = _relpos_bwd(jnp.transpose(dbias_a, (1, 0, 2)), "relpos_bwd")[None, :, :2 * MAX_REL + 1]
    grads['sinks'] = dsink[:, :2, 0].reshape(1, N_HEADS)
    dproj, dhn = _headnorm_bwd(proj, hn_w, dqa, dkpa, dvpa, dqb, dkpb, dvpb, "headnorm_bwd")
    dhn = dhn[:, :HEAD_DIM] + dhn[:, HEAD_DIM:]
    for k, n in enumerate(('q_norm_a', 'k_norm_a', 'q_norm_b', 'k_norm_b')):
        grads[n] = dhn[k:k + 1]
    dwt_attn_in = _mm_tn(dproj, h0, "mm_attn_dwin")
    dx0, _, dg_mix0, parts_attn_in = _mm_rms_bwd(dproj, wt_attn_in, 0, None, x0, g_mix0, dx1, "mm_attn_dh",
                                                 comm=scatter_of([dwt_attn_in]))
    grads['norm_mix'] = jnp.concatenate([dg_mix0, dg_mix1], axis=0)

    def summed_t(parts, name):
        return _sum_parts(parts, name).T[None]

    sm_shapes = [w[n].shape for n in _SMALL]
    rp_shapes = [w[n].shape for n in _REPL]
    recv = _exchange(_Comm(
        [_pack([_split(grads[n], _SHARD_AX[n]) for n in _SMALL], F32, lead=(N_DEV,)), _pack([grads[n] for n in _REPL], F32)],
        [True, False]), "exchange_small")
    big_parts = {
        'attn_w_in': [summed_t(parts_attn_in[0], "sum_attn_w_in")], 'attn_w_out': [parts_ssm[2]],
        'ssm_w_in': [summed_t(parts_ssm[0], "sum_ssm_w_in")], 'ssm_w_out': [parts_ssm[1]],
        'ffn_w_in': [summed_t(parts_ffn0[0], "sum_ffn_w_in0"), summed_t(parts_ffn1[0], "sum_ffn_w_in1")],
        'ffn_w_out': [parts_ffn0[1], parts_ffn1[1]],
    }
    res = [{}, {}, {}, {}]
    for n in _BIG:
        for kind, a in enumerate(_adamw(big_parts[n], w[n], mom[n], var[n], f"adamw_{n}")):
            res[kind][n] = a
    for names, shapes, parts in ((_SMALL, sm_shapes, recv[0]), (_REPL, rp_shapes, recv[1])):
        outs = _adamw([parts], _pack([w[n] for n in names], F32)[None], _pack([mom[n] for n in names], F32)[None],
                      _pack([var[n] for n in names], F32)[None], "adamw_" + ("small" if names is _SMALL else "replicated"))
        for kind, flat in enumerate(outs):
            for n, a in zip(names, _unpack(flat[0], shapes)):
                res[kind][n] = a
    return (loss, dx0[None], *[res[0][n] for n in _WEIGHTS], *[res[1][n] for n in _WEIGHTS],
            *[res[2][n] for n in _WEIGHTS], *[res[3][n] for n in _WEIGHTS])
```

```python
import jax
import jax.numpy as jnp
from jax import lax
from jax.experimental import pallas as pl
from jax.experimental.pallas import tpu as pltpu

F32 = jnp.float32
BF16 = jnp.bfloat16
HI = lax.Precision.HIGHEST
MESH = pl.DeviceIdType.MESH
NEG = -1e30

N_DEV = 8
D_MODEL = 1024
EPS = 1e-6
CHUNK = 64
HEAD_DIM = 64
N_HEADS = 8
A_PREV = 8
B_PREV = 2
MAX_REL = 256
TQ = 2 * CHUNK
ATT_SUB = 4
PAD_A = A_PREV * CHUNK
PAD_B = B_PREV * CHUNK
REL_W = PAD_A + TQ
D_INNER = 2048
SSM_HEADS = 32
SSM_GROUPS = 4
SSM_STATE = 128
XBC = D_INNER + 2 * SSM_GROUPS * SSM_STATE
ZX = D_INNER + XBC
D_FF = 2816
SSD_L = 128
LANES = 128
VMEM_LIMIT = 56 << 20

ADAM_LR, ADAM_B1, ADAM_B2, ADAM_EPS, ADAM_WD, ADAM_STEP = 0.001, 0.9, 0.999, 1e-08, 0.01, 10


def _cp(sem=None):
    return pltpu.CompilerParams(dimension_semantics=sem, vmem_limit_bytes=VMEM_LIMIT)


def _dot(a, b, ca=1, cb=0, prec=None):
    return lax.dot_general(a, b, (((ca,), (cb,)), ((), ())), preferred_element_type=F32, precision=prec)


def _pick(n, cands):
    for c in cands:
        if n % c == 0:
            return c
    return n


def _lo_mask():
    return lax.broadcasted_iota(jnp.int32, (1, LANES), 1) < HEAD_DIM


_TN_CHUNKS = (1408, 1536, 1152, 1024, 512, 256, 128)


TN_MAX_ROWS = 3072


def _mm_tn(a, b, name):
    kdim, m = a.shape
    n = b.shape[1]
    assert b.shape[0] == kdim, (a.shape, b.shape)
    mb = m if m <= TN_MAX_ROWS else m // 2
    tn = _pick(n, _TN_CHUNKS)
    tk = _pick(kdim, (512, 256, 128))
    nk = kdim // tk

    def body(a_ref, b_ref, o_ref, acc):
        k = pl.program_id(1)

        @pl.when(k == 0)
        def _():
            acc[...] = jnp.zeros_like(acc)

        av = a_ref[...]
        for c in range(0, n, tn):
            acc[:, c:c + tn] += _dot(av, b_ref[:, c:c + tn], 0, 0)

        @pl.when(k == nk - 1)
        def _():
            o_ref[...] = acc[...].astype(BF16)

    return pl.pallas_call(
        body, name=name, grid=(m // mb, nk),
        in_specs=[pl.BlockSpec((tk, mb), lambda j, k: (k, j)), pl.BlockSpec((tk, n), lambda j, k: (k, 0))],
        out_specs=pl.BlockSpec((mb, n), lambda j, k: (j, 0)), out_shape=jax.ShapeDtypeStruct((m, n), BF16),
        scratch_shapes=[pltpu.VMEM((mb, n), F32)], compiler_params=_cp(("parallel", "arbitrary")),
    )(a, b)


def _mm(a, b, name, out_dtype=F32, res=None, trans_b=False, b_rows=None):
    m, kdim = a.shape
    if b_rows is None:
        b_rows = (0, b.shape[0])
    off, rows = b_rows
    n = rows if trans_b else b.shape[1]
    assert (b.shape[1] if trans_b else rows) == kdim and off % rows == 0, (a.shape, b.shape, b_rows)
    tn = _pick(n, _TN_CHUNKS)
    tm = _pick(m, (256, 128) if n > 2304 else (512, 256, 128))

    def body(*refs):
        if res is None:
            a_ref, b_ref, o_ref = refs
        else:
            a_ref, b_ref, r_ref, o_ref = refs
        av = a_ref[...]
        for c in range(0, n, tn):
            r = _dot(av, b_ref[c:c + tn, :], 1, 1) if trans_b else _dot(av, b_ref[:, c:c + tn], 1, 0)
            if res is not None:
                r = r + r_ref[:, c:c + tn]
            o_ref[:, c:c + tn] = r.astype(out_dtype)

    in_specs = [pl.BlockSpec((tm, kdim), lambda i: (i, 0)), pl.BlockSpec((rows, b.shape[1]), lambda i: (off // rows, 0))]
    args = [a, b]
    if res is not None:
        in_specs.append(pl.BlockSpec((tm, n), lambda i: (i, 0)))
        args.append(res)
    return pl.pallas_call(
        body, name=name, grid=(m // tm,), in_specs=in_specs, out_specs=pl.BlockSpec((tm, n), lambda i: (i, 0)),
        out_shape=jax.ShapeDtypeStruct((m, n), out_dtype), compiler_params=_cp(("parallel",)),
    )(*args)


def _rms_mm(x, g, bt, n, name, out_dtype):
    t, d = x.shape
    tn = _pick(n, _TN_CHUNKS)
    tm = _pick(t, (256, 128))

    def body(x_ref, g_ref, b_ref, o_ref, h_ref):
        xv = x_ref[...]
        r = lax.rsqrt(jnp.mean(xv * xv, axis=-1, keepdims=True) + EPS)
        h = (xv * r * g_ref[...]).astype(BF16)
        h_ref[...] = h
        for c in range(0, n, tn):
            o_ref[:, c:c + tn] = _dot(h, b_ref[c:c + tn, :], 1, 1).astype(out_dtype)

    row = pl.BlockSpec((tm, d), lambda i: (i, 0))
    return pl.pallas_call(
        body, name=name, grid=(t // tm,),
        in_specs=[row, pl.BlockSpec((1, d), lambda i: (0, 0)), pl.BlockSpec(bt.shape, lambda i: (0, 0))],
        out_specs=[pl.BlockSpec((tm, n), lambda i: (i, 0)), row],
        out_shape=[jax.ShapeDtypeStruct((t, n), out_dtype), jax.ShapeDtypeStruct((t, d), BF16)],
        compiler_params=_cp(("parallel",)),
    )(x, g, bt)


def _mm_rms_bwd(a, b, b_off, dh_prev, x, g, dres, name, comm=None):
    t, d = x.shape
    kdim = a.shape[1]
    assert b_off % kdim == 0 and b.shape[1] == d, (a.shape, b.shape, b_off)
    tm = _pick(t, (256, 128))

    def body(*refs):
        if dh_prev is None:
            a_ref, b_ref, x_ref, g_ref, dr_ref, dx_ref, dxb_ref, dg_ref = refs
            dhv = _dot(a_ref[...], b_ref[...], 1, 0)
        else:
            a_ref, b_ref, p_ref, x_ref, g_ref, dr_ref, dx_ref, dxb_ref, dg_ref = refs
            dhv = _dot(a_ref[...], b_ref[...], 1, 0) + p_ref[...]
        xv = x_ref[...]
        r = lax.rsqrt(jnp.mean(xv * xv, axis=-1, keepdims=True) + EPS)
        xh = xv * r
        dxh = dhv * g_ref[...]
        dx = dr_ref[...] + r * (dxh - xh * jnp.mean(dxh * xh, axis=-1, keepdims=True))
        dx_ref[...] = dx
        dxb_ref[...] = dx.astype(BF16)

        @pl.when(pl.program_id(0) == 0)
        def _():
            dg_ref[...] = jnp.zeros_like(dg_ref)

        dg_ref[...] += jnp.sum(dhv * xh, axis=0, keepdims=True)

    row = pl.BlockSpec((tm, d), lambda i: (i, 0))
    vec = pl.BlockSpec((1, d), lambda i: (0, 0))
    in_specs = [pl.BlockSpec((tm, kdim), lambda i: (i, 0)), pl.BlockSpec((kdim, d), lambda i: (b_off // kdim, 0))]
    args = [a, b]
    if dh_prev is not None:
        in_specs.append(row)
        args.append(dh_prev)
    outs, got = _call(
        body, name=name, grid=(t // tm,), in_specs=in_specs + [row, vec, row], out_specs=[row, row, vec],
        out_shape=[jax.ShapeDtypeStruct((t, d), F32), jax.ShapeDtypeStruct((t, d), BF16), jax.ShapeDtypeStruct((1, d), F32)],
        args=(*args, x, g, dres), sem=("arbitrary",), comm=comm)
    return (*outs, got) if comm is not None else tuple(outs)


def _mm_loss(a, b, res, target, name):
    t, kdim = a.shape
    d = b.shape[1]
    tm = _pick(t, (512, 256, 128))

    def body(a_ref, b_ref, r_ref, t_ref, dy_ref, dyb_ref, acc_ref):
        @pl.when(pl.program_id(0) == 0)
        def _():
            acc_ref[...] = jnp.zeros_like(acc_ref)

        err = _dot(a_ref[...], b_ref[...], 1, 0) + r_ref[...] - t_ref[...]
        dy = err * (1.0 / d)
        dy_ref[...] = dy
        dyb_ref[...] = dy.astype(BF16)
        acc_ref[...] += jnp.sum(err * err, axis=0, keepdims=True)

    row = pl.BlockSpec((tm, d), lambda i: (i, 0))
    vec = pl.BlockSpec((1, d), lambda i: (0, 0))
    return pl.pallas_call(
        body, name=name, grid=(t // tm,),
        in_specs=[pl.BlockSpec((tm, kdim), lambda i: (i, 0)), pl.BlockSpec((kdim, d), lambda i: (0, 0)), row, row],
        out_specs=[row, row, vec],
        out_shape=[jax.ShapeDtypeStruct((t, d), F32), jax.ShapeDtypeStruct((t, d), BF16), jax.ShapeDtypeStruct((1, d), F32)],
        compiler_params=_cp(("arbitrary",)),
    )(a, b, res, target)


def _head_sums(v):
    ri = lax.broadcasted_iota(jnp.int32, (LANES, LANES), 0) // HEAD_DIM
    ci = lax.broadcasted_iota(jnp.int32, (LANES, LANES), 1) // HEAD_DIM
    ones = (ri == ci).astype(BF16)
    hi = v.astype(BF16)
    lo_part = (v - hi.astype(F32)).astype(BF16)
    return _dot(hi, ones, 1, 0) + _dot(lo_part, ones, 1, 0)


def _head_rms(xs, w, lo):
    r = lax.rsqrt(_head_sums(xs * xs) * (1.0 / HEAD_DIM) + EPS)
    return xs * r, r


def _head_rms_bwd(xs, w, dy, lo):
    xh, r = _head_rms(xs, w, lo)
    dxh = dy * w
    mm = _head_sums(dxh * xh) * (1.0 / HEAD_DIM)
    return r * (dxh - xh * mm), dy * xh


_QSCALE = HEAD_DIM ** -0.5


def _headnorm_fwd(proj, ws, name):
    t = proj.shape[0]
    tm = TQ
    lead = PAD_A // tm
    leadb = PAD_B // tm

    def body(p_ref, w_ref, qa_ref, ka_ref, va_ref, qb_ref, kb_ref, vb_ref):
        data = pl.program_id(0) >= lead
        lo = _lo_mask()

        def put(ref, c, val):
            ref[:, c:c + val.shape[1]] = jnp.where(data, val, 0.0).astype(BF16)

        def per_query_head(slab):
            other = pltpu.roll(slab, HEAD_DIM, 1)
            e0, e1 = jnp.where(lo, slab, other), jnp.where(lo, other, slab)
            return jnp.concatenate([e0, e0, e1, e1], axis=1)

        for s in range(4):
            c = LANES * s
            xh, _ = _head_rms(p_ref[:, c:c + LANES], None, lo)
            qa_ref[:, c:c + LANES] = (xh * w_ref[0:1, :] * _QSCALE).astype(BF16)
            xh, _ = _head_rms(p_ref[:, 512 + c:512 + c + LANES], None, lo)
            put(ka_ref, c, xh * w_ref[1:2, :])
            xh, _ = _head_rms(p_ref[:, 1536 + c:1536 + c + LANES], None, lo)
            qb_ref[:, c:c + LANES] = (xh * w_ref[2:3, :] * _QSCALE).astype(BF16)
        put(va_ref, 0, p_ref[:, 1024:1536])
        xh, _ = _head_rms(p_ref[:, 2048:2176], None, lo)
        put(kb_ref, 0, per_query_head(xh * w_ref[3:4, :]))
        put(vb_ref, 0, per_query_head(p_ref[:, 2176:2304]))

    src = lambda i: jnp.maximum(i - lead, 0)
    wide = pl.BlockSpec((tm, 512), lambda i: (src(i), 0))
    pad_a = pl.BlockSpec((tm, 512), lambda i: (i, 0))
    pad_b = pl.BlockSpec((tm, 512), lambda i: (jnp.maximum(i - lead + leadb, 0), 0))
    sd = lambda rows: jax.ShapeDtypeStruct((rows, 512), BF16)
    return pl.pallas_call(
        body, name=name, grid=(t // tm + lead,),
        in_specs=[pl.BlockSpec((tm, 2304), lambda i: (src(i), 0)), pl.BlockSpec((4, LANES), lambda i: (0, 0))],
        out_specs=[wide, pad_a, pad_a, wide, pad_b, pad_b],
        out_shape=[sd(t), sd(t + PAD_A), sd(t + PAD_A), sd(t), sd(t + PAD_B), sd(t + PAD_B)],
        compiler_params=_cp(("arbitrary",)),
    )(proj, ws)


def _headnorm_bwd(proj, ws, dqa, dkpa, dvpa, dqb, dkpb, dvpb, name):
    t = proj.shape[0]
    tm = TQ
    offa, offb = PAD_A // tm, PAD_B // tm

    def body(p_ref, w_ref, dqa_ref, dka_ref, dva_ref, dqb_ref, dkb_ref, dvb_ref, dp_ref, dw_ref):
        i = pl.program_id(0)
        lo = _lo_mask()

        @pl.when(i == 0)
        def _():
            dw_ref[...] = jnp.zeros_like(dw_ref)

        acc = [jnp.zeros((1, LANES), F32) for _ in range(4)]
        for s in range(4):
            c = LANES * s
            dx, dwl = _head_rms_bwd(p_ref[:, c:c + LANES], w_ref[0:1, :], dqa_ref[:, c:c + LANES] * _QSCALE, lo)
            dp_ref[:, c:c + LANES] = dx.astype(BF16)
            acc[0] += jnp.sum(dwl, axis=0, keepdims=True)
            dx, dwl = _head_rms_bwd(p_ref[:, 512 + c:512 + c + LANES], w_ref[1:2, :], dka_ref[:, c:c + LANES], lo)
            dp_ref[:, 512 + c:512 + c + LANES] = dx.astype(BF16)
            acc[1] += jnp.sum(dwl, axis=0, keepdims=True)
            dx, dwl = _head_rms_bwd(p_ref[:, 1536 + c:1536 + c + LANES], w_ref[2:3, :], dqb_ref[:, c:c + LANES] * _QSCALE, lo)
            dp_ref[:, 1536 + c:1536 + c + LANES] = dx.astype(BF16)
            acc[2] += jnp.sum(dwl, axis=0, keepdims=True)
        dp_ref[:, 1024:1536] = dva_ref[...].astype(BF16)

        def group_sum(ref):
            s0 = ref[:, 0:128] + ref[:, 128:256]
            s1 = ref[:, 256:384] + ref[:, 384:512]
            s0 = s0 + pltpu.roll(s0, HEAD_DIM, 1)
            s1 = s1 + pltpu.roll(s1, HEAD_DIM, 1)
            return jnp.where(lo, s0, s1)

        dx, dwl = _head_rms_bwd(p_ref[:, 2048:2176], w_ref[3:4, :], group_sum(dkb_ref), lo)
        dp_ref[:, 2048:2176] = dx.astype(BF16)
        acc[3] += jnp.sum(dwl, axis=0, keepdims=True)
        dp_ref[:, 2176:2304] = group_sum(dvb_ref).astype(BF16)
        for n in range(4):
            dw_ref[n:n + 1, :] += acc[n]

    wide = pl.BlockSpec((tm, 512), lambda i: (i, 0))
    pa = pl.BlockSpec((tm, 512), lambda i: (i + offa, 0))
    pb = pl.BlockSpec((tm, 512), lambda i: (i + offb, 0))
    return pl.pallas_call(
        body, name=name, grid=(t // tm,),
        in_specs=[pl.BlockSpec((tm, 2304), lambda i: (i, 0)), pl.BlockSpec((4, LANES), lambda i: (0, 0)),
                  wide, pa, pa, wide, pb, pb],
        out_specs=[pl.BlockSpec((tm, 2304), lambda i: (i, 0)), pl.BlockSpec((4, LANES), lambda i: (0, 0))],
        out_shape=[jax.ShapeDtypeStruct((t, 2304), BF16), jax.ShapeDtypeStruct((4, LANES), F32)],
        compiler_params=_cp(("arbitrary",)),
    )(proj, ws, dqa, dkpa, dvpa, dqb, dkpb, dvpb)


ROLL_W = 1024


def _rel_onehot():
    r_io = lax.broadcasted_iota(jnp.int32, (REL_W, ROLL_W), 0)
    m_io = lax.broadcasted_iota(jnp.int32, (REL_W, ROLL_W), 1)
    return (r_io == jnp.clip(REL_W - 1 - m_io, -MAX_REL, MAX_REL) + MAX_REL).astype(F32)


def _relpos_fwd(table, name):
    def body(t_ref, o_ref):
        rr = _dot(t_ref[...], _rel_onehot(), 1, 0, HI)

        def step(q, c):
            o_ref[q] = pltpu.roll(rr, (ROLL_W - (TQ - 1) + q) % ROLL_W, 1)[:, :REL_W]
            return c

        lax.fori_loop(0, TQ, step, 0)

    return pl.pallas_call(
        body, name=name, out_shape=jax.ShapeDtypeStruct((TQ, N_HEADS, REL_W), F32),
        in_specs=[pl.BlockSpec(memory_space=pltpu.VMEM)], out_specs=pl.BlockSpec(memory_space=pltpu.VMEM),
        compiler_params=_cp(),
    )(table)


def _relpos_bwd(dbias_t, name):
    def body(d_ref, o_ref):
        def step(q, acc):
            row = jnp.concatenate([d_ref[q], jnp.zeros((N_HEADS, ROLL_W - REL_W), F32)], axis=1)
            return acc + pltpu.roll(row, TQ - 1 - q, 1)

        drr = lax.fori_loop(0, TQ, step, jnp.zeros((N_HEADS, ROLL_W), F32))
        o_ref[...] = _dot(drr, _rel_onehot(), 1, 1, HI)

    return pl.pallas_call(
        body, name=name, out_shape=jax.ShapeDtypeStruct((N_HEADS, REL_W), F32),
        in_specs=[pl.BlockSpec(memory_space=pltpu.VMEM)], out_specs=pl.BlockSpec(memory_space=pltpu.VMEM),
        compiler_params=_cp(),
    )(dbias_t)


def _attn_scores(qe, kw, bias, kvalid):
    return jnp.where(kvalid, _dot(qe, kw, 1, 1) + bias, NEG)


def _stat_cols(stats, e):
    return stats[:, 64 * e:64 * e + 1], stats[:, 64 * e + 32:64 * e + 33]


def _attn_fwd(q, kp, vp, bias, sinks, pad, name, comm=None):
    t, hd = q.shape
    w = pad + TQ

    def body(sink_ref, q_ref, k_ref, v_ref, b_ref, o_ref, st_ref):
        hp, i = pl.program_id(0), pl.program_id(1)
        lo = _lo_mask()
        lane = lax.broadcasted_iota(jnp.int32, (1, LANES), 1)
        for j in range(ATT_SUB):
            start = pl.multiple_of((i * ATT_SUB + j) * TQ, TQ)
            qv = q_ref[TQ * j:TQ * (j + 1), :]
            kw = k_ref[pl.ds(start, w), :]
            vw = v_ref[pl.ds(start, w), :]
            kvalid = (start + lax.broadcasted_iota(jnp.int32, (1, w), 1)) >= pad
            outs, ms, ls = [], [], []
            for e in range(2):
                sel = lo if e == 0 else jnp.logical_not(lo)
                qe = jnp.where(sel, qv, jnp.zeros_like(qv))
                snk = sink_ref[2 * hp + e]
                s = _attn_scores(qe, kw, b_ref[e], kvalid)
                m = jnp.maximum(jnp.max(s, axis=-1, keepdims=True), snk)
                acc = _dot(jnp.exp(s - m).astype(BF16), jnp.where(sel, vw, jnp.ones_like(vw)), 1, 0)
                denom = acc[:, 64 * (1 - e):64 * (1 - e) + 1] + jnp.exp(snk - m)
                outs.append(acc * (1.0 / denom))
                ms.append(m)
                ls.append(denom)
            o_ref[TQ * j:TQ * (j + 1), :] = jnp.where(lo, outs[0], outs[1]).astype(BF16)
            st_ref[TQ * j:TQ * (j + 1), :] = jnp.where(lane < 32, ms[0], jnp.where(lane < 64, ls[0],
                                                                                 jnp.where(lane < 96, ms[1], ls[1])))

    full = pl.BlockSpec((t + pad, LANES), lambda h, i: (0, h))
    tile = pl.BlockSpec((ATT_SUB * TQ, LANES), lambda h, i: (i, h))
    (o, stats), got = _call(
        body, name=name, grid=(hd // LANES, t // (ATT_SUB * TQ)),
        in_specs=[pl.BlockSpec(memory_space=pltpu.SMEM), tile, full, full, pl.BlockSpec((2, TQ, w), lambda h, i: (h, 0, 0))],
        out_specs=[tile, tile], out_shape=[jax.ShapeDtypeStruct((t, hd), BF16), jax.ShapeDtypeStruct((t, hd), F32)],
        args=(sinks, q, kp, vp, bias), sem=("parallel", "arbitrary"), comm=comm)
    return o, stats, got


def _attn_bwd(q, kp, vp, bias, sinks, do, stats, col_off, pad, name, comm=None):
    t, hd = q.shape
    w = pad + TQ
    nhp = hd // LANES

    def body(sink_ref, q_ref, k_ref, v_ref, b_ref, do_ref, st_ref, dq_ref, dk_ref, dv_ref, db_ref, ds_ref):
        hp, i = pl.program_id(0), pl.program_id(1)

        @pl.when(i == 0)
        def _():
            dk_ref[...] = jnp.zeros_like(dk_ref)
            dv_ref[...] = jnp.zeros_like(dv_ref)
            db_ref[...] = jnp.zeros_like(db_ref)
            ds_ref[...] = jnp.zeros_like(ds_ref)

        lo = _lo_mask()
        row8 = lax.broadcasted_iota(jnp.int32, (8, LANES), 0)
        dbias = [None, None]
        dsink = jnp.zeros((8, LANES), F32)
        for j in range(ATT_SUB):
            start = pl.multiple_of((i * ATT_SUB + j) * TQ, TQ)
            qv = q_ref[TQ * j:TQ * (j + 1), :]
            dov = do_ref[TQ * j:TQ * (j + 1), :]
            kw = k_ref[pl.ds(start, w), :]
            vw = v_ref[pl.ds(start, w), :]
            kvalid = (start + lax.broadcasted_iota(jnp.int32, (1, w), 1)) >= pad
            stats = st_ref[TQ * j:TQ * (j + 1), :]
            dqs, dkw, dvw = [], None, None
            for e in range(2):
                sel = lo if e == 0 else jnp.logical_not(lo)
                qe = jnp.where(sel, qv, jnp.zeros_like(qv))
                doe = jnp.where(sel, dov, jnp.zeros_like(dov))
                m, denom = _stat_cols(stats, e)
                inv = 1.0 / denom
                p = jnp.exp(_attn_scores(qe, kw, b_ref[e], kvalid) - m) * inv
                psink = jnp.exp(sink_ref[2 * hp + e] - m) * inv
                dp = _dot(doe, vw, 1, 1)
                delta = jnp.sum(p * dp, axis=-1, keepdims=True)
                ds = p * (dp - delta)
                dbias[e] = ds if dbias[e] is None else dbias[e] + ds
                dsink = dsink + jnp.where(row8 == e, jnp.sum(-psink * delta, axis=0, keepdims=True), 0.0)
                dsb = ds.astype(BF16)
                dqs.append(_dot(dsb, kw, 1, 0))
                dk_e = _dot(dsb, qe, 0, 0)
                dv_e = _dot(p.astype(BF16), doe, 0, 0)
                dkw = dk_e if dkw is None else dkw + dk_e
                dvw = dv_e if dvw is None else dvw + dv_e
            dq_ref[TQ * j:TQ * (j + 1), :] = jnp.where(lo, dqs[0], dqs[1])
            dk_ref[pl.ds(start, w), :] += dkw
            dv_ref[pl.ds(start, w), :] += dvw
        for e in range(2):
            db_ref[e] += dbias[e]
        ds_ref[0] += dsink

    full = pl.BlockSpec((t + pad, LANES), lambda h, i: (0, h))
    tile = pl.BlockSpec((ATT_SUB * TQ, LANES), lambda h, i: (i, h))
    btile = pl.BlockSpec((2, TQ, w), lambda h, i: (h, 0, 0))
    return _call(
        body, name=name, grid=(nhp, t // (ATT_SUB * TQ)),
        in_specs=[pl.BlockSpec(memory_space=pltpu.SMEM), tile, full, full, btile,
                  pl.BlockSpec((ATT_SUB * TQ, LANES), lambda h, i: (i, h + col_off)), tile],
        out_specs=[tile, full, full, btile, pl.BlockSpec((1, 8, LANES), lambda h, i: (h, 0, 0))],
        out_shape=[jax.ShapeDtypeStruct((t, hd), F32), jax.ShapeDtypeStruct((t + pad, hd), F32),
                   jax.ShapeDtypeStruct((t + pad, hd), F32), jax.ShapeDtypeStruct((N_HEADS, TQ, w), F32),
                   jax.ShapeDtypeStruct((nhp, 8, LANES), F32)],
        args=(sinks, q, kp, vp, bias, do, stats), sem=("parallel", "arbitrary"), comm=comm)


def _conv_apply(taps, w_ref, ktaps):
    out = taps[0] * w_ref[ktaps - 1:ktaps, :]
    for s in range(1, ktaps):
        out = out + taps[s] * w_ref[ktaps - 1 - s:ktaps - s, :]
    return out


def _sigmoid(x):
    return jax.nn.sigmoid(x)


def _silu_grad(x):
    sg = _sigmoid(x)
    return x * sg, sg * (1.0 + x * (1.0 - sg))


FFN_HALO = 16
FFN_BT = 256
FFN_BC = 1408


def _ffn_in_mid(x, g, wt, w8, b, name):
    t, d = x.shape
    f = D_FF
    tm = FFN_BT

    def body(x_ref, g_ref, b_ref, w_ref, cb_ref, gu_ref, h_ref, a_ref, gc_ref, halo_ref):
        @pl.when(pl.program_id(0) == 0)
        def _():
            halo_ref[...] = jnp.zeros_like(halo_ref)

        xv = x_ref[...]
        r = lax.rsqrt(jnp.mean(xv * xv, axis=-1, keepdims=True) + EPS)
        h = (xv * r * g_ref[...]).astype(BF16)
        h_ref[...] = h
        for c in range(0, f, FFN_BC):
            cs = slice(c, c + FFN_BC)
            gate = _dot(h, b_ref[c:c + FFN_BC, :], 1, 1).astype(BF16)
            up = _dot(h, b_ref[f + c:f + c + FFN_BC, :], 1, 1).astype(BF16)
            gu_ref[:, cs] = gate
            gu_ref[:, f + c:f + c + FFN_BC] = up
            gf = gate.astype(F32)
            ext = jnp.concatenate([halo_ref[:, cs], gf], axis=0)
            gc = (cb_ref[:, cs] + gf * w_ref[2:3, cs] + pltpu.roll(ext, 1, 0)[8:] * w_ref[1:2, cs]
                  + pltpu.roll(ext, 2, 0)[8:] * w_ref[0:1, cs])
            a_ref[:, cs] = (gc * _sigmoid(gc) * up.astype(F32)).astype(BF16)
            gc_ref[:, cs] = gc.astype(BF16)
            halo_ref[:, cs] = gf[tm - 8:]

    row = pl.BlockSpec((tm, d), lambda i: (i, 0))
    row_f = pl.BlockSpec((tm, f), lambda i: (i, 0))
    return pl.pallas_call(
        body, name=name, grid=(t // tm,),
        in_specs=[row, pl.BlockSpec((1, d), lambda i: (0, 0)), pl.BlockSpec((2 * f, d), lambda i: (0, 0)),
                  pl.BlockSpec((8, f), lambda i: (0, 0)), pl.BlockSpec((1, f), lambda i: (0, 0))],
        out_specs=[pl.BlockSpec((tm, 2 * f), lambda i: (i, 0)), row, row_f, row_f],
        out_shape=[jax.ShapeDtypeStruct((t, 2 * f), BF16), jax.ShapeDtypeStruct((t, d), BF16), jax.ShapeDtypeStruct((t, f), BF16),
                   jax.ShapeDtypeStruct((t, f), BF16)],
        scratch_shapes=[pltpu.VMEM((8, f), F32)], compiler_params=_cp(("arbitrary",)),
    )(x, g, wt, w8, b)


def _ffn_mid_bwd(gu, gc, dxb, w_out, w8, name):
    t, d = dxb.shape
    f = D_FF
    tm, hr = FFN_BT, FFN_HALO
    nt = t // tm
    n = tm + hr

    def body(g_ref, u_ref, un_ref, c_ref, cn_ref, dx_ref, dxn_ref, wo_ref, w_ref, dgu_ref, dw_ref, db_ref):
        i = pl.program_id(0)
        last = i == nt - 1

        @pl.when(i == 0)
        def _():
            dw_ref[...] = jnp.zeros_like(dw_ref)
            db_ref[...] = jnp.zeros_like(db_ref)

        dxe = jnp.concatenate([dx_ref[...], dxn_ref[...]], axis=0)
        row = lax.broadcasted_iota(jnp.int32, (n, 1), 0)
        keep = (row < tm) | jnp.logical_not(last)
        for c in range(0, f, FFN_BC):
            cs = slice(c, c + FFN_BC)
            act, dact = _silu_grad(jnp.concatenate([c_ref[:, cs], cn_ref[:, cs]], axis=0).astype(F32))
            da = _dot(dxe, wo_ref[cs, :], 1, 1)
            up = jnp.concatenate([u_ref[:, cs], un_ref[:, cs]], axis=0).astype(F32)
            dgc = jnp.where(keep, da * up * dact, 0.0)
            nxt = [dgc[:tm], pltpu.roll(dgc, n - 1, 0)[:tm], pltpu.roll(dgc, n - 2, 0)[:tm]]
            dgu_ref[:, f + c:f + c + FFN_BC] = (da[:tm] * act[:tm]).astype(BF16)
            dgu_ref[:, cs] = (nxt[0] * w_ref[2:3, cs] + nxt[1] * w_ref[1:2, cs] + nxt[2] * w_ref[0:1, cs]).astype(BF16)
            gate = g_ref[:, cs].astype(F32)
            db_ref[:, cs] += jnp.sum(nxt[0], axis=0, keepdims=True)
            for s in range(3):
                dw_ref[2 - s:3 - s, cs] += jnp.sum(nxt[s] * gate, axis=0, keepdims=True)

    r = tm // hr
    nxt_blk = lambda i: jnp.minimum((i + 1) * r, t // hr - 1)
    row_f = pl.BlockSpec((tm, f), lambda i: (i, 0))
    halo_f = pl.BlockSpec((hr, f), lambda i: (nxt_blk(i), 0))
    return pl.pallas_call(
        body, name=name, grid=(nt,),
        in_specs=[row_f, pl.BlockSpec((tm, f), lambda i: (i, 1)), pl.BlockSpec((hr, f), lambda i: (nxt_blk(i), 1)),
                  row_f, halo_f,
                  pl.BlockSpec((tm, d), lambda i: (i, 0)), pl.BlockSpec((hr, d), lambda i: (nxt_blk(i), 0)),
                  pl.BlockSpec((f, d), lambda i: (0, 0)), pl.BlockSpec((8, f), lambda i: (0, 0))],
        out_specs=[pl.BlockSpec((tm, 2 * f), lambda i: (i, 0)), pl.BlockSpec((8, f), lambda i: (0, 0)),
                   pl.BlockSpec((1, f), lambda i: (0, 0))],
        out_shape=[jax.ShapeDtypeStruct((t, 2 * f), BF16), jax.ShapeDtypeStruct((8, f), F32), jax.ShapeDtypeStruct((1, f), F32)],
        compiler_params=_cp(("arbitrary",)),
    )(gu, gu, gu, gc, gc, dxb, dxb, w_out, w8)


PRE_TM = 256
PRE_TC = 1024


def _ssm_in_pre(x, g, wt, w8, b, name):
    t, d = x.shape
    tm, tc = PRE_TM, PRE_TC

    def body(x_ref, g_ref, b_ref, w_ref, cb_ref, zx_ref, h_ref, o_ref, c_ref, halo_ref):
        @pl.when(pl.program_id(0) == 0)
        def _():
            halo_ref[...] = jnp.zeros_like(halo_ref)

        xv = x_ref[...]
        r = lax.rsqrt(jnp.mean(xv * xv, axis=-1, keepdims=True) + EPS)
        h = (xv * r * g_ref[...]).astype(BF16)
        h_ref[...] = h
        for c in range(0, ZX, tc):
            v = _dot(h, b_ref[c:c + tc, :], 1, 1)
            zx_ref[:, c:c + tc] = v
            if c >= D_INNER:
                cs = slice(c - D_INNER, c - D_INNER + tc)
                ext = jnp.concatenate([halo_ref[:, cs], v], axis=0)
                conv = cb_ref[:, cs] + v * w_ref[3:4, cs]
                for s in (1, 2, 3):
                    conv = conv + pltpu.roll(ext, s, 0)[8:] * w_ref[3 - s:4 - s, cs]
                o_ref[:, cs] = conv * _sigmoid(conv)
                c_ref[:, cs] = conv.astype(BF16)
                halo_ref[:, cs] = v[tm - 8:]

    row = pl.BlockSpec((tm, d), lambda i: (i, 0))
    row_x = pl.BlockSpec((tm, XBC), lambda i: (i, 0))
    return pl.pallas_call(
        body, name=name, grid=(t // tm,),
        in_specs=[row, pl.BlockSpec((1, d), lambda i: (0, 0)), pl.BlockSpec(wt.shape, lambda i: (0, 0)),
                  pl.BlockSpec((8, XBC), lambda i: (0, 0)), pl.BlockSpec((1, XBC), lambda i: (0, 0))],
        out_specs=[pl.BlockSpec((tm, ZX), lambda i: (i, 0)), row, row_x, row_x],
        out_shape=[jax.ShapeDtypeStruct((t, ZX), F32), jax.ShapeDtypeStruct((t, d), BF16), jax.ShapeDtypeStruct((t, XBC), F32),
                   jax.ShapeDtypeStruct((t, XBC), BF16)],
        scratch_shapes=[pltpu.VMEM((8, XBC), F32)], compiler_params=_cp(("arbitrary",)),
    )(x, g, wt, w8, b)


PRE_HALO = 16


def _ssm_pre_bwd(zx, conv, dxbc, w8, name):
    t = zx.shape[0]
    tm, tc, hr = PRE_TM, PRE_TC, PRE_HALO
    off = D_INNER // tc
    nt = t // tm
    n = tm + hr

    def body(x_ref, c_ref, cn_ref, d_ref, dn_ref, w_ref, o_ref, dw_ref, db_ref):
        i = pl.program_id(1)
        last = i == nt - 1

        @pl.when(i == 0)
        def _():
            dw_ref[...] = jnp.zeros_like(dw_ref)
            db_ref[...] = jnp.zeros_like(db_ref)

        _, dact = _silu_grad(jnp.concatenate([c_ref[...], cn_ref[...]], axis=0).astype(F32))
        row = lax.broadcasted_iota(jnp.int32, (n, 1), 0)
        dc = jnp.where((row < tm) | jnp.logical_not(last), jnp.concatenate([d_ref[...], dn_ref[...]], axis=0) * dact, 0.0)
        nxt = [dc[:tm]] + [pltpu.roll(dc, n - s, 0)[:tm] for s in (1, 2, 3)]
        o_ref[...] = _conv_apply(nxt, w_ref, 4).astype(BF16)
        xv = x_ref[...]
        db_ref[...] += jnp.sum(nxt[0], axis=0, keepdims=True)
        for s in range(4):
            dw_ref[3 - s:4 - s, :] += jnp.sum(nxt[s] * xv, axis=0, keepdims=True)

    nxt_blk = lambda i: jnp.minimum((i + 1) * (tm // hr), t // hr - 1)
    tile = pl.BlockSpec((tm, tc), lambda j, i: (i, j))
    halo = pl.BlockSpec((hr, tc), lambda j, i: (nxt_blk(i), j))
    return pl.pallas_call(
        body, name=name, grid=(XBC // tc, nt),
        in_specs=[pl.BlockSpec((tm, tc), lambda j, i: (i, j + off)), tile, halo, tile, halo,
                  pl.BlockSpec((8, tc), lambda j, i: (0, j))],
        out_specs=[tile, pl.BlockSpec((8, tc), lambda j, i: (0, j)), pl.BlockSpec((1, tc), lambda j, i: (0, j))],
        out_shape=[jax.ShapeDtypeStruct((t, XBC), BF16), jax.ShapeDtypeStruct((8, XBC), F32),
                   jax.ShapeDtypeStruct((1, XBC), F32)],
        compiler_params=_cp(("parallel", "arbitrary")),
    )(zx, conv, conv, dxbc, dxbc, w8)


def _head_lanes():
    return lax.broadcasted_iota(jnp.int32, (1, LANES), 1) < SSM_HEADS


def _dt_fwd(dtraw, bias, name):
    t = dtraw.shape[0]
    tm = _pick(t, (1024, 512, 256, 128))

    def body(x_ref, b_ref, o_ref):
        v = x_ref[...] + b_ref[...]
        sp = jnp.maximum(v, 0.0) + jnp.log(1.0 + jnp.exp(-jnp.abs(v)))
        o_ref[...] = jnp.where(_head_lanes(), sp, 0.0)

    row = pl.BlockSpec((tm, LANES), lambda i: (i, 0))
    return pl.pallas_call(
        body, name=name, grid=(t // tm,), in_specs=[row, pl.BlockSpec((1, LANES), lambda i: (0, 0))], out_specs=row,
        out_shape=jax.ShapeDtypeStruct((t, LANES), F32), compiler_params=_cp(("parallel",)),
    )(dtraw, bias)


def _dt_bwd(dtraw, bias, ddt, name):
    t = dtraw.shape[0]
    tm = _pick(t, (1024, 512, 256, 128))

    def body(x_ref, b_ref, d_ref, o_ref, db_ref):
        @pl.when(pl.program_id(0) == 0)
        def _():
            db_ref[...] = jnp.zeros_like(db_ref)

        g = jnp.where(_head_lanes(), d_ref[...] * _sigmoid(x_ref[...] + b_ref[...]), 0.0)
        o_ref[...] = g.astype(BF16)
        db_ref[...] += jnp.sum(g, axis=0, keepdims=True)

    row = pl.BlockSpec((tm, LANES), lambda i: (i, 0))
    vec = pl.BlockSpec((1, LANES), lambda i: (0, 0))
    return pl.pallas_call(
        body, name=name, grid=(t // tm,), in_specs=[row, vec, row], out_specs=[row, vec],
        out_shape=[jax.ShapeDtypeStruct((t, LANES), BF16), jax.ShapeDtypeStruct((1, LANES), F32)],
        compiler_params=_cp(("arbitrary",)),
    )(dtraw, bias, ddt)


GROUP_W = D_INNER // SSM_GROUPS


def _ssd_common(dt, alog):
    ll = dt.shape[0]
    a_neg = -jnp.exp(alog)
    a = dt * a_neg
    ri = lax.broadcasted_iota(jnp.int32, (ll, ll), 0)
    ci = lax.broadcasted_iota(jnp.int32, (ll, ll), 1)
    tril = ri >= ci
    acs = _dot(tril.astype(F32), a, 1, 0, HI)
    return a_neg, tril, acs, acs.T


def _pair_terms(acs, acs_t, dt, h0, lo):
    ll = acs.shape[0]
    cols = [acs[:, h0 + e:h0 + e + 1] for e in range(2)]
    rows = [acs_t[h0 + e:h0 + e + 1, :] for e in range(2)]
    dtc = [dt[:, h0 + e:h0 + e + 1] for e in range(2)]
    lasts = [c[ll - 1:ll, :] for c in cols]
    dtx = jnp.where(lo, dtc[0], dtc[1])
    eac = jnp.where(lo, jnp.exp(cols[0]), jnp.exp(cols[1]))
    fdec = jnp.where(lo, jnp.exp(lasts[0] - cols[0]), jnp.exp(lasts[1] - cols[1]))
    elast = jnp.where(lo, jnp.exp(lasts[0]), jnp.exp(lasts[1]))
    return cols, rows, dtx, eac, fdec, elast


def _decay(col, row, tril):
    return jnp.where(tril, jnp.exp(jnp.minimum(col - row, 0.0)), 0.0)


def _two_heads_rows(v, lo):
    z = jnp.zeros_like(v)
    return jnp.concatenate([jnp.where(lo, v, z), jnp.where(lo, z, v)], axis=0)


def _two_heads_cols(ms):
    return jnp.concatenate(ms, axis=1)


def _z_group(z_refs, g):
    return z_refs[g // 2][:, GROUP_W * (g % 2):GROUP_W * (g % 2 + 1)]


def _ssd_fwd(xbc, dt, alog, zx, dexp, nw, name, comm=None):
    t = xbc.shape[0]
    ll = SSD_L
    nc = t // ll

    def body(x_ref, dt_ref, al_ref, z0_ref, z1_ref, d_ref, w_ref, y_ref, sp_ref, y4_ref, st_ref):
        @pl.when(pl.program_id(0) == 0)
        def _():
            st_ref[...] = jnp.zeros_like(st_ref)

        dtv = dt_ref[...]
        _, tril, acs, acs_t = _ssd_common(dtv, al_ref[...])
        lo = _lo_mask()
        sp_ref[0] = st_ref[...]
        for g in range(SSM_GROUPS):
            bg = x_ref[:, D_INNER + SSM_STATE * g:D_INNER + SSM_STATE * (g + 1)].astype(BF16)
            cg = x_ref[:, D_INNER + 512 + SSM_STATE * g:D_INNER + 512 + SSM_STATE * (g + 1)].astype(BF16)
            gm = _dot(cg, bg, 1, 1)
            g0 = GROUP_W * g
            terms = [_pair_terms(acs, acs_t, dtv, 8 * g + 2 * pp, lo) for pp in range(4)]
            dtx, eac, fdec, elast = [jnp.concatenate([tt[k] for tt in terms], axis=1) for k in (2, 3, 4, 5)]
            xg = x_ref[:, g0:g0 + GROUP_W]
            ug = (xg * dtx).astype(BF16)
            sg = st_ref[:, g0:g0 + GROUP_W]
            yst = _dot(cg, sg.astype(BF16), 1, 0) * eac
            st_ref[:, g0:g0 + GROUP_W] = sg * elast + _dot(bg, (xg * (fdec * dtx)).astype(BF16), 0, 0)
            ys = []
            for pp in range(4):
                cols, rows = terms[pp][0], terms[pp][1]
                sl = slice(LANES * pp, LANES * (pp + 1))
                y_in = _dot(_two_heads_cols([(gm * _decay(cols[e], rows[e], tril)).astype(BF16) for e in range(2)]),
                            _two_heads_rows(ug[:, sl], lo), 1, 0)
                ys.append(y_in + yst[:, sl])
            yg = jnp.concatenate(ys, axis=1)
            y_ref[:, g0:g0 + GROUP_W] = yg
            zg = _z_group((z0_ref, z1_ref), g)
            y3 = (yg + d_ref[:, g0:g0 + GROUP_W] * xg) * (zg * _sigmoid(zg))
            r = lax.rsqrt(jnp.mean(y3 * y3, axis=-1, keepdims=True) + EPS)
            y4_ref[:, g0:g0 + GROUP_W] = (y3 * r * w_ref[:, g0:g0 + GROUP_W]).astype(BF16)

    zblk = lambda j: pl.BlockSpec((ll, 1024), lambda c: (c, j))
    vec = pl.BlockSpec((1, D_INNER), lambda c: (0, 0))
    row = pl.BlockSpec((ll, D_INNER), lambda c: (c, 0))
    return _call(
        body, name=name, grid=(nc,),
        in_specs=[pl.BlockSpec((ll, XBC), lambda c: (c, 0)), pl.BlockSpec((ll, LANES), lambda c: (c, 0)),
                  pl.BlockSpec((1, LANES), lambda c: (0, 0)), zblk(0), zblk(1), vec, vec],
        out_specs=[row, pl.BlockSpec((1, SSM_STATE, D_INNER), lambda c: (c, 0, 0)), row],
        out_shape=[jax.ShapeDtypeStruct((t, D_INNER), F32), jax.ShapeDtypeStruct((nc, SSM_STATE, D_INNER), F32),
                   jax.ShapeDtypeStruct((t, D_INNER), BF16)],
        scratch_shapes=[pltpu.VMEM((SSM_STATE, D_INNER), F32)],
        args=(xbc, dt, alog, zx, zx, dexp, nw), sem=("arbitrary",), comm=comm)


def _ssd_bwd(xbc, dt, alog, sprev, dy4, y, zx, dexp, nw, name, comm=None):
    t = xbc.shape[0]
    ll = SSD_L
    nc = t // ll

    def body(x_ref, dt_ref, al_ref, sp_ref, g4_ref, y_ref, z0_ref, z1_ref, d_ref, w_ref,
             dx_ref, ddt_ref, dal_ref, dz_ref, dd_ref, dnw_ref, ds_ref, colt_ref):
        @pl.when(pl.program_id(0) == 0)
        def _():
            ds_ref[...] = jnp.zeros_like(ds_ref)
            dal_ref[...] = jnp.zeros_like(dal_ref)
            dd_ref[...] = jnp.zeros_like(dd_ref)
            dnw_ref[...] = jnp.zeros_like(dnw_ref)

        dtv = dt_ref[...]
        a_neg, tril, acs, acs_t = _ssd_common(dtv, al_ref[...])
        lo = _lo_mask()
        hi = jnp.logical_not(lo)
        lane = lax.broadcasted_iota(jnp.int32, (1, LANES), 1)
        colt_ref[...] = jnp.zeros_like(colt_ref)
        rowterm = jnp.zeros((ll, LANES), F32)
        ddt_u = jnp.zeros((ll, LANES), F32)
        dlast = jnp.zeros((1, LANES), F32)

        def halves(v):
            return (jnp.sum(jnp.where(lo, v, 0.0), axis=-1, keepdims=True),
                    jnp.sum(jnp.where(hi, v, 0.0), axis=-1, keepdims=True))

        for g in range(SSM_GROUPS):
            cb0 = D_INNER + SSM_STATE * g
            cc0 = D_INNER + 512 + SSM_STATE * g
            bg = x_ref[:, cb0:cb0 + SSM_STATE].astype(BF16)
            cg = x_ref[:, cc0:cc0 + SSM_STATE].astype(BF16)
            gm = _dot(cg, bg, 1, 1)
            g0 = GROUP_W * g
            terms = [_pair_terms(acs, acs_t, dtv, 8 * g + 2 * pp, lo) for pp in range(4)]
            dtx, eac, fdec, elast = [jnp.concatenate([tt[k] for tt in terms], axis=1) for k in (2, 3, 4, 5)]
            xg = x_ref[:, g0:g0 + GROUP_W]
            u32 = xg * dtx
            ug = u32.astype(BF16)
            zg = _z_group((z0_ref, z1_ref), g)
            dg = d_ref[:, g0:g0 + GROUP_W]
            act, dact = _silu_grad(zg)
            y2 = y_ref[:, g0:g0 + GROUP_W] + dg * xg
            y3 = y2 * act
            rn = lax.rsqrt(jnp.mean(y3 * y3, axis=-1, keepdims=True) + EPS)
            y3n = y3 * rn
            gv = g4_ref[:, g0:g0 + GROUP_W]
            dyn = gv * w_ref[:, g0:g0 + GROUP_W]
            dy3 = rn * (dyn - y3n * jnp.mean(dyn * y3n, axis=-1, keepdims=True))
            dyg = dy3 * act
            dskip = dyg * dg
            dz_ref[:, g0:g0 + GROUP_W] = (dy3 * y2 * dact).astype(BF16)
            dd_ref[:, g0:g0 + GROUP_W] += jnp.sum(dyg * xg, axis=0, keepdims=True)
            dnw_ref[:, g0:g0 + GROUP_W] += jnp.sum(gv * y3n, axis=0, keepdims=True)
            dyb = dyg.astype(BF16)
            spg = sp_ref[0, :, g0:g0 + GROUP_W]
            spb = spg.astype(BF16)
            dsg = ds_ref[:, g0:g0 + GROUP_W]
            dsb = dsg.astype(BF16)
            du_st = _dot(bg, dsb, 1, 0) * fdec
            yst = _dot(cg, spb, 1, 0) * eac
            dye = (dyg * eac).astype(BF16)
            dc_st = _dot(dye, spb, 1, 1)
            db_st = _dot((xg * (fdec * dtx)).astype(BF16), dsb, 1, 1)
            ds_ref[:, g0:g0 + GROUP_W] = dsg * elast + _dot(cg, dye, 0, 0)
            qst_el = du_st * u32
            rq_el = dyg * yst - qst_el
            q_row = jnp.sum(qst_el, axis=0, keepdims=True)
            s_row = jnp.sum(dsg * spg, axis=0, keepdims=True)
            dgm = jnp.zeros((ll, ll), F32)
            for pp in range(4):
                h0 = 8 * g + 2 * pp
                cols, rows = terms[pp][0], terms[pp][1]
                sl = slice(LANES * pp, LANES * (pp + 1))
                decs = [_decay(cols[e], rows[e], tril) for e in range(2)]
                wms = [gm * d for d in decs]
                dum2 = _dot(dyb[:, sl], _two_heads_rows(ug[:, sl], lo), 1, 1)
                du = _dot(jnp.concatenate([wm.astype(BF16) for wm in wms], axis=0),
                          _two_heads_rows(dyb[:, sl], lo), 0, 0) + du_st[:, sl]
                dx_ref[:, g0 + LANES * pp:g0 + LANES * (pp + 1)] = du * dtx[:, sl] + dskip[:, sl]
                ddtu = halves(du * xg[:, sl])
                rq = halves(rq_el[:, sl])
                qs = halves(q_row[:, sl])
                ss = halves(s_row[:, sl])
                for e in range(2):
                    dum = dum2[:, ll * e:ll * (e + 1)]
                    dgm = dgm + dum * decs[e]
                    tm_ = dum * wms[e]
                    oh = lane == (h0 + e)
                    rowterm = rowterm + jnp.where(oh, jnp.sum(tm_, axis=1, keepdims=True) + rq[e], 0.0)
                    ddt_u = ddt_u + jnp.where(oh, ddtu[e], 0.0)
                    dlast = dlast + jnp.where(oh, jnp.exp(cols[e][ll - 1:ll, :]) * ss[e] + qs[e], 0.0)
                    colt_ref[h0 + e:h0 + e + 1, :] = jnp.sum(tm_, axis=0, keepdims=True)
            dgb = dgm.astype(BF16)
            dx_ref[:, cc0:cc0 + SSM_STATE] = _dot(dgb, bg, 1, 0) + dc_st
            dx_ref[:, cb0:cb0 + SSM_STATE] = _dot(dgb, cg, 0, 0) + db_st
        row_io = lax.broadcasted_iota(jnp.int32, (ll, LANES), 0)
        dacs = rowterm - colt_ref[...].T + jnp.where(row_io == ll - 1, dlast, 0.0)
        da = _dot(jnp.logical_not(tril).astype(F32) + jnp.where(
            lax.broadcasted_iota(jnp.int32, (ll, ll), 0) == lax.broadcasted_iota(jnp.int32, (ll, ll), 1), 1.0, 0.0),
            dacs, 1, 0, HI)
        ddt_ref[...] = da * a_neg + ddt_u
        dal_ref[...] += jnp.sum(da * dtv, axis=0, keepdims=True) * a_neg

    rev = lambda c: nc - 1 - c
    row = pl.BlockSpec((ll, D_INNER), lambda c: (rev(c), 0))
    vec = pl.BlockSpec((1, D_INNER), lambda c: (0, 0))
    zblk = lambda j: pl.BlockSpec((ll, 1024), lambda c: (rev(c), j))
    return _call(
        body, name=name, grid=(nc,),
        in_specs=[pl.BlockSpec((ll, XBC), lambda c: (rev(c), 0)), pl.BlockSpec((ll, LANES), lambda c: (rev(c), 0)),
                  pl.BlockSpec((1, LANES), lambda c: (0, 0)),
                  pl.BlockSpec((1, SSM_STATE, D_INNER), lambda c: (rev(c), 0, 0)), row, row, zblk(0), zblk(1), vec, vec],
        out_specs=[pl.BlockSpec((ll, XBC), lambda c: (rev(c), 0)), pl.BlockSpec((ll, LANES), lambda c: (rev(c), 0)),
                   pl.BlockSpec((1, LANES), lambda c: (0, 0)), row, vec, vec],
        out_shape=[jax.ShapeDtypeStruct((t, XBC), F32), jax.ShapeDtypeStruct((t, LANES), F32),
                   jax.ShapeDtypeStruct((1, LANES), F32), jax.ShapeDtypeStruct((t, D_INNER), BF16),
                   jax.ShapeDtypeStruct((1, D_INNER), F32), jax.ShapeDtypeStruct((1, D_INNER), F32)],
        scratch_shapes=[pltpu.VMEM((SSM_STATE, D_INNER), F32), pltpu.VMEM((LANES, ll), F32)],
        args=(xbc, dt, alog, sprev, dy4, y, zx, zx, dexp, nw), sem=("arbitrary",), comm=comm)


ADAM_TR = 512


def _sum_parts(parts, name):
    nparts, r, c = parts.shape
    tc = _pick(c, (256, 128))

    def body(p_ref, o_ref):
        g = p_ref[0].astype(F32)
        for k in range(1, nparts):
            g = g + p_ref[k].astype(F32)
        o_ref[...] = g

    return pl.pallas_call(
        body, name=name, grid=(c // tc,), in_specs=[pl.BlockSpec((nparts, r, tc), lambda j: (0, 0, j))],
        out_specs=pl.BlockSpec((r, tc), lambda j: (0, j)), out_shape=jax.ShapeDtypeStruct((r, c), F32),
        compiler_params=_cp(("parallel",)),
    )(parts)


def _adamw(parts, w, m, v, name):
    nl, r, c = w.shape
    assert len(parts) == nl
    tr = _pick(r, (256, 128, 64))
    c1 = 1.0 - ADAM_B1 ** ADAM_STEP
    c2 = 1.0 - ADAM_B2 ** ADAM_STEP

    def body(*refs):
        p_refs = refs[:nl]
        w_ref, m_ref, v_ref, g_ref, d_ref, mo_ref, vo_ref = refs[nl:]
        g = None
        for l, p_ref in enumerate(p_refs):
            s = p_ref[0].astype(F32)
            for k in range(1, p_ref.shape[0]):
                s = s + p_ref[k].astype(F32)
            g = s if g is None else jnp.where(pl.program_id(0) == l, s, g)
        mn = ADAM_B1 * m_ref[0] + (1.0 - ADAM_B1) * g
        vn = ADAM_B2 * v_ref[0] + (1.0 - ADAM_B2) * (g * g)
        g_ref[0] = g
        mo_ref[0] = mn
        vo_ref[0] = vn
        d_ref[0] = -ADAM_LR * ((mn / c1) / (jnp.sqrt(vn / c2) + ADAM_EPS) + ADAM_WD * w_ref[0])

    row = pl.BlockSpec((1, tr, c), lambda l, i: (l, i, 0))
    sd = jax.ShapeDtypeStruct((nl, r, c), F32)
    return pl.pallas_call(
        body, name=name, grid=(nl, r // tr),
        in_specs=[pl.BlockSpec((p.shape[0], tr, c), lambda l, i: (0, i, 0)) for p in parts] + [row, row, row],
        out_specs=[row, row, row, row], out_shape=[sd, sd, sd, sd], compiler_params=_cp(("parallel", "parallel")),
    )(*parts, w, m, v)


def _peers():
    mx, my, mc = lax.axis_index("x"), lax.axis_index("y"), lax.axis_index("c")
    me = 4 * mx + 2 * my + mc
    out = []
    for k in range(1, N_DEV):
        px = 1 - mx if k & 4 else mx
        py = 1 - my if k & 2 else my
        pc = 1 - mc if k & 1 else mc
        out.append(((px, py, pc), 4 * px + 2 * py + pc))
    return me, out


class _Comm:
    def __init__(self, arrs, scatters):
        self.arrs, self.scatters, self.n = list(arrs), list(scatters), len(arrs)
        self.specs = [pl.BlockSpec(memory_space=pl.ANY)] * self.n
        self.out_shape = [jax.ShapeDtypeStruct(x.shape if sc else (N_DEV,) + x.shape, x.dtype)
                          for x, sc in zip(self.arrs, self.scatters)]
        np_ = N_DEV - 1
        self.scratch = [pltpu.SemaphoreType.DMA((np_ * self.n,)), pltpu.SemaphoreType.DMA((np_ * self.n,)),
                        pltpu.SemaphoreType.DMA((self.n,))]

    def _copies(self, x_refs, o_refs, sems):
        send_sems, recv_sems, local_sems = sems
        me, peers = _peers()
        np_ = N_DEV - 1
        local, sends, recvs = [], [], []
        for a in range(self.n):
            mine = x_refs[a].at[me] if self.scatters[a] else x_refs[a]
            local.append(pltpu.make_async_copy(mine, o_refs[a].at[me], local_sems.at[a]))
        for k, (dev, idx) in enumerate(peers):
            for a in range(self.n):
                mine = x_refs[a].at[me] if self.scatters[a] else x_refs[a]
                sends.append(pltpu.make_async_remote_copy(
                    src_ref=x_refs[a].at[idx] if self.scatters[a] else x_refs[a], dst_ref=o_refs[a].at[me],
                    send_sem=send_sems.at[a * np_ + k], recv_sem=recv_sems.at[a * np_ + k], device_id=dev, device_id_type=MESH))
                recvs.append(pltpu.make_async_remote_copy(
                    src_ref=mine, dst_ref=o_refs[a].at[idx], send_sem=send_sems.at[a * np_ + k],
                    recv_sem=recv_sems.at[a * np_ + k], device_id=dev, device_id_type=MESH))
        return local, sends, recvs

    def start(self, x_refs, o_refs, sems):
        local, sends, _ = self._copies(x_refs, o_refs, sems)
        for cp in local + sends:
            cp.start()

    def wait(self, x_refs, o_refs, sems):
        local, sends, recvs = self._copies(x_refs, o_refs, sems)
        for cp in recvs:
            cp.wait_recv()
        for cp in sends:
            cp.wait_send()
        for cp in local:
            cp.wait()


class _Gather2(_Comm):
    def __init__(self, arrs):
        super().__init__(arrs, [False] * len(arrs))

    def _plan(self, x_refs, o_refs, sems):
        send_sems, recv_sems, local_sems = sems
        mx, my, mc = lax.axis_index("x"), lax.axis_index("y"), lax.axis_index("c")
        slot = lambda px, py, pc: 4 * px + 2 * py + pc
        sib = (mx, my, 1 - mc)
        chips = [(1 - mx, my), (mx, 1 - my), (1 - mx, 1 - my)]
        np_ = N_DEV - 1
        local, first, passed, arrive_first, arrive_rest = [], [], [], [], []

        def copy(a, k, src, block, to):
            return pltpu.make_async_remote_copy(
                src_ref=src, dst_ref=o_refs[a].at[block], send_sem=send_sems.at[a * np_ + k], recv_sem=recv_sems.at[a * np_ + k],
                device_id=to, device_id_type=MESH)

        for a in range(self.n):
            me = slot(mx, my, mc)
            local.append(pltpu.make_async_copy(x_refs[a], o_refs[a].at[me], local_sems.at[a]))
            first.append(copy(a, 0, x_refs[a], me, sib))
            arrive_rest.append(copy(a, 0, x_refs[a], slot(*sib), sib))
            for j, (cx, cy) in enumerate(chips):
                first.append(copy(a, 1 + j, x_refs[a], me, (cx, cy, mc)))
                arrive_first.append(copy(a, 1 + j, x_refs[a], slot(cx, cy, mc), (cx, cy, mc)))
                passed.append(copy(a, 4 + j, o_refs[a].at[slot(cx, cy, mc)], slot(cx, cy, mc), sib))
                arrive_rest.append(copy(a, 4 + j, x_refs[a], slot(cx, cy, 1 - mc), sib))
        return local, first, passed, arrive_first, arrive_rest

    def start(self, x_refs, o_refs, sems):
        local, first, _, _, _ = self._plan(x_refs, o_refs, sems)
        for cp in local + first:
            cp.start()

    def wait(self, x_refs, o_refs, sems):
        local, first, passed, arrive_first, arrive_rest = self._plan(x_refs, o_refs, sems)
        for arrived, onward in zip(arrive_first, passed):
            arrived.wait_recv()
            onward.start()
        for cp in arrive_rest:
            cp.wait_recv()
        for cp in first + passed:
            cp.wait_send()
        for cp in local:
            cp.wait()


def _call(body, *, name, grid, in_specs, out_specs, out_shape, args, scratch_shapes=(), sem=None, comm=None):
    if comm is None:
        outs = pl.pallas_call(
            body, name=name, grid=grid, in_specs=list(in_specs), out_specs=list(out_specs), out_shape=list(out_shape),
            scratch_shapes=list(scratch_shapes), compiler_params=_cp(sem),
        )(*args)
        return list(outs), []
    n_in, n_out, nc = len(in_specs), len(out_specs), comm.n
    nsteps = 1
    for g in grid:
        nsteps *= g

    def carrier(*refs):
        ins, cin = refs[:n_in], refs[n_in:n_in + nc]
        outs, cout = refs[n_in + nc:n_in + nc + n_out], refs[n_in + nc + n_out:n_in + 2 * nc + n_out]
        rest = refs[n_in + 2 * nc + n_out:]
        scratch, sems = rest[:len(rest) - 3], rest[len(rest) - 3:]
        if nsteps == 1:
            comm.start(cin, cout, sems)
            body(*ins, *outs, *scratch)
            comm.wait(cin, cout, sems)
            return
        step = 0
        for d, g in enumerate(grid):
            step = step * g + pl.program_id(d)

        @pl.when(step == 0)
        def _():
            comm.start(cin, cout, sems)

        body(*ins, *outs, *scratch)

        @pl.when(step == nsteps - 1)
        def _():
            comm.wait(cin, cout, sems)

    outs = pl.pallas_call(
        carrier, name=name, grid=grid, in_specs=list(in_specs) + comm.specs, out_specs=list(out_specs) + comm.specs,
        out_shape=list(out_shape) + comm.out_shape, scratch_shapes=list(scratch_shapes) + comm.scratch,
        compiler_params=_cp(("arbitrary",) * len(grid) if grid else None),
    )(*args, *comm.arrs)
    return list(outs[:n_out]), list(outs[n_out:])


def _exchange(comm, name):
    return _call(lambda *refs: None, name=name, grid=(), in_specs=[], out_specs=[], out_shape=[], args=[], comm=comm)[1]


def _pack(arrs, dtype, lead=()):
    nl = len(lead)
    flat = jnp.concatenate([a.astype(dtype).reshape(lead + (-1,)) for a in arrs], axis=nl)
    n = flat.shape[-1]
    rows = -(-n // (LANES * ADAM_TR)) * ADAM_TR
    flat = jnp.pad(flat, [(0, 0)] * nl + [(0, rows * LANES - n)])
    return flat.reshape(lead + (rows, LANES))


def _unpack(flat, shapes, lead=()):
    nl = len(lead)
    flat = flat.reshape(lead + (-1,))
    out, o = [], 0
    for s in shapes:
        n = 1
        for d in s:
            n *= d
        out.append(lax.slice_in_dim(flat, o, o + n, axis=nl).reshape(lead + tuple(s)))
        o += n
    return out


def _join(g, ax):
    return jnp.concatenate([g[d] for d in range(N_DEV)], axis=ax)


def _split(full, ax):
    n = full.shape[ax] // N_DEV
    return jnp.stack([lax.slice_in_dim(full, d * n, (d + 1) * n, axis=ax) for d in range(N_DEV)])


_WEIGHTS = ['norm_mix', 'norm_ffn', 'attn_w_in', 'attn_w_out', 'relpos_table', 'q_norm_a', 'k_norm_a', 'q_norm_b',
            'k_norm_b', 'sinks', 'ssm_w_in', 'ssm_conv_w', 'ssm_conv_b', 'ssm_dt_bias', 'ssm_a_log', 'ssm_d', 'ssm_norm',
            'ssm_w_out', 'ffn_w_in', 'ffn_conv_w', 'ffn_conv_b', 'ffn_w_out']
_SHARD_AX = {'attn_w_in': 2, 'attn_w_out': 1, 'ssm_w_in': 2, 'ssm_conv_w': 2, 'ssm_conv_b': 1, 'ssm_norm': 1,
             'ssm_w_out': 1, 'ffn_w_in': 2, 'ffn_conv_w': 2, 'ffn_w_out': 1}
_BIG = ['attn_w_in', 'attn_w_out', 'ssm_w_in', 'ssm_w_out', 'ffn_w_in', 'ffn_w_out']
_SMALL = ['ssm_conv_w', 'ssm_conv_b', 'ssm_norm', 'ffn_conv_w']
_AX2 = {n: _SHARD_AX[n] - 1 for n in _BIG}
_REPL = [n for n in _WEIGHTS if n not in _SHARD_AX]


def _rows8(w):
    return jnp.pad(w, ((0, 8 - w.shape[0]), (0, 0)))


def _lanes128(v):
    return jnp.pad(v, (0, LANES - v.shape[0])).reshape(1, LANES)


def _band_mask(n_prev, pad):
    cq = jnp.arange(TQ)[:, None] // CHUNK
    ck = jnp.arange(pad + TQ)[None, :] // CHUNK
    return (ck >= cq) & (ck <= cq + n_prev)


def _ffn_fwd(xin, g, w_in_t, w8, cb, tag):
    gu, h, a, gc = _ffn_in_mid(xin, g, w_in_t, w8, cb, f"mm_ffn_in{tag}")
    return a, (h, gu, a, gc)


def _ffn_bwd(dx, dxb, xin, g, w_in_t, w8, w_out, saved, tag):
    h, gu, a, gc = saved
    dw_out = _mm_tn(a, dxb, f"mm_ffn_dwout{tag}")
    dgu, dw8, dcb = _ffn_mid_bwd(gu, gc, dxb, w_out, w8, f"ffn_mid_bwd{tag}")
    dw_in_t = _mm_tn(dgu, h, f"mm_ffn_dwin{tag}")
    dxp, dxpb, dg = _mm_rms_bwd(dgu, w_in_t, 0, None, xin, g, dx, f"mm_ffn_dh{tag}")
    return dxp, dxpb, dg, dw_in_t, dw8[:3], dcb, dw_out


def kernel(x, norm_mix, norm_ffn, attn_w_in, attn_w_out, relpos_table, q_norm_a, k_norm_a, q_norm_b, k_norm_b, sinks, ssm_w_in, ssm_conv_w, ssm_conv_b, ssm_dt_bias, ssm_a_log, ssm_d, ssm_norm, ssm_w_out, ffn_w_in, ffn_conv_w, ffn_conv_b, ffn_w_out, loss_target, m_norm_mix, m_norm_ffn, m_attn_w_in, m_attn_w_out, m_relpos_table, m_q_norm_a, m_k_norm_a, m_q_norm_b, m_k_norm_b, m_sinks, m_ssm_w_in, m_ssm_conv_w, m_ssm_conv_b, m_ssm_dt_bias, m_ssm_a_log, m_ssm_d, m_ssm_norm, m_ssm_w_out, m_ffn_w_in, m_ffn_conv_w, m_ffn_conv_b, m_ffn_w_out, v_norm_mix, v_norm_ffn, v_attn_w_in, v_attn_w_out, v_relpos_table, v_q_norm_a, v_k_norm_a, v_q_norm_b, v_k_norm_b, v_sinks, v_ssm_w_in, v_ssm_conv_w, v_ssm_conv_b, v_ssm_dt_bias, v_ssm_a_log, v_ssm_d, v_ssm_norm, v_ssm_w_out, v_ffn_w_in, v_ffn_conv_w, v_ffn_conv_b, v_ffn_w_out):
    w = dict(norm_mix=norm_mix, norm_ffn=norm_ffn, attn_w_in=attn_w_in, attn_w_out=attn_w_out, relpos_table=relpos_table,
             q_norm_a=q_norm_a, k_norm_a=k_norm_a, q_norm_b=q_norm_b, k_norm_b=k_norm_b, sinks=sinks, ssm_w_in=ssm_w_in,
             ssm_conv_w=ssm_conv_w, ssm_conv_b=ssm_conv_b, ssm_dt_bias=ssm_dt_bias, ssm_a_log=ssm_a_log, ssm_d=ssm_d,
             ssm_norm=ssm_norm, ssm_w_out=ssm_w_out, ffn_w_in=ffn_w_in, ffn_conv_w=ffn_conv_w, ffn_conv_b=ffn_conv_b,
             ffn_w_out=ffn_w_out)
    mom = dict(norm_mix=m_norm_mix, norm_ffn=m_norm_ffn, attn_w_in=m_attn_w_in, attn_w_out=m_attn_w_out,
               relpos_table=m_relpos_table, q_norm_a=m_q_norm_a, k_norm_a=m_k_norm_a, q_norm_b=m_q_norm_b,
               k_norm_b=m_k_norm_b, sinks=m_sinks, ssm_w_in=m_ssm_w_in, ssm_conv_w=m_ssm_conv_w, ssm_conv_b=m_ssm_conv_b,
               ssm_dt_bias=m_ssm_dt_bias, ssm_a_log=m_ssm_a_log, ssm_d=m_ssm_d, ssm_norm=m_ssm_norm, ssm_w_out=m_ssm_w_out,
               ffn_w_in=m_ffn_w_in, ffn_conv_w=m_ffn_conv_w, ffn_conv_b=m_ffn_conv_b, ffn_w_out=m_ffn_w_out)
    var = dict(norm_mix=v_norm_mix, norm_ffn=v_norm_ffn, attn_w_in=v_attn_w_in, attn_w_out=v_attn_w_out,
               relpos_table=v_relpos_table, q_norm_a=v_q_norm_a, k_norm_a=v_k_norm_a, q_norm_b=v_q_norm_b,
               k_norm_b=v_k_norm_b, sinks=v_sinks, ssm_w_in=v_ssm_w_in, ssm_conv_w=v_ssm_conv_w, ssm_conv_b=v_ssm_conv_b,
               ssm_dt_bias=v_ssm_dt_bias, ssm_a_log=v_ssm_a_log, ssm_d=v_ssm_d, ssm_norm=v_ssm_norm, ssm_w_out=v_ssm_w_out,
               ffn_w_in=v_ffn_w_in, ffn_conv_w=v_ffn_conv_w, ffn_conv_b=v_ffn_conv_b, ffn_w_out=v_ffn_w_out)

    def piece(n, l):
        return (w[n][l].T if _AX2[n] == 1 else w[n][l]).astype(BF16)

    def gather_of(names_layers):
        return _Gather2([piece(n, l) for n, l in names_layers])

    def joined(got):
        return [g.reshape(-1, D_MODEL) for g in got]

    first = [('attn_w_in', 0), ('attn_w_out', 0)]
    got = _exchange(_Gather2([piece(n, l) for n, l in first] + [_pack([w[n] for n in _SMALL], F32)]), "gather_attn")
    wt_attn_in, w_attn_out = joined(got[:2])
    full = {}
    for n, g in zip(_SMALL, _unpack(got[2], [w[n].shape for n in _SMALL], lead=(N_DEV,))):
        full[n] = _join(g, _SHARD_AX[n])
    ssm_cw8 = _rows8(full['ssm_conv_w'][0])
    ssm_cb = full['ssm_conv_b']
    ssm_nw = full['ssm_norm']
    ffn_cw8 = [_rows8(full['ffn_conv_w'][l]) for l in range(2)]
    ffn_cb = [ffn_conv_b[l:l + 1] for l in range(2)]

    x0 = x[0]
    target = loss_target[0]
    t = x0.shape[0]

    g_mix0, g_mix1 = norm_mix[0:1], norm_mix[1:2]
    g_ffn0, g_ffn1 = norm_ffn[0:1], norm_ffn[1:2]
    proj, h0 = _rms_mm(x0, g_mix0, wt_attn_in, 2304, "mm_attn_in", F32)
    hn_w = jnp.concatenate([jnp.tile(v, (1, 2)) for v in (q_norm_a, k_norm_a, q_norm_b, k_norm_b)], axis=0)
    qa, kpa, vpa, qb, kpb, vpb = _headnorm_fwd(proj, hn_w, "headnorm")
    table = jnp.pad(relpos_table[0], ((0, 0), (0, REL_W - (2 * MAX_REL + 1))))
    bias_a = jnp.where(_band_mask(A_PREV, PAD_A)[None], jnp.transpose(_relpos_fwd(table, "relpos_bias"), (1, 0, 2)), NEG)
    rel_b = jnp.arange(TQ)[:, None] - (jnp.arange(PAD_B + TQ)[None, :] - PAD_B)
    slopes = 2.0 ** (-8.0 * jnp.arange(1, N_HEADS + 1, dtype=F32) / N_HEADS)
    bias_b = jnp.where(_band_mask(B_PREV, PAD_B)[None], -slopes[:, None, None] * jnp.abs(rel_b).astype(F32)[None], NEG)
    no_sinks = jnp.full((N_HEADS,), NEG, F32)
    ffn0_w, ssm_w, ffn1_w = [('ffn_w_in', 0), ('ffn_w_out', 0)], [('ssm_w_in', 0), ('ssm_w_out', 0)], [('ffn_w_in', 1), ('ffn_w_out', 1)]
    oa, stats_a, got = _attn_fwd(qa, kpa, vpa, bias_a, no_sinks, PAD_A, "attn_a", comm=gather_of(ffn0_w + ssm_w))
    wt_ffn_in0, w_ffn_out0, wt_ssm_in, w_ssm_out = joined(got)
    ob, stats_b, got = _attn_fwd(qb, kpb, vpb, bias_b, sinks[0], PAD_B, "attn_b", comm=gather_of(ffn1_w))
    wt_ffn_in1, w_ffn_out1 = joined(got)
    wt_ssm_dt = jnp.pad(wt_ssm_in[ZX:], ((0, LANES - SSM_HEADS), (0, 0)))
    x1 = _mm(oa, w_attn_out, "mm_attn_out_a", res=x0, b_rows=(0, 512))
    x1 = _mm(ob, w_attn_out, "mm_attn_out_b", res=x1, b_rows=(512, 512))
    a0, ffn0_saved = _ffn_fwd(x1, g_ffn0, wt_ffn_in0, ffn_cw8[0], ffn_cb[0], "0")
    x2 = _mm(a0, w_ffn_out0, "mm_ffn_out0", res=x1)

    zx, h2, xbc, conv_pre = _ssm_in_pre(x2, g_mix1, wt_ssm_in, ssm_cw8, ssm_cb, "mm_ssm_in")
    dtraw = _mm(h2, wt_ssm_dt, "mm_ssm_dt", trans_b=True)
    dt_bias = _lanes128(ssm_dt_bias[0])
    alog = _lanes128(ssm_a_log[0])
    dexp = jnp.repeat(ssm_d[0], HEAD_DIM).reshape(1, D_INNER)
    dt = _dt_fwd(dtraw, dt_bias, "ssm_dt")
    (y, sprev, y4), _ = _ssd_fwd(xbc, dt, alog, zx, dexp, ssm_nw, "ssd_fwd")
    x3 = _mm(y4, w_ssm_out, "mm_ssm_out", res=x2)
    a1, ffn1_saved = _ffn_fwd(x3, g_ffn1, wt_ffn_in1, ffn_cw8[1], ffn_cb[1], "1")

    dx4, dx4b, sq = _mm_loss(a1, w_ffn_out1, x3, target, "mm_ffn_out1_loss")
    loss = lax.psum(0.5 * jnp.sum(sq) / D_MODEL, ("x", "y", "c"))

    grads = {}

    def scatter_of(grads_2d):
        return _Comm([g.reshape(N_DEV, -1, D_MODEL) for g in grads_2d], [True] * len(grads_2d))

    dx3, dx3b, dg_ffn1, dwtin1, dcw1, dcb1, dwout1 = _ffn_bwd(
        dx4, dx4b, x3, g_ffn1, wt_ffn_in1, ffn_cw8[1], w_ffn_out1, ffn1_saved, "1")

    dy4 = _mm(dx3b, w_ssm_out, "mm_ssm_dy", trans_b=True)
    dw_ssm_out = _mm_tn(y4, dx3b, "mm_ssm_dwout")
    (dxbc, ddt, dalog, dz, dd_lane, dnw), parts_ffn1 = _ssd_bwd(
        xbc, dt, alog, sprev, dy4, y, zx, dexp, ssm_nw, "ssd_bwd", comm=scatter_of([dwtin1, dwout1]))
    dxr, dcw_s, dcb_s = _ssm_pre_bwd(zx, conv_pre, dxbc, ssm_cw8, "ssm_pre_bwd")
    ddtraw, ddtb = _dt_bwd(dtraw, dt_bias, ddt, "ssm_dt_bwd")
    dh2 = _mm(dz, wt_ssm_in, "mm_ssm_dh_z", b_rows=(0, D_INNER))
    dh2 = _mm(dxr, wt_ssm_in[D_INNER:ZX], "mm_ssm_dh_x", res=dh2)
    dwt_ssm_in = jnp.concatenate([
        _mm_tn(dz, h2, "mm_ssm_dwin_z"), _mm_tn(dxr, h2, "mm_ssm_dwin_x"),
        _mm_tn(ddtraw, h2, "mm_ssm_dwin_dt")[:SSM_HEADS]], axis=0)
    dx2, dx2b, dg_mix1 = _mm_rms_bwd(ddtraw, wt_ssm_dt, 0, dh2, x2, g_mix1, dx3, "mm_ssm_dh_dt")
    grads['ssm_conv_w'] = dcw_s[:4][None]
    grads['ssm_conv_b'] = dcb_s
    grads['ssm_norm'] = dnw
    grads['ssm_dt_bias'] = ddtb[:, :SSM_HEADS]
    grads['ssm_a_log'] = dalog[:, :SSM_HEADS]
    grads['ssm_d'] = jnp.sum(dd_lane.reshape(SSM_HEADS, HEAD_DIM), axis=1)[None]

    dx1, dx1b, dg_ffn0, dwtin0, dcw0, dcb0, dwout0 = _ffn_bwd(
        dx2, dx2b, x1, g_ffn0, wt_ffn_in0, ffn_cw8[0], w_ffn_out0, ffn0_saved, "0")
    grads['ffn_conv_w'] = jnp.stack([dcw0, dcw1])
    grads['ffn_conv_b'] = jnp.concatenate([dcb0, dcb1], axis=0)
    grads['norm_ffn'] = jnp.concatenate([dg_ffn0, dg_ffn1], axis=0)

    do = _mm(dx1b, w_attn_out, "mm_attn_do", out_dtype=BF16, trans_b=True)
    dw_attn_out = jnp.concatenate([_mm_tn(oa, dx1b, "mm_attn_dwout_a"), _mm_tn(ob, dx1b, "mm_attn_dwout_b")], axis=0)
    (dqa, dkpa, dvpa, dbias_a, _), parts_ssm = _attn_bwd(
        qa, kpa, vpa, bias_a, no_sinks, do, stats_a, 0, PAD_A, "attn_a_bwd",
        comm=scatter_of([dwt_ssm_in, dw_ssm_out, dw_attn_out]))
    (dqb, dkpb, dvpb, _, dsink), parts_ffn0 = _attn_bwd(
        qb, kpb, vpb, bias_b, sinks[0], do, stats_b, 4, PAD_B, "attn_b_bwd", comm=scatter_of([dwtin0, dwout0]))
    grads['relpos_table'] = _relpos_bwd(jnp.transpose(dbias_a, (1, 0, 2)), "relpos_bwd")[None, :, :2 * MAX_REL + 1]
    grads['sinks'] = dsink[:, :2, 0].reshape(1, N_HEADS)
    dproj, dhn = _headnorm_bwd(proj, hn_w, dqa, dkpa, dvpa, dqb, dkpb, dvpb, "headnorm_bwd")
    dhn = dhn[:, :HEAD_DIM] + dhn[:, HEAD_DIM:]
    for k, n in enumerate(('q_norm_a', 'k_norm_a', 'q_norm_b', 'k_norm_b')):
        grads[n] = dhn[k:k + 1]
    dwt_attn_in = _mm_tn(dproj, h0, "mm_attn_dwin")
    dx0, _, dg_mix0, parts_attn_in = _mm_rms_bwd(dproj, wt_attn_in, 0, None, x0, g_mix0, dx1, "mm_attn_dh",
                                                 comm=scatter_of([dwt_attn_in]))
    grads['norm_mix'] = jnp.concatenate([dg_mix0, dg_mix1], axis=0)

    def summed_t(parts, name):
        return _sum_parts(parts, name).T[None]

    sm_shapes = [w[n].shape for n in _SMALL]
    rp_shapes = [w[n].shape for n in _REPL]
    recv = _exchange(_Comm(
        [_pack([_split(grads[n], _SHARD_AX[n]) for n in _SMALL], F32, lead=(N_DEV,)), _pack([grads[n] for n in _REPL], F32)],
        [True, False]), "exchange_small")
    big_parts = {
        'attn_w_in': [summed_t(parts_attn_in[0], "sum_attn_w_in")], 'attn_w_out': [parts_ssm[2]],
        'ssm_w_in': [summed_t(parts_ssm[0], "sum_ssm_w_in")], 'ssm_w_out': [parts_ssm[1]],
        'ffn_w_in': [summed_t(parts_ffn0[0], "sum_ffn_w_in0"), summed_t(parts_ffn1[0], "sum_ffn_w_in1")],
        'ffn_w_out': [parts_ffn0[1], parts_ffn1[1]],
    }
    res = [{}, {}, {}, {}]
    for n in _BIG:
        for kind, a in enumerate(_adamw(big_parts[n], w[n], mom[n], var[n], f"adamw_{n}")):
            res[kind][n] = a
    for names, shapes, parts in ((_SMALL, sm_shapes, recv[0]), (_REPL, rp_shapes, recv[1])):
        outs = _adamw([parts], _pack([w[n] for n in names], F32)[None], _pack([mom[n] for n in names], F32)[None],
                      _pack([var[n] for n in names], F32)[None], "adamw_" + ("small" if names is _SMALL else "replicated"))
        for kind, flat in enumerate(outs):
            for n, a in zip(names, _unpack(flat[0], shapes)):
                res[kind][n] = a
    return (loss, dx0[None], *[res[0][n] for n in _WEIGHTS], *[res[1][n] for n in _WEIGHTS],
            *[res[2][n] for n in _WEIGHTS], *[res[3][n] for n in _WEIGHTS])
```

```python
import jax
import jax.numpy as jnp
from jax import lax
from jax.experimental import pallas as pl
from jax.experimental.pallas import tpu as pltpu

F32 = jnp.float32
BF16 = jnp.bfloat16
HI = lax.Precision.HIGHEST
MESH = pl.DeviceIdType.MESH
NEG = -1e30

N_DEV = 8
D_MODEL = 1024
EPS = 1e-6
CHUNK = 64
HEAD_DIM = 64
N_HEADS = 8
A_PREV = 8
B_PREV = 2
MAX_REL = 256
TQ = 2 * CHUNK
ATT_SUB = 8
PAD_A = A_PREV * CHUNK
PAD_B = B_PREV * CHUNK
REL_W = PAD_A + TQ
D_INNER = 2048
SSM_HEADS = 32
SSM_GROUPS = 4
SSM_STATE = 128
XBC = D_INNER + 2 * SSM_GROUPS * SSM_STATE
ZX = D_INNER + XBC
D_FF = 2816
SSD_L = 128
LANES = 128
VMEM_LIMIT = 56 << 20

ADAM_LR, ADAM_B1, ADAM_B2, ADAM_EPS, ADAM_WD, ADAM_STEP = 0.001, 0.9, 0.999, 1e-08, 0.01, 10


def _cp(sem=None):
    return pltpu.CompilerParams(dimension_semantics=sem, vmem_limit_bytes=VMEM_LIMIT)


def _dot(a, b, ca=1, cb=0, prec=None):
    return lax.dot_general(a, b, (((ca,), (cb,)), ((), ())), preferred_element_type=F32, precision=prec)


def _pick(n, cands):
    for c in cands:
        if n % c == 0:
            return c
    return n


def _lo_mask():
    return lax.broadcasted_iota(jnp.int32, (1, LANES), 1) < HEAD_DIM


_TN_CHUNKS = (1408, 1536, 1152, 1024, 512, 256, 128)


TN_MAX_ROWS = 3072


def _mm_tn(a, b, name):
    kdim, m = a.shape
    n = b.shape[1]
    assert b.shape[0] == kdim, (a.shape, b.shape)
    mb = m if m <= TN_MAX_ROWS else m // 2
    tn = _pick(n, _TN_CHUNKS)
    tk = _pick(kdim, (512, 256, 128))
    nk = kdim // tk

    def body(a_ref, b_ref, o_ref, acc):
        k = pl.program_id(1)

        @pl.when(k == 0)
        def _():
            acc[...] = jnp.zeros_like(acc)

        av = a_ref[...]
        for c in range(0, n, tn):
            acc[:, c:c + tn] += _dot(av, b_ref[:, c:c + tn], 0, 0)

        @pl.when(k == nk - 1)
        def _():
            o_ref[...] = acc[...].astype(BF16)

    return pl.pallas_call(
        body, name=name, grid=(m // mb, nk),
        in_specs=[pl.BlockSpec((tk, mb), lambda j, k: (k, j)), pl.BlockSpec((tk, n), lambda j, k: (k, 0))],
        out_specs=pl.BlockSpec((mb, n), lambda j, k: (j, 0)), out_shape=jax.ShapeDtypeStruct((m, n), BF16),
        scratch_shapes=[pltpu.VMEM((mb, n), F32)], compiler_params=_cp(("parallel", "arbitrary")),
    )(a, b)


def _mm(a, b, name, out_dtype=F32, res=None, trans_b=False, b_rows=None):
    m, kdim = a.shape
    if b_rows is None:
        b_rows = (0, b.shape[0])
    off, rows = b_rows
    n = rows if trans_b else b.shape[1]
    assert (b.shape[1] if trans_b else rows) == kdim and off % rows == 0, (a.shape, b.shape, b_rows)
    tn = _pick(n, _TN_CHUNKS)
    tm = _pick(m, (256, 128) if n > 2304 else (512, 256, 128))

    def body(*refs):
        if res is None:
            a_ref, b_ref, o_ref = refs
        else:
            a_ref, b_ref, r_ref, o_ref = refs
        av = a_ref[...]
        for c in range(0, n, tn):
            r = _dot(av, b_ref[c:c + tn, :], 1, 1) if trans_b else _dot(av, b_ref[:, c:c + tn], 1, 0)
            if res is not None:
                r = r + r_ref[:, c:c + tn]
            o_ref[:, c:c + tn] = r.astype(out_dtype)

    in_specs = [pl.BlockSpec((tm, kdim), lambda i: (i, 0)), pl.BlockSpec((rows, b.shape[1]), lambda i: (off // rows, 0))]
    args = [a, b]
    if res is not None:
        in_specs.append(pl.BlockSpec((tm, n), lambda i: (i, 0)))
        args.append(res)
    return pl.pallas_call(
        body, name=name, grid=(m // tm,), in_specs=in_specs, out_specs=pl.BlockSpec((tm, n), lambda i: (i, 0)),
        out_shape=jax.ShapeDtypeStruct((m, n), out_dtype), compiler_params=_cp(("parallel",)),
    )(*args)


def _rms_mm(x, g, bt, n, name, out_dtype):
    t, d = x.shape
    tn = _pick(n, _TN_CHUNKS)
    tm = _pick(t, (256, 128))

    def body(x_ref, g_ref, b_ref, o_ref, h_ref):
        xv = x_ref[...]
        r = lax.rsqrt(jnp.mean(xv * xv, axis=-1, keepdims=True) + EPS)
        h = (xv * r * g_ref[...]).astype(BF16)
        h_ref[...] = h
        for c in range(0, n, tn):
            o_ref[:, c:c + tn] = _dot(h, b_ref[c:c + tn, :], 1, 1).astype(out_dtype)

    row = pl.BlockSpec((tm, d), lambda i: (i, 0))
    return pl.pallas_call(
        body, name=name, grid=(t // tm,),
        in_specs=[row, pl.BlockSpec((1, d), lambda i: (0, 0)), pl.BlockSpec(bt.shape, lambda i: (0, 0))],
        out_specs=[pl.BlockSpec((tm, n), lambda i: (i, 0)), row],
        out_shape=[jax.ShapeDtypeStruct((t, n), out_dtype), jax.ShapeDtypeStruct((t, d), BF16)],
        compiler_params=_cp(("parallel",)),
    )(x, g, bt)


def _mm_rms_bwd(a, b, b_off, dh_prev, x, g, dres, name, comm=None):
    t, d = x.shape
    kdim = a.shape[1]
    assert b_off % kdim == 0 and b.shape[1] == d, (a.shape, b.shape, b_off)
    tm = _pick(t, (256, 128))

    def body(*refs):
        if dh_prev is None:
            a_ref, b_ref, x_ref, g_ref, dr_ref, dx_ref, dxb_ref, dg_ref = refs
            dhv = _dot(a_ref[...], b_ref[...], 1, 0)
        else:
            a_ref, b_ref, p_ref, x_ref, g_ref, dr_ref, dx_ref, dxb_ref, dg_ref = refs
            dhv = _dot(a_ref[...], b_ref[...], 1, 0) + p_ref[...]
        xv = x_ref[...]
        r = lax.rsqrt(jnp.mean(xv * xv, axis=-1, keepdims=True) + EPS)
        xh = xv * r
        dxh = dhv * g_ref[...]
        dx = dr_ref[...] + r * (dxh - xh * jnp.mean(dxh * xh, axis=-1, keepdims=True))
        dx_ref[...] = dx
        dxb_ref[...] = dx.astype(BF16)

        @pl.when(pl.program_id(0) == 0)
        def _():
            dg_ref[...] = jnp.zeros_like(dg_ref)

        dg_ref[...] += jnp.sum(dhv * xh, axis=0, keepdims=True)

    row = pl.BlockSpec((tm, d), lambda i: (i, 0))
    vec = pl.BlockSpec((1, d), lambda i: (0, 0))
    in_specs = [pl.BlockSpec((tm, kdim), lambda i: (i, 0)), pl.BlockSpec((kdim, d), lambda i: (b_off // kdim, 0))]
    args = [a, b]
    if dh_prev is not None:
        in_specs.append(row)
        args.append(dh_prev)
    outs, got = _call(
        body, name=name, grid=(t // tm,), in_specs=in_specs + [row, vec, row], out_specs=[row, row, vec],
        out_shape=[jax.ShapeDtypeStruct((t, d), F32), jax.ShapeDtypeStruct((t, d), BF16), jax.ShapeDtypeStruct((1, d), F32)],
        args=(*args, x, g, dres), sem=("arbitrary",), comm=comm)
    return (*outs, got) if comm is not None else tuple(outs)


def _mm_loss(a, b, res, target, name):
    t, kdim = a.shape
    d = b.shape[1]
    tm = _pick(t, (512, 256, 128))

    def body(a_ref, b_ref, r_ref, t_ref, dy_ref, dyb_ref, acc_ref):
        @pl.when(pl.program_id(0) == 0)
        def _():
            acc_ref[...] = jnp.zeros_like(acc_ref)

        err = _dot(a_ref[...], b_ref[...], 1, 0) + r_ref[...] - t_ref[...]
        dy = err * (1.0 / d)
        dy_ref[...] = dy
        dyb_ref[...] = dy.astype(BF16)
        acc_ref[...] += jnp.sum(err * err, axis=0, keepdims=True)

    row = pl.BlockSpec((tm, d), lambda i: (i, 0))
    vec = pl.BlockSpec((1, d), lambda i: (0, 0))
    return pl.pallas_call(
        body, name=name, grid=(t // tm,),
        in_specs=[pl.BlockSpec((tm, kdim), lambda i: (i, 0)), pl.BlockSpec((kdim, d), lambda i: (0, 0)), row, row],
        out_specs=[row, row, vec],
        out_shape=[jax.ShapeDtypeStruct((t, d), F32), jax.ShapeDtypeStruct((t, d), BF16), jax.ShapeDtypeStruct((1, d), F32)],
        compiler_params=_cp(("arbitrary",)),
    )(a, b, res, target)


def _head_sums(v):
    ri = lax.broadcasted_iota(jnp.int32, (LANES, LANES), 0) // HEAD_DIM
    ci = lax.broadcasted_iota(jnp.int32, (LANES, LANES), 1) // HEAD_DIM
    ones = (ri == ci).astype(BF16)
    hi = v.astype(BF16)
    lo_part = (v - hi.astype(F32)).astype(BF16)
    return _dot(hi, ones, 1, 0) + _dot(lo_part, ones, 1, 0)


def _head_rms(xs, w, lo):
    r = lax.rsqrt(_head_sums(xs * xs) * (1.0 / HEAD_DIM) + EPS)
    return xs * r, r


def _head_rms_bwd(xs, w, dy, lo):
    xh, r = _head_rms(xs, w, lo)
    dxh = dy * w
    mm = _head_sums(dxh * xh) * (1.0 / HEAD_DIM)
    return r * (dxh - xh * mm), dy * xh


_QSCALE = HEAD_DIM ** -0.5


def _headnorm_fwd(proj, ws, name):
    t = proj.shape[0]
    tm = TQ
    lead = PAD_A // tm
    leadb = PAD_B // tm

    def body(p_ref, w_ref, qa_ref, ka_ref, va_ref, qb_ref, kb_ref, vb_ref):
        data = pl.program_id(0) >= lead
        lo = _lo_mask()

        def put(ref, c, val):
            ref[:, c:c + val.shape[1]] = jnp.where(data, val, 0.0).astype(BF16)

        def per_query_head(slab):
            other = pltpu.roll(slab, HEAD_DIM, 1)
            e0, e1 = jnp.where(lo, slab, other), jnp.where(lo, other, slab)
            return jnp.concatenate([e0, e0, e1, e1], axis=1)

        for s in range(4):
            c = LANES * s
            xh, _ = _head_rms(p_ref[:, c:c + LANES], None, lo)
            qa_ref[:, c:c + LANES] = (xh * w_ref[0:1, :] * _QSCALE).astype(BF16)
            xh, _ = _head_rms(p_ref[:, 512 + c:512 + c + LANES], None, lo)
            put(ka_ref, c, xh * w_ref[1:2, :])
            xh, _ = _head_rms(p_ref[:, 1536 + c:1536 + c + LANES], None, lo)
            qb_ref[:, c:c + LANES] = (xh * w_ref[2:3, :] * _QSCALE).astype(BF16)
        put(va_ref, 0, p_ref[:, 1024:1536])
        xh, _ = _head_rms(p_ref[:, 2048:2176], None, lo)
        put(kb_ref, 0, per_query_head(xh * w_ref[3:4, :]))
        put(vb_ref, 0, per_query_head(p_ref[:, 2176:2304]))

    src = lambda i: jnp.maximum(i - lead, 0)
    wide = pl.BlockSpec((tm, 512), lambda i: (src(i), 0))
    pad_a = pl.BlockSpec((tm, 512), lambda i: (i, 0))
    pad_b = pl.BlockSpec((tm, 512), lambda i: (jnp.maximum(i - lead + leadb, 0), 0))
    sd = lambda rows: jax.ShapeDtypeStruct((rows, 512), BF16)
    return pl.pallas_call(
        body, name=name, grid=(t // tm + lead,),
        in_specs=[pl.BlockSpec((tm, 2304), lambda i: (src(i), 0)), pl.BlockSpec((4, LANES), lambda i: (0, 0))],
        out_specs=[wide, pad_a, pad_a, wide, pad_b, pad_b],
        out_shape=[sd(t), sd(t + PAD_A), sd(t + PAD_A), sd(t), sd(t + PAD_B), sd(t + PAD_B)],
        compiler_params=_cp(("arbitrary",)),
    )(proj, ws)


def _headnorm_bwd(proj, ws, dqa, dkpa, dvpa, dqb, dkpb, dvpb, name):
    t = proj.shape[0]
    tm = TQ
    offa, offb = PAD_A // tm, PAD_B // tm

    def body(p_ref, w_ref, dqa_ref, dka_ref, dva_ref, dqb_ref, dkb_ref, dvb_ref, dp_ref, dw_ref):
        i = pl.program_id(0)
        lo = _lo_mask()

        @pl.when(i == 0)
        def _():
            dw_ref[...] = jnp.zeros_like(dw_ref)

        acc = [jnp.zeros((1, LANES), F32) for _ in range(4)]
        for s in range(4):
            c = LANES * s
            dx, dwl = _head_rms_bwd(p_ref[:, c:c + LANES], w_ref[0:1, :], dqa_ref[:, c:c + LANES] * _QSCALE, lo)
            dp_ref[:, c:c + LANES] = dx.astype(BF16)
            acc[0] += jnp.sum(dwl, axis=0, keepdims=True)
            dx, dwl = _head_rms_bwd(p_ref[:, 512 + c:512 + c + LANES], w_ref[1:2, :], dka_ref[:, c:c + LANES], lo)
            dp_ref[:, 512 + c:512 + c + LANES] = dx.astype(BF16)
            acc[1] += jnp.sum(dwl, axis=0, keepdims=True)
            dx, dwl = _head_rms_bwd(p_ref[:, 1536 + c:1536 + c + LANES], w_ref[2:3, :], dqb_ref[:, c:c + LANES] * _QSCALE, lo)
            dp_ref[:, 1536 + c:1536 + c + LANES] = dx.astype(BF16)
            acc[2] += jnp.sum(dwl, axis=0, keepdims=True)
        dp_ref[:, 1024:1536] = dva_ref[...].astype(BF16)

        def group_sum(ref):
            s0 = ref[:, 0:128] + ref[:, 128:256]
            s1 = ref[:, 256:384] + ref[:, 384:512]
            s0 = s0 + pltpu.roll(s0, HEAD_DIM, 1)
            s1 = s1 + pltpu.roll(s1, HEAD_DIM, 1)
            return jnp.where(lo, s0, s1)

        dx, dwl = _head_rms_bwd(p_ref[:, 2048:2176], w_ref[3:4, :], group_sum(dkb_ref), lo)
        dp_ref[:, 2048:2176] = dx.astype(BF16)
        acc[3] += jnp.sum(dwl, axis=0, keepdims=True)
        dp_ref[:, 2176:2304] = group_sum(dvb_ref).astype(BF16)
        for n in range(4):
            dw_ref[n:n + 1, :] += acc[n]

    wide = pl.BlockSpec((tm, 512), lambda i: (i, 0))
    pa = pl.BlockSpec((tm, 512), lambda i: (i + offa, 0))
    pb = pl.BlockSpec((tm, 512), lambda i: (i + offb, 0))
    return pl.pallas_call(
        body, name=name, grid=(t // tm,),
        in_specs=[pl.BlockSpec((tm, 2304), lambda i: (i, 0)), pl.BlockSpec((4, LANES), lambda i: (0, 0)),
                  wide, pa, pa, wide, pb, pb],
        out_specs=[pl.BlockSpec((tm, 2304), lambda i: (i, 0)), pl.BlockSpec((4, LANES), lambda i: (0, 0))],
        out_shape=[jax.ShapeDtypeStruct((t, 2304), BF16), jax.ShapeDtypeStruct((4, LANES), F32)],
        compiler_params=_cp(("arbitrary",)),
    )(proj, ws, dqa, dkpa, dvpa, dqb, dkpb, dvpb)


ROLL_W = 1024


def _rel_onehot():
    r_io = lax.broadcasted_iota(jnp.int32, (REL_W, ROLL_W), 0)
    m_io = lax.broadcasted_iota(jnp.int32, (REL_W, ROLL_W), 1)
    return (r_io == jnp.clip(REL_W - 1 - m_io, -MAX_REL, MAX_REL) + MAX_REL).astype(F32)


def _relpos_fwd(table, name):
    def body(t_ref, o_ref):
        rr = _dot(t_ref[...], _rel_onehot(), 1, 0, HI)

        def step(q, c):
            o_ref[q] = pltpu.roll(rr, (ROLL_W - (TQ - 1) + q) % ROLL_W, 1)[:, :REL_W]
            return c

        lax.fori_loop(0, TQ, step, 0)

    return pl.pallas_call(
        body, name=name, out_shape=jax.ShapeDtypeStruct((TQ, N_HEADS, REL_W), F32),
        in_specs=[pl.BlockSpec(memory_space=pltpu.VMEM)], out_specs=pl.BlockSpec(memory_space=pltpu.VMEM),
        compiler_params=_cp(),
    )(table)


def _relpos_bwd(dbias_t, name):
    def body(d_ref, o_ref):
        def step(q, acc):
            row = jnp.concatenate([d_ref[q], jnp.zeros((N_HEADS, ROLL_W - REL_W), F32)], axis=1)
            return acc + pltpu.roll(row, TQ - 1 - q, 1)

        drr = lax.fori_loop(0, TQ, step, jnp.zeros((N_HEADS, ROLL_W), F32))
        o_ref[...] = _dot(drr, _rel_onehot(), 1, 1, HI)

    return pl.pallas_call(
        body, name=name, out_shape=jax.ShapeDtypeStruct((N_HEADS, REL_W), F32),
        in_specs=[pl.BlockSpec(memory_space=pltpu.VMEM)], out_specs=pl.BlockSpec(memory_space=pltpu.VMEM),
        compiler_params=_cp(),
    )(dbias_t)


def _attn_scores(qe, kw, bias, kvalid):
    return jnp.where(kvalid, _dot(qe, kw, 1, 1) + bias, NEG)


def _stat_cols(stats, e):
    return stats[:, 64 * e:64 * e + 1], stats[:, 64 * e + 32:64 * e + 33]


def _attn_fwd(q, kp, vp, bias, sinks, pad, name, comm=None):
    t, hd = q.shape
    w = pad + TQ

    def body(sink_ref, q_ref, k_ref, v_ref, b_ref, o_ref, st_ref):
        hp, i = pl.program_id(0), pl.program_id(1)
        lo = _lo_mask()
        lane = lax.broadcasted_iota(jnp.int32, (1, LANES), 1)
        for j in range(ATT_SUB):
            start = pl.multiple_of((i * ATT_SUB + j) * TQ, TQ)
            qv = q_ref[TQ * j:TQ * (j + 1), :]
            kw = k_ref[pl.ds(start, w), :]
            vw = v_ref[pl.ds(start, w), :]
            kvalid = (start + lax.broadcasted_iota(jnp.int32, (1, w), 1)) >= pad
            outs, ms, ls = [], [], []
            for e in range(2):
                sel = lo if e == 0 else jnp.logical_not(lo)
                qe = jnp.where(sel, qv, jnp.zeros_like(qv))
                snk = sink_ref[2 * hp + e]
                s = _attn_scores(qe, kw, b_ref[e], kvalid)
                m = jnp.maximum(jnp.max(s, axis=-1, keepdims=True), snk)
                acc = _dot(jnp.exp(s - m).astype(BF16), jnp.where(sel, vw, jnp.ones_like(vw)), 1, 0)
                denom = acc[:, 64 * (1 - e):64 * (1 - e) + 1] + jnp.exp(snk - m)
                outs.append(acc * (1.0 / denom))
                ms.append(m)
                ls.append(denom)
            o_ref[TQ * j:TQ * (j + 1), :] = jnp.where(lo, outs[0], outs[1]).astype(BF16)
            st_ref[TQ * j:TQ * (j + 1), :] = jnp.where(lane < 32, ms[0], jnp.where(lane < 64, ls[0],
                                                                                 jnp.where(lane < 96, ms[1], ls[1])))

    full = pl.BlockSpec((t + pad, LANES), lambda h, i: (0, h))
    tile = pl.BlockSpec((ATT_SUB * TQ, LANES), lambda h, i: (i, h))
    (o, stats), got = _call(
        body, name=name, grid=(hd // LANES, t // (ATT_SUB * TQ)),
        in_specs=[pl.BlockSpec(memory_space=pltpu.SMEM), tile, full, full, pl.BlockSpec((2, TQ, w), lambda h, i: (h, 0, 0))],
        out_specs=[tile, tile], out_shape=[jax.ShapeDtypeStruct((t, hd), BF16), jax.ShapeDtypeStruct((t, hd), F32)],
        args=(sinks, q, kp, vp, bias), sem=("parallel", "arbitrary"), comm=comm)
    return o, stats, got


def _attn_bwd(q, kp, vp, bias, sinks, do, stats, col_off, pad, name, comm=None):
    t, hd = q.shape
    w = pad + TQ
    nhp = hd // LANES

    def body(sink_ref, q_ref, k_ref, v_ref, b_ref, do_ref, st_ref, dq_ref, dk_ref, dv_ref, db_ref, ds_ref):
        hp, i = pl.program_id(0), pl.program_id(1)

        @pl.when(i == 0)
        def _():
            dk_ref[...] = jnp.zeros_like(dk_ref)
            dv_ref[...] = jnp.zeros_like(dv_ref)
            db_ref[...] = jnp.zeros_like(db_ref)
            ds_ref[...] = jnp.zeros_like(ds_ref)

        lo = _lo_mask()
        row8 = lax.broadcasted_iota(jnp.int32, (8, LANES), 0)
        dbias = [None, None]
        dsink = jnp.zeros((8, LANES), F32)
        for j in range(ATT_SUB):
            start = pl.multiple_of((i * ATT_SUB + j) * TQ, TQ)
            qv = q_ref[TQ * j:TQ * (j + 1), :]
            dov = do_ref[TQ * j:TQ * (j + 1), :]
            kw = k_ref[pl.ds(start, w), :]
            vw = v_ref[pl.ds(start, w), :]
            kvalid = (start + lax.broadcasted_iota(jnp.int32, (1, w), 1)) >= pad
            stats = st_ref[TQ * j:TQ * (j + 1), :]
            dqs, dkw, dvw = [], None, None
            for e in range(2):
                sel = lo if e == 0 else jnp.logical_not(lo)
                qe = jnp.where(sel, qv, jnp.zeros_like(qv))
                doe = jnp.where(sel, dov, jnp.zeros_like(dov))
                m, denom = _stat_cols(stats, e)
                inv = 1.0 / denom
                p = jnp.exp(_attn_scores(qe, kw, b_ref[e], kvalid) - m) * inv
                psink = jnp.exp(sink_ref[2 * hp + e] - m) * inv
                dp = _dot(doe, vw, 1, 1)
                delta = jnp.sum(p * dp, axis=-1, keepdims=True)
                ds = p * (dp - delta)
                dbias[e] = ds if dbias[e] is None else dbias[e] + ds
                dsink = dsink + jnp.where(row8 == e, jnp.sum(-psink * delta, axis=0, keepdims=True), 0.0)
                dsb = ds.astype(BF16)
                dqs.append(_dot(dsb, kw, 1, 0))
                dk_e = _dot(dsb, qe, 0, 0)
                dv_e = _dot(p.astype(BF16), doe, 0, 0)
                dkw = dk_e if dkw is None else dkw + dk_e
                dvw = dv_e if dvw is None else dvw + dv_e
            dq_ref[TQ * j:TQ * (j + 1), :] = jnp.where(lo, dqs[0], dqs[1])
            dk_ref[pl.ds(start, w), :] += dkw
            dv_ref[pl.ds(start, w), :] += dvw
        for e in range(2):
            db_ref[e] += dbias[e]
        ds_ref[0] += dsink

    full = pl.BlockSpec((t + pad, LANES), lambda h, i: (0, h))
    tile = pl.BlockSpec((ATT_SUB * TQ, LANES), lambda h, i: (i, h))
    btile = pl.BlockSpec((2, TQ, w), lambda h, i: (h, 0, 0))
    return _call(
        body, name=name, grid=(nhp, t // (ATT_SUB * TQ)),
        in_specs=[pl.BlockSpec(memory_space=pltpu.SMEM), tile, full, full, btile,
                  pl.BlockSpec((ATT_SUB * TQ, LANES), lambda h, i: (i, h + col_off)), tile],
        out_specs=[tile, full, full, btile, pl.BlockSpec((1, 8, LANES), lambda h, i: (h, 0, 0))],
        out_shape=[jax.ShapeDtypeStruct((t, hd), F32), jax.ShapeDtypeStruct((t + pad, hd), F32),
                   jax.ShapeDtypeStruct((t + pad, hd), F32), jax.ShapeDtypeStruct((N_HEADS, TQ, w), F32),
                   jax.ShapeDtypeStruct((nhp, 8, LANES), F32)],
        args=(sinks, q, kp, vp, bias, do, stats), sem=("parallel", "arbitrary"), comm=comm)


def _conv_apply(taps, w_ref, ktaps):
    out = taps[0] * w_ref[ktaps - 1:ktaps, :]
    for s in range(1, ktaps):
        out = out + taps[s] * w_ref[ktaps - 1 - s:ktaps - s, :]
    return out


def _sigmoid(x):
    return jax.nn.sigmoid(x)


def _silu_grad(x):
    sg = _sigmoid(x)
    return x * sg, sg * (1.0 + x * (1.0 - sg))


FFN_HALO = 16
FFN_BT = 256
FFN_BC = 1408


def _ffn_in_mid(x, g, wt, w8, b, name):
    t, d = x.shape
    f = D_FF
    tm = FFN_BT

    def body(x_ref, g_ref, b_ref, w_ref, cb_ref, gu_ref, h_ref, a_ref, gc_ref, halo_ref):
        @pl.when(pl.program_id(0) == 0)
        def _():
            halo_ref[...] = jnp.zeros_like(halo_ref)

        xv = x_ref[...]
        r = lax.rsqrt(jnp.mean(xv * xv, axis=-1, keepdims=True) + EPS)
        h = (xv * r * g_ref[...]).astype(BF16)
        h_ref[...] = h
        for c in range(0, f, FFN_BC):
            cs = slice(c, c + FFN_BC)
            gate = _dot(h, b_ref[c:c + FFN_BC, :], 1, 1).astype(BF16)
            up = _dot(h, b_ref[f + c:f + c + FFN_BC, :], 1, 1).astype(BF16)
            gu_ref[:, cs] = gate
            gu_ref[:, f + c:f + c + FFN_BC] = up
            gf = gate.astype(F32)
            ext = jnp.concatenate([halo_ref[:, cs], gf], axis=0)
            gc = (cb_ref[:, cs] + gf * w_ref[2:3, cs] + pltpu.roll(ext, 1, 0)[8:] * w_ref[1:2, cs]
                  + pltpu.roll(ext, 2, 0)[8:] * w_ref[0:1, cs])
            a_ref[:, cs] = (gc * _sigmoid(gc) * up.astype(F32)).astype(BF16)
            gc_ref[:, cs] = gc.astype(BF16)
            halo_ref[:, cs] = gf[tm - 8:]

    row = pl.BlockSpec((tm, d), lambda i: (i, 0))
    row_f = pl.BlockSpec((tm, f), lambda i: (i, 0))
    return pl.pallas_call(
        body, name=name, grid=(t // tm,),
        in_specs=[row, pl.BlockSpec((1, d), lambda i: (0, 0)), pl.BlockSpec((2 * f, d), lambda i: (0, 0)),
                  pl.BlockSpec((8, f), lambda i: (0, 0)), pl.BlockSpec((1, f), lambda i: (0, 0))],
        out_specs=[pl.BlockSpec((tm, 2 * f), lambda i: (i, 0)), row, row_f, row_f],
        out_shape=[jax.ShapeDtypeStruct((t, 2 * f), BF16), jax.ShapeDtypeStruct((t, d), BF16), jax.ShapeDtypeStruct((t, f), BF16),
                   jax.ShapeDtypeStruct((t, f), BF16)],
        scratch_shapes=[pltpu.VMEM((8, f), F32)], compiler_params=_cp(("arbitrary",)),
    )(x, g, wt, w8, b)


def _ffn_mid_bwd(gu, gc, dxb, w_out, w8, name):
    t, d = dxb.shape
    f = D_FF
    tm, hr = FFN_BT, FFN_HALO
    nt = t // tm
    n = tm + hr

    def body(g_ref, u_ref, un_ref, c_ref, cn_ref, dx_ref, dxn_ref, wo_ref, w_ref, dgu_ref, dw_ref, db_ref):
        i = pl.program_id(0)
        last = i == nt - 1

        @pl.when(i == 0)
        def _():
            dw_ref[...] = jnp.zeros_like(dw_ref)
            db_ref[...] = jnp.zeros_like(db_ref)

        dxe = jnp.concatenate([dx_ref[...], dxn_ref[...]], axis=0)
        row = lax.broadcasted_iota(jnp.int32, (n, 1), 0)
        keep = (row < tm) | jnp.logical_not(last)
        for c in range(0, f, FFN_BC):
            cs = slice(c, c + FFN_BC)
            act, dact = _silu_grad(jnp.concatenate([c_ref[:, cs], cn_ref[:, cs]], axis=0).astype(F32))
            da = _dot(dxe, wo_ref[cs, :], 1, 1)
            up = jnp.concatenate([u_ref[:, cs], un_ref[:, cs]], axis=0).astype(F32)
            dgc = jnp.where(keep, da * up * dact, 0.0)
            nxt = [dgc[:tm], pltpu.roll(dgc, n - 1, 0)[:tm], pltpu.roll(dgc, n - 2, 0)[:tm]]
            dgu_ref[:, f + c:f + c + FFN_BC] = (da[:tm] * act[:tm]).astype(BF16)
            dgu_ref[:, cs] = (nxt[0] * w_ref[2:3, cs] + nxt[1] * w_ref[1:2, cs] + nxt[2] * w_ref[0:1, cs]).astype(BF16)
            gate = g_ref[:, cs].astype(F32)
            db_ref[:, cs] += jnp.sum(nxt[0], axis=0, keepdims=True)
            for s in range(3):
                dw_ref[2 - s:3 - s, cs] += jnp.sum(nxt[s] * gate, axis=0, keepdims=True)

    r = tm // hr
    nxt_blk = lambda i: jnp.minimum((i + 1) * r, t // hr - 1)
    row_f = pl.BlockSpec((tm, f), lambda i: (i, 0))
    halo_f = pl.BlockSpec((hr, f), lambda i: (nxt_blk(i), 0))
    return pl.pallas_call(
        body, name=name, grid=(nt,),
        in_specs=[row_f, pl.BlockSpec((tm, f), lambda i: (i, 1)), pl.BlockSpec((hr, f), lambda i: (nxt_blk(i), 1)),
                  row_f, halo_f,
                  pl.BlockSpec((tm, d), lambda i: (i, 0)), pl.BlockSpec((hr, d), lambda i: (nxt_blk(i), 0)),
                  pl.BlockSpec((f, d), lambda i: (0, 0)), pl.BlockSpec((8, f), lambda i: (0, 0))],
        out_specs=[pl.BlockSpec((tm, 2 * f), lambda i: (i, 0)), pl.BlockSpec((8, f), lambda i: (0, 0)),
                   pl.BlockSpec((1, f), lambda i: (0, 0))],
        out_shape=[jax.ShapeDtypeStruct((t, 2 * f), BF16), jax.ShapeDtypeStruct((8, f), F32), jax.ShapeDtypeStruct((1, f), F32)],
        compiler_params=_cp(("arbitrary",)),
    )(gu, gu, gu, gc, gc, dxb, dxb, w_out, w8)


PRE_TM = 256
PRE_TC = 1024


def _ssm_in_pre(x, g, wt, w8, b, name):
    t, d = x.shape
    tm, tc = PRE_TM, PRE_TC

    def body(x_ref, g_ref, b_ref, w_ref, cb_ref, zx_ref, h_ref, o_ref, c_ref, halo_ref):
        @pl.when(pl.program_id(0) == 0)
        def _():
            halo_ref[...] = jnp.zeros_like(halo_ref)

        xv = x_ref[...]
        r = lax.rsqrt(jnp.mean(xv * xv, axis=-1, keepdims=True) + EPS)
        h = (xv * r * g_ref[...]).astype(BF16)
        h_ref[...] = h
        for c in range(0, ZX, tc):
            v = _dot(h, b_ref[c:c + tc, :], 1, 1)
            zx_ref[:, c:c + tc] = v
            if c >= D_INNER:
                cs = slice(c - D_INNER, c - D_INNER + tc)
                ext = jnp.concatenate([halo_ref[:, cs], v], axis=0)
                conv = cb_ref[:, cs] + v * w_ref[3:4, cs]
                for s in (1, 2, 3):
                    conv = conv + pltpu.roll(ext, s, 0)[8:] * w_ref[3 - s:4 - s, cs]
                o_ref[:, cs] = conv * _sigmoid(conv)
                c_ref[:, cs] = conv.astype(BF16)
                halo_ref[:, cs] = v[tm - 8:]

    row = pl.BlockSpec((tm, d), lambda i: (i, 0))
    row_x = pl.BlockSpec((tm, XBC), lambda i: (i, 0))
    return pl.pallas_call(
        body, name=name, grid=(t // tm,),
        in_specs=[row, pl.BlockSpec((1, d), lambda i: (0, 0)), pl.BlockSpec(wt.shape, lambda i: (0, 0)),
                  pl.BlockSpec((8, XBC), lambda i: (0, 0)), pl.BlockSpec((1, XBC), lambda i: (0, 0))],
        out_specs=[pl.BlockSpec((tm, ZX), lambda i: (i, 0)), row, row_x, row_x],
        out_shape=[jax.ShapeDtypeStruct((t, ZX), F32), jax.ShapeDtypeStruct((t, d), BF16), jax.ShapeDtypeStruct((t, XBC), F32),
                   jax.ShapeDtypeStruct((t, XBC), BF16)],
        scratch_shapes=[pltpu.VMEM((8, XBC), F32)], compiler_params=_cp(("arbitrary",)),
    )(x, g, wt, w8, b)


PRE_HALO = 16


def _ssm_pre_bwd(zx, conv, dxbc, w8, name):
    t = zx.shape[0]
    tm, tc, hr = PRE_TM, PRE_TC, PRE_HALO
    off = D_INNER // tc
    nt = t // tm
    n = tm + hr

    def body(x_ref, c_ref, cn_ref, d_ref, dn_ref, w_ref, o_ref, dw_ref, db_ref):
        i = pl.program_id(1)
        last = i == nt - 1

        @pl.when(i == 0)
        def _():
            dw_ref[...] = jnp.zeros_like(dw_ref)
            db_ref[...] = jnp.zeros_like(db_ref)

        _, dact = _silu_grad(jnp.concatenate([c_ref[...], cn_ref[...]], axis=0).astype(F32))
        row = lax.broadcasted_iota(jnp.int32, (n, 1), 0)
        dc = jnp.where((row < tm) | jnp.logical_not(last), jnp.concatenate([d_ref[...], dn_ref[...]], axis=0) * dact, 0.0)
        nxt = [dc[:tm]] + [pltpu.roll(dc, n - s, 0)[:tm] for s in (1, 2, 3)]
        o_ref[...] = _conv_apply(nxt, w_ref, 4).astype(BF16)
        xv = x_ref[...]
        db_ref[...] += jnp.sum(nxt[0], axis=0, keepdims=True)
        for s in range(4):
            dw_ref[3 - s:4 - s, :] += jnp.sum(nxt[s] * xv, axis=0, keepdims=True)

    nxt_blk = lambda i: jnp.minimum((i + 1) * (tm // hr), t // hr - 1)
    tile = pl.BlockSpec((tm, tc), lambda j, i: (i, j))
    halo = pl.BlockSpec((hr, tc), lambda j, i: (nxt_blk(i), j))
    return pl.pallas_call(
        body, name=name, grid=(XBC // tc, nt),
        in_specs=[pl.BlockSpec((tm, tc), lambda j, i: (i, j + off)), tile, halo, tile, halo,
                  pl.BlockSpec((8, tc), lambda j, i: (0, j))],
        out_specs=[tile, pl.BlockSpec((8, tc), lambda j, i: (0, j)), pl.BlockSpec((1, tc), lambda j, i: (0, j))],
        out_shape=[jax.ShapeDtypeStruct((t, XBC), BF16), jax.ShapeDtypeStruct((8, XBC), F32),
                   jax.ShapeDtypeStruct((1, XBC), F32)],
        compiler_params=_cp(("parallel", "arbitrary")),
    )(zx, conv, conv, dxbc, dxbc, w8)


def _head_lanes():
    return lax.broadcasted_iota(jnp.int32, (1, LANES), 1) < SSM_HEADS


def _dt_fwd(dtraw, bias, name):
    t = dtraw.shape[0]
    tm = _pick(t, (1024, 512, 256, 128))

    def body(x_ref, b_ref, o_ref):
        v = x_ref[...] + b_ref[...]
        sp = jnp.maximum(v, 0.0) + jnp.log(1.0 + jnp.exp(-jnp.abs(v)))
        o_ref[...] = jnp.where(_head_lanes(), sp, 0.0)

    row = pl.BlockSpec((tm, LANES), lambda i: (i, 0))
    return pl.pallas_call(
        body, name=name, grid=(t // tm,), in_specs=[row, pl.BlockSpec((1, LANES), lambda i: (0, 0))], out_specs=row,
        out_shape=jax.ShapeDtypeStruct((t, LANES), F32), compiler_params=_cp(("parallel",)),
    )(dtraw, bias)


def _dt_bwd(dtraw, bias, ddt, name):
    t = dtraw.shape[0]
    tm = _pick(t, (1024, 512, 256, 128))

    def body(x_ref, b_ref, d_ref, o_ref, db_ref):
        @pl.when(pl.program_id(0) == 0)
        def _():
            db_ref[...] = jnp.zeros_like(db_ref)

        g = jnp.where(_head_lanes(), d_ref[...] * _sigmoid(x_ref[...] + b_ref[...]), 0.0)
        o_ref[...] = g.astype(BF16)
        db_ref[...] += jnp.sum(g, axis=0, keepdims=True)

    row = pl.BlockSpec((tm, LANES), lambda i: (i, 0))
    vec = pl.BlockSpec((1, LANES), lambda i: (0, 0))
    return pl.pallas_call(
        body, name=name, grid=(t // tm,), in_specs=[row, vec, row], out_specs=[row, vec],
        out_shape=[jax.ShapeDtypeStruct((t, LANES), BF16), jax.ShapeDtypeStruct((1, LANES), F32)],
        compiler_params=_cp(("arbitrary",)),
    )(dtraw, bias, ddt)


GROUP_W = D_INNER // SSM_GROUPS


def _ssd_common(dt, alog):
    ll = dt.shape[0]
    a_neg = -jnp.exp(alog)
    a = dt * a_neg
    ri = lax.broadcasted_iota(jnp.int32, (ll, ll), 0)
    ci = lax.broadcasted_iota(jnp.int32, (ll, ll), 1)
    tril = ri >= ci
    acs = _dot(tril.astype(F32), a, 1, 0, HI)
    return a_neg, tril, acs, acs.T


def _pair_terms(acs, acs_t, dt, h0, lo):
    ll = acs.shape[0]
    cols = [acs[:, h0 + e:h0 + e + 1] for e in range(2)]
    rows = [acs_t[h0 + e:h0 + e + 1, :] for e in range(2)]
    dtc = [dt[:, h0 + e:h0 + e + 1] for e in range(2)]
    lasts = [c[ll - 1:ll, :] for c in cols]
    dtx = jnp.where(lo, dtc[0], dtc[1])
    eac = jnp.where(lo, jnp.exp(cols[0]), jnp.exp(cols[1]))
    fdec = jnp.where(lo, jnp.exp(lasts[0] - cols[0]), jnp.exp(lasts[1] - cols[1]))
    elast = jnp.where(lo, jnp.exp(lasts[0]), jnp.exp(lasts[1]))
    return cols, rows, dtx, eac, fdec, elast


def _decay(col, row, tril):
    return jnp.where(tril, jnp.exp(jnp.minimum(col - row, 0.0)), 0.0)


def _two_heads_rows(v, lo):
    z = jnp.zeros_like(v)
    return jnp.concatenate([jnp.where(lo, v, z), jnp.where(lo, z, v)], axis=0)


def _two_heads_cols(ms):
    return jnp.concatenate(ms, axis=1)


def _z_group(z_refs, g):
    return z_refs[g // 2][:, GROUP_W * (g % 2):GROUP_W * (g % 2 + 1)]


def _ssd_fwd(xbc, dt, alog, zx, dexp, nw, name, comm=None):
    t = xbc.shape[0]
    ll = SSD_L
    nc = t // ll

    def body(x_ref, dt_ref, al_ref, z0_ref, z1_ref, d_ref, w_ref, y_ref, sp_ref, y4_ref, st_ref):
        @pl.when(pl.program_id(0) == 0)
        def _():
            st_ref[...] = jnp.zeros_like(st_ref)

        dtv = dt_ref[...]
        _, tril, acs, acs_t = _ssd_common(dtv, al_ref[...])
        lo = _lo_mask()
        sp_ref[0] = st_ref[...]
        for g in range(SSM_GROUPS):
            bg = x_ref[:, D_INNER + SSM_STATE * g:D_INNER + SSM_STATE * (g + 1)].astype(BF16)
            cg = x_ref[:, D_INNER + 512 + SSM_STATE * g:D_INNER + 512 + SSM_STATE * (g + 1)].astype(BF16)
            gm = _dot(cg, bg, 1, 1)
            g0 = GROUP_W * g
            terms = [_pair_terms(acs, acs_t, dtv, 8 * g + 2 * pp, lo) for pp in range(4)]
            dtx, eac, fdec, elast = [jnp.concatenate([tt[k] for tt in terms], axis=1) for k in (2, 3, 4, 5)]
            xg = x_ref[:, g0:g0 + GROUP_W]
            ug = (xg * dtx).astype(BF16)
            sg = st_ref[:, g0:g0 + GROUP_W]
            yst = _dot(cg, sg.astype(BF16), 1, 0) * eac
            st_ref[:, g0:g0 + GROUP_W] = sg * elast + _dot(bg, (xg * (fdec * dtx)).astype(BF16), 0, 0)
            ys = []
            for pp in range(4):
                cols, rows = terms[pp][0], terms[pp][1]
                sl = slice(LANES * pp, LANES * (pp + 1))
                y_in = _dot(_two_heads_cols([(gm * _decay(cols[e], rows[e], tril)).astype(BF16) for e in range(2)]),
                            _two_heads_rows(ug[:, sl], lo), 1, 0)
                ys.append(y_in + yst[:, sl])
            yg = jnp.concatenate(ys, axis=1)
            y_ref[:, g0:g0 + GROUP_W] = yg
            zg = _z_group((z0_ref, z1_ref), g)
            y3 = (yg + d_ref[:, g0:g0 + GROUP_W] * xg) * (zg * _sigmoid(zg))
            r = lax.rsqrt(jnp.mean(y3 * y3, axis=-1, keepdims=True) + EPS)
            y4_ref[:, g0:g0 + GROUP_W] = (y3 * r * w_ref[:, g0:g0 + GROUP_W]).astype(BF16)

    zblk = lambda j: pl.BlockSpec((ll, 1024), lambda c: (c, j))
    vec = pl.BlockSpec((1, D_INNER), lambda c: (0, 0))
    row = pl.BlockSpec((ll, D_INNER), lambda c: (c, 0))
    return _call(
        body, name=name, grid=(nc,),
        in_specs=[pl.BlockSpec((ll, XBC), lambda c: (c, 0)), pl.BlockSpec((ll, LANES), lambda c: (c, 0)),
                  pl.BlockSpec((1, LANES), lambda c: (0, 0)), zblk(0), zblk(1), vec, vec],
        out_specs=[row, pl.BlockSpec((1, SSM_STATE, D_INNER), lambda c: (c, 0, 0)), row],
        out_shape=[jax.ShapeDtypeStruct((t, D_INNER), F32), jax.ShapeDtypeStruct((nc, SSM_STATE, D_INNER), F32),
                   jax.ShapeDtypeStruct((t, D_INNER), BF16)],
        scratch_shapes=[pltpu.VMEM((SSM_STATE, D_INNER), F32)],
        args=(xbc, dt, alog, zx, zx, dexp, nw), sem=("arbitrary",), comm=comm)


def _ssd_bwd(xbc, dt, alog, sprev, dy4, y, zx, dexp, nw, name, comm=None):
    t = xbc.shape[0]
    ll = SSD_L
    nc = t // ll

    def body(x_ref, dt_ref, al_ref, sp_ref, g4_ref, y_ref, z0_ref, z1_ref, d_ref, w_ref,
             dx_ref, ddt_ref, dal_ref, dz_ref, dd_ref, dnw_ref, ds_ref, colt_ref):
        @pl.when(pl.program_id(0) == 0)
        def _():
            ds_ref[...] = jnp.zeros_like(ds_ref)
            dal_ref[...] = jnp.zeros_like(dal_ref)
            dd_ref[...] = jnp.zeros_like(dd_ref)
            dnw_ref[...] = jnp.zeros_like(dnw_ref)

        dtv = dt_ref[...]
        a_neg, tril, acs, acs_t = _ssd_common(dtv, al_ref[...])
        lo = _lo_mask()
        hi = jnp.logical_not(lo)
        lane = lax.broadcasted_iota(jnp.int32, (1, LANES), 1)
        colt_ref[...] = jnp.zeros_like(colt_ref)
        rowterm = jnp.zeros((ll, LANES), F32)
        ddt_u = jnp.zeros((ll, LANES), F32)
        dlast = jnp.zeros((1, LANES), F32)

        def halves(v):
            return (jnp.sum(jnp.where(lo, v, 0.0), axis=-1, keepdims=True),
                    jnp.sum(jnp.where(hi, v, 0.0), axis=-1, keepdims=True))

        for g in range(SSM_GROUPS):
            cb0 = D_INNER + SSM_STATE * g
            cc0 = D_INNER + 512 + SSM_STATE * g
            bg = x_ref[:, cb0:cb0 + SSM_STATE].astype(BF16)
            cg = x_ref[:, cc0:cc0 + SSM_STATE].astype(BF16)
            gm = _dot(cg, bg, 1, 1)
            g0 = GROUP_W * g
            terms = [_pair_terms(acs, acs_t, dtv, 8 * g + 2 * pp, lo) for pp in range(4)]
            dtx, eac, fdec, elast = [jnp.concatenate([tt[k] for tt in terms], axis=1) for k in (2, 3, 4, 5)]
            xg = x_ref[:, g0:g0 + GROUP_W]
            u32 = xg * dtx
            ug = u32.astype(BF16)
            zg = _z_group((z0_ref, z1_ref), g)
            dg = d_ref[:, g0:g0 + GROUP_W]
            act, dact = _silu_grad(zg)
            y2 = y_ref[:, g0:g0 + GROUP_W] + dg * xg
            y3 = y2 * act
            rn = lax.rsqrt(jnp.mean(y3 * y3, axis=-1, keepdims=True) + EPS)
            y3n = y3 * rn
            gv = g4_ref[:, g0:g0 + GROUP_W]
            dyn = gv * w_ref[:, g0:g0 + GROUP_W]
            dy3 = rn * (dyn - y3n * jnp.mean(dyn * y3n, axis=-1, keepdims=True))
            dyg = dy3 * act
            dskip = dyg * dg
            dz_ref[:, g0:g0 + GROUP_W] = (dy3 * y2 * dact).astype(BF16)
            dd_ref[:, g0:g0 + GROUP_W] += jnp.sum(dyg * xg, axis=0, keepdims=True)
            dnw_ref[:, g0:g0 + GROUP_W] += jnp.sum(gv * y3n, axis=0, keepdims=True)
            dyb = dyg.astype(BF16)
            spg = sp_ref[0, :, g0:g0 + GROUP_W]
            spb = spg.astype(BF16)
            dsg = ds_ref[:, g0:g0 + GROUP_W]
            dsb = dsg.astype(BF16)
            du_st = _dot(bg, dsb, 1, 0) * fdec
            yst = _dot(cg, spb, 1, 0) * eac
            dye = (dyg * eac).astype(BF16)
            dc_st = _dot(dye, spb, 1, 1)
            db_st = _dot((xg * (fdec * dtx)).astype(BF16), dsb, 1, 1)
            ds_ref[:, g0:g0 + GROUP_W] = dsg * elast + _dot(cg, dye, 0, 0)
            qst_el = du_st * u32
            rq_el = dyg * yst - qst_el
            q_row = jnp.sum(qst_el, axis=0, keepdims=True)
            s_row = jnp.sum(dsg * spg, axis=0, keepdims=True)
            dgm = jnp.zeros((ll, ll), F32)
            for pp in range(4):
                h0 = 8 * g + 2 * pp
                cols, rows = terms[pp][0], terms[pp][1]
                sl = slice(LANES * pp, LANES * (pp + 1))
                decs = [_decay(cols[e], rows[e], tril) for e in range(2)]
                wms = [gm * d for d in decs]
                dum2 = _dot(dyb[:, sl], _two_heads_rows(ug[:, sl], lo), 1, 1)
                du = _dot(jnp.concatenate([wm.astype(BF16) for wm in wms], axis=0),
                          _two_heads_rows(dyb[:, sl], lo), 0, 0) + du_st[:, sl]
                dx_ref[:, g0 + LANES * pp:g0 + LANES * (pp + 1)] = du * dtx[:, sl] + dskip[:, sl]
                ddtu = halves(du * xg[:, sl])
                rq = halves(rq_el[:, sl])
                qs = halves(q_row[:, sl])
                ss = halves(s_row[:, sl])
                for e in range(2):
                    dum = dum2[:, ll * e:ll * (e + 1)]
                    dgm = dgm + dum * decs[e]
                    tm_ = dum * wms[e]
                    oh = lane == (h0 + e)
                    rowterm = rowterm + jnp.where(oh, jnp.sum(tm_, axis=1, keepdims=True) + rq[e], 0.0)
                    ddt_u = ddt_u + jnp.where(oh, ddtu[e], 0.0)
                    dlast = dlast + jnp.where(oh, jnp.exp(cols[e][ll - 1:ll, :]) * ss[e] + qs[e], 0.0)
                    colt_ref[h0 + e:h0 + e + 1, :] = jnp.sum(tm_, axis=0, keepdims=True)
            dgb = dgm.astype(BF16)
            dx_ref[:, cc0:cc0 + SSM_STATE] = _dot(dgb, bg, 1, 0) + dc_st
            dx_ref[:, cb0:cb0 + SSM_STATE] = _dot(dgb, cg, 0, 0) + db_st
        row_io = lax.broadcasted_iota(jnp.int32, (ll, LANES), 0)
        dacs = rowterm - colt_ref[...].T + jnp.where(row_io == ll - 1, dlast, 0.0)
        da = _dot(jnp.logical_not(tril).astype(F32) + jnp.where(
            lax.broadcasted_iota(jnp.int32, (ll, ll), 0) == lax.broadcasted_iota(jnp.int32, (ll, ll), 1), 1.0, 0.0),
            dacs, 1, 0, HI)
        ddt_ref[...] = da * a_neg + ddt_u
        dal_ref[...] += jnp.sum(da * dtv, axis=0, keepdims=True) * a_neg

    rev = lambda c: nc - 1 - c
    row = pl.BlockSpec((ll, D_INNER), lambda c: (rev(c), 0))
    vec = pl.BlockSpec((1, D_INNER), lambda c: (0, 0))
    zblk = lambda j: pl.BlockSpec((ll, 1024), lambda c: (rev(c), j))
    return _call(
        body, name=name, grid=(nc,),
        in_specs=[pl.BlockSpec((ll, XBC), lambda c: (rev(c), 0)), pl.BlockSpec((ll, LANES), lambda c: (rev(c), 0)),
                  pl.BlockSpec((1, LANES), lambda c: (0, 0)),
                  pl.BlockSpec((1, SSM_STATE, D_INNER), lambda c: (rev(c), 0, 0)), row, row, zblk(0), zblk(1), vec, vec],
        out_specs=[pl.BlockSpec((ll, XBC), lambda c: (rev(c), 0)), pl.BlockSpec((ll, LANES), lambda c: (rev(c), 0)),
                   pl.BlockSpec((1, LANES), lambda c: (0, 0)), row, vec, vec],
        out_shape=[jax.ShapeDtypeStruct((t, XBC), F32), jax.ShapeDtypeStruct((t, LANES), F32),
                   jax.ShapeDtypeStruct((1, LANES), F32), jax.ShapeDtypeStruct((t, D_INNER), BF16),
                   jax.ShapeDtypeStruct((1, D_INNER), F32), jax.ShapeDtypeStruct((1, D_INNER), F32)],
        scratch_shapes=[pltpu.VMEM((SSM_STATE, D_INNER), F32), pltpu.VMEM((LANES, ll), F32)],
        args=(xbc, dt, alog, sprev, dy4, y, zx, zx, dexp, nw), sem=("arbitrary",), comm=comm)


def _sum_parts(parts, name):
    nparts, r, c = parts.shape
    tc = _pick(c, (256, 128))

    def body(p_ref, o_ref):
        g = p_ref[0].astype(F32)
        for k in range(1, nparts):
            g = g + p_ref[k].astype(F32)
        o_ref[...] = g

    return pl.pallas_call(
        body, name=name, grid=(c // tc,), in_specs=[pl.BlockSpec((nparts, r, tc), lambda j: (0, 0, j))],
        out_specs=pl.BlockSpec((r, tc), lambda j: (0, j)), out_shape=jax.ShapeDtypeStruct((r, c), F32),
        compiler_params=_cp(("parallel",)),
    )(parts)


def _adamw(parts, w, m, v, name):
    nl, r, c = w.shape
    assert len(parts) == nl
    tr = _pick(r, (256, 128, 64))
    c1 = 1.0 - ADAM_B1 ** ADAM_STEP
    c2 = 1.0 - ADAM_B2 ** ADAM_STEP

    def body(*refs):
        p_refs = refs[:nl]
        w_ref, m_ref, v_ref, g_ref, d_ref, mo_ref, vo_ref = refs[nl:]
        g = None
        for l, p_ref in enumerate(p_refs):
            s = p_ref[0].astype(F32)
            for k in range(1, p_ref.shape[0]):
                s = s + p_ref[k].astype(F32)
            g = s if g is None else jnp.where(pl.program_id(0) == l, s, g)
        mn = ADAM_B1 * m_ref[0] + (1.0 - ADAM_B1) * g
        vn = ADAM_B2 * v_ref[0] + (1.0 - ADAM_B2) * (g * g)
        g_ref[0] = g
        mo_ref[0] = mn
        vo_ref[0] = vn
        d_ref[0] = -ADAM_LR * ((mn / c1) / (jnp.sqrt(vn / c2) + ADAM_EPS) + ADAM_WD * w_ref[0])

    row = pl.BlockSpec((1, tr, c), lambda l, i: (l, i, 0))
    sd = jax.ShapeDtypeStruct((nl, r, c), F32)
    return pl.pallas_call(
        body, name=name, grid=(nl, r // tr),
        in_specs=[pl.BlockSpec((p.shape[0], tr, c), lambda l, i: (0, i, 0)) for p in parts] + [row, row, row],
        out_specs=[row, row, row, row], out_shape=[sd, sd, sd, sd], compiler_params=_cp(("parallel", "parallel")),
    )(*parts, w, m, v)


def _peers():
    mx, my, mc = lax.axis_index("x"), lax.axis_index("y"), lax.axis_index("c")
    me = 4 * mx + 2 * my + mc
    out = []
    for k in range(1, N_DEV):
        px = 1 - mx if k & 4 else mx
        py = 1 - my if k & 2 else my
        pc = 1 - mc if k & 1 else mc
        out.append(((px, py, pc), 4 * px + 2 * py + pc))
    return me, out


class _Comm:
    def __init__(self, arrs, scatters):
        self.arrs, self.scatters, self.n = list(arrs), list(scatters), len(arrs)
        self.specs = [pl.BlockSpec(memory_space=pl.ANY)] * self.n
        self.out_shape = [jax.ShapeDtypeStruct(x.shape if sc else (N_DEV,) + x.shape, x.dtype)
                          for x, sc in zip(self.arrs, self.scatters)]
        np_ = N_DEV - 1
        self.scratch = [pltpu.SemaphoreType.DMA((np_ * self.n,)), pltpu.SemaphoreType.DMA((np_ * self.n,)),
                        pltpu.SemaphoreType.DMA((self.n,))]

    def _copies(self, x_refs, o_refs, sems):
        send_sems, recv_sems, local_sems = sems
        me, peers = _peers()
        np_ = N_DEV - 1
        local, sends, recvs = [], [], []
        for a in range(self.n):
            mine = x_refs[a].at[me] if self.scatters[a] else x_refs[a]
            local.append(pltpu.make_async_copy(mine, o_refs[a].at[me], local_sems.at[a]))
        for k, (dev, idx) in enumerate(peers):
            for a in range(self.n):
                mine = x_refs[a].at[me] if self.scatters[a] else x_refs[a]
                sends.append(pltpu.make_async_remote_copy(
                    src_ref=x_refs[a].at[idx] if self.scatters[a] else x_refs[a], dst_ref=o_refs[a].at[me],
                    send_sem=send_sems.at[a * np_ + k], recv_sem=recv_sems.at[a * np_ + k], device_id=dev, device_id_type=MESH))
                recvs.append(pltpu.make_async_remote_copy(
                    src_ref=mine, dst_ref=o_refs[a].at[idx], send_sem=send_sems.at[a * np_ + k],
                    recv_sem=recv_sems.at[a * np_ + k], device_id=dev, device_id_type=MESH))
        return local, sends, recvs

    def start(self, x_refs, o_refs, sems):
        local, sends, _ = self._copies(x_refs, o_refs, sems)
        for cp in local + sends:
            cp.start()

    def wait(self, x_refs, o_refs, sems):
        local, sends, recvs = self._copies(x_refs, o_refs, sems)
        for cp in recvs:
            cp.wait_recv()
        for cp in sends:
            cp.wait_send()
        for cp in local:
            cp.wait()


class _Gather2(_Comm):
    def __init__(self, arrs):
        super().__init__(arrs, [False] * len(arrs))

    def _plan(self, x_refs, o_refs, sems):
        send_sems, recv_sems, local_sems = sems
        mx, my, mc = lax.axis_index("x"), lax.axis_index("y"), lax.axis_index("c")
        slot = lambda px, py, pc: 4 * px + 2 * py + pc
        sib = (mx, my, 1 - mc)
        chips = [(1 - mx, my), (mx, 1 - my), (1 - mx, 1 - my)]
        np_ = N_DEV - 1
        local, first, passed, arrive_first, arrive_rest = [], [], [], [], []

        def copy(a, k, src, block, to):
            return pltpu.make_async_remote_copy(
                src_ref=src, dst_ref=o_refs[a].at[block], send_sem=send_sems.at[a * np_ + k], recv_sem=recv_sems.at[a * np_ + k],
                device_id=to, device_id_type=MESH)

        for a in range(self.n):
            me = slot(mx, my, mc)
            local.append(pltpu.make_async_copy(x_refs[a], o_refs[a].at[me], local_sems.at[a]))
            first.append(copy(a, 0, x_refs[a], me, sib))
            arrive_rest.append(copy(a, 0, x_refs[a], slot(*sib), sib))
            for j, (cx, cy) in enumerate(chips):
                first.append(copy(a, 1 + j, x_refs[a], me, (cx, cy, mc)))
                arrive_first.append(copy(a, 1 + j, x_refs[a], slot(cx, cy, mc), (cx, cy, mc)))
                passed.append(copy(a, 4 + j, o_refs[a].at[slot(cx, cy, mc)], slot(cx, cy, mc), sib))
                arrive_rest.append(copy(a, 4 + j, x_refs[a], slot(cx, cy, 1 - mc), sib))
        return local, first, passed, arrive_first, arrive_rest

    def start(self, x_refs, o_refs, sems):
        local, first, _, _, _ = self._plan(x_refs, o_refs, sems)
        for cp in local + first:
            cp.start()

    def wait(self, x_refs, o_refs, sems):
        local, first, passed, arrive_first, arrive_rest = self._plan(x_refs, o_refs, sems)
        for arrived, onward in zip(arrive_first, passed):
            arrived.wait_recv()
            onward.start()
        for cp in arrive_rest:
            cp.wait_recv()
        for cp in first + passed:
            cp.wait_send()
        for cp in local:
            cp.wait()


def _call(body, *, name, grid, in_specs, out_specs, out_shape, args, scratch_shapes=(), sem=None, comm=None):
    if comm is None:
        outs = pl.pallas_call(
            body, name=name, grid=grid, in_specs=list(in_specs), out_specs=list(out_specs), out_shape=list(out_shape),
            scratch_shapes=list(scratch_shapes), compiler_params=_cp(sem),
        )(*args)
        return list(outs), []
    n_in, n_out, nc = len(in_specs), len(out_specs), comm.n
    nsteps = 1
    for g in grid:
        nsteps *= g

    def carrier(*refs):
        ins, cin = refs[:n_in], refs[n_in:n_in + nc]
        outs, cout = refs[n_in + nc:n_in + nc + n_out], refs[n_in + nc + n_out:n_in + 2 * nc + n_out]
        rest = refs[n_in + 2 * nc + n_out:]
        scratch, sems = rest[:len(rest) - 3], rest[len(rest) - 3:]
        if nsteps == 1:
            comm.start(cin, cout, sems)
            body(*ins, *outs, *scratch)
            comm.wait(cin, cout, sems)
            return
        step = 0
        for d, g in enumerate(grid):
            step = step * g + pl.program_id(d)

        @pl.when(step == 0)
        def _():
            comm.start(cin, cout, sems)

        body(*ins, *outs, *scratch)

        @pl.when(step == nsteps - 1)
        def _():
            comm.wait(cin, cout, sems)

    outs = pl.pallas_call(
        carrier, name=name, grid=grid, in_specs=list(in_specs) + comm.specs, out_specs=list(out_specs) + comm.specs,
        out_shape=list(out_shape) + comm.out_shape, scratch_shapes=list(scratch_shapes) + comm.scratch,
        compiler_params=_cp(("arbitrary",) * len(grid) if grid else None),
    )(*args, *comm.arrs)
    return list(outs[:n_out]), list(outs[n_out:])


def _exchange(comm, name):
    return _call(lambda *refs: None, name=name, grid=(), in_specs=[], out_specs=[], out_shape=[], args=[], comm=comm)[1]


def _pack(arrs, dtype, lead=()):
    nl = len(lead)
    flat = jnp.concatenate([a.astype(dtype).reshape(lead + (-1,)) for a in arrs], axis=nl)
    n = flat.shape[-1]
    rows = -(-n // (LANES * 8)) * 8
    flat = jnp.pad(flat, [(0, 0)] * nl + [(0, rows * LANES - n)])
    return flat.reshape(lead + (rows, LANES))


def _unpack(flat, shapes, lead=()):
    nl = len(lead)
    flat = flat.reshape(lead + (-1,))
    out, o = [], 0
    for s in shapes:
        n = 1
        for d in s:
            n *= d
        out.append(lax.slice_in_dim(flat, o, o + n, axis=nl).reshape(lead + tuple(s)))
        o += n
    return out


def _join(g, ax):
    return jnp.concatenate([g[d] for d in range(N_DEV)], axis=ax)


def _split(full, ax):
    n = full.shape[ax] // N_DEV
    return jnp.stack([lax.slice_in_dim(full, d * n, (d + 1) * n, axis=ax) for d in range(N_DEV)])


_WEIGHTS = ['norm_mix', 'norm_ffn', 'attn_w_in', 'attn_w_out', 'relpos_table', 'q_norm_a', 'k_norm_a', 'q_norm_b',
            'k_norm_b', 'sinks', 'ssm_w_in', 'ssm_conv_w', 'ssm_conv_b', 'ssm_dt_bias', 'ssm_a_log', 'ssm_d', 'ssm_norm',
            'ssm_w_out', 'ffn_w_in', 'ffn_conv_w', 'ffn_conv_b', 'ffn_w_out']
_SHARD_AX = {'attn_w_in': 2, 'attn_w_out': 1, 'ssm_w_in': 2, 'ssm_conv_w': 2, 'ssm_conv_b': 1, 'ssm_norm': 1,
             'ssm_w_out': 1, 'ffn_w_in': 2, 'ffn_conv_w': 2, 'ffn_w_out': 1}
_BIG = ['attn_w_in', 'attn_w_out', 'ssm_w_in', 'ssm_w_out', 'ffn_w_in', 'ffn_w_out']
_SMALL = ['ssm_conv_w', 'ssm_conv_b', 'ssm_norm', 'ffn_conv_w']
_AX2 = {n: _SHARD_AX[n] - 1 for n in _BIG}
_REPL = [n for n in _WEIGHTS if n not in _SHARD_AX]


def _rows8(w):
    return jnp.pad(w, ((0, 8 - w.shape[0]), (0, 0)))


def _lanes128(v):
    return jnp.pad(v, (0, LANES - v.shape[0])).reshape(1, LANES)


def _band_mask(n_prev, pad):
    cq = jnp.arange(TQ)[:, None] // CHUNK
    ck = jnp.arange(pad + TQ)[None, :] // CHUNK
    return (ck >= cq) & (ck <= cq + n_prev)


def _ffn_fwd(xin, g, w_in_t, w8, cb, tag):
    gu, h, a, gc = _ffn_in_mid(xin, g, w_in_t, w8, cb, f"mm_ffn_in{tag}")
    return a, (h, gu, a, gc)


def _ffn_bwd(dx, dxb, xin, g, w_in_t, w8, w_out, saved, tag):
    h, gu, a, gc = saved
    dw_out = _mm_tn(a, dxb, f"mm_ffn_dwout{tag}")
    dgu, dw8, dcb = _ffn_mid_bwd(gu, gc, dxb, w_out, w8, f"ffn_mid_bwd{tag}")
    dw_in_t = _mm_tn(dgu, h, f"mm_ffn_dwin{tag}")
    dxp, dxpb, dg = _mm_rms_bwd(dgu, w_in_t, 0, None, xin, g, dx, f"mm_ffn_dh{tag}")
    return dxp, dxpb, dg, dw_in_t, dw8[:3], dcb, dw_out


def kernel(x, norm_mix, norm_ffn, attn_w_in, attn_w_out, relpos_table, q_norm_a, k_norm_a, q_norm_b, k_norm_b, sinks, ssm_w_in, ssm_conv_w, ssm_conv_b, ssm_dt_bias, ssm_a_log, ssm_d, ssm_norm, ssm_w_out, ffn_w_in, ffn_conv_w, ffn_conv_b, ffn_w_out, loss_target, m_norm_mix, m_norm_ffn, m_attn_w_in, m_attn_w_out, m_relpos_table, m_q_norm_a, m_k_norm_a, m_q_norm_b, m_k_norm_b, m_sinks, m_ssm_w_in, m_ssm_conv_w, m_ssm_conv_b, m_ssm_dt_bias, m_ssm_a_log, m_ssm_d, m_ssm_norm, m_ssm_w_out, m_ffn_w_in, m_ffn_conv_w, m_ffn_conv_b, m_ffn_w_out, v_norm_mix, v_norm_ffn, v_attn_w_in, v_attn_w_out, v_relpos_table, v_q_norm_a, v_k_norm_a, v_q_norm_b, v_k_norm_b, v_sinks, v_ssm_w_in, v_ssm_conv_w, v_ssm_conv_b, v_ssm_dt_bias, v_ssm_a_log, v_ssm_d, v_ssm_norm, v_ssm_w_out, v_ffn_w_in, v_ffn_conv_w, v_ffn_conv_b, v_ffn_w_out):
    w = dict(norm_mix=norm_mix, norm_ffn=norm_ffn, attn_w_in=attn_w_in, attn_w_out=attn_w_out, relpos_table=relpos_table,
             q_norm_a=q_norm_a, k_norm_a=k_norm_a, q_norm_b=q_norm_b, k_norm_b=k_norm_b, sinks=sinks, ssm_w_in=ssm_w_in,
             ssm_conv_w=ssm_conv_w, ssm_conv_b=ssm_conv_b, ssm_dt_bias=ssm_dt_bias, ssm_a_log=ssm_a_log, ssm_d=ssm_d,
             ssm_norm=ssm_norm, ssm_w_out=ssm_w_out, ffn_w_in=ffn_w_in, ffn_conv_w=ffn_conv_w, ffn_conv_b=ffn_conv_b,
             ffn_w_out=ffn_w_out)
    mom = dict(norm_mix=m_norm_mix, norm_ffn=m_norm_ffn, attn_w_in=m_attn_w_in, attn_w_out=m_attn_w_out,
               relpos_table=m_relpos_table, q_norm_a=m_q_norm_a, k_norm_a=m_k_norm_a, q_norm_b=m_q_norm_b,
               k_norm_b=m_k_norm_b, sinks=m_sinks, ssm_w_in=m_ssm_w_in, ssm_conv_w=m_ssm_conv_w, ssm_conv_b=m_ssm_conv_b,
               ssm_dt_bias=m_ssm_dt_bias, ssm_a_log=m_ssm_a_log, ssm_d=m_ssm_d, ssm_norm=m_ssm_norm, ssm_w_out=m_ssm_w_out,
               ffn_w_in=m_ffn_w_in, ffn_conv_w=m_ffn_conv_w, ffn_conv_b=m_ffn_conv_b, ffn_w_out=m_ffn_w_out)
    var = dict(norm_mix=v_norm_mix, norm_ffn=v_norm_ffn, attn_w_in=v_attn_w_in, attn_w_out=v_attn_w_out,
               relpos_table=v_relpos_table, q_norm_a=v_q_norm_a, k_norm_a=v_k_norm_a, q_norm_b=v_q_norm_b,
               k_norm_b=v_k_norm_b, sinks=v_sinks, ssm_w_in=v_ssm_w_in, ssm_conv_w=v_ssm_conv_w, ssm_conv_b=v_ssm_conv_b,
               ssm_dt_bias=v_ssm_dt_bias, ssm_a_log=v_ssm_a_log, ssm_d=v_ssm_d, ssm_norm=v_ssm_norm, ssm_w_out=v_ssm_w_out,
               ffn_w_in=v_ffn_w_in, ffn_conv_w=v_ffn_conv_w, ffn_conv_b=v_ffn_conv_b, ffn_w_out=v_ffn_w_out)

    def piece(n, l):
        return (w[n][l].T if _AX2[n] == 1 else w[n][l]).astype(BF16)

    def gather_of(names_layers):
        return _Gather2([piece(n, l) for n, l in names_layers])

    def joined(got):
        return [g.reshape(-1, D_MODEL) for g in got]

    first = [('attn_w_in', 0), ('attn_w_out', 0)]
    got = _exchange(_Gather2([piece(n, l) for n, l in first] + [_pack([w[n] for n in _SMALL], F32)]), "gather_attn")
    wt_attn_in, w_attn_out = joined(got[:2])
    full = {}
    for n, g in zip(_SMALL, _unpack(got[2], [w[n].shape for n in _SMALL], lead=(N_DEV,))):
        full[n] = _join(g, _SHARD_AX[n])
    ssm_cw8 = _rows8(full['ssm_conv_w'][0])
    ssm_cb = full['ssm_conv_b']
    ssm_nw = full['ssm_norm']
    ffn_cw8 = [_rows8(full['ffn_conv_w'][l]) for l in range(2)]
    ffn_cb = [ffn_conv_b[l:l + 1] for l in range(2)]

    x0 = x[0]
    target = loss_target[0]
    t = x0.shape[0]

    g_mix0, g_mix1 = norm_mix[0:1], norm_mix[1:2]
    g_ffn0, g_ffn1 = norm_ffn[0:1], norm_ffn[1:2]
    proj, h0 = _rms_mm(x0, g_mix0, wt_attn_in, 2304, "mm_attn_in", F32)
    hn_w = jnp.concatenate([jnp.tile(v, (1, 2)) for v in (q_norm_a, k_norm_a, q_norm_b, k_norm_b)], axis=0)
    qa, kpa, vpa, qb, kpb, vpb = _headnorm_fwd(proj, hn_w, "headnorm")
    table = jnp.pad(relpos_table[0], ((0, 0), (0, REL_W - (2 * MAX_REL + 1))))
    bias_a = jnp.where(_band_mask(A_PREV, PAD_A)[None], jnp.transpose(_relpos_fwd(table, "relpos_bias"), (1, 0, 2)), NEG)
    rel_b = jnp.arange(TQ)[:, None] - (jnp.arange(PAD_B + TQ)[None, :] - PAD_B)
    slopes = 2.0 ** (-8.0 * jnp.arange(1, N_HEADS + 1, dtype=F32) / N_HEADS)
    bias_b = jnp.where(_band_mask(B_PREV, PAD_B)[None], -slopes[:, None, None] * jnp.abs(rel_b).astype(F32)[None], NEG)
    no_sinks = jnp.full((N_HEADS,), NEG, F32)
    ffn0_w, ssm_w, ffn1_w = [('ffn_w_in', 0), ('ffn_w_out', 0)], [('ssm_w_in', 0), ('ssm_w_out', 0)], [('ffn_w_in', 1), ('ffn_w_out', 1)]
    oa, stats_a, got = _attn_fwd(qa, kpa, vpa, bias_a, no_sinks, PAD_A, "attn_a", comm=gather_of(ffn0_w + ssm_w))
    wt_ffn_in0, w_ffn_out0, wt_ssm_in, w_ssm_out = joined(got)
    ob, stats_b, got = _attn_fwd(qb, kpb, vpb, bias_b, sinks[0], PAD_B, "attn_b", comm=gather_of(ffn1_w))
    wt_ffn_in1, w_ffn_out1 = joined(got)
    wt_ssm_dt = jnp.pad(wt_ssm_in[ZX:], ((0, LANES - SSM_HEADS), (0, 0)))
    x1 = _mm(oa, w_attn_out, "mm_attn_out_a", res=x0, b_rows=(0, 512))
    x1 = _mm(ob, w_attn_out, "mm_attn_out_b", res=x1, b_rows=(512, 512))
    a0, ffn0_saved = _ffn_fwd(x1, g_ffn0, wt_ffn_in0, ffn_cw8[0], ffn_cb[0], "0")
    x2 = _mm(a0, w_ffn_out0, "mm_ffn_out0", res=x1)

    zx, h2, xbc, conv_pre = _ssm_in_pre(x2, g_mix1, wt_ssm_in, ssm_cw8, ssm_cb, "mm_ssm_in")
    dtraw = _mm(h2, wt_ssm_dt, "mm_ssm_dt", trans_b=True)
    dt_bias = _lanes128(ssm_dt_bias[0])
    alog = _lanes128(ssm_a_log[0])
    dexp = jnp.repeat(ssm_d[0], HEAD_DIM).reshape(1, D_INNER)
    dt = _dt_fwd(dtraw, dt_bias, "ssm_dt")
    (y, sprev, y4), _ = _ssd_fwd(xbc, dt, alog, zx, dexp, ssm_nw, "ssd_fwd")
    x3 = _mm(y4, w_ssm_out, "mm_ssm_out", res=x2)
    a1, ffn1_saved = _ffn_fwd(x3, g_ffn1, wt_ffn_in1, ffn_cw8[1], ffn_cb[1], "1")

    dx4, dx4b, sq = _mm_loss(a1, w_ffn_out1, x3, target, "mm_ffn_out1_loss")
    loss = lax.psum(0.5 * jnp.sum(sq) / D_MODEL, ("x", "y", "c"))

    grads = {}

    def scatter_of(grads_2d):
        return _Comm([g.reshape(N_DEV, -1, D_MODEL) for g in grads_2d], [True] * len(grads_2d))

    dx3, dx3b, dg_ffn1, dwtin1, dcw1, dcb1, dwout1 = _ffn_bwd(
        dx4, dx4b, x3, g_ffn1, wt_ffn_in1, ffn_cw8[1], w_ffn_out1, ffn1_saved, "1")

    dy4 = _mm(dx3b, w_ssm_out, "mm_ssm_dy", trans_b=True)
    dw_ssm_out = _mm_tn(y4, dx3b, "mm_ssm_dwout")
    (dxbc, ddt, dalog, dz, dd_lane, dnw), parts_ffn1 = _ssd_bwd(
        xbc, dt, alog, sprev, dy4, y, zx, dexp, ssm_nw, "ssd_bwd", comm=scatter_of([dwtin1, dwout1]))
    dxr, dcw_s, dcb_s = _ssm_pre_bwd(zx, conv_pre, dxbc, ssm_cw8, "ssm_pre_bwd")
    ddtraw, ddtb = _dt_bwd(dtraw, dt_bias, ddt, "ssm_dt_bwd")
    dh2 = _mm(dz, wt_ssm_in, "mm_ssm_dh_z", b_rows=(0, D_INNER))
    dh2 = _mm(dxr, wt_ssm_in[D_INNER:ZX], "mm_ssm_dh_x", res=dh2)
    dwt_ssm_in = jnp.concatenate([
        _mm_tn(dz, h2, "mm_ssm_dwin_z"), _mm_tn(dxr, h2, "mm_ssm_dwin_x"),
        _mm_tn(ddtraw, h2, "mm_ssm_dwin_dt")[:SSM_HEADS]], axis=0)
    dx2, dx2b, dg_mix1 = _mm_rms_bwd(ddtraw, wt_ssm_dt, 0, dh2, x2, g_mix1, dx3, "mm_ssm_dh_dt")
    grads['ssm_conv_w'] = dcw_s[:4][None]
    grads['ssm_conv_b'] = dcb_s
    grads['ssm_norm'] = dnw
    grads['ssm_dt_bias'] = ddtb[:, :SSM_HEADS]
    grads['ssm_a_log'] = dalog[:, :SSM_HEADS]
    grads['ssm_d'] = jnp.sum(dd_lane.reshape(SSM_HEADS, HEAD_DIM), axis=1)[None]

    dx1, dx1b, dg_ffn0, dwtin0, dcw0, dcb0, dwout0 = _ffn_bwd(
        dx2, dx2b, x1, g_ffn0, wt_ffn_in0, ffn_cw8[0], w_ffn_out0, ffn0_saved, "0")
    grads['ffn_conv_w'] = jnp.stack([dcw0, dcw1])
    grads['ffn_conv_b'] = jnp.concatenate([dcb0, dcb1], axis=0)
    grads['norm_ffn'] = jnp.concatenate([dg_ffn0, dg_ffn1], axis=0)

    do = _mm(dx1b, w_attn_out, "mm_attn_do", out_dtype=BF16, trans_b=True)
    dw_attn_out = jnp.concatenate([_mm_tn(oa, dx1b, "mm_attn_dwout_a"), _mm_tn(ob, dx1b, "mm_attn_dwout_b")], axis=0)
    (dqa, dkpa, dvpa, dbias_a, _), parts_ssm = _attn_bwd(
        qa, kpa, vpa, bias_a, no_sinks, do, stats_a, 0, PAD_A, "attn_a_bwd",
        comm=scatter_of([dwt_ssm_in, dw_ssm_out, dw_attn_out]))
    (dqb, dkpb, dvpb, _, dsink), parts_ffn0 = _attn_bwd(
        qb, kpb, vpb, bias_b, sinks[0], do, stats_b, 4, PAD_B, "attn_b_bwd", comm=scatter_of([dwtin0, dwout0]))
    grads['relpos_table'] = _relpos_bwd(jnp.transpose(dbias_a, (1, 0, 2)), "relpos_bwd")[None, :, :2 * MAX_REL + 1]
    grads['sinks'] = dsink[:, :2, 0].reshape(1, N_HEADS)
    dproj, dhn = _headnorm_bwd(proj, hn_w, dqa, dkpa, dvpa, dqb, dkpb, dvpb, "headnorm_bwd")
    dhn = dhn[:, :HEAD_DIM] + dhn[:, HEAD_DIM:]
    for k, n in enumerate(('q_norm_a', 'k_norm_a', 'q_norm_b', 'k_norm_b')):
        grads[n] = dhn[k:k + 1]
    dwt_attn_in = _mm_tn(dproj, h0, "mm_attn_dwin")
    dx0, _, dg_mix0, parts_attn_in = _mm_rms_bwd(dproj, wt_attn_in, 0, None, x0, g_mix0, dx1, "mm_attn_dh",
                                                 comm=scatter_of([dwt_attn_in]))
    grads['norm_mix'] = jnp.concatenate([dg_mix0, dg_mix1], axis=0)

    def summed_t(parts, name):
        return _sum_parts(parts, name).T[None]

    sm_shapes = [w[n].shape for n in _SMALL]
    rp_shapes = [w[n].shape for n in _REPL]
    recv = _exchange(_Comm(
        [_pack([_split(grads[n], _SHARD_AX[n]) for n in _SMALL], F32, lead=(N_DEV,)), _pack([grads[n] for n in _REPL], F32)],
        [True, False]), "exchange_small")
    big_parts = {
        'attn_w_in': [summed_t(parts_attn_in[0], "sum_attn_w_in")], 'attn_w_out': [parts_ssm[2]],
        'ssm_w_in': [summed_t(parts_ssm[0], "sum_ssm_w_in")], 'ssm_w_out': [parts_ssm[1]],
        'ffn_w_in': [summed_t(parts_ffn0[0], "sum_ffn_w_in0"), summed_t(parts_ffn1[0], "sum_ffn_w_in1")],
        'ffn_w_out': [parts_ffn0[1], parts_ffn1[1]],
    }
    res = [{}, {}, {}, {}]
    for n in _BIG:
        for kind, a in enumerate(_adamw(big_parts[n], w[n], mom[n], var[n], f"adamw_{n}")):
            res[kind][n] = a
    for names, shapes, parts in ((_SMALL, sm_shapes, recv[0]), (_REPL, rp_shapes, recv[1])):
        outs = _adamw([parts], _pack([w[n] for n in names], F32)[None], _pack([mom[n] for n in names], F32)[None],
                      _pack([var[n] for n in names], F32)[None], "adamw_" + ("small" if names is _SMALL else "replicated"))
        for kind, flat in enumerate(outs):
            for n, a in zip(names, _unpack(flat[0], shapes)):
                res[kind][n] = a
    return (loss, dx0[None], *[res[0][n] for n in _WEIGHTS], *[res[1][n] for n in _WEIGHTS],
            *[res[2][n] for n in _WEIGHTS], *[res[3][n] for n in _WEIGHTS])
```

```python
import jax
import jax.numpy as jnp
from jax import lax
from jax.experimental import pallas as pl
from jax.experimental.pallas import tpu as pltpu

F32 = jnp.float32
BF16 = jnp.bfloat16
HI = lax.Precision.HIGHEST
MESH = pl.DeviceIdType.MESH
NEG = -1e30

N_DEV = 8
D_MODEL = 1024
EPS = 1e-6
CHUNK = 64
HEAD_DIM = 64
N_HEADS = 8
A_PREV = 8
B_PREV = 2
MAX_REL = 256
TQ = 2 * CHUNK
ATT_SUB = 8
PAD_A = A_PREV * CHUNK
PAD_B = B_PREV * CHUNK
REL_W = PAD_A + TQ
D_ATT = N_HEADS * HEAD_DIM
COL_QA, COL_KA, COL_VA, COL_QB = 0, D_ATT, 2 * D_ATT, 3 * D_ATT
COL_KB, COL_VB = 4 * D_ATT, 4 * D_ATT + 2 * HEAD_DIM
ATTN_PROJ = COL_VB + 2 * HEAD_DIM
D_INNER = 2048
SSM_HEADS = 32
SSM_GROUPS = 4
SSM_STATE = 128
XBC = D_INNER + 2 * SSM_GROUPS * SSM_STATE
ZX = D_INNER + XBC
D_FF = 2816
SSD_L = 128
LANES = 128
VMEM_LIMIT = 56 << 20

ADAM_LR, ADAM_B1, ADAM_B2, ADAM_EPS, ADAM_WD, ADAM_STEP = 0.001, 0.9, 0.999, 1e-08, 0.01, 10


def _cp(sem=None):
    return pltpu.CompilerParams(dimension_semantics=sem, vmem_limit_bytes=VMEM_LIMIT)


def _dot(a, b, ca=1, cb=0, prec=None):
    return lax.dot_general(a, b, (((ca,), (cb,)), ((), ())), preferred_element_type=F32, precision=prec)


def _pick(n, cands):
    for c in cands:
        if n % c == 0:
            return c
    return n


def _lo_mask():
    return lax.broadcasted_iota(jnp.int32, (1, LANES), 1) < HEAD_DIM


_TN_CHUNKS = (1408, 1536, 1152, 1024, 512, 256, 128)


TN_MAX_ROWS = 3072
MM_WIDE = 2304


def _mm_tn(a, b, name):
    kdim, m = a.shape
    n = b.shape[1]
    assert b.shape[0] == kdim, (a.shape, b.shape)
    mb = m if m <= TN_MAX_ROWS else m // 2
    tn = _pick(n, _TN_CHUNKS)
    tk = _pick(kdim, (512, 256, 128))
    nk = kdim // tk

    def body(a_ref, b_ref, o_ref, acc):
        k = pl.program_id(1)

        @pl.when(k == 0)
        def _():
            acc[...] = jnp.zeros_like(acc)

        av = a_ref[...]
        for c in range(0, n, tn):
            acc[:, c:c + tn] += _dot(av, b_ref[:, c:c + tn], 0, 0)

        @pl.when(k == nk - 1)
        def _():
            o_ref[...] = acc[...].astype(BF16)

    return pl.pallas_call(
        body, name=name, grid=(m // mb, nk),
        in_specs=[pl.BlockSpec((tk, mb), lambda j, k: (k, j)), pl.BlockSpec((tk, n), lambda j, k: (k, 0))],
        out_specs=pl.BlockSpec((mb, n), lambda j, k: (j, 0)), out_shape=jax.ShapeDtypeStruct((m, n), BF16),
        scratch_shapes=[pltpu.VMEM((mb, n), F32)], compiler_params=_cp(("parallel", "arbitrary")),
    )(a, b)


def _mm(a, b, name, out_dtype=F32, res=None, trans_b=False, b_rows=None):
    m, kdim = a.shape
    if b_rows is None:
        b_rows = (0, b.shape[0])
    off, rows = b_rows
    n = rows if trans_b else b.shape[1]
    assert (b.shape[1] if trans_b else rows) == kdim and off % rows == 0, (a.shape, b.shape, b_rows)
    tn = _pick(n, _TN_CHUNKS)
    tm = _pick(m, (256, 128) if n > MM_WIDE else (512, 256, 128))

    def body(*refs):
        if res is None:
            a_ref, b_ref, o_ref = refs
        else:
            a_ref, b_ref, r_ref, o_ref = refs
        av = a_ref[...]
        for c in range(0, n, tn):
            r = _dot(av, b_ref[c:c + tn, :], 1, 1) if trans_b else _dot(av, b_ref[:, c:c + tn], 1, 0)
            if res is not None:
                r = r + r_ref[:, c:c + tn]
            o_ref[:, c:c + tn] = r.astype(out_dtype)

    in_specs = [pl.BlockSpec((tm, kdim), lambda i: (i, 0)), pl.BlockSpec((rows, b.shape[1]), lambda i: (off // rows, 0))]
    args = [a, b]
    if res is not None:
        in_specs.append(pl.BlockSpec((tm, n), lambda i: (i, 0)))
        args.append(res)
    return pl.pallas_call(
        body, name=name, grid=(m // tm,), in_specs=in_specs, out_specs=pl.BlockSpec((tm, n), lambda i: (i, 0)),
        out_shape=jax.ShapeDtypeStruct((m, n), out_dtype), compiler_params=_cp(("parallel",)),
    )(*args)


def _rms_mm(x, g, bt, n, name, out_dtype):
    t, d = x.shape
    tn = _pick(n, _TN_CHUNKS)
    tm = _pick(t, (256, 128))

    def body(x_ref, g_ref, b_ref, o_ref, h_ref):
        xv = x_ref[...]
        r = lax.rsqrt(jnp.mean(xv * xv, axis=-1, keepdims=True) + EPS)
        h = (xv * r * g_ref[...]).astype(BF16)
        h_ref[...] = h
        for c in range(0, n, tn):
            o_ref[:, c:c + tn] = _dot(h, b_ref[c:c + tn, :], 1, 1).astype(out_dtype)

    row = pl.BlockSpec((tm, d), lambda i: (i, 0))
    return pl.pallas_call(
        body, name=name, grid=(t // tm,),
        in_specs=[row, pl.BlockSpec((1, d), lambda i: (0, 0)), pl.BlockSpec(bt.shape, lambda i: (0, 0))],
        out_specs=[pl.BlockSpec((tm, n), lambda i: (i, 0)), row],
        out_shape=[jax.ShapeDtypeStruct((t, n), out_dtype), jax.ShapeDtypeStruct((t, d), BF16)],
        compiler_params=_cp(("parallel",)),
    )(x, g, bt)


def _mm_rms_bwd(a, b, b_off, dh_prev, x, g, dres, name, comm=None):
    t, d = x.shape
    kdim = a.shape[1]
    assert b_off % kdim == 0 and b.shape[1] == d, (a.shape, b.shape, b_off)
    tm = _pick(t, (256, 128))

    def body(*refs):
        if dh_prev is None:
            a_ref, b_ref, x_ref, g_ref, dr_ref, dx_ref, dxb_ref, dg_ref = refs
            dhv = _dot(a_ref[...], b_ref[...], 1, 0)
        else:
            a_ref, b_ref, p_ref, x_ref, g_ref, dr_ref, dx_ref, dxb_ref, dg_ref = refs
            dhv = _dot(a_ref[...], b_ref[...], 1, 0) + p_ref[...]
        xv = x_ref[...]
        r = lax.rsqrt(jnp.mean(xv * xv, axis=-1, keepdims=True) + EPS)
        xh = xv * r
        dxh = dhv * g_ref[...]
        dx = dr_ref[...] + r * (dxh - xh * jnp.mean(dxh * xh, axis=-1, keepdims=True))
        dx_ref[...] = dx
        dxb_ref[...] = dx.astype(BF16)

        @pl.when(pl.program_id(0) == 0)
        def _():
            dg_ref[...] = jnp.zeros_like(dg_ref)

        dg_ref[...] += jnp.sum(dhv * xh, axis=0, keepdims=True)

    row = pl.BlockSpec((tm, d), lambda i: (i, 0))
    vec = pl.BlockSpec((1, d), lambda i: (0, 0))
    in_specs = [pl.BlockSpec((tm, kdim), lambda i: (i, 0)), pl.BlockSpec((kdim, d), lambda i: (b_off // kdim, 0))]
    args = [a, b]
    if dh_prev is not None:
        in_specs.append(row)
        args.append(dh_prev)
    outs, got = _call(
        body, name=name, grid=(t // tm,), in_specs=in_specs + [row, vec, row], out_specs=[row, row, vec],
        out_shape=[jax.ShapeDtypeStruct((t, d), F32), jax.ShapeDtypeStruct((t, d), BF16), jax.ShapeDtypeStruct((1, d), F32)],
        args=(*args, x, g, dres), sem=("arbitrary",), comm=comm)
    return (*outs, got) if comm is not None else tuple(outs)


def _mm_loss(a, b, res, target, name):
    t, kdim = a.shape
    d = b.shape[1]
    tm = _pick(t, (512, 256, 128))

    def body(a_ref, b_ref, r_ref, t_ref, dy_ref, dyb_ref, acc_ref):
        @pl.when(pl.program_id(0) == 0)
        def _():
            acc_ref[...] = jnp.zeros_like(acc_ref)

        err = _dot(a_ref[...], b_ref[...], 1, 0) + r_ref[...] - t_ref[...]
        dy = err * (1.0 / d)
        dy_ref[...] = dy
        dyb_ref[...] = dy.astype(BF16)
        acc_ref[...] += jnp.sum(err * err, axis=0, keepdims=True)

    row = pl.BlockSpec((tm, d), lambda i: (i, 0))
    vec = pl.BlockSpec((1, d), lambda i: (0, 0))
    return pl.pallas_call(
        body, name=name, grid=(t // tm,),
        in_specs=[pl.BlockSpec((tm, kdim), lambda i: (i, 0)), pl.BlockSpec((kdim, d), lambda i: (0, 0)), row, row],
        out_specs=[row, row, vec],
        out_shape=[jax.ShapeDtypeStruct((t, d), F32), jax.ShapeDtypeStruct((t, d), BF16), jax.ShapeDtypeStruct((1, d), F32)],
        compiler_params=_cp(("arbitrary",)),
    )(a, b, res, target)


def _head_sums(v):
    ri = lax.broadcasted_iota(jnp.int32, (LANES, LANES), 0) // HEAD_DIM
    ci = lax.broadcasted_iota(jnp.int32, (LANES, LANES), 1) // HEAD_DIM
    ones = (ri == ci).astype(BF16)
    hi = v.astype(BF16)
    lo_part = (v - hi.astype(F32)).astype(BF16)
    return _dot(hi, ones, 1, 0) + _dot(lo_part, ones, 1, 0)


def _head_rms(xs):
    r = lax.rsqrt(_head_sums(xs * xs) * (1.0 / HEAD_DIM) + EPS)
    return xs * r, r


def _head_rms_bwd(xs, w, dy):
    xh, r = _head_rms(xs)
    dxh = dy * w
    mm = _head_sums(dxh * xh) * (1.0 / HEAD_DIM)
    return r * (dxh - xh * mm), dy * xh


_QSCALE = HEAD_DIM ** -0.5


def _headnorm_fwd(proj, ws, name):
    t = proj.shape[0]
    tm = TQ
    lead = PAD_A // tm
    leadb = PAD_B // tm

    def body(p_ref, w_ref, qa_ref, ka_ref, va_ref, qb_ref, kb_ref, vb_ref):
        data = pl.program_id(0) >= lead
        lo = _lo_mask()

        def put(ref, c, val):
            ref[:, c:c + val.shape[1]] = jnp.where(data, val, 0.0).astype(BF16)

        def per_query_head(slab):
            other = pltpu.roll(slab, HEAD_DIM, 1)
            e0, e1 = jnp.where(lo, slab, other), jnp.where(lo, other, slab)
            return jnp.concatenate([e0, e0, e1, e1], axis=1)

        for s in range(D_ATT // LANES):
            c = LANES * s
            xh, _ = _head_rms(p_ref[:, COL_QA + c:COL_QA + c + LANES])
            qa_ref[:, c:c + LANES] = (xh * w_ref[0:1, :] * _QSCALE).astype(BF16)
            xh, _ = _head_rms(p_ref[:, COL_KA + c:COL_KA + c + LANES])
            put(ka_ref, c, xh * w_ref[1:2, :])
            xh, _ = _head_rms(p_ref[:, COL_QB + c:COL_QB + c + LANES])
            qb_ref[:, c:c + LANES] = (xh * w_ref[2:3, :] * _QSCALE).astype(BF16)
        put(va_ref, 0, p_ref[:, COL_VA:COL_VA + D_ATT])
        xh, _ = _head_rms(p_ref[:, COL_KB:COL_KB + LANES])
        put(kb_ref, 0, per_query_head(xh * w_ref[3:4, :]))
        put(vb_ref, 0, per_query_head(p_ref[:, COL_VB:COL_VB + LANES]))

    src = lambda i: jnp.maximum(i - lead, 0)
    wide = pl.BlockSpec((tm, D_ATT), lambda i: (src(i), 0))
    pad_a = pl.BlockSpec((tm, D_ATT), lambda i: (i, 0))
    pad_b = pl.BlockSpec((tm, D_ATT), lambda i: (jnp.maximum(i - lead + leadb, 0), 0))
    sd = lambda rows: jax.ShapeDtypeStruct((rows, D_ATT), BF16)
    return pl.pallas_call(
        body, name=name, grid=(t // tm + lead,),
        in_specs=[pl.BlockSpec((tm, ATTN_PROJ), lambda i: (src(i), 0)), pl.BlockSpec((4, LANES), lambda i: (0, 0))],
        out_specs=[wide, pad_a, pad_a, wide, pad_b, pad_b],
        out_shape=[sd(t), sd(t + PAD_A), sd(t + PAD_A), sd(t), sd(t + PAD_B), sd(t + PAD_B)],
        compiler_params=_cp(("arbitrary",)),
    )(proj, ws)


def _headnorm_bwd(proj, ws, dqa, dkpa, dvpa, dqb, dkpb, dvpb, name):
    t = proj.shape[0]
    tm = TQ
    offa, offb = PAD_A // tm, PAD_B // tm

    def body(p_ref, w_ref, dqa_ref, dka_ref, dva_ref, dqb_ref, dkb_ref, dvb_ref, dp_ref, dw_ref):
        i = pl.program_id(0)
        lo = _lo_mask()

        @pl.when(i == 0)
        def _():
            dw_ref[...] = jnp.zeros_like(dw_ref)

        acc = [jnp.zeros((1, LANES), F32) for _ in range(4)]
        for s in range(D_ATT // LANES):
            c = LANES * s
            dx, dwl = _head_rms_bwd(p_ref[:, COL_QA + c:COL_QA + c + LANES], w_ref[0:1, :], dqa_ref[:, c:c + LANES] * _QSCALE)
            dp_ref[:, COL_QA + c:COL_QA + c + LANES] = dx.astype(BF16)
            acc[0] += jnp.sum(dwl, axis=0, keepdims=True)
            dx, dwl = _head_rms_bwd(p_ref[:, COL_KA + c:COL_KA + c + LANES], w_ref[1:2, :], dka_ref[:, c:c + LANES])
            dp_ref[:, COL_KA + c:COL_KA + c + LANES] = dx.astype(BF16)
            acc[1] += jnp.sum(dwl, axis=0, keepdims=True)
            dx, dwl = _head_rms_bwd(p_ref[:, COL_QB + c:COL_QB + c + LANES], w_ref[2:3, :], dqb_ref[:, c:c + LANES] * _QSCALE)
            dp_ref[:, COL_QB + c:COL_QB + c + LANES] = dx.astype(BF16)
            acc[2] += jnp.sum(dwl, axis=0, keepdims=True)
        dp_ref[:, COL_VA:COL_VA + D_ATT] = dva_ref[...].astype(BF16)

        def group_sum(ref):
            s0 = ref[:, 0:LANES] + ref[:, LANES:2 * LANES]
            s1 = ref[:, 2 * LANES:3 * LANES] + ref[:, 3 * LANES:4 * LANES]
            s0 = s0 + pltpu.roll(s0, HEAD_DIM, 1)
            s1 = s1 + pltpu.roll(s1, HEAD_DIM, 1)
            return jnp.where(lo, s0, s1)

        dx, dwl = _head_rms_bwd(p_ref[:, COL_KB:COL_KB + LANES], w_ref[3:4, :], group_sum(dkb_ref))
        dp_ref[:, COL_KB:COL_KB + LANES] = dx.astype(BF16)
        acc[3] += jnp.sum(dwl, axis=0, keepdims=True)
        dp_ref[:, COL_VB:COL_VB + LANES] = group_sum(dvb_ref).astype(BF16)
        for n in range(4):
            dw_ref[n:n + 1, :] += acc[n]

    wide = pl.BlockSpec((tm, D_ATT), lambda i: (i, 0))
    pa = pl.BlockSpec((tm, D_ATT), lambda i: (i + offa, 0))
    pb = pl.BlockSpec((tm, D_ATT), lambda i: (i + offb, 0))
    whole = pl.BlockSpec((tm, ATTN_PROJ), lambda i: (i, 0))
    return pl.pallas_call(
        body, name=name, grid=(t // tm,),
        in_specs=[whole, pl.BlockSpec((4, LANES), lambda i: (0, 0)), wide, pa, pa, wide, pb, pb],
        out_specs=[whole, pl.BlockSpec((4, LANES), lambda i: (0, 0))],
        out_shape=[jax.ShapeDtypeStruct((t, ATTN_PROJ), BF16), jax.ShapeDtypeStruct((4, LANES), F32)],
        compiler_params=_cp(("arbitrary",)),
    )(proj, ws, dqa, dkpa, dvpa, dqb, dkpb, dvpb)


ROLL_W = 1024


def _rel_onehot():
    r_io = lax.broadcasted_iota(jnp.int32, (REL_W, ROLL_W), 0)
    m_io = lax.broadcasted_iota(jnp.int32, (REL_W, ROLL_W), 1)
    return (r_io == jnp.clip(REL_W - 1 - m_io, -MAX_REL, MAX_REL) + MAX_REL).astype(F32)


def _relpos_fwd(table, name):
    def body(t_ref, o_ref):
        rr = _dot(t_ref[...], _rel_onehot(), 1, 0, HI)

        def step(q, c):
            o_ref[q] = pltpu.roll(rr, (ROLL_W - (TQ - 1) + q) % ROLL_W, 1)[:, :REL_W]
            return c

        lax.fori_loop(0, TQ, step, 0)

    return pl.pallas_call(
        body, name=name, out_shape=jax.ShapeDtypeStruct((TQ, N_HEADS, REL_W), F32),
        in_specs=[pl.BlockSpec(memory_space=pltpu.VMEM)], out_specs=pl.BlockSpec(memory_space=pltpu.VMEM),
        compiler_params=_cp(),
    )(table)


def _relpos_bwd(dbias_t, name):
    def body(d_ref, o_ref):
        def step(q, acc):
            row = jnp.concatenate([d_ref[q], jnp.zeros((N_HEADS, ROLL_W - REL_W), F32)], axis=1)
            return acc + pltpu.roll(row, TQ - 1 - q, 1)

        drr = lax.fori_loop(0, TQ, step, jnp.zeros((N_HEADS, ROLL_W), F32))
        o_ref[...] = _dot(drr, _rel_onehot(), 1, 1, HI)

    return pl.pallas_call(
        body, name=name, out_shape=jax.ShapeDtypeStruct((N_HEADS, REL_W), F32),
        in_specs=[pl.BlockSpec(memory_space=pltpu.VMEM)], out_specs=pl.BlockSpec(memory_space=pltpu.VMEM),
        compiler_params=_cp(),
    )(dbias_t)


def _attn_scores(qe, kw, bias, kvalid):
    return jnp.where(kvalid, _dot(qe, kw, 1, 1) + bias, NEG)


def _stat_cols(stats, e):
    return stats[:, 64 * e:64 * e + 1], stats[:, 64 * e + 32:64 * e + 33]


def _attn_fwd(q, kp, vp, bias, sinks, pad, name, comm=None):
    t, hd = q.shape
    w = pad + TQ

    def body(sink_ref, q_ref, k_ref, v_ref, b_ref, o_ref, st_ref):
        hp, i = pl.program_id(0), pl.program_id(1)
        lo = _lo_mask()
        lane = lax.broadcasted_iota(jnp.int32, (1, LANES), 1)
        for j in range(ATT_SUB):
            start = pl.multiple_of((i * ATT_SUB + j) * TQ, TQ)
            qv = q_ref[TQ * j:TQ * (j + 1), :]
            kw = k_ref[pl.ds(start, w), :]
            vw = v_ref[pl.ds(start, w), :]
            kvalid = (start + lax.broadcasted_iota(jnp.int32, (1, w), 1)) >= pad
            outs, ms, ls = [], [], []
            for e in range(2):
                sel = lo if e == 0 else jnp.logical_not(lo)
                qe = jnp.where(sel, qv, jnp.zeros_like(qv))
                snk = sink_ref[2 * hp + e]
                s = _attn_scores(qe, kw, b_ref[e], kvalid)
                m = jnp.maximum(jnp.max(s, axis=-1, keepdims=True), snk)
                acc = _dot(jnp.exp(s - m).astype(BF16), jnp.where(sel, vw, jnp.ones_like(vw)), 1, 0)
                denom = acc[:, 64 * (1 - e):64 * (1 - e) + 1] + jnp.exp(snk - m)
                outs.append(acc * (1.0 / denom))
                ms.append(m)
                ls.append(denom)
            o_ref[TQ * j:TQ * (j + 1), :] = jnp.where(lo, outs[0], outs[1]).astype(BF16)
            st_ref[TQ * j:TQ * (j + 1), :] = jnp.where(lane < 32, ms[0], jnp.where(lane < 64, ls[0],
                                                                                 jnp.where(lane < 96, ms[1], ls[1])))

    full = pl.BlockSpec((t + pad, LANES), lambda h, i: (0, h))
    tile = pl.BlockSpec((ATT_SUB * TQ, LANES), lambda h, i: (i, h))
    (o, stats), got = _call(
        body, name=name, grid=(hd // LANES, t // (ATT_SUB * TQ)),
        in_specs=[pl.BlockSpec(memory_space=pltpu.SMEM), tile, full, full, pl.BlockSpec((2, TQ, w), lambda h, i: (h, 0, 0))],
        out_specs=[tile, tile], out_shape=[jax.ShapeDtypeStruct((t, hd), BF16), jax.ShapeDtypeStruct((t, hd), F32)],
        args=(sinks, q, kp, vp, bias), sem=("parallel", "arbitrary"), comm=comm)
    return o, stats, got


def _attn_bwd(q, kp, vp, bias, sinks, do, stats, col_off, pad, name, comm=None):
    t, hd = q.shape
    w = pad + TQ
    nhp = hd // LANES

    def body(sink_ref, q_ref, k_ref, v_ref, b_ref, do_ref, st_ref, dq_ref, dk_ref, dv_ref, db_ref, ds_ref):
        hp, i = pl.program_id(0), pl.program_id(1)

        @pl.when(i == 0)
        def _():
            dk_ref[...] = jnp.zeros_like(dk_ref)
            dv_ref[...] = jnp.zeros_like(dv_ref)
            db_ref[...] = jnp.zeros_like(db_ref)
            ds_ref[...] = jnp.zeros_like(ds_ref)

        lo = _lo_mask()
        row8 = lax.broadcasted_iota(jnp.int32, (8, LANES), 0)
        dbias = [None, None]
        dsink = jnp.zeros((8, LANES), F32)
        for j in range(ATT_SUB):
            start = pl.multiple_of((i * ATT_SUB + j) * TQ, TQ)
            qv = q_ref[TQ * j:TQ * (j + 1), :]
            dov = do_ref[TQ * j:TQ * (j + 1), :]
            kw = k_ref[pl.ds(start, w), :]
            vw = v_ref[pl.ds(start, w), :]
            kvalid = (start + lax.broadcasted_iota(jnp.int32, (1, w), 1)) >= pad
            stats = st_ref[TQ * j:TQ * (j + 1), :]
            dqs, dkw, dvw = [], None, None
            for e in range(2):
                sel = lo if e == 0 else jnp.logical_not(lo)
                qe = jnp.where(sel, qv, jnp.zeros_like(qv))
                doe = jnp.where(sel, dov, jnp.zeros_like(dov))
                m, denom = _stat_cols(stats, e)
                inv = 1.0 / denom
                p = jnp.exp(_attn_scores(qe, kw, b_ref[e], kvalid) - m) * inv
                psink = jnp.exp(sink_ref[2 * hp + e] - m) * inv
                dp = _dot(doe, vw, 1, 1)
                delta = jnp.sum(p * dp, axis=-1, keepdims=True)
                ds = p * (dp - delta)
                dbias[e] = ds if dbias[e] is None else dbias[e] + ds
                dsink = dsink + jnp.where(row8 == e, jnp.sum(-psink * delta, axis=0, keepdims=True), 0.0)
                dsb = ds.astype(BF16)
                dqs.append(_dot(dsb, kw, 1, 0))
                dk_e = _dot(dsb, qe, 0, 0)
                dv_e = _dot(p.astype(BF16), doe, 0, 0)
                dkw = dk_e if dkw is None else dkw + dk_e
                dvw = dv_e if dvw is None else dvw + dv_e
            dq_ref[TQ * j:TQ * (j + 1), :] = jnp.where(lo, dqs[0], dqs[1])
            dk_ref[pl.ds(start, w), :] += dkw
            dv_ref[pl.ds(start, w), :] += dvw
        for e in range(2):
            db_ref[e] += dbias[e]
        ds_ref[0] += dsink

    full = pl.BlockSpec((t + pad, LANES), lambda h, i: (0, h))
    tile = pl.BlockSpec((ATT_SUB * TQ, LANES), lambda h, i: (i, h))
    btile = pl.BlockSpec((2, TQ, w), lambda h, i: (h, 0, 0))
    return _call(
        body, name=name, grid=(nhp, t // (ATT_SUB * TQ)),
        in_specs=[pl.BlockSpec(memory_space=pltpu.SMEM), tile, full, full, btile,
                  pl.BlockSpec((ATT_SUB * TQ, LANES), lambda h, i: (i, h + col_off)), tile],
        out_specs=[tile, full, full, btile, pl.BlockSpec((1, 8, LANES), lambda h, i: (h, 0, 0))],
        out_shape=[jax.ShapeDtypeStruct((t, hd), F32), jax.ShapeDtypeStruct((t + pad, hd), F32),
                   jax.ShapeDtypeStruct((t + pad, hd), F32), jax.ShapeDtypeStruct((N_HEADS, TQ, w), F32),
                   jax.ShapeDtypeStruct((nhp, 8, LANES), F32)],
        args=(sinks, q, kp, vp, bias, do, stats), sem=("parallel", "arbitrary"), comm=comm)


def _conv_apply(taps, w_ref, ktaps):
    out = taps[0] * w_ref[ktaps - 1:ktaps, :]
    for s in range(1, ktaps):
        out = out + taps[s] * w_ref[ktaps - 1 - s:ktaps - s, :]
    return out


def _sigmoid(x):
    return jax.nn.sigmoid(x)


def _silu_grad(x):
    sg = _sigmoid(x)
    return x * sg, sg * (1.0 + x * (1.0 - sg))


FFN_HALO = 16
FFN_BT = 256
FFN_BC = 1408


def _ffn_in_mid(x, g, wt, w8, b, name):
    t, d = x.shape
    f = D_FF
    tm = FFN_BT

    def body(x_ref, g_ref, b_ref, w_ref, cb_ref, gu_ref, h_ref, a_ref, gc_ref, halo_ref):
        @pl.when(pl.program_id(0) == 0)
        def _():
            halo_ref[...] = jnp.zeros_like(halo_ref)

        xv = x_ref[...]
        r = lax.rsqrt(jnp.mean(xv * xv, axis=-1, keepdims=True) + EPS)
        h = (xv * r * g_ref[...]).astype(BF16)
        h_ref[...] = h
        for c in range(0, f, FFN_BC):
            cs = slice(c, c + FFN_BC)
            gate = _dot(h, b_ref[c:c + FFN_BC, :], 1, 1).astype(BF16)
            up = _dot(h, b_ref[f + c:f + c + FFN_BC, :], 1, 1).astype(BF16)
            gu_ref[:, cs] = gate
            gu_ref[:, f + c:f + c + FFN_BC] = up
            gf = gate.astype(F32)
            ext = jnp.concatenate([halo_ref[:, cs], gf], axis=0)
            gc = (cb_ref[:, cs] + gf * w_ref[2:3, cs] + pltpu.roll(ext, 1, 0)[8:] * w_ref[1:2, cs]
                  + pltpu.roll(ext, 2, 0)[8:] * w_ref[0:1, cs])
            a_ref[:, cs] = (gc * _sigmoid(gc) * up.astype(F32)).astype(BF16)
            gc_ref[:, cs] = gc.astype(BF16)
            halo_ref[:, cs] = gf[tm - 8:]

    row = pl.BlockSpec((tm, d), lambda i: (i, 0))
    row_f = pl.BlockSpec((tm, f), lambda i: (i, 0))
    return pl.pallas_call(
        body, name=name, grid=(t // tm,),
        in_specs=[row, pl.BlockSpec((1, d), lambda i: (0, 0)), pl.BlockSpec((2 * f, d), lambda i: (0, 0)),
                  pl.BlockSpec((8, f), lambda i: (0, 0)), pl.BlockSpec((1, f), lambda i: (0, 0))],
        out_specs=[pl.BlockSpec((tm, 2 * f), lambda i: (i, 0)), row, row_f, row_f],
        out_shape=[jax.ShapeDtypeStruct((t, 2 * f), BF16), jax.ShapeDtypeStruct((t, d), BF16), jax.ShapeDtypeStruct((t, f), BF16),
                   jax.ShapeDtypeStruct((t, f), BF16)],
        scratch_shapes=[pltpu.VMEM((8, f), F32)], compiler_params=_cp(("arbitrary",)),
    )(x, g, wt, w8, b)


def _ffn_mid_bwd(gu, gc, dxb, w_out, w8, name):
    t, d = dxb.shape
    f = D_FF
    tm, hr = FFN_BT, FFN_HALO
    nt = t // tm
    n = tm + hr

    def body(g_ref, u_ref, un_ref, c_ref, cn_ref, dx_ref, dxn_ref, wo_ref, w_ref, dgu_ref, dw_ref, db_ref):
        i = pl.program_id(0)
        last = i == nt - 1

        @pl.when(i == 0)
        def _():
            dw_ref[...] = jnp.zeros_like(dw_ref)
            db_ref[...] = jnp.zeros_like(db_ref)

        dxe = jnp.concatenate([dx_ref[...], dxn_ref[...]], axis=0)
        row = lax.broadcasted_iota(jnp.int32, (n, 1), 0)
        keep = (row < tm) | jnp.logical_not(last)
        for c in range(0, f, FFN_BC):
            cs = slice(c, c + FFN_BC)
            act, dact = _silu_grad(jnp.concatenate([c_ref[:, cs], cn_ref[:, cs]], axis=0).astype(F32))
            da = _dot(dxe, wo_ref[cs, :], 1, 1)
            up = jnp.concatenate([u_ref[:, cs], un_ref[:, cs]], axis=0).astype(F32)
            dgc = jnp.where(keep, da * up * dact, 0.0)
            nxt = [dgc[:tm], pltpu.roll(dgc, n - 1, 0)[:tm], pltpu.roll(dgc, n - 2, 0)[:tm]]
            dgu_ref[:, f + c:f + c + FFN_BC] = (da[:tm] * act[:tm]).astype(BF16)
            dgu_ref[:, cs] = (nxt[0] * w_ref[2:3, cs] + nxt[1] * w_ref[1:2, cs] + nxt[2] * w_ref[0:1, cs]).astype(BF16)
            gate = g_ref[:, cs].astype(F32)
            db_ref[:, cs] += jnp.sum(nxt[0], axis=0, keepdims=True)
            for s in range(3):
                dw_ref[2 - s:3 - s, cs] += jnp.sum(nxt[s] * gate, axis=0, keepdims=True)

    r = tm // hr
    nxt_blk = lambda i: jnp.minimum((i + 1) * r, t // hr - 1)
    row_f = pl.BlockSpec((tm, f), lambda i: (i, 0))
    halo_f = pl.BlockSpec((hr, f), lambda i: (nxt_blk(i), 0))
    return pl.pallas_call(
        body, name=name, grid=(nt,),
        in_specs=[row_f, pl.BlockSpec((tm, f), lambda i: (i, 1)), pl.BlockSpec((hr, f), lambda i: (nxt_blk(i), 1)),
                  row_f, halo_f,
                  pl.BlockSpec((tm, d), lambda i: (i, 0)), pl.BlockSpec((hr, d), lambda i: (nxt_blk(i), 0)),
                  pl.BlockSpec((f, d), lambda i: (0, 0)), pl.BlockSpec((8, f), lambda i: (0, 0))],
        out_specs=[pl.BlockSpec((tm, 2 * f), lambda i: (i, 0)), pl.BlockSpec((8, f), lambda i: (0, 0)),
                   pl.BlockSpec((1, f), lambda i: (0, 0))],
        out_shape=[jax.ShapeDtypeStruct((t, 2 * f), BF16), jax.ShapeDtypeStruct((8, f), F32), jax.ShapeDtypeStruct((1, f), F32)],
        compiler_params=_cp(("arbitrary",)),
    )(gu, gu, gu, gc, gc, dxb, dxb, w_out, w8)


PRE_TM = 256
PRE_TC = 1024


def _ssm_in_pre(x, g, wt, w8, b, name):
    t, d = x.shape
    tm, tc = PRE_TM, PRE_TC

    def body(x_ref, g_ref, b_ref, w_ref, cb_ref, zx_ref, h_ref, o_ref, c_ref, halo_ref):
        @pl.when(pl.program_id(0) == 0)
        def _():
            halo_ref[...] = jnp.zeros_like(halo_ref)

        xv = x_ref[...]
        r = lax.rsqrt(jnp.mean(xv * xv, axis=-1, keepdims=True) + EPS)
        h = (xv * r * g_ref[...]).astype(BF16)
        h_ref[...] = h
        for c in range(0, ZX, tc):
            v = _dot(h, b_ref[c:c + tc, :], 1, 1)
            zx_ref[:, c:c + tc] = v
            if c >= D_INNER:
                cs = slice(c - D_INNER, c - D_INNER + tc)
                ext = jnp.concatenate([halo_ref[:, cs], v], axis=0)
                conv = cb_ref[:, cs] + v * w_ref[3:4, cs]
                for s in (1, 2, 3):
                    conv = conv + pltpu.roll(ext, s, 0)[8:] * w_ref[3 - s:4 - s, cs]
                o_ref[:, cs] = conv * _sigmoid(conv)
                c_ref[:, cs] = conv.astype(BF16)
                halo_ref[:, cs] = v[tm - 8:]

    row = pl.BlockSpec((tm, d), lambda i: (i, 0))
    row_x = pl.BlockSpec((tm, XBC), lambda i: (i, 0))
    return pl.pallas_call(
        body, name=name, grid=(t // tm,),
        in_specs=[row, pl.BlockSpec((1, d), lambda i: (0, 0)), pl.BlockSpec(wt.shape, lambda i: (0, 0)),
                  pl.BlockSpec((8, XBC), lambda i: (0, 0)), pl.BlockSpec((1, XBC), lambda i: (0, 0))],
        out_specs=[pl.BlockSpec((tm, ZX), lambda i: (i, 0)), row, row_x, row_x],
        out_shape=[jax.ShapeDtypeStruct((t, ZX), F32), jax.ShapeDtypeStruct((t, d), BF16), jax.ShapeDtypeStruct((t, XBC), F32),
                   jax.ShapeDtypeStruct((t, XBC), BF16)],
        scratch_shapes=[pltpu.VMEM((8, XBC), F32)], compiler_params=_cp(("arbitrary",)),
    )(x, g, wt, w8, b)


PRE_HALO = 16


def _ssm_pre_bwd(zx, conv, dxbc, w8, name):
    t = zx.shape[0]
    tm, tc, hr = PRE_TM, PRE_TC, PRE_HALO
    off = D_INNER // tc
    nt = t // tm
    n = tm + hr

    def body(x_ref, c_ref, cn_ref, d_ref, dn_ref, w_ref, o_ref, dw_ref, db_ref):
        i = pl.program_id(1)
        last = i == nt - 1

        @pl.when(i == 0)
        def _():
            dw_ref[...] = jnp.zeros_like(dw_ref)
            db_ref[...] = jnp.zeros_like(db_ref)

        _, dact = _silu_grad(jnp.concatenate([c_ref[...], cn_ref[...]], axis=0).astype(F32))
        row = lax.broadcasted_iota(jnp.int32, (n, 1), 0)
        dc = jnp.where((row < tm) | jnp.logical_not(last), jnp.concatenate([d_ref[...], dn_ref[...]], axis=0) * dact, 0.0)
        nxt = [dc[:tm]] + [pltpu.roll(dc, n - s, 0)[:tm] for s in (1, 2, 3)]
        o_ref[...] = _conv_apply(nxt, w_ref, 4).astype(BF16)
        xv = x_ref[...]
        db_ref[...] += jnp.sum(nxt[0], axis=0, keepdims=True)
        for s in range(4):
            dw_ref[3 - s:4 - s, :] += jnp.sum(nxt[s] * xv, axis=0, keepdims=True)

    nxt_blk = lambda i: jnp.minimum((i + 1) * (tm // hr), t // hr - 1)
    tile = pl.BlockSpec((tm, tc), lambda j, i: (i, j))
    halo = pl.BlockSpec((hr, tc), lambda j, i: (nxt_blk(i), j))
    return pl.pallas_call(
        body, name=name, grid=(XBC // tc, nt),
        in_specs=[pl.BlockSpec((tm, tc), lambda j, i: (i, j + off)), tile, halo, tile, halo,
                  pl.BlockSpec((8, tc), lambda j, i: (0, j))],
        out_specs=[tile, pl.BlockSpec((8, tc), lambda j, i: (0, j)), pl.BlockSpec((1, tc), lambda j, i: (0, j))],
        out_shape=[jax.ShapeDtypeStruct((t, XBC), BF16), jax.ShapeDtypeStruct((8, XBC), F32),
                   jax.ShapeDtypeStruct((1, XBC), F32)],
        compiler_params=_cp(("parallel", "arbitrary")),
    )(zx, conv, conv, dxbc, dxbc, w8)


def _head_lanes():
    return lax.broadcasted_iota(jnp.int32, (1, LANES), 1) < SSM_HEADS


def _dt_fwd(dtraw, bias, name):
    t = dtraw.shape[0]
    tm = _pick(t, (1024, 512, 256, 128))

    def body(x_ref, b_ref, o_ref):
        v = x_ref[...] + b_ref[...]
        sp = jnp.maximum(v, 0.0) + jnp.log(1.0 + jnp.exp(-jnp.abs(v)))
        o_ref[...] = jnp.where(_head_lanes(), sp, 0.0)

    row = pl.BlockSpec((tm, LANES), lambda i: (i, 0))
    return pl.pallas_call(
        body, name=name, grid=(t // tm,), in_specs=[row, pl.BlockSpec((1, LANES), lambda i: (0, 0))], out_specs=row,
        out_shape=jax.ShapeDtypeStruct((t, LANES), F32), compiler_params=_cp(("parallel",)),
    )(dtraw, bias)


def _dt_bwd(dtraw, bias, ddt, name):
    t = dtraw.shape[0]
    tm = _pick(t, (1024, 512, 256, 128))

    def body(x_ref, b_ref, d_ref, o_ref, db_ref):
        @pl.when(pl.program_id(0) == 0)
        def _():
            db_ref[...] = jnp.zeros_like(db_ref)

        g = jnp.where(_head_lanes(), d_ref[...] * _sigmoid(x_ref[...] + b_ref[...]), 0.0)
        o_ref[...] = g.astype(BF16)
        db_ref[...] += jnp.sum(g, axis=0, keepdims=True)

    row = pl.BlockSpec((tm, LANES), lambda i: (i, 0))
    vec = pl.BlockSpec((1, LANES), lambda i: (0, 0))
    return pl.pallas_call(
        body, name=name, grid=(t // tm,), in_specs=[row, vec, row], out_specs=[row, vec],
        out_shape=[jax.ShapeDtypeStruct((t, LANES), BF16), jax.ShapeDtypeStruct((1, LANES), F32)],
        compiler_params=_cp(("arbitrary",)),
    )(dtraw, bias, ddt)


GROUP_W = D_INNER // SSM_GROUPS


def _ssd_common(dt, alog):
    ll = dt.shape[0]
    a_neg = -jnp.exp(alog)
    a = dt * a_neg
    ri = lax.broadcasted_iota(jnp.int32, (ll, ll), 0)
    ci = lax.broadcasted_iota(jnp.int32, (ll, ll), 1)
    tril = ri >= ci
    acs = _dot(tril.astype(F32), a, 1, 0, HI)
    return a_neg, tril, acs, acs.T


def _pair_terms(acs, acs_t, dt, h0, lo):
    ll = acs.shape[0]
    cols = [acs[:, h0 + e:h0 + e + 1] for e in range(2)]
    rows = [acs_t[h0 + e:h0 + e + 1, :] for e in range(2)]
    dtc = [dt[:, h0 + e:h0 + e + 1] for e in range(2)]
    lasts = [c[ll - 1:ll, :] for c in cols]
    dtx = jnp.where(lo, dtc[0], dtc[1])
    eac = jnp.where(lo, jnp.exp(cols[0]), jnp.exp(cols[1]))
    fdec = jnp.where(lo, jnp.exp(lasts[0] - cols[0]), jnp.exp(lasts[1] - cols[1]))
    elast = jnp.where(lo, jnp.exp(lasts[0]), jnp.exp(lasts[1]))
    return cols, rows, dtx, eac, fdec, elast


def _decay(col, row, tril):
    return jnp.where(tril, jnp.exp(jnp.minimum(col - row, 0.0)), 0.0)


def _two_heads_rows(v, lo):
    z = jnp.zeros_like(v)
    return jnp.concatenate([jnp.where(lo, v, z), jnp.where(lo, z, v)], axis=0)


def _two_heads_cols(ms):
    return jnp.concatenate(ms, axis=1)


def _z_group(z_refs, g):
    return z_refs[g // 2][:, GROUP_W * (g % 2):GROUP_W * (g % 2 + 1)]


def _ssd_fwd(xbc, dt, alog, zx, dexp, nw, name, comm=None):
    t = xbc.shape[0]
    ll = SSD_L
    nc = t // ll

    def body(x_ref, dt_ref, al_ref, z0_ref, z1_ref, d_ref, w_ref, y_ref, sp_ref, y4_ref, st_ref):
        @pl.when(pl.program_id(0) == 0)
        def _():
            st_ref[...] = jnp.zeros_like(st_ref)

        dtv = dt_ref[...]
        _, tril, acs, acs_t = _ssd_common(dtv, al_ref[...])
        lo = _lo_mask()
        sp_ref[0] = st_ref[...]
        for g in range(SSM_GROUPS):
            bg = x_ref[:, D_INNER + SSM_STATE * g:D_INNER + SSM_STATE * (g + 1)].astype(BF16)
            cg = x_ref[:, D_INNER + 512 + SSM_STATE * g:D_INNER + 512 + SSM_STATE * (g + 1)].astype(BF16)
            gm = _dot(cg, bg, 1, 1)
            g0 = GROUP_W * g
            terms = [_pair_terms(acs, acs_t, dtv, 8 * g + 2 * pp, lo) for pp in range(4)]
            dtx, eac, fdec, elast = [jnp.concatenate([tt[k] for tt in terms], axis=1) for k in (2, 3, 4, 5)]
            xg = x_ref[:, g0:g0 + GROUP_W]
            ug = (xg * dtx).astype(BF16)
            sg = st_ref[:, g0:g0 + GROUP_W]
            yst = _dot(cg, sg.astype(BF16), 1, 0) * eac
            st_ref[:, g0:g0 + GROUP_W] = sg * elast + _dot(bg, (xg * (fdec * dtx)).astype(BF16), 0, 0)
            ys = []
            for pp in range(4):
                cols, rows = terms[pp][0], terms[pp][1]
                sl = slice(LANES * pp, LANES * (pp + 1))
                y_in = _dot(_two_heads_cols([(gm * _decay(cols[e], rows[e], tril)).astype(BF16) for e in range(2)]),
                            _two_heads_rows(ug[:, sl], lo), 1, 0)
                ys.append(y_in + yst[:, sl])
            yg = jnp.concatenate(ys, axis=1)
            y_ref[:, g0:g0 + GROUP_W] = yg
            zg = _z_group((z0_ref, z1_ref), g)
            y3 = (yg + d_ref[:, g0:g0 + GROUP_W] * xg) * (zg * _sigmoid(zg))
            r = lax.rsqrt(jnp.mean(y3 * y3, axis=-1, keepdims=True) + EPS)
            y4_ref[:, g0:g0 + GROUP_W] = (y3 * r * w_ref[:, g0:g0 + GROUP_W]).astype(BF16)

    zblk = lambda j: pl.BlockSpec((ll, 1024), lambda c: (c, j))
    vec = pl.BlockSpec((1, D_INNER), lambda c: (0, 0))
    row = pl.BlockSpec((ll, D_INNER), lambda c: (c, 0))
    return _call(
        body, name=name, grid=(nc,),
        in_specs=[pl.BlockSpec((ll, XBC), lambda c: (c, 0)), pl.BlockSpec((ll, LANES), lambda c: (c, 0)),
                  pl.BlockSpec((1, LANES), lambda c: (0, 0)), zblk(0), zblk(1), vec, vec],
        out_specs=[row, pl.BlockSpec((1, SSM_STATE, D_INNER), lambda c: (c, 0, 0)), row],
        out_shape=[jax.ShapeDtypeStruct((t, D_INNER), F32), jax.ShapeDtypeStruct((nc, SSM_STATE, D_INNER), F32),
                   jax.ShapeDtypeStruct((t, D_INNER), BF16)],
        scratch_shapes=[pltpu.VMEM((SSM_STATE, D_INNER), F32)],
        args=(xbc, dt, alog, zx, zx, dexp, nw), sem=("arbitrary",), comm=comm)


def _ssd_bwd(xbc, dt, alog, sprev, dy4, y, zx, dexp, nw, name, comm=None):
    t = xbc.shape[0]
    ll = SSD_L
    nc = t // ll

    def body(x_ref, dt_ref, al_ref, sp_ref, g4_ref, y_ref, z0_ref, z1_ref, d_ref, w_ref,
             dx_ref, ddt_ref, dal_ref, dz_ref, dd_ref, dnw_ref, ds_ref, colt_ref):
        @pl.when(pl.program_id(0) == 0)
        def _():
            ds_ref[...] = jnp.zeros_like(ds_ref)
            dal_ref[...] = jnp.zeros_like(dal_ref)
            dd_ref[...] = jnp.zeros_like(dd_ref)
            dnw_ref[...] = jnp.zeros_like(dnw_ref)

        dtv = dt_ref[...]
        a_neg, tril, acs, acs_t = _ssd_common(dtv, al_ref[...])
        lo = _lo_mask()
        hi = jnp.logical_not(lo)
        lane = lax.broadcasted_iota(jnp.int32, (1, LANES), 1)
        colt_ref[...] = jnp.zeros_like(colt_ref)
        rowterm = jnp.zeros((ll, LANES), F32)
        ddt_u = jnp.zeros((ll, LANES), F32)
        dlast = jnp.zeros((1, LANES), F32)

        def halves(v):
            return (jnp.sum(jnp.where(lo, v, 0.0), axis=-1, keepdims=True),
                    jnp.sum(jnp.where(hi, v, 0.0), axis=-1, keepdims=True))

        for g in range(SSM_GROUPS):
            cb0 = D_INNER + SSM_STATE * g
            cc0 = D_INNER + 512 + SSM_STATE * g
            bg = x_ref[:, cb0:cb0 + SSM_STATE].astype(BF16)
            cg = x_ref[:, cc0:cc0 + SSM_STATE].astype(BF16)
            gm = _dot(cg, bg, 1, 1)
            g0 = GROUP_W * g
            terms = [_pair_terms(acs, acs_t, dtv, 8 * g + 2 * pp, lo) for pp in range(4)]
            dtx, eac, fdec, elast = [jnp.concatenate([tt[k] for tt in terms], axis=1) for k in (2, 3, 4, 5)]
            xg = x_ref[:, g0:g0 + GROUP_W]
            u32 = xg * dtx
            ug = u32.astype(BF16)
            zg = _z_group((z0_ref, z1_ref), g)
            dg = d_ref[:, g0:g0 + GROUP_W]
            act, dact = _silu_grad(zg)
            y2 = y_ref[:, g0:g0 + GROUP_W] + dg * xg
            y3 = y2 * act
            rn = lax.rsqrt(jnp.mean(y3 * y3, axis=-1, keepdims=True) + EPS)
            y3n = y3 * rn
            gv = g4_ref[:, g0:g0 + GROUP_W]
            dyn = gv * w_ref[:, g0:g0 + GROUP_W]
            dy3 = rn * (dyn - y3n * jnp.mean(dyn * y3n, axis=-1, keepdims=True))
            dyg = dy3 * act
            dskip = dyg * dg
            dz_ref[:, g0:g0 + GROUP_W] = (dy3 * y2 * dact).astype(BF16)
            dd_ref[:, g0:g0 + GROUP_W] += jnp.sum(dyg * xg, axis=0, keepdims=True)
            dnw_ref[:, g0:g0 + GROUP_W] += jnp.sum(gv * y3n, axis=0, keepdims=True)
            dyb = dyg.astype(BF16)
            spg = sp_ref[0, :, g0:g0 + GROUP_W]
            spb = spg.astype(BF16)
            dsg = ds_ref[:, g0:g0 + GROUP_W]
            dsb = dsg.astype(BF16)
            du_st = _dot(bg, dsb, 1, 0) * fdec
            yst = _dot(cg, spb, 1, 0) * eac
            dye = (dyg * eac).astype(BF16)
            dc_st = _dot(dye, spb, 1, 1)
            db_st = _dot((xg * (fdec * dtx)).astype(BF16), dsb, 1, 1)
            ds_ref[:, g0:g0 + GROUP_W] = dsg * elast + _dot(cg, dye, 0, 0)
            qst_el = du_st * u32
            rq_el = dyg * yst - qst_el
            q_row = jnp.sum(qst_el, axis=0, keepdims=True)
            s_row = jnp.sum(dsg * spg, axis=0, keepdims=True)
            dgm = jnp.zeros((ll, ll), F32)
            for pp in range(4):
                h0 = 8 * g + 2 * pp
                cols, rows = terms[pp][0], terms[pp][1]
                sl = slice(LANES * pp, LANES * (pp + 1))
                decs = [_decay(cols[e], rows[e], tril) for e in range(2)]
                wms = [gm * d for d in decs]
                dum2 = _dot(dyb[:, sl], _two_heads_rows(ug[:, sl], lo), 1, 1)
                du = _dot(jnp.concatenate([wm.astype(BF16) for wm in wms], axis=0),
                          _two_heads_rows(dyb[:, sl], lo), 0, 0) + du_st[:, sl]
                dx_ref[:, g0 + LANES * pp:g0 + LANES * (pp + 1)] = du * dtx[:, sl] + dskip[:, sl]
                ddtu = halves(du * xg[:, sl])
                rq = halves(rq_el[:, sl])
                qs = halves(q_row[:, sl])
                ss = halves(s_row[:, sl])
                for e in range(2):
                    dum = dum2[:, ll * e:ll * (e + 1)]
                    dgm = dgm + dum * decs[e]
                    tm_ = dum * wms[e]
                    oh = lane == (h0 + e)
                    rowterm = rowterm + jnp.where(oh, jnp.sum(tm_, axis=1, keepdims=True) + rq[e], 0.0)
                    ddt_u = ddt_u + jnp.where(oh, ddtu[e], 0.0)
                    dlast = dlast + jnp.where(oh, jnp.exp(cols[e][ll - 1:ll, :]) * ss[e] + qs[e], 0.0)
                    colt_ref[h0 + e:h0 + e + 1, :] = jnp.sum(tm_, axis=0, keepdims=True)
            dgb = dgm.astype(BF16)
            dx_ref[:, cc0:cc0 + SSM_STATE] = _dot(dgb, bg, 1, 0) + dc_st
            dx_ref[:, cb0:cb0 + SSM_STATE] = _dot(dgb, cg, 0, 0) + db_st
        row_io = lax.broadcasted_iota(jnp.int32, (ll, LANES), 0)
        dacs = rowterm - colt_ref[...].T + jnp.where(row_io == ll - 1, dlast, 0.0)
        da = _dot(jnp.logical_not(tril).astype(F32) + jnp.where(
            lax.broadcasted_iota(jnp.int32, (ll, ll), 0) == lax.broadcasted_iota(jnp.int32, (ll, ll), 1), 1.0, 0.0),
            dacs, 1, 0, HI)
        ddt_ref[...] = da * a_neg + ddt_u
        dal_ref[...] += jnp.sum(da * dtv, axis=0, keepdims=True) * a_neg

    rev = lambda c: nc - 1 - c
    row = pl.BlockSpec((ll, D_INNER), lambda c: (rev(c), 0))
    vec = pl.BlockSpec((1, D_INNER), lambda c: (0, 0))
    zblk = lambda j: pl.BlockSpec((ll, 1024), lambda c: (rev(c), j))
    return _call(
        body, name=name, grid=(nc,),
        in_specs=[pl.BlockSpec((ll, XBC), lambda c: (rev(c), 0)), pl.BlockSpec((ll, LANES), lambda c: (rev(c), 0)),
                  pl.BlockSpec((1, LANES), lambda c: (0, 0)),
                  pl.BlockSpec((1, SSM_STATE, D_INNER), lambda c: (rev(c), 0, 0)), row, row, zblk(0), zblk(1), vec, vec],
        out_specs=[pl.BlockSpec((ll, XBC), lambda c: (rev(c), 0)), pl.BlockSpec((ll, LANES), lambda c: (rev(c), 0)),
                   pl.BlockSpec((1, LANES), lambda c: (0, 0)), row, vec, vec],
        out_shape=[jax.ShapeDtypeStruct((t, XBC), F32), jax.ShapeDtypeStruct((t, LANES), F32),
                   jax.ShapeDtypeStruct((1, LANES), F32), jax.ShapeDtypeStruct((t, D_INNER), BF16),
                   jax.ShapeDtypeStruct((1, D_INNER), F32), jax.ShapeDtypeStruct((1, D_INNER), F32)],
        scratch_shapes=[pltpu.VMEM((SSM_STATE, D_INNER), F32), pltpu.VMEM((LANES, ll), F32)],
        args=(xbc, dt, alog, sprev, dy4, y, zx, zx, dexp, nw), sem=("arbitrary",), comm=comm)


def _sum_parts(parts, name):
    nparts, r, c = parts.shape
    tc = _pick(c, (256, 128))

    def body(p_ref, o_ref):
        g = p_ref[0].astype(F32)
        for k in range(1, nparts):
            g = g + p_ref[k].astype(F32)
        o_ref[...] = g

    return pl.pallas_call(
        body, name=name, grid=(c // tc,), in_specs=[pl.BlockSpec((nparts, r, tc), lambda j: (0, 0, j))],
        out_specs=pl.BlockSpec((r, tc), lambda j: (0, j)), out_shape=jax.ShapeDtypeStruct((r, c), F32),
        compiler_params=_cp(("parallel",)),
    )(parts)


def _adamw(parts, w, m, v, name):
    nl, r, c = w.shape
    assert len(parts) == nl
    tr = _pick(r, (256, 128, 64))
    c1 = 1.0 - ADAM_B1 ** ADAM_STEP
    c2 = 1.0 - ADAM_B2 ** ADAM_STEP

    def body(*refs):
        p_refs = refs[:nl]
        w_ref, m_ref, v_ref, g_ref, d_ref, mo_ref, vo_ref = refs[nl:]
        g = None
        for l, p_ref in enumerate(p_refs):
            s = p_ref[0].astype(F32)
            for k in range(1, p_ref.shape[0]):
                s = s + p_ref[k].astype(F32)
            g = s if g is None else jnp.where(pl.program_id(0) == l, s, g)
        mn = ADAM_B1 * m_ref[0] + (1.0 - ADAM_B1) * g
        vn = ADAM_B2 * v_ref[0] + (1.0 - ADAM_B2) * (g * g)
        g_ref[0] = g
        mo_ref[0] = mn
        vo_ref[0] = vn
        d_ref[0] = -ADAM_LR * ((mn / c1) / (jnp.sqrt(vn / c2) + ADAM_EPS) + ADAM_WD * w_ref[0])

    row = pl.BlockSpec((1, tr, c), lambda l, i: (l, i, 0))
    sd = jax.ShapeDtypeStruct((nl, r, c), F32)
    return pl.pallas_call(
        body, name=name, grid=(nl, r // tr),
        in_specs=[pl.BlockSpec((p.shape[0], tr, c), lambda l, i: (0, i, 0)) for p in parts] + [row, row, row],
        out_specs=[row, row, row, row], out_shape=[sd, sd, sd, sd], compiler_params=_cp(("parallel", "parallel")),
    )(*parts, w, m, v)


def _peers():
    mx, my, mc = lax.axis_index("x"), lax.axis_index("y"), lax.axis_index("c")
    me = 4 * mx + 2 * my + mc
    out = []
    for k in range(1, N_DEV):
        px = 1 - mx if k & 4 else mx
        py = 1 - my if k & 2 else my
        pc = 1 - mc if k & 1 else mc
        out.append(((px, py, pc), 4 * px + 2 * py + pc))
    return me, out


class _Comm:
    def __init__(self, arrs, scatters):
        self.arrs, self.scatters, self.n = list(arrs), list(scatters), len(arrs)
        self.specs = [pl.BlockSpec(memory_space=pl.ANY)] * self.n
        self.out_shape = [jax.ShapeDtypeStruct(x.shape if sc else (N_DEV,) + x.shape, x.dtype)
                          for x, sc in zip(self.arrs, self.scatters)]
        np_ = N_DEV - 1
        self.scratch = [pltpu.SemaphoreType.DMA((np_ * self.n,)), pltpu.SemaphoreType.DMA((np_ * self.n,)),
                        pltpu.SemaphoreType.DMA((self.n,))]

    def _copies(self, x_refs, o_refs, sems):
        send_sems, recv_sems, local_sems = sems
        me, peers = _peers()
        np_ = N_DEV - 1
        local, sends, recvs = [], [], []
        for a in range(self.n):
            mine = x_refs[a].at[me] if self.scatters[a] else x_refs[a]
            local.append(pltpu.make_async_copy(mine, o_refs[a].at[me], local_sems.at[a]))
        for k, (dev, idx) in enumerate(peers):
            for a in range(self.n):
                mine = x_refs[a].at[me] if self.scatters[a] else x_refs[a]
                sends.append(pltpu.make_async_remote_copy(
                    src_ref=x_refs[a].at[idx] if self.scatters[a] else x_refs[a], dst_ref=o_refs[a].at[me],
                    send_sem=send_sems.at[a * np_ + k], recv_sem=recv_sems.at[a * np_ + k], device_id=dev, device_id_type=MESH))
                recvs.append(pltpu.make_async_remote_copy(
                    src_ref=mine, dst_ref=o_refs[a].at[idx], send_sem=send_sems.at[a * np_ + k],
                    recv_sem=recv_sems.at[a * np_ + k], device_id=dev, device_id_type=MESH))
        return local, sends, recvs

    def start(self, x_refs, o_refs, sems):
        local, sends, _ = self._copies(x_refs, o_refs, sems)
        for cp in local + sends:
            cp.start()

    def wait(self, x_refs, o_refs, sems):
        local, sends, recvs = self._copies(x_refs, o_refs, sems)
        for cp in recvs:
            cp.wait_recv()
        for cp in sends:
            cp.wait_send()
        for cp in local:
            cp.wait()


class _Gather2(_Comm):
    def __init__(self, arrs):
        super().__init__(arrs, [False] * len(arrs))

    def _plan(self, x_refs, o_refs, sems):
        send_sems, recv_sems, local_sems = sems
        mx, my, mc = lax.axis_index("x"), lax.axis_index("y"), lax.axis_index("c")
        slot = lambda px, py, pc: 4 * px + 2 * py + pc
        sib = (mx, my, 1 - mc)
        chips = [(1 - mx, my), (mx, 1 - my), (1 - mx, 1 - my)]
        np_ = N_DEV - 1
        local, first, passed, arrive_first, arrive_rest = [], [], [], [], []

        def copy(a, k, src, block, to):
            return pltpu.make_async_remote_copy(
                src_ref=src, dst_ref=o_refs[a].at[block], send_sem=send_sems.at[a * np_ + k], recv_sem=recv_sems.at[a * np_ + k],
                device_id=to, device_id_type=MESH)

        for a in range(self.n):
            me = slot(mx, my, mc)
            local.append(pltpu.make_async_copy(x_refs[a], o_refs[a].at[me], local_sems.at[a]))
            first.append(copy(a, 0, x_refs[a], me, sib))
            arrive_rest.append(copy(a, 0, x_refs[a], slot(*sib), sib))
            for j, (cx, cy) in enumerate(chips):
                first.append(copy(a, 1 + j, x_refs[a], me, (cx, cy, mc)))
                arrive_first.append(copy(a, 1 + j, x_refs[a], slot(cx, cy, mc), (cx, cy, mc)))
                passed.append(copy(a, 4 + j, o_refs[a].at[slot(cx, cy, mc)], slot(cx, cy, mc), sib))
                arrive_rest.append(copy(a, 4 + j, x_refs[a], slot(cx, cy, 1 - mc), sib))
        return local, first, passed, arrive_first, arrive_rest

    def start(self, x_refs, o_refs, sems):
        local, first, _, _, _ = self._plan(x_refs, o_refs, sems)
        for cp in local + first:
            cp.start()

    def wait(self, x_refs, o_refs, sems):
        local, first, passed, arrive_first, arrive_rest = self._plan(x_refs, o_refs, sems)
        for arrived, onward in zip(arrive_first, passed):
            arrived.wait_recv()
            onward.start()
        for cp in arrive_rest:
            cp.wait_recv()
        for cp in first + passed:
            cp.wait_send()
        for cp in local:
            cp.wait()


def _call(body, *, name, grid, in_specs, out_specs, out_shape, args, scratch_shapes=(), sem=None, comm=None):
    if comm is None:
        outs = pl.pallas_call(
            body, name=name, grid=grid, in_specs=list(in_specs), out_specs=list(out_specs), out_shape=list(out_shape),
            scratch_shapes=list(scratch_shapes), compiler_params=_cp(sem),
        )(*args)
        return list(outs), []
    n_in, n_out, nc = len(in_specs), len(out_specs), comm.n
    nsteps = 1
    for g in grid:
        nsteps *= g

    def carrier(*refs):
        ins, cin = refs[:n_in], refs[n_in:n_in + nc]
        outs, cout = refs[n_in + nc:n_in + nc + n_out], refs[n_in + nc + n_out:n_in + 2 * nc + n_out]
        rest = refs[n_in + 2 * nc + n_out:]
        scratch, sems = rest[:len(rest) - 3], rest[len(rest) - 3:]
        if nsteps == 1:
            comm.start(cin, cout, sems)
            body(*ins, *outs, *scratch)
            comm.wait(cin, cout, sems)
            return
        step = 0
        for d, g in enumerate(grid):
            step = step * g + pl.program_id(d)

        @pl.when(step == 0)
        def _():
            comm.start(cin, cout, sems)

        body(*ins, *outs, *scratch)

        @pl.when(step == nsteps - 1)
        def _():
            comm.wait(cin, cout, sems)

    outs = pl.pallas_call(
        carrier, name=name, grid=grid, in_specs=list(in_specs) + comm.specs, out_specs=list(out_specs) + comm.specs,
        out_shape=list(out_shape) + comm.out_shape, scratch_shapes=list(scratch_shapes) + comm.scratch,
        compiler_params=_cp(("arbitrary",) * len(grid) if grid else None),
    )(*args, *comm.arrs)
    return list(outs[:n_out]), list(outs[n_out:])


def _exchange(comm, name):
    return _call(lambda *refs: None, name=name, grid=(), in_specs=[], out_specs=[], out_shape=[], args=[], comm=comm)[1]


def _pack(arrs, dtype, lead=()):
    nl = len(lead)
    flat = jnp.concatenate([a.astype(dtype).reshape(lead + (-1,)) for a in arrs], axis=nl)
    n = flat.shape[-1]
    rows = -(-n // (LANES * 8)) * 8
    flat = jnp.pad(flat, [(0, 0)] * nl + [(0, rows * LANES - n)])
    return flat.reshape(lead + (rows, LANES))


def _unpack(flat, shapes, lead=()):
    nl = len(lead)
    flat = flat.reshape(lead + (-1,))
    out, o = [], 0
    for s in shapes:
        n = 1
        for d in s:
            n *= d
        out.append(lax.slice_in_dim(flat, o, o + n, axis=nl).reshape(lead + tuple(s)))
        o += n
    return out


def _join(g, ax):
    return jnp.concatenate([g[d] for d in range(N_DEV)], axis=ax)


def _split(full, ax):
    n = full.shape[ax] // N_DEV
    return jnp.stack([lax.slice_in_dim(full, d * n, (d + 1) * n, axis=ax) for d in range(N_DEV)])


_WEIGHTS = ['norm_mix', 'norm_ffn', 'attn_w_in', 'attn_w_out', 'relpos_table', 'q_norm_a', 'k_norm_a', 'q_norm_b',
            'k_norm_b', 'sinks', 'ssm_w_in', 'ssm_conv_w', 'ssm_conv_b', 'ssm_dt_bias', 'ssm_a_log', 'ssm_d', 'ssm_norm',
            'ssm_w_out', 'ffn_w_in', 'ffn_conv_w', 'ffn_conv_b', 'ffn_w_out']
_SHARD_AX = {'attn_w_in': 2, 'attn_w_out': 1, 'ssm_w_in': 2, 'ssm_conv_w': 2, 'ssm_conv_b': 1, 'ssm_norm': 1,
             'ssm_w_out': 1, 'ffn_w_in': 2, 'ffn_conv_w': 2, 'ffn_w_out': 1}
_BIG = ['attn_w_in', 'attn_w_out', 'ssm_w_in', 'ssm_w_out', 'ffn_w_in', 'ffn_w_out']
_SMALL = ['ssm_conv_w', 'ssm_conv_b', 'ssm_norm', 'ffn_conv_w']
_AX2 = {n: _SHARD_AX[n] - 1 for n in _BIG}
_REPL = [n for n in _WEIGHTS if n not in _SHARD_AX]


def _rows8(w):
    return jnp.pad(w, ((0, 8 - w.shape[0]), (0, 0)))


def _lanes128(v):
    return jnp.pad(v, (0, LANES - v.shape[0])).reshape(1, LANES)


def _band_mask(n_prev, pad):
    cq = jnp.arange(TQ)[:, None] // CHUNK
    ck = jnp.arange(pad + TQ)[None, :] // CHUNK
    return (ck >= cq) & (ck <= cq + n_prev)


def _ffn_fwd(xin, g, w_in_t, w8, cb, tag):
    gu, h, a, gc = _ffn_in_mid(xin, g, w_in_t, w8, cb, f"mm_ffn_in{tag}")
    return a, (h, gu, a, gc)


def _ffn_bwd(dx, dxb, xin, g, w_in_t, w8, w_out, saved, tag):
    h, gu, a, gc = saved
    dw_out = _mm_tn(a, dxb, f"mm_ffn_dwout{tag}")
    dgu, dw8, dcb = _ffn_mid_bwd(gu, gc, dxb, w_out, w8, f"ffn_mid_bwd{tag}")
    dw_in_t = _mm_tn(dgu, h, f"mm_ffn_dwin{tag}")
    dxp, dxpb, dg = _mm_rms_bwd(dgu, w_in_t, 0, None, xin, g, dx, f"mm_ffn_dh{tag}")
    return dxp, dxpb, dg, dw_in_t, dw8[:3], dcb, dw_out


def kernel(x, norm_mix, norm_ffn, attn_w_in, attn_w_out, relpos_table, q_norm_a, k_norm_a, q_norm_b, k_norm_b, sinks, ssm_w_in, ssm_conv_w, ssm_conv_b, ssm_dt_bias, ssm_a_log, ssm_d, ssm_norm, ssm_w_out, ffn_w_in, ffn_conv_w, ffn_conv_b, ffn_w_out, loss_target, m_norm_mix, m_norm_ffn, m_attn_w_in, m_attn_w_out, m_relpos_table, m_q_norm_a, m_k_norm_a, m_q_norm_b, m_k_norm_b, m_sinks, m_ssm_w_in, m_ssm_conv_w, m_ssm_conv_b, m_ssm_dt_bias, m_ssm_a_log, m_ssm_d, m_ssm_norm, m_ssm_w_out, m_ffn_w_in, m_ffn_conv_w, m_ffn_conv_b, m_ffn_w_out, v_norm_mix, v_norm_ffn, v_attn_w_in, v_attn_w_out, v_relpos_table, v_q_norm_a, v_k_norm_a, v_q_norm_b, v_k_norm_b, v_sinks, v_ssm_w_in, v_ssm_conv_w, v_ssm_conv_b, v_ssm_dt_bias, v_ssm_a_log, v_ssm_d, v_ssm_norm, v_ssm_w_out, v_ffn_w_in, v_ffn_conv_w, v_ffn_conv_b, v_ffn_w_out):
    w = dict(norm_mix=norm_mix, norm_ffn=norm_ffn, attn_w_in=attn_w_in, attn_w_out=attn_w_out, relpos_table=relpos_table,
             q_norm_a=q_norm_a, k_norm_a=k_norm_a, q_norm_b=q_norm_b, k_norm_b=k_norm_b, sinks=sinks, ssm_w_in=ssm_w_in,
             ssm_conv_w=ssm_conv_w, ssm_conv_b=ssm_conv_b, ssm_dt_bias=ssm_dt_bias, ssm_a_log=ssm_a_log, ssm_d=ssm_d,
             ssm_norm=ssm_norm, ssm_w_out=ssm_w_out, ffn_w_in=ffn_w_in, ffn_conv_w=ffn_conv_w, ffn_conv_b=ffn_conv_b,
             ffn_w_out=ffn_w_out)
    mom = dict(norm_mix=m_norm_mix, norm_ffn=m_norm_ffn, attn_w_in=m_attn_w_in, attn_w_out=m_attn_w_out,
               relpos_table=m_relpos_table, q_norm_a=m_q_norm_a, k_norm_a=m_k_norm_a, q_norm_b=m_q_norm_b,
               k_norm_b=m_k_norm_b, sinks=m_sinks, ssm_w_in=m_ssm_w_in, ssm_conv_w=m_ssm_conv_w, ssm_conv_b=m_ssm_conv_b,
               ssm_dt_bias=m_ssm_dt_bias, ssm_a_log=m_ssm_a_log, ssm_d=m_ssm_d, ssm_norm=m_ssm_norm, ssm_w_out=m_ssm_w_out,
               ffn_w_in=m_ffn_w_in, ffn_conv_w=m_ffn_conv_w, ffn_conv_b=m_ffn_conv_b, ffn_w_out=m_ffn_w_out)
    var = dict(norm_mix=v_norm_mix, norm_ffn=v_norm_ffn, attn_w_in=v_attn_w_in, attn_w_out=v_attn_w_out,
               relpos_table=v_relpos_table, q_norm_a=v_q_norm_a, k_norm_a=v_k_norm_a, q_norm_b=v_q_norm_b,
               k_norm_b=v_k_norm_b, sinks=v_sinks, ssm_w_in=v_ssm_w_in, ssm_conv_w=v_ssm_conv_w, ssm_conv_b=v_ssm_conv_b,
               ssm_dt_bias=v_ssm_dt_bias, ssm_a_log=v_ssm_a_log, ssm_d=v_ssm_d, ssm_norm=v_ssm_norm, ssm_w_out=v_ssm_w_out,
               ffn_w_in=v_ffn_w_in, ffn_conv_w=v_ffn_conv_w, ffn_conv_b=v_ffn_conv_b, ffn_w_out=v_ffn_w_out)

    def piece(n, l):
        return (w[n][l].T if _AX2[n] == 1 else w[n][l]).astype(BF16)

    def gather_of(names_layers):
        return _Gather2([piece(n, l) for n, l in names_layers])

    def joined(got):
        return [g.reshape(-1, D_MODEL) for g in got]

    first = [('attn_w_in', 0), ('attn_w_out', 0)]
    got = _exchange(_Gather2([piece(n, l) for n, l in first] + [_pack([w[n] for n in _SMALL], F32)]), "gather_attn")
    wt_attn_in, w_attn_out = joined(got[:2])
    full = {}
    for n, g in zip(_SMALL, _unpack(got[2], [w[n].shape for n in _SMALL], lead=(N_DEV,))):
        full[n] = _join(g, _SHARD_AX[n])
    ssm_cw8 = _rows8(full['ssm_conv_w'][0])
    ssm_cb = full['ssm_conv_b']
    ssm_nw = full['ssm_norm']
    ffn_cw8 = [_rows8(full['ffn_conv_w'][l]) for l in range(2)]
    ffn_cb = [ffn_conv_b[l:l + 1] for l in range(2)]

    x0 = x[0]
    target = loss_target[0]
    t = x0.shape[0]

    g_mix0, g_mix1 = norm_mix[0:1], norm_mix[1:2]
    g_ffn0, g_ffn1 = norm_ffn[0:1], norm_ffn[1:2]
    proj, h0 = _rms_mm(x0, g_mix0, wt_attn_in, ATTN_PROJ, "mm_attn_in", F32)
    hn_w = jnp.concatenate([jnp.tile(v, (1, 2)) for v in (q_norm_a, k_norm_a, q_norm_b, k_norm_b)], axis=0)
    qa, kpa, vpa, qb, kpb, vpb = _headnorm_fwd(proj, hn_w, "headnorm")
    table = jnp.pad(relpos_table[0], ((0, 0), (0, REL_W - (2 * MAX_REL + 1))))
    bias_a = jnp.where(_band_mask(A_PREV, PAD_A)[None], jnp.transpose(_relpos_fwd(table, "relpos_bias"), (1, 0, 2)), NEG)
    rel_b = jnp.arange(TQ)[:, None] - (jnp.arange(PAD_B + TQ)[None, :] - PAD_B)
    slopes = 2.0 ** (-8.0 * jnp.arange(1, N_HEADS + 1, dtype=F32) / N_HEADS)
    bias_b = jnp.where(_band_mask(B_PREV, PAD_B)[None], -slopes[:, None, None] * jnp.abs(rel_b).astype(F32)[None], NEG)
    no_sinks = jnp.full((N_HEADS,), NEG, F32)
    ffn0_w, ssm_w, ffn1_w = [('ffn_w_in', 0), ('ffn_w_out', 0)], [('ssm_w_in', 0), ('ssm_w_out', 0)], [('ffn_w_in', 1), ('ffn_w_out', 1)]
    oa, stats_a, got = _attn_fwd(qa, kpa, vpa, bias_a, no_sinks, PAD_A, "attn_a", comm=gather_of(ffn0_w + ssm_w))
    wt_ffn_in0, w_ffn_out0, wt_ssm_in, w_ssm_out = joined(got)
    ob, stats_b, _ = _attn_fwd(qb, kpb, vpb, bias_b, sinks[0], PAD_B, "attn_b")
    wt_ssm_dt = jnp.pad(wt_ssm_in[ZX:], ((0, LANES - SSM_HEADS), (0, 0)))
    x1 = _mm(oa, w_attn_out, "mm_attn_out_a", res=x0, b_rows=(0, D_ATT))
    x1 = _mm(ob, w_attn_out, "mm_attn_out_b", res=x1, b_rows=(D_ATT, D_ATT))
    a0, ffn0_saved = _ffn_fwd(x1, g_ffn0, wt_ffn_in0, ffn_cw8[0], ffn_cb[0], "0")
    x2 = _mm(a0, w_ffn_out0, "mm_ffn_out0", res=x1)

    zx, h2, xbc, conv_pre = _ssm_in_pre(x2, g_mix1, wt_ssm_in, ssm_cw8, ssm_cb, "mm_ssm_in")
    dtraw = _mm(h2, wt_ssm_dt, "mm_ssm_dt", trans_b=True)
    dt_bias = _lanes128(ssm_dt_bias[0])
    alog = _lanes128(ssm_a_log[0])
    dexp = jnp.repeat(ssm_d[0], HEAD_DIM).reshape(1, D_INNER)
    dt = _dt_fwd(dtraw, dt_bias, "ssm_dt")
    (y, sprev, y4), got = _ssd_fwd(xbc, dt, alog, zx, dexp, ssm_nw, "ssd_fwd", comm=gather_of(ffn1_w))
    wt_ffn_in1, w_ffn_out1 = joined(got)
    x3 = _mm(y4, w_ssm_out, "mm_ssm_out", res=x2)
    a1, ffn1_saved = _ffn_fwd(x3, g_ffn1, wt_ffn_in1, ffn_cw8[1], ffn_cb[1], "1")

    dx4, dx4b, sq = _mm_loss(a1, w_ffn_out1, x3, target, "mm_ffn_out1_loss")
    loss = lax.psum(0.5 * jnp.sum(sq) / D_MODEL, ("x", "y", "c"))

    grads = {}

    def scatter_of(grads_2d):
        return _Comm([g.reshape(N_DEV, -1, D_MODEL) for g in grads_2d], [True] * len(grads_2d))

    dx3, dx3b, dg_ffn1, dwtin1, dcw1, dcb1, dwout1 = _ffn_bwd(
        dx4, dx4b, x3, g_ffn1, wt_ffn_in1, ffn_cw8[1], w_ffn_out1, ffn1_saved, "1")

    dy4 = _mm(dx3b, w_ssm_out, "mm_ssm_dy", trans_b=True)
    dw_ssm_out = _mm_tn(y4, dx3b, "mm_ssm_dwout")
    (dxbc, ddt, dalog, dz, dd_lane, dnw), parts_ffn1 = _ssd_bwd(
        xbc, dt, alog, sprev, dy4, y, zx, dexp, ssm_nw, "ssd_bwd", comm=scatter_of([dwtin1, dwout1]))
    dxr, dcw_s, dcb_s = _ssm_pre_bwd(zx, conv_pre, dxbc, ssm_cw8, "ssm_pre_bwd")
    ddtraw, ddtb = _dt_bwd(dtraw, dt_bias, ddt, "ssm_dt_bwd")
    dh2 = _mm(dz, wt_ssm_in, "mm_ssm_dh_z", b_rows=(0, D_INNER))
    dh2 = _mm(dxr, wt_ssm_in[D_INNER:ZX], "mm_ssm_dh_x", res=dh2)
    dwt_ssm_in = jnp.concatenate([
        _mm_tn(dz, h2, "mm_ssm_dwin_z"), _mm_tn(dxr, h2, "mm_ssm_dwin_x"),
        _mm_tn(ddtraw, h2, "mm_ssm_dwin_dt")[:SSM_HEADS]], axis=0)
    dx2, dx2b, dg_mix1 = _mm_rms_bwd(ddtraw, wt_ssm_dt, 0, dh2, x2, g_mix1, dx3, "mm_ssm_dh_dt")
    grads['ssm_conv_w'] = dcw_s[:4][None]
    grads['ssm_conv_b'] = dcb_s
    grads['ssm_norm'] = dnw
    grads['ssm_dt_bias'] = ddtb[:, :SSM_HEADS]
    grads['ssm_a_log'] = dalog[:, :SSM_HEADS]
    grads['ssm_d'] = jnp.sum(dd_lane.reshape(SSM_HEADS, HEAD_DIM), axis=1)[None]

    dx1, dx1b, dg_ffn0, dwtin0, dcw0, dcb0, dwout0 = _ffn_bwd(
        dx2, dx2b, x1, g_ffn0, wt_ffn_in0, ffn_cw8[0], w_ffn_out0, ffn0_saved, "0")
    grads['ffn_conv_w'] = jnp.stack([dcw0, dcw1])
    grads['ffn_conv_b'] = jnp.concatenate([dcb0, dcb1], axis=0)
    grads['norm_ffn'] = jnp.concatenate([dg_ffn0, dg_ffn1], axis=0)

    do = _mm(dx1b, w_attn_out, "mm_attn_do", out_dtype=BF16, trans_b=True)
    dw_attn_out = jnp.concatenate([_mm_tn(oa, dx1b, "mm_attn_dwout_a"), _mm_tn(ob, dx1b, "mm_attn_dwout_b")], axis=0)
    (dqa, dkpa, dvpa, dbias_a, _), parts_ssm = _attn_bwd(
        qa, kpa, vpa, bias_a, no_sinks, do, stats_a, 0, PAD_A, "attn_a_bwd",
        comm=scatter_of([dwt_ssm_in, dw_ssm_out, dw_attn_out]))
    (dqb, dkpb, dvpb, _, dsink), parts_ffn0 = _attn_bwd(
        qb, kpb, vpb, bias_b, sinks[0], do, stats_b, 4, PAD_B, "attn_b_bwd", comm=scatter_of([dwtin0, dwout0]))
    grads['relpos_table'] = _relpos_bwd(jnp.transpose(dbias_a, (1, 0, 2)), "relpos_bwd")[None, :, :2 * MAX_REL + 1]
    grads['sinks'] = dsink[:, :2, 0].reshape(1, N_HEADS)
    dproj, dhn = _headnorm_bwd(proj, hn_w, dqa, dkpa, dvpa, dqb, dkpb, dvpb, "headnorm_bwd")
    dhn = dhn[:, :HEAD_DIM] + dhn[:, HEAD_DIM:]
    for k, n in enumerate(('q_norm_a', 'k_norm_a', 'q_norm_b', 'k_norm_b')):
        grads[n] = dhn[k:k + 1]
    dwt_attn_in = _mm_tn(dproj, h0, "mm_attn_dwin")
    dx0, _, dg_mix0, parts_attn_in = _mm_rms_bwd(dproj, wt_attn_in, 0, None, x0, g_mix0, dx1, "mm_attn_dh",
                                                 comm=scatter_of([dwt_attn_in]))
    grads['norm_mix'] = jnp.concatenate([dg_mix0, dg_mix1], axis=0)

    def summed_t(parts, name):
        return _sum_parts(parts, name).T[None]

    sm_shapes = [w[n].shape for n in _SMALL]
    rp_shapes = [w[n].shape for n in _REPL]
    recv = _exchange(_Comm(
        [_pack([_split(grads[n], _SHARD_AX[n]) for n in _SMALL], F32, lead=(N_DEV,)), _pack([grads[n] for n in _REPL], F32)],
        [True, False]), "exchange_small")
    big_parts = {
        'attn_w_in': [summed_t(parts_attn_in[0], "sum_attn_w_in")], 'attn_w_out': [parts_ssm[2]],
        'ssm_w_in': [summed_t(parts_ssm[0], "sum_ssm_w_in")], 'ssm_w_out': [parts_ssm[1]],
        'ffn_w_in': [summed_t(parts_ffn0[0], "sum_ffn_w_in0"), summed_t(parts_ffn1[0], "sum_ffn_w_in1")],
        'ffn_w_out': [parts_ffn0[1], parts_ffn1[1]],
    }
    res = [{}, {}, {}, {}]
    for n in _BIG:
        for kind, a in enumerate(_adamw(big_parts[n], w[n], mom[n], var[n], f"adamw_{n}")):
            res[kind][n] = a
    for names, shapes, parts in ((_SMALL, sm_shapes, recv[0]), (_REPL, rp_shapes, recv[1])):
        outs = _adamw([parts], _pack([w[n] for n in names], F32)[None], _pack([mom[n] for n in names], F32)[None],
                      _pack([var[n] for n in names], F32)[None], "adamw_" + ("small" if names is _SMALL else "replicated"))
        for kind, flat in enumerate(outs):
            for n, a in zip(names, _unpack(flat[0], shapes)):
                res[kind][n] = a
    return (loss, dx0[None], *[res[0][n] for n in _WEIGHTS], *[res[1][n] for n in _WEIGHTS],
            *[res[2][n] for n in _WEIGHTS], *[res[3][n] for n in _WEIGHTS])
```

```python
import jax
import jax.numpy as jnp
from jax import lax
from jax.experimental import pallas as pl
from jax.experimental.pallas import tpu as pltpu

F32 = jnp.float32
BF16 = jnp.bfloat16
HI = lax.Precision.HIGHEST
MESH = pl.DeviceIdType.MESH
NEG = -1e30

N_DEV = 8
D_MODEL = 1024
EPS = 1e-6
CHUNK = 64
HEAD_DIM = 64
N_HEADS = 8
A_PREV = 8
B_PREV = 2
MAX_REL = 256
TQ = 2 * CHUNK
ATT_SUB = 8
PAD_A = A_PREV * CHUNK
PAD_B = B_PREV * CHUNK
REL_W = PAD_A + TQ
D_ATT = N_HEADS * HEAD_DIM
COL_QA, COL_KA, COL_VA, COL_QB = 0, D_ATT, 2 * D_ATT, 3 * D_ATT
COL_KB, COL_VB = 4 * D_ATT, 4 * D_ATT + 2 * HEAD_DIM
ATTN_PROJ = COL_VB + 2 * HEAD_DIM
D_INNER = 2048
SSM_HEADS = 32
SSM_GROUPS = 4
SSM_STATE = 128
XBC = D_INNER + 2 * SSM_GROUPS * SSM_STATE
ZX = D_INNER + XBC
D_FF = 2816
SSD_L = 128
SSD_L_BWD = 2 * SSD_L
LANES = 128
VMEM_LIMIT = 56 << 20

ADAM_LR, ADAM_B1, ADAM_B2, ADAM_EPS, ADAM_WD, ADAM_STEP = 0.001, 0.9, 0.999, 1e-08, 0.01, 10


def _cp(sem=None):
    return pltpu.CompilerParams(dimension_semantics=sem, vmem_limit_bytes=VMEM_LIMIT)


def _dot(a, b, ca=1, cb=0, prec=None):
    return lax.dot_general(a, b, (((ca,), (cb,)), ((), ())), preferred_element_type=F32, precision=prec)


def _pick(n, cands):
    for c in cands:
        if n % c == 0:
            return c
    return n


def _lo_mask():
    return lax.broadcasted_iota(jnp.int32, (1, LANES), 1) < HEAD_DIM


_TN_CHUNKS = (1408, 1536, 1152, 1024, 512, 256, 128)


TN_MAX_ROWS = 3072
MM_WIDE = 2304


def _mm_tn(a, b, name):
    kdim, m = a.shape
    n = b.shape[1]
    assert b.shape[0] == kdim, (a.shape, b.shape)
    mb = m if m <= TN_MAX_ROWS else m // 2
    tn = _pick(n, _TN_CHUNKS)
    tk = _pick(kdim, (512, 256, 128))
    nk = kdim // tk

    def body(a_ref, b_ref, o_ref, acc):
        k = pl.program_id(1)

        @pl.when(k == 0)
        def _():
            acc[...] = jnp.zeros_like(acc)

        av = a_ref[...]
        for c in range(0, n, tn):
            acc[:, c:c + tn] += _dot(av, b_ref[:, c:c + tn], 0, 0)

        @pl.when(k == nk - 1)
        def _():
            o_ref[...] = acc[...].astype(BF16)

    return pl.pallas_call(
        body, name=name, grid=(m // mb, nk),
        in_specs=[pl.BlockSpec((tk, mb), lambda j, k: (k, j)), pl.BlockSpec((tk, n), lambda j, k: (k, 0))],
        out_specs=pl.BlockSpec((mb, n), lambda j, k: (j, 0)), out_shape=jax.ShapeDtypeStruct((m, n), BF16),
        scratch_shapes=[pltpu.VMEM((mb, n), F32)], compiler_params=_cp(("parallel", "arbitrary")),
    )(a, b)


def _mm(a, b, name, out_dtype=F32, res=None, trans_b=False, b_rows=None):
    m, kdim = a.shape
    if b_rows is None:
        b_rows = (0, b.shape[0])
    off, rows = b_rows
    n = rows if trans_b else b.shape[1]
    assert (b.shape[1] if trans_b else rows) == kdim and off % rows == 0, (a.shape, b.shape, b_rows)
    tn = _pick(n, _TN_CHUNKS)
    tm = _pick(m, (256, 128) if n > MM_WIDE else (512, 256, 128))

    def body(*refs):
        if res is None:
            a_ref, b_ref, o_ref = refs
        else:
            a_ref, b_ref, r_ref, o_ref = refs
        av = a_ref[...]
        for c in range(0, n, tn):
            r = _dot(av, b_ref[c:c + tn, :], 1, 1) if trans_b else _dot(av, b_ref[:, c:c + tn], 1, 0)
            if res is not None:
                r = r + r_ref[:, c:c + tn]
            o_ref[:, c:c + tn] = r.astype(out_dtype)

    in_specs = [pl.BlockSpec((tm, kdim), lambda i: (i, 0)), pl.BlockSpec((rows, b.shape[1]), lambda i: (off // rows, 0))]
    args = [a, b]
    if res is not None:
        in_specs.append(pl.BlockSpec((tm, n), lambda i: (i, 0)))
        args.append(res)
    return pl.pallas_call(
        body, name=name, grid=(m // tm,), in_specs=in_specs, out_specs=pl.BlockSpec((tm, n), lambda i: (i, 0)),
        out_shape=jax.ShapeDtypeStruct((m, n), out_dtype), compiler_params=_cp(("parallel",)),
    )(*args)


def _rms_mm(x, g, bt, n, name, out_dtype):
    t, d = x.shape
    tn = _pick(n, _TN_CHUNKS)
    tm = _pick(t, (256, 128))

    def body(x_ref, g_ref, b_ref, o_ref, h_ref):
        xv = x_ref[...]
        r = lax.rsqrt(jnp.mean(xv * xv, axis=-1, keepdims=True) + EPS)
        h = (xv * r * g_ref[...]).astype(BF16)
        h_ref[...] = h
        for c in range(0, n, tn):
            o_ref[:, c:c + tn] = _dot(h, b_ref[c:c + tn, :], 1, 1).astype(out_dtype)

    row = pl.BlockSpec((tm, d), lambda i: (i, 0))
    return pl.pallas_call(
        body, name=name, grid=(t // tm,),
        in_specs=[row, pl.BlockSpec((1, d), lambda i: (0, 0)), pl.BlockSpec(bt.shape, lambda i: (0, 0))],
        out_specs=[pl.BlockSpec((tm, n), lambda i: (i, 0)), row],
        out_shape=[jax.ShapeDtypeStruct((t, n), out_dtype), jax.ShapeDtypeStruct((t, d), BF16)],
        compiler_params=_cp(("parallel",)),
    )(x, g, bt)


def _mm_rms_bwd(a, b, b_off, dh_prev, x, g, dres, name, comm=None):
    t, d = x.shape
    kdim = a.shape[1]
    assert b_off % kdim == 0 and b.shape[1] == d, (a.shape, b.shape, b_off)
    tm = _pick(t, (256, 128))

    def body(*refs):
        if dh_prev is None:
            a_ref, b_ref, x_ref, g_ref, dr_ref, dx_ref, dxb_ref, dg_ref = refs
            dhv = _dot(a_ref[...], b_ref[...], 1, 0)
        else:
            a_ref, b_ref, p_ref, x_ref, g_ref, dr_ref, dx_ref, dxb_ref, dg_ref = refs
            dhv = _dot(a_ref[...], b_ref[...], 1, 0) + p_ref[...]
        xv = x_ref[...]
        r = lax.rsqrt(jnp.mean(xv * xv, axis=-1, keepdims=True) + EPS)
        xh = xv * r
        dxh = dhv * g_ref[...]
        dx = dr_ref[...] + r * (dxh - xh * jnp.mean(dxh * xh, axis=-1, keepdims=True))
        dx_ref[...] = dx
        dxb_ref[...] = dx.astype(BF16)

        @pl.when(pl.program_id(0) == 0)
        def _():
            dg_ref[...] = jnp.zeros_like(dg_ref)

        dg_ref[...] += jnp.sum(dhv * xh, axis=0, keepdims=True)

    row = pl.BlockSpec((tm, d), lambda i: (i, 0))
    vec = pl.BlockSpec((1, d), lambda i: (0, 0))
    in_specs = [pl.BlockSpec((tm, kdim), lambda i: (i, 0)), pl.BlockSpec((kdim, d), lambda i: (b_off // kdim, 0))]
    args = [a, b]
    if dh_prev is not None:
        in_specs.append(row)
        args.append(dh_prev)
    outs, got = _call(
        body, name=name, grid=(t // tm,), in_specs=in_specs + [row, vec, row], out_specs=[row, row, vec],
        out_shape=[jax.ShapeDtypeStruct((t, d), F32), jax.ShapeDtypeStruct((t, d), BF16), jax.ShapeDtypeStruct((1, d), F32)],
        args=(*args, x, g, dres), sem=("arbitrary",), comm=comm)
    return (*outs, got) if comm is not None else tuple(outs)


def _mm_loss(a, b, res, target, name):
    t, kdim = a.shape
    d = b.shape[1]
    tm = _pick(t, (512, 256, 128))

    def body(a_ref, b_ref, r_ref, t_ref, dy_ref, dyb_ref, acc_ref):
        @pl.when(pl.program_id(0) == 0)
        def _():
            acc_ref[...] = jnp.zeros_like(acc_ref)

        err = _dot(a_ref[...], b_ref[...], 1, 0) + r_ref[...] - t_ref[...]
        dy = err * (1.0 / d)
        dy_ref[...] = dy
        dyb_ref[...] = dy.astype(BF16)
        acc_ref[...] += jnp.sum(err * err, axis=0, keepdims=True)

    row = pl.BlockSpec((tm, d), lambda i: (i, 0))
    vec = pl.BlockSpec((1, d), lambda i: (0, 0))
    return pl.pallas_call(
        body, name=name, grid=(t // tm,),
        in_specs=[pl.BlockSpec((tm, kdim), lambda i: (i, 0)), pl.BlockSpec((kdim, d), lambda i: (0, 0)), row, row],
        out_specs=[row, row, vec],
        out_shape=[jax.ShapeDtypeStruct((t, d), F32), jax.ShapeDtypeStruct((t, d), BF16), jax.ShapeDtypeStruct((1, d), F32)],
        compiler_params=_cp(("arbitrary",)),
    )(a, b, res, target)


def _head_sums(v):
    ri = lax.broadcasted_iota(jnp.int32, (LANES, LANES), 0) // HEAD_DIM
    ci = lax.broadcasted_iota(jnp.int32, (LANES, LANES), 1) // HEAD_DIM
    ones = (ri == ci).astype(BF16)
    hi = v.astype(BF16)
    lo_part = (v - hi.astype(F32)).astype(BF16)
    return _dot(hi, ones, 1, 0) + _dot(lo_part, ones, 1, 0)


def _head_rms(xs):
    r = lax.rsqrt(_head_sums(xs * xs) * (1.0 / HEAD_DIM) + EPS)
    return xs * r, r


def _head_rms_bwd(xs, w, dy):
    xh, r = _head_rms(xs)
    dxh = dy * w
    mm = _head_sums(dxh * xh) * (1.0 / HEAD_DIM)
    return r * (dxh - xh * mm), dy * xh


_QSCALE = HEAD_DIM ** -0.5


def _headnorm_fwd(proj, ws, name):
    t = proj.shape[0]
    tm = TQ
    lead = PAD_A // tm
    leadb = PAD_B // tm

    def body(p_ref, w_ref, qa_ref, ka_ref, va_ref, qb_ref, kb_ref, vb_ref):
        data = pl.program_id(0) >= lead
        lo = _lo_mask()

        def put(ref, c, val):
            ref[:, c:c + val.shape[1]] = jnp.where(data, val, 0.0).astype(BF16)

        def per_query_head(slab):
            other = pltpu.roll(slab, HEAD_DIM, 1)
            e0, e1 = jnp.where(lo, slab, other), jnp.where(lo, other, slab)
            return jnp.concatenate([e0, e0, e1, e1], axis=1)

        for s in range(D_ATT // LANES):
            c = LANES * s
            xh, _ = _head_rms(p_ref[:, COL_QA + c:COL_QA + c + LANES])
            qa_ref[:, c:c + LANES] = (xh * w_ref[0:1, :] * _QSCALE).astype(BF16)
            xh, _ = _head_rms(p_ref[:, COL_KA + c:COL_KA + c + LANES])
            put(ka_ref, c, xh * w_ref[1:2, :])
            xh, _ = _head_rms(p_ref[:, COL_QB + c:COL_QB + c + LANES])
            qb_ref[:, c:c + LANES] = (xh * w_ref[2:3, :] * _QSCALE).astype(BF16)
        put(va_ref, 0, p_ref[:, COL_VA:COL_VA + D_ATT])
        xh, _ = _head_rms(p_ref[:, COL_KB:COL_KB + LANES])
        put(kb_ref, 0, per_query_head(xh * w_ref[3:4, :]))
        put(vb_ref, 0, per_query_head(p_ref[:, COL_VB:COL_VB + LANES]))

    src = lambda i: jnp.maximum(i - lead, 0)
    wide = pl.BlockSpec((tm, D_ATT), lambda i: (src(i), 0))
    pad_a = pl.BlockSpec((tm, D_ATT), lambda i: (i, 0))
    pad_b = pl.BlockSpec((tm, D_ATT), lambda i: (jnp.maximum(i - lead + leadb, 0), 0))
    sd = lambda rows: jax.ShapeDtypeStruct((rows, D_ATT), BF16)
    return pl.pallas_call(
        body, name=name, grid=(t // tm + lead,),
        in_specs=[pl.BlockSpec((tm, ATTN_PROJ), lambda i: (src(i), 0)), pl.BlockSpec((4, LANES), lambda i: (0, 0))],
        out_specs=[wide, pad_a, pad_a, wide, pad_b, pad_b],
        out_shape=[sd(t), sd(t + PAD_A), sd(t + PAD_A), sd(t), sd(t + PAD_B), sd(t + PAD_B)],
        compiler_params=_cp(("arbitrary",)),
    )(proj, ws)


def _headnorm_bwd(proj, ws, dqa, dkpa, dvpa, dqb, dkpb, dvpb, name):
    t = proj.shape[0]
    tm = TQ
    offa, offb = PAD_A // tm, PAD_B // tm

    def body(p_ref, w_ref, dqa_ref, dka_ref, dva_ref, dqb_ref, dkb_ref, dvb_ref, dp_ref, dw_ref):
        i = pl.program_id(0)
        lo = _lo_mask()

        @pl.when(i == 0)
        def _():
            dw_ref[...] = jnp.zeros_like(dw_ref)

        acc = [jnp.zeros((1, LANES), F32) for _ in range(4)]
        for s in range(D_ATT // LANES):
            c = LANES * s
            dx, dwl = _head_rms_bwd(p_ref[:, COL_QA + c:COL_QA + c + LANES], w_ref[0:1, :], dqa_ref[:, c:c + LANES] * _QSCALE)
            dp_ref[:, COL_QA + c:COL_QA + c + LANES] = dx.astype(BF16)
            acc[0] += jnp.sum(dwl, axis=0, keepdims=True)
            dx, dwl = _head_rms_bwd(p_ref[:, COL_KA + c:COL_KA + c + LANES], w_ref[1:2, :], dka_ref[:, c:c + LANES])
            dp_ref[:, COL_KA + c:COL_KA + c + LANES] = dx.astype(BF16)
            acc[1] += jnp.sum(dwl, axis=0, keepdims=True)
            dx, dwl = _head_rms_bwd(p_ref[:, COL_QB + c:COL_QB + c + LANES], w_ref[2:3, :], dqb_ref[:, c:c + LANES] * _QSCALE)
            dp_ref[:, COL_QB + c:COL_QB + c + LANES] = dx.astype(BF16)
            acc[2] += jnp.sum(dwl, axis=0, keepdims=True)
        dp_ref[:, COL_VA:COL_VA + D_ATT] = dva_ref[...].astype(BF16)

        def group_sum(ref):
            s0 = ref[:, 0:LANES] + ref[:, LANES:2 * LANES]
            s1 = ref[:, 2 * LANES:3 * LANES] + ref[:, 3 * LANES:4 * LANES]
            s0 = s0 + pltpu.roll(s0, HEAD_DIM, 1)
            s1 = s1 + pltpu.roll(s1, HEAD_DIM, 1)
            return jnp.where(lo, s0, s1)

        dx, dwl = _head_rms_bwd(p_ref[:, COL_KB:COL_KB + LANES], w_ref[3:4, :], group_sum(dkb_ref))
        dp_ref[:, COL_KB:COL_KB + LANES] = dx.astype(BF16)
        acc[3] += jnp.sum(dwl, axis=0, keepdims=True)
        dp_ref[:, COL_VB:COL_VB + LANES] = group_sum(dvb_ref).astype(BF16)
        for n in range(4):
            dw_ref[n:n + 1, :] += acc[n]

    wide = pl.BlockSpec((tm, D_ATT), lambda i: (i, 0))
    pa = pl.BlockSpec((tm, D_ATT), lambda i: (i + offa, 0))
    pb = pl.BlockSpec((tm, D_ATT), lambda i: (i + offb, 0))
    whole = pl.BlockSpec((tm, ATTN_PROJ), lambda i: (i, 0))
    return pl.pallas_call(
        body, name=name, grid=(t // tm,),
        in_specs=[whole, pl.BlockSpec((4, LANES), lambda i: (0, 0)), wide, pa, pa, wide, pb, pb],
        out_specs=[whole, pl.BlockSpec((4, LANES), lambda i: (0, 0))],
        out_shape=[jax.ShapeDtypeStruct((t, ATTN_PROJ), BF16), jax.ShapeDtypeStruct((4, LANES), F32)],
        compiler_params=_cp(("arbitrary",)),
    )(proj, ws, dqa, dkpa, dvpa, dqb, dkpb, dvpb)


ROLL_W = 1024


def _rel_onehot():
    r_io = lax.broadcasted_iota(jnp.int32, (REL_W, ROLL_W), 0)
    m_io = lax.broadcasted_iota(jnp.int32, (REL_W, ROLL_W), 1)
    return (r_io == jnp.clip(REL_W - 1 - m_io, -MAX_REL, MAX_REL) + MAX_REL).astype(F32)


def _relpos_fwd(table, name):
    def body(t_ref, o_ref):
        rr = _dot(t_ref[...], _rel_onehot(), 1, 0, HI)

        def step(q, c):
            o_ref[q] = pltpu.roll(rr, (ROLL_W - (TQ - 1) + q) % ROLL_W, 1)[:, :REL_W]
            return c

        lax.fori_loop(0, TQ, step, 0)

    return pl.pallas_call(
        body, name=name, out_shape=jax.ShapeDtypeStruct((TQ, N_HEADS, REL_W), F32),
        in_specs=[pl.BlockSpec(memory_space=pltpu.VMEM)], out_specs=pl.BlockSpec(memory_space=pltpu.VMEM),
        compiler_params=_cp(),
    )(table)


def _relpos_bwd(dbias_t, name):
    def body(d_ref, o_ref):
        def step(q, acc):
            row = jnp.concatenate([d_ref[q], jnp.zeros((N_HEADS, ROLL_W - REL_W), F32)], axis=1)
            return acc + pltpu.roll(row, TQ - 1 - q, 1)

        drr = lax.fori_loop(0, TQ, step, jnp.zeros((N_HEADS, ROLL_W), F32))
        o_ref[...] = _dot(drr, _rel_onehot(), 1, 1, HI)

    return pl.pallas_call(
        body, name=name, out_shape=jax.ShapeDtypeStruct((N_HEADS, REL_W), F32),
        in_specs=[pl.BlockSpec(memory_space=pltpu.VMEM)], out_specs=pl.BlockSpec(memory_space=pltpu.VMEM),
        compiler_params=_cp(),
    )(dbias_t)


def _attn_scores(qe, kw, bias, kvalid):
    return jnp.where(kvalid, _dot(qe, kw, 1, 1) + bias, NEG)


def _stat_cols(stats, e):
    return stats[:, 64 * e:64 * e + 1], stats[:, 64 * e + 32:64 * e + 33]


def _attn_fwd(q, kp, vp, bias, sinks, pad, name, comm=None):
    t, hd = q.shape
    w = pad + TQ

    def body(sink_ref, q_ref, k_ref, v_ref, b_ref, o_ref, st_ref):
        hp, i = pl.program_id(0), pl.program_id(1)
        lo = _lo_mask()
        lane = lax.broadcasted_iota(jnp.int32, (1, LANES), 1)
        for j in range(ATT_SUB):
            start = pl.multiple_of((i * ATT_SUB + j) * TQ, TQ)
            qv = q_ref[TQ * j:TQ * (j + 1), :]
            kw = k_ref[pl.ds(start, w), :]
            vw = v_ref[pl.ds(start, w), :]
            kvalid = (start + lax.broadcasted_iota(jnp.int32, (1, w), 1)) >= pad
            outs, ms, ls = [], [], []
            for e in range(2):
                sel = lo if e == 0 else jnp.logical_not(lo)
                qe = jnp.where(sel, qv, jnp.zeros_like(qv))
                snk = sink_ref[2 * hp + e]
                s = _attn_scores(qe, kw, b_ref[e], kvalid)
                m = jnp.maximum(jnp.max(s, axis=-1, keepdims=True), snk)
                acc = _dot(jnp.exp(s - m).astype(BF16), jnp.where(sel, vw, jnp.ones_like(vw)), 1, 0)
                denom = acc[:, 64 * (1 - e):64 * (1 - e) + 1] + jnp.exp(snk - m)
                outs.append(acc * (1.0 / denom))
                ms.append(m)
                ls.append(denom)
            o_ref[TQ * j:TQ * (j + 1), :] = jnp.where(lo, outs[0], outs[1]).astype(BF16)
            st_ref[TQ * j:TQ * (j + 1), :] = jnp.where(lane < 32, ms[0], jnp.where(lane < 64, ls[0],
                                                                                 jnp.where(lane < 96, ms[1], ls[1])))

    full = pl.BlockSpec((t + pad, LANES), lambda h, i: (0, h))
    tile = pl.BlockSpec((ATT_SUB * TQ, LANES), lambda h, i: (i, h))
    (o, stats), got = _call(
        body, name=name, grid=(hd // LANES, t // (ATT_SUB * TQ)),
        in_specs=[pl.BlockSpec(memory_space=pltpu.SMEM), tile, full, full, pl.BlockSpec((2, TQ, w), lambda h, i: (h, 0, 0))],
        out_specs=[tile, tile], out_shape=[jax.ShapeDtypeStruct((t, hd), BF16), jax.ShapeDtypeStruct((t, hd), F32)],
        args=(sinks, q, kp, vp, bias), sem=("parallel", "arbitrary"), comm=comm)
    return o, stats, got


def _attn_bwd(q, kp, vp, bias, sinks, do, stats, col_off, pad, name, comm=None):
    t, hd = q.shape
    w = pad + TQ
    nhp = hd // LANES

    def body(sink_ref, q_ref, k_ref, v_ref, b_ref, do_ref, st_ref, dq_ref, dk_ref, dv_ref, db_ref, ds_ref):
        hp, i = pl.program_id(0), pl.program_id(1)

        @pl.when(i == 0)
        def _():
            dk_ref[...] = jnp.zeros_like(dk_ref)
            dv_ref[...] = jnp.zeros_like(dv_ref)
            db_ref[...] = jnp.zeros_like(db_ref)
            ds_ref[...] = jnp.zeros_like(ds_ref)

        lo = _lo_mask()
        row8 = lax.broadcasted_iota(jnp.int32, (8, LANES), 0)
        dbias = [None, None]
        dsink = jnp.zeros((8, LANES), F32)
        for j in range(ATT_SUB):
            start = pl.multiple_of((i * ATT_SUB + j) * TQ, TQ)
            qv = q_ref[TQ * j:TQ * (j + 1), :]
            dov = do_ref[TQ * j:TQ * (j + 1), :]
            kw = k_ref[pl.ds(start, w), :]
            vw = v_ref[pl.ds(start, w), :]
            kvalid = (start + lax.broadcasted_iota(jnp.int32, (1, w), 1)) >= pad
            stats = st_ref[TQ * j:TQ * (j + 1), :]
            dqs, dkw, dvw = [], None, None
            for e in range(2):
                sel = lo if e == 0 else jnp.logical_not(lo)
                qe = jnp.where(sel, qv, jnp.zeros_like(qv))
                doe = jnp.where(sel, dov, jnp.zeros_like(dov))
                m, denom = _stat_cols(stats, e)
                inv = 1.0 / denom
                p = jnp.exp(_attn_scores(qe, kw, b_ref[e], kvalid) - m) * inv
                psink = jnp.exp(sink_ref[2 * hp + e] - m) * inv
                dp = _dot(doe, vw, 1, 1)
                delta = jnp.sum(p * dp, axis=-1, keepdims=True)
                ds = p * (dp - delta)
                dbias[e] = ds if dbias[e] is None else dbias[e] + ds
                dsink = dsink + jnp.where(row8 == e, jnp.sum(-psink * delta, axis=0, keepdims=True), 0.0)
                dsb = ds.astype(BF16)
                dqs.append(_dot(dsb, kw, 1, 0))
                dk_e = _dot(dsb, qe, 0, 0)
                dv_e = _dot(p.astype(BF16), doe, 0, 0)
                dkw = dk_e if dkw is None else dkw + dk_e
                dvw = dv_e if dvw is None else dvw + dv_e
            dq_ref[TQ * j:TQ * (j + 1), :] = jnp.where(lo, dqs[0], dqs[1])
            dk_ref[pl.ds(start, w), :] += dkw
            dv_ref[pl.ds(start, w), :] += dvw
        for e in range(2):
            db_ref[e] += dbias[e]
        ds_ref[0] += dsink

    full = pl.BlockSpec((t + pad, LANES), lambda h, i: (0, h))
    tile = pl.BlockSpec((ATT_SUB * TQ, LANES), lambda h, i: (i, h))
    btile = pl.BlockSpec((2, TQ, w), lambda h, i: (h, 0, 0))
    return _call(
        body, name=name, grid=(nhp, t // (ATT_SUB * TQ)),
        in_specs=[pl.BlockSpec(memory_space=pltpu.SMEM), tile, full, full, btile,
                  pl.BlockSpec((ATT_SUB * TQ, LANES), lambda h, i: (i, h + col_off)), tile],
        out_specs=[tile, full, full, btile, pl.BlockSpec((1, 8, LANES), lambda h, i: (h, 0, 0))],
        out_shape=[jax.ShapeDtypeStruct((t, hd), F32), jax.ShapeDtypeStruct((t + pad, hd), F32),
                   jax.ShapeDtypeStruct((t + pad, hd), F32), jax.ShapeDtypeStruct((N_HEADS, TQ, w), F32),
                   jax.ShapeDtypeStruct((nhp, 8, LANES), F32)],
        args=(sinks, q, kp, vp, bias, do, stats), sem=("parallel", "arbitrary"), comm=comm)


def _conv_apply(taps, w_ref, ktaps):
    out = taps[0] * w_ref[ktaps - 1:ktaps, :]
    for s in range(1, ktaps):
        out = out + taps[s] * w_ref[ktaps - 1 - s:ktaps - s, :]
    return out


def _sigmoid(x):
    return jax.nn.sigmoid(x)


def _silu_grad(x):
    sg = _sigmoid(x)
    return x * sg, sg * (1.0 + x * (1.0 - sg))


FFN_HALO = 16
FFN_BT = 256
FFN_BC = 1408


def _ffn_in_mid(x, g, wt, w8, b, name):
    t, d = x.shape
    f = D_FF
    tm = FFN_BT

    def body(x_ref, g_ref, b_ref, w_ref, cb_ref, gu_ref, h_ref, a_ref, gc_ref, halo_ref):
        @pl.when(pl.program_id(0) == 0)
        def _():
            halo_ref[...] = jnp.zeros_like(halo_ref)

        xv = x_ref[...]
        r = lax.rsqrt(jnp.mean(xv * xv, axis=-1, keepdims=True) + EPS)
        h = (xv * r * g_ref[...]).astype(BF16)
        h_ref[...] = h
        for c in range(0, f, FFN_BC):
            cs = slice(c, c + FFN_BC)
            gate = _dot(h, b_ref[c:c + FFN_BC, :], 1, 1).astype(BF16)
            up = _dot(h, b_ref[f + c:f + c + FFN_BC, :], 1, 1).astype(BF16)
            gu_ref[:, cs] = gate
            gu_ref[:, f + c:f + c + FFN_BC] = up
            gf = gate.astype(F32)
            ext = jnp.concatenate([halo_ref[:, cs], gf], axis=0)
            gc = (cb_ref[:, cs] + gf * w_ref[2:3, cs] + pltpu.roll(ext, 1, 0)[8:] * w_ref[1:2, cs]
                  + pltpu.roll(ext, 2, 0)[8:] * w_ref[0:1, cs])
            a_ref[:, cs] = (gc * _sigmoid(gc) * up.astype(F32)).astype(BF16)
            gc_ref[:, cs] = gc.astype(BF16)
            halo_ref[:, cs] = gf[tm - 8:]

    row = pl.BlockSpec((tm, d), lambda i: (i, 0))
    row_f = pl.BlockSpec((tm, f), lambda i: (i, 0))
    return pl.pallas_call(
        body, name=name, grid=(t // tm,),
        in_specs=[row, pl.BlockSpec((1, d), lambda i: (0, 0)), pl.BlockSpec((2 * f, d), lambda i: (0, 0)),
                  pl.BlockSpec((8, f), lambda i: (0, 0)), pl.BlockSpec((1, f), lambda i: (0, 0))],
        out_specs=[pl.BlockSpec((tm, 2 * f), lambda i: (i, 0)), row, row_f, row_f],
        out_shape=[jax.ShapeDtypeStruct((t, 2 * f), BF16), jax.ShapeDtypeStruct((t, d), BF16), jax.ShapeDtypeStruct((t, f), BF16),
                   jax.ShapeDtypeStruct((t, f), BF16)],
        scratch_shapes=[pltpu.VMEM((8, f), F32)], compiler_params=_cp(("arbitrary",)),
    )(x, g, wt, w8, b)


def _ffn_mid_bwd(gu, gc, dxb, w_out, w8, name):
    t, d = dxb.shape
    f = D_FF
    tm, hr = FFN_BT, FFN_HALO
    nt = t // tm
    n = tm + hr

    def body(g_ref, u_ref, un_ref, c_ref, cn_ref, dx_ref, dxn_ref, wo_ref, w_ref, dgu_ref, dw_ref, db_ref):
        i = pl.program_id(0)
        last = i == nt - 1

        @pl.when(i == 0)
        def _():
            dw_ref[...] = jnp.zeros_like(dw_ref)
            db_ref[...] = jnp.zeros_like(db_ref)

        dxe = jnp.concatenate([dx_ref[...], dxn_ref[...]], axis=0)
        row = lax.broadcasted_iota(jnp.int32, (n, 1), 0)
        keep = (row < tm) | jnp.logical_not(last)
        for c in range(0, f, FFN_BC):
            cs = slice(c, c + FFN_BC)
            act, dact = _silu_grad(jnp.concatenate([c_ref[:, cs], cn_ref[:, cs]], axis=0).astype(F32))
            da = _dot(dxe, wo_ref[cs, :], 1, 1)
            up = jnp.concatenate([u_ref[:, cs], un_ref[:, cs]], axis=0).astype(F32)
            dgc = jnp.where(keep, da * up * dact, 0.0)
            nxt = [dgc[:tm], pltpu.roll(dgc, n - 1, 0)[:tm], pltpu.roll(dgc, n - 2, 0)[:tm]]
            dgu_ref[:, f + c:f + c + FFN_BC] = (da[:tm] * act[:tm]).astype(BF16)
            dgu_ref[:, cs] = (nxt[0] * w_ref[2:3, cs] + nxt[1] * w_ref[1:2, cs] + nxt[2] * w_ref[0:1, cs]).astype(BF16)
            gate = g_ref[:, cs].astype(F32)
            db_ref[:, cs] += jnp.sum(nxt[0], axis=0, keepdims=True)
            for s in range(3):
                dw_ref[2 - s:3 - s, cs] += jnp.sum(nxt[s] * gate, axis=0, keepdims=True)

    r = tm // hr
    nxt_blk = lambda i: jnp.minimum((i + 1) * r, t // hr - 1)
    row_f = pl.BlockSpec((tm, f), lambda i: (i, 0))
    halo_f = pl.BlockSpec((hr, f), lambda i: (nxt_blk(i), 0))
    return pl.pallas_call(
        body, name=name, grid=(nt,),
        in_specs=[row_f, pl.BlockSpec((tm, f), lambda i: (i, 1)), pl.BlockSpec((hr, f), lambda i: (nxt_blk(i), 1)),
                  row_f, halo_f,
                  pl.BlockSpec((tm, d), lambda i: (i, 0)), pl.BlockSpec((hr, d), lambda i: (nxt_blk(i), 0)),
                  pl.BlockSpec((f, d), lambda i: (0, 0)), pl.BlockSpec((8, f), lambda i: (0, 0))],
        out_specs=[pl.BlockSpec((tm, 2 * f), lambda i: (i, 0)), pl.BlockSpec((8, f), lambda i: (0, 0)),
                   pl.BlockSpec((1, f), lambda i: (0, 0))],
        out_shape=[jax.ShapeDtypeStruct((t, 2 * f), BF16), jax.ShapeDtypeStruct((8, f), F32), jax.ShapeDtypeStruct((1, f), F32)],
        compiler_params=_cp(("arbitrary",)),
    )(gu, gu, gu, gc, gc, dxb, dxb, w_out, w8)


PRE_TM = 256
PRE_TC = 1024


def _ssm_in_pre(x, g, wt, w8, b, name):
    t, d = x.shape
    tm, tc = PRE_TM, PRE_TC

    def body(x_ref, g_ref, b_ref, w_ref, cb_ref, zx_ref, h_ref, o_ref, c_ref, halo_ref):
        @pl.when(pl.program_id(0) == 0)
        def _():
            halo_ref[...] = jnp.zeros_like(halo_ref)

        xv = x_ref[...]
        r = lax.rsqrt(jnp.mean(xv * xv, axis=-1, keepdims=True) + EPS)
        h = (xv * r * g_ref[...]).astype(BF16)
        h_ref[...] = h
        for c in range(0, ZX, tc):
            v = _dot(h, b_ref[c:c + tc, :], 1, 1)
            zx_ref[:, c:c + tc] = v
            if c >= D_INNER:
                cs = slice(c - D_INNER, c - D_INNER + tc)
                ext = jnp.concatenate([halo_ref[:, cs], v], axis=0)
                conv = cb_ref[:, cs] + v * w_ref[3:4, cs]
                for s in (1, 2, 3):
                    conv = conv + pltpu.roll(ext, s, 0)[8:] * w_ref[3 - s:4 - s, cs]
                o_ref[:, cs] = conv * _sigmoid(conv)
                c_ref[:, cs] = conv.astype(BF16)
                halo_ref[:, cs] = v[tm - 8:]

    row = pl.BlockSpec((tm, d), lambda i: (i, 0))
    row_x = pl.BlockSpec((tm, XBC), lambda i: (i, 0))
    return pl.pallas_call(
        body, name=name, grid=(t // tm,),
        in_specs=[row, pl.BlockSpec((1, d), lambda i: (0, 0)), pl.BlockSpec(wt.shape, lambda i: (0, 0)),
                  pl.BlockSpec((8, XBC), lambda i: (0, 0)), pl.BlockSpec((1, XBC), lambda i: (0, 0))],
        out_specs=[pl.BlockSpec((tm, ZX), lambda i: (i, 0)), row, row_x, row_x],
        out_shape=[jax.ShapeDtypeStruct((t, ZX), F32), jax.ShapeDtypeStruct((t, d), BF16), jax.ShapeDtypeStruct((t, XBC), F32),
                   jax.ShapeDtypeStruct((t, XBC), BF16)],
        scratch_shapes=[pltpu.VMEM((8, XBC), F32)], compiler_params=_cp(("arbitrary",)),
    )(x, g, wt, w8, b)


PRE_HALO = 16


def _ssm_pre_bwd(zx, conv, dxbc, w8, name):
    t = zx.shape[0]
    tm, tc, hr = PRE_TM, PRE_TC, PRE_HALO
    off = D_INNER // tc
    nt = t // tm
    n = tm + hr

    def body(x_ref, c_ref, cn_ref, d_ref, dn_ref, w_ref, o_ref, dw_ref, db_ref):
        i = pl.program_id(1)
        last = i == nt - 1

        @pl.when(i == 0)
        def _():
            dw_ref[...] = jnp.zeros_like(dw_ref)
            db_ref[...] = jnp.zeros_like(db_ref)

        _, dact = _silu_grad(jnp.concatenate([c_ref[...], cn_ref[...]], axis=0).astype(F32))
        row = lax.broadcasted_iota(jnp.int32, (n, 1), 0)
        dc = jnp.where((row < tm) | jnp.logical_not(last), jnp.concatenate([d_ref[...], dn_ref[...]], axis=0) * dact, 0.0)
        nxt = [dc[:tm]] + [pltpu.roll(dc, n - s, 0)[:tm] for s in (1, 2, 3)]
        o_ref[...] = _conv_apply(nxt, w_ref, 4).astype(BF16)
        xv = x_ref[...]
        db_ref[...] += jnp.sum(nxt[0], axis=0, keepdims=True)
        for s in range(4):
            dw_ref[3 - s:4 - s, :] += jnp.sum(nxt[s] * xv, axis=0, keepdims=True)

    nxt_blk = lambda i: jnp.minimum((i + 1) * (tm // hr), t // hr - 1)
    tile = pl.BlockSpec((tm, tc), lambda j, i: (i, j))
    halo = pl.BlockSpec((hr, tc), lambda j, i: (nxt_blk(i), j))
    return pl.pallas_call(
        body, name=name, grid=(XBC // tc, nt),
        in_specs=[pl.BlockSpec((tm, tc), lambda j, i: (i, j + off)), tile, halo, tile, halo,
                  pl.BlockSpec((8, tc), lambda j, i: (0, j))],
        out_specs=[tile, pl.BlockSpec((8, tc), lambda j, i: (0, j)), pl.BlockSpec((1, tc), lambda j, i: (0, j))],
        out_shape=[jax.ShapeDtypeStruct((t, XBC), BF16), jax.ShapeDtypeStruct((8, XBC), F32),
                   jax.ShapeDtypeStruct((1, XBC), F32)],
        compiler_params=_cp(("parallel", "arbitrary")),
    )(zx, conv, conv, dxbc, dxbc, w8)


def _head_lanes():
    return lax.broadcasted_iota(jnp.int32, (1, LANES), 1) < SSM_HEADS


def _dt_fwd(dtraw, bias, name):
    t = dtraw.shape[0]
    tm = _pick(t, (1024, 512, 256, 128))

    def body(x_ref, b_ref, o_ref):
        v = x_ref[...] + b_ref[...]
        sp = jnp.maximum(v, 0.0) + jnp.log(1.0 + jnp.exp(-jnp.abs(v)))
        o_ref[...] = jnp.where(_head_lanes(), sp, 0.0)

    row = pl.BlockSpec((tm, LANES), lambda i: (i, 0))
    return pl.pallas_call(
        body, name=name, grid=(t // tm,), in_specs=[row, pl.BlockSpec((1, LANES), lambda i: (0, 0))], out_specs=row,
        out_shape=jax.ShapeDtypeStruct((t, LANES), F32), compiler_params=_cp(("parallel",)),
    )(dtraw, bias)


def _dt_bwd(dtraw, bias, ddt, name):
    t = dtraw.shape[0]
    tm = _pick(t, (1024, 512, 256, 128))

    def body(x_ref, b_ref, d_ref, o_ref, db_ref):
        @pl.when(pl.program_id(0) == 0)
        def _():
            db_ref[...] = jnp.zeros_like(db_ref)

        g = jnp.where(_head_lanes(), d_ref[...] * _sigmoid(x_ref[...] + b_ref[...]), 0.0)
        o_ref[...] = g.astype(BF16)
        db_ref[...] += jnp.sum(g, axis=0, keepdims=True)

    row = pl.BlockSpec((tm, LANES), lambda i: (i, 0))
    vec = pl.BlockSpec((1, LANES), lambda i: (0, 0))
    return pl.pallas_call(
        body, name=name, grid=(t // tm,), in_specs=[row, vec, row], out_specs=[row, vec],
        out_shape=[jax.ShapeDtypeStruct((t, LANES), BF16), jax.ShapeDtypeStruct((1, LANES), F32)],
        compiler_params=_cp(("arbitrary",)),
    )(dtraw, bias, ddt)


GROUP_W = D_INNER // SSM_GROUPS


def _ssd_common(dt, alog):
    ll = dt.shape[0]
    a_neg = -jnp.exp(alog)
    a = dt * a_neg
    ri = lax.broadcasted_iota(jnp.int32, (ll, ll), 0)
    ci = lax.broadcasted_iota(jnp.int32, (ll, ll), 1)
    tril = ri >= ci
    acs = _dot(tril.astype(F32), a, 1, 0, HI)
    return a_neg, tril, acs, acs.T


def _pair_terms(acs, acs_t, dt, h0, lo):
    ll = acs.shape[0]
    cols = [acs[:, h0 + e:h0 + e + 1] for e in range(2)]
    rows = [acs_t[h0 + e:h0 + e + 1, :] for e in range(2)]
    dtc = [dt[:, h0 + e:h0 + e + 1] for e in range(2)]
    lasts = [c[ll - 1:ll, :] for c in cols]
    dtx = jnp.where(lo, dtc[0], dtc[1])
    eac = jnp.where(lo, jnp.exp(cols[0]), jnp.exp(cols[1]))
    fdec = jnp.where(lo, jnp.exp(lasts[0] - cols[0]), jnp.exp(lasts[1] - cols[1]))
    elast = jnp.where(lo, jnp.exp(lasts[0]), jnp.exp(lasts[1]))
    return cols, rows, dtx, eac, fdec, elast


def _decay(col, row, tril):
    return jnp.where(tril, jnp.exp(jnp.minimum(col - row, 0.0)), 0.0)


def _two_heads_rows(v, lo):
    z = jnp.zeros_like(v)
    return jnp.concatenate([jnp.where(lo, v, z), jnp.where(lo, z, v)], axis=0)


def _two_heads_cols(ms):
    return jnp.concatenate(ms, axis=1)


def _z_group(z_refs, g):
    return z_refs[g // 2][:, GROUP_W * (g % 2):GROUP_W * (g % 2 + 1)]


def _ssd_fwd(xbc, dt, alog, zx, dexp, nw, name, comm=None):
    t = xbc.shape[0]
    ll = SSD_L
    nc = t // ll

    def body(x_ref, dt_ref, al_ref, z0_ref, z1_ref, d_ref, w_ref, y_ref, sp_ref, y4_ref, st_ref):
        @pl.when(pl.program_id(0) == 0)
        def _():
            st_ref[...] = jnp.zeros_like(st_ref)

        dtv = dt_ref[...]
        _, tril, acs, acs_t = _ssd_common(dtv, al_ref[...])
        lo = _lo_mask()
        sp_ref[0] = st_ref[...]
        for g in range(SSM_GROUPS):
            bg = x_ref[:, D_INNER + SSM_STATE * g:D_INNER + SSM_STATE * (g + 1)].astype(BF16)
            cg = x_ref[:, D_INNER + 512 + SSM_STATE * g:D_INNER + 512 + SSM_STATE * (g + 1)].astype(BF16)
            gm = _dot(cg, bg, 1, 1)
            g0 = GROUP_W * g
            terms = [_pair_terms(acs, acs_t, dtv, 8 * g + 2 * pp, lo) for pp in range(4)]
            dtx, eac, fdec, elast = [jnp.concatenate([tt[k] for tt in terms], axis=1) for k in (2, 3, 4, 5)]
            xg = x_ref[:, g0:g0 + GROUP_W]
            ug = (xg * dtx).astype(BF16)
            sg = st_ref[:, g0:g0 + GROUP_W]
            yst = _dot(cg, sg.astype(BF16), 1, 0) * eac
            st_ref[:, g0:g0 + GROUP_W] = sg * elast + _dot(bg, (xg * (fdec * dtx)).astype(BF16), 0, 0)
            ys = []
            for pp in range(4):
                cols, rows = terms[pp][0], terms[pp][1]
                sl = slice(LANES * pp, LANES * (pp + 1))
                y_in = _dot(_two_heads_cols([(gm * _decay(cols[e], rows[e], tril)).astype(BF16) for e in range(2)]),
                            _two_heads_rows(ug[:, sl], lo), 1, 0)
                ys.append(y_in + yst[:, sl])
            yg = jnp.concatenate(ys, axis=1)
            y_ref[:, g0:g0 + GROUP_W] = yg
            zg = _z_group((z0_ref, z1_ref), g)
            y3 = (yg + d_ref[:, g0:g0 + GROUP_W] * xg) * (zg * _sigmoid(zg))
            r = lax.rsqrt(jnp.mean(y3 * y3, axis=-1, keepdims=True) + EPS)
            y4_ref[:, g0:g0 + GROUP_W] = (y3 * r * w_ref[:, g0:g0 + GROUP_W]).astype(BF16)

    zblk = lambda j: pl.BlockSpec((ll, 1024), lambda c: (c, j))
    vec = pl.BlockSpec((1, D_INNER), lambda c: (0, 0))
    row = pl.BlockSpec((ll, D_INNER), lambda c: (c, 0))
    return _call(
        body, name=name, grid=(nc,),
        in_specs=[pl.BlockSpec((ll, XBC), lambda c: (c, 0)), pl.BlockSpec((ll, LANES), lambda c: (c, 0)),
                  pl.BlockSpec((1, LANES), lambda c: (0, 0)), zblk(0), zblk(1), vec, vec],
        out_specs=[row, pl.BlockSpec((1, SSM_STATE, D_INNER), lambda c: (c, 0, 0)), row],
        out_shape=[jax.ShapeDtypeStruct((t, D_INNER), F32), jax.ShapeDtypeStruct((nc, SSM_STATE, D_INNER), F32),
                   jax.ShapeDtypeStruct((t, D_INNER), BF16)],
        scratch_shapes=[pltpu.VMEM((SSM_STATE, D_INNER), F32)],
        args=(xbc, dt, alog, zx, zx, dexp, nw), sem=("arbitrary",), comm=comm)


def _ssd_bwd(xbc, dt, alog, sprev, dy4, y, zx, dexp, nw, name, comm=None):
    t = xbc.shape[0]
    ll = SSD_L_BWD if t % SSD_L_BWD == 0 else SSD_L
    nc = t // ll
    every = ll // SSD_L

    def body(x_ref, dt_ref, al_ref, sp_ref, g4_ref, y_ref, z0_ref, z1_ref, d_ref, w_ref,
             dx_ref, ddt_ref, dal_ref, dz_ref, dd_ref, dnw_ref, ds_ref, colt_ref):
        @pl.when(pl.program_id(0) == 0)
        def _():
            ds_ref[...] = jnp.zeros_like(ds_ref)
            dal_ref[...] = jnp.zeros_like(dal_ref)
            dd_ref[...] = jnp.zeros_like(dd_ref)
            dnw_ref[...] = jnp.zeros_like(dnw_ref)

        dtv = dt_ref[...]
        a_neg, tril, acs, acs_t = _ssd_common(dtv, al_ref[...])
        lo = _lo_mask()
        hi = jnp.logical_not(lo)
        lane = lax.broadcasted_iota(jnp.int32, (1, LANES), 1)
        colt_ref[...] = jnp.zeros_like(colt_ref)
        rowterm = jnp.zeros((ll, LANES), F32)
        ddt_u = jnp.zeros((ll, LANES), F32)
        dlast = jnp.zeros((1, LANES), F32)

        def halves(v):
            return (jnp.sum(jnp.where(lo, v, 0.0), axis=-1, keepdims=True),
                    jnp.sum(jnp.where(hi, v, 0.0), axis=-1, keepdims=True))

        for g in range(SSM_GROUPS):
            cb0 = D_INNER + SSM_STATE * g
            cc0 = D_INNER + 512 + SSM_STATE * g
            bg = x_ref[:, cb0:cb0 + SSM_STATE].astype(BF16)
            cg = x_ref[:, cc0:cc0 + SSM_STATE].astype(BF16)
            gm = _dot(cg, bg, 1, 1)
            g0 = GROUP_W * g
            terms = [_pair_terms(acs, acs_t, dtv, 8 * g + 2 * pp, lo) for pp in range(4)]
            dtx, eac, fdec, elast = [jnp.concatenate([tt[k] for tt in terms], axis=1) for k in (2, 3, 4, 5)]
            xg = x_ref[:, g0:g0 + GROUP_W]
            u32 = xg * dtx
            ug = u32.astype(BF16)
            zg = _z_group((z0_ref, z1_ref), g)
            dg = d_ref[:, g0:g0 + GROUP_W]
            act, dact = _silu_grad(zg)
            y2 = y_ref[:, g0:g0 + GROUP_W] + dg * xg
            y3 = y2 * act
            rn = lax.rsqrt(jnp.mean(y3 * y3, axis=-1, keepdims=True) + EPS)
            y3n = y3 * rn
            gv = g4_ref[:, g0:g0 + GROUP_W]
            dyn = gv * w_ref[:, g0:g0 + GROUP_W]
            dy3 = rn * (dyn - y3n * jnp.mean(dyn * y3n, axis=-1, keepdims=True))
            dyg = dy3 * act
            dskip = dyg * dg
            dz_ref[:, g0:g0 + GROUP_W] = (dy3 * y2 * dact).astype(BF16)
            dd_ref[:, g0:g0 + GROUP_W] += jnp.sum(dyg * xg, axis=0, keepdims=True)
            dnw_ref[:, g0:g0 + GROUP_W] += jnp.sum(gv * y3n, axis=0, keepdims=True)
            dyb = dyg.astype(BF16)
            spg = sp_ref[0, :, g0:g0 + GROUP_W]
            spb = spg.astype(BF16)
            dsg = ds_ref[:, g0:g0 + GROUP_W]
            dsb = dsg.astype(BF16)
            du_st = _dot(bg, dsb, 1, 0) * fdec
            yst = _dot(cg, spb, 1, 0) * eac
            dye = (dyg * eac).astype(BF16)
            dc_st = _dot(dye, spb, 1, 1)
            db_st = _dot((xg * (fdec * dtx)).astype(BF16), dsb, 1, 1)
            ds_ref[:, g0:g0 + GROUP_W] = dsg * elast + _dot(cg, dye, 0, 0)
            qst_el = du_st * u32
            rq_el = dyg * yst - qst_el
            q_row = jnp.sum(qst_el, axis=0, keepdims=True)
            s_row = jnp.sum(dsg * spg, axis=0, keepdims=True)
            dgm = jnp.zeros((ll, ll), F32)
            for pp in range(4):
                h0 = 8 * g + 2 * pp
                cols, rows = terms[pp][0], terms[pp][1]
                sl = slice(LANES * pp, LANES * (pp + 1))
                decs = [_decay(cols[e], rows[e], tril) for e in range(2)]
                wms = [gm * d for d in decs]
                dum2 = _dot(dyb[:, sl], _two_heads_rows(ug[:, sl], lo), 1, 1)
                du = _dot(jnp.concatenate([wm.astype(BF16) for wm in wms], axis=0),
                          _two_heads_rows(dyb[:, sl], lo), 0, 0) + du_st[:, sl]
                dx_ref[:, g0 + LANES * pp:g0 + LANES * (pp + 1)] = du * dtx[:, sl] + dskip[:, sl]
                ddtu = halves(du * xg[:, sl])
                rq = halves(rq_el[:, sl])
                qs = halves(q_row[:, sl])
                ss = halves(s_row[:, sl])
                for e in range(2):
                    dum = dum2[:, ll * e:ll * (e + 1)]
                    dgm = dgm + dum * decs[e]
                    tm_ = dum * wms[e]
                    oh = lane == (h0 + e)
                    rowterm = rowterm + jnp.where(oh, jnp.sum(tm_, axis=1, keepdims=True) + rq[e], 0.0)
                    ddt_u = ddt_u + jnp.where(oh, ddtu[e], 0.0)
                    dlast = dlast + jnp.where(oh, jnp.exp(cols[e][ll - 1:ll, :]) * ss[e] + qs[e], 0.0)
                    colt_ref[h0 + e:h0 + e + 1, :] = jnp.sum(tm_, axis=0, keepdims=True)
            dgb = dgm.astype(BF16)
            dx_ref[:, cc0:cc0 + SSM_STATE] = _dot(dgb, bg, 1, 0) + dc_st
            dx_ref[:, cb0:cb0 + SSM_STATE] = _dot(dgb, cg, 0, 0) + db_st
        row_io = lax.broadcasted_iota(jnp.int32, (ll, LANES), 0)
        dacs = rowterm - colt_ref[...].T + jnp.where(row_io == ll - 1, dlast, 0.0)
        da = _dot(jnp.logical_not(tril).astype(F32) + jnp.where(
            lax.broadcasted_iota(jnp.int32, (ll, ll), 0) == lax.broadcasted_iota(jnp.int32, (ll, ll), 1), 1.0, 0.0),
            dacs, 1, 0, HI)
        ddt_ref[...] = da * a_neg + ddt_u
        dal_ref[...] += jnp.sum(da * dtv, axis=0, keepdims=True) * a_neg

    rev = lambda c: nc - 1 - c
    row = pl.BlockSpec((ll, D_INNER), lambda c: (rev(c), 0))
    vec = pl.BlockSpec((1, D_INNER), lambda c: (0, 0))
    zblk = lambda j: pl.BlockSpec((ll, 1024), lambda c: (rev(c), j))
    return _call(
        body, name=name, grid=(nc,),
        in_specs=[pl.BlockSpec((ll, XBC), lambda c: (rev(c), 0)), pl.BlockSpec((ll, LANES), lambda c: (rev(c), 0)),
                  pl.BlockSpec((1, LANES), lambda c: (0, 0)),
                  pl.BlockSpec((1, SSM_STATE, D_INNER), lambda c: (rev(c) * every, 0, 0)), row, row, zblk(0), zblk(1), vec, vec],
        out_specs=[pl.BlockSpec((ll, XBC), lambda c: (rev(c), 0)), pl.BlockSpec((ll, LANES), lambda c: (rev(c), 0)),
                   pl.BlockSpec((1, LANES), lambda c: (0, 0)), row, vec, vec],
        out_shape=[jax.ShapeDtypeStruct((t, XBC), F32), jax.ShapeDtypeStruct((t, LANES), F32),
                   jax.ShapeDtypeStruct((1, LANES), F32), jax.ShapeDtypeStruct((t, D_INNER), BF16),
                   jax.ShapeDtypeStruct((1, D_INNER), F32), jax.ShapeDtypeStruct((1, D_INNER), F32)],
        scratch_shapes=[pltpu.VMEM((SSM_STATE, D_INNER), F32), pltpu.VMEM((LANES, ll), F32)],
        args=(xbc, dt, alog, sprev, dy4, y, zx, zx, dexp, nw), sem=("arbitrary",), comm=comm)


def _sum_parts(parts, name):
    nparts, r, c = parts.shape
    tc = _pick(c, (256, 128))

    def body(p_ref, o_ref):
        g = p_ref[0].astype(F32)
        for k in range(1, nparts):
            g = g + p_ref[k].astype(F32)
        o_ref[...] = g

    return pl.pallas_call(
        body, name=name, grid=(c // tc,), in_specs=[pl.BlockSpec((nparts, r, tc), lambda j: (0, 0, j))],
        out_specs=pl.BlockSpec((r, tc), lambda j: (0, j)), out_shape=jax.ShapeDtypeStruct((r, c), F32),
        compiler_params=_cp(("parallel",)),
    )(parts)


def _adamw(parts, w, m, v, name):
    nl, r, c = w.shape
    assert len(parts) == nl
    tr = _pick(r, (256, 128, 64))
    c1 = 1.0 - ADAM_B1 ** ADAM_STEP
    c2 = 1.0 - ADAM_B2 ** ADAM_STEP

    def body(*refs):
        p_refs = refs[:nl]
        w_ref, m_ref, v_ref, g_ref, d_ref, mo_ref, vo_ref = refs[nl:]
        g = None
        for l, p_ref in enumerate(p_refs):
            s = p_ref[0].astype(F32)
            for k in range(1, p_ref.shape[0]):
                s = s + p_ref[k].astype(F32)
            g = s if g is None else jnp.where(pl.program_id(0) == l, s, g)
        mn = ADAM_B1 * m_ref[0] + (1.0 - ADAM_B1) * g
        vn = ADAM_B2 * v_ref[0] + (1.0 - ADAM_B2) * (g * g)
        g_ref[0] = g
        mo_ref[0] = mn
        vo_ref[0] = vn
        d_ref[0] = -ADAM_LR * ((mn / c1) / (jnp.sqrt(vn / c2) + ADAM_EPS) + ADAM_WD * w_ref[0])

    row = pl.BlockSpec((1, tr, c), lambda l, i: (l, i, 0))
    sd = jax.ShapeDtypeStruct((nl, r, c), F32)
    return pl.pallas_call(
        body, name=name, grid=(nl, r // tr),
        in_specs=[pl.BlockSpec((p.shape[0], tr, c), lambda l, i: (0, i, 0)) for p in parts] + [row, row, row],
        out_specs=[row, row, row, row], out_shape=[sd, sd, sd, sd], compiler_params=_cp(("parallel", "parallel")),
    )(*parts, w, m, v)


def _peers():
    mx, my, mc = lax.axis_index("x"), lax.axis_index("y"), lax.axis_index("c")
    me = 4 * mx + 2 * my + mc
    out = []
    for k in range(1, N_DEV):
        px = 1 - mx if k & 4 else mx
        py = 1 - my if k & 2 else my
        pc = 1 - mc if k & 1 else mc
        out.append(((px, py, pc), 4 * px + 2 * py + pc))
    return me, out


class _Comm:
    def __init__(self, arrs, scatters):
        self.arrs, self.scatters, self.n = list(arrs), list(scatters), len(arrs)
        self.specs = [pl.BlockSpec(memory_space=pl.ANY)] * self.n
        self.out_shape = [jax.ShapeDtypeStruct(x.shape if sc else (N_DEV,) + x.shape, x.dtype)
                          for x, sc in zip(self.arrs, self.scatters)]
        np_ = N_DEV - 1
        self.scratch = [pltpu.SemaphoreType.DMA((np_ * self.n,)), pltpu.SemaphoreType.DMA((np_ * self.n,)),
                        pltpu.SemaphoreType.DMA((self.n,))]

    def _copies(self, x_refs, o_refs, sems):
        send_sems, recv_sems, local_sems = sems
        me, peers = _peers()
        np_ = N_DEV - 1
        local, sends, recvs = [], [], []
        for a in range(self.n):
            mine = x_refs[a].at[me] if self.scatters[a] else x_refs[a]
            local.append(pltpu.make_async_copy(mine, o_refs[a].at[me], local_sems.at[a]))
        for k, (dev, idx) in enumerate(peers):
            for a in range(self.n):
                mine = x_refs[a].at[me] if self.scatters[a] else x_refs[a]
                sends.append(pltpu.make_async_remote_copy(
                    src_ref=x_refs[a].at[idx] if self.scatters[a] else x_refs[a], dst_ref=o_refs[a].at[me],
                    send_sem=send_sems.at[a * np_ + k], recv_sem=recv_sems.at[a * np_ + k], device_id=dev, device_id_type=MESH))
                recvs.append(pltpu.make_async_remote_copy(
                    src_ref=mine, dst_ref=o_refs[a].at[idx], send_sem=send_sems.at[a * np_ + k],
                    recv_sem=recv_sems.at[a * np_ + k], device_id=dev, device_id_type=MESH))
        return local, sends, recvs

    def start(self, x_refs, o_refs, sems):
        local, sends, _ = self._copies(x_refs, o_refs, sems)
        for cp in local + sends:
            cp.start()

    def wait(self, x_refs, o_refs, sems):
        local, sends, recvs = self._copies(x_refs, o_refs, sems)
        for cp in recvs:
            cp.wait_recv()
        for cp in sends:
            cp.wait_send()
        for cp in local:
            cp.wait()


class _Gather2(_Comm):
    def __init__(self, arrs):
        super().__init__(arrs, [False] * len(arrs))

    def _plan(self, x_refs, o_refs, sems):
        send_sems, recv_sems, local_sems = sems
        mx, my, mc = lax.axis_index("x"), lax.axis_index("y"), lax.axis_index("c")
        slot = lambda px, py, pc: 4 * px + 2 * py + pc
        sib = (mx, my, 1 - mc)
        chips = [(1 - mx, my), (mx, 1 - my), (1 - mx, 1 - my)]
        np_ = N_DEV - 1
        local, first, passed, arrive_first, arrive_rest = [], [], [], [], []

        def copy(a, k, src, block, to):
            return pltpu.make_async_remote_copy(
                src_ref=src, dst_ref=o_refs[a].at[block], send_sem=send_sems.at[a * np_ + k], recv_sem=recv_sems.at[a * np_ + k],
                device_id=to, device_id_type=MESH)

        for a in range(self.n):
            me = slot(mx, my, mc)
            local.append(pltpu.make_async_copy(x_refs[a], o_refs[a].at[me], local_sems.at[a]))
            first.append(copy(a, 0, x_refs[a], me, sib))
            arrive_rest.append(copy(a, 0, x_refs[a], slot(*sib), sib))
            for j, (cx, cy) in enumerate(chips):
                first.append(copy(a, 1 + j, x_refs[a], me, (cx, cy, mc)))
                arrive_first.append(copy(a, 1 + j, x_refs[a], slot(cx, cy, mc), (cx, cy, mc)))
                passed.append(copy(a, 4 + j, o_refs[a].at[slot(cx, cy, mc)], slot(cx, cy, mc), sib))
                arrive_rest.append(copy(a, 4 + j, x_refs[a], slot(cx, cy, 1 - mc), sib))
        return local, first, passed, arrive_first, arrive_rest

    def start(self, x_refs, o_refs, sems):
        local, first, _, _, _ = self._plan(x_refs, o_refs, sems)
        for cp in local + first:
            cp.start()

    def wait(self, x_refs, o_refs, sems):
        local, first, passed, arrive_first, arrive_rest = self._plan(x_refs, o_refs, sems)
        for arrived, onward in zip(arrive_first, passed):
            arrived.wait_recv()
            onward.start()
        for cp in arrive_rest:
            cp.wait_recv()
        for cp in first + passed:
            cp.wait_send()
        for cp in local:
            cp.wait()


def _call(body, *, name, grid, in_specs, out_specs, out_shape, args, scratch_shapes=(), sem=None, comm=None):
    if comm is None:
        outs = pl.pallas_call(
            body, name=name, grid=grid, in_specs=list(in_specs), out_specs=list(out_specs), out_shape=list(out_shape),
            scratch_shapes=list(scratch_shapes), compiler_params=_cp(sem),
        )(*args)
        return list(outs), []
    n_in, n_out, nc = len(in_specs), len(out_specs), comm.n
    nsteps = 1
    for g in grid:
        nsteps *= g

    def carrier(*refs):
        ins, cin = refs[:n_in], refs[n_in:n_in + nc]
        outs, cout = refs[n_in + nc:n_in + nc + n_out], refs[n_in + nc + n_out:n_in + 2 * nc + n_out]
        rest = refs[n_in + 2 * nc + n_out:]
        scratch, sems = rest[:len(rest) - 3], rest[len(rest) - 3:]
        if nsteps == 1:
            comm.start(cin, cout, sems)
            body(*ins, *outs, *scratch)
            comm.wait(cin, cout, sems)
            return
        step = 0
        for d, g in enumerate(grid):
            step = step * g + pl.program_id(d)

        @pl.when(step == 0)
        def _():
            comm.start(cin, cout, sems)

        body(*ins, *outs, *scratch)

        @pl.when(step == nsteps - 1)
        def _():
            comm.wait(cin, cout, sems)

    outs = pl.pallas_call(
        carrier, name=name, grid=grid, in_specs=list(in_specs) + comm.specs, out_specs=list(out_specs) + comm.specs,
        out_shape=list(out_shape) + comm.out_shape, scratch_shapes=list(scratch_shapes) + comm.scratch,
        compiler_params=_cp(("arbitrary",) * len(grid) if grid else None),
    )(*args, *comm.arrs)
    return list(outs[:n_out]), list(outs[n_out:])


def _exchange(comm, name):
    return _call(lambda *refs: None, name=name, grid=(), in_specs=[], out_specs=[], out_shape=[], args=[], comm=comm)[1]


def _pack(arrs, dtype, lead=()):
    nl = len(lead)
    flat = jnp.concatenate([a.astype(dtype).reshape(lead + (-1,)) for a in arrs], axis=nl)
    n = flat.shape[-1]
    rows = -(-n // (LANES * 8)) * 8
    flat = jnp.pad(flat, [(0, 0)] * nl + [(0, rows * LANES - n)])
    return flat.reshape(lead + (rows, LANES))


def _unpack(flat, shapes, lead=()):
    nl = len(lead)
    flat = flat.reshape(lead + (-1,))
    out, o = [], 0
    for s in shapes:
        n = 1
        for d in s:
            n *= d
        out.append(lax.slice_in_dim(flat, o, o + n, axis=nl).reshape(lead + tuple(s)))
        o += n
    return out


def _join(g, ax):
    return jnp.concatenate([g[d] for d in range(N_DEV)], axis=ax)


def _split(full, ax):
    n = full.shape[ax] // N_DEV
    return jnp.stack([lax.slice_in_dim(full, d * n, (d + 1) * n, axis=ax) for d in range(N_DEV)])


_WEIGHTS = ['norm_mix', 'norm_ffn', 'attn_w_in', 'attn_w_out', 'relpos_table', 'q_norm_a', 'k_norm_a', 'q_norm_b',
            'k_norm_b', 'sinks', 'ssm_w_in', 'ssm_conv_w', 'ssm_conv_b', 'ssm_dt_bias', 'ssm_a_log', 'ssm_d', 'ssm_norm',
            'ssm_w_out', 'ffn_w_in', 'ffn_conv_w', 'ffn_conv_b', 'ffn_w_out']
_SHARD_AX = {'attn_w_in': 2, 'attn_w_out': 1, 'ssm_w_in': 2, 'ssm_conv_w': 2, 'ssm_conv_b': 1, 'ssm_norm': 1,
             'ssm_w_out': 1, 'ffn_w_in': 2, 'ffn_conv_w': 2, 'ffn_w_out': 1}
_BIG = ['attn_w_in', 'attn_w_out', 'ssm_w_in', 'ssm_w_out', 'ffn_w_in', 'ffn_w_out']
_SMALL = ['ssm_conv_w', 'ssm_conv_b', 'ssm_norm', 'ffn_conv_w']
_AX2 = {n: _SHARD_AX[n] - 1 for n in _BIG}
_REPL = [n for n in _WEIGHTS if n not in _SHARD_AX]


def _rows8(w):
    return jnp.pad(w, ((0, 8 - w.shape[0]), (0, 0)))


def _lanes128(v):
    return jnp.pad(v, (0, LANES - v.shape[0])).reshape(1, LANES)


def _band_mask(n_prev, pad):
    cq = jnp.arange(TQ)[:, None] // CHUNK
    ck = jnp.arange(pad + TQ)[None, :] // CHUNK
    return (ck >= cq) & (ck <= cq + n_prev)


def _ffn_fwd(xin, g, w_in_t, w8, cb, tag):
    gu, h, a, gc = _ffn_in_mid(xin, g, w_in_t, w8, cb, f"mm_ffn_in{tag}")
    return a, (h, gu, a, gc)


def _ffn_bwd(dx, dxb, xin, g, w_in_t, w8, w_out, saved, tag):
    h, gu, a, gc = saved
    dw_out = _mm_tn(a, dxb, f"mm_ffn_dwout{tag}")
    dgu, dw8, dcb = _ffn_mid_bwd(gu, gc, dxb, w_out, w8, f"ffn_mid_bwd{tag}")
    dw_in_t = _mm_tn(dgu, h, f"mm_ffn_dwin{tag}")
    dxp, dxpb, dg = _mm_rms_bwd(dgu, w_in_t, 0, None, xin, g, dx, f"mm_ffn_dh{tag}")
    return dxp, dxpb, dg, dw_in_t, dw8[:3], dcb, dw_out


def kernel(x, norm_mix, norm_ffn, attn_w_in, attn_w_out, relpos_table, q_norm_a, k_norm_a, q_norm_b, k_norm_b, sinks, ssm_w_in, ssm_conv_w, ssm_conv_b, ssm_dt_bias, ssm_a_log, ssm_d, ssm_norm, ssm_w_out, ffn_w_in, ffn_conv_w, ffn_conv_b, ffn_w_out, loss_target, m_norm_mix, m_norm_ffn, m_attn_w_in, m_attn_w_out, m_relpos_table, m_q_norm_a, m_k_norm_a, m_q_norm_b, m_k_norm_b, m_sinks, m_ssm_w_in, m_ssm_conv_w, m_ssm_conv_b, m_ssm_dt_bias, m_ssm_a_log, m_ssm_d, m_ssm_norm, m_ssm_w_out, m_ffn_w_in, m_ffn_conv_w, m_ffn_conv_b, m_ffn_w_out, v_norm_mix, v_norm_ffn, v_attn_w_in, v_attn_w_out, v_relpos_table, v_q_norm_a, v_k_norm_a, v_q_norm_b, v_k_norm_b, v_sinks, v_ssm_w_in, v_ssm_conv_w, v_ssm_conv_b, v_ssm_dt_bias, v_ssm_a_log, v_ssm_d, v_ssm_norm, v_ssm_w_out, v_ffn_w_in, v_ffn_conv_w, v_ffn_conv_b, v_ffn_w_out):
    w = dict(norm_mix=norm_mix, norm_ffn=norm_ffn, attn_w_in=attn_w_in, attn_w_out=attn_w_out, relpos_table=relpos_table,
             q_norm_a=q_norm_a, k_norm_a=k_norm_a, q_norm_b=q_norm_b, k_norm_b=k_norm_b, sinks=sinks, ssm_w_in=ssm_w_in,
             ssm_conv_w=ssm_conv_w, ssm_conv_b=ssm_conv_b, ssm_dt_bias=ssm_dt_bias, ssm_a_log=ssm_a_log, ssm_d=ssm_d,
             ssm_norm=ssm_norm, ssm_w_out=ssm_w_out, ffn_w_in=ffn_w_in, ffn_conv_w=ffn_conv_w, ffn_conv_b=ffn_conv_b,
             ffn_w_out=ffn_w_out)
    mom = dict(norm_mix=m_norm_mix, norm_ffn=m_norm_ffn, attn_w_in=m_attn_w_in, attn_w_out=m_attn_w_out,
               relpos_table=m_relpos_table, q_norm_a=m_q_norm_a, k_norm_a=m_k_norm_a, q_norm_b=m_q_norm_b,
               k_norm_b=m_k_norm_b, sinks=m_sinks, ssm_w_in=m_ssm_w_in, ssm_conv_w=m_ssm_conv_w, ssm_conv_b=m_ssm_conv_b,
               ssm_dt_bias=m_ssm_dt_bias, ssm_a_log=m_ssm_a_log, ssm_d=m_ssm_d, ssm_norm=m_ssm_norm, ssm_w_out=m_ssm_w_out,
               ffn_w_in=m_ffn_w_in, ffn_conv_w=m_ffn_conv_w, ffn_conv_b=m_ffn_conv_b, ffn_w_out=m_ffn_w_out)
    var = dict(norm_mix=v_norm_mix, norm_ffn=v_norm_ffn, attn_w_in=v_attn_w_in, attn_w_out=v_attn_w_out,
               relpos_table=v_relpos_table, q_norm_a=v_q_norm_a, k_norm_a=v_k_norm_a, q_norm_b=v_q_norm_b,
               k_norm_b=v_k_norm_b, sinks=v_sinks, ssm_w_in=v_ssm_w_in, ssm_conv_w=v_ssm_conv_w, ssm_conv_b=v_ssm_conv_b,
               ssm_dt_bias=v_ssm_dt_bias, ssm_a_log=v_ssm_a_log, ssm_d=v_ssm_d, ssm_norm=v_ssm_norm, ssm_w_out=v_ssm_w_out,
               ffn_w_in=v_ffn_w_in, ffn_conv_w=v_ffn_conv_w, ffn_conv_b=v_ffn_conv_b, ffn_w_out=v_ffn_w_out)

    def piece(n, l):
        return (w[n][l].T if _AX2[n] == 1 else w[n][l]).astype(BF16)

    def gather_of(names_layers):
        return _Gather2([piece(n, l) for n, l in names_layers])

    def joined(got):
        return [g.reshape(-1, D_MODEL) for g in got]

    first = [('attn_w_in', 0), ('attn_w_out', 0)]
    got = _exchange(_Gather2([piece(n, l) for n, l in first] + [_pack([w[n] for n in _SMALL], F32)]), "gather_attn")
    wt_attn_in, w_attn_out = joined(got[:2])
    full = {}
    for n, g in zip(_SMALL, _unpack(got[2], [w[n].shape for n in _SMALL], lead=(N_DEV,))):
        full[n] = _join(g, _SHARD_AX[n])
    ssm_cw8 = _rows8(full['ssm_conv_w'][0])
    ssm_cb = full['ssm_conv_b']
    ssm_nw = full['ssm_norm']
    ffn_cw8 = [_rows8(full['ffn_conv_w'][l]) for l in range(2)]
    ffn_cb = [ffn_conv_b[l:l + 1] for l in range(2)]

    x0 = x[0]
    target = loss_target[0]
    t = x0.shape[0]

    g_mix0, g_mix1 = norm_mix[0:1], norm_mix[1:2]
    g_ffn0, g_ffn1 = norm_ffn[0:1], norm_ffn[1:2]
    proj, h0 = _rms_mm(x0, g_mix0, wt_attn_in, ATTN_PROJ, "mm_attn_in", F32)
    hn_w = jnp.concatenate([jnp.tile(v, (1, 2)) for v in (q_norm_a, k_norm_a, q_norm_b, k_norm_b)], axis=0)
    qa, kpa, vpa, qb, kpb, vpb = _headnorm_fwd(proj, hn_w, "headnorm")
    table = jnp.pad(relpos_table[0], ((0, 0), (0, REL_W - (2 * MAX_REL + 1))))
    bias_a = jnp.where(_band_mask(A_PREV, PAD_A)[None], jnp.transpose(_relpos_fwd(table, "relpos_bias"), (1, 0, 2)), NEG)
    rel_b = jnp.arange(TQ)[:, None] - (jnp.arange(PAD_B + TQ)[None, :] - PAD_B)
    slopes = 2.0 ** (-8.0 * jnp.arange(1, N_HEADS + 1, dtype=F32) / N_HEADS)
    bias_b = jnp.where(_band_mask(B_PREV, PAD_B)[None], -slopes[:, None, None] * jnp.abs(rel_b).astype(F32)[None], NEG)
    no_sinks = jnp.full((N_HEADS,), NEG, F32)
    ffn0_w, ssm_w, ffn1_w = [('ffn_w_in', 0), ('ffn_w_out', 0)], [('ssm_w_in', 0), ('ssm_w_out', 0)], [('ffn_w_in', 1), ('ffn_w_out', 1)]
    oa, stats_a, got = _attn_fwd(qa, kpa, vpa, bias_a, no_sinks, PAD_A, "attn_a", comm=gather_of(ffn0_w + ssm_w))
    wt_ffn_in0, w_ffn_out0, wt_ssm_in, w_ssm_out = joined(got)
    ob, stats_b, _ = _attn_fwd(qb, kpb, vpb, bias_b, sinks[0], PAD_B, "attn_b")
    wt_ssm_dt = jnp.pad(wt_ssm_in[ZX:], ((0, LANES - SSM_HEADS), (0, 0)))
    x1 = _mm(oa, w_attn_out, "mm_attn_out_a", res=x0, b_rows=(0, D_ATT))
    x1 = _mm(ob, w_attn_out, "mm_attn_out_b", res=x1, b_rows=(D_ATT, D_ATT))
    a0, ffn0_saved = _ffn_fwd(x1, g_ffn0, wt_ffn_in0, ffn_cw8[0], ffn_cb[0], "0")
    x2 = _mm(a0, w_ffn_out0, "mm_ffn_out0", res=x1)

    zx, h2, xbc, conv_pre = _ssm_in_pre(x2, g_mix1, wt_ssm_in, ssm_cw8, ssm_cb, "mm_ssm_in")
    dtraw = _mm(h2, wt_ssm_dt, "mm_ssm_dt", trans_b=True)
    dt_bias = _lanes128(ssm_dt_bias[0])
    alog = _lanes128(ssm_a_log[0])
    dexp = jnp.repeat(ssm_d[0], HEAD_DIM).reshape(1, D_INNER)
    dt = _dt_fwd(dtraw, dt_bias, "ssm_dt")
    (y, sprev, y4), got = _ssd_fwd(xbc, dt, alog, zx, dexp, ssm_nw, "ssd_fwd", comm=gather_of(ffn1_w))
    wt_ffn_in1, w_ffn_out1 = joined(got)
    x3 = _mm(y4, w_ssm_out, "mm_ssm_out", res=x2)
    a1, ffn1_saved = _ffn_fwd(x3, g_ffn1, wt_ffn_in1, ffn_cw8[1], ffn_cb[1], "1")

    dx4, dx4b, sq = _mm_loss(a1, w_ffn_out1, x3, target, "mm_ffn_out1_loss")
    loss = lax.psum(0.5 * jnp.sum(sq) / D_MODEL, ("x", "y", "c"))

    grads = {}

    def scatter_of(grads_2d):
        return _Comm([g.reshape(N_DEV, -1, D_MODEL) for g in grads_2d], [True] * len(grads_2d))

    dx3, dx3b, dg_ffn1, dwtin1, dcw1, dcb1, dwout1 = _ffn_bwd(
        dx4, dx4b, x3, g_ffn1, wt_ffn_in1, ffn_cw8[1], w_ffn_out1, ffn1_saved, "1")

    dy4 = _mm(dx3b, w_ssm_out, "mm_ssm_dy", trans_b=True)
    dw_ssm_out = _mm_tn(y4, dx3b, "mm_ssm_dwout")
    (dxbc, ddt, dalog, dz, dd_lane, dnw), parts_ffn1 = _ssd_bwd(
        xbc, dt, alog, sprev, dy4, y, zx, dexp, ssm_nw, "ssd_bwd", comm=scatter_of([dwtin1, dwout1]))
    dxr, dcw_s, dcb_s = _ssm_pre_bwd(zx, conv_pre, dxbc, ssm_cw8, "ssm_pre_bwd")
    ddtraw, ddtb = _dt_bwd(dtraw, dt_bias, ddt, "ssm_dt_bwd")
    dh2 = _mm(dz, wt_ssm_in, "mm_ssm_dh_z", b_rows=(0, D_INNER))
    dh2 = _mm(dxr, wt_ssm_in[D_INNER:ZX], "mm_ssm_dh_x", res=dh2)
    dwt_ssm_in = jnp.concatenate([
        _mm_tn(dz, h2, "mm_ssm_dwin_z"), _mm_tn(dxr, h2, "mm_ssm_dwin_x"),
        _mm_tn(ddtraw, h2, "mm_ssm_dwin_dt")[:SSM_HEADS]], axis=0)
    dx2, dx2b, dg_mix1 = _mm_rms_bwd(ddtraw, wt_ssm_dt, 0, dh2, x2, g_mix1, dx3, "mm_ssm_dh_dt")
    grads['ssm_conv_w'] = dcw_s[:4][None]
    grads['ssm_conv_b'] = dcb_s
    grads['ssm_norm'] = dnw
    grads['ssm_dt_bias'] = ddtb[:, :SSM_HEADS]
    grads['ssm_a_log'] = dalog[:, :SSM_HEADS]
    grads['ssm_d'] = jnp.sum(dd_lane.reshape(SSM_HEADS, HEAD_DIM), axis=1)[None]

    dx1, dx1b, dg_ffn0, dwtin0, dcw0, dcb0, dwout0 = _ffn_bwd(
        dx2, dx2b, x1, g_ffn0, wt_ffn_in0, ffn_cw8[0], w_ffn_out0, ffn0_saved, "0")
    grads['ffn_conv_w'] = jnp.stack([dcw0, dcw1])
    grads['ffn_conv_b'] = jnp.concatenate([dcb0, dcb1], axis=0)
    grads['norm_ffn'] = jnp.concatenate([dg_ffn0, dg_ffn1], axis=0)

    do = _mm(dx1b, w_attn_out, "mm_attn_do", out_dtype=BF16, trans_b=True)
    dw_attn_out = jnp.concatenate([_mm_tn(oa, dx1b, "mm_attn_dwout_a"), _mm_tn(ob, dx1b, "mm_attn_dwout_b")], axis=0)
    (dqa, dkpa, dvpa, dbias_a, _), parts_ssm = _attn_bwd(
        qa, kpa, vpa, bias_a, no_sinks, do, stats_a, 0, PAD_A, "attn_a_bwd",
        comm=scatter_of([dwt_ssm_in, dw_ssm_out, dw_attn_out]))
    (dqb, dkpb, dvpb, _, dsink), parts_ffn0 = _attn_bwd(
        qb, kpb, vpb, bias_b, sinks[0], do, stats_b, 4, PAD_B, "attn_b_bwd", comm=scatter_of([dwtin0, dwout0]))
    grads['relpos_table'] = _relpos_bwd(jnp.transpose(dbias_a, (1, 0, 2)), "relpos_bwd")[None, :, :2 * MAX_REL + 1]
    grads['sinks'] = dsink[:, :2, 0].reshape(1, N_HEADS)
    dproj, dhn = _headnorm_bwd(proj, hn_w, dqa, dkpa, dvpa, dqb, dkpb, dvpb, "headnorm_bwd")
    dhn = dhn[:, :HEAD_DIM] + dhn[:, HEAD_DIM:]
    for k, n in enumerate(('q_norm_a', 'k_norm_a', 'q_norm_b', 'k_norm_b')):
        grads[n] = dhn[k:k + 1]
    dwt_attn_in = _mm_tn(dproj, h0, "mm_attn_dwin")
    dx0, _, dg_mix0, parts_attn_in = _mm_rms_bwd(dproj, wt_attn_in, 0, None, x0, g_mix0, dx1, "mm_attn_dh",
                                                 comm=scatter_of([dwt_attn_in]))
    grads['norm_mix'] = jnp.concatenate([dg_mix0, dg_mix1], axis=0)

    def summed_t(parts, name):
        return _sum_parts(parts, name).T[None]

    sm_shapes = [w[n].shape for n in _SMALL]
    rp_shapes = [w[n].shape for n in _REPL]
    recv = _exchange(_Comm(
        [_pack([_split(grads[n], _SHARD_AX[n]) for n in _SMALL], F32, lead=(N_DEV,)), _pack([grads[n] for n in _REPL], F32)],
        [True, False]), "exchange_small")
    big_parts = {
        'attn_w_in': [summed_t(parts_attn_in[0], "sum_attn_w_in")], 'attn_w_out': [parts_ssm[2]],
        'ssm_w_in': [summed_t(parts_ssm[0], "sum_ssm_w_in")], 'ssm_w_out': [parts_ssm[1]],
        'ffn_w_in': [summed_t(parts_ffn0[0], "sum_ffn_w_in0"), summed_t(parts_ffn1[0], "sum_ffn_w_in1")],
        'ffn_w_out': [parts_ffn0[1], parts_ffn1[1]],
    }
    res = [{}, {}, {}, {}]
    for n in _BIG:
        for kind, a in enumerate(_adamw(big_parts[n], w[n], mom[n], var[n], f"adamw_{n}")):
            res[kind][n] = a
    for names, shapes, parts in ((_SMALL, sm_shapes, recv[0]), (_REPL, rp_shapes, recv[1])):
        outs = _adamw([parts], _pack([w[n] for n in names], F32)[None], _pack([mom[n] for n in names], F32)[None],
                      _pack([var[n] for n in names], F32)[None], "adamw_" + ("small" if names is _SMALL else "replicated"))
        for kind, flat in enumerate(outs):
            for n, a in zip(names, _unpack(flat[0], shapes)):
                res[kind][n] = a
    return (loss, dx0[None], *[res[0][n] for n in _WEIGHTS], *[res[1][n] for n in _WEIGHTS],
            *[res[2][n] for n in _WEIGHTS], *[res[3][n] for n in _WEIGHTS])
```

```python
import jax
import jax.numpy as jnp
from jax import lax
from jax.experimental import pallas as pl
from jax.experimental.pallas import tpu as pltpu

F32 = jnp.float32
BF16 = jnp.bfloat16
HI = lax.Precision.HIGHEST
MESH = pl.DeviceIdType.MESH
NEG = -1e30

N_DEV = 8
D_MODEL = 1024
EPS = 1e-6
CHUNK = 64
HEAD_DIM = 64
N_HEADS = 8
A_PREV = 8
B_PREV = 2
MAX_REL = 256
TQ = 2 * CHUNK
ATT_SUB = 8
PAD_A = A_PREV * CHUNK
PAD_B = B_PREV * CHUNK
REL_W = PAD_A + TQ
D_ATT = N_HEADS * HEAD_DIM
COL_QA, COL_KA, COL_VA, COL_QB = 0, D_ATT, 2 * D_ATT, 3 * D_ATT
COL_KB, COL_VB = 4 * D_ATT, 4 * D_ATT + 2 * HEAD_DIM
ATTN_PROJ = COL_VB + 2 * HEAD_DIM
D_INNER = 2048
SSM_HEADS = 32
SSM_GROUPS = 4
SSM_STATE = 128
XBC = D_INNER + 2 * SSM_GROUPS * SSM_STATE
ZX = D_INNER + XBC
D_FF = 2816
SSD_L = 128
SSD_L_BWD = 2 * SSD_L
LANES = 128
VMEM_LIMIT = 56 << 20

ADAM_LR, ADAM_B1, ADAM_B2, ADAM_EPS, ADAM_WD, ADAM_STEP = 0.001, 0.9, 0.999, 1e-08, 0.01, 10


def _cp(sem=None):
    return pltpu.CompilerParams(dimension_semantics=sem, vmem_limit_bytes=VMEM_LIMIT)


def _dot(a, b, ca=1, cb=0, prec=None):
    return lax.dot_general(a, b, (((ca,), (cb,)), ((), ())), preferred_element_type=F32, precision=prec)


def _pick(n, cands):
    for c in cands:
        if n % c == 0:
            return c
    return n


def _lo_mask():
    return lax.broadcasted_iota(jnp.int32, (1, LANES), 1) < HEAD_DIM


_TN_CHUNKS = (1408, 1536, 1152, 1024, 512, 256, 128)


TN_MAX_ROWS = 3072
MM_WIDE = 2304


def _mm_tn(a, b, name):
    kdim, m = a.shape
    n = b.shape[1]
    assert b.shape[0] == kdim, (a.shape, b.shape)
    mb = m if m <= TN_MAX_ROWS else m // 2
    tn = _pick(n, _TN_CHUNKS)
    tk = _pick(kdim, (512, 256, 128))
    nk = kdim // tk

    def body(a_ref, b_ref, o_ref, acc):
        k = pl.program_id(1)

        @pl.when(k == 0)
        def _():
            acc[...] = jnp.zeros_like(acc)

        av = a_ref[...]
        for c in range(0, n, tn):
            acc[:, c:c + tn] += _dot(av, b_ref[:, c:c + tn], 0, 0)

        @pl.when(k == nk - 1)
        def _():
            o_ref[...] = acc[...].astype(BF16)

    return pl.pallas_call(
        body, name=name, grid=(m // mb, nk),
        in_specs=[pl.BlockSpec((tk, mb), lambda j, k: (k, j)), pl.BlockSpec((tk, n), lambda j, k: (k, 0))],
        out_specs=pl.BlockSpec((mb, n), lambda j, k: (j, 0)), out_shape=jax.ShapeDtypeStruct((m, n), BF16),
        scratch_shapes=[pltpu.VMEM((mb, n), F32)], compiler_params=_cp(("parallel", "arbitrary")),
    )(a, b)


def _mm(a, b, name, out_dtype=F32, res=None, trans_b=False, b_rows=None):
    m, kdim = a.shape
    if b_rows is None:
        b_rows = (0, b.shape[0])
    off, rows = b_rows
    n = rows if trans_b else b.shape[1]
    assert (b.shape[1] if trans_b else rows) == kdim and off % rows == 0, (a.shape, b.shape, b_rows)
    tn = _pick(n, _TN_CHUNKS)
    tm = _pick(m, (256, 128) if n > MM_WIDE else (512, 256, 128))

    def body(*refs):
        if res is None:
            a_ref, b_ref, o_ref = refs
        else:
            a_ref, b_ref, r_ref, o_ref = refs
        av = a_ref[...]
        for c in range(0, n, tn):
            r = _dot(av, b_ref[c:c + tn, :], 1, 1) if trans_b else _dot(av, b_ref[:, c:c + tn], 1, 0)
            if res is not None:
                r = r + r_ref[:, c:c + tn]
            o_ref[:, c:c + tn] = r.astype(out_dtype)

    in_specs = [pl.BlockSpec((tm, kdim), lambda i: (i, 0)), pl.BlockSpec((rows, b.shape[1]), lambda i: (off // rows, 0))]
    args = [a, b]
    if res is not None:
        in_specs.append(pl.BlockSpec((tm, n), lambda i: (i, 0)))
        args.append(res)
    return pl.pallas_call(
        body, name=name, grid=(m // tm,), in_specs=in_specs, out_specs=pl.BlockSpec((tm, n), lambda i: (i, 0)),
        out_shape=jax.ShapeDtypeStruct((m, n), out_dtype), compiler_params=_cp(("parallel",)),
    )(*args)


def _rms_mm(x, g, bt, n, name, out_dtype):
    t, d = x.shape
    tn = _pick(n, _TN_CHUNKS)
    tm = _pick(t, (256, 128))

    def body(x_ref, g_ref, b_ref, o_ref, h_ref):
        xv = x_ref[...]
        r = lax.rsqrt(jnp.mean(xv * xv, axis=-1, keepdims=True) + EPS)
        h = (xv * r * g_ref[...]).astype(BF16)
        h_ref[...] = h
        for c in range(0, n, tn):
            o_ref[:, c:c + tn] = _dot(h, b_ref[c:c + tn, :], 1, 1).astype(out_dtype)

    row = pl.BlockSpec((tm, d), lambda i: (i, 0))
    return pl.pallas_call(
        body, name=name, grid=(t // tm,),
        in_specs=[row, pl.BlockSpec((1, d), lambda i: (0, 0)), pl.BlockSpec(bt.shape, lambda i: (0, 0))],
        out_specs=[pl.BlockSpec((tm, n), lambda i: (i, 0)), row],
        out_shape=[jax.ShapeDtypeStruct((t, n), out_dtype), jax.ShapeDtypeStruct((t, d), BF16)],
        compiler_params=_cp(("parallel",)),
    )(x, g, bt)


def _mm_rms_bwd(a, b, b_off, dh_prev, x, g, dres, name, comm=None):
    t, d = x.shape
    kdim = a.shape[1]
    assert b_off % kdim == 0 and b.shape[1] == d, (a.shape, b.shape, b_off)
    tm = _pick(t, (256, 128))

    def body(*refs):
        if dh_prev is None:
            a_ref, b_ref, x_ref, g_ref, dr_ref, dx_ref, dxb_ref, dg_ref = refs
            dhv = _dot(a_ref[...], b_ref[...], 1, 0)
        else:
            a_ref, b_ref, p_ref, x_ref, g_ref, dr_ref, dx_ref, dxb_ref, dg_ref = refs
            dhv = _dot(a_ref[...], b_ref[...], 1, 0) + p_ref[...]
        xv = x_ref[...]
        r = lax.rsqrt(jnp.mean(xv * xv, axis=-1, keepdims=True) + EPS)
        xh = xv * r
        dxh = dhv * g_ref[...]
        dx = dr_ref[...] + r * (dxh - xh * jnp.mean(dxh * xh, axis=-1, keepdims=True))
        dx_ref[...] = dx
        dxb_ref[...] = dx.astype(BF16)

        @pl.when(pl.program_id(0) == 0)
        def _():
            dg_ref[...] = jnp.zeros_like(dg_ref)

        dg_ref[...] += jnp.sum(dhv * xh, axis=0, keepdims=True)

    row = pl.BlockSpec((tm, d), lambda i: (i, 0))
    vec = pl.BlockSpec((1, d), lambda i: (0, 0))
    in_specs = [pl.BlockSpec((tm, kdim), lambda i: (i, 0)), pl.BlockSpec((kdim, d), lambda i: (b_off // kdim, 0))]
    args = [a, b]
    if dh_prev is not None:
        in_specs.append(row)
        args.append(dh_prev)
    outs, got = _call(
        body, name=name, grid=(t // tm,), in_specs=in_specs + [row, vec, row], out_specs=[row, row, vec],
        out_shape=[jax.ShapeDtypeStruct((t, d), F32), jax.ShapeDtypeStruct((t, d), BF16), jax.ShapeDtypeStruct((1, d), F32)],
        args=(*args, x, g, dres), sem=("arbitrary",), comm=comm)
    return (*outs, got) if comm is not None else tuple(outs)


def _mm_loss(a, b, res, target, name):
    t, kdim = a.shape
    d = b.shape[1]
    tm = _pick(t, (512, 256, 128))

    def body(a_ref, b_ref, r_ref, t_ref, dy_ref, dyb_ref, acc_ref):
        @pl.when(pl.program_id(0) == 0)
        def _():
            acc_ref[...] = jnp.zeros_like(acc_ref)

        err = _dot(a_ref[...], b_ref[...], 1, 0) + r_ref[...] - t_ref[...]
        dy = err * (1.0 / d)
        dy_ref[...] = dy
        dyb_ref[...] = dy.astype(BF16)
        acc_ref[...] += jnp.sum(err * err, axis=0, keepdims=True)

    row = pl.BlockSpec((tm, d), lambda i: (i, 0))
    vec = pl.BlockSpec((1, d), lambda i: (0, 0))
    return pl.pallas_call(
        body, name=name, grid=(t // tm,),
        in_specs=[pl.BlockSpec((tm, kdim), lambda i: (i, 0)), pl.BlockSpec((kdim, d), lambda i: (0, 0)), row, row],
        out_specs=[row, row, vec],
        out_shape=[jax.ShapeDtypeStruct((t, d), F32), jax.ShapeDtypeStruct((t, d), BF16), jax.ShapeDtypeStruct((1, d), F32)],
        compiler_params=_cp(("arbitrary",)),
    )(a, b, res, target)


def _head_sums(v):
    ri = lax.broadcasted_iota(jnp.int32, (LANES, LANES), 0) // HEAD_DIM
    ci = lax.broadcasted_iota(jnp.int32, (LANES, LANES), 1) // HEAD_DIM
    ones = (ri == ci).astype(BF16)
    hi = v.astype(BF16)
    lo_part = (v - hi.astype(F32)).astype(BF16)
    return _dot(hi, ones, 1, 0) + _dot(lo_part, ones, 1, 0)


def _head_rms(xs):
    r = lax.rsqrt(_head_sums(xs * xs) * (1.0 / HEAD_DIM) + EPS)
    return xs * r, r


def _head_rms_bwd(xs, w, dy):
    xh, r = _head_rms(xs)
    dxh = dy * w
    mm = _head_sums(dxh * xh) * (1.0 / HEAD_DIM)
    return r * (dxh - xh * mm), dy * xh


_QSCALE = HEAD_DIM ** -0.5


def _headnorm_fwd(proj, ws, name):
    t = proj.shape[0]
    tm = TQ
    lead = PAD_A // tm
    leadb = PAD_B // tm

    def body(p_ref, w_ref, qa_ref, ka_ref, va_ref, qb_ref, kb_ref, vb_ref):
        data = pl.program_id(0) >= lead
        lo = _lo_mask()

        def put(ref, c, val):
            ref[:, c:c + val.shape[1]] = jnp.where(data, val, 0.0).astype(BF16)

        def per_query_head(slab):
            other = pltpu.roll(slab, HEAD_DIM, 1)
            e0, e1 = jnp.where(lo, slab, other), jnp.where(lo, other, slab)
            return jnp.concatenate([e0, e0, e1, e1], axis=1)

        for s in range(D_ATT // LANES):
            c = LANES * s
            xh, _ = _head_rms(p_ref[:, COL_QA + c:COL_QA + c + LANES])
            qa_ref[:, c:c + LANES] = (xh * w_ref[0:1, :] * _QSCALE).astype(BF16)
            xh, _ = _head_rms(p_ref[:, COL_KA + c:COL_KA + c + LANES])
            put(ka_ref, c, xh * w_ref[1:2, :])
            xh, _ = _head_rms(p_ref[:, COL_QB + c:COL_QB + c + LANES])
            qb_ref[:, c:c + LANES] = (xh * w_ref[2:3, :] * _QSCALE).astype(BF16)
        put(va_ref, 0, p_ref[:, COL_VA:COL_VA + D_ATT])
        xh, _ = _head_rms(p_ref[:, COL_KB:COL_KB + LANES])
        put(kb_ref, 0, per_query_head(xh * w_ref[3:4, :]))
        put(vb_ref, 0, per_query_head(p_ref[:, COL_VB:COL_VB + LANES]))

    src = lambda i: jnp.maximum(i - lead, 0)
    wide = pl.BlockSpec((tm, D_ATT), lambda i: (src(i), 0))
    pad_a = pl.BlockSpec((tm, D_ATT), lambda i: (i, 0))
    pad_b = pl.BlockSpec((tm, D_ATT), lambda i: (jnp.maximum(i - lead + leadb, 0), 0))
    sd = lambda rows: jax.ShapeDtypeStruct((rows, D_ATT), BF16)
    return pl.pallas_call(
        body, name=name, grid=(t // tm + lead,),
        in_specs=[pl.BlockSpec((tm, ATTN_PROJ), lambda i: (src(i), 0)), pl.BlockSpec((4, LANES), lambda i: (0, 0))],
        out_specs=[wide, pad_a, pad_a, wide, pad_b, pad_b],
        out_shape=[sd(t), sd(t + PAD_A), sd(t + PAD_A), sd(t), sd(t + PAD_B), sd(t + PAD_B)],
        compiler_params=_cp(("arbitrary",)),
    )(proj, ws)


def _headnorm_bwd(proj, ws, dqa, dkpa, dvpa, dqb, dkpb, dvpb, name):
    t = proj.shape[0]
    tm = TQ
    offa, offb = PAD_A // tm, PAD_B // tm

    def body(p_ref, w_ref, dqa_ref, dka_ref, dva_ref, dqb_ref, dkb_ref, dvb_ref, dp_ref, dw_ref):
        i = pl.program_id(0)
        lo = _lo_mask()

        @pl.when(i == 0)
        def _():
            dw_ref[...] = jnp.zeros_like(dw_ref)

        acc = [jnp.zeros((1, LANES), F32) for _ in range(4)]
        for s in range(D_ATT // LANES):
            c = LANES * s
            dx, dwl = _head_rms_bwd(p_ref[:, COL_QA + c:COL_QA + c + LANES], w_ref[0:1, :], dqa_ref[:, c:c + LANES] * _QSCALE)
            dp_ref[:, COL_QA + c:COL_QA + c + LANES] = dx.astype(BF16)
            acc[0] += jnp.sum(dwl, axis=0, keepdims=True)
            dx, dwl = _head_rms_bwd(p_ref[:, COL_KA + c:COL_KA + c + LANES], w_ref[1:2, :], dka_ref[:, c:c + LANES])
            dp_ref[:, COL_KA + c:COL_KA + c + LANES] = dx.astype(BF16)
            acc[1] += jnp.sum(dwl, axis=0, keepdims=True)
            dx, dwl = _head_rms_bwd(p_ref[:, COL_QB + c:COL_QB + c + LANES], w_ref[2:3, :], dqb_ref[:, c:c + LANES] * _QSCALE)
            dp_ref[:, COL_QB + c:COL_QB + c + LANES] = dx.astype(BF16)
            acc[2] += jnp.sum(dwl, axis=0, keepdims=True)
        dp_ref[:, COL_VA:COL_VA + D_ATT] = dva_ref[...].astype(BF16)

        def group_sum(ref):
            s0 = ref[:, 0:LANES] + ref[:, LANES:2 * LANES]
            s1 = ref[:, 2 * LANES:3 * LANES] + ref[:, 3 * LANES:4 * LANES]
            s0 = s0 + pltpu.roll(s0, HEAD_DIM, 1)
            s1 = s1 + pltpu.roll(s1, HEAD_DIM, 1)
            return jnp.where(lo, s0, s1)

        dx, dwl = _head_rms_bwd(p_ref[:, COL_KB:COL_KB + LANES], w_ref[3:4, :], group_sum(dkb_ref))
        dp_ref[:, COL_KB:COL_KB + LANES] = dx.astype(BF16)
        acc[3] += jnp.sum(dwl, axis=0, keepdims=True)
        dp_ref[:, COL_VB:COL_VB + LANES] = group_sum(dvb_ref).astype(BF16)
        for n in range(4):
            dw_ref[n:n + 1, :] += acc[n]

    wide = pl.BlockSpec((tm, D_ATT), lambda i: (i, 0))
    pa = pl.BlockSpec((tm, D_ATT), lambda i: (i + offa, 0))
    pb = pl.BlockSpec((tm, D_ATT), lambda i: (i + offb, 0))
    whole = pl.BlockSpec((tm, ATTN_PROJ), lambda i: (i, 0))
    return pl.pallas_call(
        body, name=name, grid=(t // tm,),
        in_specs=[whole, pl.BlockSpec((4, LANES), lambda i: (0, 0)), wide, pa, pa, wide, pb, pb],
        out_specs=[whole, pl.BlockSpec((4, LANES), lambda i: (0, 0))],
        out_shape=[jax.ShapeDtypeStruct((t, ATTN_PROJ), BF16), jax.ShapeDtypeStruct((4, LANES), F32)],
        compiler_params=_cp(("arbitrary",)),
    )(proj, ws, dqa, dkpa, dvpa, dqb, dkpb, dvpb)


ROLL_W = 1024


def _rel_onehot():
    r_io = lax.broadcasted_iota(jnp.int32, (REL_W, ROLL_W), 0)
    m_io = lax.broadcasted_iota(jnp.int32, (REL_W, ROLL_W), 1)
    return (r_io == jnp.clip(REL_W - 1 - m_io, -MAX_REL, MAX_REL) + MAX_REL).astype(F32)


def _relpos_fwd(table, name):
    def body(t_ref, o_ref):
        rr = _dot(t_ref[...], _rel_onehot(), 1, 0, HI)

        def step(q, c):
            o_ref[q] = pltpu.roll(rr, (ROLL_W - (TQ - 1) + q) % ROLL_W, 1)[:, :REL_W]
            return c

        lax.fori_loop(0, TQ, step, 0)

    return pl.pallas_call(
        body, name=name, out_shape=jax.ShapeDtypeStruct((TQ, N_HEADS, REL_W), F32),
        in_specs=[pl.BlockSpec(memory_space=pltpu.VMEM)], out_specs=pl.BlockSpec(memory_space=pltpu.VMEM),
        compiler_params=_cp(),
    )(table)


def _relpos_bwd(dbias_t, name):
    def body(d_ref, o_ref):
        def step(q, acc):
            row = jnp.concatenate([d_ref[q], jnp.zeros((N_HEADS, ROLL_W - REL_W), F32)], axis=1)
            return acc + pltpu.roll(row, TQ - 1 - q, 1)

        drr = lax.fori_loop(0, TQ, step, jnp.zeros((N_HEADS, ROLL_W), F32))
        o_ref[...] = _dot(drr, _rel_onehot(), 1, 1, HI)

    return pl.pallas_call(
        body, name=name, out_shape=jax.ShapeDtypeStruct((N_HEADS, REL_W), F32),
        in_specs=[pl.BlockSpec(memory_space=pltpu.VMEM)], out_specs=pl.BlockSpec(memory_space=pltpu.VMEM),
        compiler_params=_cp(),
    )(dbias_t)


def _attn_scores(qe, kw, bias, kvalid):
    return jnp.where(kvalid, _dot(qe, kw, 1, 1) + bias, NEG)


def _stat_cols(stats, e):
    return stats[:, 64 * e:64 * e + 1], stats[:, 64 * e + 32:64 * e + 33]


def _attn_fwd(q, kp, vp, bias, sinks, pad, name, comm=None):
    t, hd = q.shape
    w = pad + TQ

    def body(sink_ref, q_ref, k_ref, v_ref, b_ref, o_ref, st_ref):
        hp, i = pl.program_id(0), pl.program_id(1)
        lo = _lo_mask()
        lane = lax.broadcasted_iota(jnp.int32, (1, LANES), 1)
        for j in range(ATT_SUB):
            start = pl.multiple_of((i * ATT_SUB + j) * TQ, TQ)
            qv = q_ref[TQ * j:TQ * (j + 1), :]
            kw = k_ref[pl.ds(start, w), :]
            vw = v_ref[pl.ds(start, w), :]
            kvalid = (start + lax.broadcasted_iota(jnp.int32, (1, w), 1)) >= pad
            outs, ms, ls = [], [], []
            for e in range(2):
                sel = lo if e == 0 else jnp.logical_not(lo)
                qe = jnp.where(sel, qv, jnp.zeros_like(qv))
                snk = sink_ref[2 * hp + e]
                s = _attn_scores(qe, kw, b_ref[e], kvalid)
                m = jnp.maximum(jnp.max(s, axis=-1, keepdims=True), snk)
                acc = _dot(jnp.exp(s - m).astype(BF16), jnp.where(sel, vw, jnp.ones_like(vw)), 1, 0)
                denom = acc[:, 64 * (1 - e):64 * (1 - e) + 1] + jnp.exp(snk - m)
                outs.append(acc * (1.0 / denom))
                ms.append(m)
                ls.append(denom)
            o_ref[TQ * j:TQ * (j + 1), :] = jnp.where(lo, outs[0], outs[1]).astype(BF16)
            st_ref[TQ * j:TQ * (j + 1), :] = jnp.where(lane < 32, ms[0], jnp.where(lane < 64, ls[0],
                                                                                 jnp.where(lane < 96, ms[1], ls[1])))

    full = pl.BlockSpec((t + pad, LANES), lambda h, i: (0, h))
    tile = pl.BlockSpec((ATT_SUB * TQ, LANES), lambda h, i: (i, h))
    (o, stats), got = _call(
        body, name=name, grid=(hd // LANES, t // (ATT_SUB * TQ)),
        in_specs=[pl.BlockSpec(memory_space=pltpu.SMEM), tile, full, full, pl.BlockSpec((2, TQ, w), lambda h, i: (h, 0, 0))],
        out_specs=[tile, tile], out_shape=[jax.ShapeDtypeStruct((t, hd), BF16), jax.ShapeDtypeStruct((t, hd), F32)],
        args=(sinks, q, kp, vp, bias), sem=("parallel", "arbitrary"), comm=comm)
    return o, stats, got


def _attn_bwd(q, kp, vp, bias, sinks, do, stats, o, col_off, pad, name, comm=None):
    t, hd = q.shape
    w = pad + TQ
    nhp = hd // LANES

    def body(sink_ref, q_ref, k_ref, v_ref, b_ref, do_ref, st_ref, o_ref, dq_ref, dk_ref, dv_ref, db_ref, ds_ref):
        hp, i = pl.program_id(0), pl.program_id(1)

        @pl.when(i == 0)
        def _():
            dk_ref[...] = jnp.zeros_like(dk_ref)
            dv_ref[...] = jnp.zeros_like(dv_ref)
            db_ref[...] = jnp.zeros_like(db_ref)
            ds_ref[...] = jnp.zeros_like(ds_ref)

        lo = _lo_mask()
        row8 = lax.broadcasted_iota(jnp.int32, (8, LANES), 0)
        dbias = [None, None]
        dsink = jnp.zeros((8, LANES), F32)
        for j in range(ATT_SUB):
            start = pl.multiple_of((i * ATT_SUB + j) * TQ, TQ)
            qv = q_ref[TQ * j:TQ * (j + 1), :]
            dov = do_ref[TQ * j:TQ * (j + 1), :]
            kw = k_ref[pl.ds(start, w), :]
            vw = v_ref[pl.ds(start, w), :]
            kvalid = (start + lax.broadcasted_iota(jnp.int32, (1, w), 1)) >= pad
            stats = st_ref[TQ * j:TQ * (j + 1), :]
            od = dov.astype(F32) * o_ref[TQ * j:TQ * (j + 1), :].astype(F32)
            dqs, dkw, dvw = [], None, None
            for e in range(2):
                sel = lo if e == 0 else jnp.logical_not(lo)
                qe = jnp.where(sel, qv, jnp.zeros_like(qv))
                doe = jnp.where(sel, dov, jnp.zeros_like(dov))
                m, denom = _stat_cols(stats, e)
                inv = 1.0 / denom
                p = jnp.exp(_attn_scores(qe, kw, b_ref[e], kvalid) - m) * inv
                psink = jnp.exp(sink_ref[2 * hp + e] - m) * inv
                dp = _dot(doe, vw, 1, 1)
                delta = jnp.sum(jnp.where(sel, od, 0.0), axis=-1, keepdims=True)
                ds = p * (dp - delta)
                dbias[e] = ds if dbias[e] is None else dbias[e] + ds
                dsink = dsink + jnp.where(row8 == e, jnp.sum(-psink * delta, axis=0, keepdims=True), 0.0)
                dsb = ds.astype(BF16)
                dqs.append(_dot(dsb, kw, 1, 0))
                dk_e = _dot(dsb, qe, 0, 0)
                dv_e = _dot(p.astype(BF16), doe, 0, 0)
                dkw = dk_e if dkw is None else dkw + dk_e
                dvw = dv_e if dvw is None else dvw + dv_e
            dq_ref[TQ * j:TQ * (j + 1), :] = jnp.where(lo, dqs[0], dqs[1])
            dk_ref[pl.ds(start, w), :] += dkw
            dv_ref[pl.ds(start, w), :] += dvw
        for e in range(2):
            db_ref[e] += dbias[e]
        ds_ref[0] += dsink

    full = pl.BlockSpec((t + pad, LANES), lambda h, i: (0, h))
    tile = pl.BlockSpec((ATT_SUB * TQ, LANES), lambda h, i: (i, h))
    btile = pl.BlockSpec((2, TQ, w), lambda h, i: (h, 0, 0))
    return _call(
        body, name=name, grid=(nhp, t // (ATT_SUB * TQ)),
        in_specs=[pl.BlockSpec(memory_space=pltpu.SMEM), tile, full, full, btile,
                  pl.BlockSpec((ATT_SUB * TQ, LANES), lambda h, i: (i, h + col_off)), tile, tile],
        out_specs=[tile, full, full, btile, pl.BlockSpec((1, 8, LANES), lambda h, i: (h, 0, 0))],
        out_shape=[jax.ShapeDtypeStruct((t, hd), F32), jax.ShapeDtypeStruct((t + pad, hd), F32),
                   jax.ShapeDtypeStruct((t + pad, hd), F32), jax.ShapeDtypeStruct((N_HEADS, TQ, w), F32),
                   jax.ShapeDtypeStruct((nhp, 8, LANES), F32)],
        args=(sinks, q, kp, vp, bias, do, stats, o), sem=("parallel", "arbitrary"), comm=comm)


def _conv_apply(taps, w_ref, ktaps):
    out = taps[0] * w_ref[ktaps - 1:ktaps, :]
    for s in range(1, ktaps):
        out = out + taps[s] * w_ref[ktaps - 1 - s:ktaps - s, :]
    return out


def _sigmoid(x):
    return jax.nn.sigmoid(x)


def _silu_grad(x):
    sg = _sigmoid(x)
    return x * sg, sg * (1.0 + x * (1.0 - sg))


FFN_HALO = 16
FFN_BT = 256
FFN_BC = 1408


def _ffn_in_mid(x, g, wt, w8, b, name):
    t, d = x.shape
    f = D_FF
    tm = FFN_BT

    def body(x_ref, g_ref, b_ref, w_ref, cb_ref, gu_ref, h_ref, a_ref, gc_ref, halo_ref):
        @pl.when(pl.program_id(0) == 0)
        def _():
            halo_ref[...] = jnp.zeros_like(halo_ref)

        xv = x_ref[...]
        r = lax.rsqrt(jnp.mean(xv * xv, axis=-1, keepdims=True) + EPS)
        h = (xv * r * g_ref[...]).astype(BF16)
        h_ref[...] = h
        for c in range(0, f, FFN_BC):
            cs = slice(c, c + FFN_BC)
            gate = _dot(h, b_ref[c:c + FFN_BC, :], 1, 1).astype(BF16)
            up = _dot(h, b_ref[f + c:f + c + FFN_BC, :], 1, 1).astype(BF16)
            gu_ref[:, cs] = gate
            gu_ref[:, f + c:f + c + FFN_BC] = up
            gf = gate.astype(F32)
            ext = jnp.concatenate([halo_ref[:, cs], gf], axis=0)
            gc = (cb_ref[:, cs] + gf * w_ref[2:3, cs] + pltpu.roll(ext, 1, 0)[8:] * w_ref[1:2, cs]
                  + pltpu.roll(ext, 2, 0)[8:] * w_ref[0:1, cs])
            a_ref[:, cs] = (gc * _sigmoid(gc) * up.astype(F32)).astype(BF16)
            gc_ref[:, cs] = gc.astype(BF16)
            halo_ref[:, cs] = gf[tm - 8:]

    row = pl.BlockSpec((tm, d), lambda i: (i, 0))
    row_f = pl.BlockSpec((tm, f), lambda i: (i, 0))
    return pl.pallas_call(
        body, name=name, grid=(t // tm,),
        in_specs=[row, pl.BlockSpec((1, d), lambda i: (0, 0)), pl.BlockSpec((2 * f, d), lambda i: (0, 0)),
                  pl.BlockSpec((8, f), lambda i: (0, 0)), pl.BlockSpec((1, f), lambda i: (0, 0))],
        out_specs=[pl.BlockSpec((tm, 2 * f), lambda i: (i, 0)), row, row_f, row_f],
        out_shape=[jax.ShapeDtypeStruct((t, 2 * f), BF16), jax.ShapeDtypeStruct((t, d), BF16), jax.ShapeDtypeStruct((t, f), BF16),
                   jax.ShapeDtypeStruct((t, f), BF16)],
        scratch_shapes=[pltpu.VMEM((8, f), F32)], compiler_params=_cp(("arbitrary",)),
    )(x, g, wt, w8, b)


def _ffn_mid_bwd(gu, gc, dxb, w_out, w8, name):
    t, d = dxb.shape
    f = D_FF
    tm, hr = FFN_BT, FFN_HALO
    nt = t // tm
    n = tm + hr

    def body(g_ref, u_ref, un_ref, c_ref, cn_ref, dx_ref, dxn_ref, wo_ref, w_ref, dgu_ref, dw_ref, db_ref):
        i = pl.program_id(0)
        last = i == nt - 1

        @pl.when(i == 0)
        def _():
            dw_ref[...] = jnp.zeros_like(dw_ref)
            db_ref[...] = jnp.zeros_like(db_ref)

        dxe = jnp.concatenate([dx_ref[...], dxn_ref[...]], axis=0)
        row = lax.broadcasted_iota(jnp.int32, (n, 1), 0)
        keep = (row < tm) | jnp.logical_not(last)
        for c in range(0, f, FFN_BC):
            cs = slice(c, c + FFN_BC)
            act, dact = _silu_grad(jnp.concatenate([c_ref[:, cs], cn_ref[:, cs]], axis=0).astype(F32))
            da = _dot(dxe, wo_ref[cs, :], 1, 1)
            up = jnp.concatenate([u_ref[:, cs], un_ref[:, cs]], axis=0).astype(F32)
            dgc = jnp.where(keep, da * up * dact, 0.0)
            nxt = [dgc[:tm], pltpu.roll(dgc, n - 1, 0)[:tm], pltpu.roll(dgc, n - 2, 0)[:tm]]
            dgu_ref[:, f + c:f + c + FFN_BC] = (da[:tm] * act[:tm]).astype(BF16)
            dgu_ref[:, cs] = (nxt[0] * w_ref[2:3, cs] + nxt[1] * w_ref[1:2, cs] + nxt[2] * w_ref[0:1, cs]).astype(BF16)
            gate = g_ref[:, cs].astype(F32)
            db_ref[:, cs] += jnp.sum(nxt[0], axis=0, keepdims=True)
            for s in range(3):
                dw_ref[2 - s:3 - s, cs] += jnp.sum(nxt[s] * gate, axis=0, keepdims=True)

    r = tm // hr
    nxt_blk = lambda i: jnp.minimum((i + 1) * r, t // hr - 1)
    row_f = pl.BlockSpec((tm, f), lambda i: (i, 0))
    halo_f = pl.BlockSpec((hr, f), lambda i: (nxt_blk(i), 0))
    return pl.pallas_call(
        body, name=name, grid=(nt,),
        in_specs=[row_f, pl.BlockSpec((tm, f), lambda i: (i, 1)), pl.BlockSpec((hr, f), lambda i: (nxt_blk(i), 1)),
                  row_f, halo_f,
                  pl.BlockSpec((tm, d), lambda i: (i, 0)), pl.BlockSpec((hr, d), lambda i: (nxt_blk(i), 0)),
                  pl.BlockSpec((f, d), lambda i: (0, 0)), pl.BlockSpec((8, f), lambda i: (0, 0))],
        out_specs=[pl.BlockSpec((tm, 2 * f), lambda i: (i, 0)), pl.BlockSpec((8, f), lambda i: (0, 0)),
                   pl.BlockSpec((1, f), lambda i: (0, 0))],
        out_shape=[jax.ShapeDtypeStruct((t, 2 * f), BF16), jax.ShapeDtypeStruct((8, f), F32), jax.ShapeDtypeStruct((1, f), F32)],
        compiler_params=_cp(("arbitrary",)),
    )(gu, gu, gu, gc, gc, dxb, dxb, w_out, w8)


PRE_TM = 256
PRE_TC = 1024


def _ssm_in_pre(x, g, wt, w8, b, name):
    t, d = x.shape
    tm, tc = PRE_TM, PRE_TC

    def body(x_ref, g_ref, b_ref, w_ref, cb_ref, zx_ref, h_ref, o_ref, c_ref, halo_ref):
        @pl.when(pl.program_id(0) == 0)
        def _():
            halo_ref[...] = jnp.zeros_like(halo_ref)

        xv = x_ref[...]
        r = lax.rsqrt(jnp.mean(xv * xv, axis=-1, keepdims=True) + EPS)
        h = (xv * r * g_ref[...]).astype(BF16)
        h_ref[...] = h
        for c in range(0, ZX, tc):
            v = _dot(h, b_ref[c:c + tc, :], 1, 1)
            zx_ref[:, c:c + tc] = v
            if c >= D_INNER:
                cs = slice(c - D_INNER, c - D_INNER + tc)
                ext = jnp.concatenate([halo_ref[:, cs], v], axis=0)
                conv = cb_ref[:, cs] + v * w_ref[3:4, cs]
                for s in (1, 2, 3):
                    conv = conv + pltpu.roll(ext, s, 0)[8:] * w_ref[3 - s:4 - s, cs]
                o_ref[:, cs] = conv * _sigmoid(conv)
                c_ref[:, cs] = conv.astype(BF16)
                halo_ref[:, cs] = v[tm - 8:]

    row = pl.BlockSpec((tm, d), lambda i: (i, 0))
    row_x = pl.BlockSpec((tm, XBC), lambda i: (i, 0))
    return pl.pallas_call(
        body, name=name, grid=(t // tm,),
        in_specs=[row, pl.BlockSpec((1, d), lambda i: (0, 0)), pl.BlockSpec(wt.shape, lambda i: (0, 0)),
                  pl.BlockSpec((8, XBC), lambda i: (0, 0)), pl.BlockSpec((1, XBC), lambda i: (0, 0))],
        out_specs=[pl.BlockSpec((tm, ZX), lambda i: (i, 0)), row, row_x, row_x],
        out_shape=[jax.ShapeDtypeStruct((t, ZX), F32), jax.ShapeDtypeStruct((t, d), BF16), jax.ShapeDtypeStruct((t, XBC), F32),
                   jax.ShapeDtypeStruct((t, XBC), BF16)],
        scratch_shapes=[pltpu.VMEM((8, XBC), F32)], compiler_params=_cp(("arbitrary",)),
    )(x, g, wt, w8, b)


PRE_HALO = 16


def _ssm_pre_bwd(zx, conv, dxbc, w8, name):
    t = zx.shape[0]
    tm, tc, hr = PRE_TM, PRE_TC, PRE_HALO
    off = D_INNER // tc
    nt = t // tm
    n = tm + hr

    def body(x_ref, c_ref, cn_ref, d_ref, dn_ref, w_ref, o_ref, dw_ref, db_ref):
        i = pl.program_id(1)
        last = i == nt - 1

        @pl.when(i == 0)
        def _():
            dw_ref[...] = jnp.zeros_like(dw_ref)
            db_ref[...] = jnp.zeros_like(db_ref)

        _, dact = _silu_grad(jnp.concatenate([c_ref[...], cn_ref[...]], axis=0).astype(F32))
        row = lax.broadcasted_iota(jnp.int32, (n, 1), 0)
        dc = jnp.where((row < tm) | jnp.logical_not(last), jnp.concatenate([d_ref[...], dn_ref[...]], axis=0) * dact, 0.0)
        nxt = [dc[:tm]] + [pltpu.roll(dc, n - s, 0)[:tm] for s in (1, 2, 3)]
        o_ref[...] = _conv_apply(nxt, w_ref, 4).astype(BF16)
        xv = x_ref[...]
        db_ref[...] += jnp.sum(nxt[0], axis=0, keepdims=True)
        for s in range(4):
            dw_ref[3 - s:4 - s, :] += jnp.sum(nxt[s] * xv, axis=0, keepdims=True)

    nxt_blk = lambda i: jnp.minimum((i + 1) * (tm // hr), t // hr - 1)
    tile = pl.BlockSpec((tm, tc), lambda j, i: (i, j))
    halo = pl.BlockSpec((hr, tc), lambda j, i: (nxt_blk(i), j))
    return pl.pallas_call(
        body, name=name, grid=(XBC // tc, nt),
        in_specs=[pl.BlockSpec((tm, tc), lambda j, i: (i, j + off)), tile, halo, tile, halo,
                  pl.BlockSpec((8, tc), lambda j, i: (0, j))],
        out_specs=[tile, pl.BlockSpec((8, tc), lambda j, i: (0, j)), pl.BlockSpec((1, tc), lambda j, i: (0, j))],
        out_shape=[jax.ShapeDtypeStruct((t, XBC), BF16), jax.ShapeDtypeStruct((8, XBC), F32),
                   jax.ShapeDtypeStruct((1, XBC), F32)],
        compiler_params=_cp(("parallel", "arbitrary")),
    )(zx, conv, conv, dxbc, dxbc, w8)


def _head_lanes():
    return lax.broadcasted_iota(jnp.int32, (1, LANES), 1) < SSM_HEADS


def _dt_fwd(dtraw, bias, name):
    t = dtraw.shape[0]
    tm = _pick(t, (1024, 512, 256, 128))

    def body(x_ref, b_ref, o_ref):
        v = x_ref[...] + b_ref[...]
        sp = jnp.maximum(v, 0.0) + jnp.log(1.0 + jnp.exp(-jnp.abs(v)))
        o_ref[...] = jnp.where(_head_lanes(), sp, 0.0)

    row = pl.BlockSpec((tm, LANES), lambda i: (i, 0))
    return pl.pallas_call(
        body, name=name, grid=(t // tm,), in_specs=[row, pl.BlockSpec((1, LANES), lambda i: (0, 0))], out_specs=row,
        out_shape=jax.ShapeDtypeStruct((t, LANES), F32), compiler_params=_cp(("parallel",)),
    )(dtraw, bias)


def _dt_bwd(dtraw, bias, ddt, name):
    t = dtraw.shape[0]
    tm = _pick(t, (1024, 512, 256, 128))

    def body(x_ref, b_ref, d_ref, o_ref, db_ref):
        @pl.when(pl.program_id(0) == 0)
        def _():
            db_ref[...] = jnp.zeros_like(db_ref)

        g = jnp.where(_head_lanes(), d_ref[...] * _sigmoid(x_ref[...] + b_ref[...]), 0.0)
        o_ref[...] = g.astype(BF16)
        db_ref[...] += jnp.sum(g, axis=0, keepdims=True)

    row = pl.BlockSpec((tm, LANES), lambda i: (i, 0))
    vec = pl.BlockSpec((1, LANES), lambda i: (0, 0))
    return pl.pallas_call(
        body, name=name, grid=(t // tm,), in_specs=[row, vec, row], out_specs=[row, vec],
        out_shape=[jax.ShapeDtypeStruct((t, LANES), BF16), jax.ShapeDtypeStruct((1, LANES), F32)],
        compiler_params=_cp(("arbitrary",)),
    )(dtraw, bias, ddt)


GROUP_W = D_INNER // SSM_GROUPS


def _ssd_common(dt, alog):
    ll = dt.shape[0]
    a_neg = -jnp.exp(alog)
    a = dt * a_neg
    ri = lax.broadcasted_iota(jnp.int32, (ll, ll), 0)
    ci = lax.broadcasted_iota(jnp.int32, (ll, ll), 1)
    tril = ri >= ci
    acs = _dot(tril.astype(F32), a, 1, 0, HI)
    return a_neg, tril, acs, acs.T


def _pair_terms(acs, acs_t, dt, h0, lo):
    ll = acs.shape[0]
    cols = [acs[:, h0 + e:h0 + e + 1] for e in range(2)]
    rows = [acs_t[h0 + e:h0 + e + 1, :] for e in range(2)]
    dtc = [dt[:, h0 + e:h0 + e + 1] for e in range(2)]
    lasts = [c[ll - 1:ll, :] for c in cols]
    dtx = jnp.where(lo, dtc[0], dtc[1])
    eac = jnp.where(lo, jnp.exp(cols[0]), jnp.exp(cols[1]))
    fdec = jnp.where(lo, jnp.exp(lasts[0] - cols[0]), jnp.exp(lasts[1] - cols[1]))
    elast = jnp.where(lo, jnp.exp(lasts[0]), jnp.exp(lasts[1]))
    return cols, rows, dtx, eac, fdec, elast


def _decay(col, row, tril):
    return jnp.where(tril, jnp.exp(jnp.minimum(col - row, 0.0)), 0.0)


def _two_heads_rows(v, lo):
    z = jnp.zeros_like(v)
    return jnp.concatenate([jnp.where(lo, v, z), jnp.where(lo, z, v)], axis=0)


def _two_heads_cols(ms):
    return jnp.concatenate(ms, axis=1)


def _z_group(z_refs, g):
    return z_refs[g // 2][:, GROUP_W * (g % 2):GROUP_W * (g % 2 + 1)]


def _ssd_fwd(xbc, dt, alog, zx, dexp, nw, name, comm=None):
    t = xbc.shape[0]
    ll = SSD_L
    nc = t // ll

    def body(x_ref, dt_ref, al_ref, z0_ref, z1_ref, d_ref, w_ref, y_ref, sp_ref, y4_ref, st_ref):
        @pl.when(pl.program_id(0) == 0)
        def _():
            st_ref[...] = jnp.zeros_like(st_ref)

        dtv = dt_ref[...]
        _, tril, acs, acs_t = _ssd_common(dtv, al_ref[...])
        lo = _lo_mask()
        sp_ref[0] = st_ref[...]
        for g in range(SSM_GROUPS):
            bg = x_ref[:, D_INNER + SSM_STATE * g:D_INNER + SSM_STATE * (g + 1)].astype(BF16)
            cg = x_ref[:, D_INNER + 512 + SSM_STATE * g:D_INNER + 512 + SSM_STATE * (g + 1)].astype(BF16)
            gm = _dot(cg, bg, 1, 1)
            g0 = GROUP_W * g
            terms = [_pair_terms(acs, acs_t, dtv, 8 * g + 2 * pp, lo) for pp in range(4)]
            dtx, eac, fdec, elast = [jnp.concatenate([tt[k] for tt in terms], axis=1) for k in (2, 3, 4, 5)]
            xg = x_ref[:, g0:g0 + GROUP_W]
            ug = (xg * dtx).astype(BF16)
            sg = st_ref[:, g0:g0 + GROUP_W]
            yst = _dot(cg, sg.astype(BF16), 1, 0) * eac
            st_ref[:, g0:g0 + GROUP_W] = sg * elast + _dot(bg, (xg * (fdec * dtx)).astype(BF16), 0, 0)
            ys = []
            for pp in range(4):
                cols, rows = terms[pp][0], terms[pp][1]
                sl = slice(LANES * pp, LANES * (pp + 1))
                y_in = _dot(_two_heads_cols([(gm * _decay(cols[e], rows[e], tril)).astype(BF16) for e in range(2)]),
                            _two_heads_rows(ug[:, sl], lo), 1, 0)
                ys.append(y_in + yst[:, sl])
            yg = jnp.concatenate(ys, axis=1)
            y_ref[:, g0:g0 + GROUP_W] = yg
            zg = _z_group((z0_ref, z1_ref), g)
            y3 = (yg + d_ref[:, g0:g0 + GROUP_W] * xg) * (zg * _sigmoid(zg))
            r = lax.rsqrt(jnp.mean(y3 * y3, axis=-1, keepdims=True) + EPS)
            y4_ref[:, g0:g0 + GROUP_W] = (y3 * r * w_ref[:, g0:g0 + GROUP_W]).astype(BF16)

    zblk = lambda j: pl.BlockSpec((ll, 1024), lambda c: (c, j))
    vec = pl.BlockSpec((1, D_INNER), lambda c: (0, 0))
    row = pl.BlockSpec((ll, D_INNER), lambda c: (c, 0))
    return _call(
        body, name=name, grid=(nc,),
        in_specs=[pl.BlockSpec((ll, XBC), lambda c: (c, 0)), pl.BlockSpec((ll, LANES), lambda c: (c, 0)),
                  pl.BlockSpec((1, LANES), lambda c: (0, 0)), zblk(0), zblk(1), vec, vec],
        out_specs=[row, pl.BlockSpec((1, SSM_STATE, D_INNER), lambda c: (c, 0, 0)), row],
        out_shape=[jax.ShapeDtypeStruct((t, D_INNER), F32), jax.ShapeDtypeStruct((nc, SSM_STATE, D_INNER), F32),
                   jax.ShapeDtypeStruct((t, D_INNER), BF16)],
        scratch_shapes=[pltpu.VMEM((SSM_STATE, D_INNER), F32)],
        args=(xbc, dt, alog, zx, zx, dexp, nw), sem=("arbitrary",), comm=comm)


def _ssd_bwd(xbc, dt, alog, sprev, dy4, y, zx, dexp, nw, name, comm=None):
    t = xbc.shape[0]
    ll = SSD_L_BWD if t % SSD_L_BWD == 0 else SSD_L
    nc = t // ll
    every = ll // SSD_L

    def body(x_ref, dt_ref, al_ref, sp_ref, g4_ref, y_ref, z0_ref, z1_ref, d_ref, w_ref,
             dx_ref, ddt_ref, dal_ref, dz_ref, dd_ref, dnw_ref, ds_ref, colt_ref):
        @pl.when(pl.program_id(0) == 0)
        def _():
            ds_ref[...] = jnp.zeros_like(ds_ref)
            dal_ref[...] = jnp.zeros_like(dal_ref)
            dd_ref[...] = jnp.zeros_like(dd_ref)
            dnw_ref[...] = jnp.zeros_like(dnw_ref)

        dtv = dt_ref[...]
        a_neg, tril, acs, acs_t = _ssd_common(dtv, al_ref[...])
        lo = _lo_mask()
        hi = jnp.logical_not(lo)
        lane = lax.broadcasted_iota(jnp.int32, (1, LANES), 1)
        colt_ref[...] = jnp.zeros_like(colt_ref)
        rowterm = jnp.zeros((ll, LANES), F32)
        ddt_u = jnp.zeros((ll, LANES), F32)
        dlast = jnp.zeros((1, LANES), F32)

        def halves(v):
            return (jnp.sum(jnp.where(lo, v, 0.0), axis=-1, keepdims=True),
                    jnp.sum(jnp.where(hi, v, 0.0), axis=-1, keepdims=True))

        for g in range(SSM_GROUPS):
            cb0 = D_INNER + SSM_STATE * g
            cc0 = D_INNER + 512 + SSM_STATE * g
            bg = x_ref[:, cb0:cb0 + SSM_STATE].astype(BF16)
            cg = x_ref[:, cc0:cc0 + SSM_STATE].astype(BF16)
            gm = _dot(cg, bg, 1, 1)
            g0 = GROUP_W * g
            terms = [_pair_terms(acs, acs_t, dtv, 8 * g + 2 * pp, lo) for pp in range(4)]
            dtx, eac, fdec, elast = [jnp.concatenate([tt[k] for tt in terms], axis=1) for k in (2, 3, 4, 5)]
            xg = x_ref[:, g0:g0 + GROUP_W]
            u32 = xg * dtx
            ug = u32.astype(BF16)
            zg = _z_group((z0_ref, z1_ref), g)
            dg = d_ref[:, g0:g0 + GROUP_W]
            act, dact = _silu_grad(zg)
            y2 = y_ref[:, g0:g0 + GROUP_W] + dg * xg
            y3 = y2 * act
            rn = lax.rsqrt(jnp.mean(y3 * y3, axis=-1, keepdims=True) + EPS)
            y3n = y3 * rn
            gv = g4_ref[:, g0:g0 + GROUP_W]
            dyn = gv * w_ref[:, g0:g0 + GROUP_W]
            dy3 = rn * (dyn - y3n * jnp.mean(dyn * y3n, axis=-1, keepdims=True))
            dyg = dy3 * act
            dskip = dyg * dg
            dz_ref[:, g0:g0 + GROUP_W] = (dy3 * y2 * dact).astype(BF16)
            dd_ref[:, g0:g0 + GROUP_W] += jnp.sum(dyg * xg, axis=0, keepdims=True)
            dnw_ref[:, g0:g0 + GROUP_W] += jnp.sum(gv * y3n, axis=0, keepdims=True)
            dyb = dyg.astype(BF16)
            spg = sp_ref[0, :, g0:g0 + GROUP_W]
            spb = spg.astype(BF16)
            dsg = ds_ref[:, g0:g0 + GROUP_W]
            dsb = dsg.astype(BF16)
            du_st = _dot(bg, dsb, 1, 0) * fdec
            yst = _dot(cg, spb, 1, 0) * eac
            dye = (dyg * eac).astype(BF16)
            dc_st = _dot(dye, spb, 1, 1)
            db_st = _dot((xg * (fdec * dtx)).astype(BF16), dsb, 1, 1)
            ds_ref[:, g0:g0 + GROUP_W] = dsg * elast + _dot(cg, dye, 0, 0)
            qst_el = du_st * u32
            rq_el = dyg * yst - qst_el
            q_row = jnp.sum(qst_el, axis=0, keepdims=True)
            s_row = jnp.sum(dsg * spg, axis=0, keepdims=True)
            dgm = jnp.zeros((ll, ll), F32)
            for pp in range(4):
                h0 = 8 * g + 2 * pp
                cols, rows = terms[pp][0], terms[pp][1]
                sl = slice(LANES * pp, LANES * (pp + 1))
                decs = [_decay(cols[e], rows[e], tril) for e in range(2)]
                wms = [gm * d for d in decs]
                dum2 = _dot(dyb[:, sl], _two_heads_rows(ug[:, sl], lo), 1, 1)
                du = _dot(jnp.concatenate([wm.astype(BF16) for wm in wms], axis=0),
                          _two_heads_rows(dyb[:, sl], lo), 0, 0) + du_st[:, sl]
                dx_ref[:, g0 + LANES * pp:g0 + LANES * (pp + 1)] = du * dtx[:, sl] + dskip[:, sl]
                ddtu = halves(du * xg[:, sl])
                rq = halves(rq_el[:, sl])
                qs = halves(q_row[:, sl])
                ss = halves(s_row[:, sl])
                for e in range(2):
                    dum = dum2[:, ll * e:ll * (e + 1)]
                    dgm = dgm + dum * decs[e]
                    tm_ = dum * wms[e]
                    oh = lane == (h0 + e)
                    rowterm = rowterm + jnp.where(oh, jnp.sum(tm_, axis=1, keepdims=True) + rq[e], 0.0)
                    ddt_u = ddt_u + jnp.where(oh, ddtu[e], 0.0)
                    dlast = dlast + jnp.where(oh, jnp.exp(cols[e][ll - 1:ll, :]) * ss[e] + qs[e], 0.0)
                    colt_ref[h0 + e:h0 + e + 1, :] = jnp.sum(tm_, axis=0, keepdims=True)
            dgb = dgm.astype(BF16)
            dx_ref[:, cc0:cc0 + SSM_STATE] = _dot(dgb, bg, 1, 0) + dc_st
            dx_ref[:, cb0:cb0 + SSM_STATE] = _dot(dgb, cg, 0, 0) + db_st
        row_io = lax.broadcasted_iota(jnp.int32, (ll, LANES), 0)
        dacs = rowterm - colt_ref[...].T + jnp.where(row_io == ll - 1, dlast, 0.0)
        da = _dot(jnp.logical_not(tril).astype(F32) + jnp.where(
            lax.broadcasted_iota(jnp.int32, (ll, ll), 0) == lax.broadcasted_iota(jnp.int32, (ll, ll), 1), 1.0, 0.0),
            dacs, 1, 0, HI)
        ddt_ref[...] = da * a_neg + ddt_u
        dal_ref[...] += jnp.sum(da * dtv, axis=0, keepdims=True) * a_neg

    rev = lambda c: nc - 1 - c
    row = pl.BlockSpec((ll, D_INNER), lambda c: (rev(c), 0))
    vec = pl.BlockSpec((1, D_INNER), lambda c: (0, 0))
    zblk = lambda j: pl.BlockSpec((ll, 1024), lambda c: (rev(c), j))
    return _call(
        body, name=name, grid=(nc,),
        in_specs=[pl.BlockSpec((ll, XBC), lambda c: (rev(c), 0)), pl.BlockSpec((ll, LANES), lambda c: (rev(c), 0)),
                  pl.BlockSpec((1, LANES), lambda c: (0, 0)),
                  pl.BlockSpec((1, SSM_STATE, D_INNER), lambda c: (rev(c) * every, 0, 0)), row, row, zblk(0), zblk(1), vec, vec],
        out_specs=[pl.BlockSpec((ll, XBC), lambda c: (rev(c), 0)), pl.BlockSpec((ll, LANES), lambda c: (rev(c), 0)),
                   pl.BlockSpec((1, LANES), lambda c: (0, 0)), row, vec, vec],
        out_shape=[jax.ShapeDtypeStruct((t, XBC), F32), jax.ShapeDtypeStruct((t, LANES), F32),
                   jax.ShapeDtypeStruct((1, LANES), F32), jax.ShapeDtypeStruct((t, D_INNER), BF16),
                   jax.ShapeDtypeStruct((1, D_INNER), F32), jax.ShapeDtypeStruct((1, D_INNER), F32)],
        scratch_shapes=[pltpu.VMEM((SSM_STATE, D_INNER), F32), pltpu.VMEM((LANES, ll), F32)],
        args=(xbc, dt, alog, sprev, dy4, y, zx, zx, dexp, nw), sem=("arbitrary",), comm=comm)


def _sum_parts(parts, name):
    nparts, r, c = parts.shape
    tc = _pick(c, (256, 128))

    def body(p_ref, o_ref):
        g = p_ref[0].astype(F32)
        for k in range(1, nparts):
            g = g + p_ref[k].astype(F32)
        o_ref[...] = g

    return pl.pallas_call(
        body, name=name, grid=(c // tc,), in_specs=[pl.BlockSpec((nparts, r, tc), lambda j: (0, 0, j))],
        out_specs=pl.BlockSpec((r, tc), lambda j: (0, j)), out_shape=jax.ShapeDtypeStruct((r, c), F32),
        compiler_params=_cp(("parallel",)),
    )(parts)


def _adamw(parts, w, m, v, name):
    nl, r, c = w.shape
    assert len(parts) == nl
    tr = _pick(r, (256, 128, 64))
    c1 = 1.0 - ADAM_B1 ** ADAM_STEP
    c2 = 1.0 - ADAM_B2 ** ADAM_STEP

    def body(*refs):
        p_refs = refs[:nl]
        w_ref, m_ref, v_ref, g_ref, d_ref, mo_ref, vo_ref = refs[nl:]
        g = None
        for l, p_ref in enumerate(p_refs):
            s = p_ref[0].astype(F32)
            for k in range(1, p_ref.shape[0]):
                s = s + p_ref[k].astype(F32)
            g = s if g is None else jnp.where(pl.program_id(0) == l, s, g)
        mn = ADAM_B1 * m_ref[0] + (1.0 - ADAM_B1) * g
        vn = ADAM_B2 * v_ref[0] + (1.0 - ADAM_B2) * (g * g)
        g_ref[0] = g
        mo_ref[0] = mn
        vo_ref[0] = vn
        d_ref[0] = -ADAM_LR * ((mn / c1) / (jnp.sqrt(vn / c2) + ADAM_EPS) + ADAM_WD * w_ref[0])

    row = pl.BlockSpec((1, tr, c), lambda l, i: (l, i, 0))
    sd = jax.ShapeDtypeStruct((nl, r, c), F32)
    return pl.pallas_call(
        body, name=name, grid=(nl, r // tr),
        in_specs=[pl.BlockSpec((p.shape[0], tr, c), lambda l, i: (0, i, 0)) for p in parts] + [row, row, row],
        out_specs=[row, row, row, row], out_shape=[sd, sd, sd, sd], compiler_params=_cp(("parallel", "parallel")),
    )(*parts, w, m, v)


def _peers():
    mx, my, mc = lax.axis_index("x"), lax.axis_index("y"), lax.axis_index("c")
    me = 4 * mx + 2 * my + mc
    out = []
    for k in range(1, N_DEV):
        px = 1 - mx if k & 4 else mx
        py = 1 - my if k & 2 else my
        pc = 1 - mc if k & 1 else mc
        out.append(((px, py, pc), 4 * px + 2 * py + pc))
    return me, out


class _Comm:
    def __init__(self, arrs, scatters):
        self.arrs, self.scatters, self.n = list(arrs), list(scatters), len(arrs)
        self.specs = [pl.BlockSpec(memory_space=pl.ANY)] * self.n
        self.out_shape = [jax.ShapeDtypeStruct(x.shape if sc else (N_DEV,) + x.shape, x.dtype)
                          for x, sc in zip(self.arrs, self.scatters)]
        np_ = N_DEV - 1
        self.scratch = [pltpu.SemaphoreType.DMA((np_ * self.n,)), pltpu.SemaphoreType.DMA((np_ * self.n,)),
                        pltpu.SemaphoreType.DMA((self.n,))]

    def _copies(self, x_refs, o_refs, sems):
        send_sems, recv_sems, local_sems = sems
        me, peers = _peers()
        np_ = N_DEV - 1
        local, sends, recvs = [], [], []
        for a in range(self.n):
            mine = x_refs[a].at[me] if self.scatters[a] else x_refs[a]
            local.append(pltpu.make_async_copy(mine, o_refs[a].at[me], local_sems.at[a]))
        for k, (dev, idx) in enumerate(peers):
            for a in range(self.n):
                mine = x_refs[a].at[me] if self.scatters[a] else x_refs[a]
                sends.append(pltpu.make_async_remote_copy(
                    src_ref=x_refs[a].at[idx] if self.scatters[a] else x_refs[a], dst_ref=o_refs[a].at[me],
                    send_sem=send_sems.at[a * np_ + k], recv_sem=recv_sems.at[a * np_ + k], device_id=dev, device_id_type=MESH))
                recvs.append(pltpu.make_async_remote_copy(
                    src_ref=mine, dst_ref=o_refs[a].at[idx], send_sem=send_sems.at[a * np_ + k],
                    recv_sem=recv_sems.at[a * np_ + k], device_id=dev, device_id_type=MESH))
        return local, sends, recvs

    def start(self, x_refs, o_refs, sems):
        local, sends, _ = self._copies(x_refs, o_refs, sems)
        for cp in local + sends:
            cp.start()

    def wait(self, x_refs, o_refs, sems):
        local, sends, recvs = self._copies(x_refs, o_refs, sems)
        for cp in recvs:
            cp.wait_recv()
        for cp in sends:
            cp.wait_send()
        for cp in local:
            cp.wait()


class _Gather2(_Comm):
    def __init__(self, arrs):
        super().__init__(arrs, [False] * len(arrs))

    def _plan(self, x_refs, o_refs, sems):
        send_sems, recv_sems, local_sems = sems
        mx, my, mc = lax.axis_index("x"), lax.axis_index("y"), lax.axis_index("c")
        slot = lambda px, py, pc: 4 * px + 2 * py + pc
        sib = (mx, my, 1 - mc)
        chips = [(1 - mx, my), (mx, 1 - my), (1 - mx, 1 - my)]
        np_ = N_DEV - 1
        local, first, passed, arrive_first, arrive_rest = [], [], [], [], []

        def copy(a, k, src, block, to):
            return pltpu.make_async_remote_copy(
                src_ref=src, dst_ref=o_refs[a].at[block], send_sem=send_sems.at[a * np_ + k], recv_sem=recv_sems.at[a * np_ + k],
                device_id=to, device_id_type=MESH)

        for a in range(self.n):
            me = slot(mx, my, mc)
            local.append(pltpu.make_async_copy(x_refs[a], o_refs[a].at[me], local_sems.at[a]))
            first.append(copy(a, 0, x_refs[a], me, sib))
            arrive_rest.append(copy(a, 0, x_refs[a], slot(*sib), sib))
            for j, (cx, cy) in enumerate(chips):
                first.append(copy(a, 1 + j, x_refs[a], me, (cx, cy, mc)))
                arrive_first.append(copy(a, 1 + j, x_refs[a], slot(cx, cy, mc), (cx, cy, mc)))
                passed.append(copy(a, 4 + j, o_refs[a].at[slot(cx, cy, mc)], slot(cx, cy, mc), sib))
                arrive_rest.append(copy(a, 4 + j, x_refs[a], slot(cx, cy, 1 - mc), sib))
        return local, first, passed, arrive_first, arrive_rest

    def start(self, x_refs, o_refs, sems):
        local, first, _, _, _ = self._plan(x_refs, o_refs, sems)
        for cp in local + first:
            cp.start()

    def wait(self, x_refs, o_refs, sems):
        local, first, passed, arrive_first, arrive_rest = self._plan(x_refs, o_refs, sems)
        for arrived, onward in zip(arrive_first, passed):
            arrived.wait_recv()
            onward.start()
        for cp in arrive_rest:
            cp.wait_recv()
        for cp in first + passed:
            cp.wait_send()
        for cp in local:
            cp.wait()


def _call(body, *, name, grid, in_specs, out_specs, out_shape, args, scratch_shapes=(), sem=None, comm=None):
    if comm is None:
        outs = pl.pallas_call(
            body, name=name, grid=grid, in_specs=list(in_specs), out_specs=list(out_specs), out_shape=list(out_shape),
            scratch_shapes=list(scratch_shapes), compiler_params=_cp(sem),
        )(*args)
        return list(outs), []
    n_in, n_out, nc = len(in_specs), len(out_specs), comm.n
    nsteps = 1
    for g in grid:
        nsteps *= g

    def carrier(*refs):
        ins, cin = refs[:n_in], refs[n_in:n_in + nc]
        outs, cout = refs[n_in + nc:n_in + nc + n_out], refs[n_in + nc + n_out:n_in + 2 * nc + n_out]
        rest = refs[n_in + 2 * nc + n_out:]
        scratch, sems = rest[:len(rest) - 3], rest[len(rest) - 3:]
        if nsteps == 1:
            comm.start(cin, cout, sems)
            body(*ins, *outs, *scratch)
            comm.wait(cin, cout, sems)
            return
        step = 0
        for d, g in enumerate(grid):
            step = step * g + pl.program_id(d)

        @pl.when(step == 0)
        def _():
            comm.start(cin, cout, sems)

        body(*ins, *outs, *scratch)

        @pl.when(step == nsteps - 1)
        def _():
            comm.wait(cin, cout, sems)

    outs = pl.pallas_call(
        carrier, name=name, grid=grid, in_specs=list(in_specs) + comm.specs, out_specs=list(out_specs) + comm.specs,
        out_shape=list(out_shape) + comm.out_shape, scratch_shapes=list(scratch_shapes) + comm.scratch,
        compiler_params=_cp(("arbitrary",) * len(grid) if grid else None),
    )(*args, *comm.arrs)
    return list(outs[:n_out]), list(outs[n_out:])


def _exchange(comm, name):
    return _call(lambda *refs: None, name=name, grid=(), in_specs=[], out_specs=[], out_shape=[], args=[], comm=comm)[1]


def _pack(arrs, dtype, lead=()):
    nl = len(lead)
    flat = jnp.concatenate([a.astype(dtype).reshape(lead + (-1,)) for a in arrs], axis=nl)
    n = flat.shape[-1]
    rows = -(-n // (LANES * 8)) * 8
    flat = jnp.pad(flat, [(0, 0)] * nl + [(0, rows * LANES - n)])
    return flat.reshape(lead + (rows, LANES))


def _unpack(flat, shapes, lead=()):
    nl = len(lead)
    flat = flat.reshape(lead + (-1,))
    out, o = [], 0
    for s in shapes:
        n = 1
        for d in s:
            n *= d
        out.append(lax.slice_in_dim(flat, o, o + n, axis=nl).reshape(lead + tuple(s)))
        o += n
    return out


def _join(g, ax):
    return jnp.concatenate([g[d] for d in range(N_DEV)], axis=ax)


def _split(full, ax):
    n = full.shape[ax] // N_DEV
    return jnp.stack([lax.slice_in_dim(full, d * n, (d + 1) * n, axis=ax) for d in range(N_DEV)])


_WEIGHTS = ['norm_mix', 'norm_ffn', 'attn_w_in', 'attn_w_out', 'relpos_table', 'q_norm_a', 'k_norm_a', 'q_norm_b',
            'k_norm_b', 'sinks', 'ssm_w_in', 'ssm_conv_w', 'ssm_conv_b', 'ssm_dt_bias', 'ssm_a_log', 'ssm_d', 'ssm_norm',
            'ssm_w_out', 'ffn_w_in', 'ffn_conv_w', 'ffn_conv_b', 'ffn_w_out']
_SHARD_AX = {'attn_w_in': 2, 'attn_w_out': 1, 'ssm_w_in': 2, 'ssm_conv_w': 2, 'ssm_conv_b': 1, 'ssm_norm': 1,
             'ssm_w_out': 1, 'ffn_w_in': 2, 'ffn_conv_w': 2, 'ffn_w_out': 1}
_BIG = ['attn_w_in', 'attn_w_out', 'ssm_w_in', 'ssm_w_out', 'ffn_w_in', 'ffn_w_out']
_SMALL = ['ssm_conv_w', 'ssm_conv_b', 'ssm_norm', 'ffn_conv_w']
_AX2 = {n: _SHARD_AX[n] - 1 for n in _BIG}
_REPL = [n for n in _WEIGHTS if n not in _SHARD_AX]


def _rows8(w):
    return jnp.pad(w, ((0, 8 - w.shape[0]), (0, 0)))


def _lanes128(v):
    return jnp.pad(v, (0, LANES - v.shape[0])).reshape(1, LANES)


def _band_mask(n_prev, pad):
    cq = jnp.arange(TQ)[:, None] // CHUNK
    ck = jnp.arange(pad + TQ)[None, :] // CHUNK
    return (ck >= cq) & (ck <= cq + n_prev)


def _ffn_fwd(xin, g, w_in_t, w8, cb, tag):
    gu, h, a, gc = _ffn_in_mid(xin, g, w_in_t, w8, cb, f"mm_ffn_in{tag}")
    return a, (h, gu, a, gc)


def _ffn_bwd(dx, dxb, xin, g, w_in_t, w8, w_out, saved, tag):
    h, gu, a, gc = saved
    dw_out = _mm_tn(a, dxb, f"mm_ffn_dwout{tag}")
    dgu, dw8, dcb = _ffn_mid_bwd(gu, gc, dxb, w_out, w8, f"ffn_mid_bwd{tag}")
    dw_in_t = _mm_tn(dgu, h, f"mm_ffn_dwin{tag}")
    dxp, dxpb, dg = _mm_rms_bwd(dgu, w_in_t, 0, None, xin, g, dx, f"mm_ffn_dh{tag}")
    return dxp, dxpb, dg, dw_in_t, dw8[:3], dcb, dw_out


def kernel(x, norm_mix, norm_ffn, attn_w_in, attn_w_out, relpos_table, q_norm_a, k_norm_a, q_norm_b, k_norm_b, sinks, ssm_w_in, ssm_conv_w, ssm_conv_b, ssm_dt_bias, ssm_a_log, ssm_d, ssm_norm, ssm_w_out, ffn_w_in, ffn_conv_w, ffn_conv_b, ffn_w_out, loss_target, m_norm_mix, m_norm_ffn, m_attn_w_in, m_attn_w_out, m_relpos_table, m_q_norm_a, m_k_norm_a, m_q_norm_b, m_k_norm_b, m_sinks, m_ssm_w_in, m_ssm_conv_w, m_ssm_conv_b, m_ssm_dt_bias, m_ssm_a_log, m_ssm_d, m_ssm_norm, m_ssm_w_out, m_ffn_w_in, m_ffn_conv_w, m_ffn_conv_b, m_ffn_w_out, v_norm_mix, v_norm_ffn, v_attn_w_in, v_attn_w_out, v_relpos_table, v_q_norm_a, v_k_norm_a, v_q_norm_b, v_k_norm_b, v_sinks, v_ssm_w_in, v_ssm_conv_w, v_ssm_conv_b, v_ssm_dt_bias, v_ssm_a_log, v_ssm_d, v_ssm_norm, v_ssm_w_out, v_ffn_w_in, v_ffn_conv_w, v_ffn_conv_b, v_ffn_w_out):
    w = dict(norm_mix=norm_mix, norm_ffn=norm_ffn, attn_w_in=attn_w_in, attn_w_out=attn_w_out, relpos_table=relpos_table,
             q_norm_a=q_norm_a, k_norm_a=k_norm_a, q_norm_b=q_norm_b, k_norm_b=k_norm_b, sinks=sinks, ssm_w_in=ssm_w_in,
             ssm_conv_w=ssm_conv_w, ssm_conv_b=ssm_conv_b, ssm_dt_bias=ssm_dt_bias, ssm_a_log=ssm_a_log, ssm_d=ssm_d,
             ssm_norm=ssm_norm, ssm_w_out=ssm_w_out, ffn_w_in=ffn_w_in, ffn_conv_w=ffn_conv_w, ffn_conv_b=ffn_conv_b,
             ffn_w_out=ffn_w_out)
    mom = dict(norm_mix=m_norm_mix, norm_ffn=m_norm_ffn, attn_w_in=m_attn_w_in, attn_w_out=m_attn_w_out,
               relpos_table=m_relpos_table, q_norm_a=m_q_norm_a, k_norm_a=m_k_norm_a, q_norm_b=m_q_norm_b,
               k_norm_b=m_k_norm_b, sinks=m_sinks, ssm_w_in=m_ssm_w_in, ssm_conv_w=m_ssm_conv_w, ssm_conv_b=m_ssm_conv_b,
               ssm_dt_bias=m_ssm_dt_bias, ssm_a_log=m_ssm_a_log, ssm_d=m_ssm_d, ssm_norm=m_ssm_norm, ssm_w_out=m_ssm_w_out,
               ffn_w_in=m_ffn_w_in, ffn_conv_w=m_ffn_conv_w, ffn_conv_b=m_ffn_conv_b, ffn_w_out=m_ffn_w_out)
    var = dict(norm_mix=v_norm_mix, norm_ffn=v_norm_ffn, attn_w_in=v_attn_w_in, attn_w_out=v_attn_w_out,
               relpos_table=v_relpos_table, q_norm_a=v_q_norm_a, k_norm_a=v_k_norm_a, q_norm_b=v_q_norm_b,
               k_norm_b=v_k_norm_b, sinks=v_sinks, ssm_w_in=v_ssm_w_in, ssm_conv_w=v_ssm_conv_w, ssm_conv_b=v_ssm_conv_b,
               ssm_dt_bias=v_ssm_dt_bias, ssm_a_log=v_ssm_a_log, ssm_d=v_ssm_d, ssm_norm=v_ssm_norm, ssm_w_out=v_ssm_w_out,
               ffn_w_in=v_ffn_w_in, ffn_conv_w=v_ffn_conv_w, ffn_conv_b=v_ffn_conv_b, ffn_w_out=v_ffn_w_out)

    def piece(n, l):
        return (w[n][l].T if _AX2[n] == 1 else w[n][l]).astype(BF16)

    def gather_of(names_layers):
        return _Gather2([piece(n, l) for n, l in names_layers])

    def joined(got):
        return [g.reshape(-1, D_MODEL) for g in got]

    first = [('attn_w_in', 0), ('attn_w_out', 0)]
    got = _exchange(_Gather2([piece(n, l) for n, l in first] + [_pack([w[n] for n in _SMALL], F32)]), "gather_attn")
    wt_attn_in, w_attn_out = joined(got[:2])
    full = {}
    for n, g in zip(_SMALL, _unpack(got[2], [w[n].shape for n in _SMALL], lead=(N_DEV,))):
        full[n] = _join(g, _SHARD_AX[n])
    ssm_cw8 = _rows8(full['ssm_conv_w'][0])
    ssm_cb = full['ssm_conv_b']
    ssm_nw = full['ssm_norm']
    ffn_cw8 = [_rows8(full['ffn_conv_w'][l]) for l in range(2)]
    ffn_cb = [ffn_conv_b[l:l + 1] for l in range(2)]

    x0 = x[0]
    target = loss_target[0]
    t = x0.shape[0]

    g_mix0, g_mix1 = norm_mix[0:1], norm_mix[1:2]
    g_ffn0, g_ffn1 = norm_ffn[0:1], norm_ffn[1:2]
    proj, h0 = _rms_mm(x0, g_mix0, wt_attn_in, ATTN_PROJ, "mm_attn_in", F32)
    hn_w = jnp.concatenate([jnp.tile(v, (1, 2)) for v in (q_norm_a, k_norm_a, q_norm_b, k_norm_b)], axis=0)
    qa, kpa, vpa, qb, kpb, vpb = _headnorm_fwd(proj, hn_w, "headnorm")
    table = jnp.pad(relpos_table[0], ((0, 0), (0, REL_W - (2 * MAX_REL + 1))))
    bias_a = jnp.where(_band_mask(A_PREV, PAD_A)[None], jnp.transpose(_relpos_fwd(table, "relpos_bias"), (1, 0, 2)), NEG)
    rel_b = jnp.arange(TQ)[:, None] - (jnp.arange(PAD_B + TQ)[None, :] - PAD_B)
    slopes = 2.0 ** (-8.0 * jnp.arange(1, N_HEADS + 1, dtype=F32) / N_HEADS)
    bias_b = jnp.where(_band_mask(B_PREV, PAD_B)[None], -slopes[:, None, None] * jnp.abs(rel_b).astype(F32)[None], NEG)
    no_sinks = jnp.full((N_HEADS,), NEG, F32)
    ffn0_w, ssm_w, ffn1_w = [('ffn_w_in', 0), ('ffn_w_out', 0)], [('ssm_w_in', 0), ('ssm_w_out', 0)], [('ffn_w_in', 1), ('ffn_w_out', 1)]
    oa, stats_a, got = _attn_fwd(qa, kpa, vpa, bias_a, no_sinks, PAD_A, "attn_a", comm=gather_of(ffn0_w + ssm_w))
    wt_ffn_in0, w_ffn_out0, wt_ssm_in, w_ssm_out = joined(got)
    ob, stats_b, _ = _attn_fwd(qb, kpb, vpb, bias_b, sinks[0], PAD_B, "attn_b")
    wt_ssm_dt = jnp.pad(wt_ssm_in[ZX:], ((0, LANES - SSM_HEADS), (0, 0)))
    x1 = _mm(oa, w_attn_out, "mm_attn_out_a", res=x0, b_rows=(0, D_ATT))
    x1 = _mm(ob, w_attn_out, "mm_attn_out_b", res=x1, b_rows=(D_ATT, D_ATT))
    a0, ffn0_saved = _ffn_fwd(x1, g_ffn0, wt_ffn_in0, ffn_cw8[0], ffn_cb[0], "0")
    x2 = _mm(a0, w_ffn_out0, "mm_ffn_out0", res=x1)

    zx, h2, xbc, conv_pre = _ssm_in_pre(x2, g_mix1, wt_ssm_in, ssm_cw8, ssm_cb, "mm_ssm_in")
    dtraw = _mm(h2, wt_ssm_dt, "mm_ssm_dt", trans_b=True)
    dt_bias = _lanes128(ssm_dt_bias[0])
    alog = _lanes128(ssm_a_log[0])
    dexp = jnp.repeat(ssm_d[0], HEAD_DIM).reshape(1, D_INNER)
    dt = _dt_fwd(dtraw, dt_bias, "ssm_dt")
    (y, sprev, y4), got = _ssd_fwd(xbc, dt, alog, zx, dexp, ssm_nw, "ssd_fwd", comm=gather_of(ffn1_w))
    wt_ffn_in1, w_ffn_out1 = joined(got)
    x3 = _mm(y4, w_ssm_out, "mm_ssm_out", res=x2)
    a1, ffn1_saved = _ffn_fwd(x3, g_ffn1, wt_ffn_in1, ffn_cw8[1], ffn_cb[1], "1")

    dx4, dx4b, sq = _mm_loss(a1, w_ffn_out1, x3, target, "mm_ffn_out1_loss")
    loss = lax.psum(0.5 * jnp.sum(sq) / D_MODEL, ("x", "y", "c"))

    grads = {}

    def scatter_of(grads_2d):
        return _Comm([g.reshape(N_DEV, -1, D_MODEL) for g in grads_2d], [True] * len(grads_2d))

    dx3, dx3b, dg_ffn1, dwtin1, dcw1, dcb1, dwout1 = _ffn_bwd(
        dx4, dx4b, x3, g_ffn1, wt_ffn_in1, ffn_cw8[1], w_ffn_out1, ffn1_saved, "1")

    dy4 = _mm(dx3b, w_ssm_out, "mm_ssm_dy", trans_b=True)
    dw_ssm_out = _mm_tn(y4, dx3b, "mm_ssm_dwout")
    (dxbc, ddt, dalog, dz, dd_lane, dnw), parts_ffn1 = _ssd_bwd(
        xbc, dt, alog, sprev, dy4, y, zx, dexp, ssm_nw, "ssd_bwd", comm=scatter_of([dwtin1, dwout1]))
    dxr, dcw_s, dcb_s = _ssm_pre_bwd(zx, conv_pre, dxbc, ssm_cw8, "ssm_pre_bwd")
    ddtraw, ddtb = _dt_bwd(dtraw, dt_bias, ddt, "ssm_dt_bwd")
    dh2 = _mm(dz, wt_ssm_in, "mm_ssm_dh_z", b_rows=(0, D_INNER))
    dh2 = _mm(dxr, wt_ssm_in[D_INNER:ZX], "mm_ssm_dh_x", res=dh2)
    dwt_ssm_in = jnp.concatenate([
        _mm_tn(dz, h2, "mm_ssm_dwin_z"), _mm_tn(dxr, h2, "mm_ssm_dwin_x"),
        _mm_tn(ddtraw, h2, "mm_ssm_dwin_dt")[:SSM_HEADS]], axis=0)
    dx2, dx2b, dg_mix1 = _mm_rms_bwd(ddtraw, wt_ssm_dt, 0, dh2, x2, g_mix1, dx3, "mm_ssm_dh_dt")
    grads['ssm_conv_w'] = dcw_s[:4][None]
    grads['ssm_conv_b'] = dcb_s
    grads['ssm_norm'] = dnw
    grads['ssm_dt_bias'] = ddtb[:, :SSM_HEADS]
    grads['ssm_a_log'] = dalog[:, :SSM_HEADS]
    grads['ssm_d'] = jnp.sum(dd_lane.reshape(SSM_HEADS, HEAD_DIM), axis=1)[None]

    dx1, dx1b, dg_ffn0, dwtin0, dcw0, dcb0, dwout0 = _ffn_bwd(
        dx2, dx2b, x1, g_ffn0, wt_ffn_in0, ffn_cw8[0], w_ffn_out0, ffn0_saved, "0")
    grads['ffn_conv_w'] = jnp.stack([dcw0, dcw1])
    grads['ffn_conv_b'] = jnp.concatenate([dcb0, dcb1], axis=0)
    grads['norm_ffn'] = jnp.concatenate([dg_ffn0, dg_ffn1], axis=0)

    do = _mm(dx1b, w_attn_out, "mm_attn_do", out_dtype=BF16, trans_b=True)
    dw_attn_out = jnp.concatenate([_mm_tn(oa, dx1b, "mm_attn_dwout_a"), _mm_tn(ob, dx1b, "mm_attn_dwout_b")], axis=0)
    (dqa, dkpa, dvpa, dbias_a, _), parts_ssm = _attn_bwd(
        qa, kpa, vpa, bias_a, no_sinks, do, stats_a, oa, 0, PAD_A, "attn_a_bwd",
        comm=scatter_of([dwt_ssm_in, dw_ssm_out, dw_attn_out]))
    (dqb, dkpb, dvpb, _, dsink), parts_ffn0 = _attn_bwd(
        qb, kpb, vpb, bias_b, sinks[0], do, stats_b, ob, 4, PAD_B, "attn_b_bwd", comm=scatter_of([dwtin0, dwout0]))
    grads['relpos_table'] = _relpos_bwd(jnp.transpose(dbias_a, (1, 0, 2)), "relpos_bwd")[None, :, :2 * MAX_REL + 1]
    grads['sinks'] = dsink[:, :2, 0].reshape(1, N_HEADS)
    dproj, dhn = _headnorm_bwd(proj, hn_w, dqa, dkpa, dvpa, dqb, dkpb, dvpb, "headnorm_bwd")
    dhn = dhn[:, :HEAD_DIM] + dhn[:, HEAD_DIM:]
    for k, n in enumerate(('q_norm_a', 'k_norm_a', 'q_norm_b', 'k_norm_b')):
        grads[n] = dhn[k:k + 1]
    dwt_attn_in = _mm_tn(dproj, h0, "mm_attn_dwin")
    dx0, _, dg_mix0, parts_attn_in = _mm_rms_bwd(dproj, wt_attn_in, 0, None, x0, g_mix0, dx1, "mm_attn_dh",
                                                 comm=scatter_of([dwt_attn_in]))
    grads['norm_mix'] = jnp.concatenate([dg_mix0, dg_mix1], axis=0)

    def summed_t(parts, name):
        return _sum_parts(parts, name).T[None]

    sm_shapes = [w[n].shape for n in _SMALL]
    rp_shapes = [w[n].shape for n in _REPL]
    recv = _exchange(_Comm(
        [_pack([_split(grads[n], _SHARD_AX[n]) for n in _SMALL], F32, lead=(N_DEV,)), _pack([grads[n] for n in _REPL], F32)],
        [True, False]), "exchange_small")
    big_parts = {
        'attn_w_in': [summed_t(parts_attn_in[0], "sum_attn_w_in")], 'attn_w_out': [parts_ssm[2]],
        'ssm_w_in': [summed_t(parts_ssm[0], "sum_ssm_w_in")], 'ssm_w_out': [parts_ssm[1]],
        'ffn_w_in': [summed_t(parts_ffn0[0], "sum_ffn_w_in0"), summed_t(parts_ffn1[0], "sum_ffn_w_in1")],
        'ffn_w_out': [parts_ffn0[1], parts_ffn1[1]],
    }
    res = [{}, {}, {}, {}]
    for n in _BIG:
        for kind, a in enumerate(_adamw(big_parts[n], w[n], mom[n], var[n], f"adamw_{n}")):
            res[kind][n] = a
    for names, shapes, parts in ((_SMALL, sm_shapes, recv[0]), (_REPL, rp_shapes, recv[1])):
        outs = _adamw([parts], _pack([w[n] for n in names], F32)[None], _pack([mom[n] for n in names], F32)[None],
                      _pack([var[n] for n in names], F32)[None], "adamw_" + ("small" if names is _SMALL else "replicated"))
        for kind, flat in enumerate(outs):
            for n, a in zip(names, _unpack(flat[0], shapes)):
                res[kind][n] = a
    return (loss, dx0[None], *[res[0][n] for n in _WEIGHTS], *[res[1][n] for n in _WEIGHTS],
            *[res[2][n] for n in _WEIGHTS], *[res[3][n] for n in _WEIGHTS])
```

```python
import jax
import jax.numpy as jnp
from jax import lax
from jax.experimental import pallas as pl
from jax.experimental.pallas import tpu as pltpu

F32 = jnp.float32
BF16 = jnp.bfloat16
HI = lax.Precision.HIGHEST
MESH = pl.DeviceIdType.MESH
NEG = -1e30

N_DEV = 8
D_MODEL = 1024
EPS = 1e-6
CHUNK = 64
HEAD_DIM = 64
N_HEADS = 8
A_PREV = 8
B_PREV = 2
MAX_REL = 256
TQ = 2 * CHUNK
ATT_SUB = 16
PAD_A = A_PREV * CHUNK
PAD_B = B_PREV * CHUNK
REL_W = PAD_A + TQ
D_ATT = N_HEADS * HEAD_DIM
COL_QA, COL_KA, COL_VA, COL_QB = 0, D_ATT, 2 * D_ATT, 3 * D_ATT
COL_KB, COL_VB = 4 * D_ATT, 4 * D_ATT + 2 * HEAD_DIM
ATTN_PROJ = COL_VB + 2 * HEAD_DIM
D_INNER = 2048
SSM_HEADS = 32
SSM_GROUPS = 4
SSM_STATE = 128
XBC = D_INNER + 2 * SSM_GROUPS * SSM_STATE
ZX = D_INNER + XBC
D_FF = 2816
SSD_L = 128
SSD_L_BWD = 2 * SSD_L
LANES = 128
VMEM_LIMIT = 56 << 20

ADAM_LR, ADAM_B1, ADAM_B2, ADAM_EPS, ADAM_WD, ADAM_STEP = 0.001, 0.9, 0.999, 1e-08, 0.01, 10


def _cp(sem=None):
    return pltpu.CompilerParams(dimension_semantics=sem, vmem_limit_bytes=VMEM_LIMIT)


def _dot(a, b, ca=1, cb=0, prec=None):
    return lax.dot_general(a, b, (((ca,), (cb,)), ((), ())), preferred_element_type=F32, precision=prec)


def _pick(n, cands):
    for c in cands:
        if n % c == 0:
            return c
    return n


def _lo_mask():
    return lax.broadcasted_iota(jnp.int32, (1, LANES), 1) < HEAD_DIM


_TN_CHUNKS = (1408, 1536, 1152, 1024, 512, 256, 128)


TN_MAX_ROWS = 3072
MM_WIDE = 2304


def _mm_tn(a, b, name):
    kdim, m = a.shape
    n = b.shape[1]
    assert b.shape[0] == kdim, (a.shape, b.shape)
    mb = m if m <= TN_MAX_ROWS else m // 2
    tn = _pick(n, _TN_CHUNKS)
    tk = _pick(kdim, (512, 256, 128))
    nk = kdim // tk

    def body(a_ref, b_ref, o_ref, acc):
        k = pl.program_id(1)

        @pl.when(k == 0)
        def _():
            acc[...] = jnp.zeros_like(acc)

        av = a_ref[...]
        for c in range(0, n, tn):
            acc[:, c:c + tn] += _dot(av, b_ref[:, c:c + tn], 0, 0)

        @pl.when(k == nk - 1)
        def _():
            o_ref[...] = acc[...].astype(BF16)

    return pl.pallas_call(
        body, name=name, grid=(m // mb, nk),
        in_specs=[pl.BlockSpec((tk, mb), lambda j, k: (k, j)), pl.BlockSpec((tk, n), lambda j, k: (k, 0))],
        out_specs=pl.BlockSpec((mb, n), lambda j, k: (j, 0)), out_shape=jax.ShapeDtypeStruct((m, n), BF16),
        scratch_shapes=[pltpu.VMEM((mb, n), F32)], compiler_params=_cp(("parallel", "arbitrary")),
    )(a, b)


def _mm(a, b, name, out_dtype=F32, res=None, trans_b=False, b_rows=None):
    m, kdim = a.shape
    if b_rows is None:
        b_rows = (0, b.shape[0])
    off, rows = b_rows
    n = rows if trans_b else b.shape[1]
    assert (b.shape[1] if trans_b else rows) == kdim and off % rows == 0, (a.shape, b.shape, b_rows)
    tn = _pick(n, _TN_CHUNKS)
    tm = _pick(m, (256, 128) if n > MM_WIDE else (512, 256, 128))

    def body(*refs):
        if res is None:
            a_ref, b_ref, o_ref = refs
        else:
            a_ref, b_ref, r_ref, o_ref = refs
        av = a_ref[...]
        for c in range(0, n, tn):
            r = _dot(av, b_ref[c:c + tn, :], 1, 1) if trans_b else _dot(av, b_ref[:, c:c + tn], 1, 0)
            if res is not None:
                r = r + r_ref[:, c:c + tn]
            o_ref[:, c:c + tn] = r.astype(out_dtype)

    in_specs = [pl.BlockSpec((tm, kdim), lambda i: (i, 0)), pl.BlockSpec((rows, b.shape[1]), lambda i: (off // rows, 0))]
    args = [a, b]
    if res is not None:
        in_specs.append(pl.BlockSpec((tm, n), lambda i: (i, 0)))
        args.append(res)
    return pl.pallas_call(
        body, name=name, grid=(m // tm,), in_specs=in_specs, out_specs=pl.BlockSpec((tm, n), lambda i: (i, 0)),
        out_shape=jax.ShapeDtypeStruct((m, n), out_dtype), compiler_params=_cp(("parallel",)),
    )(*args)


def _rms_mm(x, g, bt, n, name, out_dtype):
    t, d = x.shape
    tn = _pick(n, _TN_CHUNKS)
    tm = _pick(t, (256, 128))

    def body(x_ref, g_ref, b_ref, o_ref, h_ref):
        xv = x_ref[...]
        r = lax.rsqrt(jnp.mean(xv * xv, axis=-1, keepdims=True) + EPS)
        h = (xv * r * g_ref[...]).astype(BF16)
        h_ref[...] = h
        for c in range(0, n, tn):
            o_ref[:, c:c + tn] = _dot(h, b_ref[c:c + tn, :], 1, 1).astype(out_dtype)

    row = pl.BlockSpec((tm, d), lambda i: (i, 0))
    return pl.pallas_call(
        body, name=name, grid=(t // tm,),
        in_specs=[row, pl.BlockSpec((1, d), lambda i: (0, 0)), pl.BlockSpec(bt.shape, lambda i: (0, 0))],
        out_specs=[pl.BlockSpec((tm, n), lambda i: (i, 0)), row],
        out_shape=[jax.ShapeDtypeStruct((t, n), out_dtype), jax.ShapeDtypeStruct((t, d), BF16)],
        compiler_params=_cp(("parallel",)),
    )(x, g, bt)


def _mm_rms_bwd(a, b, b_off, dh_prev, x, g, dres, name, comm=None):
    t, d = x.shape
    kdim = a.shape[1]
    assert b_off % kdim == 0 and b.shape[1] == d, (a.shape, b.shape, b_off)
    tm = _pick(t, (256, 128))

    def body(*refs):
        if dh_prev is None:
            a_ref, b_ref, x_ref, g_ref, dr_ref, dx_ref, dxb_ref, dg_ref = refs
            dhv = _dot(a_ref[...], b_ref[...], 1, 0)
        else:
            a_ref, b_ref, p_ref, x_ref, g_ref, dr_ref, dx_ref, dxb_ref, dg_ref = refs
            dhv = _dot(a_ref[...], b_ref[...], 1, 0) + p_ref[...]
        xv = x_ref[...]
        r = lax.rsqrt(jnp.mean(xv * xv, axis=-1, keepdims=True) + EPS)
        xh = xv * r
        dxh = dhv * g_ref[...]
        dx = dr_ref[...] + r * (dxh - xh * jnp.mean(dxh * xh, axis=-1, keepdims=True))
        dx_ref[...] = dx
        dxb_ref[...] = dx.astype(BF16)

        @pl.when(pl.program_id(0) == 0)
        def _():
            dg_ref[...] = jnp.zeros_like(dg_ref)

        dg_ref[...] += jnp.sum(dhv * xh, axis=0, keepdims=True)

    row = pl.BlockSpec((tm, d), lambda i: (i, 0))
    vec = pl.BlockSpec((1, d), lambda i: (0, 0))
    in_specs = [pl.BlockSpec((tm, kdim), lambda i: (i, 0)), pl.BlockSpec((kdim, d), lambda i: (b_off // kdim, 0))]
    args = [a, b]
    if dh_prev is not None:
        in_specs.append(row)
        args.append(dh_prev)
    outs, got = _call(
        body, name=name, grid=(t // tm,), in_specs=in_specs + [row, vec, row], out_specs=[row, row, vec],
        out_shape=[jax.ShapeDtypeStruct((t, d), F32), jax.ShapeDtypeStruct((t, d), BF16), jax.ShapeDtypeStruct((1, d), F32)],
        args=(*args, x, g, dres), sem=("arbitrary",), comm=comm)
    return (*outs, got) if comm is not None else tuple(outs)


def _mm_loss(a, b, res, target, name):
    t, kdim = a.shape
    d = b.shape[1]
    tm = _pick(t, (512, 256, 128))

    def body(a_ref, b_ref, r_ref, t_ref, dy_ref, dyb_ref, acc_ref):
        @pl.when(pl.program_id(0) == 0)
        def _():
            acc_ref[...] = jnp.zeros_like(acc_ref)

        err = _dot(a_ref[...], b_ref[...], 1, 0) + r_ref[...] - t_ref[...]
        dy = err * (1.0 / d)
        dy_ref[...] = dy
        dyb_ref[...] = dy.astype(BF16)
        acc_ref[...] += jnp.sum(err * err, axis=0, keepdims=True)

    row = pl.BlockSpec((tm, d), lambda i: (i, 0))
    vec = pl.BlockSpec((1, d), lambda i: (0, 0))
    return pl.pallas_call(
        body, name=name, grid=(t // tm,),
        in_specs=[pl.BlockSpec((tm, kdim), lambda i: (i, 0)), pl.BlockSpec((kdim, d), lambda i: (0, 0)), row, row],
        out_specs=[row, row, vec],
        out_shape=[jax.ShapeDtypeStruct((t, d), F32), jax.ShapeDtypeStruct((t, d), BF16), jax.ShapeDtypeStruct((1, d), F32)],
        compiler_params=_cp(("arbitrary",)),
    )(a, b, res, target)


def _head_sums(v):
    ri = lax.broadcasted_iota(jnp.int32, (LANES, LANES), 0) // HEAD_DIM
    ci = lax.broadcasted_iota(jnp.int32, (LANES, LANES), 1) // HEAD_DIM
    ones = (ri == ci).astype(BF16)
    hi = v.astype(BF16)
    lo_part = (v - hi.astype(F32)).astype(BF16)
    return _dot(hi, ones, 1, 0) + _dot(lo_part, ones, 1, 0)


def _head_rms(xs):
    r = lax.rsqrt(_head_sums(xs * xs) * (1.0 / HEAD_DIM) + EPS)
    return xs * r, r


def _head_rms_bwd(xs, w, dy):
    xh, r = _head_rms(xs)
    dxh = dy * w
    mm = _head_sums(dxh * xh) * (1.0 / HEAD_DIM)
    return r * (dxh - xh * mm), dy * xh


_QSCALE = HEAD_DIM ** -0.5


def _headnorm_fwd(proj, ws, name):
    t = proj.shape[0]
    tm = TQ
    lead = PAD_A // tm
    leadb = PAD_B // tm

    def body(p_ref, w_ref, qa_ref, ka_ref, va_ref, qb_ref, kb_ref, vb_ref):
        data = pl.program_id(0) >= lead
        lo = _lo_mask()

        def put(ref, c, val):
            ref[:, c:c + val.shape[1]] = jnp.where(data, val, 0.0).astype(BF16)

        def per_query_head(slab):
            other = pltpu.roll(slab, HEAD_DIM, 1)
            e0, e1 = jnp.where(lo, slab, other), jnp.where(lo, other, slab)
            return jnp.concatenate([e0, e0, e1, e1], axis=1)

        for s in range(D_ATT // LANES):
            c = LANES * s
            xh, _ = _head_rms(p_ref[:, COL_QA + c:COL_QA + c + LANES])
            qa_ref[:, c:c + LANES] = (xh * w_ref[0:1, :] * _QSCALE).astype(BF16)
            xh, _ = _head_rms(p_ref[:, COL_KA + c:COL_KA + c + LANES])
            put(ka_ref, c, xh * w_ref[1:2, :])
            xh, _ = _head_rms(p_ref[:, COL_QB + c:COL_QB + c + LANES])
            qb_ref[:, c:c + LANES] = (xh * w_ref[2:3, :] * _QSCALE).astype(BF16)
        put(va_ref, 0, p_ref[:, COL_VA:COL_VA + D_ATT])
        xh, _ = _head_rms(p_ref[:, COL_KB:COL_KB + LANES])
        put(kb_ref, 0, per_query_head(xh * w_ref[3:4, :]))
        put(vb_ref, 0, per_query_head(p_ref[:, COL_VB:COL_VB + LANES]))

    src = lambda i: jnp.maximum(i - lead, 0)
    wide = pl.BlockSpec((tm, D_ATT), lambda i: (src(i), 0))
    pad_a = pl.BlockSpec((tm, D_ATT), lambda i: (i, 0))
    pad_b = pl.BlockSpec((tm, D_ATT), lambda i: (jnp.maximum(i - lead + leadb, 0), 0))
    sd = lambda rows: jax.ShapeDtypeStruct((rows, D_ATT), BF16)
    return pl.pallas_call(
        body, name=name, grid=(t // tm + lead,),
        in_specs=[pl.BlockSpec((tm, ATTN_PROJ), lambda i: (src(i), 0)), pl.BlockSpec((4, LANES), lambda i: (0, 0))],
        out_specs=[wide, pad_a, pad_a, wide, pad_b, pad_b],
        out_shape=[sd(t), sd(t + PAD_A), sd(t + PAD_A), sd(t), sd(t + PAD_B), sd(t + PAD_B)],
        compiler_params=_cp(("arbitrary",)),
    )(proj, ws)


def _headnorm_bwd(proj, ws, dqa, dkpa, dvpa, dqb, dkpb, dvpb, name):
    t = proj.shape[0]
    tm = TQ
    offa, offb = PAD_A // tm, PAD_B // tm

    def body(p_ref, w_ref, dqa_ref, dka_ref, dva_ref, dqb_ref, dkb_ref, dvb_ref, dp_ref, dw_ref):
        i = pl.program_id(0)
        lo = _lo_mask()

        @pl.when(i == 0)
        def _():
            dw_ref[...] = jnp.zeros_like(dw_ref)

        acc = [jnp.zeros((1, LANES), F32) for _ in range(4)]
        for s in range(D_ATT // LANES):
            c = LANES * s
            dx, dwl = _head_rms_bwd(p_ref[:, COL_QA + c:COL_QA + c + LANES], w_ref[0:1, :], dqa_ref[:, c:c + LANES] * _QSCALE)
            dp_ref[:, COL_QA + c:COL_QA + c + LANES] = dx.astype(BF16)
            acc[0] += jnp.sum(dwl, axis=0, keepdims=True)
            dx, dwl = _head_rms_bwd(p_ref[:, COL_KA + c:COL_KA + c + LANES], w_ref[1:2, :], dka_ref[:, c:c + LANES])
            dp_ref[:, COL_KA + c:COL_KA + c + LANES] = dx.astype(BF16)
            acc[1] += jnp.sum(dwl, axis=0, keepdims=True)
            dx, dwl = _head_rms_bwd(p_ref[:, COL_QB + c:COL_QB + c + LANES], w_ref[2:3, :], dqb_ref[:, c:c + LANES] * _QSCALE)
            dp_ref[:, COL_QB + c:COL_QB + c + LANES] = dx.astype(BF16)
            acc[2] += jnp.sum(dwl, axis=0, keepdims=True)
        dp_ref[:, COL_VA:COL_VA + D_ATT] = dva_ref[...].astype(BF16)

        def group_sum(ref):
            s0 = ref[:, 0:LANES] + ref[:, LANES:2 * LANES]
            s1 = ref[:, 2 * LANES:3 * LANES] + ref[:, 3 * LANES:4 * LANES]
            s0 = s0 + pltpu.roll(s0, HEAD_DIM, 1)
            s1 = s1 + pltpu.roll(s1, HEAD_DIM, 1)
            return jnp.where(lo, s0, s1)

        dx, dwl = _head_rms_bwd(p_ref[:, COL_KB:COL_KB + LANES], w_ref[3:4, :], group_sum(dkb_ref))
        dp_ref[:, COL_KB:COL_KB + LANES] = dx.astype(BF16)
        acc[3] += jnp.sum(dwl, axis=0, keepdims=True)
        dp_ref[:, COL_VB:COL_VB + LANES] = group_sum(dvb_ref).astype(BF16)
        for n in range(4):
            dw_ref[n:n + 1, :] += acc[n]

    wide = pl.BlockSpec((tm, D_ATT), lambda i: (i, 0))
    pa = pl.BlockSpec((tm, D_ATT), lambda i: (i + offa, 0))
    pb = pl.BlockSpec((tm, D_ATT), lambda i: (i + offb, 0))
    whole = pl.BlockSpec((tm, ATTN_PROJ), lambda i: (i, 0))
    return pl.pallas_call(
        body, name=name, grid=(t // tm,),
        in_specs=[whole, pl.BlockSpec((4, LANES), lambda i: (0, 0)), wide, pa, pa, wide, pb, pb],
        out_specs=[whole, pl.BlockSpec((4, LANES), lambda i: (0, 0))],
        out_shape=[jax.ShapeDtypeStruct((t, ATTN_PROJ), BF16), jax.ShapeDtypeStruct((4, LANES), F32)],
        compiler_params=_cp(("arbitrary",)),
    )(proj, ws, dqa, dkpa, dvpa, dqb, dkpb, dvpb)


ROLL_W = 1024


def _rel_onehot():
    r_io = lax.broadcasted_iota(jnp.int32, (REL_W, ROLL_W), 0)
    m_io = lax.broadcasted_iota(jnp.int32, (REL_W, ROLL_W), 1)
    return (r_io == jnp.clip(REL_W - 1 - m_io, -MAX_REL, MAX_REL) + MAX_REL).astype(F32)


def _relpos_fwd(table, name):
    def body(t_ref, o_ref):
        rr = _dot(t_ref[...], _rel_onehot(), 1, 0, HI)

        def step(q, c):
            o_ref[q] = pltpu.roll(rr, (ROLL_W - (TQ - 1) + q) % ROLL_W, 1)[:, :REL_W]
            return c

        lax.fori_loop(0, TQ, step, 0)

    return pl.pallas_call(
        body, name=name, out_shape=jax.ShapeDtypeStruct((TQ, N_HEADS, REL_W), F32),
        in_specs=[pl.BlockSpec(memory_space=pltpu.VMEM)], out_specs=pl.BlockSpec(memory_space=pltpu.VMEM),
        compiler_params=_cp(),
    )(table)


def _relpos_bwd(dbias_t, name):
    def body(d_ref, o_ref):
        def step(q, acc):
            row = jnp.concatenate([d_ref[q], jnp.zeros((N_HEADS, ROLL_W - REL_W), F32)], axis=1)
            return acc + pltpu.roll(row, TQ - 1 - q, 1)

        drr = lax.fori_loop(0, TQ, step, jnp.zeros((N_HEADS, ROLL_W), F32))
        o_ref[...] = _dot(drr, _rel_onehot(), 1, 1, HI)

    return pl.pallas_call(
        body, name=name, out_shape=jax.ShapeDtypeStruct((N_HEADS, REL_W), F32),
        in_specs=[pl.BlockSpec(memory_space=pltpu.VMEM)], out_specs=pl.BlockSpec(memory_space=pltpu.VMEM),
        compiler_params=_cp(),
    )(dbias_t)


def _attn_scores(qe, kw, bias, kvalid):
    return jnp.where(kvalid, _dot(qe, kw, 1, 1) + bias, NEG)


def _stat_cols(stats, e):
    return stats[:, 64 * e:64 * e + 1], stats[:, 64 * e + 32:64 * e + 33]


def _attn_fwd(q, kp, vp, bias, sinks, pad, name, comm=None):
    t, hd = q.shape
    w = pad + TQ

    def body(sink_ref, q_ref, k_ref, v_ref, b_ref, o_ref, st_ref):
        hp, i = pl.program_id(0), pl.program_id(1)
        lo = _lo_mask()
        lane = lax.broadcasted_iota(jnp.int32, (1, LANES), 1)
        for j in range(ATT_SUB):
            start = pl.multiple_of((i * ATT_SUB + j) * TQ, TQ)
            qv = q_ref[TQ * j:TQ * (j + 1), :]
            kw = k_ref[pl.ds(start, w), :]
            vw = v_ref[pl.ds(start, w), :]
            kvalid = (start + lax.broadcasted_iota(jnp.int32, (1, w), 1)) >= pad
            outs, ms, ls = [], [], []
            for e in range(2):
                sel = lo if e == 0 else jnp.logical_not(lo)
                qe = jnp.where(sel, qv, jnp.zeros_like(qv))
                snk = sink_ref[2 * hp + e]
                s = _attn_scores(qe, kw, b_ref[e], kvalid)
                m = jnp.maximum(jnp.max(s, axis=-1, keepdims=True), snk)
                acc = _dot(jnp.exp(s - m).astype(BF16), jnp.where(sel, vw, jnp.ones_like(vw)), 1, 0)
                denom = acc[:, 64 * (1 - e):64 * (1 - e) + 1] + jnp.exp(snk - m)
                outs.append(acc * (1.0 / denom))
                ms.append(m)
                ls.append(denom)
            o_ref[TQ * j:TQ * (j + 1), :] = jnp.where(lo, outs[0], outs[1]).astype(BF16)
            st_ref[TQ * j:TQ * (j + 1), :] = jnp.where(lane < 32, ms[0], jnp.where(lane < 64, ls[0],
                                                                                 jnp.where(lane < 96, ms[1], ls[1])))

    full = pl.BlockSpec((t + pad, LANES), lambda h, i: (0, h))
    tile = pl.BlockSpec((ATT_SUB * TQ, LANES), lambda h, i: (i, h))
    (o, stats), got = _call(
        body, name=name, grid=(hd // LANES, t // (ATT_SUB * TQ)),
        in_specs=[pl.BlockSpec(memory_space=pltpu.SMEM), tile, full, full, pl.BlockSpec((2, TQ, w), lambda h, i: (h, 0, 0))],
        out_specs=[tile, tile], out_shape=[jax.ShapeDtypeStruct((t, hd), BF16), jax.ShapeDtypeStruct((t, hd), F32)],
        args=(sinks, q, kp, vp, bias), sem=("parallel", "arbitrary"), comm=comm)
    return o, stats, got


def _attn_bwd(q, kp, vp, bias, sinks, do, stats, o, col_off, pad, name, comm=None):
    t, hd = q.shape
    w = pad + TQ
    nhp = hd // LANES

    def body(sink_ref, q_ref, k_ref, v_ref, b_ref, do_ref, st_ref, o_ref, dq_ref, dk_ref, dv_ref, db_ref, ds_ref):
        hp, i = pl.program_id(0), pl.program_id(1)

        @pl.when(i == 0)
        def _():
            dk_ref[...] = jnp.zeros_like(dk_ref)
            dv_ref[...] = jnp.zeros_like(dv_ref)
            db_ref[...] = jnp.zeros_like(db_ref)
            ds_ref[...] = jnp.zeros_like(ds_ref)

        lo = _lo_mask()
        row8 = lax.broadcasted_iota(jnp.int32, (8, LANES), 0)
        dbias = [None, None]
        dsink = jnp.zeros((8, LANES), F32)
        for j in range(ATT_SUB):
            start = pl.multiple_of((i * ATT_SUB + j) * TQ, TQ)
            qv = q_ref[TQ * j:TQ * (j + 1), :]
            dov = do_ref[TQ * j:TQ * (j + 1), :]
            kw = k_ref[pl.ds(start, w), :]
            vw = v_ref[pl.ds(start, w), :]
            kvalid = (start + lax.broadcasted_iota(jnp.int32, (1, w), 1)) >= pad
            stats = st_ref[TQ * j:TQ * (j + 1), :]
            od = dov.astype(F32) * o_ref[TQ * j:TQ * (j + 1), :].astype(F32)
            dqs, dkw, dvw = [], None, None
            for e in range(2):
                sel = lo if e == 0 else jnp.logical_not(lo)
                qe = jnp.where(sel, qv, jnp.zeros_like(qv))
                doe = jnp.where(sel, dov, jnp.zeros_like(dov))
                m, denom = _stat_cols(stats, e)
                inv = 1.0 / denom
                p = jnp.exp(_attn_scores(qe, kw, b_ref[e], kvalid) - m) * inv
                psink = jnp.exp(sink_ref[2 * hp + e] - m) * inv
                dp = _dot(doe, vw, 1, 1)
                delta = jnp.sum(jnp.where(sel, od, 0.0), axis=-1, keepdims=True)
                ds = p * (dp - delta)
                dbias[e] = ds if dbias[e] is None else dbias[e] + ds
                dsink = dsink + jnp.where(row8 == e, jnp.sum(-psink * delta, axis=0, keepdims=True), 0.0)
                dsb = ds.astype(BF16)
                dqs.append(_dot(dsb, kw, 1, 0))
                dk_e = _dot(dsb, qe, 0, 0)
                dv_e = _dot(p.astype(BF16), doe, 0, 0)
                dkw = dk_e if dkw is None else dkw + dk_e
                dvw = dv_e if dvw is None else dvw + dv_e
            dq_ref[TQ * j:TQ * (j + 1), :] = jnp.where(lo, dqs[0], dqs[1])
            dk_ref[pl.ds(start, w), :] += dkw
            dv_ref[pl.ds(start, w), :] += dvw
        for e in range(2):
            db_ref[e] += dbias[e]
        ds_ref[0] += dsink

    full = pl.BlockSpec((t + pad, LANES), lambda h, i: (0, h))
    tile = pl.BlockSpec((ATT_SUB * TQ, LANES), lambda h, i: (i, h))
    btile = pl.BlockSpec((2, TQ, w), lambda h, i: (h, 0, 0))
    return _call(
        body, name=name, grid=(nhp, t // (ATT_SUB * TQ)),
        in_specs=[pl.BlockSpec(memory_space=pltpu.SMEM), tile, full, full, btile,
                  pl.BlockSpec((ATT_SUB * TQ, LANES), lambda h, i: (i, h + col_off)), tile, tile],
        out_specs=[tile, full, full, btile, pl.BlockSpec((1, 8, LANES), lambda h, i: (h, 0, 0))],
        out_shape=[jax.ShapeDtypeStruct((t, hd), F32), jax.ShapeDtypeStruct((t + pad, hd), F32),
                   jax.ShapeDtypeStruct((t + pad, hd), F32), jax.ShapeDtypeStruct((N_HEADS, TQ, w), F32),
                   jax.ShapeDtypeStruct((nhp, 8, LANES), F32)],
        args=(sinks, q, kp, vp, bias, do, stats, o), sem=("parallel", "arbitrary"), comm=comm)


def _conv_apply(taps, w_ref, ktaps):
    out = taps[0] * w_ref[ktaps - 1:ktaps, :]
    for s in range(1, ktaps):
        out = out + taps[s] * w_ref[ktaps - 1 - s:ktaps - s, :]
    return out


def _sigmoid(x):
    return jax.nn.sigmoid(x)


def _silu_grad(x):
    sg = _sigmoid(x)
    return x * sg, sg * (1.0 + x * (1.0 - sg))


FFN_HALO = 16
FFN_BT = 256
FFN_BC = 1408


def _ffn_in_mid(x, g, wt, w8, b, name):
    t, d = x.shape
    f = D_FF
    tm = FFN_BT

    def body(x_ref, g_ref, b_ref, w_ref, cb_ref, gu_ref, h_ref, a_ref, gc_ref, halo_ref):
        @pl.when(pl.program_id(0) == 0)
        def _():
            halo_ref[...] = jnp.zeros_like(halo_ref)

        xv = x_ref[...]
        r = lax.rsqrt(jnp.mean(xv * xv, axis=-1, keepdims=True) + EPS)
        h = (xv * r * g_ref[...]).astype(BF16)
        h_ref[...] = h
        for c in range(0, f, FFN_BC):
            cs = slice(c, c + FFN_BC)
            gate = _dot(h, b_ref[c:c + FFN_BC, :], 1, 1).astype(BF16)
            up = _dot(h, b_ref[f + c:f + c + FFN_BC, :], 1, 1).astype(BF16)
            gu_ref[:, cs] = gate
            gu_ref[:, f + c:f + c + FFN_BC] = up
            gf = gate.astype(F32)
            ext = jnp.concatenate([halo_ref[:, cs], gf], axis=0)
            gc = (cb_ref[:, cs] + gf * w_ref[2:3, cs] + pltpu.roll(ext, 1, 0)[8:] * w_ref[1:2, cs]
                  + pltpu.roll(ext, 2, 0)[8:] * w_ref[0:1, cs])
            a_ref[:, cs] = (gc * _sigmoid(gc) * up.astype(F32)).astype(BF16)
            gc_ref[:, cs] = gc.astype(BF16)
            halo_ref[:, cs] = gf[tm - 8:]

    row = pl.BlockSpec((tm, d), lambda i: (i, 0))
    row_f = pl.BlockSpec((tm, f), lambda i: (i, 0))
    return pl.pallas_call(
        body, name=name, grid=(t // tm,),
        in_specs=[row, pl.BlockSpec((1, d), lambda i: (0, 0)), pl.BlockSpec((2 * f, d), lambda i: (0, 0)),
                  pl.BlockSpec((8, f), lambda i: (0, 0)), pl.BlockSpec((1, f), lambda i: (0, 0))],
        out_specs=[pl.BlockSpec((tm, 2 * f), lambda i: (i, 0)), row, row_f, row_f],
        out_shape=[jax.ShapeDtypeStruct((t, 2 * f), BF16), jax.ShapeDtypeStruct((t, d), BF16), jax.ShapeDtypeStruct((t, f), BF16),
                   jax.ShapeDtypeStruct((t, f), BF16)],
        scratch_shapes=[pltpu.VMEM((8, f), F32)], compiler_params=_cp(("arbitrary",)),
    )(x, g, wt, w8, b)


def _ffn_mid_bwd(gu, gc, dxb, w_out, w8, name):
    t, d = dxb.shape
    f = D_FF
    tm, hr = FFN_BT, FFN_HALO
    nt = t // tm
    n = tm + hr

    def body(g_ref, u_ref, un_ref, c_ref, cn_ref, dx_ref, dxn_ref, wo_ref, w_ref, dgu_ref, dw_ref, db_ref):
        i = pl.program_id(0)
        last = i == nt - 1

        @pl.when(i == 0)
        def _():
            dw_ref[...] = jnp.zeros_like(dw_ref)
            db_ref[...] = jnp.zeros_like(db_ref)

        dxe = jnp.concatenate([dx_ref[...], dxn_ref[...]], axis=0)
        row = lax.broadcasted_iota(jnp.int32, (n, 1), 0)
        keep = (row < tm) | jnp.logical_not(last)
        for c in range(0, f, FFN_BC):
            cs = slice(c, c + FFN_BC)
            act, dact = _silu_grad(jnp.concatenate([c_ref[:, cs], cn_ref[:, cs]], axis=0).astype(F32))
            da = _dot(dxe, wo_ref[cs, :], 1, 1)
            up = jnp.concatenate([u_ref[:, cs], un_ref[:, cs]], axis=0).astype(F32)
            dgc = jnp.where(keep, da * up * dact, 0.0)
            nxt = [dgc[:tm], pltpu.roll(dgc, n - 1, 0)[:tm], pltpu.roll(dgc, n - 2, 0)[:tm]]
            dgu_ref[:, f + c:f + c + FFN_BC] = (da[:tm] * act[:tm]).astype(BF16)
            dgu_ref[:, cs] = (nxt[0] * w_ref[2:3, cs] + nxt[1] * w_ref[1:2, cs] + nxt[2] * w_ref[0:1, cs]).astype(BF16)
            gate = g_ref[:, cs].astype(F32)
            db_ref[:, cs] += jnp.sum(nxt[0], axis=0, keepdims=True)
            for s in range(3):
                dw_ref[2 - s:3 - s, cs] += jnp.sum(nxt[s] * gate, axis=0, keepdims=True)

    r = tm // hr
    nxt_blk = lambda i: jnp.minimum((i + 1) * r, t // hr - 1)
    row_f = pl.BlockSpec((tm, f), lambda i: (i, 0))
    halo_f = pl.BlockSpec((hr, f), lambda i: (nxt_blk(i), 0))
    return pl.pallas_call(
        body, name=name, grid=(nt,),
        in_specs=[row_f, pl.BlockSpec((tm, f), lambda i: (i, 1)), pl.BlockSpec((hr, f), lambda i: (nxt_blk(i), 1)),
                  row_f, halo_f,
                  pl.BlockSpec((tm, d), lambda i: (i, 0)), pl.BlockSpec((hr, d), lambda i: (nxt_blk(i), 0)),
                  pl.BlockSpec((f, d), lambda i: (0, 0)), pl.BlockSpec((8, f), lambda i: (0, 0))],
        out_specs=[pl.BlockSpec((tm, 2 * f), lambda i: (i, 0)), pl.BlockSpec((8, f), lambda i: (0, 0)),
                   pl.BlockSpec((1, f), lambda i: (0, 0))],
        out_shape=[jax.ShapeDtypeStruct((t, 2 * f), BF16), jax.ShapeDtypeStruct((8, f), F32), jax.ShapeDtypeStruct((1, f), F32)],
        compiler_params=_cp(("arbitrary",)),
    )(gu, gu, gu, gc, gc, dxb, dxb, w_out, w8)


PRE_TM = 256
PRE_TC = 1024


def _ssm_in_pre(x, g, wt, w8, b, name):
    t, d = x.shape
    tm, tc = PRE_TM, PRE_TC

    def body(x_ref, g_ref, b_ref, w_ref, cb_ref, zx_ref, h_ref, o_ref, c_ref, halo_ref):
        @pl.when(pl.program_id(0) == 0)
        def _():
            halo_ref[...] = jnp.zeros_like(halo_ref)

        xv = x_ref[...]
        r = lax.rsqrt(jnp.mean(xv * xv, axis=-1, keepdims=True) + EPS)
        h = (xv * r * g_ref[...]).astype(BF16)
        h_ref[...] = h
        for c in range(0, ZX, tc):
            v = _dot(h, b_ref[c:c + tc, :], 1, 1)
            zx_ref[:, c:c + tc] = v
            if c >= D_INNER:
                cs = slice(c - D_INNER, c - D_INNER + tc)
                ext = jnp.concatenate([halo_ref[:, cs], v], axis=0)
                conv = cb_ref[:, cs] + v * w_ref[3:4, cs]
                for s in (1, 2, 3):
                    conv = conv + pltpu.roll(ext, s, 0)[8:] * w_ref[3 - s:4 - s, cs]
                o_ref[:, cs] = conv * _sigmoid(conv)
                c_ref[:, cs] = conv.astype(BF16)
                halo_ref[:, cs] = v[tm - 8:]

    row = pl.BlockSpec((tm, d), lambda i: (i, 0))
    row_x = pl.BlockSpec((tm, XBC), lambda i: (i, 0))
    return pl.pallas_call(
        body, name=name, grid=(t // tm,),
        in_specs=[row, pl.BlockSpec((1, d), lambda i: (0, 0)), pl.BlockSpec(wt.shape, lambda i: (0, 0)),
                  pl.BlockSpec((8, XBC), lambda i: (0, 0)), pl.BlockSpec((1, XBC), lambda i: (0, 0))],
        out_specs=[pl.BlockSpec((tm, ZX), lambda i: (i, 0)), row, row_x, row_x],
        out_shape=[jax.ShapeDtypeStruct((t, ZX), F32), jax.ShapeDtypeStruct((t, d), BF16), jax.ShapeDtypeStruct((t, XBC), F32),
                   jax.ShapeDtypeStruct((t, XBC), BF16)],
        scratch_shapes=[pltpu.VMEM((8, XBC), F32)], compiler_params=_cp(("arbitrary",)),
    )(x, g, wt, w8, b)


PRE_HALO = 16


def _ssm_pre_bwd(zx, conv, dxbc, w8, name):
    t = zx.shape[0]
    tm, tc, hr = PRE_TM, PRE_TC, PRE_HALO
    off = D_INNER // tc
    nt = t // tm
    n = tm + hr

    def body(x_ref, c_ref, cn_ref, d_ref, dn_ref, w_ref, o_ref, dw_ref, db_ref):
        i = pl.program_id(1)
        last = i == nt - 1

        @pl.when(i == 0)
        def _():
            dw_ref[...] = jnp.zeros_like(dw_ref)
            db_ref[...] = jnp.zeros_like(db_ref)

        _, dact = _silu_grad(jnp.concatenate([c_ref[...], cn_ref[...]], axis=0).astype(F32))
        row = lax.broadcasted_iota(jnp.int32, (n, 1), 0)
        dc = jnp.where((row < tm) | jnp.logical_not(last), jnp.concatenate([d_ref[...], dn_ref[...]], axis=0) * dact, 0.0)
        nxt = [dc[:tm]] + [pltpu.roll(dc, n - s, 0)[:tm] for s in (1, 2, 3)]
        o_ref[...] = _conv_apply(nxt, w_ref, 4).astype(BF16)
        xv = x_ref[...]
        db_ref[...] += jnp.sum(nxt[0], axis=0, keepdims=True)
        for s in range(4):
            dw_ref[3 - s:4 - s, :] += jnp.sum(nxt[s] * xv, axis=0, keepdims=True)

    nxt_blk = lambda i: jnp.minimum((i + 1) * (tm // hr), t // hr - 1)
    tile = pl.BlockSpec((tm, tc), lambda j, i: (i, j))
    halo = pl.BlockSpec((hr, tc), lambda j, i: (nxt_blk(i), j))
    return pl.pallas_call(
        body, name=name, grid=(XBC // tc, nt),
        in_specs=[pl.BlockSpec((tm, tc), lambda j, i: (i, j + off)), tile, halo, tile, halo,
                  pl.BlockSpec((8, tc), lambda j, i: (0, j))],
        out_specs=[tile, pl.BlockSpec((8, tc), lambda j, i: (0, j)), pl.BlockSpec((1, tc), lambda j, i: (0, j))],
        out_shape=[jax.ShapeDtypeStruct((t, XBC), BF16), jax.ShapeDtypeStruct((8, XBC), F32),
                   jax.ShapeDtypeStruct((1, XBC), F32)],
        compiler_params=_cp(("parallel", "arbitrary")),
    )(zx, conv, conv, dxbc, dxbc, w8)


def _head_lanes():
    return lax.broadcasted_iota(jnp.int32, (1, LANES), 1) < SSM_HEADS


def _dt_fwd(dtraw, bias, name):
    t = dtraw.shape[0]
    tm = _pick(t, (1024, 512, 256, 128))

    def body(x_ref, b_ref, o_ref):
        v = x_ref[...] + b_ref[...]
        sp = jnp.maximum(v, 0.0) + jnp.log(1.0 + jnp.exp(-jnp.abs(v)))
        o_ref[...] = jnp.where(_head_lanes(), sp, 0.0)

    row = pl.BlockSpec((tm, LANES), lambda i: (i, 0))
    return pl.pallas_call(
        body, name=name, grid=(t // tm,), in_specs=[row, pl.BlockSpec((1, LANES), lambda i: (0, 0))], out_specs=row,
        out_shape=jax.ShapeDtypeStruct((t, LANES), F32), compiler_params=_cp(("parallel",)),
    )(dtraw, bias)


def _dt_bwd(dtraw, bias, ddt, name):
    t = dtraw.shape[0]
    tm = _pick(t, (1024, 512, 256, 128))

    def body(x_ref, b_ref, d_ref, o_ref, db_ref):
        @pl.when(pl.program_id(0) == 0)
        def _():
            db_ref[...] = jnp.zeros_like(db_ref)

        g = jnp.where(_head_lanes(), d_ref[...] * _sigmoid(x_ref[...] + b_ref[...]), 0.0)
        o_ref[...] = g.astype(BF16)
        db_ref[...] += jnp.sum(g, axis=0, keepdims=True)

    row = pl.BlockSpec((tm, LANES), lambda i: (i, 0))
    vec = pl.BlockSpec((1, LANES), lambda i: (0, 0))
    return pl.pallas_call(
        body, name=name, grid=(t // tm,), in_specs=[row, vec, row], out_specs=[row, vec],
        out_shape=[jax.ShapeDtypeStruct((t, LANES), BF16), jax.ShapeDtypeStruct((1, LANES), F32)],
        compiler_params=_cp(("arbitrary",)),
    )(dtraw, bias, ddt)


GROUP_W = D_INNER // SSM_GROUPS


def _ssd_common(dt, alog):
    ll = dt.shape[0]
    a_neg = -jnp.exp(alog)
    a = dt * a_neg
    ri = lax.broadcasted_iota(jnp.int32, (ll, ll), 0)
    ci = lax.broadcasted_iota(jnp.int32, (ll, ll), 1)
    tril = ri >= ci
    acs = _dot(tril.astype(F32), a, 1, 0, HI)
    return a_neg, tril, acs, acs.T


def _pair_terms(acs, acs_t, dt, h0, lo):
    ll = acs.shape[0]
    cols = [acs[:, h0 + e:h0 + e + 1] for e in range(2)]
    rows = [acs_t[h0 + e:h0 + e + 1, :] for e in range(2)]
    dtc = [dt[:, h0 + e:h0 + e + 1] for e in range(2)]
    lasts = [c[ll - 1:ll, :] for c in cols]
    dtx = jnp.where(lo, dtc[0], dtc[1])
    eac = jnp.where(lo, jnp.exp(cols[0]), jnp.exp(cols[1]))
    fdec = jnp.where(lo, jnp.exp(lasts[0] - cols[0]), jnp.exp(lasts[1] - cols[1]))
    elast = jnp.where(lo, jnp.exp(lasts[0]), jnp.exp(lasts[1]))
    return cols, rows, dtx, eac, fdec, elast


def _decay(col, row, tril):
    return jnp.where(tril, jnp.exp(jnp.minimum(col - row, 0.0)), 0.0)


def _two_heads_rows(v, lo):
    z = jnp.zeros_like(v)
    return jnp.concatenate([jnp.where(lo, v, z), jnp.where(lo, z, v)], axis=0)


def _two_heads_cols(ms):
    return jnp.concatenate(ms, axis=1)


def _z_group(z_refs, g):
    return z_refs[g // 2][:, GROUP_W * (g % 2):GROUP_W * (g % 2 + 1)]


def _ssd_fwd(xbc, dt, alog, zx, dexp, nw, name, comm=None):
    t = xbc.shape[0]
    ll = SSD_L
    nc = t // ll

    def body(x_ref, dt_ref, al_ref, z0_ref, z1_ref, d_ref, w_ref, y_ref, sp_ref, y4_ref, st_ref):
        @pl.when(pl.program_id(0) == 0)
        def _():
            st_ref[...] = jnp.zeros_like(st_ref)

        dtv = dt_ref[...]
        _, tril, acs, acs_t = _ssd_common(dtv, al_ref[...])
        lo = _lo_mask()
        sp_ref[0] = st_ref[...]
        for g in range(SSM_GROUPS):
            bg = x_ref[:, D_INNER + SSM_STATE * g:D_INNER + SSM_STATE * (g + 1)].astype(BF16)
            cg = x_ref[:, D_INNER + 512 + SSM_STATE * g:D_INNER + 512 + SSM_STATE * (g + 1)].astype(BF16)
            gm = _dot(cg, bg, 1, 1)
            g0 = GROUP_W * g
            terms = [_pair_terms(acs, acs_t, dtv, 8 * g + 2 * pp, lo) for pp in range(4)]
            dtx, eac, fdec, elast = [jnp.concatenate([tt[k] for tt in terms], axis=1) for k in (2, 3, 4, 5)]
            xg = x_ref[:, g0:g0 + GROUP_W]
            ug = (xg * dtx).astype(BF16)
            sg = st_ref[:, g0:g0 + GROUP_W]
            yst = _dot(cg, sg.astype(BF16), 1, 0) * eac
            st_ref[:, g0:g0 + GROUP_W] = sg * elast + _dot(bg, (xg * (fdec * dtx)).astype(BF16), 0, 0)
            ys = []
            for pp in range(4):
                cols, rows = terms[pp][0], terms[pp][1]
                sl = slice(LANES * pp, LANES * (pp + 1))
                y_in = _dot(_two_heads_cols([(gm * _decay(cols[e], rows[e], tril)).astype(BF16) for e in range(2)]),
                            _two_heads_rows(ug[:, sl], lo), 1, 0)
                ys.append(y_in + yst[:, sl])
            yg = jnp.concatenate(ys, axis=1)
            y_ref[:, g0:g0 + GROUP_W] = yg
            zg = _z_group((z0_ref, z1_ref), g)
            y3 = (yg + d_ref[:, g0:g0 + GROUP_W] * xg) * (zg * _sigmoid(zg))
            r = lax.rsqrt(jnp.mean(y3 * y3, axis=-1, keepdims=True) + EPS)
            y4_ref[:, g0:g0 + GROUP_W] = (y3 * r * w_ref[:, g0:g0 + GROUP_W]).astype(BF16)

    zblk = lambda j: pl.BlockSpec((ll, 1024), lambda c: (c, j))
    vec = pl.BlockSpec((1, D_INNER), lambda c: (0, 0))
    row = pl.BlockSpec((ll, D_INNER), lambda c: (c, 0))
    return _call(
        body, name=name, grid=(nc,),
        in_specs=[pl.BlockSpec((ll, XBC), lambda c: (c, 0)), pl.BlockSpec((ll, LANES), lambda c: (c, 0)),
                  pl.BlockSpec((1, LANES), lambda c: (0, 0)), zblk(0), zblk(1), vec, vec],
        out_specs=[row, pl.BlockSpec((1, SSM_STATE, D_INNER), lambda c: (c, 0, 0)), row],
        out_shape=[jax.ShapeDtypeStruct((t, D_INNER), F32), jax.ShapeDtypeStruct((nc, SSM_STATE, D_INNER), F32),
                   jax.ShapeDtypeStruct((t, D_INNER), BF16)],
        scratch_shapes=[pltpu.VMEM((SSM_STATE, D_INNER), F32)],
        args=(xbc, dt, alog, zx, zx, dexp, nw), sem=("arbitrary",), comm=comm)


def _ssd_bwd(xbc, dt, alog, sprev, dy4, y, zx, dexp, nw, name, comm=None):
    t = xbc.shape[0]
    ll = SSD_L_BWD if t % SSD_L_BWD == 0 else SSD_L
    nc = t // ll
    every = ll // SSD_L

    def body(x_ref, dt_ref, al_ref, sp_ref, g4_ref, y_ref, z0_ref, z1_ref, d_ref, w_ref,
             dx_ref, ddt_ref, dal_ref, dz_ref, dd_ref, dnw_ref, ds_ref, colt_ref):
        @pl.when(pl.program_id(0) == 0)
        def _():
            ds_ref[...] = jnp.zeros_like(ds_ref)
            dal_ref[...] = jnp.zeros_like(dal_ref)
            dd_ref[...] = jnp.zeros_like(dd_ref)
            dnw_ref[...] = jnp.zeros_like(dnw_ref)

        dtv = dt_ref[...]
        a_neg, tril, acs, acs_t = _ssd_common(dtv, al_ref[...])
        lo = _lo_mask()
        hi = jnp.logical_not(lo)
        lane = lax.broadcasted_iota(jnp.int32, (1, LANES), 1)
        colt_ref[...] = jnp.zeros_like(colt_ref)
        rowterm = jnp.zeros((ll, LANES), F32)
        ddt_u = jnp.zeros((ll, LANES), F32)
        dlast = jnp.zeros((1, LANES), F32)

        def halves(v):
            return (jnp.sum(jnp.where(lo, v, 0.0), axis=-1, keepdims=True),
                    jnp.sum(jnp.where(hi, v, 0.0), axis=-1, keepdims=True))

        for g in range(SSM_GROUPS):
            cb0 = D_INNER + SSM_STATE * g
            cc0 = D_INNER + 512 + SSM_STATE * g
            bg = x_ref[:, cb0:cb0 + SSM_STATE].astype(BF16)
            cg = x_ref[:, cc0:cc0 + SSM_STATE].astype(BF16)
            gm = _dot(cg, bg, 1, 1)
            g0 = GROUP_W * g
            terms = [_pair_terms(acs, acs_t, dtv, 8 * g + 2 * pp, lo) for pp in range(4)]
            dtx, eac, fdec, elast = [jnp.concatenate([tt[k] for tt in terms], axis=1) for k in (2, 3, 4, 5)]
            xg = x_ref[:, g0:g0 + GROUP_W]
            u32 = xg * dtx
            ug = u32.astype(BF16)
            zg = _z_group((z0_ref, z1_ref), g)
            dg = d_ref[:, g0:g0 + GROUP_W]
            act, dact = _silu_grad(zg)
            y2 = y_ref[:, g0:g0 + GROUP_W] + dg * xg
            y3 = y2 * act
            rn = lax.rsqrt(jnp.mean(y3 * y3, axis=-1, keepdims=True) + EPS)
            y3n = y3 * rn
            gv = g4_ref[:, g0:g0 + GROUP_W]
            dyn = gv * w_ref[:, g0:g0 + GROUP_W]
            dy3 = rn * (dyn - y3n * jnp.mean(dyn * y3n, axis=-1, keepdims=True))
            dyg = dy3 * act
            dskip = dyg * dg
            dz_ref[:, g0:g0 + GROUP_W] = (dy3 * y2 * dact).astype(BF16)
            dd_ref[:, g0:g0 + GROUP_W] += jnp.sum(dyg * xg, axis=0, keepdims=True)
            dnw_ref[:, g0:g0 + GROUP_W] += jnp.sum(gv * y3n, axis=0, keepdims=True)
            dyb = dyg.astype(BF16)
            spg = sp_ref[0, :, g0:g0 + GROUP_W]
            spb = spg.astype(BF16)
            dsg = ds_ref[:, g0:g0 + GROUP_W]
            dsb = dsg.astype(BF16)
            du_st = _dot(bg, dsb, 1, 0) * fdec
            yst = _dot(cg, spb, 1, 0) * eac
            dye = (dyg * eac).astype(BF16)
            dc_st = _dot(dye, spb, 1, 1)
            db_st = _dot((xg * (fdec * dtx)).astype(BF16), dsb, 1, 1)
            ds_ref[:, g0:g0 + GROUP_W] = dsg * elast + _dot(cg, dye, 0, 0)
            qst_el = du_st * u32
            rq_el = dyg * yst - qst_el
            q_row = jnp.sum(qst_el, axis=0, keepdims=True)
            s_row = jnp.sum(dsg * spg, axis=0, keepdims=True)
            dgm = jnp.zeros((ll, ll), F32)
            for pp in range(4):
                h0 = 8 * g + 2 * pp
                cols, rows = terms[pp][0], terms[pp][1]
                sl = slice(LANES * pp, LANES * (pp + 1))
                decs = [_decay(cols[e], rows[e], tril) for e in range(2)]
                wms = [gm * d for d in decs]
                dum2 = _dot(dyb[:, sl], _two_heads_rows(ug[:, sl], lo), 1, 1)
                du = _dot(jnp.concatenate([wm.astype(BF16) for wm in wms], axis=0),
                          _two_heads_rows(dyb[:, sl], lo), 0, 0) + du_st[:, sl]
                dx_ref[:, g0 + LANES * pp:g0 + LANES * (pp + 1)] = du * dtx[:, sl] + dskip[:, sl]
                ddtu = halves(du * xg[:, sl])
                rq = halves(rq_el[:, sl])
                qs = halves(q_row[:, sl])
                ss = halves(s_row[:, sl])
                for e in range(2):
                    dum = dum2[:, ll * e:ll * (e + 1)]
                    dgm = dgm + dum * decs[e]
                    tm_ = dum * wms[e]
                    oh = lane == (h0 + e)
                    rowterm = rowterm + jnp.where(oh, jnp.sum(tm_, axis=1, keepdims=True) + rq[e], 0.0)
                    ddt_u = ddt_u + jnp.where(oh, ddtu[e], 0.0)
                    dlast = dlast + jnp.where(oh, jnp.exp(cols[e][ll - 1:ll, :]) * ss[e] + qs[e], 0.0)
                    colt_ref[h0 + e:h0 + e + 1, :] = jnp.sum(tm_, axis=0, keepdims=True)
            dgb = dgm.astype(BF16)
            dx_ref[:, cc0:cc0 + SSM_STATE] = _dot(dgb, bg, 1, 0) + dc_st
            dx_ref[:, cb0:cb0 + SSM_STATE] = _dot(dgb, cg, 0, 0) + db_st
        row_io = lax.broadcasted_iota(jnp.int32, (ll, LANES), 0)
        dacs = rowterm - colt_ref[...].T + jnp.where(row_io == ll - 1, dlast, 0.0)
        da = _dot(jnp.logical_not(tril).astype(F32) + jnp.where(
            lax.broadcasted_iota(jnp.int32, (ll, ll), 0) == lax.broadcasted_iota(jnp.int32, (ll, ll), 1), 1.0, 0.0),
            dacs, 1, 0, HI)
        ddt_ref[...] = da * a_neg + ddt_u
        dal_ref[...] += jnp.sum(da * dtv, axis=0, keepdims=True) * a_neg

    rev = lambda c: nc - 1 - c
    row = pl.BlockSpec((ll, D_INNER), lambda c: (rev(c), 0))
    vec = pl.BlockSpec((1, D_INNER), lambda c: (0, 0))
    zblk = lambda j: pl.BlockSpec((ll, 1024), lambda c: (rev(c), j))
    return _call(
        body, name=name, grid=(nc,),
        in_specs=[pl.BlockSpec((ll, XBC), lambda c: (rev(c), 0)), pl.BlockSpec((ll, LANES), lambda c: (rev(c), 0)),
                  pl.BlockSpec((1, LANES), lambda c: (0, 0)),
                  pl.BlockSpec((1, SSM_STATE, D_INNER), lambda c: (rev(c) * every, 0, 0)), row, row, zblk(0), zblk(1), vec, vec],
        out_specs=[pl.BlockSpec((ll, XBC), lambda c: (rev(c), 0)), pl.BlockSpec((ll, LANES), lambda c: (rev(c), 0)),
                   pl.BlockSpec((1, LANES), lambda c: (0, 0)), row, vec, vec],
        out_shape=[jax.ShapeDtypeStruct((t, XBC), F32), jax.ShapeDtypeStruct((t, LANES), F32),
                   jax.ShapeDtypeStruct((1, LANES), F32), jax.ShapeDtypeStruct((t, D_INNER), BF16),
                   jax.ShapeDtypeStruct((1, D_INNER), F32), jax.ShapeDtypeStruct((1, D_INNER), F32)],
        scratch_shapes=[pltpu.VMEM((SSM_STATE, D_INNER), F32), pltpu.VMEM((LANES, ll), F32)],
        args=(xbc, dt, alog, sprev, dy4, y, zx, zx, dexp, nw), sem=("arbitrary",), comm=comm)


def _sum_parts(parts, name):
    nparts, r, c = parts.shape
    tc = _pick(c, (256, 128))

    def body(p_ref, o_ref):
        g = p_ref[0].astype(F32)
        for k in range(1, nparts):
            g = g + p_ref[k].astype(F32)
        o_ref[...] = g

    return pl.pallas_call(
        body, name=name, grid=(c // tc,), in_specs=[pl.BlockSpec((nparts, r, tc), lambda j: (0, 0, j))],
        out_specs=pl.BlockSpec((r, tc), lambda j: (0, j)), out_shape=jax.ShapeDtypeStruct((r, c), F32),
        compiler_params=_cp(("parallel",)),
    )(parts)


def _adamw(parts, w, m, v, name):
    nl, r, c = w.shape
    assert len(parts) == nl
    tr = _pick(r, (256, 128, 64))
    c1 = 1.0 - ADAM_B1 ** ADAM_STEP
    c2 = 1.0 - ADAM_B2 ** ADAM_STEP

    def body(*refs):
        p_refs = refs[:nl]
        w_ref, m_ref, v_ref, g_ref, d_ref, mo_ref, vo_ref = refs[nl:]
        g = None
        for l, p_ref in enumerate(p_refs):
            s = p_ref[0].astype(F32)
            for k in range(1, p_ref.shape[0]):
                s = s + p_ref[k].astype(F32)
            g = s if g is None else jnp.where(pl.program_id(0) == l, s, g)
        mn = ADAM_B1 * m_ref[0] + (1.0 - ADAM_B1) * g
        vn = ADAM_B2 * v_ref[0] + (1.0 - ADAM_B2) * (g * g)
        g_ref[0] = g
        mo_ref[0] = mn
        vo_ref[0] = vn
        d_ref[0] = -ADAM_LR * ((mn / c1) / (jnp.sqrt(vn / c2) + ADAM_EPS) + ADAM_WD * w_ref[0])

    row = pl.BlockSpec((1, tr, c), lambda l, i: (l, i, 0))
    sd = jax.ShapeDtypeStruct((nl, r, c), F32)
    return pl.pallas_call(
        body, name=name, grid=(nl, r // tr),
        in_specs=[pl.BlockSpec((p.shape[0], tr, c), lambda l, i: (0, i, 0)) for p in parts] + [row, row, row],
        out_specs=[row, row, row, row], out_shape=[sd, sd, sd, sd], compiler_params=_cp(("parallel", "parallel")),
    )(*parts, w, m, v)


def _peers():
    mx, my, mc = lax.axis_index("x"), lax.axis_index("y"), lax.axis_index("c")
    me = 4 * mx + 2 * my + mc
    out = []
    for k in range(1, N_DEV):
        px = 1 - mx if k & 4 else mx
        py = 1 - my if k & 2 else my
        pc = 1 - mc if k & 1 else mc
        out.append(((px, py, pc), 4 * px + 2 * py + pc))
    return me, out


class _Comm:
    def __init__(self, arrs, scatters):
        self.arrs, self.scatters, self.n = list(arrs), list(scatters), len(arrs)
        self.specs = [pl.BlockSpec(memory_space=pl.ANY)] * self.n
        self.out_shape = [jax.ShapeDtypeStruct(x.shape if sc else (N_DEV,) + x.shape, x.dtype)
                          for x, sc in zip(self.arrs, self.scatters)]
        np_ = N_DEV - 1
        self.scratch = [pltpu.SemaphoreType.DMA((np_ * self.n,)), pltpu.SemaphoreType.DMA((np_ * self.n,)),
                        pltpu.SemaphoreType.DMA((self.n,))]

    def _copies(self, x_refs, o_refs, sems):
        send_sems, recv_sems, local_sems = sems
        me, peers = _peers()
        np_ = N_DEV - 1
        local, sends, recvs = [], [], []
        for a in range(self.n):
            mine = x_refs[a].at[me] if self.scatters[a] else x_refs[a]
            local.append(pltpu.make_async_copy(mine, o_refs[a].at[me], local_sems.at[a]))
        for k, (dev, idx) in enumerate(peers):
            for a in range(self.n):
                mine = x_refs[a].at[me] if self.scatters[a] else x_refs[a]
                sends.append(pltpu.make_async_remote_copy(
                    src_ref=x_refs[a].at[idx] if self.scatters[a] else x_refs[a], dst_ref=o_refs[a].at[me],
                    send_sem=send_sems.at[a * np_ + k], recv_sem=recv_sems.at[a * np_ + k], device_id=dev, device_id_type=MESH))
                recvs.append(pltpu.make_async_remote_copy(
                    src_ref=mine, dst_ref=o_refs[a].at[idx], send_sem=send_sems.at[a * np_ + k],
                    recv_sem=recv_sems.at[a * np_ + k], device_id=dev, device_id_type=MESH))
        return local, sends, recvs

    def start(self, x_refs, o_refs, sems):
        local, sends, _ = self._copies(x_refs, o_refs, sems)
        for cp in local + sends:
            cp.start()

    def wait(self, x_refs, o_refs, sems):
        local, sends, recvs = self._copies(x_refs, o_refs, sems)
        for cp in recvs:
            cp.wait_recv()
        for cp in sends:
            cp.wait_send()
        for cp in local:
            cp.wait()


class _Gather2(_Comm):
    def __init__(self, arrs):
        super().__init__(arrs, [False] * len(arrs))

    def _plan(self, x_refs, o_refs, sems):
        send_sems, recv_sems, local_sems = sems
        mx, my, mc = lax.axis_index("x"), lax.axis_index("y"), lax.axis_index("c")
        slot = lambda px, py, pc: 4 * px + 2 * py + pc
        sib = (mx, my, 1 - mc)
        chips = [(1 - mx, my), (mx, 1 - my), (1 - mx, 1 - my)]
        np_ = N_DEV - 1
        local, first, passed, arrive_first, arrive_rest = [], [], [], [], []

        def copy(a, k, src, block, to):
            return pltpu.make_async_remote_copy(
                src_ref=src, dst_ref=o_refs[a].at[block], send_sem=send_sems.at[a * np_ + k], recv_sem=recv_sems.at[a * np_ + k],
                device_id=to, device_id_type=MESH)

        for a in range(self.n):
            me = slot(mx, my, mc)
            local.append(pltpu.make_async_copy(x_refs[a], o_refs[a].at[me], local_sems.at[a]))
            first.append(copy(a, 0, x_refs[a], me, sib))
            arrive_rest.append(copy(a, 0, x_refs[a], slot(*sib), sib))
            for j, (cx, cy) in enumerate(chips):
                first.append(copy(a, 1 + j, x_refs[a], me, (cx, cy, mc)))
                arrive_first.append(copy(a, 1 + j, x_refs[a], slot(cx, cy, mc), (cx, cy, mc)))
                passed.append(copy(a, 4 + j, o_refs[a].at[slot(cx, cy, mc)], slot(cx, cy, mc), sib))
                arrive_rest.append(copy(a, 4 + j, x_refs[a], slot(cx, cy, 1 - mc), sib))
        return local, first, passed, arrive_first, arrive_rest

    def start(self, x_refs, o_refs, sems):
        local, first, _, _, _ = self._plan(x_refs, o_refs, sems)
        for cp in local + first:
            cp.start()

    def wait(self, x_refs, o_refs, sems):
        local, first, passed, arrive_first, arrive_rest = self._plan(x_refs, o_refs, sems)
        for arrived, onward in zip(arrive_first, passed):
            arrived.wait_recv()
            onward.start()
        for cp in arrive_rest:
            cp.wait_recv()
        for cp in first + passed:
            cp.wait_send()
        for cp in local:
            cp.wait()


def _call(body, *, name, grid, in_specs, out_specs, out_shape, args, scratch_shapes=(), sem=None, comm=None):
    if comm is None:
        outs = pl.pallas_call(
            body, name=name, grid=grid, in_specs=list(in_specs), out_specs=list(out_specs), out_shape=list(out_shape),
            scratch_shapes=list(scratch_shapes), compiler_params=_cp(sem),
        )(*args)
        return list(outs), []
    n_in, n_out, nc = len(in_specs), len(out_specs), comm.n
    nsteps = 1
    for g in grid:
        nsteps *= g

    def carrier(*refs):
        ins, cin = refs[:n_in], refs[n_in:n_in + nc]
        outs, cout = refs[n_in + nc:n_in + nc + n_out], refs[n_in + nc + n_out:n_in + 2 * nc + n_out]
        rest = refs[n_in + 2 * nc + n_out:]
        scratch, sems = rest[:len(rest) - 3], rest[len(rest) - 3:]
        if nsteps == 1:
            comm.start(cin, cout, sems)
            body(*ins, *outs, *scratch)
            comm.wait(cin, cout, sems)
            return
        step = 0
        for d, g in enumerate(grid):
            step = step * g + pl.program_id(d)

        @pl.when(step == 0)
        def _():
            comm.start(cin, cout, sems)

        body(*ins, *outs, *scratch)

        @pl.when(step == nsteps - 1)
        def _():
            comm.wait(cin, cout, sems)

    outs = pl.pallas_call(
        carrier, name=name, grid=grid, in_specs=list(in_specs) + comm.specs, out_specs=list(out_specs) + comm.specs,
        out_shape=list(out_shape) + comm.out_shape, scratch_shapes=list(scratch_shapes) + comm.scratch,
        compiler_params=_cp(("arbitrary",) * len(grid) if grid else None),
    )(*args, *comm.arrs)
    return list(outs[:n_out]), list(outs[n_out:])


def _exchange(comm, name):
    return _call(lambda *refs: None, name=name, grid=(), in_specs=[], out_specs=[], out_shape=[], args=[], comm=comm)[1]


def _pack(arrs, dtype, lead=()):
    nl = len(lead)
    flat = jnp.concatenate([a.astype(dtype).reshape(lead + (-1,)) for a in arrs], axis=nl)
    n = flat.shape[-1]
    rows = -(-n // (LANES * 8)) * 8
    flat = jnp.pad(flat, [(0, 0)] * nl + [(0, rows * LANES - n)])
    return flat.reshape(lead + (rows, LANES))


def _unpack(flat, shapes, lead=()):
    nl = len(lead)
    flat = flat.reshape(lead + (-1,))
    out, o = [], 0
    for s in shapes:
        n = 1
        for d in s:
            n *= d
        out.append(lax.slice_in_dim(flat, o, o + n, axis=nl).reshape(lead + tuple(s)))
        o += n
    return out


def _join(g, ax):
    return jnp.concatenate([g[d] for d in range(N_DEV)], axis=ax)


def _split(full, ax):
    n = full.shape[ax] // N_DEV
    return jnp.stack([lax.slice_in_dim(full, d * n, (d + 1) * n, axis=ax) for d in range(N_DEV)])


_WEIGHTS = ['norm_mix', 'norm_ffn', 'attn_w_in', 'attn_w_out', 'relpos_table', 'q_norm_a', 'k_norm_a', 'q_norm_b',
            'k_norm_b', 'sinks', 'ssm_w_in', 'ssm_conv_w', 'ssm_conv_b', 'ssm_dt_bias', 'ssm_a_log', 'ssm_d', 'ssm_norm',
            'ssm_w_out', 'ffn_w_in', 'ffn_conv_w', 'ffn_conv_b', 'ffn_w_out']
_SHARD_AX = {'attn_w_in': 2, 'attn_w_out': 1, 'ssm_w_in': 2, 'ssm_conv_w': 2, 'ssm_conv_b': 1, 'ssm_norm': 1,
             'ssm_w_out': 1, 'ffn_w_in': 2, 'ffn_conv_w': 2, 'ffn_w_out': 1}
_BIG = ['attn_w_in', 'attn_w_out', 'ssm_w_in', 'ssm_w_out', 'ffn_w_in', 'ffn_w_out']
_SMALL = ['ssm_conv_w', 'ssm_conv_b', 'ssm_norm', 'ffn_conv_w']
_AX2 = {n: _SHARD_AX[n] - 1 for n in _BIG}
_REPL = [n for n in _WEIGHTS if n not in _SHARD_AX]


def _rows8(w):
    return jnp.pad(w, ((0, 8 - w.shape[0]), (0, 0)))


def _lanes128(v):
    return jnp.pad(v, (0, LANES - v.shape[0])).reshape(1, LANES)


def _band_mask(n_prev, pad):
    cq = jnp.arange(TQ)[:, None] // CHUNK
    ck = jnp.arange(pad + TQ)[None, :] // CHUNK
    return (ck >= cq) & (ck <= cq + n_prev)


def _ffn_fwd(xin, g, w_in_t, w8, cb, tag):
    gu, h, a, gc = _ffn_in_mid(xin, g, w_in_t, w8, cb, f"mm_ffn_in{tag}")
    return a, (h, gu, a, gc)


def _ffn_bwd(dx, dxb, xin, g, w_in_t, w8, w_out, saved, tag):
    h, gu, a, gc = saved
    dw_out = _mm_tn(a, dxb, f"mm_ffn_dwout{tag}")
    dgu, dw8, dcb = _ffn_mid_bwd(gu, gc, dxb, w_out, w8, f"ffn_mid_bwd{tag}")
    dw_in_t = _mm_tn(dgu, h, f"mm_ffn_dwin{tag}")
    dxp, dxpb, dg = _mm_rms_bwd(dgu, w_in_t, 0, None, xin, g, dx, f"mm_ffn_dh{tag}")
    return dxp, dxpb, dg, dw_in_t, dw8[:3], dcb, dw_out


def kernel(x, norm_mix, norm_ffn, attn_w_in, attn_w_out, relpos_table, q_norm_a, k_norm_a, q_norm_b, k_norm_b, sinks, ssm_w_in, ssm_conv_w, ssm_conv_b, ssm_dt_bias, ssm_a_log, ssm_d, ssm_norm, ssm_w_out, ffn_w_in, ffn_conv_w, ffn_conv_b, ffn_w_out, loss_target, m_norm_mix, m_norm_ffn, m_attn_w_in, m_attn_w_out, m_relpos_table, m_q_norm_a, m_k_norm_a, m_q_norm_b, m_k_norm_b, m_sinks, m_ssm_w_in, m_ssm_conv_w, m_ssm_conv_b, m_ssm_dt_bias, m_ssm_a_log, m_ssm_d, m_ssm_norm, m_ssm_w_out, m_ffn_w_in, m_ffn_conv_w, m_ffn_conv_b, m_ffn_w_out, v_norm_mix, v_norm_ffn, v_attn_w_in, v_attn_w_out, v_relpos_table, v_q_norm_a, v_k_norm_a, v_q_norm_b, v_k_norm_b, v_sinks, v_ssm_w_in, v_ssm_conv_w, v_ssm_conv_b, v_ssm_dt_bias, v_ssm_a_log, v_ssm_d, v_ssm_norm, v_ssm_w_out, v_ffn_w_in, v_ffn_conv_w, v_ffn_conv_b, v_ffn_w_out):
    w = dict(norm_mix=norm_mix, norm_ffn=norm_ffn, attn_w_in=attn_w_in, attn_w_out=attn_w_out, relpos_table=relpos_table,
             q_norm_a=q_norm_a, k_norm_a=k_norm_a, q_norm_b=q_norm_b, k_norm_b=k_norm_b, sinks=sinks, ssm_w_in=ssm_w_in,
             ssm_conv_w=ssm_conv_w, ssm_conv_b=ssm_conv_b, ssm_dt_bias=ssm_dt_bias, ssm_a_log=ssm_a_log, ssm_d=ssm_d,
             ssm_norm=ssm_norm, ssm_w_out=ssm_w_out, ffn_w_in=ffn_w_in, ffn_conv_w=ffn_conv_w, ffn_conv_b=ffn_conv_b,
             ffn_w_out=ffn_w_out)
    mom = dict(norm_mix=m_norm_mix, norm_ffn=m_norm_ffn, attn_w_in=m_attn_w_in, attn_w_out=m_attn_w_out,
               relpos_table=m_relpos_table, q_norm_a=m_q_norm_a, k_norm_a=m_k_norm_a, q_norm_b=m_q_norm_b,
               k_norm_b=m_k_norm_b, sinks=m_sinks, ssm_w_in=m_ssm_w_in, ssm_conv_w=m_ssm_conv_w, ssm_conv_b=m_ssm_conv_b,
               ssm_dt_bias=m_ssm_dt_bias, ssm_a_log=m_ssm_a_log, ssm_d=m_ssm_d, ssm_norm=m_ssm_norm, ssm_w_out=m_ssm_w_out,
               ffn_w_in=m_ffn_w_in, ffn_conv_w=m_ffn_conv_w, ffn_conv_b=m_ffn_conv_b, ffn_w_out=m_ffn_w_out)
    var = dict(norm_mix=v_norm_mix, norm_ffn=v_norm_ffn, attn_w_in=v_attn_w_in, attn_w_out=v_attn_w_out,
               relpos_table=v_relpos_table, q_norm_a=v_q_norm_a, k_norm_a=v_k_norm_a, q_norm_b=v_q_norm_b,
               k_norm_b=v_k_norm_b, sinks=v_sinks, ssm_w_in=v_ssm_w_in, ssm_conv_w=v_ssm_conv_w, ssm_conv_b=v_ssm_conv_b,
               ssm_dt_bias=v_ssm_dt_bias, ssm_a_log=v_ssm_a_log, ssm_d=v_ssm_d, ssm_norm=v_ssm_norm, ssm_w_out=v_ssm_w_out,
               ffn_w_in=v_ffn_w_in, ffn_conv_w=v_ffn_conv_w, ffn_conv_b=v_ffn_conv_b, ffn_w_out=v_ffn_w_out)

    def piece(n, l):
        return (w[n][l].T if _AX2[n] == 1 else w[n][l]).astype(BF16)

    def gather_of(names_layers):
        return _Gather2([piece(n, l) for n, l in names_layers])

    def joined(got):
        return [g.reshape(-1, D_MODEL) for g in got]

    first = [('attn_w_in', 0), ('attn_w_out', 0)]
    got = _exchange(_Gather2([piece(n, l) for n, l in first] + [_pack([w[n] for n in _SMALL], F32)]), "gather_attn")
    wt_attn_in, w_attn_out = joined(got[:2])
    full = {}
    for n, g in zip(_SMALL, _unpack(got[2], [w[n].shape for n in _SMALL], lead=(N_DEV,))):
        full[n] = _join(g, _SHARD_AX[n])
    ssm_cw8 = _rows8(full['ssm_conv_w'][0])
    ssm_cb = full['ssm_conv_b']
    ssm_nw = full['ssm_norm']
    ffn_cw8 = [_rows8(full['ffn_conv_w'][l]) for l in range(2)]
    ffn_cb = [ffn_conv_b[l:l + 1] for l in range(2)]

    x0 = x[0]
    target = loss_target[0]
    t = x0.shape[0]

    g_mix0, g_mix1 = norm_mix[0:1], norm_mix[1:2]
    g_ffn0, g_ffn1 = norm_ffn[0:1], norm_ffn[1:2]
    proj, h0 = _rms_mm(x0, g_mix0, wt_attn_in, ATTN_PROJ, "mm_attn_in", F32)
    hn_w = jnp.concatenate([jnp.tile(v, (1, 2)) for v in (q_norm_a, k_norm_a, q_norm_b, k_norm_b)], axis=0)
    qa, kpa, vpa, qb, kpb, vpb = _headnorm_fwd(proj, hn_w, "headnorm")
    table = jnp.pad(relpos_table[0], ((0, 0), (0, REL_W - (2 * MAX_REL + 1))))
    bias_a = jnp.where(_band_mask(A_PREV, PAD_A)[None], jnp.transpose(_relpos_fwd(table, "relpos_bias"), (1, 0, 2)), NEG)
    rel_b = jnp.arange(TQ)[:, None] - (jnp.arange(PAD_B + TQ)[None, :] - PAD_B)
    slopes = 2.0 ** (-8.0 * jnp.arange(1, N_HEADS + 1, dtype=F32) / N_HEADS)
    bias_b = jnp.where(_band_mask(B_PREV, PAD_B)[None], -slopes[:, None, None] * jnp.abs(rel_b).astype(F32)[None], NEG)
    no_sinks = jnp.full((N_HEADS,), NEG, F32)
    ffn0_w, ssm_w, ffn1_w = [('ffn_w_in', 0), ('ffn_w_out', 0)], [('ssm_w_in', 0), ('ssm_w_out', 0)], [('ffn_w_in', 1), ('ffn_w_out', 1)]
    oa, stats_a, got = _attn_fwd(qa, kpa, vpa, bias_a, no_sinks, PAD_A, "attn_a", comm=gather_of(ffn0_w + ssm_w))
    wt_ffn_in0, w_ffn_out0, wt_ssm_in, w_ssm_out = joined(got)
    ob, stats_b, _ = _attn_fwd(qb, kpb, vpb, bias_b, sinks[0], PAD_B, "attn_b")
    wt_ssm_dt = jnp.pad(wt_ssm_in[ZX:], ((0, LANES - SSM_HEADS), (0, 0)))
    x1 = _mm(oa, w_attn_out, "mm_attn_out_a", res=x0, b_rows=(0, D_ATT))
    x1 = _mm(ob, w_attn_out, "mm_attn_out_b", res=x1, b_rows=(D_ATT, D_ATT))
    a0, ffn0_saved = _ffn_fwd(x1, g_ffn0, wt_ffn_in0, ffn_cw8[0], ffn_cb[0], "0")
    x2 = _mm(a0, w_ffn_out0, "mm_ffn_out0", res=x1)

    zx, h2, xbc, conv_pre = _ssm_in_pre(x2, g_mix1, wt_ssm_in, ssm_cw8, ssm_cb, "mm_ssm_in")
    dtraw = _mm(h2, wt_ssm_dt, "mm_ssm_dt", trans_b=True)
    dt_bias = _lanes128(ssm_dt_bias[0])
    alog = _lanes128(ssm_a_log[0])
    dexp = jnp.repeat(ssm_d[0], HEAD_DIM).reshape(1, D_INNER)
    dt = _dt_fwd(dtraw, dt_bias, "ssm_dt")
    (y, sprev, y4), got = _ssd_fwd(xbc, dt, alog, zx, dexp, ssm_nw, "ssd_fwd", comm=gather_of(ffn1_w))
    wt_ffn_in1, w_ffn_out1 = joined(got)
    x3 = _mm(y4, w_ssm_out, "mm_ssm_out", res=x2)
    a1, ffn1_saved = _ffn_fwd(x3, g_ffn1, wt_ffn_in1, ffn_cw8[1], ffn_cb[1], "1")

    dx4, dx4b, sq = _mm_loss(a1, w_ffn_out1, x3, target, "mm_ffn_out1_loss")
    loss = lax.psum(0.5 * jnp.sum(sq) / D_MODEL, ("x", "y", "c"))

    grads = {}

    def scatter_of(grads_2d):
        return _Comm([g.reshape(N_DEV, -1, D_MODEL) for g in grads_2d], [True] * len(grads_2d))

    dx3, dx3b, dg_ffn1, dwtin1, dcw1, dcb1, dwout1 = _ffn_bwd(
        dx4, dx4b, x3, g_ffn1, wt_ffn_in1, ffn_cw8[1], w_ffn_out1, ffn1_saved, "1")

    dy4 = _mm(dx3b, w_ssm_out, "mm_ssm_dy", trans_b=True)
    dw_ssm_out = _mm_tn(y4, dx3b, "mm_ssm_dwout")
    (dxbc, ddt, dalog, dz, dd_lane, dnw), parts_ffn1 = _ssd_bwd(
        xbc, dt, alog, sprev, dy4, y, zx, dexp, ssm_nw, "ssd_bwd", comm=scatter_of([dwtin1, dwout1]))
    dxr, dcw_s, dcb_s = _ssm_pre_bwd(zx, conv_pre, dxbc, ssm_cw8, "ssm_pre_bwd")
    ddtraw, ddtb = _dt_bwd(dtraw, dt_bias, ddt, "ssm_dt_bwd")
    dh2 = _mm(dz, wt_ssm_in, "mm_ssm_dh_z", b_rows=(0, D_INNER))
    dh2 = _mm(dxr, wt_ssm_in[D_INNER:ZX], "mm_ssm_dh_x", res=dh2)
    dwt_ssm_in = jnp.concatenate([
        _mm_tn(dz, h2, "mm_ssm_dwin_z"), _mm_tn(dxr, h2, "mm_ssm_dwin_x"),
        _mm_tn(ddtraw, h2, "mm_ssm_dwin_dt")[:SSM_HEADS]], axis=0)
    dx2, dx2b, dg_mix1 = _mm_rms_bwd(ddtraw, wt_ssm_dt, 0, dh2, x2, g_mix1, dx3, "mm_ssm_dh_dt")
    grads['ssm_conv_w'] = dcw_s[:4][None]
    grads['ssm_conv_b'] = dcb_s
    grads['ssm_norm'] = dnw
    grads['ssm_dt_bias'] = ddtb[:, :SSM_HEADS]
    grads['ssm_a_log'] = dalog[:, :SSM_HEADS]
    grads['ssm_d'] = jnp.sum(dd_lane.reshape(SSM_HEADS, HEAD_DIM), axis=1)[None]

    dx1, dx1b, dg_ffn0, dwtin0, dcw0, dcb0, dwout0 = _ffn_bwd(
        dx2, dx2b, x1, g_ffn0, wt_ffn_in0, ffn_cw8[0], w_ffn_out0, ffn0_saved, "0")
    grads['ffn_conv_w'] = jnp.stack([dcw0, dcw1])
    grads['ffn_conv_b'] = jnp.concatenate([dcb0, dcb1], axis=0)
    grads['norm_ffn'] = jnp.concatenate([dg_ffn0, dg_ffn1], axis=0)

    do = _mm(dx1b, w_attn_out, "mm_attn_do", out_dtype=BF16, trans_b=True)
    dw_attn_out = jnp.concatenate([_mm_tn(oa, dx1b, "mm_attn_dwout_a"), _mm_tn(ob, dx1b, "mm_attn_dwout_b")], axis=0)
    (dqa, dkpa, dvpa, dbias_a, _), parts_ssm = _attn_bwd(
        qa, kpa, vpa, bias_a, no_sinks, do, stats_a, oa, 0, PAD_A, "attn_a_bwd",
        comm=scatter_of([dwt_ssm_in, dw_ssm_out, dw_attn_out]))
    (dqb, dkpb, dvpb, _, dsink), parts_ffn0 = _attn_bwd(
        qb, kpb, vpb, bias_b, sinks[0], do, stats_b, ob, 4, PAD_B, "attn_b_bwd", comm=scatter_of([dwtin0, dwout0]))
    grads['relpos_table'] = _relpos_bwd(jnp.transpose(dbias_a, (1, 0, 2)), "relpos_bwd")[None, :, :2 * MAX_REL + 1]
    grads['sinks'] = dsink[:, :2, 0].reshape(1, N_HEADS)
    dproj, dhn = _headnorm_bwd(proj, hn_w, dqa, dkpa, dvpa, dqb, dkpb, dvpb, "headnorm_bwd")
    dhn = dhn[:, :HEAD_DIM] + dhn[:, HEAD_DIM:]
    for k, n in enumerate(('q_norm_a', 'k_norm_a', 'q_norm_b', 'k_norm_b')):
        grads[n] = dhn[k:k + 1]
    dwt_attn_in = _mm_tn(dproj, h0, "mm_attn_dwin")
    dx0, _, dg_mix0, parts_attn_in = _mm_rms_bwd(dproj, wt_attn_in, 0, None, x0, g_mix0, dx1, "mm_attn_dh",
                                                 comm=scatter_of([dwt_attn_in]))
    grads['norm_mix'] = jnp.concatenate([dg_mix0, dg_mix1], axis=0)

    def summed_t(parts, name):
        return _sum_parts(parts, name).T[None]

    sm_shapes = [w[n].shape for n in _SMALL]
    rp_shapes = [w[n].shape for n in _REPL]
    recv = _exchange(_Comm(
        [_pack([_split(grads[n], _SHARD_AX[n]) for n in _SMALL], F32, lead=(N_DEV,)), _pack([grads[n] for n in _REPL], F32)],
        [True, False]), "exchange_small")
    big_parts = {
        'attn_w_in': [summed_t(parts_attn_in[0], "sum_attn_w_in")], 'attn_w_out': [parts_ssm[2]],
        'ssm_w_in': [summed_t(parts_ssm[0], "sum_ssm_w_in")], 'ssm_w_out': [parts_ssm[1]],
        'ffn_w_in': [summed_t(parts_ffn0[0], "sum_ffn_w_in0"), summed_t(parts_ffn1[0], "sum_ffn_w_in1")],
        'ffn_w_out': [parts_ffn0[1], parts_ffn1[1]],
    }
    res = [{}, {}, {}, {}]
    for n in _BIG:
        for kind, a in enumerate(_adamw(big_parts[n], w[n], mom[n], var[n], f"adamw_{n}")):
            res[kind][n] = a
    for names, shapes, parts in ((_SMALL, sm_shapes, recv[0]), (_REPL, rp_shapes, recv[1])):
        outs = _adamw([parts], _pack([w[n] for n in names], F32)[None], _pack([mom[n] for n in names], F32)[None],
                      _pack([var[n] for n in names], F32)[None], "adamw_" + ("small" if names is _SMALL else "replicated"))
        for kind, flat in enumerate(outs):
            for n, a in zip(names, _unpack(flat[0], shapes)):
                res[kind][n] = a
    return (loss, dx0[None], *[res[0][n] for n in _WEIGHTS], *[res[1][n] for n in _WEIGHTS],
            *[res[2][n] for n in _WEIGHTS], *[res[3][n] for n in _WEIGHTS])
```

```python
import jax
import jax.numpy as jnp
from jax import lax
from jax.experimental import pallas as pl
from jax.experimental.pallas import tpu as pltpu

F32 = jnp.float32
BF16 = jnp.bfloat16
HI = lax.Precision.HIGHEST
MESH = pl.DeviceIdType.MESH
NEG = -1e30

N_DEV = 8
D_MODEL = 1024
EPS = 1e-6
CHUNK = 64
HEAD_DIM = 64
N_HEADS = 8
A_PREV = 8
B_PREV = 2
MAX_REL = 256
TQ = 2 * CHUNK
ATT_SUB = 16
PAD_A = A_PREV * CHUNK
PAD_B = B_PREV * CHUNK
REL_W = PAD_A + TQ
D_ATT = N_HEADS * HEAD_DIM
COL_QA, COL_KA, COL_VA, COL_QB = 0, D_ATT, 2 * D_ATT, 3 * D_ATT
COL_KB, COL_VB = 4 * D_ATT, 4 * D_ATT + 2 * HEAD_DIM
ATTN_PROJ = COL_VB + 2 * HEAD_DIM
D_INNER = 2048
SSM_HEADS = 32
SSM_GROUPS = 4
SSM_STATE = 128
XBC = D_INNER + 2 * SSM_GROUPS * SSM_STATE
ZX = D_INNER + XBC
D_FF = 2816
SSD_L = 128
SSD_L_BWD = 2 * SSD_L
LANES = 128
VMEM_LIMIT = 56 << 20

ADAM_LR, ADAM_B1, ADAM_B2, ADAM_EPS, ADAM_WD, ADAM_STEP = 0.001, 0.9, 0.999, 1e-08, 0.01, 10


def _cp(sem=None):
    return pltpu.CompilerParams(dimension_semantics=sem, vmem_limit_bytes=VMEM_LIMIT)


def _dot(a, b, ca=1, cb=0, prec=None):
    return lax.dot_general(a, b, (((ca,), (cb,)), ((), ())), preferred_element_type=F32, precision=prec)


def _pick(n, cands):
    for c in cands:
        if n % c == 0:
            return c
    return n


def _lo_mask():
    return lax.broadcasted_iota(jnp.int32, (1, LANES), 1) < HEAD_DIM


_TN_CHUNKS = (1408, 1536, 1152, 1024, 512, 256, 128)


TN_MAX_ROWS = 3072
MM_WIDE = 2304


def _mm_tn(a, b, name):
    kdim, m = a.shape
    n = b.shape[1]
    assert b.shape[0] == kdim, (a.shape, b.shape)
    mb = m if m <= TN_MAX_ROWS else m // 2
    tn = _pick(n, _TN_CHUNKS)
    tk = _pick(kdim, (512, 256, 128))
    nk = kdim // tk

    def body(a_ref, b_ref, o_ref, acc):
        k = pl.program_id(1)

        @pl.when(k == 0)
        def _():
            acc[...] = jnp.zeros_like(acc)

        av = a_ref[...]
        for c in range(0, n, tn):
            acc[:, c:c + tn] += _dot(av, b_ref[:, c:c + tn], 0, 0)

        @pl.when(k == nk - 1)
        def _():
            o_ref[...] = acc[...].astype(BF16)

    return pl.pallas_call(
        body, name=name, grid=(m // mb, nk),
        in_specs=[pl.BlockSpec((tk, mb), lambda j, k: (k, j)), pl.BlockSpec((tk, n), lambda j, k: (k, 0))],
        out_specs=pl.BlockSpec((mb, n), lambda j, k: (j, 0)), out_shape=jax.ShapeDtypeStruct((m, n), BF16),
        scratch_shapes=[pltpu.VMEM((mb, n), F32)], compiler_params=_cp(("parallel", "arbitrary")),
    )(a, b)


def _mm(a, b, name, out_dtype=F32, res=None, trans_b=False):
    m, kdim = a.shape
    n = b.shape[0] if trans_b else b.shape[1]
    assert (b.shape[1] if trans_b else b.shape[0]) == kdim, (a.shape, b.shape)
    tn = _pick(n, _TN_CHUNKS)
    tm = _pick(m, (256, 128) if n > MM_WIDE else (512, 256, 128))

    def body(*refs):
        if res is None:
            a_ref, b_ref, o_ref = refs
        else:
            a_ref, b_ref, r_ref, o_ref = refs
        av = a_ref[...]
        for c in range(0, n, tn):
            r = _dot(av, b_ref[c:c + tn, :], 1, 1) if trans_b else _dot(av, b_ref[:, c:c + tn], 1, 0)
            if res is not None:
                r = r + r_ref[:, c:c + tn]
            o_ref[:, c:c + tn] = r.astype(out_dtype)

    in_specs = [pl.BlockSpec((tm, kdim), lambda i: (i, 0)), pl.BlockSpec(b.shape, lambda i: (0, 0))]
    args = [a, b]
    if res is not None:
        in_specs.append(pl.BlockSpec((tm, n), lambda i: (i, 0)))
        args.append(res)
    return pl.pallas_call(
        body, name=name, grid=(m // tm,), in_specs=in_specs, out_specs=pl.BlockSpec((tm, n), lambda i: (i, 0)),
        out_shape=jax.ShapeDtypeStruct((m, n), out_dtype), compiler_params=_cp(("parallel",)),
    )(*args)


def _mm_pair(a1, a2, b, res, name):
    m, k1 = a1.shape
    k2 = a2.shape[1]
    n = b.shape[1]
    assert b.shape[0] == k1 + k2
    tm = _pick(m, (512, 256, 128))

    def body(a1_ref, a2_ref, b_ref, r_ref, o_ref):
        o_ref[...] = _dot(a1_ref[...], b_ref[:k1, :], 1, 0) + _dot(a2_ref[...], b_ref[k1:, :], 1, 0) + r_ref[...]

    row = pl.BlockSpec((tm, n), lambda i: (i, 0))
    return pl.pallas_call(
        body, name=name, grid=(m // tm,),
        in_specs=[pl.BlockSpec((tm, k1), lambda i: (i, 0)), pl.BlockSpec((tm, k2), lambda i: (i, 0)),
                  pl.BlockSpec(b.shape, lambda i: (0, 0)), row],
        out_specs=row, out_shape=jax.ShapeDtypeStruct((m, n), F32), compiler_params=_cp(("parallel",)),
    )(a1, a2, b, res)


def _rms_mm(x, g, bt, n, name, out_dtype):
    t, d = x.shape
    tn = _pick(n, _TN_CHUNKS)
    tm = _pick(t, (256, 128))

    def body(x_ref, g_ref, b_ref, o_ref, h_ref):
        xv = x_ref[...]
        r = lax.rsqrt(jnp.mean(xv * xv, axis=-1, keepdims=True) + EPS)
        h = (xv * r * g_ref[...]).astype(BF16)
        h_ref[...] = h
        for c in range(0, n, tn):
            o_ref[:, c:c + tn] = _dot(h, b_ref[c:c + tn, :], 1, 1).astype(out_dtype)

    row = pl.BlockSpec((tm, d), lambda i: (i, 0))
    return pl.pallas_call(
        body, name=name, grid=(t // tm,),
        in_specs=[row, pl.BlockSpec((1, d), lambda i: (0, 0)), pl.BlockSpec(bt.shape, lambda i: (0, 0))],
        out_specs=[pl.BlockSpec((tm, n), lambda i: (i, 0)), row],
        out_shape=[jax.ShapeDtypeStruct((t, n), out_dtype), jax.ShapeDtypeStruct((t, d), BF16)],
        compiler_params=_cp(("parallel",)),
    )(x, g, bt)


def _mm_rms_bwd(terms, x, g, dres, name, comm=None):
    t, d = x.shape
    tm = _pick(t, (256, 128))
    weights = []
    for _, b, _ in terms:
        if not any(b is wgt for wgt in weights):
            weights.append(b)
    which = [next(k for k, wgt in enumerate(weights) if wgt is b) for _, b, _ in terms]
    na, nw = len(terms), len(weights)

    def body(*refs):
        a_refs, w_refs = refs[:na], refs[na:na + nw]
        x_ref, g_ref, dr_ref, dx_ref, dxb_ref, dg_ref = refs[na + nw:]
        dhv = None
        for (a, _, row), a_ref, k in zip(terms, a_refs, which):
            part = _dot(a_ref[...], w_refs[k][row:row + a.shape[1], :], 1, 0)
            dhv = part if dhv is None else dhv + part
        xv = x_ref[...]
        r = lax.rsqrt(jnp.mean(xv * xv, axis=-1, keepdims=True) + EPS)
        xh = xv * r
        dxh = dhv * g_ref[...]
        dx = dr_ref[...] + r * (dxh - xh * jnp.mean(dxh * xh, axis=-1, keepdims=True))
        dx_ref[...] = dx
        dxb_ref[...] = dx.astype(BF16)

        @pl.when(pl.program_id(0) == 0)
        def _():
            dg_ref[...] = jnp.zeros_like(dg_ref)

        dg_ref[...] += jnp.sum(dhv * xh, axis=0, keepdims=True)

    row = pl.BlockSpec((tm, d), lambda i: (i, 0))
    vec = pl.BlockSpec((1, d), lambda i: (0, 0))
    in_specs = ([pl.BlockSpec((tm, a.shape[1]), lambda i: (i, 0)) for a, _, _ in terms]
                + [pl.BlockSpec(wgt.shape, lambda i: (0, 0)) for wgt in weights])
    outs, got = _call(
        body, name=name, grid=(t // tm,), in_specs=in_specs + [row, vec, row], out_specs=[row, row, vec],
        out_shape=[jax.ShapeDtypeStruct((t, d), F32), jax.ShapeDtypeStruct((t, d), BF16), jax.ShapeDtypeStruct((1, d), F32)],
        args=(*[a for a, _, _ in terms], *weights, x, g, dres), sem=("arbitrary",), comm=comm)
    return (*outs, got) if comm is not None else tuple(outs)


def _mm_loss(a, b, res, target, name):
    t, kdim = a.shape
    d = b.shape[1]
    tm = _pick(t, (512, 256, 128))

    def body(a_ref, b_ref, r_ref, t_ref, dy_ref, dyb_ref, acc_ref):
        @pl.when(pl.program_id(0) == 0)
        def _():
            acc_ref[...] = jnp.zeros_like(acc_ref)

        err = _dot(a_ref[...], b_ref[...], 1, 0) + r_ref[...] - t_ref[...]
        dy = err * (1.0 / d)
        dy_ref[...] = dy
        dyb_ref[...] = dy.astype(BF16)
        acc_ref[...] += jnp.sum(err * err, axis=0, keepdims=True)

    row = pl.BlockSpec((tm, d), lambda i: (i, 0))
    vec = pl.BlockSpec((1, d), lambda i: (0, 0))
    return pl.pallas_call(
        body, name=name, grid=(t // tm,),
        in_specs=[pl.BlockSpec((tm, kdim), lambda i: (i, 0)), pl.BlockSpec((kdim, d), lambda i: (0, 0)), row, row],
        out_specs=[row, row, vec],
        out_shape=[jax.ShapeDtypeStruct((t, d), F32), jax.ShapeDtypeStruct((t, d), BF16), jax.ShapeDtypeStruct((1, d), F32)],
        compiler_params=_cp(("arbitrary",)),
    )(a, b, res, target)


def _head_sums(v):
    ri = lax.broadcasted_iota(jnp.int32, (LANES, LANES), 0) // HEAD_DIM
    ci = lax.broadcasted_iota(jnp.int32, (LANES, LANES), 1) // HEAD_DIM
    ones = (ri == ci).astype(BF16)
    hi = v.astype(BF16)
    lo_part = (v - hi.astype(F32)).astype(BF16)
    return _dot(hi, ones, 1, 0) + _dot(lo_part, ones, 1, 0)


def _head_rms(xs):
    r = lax.rsqrt(_head_sums(xs * xs) * (1.0 / HEAD_DIM) + EPS)
    return xs * r, r


def _head_rms_bwd(xs, w, dy):
    xh, r = _head_rms(xs)
    dxh = dy * w
    mm = _head_sums(dxh * xh) * (1.0 / HEAD_DIM)
    return r * (dxh - xh * mm), dy * xh


_QSCALE = HEAD_DIM ** -0.5


def _headnorm_fwd(proj, ws, name):
    t = proj.shape[0]
    tm = TQ
    lead = PAD_A // tm
    leadb = PAD_B // tm

    def body(p_ref, w_ref, qa_ref, ka_ref, va_ref, qb_ref, kb_ref, vb_ref):
        data = pl.program_id(0) >= lead
        lo = _lo_mask()

        def put(ref, c, val):
            ref[:, c:c + val.shape[1]] = jnp.where(data, val, 0.0).astype(BF16)

        def per_query_head(slab):
            other = pltpu.roll(slab, HEAD_DIM, 1)
            e0, e1 = jnp.where(lo, slab, other), jnp.where(lo, other, slab)
            return jnp.concatenate([e0, e0, e1, e1], axis=1)

        for s in range(D_ATT // LANES):
            c = LANES * s
            xh, _ = _head_rms(p_ref[:, COL_QA + c:COL_QA + c + LANES])
            qa_ref[:, c:c + LANES] = (xh * w_ref[0:1, :] * _QSCALE).astype(BF16)
            xh, _ = _head_rms(p_ref[:, COL_KA + c:COL_KA + c + LANES])
            put(ka_ref, c, xh * w_ref[1:2, :])
            xh, _ = _head_rms(p_ref[:, COL_QB + c:COL_QB + c + LANES])
            qb_ref[:, c:c + LANES] = (xh * w_ref[2:3, :] * _QSCALE).astype(BF16)
        put(va_ref, 0, p_ref[:, COL_VA:COL_VA + D_ATT])
        xh, _ = _head_rms(p_ref[:, COL_KB:COL_KB + LANES])
        put(kb_ref, 0, per_query_head(xh * w_ref[3:4, :]))
        put(vb_ref, 0, per_query_head(p_ref[:, COL_VB:COL_VB + LANES]))

    src = lambda i: jnp.maximum(i - lead, 0)
    wide = pl.BlockSpec((tm, D_ATT), lambda i: (src(i), 0))
    pad_a = pl.BlockSpec((tm, D_ATT), lambda i: (i, 0))
    pad_b = pl.BlockSpec((tm, D_ATT), lambda i: (jnp.maximum(i - lead + leadb, 0), 0))
    sd = lambda rows: jax.ShapeDtypeStruct((rows, D_ATT), BF16)
    return pl.pallas_call(
        body, name=name, grid=(t // tm + lead,),
        in_specs=[pl.BlockSpec((tm, ATTN_PROJ), lambda i: (src(i), 0)), pl.BlockSpec((4, LANES), lambda i: (0, 0))],
        out_specs=[wide, pad_a, pad_a, wide, pad_b, pad_b],
        out_shape=[sd(t), sd(t + PAD_A), sd(t + PAD_A), sd(t), sd(t + PAD_B), sd(t + PAD_B)],
        compiler_params=_cp(("arbitrary",)),
    )(proj, ws)


def _headnorm_bwd(proj, ws, dqa, dkpa, dvpa, dqb, dkpb, dvpb, name):
    t = proj.shape[0]
    tm = TQ
    offa, offb = PAD_A // tm, PAD_B // tm

    def body(p_ref, w_ref, dqa_ref, dka_ref, dva_ref, dqb_ref, dkb_ref, dvb_ref, dp_ref, dw_ref):
        i = pl.program_id(0)
        lo = _lo_mask()

        @pl.when(i == 0)
        def _():
            dw_ref[...] = jnp.zeros_like(dw_ref)

        acc = [jnp.zeros((1, LANES), F32) for _ in range(4)]
        for s in range(D_ATT // LANES):
            c = LANES * s
            dx, dwl = _head_rms_bwd(p_ref[:, COL_QA + c:COL_QA + c + LANES], w_ref[0:1, :], dqa_ref[:, c:c + LANES] * _QSCALE)
            dp_ref[:, COL_QA + c:COL_QA + c + LANES] = dx.astype(BF16)
            acc[0] += jnp.sum(dwl, axis=0, keepdims=True)
            dx, dwl = _head_rms_bwd(p_ref[:, COL_KA + c:COL_KA + c + LANES], w_ref[1:2, :], dka_ref[:, c:c + LANES])
            dp_ref[:, COL_KA + c:COL_KA + c + LANES] = dx.astype(BF16)
            acc[1] += jnp.sum(dwl, axis=0, keepdims=True)
            dx, dwl = _head_rms_bwd(p_ref[:, COL_QB + c:COL_QB + c + LANES], w_ref[2:3, :], dqb_ref[:, c:c + LANES] * _QSCALE)
            dp_ref[:, COL_QB + c:COL_QB + c + LANES] = dx.astype(BF16)
            acc[2] += jnp.sum(dwl, axis=0, keepdims=True)
        dp_ref[:, COL_VA:COL_VA + D_ATT] = dva_ref[...].astype(BF16)

        def group_sum(ref):
            s0 = ref[:, 0:LANES] + ref[:, LANES:2 * LANES]
            s1 = ref[:, 2 * LANES:3 * LANES] + ref[:, 3 * LANES:4 * LANES]
            s0 = s0 + pltpu.roll(s0, HEAD_DIM, 1)
            s1 = s1 + pltpu.roll(s1, HEAD_DIM, 1)
            return jnp.where(lo, s0, s1)

        dx, dwl = _head_rms_bwd(p_ref[:, COL_KB:COL_KB + LANES], w_ref[3:4, :], group_sum(dkb_ref))
        dp_ref[:, COL_KB:COL_KB + LANES] = dx.astype(BF16)
        acc[3] += jnp.sum(dwl, axis=0, keepdims=True)
        dp_ref[:, COL_VB:COL_VB + LANES] = group_sum(dvb_ref).astype(BF16)
        for n in range(4):
            dw_ref[n:n + 1, :] += acc[n]

    wide = pl.BlockSpec((tm, D_ATT), lambda i: (i, 0))
    pa = pl.BlockSpec((tm, D_ATT), lambda i: (i + offa, 0))
    pb = pl.BlockSpec((tm, D_ATT), lambda i: (i + offb, 0))
    whole = pl.BlockSpec((tm, ATTN_PROJ), lambda i: (i, 0))
    return pl.pallas_call(
        body, name=name, grid=(t // tm,),
        in_specs=[whole, pl.BlockSpec((4, LANES), lambda i: (0, 0)), wide, pa, pa, wide, pb, pb],
        out_specs=[whole, pl.BlockSpec((4, LANES), lambda i: (0, 0))],
        out_shape=[jax.ShapeDtypeStruct((t, ATTN_PROJ), BF16), jax.ShapeDtypeStruct((4, LANES), F32)],
        compiler_params=_cp(("arbitrary",)),
    )(proj, ws, dqa, dkpa, dvpa, dqb, dkpb, dvpb)


ROLL_W = 1024


def _rel_onehot():
    r_io = lax.broadcasted_iota(jnp.int32, (REL_W, ROLL_W), 0)
    m_io = lax.broadcasted_iota(jnp.int32, (REL_W, ROLL_W), 1)
    return (r_io == jnp.clip(REL_W - 1 - m_io, -MAX_REL, MAX_REL) + MAX_REL).astype(F32)


def _relpos_fwd(table, name):
    def body(t_ref, o_ref):
        rr = _dot(t_ref[...], _rel_onehot(), 1, 0, HI)

        def step(q, c):
            o_ref[q] = pltpu.roll(rr, (ROLL_W - (TQ - 1) + q) % ROLL_W, 1)[:, :REL_W]
            return c

        lax.fori_loop(0, TQ, step, 0)

    return pl.pallas_call(
        body, name=name, out_shape=jax.ShapeDtypeStruct((TQ, N_HEADS, REL_W), F32),
        in_specs=[pl.BlockSpec(memory_space=pltpu.VMEM)], out_specs=pl.BlockSpec(memory_space=pltpu.VMEM),
        compiler_params=_cp(),
    )(table)


def _relpos_bwd(dbias_t, name):
    def body(d_ref, o_ref):
        def step(q, acc):
            row = jnp.concatenate([d_ref[q], jnp.zeros((N_HEADS, ROLL_W - REL_W), F32)], axis=1)
            return acc + pltpu.roll(row, TQ - 1 - q, 1)

        drr = lax.fori_loop(0, TQ, step, jnp.zeros((N_HEADS, ROLL_W), F32))
        o_ref[...] = _dot(drr, _rel_onehot(), 1, 1, HI)

    return pl.pallas_call(
        body, name=name, out_shape=jax.ShapeDtypeStruct((N_HEADS, REL_W), F32),
        in_specs=[pl.BlockSpec(memory_space=pltpu.VMEM)], out_specs=pl.BlockSpec(memory_space=pltpu.VMEM),
        compiler_params=_cp(),
    )(dbias_t)


def _attn_scores(qe, kw, bias, kvalid):
    return jnp.where(kvalid, _dot(qe, kw, 1, 1) + bias, NEG)


def _stat_cols(stats, e):
    return stats[:, 64 * e:64 * e + 1], stats[:, 64 * e + 32:64 * e + 33]


def _attn_fwd(q, kp, vp, bias, sinks, pad, name, comm=None):
    t, hd = q.shape
    w = pad + TQ

    def body(sink_ref, q_ref, k_ref, v_ref, b_ref, o_ref, st_ref):
        hp, i = pl.program_id(0), pl.program_id(1)
        lo = _lo_mask()
        lane = lax.broadcasted_iota(jnp.int32, (1, LANES), 1)
        for j in range(ATT_SUB):
            start = pl.multiple_of((i * ATT_SUB + j) * TQ, TQ)
            qv = q_ref[TQ * j:TQ * (j + 1), :]
            kw = k_ref[pl.ds(start, w), :]
            vw = v_ref[pl.ds(start, w), :]
            kvalid = (start + lax.broadcasted_iota(jnp.int32, (1, w), 1)) >= pad
            outs, ms, ls = [], [], []
            for e in range(2):
                sel = lo if e == 0 else jnp.logical_not(lo)
                qe = jnp.where(sel, qv, jnp.zeros_like(qv))
                snk = sink_ref[2 * hp + e]
                s = _attn_scores(qe, kw, b_ref[e], kvalid)
                m = jnp.maximum(jnp.max(s, axis=-1, keepdims=True), snk)
                acc = _dot(jnp.exp(s - m).astype(BF16), jnp.where(sel, vw, jnp.ones_like(vw)), 1, 0)
                denom = acc[:, 64 * (1 - e):64 * (1 - e) + 1] + jnp.exp(snk - m)
                outs.append(acc * (1.0 / denom))
                ms.append(m)
                ls.append(denom)
            o_ref[TQ * j:TQ * (j + 1), :] = jnp.where(lo, outs[0], outs[1]).astype(BF16)
            st_ref[TQ * j:TQ * (j + 1), :] = jnp.where(lane < 32, ms[0], jnp.where(lane < 64, ls[0],
                                                                                 jnp.where(lane < 96, ms[1], ls[1])))

    full = pl.BlockSpec((t + pad, LANES), lambda h, i: (0, h))
    tile = pl.BlockSpec((ATT_SUB * TQ, LANES), lambda h, i: (i, h))
    (o, stats), got = _call(
        body, name=name, grid=(hd // LANES, t // (ATT_SUB * TQ)),
        in_specs=[pl.BlockSpec(memory_space=pltpu.SMEM), tile, full, full, pl.BlockSpec((2, TQ, w), lambda h, i: (h, 0, 0))],
        out_specs=[tile, tile], out_shape=[jax.ShapeDtypeStruct((t, hd), BF16), jax.ShapeDtypeStruct((t, hd), F32)],
        args=(sinks, q, kp, vp, bias), sem=("parallel", "arbitrary"), comm=comm)
    return o, stats, got


def _attn_bwd(q, kp, vp, bias, sinks, do, stats, o, col_off, pad, name, comm=None):
    t, hd = q.shape
    w = pad + TQ
    nhp = hd // LANES

    def body(sink_ref, q_ref, k_ref, v_ref, b_ref, do_ref, st_ref, o_ref, dq_ref, dk_ref, dv_ref, db_ref, ds_ref):
        hp, i = pl.program_id(0), pl.program_id(1)

        @pl.when(i == 0)
        def _():
            dk_ref[...] = jnp.zeros_like(dk_ref)
            dv_ref[...] = jnp.zeros_like(dv_ref)
            db_ref[...] = jnp.zeros_like(db_ref)
            ds_ref[...] = jnp.zeros_like(ds_ref)

        lo = _lo_mask()
        row8 = lax.broadcasted_iota(jnp.int32, (8, LANES), 0)
        dbias = [None, None]
        dsink = jnp.zeros((8, LANES), F32)
        for j in range(ATT_SUB):
            start = pl.multiple_of((i * ATT_SUB + j) * TQ, TQ)
            qv = q_ref[TQ * j:TQ * (j + 1), :]
            dov = do_ref[TQ * j:TQ * (j + 1), :]
            kw = k_ref[pl.ds(start, w), :]
            vw = v_ref[pl.ds(start, w), :]
            kvalid = (start + lax.broadcasted_iota(jnp.int32, (1, w), 1)) >= pad
            stats = st_ref[TQ * j:TQ * (j + 1), :]
            od = dov.astype(F32) * o_ref[TQ * j:TQ * (j + 1), :].astype(F32)
            dqs, dkw, dvw = [], None, None
            for e in range(2):
                sel = lo if e == 0 else jnp.logical_not(lo)
                qe = jnp.where(sel, qv, jnp.zeros_like(qv))
                doe = jnp.where(sel, dov, jnp.zeros_like(dov))
                m, denom = _stat_cols(stats, e)
                inv = 1.0 / denom
                p = jnp.exp(_attn_scores(qe, kw, b_ref[e], kvalid) - m) * inv
                psink = jnp.exp(sink_ref[2 * hp + e] - m) * inv
                dp = _dot(doe, vw, 1, 1)
                delta = jnp.sum(jnp.where(sel, od, 0.0), axis=-1, keepdims=True)
                ds = p * (dp - delta)
                dbias[e] = ds if dbias[e] is None else dbias[e] + ds
                dsink = dsink + jnp.where(row8 == e, jnp.sum(-psink * delta, axis=0, keepdims=True), 0.0)
                dsb = ds.astype(BF16)
                dqs.append(_dot(dsb, kw, 1, 0))
                dk_e = _dot(dsb, qe, 0, 0)
                dv_e = _dot(p.astype(BF16), doe, 0, 0)
                dkw = dk_e if dkw is None else dkw + dk_e
                dvw = dv_e if dvw is None else dvw + dv_e
            dq_ref[TQ * j:TQ * (j + 1), :] = jnp.where(lo, dqs[0], dqs[1])
            dk_ref[pl.ds(start, w), :] += dkw
            dv_ref[pl.ds(start, w), :] += dvw
        for e in range(2):
            db_ref[e] += dbias[e]
        ds_ref[0] += dsink

    full = pl.BlockSpec((t + pad, LANES), lambda h, i: (0, h))
    tile = pl.BlockSpec((ATT_SUB * TQ, LANES), lambda h, i: (i, h))
    btile = pl.BlockSpec((2, TQ, w), lambda h, i: (h, 0, 0))
    return _call(
        body, name=name, grid=(nhp, t // (ATT_SUB * TQ)),
        in_specs=[pl.BlockSpec(memory_space=pltpu.SMEM), tile, full, full, btile,
                  pl.BlockSpec((ATT_SUB * TQ, LANES), lambda h, i: (i, h + col_off)), tile, tile],
        out_specs=[tile, full, full, btile, pl.BlockSpec((1, 8, LANES), lambda h, i: (h, 0, 0))],
        out_shape=[jax.ShapeDtypeStruct((t, hd), F32), jax.ShapeDtypeStruct((t + pad, hd), F32),
                   jax.ShapeDtypeStruct((t + pad, hd), F32), jax.ShapeDtypeStruct((N_HEADS, TQ, w), F32),
                   jax.ShapeDtypeStruct((nhp, 8, LANES), F32)],
        args=(sinks, q, kp, vp, bias, do, stats, o), sem=("parallel", "arbitrary"), comm=comm)


def _conv_apply(taps, w_ref, ktaps):
    out = taps[0] * w_ref[ktaps - 1:ktaps, :]
    for s in range(1, ktaps):
        out = out + taps[s] * w_ref[ktaps - 1 - s:ktaps - s, :]
    return out


def _sigmoid(x):
    return jax.nn.sigmoid(x)


def _silu_grad(x):
    sg = _sigmoid(x)
    return x * sg, sg * (1.0 + x * (1.0 - sg))


FFN_HALO = 16
FFN_BT = 256
FFN_BC = 1408


def _ffn_in_mid(x, g, wt, w8, b, name):
    t, d = x.shape
    f = D_FF
    tm = FFN_BT

    def body(x_ref, g_ref, b_ref, w_ref, cb_ref, gu_ref, h_ref, a_ref, gc_ref, halo_ref):
        @pl.when(pl.program_id(0) == 0)
        def _():
            halo_ref[...] = jnp.zeros_like(halo_ref)

        xv = x_ref[...]
        r = lax.rsqrt(jnp.mean(xv * xv, axis=-1, keepdims=True) + EPS)
        h = (xv * r * g_ref[...]).astype(BF16)
        h_ref[...] = h
        for c in range(0, f, FFN_BC):
            cs = slice(c, c + FFN_BC)
            gate = _dot(h, b_ref[c:c + FFN_BC, :], 1, 1).astype(BF16)
            up = _dot(h, b_ref[f + c:f + c + FFN_BC, :], 1, 1).astype(BF16)
            gu_ref[:, cs] = gate
            gu_ref[:, f + c:f + c + FFN_BC] = up
            gf = gate.astype(F32)
            ext = jnp.concatenate([halo_ref[:, cs], gf], axis=0)
            gc = (cb_ref[:, cs] + gf * w_ref[2:3, cs] + pltpu.roll(ext, 1, 0)[8:] * w_ref[1:2, cs]
                  + pltpu.roll(ext, 2, 0)[8:] * w_ref[0:1, cs])
            a_ref[:, cs] = (gc * _sigmoid(gc) * up.astype(F32)).astype(BF16)
            gc_ref[:, cs] = gc.astype(BF16)
            halo_ref[:, cs] = gf[tm - 8:]

    row = pl.BlockSpec((tm, d), lambda i: (i, 0))
    row_f = pl.BlockSpec((tm, f), lambda i: (i, 0))
    return pl.pallas_call(
        body, name=name, grid=(t // tm,),
        in_specs=[row, pl.BlockSpec((1, d), lambda i: (0, 0)), pl.BlockSpec((2 * f, d), lambda i: (0, 0)),
                  pl.BlockSpec((8, f), lambda i: (0, 0)), pl.BlockSpec((1, f), lambda i: (0, 0))],
        out_specs=[pl.BlockSpec((tm, 2 * f), lambda i: (i, 0)), row, row_f, row_f],
        out_shape=[jax.ShapeDtypeStruct((t, 2 * f), BF16), jax.ShapeDtypeStruct((t, d), BF16), jax.ShapeDtypeStruct((t, f), BF16),
                   jax.ShapeDtypeStruct((t, f), BF16)],
        scratch_shapes=[pltpu.VMEM((8, f), F32)], compiler_params=_cp(("arbitrary",)),
    )(x, g, wt, w8, b)


def _ffn_mid_bwd(gu, gc, dxb, w_out, w8, name):
    t, d = dxb.shape
    f = D_FF
    tm, hr = FFN_BT, FFN_HALO
    nt = t // tm
    n = tm + hr

    def body(g_ref, u_ref, un_ref, c_ref, cn_ref, dx_ref, dxn_ref, wo_ref, w_ref, dgu_ref, dw_ref, db_ref):
        i = pl.program_id(0)
        last = i == nt - 1

        @pl.when(i == 0)
        def _():
            dw_ref[...] = jnp.zeros_like(dw_ref)
            db_ref[...] = jnp.zeros_like(db_ref)

        dxe = jnp.concatenate([dx_ref[...], dxn_ref[...]], axis=0)
        row = lax.broadcasted_iota(jnp.int32, (n, 1), 0)
        keep = (row < tm) | jnp.logical_not(last)
        for c in range(0, f, FFN_BC):
            cs = slice(c, c + FFN_BC)
            act, dact = _silu_grad(jnp.concatenate([c_ref[:, cs], cn_ref[:, cs]], axis=0).astype(F32))
            da = _dot(dxe, wo_ref[cs, :], 1, 1)
            up = jnp.concatenate([u_ref[:, cs], un_ref[:, cs]], axis=0).astype(F32)
            dgc = jnp.where(keep, da * up * dact, 0.0)
            nxt = [dgc[:tm], pltpu.roll(dgc, n - 1, 0)[:tm], pltpu.roll(dgc, n - 2, 0)[:tm]]
            dgu_ref[:, f + c:f + c + FFN_BC] = (da[:tm] * act[:tm]).astype(BF16)
            dgu_ref[:, cs] = (nxt[0] * w_ref[2:3, cs] + nxt[1] * w_ref[1:2, cs] + nxt[2] * w_ref[0:1, cs]).astype(BF16)
            gate = g_ref[:, cs].astype(F32)
            db_ref[:, cs] += jnp.sum(nxt[0], axis=0, keepdims=True)
            for s in range(3):
                dw_ref[2 - s:3 - s, cs] += jnp.sum(nxt[s] * gate, axis=0, keepdims=True)

    r = tm // hr
    nxt_blk = lambda i: jnp.minimum((i + 1) * r, t // hr - 1)
    row_f = pl.BlockSpec((tm, f), lambda i: (i, 0))
    halo_f = pl.BlockSpec((hr, f), lambda i: (nxt_blk(i), 0))
    return pl.pallas_call(
        body, name=name, grid=(nt,),
        in_specs=[row_f, pl.BlockSpec((tm, f), lambda i: (i, 1)), pl.BlockSpec((hr, f), lambda i: (nxt_blk(i), 1)),
                  row_f, halo_f,
                  pl.BlockSpec((tm, d), lambda i: (i, 0)), pl.BlockSpec((hr, d), lambda i: (nxt_blk(i), 0)),
                  pl.BlockSpec((f, d), lambda i: (0, 0)), pl.BlockSpec((8, f), lambda i: (0, 0))],
        out_specs=[pl.BlockSpec((tm, 2 * f), lambda i: (i, 0)), pl.BlockSpec((8, f), lambda i: (0, 0)),
                   pl.BlockSpec((1, f), lambda i: (0, 0))],
        out_shape=[jax.ShapeDtypeStruct((t, 2 * f), BF16), jax.ShapeDtypeStruct((8, f), F32), jax.ShapeDtypeStruct((1, f), F32)],
        compiler_params=_cp(("arbitrary",)),
    )(gu, gu, gu, gc, gc, dxb, dxb, w_out, w8)


PRE_TM = 256
PRE_TC = 1024


def _ssm_in_pre(x, g, wt, w8, b, name):
    t, d = x.shape
    tm, tc = PRE_TM, PRE_TC

    def body(x_ref, g_ref, b_ref, w_ref, cb_ref, zx_ref, h_ref, o_ref, c_ref, halo_ref):
        @pl.when(pl.program_id(0) == 0)
        def _():
            halo_ref[...] = jnp.zeros_like(halo_ref)

        xv = x_ref[...]
        r = lax.rsqrt(jnp.mean(xv * xv, axis=-1, keepdims=True) + EPS)
        h = (xv * r * g_ref[...]).astype(BF16)
        h_ref[...] = h
        for c in range(0, ZX, tc):
            v = _dot(h, b_ref[c:c + tc, :], 1, 1)
            zx_ref[:, c:c + tc] = v
            if c >= D_INNER:
                cs = slice(c - D_INNER, c - D_INNER + tc)
                ext = jnp.concatenate([halo_ref[:, cs], v], axis=0)
                conv = cb_ref[:, cs] + v * w_ref[3:4, cs]
                for s in (1, 2, 3):
                    conv = conv + pltpu.roll(ext, s, 0)[8:] * w_ref[3 - s:4 - s, cs]
                o_ref[:, cs] = conv * _sigmoid(conv)
                c_ref[:, cs] = conv.astype(BF16)
                halo_ref[:, cs] = v[tm - 8:]

    row = pl.BlockSpec((tm, d), lambda i: (i, 0))
    row_x = pl.BlockSpec((tm, XBC), lambda i: (i, 0))
    return pl.pallas_call(
        body, name=name, grid=(t // tm,),
        in_specs=[row, pl.BlockSpec((1, d), lambda i: (0, 0)), pl.BlockSpec(wt.shape, lambda i: (0, 0)),
                  pl.BlockSpec((8, XBC), lambda i: (0, 0)), pl.BlockSpec((1, XBC), lambda i: (0, 0))],
        out_specs=[pl.BlockSpec((tm, ZX), lambda i: (i, 0)), row, row_x, row_x],
        out_shape=[jax.ShapeDtypeStruct((t, ZX), F32), jax.ShapeDtypeStruct((t, d), BF16), jax.ShapeDtypeStruct((t, XBC), F32),
                   jax.ShapeDtypeStruct((t, XBC), BF16)],
        scratch_shapes=[pltpu.VMEM((8, XBC), F32)], compiler_params=_cp(("arbitrary",)),
    )(x, g, wt, w8, b)


PRE_HALO = 16


def _ssm_pre_bwd(zx, conv, dxbc, w8, name):
    t = zx.shape[0]
    tm, tc, hr = PRE_TM, PRE_TC, PRE_HALO
    off = D_INNER // tc
    nt = t // tm
    n = tm + hr

    def body(x_ref, c_ref, cn_ref, d_ref, dn_ref, w_ref, o_ref, dw_ref, db_ref):
        i = pl.program_id(1)
        last = i == nt - 1

        @pl.when(i == 0)
        def _():
            dw_ref[...] = jnp.zeros_like(dw_ref)
            db_ref[...] = jnp.zeros_like(db_ref)

        _, dact = _silu_grad(jnp.concatenate([c_ref[...], cn_ref[...]], axis=0).astype(F32))
        row = lax.broadcasted_iota(jnp.int32, (n, 1), 0)
        dc = jnp.where((row < tm) | jnp.logical_not(last), jnp.concatenate([d_ref[...], dn_ref[...]], axis=0) * dact, 0.0)
        nxt = [dc[:tm]] + [pltpu.roll(dc, n - s, 0)[:tm] for s in (1, 2, 3)]
        o_ref[...] = _conv_apply(nxt, w_ref, 4).astype(BF16)
        xv = x_ref[...]
        db_ref[...] += jnp.sum(nxt[0], axis=0, keepdims=True)
        for s in range(4):
            dw_ref[3 - s:4 - s, :] += jnp.sum(nxt[s] * xv, axis=0, keepdims=True)

    nxt_blk = lambda i: jnp.minimum((i + 1) * (tm // hr), t // hr - 1)
    tile = pl.BlockSpec((tm, tc), lambda j, i: (i, j))
    halo = pl.BlockSpec((hr, tc), lambda j, i: (nxt_blk(i), j))
    return pl.pallas_call(
        body, name=name, grid=(XBC // tc, nt),
        in_specs=[pl.BlockSpec((tm, tc), lambda j, i: (i, j + off)), tile, halo, tile, halo,
                  pl.BlockSpec((8, tc), lambda j, i: (0, j))],
        out_specs=[tile, pl.BlockSpec((8, tc), lambda j, i: (0, j)), pl.BlockSpec((1, tc), lambda j, i: (0, j))],
        out_shape=[jax.ShapeDtypeStruct((t, XBC), BF16), jax.ShapeDtypeStruct((8, XBC), F32),
                   jax.ShapeDtypeStruct((1, XBC), F32)],
        compiler_params=_cp(("parallel", "arbitrary")),
    )(zx, conv, conv, dxbc, dxbc, w8)


def _head_lanes():
    return lax.broadcasted_iota(jnp.int32, (1, LANES), 1) < SSM_HEADS


def _dt_fwd(dtraw, bias, name):
    t = dtraw.shape[0]
    tm = _pick(t, (1024, 512, 256, 128))

    def body(x_ref, b_ref, o_ref):
        v = x_ref[...] + b_ref[...]
        sp = jnp.maximum(v, 0.0) + jnp.log(1.0 + jnp.exp(-jnp.abs(v)))
        o_ref[...] = jnp.where(_head_lanes(), sp, 0.0)

    row = pl.BlockSpec((tm, LANES), lambda i: (i, 0))
    return pl.pallas_call(
        body, name=name, grid=(t // tm,), in_specs=[row, pl.BlockSpec((1, LANES), lambda i: (0, 0))], out_specs=row,
        out_shape=jax.ShapeDtypeStruct((t, LANES), F32), compiler_params=_cp(("parallel",)),
    )(dtraw, bias)


def _dt_bwd(dtraw, bias, ddt, name):
    t = dtraw.shape[0]
    tm = _pick(t, (1024, 512, 256, 128))

    def body(x_ref, b_ref, d_ref, o_ref, db_ref):
        @pl.when(pl.program_id(0) == 0)
        def _():
            db_ref[...] = jnp.zeros_like(db_ref)

        g = jnp.where(_head_lanes(), d_ref[...] * _sigmoid(x_ref[...] + b_ref[...]), 0.0)
        o_ref[...] = g.astype(BF16)
        db_ref[...] += jnp.sum(g, axis=0, keepdims=True)

    row = pl.BlockSpec((tm, LANES), lambda i: (i, 0))
    vec = pl.BlockSpec((1, LANES), lambda i: (0, 0))
    return pl.pallas_call(
        body, name=name, grid=(t // tm,), in_specs=[row, vec, row], out_specs=[row, vec],
        out_shape=[jax.ShapeDtypeStruct((t, LANES), BF16), jax.ShapeDtypeStruct((1, LANES), F32)],
        compiler_params=_cp(("arbitrary",)),
    )(dtraw, bias, ddt)


GROUP_W = D_INNER // SSM_GROUPS


def _ssd_common(dt, alog):
    ll = dt.shape[0]
    a_neg = -jnp.exp(alog)
    a = dt * a_neg
    ri = lax.broadcasted_iota(jnp.int32, (ll, ll), 0)
    ci = lax.broadcasted_iota(jnp.int32, (ll, ll), 1)
    tril = ri >= ci
    acs = _dot(tril.astype(F32), a, 1, 0, HI)
    return a_neg, tril, acs, acs.T


def _pair_terms(acs, acs_t, dt, h0, lo):
    ll = acs.shape[0]
    cols = [acs[:, h0 + e:h0 + e + 1] for e in range(2)]
    rows = [acs_t[h0 + e:h0 + e + 1, :] for e in range(2)]
    dtc = [dt[:, h0 + e:h0 + e + 1] for e in range(2)]
    lasts = [c[ll - 1:ll, :] for c in cols]
    dtx = jnp.where(lo, dtc[0], dtc[1])
    eac = jnp.where(lo, jnp.exp(cols[0]), jnp.exp(cols[1]))
    fdec = jnp.where(lo, jnp.exp(lasts[0] - cols[0]), jnp.exp(lasts[1] - cols[1]))
    elast = jnp.where(lo, jnp.exp(lasts[0]), jnp.exp(lasts[1]))
    return cols, rows, dtx, eac, fdec, elast


def _decay(col, row, tril):
    return jnp.where(tril, jnp.exp(jnp.minimum(col - row, 0.0)), 0.0)


def _two_heads_rows(v, lo):
    z = jnp.zeros_like(v)
    return jnp.concatenate([jnp.where(lo, v, z), jnp.where(lo, z, v)], axis=0)


def _two_heads_cols(ms):
    return jnp.concatenate(ms, axis=1)


def _z_group(z_refs, g):
    return z_refs[g // 2][:, GROUP_W * (g % 2):GROUP_W * (g % 2 + 1)]


def _ssd_fwd(xbc, dt, alog, zx, dexp, nw, name, comm=None):
    t = xbc.shape[0]
    ll = SSD_L
    nc = t // ll

    def body(x_ref, dt_ref, al_ref, z0_ref, z1_ref, d_ref, w_ref, y_ref, sp_ref, y4_ref, st_ref):
        @pl.when(pl.program_id(0) == 0)
        def _():
            st_ref[...] = jnp.zeros_like(st_ref)

        dtv = dt_ref[...]
        _, tril, acs, acs_t = _ssd_common(dtv, al_ref[...])
        lo = _lo_mask()
        sp_ref[0] = st_ref[...]
        for g in range(SSM_GROUPS):
            bg = x_ref[:, D_INNER + SSM_STATE * g:D_INNER + SSM_STATE * (g + 1)].astype(BF16)
            cg = x_ref[:, D_INNER + 512 + SSM_STATE * g:D_INNER + 512 + SSM_STATE * (g + 1)].astype(BF16)
            gm = _dot(cg, bg, 1, 1)
            g0 = GROUP_W * g
            terms = [_pair_terms(acs, acs_t, dtv, 8 * g + 2 * pp, lo) for pp in range(4)]
            dtx, eac, fdec, elast = [jnp.concatenate([tt[k] for tt in terms], axis=1) for k in (2, 3, 4, 5)]
            xg = x_ref[:, g0:g0 + GROUP_W]
            ug = (xg * dtx).astype(BF16)
            sg = st_ref[:, g0:g0 + GROUP_W]
            yst = _dot(cg, sg.astype(BF16), 1, 0) * eac
            st_ref[:, g0:g0 + GROUP_W] = sg * elast + _dot(bg, (xg * (fdec * dtx)).astype(BF16), 0, 0)
            ys = []
            for pp in range(4):
                cols, rows = terms[pp][0], terms[pp][1]
                sl = slice(LANES * pp, LANES * (pp + 1))
                y_in = _dot(_two_heads_cols([(gm * _decay(cols[e], rows[e], tril)).astype(BF16) for e in range(2)]),
                            _two_heads_rows(ug[:, sl], lo), 1, 0)
                ys.append(y_in + yst[:, sl])
            yg = jnp.concatenate(ys, axis=1)
            y_ref[:, g0:g0 + GROUP_W] = yg
            zg = _z_group((z0_ref, z1_ref), g)
            y3 = (yg + d_ref[:, g0:g0 + GROUP_W] * xg) * (zg * _sigmoid(zg))
            r = lax.rsqrt(jnp.mean(y3 * y3, axis=-1, keepdims=True) + EPS)
            y4_ref[:, g0:g0 + GROUP_W] = (y3 * r * w_ref[:, g0:g0 + GROUP_W]).astype(BF16)

    zblk = lambda j: pl.BlockSpec((ll, 1024), lambda c: (c, j))
    vec = pl.BlockSpec((1, D_INNER), lambda c: (0, 0))
    row = pl.BlockSpec((ll, D_INNER), lambda c: (c, 0))
    return _call(
        body, name=name, grid=(nc,),
        in_specs=[pl.BlockSpec((ll, XBC), lambda c: (c, 0)), pl.BlockSpec((ll, LANES), lambda c: (c, 0)),
                  pl.BlockSpec((1, LANES), lambda c: (0, 0)), zblk(0), zblk(1), vec, vec],
        out_specs=[row, pl.BlockSpec((1, SSM_STATE, D_INNER), lambda c: (c, 0, 0)), row],
        out_shape=[jax.ShapeDtypeStruct((t, D_INNER), F32), jax.ShapeDtypeStruct((nc, SSM_STATE, D_INNER), F32),
                   jax.ShapeDtypeStruct((t, D_INNER), BF16)],
        scratch_shapes=[pltpu.VMEM((SSM_STATE, D_INNER), F32)],
        args=(xbc, dt, alog, zx, zx, dexp, nw), sem=("arbitrary",), comm=comm)


def _ssd_bwd(xbc, dt, alog, sprev, dy4, y, zx, dexp, nw, name, comm=None):
    t = xbc.shape[0]
    ll = SSD_L_BWD if t % SSD_L_BWD == 0 else SSD_L
    nc = t // ll
    every = ll // SSD_L

    def body(x_ref, dt_ref, al_ref, sp_ref, g4_ref, y_ref, z0_ref, z1_ref, d_ref, w_ref,
             dx_ref, ddt_ref, dal_ref, dz_ref, dd_ref, dnw_ref, ds_ref, colt_ref):
        @pl.when(pl.program_id(0) == 0)
        def _():
            ds_ref[...] = jnp.zeros_like(ds_ref)
            dal_ref[...] = jnp.zeros_like(dal_ref)
            dd_ref[...] = jnp.zeros_like(dd_ref)
            dnw_ref[...] = jnp.zeros_like(dnw_ref)

        dtv = dt_ref[...]
        a_neg, tril, acs, acs_t = _ssd_common(dtv, al_ref[...])
        lo = _lo_mask()
        hi = jnp.logical_not(lo)
        lane = lax.broadcasted_iota(jnp.int32, (1, LANES), 1)
        colt_ref[...] = jnp.zeros_like(colt_ref)
        rowterm = jnp.zeros((ll, LANES), F32)
        ddt_u = jnp.zeros((ll, LANES), F32)
        dlast = jnp.zeros((1, LANES), F32)

        def halves(v):
            return (jnp.sum(jnp.where(lo, v, 0.0), axis=-1, keepdims=True),
                    jnp.sum(jnp.where(hi, v, 0.0), axis=-1, keepdims=True))

        for g in range(SSM_GROUPS):
            cb0 = D_INNER + SSM_STATE * g
            cc0 = D_INNER + 512 + SSM_STATE * g
            bg = x_ref[:, cb0:cb0 + SSM_STATE].astype(BF16)
            cg = x_ref[:, cc0:cc0 + SSM_STATE].astype(BF16)
            gm = _dot(cg, bg, 1, 1)
            g0 = GROUP_W * g
            terms = [_pair_terms(acs, acs_t, dtv, 8 * g + 2 * pp, lo) for pp in range(4)]
            dtx, eac, fdec, elast = [jnp.concatenate([tt[k] for tt in terms], axis=1) for k in (2, 3, 4, 5)]
            xg = x_ref[:, g0:g0 + GROUP_W]
            u32 = xg * dtx
            ug = u32.astype(BF16)
            zg = _z_group((z0_ref, z1_ref), g)
            dg = d_ref[:, g0:g0 + GROUP_W]
            act, dact = _silu_grad(zg)
            y2 = y_ref[:, g0:g0 + GROUP_W] + dg * xg
            y3 = y2 * act
            rn = lax.rsqrt(jnp.mean(y3 * y3, axis=-1, keepdims=True) + EPS)
            y3n = y3 * rn
            gv = g4_ref[:, g0:g0 + GROUP_W]
            dyn = gv * w_ref[:, g0:g0 + GROUP_W]
            dy3 = rn * (dyn - y3n * jnp.mean(dyn * y3n, axis=-1, keepdims=True))
            dyg = dy3 * act
            dskip = dyg * dg
            dz_ref[:, g0:g0 + GROUP_W] = (dy3 * y2 * dact).astype(BF16)
            dd_ref[:, g0:g0 + GROUP_W] += jnp.sum(dyg * xg, axis=0, keepdims=True)
            dnw_ref[:, g0:g0 + GROUP_W] += jnp.sum(gv * y3n, axis=0, keepdims=True)
            dyb = dyg.astype(BF16)
            spg = sp_ref[0, :, g0:g0 + GROUP_W]
            spb = spg.astype(BF16)
            dsg = ds_ref[:, g0:g0 + GROUP_W]
            dsb = dsg.astype(BF16)
            du_st = _dot(bg, dsb, 1, 0) * fdec
            yst = _dot(cg, spb, 1, 0) * eac
            dye = (dyg * eac).astype(BF16)
            dc_st = _dot(dye, spb, 1, 1)
            db_st = _dot((xg * (fdec * dtx)).astype(BF16), dsb, 1, 1)
            ds_ref[:, g0:g0 + GROUP_W] = dsg * elast + _dot(cg, dye, 0, 0)
            qst_el = du_st * u32
            rq_el = dyg * yst - qst_el
            q_row = jnp.sum(qst_el, axis=0, keepdims=True)
            s_row = jnp.sum(dsg * spg, axis=0, keepdims=True)
            dgm = jnp.zeros((ll, ll), F32)
            for pp in range(4):
                h0 = 8 * g + 2 * pp
                cols, rows = terms[pp][0], terms[pp][1]
                sl = slice(LANES * pp, LANES * (pp + 1))
                decs = [_decay(cols[e], rows[e], tril) for e in range(2)]
                wms = [gm * d for d in decs]
                dum2 = _dot(dyb[:, sl], _two_heads_rows(ug[:, sl], lo), 1, 1)
                du = _dot(jnp.concatenate([wm.astype(BF16) for wm in wms], axis=0),
                          _two_heads_rows(dyb[:, sl], lo), 0, 0) + du_st[:, sl]
                dx_ref[:, g0 + LANES * pp:g0 + LANES * (pp + 1)] = du * dtx[:, sl] + dskip[:, sl]
                ddtu = halves(du * xg[:, sl])
                rq = halves(rq_el[:, sl])
                qs = halves(q_row[:, sl])
                ss = halves(s_row[:, sl])
                for e in range(2):
                    dum = dum2[:, ll * e:ll * (e + 1)]
                    dgm = dgm + dum * decs[e]
                    tm_ = dum * wms[e]
                    oh = lane == (h0 + e)
                    rowterm = rowterm + jnp.where(oh, jnp.sum(tm_, axis=1, keepdims=True) + rq[e], 0.0)
                    ddt_u = ddt_u + jnp.where(oh, ddtu[e], 0.0)
                    dlast = dlast + jnp.where(oh, jnp.exp(cols[e][ll - 1:ll, :]) * ss[e] + qs[e], 0.0)
                    colt_ref[h0 + e:h0 + e + 1, :] = jnp.sum(tm_, axis=0, keepdims=True)
            dgb = dgm.astype(BF16)
            dx_ref[:, cc0:cc0 + SSM_STATE] = _dot(dgb, bg, 1, 0) + dc_st
            dx_ref[:, cb0:cb0 + SSM_STATE] = _dot(dgb, cg, 0, 0) + db_st
        row_io = lax.broadcasted_iota(jnp.int32, (ll, LANES), 0)
        dacs = rowterm - colt_ref[...].T + jnp.where(row_io == ll - 1, dlast, 0.0)
        da = _dot(jnp.logical_not(tril).astype(F32) + jnp.where(
            lax.broadcasted_iota(jnp.int32, (ll, ll), 0) == lax.broadcasted_iota(jnp.int32, (ll, ll), 1), 1.0, 0.0),
            dacs, 1, 0, HI)
        ddt_ref[...] = da * a_neg + ddt_u
        dal_ref[...] += jnp.sum(da * dtv, axis=0, keepdims=True) * a_neg

    rev = lambda c: nc - 1 - c
    row = pl.BlockSpec((ll, D_INNER), lambda c: (rev(c), 0))
    vec = pl.BlockSpec((1, D_INNER), lambda c: (0, 0))
    zblk = lambda j: pl.BlockSpec((ll, 1024), lambda c: (rev(c), j))
    return _call(
        body, name=name, grid=(nc,),
        in_specs=[pl.BlockSpec((ll, XBC), lambda c: (rev(c), 0)), pl.BlockSpec((ll, LANES), lambda c: (rev(c), 0)),
                  pl.BlockSpec((1, LANES), lambda c: (0, 0)),
                  pl.BlockSpec((1, SSM_STATE, D_INNER), lambda c: (rev(c) * every, 0, 0)), row, row, zblk(0), zblk(1), vec, vec],
        out_specs=[pl.BlockSpec((ll, XBC), lambda c: (rev(c), 0)), pl.BlockSpec((ll, LANES), lambda c: (rev(c), 0)),
                   pl.BlockSpec((1, LANES), lambda c: (0, 0)), row, vec, vec],
        out_shape=[jax.ShapeDtypeStruct((t, XBC), F32), jax.ShapeDtypeStruct((t, LANES), F32),
                   jax.ShapeDtypeStruct((1, LANES), F32), jax.ShapeDtypeStruct((t, D_INNER), BF16),
                   jax.ShapeDtypeStruct((1, D_INNER), F32), jax.ShapeDtypeStruct((1, D_INNER), F32)],
        scratch_shapes=[pltpu.VMEM((SSM_STATE, D_INNER), F32), pltpu.VMEM((LANES, ll), F32)],
        args=(xbc, dt, alog, sprev, dy4, y, zx, zx, dexp, nw), sem=("arbitrary",), comm=comm)


def _sum_parts(parts, name):
    nparts, r, c = parts.shape
    tc = _pick(c, (256, 128))

    def body(p_ref, o_ref):
        g = p_ref[0].astype(F32)
        for k in range(1, nparts):
            g = g + p_ref[k].astype(F32)
        o_ref[...] = g

    return pl.pallas_call(
        body, name=name, grid=(c // tc,), in_specs=[pl.BlockSpec((nparts, r, tc), lambda j: (0, 0, j))],
        out_specs=pl.BlockSpec((r, tc), lambda j: (0, j)), out_shape=jax.ShapeDtypeStruct((r, c), F32),
        compiler_params=_cp(("parallel",)),
    )(parts)


def _adamw(parts, w, m, v, name):
    nl, r, c = w.shape
    assert len(parts) == nl
    tr = _pick(r, (256, 128, 64))
    c1 = 1.0 - ADAM_B1 ** ADAM_STEP
    c2 = 1.0 - ADAM_B2 ** ADAM_STEP

    def body(*refs):
        p_refs = refs[:nl]
        w_ref, m_ref, v_ref, g_ref, d_ref, mo_ref, vo_ref = refs[nl:]
        g = None
        for l, p_ref in enumerate(p_refs):
            s = p_ref[0].astype(F32)
            for k in range(1, p_ref.shape[0]):
                s = s + p_ref[k].astype(F32)
            g = s if g is None else jnp.where(pl.program_id(0) == l, s, g)
        mn = ADAM_B1 * m_ref[0] + (1.0 - ADAM_B1) * g
        vn = ADAM_B2 * v_ref[0] + (1.0 - ADAM_B2) * (g * g)
        g_ref[0] = g
        mo_ref[0] = mn
        vo_ref[0] = vn
        d_ref[0] = -ADAM_LR * ((mn / c1) / (jnp.sqrt(vn / c2) + ADAM_EPS) + ADAM_WD * w_ref[0])

    row = pl.BlockSpec((1, tr, c), lambda l, i: (l, i, 0))
    sd = jax.ShapeDtypeStruct((nl, r, c), F32)
    return pl.pallas_call(
        body, name=name, grid=(nl, r // tr),
        in_specs=[pl.BlockSpec((p.shape[0], tr, c), lambda l, i: (0, i, 0)) for p in parts] + [row, row, row],
        out_specs=[row, row, row, row], out_shape=[sd, sd, sd, sd], compiler_params=_cp(("parallel", "parallel")),
    )(*parts, w, m, v)


def _peers():
    mx, my, mc = lax.axis_index("x"), lax.axis_index("y"), lax.axis_index("c")
    me = 4 * mx + 2 * my + mc
    out = []
    for k in range(1, N_DEV):
        px = 1 - mx if k & 4 else mx
        py = 1 - my if k & 2 else my
        pc = 1 - mc if k & 1 else mc
        out.append(((px, py, pc), 4 * px + 2 * py + pc))
    return me, out


class _Comm:
    def __init__(self, arrs, scatters):
        self.arrs, self.scatters, self.n = list(arrs), list(scatters), len(arrs)
        self.specs = [pl.BlockSpec(memory_space=pl.ANY)] * self.n
        self.out_shape = [jax.ShapeDtypeStruct(x.shape if sc else (N_DEV,) + x.shape, x.dtype)
                          for x, sc in zip(self.arrs, self.scatters)]
        np_ = N_DEV - 1
        self.scratch = [pltpu.SemaphoreType.DMA((np_ * self.n,)), pltpu.SemaphoreType.DMA((np_ * self.n,)),
                        pltpu.SemaphoreType.DMA((self.n,))]

    def _copies(self, x_refs, o_refs, sems):
        send_sems, recv_sems, local_sems = sems
        me, peers = _peers()
        np_ = N_DEV - 1
        local, sends, recvs = [], [], []
        for a in range(self.n):
            mine = x_refs[a].at[me] if self.scatters[a] else x_refs[a]
            local.append(pltpu.make_async_copy(mine, o_refs[a].at[me], local_sems.at[a]))
        for k, (dev, idx) in enumerate(peers):
            for a in range(self.n):
                mine = x_refs[a].at[me] if self.scatters[a] else x_refs[a]
                sends.append(pltpu.make_async_remote_copy(
                    src_ref=x_refs[a].at[idx] if self.scatters[a] else x_refs[a], dst_ref=o_refs[a].at[me],
                    send_sem=send_sems.at[a * np_ + k], recv_sem=recv_sems.at[a * np_ + k], device_id=dev, device_id_type=MESH))
                recvs.append(pltpu.make_async_remote_copy(
                    src_ref=mine, dst_ref=o_refs[a].at[idx], send_sem=send_sems.at[a * np_ + k],
                    recv_sem=recv_sems.at[a * np_ + k], device_id=dev, device_id_type=MESH))
        return local, sends, recvs

    def start(self, x_refs, o_refs, sems):
        local, sends, _ = self._copies(x_refs, o_refs, sems)
        for cp in local + sends:
            cp.start()

    def wait(self, x_refs, o_refs, sems):
        local, sends, recvs = self._copies(x_refs, o_refs, sems)
        for cp in recvs:
            cp.wait_recv()
        for cp in sends:
            cp.wait_send()
        for cp in local:
            cp.wait()


class _Gather2(_Comm):
    def __init__(self, arrs):
        super().__init__(arrs, [False] * len(arrs))

    def _plan(self, x_refs, o_refs, sems):
        send_sems, recv_sems, local_sems = sems
        mx, my, mc = lax.axis_index("x"), lax.axis_index("y"), lax.axis_index("c")
        slot = lambda px, py, pc: 4 * px + 2 * py + pc
        sib = (mx, my, 1 - mc)
        chips = [(1 - mx, my), (mx, 1 - my), (1 - mx, 1 - my)]
        np_ = N_DEV - 1
        local, first, passed, arrive_first, arrive_rest = [], [], [], [], []

        def copy(a, k, src, block, to):
            return pltpu.make_async_remote_copy(
                src_ref=src, dst_ref=o_refs[a].at[block], send_sem=send_sems.at[a * np_ + k], recv_sem=recv_sems.at[a * np_ + k],
                device_id=to, device_id_type=MESH)

        for a in range(self.n):
            me = slot(mx, my, mc)
            local.append(pltpu.make_async_copy(x_refs[a], o_refs[a].at[me], local_sems.at[a]))
            first.append(copy(a, 0, x_refs[a], me, sib))
            arrive_rest.append(copy(a, 0, x_refs[a], slot(*sib), sib))
            for j, (cx, cy) in enumerate(chips):
                first.append(copy(a, 1 + j, x_refs[a], me, (cx, cy, mc)))
                arrive_first.append(copy(a, 1 + j, x_refs[a], slot(cx, cy, mc), (cx, cy, mc)))
                passed.append(copy(a, 4 + j, o_refs[a].at[slot(cx, cy, mc)], slot(cx, cy, mc), sib))
                arrive_rest.append(copy(a, 4 + j, x_refs[a], slot(cx, cy, 1 - mc), sib))
        return local, first, passed, arrive_first, arrive_rest

    def start(self, x_refs, o_refs, sems):
        local, first, _, _, _ = self._plan(x_refs, o_refs, sems)
        for cp in local + first:
            cp.start()

    def wait(self, x_refs, o_refs, sems):
        local, first, passed, arrive_first, arrive_rest = self._plan(x_refs, o_refs, sems)
        for arrived, onward in zip(arrive_first, passed):
            arrived.wait_recv()
            onward.start()
        for cp in arrive_rest:
            cp.wait_recv()
        for cp in first + passed:
            cp.wait_send()
        for cp in local:
            cp.wait()


def _call(body, *, name, grid, in_specs, out_specs, out_shape, args, scratch_shapes=(), sem=None, comm=None):
    if comm is None:
        outs = pl.pallas_call(
            body, name=name, grid=grid, in_specs=list(in_specs), out_specs=list(out_specs), out_shape=list(out_shape),
            scratch_shapes=list(scratch_shapes), compiler_params=_cp(sem),
        )(*args)
        return list(outs), []
    n_in, n_out, nc = len(in_specs), len(out_specs), comm.n
    nsteps = 1
    for g in grid:
        nsteps *= g

    def carrier(*refs):
        ins, cin = refs[:n_in], refs[n_in:n_in + nc]
        outs, cout = refs[n_in + nc:n_in + nc + n_out], refs[n_in + nc + n_out:n_in + 2 * nc + n_out]
        rest = refs[n_in + 2 * nc + n_out:]
        scratch, sems = rest[:len(rest) - 3], rest[len(rest) - 3:]
        if nsteps == 1:
            comm.start(cin, cout, sems)
            body(*ins, *outs, *scratch)
            comm.wait(cin, cout, sems)
            return
        step = 0
        for d, g in enumerate(grid):
            step = step * g + pl.program_id(d)

        @pl.when(step == 0)
        def _():
            comm.start(cin, cout, sems)

        body(*ins, *outs, *scratch)

        @pl.when(step == nsteps - 1)
        def _():
            comm.wait(cin, cout, sems)

    outs = pl.pallas_call(
        carrier, name=name, grid=grid, in_specs=list(in_specs) + comm.specs, out_specs=list(out_specs) + comm.specs,
        out_shape=list(out_shape) + comm.out_shape, scratch_shapes=list(scratch_shapes) + comm.scratch,
        compiler_params=_cp(("arbitrary",) * len(grid) if grid else None),
    )(*args, *comm.arrs)
    return list(outs[:n_out]), list(outs[n_out:])


def _exchange(comm, name):
    return _call(lambda *refs: None, name=name, grid=(), in_specs=[], out_specs=[], out_shape=[], args=[], comm=comm)[1]


def _pack(arrs, dtype, lead=()):
    nl = len(lead)
    flat = jnp.concatenate([a.astype(dtype).reshape(lead + (-1,)) for a in arrs], axis=nl)
    n = flat.shape[-1]
    rows = -(-n // (LANES * 8)) * 8
    flat = jnp.pad(flat, [(0, 0)] * nl + [(0, rows * LANES - n)])
    return flat.reshape(lead + (rows, LANES))


def _unpack(flat, shapes, lead=()):
    nl = len(lead)
    flat = flat.reshape(lead + (-1,))
    out, o = [], 0
    for s in shapes:
        n = 1
        for d in s:
            n *= d
        out.append(lax.slice_in_dim(flat, o, o + n, axis=nl).reshape(lead + tuple(s)))
        o += n
    return out


def _join(g, ax):
    return jnp.concatenate([g[d] for d in range(N_DEV)], axis=ax)


def _split(full, ax):
    n = full.shape[ax] // N_DEV
    return jnp.stack([lax.slice_in_dim(full, d * n, (d + 1) * n, axis=ax) for d in range(N_DEV)])


_WEIGHTS = ['norm_mix', 'norm_ffn', 'attn_w_in', 'attn_w_out', 'relpos_table', 'q_norm_a', 'k_norm_a', 'q_norm_b',
            'k_norm_b', 'sinks', 'ssm_w_in', 'ssm_conv_w', 'ssm_conv_b', 'ssm_dt_bias', 'ssm_a_log', 'ssm_d', 'ssm_norm',
            'ssm_w_out', 'ffn_w_in', 'ffn_conv_w', 'ffn_conv_b', 'ffn_w_out']
_SHARD_AX = {'attn_w_in': 2, 'attn_w_out': 1, 'ssm_w_in': 2, 'ssm_conv_w': 2, 'ssm_conv_b': 1, 'ssm_norm': 1,
             'ssm_w_out': 1, 'ffn_w_in': 2, 'ffn_conv_w': 2, 'ffn_w_out': 1}
_BIG = ['attn_w_in', 'attn_w_out', 'ssm_w_in', 'ssm_w_out', 'ffn_w_in', 'ffn_w_out']
_SMALL = ['ssm_conv_w', 'ssm_conv_b', 'ssm_norm', 'ffn_conv_w']
_AX2 = {n: _SHARD_AX[n] - 1 for n in _BIG}
_REPL = [n for n in _WEIGHTS if n not in _SHARD_AX]


def _rows8(w):
    return jnp.pad(w, ((0, 8 - w.shape[0]), (0, 0)))


def _lanes128(v):
    return jnp.pad(v, (0, LANES - v.shape[0])).reshape(1, LANES)


def _band_mask(n_prev, pad):
    cq = jnp.arange(TQ)[:, None] // CHUNK
    ck = jnp.arange(pad + TQ)[None, :] // CHUNK
    return (ck >= cq) & (ck <= cq + n_prev)


def _ffn_fwd(xin, g, w_in_t, w8, cb, tag):
    gu, h, a, gc = _ffn_in_mid(xin, g, w_in_t, w8, cb, f"mm_ffn_in{tag}")
    return a, (h, gu, a, gc)


def _ffn_bwd(dx, dxb, xin, g, w_in_t, w8, w_out, saved, tag):
    h, gu, a, gc = saved
    dw_out = _mm_tn(a, dxb, f"mm_ffn_dwout{tag}")
    dgu, dw8, dcb = _ffn_mid_bwd(gu, gc, dxb, w_out, w8, f"ffn_mid_bwd{tag}")
    dw_in_t = _mm_tn(dgu, h, f"mm_ffn_dwin{tag}")
    dxp, dxpb, dg = _mm_rms_bwd([(dgu, w_in_t, 0)], xin, g, dx, f"mm_ffn_dh{tag}")
    return dxp, dxpb, dg, dw_in_t, dw8[:3], dcb, dw_out


def kernel(x, norm_mix, norm_ffn, attn_w_in, attn_w_out, relpos_table, q_norm_a, k_norm_a, q_norm_b, k_norm_b, sinks, ssm_w_in, ssm_conv_w, ssm_conv_b, ssm_dt_bias, ssm_a_log, ssm_d, ssm_norm, ssm_w_out, ffn_w_in, ffn_conv_w, ffn_conv_b, ffn_w_out, loss_target, m_norm_mix, m_norm_ffn, m_attn_w_in, m_attn_w_out, m_relpos_table, m_q_norm_a, m_k_norm_a, m_q_norm_b, m_k_norm_b, m_sinks, m_ssm_w_in, m_ssm_conv_w, m_ssm_conv_b, m_ssm_dt_bias, m_ssm_a_log, m_ssm_d, m_ssm_norm, m_ssm_w_out, m_ffn_w_in, m_ffn_conv_w, m_ffn_conv_b, m_ffn_w_out, v_norm_mix, v_norm_ffn, v_attn_w_in, v_attn_w_out, v_relpos_table, v_q_norm_a, v_k_norm_a, v_q_norm_b, v_k_norm_b, v_sinks, v_ssm_w_in, v_ssm_conv_w, v_ssm_conv_b, v_ssm_dt_bias, v_ssm_a_log, v_ssm_d, v_ssm_norm, v_ssm_w_out, v_ffn_w_in, v_ffn_conv_w, v_ffn_conv_b, v_ffn_w_out):
    w = dict(norm_mix=norm_mix, norm_ffn=norm_ffn, attn_w_in=attn_w_in, attn_w_out=attn_w_out, relpos_table=relpos_table,
             q_norm_a=q_norm_a, k_norm_a=k_norm_a, q_norm_b=q_norm_b, k_norm_b=k_norm_b, sinks=sinks, ssm_w_in=ssm_w_in,
             ssm_conv_w=ssm_conv_w, ssm_conv_b=ssm_conv_b, ssm_dt_bias=ssm_dt_bias, ssm_a_log=ssm_a_log, ssm_d=ssm_d,
             ssm_norm=ssm_norm, ssm_w_out=ssm_w_out, ffn_w_in=ffn_w_in, ffn_conv_w=ffn_conv_w, ffn_conv_b=ffn_conv_b,
             ffn_w_out=ffn_w_out)
    mom = dict(norm_mix=m_norm_mix, norm_ffn=m_norm_ffn, attn_w_in=m_attn_w_in, attn_w_out=m_attn_w_out,
               relpos_table=m_relpos_table, q_norm_a=m_q_norm_a, k_norm_a=m_k_norm_a, q_norm_b=m_q_norm_b,
               k_norm_b=m_k_norm_b, sinks=m_sinks, ssm_w_in=m_ssm_w_in, ssm_conv_w=m_ssm_conv_w, ssm_conv_b=m_ssm_conv_b,
               ssm_dt_bias=m_ssm_dt_bias, ssm_a_log=m_ssm_a_log, ssm_d=m_ssm_d, ssm_norm=m_ssm_norm, ssm_w_out=m_ssm_w_out,
               ffn_w_in=m_ffn_w_in, ffn_conv_w=m_ffn_conv_w, ffn_conv_b=m_ffn_conv_b, ffn_w_out=m_ffn_w_out)
    var = dict(norm_mix=v_norm_mix, norm_ffn=v_norm_ffn, attn_w_in=v_attn_w_in, attn_w_out=v_attn_w_out,
               relpos_table=v_relpos_table, q_norm_a=v_q_norm_a, k_norm_a=v_k_norm_a, q_norm_b=v_q_norm_b,
               k_norm_b=v_k_norm_b, sinks=v_sinks, ssm_w_in=v_ssm_w_in, ssm_conv_w=v_ssm_conv_w, ssm_conv_b=v_ssm_conv_b,
               ssm_dt_bias=v_ssm_dt_bias, ssm_a_log=v_ssm_a_log, ssm_d=v_ssm_d, ssm_norm=v_ssm_norm, ssm_w_out=v_ssm_w_out,
               ffn_w_in=v_ffn_w_in, ffn_conv_w=v_ffn_conv_w, ffn_conv_b=v_ffn_conv_b, ffn_w_out=v_ffn_w_out)

    def piece(n, l):
        return (w[n][l].T if _AX2[n] == 1 else w[n][l]).astype(BF16)

    def gather_of(names_layers):
        return _Gather2([piece(n, l) for n, l in names_layers])

    def joined(got):
        return [g.reshape(-1, D_MODEL) for g in got]

    first = [('attn_w_in', 0), ('attn_w_out', 0)]
    got = _exchange(_Gather2([piece(n, l) for n, l in first] + [_pack([w[n] for n in _SMALL], F32)]), "gather_attn")
    wt_attn_in, w_attn_out = joined(got[:2])
    full = {}
    for n, g in zip(_SMALL, _unpack(got[2], [w[n].shape for n in _SMALL], lead=(N_DEV,))):
        full[n] = _join(g, _SHARD_AX[n])
    ssm_cw8 = _rows8(full['ssm_conv_w'][0])
    ssm_cb = full['ssm_conv_b']
    ssm_nw = full['ssm_norm']
    ffn_cw8 = [_rows8(full['ffn_conv_w'][l]) for l in range(2)]
    ffn_cb = [ffn_conv_b[l:l + 1] for l in range(2)]

    x0 = x[0]
    target = loss_target[0]
    t = x0.shape[0]

    g_mix0, g_mix1 = norm_mix[0:1], norm_mix[1:2]
    g_ffn0, g_ffn1 = norm_ffn[0:1], norm_ffn[1:2]
    proj, h0 = _rms_mm(x0, g_mix0, wt_attn_in, ATTN_PROJ, "mm_attn_in", F32)
    hn_w = jnp.concatenate([jnp.tile(v, (1, 2)) for v in (q_norm_a, k_norm_a, q_norm_b, k_norm_b)], axis=0)
    qa, kpa, vpa, qb, kpb, vpb = _headnorm_fwd(proj, hn_w, "headnorm")
    table = jnp.pad(relpos_table[0], ((0, 0), (0, REL_W - (2 * MAX_REL + 1))))
    bias_a = jnp.where(_band_mask(A_PREV, PAD_A)[None], jnp.transpose(_relpos_fwd(table, "relpos_bias"), (1, 0, 2)), NEG)
    rel_b = jnp.arange(TQ)[:, None] - (jnp.arange(PAD_B + TQ)[None, :] - PAD_B)
    slopes = 2.0 ** (-8.0 * jnp.arange(1, N_HEADS + 1, dtype=F32) / N_HEADS)
    bias_b = jnp.where(_band_mask(B_PREV, PAD_B)[None], -slopes[:, None, None] * jnp.abs(rel_b).astype(F32)[None], NEG)
    no_sinks = jnp.full((N_HEADS,), NEG, F32)
    ffn0_w, ssm_w, ffn1_w = [('ffn_w_in', 0), ('ffn_w_out', 0)], [('ssm_w_in', 0), ('ssm_w_out', 0)], [('ffn_w_in', 1), ('ffn_w_out', 1)]
    oa, stats_a, got = _attn_fwd(qa, kpa, vpa, bias_a, no_sinks, PAD_A, "attn_a", comm=gather_of(ffn0_w + ssm_w))
    wt_ffn_in0, w_ffn_out0, wt_ssm_in, w_ssm_out = joined(got)
    ob, stats_b, _ = _attn_fwd(qb, kpb, vpb, bias_b, sinks[0], PAD_B, "attn_b")
    wt_ssm_dt = jnp.pad(wt_ssm_in[ZX:], ((0, LANES - SSM_HEADS), (0, 0)))
    x1 = _mm_pair(oa, ob, w_attn_out, x0, "mm_attn_out")
    a0, ffn0_saved = _ffn_fwd(x1, g_ffn0, wt_ffn_in0, ffn_cw8[0], ffn_cb[0], "0")
    x2 = _mm(a0, w_ffn_out0, "mm_ffn_out0", res=x1)

    zx, h2, xbc, conv_pre = _ssm_in_pre(x2, g_mix1, wt_ssm_in, ssm_cw8, ssm_cb, "mm_ssm_in")
    dtraw = _mm(h2, wt_ssm_dt, "mm_ssm_dt", trans_b=True)
    dt_bias = _lanes128(ssm_dt_bias[0])
    alog = _lanes128(ssm_a_log[0])
    dexp = jnp.repeat(ssm_d[0], HEAD_DIM).reshape(1, D_INNER)
    dt = _dt_fwd(dtraw, dt_bias, "ssm_dt")
    (y, sprev, y4), got = _ssd_fwd(xbc, dt, alog, zx, dexp, ssm_nw, "ssd_fwd", comm=gather_of(ffn1_w))
    wt_ffn_in1, w_ffn_out1 = joined(got)
    x3 = _mm(y4, w_ssm_out, "mm_ssm_out", res=x2)
    a1, ffn1_saved = _ffn_fwd(x3, g_ffn1, wt_ffn_in1, ffn_cw8[1], ffn_cb[1], "1")

    dx4, dx4b, sq = _mm_loss(a1, w_ffn_out1, x3, target, "mm_ffn_out1_loss")
    loss = lax.psum(0.5 * jnp.sum(sq) / D_MODEL, ("x", "y", "c"))

    grads = {}

    def scatter_of(grads_2d):
        return _Comm([g.reshape(N_DEV, -1, D_MODEL) for g in grads_2d], [True] * len(grads_2d))

    dx3, dx3b, dg_ffn1, dwtin1, dcw1, dcb1, dwout1 = _ffn_bwd(
        dx4, dx4b, x3, g_ffn1, wt_ffn_in1, ffn_cw8[1], w_ffn_out1, ffn1_saved, "1")

    dy4 = _mm(dx3b, w_ssm_out, "mm_ssm_dy", trans_b=True)
    dw_ssm_out = _mm_tn(y4, dx3b, "mm_ssm_dwout")
    (dxbc, ddt, dalog, dz, dd_lane, dnw), parts_ffn1 = _ssd_bwd(
        xbc, dt, alog, sprev, dy4, y, zx, dexp, ssm_nw, "ssd_bwd", comm=scatter_of([dwtin1, dwout1]))
    dxr, dcw_s, dcb_s = _ssm_pre_bwd(zx, conv_pre, dxbc, ssm_cw8, "ssm_pre_bwd")
    ddtraw, ddtb = _dt_bwd(dtraw, dt_bias, ddt, "ssm_dt_bwd")
    dwt_ssm_in = jnp.concatenate([
        _mm_tn(dz, h2, "mm_ssm_dwin_z"), _mm_tn(dxr, h2, "mm_ssm_dwin_x"),
        _mm_tn(ddtraw, h2, "mm_ssm_dwin_dt")[:SSM_HEADS]], axis=0)
    dx2, dx2b, dg_mix1 = _mm_rms_bwd([(dz, wt_ssm_in, 0), (dxr, wt_ssm_in, D_INNER), (ddtraw, wt_ssm_dt, 0)],
                                     x2, g_mix1, dx3, "mm_ssm_dh")
    grads['ssm_conv_w'] = dcw_s[:4][None]
    grads['ssm_conv_b'] = dcb_s
    grads['ssm_norm'] = dnw
    grads['ssm_dt_bias'] = ddtb[:, :SSM_HEADS]
    grads['ssm_a_log'] = dalog[:, :SSM_HEADS]
    grads['ssm_d'] = jnp.sum(dd_lane.reshape(SSM_HEADS, HEAD_DIM), axis=1)[None]

    dx1, dx1b, dg_ffn0, dwtin0, dcw0, dcb0, dwout0 = _ffn_bwd(
        dx2, dx2b, x1, g_ffn0, wt_ffn_in0, ffn_cw8[0], w_ffn_out0, ffn0_saved, "0")
    grads['ffn_conv_w'] = jnp.stack([dcw0, dcw1])
    grads['ffn_conv_b'] = jnp.concatenate([dcb0, dcb1], axis=0)
    grads['norm_ffn'] = jnp.concatenate([dg_ffn0, dg_ffn1], axis=0)

    do = _mm(dx1b, w_attn_out, "mm_attn_do", out_dtype=BF16, trans_b=True)
    dw_attn_out = jnp.concatenate([_mm_tn(oa, dx1b, "mm_attn_dwout_a"), _mm_tn(ob, dx1b, "mm_attn_dwout_b")], axis=0)
    (dqa, dkpa, dvpa, dbias_a, _), parts_ssm = _attn_bwd(
        qa, kpa, vpa, bias_a, no_sinks, do, stats_a, oa, 0, PAD_A, "attn_a_bwd",
        comm=scatter_of([dwt_ssm_in, dw_ssm_out, dw_attn_out]))
    (dqb, dkpb, dvpb, _, dsink), parts_ffn0 = _attn_bwd(
        qb, kpb, vpb, bias_b, sinks[0], do, stats_b, ob, 4, PAD_B, "attn_b_bwd", comm=scatter_of([dwtin0, dwout0]))
    grads['relpos_table'] = _relpos_bwd(jnp.transpose(dbias_a, (1, 0, 2)), "relpos_bwd")[None, :, :2 * MAX_REL + 1]
    grads['sinks'] = dsink[:, :2, 0].reshape(1, N_HEADS)
    dproj, dhn = _headnorm_bwd(proj, hn_w, dqa, dkpa, dvpa, dqb, dkpb, dvpb, "headnorm_bwd")
    dhn = dhn[:, :HEAD_DIM] + dhn[:, HEAD_DIM:]
    for k, n in enumerate(('q_norm_a', 'k_norm_a', 'q_norm_b', 'k_norm_b')):
        grads[n] = dhn[k:k + 1]
    dwt_attn_in = _mm_tn(dproj, h0, "mm_attn_dwin")
    dx0, _, dg_mix0, parts_attn_in = _mm_rms_bwd([(dproj, wt_attn_in, 0)], x0, g_mix0, dx1, "mm_attn_dh",
                                                 comm=scatter_of([dwt_attn_in]))
    grads['norm_mix'] = jnp.concatenate([dg_mix0, dg_mix1], axis=0)

    def summed_t(parts, name):
        return _sum_parts(parts, name).T[None]

    sm_shapes = [w[n].shape for n in _SMALL]
    rp_shapes = [w[n].shape for n in _REPL]
    recv = _exchange(_Comm(
        [_pack([_split(grads[n], _SHARD_AX[n]) for n in _SMALL], F32, lead=(N_DEV,)), _pack([grads[n] for n in _REPL], F32)],
        [True, False]), "exchange_small")
    big_parts = {
        'attn_w_in': [summed_t(parts_attn_in[0], "sum_attn_w_in")], 'attn_w_out': [parts_ssm[2]],
        'ssm_w_in': [summed_t(parts_ssm[0], "sum_ssm_w_in")], 'ssm_w_out': [parts_ssm[1]],
        'ffn_w_in': [summed_t(parts_ffn0[0], "sum_ffn_w_in0"), summed_t(parts_ffn1[0], "sum_ffn_w_in1")],
        'ffn_w_out': [parts_ffn0[1], parts_ffn1[1]],
    }
    res = [{}, {}, {}, {}]
    for n in _BIG:
        for kind, a in enumerate(_adamw(big_parts[n], w[n], mom[n], var[n], f"adamw_{n}")):
            res[kind][n] = a
    for names, shapes, parts in ((_SMALL, sm_shapes, recv[0]), (_REPL, rp_shapes, recv[1])):
        outs = _adamw([parts], _pack([w[n] for n in names], F32)[None], _pack([mom[n] for n in names], F32)[None],
                      _pack([var[n] for n in names], F32)[None], "adamw_" + ("small" if names is _SMALL else "replicated"))
        for kind, flat in enumerate(outs):
            for n, a in zip(names, _unpack(flat[0], shapes)):
                res[kind][n] = a
    return (loss, dx0[None], *[res[0][n] for n in _WEIGHTS], *[res[1][n] for n in _WEIGHTS],
            *[res[2][n] for n in _WEIGHTS], *[res[3][n] for n in _WEIGHTS])
```

```python
import jax
import jax.numpy as jnp
from jax import lax
from jax.experimental import pallas as pl
from jax.experimental.pallas import tpu as pltpu

F32 = jnp.float32
BF16 = jnp.bfloat16
HI = lax.Precision.HIGHEST
MESH = pl.DeviceIdType.MESH
NEG = -1e30

N_DEV = 8
D_MODEL = 1024
EPS = 1e-6
CHUNK = 64
HEAD_DIM = 64
N_HEADS = 8
A_PREV = 8
B_PREV = 2
MAX_REL = 256
TQ = 2 * CHUNK
ATT_SUB = 16
PAD_A = A_PREV * CHUNK
PAD_B = B_PREV * CHUNK
REL_W = PAD_A + TQ
D_ATT = N_HEADS * HEAD_DIM
COL_QA, COL_KA, COL_VA, COL_QB = 0, D_ATT, 2 * D_ATT, 3 * D_ATT
COL_KB, COL_VB = 4 * D_ATT, 4 * D_ATT + 2 * HEAD_DIM
ATTN_PROJ = COL_VB + 2 * HEAD_DIM
D_INNER = 2048
SSM_HEADS = 32
SSM_GROUPS = 4
SSM_STATE = 128
XBC = D_INNER + 2 * SSM_GROUPS * SSM_STATE
ZX = D_INNER + XBC
D_FF = 2816
SSD_L = 128
SSD_L_BWD = 2 * SSD_L
LANES = 128
VMEM_LIMIT = 56 << 20

ADAM_LR, ADAM_B1, ADAM_B2, ADAM_EPS, ADAM_WD, ADAM_STEP = 0.001, 0.9, 0.999, 1e-08, 0.01, 10


def _cp(sem=None):
    return pltpu.CompilerParams(dimension_semantics=sem, vmem_limit_bytes=VMEM_LIMIT)


def _dot(a, b, ca=1, cb=0, prec=None):
    return lax.dot_general(a, b, (((ca,), (cb,)), ((), ())), preferred_element_type=F32, precision=prec)


def _pick(n, cands):
    for c in cands:
        if n % c == 0:
            return c
    return n


def _lo_mask():
    return lax.broadcasted_iota(jnp.int32, (1, LANES), 1) < HEAD_DIM


_TN_CHUNKS = (1408, 1536, 1152, 1024, 512, 256, 128)


TN_MAX_ROWS = 3072
MM_WIDE = 2304


def _mm_tn(a, b, name):
    kdim, m = a.shape
    n = b.shape[1]
    assert b.shape[0] == kdim, (a.shape, b.shape)
    mb = m if m <= TN_MAX_ROWS else m // 2
    tn = _pick(n, _TN_CHUNKS)
    tk = _pick(kdim, (512, 256, 128))
    nk = kdim // tk

    def body(a_ref, b_ref, o_ref, acc):
        k = pl.program_id(1)

        @pl.when(k == 0)
        def _():
            acc[...] = jnp.zeros_like(acc)

        av = a_ref[...]
        for c in range(0, n, tn):
            acc[:, c:c + tn] += _dot(av, b_ref[:, c:c + tn], 0, 0)

        @pl.when(k == nk - 1)
        def _():
            o_ref[...] = acc[...].astype(BF16)

    return pl.pallas_call(
        body, name=name, grid=(m // mb, nk),
        in_specs=[pl.BlockSpec((tk, mb), lambda j, k: (k, j)), pl.BlockSpec((tk, n), lambda j, k: (k, 0))],
        out_specs=pl.BlockSpec((mb, n), lambda j, k: (j, 0)), out_shape=jax.ShapeDtypeStruct((m, n), BF16),
        scratch_shapes=[pltpu.VMEM((mb, n), F32)], compiler_params=_cp(("parallel", "arbitrary")),
    )(a, b)


def _mm(a, b, name, out_dtype=F32, res=None, trans_b=False):
    m, kdim = a.shape
    n = b.shape[0] if trans_b else b.shape[1]
    assert (b.shape[1] if trans_b else b.shape[0]) == kdim, (a.shape, b.shape)
    tn = _pick(n, _TN_CHUNKS)
    tm = _pick(m, (256, 128) if n > MM_WIDE else (512, 256, 128))

    def body(*refs):
        if res is None:
            a_ref, b_ref, o_ref = refs
        else:
            a_ref, b_ref, r_ref, o_ref = refs
        av = a_ref[...]
        for c in range(0, n, tn):
            r = _dot(av, b_ref[c:c + tn, :], 1, 1) if trans_b else _dot(av, b_ref[:, c:c + tn], 1, 0)
            if res is not None:
                r = r + r_ref[:, c:c + tn]
            o_ref[:, c:c + tn] = r.astype(out_dtype)

    in_specs = [pl.BlockSpec((tm, kdim), lambda i: (i, 0)), pl.BlockSpec(b.shape, lambda i: (0, 0))]
    args = [a, b]
    if res is not None:
        in_specs.append(pl.BlockSpec((tm, n), lambda i: (i, 0)))
        args.append(res)
    return pl.pallas_call(
        body, name=name, grid=(m // tm,), in_specs=in_specs, out_specs=pl.BlockSpec((tm, n), lambda i: (i, 0)),
        out_shape=jax.ShapeDtypeStruct((m, n), out_dtype), compiler_params=_cp(("parallel",)),
    )(*args)


def _mm_pair(a1, a2, b, res, name):
    m, k1 = a1.shape
    k2 = a2.shape[1]
    n = b.shape[1]
    assert b.shape[0] == k1 + k2
    tm = _pick(m, (512, 256, 128))

    def body(a1_ref, a2_ref, b_ref, r_ref, o_ref):
        o_ref[...] = _dot(a1_ref[...], b_ref[:k1, :], 1, 0) + _dot(a2_ref[...], b_ref[k1:, :], 1, 0) + r_ref[...]

    row = pl.BlockSpec((tm, n), lambda i: (i, 0))
    return pl.pallas_call(
        body, name=name, grid=(m // tm,),
        in_specs=[pl.BlockSpec((tm, k1), lambda i: (i, 0)), pl.BlockSpec((tm, k2), lambda i: (i, 0)),
                  pl.BlockSpec(b.shape, lambda i: (0, 0)), row],
        out_specs=row, out_shape=jax.ShapeDtypeStruct((m, n), F32), compiler_params=_cp(("parallel",)),
    )(a1, a2, b, res)


def _rms_mm(x, g, bt, n, name, out_dtype):
    t, d = x.shape
    tn = _pick(n, _TN_CHUNKS)
    tm = _pick(t, (256, 128))

    def body(x_ref, g_ref, b_ref, o_ref, h_ref):
        xv = x_ref[...]
        r = lax.rsqrt(jnp.mean(xv * xv, axis=-1, keepdims=True) + EPS)
        h = (xv * r * g_ref[...]).astype(BF16)
        h_ref[...] = h
        for c in range(0, n, tn):
            o_ref[:, c:c + tn] = _dot(h, b_ref[c:c + tn, :], 1, 1).astype(out_dtype)

    row = pl.BlockSpec((tm, d), lambda i: (i, 0))
    return pl.pallas_call(
        body, name=name, grid=(t // tm,),
        in_specs=[row, pl.BlockSpec((1, d), lambda i: (0, 0)), pl.BlockSpec(bt.shape, lambda i: (0, 0))],
        out_specs=[pl.BlockSpec((tm, n), lambda i: (i, 0)), row],
        out_shape=[jax.ShapeDtypeStruct((t, n), out_dtype), jax.ShapeDtypeStruct((t, d), BF16)],
        compiler_params=_cp(("parallel",)),
    )(x, g, bt)


def _mm_rms_bwd(terms, x, g, dres, name, comm=None):
    t, d = x.shape
    tm = _pick(t, (256, 128))
    weights = []
    for _, b, _ in terms:
        if not any(b is wgt for wgt in weights):
            weights.append(b)
    which = [next(k for k, wgt in enumerate(weights) if wgt is b) for _, b, _ in terms]
    na, nw = len(terms), len(weights)

    def body(*refs):
        a_refs, w_refs = refs[:na], refs[na:na + nw]
        x_ref, g_ref, dr_ref, dx_ref, dxb_ref, dg_ref = refs[na + nw:]
        dhv = None
        for (a, _, row), a_ref, k in zip(terms, a_refs, which):
            part = _dot(a_ref[...], w_refs[k][row:row + a.shape[1], :], 1, 0)
            dhv = part if dhv is None else dhv + part
        xv = x_ref[...]
        r = lax.rsqrt(jnp.mean(xv * xv, axis=-1, keepdims=True) + EPS)
        xh = xv * r
        dxh = dhv * g_ref[...]
        dx = dr_ref[...] + r * (dxh - xh * jnp.mean(dxh * xh, axis=-1, keepdims=True))
        dx_ref[...] = dx
        dxb_ref[...] = dx.astype(BF16)

        @pl.when(pl.program_id(0) == 0)
        def _():
            dg_ref[...] = jnp.zeros_like(dg_ref)

        dg_ref[...] += jnp.sum(dhv * xh, axis=0, keepdims=True)

    row = pl.BlockSpec((tm, d), lambda i: (i, 0))
    vec = pl.BlockSpec((1, d), lambda i: (0, 0))
    in_specs = ([pl.BlockSpec((tm, a.shape[1]), lambda i: (i, 0)) for a, _, _ in terms]
                + [pl.BlockSpec(wgt.shape, lambda i: (0, 0)) for wgt in weights])
    outs, got = _call(
        body, name=name, grid=(t // tm,), in_specs=in_specs + [row, vec, row], out_specs=[row, row, vec],
        out_shape=[jax.ShapeDtypeStruct((t, d), F32), jax.ShapeDtypeStruct((t, d), BF16), jax.ShapeDtypeStruct((1, d), F32)],
        args=(*[a for a, _, _ in terms], *weights, x, g, dres), sem=("arbitrary",), comm=comm)
    return (*outs, got) if comm is not None else tuple(outs)


def _mm_loss(a, b, res, target, name):
    t, kdim = a.shape
    d = b.shape[1]
    tm = _pick(t, (512, 256, 128))

    def body(a_ref, b_ref, r_ref, t_ref, dy_ref, dyb_ref, acc_ref):
        @pl.when(pl.program_id(0) == 0)
        def _():
            acc_ref[...] = jnp.zeros_like(acc_ref)

        err = _dot(a_ref[...], b_ref[...], 1, 0) + r_ref[...] - t_ref[...]
        dy = err * (1.0 / d)
        dy_ref[...] = dy
        dyb_ref[...] = dy.astype(BF16)
        acc_ref[...] += jnp.sum(err * err, axis=0, keepdims=True)

    row = pl.BlockSpec((tm, d), lambda i: (i, 0))
    vec = pl.BlockSpec((1, d), lambda i: (0, 0))
    return pl.pallas_call(
        body, name=name, grid=(t // tm,),
        in_specs=[pl.BlockSpec((tm, kdim), lambda i: (i, 0)), pl.BlockSpec((kdim, d), lambda i: (0, 0)), row, row],
        out_specs=[row, row, vec],
        out_shape=[jax.ShapeDtypeStruct((t, d), F32), jax.ShapeDtypeStruct((t, d), BF16), jax.ShapeDtypeStruct((1, d), F32)],
        compiler_params=_cp(("arbitrary",)),
    )(a, b, res, target)


def _head_sums(v):
    ri = lax.broadcasted_iota(jnp.int32, (LANES, LANES), 0) // HEAD_DIM
    ci = lax.broadcasted_iota(jnp.int32, (LANES, LANES), 1) // HEAD_DIM
    ones = (ri == ci).astype(BF16)
    hi = v.astype(BF16)
    lo_part = (v - hi.astype(F32)).astype(BF16)
    return _dot(hi, ones, 1, 0) + _dot(lo_part, ones, 1, 0)


def _head_rms(xs):
    r = lax.rsqrt(_head_sums(xs * xs) * (1.0 / HEAD_DIM) + EPS)
    return xs * r, r


def _head_rms_bwd(xs, w, dy):
    xh, r = _head_rms(xs)
    dxh = dy * w
    mm = _head_sums(dxh * xh) * (1.0 / HEAD_DIM)
    return r * (dxh - xh * mm), dy * xh


_QSCALE = HEAD_DIM ** -0.5


def _headnorm_fwd(proj, ws, name):
    t = proj.shape[0]
    tm = TQ
    lead = PAD_A // tm
    leadb = PAD_B // tm

    def body(p_ref, w_ref, qa_ref, ka_ref, va_ref, qb_ref, kb_ref, vb_ref):
        data = pl.program_id(0) >= lead
        lo = _lo_mask()

        def put(ref, c, val):
            ref[:, c:c + val.shape[1]] = jnp.where(data, val, 0.0).astype(BF16)

        def per_query_head(slab):
            other = pltpu.roll(slab, HEAD_DIM, 1)
            e0, e1 = jnp.where(lo, slab, other), jnp.where(lo, other, slab)
            return jnp.concatenate([e0, e0, e1, e1], axis=1)

        for s in range(D_ATT // LANES):
            c = LANES * s
            xh, _ = _head_rms(p_ref[:, COL_QA + c:COL_QA + c + LANES])
            qa_ref[:, c:c + LANES] = (xh * w_ref[0:1, :] * _QSCALE).astype(BF16)
            xh, _ = _head_rms(p_ref[:, COL_KA + c:COL_KA + c + LANES])
            put(ka_ref, c, xh * w_ref[1:2, :])
            xh, _ = _head_rms(p_ref[:, COL_QB + c:COL_QB + c + LANES])
            qb_ref[:, c:c + LANES] = (xh * w_ref[2:3, :] * _QSCALE).astype(BF16)
        put(va_ref, 0, p_ref[:, COL_VA:COL_VA + D_ATT])
        xh, _ = _head_rms(p_ref[:, COL_KB:COL_KB + LANES])
        put(kb_ref, 0, per_query_head(xh * w_ref[3:4, :]))
        put(vb_ref, 0, per_query_head(p_ref[:, COL_VB:COL_VB + LANES]))

    src = lambda i: jnp.maximum(i - lead, 0)
    wide = pl.BlockSpec((tm, D_ATT), lambda i: (src(i), 0))
    pad_a = pl.BlockSpec((tm, D_ATT), lambda i: (i, 0))
    pad_b = pl.BlockSpec((tm, D_ATT), lambda i: (jnp.maximum(i - lead + leadb, 0), 0))
    sd = lambda rows: jax.ShapeDtypeStruct((rows, D_ATT), BF16)
    return pl.pallas_call(
        body, name=name, grid=(t // tm + lead,),
        in_specs=[pl.BlockSpec((tm, ATTN_PROJ), lambda i: (src(i), 0)), pl.BlockSpec((4, LANES), lambda i: (0, 0))],
        out_specs=[wide, pad_a, pad_a, wide, pad_b, pad_b],
        out_shape=[sd(t), sd(t + PAD_A), sd(t + PAD_A), sd(t), sd(t + PAD_B), sd(t + PAD_B)],
        compiler_params=_cp(("arbitrary",)),
    )(proj, ws)


def _headnorm_bwd(proj, ws, dqa, dkpa, dvpa, dqb, dkpb, dvpb, name):
    t = proj.shape[0]
    tm = TQ
    offa, offb = PAD_A // tm, PAD_B // tm

    def body(p_ref, w_ref, dqa_ref, dka_ref, dva_ref, dqb_ref, dkb_ref, dvb_ref, dp_ref, dw_ref):
        i = pl.program_id(0)
        lo = _lo_mask()

        @pl.when(i == 0)
        def _():
            dw_ref[...] = jnp.zeros_like(dw_ref)

        acc = [jnp.zeros((1, LANES), F32) for _ in range(4)]
        for s in range(D_ATT // LANES):
            c = LANES * s
            dx, dwl = _head_rms_bwd(p_ref[:, COL_QA + c:COL_QA + c + LANES], w_ref[0:1, :], dqa_ref[:, c:c + LANES] * _QSCALE)
            dp_ref[:, COL_QA + c:COL_QA + c + LANES] = dx.astype(BF16)
            acc[0] += jnp.sum(dwl, axis=0, keepdims=True)
            dx, dwl = _head_rms_bwd(p_ref[:, COL_KA + c:COL_KA + c + LANES], w_ref[1:2, :], dka_ref[:, c:c + LANES])
            dp_ref[:, COL_KA + c:COL_KA + c + LANES] = dx.astype(BF16)
            acc[1] += jnp.sum(dwl, axis=0, keepdims=True)
            dx, dwl = _head_rms_bwd(p_ref[:, COL_QB + c:COL_QB + c + LANES], w_ref[2:3, :], dqb_ref[:, c:c + LANES] * _QSCALE)
            dp_ref[:, COL_QB + c:COL_QB + c + LANES] = dx.astype(BF16)
            acc[2] += jnp.sum(dwl, axis=0, keepdims=True)
        dp_ref[:, COL_VA:COL_VA + D_ATT] = dva_ref[...].astype(BF16)

        def group_sum(ref):
            s0 = ref[:, 0:LANES] + ref[:, LANES:2 * LANES]
            s1 = ref[:, 2 * LANES:3 * LANES] + ref[:, 3 * LANES:4 * LANES]
            s0 = s0 + pltpu.roll(s0, HEAD_DIM, 1)
            s1 = s1 + pltpu.roll(s1, HEAD_DIM, 1)
            return jnp.where(lo, s0, s1)

        dx, dwl = _head_rms_bwd(p_ref[:, COL_KB:COL_KB + LANES], w_ref[3:4, :], group_sum(dkb_ref))
        dp_ref[:, COL_KB:COL_KB + LANES] = dx.astype(BF16)
        acc[3] += jnp.sum(dwl, axis=0, keepdims=True)
        dp_ref[:, COL_VB:COL_VB + LANES] = group_sum(dvb_ref).astype(BF16)
        for n in range(4):
            dw_ref[n:n + 1, :] += acc[n]

    wide = pl.BlockSpec((tm, D_ATT), lambda i: (i, 0))
    pa = pl.BlockSpec((tm, D_ATT), lambda i: (i + offa, 0))
    pb = pl.BlockSpec((tm, D_ATT), lambda i: (i + offb, 0))
    whole = pl.BlockSpec((tm, ATTN_PROJ), lambda i: (i, 0))
    return pl.pallas_call(
        body, name=name, grid=(t // tm,),
        in_specs=[whole, pl.BlockSpec((4, LANES), lambda i: (0, 0)), wide, pa, pa, wide, pb, pb],
        out_specs=[whole, pl.BlockSpec((4, LANES), lambda i: (0, 0))],
        out_shape=[jax.ShapeDtypeStruct((t, ATTN_PROJ), BF16), jax.ShapeDtypeStruct((4, LANES), F32)],
        compiler_params=_cp(("arbitrary",)),
    )(proj, ws, dqa, dkpa, dvpa, dqb, dkpb, dvpb)


ROLL_W = 1024


def _rel_onehot():
    r_io = lax.broadcasted_iota(jnp.int32, (REL_W, ROLL_W), 0)
    m_io = lax.broadcasted_iota(jnp.int32, (REL_W, ROLL_W), 1)
    return (r_io == jnp.clip(REL_W - 1 - m_io, -MAX_REL, MAX_REL) + MAX_REL).astype(F32)


def _relpos_fwd(table, name):
    def body(t_ref, o_ref):
        rr = _dot(t_ref[...], _rel_onehot(), 1, 0, HI)

        def step(q, c):
            o_ref[q] = pltpu.roll(rr, (ROLL_W - (TQ - 1) + q) % ROLL_W, 1)[:, :REL_W]
            return c

        lax.fori_loop(0, TQ, step, 0)

    return pl.pallas_call(
        body, name=name, out_shape=jax.ShapeDtypeStruct((TQ, N_HEADS, REL_W), F32),
        in_specs=[pl.BlockSpec(memory_space=pltpu.VMEM)], out_specs=pl.BlockSpec(memory_space=pltpu.VMEM),
        compiler_params=_cp(),
    )(table)


def _relpos_bwd(dbias_t, name):
    def body(d_ref, o_ref):
        def step(q, acc):
            row = jnp.concatenate([d_ref[q], jnp.zeros((N_HEADS, ROLL_W - REL_W), F32)], axis=1)
            return acc + pltpu.roll(row, TQ - 1 - q, 1)

        drr = lax.fori_loop(0, TQ, step, jnp.zeros((N_HEADS, ROLL_W), F32))
        o_ref[...] = _dot(drr, _rel_onehot(), 1, 1, HI)

    return pl.pallas_call(
        body, name=name, out_shape=jax.ShapeDtypeStruct((N_HEADS, REL_W), F32),
        in_specs=[pl.BlockSpec(memory_space=pltpu.VMEM)], out_specs=pl.BlockSpec(memory_space=pltpu.VMEM),
        compiler_params=_cp(),
    )(dbias_t)


def _attn_scores(qe, kw, bias, kvalid):
    return jnp.where(kvalid, _dot(qe, kw, 1, 1) + bias, NEG)


def _stat_cols(stats, e):
    return stats[:, 64 * e:64 * e + 1], stats[:, 64 * e + 32:64 * e + 33]


def _attn_fwd(q, kp, vp, bias, sinks, pad, name, comm=None):
    t, hd = q.shape
    w = pad + TQ

    def body(sink_ref, q_ref, k_ref, v_ref, b_ref, o_ref, st_ref):
        hp, i = pl.program_id(0), pl.program_id(1)
        lo = _lo_mask()
        lane = lax.broadcasted_iota(jnp.int32, (1, LANES), 1)
        for j in range(ATT_SUB):
            start = pl.multiple_of((i * ATT_SUB + j) * TQ, TQ)
            qv = q_ref[TQ * j:TQ * (j + 1), :]
            kw = k_ref[pl.ds(start, w), :]
            vw = v_ref[pl.ds(start, w), :]
            kvalid = (start + lax.broadcasted_iota(jnp.int32, (1, w), 1)) >= pad
            outs, ms, ls = [], [], []
            for e in range(2):
                sel = lo if e == 0 else jnp.logical_not(lo)
                qe = jnp.where(sel, qv, jnp.zeros_like(qv))
                snk = sink_ref[2 * hp + e]
                s = _attn_scores(qe, kw, b_ref[e], kvalid)
                m = jnp.maximum(jnp.max(s, axis=-1, keepdims=True), snk)
                acc = _dot(jnp.exp(s - m).astype(BF16), jnp.where(sel, vw, jnp.ones_like(vw)), 1, 0)
                denom = acc[:, 64 * (1 - e):64 * (1 - e) + 1] + jnp.exp(snk - m)
                outs.append(acc * (1.0 / denom))
                ms.append(m)
                ls.append(denom)
            o_ref[TQ * j:TQ * (j + 1), :] = jnp.where(lo, outs[0], outs[1]).astype(BF16)
            st_ref[TQ * j:TQ * (j + 1), :] = jnp.where(lane < 32, ms[0], jnp.where(lane < 64, ls[0],
                                                                                 jnp.where(lane < 96, ms[1], ls[1])))

    full = pl.BlockSpec((t + pad, LANES), lambda h, i: (0, h))
    tile = pl.BlockSpec((ATT_SUB * TQ, LANES), lambda h, i: (i, h))
    (o, stats), got = _call(
        body, name=name, grid=(hd // LANES, t // (ATT_SUB * TQ)),
        in_specs=[pl.BlockSpec(memory_space=pltpu.SMEM), tile, full, full, pl.BlockSpec((2, TQ, w), lambda h, i: (h, 0, 0))],
        out_specs=[tile, tile], out_shape=[jax.ShapeDtypeStruct((t, hd), BF16), jax.ShapeDtypeStruct((t, hd), F32)],
        args=(sinks, q, kp, vp, bias), sem=("parallel", "arbitrary"), comm=comm)
    return o, stats, got


def _attn_bwd(q, kp, vp, bias, sinks, do, stats, o, col_off, pad, name, comm=None):
    t, hd = q.shape
    w = pad + TQ
    nhp = hd // LANES

    def body(sink_ref, q_ref, k_ref, v_ref, b_ref, do_ref, st_ref, o_ref, dq_ref, dk_ref, dv_ref, db_ref, ds_ref):
        hp, i = pl.program_id(0), pl.program_id(1)

        @pl.when(i == 0)
        def _():
            dk_ref[...] = jnp.zeros_like(dk_ref)
            dv_ref[...] = jnp.zeros_like(dv_ref)
            db_ref[...] = jnp.zeros_like(db_ref)
            ds_ref[...] = jnp.zeros_like(ds_ref)

        lo = _lo_mask()
        row8 = lax.broadcasted_iota(jnp.int32, (8, LANES), 0)
        dbias = [None, None]
        dsink = jnp.zeros((8, LANES), F32)
        for j in range(ATT_SUB):
            start = pl.multiple_of((i * ATT_SUB + j) * TQ, TQ)
            qv = q_ref[TQ * j:TQ * (j + 1), :]
            dov = do_ref[TQ * j:TQ * (j + 1), :]
            kw = k_ref[pl.ds(start, w), :]
            vw = v_ref[pl.ds(start, w), :]
            kvalid = (start + lax.broadcasted_iota(jnp.int32, (1, w), 1)) >= pad
            stats = st_ref[TQ * j:TQ * (j + 1), :]
            od = dov.astype(F32) * o_ref[TQ * j:TQ * (j + 1), :].astype(F32)
            dqs, dkw, dvw = [], None, None
            for e in range(2):
                sel = lo if e == 0 else jnp.logical_not(lo)
                qe = jnp.where(sel, qv, jnp.zeros_like(qv))
                doe = jnp.where(sel, dov, jnp.zeros_like(dov))
                m, denom = _stat_cols(stats, e)
                inv = 1.0 / denom
                p = jnp.exp(_attn_scores(qe, kw, b_ref[e], kvalid) - m) * inv
                psink = jnp.exp(sink_ref[2 * hp + e] - m) * inv
                dp = _dot(doe, vw, 1, 1)
                delta = jnp.sum(jnp.where(sel, od, 0.0), axis=-1, keepdims=True)
                ds = p * (dp - delta)
                dbias[e] = ds if dbias[e] is None else dbias[e] + ds
                dsink = dsink + jnp.where(row8 == e, jnp.sum(-psink * delta, axis=0, keepdims=True), 0.0)
                dsb = ds.astype(BF16)
                dqs.append(_dot(dsb, kw, 1, 0))
                dk_e = _dot(dsb, qe, 0, 0)
                dv_e = _dot(p.astype(BF16), doe, 0, 0)
                dkw = dk_e if dkw is None else dkw + dk_e
                dvw = dv_e if dvw is None else dvw + dv_e
            dq_ref[TQ * j:TQ * (j + 1), :] = jnp.where(lo, dqs[0], dqs[1])
            dk_ref[pl.ds(start, w), :] += dkw
            dv_ref[pl.ds(start, w), :] += dvw
        for e in range(2):
            db_ref[e] += dbias[e]
        ds_ref[0] += dsink

    full = pl.BlockSpec((t + pad, LANES), lambda h, i: (0, h))
    tile = pl.BlockSpec((ATT_SUB * TQ, LANES), lambda h, i: (i, h))
    btile = pl.BlockSpec((2, TQ, w), lambda h, i: (h, 0, 0))
    return _call(
        body, name=name, grid=(nhp, t // (ATT_SUB * TQ)),
        in_specs=[pl.BlockSpec(memory_space=pltpu.SMEM), tile, full, full, btile,
                  pl.BlockSpec((ATT_SUB * TQ, LANES), lambda h, i: (i, h + col_off)), tile, tile],
        out_specs=[tile, full, full, btile, pl.BlockSpec((1, 8, LANES), lambda h, i: (h, 0, 0))],
        out_shape=[jax.ShapeDtypeStruct((t, hd), F32), jax.ShapeDtypeStruct((t + pad, hd), F32),
                   jax.ShapeDtypeStruct((t + pad, hd), F32), jax.ShapeDtypeStruct((N_HEADS, TQ, w), F32),
                   jax.ShapeDtypeStruct((nhp, 8, LANES), F32)],
        args=(sinks, q, kp, vp, bias, do, stats, o), sem=("parallel", "arbitrary"), comm=comm)


def _conv_apply(taps, w_ref, ktaps):
    out = taps[0] * w_ref[ktaps - 1:ktaps, :]
    for s in range(1, ktaps):
        out = out + taps[s] * w_ref[ktaps - 1 - s:ktaps - s, :]
    return out


def _sigmoid(x):
    return jax.nn.sigmoid(x)


def _silu_grad(x):
    sg = _sigmoid(x)
    return x * sg, sg * (1.0 + x * (1.0 - sg))


FFN_HALO = 16
FFN_BT = 256
FFN_BC = 1408


def _ffn_in_mid(x, g, wt, w8, b, name):
    t, d = x.shape
    f = D_FF
    tm = FFN_BT

    def body(x_ref, g_ref, b_ref, w_ref, cb_ref, gu_ref, h_ref, a_ref, gc_ref, halo_ref):
        @pl.when(pl.program_id(0) == 0)
        def _():
            halo_ref[...] = jnp.zeros_like(halo_ref)

        xv = x_ref[...]
        r = lax.rsqrt(jnp.mean(xv * xv, axis=-1, keepdims=True) + EPS)
        h = (xv * r * g_ref[...]).astype(BF16)
        h_ref[...] = h
        for c in range(0, f, FFN_BC):
            cs = slice(c, c + FFN_BC)
            gate = _dot(h, b_ref[c:c + FFN_BC, :], 1, 1).astype(BF16)
            up = _dot(h, b_ref[f + c:f + c + FFN_BC, :], 1, 1).astype(BF16)
            gu_ref[:, cs] = gate
            gu_ref[:, f + c:f + c + FFN_BC] = up
            gf = gate.astype(F32)
            ext = jnp.concatenate([halo_ref[:, cs], gf], axis=0)
            gc = (cb_ref[:, cs] + gf * w_ref[2:3, cs] + pltpu.roll(ext, 1, 0)[8:] * w_ref[1:2, cs]
                  + pltpu.roll(ext, 2, 0)[8:] * w_ref[0:1, cs])
            a_ref[:, cs] = (gc * _sigmoid(gc) * up.astype(F32)).astype(BF16)
            gc_ref[:, cs] = gc.astype(BF16)
            halo_ref[:, cs] = gf[tm - 8:]

    row = pl.BlockSpec((tm, d), lambda i: (i, 0))
    row_f = pl.BlockSpec((tm, f), lambda i: (i, 0))
    return pl.pallas_call(
        body, name=name, grid=(t // tm,),
        in_specs=[row, pl.BlockSpec((1, d), lambda i: (0, 0)), pl.BlockSpec((2 * f, d), lambda i: (0, 0)),
                  pl.BlockSpec((8, f), lambda i: (0, 0)), pl.BlockSpec((1, f), lambda i: (0, 0))],
        out_specs=[pl.BlockSpec((tm, 2 * f), lambda i: (i, 0)), row, row_f, row_f],
        out_shape=[jax.ShapeDtypeStruct((t, 2 * f), BF16), jax.ShapeDtypeStruct((t, d), BF16), jax.ShapeDtypeStruct((t, f), BF16),
                   jax.ShapeDtypeStruct((t, f), BF16)],
        scratch_shapes=[pltpu.VMEM((8, f), F32)], compiler_params=_cp(("arbitrary",)),
    )(x, g, wt, w8, b)


def _ffn_mid_bwd(gu, gc, dxb, w_out, w8, name):
    t, d = dxb.shape
    f = D_FF
    tm, hr = FFN_BT, FFN_HALO
    nt = t // tm
    n = tm + hr

    def body(g_ref, u_ref, un_ref, c_ref, cn_ref, dx_ref, dxn_ref, wo_ref, w_ref, dgu_ref, dw_ref, db_ref):
        i = pl.program_id(0)
        last = i == nt - 1

        @pl.when(i == 0)
        def _():
            dw_ref[...] = jnp.zeros_like(dw_ref)
            db_ref[...] = jnp.zeros_like(db_ref)

        dxe = jnp.concatenate([dx_ref[...], dxn_ref[...]], axis=0)
        row = lax.broadcasted_iota(jnp.int32, (n, 1), 0)
        keep = (row < tm) | jnp.logical_not(last)
        for c in range(0, f, FFN_BC):
            cs = slice(c, c + FFN_BC)
            act, dact = _silu_grad(jnp.concatenate([c_ref[:, cs], cn_ref[:, cs]], axis=0).astype(F32))
            da = _dot(dxe, wo_ref[cs, :], 1, 1)
            up = jnp.concatenate([u_ref[:, cs], un_ref[:, cs]], axis=0).astype(F32)
            dgc = jnp.where(keep, da * up * dact, 0.0)
            nxt = [dgc[:tm], pltpu.roll(dgc, n - 1, 0)[:tm], pltpu.roll(dgc, n - 2, 0)[:tm]]
            dgu_ref[:, f + c:f + c + FFN_BC] = (da[:tm] * act[:tm]).astype(BF16)
            dgu_ref[:, cs] = (nxt[0] * w_ref[2:3, cs] + nxt[1] * w_ref[1:2, cs] + nxt[2] * w_ref[0:1, cs]).astype(BF16)
            gate = g_ref[:, cs].astype(F32)
            db_ref[:, cs] += jnp.sum(nxt[0], axis=0, keepdims=True)
            for s in range(3):
                dw_ref[2 - s:3 - s, cs] += jnp.sum(nxt[s] * gate, axis=0, keepdims=True)

    r = tm // hr
    nxt_blk = lambda i: jnp.minimum((i + 1) * r, t // hr - 1)
    row_f = pl.BlockSpec((tm, f), lambda i: (i, 0))
    halo_f = pl.BlockSpec((hr, f), lambda i: (nxt_blk(i), 0))
    return pl.pallas_call(
        body, name=name, grid=(nt,),
        in_specs=[row_f, pl.BlockSpec((tm, f), lambda i: (i, 1)), pl.BlockSpec((hr, f), lambda i: (nxt_blk(i), 1)),
                  row_f, halo_f,
                  pl.BlockSpec((tm, d), lambda i: (i, 0)), pl.BlockSpec((hr, d), lambda i: (nxt_blk(i), 0)),
                  pl.BlockSpec((f, d), lambda i: (0, 0)), pl.BlockSpec((8, f), lambda i: (0, 0))],
        out_specs=[pl.BlockSpec((tm, 2 * f), lambda i: (i, 0)), pl.BlockSpec((8, f), lambda i: (0, 0)),
                   pl.BlockSpec((1, f), lambda i: (0, 0))],
        out_shape=[jax.ShapeDtypeStruct((t, 2 * f), BF16), jax.ShapeDtypeStruct((8, f), F32), jax.ShapeDtypeStruct((1, f), F32)],
        compiler_params=_cp(("arbitrary",)),
    )(gu, gu, gu, gc, gc, dxb, dxb, w_out, w8)


PRE_TM = 256
PRE_TC = 1024


def _softplus_heads(v):
    sp = jnp.maximum(v, 0.0) + jnp.log(1.0 + jnp.exp(-jnp.abs(v)))
    return jnp.where(lax.broadcasted_iota(jnp.int32, (1, LANES), 1) < SSM_HEADS, sp, 0.0)


def _ssm_in_pre(x, g, wt, wt_dt, w8, b, dt_bias, name):
    t, d = x.shape
    tm, tc = PRE_TM, PRE_TC

    def body(x_ref, g_ref, b_ref, bdt_ref, w_ref, cb_ref, db_ref, zx_ref, h_ref, o_ref, c_ref, dtr_ref, dt_ref, halo_ref):
        @pl.when(pl.program_id(0) == 0)
        def _():
            halo_ref[...] = jnp.zeros_like(halo_ref)

        xv = x_ref[...]
        r = lax.rsqrt(jnp.mean(xv * xv, axis=-1, keepdims=True) + EPS)
        h = (xv * r * g_ref[...]).astype(BF16)
        h_ref[...] = h
        dtr = _dot(h, bdt_ref[...], 1, 1)
        dtr_ref[...] = dtr
        dt_ref[...] = _softplus_heads(dtr + db_ref[...])
        for c in range(0, ZX, tc):
            v = _dot(h, b_ref[c:c + tc, :], 1, 1)
            zx_ref[:, c:c + tc] = v
            if c >= D_INNER:
                cs = slice(c - D_INNER, c - D_INNER + tc)
                ext = jnp.concatenate([halo_ref[:, cs], v], axis=0)
                conv = cb_ref[:, cs] + v * w_ref[3:4, cs]
                for s in (1, 2, 3):
                    conv = conv + pltpu.roll(ext, s, 0)[8:] * w_ref[3 - s:4 - s, cs]
                o_ref[:, cs] = conv * _sigmoid(conv)
                c_ref[:, cs] = conv.astype(BF16)
                halo_ref[:, cs] = v[tm - 8:]

    row = pl.BlockSpec((tm, d), lambda i: (i, 0))
    row_x = pl.BlockSpec((tm, XBC), lambda i: (i, 0))
    row_h = pl.BlockSpec((tm, LANES), lambda i: (i, 0))
    return pl.pallas_call(
        body, name=name, grid=(t // tm,),
        in_specs=[row, pl.BlockSpec((1, d), lambda i: (0, 0)), pl.BlockSpec(wt.shape, lambda i: (0, 0)),
                  pl.BlockSpec(wt_dt.shape, lambda i: (0, 0)),
                  pl.BlockSpec((8, XBC), lambda i: (0, 0)), pl.BlockSpec((1, XBC), lambda i: (0, 0)),
                  pl.BlockSpec((1, LANES), lambda i: (0, 0))],
        out_specs=[pl.BlockSpec((tm, ZX), lambda i: (i, 0)), row, row_x, row_x, row_h, row_h],
        out_shape=[jax.ShapeDtypeStruct((t, ZX), F32), jax.ShapeDtypeStruct((t, d), BF16), jax.ShapeDtypeStruct((t, XBC), F32),
                   jax.ShapeDtypeStruct((t, XBC), BF16), jax.ShapeDtypeStruct((t, LANES), F32),
                   jax.ShapeDtypeStruct((t, LANES), F32)],
        scratch_shapes=[pltpu.VMEM((8, XBC), F32)], compiler_params=_cp(("arbitrary",)),
    )(x, g, wt, wt_dt, w8, b, dt_bias)


PRE_HALO = 16


def _ssm_pre_bwd(zx, conv, dxbc, w8, name):
    t = zx.shape[0]
    tm, tc, hr = PRE_TM, PRE_TC, PRE_HALO
    off = D_INNER // tc
    nt = t // tm
    n = tm + hr

    def body(x_ref, c_ref, cn_ref, d_ref, dn_ref, w_ref, o_ref, dw_ref, db_ref):
        i = pl.program_id(1)
        last = i == nt - 1

        @pl.when(i == 0)
        def _():
            dw_ref[...] = jnp.zeros_like(dw_ref)
            db_ref[...] = jnp.zeros_like(db_ref)

        _, dact = _silu_grad(jnp.concatenate([c_ref[...], cn_ref[...]], axis=0).astype(F32))
        row = lax.broadcasted_iota(jnp.int32, (n, 1), 0)
        dc = jnp.where((row < tm) | jnp.logical_not(last), jnp.concatenate([d_ref[...], dn_ref[...]], axis=0) * dact, 0.0)
        nxt = [dc[:tm]] + [pltpu.roll(dc, n - s, 0)[:tm] for s in (1, 2, 3)]
        o_ref[...] = _conv_apply(nxt, w_ref, 4).astype(BF16)
        xv = x_ref[...]
        db_ref[...] += jnp.sum(nxt[0], axis=0, keepdims=True)
        for s in range(4):
            dw_ref[3 - s:4 - s, :] += jnp.sum(nxt[s] * xv, axis=0, keepdims=True)

    nxt_blk = lambda i: jnp.minimum((i + 1) * (tm // hr), t // hr - 1)
    tile = pl.BlockSpec((tm, tc), lambda j, i: (i, j))
    halo = pl.BlockSpec((hr, tc), lambda j, i: (nxt_blk(i), j))
    return pl.pallas_call(
        body, name=name, grid=(XBC // tc, nt),
        in_specs=[pl.BlockSpec((tm, tc), lambda j, i: (i, j + off)), tile, halo, tile, halo,
                  pl.BlockSpec((8, tc), lambda j, i: (0, j))],
        out_specs=[tile, pl.BlockSpec((8, tc), lambda j, i: (0, j)), pl.BlockSpec((1, tc), lambda j, i: (0, j))],
        out_shape=[jax.ShapeDtypeStruct((t, XBC), BF16), jax.ShapeDtypeStruct((8, XBC), F32),
                   jax.ShapeDtypeStruct((1, XBC), F32)],
        compiler_params=_cp(("parallel", "arbitrary")),
    )(zx, conv, conv, dxbc, dxbc, w8)


def _head_lanes():
    return lax.broadcasted_iota(jnp.int32, (1, LANES), 1) < SSM_HEADS


def _dt_bwd(dtraw, bias, ddt, name):
    t = dtraw.shape[0]
    tm = _pick(t, (1024, 512, 256, 128))

    def body(x_ref, b_ref, d_ref, o_ref, db_ref):
        @pl.when(pl.program_id(0) == 0)
        def _():
            db_ref[...] = jnp.zeros_like(db_ref)

        g = jnp.where(_head_lanes(), d_ref[...] * _sigmoid(x_ref[...] + b_ref[...]), 0.0)
        o_ref[...] = g.astype(BF16)
        db_ref[...] += jnp.sum(g, axis=0, keepdims=True)

    row = pl.BlockSpec((tm, LANES), lambda i: (i, 0))
    vec = pl.BlockSpec((1, LANES), lambda i: (0, 0))
    return pl.pallas_call(
        body, name=name, grid=(t // tm,), in_specs=[row, vec, row], out_specs=[row, vec],
        out_shape=[jax.ShapeDtypeStruct((t, LANES), BF16), jax.ShapeDtypeStruct((1, LANES), F32)],
        compiler_params=_cp(("arbitrary",)),
    )(dtraw, bias, ddt)


GROUP_W = D_INNER // SSM_GROUPS


def _ssd_common(dt, alog):
    ll = dt.shape[0]
    a_neg = -jnp.exp(alog)
    a = dt * a_neg
    ri = lax.broadcasted_iota(jnp.int32, (ll, ll), 0)
    ci = lax.broadcasted_iota(jnp.int32, (ll, ll), 1)
    tril = ri >= ci
    acs = _dot(tril.astype(F32), a, 1, 0, HI)
    return a_neg, tril, acs, acs.T


def _pair_terms(acs, acs_t, dt, h0, lo):
    ll = acs.shape[0]
    cols = [acs[:, h0 + e:h0 + e + 1] for e in range(2)]
    rows = [acs_t[h0 + e:h0 + e + 1, :] for e in range(2)]
    dtc = [dt[:, h0 + e:h0 + e + 1] for e in range(2)]
    lasts = [c[ll - 1:ll, :] for c in cols]
    dtx = jnp.where(lo, dtc[0], dtc[1])
    eac = jnp.where(lo, jnp.exp(cols[0]), jnp.exp(cols[1]))
    fdec = jnp.where(lo, jnp.exp(lasts[0] - cols[0]), jnp.exp(lasts[1] - cols[1]))
    elast = jnp.where(lo, jnp.exp(lasts[0]), jnp.exp(lasts[1]))
    return cols, rows, dtx, eac, fdec, elast


def _decay(col, row, tril):
    return jnp.where(tril, jnp.exp(jnp.minimum(col - row, 0.0)), 0.0)


def _two_heads_rows(v, lo):
    z = jnp.zeros_like(v)
    return jnp.concatenate([jnp.where(lo, v, z), jnp.where(lo, z, v)], axis=0)


def _two_heads_cols(ms):
    return jnp.concatenate(ms, axis=1)


def _z_group(z_refs, g):
    return z_refs[g // 2][:, GROUP_W * (g % 2):GROUP_W * (g % 2 + 1)]


def _ssd_fwd(xbc, dt, alog, zx, dexp, nw, name, comm=None):
    t = xbc.shape[0]
    ll = SSD_L
    nc = t // ll

    def body(x_ref, dt_ref, al_ref, z0_ref, z1_ref, d_ref, w_ref, y_ref, sp_ref, y4_ref, st_ref):
        @pl.when(pl.program_id(0) == 0)
        def _():
            st_ref[...] = jnp.zeros_like(st_ref)

        dtv = dt_ref[...]
        _, tril, acs, acs_t = _ssd_common(dtv, al_ref[...])
        lo = _lo_mask()
        sp_ref[0] = st_ref[...]
        for g in range(SSM_GROUPS):
            bg = x_ref[:, D_INNER + SSM_STATE * g:D_INNER + SSM_STATE * (g + 1)].astype(BF16)
            cg = x_ref[:, D_INNER + 512 + SSM_STATE * g:D_INNER + 512 + SSM_STATE * (g + 1)].astype(BF16)
            gm = _dot(cg, bg, 1, 1)
            g0 = GROUP_W * g
            terms = [_pair_terms(acs, acs_t, dtv, 8 * g + 2 * pp, lo) for pp in range(4)]
            dtx, eac, fdec, elast = [jnp.concatenate([tt[k] for tt in terms], axis=1) for k in (2, 3, 4, 5)]
            xg = x_ref[:, g0:g0 + GROUP_W]
            ug = (xg * dtx).astype(BF16)
            sg = st_ref[:, g0:g0 + GROUP_W]
            yst = _dot(cg, sg.astype(BF16), 1, 0) * eac
            st_ref[:, g0:g0 + GROUP_W] = sg * elast + _dot(bg, (xg * (fdec * dtx)).astype(BF16), 0, 0)
            ys = []
            for pp in range(4):
                cols, rows = terms[pp][0], terms[pp][1]
                sl = slice(LANES * pp, LANES * (pp + 1))
                y_in = _dot(_two_heads_cols([(gm * _decay(cols[e], rows[e], tril)).astype(BF16) for e in range(2)]),
                            _two_heads_rows(ug[:, sl], lo), 1, 0)
                ys.append(y_in + yst[:, sl])
            yg = jnp.concatenate(ys, axis=1)
            y_ref[:, g0:g0 + GROUP_W] = yg
            zg = _z_group((z0_ref, z1_ref), g)
            y3 = (yg + d_ref[:, g0:g0 + GROUP_W] * xg) * (zg * _sigmoid(zg))
            r = lax.rsqrt(jnp.mean(y3 * y3, axis=-1, keepdims=True) + EPS)
            y4_ref[:, g0:g0 + GROUP_W] = (y3 * r * w_ref[:, g0:g0 + GROUP_W]).astype(BF16)

    zblk = lambda j: pl.BlockSpec((ll, 1024), lambda c: (c, j))
    vec = pl.BlockSpec((1, D_INNER), lambda c: (0, 0))
    row = pl.BlockSpec((ll, D_INNER), lambda c: (c, 0))
    return _call(
        body, name=name, grid=(nc,),
        in_specs=[pl.BlockSpec((ll, XBC), lambda c: (c, 0)), pl.BlockSpec((ll, LANES), lambda c: (c, 0)),
                  pl.BlockSpec((1, LANES), lambda c: (0, 0)), zblk(0), zblk(1), vec, vec],
        out_specs=[row, pl.BlockSpec((1, SSM_STATE, D_INNER), lambda c: (c, 0, 0)), row],
        out_shape=[jax.ShapeDtypeStruct((t, D_INNER), F32), jax.ShapeDtypeStruct((nc, SSM_STATE, D_INNER), F32),
                   jax.ShapeDtypeStruct((t, D_INNER), BF16)],
        scratch_shapes=[pltpu.VMEM((SSM_STATE, D_INNER), F32)],
        args=(xbc, dt, alog, zx, zx, dexp, nw), sem=("arbitrary",), comm=comm)


def _ssd_bwd(xbc, dt, alog, sprev, dy4, y, zx, dexp, nw, name, comm=None):
    t = xbc.shape[0]
    ll = SSD_L_BWD if t % SSD_L_BWD == 0 else SSD_L
    nc = t // ll
    every = ll // SSD_L

    def body(x_ref, dt_ref, al_ref, sp_ref, g4_ref, y_ref, z0_ref, z1_ref, d_ref, w_ref,
             dx_ref, ddt_ref, dal_ref, dz_ref, dd_ref, dnw_ref, ds_ref, colt_ref):
        @pl.when(pl.program_id(0) == 0)
        def _():
            ds_ref[...] = jnp.zeros_like(ds_ref)
            dal_ref[...] = jnp.zeros_like(dal_ref)
            dd_ref[...] = jnp.zeros_like(dd_ref)
            dnw_ref[...] = jnp.zeros_like(dnw_ref)

        dtv = dt_ref[...]
        a_neg, tril, acs, acs_t = _ssd_common(dtv, al_ref[...])
        lo = _lo_mask()
        hi = jnp.logical_not(lo)
        lane = lax.broadcasted_iota(jnp.int32, (1, LANES), 1)
        colt_ref[...] = jnp.zeros_like(colt_ref)
        rowterm = jnp.zeros((ll, LANES), F32)
        ddt_u = jnp.zeros((ll, LANES), F32)
        dlast = jnp.zeros((1, LANES), F32)

        def halves(v):
            return (jnp.sum(jnp.where(lo, v, 0.0), axis=-1, keepdims=True),
                    jnp.sum(jnp.where(hi, v, 0.0), axis=-1, keepdims=True))

        for g in range(SSM_GROUPS):
            cb0 = D_INNER + SSM_STATE * g
            cc0 = D_INNER + 512 + SSM_STATE * g
            bg = x_ref[:, cb0:cb0 + SSM_STATE].astype(BF16)
            cg = x_ref[:, cc0:cc0 + SSM_STATE].astype(BF16)
            gm = _dot(cg, bg, 1, 1)
            g0 = GROUP_W * g
            terms = [_pair_terms(acs, acs_t, dtv, 8 * g + 2 * pp, lo) for pp in range(4)]
            dtx, eac, fdec, elast = [jnp.concatenate([tt[k] for tt in terms], axis=1) for k in (2, 3, 4, 5)]
            xg = x_ref[:, g0:g0 + GROUP_W]
            u32 = xg * dtx
            ug = u32.astype(BF16)
            zg = _z_group((z0_ref, z1_ref), g)
            dg = d_ref[:, g0:g0 + GROUP_W]
            act, dact = _silu_grad(zg)
            y2 = y_ref[:, g0:g0 + GROUP_W] + dg * xg
            y3 = y2 * act
            rn = lax.rsqrt(jnp.mean(y3 * y3, axis=-1, keepdims=True) + EPS)
            y3n = y3 * rn
            gv = g4_ref[:, g0:g0 + GROUP_W]
            dyn = gv * w_ref[:, g0:g0 + GROUP_W]
            dy3 = rn * (dyn - y3n * jnp.mean(dyn * y3n, axis=-1, keepdims=True))
            dyg = dy3 * act
            dskip = dyg * dg
            dz_ref[:, g0:g0 + GROUP_W] = (dy3 * y2 * dact).astype(BF16)
            dd_ref[:, g0:g0 + GROUP_W] += jnp.sum(dyg * xg, axis=0, keepdims=True)
            dnw_ref[:, g0:g0 + GROUP_W] += jnp.sum(gv * y3n, axis=0, keepdims=True)
            dyb = dyg.astype(BF16)
            spg = sp_ref[0, :, g0:g0 + GROUP_W]
            spb = spg.astype(BF16)
            dsg = ds_ref[:, g0:g0 + GROUP_W]
            dsb = dsg.astype(BF16)
            du_st = _dot(bg, dsb, 1, 0) * fdec
            yst = _dot(cg, spb, 1, 0) * eac
            dye = (dyg * eac).astype(BF16)
            dc_st = _dot(dye, spb, 1, 1)
            db_st = _dot((xg * (fdec * dtx)).astype(BF16), dsb, 1, 1)
            ds_ref[:, g0:g0 + GROUP_W] = dsg * elast + _dot(cg, dye, 0, 0)
            qst_el = du_st * u32
            rq_el = dyg * yst - qst_el
            q_row = jnp.sum(qst_el, axis=0, keepdims=True)
            s_row = jnp.sum(dsg * spg, axis=0, keepdims=True)
            dgm = jnp.zeros((ll, ll), F32)
            for pp in range(4):
                h0 = 8 * g + 2 * pp
                cols, rows = terms[pp][0], terms[pp][1]
                sl = slice(LANES * pp, LANES * (pp + 1))
                decs = [_decay(cols[e], rows[e], tril) for e in range(2)]
                wms = [gm * d for d in decs]
                dum2 = _dot(dyb[:, sl], _two_heads_rows(ug[:, sl], lo), 1, 1)
                du = _dot(jnp.concatenate([wm.astype(BF16) for wm in wms], axis=0),
                          _two_heads_rows(dyb[:, sl], lo), 0, 0) + du_st[:, sl]
                dx_ref[:, g0 + LANES * pp:g0 + LANES * (pp + 1)] = du * dtx[:, sl] + dskip[:, sl]
                ddtu = halves(du * xg[:, sl])
                rq = halves(rq_el[:, sl])
                qs = halves(q_row[:, sl])
                ss = halves(s_row[:, sl])
                for e in range(2):
                    dum = dum2[:, ll * e:ll * (e + 1)]
                    dgm = dgm + dum * decs[e]
                    tm_ = dum * wms[e]
                    oh = lane == (h0 + e)
                    rowterm = rowterm + jnp.where(oh, jnp.sum(tm_, axis=1, keepdims=True) + rq[e], 0.0)
                    ddt_u = ddt_u + jnp.where(oh, ddtu[e], 0.0)
                    dlast = dlast + jnp.where(oh, jnp.exp(cols[e][ll - 1:ll, :]) * ss[e] + qs[e], 0.0)
                    colt_ref[h0 + e:h0 + e + 1, :] = jnp.sum(tm_, axis=0, keepdims=True)
            dgb = dgm.astype(BF16)
            dx_ref[:, cc0:cc0 + SSM_STATE] = _dot(dgb, bg, 1, 0) + dc_st
            dx_ref[:, cb0:cb0 + SSM_STATE] = _dot(dgb, cg, 0, 0) + db_st
        row_io = lax.broadcasted_iota(jnp.int32, (ll, LANES), 0)
        dacs = rowterm - colt_ref[...].T + jnp.where(row_io == ll - 1, dlast, 0.0)
        da = _dot(jnp.logical_not(tril).astype(F32) + jnp.where(
            lax.broadcasted_iota(jnp.int32, (ll, ll), 0) == lax.broadcasted_iota(jnp.int32, (ll, ll), 1), 1.0, 0.0),
            dacs, 1, 0, HI)
        ddt_ref[...] = da * a_neg + ddt_u
        dal_ref[...] += jnp.sum(da * dtv, axis=0, keepdims=True) * a_neg

    rev = lambda c: nc - 1 - c
    row = pl.BlockSpec((ll, D_INNER), lambda c: (rev(c), 0))
    vec = pl.BlockSpec((1, D_INNER), lambda c: (0, 0))
    zblk = lambda j: pl.BlockSpec((ll, 1024), lambda c: (rev(c), j))
    return _call(
        body, name=name, grid=(nc,),
        in_specs=[pl.BlockSpec((ll, XBC), lambda c: (rev(c), 0)), pl.BlockSpec((ll, LANES), lambda c: (rev(c), 0)),
                  pl.BlockSpec((1, LANES), lambda c: (0, 0)),
                  pl.BlockSpec((1, SSM_STATE, D_INNER), lambda c: (rev(c) * every, 0, 0)), row, row, zblk(0), zblk(1), vec, vec],
        out_specs=[pl.BlockSpec((ll, XBC), lambda c: (rev(c), 0)), pl.BlockSpec((ll, LANES), lambda c: (rev(c), 0)),
                   pl.BlockSpec((1, LANES), lambda c: (0, 0)), row, vec, vec],
        out_shape=[jax.ShapeDtypeStruct((t, XBC), F32), jax.ShapeDtypeStruct((t, LANES), F32),
                   jax.ShapeDtypeStruct((1, LANES), F32), jax.ShapeDtypeStruct((t, D_INNER), BF16),
                   jax.ShapeDtypeStruct((1, D_INNER), F32), jax.ShapeDtypeStruct((1, D_INNER), F32)],
        scratch_shapes=[pltpu.VMEM((SSM_STATE, D_INNER), F32), pltpu.VMEM((LANES, ll), F32)],
        args=(xbc, dt, alog, sprev, dy4, y, zx, zx, dexp, nw), sem=("arbitrary",), comm=comm)


def _sum_parts(parts, name):
    nparts, r, c = parts.shape
    tc = _pick(c, (256, 128))

    def body(p_ref, o_ref):
        g = p_ref[0].astype(F32)
        for k in range(1, nparts):
            g = g + p_ref[k].astype(F32)
        o_ref[...] = g

    return pl.pallas_call(
        body, name=name, grid=(c // tc,), in_specs=[pl.BlockSpec((nparts, r, tc), lambda j: (0, 0, j))],
        out_specs=pl.BlockSpec((r, tc), lambda j: (0, j)), out_shape=jax.ShapeDtypeStruct((r, c), F32),
        compiler_params=_cp(("parallel",)),
    )(parts)


def _adamw(parts, w, m, v, name):
    nl, r, c = w.shape
    assert len(parts) == nl
    tr = _pick(r, (256, 128, 64))
    c1 = 1.0 - ADAM_B1 ** ADAM_STEP
    c2 = 1.0 - ADAM_B2 ** ADAM_STEP

    def body(*refs):
        p_refs = refs[:nl]
        w_ref, m_ref, v_ref, g_ref, d_ref, mo_ref, vo_ref = refs[nl:]
        g = None
        for l, p_ref in enumerate(p_refs):
            s = p_ref[0].astype(F32)
            for k in range(1, p_ref.shape[0]):
                s = s + p_ref[k].astype(F32)
            g = s if g is None else jnp.where(pl.program_id(0) == l, s, g)
        mn = ADAM_B1 * m_ref[0] + (1.0 - ADAM_B1) * g
        vn = ADAM_B2 * v_ref[0] + (1.0 - ADAM_B2) * (g * g)
        g_ref[0] = g
        mo_ref[0] = mn
        vo_ref[0] = vn
        d_ref[0] = -ADAM_LR * ((mn / c1) / (jnp.sqrt(vn / c2) + ADAM_EPS) + ADAM_WD * w_ref[0])

    row = pl.BlockSpec((1, tr, c), lambda l, i: (l, i, 0))
    sd = jax.ShapeDtypeStruct((nl, r, c), F32)
    return pl.pallas_call(
        body, name=name, grid=(nl, r // tr),
        in_specs=[pl.BlockSpec((p.shape[0], tr, c), lambda l, i: (0, i, 0)) for p in parts] + [row, row, row],
        out_specs=[row, row, row, row], out_shape=[sd, sd, sd, sd], compiler_params=_cp(("parallel", "parallel")),
    )(*parts, w, m, v)


def _peers():
    mx, my, mc = lax.axis_index("x"), lax.axis_index("y"), lax.axis_index("c")
    me = 4 * mx + 2 * my + mc
    out = []
    for k in range(1, N_DEV):
        px = 1 - mx if k & 4 else mx
        py = 1 - my if k & 2 else my
        pc = 1 - mc if k & 1 else mc
        out.append(((px, py, pc), 4 * px + 2 * py + pc))
    return me, out


class _Comm:
    def __init__(self, arrs, scatters):
        self.arrs, self.scatters, self.n = list(arrs), list(scatters), len(arrs)
        self.specs = [pl.BlockSpec(memory_space=pl.ANY)] * self.n
        self.out_shape = [jax.ShapeDtypeStruct(x.shape if sc else (N_DEV,) + x.shape, x.dtype)
                          for x, sc in zip(self.arrs, self.scatters)]
        np_ = N_DEV - 1
        self.scratch = [pltpu.SemaphoreType.DMA((np_ * self.n,)), pltpu.SemaphoreType.DMA((np_ * self.n,)),
                        pltpu.SemaphoreType.DMA((self.n,))]

    def _copies(self, x_refs, o_refs, sems):
        send_sems, recv_sems, local_sems = sems
        me, peers = _peers()
        np_ = N_DEV - 1
        local, sends, recvs = [], [], []
        for a in range(self.n):
            mine = x_refs[a].at[me] if self.scatters[a] else x_refs[a]
            local.append(pltpu.make_async_copy(mine, o_refs[a].at[me], local_sems.at[a]))
        for k, (dev, idx) in enumerate(peers):
            for a in range(self.n):
                mine = x_refs[a].at[me] if self.scatters[a] else x_refs[a]
                sends.append(pltpu.make_async_remote_copy(
                    src_ref=x_refs[a].at[idx] if self.scatters[a] else x_refs[a], dst_ref=o_refs[a].at[me],
                    send_sem=send_sems.at[a * np_ + k], recv_sem=recv_sems.at[a * np_ + k], device_id=dev, device_id_type=MESH))
                recvs.append(pltpu.make_async_remote_copy(
                    src_ref=mine, dst_ref=o_refs[a].at[idx], send_sem=send_sems.at[a * np_ + k],
                    recv_sem=recv_sems.at[a * np_ + k], device_id=dev, device_id_type=MESH))
        return local, sends, recvs

    def start(self, x_refs, o_refs, sems):
        local, sends, _ = self._copies(x_refs, o_refs, sems)
        for cp in local + sends:
            cp.start()

    def wait(self, x_refs, o_refs, sems):
        local, sends, recvs = self._copies(x_refs, o_refs, sems)
        for cp in recvs:
            cp.wait_recv()
        for cp in sends:
            cp.wait_send()
        for cp in local:
            cp.wait()


class _Gather2(_Comm):
    def __init__(self, arrs):
        super().__init__(arrs, [False] * len(arrs))

    def _plan(self, x_refs, o_refs, sems):
        send_sems, recv_sems, local_sems = sems
        mx, my, mc = lax.axis_index("x"), lax.axis_index("y"), lax.axis_index("c")
        slot = lambda px, py, pc: 4 * px + 2 * py + pc
        sib = (mx, my, 1 - mc)
        chips = [(1 - mx, my), (mx, 1 - my), (1 - mx, 1 - my)]
        np_ = N_DEV - 1
        local, first, passed, arrive_first, arrive_rest = [], [], [], [], []

        def copy(a, k, src, block, to):
            return pltpu.make_async_remote_copy(
                src_ref=src, dst_ref=o_refs[a].at[block], send_sem=send_sems.at[a * np_ + k], recv_sem=recv_sems.at[a * np_ + k],
                device_id=to, device_id_type=MESH)

        for a in range(self.n):
            me = slot(mx, my, mc)
            local.append(pltpu.make_async_copy(x_refs[a], o_refs[a].at[me], local_sems.at[a]))
            first.append(copy(a, 0, x_refs[a], me, sib))
            arrive_rest.append(copy(a, 0, x_refs[a], slot(*sib), sib))
            for j, (cx, cy) in enumerate(chips):
                first.append(copy(a, 1 + j, x_refs[a], me, (cx, cy, mc)))
                arrive_first.append(copy(a, 1 + j, x_refs[a], slot(cx, cy, mc), (cx, cy, mc)))
                passed.append(copy(a, 4 + j, o_refs[a].at[slot(cx, cy, mc)], slot(cx, cy, mc), sib))
                arrive_rest.append(copy(a, 4 + j, x_refs[a], slot(cx, cy, 1 - mc), sib))
        return local, first, passed, arrive_first, arrive_rest

    def start(self, x_refs, o_refs, sems):
        local, first, _, _, _ = self._plan(x_refs, o_refs, sems)
        for cp in local + first:
            cp.start()

    def wait(self, x_refs, o_refs, sems):
        local, first, passed, arrive_first, arrive_rest = self._plan(x_refs, o_refs, sems)
        for arrived, onward in zip(arrive_first, passed):
            arrived.wait_recv()
            onward.start()
        for cp in arrive_rest:
            cp.wait_recv()
        for cp in first + passed:
            cp.wait_send()
        for cp in local:
            cp.wait()


def _call(body, *, name, grid, in_specs, out_specs, out_shape, args, scratch_shapes=(), sem=None, comm=None):
    if comm is None:
        outs = pl.pallas_call(
            body, name=name, grid=grid, in_specs=list(in_specs), out_specs=list(out_specs), out_shape=list(out_shape),
            scratch_shapes=list(scratch_shapes), compiler_params=_cp(sem),
        )(*args)
        return list(outs), []
    n_in, n_out, nc = len(in_specs), len(out_specs), comm.n
    nsteps = 1
    for g in grid:
        nsteps *= g

    def carrier(*refs):
        ins, cin = refs[:n_in], refs[n_in:n_in + nc]
        outs, cout = refs[n_in + nc:n_in + nc + n_out], refs[n_in + nc + n_out:n_in + 2 * nc + n_out]
        rest = refs[n_in + 2 * nc + n_out:]
        scratch, sems = rest[:len(rest) - 3], rest[len(rest) - 3:]
        if nsteps == 1:
            comm.start(cin, cout, sems)
            body(*ins, *outs, *scratch)
            comm.wait(cin, cout, sems)
            return
        step = 0
        for d, g in enumerate(grid):
            step = step * g + pl.program_id(d)

        @pl.when(step == 0)
        def _():
            comm.start(cin, cout, sems)

        body(*ins, *outs, *scratch)

        @pl.when(step == nsteps - 1)
        def _():
            comm.wait(cin, cout, sems)

    outs = pl.pallas_call(
        carrier, name=name, grid=grid, in_specs=list(in_specs) + comm.specs, out_specs=list(out_specs) + comm.specs,
        out_shape=list(out_shape) + comm.out_shape, scratch_shapes=list(scratch_shapes) + comm.scratch,
        compiler_params=_cp(("arbitrary",) * len(grid) if grid else None),
    )(*args, *comm.arrs)
    return list(outs[:n_out]), list(outs[n_out:])


def _exchange(comm, name):
    return _call(lambda *refs: None, name=name, grid=(), in_specs=[], out_specs=[], out_shape=[], args=[], comm=comm)[1]


def _pack(arrs, dtype, lead=()):
    nl = len(lead)
    flat = jnp.concatenate([a.astype(dtype).reshape(lead + (-1,)) for a in arrs], axis=nl)
    n = flat.shape[-1]
    rows = -(-n // (LANES * 8)) * 8
    flat = jnp.pad(flat, [(0, 0)] * nl + [(0, rows * LANES - n)])
    return flat.reshape(lead + (rows, LANES))


def _unpack(flat, shapes, lead=()):
    nl = len(lead)
    flat = flat.reshape(lead + (-1,))
    out, o = [], 0
    for s in shapes:
        n = 1
        for d in s:
            n *= d
        out.append(lax.slice_in_dim(flat, o, o + n, axis=nl).reshape(lead + tuple(s)))
        o += n
    return out


def _join(g, ax):
    return jnp.concatenate([g[d] for d in range(N_DEV)], axis=ax)


def _split(full, ax):
    n = full.shape[ax] // N_DEV
    return jnp.stack([lax.slice_in_dim(full, d * n, (d + 1) * n, axis=ax) for d in range(N_DEV)])


_WEIGHTS = ['norm_mix', 'norm_ffn', 'attn_w_in', 'attn_w_out', 'relpos_table', 'q_norm_a', 'k_norm_a', 'q_norm_b',
            'k_norm_b', 'sinks', 'ssm_w_in', 'ssm_conv_w', 'ssm_conv_b', 'ssm_dt_bias', 'ssm_a_log', 'ssm_d', 'ssm_norm',
            'ssm_w_out', 'ffn_w_in', 'ffn_conv_w', 'ffn_conv_b', 'ffn_w_out']
_SHARD_AX = {'attn_w_in': 2, 'attn_w_out': 1, 'ssm_w_in': 2, 'ssm_conv_w': 2, 'ssm_conv_b': 1, 'ssm_norm': 1,
             'ssm_w_out': 1, 'ffn_w_in': 2, 'ffn_conv_w': 2, 'ffn_w_out': 1}
_BIG = ['attn_w_in', 'attn_w_out', 'ssm_w_in', 'ssm_w_out', 'ffn_w_in', 'ffn_w_out']
_SMALL = ['ssm_conv_w', 'ssm_conv_b', 'ssm_norm', 'ffn_conv_w']
_AX2 = {n: _SHARD_AX[n] - 1 for n in _BIG}
_REPL = [n for n in _WEIGHTS if n not in _SHARD_AX]


def _rows8(w):
    return jnp.pad(w, ((0, 8 - w.shape[0]), (0, 0)))


def _lanes128(v):
    return jnp.pad(v, (0, LANES - v.shape[0])).reshape(1, LANES)


def _band_mask(n_prev, pad):
    cq = jnp.arange(TQ)[:, None] // CHUNK
    ck = jnp.arange(pad + TQ)[None, :] // CHUNK
    return (ck >= cq) & (ck <= cq + n_prev)


def _ffn_fwd(xin, g, w_in_t, w8, cb, tag):
    gu, h, a, gc = _ffn_in_mid(xin, g, w_in_t, w8, cb, f"mm_ffn_in{tag}")
    return a, (h, gu, a, gc)


def _ffn_bwd(dx, dxb, xin, g, w_in_t, w8, w_out, saved, tag):
    h, gu, a, gc = saved
    dw_out = _mm_tn(a, dxb, f"mm_ffn_dwout{tag}")
    dgu, dw8, dcb = _ffn_mid_bwd(gu, gc, dxb, w_out, w8, f"ffn_mid_bwd{tag}")
    dw_in_t = _mm_tn(dgu, h, f"mm_ffn_dwin{tag}")
    dxp, dxpb, dg = _mm_rms_bwd([(dgu, w_in_t, 0)], xin, g, dx, f"mm_ffn_dh{tag}")
    return dxp, dxpb, dg, dw_in_t, dw8[:3], dcb, dw_out


def kernel(x, norm_mix, norm_ffn, attn_w_in, attn_w_out, relpos_table, q_norm_a, k_norm_a, q_norm_b, k_norm_b, sinks, ssm_w_in, ssm_conv_w, ssm_conv_b, ssm_dt_bias, ssm_a_log, ssm_d, ssm_norm, ssm_w_out, ffn_w_in, ffn_conv_w, ffn_conv_b, ffn_w_out, loss_target, m_norm_mix, m_norm_ffn, m_attn_w_in, m_attn_w_out, m_relpos_table, m_q_norm_a, m_k_norm_a, m_q_norm_b, m_k_norm_b, m_sinks, m_ssm_w_in, m_ssm_conv_w, m_ssm_conv_b, m_ssm_dt_bias, m_ssm_a_log, m_ssm_d, m_ssm_norm, m_ssm_w_out, m_ffn_w_in, m_ffn_conv_w, m_ffn_conv_b, m_ffn_w_out, v_norm_mix, v_norm_ffn, v_attn_w_in, v_attn_w_out, v_relpos_table, v_q_norm_a, v_k_norm_a, v_q_norm_b, v_k_norm_b, v_sinks, v_ssm_w_in, v_ssm_conv_w, v_ssm_conv_b, v_ssm_dt_bias, v_ssm_a_log, v_ssm_d, v_ssm_norm, v_ssm_w_out, v_ffn_w_in, v_ffn_conv_w, v_ffn_conv_b, v_ffn_w_out):
    w = dict(norm_mix=norm_mix, norm_ffn=norm_ffn, attn_w_in=attn_w_in, attn_w_out=attn_w_out, relpos_table=relpos_table,
             q_norm_a=q_norm_a, k_norm_a=k_norm_a, q_norm_b=q_norm_b, k_norm_b=k_norm_b, sinks=sinks, ssm_w_in=ssm_w_in,
             ssm_conv_w=ssm_conv_w, ssm_conv_b=ssm_conv_b, ssm_dt_bias=ssm_dt_bias, ssm_a_log=ssm_a_log, ssm_d=ssm_d,
             ssm_norm=ssm_norm, ssm_w_out=ssm_w_out, ffn_w_in=ffn_w_in, ffn_conv_w=ffn_conv_w, ffn_conv_b=ffn_conv_b,
             ffn_w_out=ffn_w_out)
    mom = dict(norm_mix=m_norm_mix, norm_ffn=m_norm_ffn, attn_w_in=m_attn_w_in, attn_w_out=m_attn_w_out,
               relpos_table=m_relpos_table, q_norm_a=m_q_norm_a, k_norm_a=m_k_norm_a, q_norm_b=m_q_norm_b,
               k_norm_b=m_k_norm_b, sinks=m_sinks, ssm_w_in=m_ssm_w_in, ssm_conv_w=m_ssm_conv_w, ssm_conv_b=m_ssm_conv_b,
               ssm_dt_bias=m_ssm_dt_bias, ssm_a_log=m_ssm_a_log, ssm_d=m_ssm_d, ssm_norm=m_ssm_norm, ssm_w_out=m_ssm_w_out,
               ffn_w_in=m_ffn_w_in, ffn_conv_w=m_ffn_conv_w, ffn_conv_b=m_ffn_conv_b, ffn_w_out=m_ffn_w_out)
    var = dict(norm_mix=v_norm_mix, norm_ffn=v_norm_ffn, attn_w_in=v_attn_w_in, attn_w_out=v_attn_w_out,
               relpos_table=v_relpos_table, q_norm_a=v_q_norm_a, k_norm_a=v_k_norm_a, q_norm_b=v_q_norm_b,
               k_norm_b=v_k_norm_b, sinks=v_sinks, ssm_w_in=v_ssm_w_in, ssm_conv_w=v_ssm_conv_w, ssm_conv_b=v_ssm_conv_b,
               ssm_dt_bias=v_ssm_dt_bias, ssm_a_log=v_ssm_a_log, ssm_d=v_ssm_d, ssm_norm=v_ssm_norm, ssm_w_out=v_ssm_w_out,
               ffn_w_in=v_ffn_w_in, ffn_conv_w=v_ffn_conv_w, ffn_conv_b=v_ffn_conv_b, ffn_w_out=v_ffn_w_out)

    def piece(n, l):
        return (w[n][l].T if _AX2[n] == 1 else w[n][l]).astype(BF16)

    def gather_of(names_layers):
        return _Gather2([piece(n, l) for n, l in names_layers])

    def joined(got):
        return [g.reshape(-1, D_MODEL) for g in got]

    first = [('attn_w_in', 0), ('attn_w_out', 0)]
    got = _exchange(_Gather2([piece(n, l) for n, l in first] + [_pack([w[n] for n in _SMALL], F32)]), "gather_attn")
    wt_attn_in, w_attn_out = joined(got[:2])
    full = {}
    for n, g in zip(_SMALL, _unpack(got[2], [w[n].shape for n in _SMALL], lead=(N_DEV,))):
        full[n] = _join(g, _SHARD_AX[n])
    ssm_cw8 = _rows8(full['ssm_conv_w'][0])
    ssm_cb = full['ssm_conv_b']
    ssm_nw = full['ssm_norm']
    ffn_cw8 = [_rows8(full['ffn_conv_w'][l]) for l in range(2)]
    ffn_cb = [ffn_conv_b[l:l + 1] for l in range(2)]

    x0 = x[0]
    target = loss_target[0]
    t = x0.shape[0]

    g_mix0, g_mix1 = norm_mix[0:1], norm_mix[1:2]
    g_ffn0, g_ffn1 = norm_ffn[0:1], norm_ffn[1:2]
    proj, h0 = _rms_mm(x0, g_mix0, wt_attn_in, ATTN_PROJ, "mm_attn_in", F32)
    hn_w = jnp.concatenate([jnp.tile(v, (1, 2)) for v in (q_norm_a, k_norm_a, q_norm_b, k_norm_b)], axis=0)
    qa, kpa, vpa, qb, kpb, vpb = _headnorm_fwd(proj, hn_w, "headnorm")
    table = jnp.pad(relpos_table[0], ((0, 0), (0, REL_W - (2 * MAX_REL + 1))))
    bias_a = jnp.where(_band_mask(A_PREV, PAD_A)[None], jnp.transpose(_relpos_fwd(table, "relpos_bias"), (1, 0, 2)), NEG)
    rel_b = jnp.arange(TQ)[:, None] - (jnp.arange(PAD_B + TQ)[None, :] - PAD_B)
    slopes = 2.0 ** (-8.0 * jnp.arange(1, N_HEADS + 1, dtype=F32) / N_HEADS)
    bias_b = jnp.where(_band_mask(B_PREV, PAD_B)[None], -slopes[:, None, None] * jnp.abs(rel_b).astype(F32)[None], NEG)
    no_sinks = jnp.full((N_HEADS,), NEG, F32)
    ffn0_w, ssm_w, ffn1_w = [('ffn_w_in', 0), ('ffn_w_out', 0)], [('ssm_w_in', 0), ('ssm_w_out', 0)], [('ffn_w_in', 1), ('ffn_w_out', 1)]
    oa, stats_a, got = _attn_fwd(qa, kpa, vpa, bias_a, no_sinks, PAD_A, "attn_a", comm=gather_of(ffn0_w + ssm_w))
    wt_ffn_in0, w_ffn_out0, wt_ssm_in, w_ssm_out = joined(got)
    ob, stats_b, _ = _attn_fwd(qb, kpb, vpb, bias_b, sinks[0], PAD_B, "attn_b")
    wt_ssm_dt = jnp.pad(wt_ssm_in[ZX:], ((0, LANES - SSM_HEADS), (0, 0)))
    x1 = _mm_pair(oa, ob, w_attn_out, x0, "mm_attn_out")
    a0, ffn0_saved = _ffn_fwd(x1, g_ffn0, wt_ffn_in0, ffn_cw8[0], ffn_cb[0], "0")
    x2 = _mm(a0, w_ffn_out0, "mm_ffn_out0", res=x1)

    dt_bias = _lanes128(ssm_dt_bias[0])
    alog = _lanes128(ssm_a_log[0])
    dexp = jnp.repeat(ssm_d[0], HEAD_DIM).reshape(1, D_INNER)
    zx, h2, xbc, conv_pre, dtraw, dt = _ssm_in_pre(x2, g_mix1, wt_ssm_in, wt_ssm_dt, ssm_cw8, ssm_cb, dt_bias, "mm_ssm_in")
    (y, sprev, y4), got = _ssd_fwd(xbc, dt, alog, zx, dexp, ssm_nw, "ssd_fwd", comm=gather_of(ffn1_w))
    wt_ffn_in1, w_ffn_out1 = joined(got)
    x3 = _mm(y4, w_ssm_out, "mm_ssm_out", res=x2)
    a1, ffn1_saved = _ffn_fwd(x3, g_ffn1, wt_ffn_in1, ffn_cw8[1], ffn_cb[1], "1")

    dx4, dx4b, sq = _mm_loss(a1, w_ffn_out1, x3, target, "mm_ffn_out1_loss")
    loss = lax.psum(0.5 * jnp.sum(sq) / D_MODEL, ("x", "y", "c"))

    grads = {}

    def scatter_of(grads_2d):
        return _Comm([g.reshape(N_DEV, -1, D_MODEL) for g in grads_2d], [True] * len(grads_2d))

    dx3, dx3b, dg_ffn1, dwtin1, dcw1, dcb1, dwout1 = _ffn_bwd(
        dx4, dx4b, x3, g_ffn1, wt_ffn_in1, ffn_cw8[1], w_ffn_out1, ffn1_saved, "1")

    dy4 = _mm(dx3b, w_ssm_out, "mm_ssm_dy", trans_b=True)
    dw_ssm_out = _mm_tn(y4, dx3b, "mm_ssm_dwout")
    (dxbc, ddt, dalog, dz, dd_lane, dnw), parts_ffn1 = _ssd_bwd(
        xbc, dt, alog, sprev, dy4, y, zx, dexp, ssm_nw, "ssd_bwd", comm=scatter_of([dwtin1, dwout1]))
    dxr, dcw_s, dcb_s = _ssm_pre_bwd(zx, conv_pre, dxbc, ssm_cw8, "ssm_pre_bwd")
    ddtraw, ddtb = _dt_bwd(dtraw, dt_bias, ddt, "ssm_dt_bwd")
    dwt_ssm_in = jnp.concatenate([
        _mm_tn(dz, h2, "mm_ssm_dwin_z"), _mm_tn(dxr, h2, "mm_ssm_dwin_x"),
        _mm_tn(ddtraw, h2, "mm_ssm_dwin_dt")[:SSM_HEADS]], axis=0)
    dx2, dx2b, dg_mix1 = _mm_rms_bwd([(dz, wt_ssm_in, 0), (dxr, wt_ssm_in, D_INNER), (ddtraw, wt_ssm_dt, 0)],
                                     x2, g_mix1, dx3, "mm_ssm_dh")
    grads['ssm_conv_w'] = dcw_s[:4][None]
    grads['ssm_conv_b'] = dcb_s
    grads['ssm_norm'] = dnw
    grads['ssm_dt_bias'] = ddtb[:, :SSM_HEADS]
    grads['ssm_a_log'] = dalog[:, :SSM_HEADS]
    grads['ssm_d'] = jnp.sum(dd_lane.reshape(SSM_HEADS, HEAD_DIM), axis=1)[None]

    dx1, dx1b, dg_ffn0, dwtin0, dcw0, dcb0, dwout0 = _ffn_bwd(
        dx2, dx2b, x1, g_ffn0, wt_ffn_in0, ffn_cw8[0], w_ffn_out0, ffn0_saved, "0")
    grads['ffn_conv_w'] = jnp.stack([dcw0, dcw1])
    grads['ffn_conv_b'] = jnp.concatenate([dcb0, dcb1], axis=0)
    grads['norm_ffn'] = jnp.concatenate([dg_ffn0, dg_ffn1], axis=0)

    do = _mm(dx1b, w_attn_out, "mm_attn_do", out_dtype=BF16, trans_b=True)
    dw_attn_out = jnp.concatenate([_mm_tn(oa, dx1b, "mm_attn_dwout_a"), _mm_tn(ob, dx1b, "mm_attn_dwout_b")], axis=0)
    (dqa, dkpa, dvpa, dbias_a, _), parts_ssm = _attn_bwd(
        qa, kpa, vpa, bias_a, no_sinks, do, stats_a, oa, 0, PAD_A, "attn_a_bwd",
        comm=scatter_of([dwt_ssm_in, dw_ssm_out, dw_attn_out]))
    (dqb, dkpb, dvpb, _, dsink), parts_ffn0 = _attn_bwd(
        qb, kpb, vpb, bias_b, sinks[0], do, stats_b, ob, 4, PAD_B, "attn_b_bwd", comm=scatter_of([dwtin0, dwout0]))
    grads['relpos_table'] = _relpos_bwd(jnp.transpose(dbias_a, (1, 0, 2)), "relpos_bwd")[None, :, :2 * MAX_REL + 1]
    grads['sinks'] = dsink[:, :2, 0].reshape(1, N_HEADS)
    dproj, dhn = _headnorm_bwd(proj, hn_w, dqa, dkpa, dvpa, dqb, dkpb, dvpb, "headnorm_bwd")
    dhn = dhn[:, :HEAD_DIM] + dhn[:, HEAD_DIM:]
    for k, n in enumerate(('q_norm_a', 'k_norm_a', 'q_norm_b', 'k_norm_b')):
        grads[n] = dhn[k:k + 1]
    dwt_attn_in = _mm_tn(dproj, h0, "mm_attn_dwin")
    dx0, _, dg_mix0, parts_attn_in = _mm_rms_bwd([(dproj, wt_attn_in, 0)], x0, g_mix0, dx1, "mm_attn_dh",
                                                 comm=scatter_of([dwt_attn_in]))
    grads['norm_mix'] = jnp.concatenate([dg_mix0, dg_mix1], axis=0)

    def summed_t(parts, name):
        return _sum_parts(parts, name).T[None]

    sm_shapes = [w[n].shape for n in _SMALL]
    rp_shapes = [w[n].shape for n in _REPL]
    recv = _exchange(_Comm(
        [_pack([_split(grads[n], _SHARD_AX[n]) for n in _SMALL], F32, lead=(N_DEV,)), _pack([grads[n] for n in _REPL], F32)],
        [True, False]), "exchange_small")
    big_parts = {
        'attn_w_in': [summed_t(parts_attn_in[0], "sum_attn_w_in")], 'attn_w_out': [parts_ssm[2]],
        'ssm_w_in': [summed_t(parts_ssm[0], "sum_ssm_w_in")], 'ssm_w_out': [parts_ssm[1]],
        'ffn_w_in': [summed_t(parts_ffn0[0], "sum_ffn_w_in0"), summed_t(parts_ffn1[0], "sum_ffn_w_in1")],
        'ffn_w_out': [parts_ffn0[1], parts_ffn1[1]],
    }
    res = [{}, {}, {}, {}]
    for n in _BIG:
        for kind, a in enumerate(_adamw(big_parts[n], w[n], mom[n], var[n], f"adamw_{n}")):
            res[kind][n] = a
    for names, shapes, parts in ((_SMALL, sm_shapes, recv[0]), (_REPL, rp_shapes, recv[1])):
        outs = _adamw([parts], _pack([w[n] for n in names], F32)[None], _pack([mom[n] for n in names], F32)[None],
                      _pack([var[n] for n in names], F32)[None], "adamw_" + ("small" if names is _SMALL else "replicated"))
        for kind, flat in enumerate(outs):
            for n, a in zip(names, _unpack(flat[0], shapes)):
                res[kind][n] = a
    return (loss, dx0[None], *[res[0][n] for n in _WEIGHTS], *[res[1][n] for n in _WEIGHTS],
            *[res[2][n] for n in _WEIGHTS], *[res[3][n] for n in _WEIGHTS])
```

```python
import jax
import jax.numpy as jnp
from jax import lax
from jax.experimental import pallas as pl
from jax.experimental.pallas import tpu as pltpu

F32 = jnp.float32
BF16 = jnp.bfloat16
HI = lax.Precision.HIGHEST
MESH = pl.DeviceIdType.MESH
NEG = -1e30

N_DEV = 8
D_MODEL = 1024
EPS = 1e-6
CHUNK = 64
HEAD_DIM = 64
N_HEADS = 8
A_PREV = 8
B_PREV = 2
MAX_REL = 256
TQ = 2 * CHUNK
ATT_SUB = 32
PAD_A = A_PREV * CHUNK
PAD_B = B_PREV * CHUNK
REL_W = PAD_A + TQ
D_ATT = N_HEADS * HEAD_DIM
COL_QA, COL_KA, COL_VA, COL_QB = 0, D_ATT, 2 * D_ATT, 3 * D_ATT
COL_KB, COL_VB = 4 * D_ATT, 4 * D_ATT + 2 * HEAD_DIM
ATTN_PROJ = COL_VB + 2 * HEAD_DIM
D_INNER = 2048
SSM_HEADS = 32
SSM_GROUPS = 4
SSM_STATE = 128
XBC = D_INNER + 2 * SSM_GROUPS * SSM_STATE
ZX = D_INNER + XBC
D_FF = 2816
SSD_L = 128
SSD_L_BWD = 2 * SSD_L
LANES = 128
VMEM_LIMIT = 56 << 20

ADAM_LR, ADAM_B1, ADAM_B2, ADAM_EPS, ADAM_WD, ADAM_STEP = 0.001, 0.9, 0.999, 1e-08, 0.01, 10


def _cp(sem=None):
    return pltpu.CompilerParams(dimension_semantics=sem, vmem_limit_bytes=VMEM_LIMIT)


def _dot(a, b, ca=1, cb=0, prec=None):
    return lax.dot_general(a, b, (((ca,), (cb,)), ((), ())), preferred_element_type=F32, precision=prec)


def _pick(n, cands):
    for c in cands:
        if n % c == 0:
            return c
    return n


def _lo_mask():
    return lax.broadcasted_iota(jnp.int32, (1, LANES), 1) < HEAD_DIM


_TN_CHUNKS = (1408, 1536, 1152, 1024, 512, 256, 128)


TN_MAX_ROWS = 3072
MM_WIDE = 2304


def _mm_tn(a, b, name):
    kdim, m = a.shape
    n = b.shape[1]
    assert b.shape[0] == kdim, (a.shape, b.shape)
    mb = m if m <= TN_MAX_ROWS else m // 2
    tn = _pick(n, _TN_CHUNKS)
    tk = _pick(kdim, (512, 256, 128))
    nk = kdim // tk

    def body(a_ref, b_ref, o_ref, acc):
        k = pl.program_id(1)

        @pl.when(k == 0)
        def _():
            acc[...] = jnp.zeros_like(acc)

        av = a_ref[...]
        for c in range(0, n, tn):
            acc[:, c:c + tn] += _dot(av, b_ref[:, c:c + tn], 0, 0)

        @pl.when(k == nk - 1)
        def _():
            o_ref[...] = acc[...].astype(BF16)

    return pl.pallas_call(
        body, name=name, grid=(m // mb, nk),
        in_specs=[pl.BlockSpec((tk, mb), lambda j, k: (k, j)), pl.BlockSpec((tk, n), lambda j, k: (k, 0))],
        out_specs=pl.BlockSpec((mb, n), lambda j, k: (j, 0)), out_shape=jax.ShapeDtypeStruct((m, n), BF16),
        scratch_shapes=[pltpu.VMEM((mb, n), F32)], compiler_params=_cp(("parallel", "arbitrary")),
    )(a, b)


def _mm(a, b, name, out_dtype=F32, res=None, trans_b=False):
    m, kdim = a.shape
    n = b.shape[0] if trans_b else b.shape[1]
    assert (b.shape[1] if trans_b else b.shape[0]) == kdim, (a.shape, b.shape)
    tn = _pick(n, _TN_CHUNKS)
    tm = _pick(m, (256, 128) if n > MM_WIDE else (512, 256, 128))

    def body(*refs):
        if res is None:
            a_ref, b_ref, o_ref = refs
        else:
            a_ref, b_ref, r_ref, o_ref = refs
        av = a_ref[...]
        for c in range(0, n, tn):
            r = _dot(av, b_ref[c:c + tn, :], 1, 1) if trans_b else _dot(av, b_ref[:, c:c + tn], 1, 0)
            if res is not None:
                r = r + r_ref[:, c:c + tn]
            o_ref[:, c:c + tn] = r.astype(out_dtype)

    in_specs = [pl.BlockSpec((tm, kdim), lambda i: (i, 0)), pl.BlockSpec(b.shape, lambda i: (0, 0))]
    args = [a, b]
    if res is not None:
        in_specs.append(pl.BlockSpec((tm, n), lambda i: (i, 0)))
        args.append(res)
    return pl.pallas_call(
        body, name=name, grid=(m // tm,), in_specs=in_specs, out_specs=pl.BlockSpec((tm, n), lambda i: (i, 0)),
        out_shape=jax.ShapeDtypeStruct((m, n), out_dtype), compiler_params=_cp(("parallel",)),
    )(*args)


def _mm_pair(a1, a2, b, res, name):
    m, k1 = a1.shape
    k2 = a2.shape[1]
    n = b.shape[1]
    assert b.shape[0] == k1 + k2
    tm = _pick(m, (512, 256, 128))

    def body(a1_ref, a2_ref, b_ref, r_ref, o_ref):
        o_ref[...] = _dot(a1_ref[...], b_ref[:k1, :], 1, 0) + _dot(a2_ref[...], b_ref[k1:, :], 1, 0) + r_ref[...]

    row = pl.BlockSpec((tm, n), lambda i: (i, 0))
    return pl.pallas_call(
        body, name=name, grid=(m // tm,),
        in_specs=[pl.BlockSpec((tm, k1), lambda i: (i, 0)), pl.BlockSpec((tm, k2), lambda i: (i, 0)),
                  pl.BlockSpec(b.shape, lambda i: (0, 0)), row],
        out_specs=row, out_shape=jax.ShapeDtypeStruct((m, n), F32), compiler_params=_cp(("parallel",)),
    )(a1, a2, b, res)


def _rms_mm(x, g, bt, n, name, out_dtype):
    t, d = x.shape
    tn = _pick(n, _TN_CHUNKS)
    tm = _pick(t, (256, 128))

    def body(x_ref, g_ref, b_ref, o_ref, h_ref):
        xv = x_ref[...]
        r = lax.rsqrt(jnp.mean(xv * xv, axis=-1, keepdims=True) + EPS)
        h = (xv * r * g_ref[...]).astype(BF16)
        h_ref[...] = h
        for c in range(0, n, tn):
            o_ref[:, c:c + tn] = _dot(h, b_ref[c:c + tn, :], 1, 1).astype(out_dtype)

    row = pl.BlockSpec((tm, d), lambda i: (i, 0))
    return pl.pallas_call(
        body, name=name, grid=(t // tm,),
        in_specs=[row, pl.BlockSpec((1, d), lambda i: (0, 0)), pl.BlockSpec(bt.shape, lambda i: (0, 0))],
        out_specs=[pl.BlockSpec((tm, n), lambda i: (i, 0)), row],
        out_shape=[jax.ShapeDtypeStruct((t, n), out_dtype), jax.ShapeDtypeStruct((t, d), BF16)],
        compiler_params=_cp(("parallel",)),
    )(x, g, bt)


def _mm_rms_bwd(terms, x, g, dres, name, comm=None):
    t, d = x.shape
    tm = _pick(t, (256, 128))
    weights = []
    for _, b, _ in terms:
        if not any(b is wgt for wgt in weights):
            weights.append(b)
    which = [next(k for k, wgt in enumerate(weights) if wgt is b) for _, b, _ in terms]
    na, nw = len(terms), len(weights)

    def body(*refs):
        a_refs, w_refs = refs[:na], refs[na:na + nw]
        x_ref, g_ref, dr_ref, dx_ref, dxb_ref, dg_ref = refs[na + nw:]
        dhv = None
        for (a, _, row), a_ref, k in zip(terms, a_refs, which):
            part = _dot(a_ref[...], w_refs[k][row:row + a.shape[1], :], 1, 0)
            dhv = part if dhv is None else dhv + part
        xv = x_ref[...]
        r = lax.rsqrt(jnp.mean(xv * xv, axis=-1, keepdims=True) + EPS)
        xh = xv * r
        dxh = dhv * g_ref[...]
        dx = dr_ref[...] + r * (dxh - xh * jnp.mean(dxh * xh, axis=-1, keepdims=True))
        dx_ref[...] = dx
        dxb_ref[...] = dx.astype(BF16)

        @pl.when(pl.program_id(0) == 0)
        def _():
            dg_ref[...] = jnp.zeros_like(dg_ref)

        dg_ref[...] += jnp.sum(dhv * xh, axis=0, keepdims=True)

    row = pl.BlockSpec((tm, d), lambda i: (i, 0))
    vec = pl.BlockSpec((1, d), lambda i: (0, 0))
    in_specs = ([pl.BlockSpec((tm, a.shape[1]), lambda i: (i, 0)) for a, _, _ in terms]
                + [pl.BlockSpec(wgt.shape, lambda i: (0, 0)) for wgt in weights])
    outs, got = _call(
        body, name=name, grid=(t // tm,), in_specs=in_specs + [row, vec, row], out_specs=[row, row, vec],
        out_shape=[jax.ShapeDtypeStruct((t, d), F32), jax.ShapeDtypeStruct((t, d), BF16), jax.ShapeDtypeStruct((1, d), F32)],
        args=(*[a for a, _, _ in terms], *weights, x, g, dres), sem=("arbitrary",), comm=comm)
    return (*outs, got) if comm is not None else tuple(outs)


def _mm_loss(a, b, res, target, name):
    t, kdim = a.shape
    d = b.shape[1]
    tm = _pick(t, (512, 256, 128))

    def body(a_ref, b_ref, r_ref, t_ref, dy_ref, dyb_ref, acc_ref):
        @pl.when(pl.program_id(0) == 0)
        def _():
            acc_ref[...] = jnp.zeros_like(acc_ref)

        err = _dot(a_ref[...], b_ref[...], 1, 0) + r_ref[...] - t_ref[...]
        dy = err * (1.0 / d)
        dy_ref[...] = dy
        dyb_ref[...] = dy.astype(BF16)
        acc_ref[...] += jnp.sum(err * err, axis=0, keepdims=True)

    row = pl.BlockSpec((tm, d), lambda i: (i, 0))
    vec = pl.BlockSpec((1, d), lambda i: (0, 0))
    return pl.pallas_call(
        body, name=name, grid=(t // tm,),
        in_specs=[pl.BlockSpec((tm, kdim), lambda i: (i, 0)), pl.BlockSpec((kdim, d), lambda i: (0, 0)), row, row],
        out_specs=[row, row, vec],
        out_shape=[jax.ShapeDtypeStruct((t, d), F32), jax.ShapeDtypeStruct((t, d), BF16), jax.ShapeDtypeStruct((1, d), F32)],
        compiler_params=_cp(("arbitrary",)),
    )(a, b, res, target)


def _head_sums(v):
    ri = lax.broadcasted_iota(jnp.int32, (LANES, LANES), 0) // HEAD_DIM
    ci = lax.broadcasted_iota(jnp.int32, (LANES, LANES), 1) // HEAD_DIM
    ones = (ri == ci).astype(BF16)
    hi = v.astype(BF16)
    lo_part = (v - hi.astype(F32)).astype(BF16)
    return _dot(hi, ones, 1, 0) + _dot(lo_part, ones, 1, 0)


def _head_rms(xs):
    r = lax.rsqrt(_head_sums(xs * xs) * (1.0 / HEAD_DIM) + EPS)
    return xs * r, r


def _head_rms_bwd(xs, w, dy):
    xh, r = _head_rms(xs)
    dxh = dy * w
    mm = _head_sums(dxh * xh) * (1.0 / HEAD_DIM)
    return r * (dxh - xh * mm), dy * xh


_QSCALE = HEAD_DIM ** -0.5


def _headnorm_fwd(proj, ws, name):
    t = proj.shape[0]
    tm = TQ
    lead = PAD_A // tm
    leadb = PAD_B // tm

    def body(p_ref, w_ref, qa_ref, ka_ref, va_ref, qb_ref, kb_ref, vb_ref):
        data = pl.program_id(0) >= lead
        lo = _lo_mask()

        def put(ref, c, val):
            ref[:, c:c + val.shape[1]] = jnp.where(data, val, 0.0).astype(BF16)

        def per_query_head(slab):
            other = pltpu.roll(slab, HEAD_DIM, 1)
            e0, e1 = jnp.where(lo, slab, other), jnp.where(lo, other, slab)
            return jnp.concatenate([e0, e0, e1, e1], axis=1)

        for s in range(D_ATT // LANES):
            c = LANES * s
            xh, _ = _head_rms(p_ref[:, COL_QA + c:COL_QA + c + LANES])
            qa_ref[:, c:c + LANES] = (xh * w_ref[0:1, :] * _QSCALE).astype(BF16)
            xh, _ = _head_rms(p_ref[:, COL_KA + c:COL_KA + c + LANES])
            put(ka_ref, c, xh * w_ref[1:2, :])
            xh, _ = _head_rms(p_ref[:, COL_QB + c:COL_QB + c + LANES])
            qb_ref[:, c:c + LANES] = (xh * w_ref[2:3, :] * _QSCALE).astype(BF16)
        put(va_ref, 0, p_ref[:, COL_VA:COL_VA + D_ATT])
        xh, _ = _head_rms(p_ref[:, COL_KB:COL_KB + LANES])
        put(kb_ref, 0, per_query_head(xh * w_ref[3:4, :]))
        put(vb_ref, 0, per_query_head(p_ref[:, COL_VB:COL_VB + LANES]))

    src = lambda i: jnp.maximum(i - lead, 0)
    wide = pl.BlockSpec((tm, D_ATT), lambda i: (src(i), 0))
    pad_a = pl.BlockSpec((tm, D_ATT), lambda i: (i, 0))
    pad_b = pl.BlockSpec((tm, D_ATT), lambda i: (jnp.maximum(i - lead + leadb, 0), 0))
    sd = lambda rows: jax.ShapeDtypeStruct((rows, D_ATT), BF16)
    return pl.pallas_call(
        body, name=name, grid=(t // tm + lead,),
        in_specs=[pl.BlockSpec((tm, ATTN_PROJ), lambda i: (src(i), 0)), pl.BlockSpec((4, LANES), lambda i: (0, 0))],
        out_specs=[wide, pad_a, pad_a, wide, pad_b, pad_b],
        out_shape=[sd(t), sd(t + PAD_A), sd(t + PAD_A), sd(t), sd(t + PAD_B), sd(t + PAD_B)],
        compiler_params=_cp(("arbitrary",)),
    )(proj, ws)


def _headnorm_bwd(proj, ws, dqa, dkpa, dvpa, dqb, dkpb, dvpb, name):
    t = proj.shape[0]
    tm = TQ
    offa, offb = PAD_A // tm, PAD_B // tm

    def body(p_ref, w_ref, dqa_ref, dka_ref, dva_ref, dqb_ref, dkb_ref, dvb_ref, dp_ref, dw_ref):
        i = pl.program_id(0)
        lo = _lo_mask()

        @pl.when(i == 0)
        def _():
            dw_ref[...] = jnp.zeros_like(dw_ref)

        acc = [jnp.zeros((1, LANES), F32) for _ in range(4)]
        for s in range(D_ATT // LANES):
            c = LANES * s
            dx, dwl = _head_rms_bwd(p_ref[:, COL_QA + c:COL_QA + c + LANES], w_ref[0:1, :], dqa_ref[:, c:c + LANES] * _QSCALE)
            dp_ref[:, COL_QA + c:COL_QA + c + LANES] = dx.astype(BF16)
            acc[0] += jnp.sum(dwl, axis=0, keepdims=True)
            dx, dwl = _head_rms_bwd(p_ref[:, COL_KA + c:COL_KA + c + LANES], w_ref[1:2, :], dka_ref[:, c:c + LANES])
            dp_ref[:, COL_KA + c:COL_KA + c + LANES] = dx.astype(BF16)
            acc[1] += jnp.sum(dwl, axis=0, keepdims=True)
            dx, dwl = _head_rms_bwd(p_ref[:, COL_QB + c:COL_QB + c + LANES], w_ref[2:3, :], dqb_ref[:, c:c + LANES] * _QSCALE)
            dp_ref[:, COL_QB + c:COL_QB + c + LANES] = dx.astype(BF16)
            acc[2] += jnp.sum(dwl, axis=0, keepdims=True)
        dp_ref[:, COL_VA:COL_VA + D_ATT] = dva_ref[...].astype(BF16)

        def group_sum(ref):
            s0 = ref[:, 0:LANES] + ref[:, LANES:2 * LANES]
            s1 = ref[:, 2 * LANES:3 * LANES] + ref[:, 3 * LANES:4 * LANES]
            s0 = s0 + pltpu.roll(s0, HEAD_DIM, 1)
            s1 = s1 + pltpu.roll(s1, HEAD_DIM, 1)
            return jnp.where(lo, s0, s1)

        dx, dwl = _head_rms_bwd(p_ref[:, COL_KB:COL_KB + LANES], w_ref[3:4, :], group_sum(dkb_ref))
        dp_ref[:, COL_KB:COL_KB + LANES] = dx.astype(BF16)
        acc[3] += jnp.sum(dwl, axis=0, keepdims=True)
        dp_ref[:, COL_VB:COL_VB + LANES] = group_sum(dvb_ref).astype(BF16)
        for n in range(4):
            dw_ref[n:n + 1, :] += acc[n]

    wide = pl.BlockSpec((tm, D_ATT), lambda i: (i, 0))
    pa = pl.BlockSpec((tm, D_ATT), lambda i: (i + offa, 0))
    pb = pl.BlockSpec((tm, D_ATT), lambda i: (i + offb, 0))
    whole = pl.BlockSpec((tm, ATTN_PROJ), lambda i: (i, 0))
    return pl.pallas_call(
        body, name=name, grid=(t // tm,),
        in_specs=[whole, pl.BlockSpec((4, LANES), lambda i: (0, 0)), wide, pa, pa, wide, pb, pb],
        out_specs=[whole, pl.BlockSpec((4, LANES), lambda i: (0, 0))],
        out_shape=[jax.ShapeDtypeStruct((t, ATTN_PROJ), BF16), jax.ShapeDtypeStruct((4, LANES), F32)],
        compiler_params=_cp(("arbitrary",)),
    )(proj, ws, dqa, dkpa, dvpa, dqb, dkpb, dvpb)


ROLL_W = 1024


def _rel_onehot():
    r_io = lax.broadcasted_iota(jnp.int32, (REL_W, ROLL_W), 0)
    m_io = lax.broadcasted_iota(jnp.int32, (REL_W, ROLL_W), 1)
    return (r_io == jnp.clip(REL_W - 1 - m_io, -MAX_REL, MAX_REL) + MAX_REL).astype(F32)


def _relpos_fwd(table, name):
    def body(t_ref, o_ref):
        rr = _dot(t_ref[...], _rel_onehot(), 1, 0, HI)

        def step(q, c):
            o_ref[q] = pltpu.roll(rr, (ROLL_W - (TQ - 1) + q) % ROLL_W, 1)[:, :REL_W]
            return c

        lax.fori_loop(0, TQ, step, 0)

    return pl.pallas_call(
        body, name=name, out_shape=jax.ShapeDtypeStruct((TQ, N_HEADS, REL_W), F32),
        in_specs=[pl.BlockSpec(memory_space=pltpu.VMEM)], out_specs=pl.BlockSpec(memory_space=pltpu.VMEM),
        compiler_params=_cp(),
    )(table)


def _relpos_bwd(dbias_t, name):
    def body(d_ref, o_ref):
        def step(q, acc):
            row = jnp.concatenate([d_ref[q], jnp.zeros((N_HEADS, ROLL_W - REL_W), F32)], axis=1)
            return acc + pltpu.roll(row, TQ - 1 - q, 1)

        drr = lax.fori_loop(0, TQ, step, jnp.zeros((N_HEADS, ROLL_W), F32))
        o_ref[...] = _dot(drr, _rel_onehot(), 1, 1, HI)

    return pl.pallas_call(
        body, name=name, out_shape=jax.ShapeDtypeStruct((N_HEADS, REL_W), F32),
        in_specs=[pl.BlockSpec(memory_space=pltpu.VMEM)], out_specs=pl.BlockSpec(memory_space=pltpu.VMEM),
        compiler_params=_cp(),
    )(dbias_t)


def _attn_scores(qe, kw, bias, kvalid):
    return jnp.where(kvalid, _dot(qe, kw, 1, 1) + bias, NEG)


def _stat_cols(stats, e):
    return stats[:, 64 * e:64 * e + 1], stats[:, 64 * e + 32:64 * e + 33]


def _attn_fwd(q, kp, vp, bias, sinks, pad, name, comm=None):
    t, hd = q.shape
    w = pad + TQ

    def body(sink_ref, q_ref, k_ref, v_ref, b_ref, o_ref, st_ref):
        hp, i = pl.program_id(0), pl.program_id(1)
        lo = _lo_mask()
        lane = lax.broadcasted_iota(jnp.int32, (1, LANES), 1)
        for j in range(ATT_SUB):
            start = pl.multiple_of((i * ATT_SUB + j) * TQ, TQ)
            qv = q_ref[TQ * j:TQ * (j + 1), :]
            kw = k_ref[pl.ds(start, w), :]
            vw = v_ref[pl.ds(start, w), :]
            kvalid = (start + lax.broadcasted_iota(jnp.int32, (1, w), 1)) >= pad
            outs, ms, ls = [], [], []
            for e in range(2):
                sel = lo if e == 0 else jnp.logical_not(lo)
                qe = jnp.where(sel, qv, jnp.zeros_like(qv))
                snk = sink_ref[2 * hp + e]
                s = _attn_scores(qe, kw, b_ref[e], kvalid)
                m = jnp.maximum(jnp.max(s, axis=-1, keepdims=True), snk)
                acc = _dot(jnp.exp(s - m).astype(BF16), jnp.where(sel, vw, jnp.ones_like(vw)), 1, 0)
                denom = acc[:, 64 * (1 - e):64 * (1 - e) + 1] + jnp.exp(snk - m)
                outs.append(acc * (1.0 / denom))
                ms.append(m)
                ls.append(denom)
            o_ref[TQ * j:TQ * (j + 1), :] = jnp.where(lo, outs[0], outs[1]).astype(BF16)
            st_ref[TQ * j:TQ * (j + 1), :] = jnp.where(lane < 32, ms[0], jnp.where(lane < 64, ls[0],
                                                                                 jnp.where(lane < 96, ms[1], ls[1])))

    full = pl.BlockSpec((t + pad, LANES), lambda h, i: (0, h))
    tile = pl.BlockSpec((ATT_SUB * TQ, LANES), lambda h, i: (i, h))
    (o, stats), got = _call(
        body, name=name, grid=(hd // LANES, t // (ATT_SUB * TQ)),
        in_specs=[pl.BlockSpec(memory_space=pltpu.SMEM), tile, full, full, pl.BlockSpec((2, TQ, w), lambda h, i: (h, 0, 0))],
        out_specs=[tile, tile], out_shape=[jax.ShapeDtypeStruct((t, hd), BF16), jax.ShapeDtypeStruct((t, hd), F32)],
        args=(sinks, q, kp, vp, bias), sem=("parallel", "arbitrary"), comm=comm)
    return o, stats, got


def _attn_bwd(q, kp, vp, bias, sinks, do, stats, o, col_off, pad, name, comm=None):
    t, hd = q.shape
    w = pad + TQ
    nhp = hd // LANES

    def body(sink_ref, q_ref, k_ref, v_ref, b_ref, do_ref, st_ref, o_ref, dq_ref, dk_ref, dv_ref, db_ref, ds_ref):
        hp, i = pl.program_id(0), pl.program_id(1)

        @pl.when(i == 0)
        def _():
            dk_ref[...] = jnp.zeros_like(dk_ref)
            dv_ref[...] = jnp.zeros_like(dv_ref)
            db_ref[...] = jnp.zeros_like(db_ref)
            ds_ref[...] = jnp.zeros_like(ds_ref)

        lo = _lo_mask()
        row8 = lax.broadcasted_iota(jnp.int32, (8, LANES), 0)
        dbias = [None, None]
        dsink = jnp.zeros((8, LANES), F32)
        for j in range(ATT_SUB):
            start = pl.multiple_of((i * ATT_SUB + j) * TQ, TQ)
            qv = q_ref[TQ * j:TQ * (j + 1), :]
            dov = do_ref[TQ * j:TQ * (j + 1), :]
            kw = k_ref[pl.ds(start, w), :]
            vw = v_ref[pl.ds(start, w), :]
            kvalid = (start + lax.broadcasted_iota(jnp.int32, (1, w), 1)) >= pad
            stats = st_ref[TQ * j:TQ * (j + 1), :]
            od = dov.astype(F32) * o_ref[TQ * j:TQ * (j + 1), :].astype(F32)
            dqs, dkw, dvw = [], None, None
            for e in range(2):
                sel = lo if e == 0 else jnp.logical_not(lo)
                qe = jnp.where(sel, qv, jnp.zeros_like(qv))
                doe = jnp.where(sel, dov, jnp.zeros_like(dov))
                m, denom = _stat_cols(stats, e)
                inv = 1.0 / denom
                p = jnp.exp(_attn_scores(qe, kw, b_ref[e], kvalid) - m) * inv
                psink = jnp.exp(sink_ref[2 * hp + e] - m) * inv
                dp = _dot(doe, vw, 1, 1)
                delta = jnp.sum(jnp.where(sel, od, 0.0), axis=-1, keepdims=True)
                ds = p * (dp - delta)
                dbias[e] = ds if dbias[e] is None else dbias[e] + ds
                dsink = dsink + jnp.where(row8 == e, jnp.sum(-psink * delta, axis=0, keepdims=True), 0.0)
                dsb = ds.astype(BF16)
                dqs.append(_dot(dsb, kw, 1, 0))
                dk_e = _dot(dsb, qe, 0, 0)
                dv_e = _dot(p.astype(BF16), doe, 0, 0)
                dkw = dk_e if dkw is None else dkw + dk_e
                dvw = dv_e if dvw is None else dvw + dv_e
            dq_ref[TQ * j:TQ * (j + 1), :] = jnp.where(lo, dqs[0], dqs[1])
            dk_ref[pl.ds(start, w), :] += dkw
            dv_ref[pl.ds(start, w), :] += dvw
        for e in range(2):
            db_ref[e] += dbias[e]
        ds_ref[0] += dsink

    full = pl.BlockSpec((t + pad, LANES), lambda h, i: (0, h))
    tile = pl.BlockSpec((ATT_SUB * TQ, LANES), lambda h, i: (i, h))
    btile = pl.BlockSpec((2, TQ, w), lambda h, i: (h, 0, 0))
    return _call(
        body, name=name, grid=(nhp, t // (ATT_SUB * TQ)),
        in_specs=[pl.BlockSpec(memory_space=pltpu.SMEM), tile, full, full, btile,
                  pl.BlockSpec((ATT_SUB * TQ, LANES), lambda h, i: (i, h + col_off)), tile, tile],
        out_specs=[tile, full, full, btile, pl.BlockSpec((1, 8, LANES), lambda h, i: (h, 0, 0))],
        out_shape=[jax.ShapeDtypeStruct((t, hd), F32), jax.ShapeDtypeStruct((t + pad, hd), F32),
                   jax.ShapeDtypeStruct((t + pad, hd), F32), jax.ShapeDtypeStruct((N_HEADS, TQ, w), F32),
                   jax.ShapeDtypeStruct((nhp, 8, LANES), F32)],
        args=(sinks, q, kp, vp, bias, do, stats, o), sem=("parallel", "arbitrary"), comm=comm)


def _conv_apply(taps, w_ref, ktaps):
    out = taps[0] * w_ref[ktaps - 1:ktaps, :]
    for s in range(1, ktaps):
        out = out + taps[s] * w_ref[ktaps - 1 - s:ktaps - s, :]
    return out


def _sigmoid(x):
    return jax.nn.sigmoid(x)


def _silu_grad(x):
    sg = _sigmoid(x)
    return x * sg, sg * (1.0 + x * (1.0 - sg))


FFN_HALO = 16
FFN_BT = 256
FFN_BC = 1408


def _ffn_in_mid(x, g, wt, w8, b, name):
    t, d = x.shape
    f = D_FF
    tm = FFN_BT

    def body(x_ref, g_ref, b_ref, w_ref, cb_ref, gu_ref, h_ref, a_ref, gc_ref, halo_ref):
        @pl.when(pl.program_id(0) == 0)
        def _():
            halo_ref[...] = jnp.zeros_like(halo_ref)

        xv = x_ref[...]
        r = lax.rsqrt(jnp.mean(xv * xv, axis=-1, keepdims=True) + EPS)
        h = (xv * r * g_ref[...]).astype(BF16)
        h_ref[...] = h
        for c in range(0, f, FFN_BC):
            cs = slice(c, c + FFN_BC)
            gate = _dot(h, b_ref[c:c + FFN_BC, :], 1, 1).astype(BF16)
            up = _dot(h, b_ref[f + c:f + c + FFN_BC, :], 1, 1).astype(BF16)
            gu_ref[:, cs] = gate
            gu_ref[:, f + c:f + c + FFN_BC] = up
            gf = gate.astype(F32)
            ext = jnp.concatenate([halo_ref[:, cs], gf], axis=0)
            gc = (cb_ref[:, cs] + gf * w_ref[2:3, cs] + pltpu.roll(ext, 1, 0)[8:] * w_ref[1:2, cs]
                  + pltpu.roll(ext, 2, 0)[8:] * w_ref[0:1, cs])
            a_ref[:, cs] = (gc * _sigmoid(gc) * up.astype(F32)).astype(BF16)
            gc_ref[:, cs] = gc.astype(BF16)
            halo_ref[:, cs] = gf[tm - 8:]

    row = pl.BlockSpec((tm, d), lambda i: (i, 0))
    row_f = pl.BlockSpec((tm, f), lambda i: (i, 0))
    return pl.pallas_call(
        body, name=name, grid=(t // tm,),
        in_specs=[row, pl.BlockSpec((1, d), lambda i: (0, 0)), pl.BlockSpec((2 * f, d), lambda i: (0, 0)),
                  pl.BlockSpec((8, f), lambda i: (0, 0)), pl.BlockSpec((1, f), lambda i: (0, 0))],
        out_specs=[pl.BlockSpec((tm, 2 * f), lambda i: (i, 0)), row, row_f, row_f],
        out_shape=[jax.ShapeDtypeStruct((t, 2 * f), BF16), jax.ShapeDtypeStruct((t, d), BF16), jax.ShapeDtypeStruct((t, f), BF16),
                   jax.ShapeDtypeStruct((t, f), BF16)],
        scratch_shapes=[pltpu.VMEM((8, f), F32)], compiler_params=_cp(("arbitrary",)),
    )(x, g, wt, w8, b)


def _ffn_mid_bwd(gu, gc, dxb, w_out, w8, name):
    t, d = dxb.shape
    f = D_FF
    tm, hr = FFN_BT, FFN_HALO
    nt = t // tm
    n = tm + hr

    def body(g_ref, u_ref, un_ref, c_ref, cn_ref, dx_ref, dxn_ref, wo_ref, w_ref, dgu_ref, dw_ref, db_ref):
        i = pl.program_id(0)
        last = i == nt - 1

        @pl.when(i == 0)
        def _():
            dw_ref[...] = jnp.zeros_like(dw_ref)
            db_ref[...] = jnp.zeros_like(db_ref)

        dxe = jnp.concatenate([dx_ref[...], dxn_ref[...]], axis=0)
        row = lax.broadcasted_iota(jnp.int32, (n, 1), 0)
        keep = (row < tm) | jnp.logical_not(last)
        for c in range(0, f, FFN_BC):
            cs = slice(c, c + FFN_BC)
            act, dact = _silu_grad(jnp.concatenate([c_ref[:, cs], cn_ref[:, cs]], axis=0).astype(F32))
            da = _dot(dxe, wo_ref[cs, :], 1, 1)
            up = jnp.concatenate([u_ref[:, cs], un_ref[:, cs]], axis=0).astype(F32)
            dgc = jnp.where(keep, da * up * dact, 0.0)
            nxt = [dgc[:tm], pltpu.roll(dgc, n - 1, 0)[:tm], pltpu.roll(dgc, n - 2, 0)[:tm]]
            dgu_ref[:, f + c:f + c + FFN_BC] = (da[:tm] * act[:tm]).astype(BF16)
            dgu_ref[:, cs] = (nxt[0] * w_ref[2:3, cs] + nxt[1] * w_ref[1:2, cs] + nxt[2] * w_ref[0:1, cs]).astype(BF16)
            gate = g_ref[:, cs].astype(F32)
            db_ref[:, cs] += jnp.sum(nxt[0], axis=0, keepdims=True)
            for s in range(3):
                dw_ref[2 - s:3 - s, cs] += jnp.sum(nxt[s] * gate, axis=0, keepdims=True)

    r = tm // hr
    nxt_blk = lambda i: jnp.minimum((i + 1) * r, t // hr - 1)
    row_f = pl.BlockSpec((tm, f), lambda i: (i, 0))
    halo_f = pl.BlockSpec((hr, f), lambda i: (nxt_blk(i), 0))
    return pl.pallas_call(
        body, name=name, grid=(nt,),
        in_specs=[row_f, pl.BlockSpec((tm, f), lambda i: (i, 1)), pl.BlockSpec((hr, f), lambda i: (nxt_blk(i), 1)),
                  row_f, halo_f,
                  pl.BlockSpec((tm, d), lambda i: (i, 0)), pl.BlockSpec((hr, d), lambda i: (nxt_blk(i), 0)),
                  pl.BlockSpec((f, d), lambda i: (0, 0)), pl.BlockSpec((8, f), lambda i: (0, 0))],
        out_specs=[pl.BlockSpec((tm, 2 * f), lambda i: (i, 0)), pl.BlockSpec((8, f), lambda i: (0, 0)),
                   pl.BlockSpec((1, f), lambda i: (0, 0))],
        out_shape=[jax.ShapeDtypeStruct((t, 2 * f), BF16), jax.ShapeDtypeStruct((8, f), F32), jax.ShapeDtypeStruct((1, f), F32)],
        compiler_params=_cp(("arbitrary",)),
    )(gu, gu, gu, gc, gc, dxb, dxb, w_out, w8)


PRE_TM = 256
PRE_TC = 1024


def _softplus_heads(v):
    sp = jnp.maximum(v, 0.0) + jnp.log(1.0 + jnp.exp(-jnp.abs(v)))
    return jnp.where(lax.broadcasted_iota(jnp.int32, (1, LANES), 1) < SSM_HEADS, sp, 0.0)


def _ssm_in_pre(x, g, wt, wt_dt, w8, b, dt_bias, name):
    t, d = x.shape
    tm, tc = PRE_TM, PRE_TC

    def body(x_ref, g_ref, b_ref, bdt_ref, w_ref, cb_ref, db_ref, zx_ref, h_ref, o_ref, c_ref, dtr_ref, dt_ref, halo_ref):
        @pl.when(pl.program_id(0) == 0)
        def _():
            halo_ref[...] = jnp.zeros_like(halo_ref)

        xv = x_ref[...]
        r = lax.rsqrt(jnp.mean(xv * xv, axis=-1, keepdims=True) + EPS)
        h = (xv * r * g_ref[...]).astype(BF16)
        h_ref[...] = h
        dtr = _dot(h, bdt_ref[...], 1, 1)
        dtr_ref[...] = dtr
        dt_ref[...] = _softplus_heads(dtr + db_ref[...])
        for c in range(0, ZX, tc):
            v = _dot(h, b_ref[c:c + tc, :], 1, 1)
            zx_ref[:, c:c + tc] = v
            if c >= D_INNER:
                cs = slice(c - D_INNER, c - D_INNER + tc)
                ext = jnp.concatenate([halo_ref[:, cs], v], axis=0)
                conv = cb_ref[:, cs] + v * w_ref[3:4, cs]
                for s in (1, 2, 3):
                    conv = conv + pltpu.roll(ext, s, 0)[8:] * w_ref[3 - s:4 - s, cs]
                o_ref[:, cs] = conv * _sigmoid(conv)
                c_ref[:, cs] = conv.astype(BF16)
                halo_ref[:, cs] = v[tm - 8:]

    row = pl.BlockSpec((tm, d), lambda i: (i, 0))
    row_x = pl.BlockSpec((tm, XBC), lambda i: (i, 0))
    row_h = pl.BlockSpec((tm, LANES), lambda i: (i, 0))
    return pl.pallas_call(
        body, name=name, grid=(t // tm,),
        in_specs=[row, pl.BlockSpec((1, d), lambda i: (0, 0)), pl.BlockSpec(wt.shape, lambda i: (0, 0)),
                  pl.BlockSpec(wt_dt.shape, lambda i: (0, 0)),
                  pl.BlockSpec((8, XBC), lambda i: (0, 0)), pl.BlockSpec((1, XBC), lambda i: (0, 0)),
                  pl.BlockSpec((1, LANES), lambda i: (0, 0))],
        out_specs=[pl.BlockSpec((tm, ZX), lambda i: (i, 0)), row, row_x, row_x, row_h, row_h],
        out_shape=[jax.ShapeDtypeStruct((t, ZX), F32), jax.ShapeDtypeStruct((t, d), BF16), jax.ShapeDtypeStruct((t, XBC), F32),
                   jax.ShapeDtypeStruct((t, XBC), BF16), jax.ShapeDtypeStruct((t, LANES), F32),
                   jax.ShapeDtypeStruct((t, LANES), F32)],
        scratch_shapes=[pltpu.VMEM((8, XBC), F32)], compiler_params=_cp(("arbitrary",)),
    )(x, g, wt, wt_dt, w8, b, dt_bias)


PRE_HALO = 16


def _ssm_pre_bwd(zx, conv, dxbc, w8, name):
    t = zx.shape[0]
    tm, tc, hr = PRE_TM, PRE_TC, PRE_HALO
    off = D_INNER // tc
    nt = t // tm
    n = tm + hr

    def body(x_ref, c_ref, cn_ref, d_ref, dn_ref, w_ref, o_ref, dw_ref, db_ref):
        i = pl.program_id(1)
        last = i == nt - 1

        @pl.when(i == 0)
        def _():
            dw_ref[...] = jnp.zeros_like(dw_ref)
            db_ref[...] = jnp.zeros_like(db_ref)

        _, dact = _silu_grad(jnp.concatenate([c_ref[...], cn_ref[...]], axis=0).astype(F32))
        row = lax.broadcasted_iota(jnp.int32, (n, 1), 0)
        dc = jnp.where((row < tm) | jnp.logical_not(last), jnp.concatenate([d_ref[...], dn_ref[...]], axis=0) * dact, 0.0)
        nxt = [dc[:tm]] + [pltpu.roll(dc, n - s, 0)[:tm] for s in (1, 2, 3)]
        o_ref[...] = _conv_apply(nxt, w_ref, 4).astype(BF16)
        xv = x_ref[...]
        db_ref[...] += jnp.sum(nxt[0], axis=0, keepdims=True)
        for s in range(4):
            dw_ref[3 - s:4 - s, :] += jnp.sum(nxt[s] * xv, axis=0, keepdims=True)

    nxt_blk = lambda i: jnp.minimum((i + 1) * (tm // hr), t // hr - 1)
    tile = pl.BlockSpec((tm, tc), lambda j, i: (i, j))
    halo = pl.BlockSpec((hr, tc), lambda j, i: (nxt_blk(i), j))
    return pl.pallas_call(
        body, name=name, grid=(XBC // tc, nt),
        in_specs=[pl.BlockSpec((tm, tc), lambda j, i: (i, j + off)), tile, halo, tile, halo,
                  pl.BlockSpec((8, tc), lambda j, i: (0, j))],
        out_specs=[tile, pl.BlockSpec((8, tc), lambda j, i: (0, j)), pl.BlockSpec((1, tc), lambda j, i: (0, j))],
        out_shape=[jax.ShapeDtypeStruct((t, XBC), BF16), jax.ShapeDtypeStruct((8, XBC), F32),
                   jax.ShapeDtypeStruct((1, XBC), F32)],
        compiler_params=_cp(("parallel", "arbitrary")),
    )(zx, conv, conv, dxbc, dxbc, w8)


def _head_lanes():
    return lax.broadcasted_iota(jnp.int32, (1, LANES), 1) < SSM_HEADS


def _dt_bwd(dtraw, bias, ddt, name):
    t = dtraw.shape[0]
    tm = _pick(t, (1024, 512, 256, 128))

    def body(x_ref, b_ref, d_ref, o_ref, db_ref):
        @pl.when(pl.program_id(0) == 0)
        def _():
            db_ref[...] = jnp.zeros_like(db_ref)

        g = jnp.where(_head_lanes(), d_ref[...] * _sigmoid(x_ref[...] + b_ref[...]), 0.0)
        o_ref[...] = g.astype(BF16)
        db_ref[...] += jnp.sum(g, axis=0, keepdims=True)

    row = pl.BlockSpec((tm, LANES), lambda i: (i, 0))
    vec = pl.BlockSpec((1, LANES), lambda i: (0, 0))
    return pl.pallas_call(
        body, name=name, grid=(t // tm,), in_specs=[row, vec, row], out_specs=[row, vec],
        out_shape=[jax.ShapeDtypeStruct((t, LANES), BF16), jax.ShapeDtypeStruct((1, LANES), F32)],
        compiler_params=_cp(("arbitrary",)),
    )(dtraw, bias, ddt)


GROUP_W = D_INNER // SSM_GROUPS


def _ssd_common(dt, alog):
    ll = dt.shape[0]
    a_neg = -jnp.exp(alog)
    a = dt * a_neg
    ri = lax.broadcasted_iota(jnp.int32, (ll, ll), 0)
    ci = lax.broadcasted_iota(jnp.int32, (ll, ll), 1)
    tril = ri >= ci
    acs = _dot(tril.astype(F32), a, 1, 0, HI)
    return a_neg, tril, acs, acs.T


def _pair_terms(acs, acs_t, dt, h0, lo):
    ll = acs.shape[0]
    cols = [acs[:, h0 + e:h0 + e + 1] for e in range(2)]
    rows = [acs_t[h0 + e:h0 + e + 1, :] for e in range(2)]
    dtc = [dt[:, h0 + e:h0 + e + 1] for e in range(2)]
    lasts = [c[ll - 1:ll, :] for c in cols]
    dtx = jnp.where(lo, dtc[0], dtc[1])
    eac = jnp.where(lo, jnp.exp(cols[0]), jnp.exp(cols[1]))
    fdec = jnp.where(lo, jnp.exp(lasts[0] - cols[0]), jnp.exp(lasts[1] - cols[1]))
    elast = jnp.where(lo, jnp.exp(lasts[0]), jnp.exp(lasts[1]))
    return cols, rows, dtx, eac, fdec, elast


def _decay(col, row, tril):
    return jnp.where(tril, jnp.exp(jnp.minimum(col - row, 0.0)), 0.0)


def _two_heads_rows(v, lo):
    z = jnp.zeros_like(v)
    return jnp.concatenate([jnp.where(lo, v, z), jnp.where(lo, z, v)], axis=0)


def _two_heads_cols(ms):
    return jnp.concatenate(ms, axis=1)


def _z_group(z_refs, g):
    return z_refs[g // 2][:, GROUP_W * (g % 2):GROUP_W * (g % 2 + 1)]


def _ssd_fwd(xbc, dt, alog, zx, dexp, nw, name, comm=None):
    t = xbc.shape[0]
    ll = SSD_L
    nc = t // ll

    def body(x_ref, dt_ref, al_ref, z0_ref, z1_ref, d_ref, w_ref, y_ref, sp_ref, y4_ref, st_ref):
        @pl.when(pl.program_id(0) == 0)
        def _():
            st_ref[...] = jnp.zeros_like(st_ref)

        dtv = dt_ref[...]
        _, tril, acs, acs_t = _ssd_common(dtv, al_ref[...])
        lo = _lo_mask()
        sp_ref[0] = st_ref[...]
        for g in range(SSM_GROUPS):
            bg = x_ref[:, D_INNER + SSM_STATE * g:D_INNER + SSM_STATE * (g + 1)].astype(BF16)
            cg = x_ref[:, D_INNER + 512 + SSM_STATE * g:D_INNER + 512 + SSM_STATE * (g + 1)].astype(BF16)
            gm = _dot(cg, bg, 1, 1)
            g0 = GROUP_W * g
            terms = [_pair_terms(acs, acs_t, dtv, 8 * g + 2 * pp, lo) for pp in range(4)]
            dtx, eac, fdec, elast = [jnp.concatenate([tt[k] for tt in terms], axis=1) for k in (2, 3, 4, 5)]
            xg = x_ref[:, g0:g0 + GROUP_W]
            ug = (xg * dtx).astype(BF16)
            sg = st_ref[:, g0:g0 + GROUP_W]
            yst = _dot(cg, sg.astype(BF16), 1, 0) * eac
            st_ref[:, g0:g0 + GROUP_W] = sg * elast + _dot(bg, (xg * (fdec * dtx)).astype(BF16), 0, 0)
            ys = []
            for pp in range(4):
                cols, rows = terms[pp][0], terms[pp][1]
                sl = slice(LANES * pp, LANES * (pp + 1))
                y_in = _dot(_two_heads_cols([(gm * _decay(cols[e], rows[e], tril)).astype(BF16) for e in range(2)]),
                            _two_heads_rows(ug[:, sl], lo), 1, 0)
                ys.append(y_in + yst[:, sl])
            yg = jnp.concatenate(ys, axis=1)
            y_ref[:, g0:g0 + GROUP_W] = yg
            zg = _z_group((z0_ref, z1_ref), g)
            y3 = (yg + d_ref[:, g0:g0 + GROUP_W] * xg) * (zg * _sigmoid(zg))
            r = lax.rsqrt(jnp.mean(y3 * y3, axis=-1, keepdims=True) + EPS)
            y4_ref[:, g0:g0 + GROUP_W] = (y3 * r * w_ref[:, g0:g0 + GROUP_W]).astype(BF16)

    zblk = lambda j: pl.BlockSpec((ll, 1024), lambda c: (c, j))
    vec = pl.BlockSpec((1, D_INNER), lambda c: (0, 0))
    row = pl.BlockSpec((ll, D_INNER), lambda c: (c, 0))
    return _call(
        body, name=name, grid=(nc,),
        in_specs=[pl.BlockSpec((ll, XBC), lambda c: (c, 0)), pl.BlockSpec((ll, LANES), lambda c: (c, 0)),
                  pl.BlockSpec((1, LANES), lambda c: (0, 0)), zblk(0), zblk(1), vec, vec],
        out_specs=[row, pl.BlockSpec((1, SSM_STATE, D_INNER), lambda c: (c, 0, 0)), row],
        out_shape=[jax.ShapeDtypeStruct((t, D_INNER), F32), jax.ShapeDtypeStruct((nc, SSM_STATE, D_INNER), F32),
                   jax.ShapeDtypeStruct((t, D_INNER), BF16)],
        scratch_shapes=[pltpu.VMEM((SSM_STATE, D_INNER), F32)],
        args=(xbc, dt, alog, zx, zx, dexp, nw), sem=("arbitrary",), comm=comm)


def _ssd_bwd(xbc, dt, alog, sprev, dy4, y, zx, dexp, nw, name, comm=None):
    t = xbc.shape[0]
    ll = SSD_L_BWD if t % SSD_L_BWD == 0 else SSD_L
    nc = t // ll
    every = ll // SSD_L

    def body(x_ref, dt_ref, al_ref, sp_ref, g4_ref, y_ref, z0_ref, z1_ref, d_ref, w_ref,
             dx_ref, ddt_ref, dal_ref, dz_ref, dd_ref, dnw_ref, ds_ref, colt_ref):
        @pl.when(pl.program_id(0) == 0)
        def _():
            ds_ref[...] = jnp.zeros_like(ds_ref)
            dal_ref[...] = jnp.zeros_like(dal_ref)
            dd_ref[...] = jnp.zeros_like(dd_ref)
            dnw_ref[...] = jnp.zeros_like(dnw_ref)

        dtv = dt_ref[...]
        a_neg, tril, acs, acs_t = _ssd_common(dtv, al_ref[...])
        lo = _lo_mask()
        hi = jnp.logical_not(lo)
        lane = lax.broadcasted_iota(jnp.int32, (1, LANES), 1)
        colt_ref[...] = jnp.zeros_like(colt_ref)
        rowterm = jnp.zeros((ll, LANES), F32)
        ddt_u = jnp.zeros((ll, LANES), F32)
        dlast = jnp.zeros((1, LANES), F32)

        def halves(v):
            return (jnp.sum(jnp.where(lo, v, 0.0), axis=-1, keepdims=True),
                    jnp.sum(jnp.where(hi, v, 0.0), axis=-1, keepdims=True))

        for g in range(SSM_GROUPS):
            cb0 = D_INNER + SSM_STATE * g
            cc0 = D_INNER + 512 + SSM_STATE * g
            bg = x_ref[:, cb0:cb0 + SSM_STATE].astype(BF16)
            cg = x_ref[:, cc0:cc0 + SSM_STATE].astype(BF16)
            gm = _dot(cg, bg, 1, 1)
            g0 = GROUP_W * g
            terms = [_pair_terms(acs, acs_t, dtv, 8 * g + 2 * pp, lo) for pp in range(4)]
            dtx, eac, fdec, elast = [jnp.concatenate([tt[k] for tt in terms], axis=1) for k in (2, 3, 4, 5)]
            xg = x_ref[:, g0:g0 + GROUP_W]
            u32 = xg * dtx
            ug = u32.astype(BF16)
            zg = _z_group((z0_ref, z1_ref), g)
            dg = d_ref[:, g0:g0 + GROUP_W]
            act, dact = _silu_grad(zg)
            y2 = y_ref[:, g0:g0 + GROUP_W] + dg * xg
            y3 = y2 * act
            rn = lax.rsqrt(jnp.mean(y3 * y3, axis=-1, keepdims=True) + EPS)
            y3n = y3 * rn
            gv = g4_ref[:, g0:g0 + GROUP_W]
            dyn = gv * w_ref[:, g0:g0 + GROUP_W]
            dy3 = rn * (dyn - y3n * jnp.mean(dyn * y3n, axis=-1, keepdims=True))
            dyg = dy3 * act
            dskip = dyg * dg
            dz_ref[:, g0:g0 + GROUP_W] = (dy3 * y2 * dact).astype(BF16)
            dd_ref[:, g0:g0 + GROUP_W] += jnp.sum(dyg * xg, axis=0, keepdims=True)
            dnw_ref[:, g0:g0 + GROUP_W] += jnp.sum(gv * y3n, axis=0, keepdims=True)
            dyb = dyg.astype(BF16)
            spg = sp_ref[0, :, g0:g0 + GROUP_W]
            spb = spg.astype(BF16)
            dsg = ds_ref[:, g0:g0 + GROUP_W]
            dsb = dsg.astype(BF16)
            du_st = _dot(bg, dsb, 1, 0) * fdec
            yst = _dot(cg, spb, 1, 0) * eac
            dye = (dyg * eac).astype(BF16)
            dc_st = _dot(dye, spb, 1, 1)
            db_st = _dot((xg * (fdec * dtx)).astype(BF16), dsb, 1, 1)
            ds_ref[:, g0:g0 + GROUP_W] = dsg * elast + _dot(cg, dye, 0, 0)
            qst_el = du_st * u32
            rq_el = dyg * yst - qst_el
            q_row = jnp.sum(qst_el, axis=0, keepdims=True)
            s_row = jnp.sum(dsg * spg, axis=0, keepdims=True)
            dgm = jnp.zeros((ll, ll), F32)
            for pp in range(4):
                h0 = 8 * g + 2 * pp
                cols, rows = terms[pp][0], terms[pp][1]
                sl = slice(LANES * pp, LANES * (pp + 1))
                decs = [_decay(cols[e], rows[e], tril) for e in range(2)]
                wms = [gm * d for d in decs]
                dum2 = _dot(dyb[:, sl], _two_heads_rows(ug[:, sl], lo), 1, 1)
                du = _dot(jnp.concatenate([wm.astype(BF16) for wm in wms], axis=0),
                          _two_heads_rows(dyb[:, sl], lo), 0, 0) + du_st[:, sl]
                dx_ref[:, g0 + LANES * pp:g0 + LANES * (pp + 1)] = du * dtx[:, sl] + dskip[:, sl]
                ddtu = halves(du * xg[:, sl])
                rq = halves(rq_el[:, sl])
                qs = halves(q_row[:, sl])
                ss = halves(s_row[:, sl])
                for e in range(2):
                    dum = dum2[:, ll * e:ll * (e + 1)]
                    dgm = dgm + dum * decs[e]
                    tm_ = dum * wms[e]
                    oh = lane == (h0 + e)
                    rowterm = rowterm + jnp.where(oh, jnp.sum(tm_, axis=1, keepdims=True) + rq[e], 0.0)
                    ddt_u = ddt_u + jnp.where(oh, ddtu[e], 0.0)
                    dlast = dlast + jnp.where(oh, jnp.exp(cols[e][ll - 1:ll, :]) * ss[e] + qs[e], 0.0)
                    colt_ref[h0 + e:h0 + e + 1, :] = jnp.sum(tm_, axis=0, keepdims=True)
            dgb = dgm.astype(BF16)
            dx_ref[:, cc0:cc0 + SSM_STATE] = _dot(dgb, bg, 1, 0) + dc_st
            dx_ref[:, cb0:cb0 + SSM_STATE] = _dot(dgb, cg, 0, 0) + db_st
        row_io = lax.broadcasted_iota(jnp.int32, (ll, LANES), 0)
        dacs = rowterm - colt_ref[...].T + jnp.where(row_io == ll - 1, dlast, 0.0)
        da = _dot(jnp.logical_not(tril).astype(F32) + jnp.where(
            lax.broadcasted_iota(jnp.int32, (ll, ll), 0) == lax.broadcasted_iota(jnp.int32, (ll, ll), 1), 1.0, 0.0),
            dacs, 1, 0, HI)
        ddt_ref[...] = da * a_neg + ddt_u
        dal_ref[...] += jnp.sum(da * dtv, axis=0, keepdims=True) * a_neg

    rev = lambda c: nc - 1 - c
    row = pl.BlockSpec((ll, D_INNER), lambda c: (rev(c), 0))
    vec = pl.BlockSpec((1, D_INNER), lambda c: (0, 0))
    zblk = lambda j: pl.BlockSpec((ll, 1024), lambda c: (rev(c), j))
    return _call(
        body, name=name, grid=(nc,),
        in_specs=[pl.BlockSpec((ll, XBC), lambda c: (rev(c), 0)), pl.BlockSpec((ll, LANES), lambda c: (rev(c), 0)),
                  pl.BlockSpec((1, LANES), lambda c: (0, 0)),
                  pl.BlockSpec((1, SSM_STATE, D_INNER), lambda c: (rev(c) * every, 0, 0)), row, row, zblk(0), zblk(1), vec, vec],
        out_specs=[pl.BlockSpec((ll, XBC), lambda c: (rev(c), 0)), pl.BlockSpec((ll, LANES), lambda c: (rev(c), 0)),
                   pl.BlockSpec((1, LANES), lambda c: (0, 0)), row, vec, vec],
        out_shape=[jax.ShapeDtypeStruct((t, XBC), F32), jax.ShapeDtypeStruct((t, LANES), F32),
                   jax.ShapeDtypeStruct((1, LANES), F32), jax.ShapeDtypeStruct((t, D_INNER), BF16),
                   jax.ShapeDtypeStruct((1, D_INNER), F32), jax.ShapeDtypeStruct((1, D_INNER), F32)],
        scratch_shapes=[pltpu.VMEM((SSM_STATE, D_INNER), F32), pltpu.VMEM((LANES, ll), F32)],
        args=(xbc, dt, alog, sprev, dy4, y, zx, zx, dexp, nw), sem=("arbitrary",), comm=comm)


def _sum_parts(parts, name):
    nparts, r, c = parts.shape
    tc = _pick(c, (256, 128))

    def body(p_ref, o_ref):
        g = p_ref[0].astype(F32)
        for k in range(1, nparts):
            g = g + p_ref[k].astype(F32)
        o_ref[...] = g

    return pl.pallas_call(
        body, name=name, grid=(c // tc,), in_specs=[pl.BlockSpec((nparts, r, tc), lambda j: (0, 0, j))],
        out_specs=pl.BlockSpec((r, tc), lambda j: (0, j)), out_shape=jax.ShapeDtypeStruct((r, c), F32),
        compiler_params=_cp(("parallel",)),
    )(parts)


def _adamw(parts, w, m, v, name):
    nl, r, c = w.shape
    assert len(parts) == nl
    tr = _pick(r, (256, 128, 64))
    c1 = 1.0 - ADAM_B1 ** ADAM_STEP
    c2 = 1.0 - ADAM_B2 ** ADAM_STEP

    def body(*refs):
        p_refs = refs[:nl]
        w_ref, m_ref, v_ref, g_ref, d_ref, mo_ref, vo_ref = refs[nl:]
        g = None
        for l, p_ref in enumerate(p_refs):
            s = p_ref[0].astype(F32)
            for k in range(1, p_ref.shape[0]):
                s = s + p_ref[k].astype(F32)
            g = s if g is None else jnp.where(pl.program_id(0) == l, s, g)
        mn = ADAM_B1 * m_ref[0] + (1.0 - ADAM_B1) * g
        vn = ADAM_B2 * v_ref[0] + (1.0 - ADAM_B2) * (g * g)
        g_ref[0] = g
        mo_ref[0] = mn
        vo_ref[0] = vn
        d_ref[0] = -ADAM_LR * ((mn / c1) / (jnp.sqrt(vn / c2) + ADAM_EPS) + ADAM_WD * w_ref[0])

    row = pl.BlockSpec((1, tr, c), lambda l, i: (l, i, 0))
    sd = jax.ShapeDtypeStruct((nl, r, c), F32)
    return pl.pallas_call(
        body, name=name, grid=(nl, r // tr),
        in_specs=[pl.BlockSpec((p.shape[0], tr, c), lambda l, i: (0, i, 0)) for p in parts] + [row, row, row],
        out_specs=[row, row, row, row], out_shape=[sd, sd, sd, sd], compiler_params=_cp(("parallel", "parallel")),
    )(*parts, w, m, v)


def _peers():
    mx, my, mc = lax.axis_index("x"), lax.axis_index("y"), lax.axis_index("c")
    me = 4 * mx + 2 * my + mc
    out = []
    for k in range(1, N_DEV):
        px = 1 - mx if k & 4 else mx
        py = 1 - my if k & 2 else my
        pc = 1 - mc if k & 1 else mc
        out.append(((px, py, pc), 4 * px + 2 * py + pc))
    return me, out


class _Comm:
    def __init__(self, arrs, scatters):
        self.arrs, self.scatters, self.n = list(arrs), list(scatters), len(arrs)
        self.specs = [pl.BlockSpec(memory_space=pl.ANY)] * self.n
        self.out_shape = [jax.ShapeDtypeStruct(x.shape if sc else (N_DEV,) + x.shape, x.dtype)
                          for x, sc in zip(self.arrs, self.scatters)]
        np_ = N_DEV - 1
        self.scratch = [pltpu.SemaphoreType.DMA((np_ * self.n,)), pltpu.SemaphoreType.DMA((np_ * self.n,)),
                        pltpu.SemaphoreType.DMA((self.n,))]

    def _copies(self, x_refs, o_refs, sems):
        send_sems, recv_sems, local_sems = sems
        me, peers = _peers()
        np_ = N_DEV - 1
        local, sends, recvs = [], [], []
        for a in range(self.n):
            mine = x_refs[a].at[me] if self.scatters[a] else x_refs[a]
            local.append(pltpu.make_async_copy(mine, o_refs[a].at[me], local_sems.at[a]))
        for k, (dev, idx) in enumerate(peers):
            for a in range(self.n):
                mine = x_refs[a].at[me] if self.scatters[a] else x_refs[a]
                sends.append(pltpu.make_async_remote_copy(
                    src_ref=x_refs[a].at[idx] if self.scatters[a] else x_refs[a], dst_ref=o_refs[a].at[me],
                    send_sem=send_sems.at[a * np_ + k], recv_sem=recv_sems.at[a * np_ + k], device_id=dev, device_id_type=MESH))
                recvs.append(pltpu.make_async_remote_copy(
                    src_ref=mine, dst_ref=o_refs[a].at[idx], send_sem=send_sems.at[a * np_ + k],
                    recv_sem=recv_sems.at[a * np_ + k], device_id=dev, device_id_type=MESH))
        return local, sends, recvs

    def start(self, x_refs, o_refs, sems):
        local, sends, _ = self._copies(x_refs, o_refs, sems)
        for cp in local + sends:
            cp.start()

    def wait(self, x_refs, o_refs, sems):
        local, sends, recvs = self._copies(x_refs, o_refs, sems)
        for cp in recvs:
            cp.wait_recv()
        for cp in sends:
            cp.wait_send()
        for cp in local:
            cp.wait()


class _Gather2(_Comm):
    def __init__(self, arrs):
        super().__init__(arrs, [False] * len(arrs))

    def _plan(self, x_refs, o_refs, sems):
        send_sems, recv_sems, local_sems = sems
        mx, my, mc = lax.axis_index("x"), lax.axis_index("y"), lax.axis_index("c")
        slot = lambda px, py, pc: 4 * px + 2 * py + pc
        sib = (mx, my, 1 - mc)
        chips = [(1 - mx, my), (mx, 1 - my), (1 - mx, 1 - my)]
        np_ = N_DEV - 1
        local, first, passed, arrive_first, arrive_rest = [], [], [], [], []

        def copy(a, k, src, block, to):
            return pltpu.make_async_remote_copy(
                src_ref=src, dst_ref=o_refs[a].at[block], send_sem=send_sems.at[a * np_ + k], recv_sem=recv_sems.at[a * np_ + k],
                device_id=to, device_id_type=MESH)

        for a in range(self.n):
            me = slot(mx, my, mc)
            local.append(pltpu.make_async_copy(x_refs[a], o_refs[a].at[me], local_sems.at[a]))
            first.append(copy(a, 0, x_refs[a], me, sib))
            arrive_rest.append(copy(a, 0, x_refs[a], slot(*sib), sib))
            for j, (cx, cy) in enumerate(chips):
                first.append(copy(a, 1 + j, x_refs[a], me, (cx, cy, mc)))
                arrive_first.append(copy(a, 1 + j, x_refs[a], slot(cx, cy, mc), (cx, cy, mc)))
                passed.append(copy(a, 4 + j, o_refs[a].at[slot(cx, cy, mc)], slot(cx, cy, mc), sib))
                arrive_rest.append(copy(a, 4 + j, x_refs[a], slot(cx, cy, 1 - mc), sib))
        return local, first, passed, arrive_first, arrive_rest

    def start(self, x_refs, o_refs, sems):
        local, first, _, _, _ = self._plan(x_refs, o_refs, sems)
        for cp in local + first:
            cp.start()

    def wait(self, x_refs, o_refs, sems):
        local, first, passed, arrive_first, arrive_rest = self._plan(x_refs, o_refs, sems)
        for arrived, onward in zip(arrive_first, passed):
            arrived.wait_recv()
            onward.start()
        for cp in arrive_rest:
            cp.wait_recv()
        for cp in first + passed:
            cp.wait_send()
        for cp in local:
            cp.wait()


def _call(body, *, name, grid, in_specs, out_specs, out_shape, args, scratch_shapes=(), sem=None, comm=None):
    if comm is None:
        outs = pl.pallas_call(
            body, name=name, grid=grid, in_specs=list(in_specs), out_specs=list(out_specs), out_shape=list(out_shape),
            scratch_shapes=list(scratch_shapes), compiler_params=_cp(sem),
        )(*args)
        return list(outs), []
    n_in, n_out, nc = len(in_specs), len(out_specs), comm.n
    nsteps = 1
    for g in grid:
        nsteps *= g

    def carrier(*refs):
        ins, cin = refs[:n_in], refs[n_in:n_in + nc]
        outs, cout = refs[n_in + nc:n_in + nc + n_out], refs[n_in + nc + n_out:n_in + 2 * nc + n_out]
        rest = refs[n_in + 2 * nc + n_out:]
        scratch, sems = rest[:len(rest) - 3], rest[len(rest) - 3:]
        if nsteps == 1:
            comm.start(cin, cout, sems)
            body(*ins, *outs, *scratch)
            comm.wait(cin, cout, sems)
            return
        step = 0
        for d, g in enumerate(grid):
            step = step * g + pl.program_id(d)

        @pl.when(step == 0)
        def _():
            comm.start(cin, cout, sems)

        body(*ins, *outs, *scratch)

        @pl.when(step == nsteps - 1)
        def _():
            comm.wait(cin, cout, sems)

    outs = pl.pallas_call(
        carrier, name=name, grid=grid, in_specs=list(in_specs) + comm.specs, out_specs=list(out_specs) + comm.specs,
        out_shape=list(out_shape) + comm.out_shape, scratch_shapes=list(scratch_shapes) + comm.scratch,
        compiler_params=_cp(("arbitrary",) * len(grid) if grid else None),
    )(*args, *comm.arrs)
    return list(outs[:n_out]), list(outs[n_out:])


def _exchange(comm, name):
    return _call(lambda *refs: None, name=name, grid=(), in_specs=[], out_specs=[], out_shape=[], args=[], comm=comm)[1]


def _pack(arrs, dtype, lead=()):
    nl = len(lead)
    flat = jnp.concatenate([a.astype(dtype).reshape(lead + (-1,)) for a in arrs], axis=nl)
    n = flat.shape[-1]
    rows = -(-n // (LANES * 8)) * 8
    flat = jnp.pad(flat, [(0, 0)] * nl + [(0, rows * LANES - n)])
    return flat.reshape(lead + (rows, LANES))


def _unpack(flat, shapes, lead=()):
    nl = len(lead)
    flat = flat.reshape(lead + (-1,))
    out, o = [], 0
    for s in shapes:
        n = 1
        for d in s:
            n *= d
        out.append(lax.slice_in_dim(flat, o, o + n, axis=nl).reshape(lead + tuple(s)))
        o += n
    return out


def _join(g, ax):
    return jnp.concatenate([g[d] for d in range(N_DEV)], axis=ax)


def _split(full, ax):
    n = full.shape[ax] // N_DEV
    return jnp.stack([lax.slice_in_dim(full, d * n, (d + 1) * n, axis=ax) for d in range(N_DEV)])


_WEIGHTS = ['norm_mix', 'norm_ffn', 'attn_w_in', 'attn_w_out', 'relpos_table', 'q_norm_a', 'k_norm_a', 'q_norm_b',
            'k_norm_b', 'sinks', 'ssm_w_in', 'ssm_conv_w', 'ssm_conv_b', 'ssm_dt_bias', 'ssm_a_log', 'ssm_d', 'ssm_norm',
            'ssm_w_out', 'ffn_w_in', 'ffn_conv_w', 'ffn_conv_b', 'ffn_w_out']
_SHARD_AX = {'attn_w_in': 2, 'attn_w_out': 1, 'ssm_w_in': 2, 'ssm_conv_w': 2, 'ssm_conv_b': 1, 'ssm_norm': 1,
             'ssm_w_out': 1, 'ffn_w_in': 2, 'ffn_conv_w': 2, 'ffn_w_out': 1}
_BIG = ['attn_w_in', 'attn_w_out', 'ssm_w_in', 'ssm_w_out', 'ffn_w_in', 'ffn_w_out']
_SMALL = ['ssm_conv_w', 'ssm_conv_b', 'ssm_norm', 'ffn_conv_w']
_AX2 = {n: _SHARD_AX[n] - 1 for n in _BIG}
_REPL = [n for n in _WEIGHTS if n not in _SHARD_AX]


def _rows8(w):
    return jnp.pad(w, ((0, 8 - w.shape[0]), (0, 0)))


def _lanes128(v):
    return jnp.pad(v, (0, LANES - v.shape[0])).reshape(1, LANES)


def _band_mask(n_prev, pad):
    cq = jnp.arange(TQ)[:, None] // CHUNK
    ck = jnp.arange(pad + TQ)[None, :] // CHUNK
    return (ck >= cq) & (ck <= cq + n_prev)


def _ffn_fwd(xin, g, w_in_t, w8, cb, tag):
    gu, h, a, gc = _ffn_in_mid(xin, g, w_in_t, w8, cb, f"mm_ffn_in{tag}")
    return a, (h, gu, a, gc)


def _ffn_bwd(dx, dxb, xin, g, w_in_t, w8, w_out, saved, tag):
    h, gu, a, gc = saved
    dw_out = _mm_tn(a, dxb, f"mm_ffn_dwout{tag}")
    dgu, dw8, dcb = _ffn_mid_bwd(gu, gc, dxb, w_out, w8, f"ffn_mid_bwd{tag}")
    dw_in_t = _mm_tn(dgu, h, f"mm_ffn_dwin{tag}")
    dxp, dxpb, dg = _mm_rms_bwd([(dgu, w_in_t, 0)], xin, g, dx, f"mm_ffn_dh{tag}")
    return dxp, dxpb, dg, dw_in_t, dw8[:3], dcb, dw_out


def kernel(x, norm_mix, norm_ffn, attn_w_in, attn_w_out, relpos_table, q_norm_a, k_norm_a, q_norm_b, k_norm_b, sinks, ssm_w_in, ssm_conv_w, ssm_conv_b, ssm_dt_bias, ssm_a_log, ssm_d, ssm_norm, ssm_w_out, ffn_w_in, ffn_conv_w, ffn_conv_b, ffn_w_out, loss_target, m_norm_mix, m_norm_ffn, m_attn_w_in, m_attn_w_out, m_relpos_table, m_q_norm_a, m_k_norm_a, m_q_norm_b, m_k_norm_b, m_sinks, m_ssm_w_in, m_ssm_conv_w, m_ssm_conv_b, m_ssm_dt_bias, m_ssm_a_log, m_ssm_d, m_ssm_norm, m_ssm_w_out, m_ffn_w_in, m_ffn_conv_w, m_ffn_conv_b, m_ffn_w_out, v_norm_mix, v_norm_ffn, v_attn_w_in, v_attn_w_out, v_relpos_table, v_q_norm_a, v_k_norm_a, v_q_norm_b, v_k_norm_b, v_sinks, v_ssm_w_in, v_ssm_conv_w, v_ssm_conv_b, v_ssm_dt_bias, v_ssm_a_log, v_ssm_d, v_ssm_norm, v_ssm_w_out, v_ffn_w_in, v_ffn_conv_w, v_ffn_conv_b, v_ffn_w_out):
    w = dict(norm_mix=norm_mix, norm_ffn=norm_ffn, attn_w_in=attn_w_in, attn_w_out=attn_w_out, relpos_table=relpos_table,
             q_norm_a=q_norm_a, k_norm_a=k_norm_a, q_norm_b=q_norm_b, k_norm_b=k_norm_b, sinks=sinks, ssm_w_in=ssm_w_in,
             ssm_conv_w=ssm_conv_w, ssm_conv_b=ssm_conv_b, ssm_dt_bias=ssm_dt_bias, ssm_a_log=ssm_a_log, ssm_d=ssm_d,
             ssm_norm=ssm_norm, ssm_w_out=ssm_w_out, ffn_w_in=ffn_w_in, ffn_conv_w=ffn_conv_w, ffn_conv_b=ffn_conv_b,
             ffn_w_out=ffn_w_out)
    mom = dict(norm_mix=m_norm_mix, norm_ffn=m_norm_ffn, attn_w_in=m_attn_w_in, attn_w_out=m_attn_w_out,
               relpos_table=m_relpos_table, q_norm_a=m_q_norm_a, k_norm_a=m_k_norm_a, q_norm_b=m_q_norm_b,
               k_norm_b=m_k_norm_b, sinks=m_sinks, ssm_w_in=m_ssm_w_in, ssm_conv_w=m_ssm_conv_w, ssm_conv_b=m_ssm_conv_b,
               ssm_dt_bias=m_ssm_dt_bias, ssm_a_log=m_ssm_a_log, ssm_d=m_ssm_d, ssm_norm=m_ssm_norm, ssm_w_out=m_ssm_w_out,
               ffn_w_in=m_ffn_w_in, ffn_conv_w=m_ffn_conv_w, ffn_conv_b=m_ffn_conv_b, ffn_w_out=m_ffn_w_out)
    var = dict(norm_mix=v_norm_mix, norm_ffn=v_norm_ffn, attn_w_in=v_attn_w_in, attn_w_out=v_attn_w_out,
               relpos_table=v_relpos_table, q_norm_a=v_q_norm_a, k_norm_a=v_k_norm_a, q_norm_b=v_q_norm_b,
               k_norm_b=v_k_norm_b, sinks=v_sinks, ssm_w_in=v_ssm_w_in, ssm_conv_w=v_ssm_conv_w, ssm_conv_b=v_ssm_conv_b,
               ssm_dt_bias=v_ssm_dt_bias, ssm_a_log=v_ssm_a_log, ssm_d=v_ssm_d, ssm_norm=v_ssm_norm, ssm_w_out=v_ssm_w_out,
               ffn_w_in=v_ffn_w_in, ffn_conv_w=v_ffn_conv_w, ffn_conv_b=v_ffn_conv_b, ffn_w_out=v_ffn_w_out)

    def piece(n, l):
        return (w[n][l].T if _AX2[n] == 1 else w[n][l]).astype(BF16)

    def gather_of(names_layers):
        return _Gather2([piece(n, l) for n, l in names_layers])

    def joined(got):
        return [g.reshape(-1, D_MODEL) for g in got]

    first = [('attn_w_in', 0), ('attn_w_out', 0)]
    got = _exchange(_Gather2([piece(n, l) for n, l in first] + [_pack([w[n] for n in _SMALL], F32)]), "gather_attn")
    wt_attn_in, w_attn_out = joined(got[:2])
    full = {}
    for n, g in zip(_SMALL, _unpack(got[2], [w[n].shape for n in _SMALL], lead=(N_DEV,))):
        full[n] = _join(g, _SHARD_AX[n])
    ssm_cw8 = _rows8(full['ssm_conv_w'][0])
    ssm_cb = full['ssm_conv_b']
    ssm_nw = full['ssm_norm']
    ffn_cw8 = [_rows8(full['ffn_conv_w'][l]) for l in range(2)]
    ffn_cb = [ffn_conv_b[l:l + 1] for l in range(2)]

    x0 = x[0]
    target = loss_target[0]
    t = x0.shape[0]

    g_mix0, g_mix1 = norm_mix[0:1], norm_mix[1:2]
    g_ffn0, g_ffn1 = norm_ffn[0:1], norm_ffn[1:2]
    proj, h0 = _rms_mm(x0, g_mix0, wt_attn_in, ATTN_PROJ, "mm_attn_in", F32)
    hn_w = jnp.concatenate([jnp.tile(v, (1, 2)) for v in (q_norm_a, k_norm_a, q_norm_b, k_norm_b)], axis=0)
    qa, kpa, vpa, qb, kpb, vpb = _headnorm_fwd(proj, hn_w, "headnorm")
    table = jnp.pad(relpos_table[0], ((0, 0), (0, REL_W - (2 * MAX_REL + 1))))
    bias_a = jnp.where(_band_mask(A_PREV, PAD_A)[None], jnp.transpose(_relpos_fwd(table, "relpos_bias"), (1, 0, 2)), NEG)
    rel_b = jnp.arange(TQ)[:, None] - (jnp.arange(PAD_B + TQ)[None, :] - PAD_B)
    slopes = 2.0 ** (-8.0 * jnp.arange(1, N_HEADS + 1, dtype=F32) / N_HEADS)
    bias_b = jnp.where(_band_mask(B_PREV, PAD_B)[None], -slopes[:, None, None] * jnp.abs(rel_b).astype(F32)[None], NEG)
    no_sinks = jnp.full((N_HEADS,), NEG, F32)
    ffn0_w, ssm_w, ffn1_w = [('ffn_w_in', 0), ('ffn_w_out', 0)], [('ssm_w_in', 0), ('ssm_w_out', 0)], [('ffn_w_in', 1), ('ffn_w_out', 1)]
    oa, stats_a, got = _attn_fwd(qa, kpa, vpa, bias_a, no_sinks, PAD_A, "attn_a", comm=gather_of(ffn0_w + ssm_w))
    wt_ffn_in0, w_ffn_out0, wt_ssm_in, w_ssm_out = joined(got)
    ob, stats_b, _ = _attn_fwd(qb, kpb, vpb, bias_b, sinks[0], PAD_B, "attn_b")
    wt_ssm_dt = jnp.pad(wt_ssm_in[ZX:], ((0, LANES - SSM_HEADS), (0, 0)))
    x1 = _mm_pair(oa, ob, w_attn_out, x0, "mm_attn_out")
    a0, ffn0_saved = _ffn_fwd(x1, g_ffn0, wt_ffn_in0, ffn_cw8[0], ffn_cb[0], "0")
    x2 = _mm(a0, w_ffn_out0, "mm_ffn_out0", res=x1)

    dt_bias = _lanes128(ssm_dt_bias[0])
    alog = _lanes128(ssm_a_log[0])
    dexp = jnp.repeat(ssm_d[0], HEAD_DIM).reshape(1, D_INNER)
    zx, h2, xbc, conv_pre, dtraw, dt = _ssm_in_pre(x2, g_mix1, wt_ssm_in, wt_ssm_dt, ssm_cw8, ssm_cb, dt_bias, "mm_ssm_in")
    (y, sprev, y4), got = _ssd_fwd(xbc, dt, alog, zx, dexp, ssm_nw, "ssd_fwd", comm=gather_of(ffn1_w))
    wt_ffn_in1, w_ffn_out1 = joined(got)
    x3 = _mm(y4, w_ssm_out, "mm_ssm_out", res=x2)
    a1, ffn1_saved = _ffn_fwd(x3, g_ffn1, wt_ffn_in1, ffn_cw8[1], ffn_cb[1], "1")

    dx4, dx4b, sq = _mm_loss(a1, w_ffn_out1, x3, target, "mm_ffn_out1_loss")
    loss = lax.psum(0.5 * jnp.sum(sq) / D_MODEL, ("x", "y", "c"))

    grads = {}

    def scatter_of(grads_2d):
        return _Comm([g.reshape(N_DEV, -1, D_MODEL) for g in grads_2d], [True] * len(grads_2d))

    dx3, dx3b, dg_ffn1, dwtin1, dcw1, dcb1, dwout1 = _ffn_bwd(
        dx4, dx4b, x3, g_ffn1, wt_ffn_in1, ffn_cw8[1], w_ffn_out1, ffn1_saved, "1")

    dy4 = _mm(dx3b, w_ssm_out, "mm_ssm_dy", trans_b=True)
    dw_ssm_out = _mm_tn(y4, dx3b, "mm_ssm_dwout")
    (dxbc, ddt, dalog, dz, dd_lane, dnw), parts_ffn1 = _ssd_bwd(
        xbc, dt, alog, sprev, dy4, y, zx, dexp, ssm_nw, "ssd_bwd", comm=scatter_of([dwtin1, dwout1]))
    dxr, dcw_s, dcb_s = _ssm_pre_bwd(zx, conv_pre, dxbc, ssm_cw8, "ssm_pre_bwd")
    ddtraw, ddtb = _dt_bwd(dtraw, dt_bias, ddt, "ssm_dt_bwd")
    dwt_ssm_in = jnp.concatenate([
        _mm_tn(dz, h2, "mm_ssm_dwin_z"), _mm_tn(dxr, h2, "mm_ssm_dwin_x"),
        _mm_tn(ddtraw, h2, "mm_ssm_dwin_dt")[:SSM_HEADS]], axis=0)
    dx2, dx2b, dg_mix1 = _mm_rms_bwd([(dz, wt_ssm_in, 0), (dxr, wt_ssm_in, D_INNER), (ddtraw, wt_ssm_dt, 0)],
                                     x2, g_mix1, dx3, "mm_ssm_dh")
    grads['ssm_conv_w'] = dcw_s[:4][None]
    grads['ssm_conv_b'] = dcb_s
    grads['ssm_norm'] = dnw
    grads['ssm_dt_bias'] = ddtb[:, :SSM_HEADS]
    grads['ssm_a_log'] = dalog[:, :SSM_HEADS]
    grads['ssm_d'] = jnp.sum(dd_lane.reshape(SSM_HEADS, HEAD_DIM), axis=1)[None]

    dx1, dx1b, dg_ffn0, dwtin0, dcw0, dcb0, dwout0 = _ffn_bwd(
        dx2, dx2b, x1, g_ffn0, wt_ffn_in0, ffn_cw8[0], w_ffn_out0, ffn0_saved, "0")
    grads['ffn_conv_w'] = jnp.stack([dcw0, dcw1])
    grads['ffn_conv_b'] = jnp.concatenate([dcb0, dcb1], axis=0)
    grads['norm_ffn'] = jnp.concatenate([dg_ffn0, dg_ffn1], axis=0)

    do = _mm(dx1b, w_attn_out, "mm_attn_do", out_dtype=BF16, trans_b=True)
    dw_attn_out = jnp.concatenate([_mm_tn(oa, dx1b, "mm_attn_dwout_a"), _mm_tn(ob, dx1b, "mm_attn_dwout_b")], axis=0)
    (dqa, dkpa, dvpa, dbias_a, _), parts_ssm = _attn_bwd(
        qa, kpa, vpa, bias_a, no_sinks, do, stats_a, oa, 0, PAD_A, "attn_a_bwd",
        comm=scatter_of([dwt_ssm_in, dw_ssm_out, dw_attn_out]))
    (dqb, dkpb, dvpb, _, dsink), parts_ffn0 = _attn_bwd(
        qb, kpb, vpb, bias_b, sinks[0], do, stats_b, ob, 4, PAD_B, "attn_b_bwd", comm=scatter_of([dwtin0, dwout0]))
    grads['relpos_table'] = _relpos_bwd(jnp.transpose(dbias_a, (1, 0, 2)), "relpos_bwd")[None, :, :2 * MAX_REL + 1]
    grads['sinks'] = dsink[:, :2, 0].reshape(1, N_HEADS)
    dproj, dhn = _headnorm_bwd(proj, hn_w, dqa, dkpa, dvpa, dqb, dkpb, dvpb, "headnorm_bwd")
    dhn = dhn[:, :HEAD_DIM] + dhn[:, HEAD_DIM:]
    for k, n in enumerate(('q_norm_a', 'k_norm_a', 'q_norm_b', 'k_norm_b')):
        grads[n] = dhn[k:k + 1]
    dwt_attn_in = _mm_tn(dproj, h0, "mm_attn_dwin")
    dx0, _, dg_mix0, parts_attn_in = _mm_rms_bwd([(dproj, wt_attn_in, 0)], x0, g_mix0, dx1, "mm_attn_dh",
                                                 comm=scatter_of([dwt_attn_in]))
    grads['norm_mix'] = jnp.concatenate([dg_mix0, dg_mix1], axis=0)

    def summed_t(parts, name):
        return _sum_parts(parts, name).T[None]

    sm_shapes = [w[n].shape for n in _SMALL]
    rp_shapes = [w[n].shape for n in _REPL]
    recv = _exchange(_Comm(
        [_pack([_split(grads[n], _SHARD_AX[n]) for n in _SMALL], F32, lead=(N_DEV,)), _pack([grads[n] for n in _REPL], F32)],
        [True, False]), "exchange_small")
    big_parts = {
        'attn_w_in': [summed_t(parts_attn_in[0], "sum_attn_w_in")], 'attn_w_out': [parts_ssm[2]],
        'ssm_w_in': [summed_t(parts_ssm[0], "sum_ssm_w_in")], 'ssm_w_out': [parts_ssm[1]],
        'ffn_w_in': [summed_t(parts_ffn0[0], "sum_ffn_w_in0"), summed_t(parts_ffn1[0], "sum_ffn_w_in1")],
        'ffn_w_out': [parts_ffn0[1], parts_ffn1[1]],
    }
    res = [{}, {}, {}, {}]
    for n in _BIG:
        for kind, a in enumerate(_adamw(big_parts[n], w[n], mom[n], var[n], f"adamw_{n}")):
            res[kind][n] = a
    for names, shapes, parts in ((_SMALL, sm_shapes, recv[0]), (_REPL, rp_shapes, recv[1])):
        outs = _adamw([parts], _pack([w[n] for n in names], F32)[None], _pack([mom[n] for n in names], F32)[None],
                      _pack([var[n] for n in names], F32)[None], "adamw_" + ("small" if names is _SMALL else "replicated"))
        for kind, flat in enumerate(outs):
            for n, a in zip(names, _unpack(flat[0], shapes)):
                res[kind][n] = a
    return (loss, dx0[None], *[res[0][n] for n in _WEIGHTS], *[res[1][n] for n in _WEIGHTS],
            *[res[2][n] for n in _WEIGHTS], *[res[3][n] for n in _WEIGHTS])
```

```python
import jax
import jax.numpy as jnp
from jax import lax
from jax.experimental import pallas as pl
from jax.experimental.pallas import tpu as pltpu

F32 = jnp.float32
BF16 = jnp.bfloat16
HI = lax.Precision.HIGHEST
MESH = pl.DeviceIdType.MESH
NEG = -1e30

N_DEV = 8
D_MODEL = 1024
EPS = 1e-6
CHUNK = 64
HEAD_DIM = 64
N_HEADS = 8
A_PREV = 8
B_PREV = 2
MAX_REL = 256
TQ = 2 * CHUNK
ATT_SUB = 32
PAD_A = A_PREV * CHUNK
PAD_B = B_PREV * CHUNK
REL_W = PAD_A + TQ
D_ATT = N_HEADS * HEAD_DIM
COL_QA, COL_KA, COL_VA, COL_QB = 0, D_ATT, 2 * D_ATT, 3 * D_ATT
COL_KB, COL_VB = 4 * D_ATT, 4 * D_ATT + 2 * HEAD_DIM
ATTN_PROJ = COL_VB + 2 * HEAD_DIM
D_INNER = 2048
SSM_HEADS = 32
SSM_GROUPS = 4
SSM_STATE = 128
XBC = D_INNER + 2 * SSM_GROUPS * SSM_STATE
ZX = D_INNER + XBC
D_FF = 2816
SSD_L = 128
SSD_L_BWD = 2 * SSD_L
LANES = 128
VMEM_LIMIT = 56 << 20

ADAM_LR, ADAM_B1, ADAM_B2, ADAM_EPS, ADAM_WD, ADAM_STEP = 0.001, 0.9, 0.999, 1e-08, 0.01, 10


def _cp(sem=None):
    return pltpu.CompilerParams(dimension_semantics=sem, vmem_limit_bytes=VMEM_LIMIT)


def _dot(a, b, ca=1, cb=0, prec=None):
    return lax.dot_general(a, b, (((ca,), (cb,)), ((), ())), preferred_element_type=F32, precision=prec)


def _pick(n, cands):
    for c in cands:
        if n % c == 0:
            return c
    return n


def _lo_mask():
    return lax.broadcasted_iota(jnp.int32, (1, LANES), 1) < HEAD_DIM


_TN_CHUNKS = (1408, 1536, 1152, 1024, 512, 256, 128)


TN_MAX_ROWS = 3072
MM_WIDE = 2304


def _mm_tn(a, b, name):
    kdim, m = a.shape
    n = b.shape[1]
    assert b.shape[0] == kdim, (a.shape, b.shape)
    mb = m if m <= TN_MAX_ROWS else m // 2
    tn = _pick(n, _TN_CHUNKS)
    tk = _pick(kdim, (512, 256, 128))
    nk = kdim // tk

    def body(a_ref, b_ref, o_ref, acc):
        k = pl.program_id(1)

        @pl.when(k == 0)
        def _():
            acc[...] = jnp.zeros_like(acc)

        av = a_ref[...]
        for c in range(0, n, tn):
            acc[:, c:c + tn] += _dot(av, b_ref[:, c:c + tn], 0, 0)

        @pl.when(k == nk - 1)
        def _():
            o_ref[...] = acc[...].astype(BF16)

    return pl.pallas_call(
        body, name=name, grid=(m // mb, nk),
        in_specs=[pl.BlockSpec((tk, mb), lambda j, k: (k, j)), pl.BlockSpec((tk, n), lambda j, k: (k, 0))],
        out_specs=pl.BlockSpec((mb, n), lambda j, k: (j, 0)), out_shape=jax.ShapeDtypeStruct((m, n), BF16),
        scratch_shapes=[pltpu.VMEM((mb, n), F32)], compiler_params=_cp(("parallel", "arbitrary")),
    )(a, b)


def _mm(a, b, name, out_dtype=F32, res=None, trans_b=False):
    m, kdim = a.shape
    n = b.shape[0] if trans_b else b.shape[1]
    assert (b.shape[1] if trans_b else b.shape[0]) == kdim, (a.shape, b.shape)
    tn = _pick(n, _TN_CHUNKS)
    tm = _pick(m, (256, 128) if n > MM_WIDE else (512, 256, 128))

    def body(*refs):
        if res is None:
            a_ref, b_ref, o_ref = refs
        else:
            a_ref, b_ref, r_ref, o_ref = refs
        av = a_ref[...]
        for c in range(0, n, tn):
            r = _dot(av, b_ref[c:c + tn, :], 1, 1) if trans_b else _dot(av, b_ref[:, c:c + tn], 1, 0)
            if res is not None:
                r = r + r_ref[:, c:c + tn]
            o_ref[:, c:c + tn] = r.astype(out_dtype)

    in_specs = [pl.BlockSpec((tm, kdim), lambda i: (i, 0)), pl.BlockSpec(b.shape, lambda i: (0, 0))]
    args = [a, b]
    if res is not None:
        in_specs.append(pl.BlockSpec((tm, n), lambda i: (i, 0)))
        args.append(res)
    return pl.pallas_call(
        body, name=name, grid=(m // tm,), in_specs=in_specs, out_specs=pl.BlockSpec((tm, n), lambda i: (i, 0)),
        out_shape=jax.ShapeDtypeStruct((m, n), out_dtype), compiler_params=_cp(("parallel",)),
    )(*args)


def _mm_pair(a1, a2, b, res, name):
    m, k1 = a1.shape
    k2 = a2.shape[1]
    n = b.shape[1]
    assert b.shape[0] == k1 + k2
    tm = _pick(m, (512, 256, 128))

    def body(a1_ref, a2_ref, b_ref, r_ref, o_ref):
        o_ref[...] = _dot(a1_ref[...], b_ref[:k1, :], 1, 0) + _dot(a2_ref[...], b_ref[k1:, :], 1, 0) + r_ref[...]

    row = pl.BlockSpec((tm, n), lambda i: (i, 0))
    return pl.pallas_call(
        body, name=name, grid=(m // tm,),
        in_specs=[pl.BlockSpec((tm, k1), lambda i: (i, 0)), pl.BlockSpec((tm, k2), lambda i: (i, 0)),
                  pl.BlockSpec(b.shape, lambda i: (0, 0)), row],
        out_specs=row, out_shape=jax.ShapeDtypeStruct((m, n), F32), compiler_params=_cp(("parallel",)),
    )(a1, a2, b, res)


def _rms_mm(x, g, bt, n, name, out_dtype):
    t, d = x.shape
    tn = _pick(n, _TN_CHUNKS)
    tm = _pick(t, (256, 128))

    def body(x_ref, g_ref, b_ref, o_ref, h_ref):
        xv = x_ref[...]
        r = lax.rsqrt(jnp.mean(xv * xv, axis=-1, keepdims=True) + EPS)
        h = (xv * r * g_ref[...]).astype(BF16)
        h_ref[...] = h
        for c in range(0, n, tn):
            o_ref[:, c:c + tn] = _dot(h, b_ref[c:c + tn, :], 1, 1).astype(out_dtype)

    row = pl.BlockSpec((tm, d), lambda i: (i, 0))
    return pl.pallas_call(
        body, name=name, grid=(t // tm,),
        in_specs=[row, pl.BlockSpec((1, d), lambda i: (0, 0)), pl.BlockSpec(bt.shape, lambda i: (0, 0))],
        out_specs=[pl.BlockSpec((tm, n), lambda i: (i, 0)), row],
        out_shape=[jax.ShapeDtypeStruct((t, n), out_dtype), jax.ShapeDtypeStruct((t, d), BF16)],
        compiler_params=_cp(("parallel",)),
    )(x, g, bt)


def _mm_rms_bwd(terms, x, g, dres, name, comm=None):
    t, d = x.shape
    tm = _pick(t, (256, 128))
    weights = []
    for _, b, _ in terms:
        if not any(b is wgt for wgt in weights):
            weights.append(b)
    which = [next(k for k, wgt in enumerate(weights) if wgt is b) for _, b, _ in terms]
    na, nw = len(terms), len(weights)

    def body(*refs):
        a_refs, w_refs = refs[:na], refs[na:na + nw]
        x_ref, g_ref, dr_ref, dx_ref, dxb_ref, dg_ref = refs[na + nw:]
        dhv = None
        for (a, _, row), a_ref, k in zip(terms, a_refs, which):
            part = _dot(a_ref[...], w_refs[k][row:row + a.shape[1], :], 1, 0)
            dhv = part if dhv is None else dhv + part
        xv = x_ref[...]
        r = lax.rsqrt(jnp.mean(xv * xv, axis=-1, keepdims=True) + EPS)
        xh = xv * r
        dxh = dhv * g_ref[...]
        dx = dr_ref[...] + r * (dxh - xh * jnp.mean(dxh * xh, axis=-1, keepdims=True))
        dx_ref[...] = dx
        dxb_ref[...] = dx.astype(BF16)

        @pl.when(pl.program_id(0) == 0)
        def _():
            dg_ref[...] = jnp.zeros_like(dg_ref)

        dg_ref[...] += jnp.sum(dhv * xh, axis=0, keepdims=True)

    row = pl.BlockSpec((tm, d), lambda i: (i, 0))
    vec = pl.BlockSpec((1, d), lambda i: (0, 0))
    in_specs = ([pl.BlockSpec((tm, a.shape[1]), lambda i: (i, 0)) for a, _, _ in terms]
                + [pl.BlockSpec(wgt.shape, lambda i: (0, 0)) for wgt in weights])
    outs, got = _call(
        body, name=name, grid=(t // tm,), in_specs=in_specs + [row, vec, row], out_specs=[row, row, vec],
        out_shape=[jax.ShapeDtypeStruct((t, d), F32), jax.ShapeDtypeStruct((t, d), BF16), jax.ShapeDtypeStruct((1, d), F32)],
        args=(*[a for a, _, _ in terms], *weights, x, g, dres), sem=("arbitrary",), comm=comm)
    return (*outs, got) if comm is not None else tuple(outs)


def _mm_loss(a, b, res, target, name):
    t, kdim = a.shape
    d = b.shape[1]
    tm = _pick(t, (512, 256, 128))

    def body(a_ref, b_ref, r_ref, t_ref, dy_ref, dyb_ref, acc_ref):
        @pl.when(pl.program_id(0) == 0)
        def _():
            acc_ref[...] = jnp.zeros_like(acc_ref)

        err = _dot(a_ref[...], b_ref[...], 1, 0) + r_ref[...] - t_ref[...]
        dy = err * (1.0 / d)
        dy_ref[...] = dy
        dyb_ref[...] = dy.astype(BF16)
        acc_ref[...] += jnp.sum(err * err, axis=0, keepdims=True)

    row = pl.BlockSpec((tm, d), lambda i: (i, 0))
    vec = pl.BlockSpec((1, d), lambda i: (0, 0))
    return pl.pallas_call(
        body, name=name, grid=(t // tm,),
        in_specs=[pl.BlockSpec((tm, kdim), lambda i: (i, 0)), pl.BlockSpec((kdim, d), lambda i: (0, 0)), row, row],
        out_specs=[row, row, vec],
        out_shape=[jax.ShapeDtypeStruct((t, d), F32), jax.ShapeDtypeStruct((t, d), BF16), jax.ShapeDtypeStruct((1, d), F32)],
        compiler_params=_cp(("arbitrary",)),
    )(a, b, res, target)


def _head_sums(v):
    ri = lax.broadcasted_iota(jnp.int32, (LANES, LANES), 0) // HEAD_DIM
    ci = lax.broadcasted_iota(jnp.int32, (LANES, LANES), 1) // HEAD_DIM
    ones = (ri == ci).astype(BF16)
    hi = v.astype(BF16)
    lo_part = (v - hi.astype(F32)).astype(BF16)
    return _dot(hi, ones, 1, 0) + _dot(lo_part, ones, 1, 0)


def _head_rms(xs):
    r = lax.rsqrt(_head_sums(xs * xs) * (1.0 / HEAD_DIM) + EPS)
    return xs * r, r


def _head_rms_bwd(xs, w, dy):
    xh, r = _head_rms(xs)
    dxh = dy * w
    mm = _head_sums(dxh * xh) * (1.0 / HEAD_DIM)
    return r * (dxh - xh * mm), dy * xh


_QSCALE = HEAD_DIM ** -0.5


def _headnorm_fwd(proj, ws, name):
    t = proj.shape[0]
    tm = TQ
    lead = PAD_A // tm
    leadb = PAD_B // tm

    def body(p_ref, w_ref, qa_ref, ka_ref, va_ref, qb_ref, kb_ref, vb_ref):
        data = pl.program_id(0) >= lead
        lo = _lo_mask()

        def put(ref, c, val):
            ref[:, c:c + val.shape[1]] = jnp.where(data, val, 0.0).astype(BF16)

        def per_query_head(slab):
            other = pltpu.roll(slab, HEAD_DIM, 1)
            e0, e1 = jnp.where(lo, slab, other), jnp.where(lo, other, slab)
            return jnp.concatenate([e0, e0, e1, e1], axis=1)

        for s in range(D_ATT // LANES):
            c = LANES * s
            xh, _ = _head_rms(p_ref[:, COL_QA + c:COL_QA + c + LANES])
            qa_ref[:, c:c + LANES] = (xh * w_ref[0:1, :] * _QSCALE).astype(BF16)
            xh, _ = _head_rms(p_ref[:, COL_KA + c:COL_KA + c + LANES])
            put(ka_ref, c, xh * w_ref[1:2, :])
            xh, _ = _head_rms(p_ref[:, COL_QB + c:COL_QB + c + LANES])
            qb_ref[:, c:c + LANES] = (xh * w_ref[2:3, :] * _QSCALE).astype(BF16)
        put(va_ref, 0, p_ref[:, COL_VA:COL_VA + D_ATT])
        xh, _ = _head_rms(p_ref[:, COL_KB:COL_KB + LANES])
        put(kb_ref, 0, per_query_head(xh * w_ref[3:4, :]))
        put(vb_ref, 0, per_query_head(p_ref[:, COL_VB:COL_VB + LANES]))

    src = lambda i: jnp.maximum(i - lead, 0)
    wide = pl.BlockSpec((tm, D_ATT), lambda i: (src(i), 0))
    pad_a = pl.BlockSpec((tm, D_ATT), lambda i: (i, 0))
    pad_b = pl.BlockSpec((tm, D_ATT), lambda i: (jnp.maximum(i - lead + leadb, 0), 0))
    sd = lambda rows: jax.ShapeDtypeStruct((rows, D_ATT), BF16)
    return pl.pallas_call(
        body, name=name, grid=(t // tm + lead,),
        in_specs=[pl.BlockSpec((tm, ATTN_PROJ), lambda i: (src(i), 0)), pl.BlockSpec((4, LANES), lambda i: (0, 0))],
        out_specs=[wide, pad_a, pad_a, wide, pad_b, pad_b],
        out_shape=[sd(t), sd(t + PAD_A), sd(t + PAD_A), sd(t), sd(t + PAD_B), sd(t + PAD_B)],
        compiler_params=_cp(("arbitrary",)),
    )(proj, ws)


def _headnorm_bwd(proj, ws, dqa, dkpa, dvpa, dqb, dkpb, dvpb, name):
    t = proj.shape[0]
    tm = TQ
    offa, offb = PAD_A // tm, PAD_B // tm

    def body(p_ref, w_ref, dqa_ref, dka_ref, dva_ref, dqb_ref, dkb_ref, dvb_ref, dp_ref, dw_ref):
        i = pl.program_id(0)
        lo = _lo_mask()

        @pl.when(i == 0)
        def _():
            dw_ref[...] = jnp.zeros_like(dw_ref)

        acc = [jnp.zeros((1, LANES), F32) for _ in range(4)]
        for s in range(D_ATT // LANES):
            c = LANES * s
            dx, dwl = _head_rms_bwd(p_ref[:, COL_QA + c:COL_QA + c + LANES], w_ref[0:1, :], dqa_ref[:, c:c + LANES] * _QSCALE)
            dp_ref[:, COL_QA + c:COL_QA + c + LANES] = dx.astype(BF16)
            acc[0] += jnp.sum(dwl, axis=0, keepdims=True)
            dx, dwl = _head_rms_bwd(p_ref[:, COL_KA + c:COL_KA + c + LANES], w_ref[1:2, :], dka_ref[:, c:c + LANES])
            dp_ref[:, COL_KA + c:COL_KA + c + LANES] = dx.astype(BF16)
            acc[1] += jnp.sum(dwl, axis=0, keepdims=True)
            dx, dwl = _head_rms_bwd(p_ref[:, COL_QB + c:COL_QB + c + LANES], w_ref[2:3, :], dqb_ref[:, c:c + LANES] * _QSCALE)
            dp_ref[:, COL_QB + c:COL_QB + c + LANES] = dx.astype(BF16)
            acc[2] += jnp.sum(dwl, axis=0, keepdims=True)
        dp_ref[:, COL_VA:COL_VA + D_ATT] = dva_ref[...].astype(BF16)

        def group_sum(ref):
            s0 = ref[:, 0:LANES] + ref[:, LANES:2 * LANES]
            s1 = ref[:, 2 * LANES:3 * LANES] + ref[:, 3 * LANES:4 * LANES]
            s0 = s0 + pltpu.roll(s0, HEAD_DIM, 1)
            s1 = s1 + pltpu.roll(s1, HEAD_DIM, 1)
            return jnp.where(lo, s0, s1)

        dx, dwl = _head_rms_bwd(p_ref[:, COL_KB:COL_KB + LANES], w_ref[3:4, :], group_sum(dkb_ref))
        dp_ref[:, COL_KB:COL_KB + LANES] = dx.astype(BF16)
        acc[3] += jnp.sum(dwl, axis=0, keepdims=True)
        dp_ref[:, COL_VB:COL_VB + LANES] = group_sum(dvb_ref).astype(BF16)
        for n in range(4):
            dw_ref[n:n + 1, :] += acc[n]

    wide = pl.BlockSpec((tm, D_ATT), lambda i: (i, 0))
    pa = pl.BlockSpec((tm, D_ATT), lambda i: (i + offa, 0))
    pb = pl.BlockSpec((tm, D_ATT), lambda i: (i + offb, 0))
    whole = pl.BlockSpec((tm, ATTN_PROJ), lambda i: (i, 0))
    return pl.pallas_call(
        body, name=name, grid=(t // tm,),
        in_specs=[whole, pl.BlockSpec((4, LANES), lambda i: (0, 0)), wide, pa, pa, wide, pb, pb],
        out_specs=[whole, pl.BlockSpec((4, LANES), lambda i: (0, 0))],
        out_shape=[jax.ShapeDtypeStruct((t, ATTN_PROJ), BF16), jax.ShapeDtypeStruct((4, LANES), F32)],
        compiler_params=_cp(("arbitrary",)),
    )(proj, ws, dqa, dkpa, dvpa, dqb, dkpb, dvpb)


ROLL_W = 1024


def _rel_onehot():
    r_io = lax.broadcasted_iota(jnp.int32, (REL_W, ROLL_W), 0)
    m_io = lax.broadcasted_iota(jnp.int32, (REL_W, ROLL_W), 1)
    return (r_io == jnp.clip(REL_W - 1 - m_io, -MAX_REL, MAX_REL) + MAX_REL).astype(F32)


def _relpos_fwd(table, name):
    def body(t_ref, o_ref):
        rr = _dot(t_ref[...], _rel_onehot(), 1, 0, HI)

        def step(q, c):
            o_ref[q] = pltpu.roll(rr, (ROLL_W - (TQ - 1) + q) % ROLL_W, 1)[:, :REL_W]
            return c

        lax.fori_loop(0, TQ, step, 0)

    return pl.pallas_call(
        body, name=name, out_shape=jax.ShapeDtypeStruct((TQ, N_HEADS, REL_W), F32),
        in_specs=[pl.BlockSpec(memory_space=pltpu.VMEM)], out_specs=pl.BlockSpec(memory_space=pltpu.VMEM),
        compiler_params=_cp(),
    )(table)


def _relpos_bwd(dbias_t, name):
    def body(d_ref, o_ref):
        def step(q, acc):
            row = jnp.concatenate([d_ref[q], jnp.zeros((N_HEADS, ROLL_W - REL_W), F32)], axis=1)
            return acc + pltpu.roll(row, TQ - 1 - q, 1)

        drr = lax.fori_loop(0, TQ, step, jnp.zeros((N_HEADS, ROLL_W), F32))
        o_ref[...] = _dot(drr, _rel_onehot(), 1, 1, HI)

    return pl.pallas_call(
        body, name=name, out_shape=jax.ShapeDtypeStruct((N_HEADS, REL_W), F32),
        in_specs=[pl.BlockSpec(memory_space=pltpu.VMEM)], out_specs=pl.BlockSpec(memory_space=pltpu.VMEM),
        compiler_params=_cp(),
    )(dbias_t)


def _attn_scores(qe, kw, bias, kvalid):
    return jnp.where(kvalid, _dot(qe, kw, 1, 1) + bias, NEG)


def _stat_cols(stats, e):
    return stats[:, 64 * e:64 * e + 1], stats[:, 64 * e + 32:64 * e + 33]


def _attn_fwd(q, kp, vp, bias, sinks, pad, name, comm=None):
    t, hd = q.shape
    w = pad + TQ

    def body(sink_ref, q_ref, k_ref, v_ref, b_ref, o_ref, st_ref):
        hp, i = pl.program_id(0), pl.program_id(1)
        lo = _lo_mask()
        lane = lax.broadcasted_iota(jnp.int32, (1, LANES), 1)
        for j in range(ATT_SUB):
            start = pl.multiple_of((i * ATT_SUB + j) * TQ, TQ)
            qv = q_ref[TQ * j:TQ * (j + 1), :]
            kw = k_ref[pl.ds(start, w), :]
            vw = v_ref[pl.ds(start, w), :]
            kvalid = (start + lax.broadcasted_iota(jnp.int32, (1, w), 1)) >= pad
            outs, ms, ls = [], [], []
            for e in range(2):
                sel = lo if e == 0 else jnp.logical_not(lo)
                qe = jnp.where(sel, qv, jnp.zeros_like(qv))
                snk = sink_ref[2 * hp + e]
                s = _attn_scores(qe, kw, b_ref[e], kvalid)
                m = jnp.maximum(jnp.max(s, axis=-1, keepdims=True), snk)
                acc = _dot(jnp.exp(s - m).astype(BF16), jnp.where(sel, vw, jnp.ones_like(vw)), 1, 0)
                denom = acc[:, 64 * (1 - e):64 * (1 - e) + 1] + jnp.exp(snk - m)
                outs.append(acc * (1.0 / denom))
                ms.append(m)
                ls.append(denom)
            o_ref[TQ * j:TQ * (j + 1), :] = jnp.where(lo, outs[0], outs[1]).astype(BF16)
            st_ref[TQ * j:TQ * (j + 1), :] = jnp.where(lane < 32, ms[0], jnp.where(lane < 64, ls[0],
                                                                                 jnp.where(lane < 96, ms[1], ls[1])))

    full = pl.BlockSpec((t + pad, LANES), lambda h, i: (0, h))
    tile = pl.BlockSpec((ATT_SUB * TQ, LANES), lambda h, i: (i, h))
    (o, stats), got = _call(
        body, name=name, grid=(hd // LANES, t // (ATT_SUB * TQ)),
        in_specs=[pl.BlockSpec(memory_space=pltpu.SMEM), tile, full, full, pl.BlockSpec((2, TQ, w), lambda h, i: (h, 0, 0))],
        out_specs=[tile, tile], out_shape=[jax.ShapeDtypeStruct((t, hd), BF16), jax.ShapeDtypeStruct((t, hd), F32)],
        args=(sinks, q, kp, vp, bias), sem=("parallel", "arbitrary"), comm=comm)
    return o, stats, got


def _attn_bwd(q, kp, vp, bias, sinks, do, stats, o, col_off, pad, name, comm=None):
    t, hd = q.shape
    w = pad + TQ
    nhp = hd // LANES

    def body(sink_ref, q_ref, k_ref, v_ref, b_ref, do_ref, st_ref, o_ref, dq_ref, dk_ref, dv_ref, db_ref, ds_ref):
        hp, i = pl.program_id(0), pl.program_id(1)

        @pl.when(i == 0)
        def _():
            dk_ref[...] = jnp.zeros_like(dk_ref)
            dv_ref[...] = jnp.zeros_like(dv_ref)
            db_ref[...] = jnp.zeros_like(db_ref)
            ds_ref[...] = jnp.zeros_like(ds_ref)

        lo = _lo_mask()
        row8 = lax.broadcasted_iota(jnp.int32, (8, LANES), 0)
        dbias = [None, None]
        dsink = jnp.zeros((8, LANES), F32)
        for j in range(ATT_SUB):
            start = pl.multiple_of((i * ATT_SUB + j) * TQ, TQ)
            qv = q_ref[TQ * j:TQ * (j + 1), :]
            dov = do_ref[TQ * j:TQ * (j + 1), :]
            kw = k_ref[pl.ds(start, w), :]
            vw = v_ref[pl.ds(start, w), :]
            kvalid = (start + lax.broadcasted_iota(jnp.int32, (1, w), 1)) >= pad
            stats = st_ref[TQ * j:TQ * (j + 1), :]
            od = dov.astype(F32) * o_ref[TQ * j:TQ * (j + 1), :].astype(F32)
            dqs, dkw, dvw = [], None, None
            for e in range(2):
                sel = lo if e == 0 else jnp.logical_not(lo)
                qe = jnp.where(sel, qv, jnp.zeros_like(qv))
                doe = jnp.where(sel, dov, jnp.zeros_like(dov))
                m, denom = _stat_cols(stats, e)
                inv = 1.0 / denom
                p = jnp.exp(_attn_scores(qe, kw, b_ref[e], kvalid) - m) * inv
                psink = jnp.exp(sink_ref[2 * hp + e] - m) * inv
                dp = _dot(doe, vw, 1, 1)
                delta = jnp.sum(jnp.where(sel, od, 0.0), axis=-1, keepdims=True)
                ds = p * (dp - delta)
                dbias[e] = ds if dbias[e] is None else dbias[e] + ds
                dsink = dsink + jnp.where(row8 == e, jnp.sum(-psink * delta, axis=0, keepdims=True), 0.0)
                dsb = ds.astype(BF16)
                dqs.append(_dot(dsb, kw, 1, 0))
                dk_e = _dot(dsb, qe, 0, 0)
                dv_e = _dot(p.astype(BF16), doe, 0, 0)
                dkw = dk_e if dkw is None else dkw + dk_e
                dvw = dv_e if dvw is None else dvw + dv_e
            dq_ref[TQ * j:TQ * (j + 1), :] = jnp.where(lo, dqs[0], dqs[1])
            dk_ref[pl.ds(start, w), :] += dkw
            dv_ref[pl.ds(start, w), :] += dvw
        for e in range(2):
            db_ref[e] += dbias[e]
        ds_ref[0] += dsink

    full = pl.BlockSpec((t + pad, LANES), lambda h, i: (0, h))
    tile = pl.BlockSpec((ATT_SUB * TQ, LANES), lambda h, i: (i, h))
    btile = pl.BlockSpec((2, TQ, w), lambda h, i: (h, 0, 0))
    return _call(
        body, name=name, grid=(nhp, t // (ATT_SUB * TQ)),
        in_specs=[pl.BlockSpec(memory_space=pltpu.SMEM), tile, full, full, btile,
                  pl.BlockSpec((ATT_SUB * TQ, LANES), lambda h, i: (i, h + col_off)), tile, tile],
        out_specs=[tile, full, full, btile, pl.BlockSpec((1, 8, LANES), lambda h, i: (h, 0, 0))],
        out_shape=[jax.ShapeDtypeStruct((t, hd), F32), jax.ShapeDtypeStruct((t + pad, hd), F32),
                   jax.ShapeDtypeStruct((t + pad, hd), F32), jax.ShapeDtypeStruct((N_HEADS, TQ, w), F32),
                   jax.ShapeDtypeStruct((nhp, 8, LANES), F32)],
        args=(sinks, q, kp, vp, bias, do, stats, o), sem=("parallel", "arbitrary"), comm=comm)


def _conv_apply(taps, w_ref, ktaps):
    out = taps[0] * w_ref[ktaps - 1:ktaps, :]
    for s in range(1, ktaps):
        out = out + taps[s] * w_ref[ktaps - 1 - s:ktaps - s, :]
    return out


def _sigmoid(x):
    return jax.nn.sigmoid(x)


def _silu_grad(x):
    sg = _sigmoid(x)
    return x * sg, sg * (1.0 + x * (1.0 - sg))


FFN_HALO = 16
FFN_BT = 256
FFN_BC = 1408


def _ffn_in_mid(x, g, wt, w8, b, name):
    t, d = x.shape
    f = D_FF
    tm = FFN_BT

    def body(x_ref, g_ref, b_ref, w_ref, cb_ref, gu_ref, h_ref, a_ref, gc_ref, halo_ref):
        @pl.when(pl.program_id(0) == 0)
        def _():
            halo_ref[...] = jnp.zeros_like(halo_ref)

        xv = x_ref[...]
        r = lax.rsqrt(jnp.mean(xv * xv, axis=-1, keepdims=True) + EPS)
        h = (xv * r * g_ref[...]).astype(BF16)
        h_ref[...] = h
        for c in range(0, f, FFN_BC):
            cs = slice(c, c + FFN_BC)
            gate = _dot(h, b_ref[c:c + FFN_BC, :], 1, 1).astype(BF16)
            up = _dot(h, b_ref[f + c:f + c + FFN_BC, :], 1, 1).astype(BF16)
            gu_ref[:, cs] = gate
            gu_ref[:, f + c:f + c + FFN_BC] = up
            gf = gate.astype(F32)
            ext = jnp.concatenate([halo_ref[:, cs], gf], axis=0)
            gc = (cb_ref[:, cs] + gf * w_ref[2:3, cs] + pltpu.roll(ext, 1, 0)[8:] * w_ref[1:2, cs]
                  + pltpu.roll(ext, 2, 0)[8:] * w_ref[0:1, cs])
            a_ref[:, cs] = (gc * _sigmoid(gc) * up.astype(F32)).astype(BF16)
            gc_ref[:, cs] = gc.astype(BF16)
            halo_ref[:, cs] = gf[tm - 8:]

    row = pl.BlockSpec((tm, d), lambda i: (i, 0))
    row_f = pl.BlockSpec((tm, f), lambda i: (i, 0))
    return pl.pallas_call(
        body, name=name, grid=(t // tm,),
        in_specs=[row, pl.BlockSpec((1, d), lambda i: (0, 0)), pl.BlockSpec((2 * f, d), lambda i: (0, 0)),
                  pl.BlockSpec((8, f), lambda i: (0, 0)), pl.BlockSpec((1, f), lambda i: (0, 0))],
        out_specs=[pl.BlockSpec((tm, 2 * f), lambda i: (i, 0)), row, row_f, row_f],
        out_shape=[jax.ShapeDtypeStruct((t, 2 * f), BF16), jax.ShapeDtypeStruct((t, d), BF16), jax.ShapeDtypeStruct((t, f), BF16),
                   jax.ShapeDtypeStruct((t, f), BF16)],
        scratch_shapes=[pltpu.VMEM((8, f), F32)], compiler_params=_cp(("arbitrary",)),
    )(x, g, wt, w8, b)


def _ffn_mid_bwd(gu, gc, dxb, w_out, w8, name, comm=None):
    t, d = dxb.shape
    f = D_FF
    tm, hr = FFN_BT, FFN_HALO
    nt = t // tm
    n = tm + hr

    def body(g_ref, u_ref, un_ref, c_ref, cn_ref, dx_ref, dxn_ref, wo_ref, w_ref, dgu_ref, dw_ref, db_ref):
        i = pl.program_id(0)
        last = i == nt - 1

        @pl.when(i == 0)
        def _():
            dw_ref[...] = jnp.zeros_like(dw_ref)
            db_ref[...] = jnp.zeros_like(db_ref)

        dxe = jnp.concatenate([dx_ref[...], dxn_ref[...]], axis=0)
        row = lax.broadcasted_iota(jnp.int32, (n, 1), 0)
        keep = (row < tm) | jnp.logical_not(last)
        for c in range(0, f, FFN_BC):
            cs = slice(c, c + FFN_BC)
            act, dact = _silu_grad(jnp.concatenate([c_ref[:, cs], cn_ref[:, cs]], axis=0).astype(F32))
            da = _dot(dxe, wo_ref[cs, :], 1, 1)
            up = jnp.concatenate([u_ref[:, cs], un_ref[:, cs]], axis=0).astype(F32)
            dgc = jnp.where(keep, da * up * dact, 0.0)
            nxt = [dgc[:tm], pltpu.roll(dgc, n - 1, 0)[:tm], pltpu.roll(dgc, n - 2, 0)[:tm]]
            dgu_ref[:, f + c:f + c + FFN_BC] = (da[:tm] * act[:tm]).astype(BF16)
            dgu_ref[:, cs] = (nxt[0] * w_ref[2:3, cs] + nxt[1] * w_ref[1:2, cs] + nxt[2] * w_ref[0:1, cs]).astype(BF16)
            gate = g_ref[:, cs].astype(F32)
            db_ref[:, cs] += jnp.sum(nxt[0], axis=0, keepdims=True)
            for s in range(3):
                dw_ref[2 - s:3 - s, cs] += jnp.sum(nxt[s] * gate, axis=0, keepdims=True)

    r = tm // hr
    nxt_blk = lambda i: jnp.minimum((i + 1) * r, t // hr - 1)
    row_f = pl.BlockSpec((tm, f), lambda i: (i, 0))
    halo_f = pl.BlockSpec((hr, f), lambda i: (nxt_blk(i), 0))
    return _call(
        body, name=name, grid=(nt,),
        in_specs=[row_f, pl.BlockSpec((tm, f), lambda i: (i, 1)), pl.BlockSpec((hr, f), lambda i: (nxt_blk(i), 1)),
                  row_f, halo_f,
                  pl.BlockSpec((tm, d), lambda i: (i, 0)), pl.BlockSpec((hr, d), lambda i: (nxt_blk(i), 0)),
                  pl.BlockSpec((f, d), lambda i: (0, 0)), pl.BlockSpec((8, f), lambda i: (0, 0))],
        out_specs=[pl.BlockSpec((tm, 2 * f), lambda i: (i, 0)), pl.BlockSpec((8, f), lambda i: (0, 0)),
                   pl.BlockSpec((1, f), lambda i: (0, 0))],
        out_shape=[jax.ShapeDtypeStruct((t, 2 * f), BF16), jax.ShapeDtypeStruct((8, f), F32), jax.ShapeDtypeStruct((1, f), F32)],
        args=(gu, gu, gu, gc, gc, dxb, dxb, w_out, w8), sem=("arbitrary",), comm=comm)


PRE_TM = 256
PRE_TC = 1024


def _softplus_heads(v):
    sp = jnp.maximum(v, 0.0) + jnp.log(1.0 + jnp.exp(-jnp.abs(v)))
    return jnp.where(lax.broadcasted_iota(jnp.int32, (1, LANES), 1) < SSM_HEADS, sp, 0.0)


def _ssm_in_pre(x, g, wt, wt_dt, w8, b, dt_bias, name):
    t, d = x.shape
    tm, tc = PRE_TM, PRE_TC

    def body(x_ref, g_ref, b_ref, bdt_ref, w_ref, cb_ref, db_ref, zx_ref, h_ref, o_ref, c_ref, dtr_ref, dt_ref, halo_ref):
        @pl.when(pl.program_id(0) == 0)
        def _():
            halo_ref[...] = jnp.zeros_like(halo_ref)

        xv = x_ref[...]
        r = lax.rsqrt(jnp.mean(xv * xv, axis=-1, keepdims=True) + EPS)
        h = (xv * r * g_ref[...]).astype(BF16)
        h_ref[...] = h
        dtr = _dot(h, bdt_ref[...], 1, 1)
        dtr_ref[...] = dtr
        dt_ref[...] = _softplus_heads(dtr + db_ref[...])
        for c in range(0, ZX, tc):
            v = _dot(h, b_ref[c:c + tc, :], 1, 1)
            zx_ref[:, c:c + tc] = v
            if c >= D_INNER:
                cs = slice(c - D_INNER, c - D_INNER + tc)
                ext = jnp.concatenate([halo_ref[:, cs], v], axis=0)
                conv = cb_ref[:, cs] + v * w_ref[3:4, cs]
                for s in (1, 2, 3):
                    conv = conv + pltpu.roll(ext, s, 0)[8:] * w_ref[3 - s:4 - s, cs]
                o_ref[:, cs] = conv * _sigmoid(conv)
                c_ref[:, cs] = conv.astype(BF16)
                halo_ref[:, cs] = v[tm - 8:]

    row = pl.BlockSpec((tm, d), lambda i: (i, 0))
    row_x = pl.BlockSpec((tm, XBC), lambda i: (i, 0))
    row_h = pl.BlockSpec((tm, LANES), lambda i: (i, 0))
    return pl.pallas_call(
        body, name=name, grid=(t // tm,),
        in_specs=[row, pl.BlockSpec((1, d), lambda i: (0, 0)), pl.BlockSpec(wt.shape, lambda i: (0, 0)),
                  pl.BlockSpec(wt_dt.shape, lambda i: (0, 0)),
                  pl.BlockSpec((8, XBC), lambda i: (0, 0)), pl.BlockSpec((1, XBC), lambda i: (0, 0)),
                  pl.BlockSpec((1, LANES), lambda i: (0, 0))],
        out_specs=[pl.BlockSpec((tm, ZX), lambda i: (i, 0)), row, row_x, row_x, row_h, row_h],
        out_shape=[jax.ShapeDtypeStruct((t, ZX), F32), jax.ShapeDtypeStruct((t, d), BF16), jax.ShapeDtypeStruct((t, XBC), F32),
                   jax.ShapeDtypeStruct((t, XBC), BF16), jax.ShapeDtypeStruct((t, LANES), F32),
                   jax.ShapeDtypeStruct((t, LANES), F32)],
        scratch_shapes=[pltpu.VMEM((8, XBC), F32)], compiler_params=_cp(("arbitrary",)),
    )(x, g, wt, wt_dt, w8, b, dt_bias)


PRE_HALO = 16


def _ssm_pre_bwd(zx, conv, dxbc, w8, name):
    t = zx.shape[0]
    tm, tc, hr = PRE_TM, PRE_TC, PRE_HALO
    off = D_INNER // tc
    nt = t // tm
    n = tm + hr

    def body(x_ref, c_ref, cn_ref, d_ref, dn_ref, w_ref, o_ref, dw_ref, db_ref):
        i = pl.program_id(1)
        last = i == nt - 1

        @pl.when(i == 0)
        def _():
            dw_ref[...] = jnp.zeros_like(dw_ref)
            db_ref[...] = jnp.zeros_like(db_ref)

        _, dact = _silu_grad(jnp.concatenate([c_ref[...], cn_ref[...]], axis=0).astype(F32))
        row = lax.broadcasted_iota(jnp.int32, (n, 1), 0)
        dc = jnp.where((row < tm) | jnp.logical_not(last), jnp.concatenate([d_ref[...], dn_ref[...]], axis=0) * dact, 0.0)
        nxt = [dc[:tm]] + [pltpu.roll(dc, n - s, 0)[:tm] for s in (1, 2, 3)]
        o_ref[...] = _conv_apply(nxt, w_ref, 4).astype(BF16)
        xv = x_ref[...]
        db_ref[...] += jnp.sum(nxt[0], axis=0, keepdims=True)
        for s in range(4):
            dw_ref[3 - s:4 - s, :] += jnp.sum(nxt[s] * xv, axis=0, keepdims=True)

    nxt_blk = lambda i: jnp.minimum((i + 1) * (tm // hr), t // hr - 1)
    tile = pl.BlockSpec((tm, tc), lambda j, i: (i, j))
    halo = pl.BlockSpec((hr, tc), lambda j, i: (nxt_blk(i), j))
    return pl.pallas_call(
        body, name=name, grid=(XBC // tc, nt),
        in_specs=[pl.BlockSpec((tm, tc), lambda j, i: (i, j + off)), tile, halo, tile, halo,
                  pl.BlockSpec((8, tc), lambda j, i: (0, j))],
        out_specs=[tile, pl.BlockSpec((8, tc), lambda j, i: (0, j)), pl.BlockSpec((1, tc), lambda j, i: (0, j))],
        out_shape=[jax.ShapeDtypeStruct((t, XBC), BF16), jax.ShapeDtypeStruct((8, XBC), F32),
                   jax.ShapeDtypeStruct((1, XBC), F32)],
        compiler_params=_cp(("parallel", "arbitrary")),
    )(zx, conv, conv, dxbc, dxbc, w8)


def _head_lanes():
    return lax.broadcasted_iota(jnp.int32, (1, LANES), 1) < SSM_HEADS


def _dt_bwd(dtraw, bias, ddt, name):
    t = dtraw.shape[0]
    tm = _pick(t, (1024, 512, 256, 128))

    def body(x_ref, b_ref, d_ref, o_ref, db_ref):
        @pl.when(pl.program_id(0) == 0)
        def _():
            db_ref[...] = jnp.zeros_like(db_ref)

        g = jnp.where(_head_lanes(), d_ref[...] * _sigmoid(x_ref[...] + b_ref[...]), 0.0)
        o_ref[...] = g.astype(BF16)
        db_ref[...] += jnp.sum(g, axis=0, keepdims=True)

    row = pl.BlockSpec((tm, LANES), lambda i: (i, 0))
    vec = pl.BlockSpec((1, LANES), lambda i: (0, 0))
    return pl.pallas_call(
        body, name=name, grid=(t // tm,), in_specs=[row, vec, row], out_specs=[row, vec],
        out_shape=[jax.ShapeDtypeStruct((t, LANES), BF16), jax.ShapeDtypeStruct((1, LANES), F32)],
        compiler_params=_cp(("arbitrary",)),
    )(dtraw, bias, ddt)


GROUP_W = D_INNER // SSM_GROUPS


def _ssd_common(dt, alog):
    ll = dt.shape[0]
    a_neg = -jnp.exp(alog)
    a = dt * a_neg
    ri = lax.broadcasted_iota(jnp.int32, (ll, ll), 0)
    ci = lax.broadcasted_iota(jnp.int32, (ll, ll), 1)
    tril = ri >= ci
    acs = _dot(tril.astype(F32), a, 1, 0, HI)
    return a_neg, tril, acs, acs.T


def _pair_terms(acs, acs_t, dt, h0, lo):
    ll = acs.shape[0]
    cols = [acs[:, h0 + e:h0 + e + 1] for e in range(2)]
    rows = [acs_t[h0 + e:h0 + e + 1, :] for e in range(2)]
    dtc = [dt[:, h0 + e:h0 + e + 1] for e in range(2)]
    lasts = [c[ll - 1:ll, :] for c in cols]
    dtx = jnp.where(lo, dtc[0], dtc[1])
    eac = jnp.where(lo, jnp.exp(cols[0]), jnp.exp(cols[1]))
    fdec = jnp.where(lo, jnp.exp(lasts[0] - cols[0]), jnp.exp(lasts[1] - cols[1]))
    elast = jnp.where(lo, jnp.exp(lasts[0]), jnp.exp(lasts[1]))
    return cols, rows, dtx, eac, fdec, elast


def _decay(col, row, tril):
    return jnp.where(tril, jnp.exp(jnp.minimum(col - row, 0.0)), 0.0)


def _two_heads_rows(v, lo):
    z = jnp.zeros_like(v)
    return jnp.concatenate([jnp.where(lo, v, z), jnp.where(lo, z, v)], axis=0)


def _two_heads_cols(ms):
    return jnp.concatenate(ms, axis=1)


def _z_group(z_refs, g):
    return z_refs[g // 2][:, GROUP_W * (g % 2):GROUP_W * (g % 2 + 1)]


def _ssd_fwd(xbc, dt, alog, zx, dexp, nw, name, comm=None):
    t = xbc.shape[0]
    ll = SSD_L
    nc = t // ll

    def body(x_ref, dt_ref, al_ref, z0_ref, z1_ref, d_ref, w_ref, y_ref, sp_ref, y4_ref, st_ref):
        @pl.when(pl.program_id(0) == 0)
        def _():
            st_ref[...] = jnp.zeros_like(st_ref)

        dtv = dt_ref[...]
        _, tril, acs, acs_t = _ssd_common(dtv, al_ref[...])
        lo = _lo_mask()
        sp_ref[0] = st_ref[...]
        for g in range(SSM_GROUPS):
            bg = x_ref[:, D_INNER + SSM_STATE * g:D_INNER + SSM_STATE * (g + 1)].astype(BF16)
            cg = x_ref[:, D_INNER + 512 + SSM_STATE * g:D_INNER + 512 + SSM_STATE * (g + 1)].astype(BF16)
            gm = _dot(cg, bg, 1, 1)
            g0 = GROUP_W * g
            terms = [_pair_terms(acs, acs_t, dtv, 8 * g + 2 * pp, lo) for pp in range(4)]
            dtx, eac, fdec, elast = [jnp.concatenate([tt[k] for tt in terms], axis=1) for k in (2, 3, 4, 5)]
            xg = x_ref[:, g0:g0 + GROUP_W]
            ug = (xg * dtx).astype(BF16)
            sg = st_ref[:, g0:g0 + GROUP_W]
            yst = _dot(cg, sg.astype(BF16), 1, 0) * eac
            st_ref[:, g0:g0 + GROUP_W] = sg * elast + _dot(bg, (xg * (fdec * dtx)).astype(BF16), 0, 0)
            ys = []
            for pp in range(4):
                cols, rows = terms[pp][0], terms[pp][1]
                sl = slice(LANES * pp, LANES * (pp + 1))
                y_in = _dot(_two_heads_cols([(gm * _decay(cols[e], rows[e], tril)).astype(BF16) for e in range(2)]),
                            _two_heads_rows(ug[:, sl], lo), 1, 0)
                ys.append(y_in + yst[:, sl])
            yg = jnp.concatenate(ys, axis=1)
            y_ref[:, g0:g0 + GROUP_W] = yg
            zg = _z_group((z0_ref, z1_ref), g)
            y3 = (yg + d_ref[:, g0:g0 + GROUP_W] * xg) * (zg * _sigmoid(zg))
            r = lax.rsqrt(jnp.mean(y3 * y3, axis=-1, keepdims=True) + EPS)
            y4_ref[:, g0:g0 + GROUP_W] = (y3 * r * w_ref[:, g0:g0 + GROUP_W]).astype(BF16)

    zblk = lambda j: pl.BlockSpec((ll, 1024), lambda c: (c, j))
    vec = pl.BlockSpec((1, D_INNER), lambda c: (0, 0))
    row = pl.BlockSpec((ll, D_INNER), lambda c: (c, 0))
    return _call(
        body, name=name, grid=(nc,),
        in_specs=[pl.BlockSpec((ll, XBC), lambda c: (c, 0)), pl.BlockSpec((ll, LANES), lambda c: (c, 0)),
                  pl.BlockSpec((1, LANES), lambda c: (0, 0)), zblk(0), zblk(1), vec, vec],
        out_specs=[row, pl.BlockSpec((1, SSM_STATE, D_INNER), lambda c: (c, 0, 0)), row],
        out_shape=[jax.ShapeDtypeStruct((t, D_INNER), F32), jax.ShapeDtypeStruct((nc, SSM_STATE, D_INNER), F32),
                   jax.ShapeDtypeStruct((t, D_INNER), BF16)],
        scratch_shapes=[pltpu.VMEM((SSM_STATE, D_INNER), F32)],
        args=(xbc, dt, alog, zx, zx, dexp, nw), sem=("arbitrary",), comm=comm)


def _ssd_bwd(xbc, dt, alog, sprev, dy4, y, zx, dexp, nw, name, comm=None):
    t = xbc.shape[0]
    ll = SSD_L_BWD if t % SSD_L_BWD == 0 else SSD_L
    nc = t // ll
    every = ll // SSD_L

    def body(x_ref, dt_ref, al_ref, sp_ref, g4_ref, y_ref, z0_ref, z1_ref, d_ref, w_ref,
             dx_ref, ddt_ref, dal_ref, dz_ref, dd_ref, dnw_ref, ds_ref, colt_ref):
        @pl.when(pl.program_id(0) == 0)
        def _():
            ds_ref[...] = jnp.zeros_like(ds_ref)
            dal_ref[...] = jnp.zeros_like(dal_ref)
            dd_ref[...] = jnp.zeros_like(dd_ref)
            dnw_ref[...] = jnp.zeros_like(dnw_ref)

        dtv = dt_ref[...]
        a_neg, tril, acs, acs_t = _ssd_common(dtv, al_ref[...])
        lo = _lo_mask()
        hi = jnp.logical_not(lo)
        lane = lax.broadcasted_iota(jnp.int32, (1, LANES), 1)
        colt_ref[...] = jnp.zeros_like(colt_ref)
        rowterm = jnp.zeros((ll, LANES), F32)
        ddt_u = jnp.zeros((ll, LANES), F32)
        dlast = jnp.zeros((1, LANES), F32)

        def halves(v):
            return (jnp.sum(jnp.where(lo, v, 0.0), axis=-1, keepdims=True),
                    jnp.sum(jnp.where(hi, v, 0.0), axis=-1, keepdims=True))

        for g in range(SSM_GROUPS):
            cb0 = D_INNER + SSM_STATE * g
            cc0 = D_INNER + 512 + SSM_STATE * g
            bg = x_ref[:, cb0:cb0 + SSM_STATE].astype(BF16)
            cg = x_ref[:, cc0:cc0 + SSM_STATE].astype(BF16)
            gm = _dot(cg, bg, 1, 1)
            g0 = GROUP_W * g
            terms = [_pair_terms(acs, acs_t, dtv, 8 * g + 2 * pp, lo) for pp in range(4)]
            dtx, eac, fdec, elast = [jnp.concatenate([tt[k] for tt in terms], axis=1) for k in (2, 3, 4, 5)]
            xg = x_ref[:, g0:g0 + GROUP_W]
            u32 = xg * dtx
            ug = u32.astype(BF16)
            zg = _z_group((z0_ref, z1_ref), g)
            dg = d_ref[:, g0:g0 + GROUP_W]
            act, dact = _silu_grad(zg)
            y2 = y_ref[:, g0:g0 + GROUP_W] + dg * xg
            y3 = y2 * act
            rn = lax.rsqrt(jnp.mean(y3 * y3, axis=-1, keepdims=True) + EPS)
            y3n = y3 * rn
            gv = g4_ref[:, g0:g0 + GROUP_W]
            dyn = gv * w_ref[:, g0:g0 + GROUP_W]
            dy3 = rn * (dyn - y3n * jnp.mean(dyn * y3n, axis=-1, keepdims=True))
            dyg = dy3 * act
            dskip = dyg * dg
            dz_ref[:, g0:g0 + GROUP_W] = (dy3 * y2 * dact).astype(BF16)
            dd_ref[:, g0:g0 + GROUP_W] += jnp.sum(dyg * xg, axis=0, keepdims=True)
            dnw_ref[:, g0:g0 + GROUP_W] += jnp.sum(gv * y3n, axis=0, keepdims=True)
            dyb = dyg.astype(BF16)
            spg = sp_ref[0, :, g0:g0 + GROUP_W]
            spb = spg.astype(BF16)
            dsg = ds_ref[:, g0:g0 + GROUP_W]
            dsb = dsg.astype(BF16)
            du_st = _dot(bg, dsb, 1, 0) * fdec
            yst = _dot(cg, spb, 1, 0) * eac
            dye = (dyg * eac).astype(BF16)
            dc_st = _dot(dye, spb, 1, 1)
            db_st = _dot((xg * (fdec * dtx)).astype(BF16), dsb, 1, 1)
            ds_ref[:, g0:g0 + GROUP_W] = dsg * elast + _dot(cg, dye, 0, 0)
            qst_el = du_st * u32
            rq_el = dyg * yst - qst_el
            q_row = jnp.sum(qst_el, axis=0, keepdims=True)
            s_row = jnp.sum(dsg * spg, axis=0, keepdims=True)
            dgm = jnp.zeros((ll, ll), F32)
            for pp in range(4):
                h0 = 8 * g + 2 * pp
                cols, rows = terms[pp][0], terms[pp][1]
                sl = slice(LANES * pp, LANES * (pp + 1))
                decs = [_decay(cols[e], rows[e], tril) for e in range(2)]
                wms = [gm * d for d in decs]
                dum2 = _dot(dyb[:, sl], _two_heads_rows(ug[:, sl], lo), 1, 1)
                du = _dot(jnp.concatenate([wm.astype(BF16) for wm in wms], axis=0),
                          _two_heads_rows(dyb[:, sl], lo), 0, 0) + du_st[:, sl]
                dx_ref[:, g0 + LANES * pp:g0 + LANES * (pp + 1)] = du * dtx[:, sl] + dskip[:, sl]
                ddtu = halves(du * xg[:, sl])
                rq = halves(rq_el[:, sl])
                qs = halves(q_row[:, sl])
                ss = halves(s_row[:, sl])
                for e in range(2):
                    dum = dum2[:, ll * e:ll * (e + 1)]
                    dgm = dgm + dum * decs[e]
                    tm_ = dum * wms[e]
                    oh = lane == (h0 + e)
                    rowterm = rowterm + jnp.where(oh, jnp.sum(tm_, axis=1, keepdims=True) + rq[e], 0.0)
                    ddt_u = ddt_u + jnp.where(oh, ddtu[e], 0.0)
                    dlast = dlast + jnp.where(oh, jnp.exp(cols[e][ll - 1:ll, :]) * ss[e] + qs[e], 0.0)
                    colt_ref[h0 + e:h0 + e + 1, :] = jnp.sum(tm_, axis=0, keepdims=True)
            dgb = dgm.astype(BF16)
            dx_ref[:, cc0:cc0 + SSM_STATE] = _dot(dgb, bg, 1, 0) + dc_st
            dx_ref[:, cb0:cb0 + SSM_STATE] = _dot(dgb, cg, 0, 0) + db_st
        row_io = lax.broadcasted_iota(jnp.int32, (ll, LANES), 0)
        dacs = rowterm - colt_ref[...].T + jnp.where(row_io == ll - 1, dlast, 0.0)
        da = _dot(jnp.logical_not(tril).astype(F32) + jnp.where(
            lax.broadcasted_iota(jnp.int32, (ll, ll), 0) == lax.broadcasted_iota(jnp.int32, (ll, ll), 1), 1.0, 0.0),
            dacs, 1, 0, HI)
        ddt_ref[...] = da * a_neg + ddt_u
        dal_ref[...] += jnp.sum(da * dtv, axis=0, keepdims=True) * a_neg

    rev = lambda c: nc - 1 - c
    row = pl.BlockSpec((ll, D_INNER), lambda c: (rev(c), 0))
    vec = pl.BlockSpec((1, D_INNER), lambda c: (0, 0))
    zblk = lambda j: pl.BlockSpec((ll, 1024), lambda c: (rev(c), j))
    return _call(
        body, name=name, grid=(nc,),
        in_specs=[pl.BlockSpec((ll, XBC), lambda c: (rev(c), 0)), pl.BlockSpec((ll, LANES), lambda c: (rev(c), 0)),
                  pl.BlockSpec((1, LANES), lambda c: (0, 0)),
                  pl.BlockSpec((1, SSM_STATE, D_INNER), lambda c: (rev(c) * every, 0, 0)), row, row, zblk(0), zblk(1), vec, vec],
        out_specs=[pl.BlockSpec((ll, XBC), lambda c: (rev(c), 0)), pl.BlockSpec((ll, LANES), lambda c: (rev(c), 0)),
                   pl.BlockSpec((1, LANES), lambda c: (0, 0)), row, vec, vec],
        out_shape=[jax.ShapeDtypeStruct((t, XBC), F32), jax.ShapeDtypeStruct((t, LANES), F32),
                   jax.ShapeDtypeStruct((1, LANES), F32), jax.ShapeDtypeStruct((t, D_INNER), BF16),
                   jax.ShapeDtypeStruct((1, D_INNER), F32), jax.ShapeDtypeStruct((1, D_INNER), F32)],
        scratch_shapes=[pltpu.VMEM((SSM_STATE, D_INNER), F32), pltpu.VMEM((LANES, ll), F32)],
        args=(xbc, dt, alog, sprev, dy4, y, zx, zx, dexp, nw), sem=("arbitrary",), comm=comm)


def _sum_parts(parts, name):
    nparts, r, c = parts.shape
    tc = _pick(c, (256, 128))

    def body(p_ref, o_ref):
        g = p_ref[0].astype(F32)
        for k in range(1, nparts):
            g = g + p_ref[k].astype(F32)
        o_ref[...] = g

    return pl.pallas_call(
        body, name=name, grid=(c // tc,), in_specs=[pl.BlockSpec((nparts, r, tc), lambda j: (0, 0, j))],
        out_specs=pl.BlockSpec((r, tc), lambda j: (0, j)), out_shape=jax.ShapeDtypeStruct((r, c), F32),
        compiler_params=_cp(("parallel",)),
    )(parts)


def _adamw(parts, w, m, v, name):
    nl, r, c = w.shape
    assert len(parts) == nl
    tr = _pick(r, (256, 128, 64))
    c1 = 1.0 - ADAM_B1 ** ADAM_STEP
    c2 = 1.0 - ADAM_B2 ** ADAM_STEP

    def body(*refs):
        p_refs = refs[:nl]
        w_ref, m_ref, v_ref, g_ref, d_ref, mo_ref, vo_ref = refs[nl:]
        g = None
        for l, p_ref in enumerate(p_refs):
            s = p_ref[0].astype(F32)
            for k in range(1, p_ref.shape[0]):
                s = s + p_ref[k].astype(F32)
            g = s if g is None else jnp.where(pl.program_id(0) == l, s, g)
        mn = ADAM_B1 * m_ref[0] + (1.0 - ADAM_B1) * g
        vn = ADAM_B2 * v_ref[0] + (1.0 - ADAM_B2) * (g * g)
        g_ref[0] = g
        mo_ref[0] = mn
        vo_ref[0] = vn
        d_ref[0] = -ADAM_LR * ((mn / c1) / (jnp.sqrt(vn / c2) + ADAM_EPS) + ADAM_WD * w_ref[0])

    row = pl.BlockSpec((1, tr, c), lambda l, i: (l, i, 0))
    sd = jax.ShapeDtypeStruct((nl, r, c), F32)
    return pl.pallas_call(
        body, name=name, grid=(nl, r // tr),
        in_specs=[pl.BlockSpec((p.shape[0], tr, c), lambda l, i: (0, i, 0)) for p in parts] + [row, row, row],
        out_specs=[row, row, row, row], out_shape=[sd, sd, sd, sd], compiler_params=_cp(("parallel", "parallel")),
    )(*parts, w, m, v)


def _peers():
    mx, my, mc = lax.axis_index("x"), lax.axis_index("y"), lax.axis_index("c")
    me = 4 * mx + 2 * my + mc
    out = []
    for k in range(1, N_DEV):
        px = 1 - mx if k & 4 else mx
        py = 1 - my if k & 2 else my
        pc = 1 - mc if k & 1 else mc
        out.append(((px, py, pc), 4 * px + 2 * py + pc))
    return me, out


class _Comm:
    def __init__(self, arrs, scatters):
        self.arrs, self.scatters, self.n = list(arrs), list(scatters), len(arrs)
        self.specs = [pl.BlockSpec(memory_space=pl.ANY)] * self.n
        self.out_shape = [jax.ShapeDtypeStruct(x.shape if sc else (N_DEV,) + x.shape, x.dtype)
                          for x, sc in zip(self.arrs, self.scatters)]
        np_ = N_DEV - 1
        self.scratch = [pltpu.SemaphoreType.DMA((np_ * self.n,)), pltpu.SemaphoreType.DMA((np_ * self.n,)),
                        pltpu.SemaphoreType.DMA((self.n,))]

    def _copies(self, x_refs, o_refs, sems):
        send_sems, recv_sems, local_sems = sems
        me, peers = _peers()
        np_ = N_DEV - 1
        local, sends, recvs = [], [], []
        for a in range(self.n):
            mine = x_refs[a].at[me] if self.scatters[a] else x_refs[a]
            local.append(pltpu.make_async_copy(mine, o_refs[a].at[me], local_sems.at[a]))
        for k, (dev, idx) in enumerate(peers):
            for a in range(self.n):
                mine = x_refs[a].at[me] if self.scatters[a] else x_refs[a]
                sends.append(pltpu.make_async_remote_copy(
                    src_ref=x_refs[a].at[idx] if self.scatters[a] else x_refs[a], dst_ref=o_refs[a].at[me],
                    send_sem=send_sems.at[a * np_ + k], recv_sem=recv_sems.at[a * np_ + k], device_id=dev, device_id_type=MESH))
                recvs.append(pltpu.make_async_remote_copy(
                    src_ref=mine, dst_ref=o_refs[a].at[idx], send_sem=send_sems.at[a * np_ + k],
                    recv_sem=recv_sems.at[a * np_ + k], device_id=dev, device_id_type=MESH))
        return local, sends, recvs

    def start(self, x_refs, o_refs, sems):
        local, sends, _ = self._copies(x_refs, o_refs, sems)
        for cp in local + sends:
            cp.start()

    def wait(self, x_refs, o_refs, sems):
        local, sends, recvs = self._copies(x_refs, o_refs, sems)
        for cp in recvs:
            cp.wait_recv()
        for cp in sends:
            cp.wait_send()
        for cp in local:
            cp.wait()


class _Gather2(_Comm):
    def __init__(self, arrs):
        super().__init__(arrs, [False] * len(arrs))

    def _plan(self, x_refs, o_refs, sems):
        send_sems, recv_sems, local_sems = sems
        mx, my, mc = lax.axis_index("x"), lax.axis_index("y"), lax.axis_index("c")
        slot = lambda px, py, pc: 4 * px + 2 * py + pc
        sib = (mx, my, 1 - mc)
        chips = [(1 - mx, my), (mx, 1 - my), (1 - mx, 1 - my)]
        np_ = N_DEV - 1
        local, first, passed, arrive_first, arrive_rest = [], [], [], [], []

        def copy(a, k, src, block, to):
            return pltpu.make_async_remote_copy(
                src_ref=src, dst_ref=o_refs[a].at[block], send_sem=send_sems.at[a * np_ + k], recv_sem=recv_sems.at[a * np_ + k],
                device_id=to, device_id_type=MESH)

        for a in range(self.n):
            me = slot(mx, my, mc)
            local.append(pltpu.make_async_copy(x_refs[a], o_refs[a].at[me], local_sems.at[a]))
            first.append(copy(a, 0, x_refs[a], me, sib))
            arrive_rest.append(copy(a, 0, x_refs[a], slot(*sib), sib))
            for j, (cx, cy) in enumerate(chips):
                first.append(copy(a, 1 + j, x_refs[a], me, (cx, cy, mc)))
                arrive_first.append(copy(a, 1 + j, x_refs[a], slot(cx, cy, mc), (cx, cy, mc)))
                passed.append(copy(a, 4 + j, o_refs[a].at[slot(cx, cy, mc)], slot(cx, cy, mc), sib))
                arrive_rest.append(copy(a, 4 + j, x_refs[a], slot(cx, cy, 1 - mc), sib))
        return local, first, passed, arrive_first, arrive_rest

    def start(self, x_refs, o_refs, sems):
        local, first, _, _, _ = self._plan(x_refs, o_refs, sems)
        for cp in local + first:
            cp.start()

    def wait(self, x_refs, o_refs, sems):
        local, first, passed, arrive_first, arrive_rest = self._plan(x_refs, o_refs, sems)
        for arrived, onward in zip(arrive_first, passed):
            arrived.wait_recv()
            onward.start()
        for cp in arrive_rest:
            cp.wait_recv()
        for cp in first + passed:
            cp.wait_send()
        for cp in local:
            cp.wait()


def _call(body, *, name, grid, in_specs, out_specs, out_shape, args, scratch_shapes=(), sem=None, comm=None):
    if comm is None:
        outs = pl.pallas_call(
            body, name=name, grid=grid, in_specs=list(in_specs), out_specs=list(out_specs), out_shape=list(out_shape),
            scratch_shapes=list(scratch_shapes), compiler_params=_cp(sem),
        )(*args)
        return list(outs), []
    n_in, n_out, nc = len(in_specs), len(out_specs), comm.n
    nsteps = 1
    for g in grid:
        nsteps *= g

    def carrier(*refs):
        ins, cin = refs[:n_in], refs[n_in:n_in + nc]
        outs, cout = refs[n_in + nc:n_in + nc + n_out], refs[n_in + nc + n_out:n_in + 2 * nc + n_out]
        rest = refs[n_in + 2 * nc + n_out:]
        scratch, sems = rest[:len(rest) - 3], rest[len(rest) - 3:]
        if nsteps == 1:
            comm.start(cin, cout, sems)
            body(*ins, *outs, *scratch)
            comm.wait(cin, cout, sems)
            return
        step = 0
        for d, g in enumerate(grid):
            step = step * g + pl.program_id(d)

        @pl.when(step == 0)
        def _():
            comm.start(cin, cout, sems)

        body(*ins, *outs, *scratch)

        @pl.when(step == nsteps - 1)
        def _():
            comm.wait(cin, cout, sems)

    outs = pl.pallas_call(
        carrier, name=name, grid=grid, in_specs=list(in_specs) + comm.specs, out_specs=list(out_specs) + comm.specs,
        out_shape=list(out_shape) + comm.out_shape, scratch_shapes=list(scratch_shapes) + comm.scratch,
        compiler_params=_cp(("arbitrary",) * len(grid) if grid else None),
    )(*args, *comm.arrs)
    return list(outs[:n_out]), list(outs[n_out:])


def _exchange(comm, name):
    return _call(lambda *refs: None, name=name, grid=(), in_specs=[], out_specs=[], out_shape=[], args=[], comm=comm)[1]


def _pack(arrs, dtype, lead=()):
    nl = len(lead)
    flat = jnp.concatenate([a.astype(dtype).reshape(lead + (-1,)) for a in arrs], axis=nl)
    n = flat.shape[-1]
    rows = -(-n // (LANES * 8)) * 8
    flat = jnp.pad(flat, [(0, 0)] * nl + [(0, rows * LANES - n)])
    return flat.reshape(lead + (rows, LANES))


def _unpack(flat, shapes, lead=()):
    nl = len(lead)
    flat = flat.reshape(lead + (-1,))
    out, o = [], 0
    for s in shapes:
        n = 1
        for d in s:
            n *= d
        out.append(lax.slice_in_dim(flat, o, o + n, axis=nl).reshape(lead + tuple(s)))
        o += n
    return out


def _join(g, ax):
    return jnp.concatenate([g[d] for d in range(N_DEV)], axis=ax)


def _split(full, ax):
    n = full.shape[ax] // N_DEV
    return jnp.stack([lax.slice_in_dim(full, d * n, (d + 1) * n, axis=ax) for d in range(N_DEV)])


_WEIGHTS = ['norm_mix', 'norm_ffn', 'attn_w_in', 'attn_w_out', 'relpos_table', 'q_norm_a', 'k_norm_a', 'q_norm_b',
            'k_norm_b', 'sinks', 'ssm_w_in', 'ssm_conv_w', 'ssm_conv_b', 'ssm_dt_bias', 'ssm_a_log', 'ssm_d', 'ssm_norm',
            'ssm_w_out', 'ffn_w_in', 'ffn_conv_w', 'ffn_conv_b', 'ffn_w_out']
_SHARD_AX = {'attn_w_in': 2, 'attn_w_out': 1, 'ssm_w_in': 2, 'ssm_conv_w': 2, 'ssm_conv_b': 1, 'ssm_norm': 1,
             'ssm_w_out': 1, 'ffn_w_in': 2, 'ffn_conv_w': 2, 'ffn_w_out': 1}
_BIG = ['attn_w_in', 'attn_w_out', 'ssm_w_in', 'ssm_w_out', 'ffn_w_in', 'ffn_w_out']
_SMALL = ['ssm_conv_w', 'ssm_conv_b', 'ssm_norm', 'ffn_conv_w']
_AX2 = {n: _SHARD_AX[n] - 1 for n in _BIG}
_REPL = [n for n in _WEIGHTS if n not in _SHARD_AX]


def _rows8(w):
    return jnp.pad(w, ((0, 8 - w.shape[0]), (0, 0)))


def _lanes128(v):
    return jnp.pad(v, (0, LANES - v.shape[0])).reshape(1, LANES)


def _band_mask(n_prev, pad):
    cq = jnp.arange(TQ)[:, None] // CHUNK
    ck = jnp.arange(pad + TQ)[None, :] // CHUNK
    return (ck >= cq) & (ck <= cq + n_prev)


def _ffn_fwd(xin, g, w_in_t, w8, cb, tag):
    gu, h, a, gc = _ffn_in_mid(xin, g, w_in_t, w8, cb, f"mm_ffn_in{tag}")
    return a, (h, gu, a, gc)


def _ffn_bwd(dx, dxb, xin, g, w_in_t, w8, w_out, saved, tag, scatter_of=None):
    h, gu, a, gc = saved
    dw_out = _mm_tn(a, dxb, f"mm_ffn_dwout{tag}")
    (dgu, dw8, dcb), got_out = _ffn_mid_bwd(gu, gc, dxb, w_out, w8, f"ffn_mid_bwd{tag}",
                                            comm=scatter_of([dw_out]) if scatter_of else None)
    dw_in_t = _mm_tn(dgu, h, f"mm_ffn_dwin{tag}")
    if scatter_of is None:
        dxp, dxpb, dg = _mm_rms_bwd([(dgu, w_in_t, 0)], xin, g, dx, f"mm_ffn_dh{tag}")
        return dxp, dxpb, dg, dw_in_t, dw8[:3], dcb, dw_out
    dxp, dxpb, dg, got_in = _mm_rms_bwd([(dgu, w_in_t, 0)], xin, g, dx, f"mm_ffn_dh{tag}", comm=scatter_of([dw_in_t]))
    return dxp, dxpb, dg, got_in[0], dw8[:3], dcb, got_out[0]


def kernel(x, norm_mix, norm_ffn, attn_w_in, attn_w_out, relpos_table, q_norm_a, k_norm_a, q_norm_b, k_norm_b, sinks, ssm_w_in, ssm_conv_w, ssm_conv_b, ssm_dt_bias, ssm_a_log, ssm_d, ssm_norm, ssm_w_out, ffn_w_in, ffn_conv_w, ffn_conv_b, ffn_w_out, loss_target, m_norm_mix, m_norm_ffn, m_attn_w_in, m_attn_w_out, m_relpos_table, m_q_norm_a, m_k_norm_a, m_q_norm_b, m_k_norm_b, m_sinks, m_ssm_w_in, m_ssm_conv_w, m_ssm_conv_b, m_ssm_dt_bias, m_ssm_a_log, m_ssm_d, m_ssm_norm, m_ssm_w_out, m_ffn_w_in, m_ffn_conv_w, m_ffn_conv_b, m_ffn_w_out, v_norm_mix, v_norm_ffn, v_attn_w_in, v_attn_w_out, v_relpos_table, v_q_norm_a, v_k_norm_a, v_q_norm_b, v_k_norm_b, v_sinks, v_ssm_w_in, v_ssm_conv_w, v_ssm_conv_b, v_ssm_dt_bias, v_ssm_a_log, v_ssm_d, v_ssm_norm, v_ssm_w_out, v_ffn_w_in, v_ffn_conv_w, v_ffn_conv_b, v_ffn_w_out):
    w = dict(norm_mix=norm_mix, norm_ffn=norm_ffn, attn_w_in=attn_w_in, attn_w_out=attn_w_out, relpos_table=relpos_table,
             q_norm_a=q_norm_a, k_norm_a=k_norm_a, q_norm_b=q_norm_b, k_norm_b=k_norm_b, sinks=sinks, ssm_w_in=ssm_w_in,
             ssm_conv_w=ssm_conv_w, ssm_conv_b=ssm_conv_b, ssm_dt_bias=ssm_dt_bias, ssm_a_log=ssm_a_log, ssm_d=ssm_d,
             ssm_norm=ssm_norm, ssm_w_out=ssm_w_out, ffn_w_in=ffn_w_in, ffn_conv_w=ffn_conv_w, ffn_conv_b=ffn_conv_b,
             ffn_w_out=ffn_w_out)
    mom = dict(norm_mix=m_norm_mix, norm_ffn=m_norm_ffn, attn_w_in=m_attn_w_in, attn_w_out=m_attn_w_out,
               relpos_table=m_relpos_table, q_norm_a=m_q_norm_a, k_norm_a=m_k_norm_a, q_norm_b=m_q_norm_b,
               k_norm_b=m_k_norm_b, sinks=m_sinks, ssm_w_in=m_ssm_w_in, ssm_conv_w=m_ssm_conv_w, ssm_conv_b=m_ssm_conv_b,
               ssm_dt_bias=m_ssm_dt_bias, ssm_a_log=m_ssm_a_log, ssm_d=m_ssm_d, ssm_norm=m_ssm_norm, ssm_w_out=m_ssm_w_out,
               ffn_w_in=m_ffn_w_in, ffn_conv_w=m_ffn_conv_w, ffn_conv_b=m_ffn_conv_b, ffn_w_out=m_ffn_w_out)
    var = dict(norm_mix=v_norm_mix, norm_ffn=v_norm_ffn, attn_w_in=v_attn_w_in, attn_w_out=v_attn_w_out,
               relpos_table=v_relpos_table, q_norm_a=v_q_norm_a, k_norm_a=v_k_norm_a, q_norm_b=v_q_norm_b,
               k_norm_b=v_k_norm_b, sinks=v_sinks, ssm_w_in=v_ssm_w_in, ssm_conv_w=v_ssm_conv_w, ssm_conv_b=v_ssm_conv_b,
               ssm_dt_bias=v_ssm_dt_bias, ssm_a_log=v_ssm_a_log, ssm_d=v_ssm_d, ssm_norm=v_ssm_norm, ssm_w_out=v_ssm_w_out,
               ffn_w_in=v_ffn_w_in, ffn_conv_w=v_ffn_conv_w, ffn_conv_b=v_ffn_conv_b, ffn_w_out=v_ffn_w_out)

    def piece(n, l):
        return (w[n][l].T if _AX2[n] == 1 else w[n][l]).astype(BF16)

    def gather_of(names_layers):
        return _Gather2([piece(n, l) for n, l in names_layers])

    def joined(got):
        return [g.reshape(-1, D_MODEL) for g in got]

    first = [('attn_w_in', 0), ('attn_w_out', 0)]
    got = _exchange(_Gather2([piece(n, l) for n, l in first] + [_pack([w[n] for n in _SMALL], F32)]), "gather_attn")
    wt_attn_in, w_attn_out = joined(got[:2])
    full = {}
    for n, g in zip(_SMALL, _unpack(got[2], [w[n].shape for n in _SMALL], lead=(N_DEV,))):
        full[n] = _join(g, _SHARD_AX[n])
    ssm_cw8 = _rows8(full['ssm_conv_w'][0])
    ssm_cb = full['ssm_conv_b']
    ssm_nw = full['ssm_norm']
    ffn_cw8 = [_rows8(full['ffn_conv_w'][l]) for l in range(2)]
    ffn_cb = [ffn_conv_b[l:l + 1] for l in range(2)]

    x0 = x[0]
    target = loss_target[0]
    t = x0.shape[0]

    g_mix0, g_mix1 = norm_mix[0:1], norm_mix[1:2]
    g_ffn0, g_ffn1 = norm_ffn[0:1], norm_ffn[1:2]
    proj, h0 = _rms_mm(x0, g_mix0, wt_attn_in, ATTN_PROJ, "mm_attn_in", F32)
    hn_w = jnp.concatenate([jnp.tile(v, (1, 2)) for v in (q_norm_a, k_norm_a, q_norm_b, k_norm_b)], axis=0)
    qa, kpa, vpa, qb, kpb, vpb = _headnorm_fwd(proj, hn_w, "headnorm")
    table = jnp.pad(relpos_table[0], ((0, 0), (0, REL_W - (2 * MAX_REL + 1))))
    bias_a = jnp.where(_band_mask(A_PREV, PAD_A)[None], jnp.transpose(_relpos_fwd(table, "relpos_bias"), (1, 0, 2)), NEG)
    rel_b = jnp.arange(TQ)[:, None] - (jnp.arange(PAD_B + TQ)[None, :] - PAD_B)
    slopes = 2.0 ** (-8.0 * jnp.arange(1, N_HEADS + 1, dtype=F32) / N_HEADS)
    bias_b = jnp.where(_band_mask(B_PREV, PAD_B)[None], -slopes[:, None, None] * jnp.abs(rel_b).astype(F32)[None], NEG)
    no_sinks = jnp.full((N_HEADS,), NEG, F32)
    ffn0_w, ssm_w, ffn1_w = [('ffn_w_in', 0), ('ffn_w_out', 0)], [('ssm_w_in', 0), ('ssm_w_out', 0)], [('ffn_w_in', 1), ('ffn_w_out', 1)]
    oa, stats_a, got = _attn_fwd(qa, kpa, vpa, bias_a, no_sinks, PAD_A, "attn_a", comm=gather_of(ffn0_w + ssm_w))
    wt_ffn_in0, w_ffn_out0, wt_ssm_in, w_ssm_out = joined(got)
    ob, stats_b, _ = _attn_fwd(qb, kpb, vpb, bias_b, sinks[0], PAD_B, "attn_b")
    wt_ssm_dt = jnp.pad(wt_ssm_in[ZX:], ((0, LANES - SSM_HEADS), (0, 0)))
    x1 = _mm_pair(oa, ob, w_attn_out, x0, "mm_attn_out")
    a0, ffn0_saved = _ffn_fwd(x1, g_ffn0, wt_ffn_in0, ffn_cw8[0], ffn_cb[0], "0")
    x2 = _mm(a0, w_ffn_out0, "mm_ffn_out0", res=x1)

    dt_bias = _lanes128(ssm_dt_bias[0])
    alog = _lanes128(ssm_a_log[0])
    dexp = jnp.repeat(ssm_d[0], HEAD_DIM).reshape(1, D_INNER)
    zx, h2, xbc, conv_pre, dtraw, dt = _ssm_in_pre(x2, g_mix1, wt_ssm_in, wt_ssm_dt, ssm_cw8, ssm_cb, dt_bias, "mm_ssm_in")
    (y, sprev, y4), got = _ssd_fwd(xbc, dt, alog, zx, dexp, ssm_nw, "ssd_fwd", comm=gather_of(ffn1_w))
    wt_ffn_in1, w_ffn_out1 = joined(got)
    x3 = _mm(y4, w_ssm_out, "mm_ssm_out", res=x2)
    a1, ffn1_saved = _ffn_fwd(x3, g_ffn1, wt_ffn_in1, ffn_cw8[1], ffn_cb[1], "1")

    dx4, dx4b, sq = _mm_loss(a1, w_ffn_out1, x3, target, "mm_ffn_out1_loss")
    loss = lax.psum(0.5 * jnp.sum(sq) / D_MODEL, ("x", "y", "c"))

    grads = {}

    def scatter_of(grads_2d):
        return _Comm([g.reshape(N_DEV, -1, D_MODEL) for g in grads_2d], [True] * len(grads_2d))

    dx3, dx3b, dg_ffn1, dwtin1, dcw1, dcb1, dwout1 = _ffn_bwd(
        dx4, dx4b, x3, g_ffn1, wt_ffn_in1, ffn_cw8[1], w_ffn_out1, ffn1_saved, "1")

    dy4 = _mm(dx3b, w_ssm_out, "mm_ssm_dy", trans_b=True)
    dw_ssm_out = _mm_tn(y4, dx3b, "mm_ssm_dwout")
    (dxbc, ddt, dalog, dz, dd_lane, dnw), parts_ffn1 = _ssd_bwd(
        xbc, dt, alog, sprev, dy4, y, zx, dexp, ssm_nw, "ssd_bwd", comm=scatter_of([dwtin1, dwout1]))
    dxr, dcw_s, dcb_s = _ssm_pre_bwd(zx, conv_pre, dxbc, ssm_cw8, "ssm_pre_bwd")
    ddtraw, ddtb = _dt_bwd(dtraw, dt_bias, ddt, "ssm_dt_bwd")
    dwt_ssm_in = jnp.concatenate([
        _mm_tn(dz, h2, "mm_ssm_dwin_z"), _mm_tn(dxr, h2, "mm_ssm_dwin_x"),
        _mm_tn(ddtraw, h2, "mm_ssm_dwin_dt")[:SSM_HEADS]], axis=0)
    dx2, dx2b, dg_mix1 = _mm_rms_bwd([(dz, wt_ssm_in, 0), (dxr, wt_ssm_in, D_INNER), (ddtraw, wt_ssm_dt, 0)],
                                     x2, g_mix1, dx3, "mm_ssm_dh")
    grads['ssm_conv_w'] = dcw_s[:4][None]
    grads['ssm_conv_b'] = dcb_s
    grads['ssm_norm'] = dnw
    grads['ssm_dt_bias'] = ddtb[:, :SSM_HEADS]
    grads['ssm_a_log'] = dalog[:, :SSM_HEADS]
    grads['ssm_d'] = jnp.sum(dd_lane.reshape(SSM_HEADS, HEAD_DIM), axis=1)[None]

    dx1, dx1b, dg_ffn0, parts_in0, dcw0, dcb0, parts_out0 = _ffn_bwd(
        dx2, dx2b, x1, g_ffn0, wt_ffn_in0, ffn_cw8[0], w_ffn_out0, ffn0_saved, "0", scatter_of=scatter_of)
    parts_ffn0 = [parts_in0, parts_out0]
    grads['ffn_conv_w'] = jnp.stack([dcw0, dcw1])
    grads['ffn_conv_b'] = jnp.concatenate([dcb0, dcb1], axis=0)
    grads['norm_ffn'] = jnp.concatenate([dg_ffn0, dg_ffn1], axis=0)

    do = _mm(dx1b, w_attn_out, "mm_attn_do", out_dtype=BF16, trans_b=True)
    dw_attn_out = jnp.concatenate([_mm_tn(oa, dx1b, "mm_attn_dwout_a"), _mm_tn(ob, dx1b, "mm_attn_dwout_b")], axis=0)
    (dqa, dkpa, dvpa, dbias_a, _), parts_ssm = _attn_bwd(
        qa, kpa, vpa, bias_a, no_sinks, do, stats_a, oa, 0, PAD_A, "attn_a_bwd",
        comm=scatter_of([dwt_ssm_in, dw_ssm_out, dw_attn_out]))
    (dqb, dkpb, dvpb, _, dsink), _ = _attn_bwd(qb, kpb, vpb, bias_b, sinks[0], do, stats_b, ob, 4, PAD_B, "attn_b_bwd")
    grads['relpos_table'] = _relpos_bwd(jnp.transpose(dbias_a, (1, 0, 2)), "relpos_bwd")[None, :, :2 * MAX_REL + 1]
    grads['sinks'] = dsink[:, :2, 0].reshape(1, N_HEADS)
    dproj, dhn = _headnorm_bwd(proj, hn_w, dqa, dkpa, dvpa, dqb, dkpb, dvpb, "headnorm_bwd")
    dhn = dhn[:, :HEAD_DIM] + dhn[:, HEAD_DIM:]
    for k, n in enumerate(('q_norm_a', 'k_norm_a', 'q_norm_b', 'k_norm_b')):
        grads[n] = dhn[k:k + 1]
    dwt_attn_in = _mm_tn(dproj, h0, "mm_attn_dwin")
    dx0, _, dg_mix0, parts_attn_in = _mm_rms_bwd([(dproj, wt_attn_in, 0)], x0, g_mix0, dx1, "mm_attn_dh",
                                                 comm=scatter_of([dwt_attn_in]))
    grads['norm_mix'] = jnp.concatenate([dg_mix0, dg_mix1], axis=0)

    def summed_t(parts, name):
        return _sum_parts(parts, name).T[None]

    sm_shapes = [w[n].shape for n in _SMALL]
    rp_shapes = [w[n].shape for n in _REPL]
    recv = _exchange(_Comm(
        [_pack([_split(grads[n], _SHARD_AX[n]) for n in _SMALL], F32, lead=(N_DEV,)), _pack([grads[n] for n in _REPL], F32)],
        [True, False]), "exchange_small")
    big_parts = {
        'attn_w_in': [summed_t(parts_attn_in[0], "sum_attn_w_in")], 'attn_w_out': [parts_ssm[2]],
        'ssm_w_in': [summed_t(parts_ssm[0], "sum_ssm_w_in")], 'ssm_w_out': [parts_ssm[1]],
        'ffn_w_in': [summed_t(parts_ffn0[0], "sum_ffn_w_in0"), summed_t(parts_ffn1[0], "sum_ffn_w_in1")],
        'ffn_w_out': [parts_ffn0[1], parts_ffn1[1]],
    }
    res = [{}, {}, {}, {}]
    for n in _BIG:
        for kind, a in enumerate(_adamw(big_parts[n], w[n], mom[n], var[n], f"adamw_{n}")):
            res[kind][n] = a
    for names, shapes, parts in ((_SMALL, sm_shapes, recv[0]), (_REPL, rp_shapes, recv[1])):
        outs = _adamw([parts], _pack([w[n] for n in names], F32)[None], _pack([mom[n] for n in names], F32)[None],
                      _pack([var[n] for n in names], F32)[None], "adamw_" + ("small" if names is _SMALL else "replicated"))
        for kind, flat in enumerate(outs):
            for n, a in zip(names, _unpack(flat[0], shapes)):
                res[kind][n] = a
    return (loss, dx0[None], *[res[0][n] for n in _WEIGHTS], *[res[1][n] for n in _WEIGHTS],
            *[res[2][n] for n in _WEIGHTS], *[res[3][n] for n in _WEIGHTS])
```

```python
import jax
import jax.numpy as jnp
from jax import lax
from jax.experimental import pallas as pl
from jax.experimental.pallas import tpu as pltpu

F32 = jnp.float32
BF16 = jnp.bfloat16
HI = lax.Precision.HIGHEST
MESH = pl.DeviceIdType.MESH
NEG = -1e30

N_DEV = 8
D_MODEL = 1024
EPS = 1e-6
CHUNK = 64
HEAD_DIM = 64
N_HEADS = 8
A_PREV = 8
B_PREV = 2
MAX_REL = 256
TQ = 2 * CHUNK
ATT_SUB = 32
PAD_A = A_PREV * CHUNK
PAD_B = B_PREV * CHUNK
REL_W = PAD_A + TQ
D_ATT = N_HEADS * HEAD_DIM
COL_QA, COL_KA, COL_VA, COL_QB = 0, D_ATT, 2 * D_ATT, 3 * D_ATT
COL_KB, COL_VB = 4 * D_ATT, 4 * D_ATT + 2 * HEAD_DIM
ATTN_PROJ = COL_VB + 2 * HEAD_DIM
D_INNER = 2048
SSM_HEADS = 32
SSM_GROUPS = 4
SSM_STATE = 128
XBC = D_INNER + 2 * SSM_GROUPS * SSM_STATE
ZX = D_INNER + XBC
D_FF = 2816
SSD_L = 128
SSD_L_BWD = 2 * SSD_L
LANES = 128
VMEM_LIMIT = 56 << 20

ADAM_LR, ADAM_B1, ADAM_B2, ADAM_EPS, ADAM_WD, ADAM_STEP = 0.001, 0.9, 0.999, 1e-08, 0.01, 10


def _cp(sem=None):
    return pltpu.CompilerParams(dimension_semantics=sem, vmem_limit_bytes=VMEM_LIMIT)


def _dot(a, b, ca=1, cb=0, prec=None):
    return lax.dot_general(a, b, (((ca,), (cb,)), ((), ())), preferred_element_type=F32, precision=prec)


def _pick(n, cands):
    for c in cands:
        if n % c == 0:
            return c
    return n


def _lo_mask():
    return lax.broadcasted_iota(jnp.int32, (1, LANES), 1) < HEAD_DIM


_TN_CHUNKS = (1408, 1536, 1152, 1024, 512, 256, 128)


TN_MAX_ROWS = 3072
MM_WIDE = 2304


def _mm_tn(a, b, name):
    kdim, m = a.shape
    n = b.shape[1]
    assert b.shape[0] == kdim, (a.shape, b.shape)
    mb = m if m <= TN_MAX_ROWS else m // 2
    tn = _pick(n, _TN_CHUNKS)
    tk = _pick(kdim, (512, 256, 128))
    nk = kdim // tk

    def body(a_ref, b_ref, o_ref, acc):
        k = pl.program_id(1)

        @pl.when(k == 0)
        def _():
            acc[...] = jnp.zeros_like(acc)

        av = a_ref[...]
        for c in range(0, n, tn):
            acc[:, c:c + tn] += _dot(av, b_ref[:, c:c + tn], 0, 0)

        @pl.when(k == nk - 1)
        def _():
            o_ref[...] = acc[...].astype(BF16)

    return pl.pallas_call(
        body, name=name, grid=(m // mb, nk),
        in_specs=[pl.BlockSpec((tk, mb), lambda j, k: (k, j)), pl.BlockSpec((tk, n), lambda j, k: (k, 0))],
        out_specs=pl.BlockSpec((mb, n), lambda j, k: (j, 0)), out_shape=jax.ShapeDtypeStruct((m, n), BF16),
        scratch_shapes=[pltpu.VMEM((mb, n), F32)], compiler_params=_cp(("parallel", "arbitrary")),
    )(a, b)


def _mm(a, b, name, out_dtype=F32, res=None, trans_b=False):
    m, kdim = a.shape
    n = b.shape[0] if trans_b else b.shape[1]
    assert (b.shape[1] if trans_b else b.shape[0]) == kdim, (a.shape, b.shape)
    tn = _pick(n, _TN_CHUNKS)
    tm = _pick(m, (256, 128) if n > MM_WIDE else (512, 256, 128))

    def body(*refs):
        if res is None:
            a_ref, b_ref, o_ref = refs
        else:
            a_ref, b_ref, r_ref, o_ref = refs
        av = a_ref[...]
        for c in range(0, n, tn):
            r = _dot(av, b_ref[c:c + tn, :], 1, 1) if trans_b else _dot(av, b_ref[:, c:c + tn], 1, 0)
            if res is not None:
                r = r + r_ref[:, c:c + tn]
            o_ref[:, c:c + tn] = r.astype(out_dtype)

    in_specs = [pl.BlockSpec((tm, kdim), lambda i: (i, 0)), pl.BlockSpec(b.shape, lambda i: (0, 0))]
    args = [a, b]
    if res is not None:
        in_specs.append(pl.BlockSpec((tm, n), lambda i: (i, 0)))
        args.append(res)
    return pl.pallas_call(
        body, name=name, grid=(m // tm,), in_specs=in_specs, out_specs=pl.BlockSpec((tm, n), lambda i: (i, 0)),
        out_shape=jax.ShapeDtypeStruct((m, n), out_dtype), compiler_params=_cp(("parallel",)),
    )(*args)


def _mm_pair(a1, a2, b, res, name):
    m, k1 = a1.shape
    k2 = a2.shape[1]
    n = b.shape[1]
    assert b.shape[0] == k1 + k2
    tm = _pick(m, (512, 256, 128))

    def body(a1_ref, a2_ref, b_ref, r_ref, o_ref):
        o_ref[...] = _dot(a1_ref[...], b_ref[:k1, :], 1, 0) + _dot(a2_ref[...], b_ref[k1:, :], 1, 0) + r_ref[...]

    row = pl.BlockSpec((tm, n), lambda i: (i, 0))
    return pl.pallas_call(
        body, name=name, grid=(m // tm,),
        in_specs=[pl.BlockSpec((tm, k1), lambda i: (i, 0)), pl.BlockSpec((tm, k2), lambda i: (i, 0)),
                  pl.BlockSpec(b.shape, lambda i: (0, 0)), row],
        out_specs=row, out_shape=jax.ShapeDtypeStruct((m, n), F32), compiler_params=_cp(("parallel",)),
    )(a1, a2, b, res)


def _rms_mm(x, g, bt, n, name, out_dtype):
    t, d = x.shape
    tn = _pick(n, _TN_CHUNKS)
    tm = _pick(t, (256, 128))

    def body(x_ref, g_ref, b_ref, o_ref, h_ref):
        xv = x_ref[...]
        r = lax.rsqrt(jnp.mean(xv * xv, axis=-1, keepdims=True) + EPS)
        h = (xv * r * g_ref[...]).astype(BF16)
        h_ref[...] = h
        for c in range(0, n, tn):
            o_ref[:, c:c + tn] = _dot(h, b_ref[c:c + tn, :], 1, 1).astype(out_dtype)

    row = pl.BlockSpec((tm, d), lambda i: (i, 0))
    return pl.pallas_call(
        body, name=name, grid=(t // tm,),
        in_specs=[row, pl.BlockSpec((1, d), lambda i: (0, 0)), pl.BlockSpec(bt.shape, lambda i: (0, 0))],
        out_specs=[pl.BlockSpec((tm, n), lambda i: (i, 0)), row],
        out_shape=[jax.ShapeDtypeStruct((t, n), out_dtype), jax.ShapeDtypeStruct((t, d), BF16)],
        compiler_params=_cp(("parallel",)),
    )(x, g, bt)


def _mm_rms_bwd(terms, x, g, dres, name, comm=None):
    t, d = x.shape
    tm = _pick(t, (256, 128))
    weights = []
    for _, b, _ in terms:
        if not any(b is wgt for wgt in weights):
            weights.append(b)
    which = [next(k for k, wgt in enumerate(weights) if wgt is b) for _, b, _ in terms]
    na, nw = len(terms), len(weights)

    def body(*refs):
        a_refs, w_refs = refs[:na], refs[na:na + nw]
        x_ref, g_ref, dr_ref, dx_ref, dxb_ref, dg_ref = refs[na + nw:]
        dhv = None
        for (a, _, row), a_ref, k in zip(terms, a_refs, which):
            part = _dot(a_ref[...], w_refs[k][row:row + a.shape[1], :], 1, 0)
            dhv = part if dhv is None else dhv + part
        xv = x_ref[...]
        r = lax.rsqrt(jnp.mean(xv * xv, axis=-1, keepdims=True) + EPS)
        xh = xv * r
        dxh = dhv * g_ref[...]
        dx = dr_ref[...] + r * (dxh - xh * jnp.mean(dxh * xh, axis=-1, keepdims=True))
        dx_ref[...] = dx
        dxb_ref[...] = dx.astype(BF16)

        @pl.when(pl.program_id(0) == 0)
        def _():
            dg_ref[...] = jnp.zeros_like(dg_ref)

        dg_ref[...] += jnp.sum(dhv * xh, axis=0, keepdims=True)

    row = pl.BlockSpec((tm, d), lambda i: (i, 0))
    vec = pl.BlockSpec((1, d), lambda i: (0, 0))
    in_specs = ([pl.BlockSpec((tm, a.shape[1]), lambda i: (i, 0)) for a, _, _ in terms]
                + [pl.BlockSpec(wgt.shape, lambda i: (0, 0)) for wgt in weights])
    outs, got = _call(
        body, name=name, grid=(t // tm,), in_specs=in_specs + [row, vec, row], out_specs=[row, row, vec],
        out_shape=[jax.ShapeDtypeStruct((t, d), F32), jax.ShapeDtypeStruct((t, d), BF16), jax.ShapeDtypeStruct((1, d), F32)],
        args=(*[a for a, _, _ in terms], *weights, x, g, dres), sem=("arbitrary",), comm=comm)
    return (*outs, got) if comm is not None else tuple(outs)


def _mm_loss(a, b, res, target, name):
    t, kdim = a.shape
    d = b.shape[1]
    tm = _pick(t, (512, 256, 128))

    def body(a_ref, b_ref, r_ref, t_ref, dy_ref, dyb_ref, acc_ref):
        @pl.when(pl.program_id(0) == 0)
        def _():
            acc_ref[...] = jnp.zeros_like(acc_ref)

        err = _dot(a_ref[...], b_ref[...], 1, 0) + r_ref[...] - t_ref[...]
        dy = err * (1.0 / d)
        dy_ref[...] = dy
        dyb_ref[...] = dy.astype(BF16)
        acc_ref[...] += jnp.sum(err * err, axis=0, keepdims=True)

    row = pl.BlockSpec((tm, d), lambda i: (i, 0))
    vec = pl.BlockSpec((1, d), lambda i: (0, 0))
    return pl.pallas_call(
        body, name=name, grid=(t // tm,),
        in_specs=[pl.BlockSpec((tm, kdim), lambda i: (i, 0)), pl.BlockSpec((kdim, d), lambda i: (0, 0)), row, row],
        out_specs=[row, row, vec],
        out_shape=[jax.ShapeDtypeStruct((t, d), F32), jax.ShapeDtypeStruct((t, d), BF16), jax.ShapeDtypeStruct((1, d), F32)],
        compiler_params=_cp(("arbitrary",)),
    )(a, b, res, target)


def _head_sums(v):
    ri = lax.broadcasted_iota(jnp.int32, (LANES, LANES), 0) // HEAD_DIM
    ci = lax.broadcasted_iota(jnp.int32, (LANES, LANES), 1) // HEAD_DIM
    ones = (ri == ci).astype(BF16)
    hi = v.astype(BF16)
    lo_part = (v - hi.astype(F32)).astype(BF16)
    return _dot(hi, ones, 1, 0) + _dot(lo_part, ones, 1, 0)


def _head_rms(xs):
    r = lax.rsqrt(_head_sums(xs * xs) * (1.0 / HEAD_DIM) + EPS)
    return xs * r, r


def _head_rms_bwd(xs, w, dy):
    xh, r = _head_rms(xs)
    dxh = dy * w
    mm = _head_sums(dxh * xh) * (1.0 / HEAD_DIM)
    return r * (dxh - xh * mm), dy * xh


_QSCALE = HEAD_DIM ** -0.5


def _headnorm_fwd(proj, ws, name):
    t = proj.shape[0]
    tm = TQ
    lead = PAD_A // tm
    leadb = PAD_B // tm

    def body(p_ref, w_ref, qa_ref, ka_ref, va_ref, qb_ref, kb_ref, vb_ref):
        data = pl.program_id(0) >= lead
        lo = _lo_mask()

        def put(ref, c, val):
            ref[:, c:c + val.shape[1]] = jnp.where(data, val, 0.0).astype(BF16)

        def per_query_head(slab):
            other = pltpu.roll(slab, HEAD_DIM, 1)
            e0, e1 = jnp.where(lo, slab, other), jnp.where(lo, other, slab)
            return jnp.concatenate([e0, e0, e1, e1], axis=1)

        for s in range(D_ATT // LANES):
            c = LANES * s
            xh, _ = _head_rms(p_ref[:, COL_QA + c:COL_QA + c + LANES])
            qa_ref[:, c:c + LANES] = (xh * w_ref[0:1, :] * _QSCALE).astype(BF16)
            xh, _ = _head_rms(p_ref[:, COL_KA + c:COL_KA + c + LANES])
            put(ka_ref, c, xh * w_ref[1:2, :])
            xh, _ = _head_rms(p_ref[:, COL_QB + c:COL_QB + c + LANES])
            qb_ref[:, c:c + LANES] = (xh * w_ref[2:3, :] * _QSCALE).astype(BF16)
        put(va_ref, 0, p_ref[:, COL_VA:COL_VA + D_ATT])
        xh, _ = _head_rms(p_ref[:, COL_KB:COL_KB + LANES])
        put(kb_ref, 0, per_query_head(xh * w_ref[3:4, :]))
        put(vb_ref, 0, per_query_head(p_ref[:, COL_VB:COL_VB + LANES]))

    src = lambda i: jnp.maximum(i - lead, 0)
    wide = pl.BlockSpec((tm, D_ATT), lambda i: (src(i), 0))
    pad_a = pl.BlockSpec((tm, D_ATT), lambda i: (i, 0))
    pad_b = pl.BlockSpec((tm, D_ATT), lambda i: (jnp.maximum(i - lead + leadb, 0), 0))
    sd = lambda rows: jax.ShapeDtypeStruct((rows, D_ATT), BF16)
    return pl.pallas_call(
        body, name=name, grid=(t // tm + lead,),
        in_specs=[pl.BlockSpec((tm, ATTN_PROJ), lambda i: (src(i), 0)), pl.BlockSpec((4, LANES), lambda i: (0, 0))],
        out_specs=[wide, pad_a, pad_a, wide, pad_b, pad_b],
        out_shape=[sd(t), sd(t + PAD_A), sd(t + PAD_A), sd(t), sd(t + PAD_B), sd(t + PAD_B)],
        compiler_params=_cp(("arbitrary",)),
    )(proj, ws)


def _headnorm_bwd(proj, ws, dqa, dkpa, dvpa, dqb, dkpb, dvpb, name):
    t = proj.shape[0]
    tm = TQ
    offa, offb = PAD_A // tm, PAD_B // tm

    def body(p_ref, w_ref, dqa_ref, dka_ref, dva_ref, dqb_ref, dkb_ref, dvb_ref, dp_ref, dw_ref):
        i = pl.program_id(0)
        lo = _lo_mask()

        @pl.when(i == 0)
        def _():
            dw_ref[...] = jnp.zeros_like(dw_ref)

        acc = [jnp.zeros((1, LANES), F32) for _ in range(4)]
        for s in range(D_ATT // LANES):
            c = LANES * s
            dx, dwl = _head_rms_bwd(p_ref[:, COL_QA + c:COL_QA + c + LANES], w_ref[0:1, :], dqa_ref[:, c:c + LANES] * _QSCALE)
            dp_ref[:, COL_QA + c:COL_QA + c + LANES] = dx.astype(BF16)
            acc[0] += jnp.sum(dwl, axis=0, keepdims=True)
            dx, dwl = _head_rms_bwd(p_ref[:, COL_KA + c:COL_KA + c + LANES], w_ref[1:2, :], dka_ref[:, c:c + LANES])
            dp_ref[:, COL_KA + c:COL_KA + c + LANES] = dx.astype(BF16)
            acc[1] += jnp.sum(dwl, axis=0, keepdims=True)
            dx, dwl = _head_rms_bwd(p_ref[:, COL_QB + c:COL_QB + c + LANES], w_ref[2:3, :], dqb_ref[:, c:c + LANES] * _QSCALE)
            dp_ref[:, COL_QB + c:COL_QB + c + LANES] = dx.astype(BF16)
            acc[2] += jnp.sum(dwl, axis=0, keepdims=True)
        dp_ref[:, COL_VA:COL_VA + D_ATT] = dva_ref[...].astype(BF16)

        def group_sum(ref):
            s0 = ref[:, 0:LANES] + ref[:, LANES:2 * LANES]
            s1 = ref[:, 2 * LANES:3 * LANES] + ref[:, 3 * LANES:4 * LANES]
            s0 = s0 + pltpu.roll(s0, HEAD_DIM, 1)
            s1 = s1 + pltpu.roll(s1, HEAD_DIM, 1)
            return jnp.where(lo, s0, s1)

        dx, dwl = _head_rms_bwd(p_ref[:, COL_KB:COL_KB + LANES], w_ref[3:4, :], group_sum(dkb_ref))
        dp_ref[:, COL_KB:COL_KB + LANES] = dx.astype(BF16)
        acc[3] += jnp.sum(dwl, axis=0, keepdims=True)
        dp_ref[:, COL_VB:COL_VB + LANES] = group_sum(dvb_ref).astype(BF16)
        for n in range(4):
            dw_ref[n:n + 1, :] += acc[n]

    wide = pl.BlockSpec((tm, D_ATT), lambda i: (i, 0))
    pa = pl.BlockSpec((tm, D_ATT), lambda i: (i + offa, 0))
    pb = pl.BlockSpec((tm, D_ATT), lambda i: (i + offb, 0))
    whole = pl.BlockSpec((tm, ATTN_PROJ), lambda i: (i, 0))
    return pl.pallas_call(
        body, name=name, grid=(t // tm,),
        in_specs=[whole, pl.BlockSpec((4, LANES), lambda i: (0, 0)), wide, pa, pa, wide, pb, pb],
        out_specs=[whole, pl.BlockSpec((4, LANES), lambda i: (0, 0))],
        out_shape=[jax.ShapeDtypeStruct((t, ATTN_PROJ), BF16), jax.ShapeDtypeStruct((4, LANES), F32)],
        compiler_params=_cp(("arbitrary",)),
    )(proj, ws, dqa, dkpa, dvpa, dqb, dkpb, dvpb)


ROLL_W = 1024


def _rel_onehot():
    r_io = lax.broadcasted_iota(jnp.int32, (REL_W, ROLL_W), 0)
    m_io = lax.broadcasted_iota(jnp.int32, (REL_W, ROLL_W), 1)
    return (r_io == jnp.clip(REL_W - 1 - m_io, -MAX_REL, MAX_REL) + MAX_REL).astype(F32)


def _relpos_fwd(table, name):
    def body(t_ref, o_ref):
        rr = _dot(t_ref[...], _rel_onehot(), 1, 0, HI)

        def step(q, c):
            o_ref[q] = pltpu.roll(rr, (ROLL_W - (TQ - 1) + q) % ROLL_W, 1)[:, :REL_W]
            return c

        lax.fori_loop(0, TQ, step, 0)

    return pl.pallas_call(
        body, name=name, out_shape=jax.ShapeDtypeStruct((TQ, N_HEADS, REL_W), F32),
        in_specs=[pl.BlockSpec(memory_space=pltpu.VMEM)], out_specs=pl.BlockSpec(memory_space=pltpu.VMEM),
        compiler_params=_cp(),
    )(table)


def _relpos_bwd(dbias_t, name):
    def body(d_ref, o_ref):
        def step(q, acc):
            row = jnp.concatenate([d_ref[q], jnp.zeros((N_HEADS, ROLL_W - REL_W), F32)], axis=1)
            return acc + pltpu.roll(row, TQ - 1 - q, 1)

        drr = lax.fori_loop(0, TQ, step, jnp.zeros((N_HEADS, ROLL_W), F32))
        o_ref[...] = _dot(drr, _rel_onehot(), 1, 1, HI)

    return pl.pallas_call(
        body, name=name, out_shape=jax.ShapeDtypeStruct((N_HEADS, REL_W), F32),
        in_specs=[pl.BlockSpec(memory_space=pltpu.VMEM)], out_specs=pl.BlockSpec(memory_space=pltpu.VMEM),
        compiler_params=_cp(),
    )(dbias_t)


def _attn_scores(qe, kw, bias, kvalid):
    return jnp.where(kvalid, _dot(qe, kw, 1, 1) + bias, NEG)


def _stat_cols(stats, e):
    return stats[:, 64 * e:64 * e + 1], stats[:, 64 * e + 32:64 * e + 33]


def _attn_fwd(q, kp, vp, bias, sinks, pad, name, comm=None):
    t, hd = q.shape
    w = pad + TQ

    def body(sink_ref, q_ref, k_ref, v_ref, b_ref, o_ref, st_ref):
        hp, i = pl.program_id(0), pl.program_id(1)
        lo = _lo_mask()
        lane = lax.broadcasted_iota(jnp.int32, (1, LANES), 1)
        for j in range(ATT_SUB):
            start = pl.multiple_of((i * ATT_SUB + j) * TQ, TQ)
            qv = q_ref[TQ * j:TQ * (j + 1), :]
            kw = k_ref[pl.ds(start, w), :]
            vw = v_ref[pl.ds(start, w), :]
            kvalid = (start + lax.broadcasted_iota(jnp.int32, (1, w), 1)) >= pad
            outs, ms, ls = [], [], []
            for e in range(2):
                sel = lo if e == 0 else jnp.logical_not(lo)
                qe = jnp.where(sel, qv, jnp.zeros_like(qv))
                snk = sink_ref[2 * hp + e]
                s = _attn_scores(qe, kw, b_ref[e], kvalid)
                m = jnp.maximum(jnp.max(s, axis=-1, keepdims=True), snk)
                acc = _dot(jnp.exp(s - m).astype(BF16), jnp.where(sel, vw, jnp.ones_like(vw)), 1, 0)
                denom = acc[:, 64 * (1 - e):64 * (1 - e) + 1] + jnp.exp(snk - m)
                outs.append(acc * (1.0 / denom))
                ms.append(m)
                ls.append(denom)
            o_ref[TQ * j:TQ * (j + 1), :] = jnp.where(lo, outs[0], outs[1]).astype(BF16)
            st_ref[TQ * j:TQ * (j + 1), :] = jnp.where(lane < 32, ms[0], jnp.where(lane < 64, ls[0],
                                                                                 jnp.where(lane < 96, ms[1], ls[1])))

    full = pl.BlockSpec((t + pad, LANES), lambda h, i: (0, h))
    tile = pl.BlockSpec((ATT_SUB * TQ, LANES), lambda h, i: (i, h))
    (o, stats), got = _call(
        body, name=name, grid=(hd // LANES, t // (ATT_SUB * TQ)),
        in_specs=[pl.BlockSpec(memory_space=pltpu.SMEM), tile, full, full, pl.BlockSpec((2, TQ, w), lambda h, i: (h, 0, 0))],
        out_specs=[tile, tile], out_shape=[jax.ShapeDtypeStruct((t, hd), BF16), jax.ShapeDtypeStruct((t, hd), F32)],
        args=(sinks, q, kp, vp, bias), sem=("parallel", "arbitrary"), comm=comm)
    return o, stats, got


def _attn_bwd(q, kp, vp, bias, sinks, do, stats, o, col_off, pad, name, comm=None):
    t, hd = q.shape
    w = pad + TQ
    nhp = hd // LANES

    def body(sink_ref, q_ref, k_ref, v_ref, b_ref, do_ref, st_ref, o_ref, dq_ref, dk_ref, dv_ref, db_ref, ds_ref):
        hp, i = pl.program_id(0), pl.program_id(1)

        @pl.when(i == 0)
        def _():
            dk_ref[...] = jnp.zeros_like(dk_ref)
            dv_ref[...] = jnp.zeros_like(dv_ref)
            db_ref[...] = jnp.zeros_like(db_ref)
            ds_ref[...] = jnp.zeros_like(ds_ref)

        lo = _lo_mask()
        row8 = lax.broadcasted_iota(jnp.int32, (8, LANES), 0)
        dbias = [None, None]
        dsink = jnp.zeros((8, LANES), F32)
        for j in range(ATT_SUB):
            start = pl.multiple_of((i * ATT_SUB + j) * TQ, TQ)
            qv = q_ref[TQ * j:TQ * (j + 1), :]
            dov = do_ref[TQ * j:TQ * (j + 1), :]
            kw = k_ref[pl.ds(start, w), :]
            vw = v_ref[pl.ds(start, w), :]
            kvalid = (start + lax.broadcasted_iota(jnp.int32, (1, w), 1)) >= pad
            stats = st_ref[TQ * j:TQ * (j + 1), :]
            od = dov.astype(F32) * o_ref[TQ * j:TQ * (j + 1), :].astype(F32)
            dqs, dkw, dvw = [], None, None
            for e in range(2):
                sel = lo if e == 0 else jnp.logical_not(lo)
                qe = jnp.where(sel, qv, jnp.zeros_like(qv))
                doe = jnp.where(sel, dov, jnp.zeros_like(dov))
                m, denom = _stat_cols(stats, e)
                inv = 1.0 / denom
                p = jnp.exp(_attn_scores(qe, kw, b_ref[e], kvalid) - m) * inv
                psink = jnp.exp(sink_ref[2 * hp + e] - m) * inv
                dp = _dot(doe, vw, 1, 1)
                delta = jnp.sum(jnp.where(sel, od, 0.0), axis=-1, keepdims=True)
                ds = p * (dp - delta)
                dbias[e] = ds if dbias[e] is None else dbias[e] + ds
                dsink = dsink + jnp.where(row8 == e, jnp.sum(-psink * delta, axis=0, keepdims=True), 0.0)
                dsb = ds.astype(BF16)
                dqs.append(_dot(dsb, kw, 1, 0))
                dk_e = _dot(dsb, qe, 0, 0)
                dv_e = _dot(p.astype(BF16), doe, 0, 0)
                dkw = dk_e if dkw is None else dkw + dk_e
                dvw = dv_e if dvw is None else dvw + dv_e
            dq_ref[TQ * j:TQ * (j + 1), :] = jnp.where(lo, dqs[0], dqs[1])
            dk_ref[pl.ds(start, w), :] += dkw
            dv_ref[pl.ds(start, w), :] += dvw
        for e in range(2):
            db_ref[e] += dbias[e]
        ds_ref[0] += dsink

    full = pl.BlockSpec((t + pad, LANES), lambda h, i: (0, h))
    tile = pl.BlockSpec((ATT_SUB * TQ, LANES), lambda h, i: (i, h))
    btile = pl.BlockSpec((2, TQ, w), lambda h, i: (h, 0, 0))
    return _call(
        body, name=name, grid=(nhp, t // (ATT_SUB * TQ)),
        in_specs=[pl.BlockSpec(memory_space=pltpu.SMEM), tile, full, full, btile,
                  pl.BlockSpec((ATT_SUB * TQ, LANES), lambda h, i: (i, h + col_off)), tile, tile],
        out_specs=[tile, full, full, btile, pl.BlockSpec((1, 8, LANES), lambda h, i: (h, 0, 0))],
        out_shape=[jax.ShapeDtypeStruct((t, hd), F32), jax.ShapeDtypeStruct((t + pad, hd), F32),
                   jax.ShapeDtypeStruct((t + pad, hd), F32), jax.ShapeDtypeStruct((N_HEADS, TQ, w), F32),
                   jax.ShapeDtypeStruct((nhp, 8, LANES), F32)],
        args=(sinks, q, kp, vp, bias, do, stats, o), sem=("parallel", "arbitrary"), comm=comm)


def _conv_apply(taps, w_ref, ktaps):
    out = taps[0] * w_ref[ktaps - 1:ktaps, :]
    for s in range(1, ktaps):
        out = out + taps[s] * w_ref[ktaps - 1 - s:ktaps - s, :]
    return out


def _sigmoid(x):
    return jax.nn.sigmoid(x)


def _silu_grad(x):
    sg = _sigmoid(x)
    return x * sg, sg * (1.0 + x * (1.0 - sg))


FFN_HALO = 16
FFN_BT = 256
FFN_BC = 1408


def _ffn_in_mid(x, g, wt, w8, b, name):
    t, d = x.shape
    f = D_FF
    tm = FFN_BT

    def body(x_ref, g_ref, b_ref, w_ref, cb_ref, gu_ref, h_ref, a_ref, gc_ref, halo_ref):
        @pl.when(pl.program_id(0) == 0)
        def _():
            halo_ref[...] = jnp.zeros_like(halo_ref)

        xv = x_ref[...]
        r = lax.rsqrt(jnp.mean(xv * xv, axis=-1, keepdims=True) + EPS)
        h = (xv * r * g_ref[...]).astype(BF16)
        h_ref[...] = h
        for c in range(0, f, FFN_BC):
            cs = slice(c, c + FFN_BC)
            gate = _dot(h, b_ref[c:c + FFN_BC, :], 1, 1).astype(BF16)
            up = _dot(h, b_ref[f + c:f + c + FFN_BC, :], 1, 1).astype(BF16)
            gu_ref[:, cs] = gate
            gu_ref[:, f + c:f + c + FFN_BC] = up
            gf = gate.astype(F32)
            ext = jnp.concatenate([halo_ref[:, cs], gf], axis=0)
            gc = (cb_ref[:, cs] + gf * w_ref[2:3, cs] + pltpu.roll(ext, 1, 0)[8:] * w_ref[1:2, cs]
                  + pltpu.roll(ext, 2, 0)[8:] * w_ref[0:1, cs])
            a_ref[:, cs] = (gc * _sigmoid(gc) * up.astype(F32)).astype(BF16)
            gc_ref[:, cs] = gc.astype(BF16)
            halo_ref[:, cs] = gf[tm - 8:]

    row = pl.BlockSpec((tm, d), lambda i: (i, 0))
    row_f = pl.BlockSpec((tm, f), lambda i: (i, 0))
    return pl.pallas_call(
        body, name=name, grid=(t // tm,),
        in_specs=[row, pl.BlockSpec((1, d), lambda i: (0, 0)), pl.BlockSpec((2 * f, d), lambda i: (0, 0)),
                  pl.BlockSpec((8, f), lambda i: (0, 0)), pl.BlockSpec((1, f), lambda i: (0, 0))],
        out_specs=[pl.BlockSpec((tm, 2 * f), lambda i: (i, 0)), row, row_f, row_f],
        out_shape=[jax.ShapeDtypeStruct((t, 2 * f), BF16), jax.ShapeDtypeStruct((t, d), BF16), jax.ShapeDtypeStruct((t, f), BF16),
                   jax.ShapeDtypeStruct((t, f), BF16)],
        scratch_shapes=[pltpu.VMEM((8, f), F32)], compiler_params=_cp(("arbitrary",)),
    )(x, g, wt, w8, b)


def _ffn_mid_bwd(gu, gc, dxb, w_out, w8, name, comm=None):
    t, d = dxb.shape
    f = D_FF
    tm, hr = FFN_BT, FFN_HALO
    nt = t // tm
    n = tm + hr

    def body(g_ref, u_ref, un_ref, c_ref, cn_ref, dx_ref, dxn_ref, wo_ref, w_ref, dgu_ref, dw_ref, db_ref):
        i = pl.program_id(0)
        last = i == nt - 1

        @pl.when(i == 0)
        def _():
            dw_ref[...] = jnp.zeros_like(dw_ref)
            db_ref[...] = jnp.zeros_like(db_ref)

        dxe = jnp.concatenate([dx_ref[...], dxn_ref[...]], axis=0)
        row = lax.broadcasted_iota(jnp.int32, (n, 1), 0)
        keep = (row < tm) | jnp.logical_not(last)
        for c in range(0, f, FFN_BC):
            cs = slice(c, c + FFN_BC)
            act, dact = _silu_grad(jnp.concatenate([c_ref[:, cs], cn_ref[:, cs]], axis=0).astype(F32))
            da = _dot(dxe, wo_ref[cs, :], 1, 1)
            up = jnp.concatenate([u_ref[:, cs], un_ref[:, cs]], axis=0).astype(F32)
            dgc = jnp.where(keep, da * up * dact, 0.0)
            nxt = [dgc[:tm], pltpu.roll(dgc, n - 1, 0)[:tm], pltpu.roll(dgc, n - 2, 0)[:tm]]
            dgu_ref[:, f + c:f + c + FFN_BC] = (da[:tm] * act[:tm]).astype(BF16)
            dgu_ref[:, cs] = (nxt[0] * w_ref[2:3, cs] + nxt[1] * w_ref[1:2, cs] + nxt[2] * w_ref[0:1, cs]).astype(BF16)
            gate = g_ref[:, cs].astype(F32)
            db_ref[:, cs] += jnp.sum(nxt[0], axis=0, keepdims=True)
            for s in range(3):
                dw_ref[2 - s:3 - s, cs] += jnp.sum(nxt[s] * gate, axis=0, keepdims=True)

    r = tm // hr
    nxt_blk = lambda i: jnp.minimum((i + 1) * r, t // hr - 1)
    row_f = pl.BlockSpec((tm, f), lambda i: (i, 0))
    halo_f = pl.BlockSpec((hr, f), lambda i: (nxt_blk(i), 0))
    return _call(
        body, name=name, grid=(nt,),
        in_specs=[row_f, pl.BlockSpec((tm, f), lambda i: (i, 1)), pl.BlockSpec((hr, f), lambda i: (nxt_blk(i), 1)),
                  row_f, halo_f,
                  pl.BlockSpec((tm, d), lambda i: (i, 0)), pl.BlockSpec((hr, d), lambda i: (nxt_blk(i), 0)),
                  pl.BlockSpec((f, d), lambda i: (0, 0)), pl.BlockSpec((8, f), lambda i: (0, 0))],
        out_specs=[pl.BlockSpec((tm, 2 * f), lambda i: (i, 0)), pl.BlockSpec((8, f), lambda i: (0, 0)),
                   pl.BlockSpec((1, f), lambda i: (0, 0))],
        out_shape=[jax.ShapeDtypeStruct((t, 2 * f), BF16), jax.ShapeDtypeStruct((8, f), F32), jax.ShapeDtypeStruct((1, f), F32)],
        args=(gu, gu, gu, gc, gc, dxb, dxb, w_out, w8), sem=("arbitrary",), comm=comm)


PRE_TM = 256
PRE_TC = 1024


def _softplus_heads(v):
    sp = jnp.maximum(v, 0.0) + jnp.log(1.0 + jnp.exp(-jnp.abs(v)))
    return jnp.where(lax.broadcasted_iota(jnp.int32, (1, LANES), 1) < SSM_HEADS, sp, 0.0)


def _ssm_in_pre(x, g, wt, wt_dt, w8, b, dt_bias, name):
    t, d = x.shape
    tm, tc = PRE_TM, PRE_TC

    def body(x_ref, g_ref, b_ref, bdt_ref, w_ref, cb_ref, db_ref, zx_ref, h_ref, o_ref, c_ref, dtr_ref, dt_ref, halo_ref):
        @pl.when(pl.program_id(0) == 0)
        def _():
            halo_ref[...] = jnp.zeros_like(halo_ref)

        xv = x_ref[...]
        r = lax.rsqrt(jnp.mean(xv * xv, axis=-1, keepdims=True) + EPS)
        h = (xv * r * g_ref[...]).astype(BF16)
        h_ref[...] = h
        dtr = _dot(h, bdt_ref[...], 1, 1)
        dtr_ref[...] = dtr
        dt_ref[...] = _softplus_heads(dtr + db_ref[...])
        for c in range(0, ZX, tc):
            v = _dot(h, b_ref[c:c + tc, :], 1, 1)
            zx_ref[:, c:c + tc] = v
            if c >= D_INNER:
                cs = slice(c - D_INNER, c - D_INNER + tc)
                ext = jnp.concatenate([halo_ref[:, cs], v], axis=0)
                conv = cb_ref[:, cs] + v * w_ref[3:4, cs]
                for s in (1, 2, 3):
                    conv = conv + pltpu.roll(ext, s, 0)[8:] * w_ref[3 - s:4 - s, cs]
                o_ref[:, cs] = conv * _sigmoid(conv)
                c_ref[:, cs] = conv.astype(BF16)
                halo_ref[:, cs] = v[tm - 8:]

    row = pl.BlockSpec((tm, d), lambda i: (i, 0))
    row_x = pl.BlockSpec((tm, XBC), lambda i: (i, 0))
    row_h = pl.BlockSpec((tm, LANES), lambda i: (i, 0))
    return pl.pallas_call(
        body, name=name, grid=(t // tm,),
        in_specs=[row, pl.BlockSpec((1, d), lambda i: (0, 0)), pl.BlockSpec(wt.shape, lambda i: (0, 0)),
                  pl.BlockSpec(wt_dt.shape, lambda i: (0, 0)),
                  pl.BlockSpec((8, XBC), lambda i: (0, 0)), pl.BlockSpec((1, XBC), lambda i: (0, 0)),
                  pl.BlockSpec((1, LANES), lambda i: (0, 0))],
        out_specs=[pl.BlockSpec((tm, ZX), lambda i: (i, 0)), row, row_x, row_x, row_h, row_h],
        out_shape=[jax.ShapeDtypeStruct((t, ZX), F32), jax.ShapeDtypeStruct((t, d), BF16), jax.ShapeDtypeStruct((t, XBC), F32),
                   jax.ShapeDtypeStruct((t, XBC), BF16), jax.ShapeDtypeStruct((t, LANES), F32),
                   jax.ShapeDtypeStruct((t, LANES), F32)],
        scratch_shapes=[pltpu.VMEM((8, XBC), F32)], compiler_params=_cp(("arbitrary",)),
    )(x, g, wt, wt_dt, w8, b, dt_bias)


PRE_HALO = 16


def _ssm_pre_bwd(zx, conv, dxbc, w8, name):
    t = zx.shape[0]
    tm, tc, hr = PRE_TM, PRE_TC, PRE_HALO
    off = D_INNER // tc
    nt = t // tm
    n = tm + hr

    def body(x_ref, c_ref, cn_ref, d_ref, dn_ref, w_ref, o_ref, dw_ref, db_ref):
        i = pl.program_id(1)
        last = i == nt - 1

        @pl.when(i == 0)
        def _():
            dw_ref[...] = jnp.zeros_like(dw_ref)
            db_ref[...] = jnp.zeros_like(db_ref)

        _, dact = _silu_grad(jnp.concatenate([c_ref[...], cn_ref[...]], axis=0).astype(F32))
        row = lax.broadcasted_iota(jnp.int32, (n, 1), 0)
        dc = jnp.where((row < tm) | jnp.logical_not(last), jnp.concatenate([d_ref[...], dn_ref[...]], axis=0) * dact, 0.0)
        nxt = [dc[:tm]] + [pltpu.roll(dc, n - s, 0)[:tm] for s in (1, 2, 3)]
        o_ref[...] = _conv_apply(nxt, w_ref, 4).astype(BF16)
        xv = x_ref[...]
        db_ref[...] += jnp.sum(nxt[0], axis=0, keepdims=True)
        for s in range(4):
            dw_ref[3 - s:4 - s, :] += jnp.sum(nxt[s] * xv, axis=0, keepdims=True)

    nxt_blk = lambda i: jnp.minimum((i + 1) * (tm // hr), t // hr - 1)
    tile = pl.BlockSpec((tm, tc), lambda j, i: (i, j))
    halo = pl.BlockSpec((hr, tc), lambda j, i: (nxt_blk(i), j))
    return pl.pallas_call(
        body, name=name, grid=(XBC // tc, nt),
        in_specs=[pl.BlockSpec((tm, tc), lambda j, i: (i, j + off)), tile, halo, tile, halo,
                  pl.BlockSpec((8, tc), lambda j, i: (0, j))],
        out_specs=[tile, pl.BlockSpec((8, tc), lambda j, i: (0, j)), pl.BlockSpec((1, tc), lambda j, i: (0, j))],
        out_shape=[jax.ShapeDtypeStruct((t, XBC), BF16), jax.ShapeDtypeStruct((8, XBC), F32),
                   jax.ShapeDtypeStruct((1, XBC), F32)],
        compiler_params=_cp(("parallel", "arbitrary")),
    )(zx, conv, conv, dxbc, dxbc, w8)


def _head_lanes():
    return lax.broadcasted_iota(jnp.int32, (1, LANES), 1) < SSM_HEADS


def _dt_bwd(dtraw, bias, ddt, name):
    t = dtraw.shape[0]
    tm = _pick(t, (1024, 512, 256, 128))

    def body(x_ref, b_ref, d_ref, o_ref, db_ref):
        @pl.when(pl.program_id(0) == 0)
        def _():
            db_ref[...] = jnp.zeros_like(db_ref)

        g = jnp.where(_head_lanes(), d_ref[...] * _sigmoid(x_ref[...] + b_ref[...]), 0.0)
        o_ref[...] = g.astype(BF16)
        db_ref[...] += jnp.sum(g, axis=0, keepdims=True)

    row = pl.BlockSpec((tm, LANES), lambda i: (i, 0))
    vec = pl.BlockSpec((1, LANES), lambda i: (0, 0))
    return pl.pallas_call(
        body, name=name, grid=(t // tm,), in_specs=[row, vec, row], out_specs=[row, vec],
        out_shape=[jax.ShapeDtypeStruct((t, LANES), BF16), jax.ShapeDtypeStruct((1, LANES), F32)],
        compiler_params=_cp(("arbitrary",)),
    )(dtraw, bias, ddt)


GROUP_W = D_INNER // SSM_GROUPS


def _ssd_common(dt, alog):
    ll = dt.shape[0]
    a_neg = -jnp.exp(alog)
    a = dt * a_neg
    ri = lax.broadcasted_iota(jnp.int32, (ll, ll), 0)
    ci = lax.broadcasted_iota(jnp.int32, (ll, ll), 1)
    tril = ri >= ci
    acs = _dot(tril.astype(F32), a, 1, 0, HI)
    return a_neg, tril, acs, acs.T


def _pair_terms(acs, acs_t, dt, h0, lo):
    ll = acs.shape[0]
    cols = [acs[:, h0 + e:h0 + e + 1] for e in range(2)]
    rows = [acs_t[h0 + e:h0 + e + 1, :] for e in range(2)]
    dtc = [dt[:, h0 + e:h0 + e + 1] for e in range(2)]
    lasts = [c[ll - 1:ll, :] for c in cols]
    dtx = jnp.where(lo, dtc[0], dtc[1])
    eac = jnp.where(lo, jnp.exp(cols[0]), jnp.exp(cols[1]))
    fdec = jnp.where(lo, jnp.exp(lasts[0] - cols[0]), jnp.exp(lasts[1] - cols[1]))
    elast = jnp.where(lo, jnp.exp(lasts[0]), jnp.exp(lasts[1]))
    return cols, rows, dtx, eac, fdec, elast


def _decay(col, row, tril):
    return jnp.where(tril, jnp.exp(jnp.minimum(col - row, 0.0)), 0.0)


def _two_heads_rows(v, lo):
    z = jnp.zeros_like(v)
    return jnp.concatenate([jnp.where(lo, v, z), jnp.where(lo, z, v)], axis=0)


def _two_heads_cols(ms):
    return jnp.concatenate(ms, axis=1)


def _z_group(z_refs, g):
    return z_refs[g // 2][:, GROUP_W * (g % 2):GROUP_W * (g % 2 + 1)]


def _ssd_fwd(xbc, dt, alog, zx, dexp, nw, name, comm=None):
    t = xbc.shape[0]
    ll = SSD_L
    nc = t // ll

    def body(x_ref, dt_ref, al_ref, z0_ref, z1_ref, d_ref, w_ref, y_ref, sp_ref, y4_ref, st_ref):
        @pl.when(pl.program_id(0) == 0)
        def _():
            st_ref[...] = jnp.zeros_like(st_ref)

        dtv = dt_ref[...]
        _, tril, acs, acs_t = _ssd_common(dtv, al_ref[...])
        lo = _lo_mask()
        sp_ref[0] = st_ref[...]
        for g in range(SSM_GROUPS):
            bg = x_ref[:, D_INNER + SSM_STATE * g:D_INNER + SSM_STATE * (g + 1)].astype(BF16)
            cg = x_ref[:, D_INNER + 512 + SSM_STATE * g:D_INNER + 512 + SSM_STATE * (g + 1)].astype(BF16)
            gm = _dot(cg, bg, 1, 1)
            g0 = GROUP_W * g
            terms = [_pair_terms(acs, acs_t, dtv, 8 * g + 2 * pp, lo) for pp in range(4)]
            dtx, eac, fdec, elast = [jnp.concatenate([tt[k] for tt in terms], axis=1) for k in (2, 3, 4, 5)]
            xg = x_ref[:, g0:g0 + GROUP_W]
            ug = (xg * dtx).astype(BF16)
            sg = st_ref[:, g0:g0 + GROUP_W]
            yst = _dot(cg, sg.astype(BF16), 1, 0) * eac
            st_ref[:, g0:g0 + GROUP_W] = sg * elast + _dot(bg, (xg * (fdec * dtx)).astype(BF16), 0, 0)
            ys = []
            for pp in range(4):
                cols, rows = terms[pp][0], terms[pp][1]
                sl = slice(LANES * pp, LANES * (pp + 1))
                y_in = _dot(_two_heads_cols([(gm * _decay(cols[e], rows[e], tril)).astype(BF16) for e in range(2)]),
                            _two_heads_rows(ug[:, sl], lo), 1, 0)
                ys.append(y_in + yst[:, sl])
            yg = jnp.concatenate(ys, axis=1)
            y_ref[:, g0:g0 + GROUP_W] = yg
            zg = _z_group((z0_ref, z1_ref), g)
            y3 = (yg + d_ref[:, g0:g0 + GROUP_W] * xg) * (zg * _sigmoid(zg))
            r = lax.rsqrt(jnp.mean(y3 * y3, axis=-1, keepdims=True) + EPS)
            y4_ref[:, g0:g0 + GROUP_W] = (y3 * r * w_ref[:, g0:g0 + GROUP_W]).astype(BF16)

    zblk = lambda j: pl.BlockSpec((ll, 1024), lambda c: (c, j))
    vec = pl.BlockSpec((1, D_INNER), lambda c: (0, 0))
    row = pl.BlockSpec((ll, D_INNER), lambda c: (c, 0))
    return _call(
        body, name=name, grid=(nc,),
        in_specs=[pl.BlockSpec((ll, XBC), lambda c: (c, 0)), pl.BlockSpec((ll, LANES), lambda c: (c, 0)),
                  pl.BlockSpec((1, LANES), lambda c: (0, 0)), zblk(0), zblk(1), vec, vec],
        out_specs=[row, pl.BlockSpec((1, SSM_STATE, D_INNER), lambda c: (c, 0, 0)), row],
        out_shape=[jax.ShapeDtypeStruct((t, D_INNER), F32), jax.ShapeDtypeStruct((nc, SSM_STATE, D_INNER), F32),
                   jax.ShapeDtypeStruct((t, D_INNER), BF16)],
        scratch_shapes=[pltpu.VMEM((SSM_STATE, D_INNER), F32)],
        args=(xbc, dt, alog, zx, zx, dexp, nw), sem=("arbitrary",), comm=comm)


def _ssd_bwd(xbc, dt, alog, sprev, dy4, y, zx, dexp, nw, name, comm=None):
    t = xbc.shape[0]
    ll = SSD_L_BWD if t % SSD_L_BWD == 0 else SSD_L
    nc = t // ll
    every = ll // SSD_L

    def body(x_ref, dt_ref, al_ref, sp_ref, g4_ref, y_ref, z0_ref, z1_ref, d_ref, w_ref,
             dx_ref, ddt_ref, dal_ref, dz_ref, dd_ref, dnw_ref, ds_ref, colt_ref):
        @pl.when(pl.program_id(0) == 0)
        def _():
            ds_ref[...] = jnp.zeros_like(ds_ref)
            dal_ref[...] = jnp.zeros_like(dal_ref)
            dd_ref[...] = jnp.zeros_like(dd_ref)
            dnw_ref[...] = jnp.zeros_like(dnw_ref)

        dtv = dt_ref[...]
        a_neg, tril, acs, acs_t = _ssd_common(dtv, al_ref[...])
        lo = _lo_mask()
        hi = jnp.logical_not(lo)
        lane = lax.broadcasted_iota(jnp.int32, (1, LANES), 1)
        colt_ref[...] = jnp.zeros_like(colt_ref)
        rowterm = jnp.zeros((ll, LANES), F32)
        ddt_u = jnp.zeros((ll, LANES), F32)
        dlast = jnp.zeros((1, LANES), F32)

        def halves(v):
            return (jnp.sum(jnp.where(lo, v, 0.0), axis=-1, keepdims=True),
                    jnp.sum(jnp.where(hi, v, 0.0), axis=-1, keepdims=True))

        for g in range(SSM_GROUPS):
            cb0 = D_INNER + SSM_STATE * g
            cc0 = D_INNER + 512 + SSM_STATE * g
            bg = x_ref[:, cb0:cb0 + SSM_STATE].astype(BF16)
            cg = x_ref[:, cc0:cc0 + SSM_STATE].astype(BF16)
            gm = _dot(cg, bg, 1, 1)
            g0 = GROUP_W * g
            terms = [_pair_terms(acs, acs_t, dtv, 8 * g + 2 * pp, lo) for pp in range(4)]
            dtx, eac, fdec, elast = [jnp.concatenate([tt[k] for tt in terms], axis=1) for k in (2, 3, 4, 5)]
            xg = x_ref[:, g0:g0 + GROUP_W]
            u32 = xg * dtx
            ug = u32.astype(BF16)
            zg = _z_group((z0_ref, z1_ref), g)
            dg = d_ref[:, g0:g0 + GROUP_W]
            act, dact = _silu_grad(zg)
            y2 = y_ref[:, g0:g0 + GROUP_W] + dg * xg
            y3 = y2 * act
            rn = lax.rsqrt(jnp.mean(y3 * y3, axis=-1, keepdims=True) + EPS)
            y3n = y3 * rn
            gv = g4_ref[:, g0:g0 + GROUP_W]
            dyn = gv * w_ref[:, g0:g0 + GROUP_W]
            dy3 = rn * (dyn - y3n * jnp.mean(dyn * y3n, axis=-1, keepdims=True))
            dyg = dy3 * act
            dskip = dyg * dg
            dz_ref[:, g0:g0 + GROUP_W] = (dy3 * y2 * dact).astype(BF16)
            dd_ref[:, g0:g0 + GROUP_W] += jnp.sum(dyg * xg, axis=0, keepdims=True)
            dnw_ref[:, g0:g0 + GROUP_W] += jnp.sum(gv * y3n, axis=0, keepdims=True)
            dyb = dyg.astype(BF16)
            spg = sp_ref[0, :, g0:g0 + GROUP_W]
            spb = spg.astype(BF16)
            dsg = ds_ref[:, g0:g0 + GROUP_W]
            dsb = dsg.astype(BF16)
            du_st = _dot(bg, dsb, 1, 0) * fdec
            yst = _dot(cg, spb, 1, 0) * eac
            dye = (dyg * eac).astype(BF16)
            dc_st = _dot(dye, spb, 1, 1)
            db_st = _dot((xg * (fdec * dtx)).astype(BF16), dsb, 1, 1)
            ds_ref[:, g0:g0 + GROUP_W] = dsg * elast + _dot(cg, dye, 0, 0)
            qst_el = du_st * u32
            rq_el = dyg * yst - qst_el
            q_row = jnp.sum(qst_el, axis=0, keepdims=True)
            s_row = jnp.sum(dsg * spg, axis=0, keepdims=True)
            dgm = jnp.zeros((ll, ll), F32)
            for pp in range(4):
                h0 = 8 * g + 2 * pp
                cols, rows = terms[pp][0], terms[pp][1]
                sl = slice(LANES * pp, LANES * (pp + 1))
                decs = [_decay(cols[e], rows[e], tril) for e in range(2)]
                wms = [gm * d for d in decs]
                dum2 = _dot(dyb[:, sl], _two_heads_rows(ug[:, sl], lo), 1, 1)
                du = _dot(jnp.concatenate([wm.astype(BF16) for wm in wms], axis=0),
                          _two_heads_rows(dyb[:, sl], lo), 0, 0) + du_st[:, sl]
                dx_ref[:, g0 + LANES * pp:g0 + LANES * (pp + 1)] = du * dtx[:, sl] + dskip[:, sl]
                ddtu = halves(du * xg[:, sl])
                rq = halves(rq_el[:, sl])
                qs = halves(q_row[:, sl])
                ss = halves(s_row[:, sl])
                for e in range(2):
                    dum = dum2[:, ll * e:ll * (e + 1)]
                    dgm = dgm + dum * decs[e]
                    tm_ = dum * wms[e]
                    oh = lane == (h0 + e)
                    rowterm = rowterm + jnp.where(oh, jnp.sum(tm_, axis=1, keepdims=True) + rq[e], 0.0)
                    ddt_u = ddt_u + jnp.where(oh, ddtu[e], 0.0)
                    dlast = dlast + jnp.where(oh, jnp.exp(cols[e][ll - 1:ll, :]) * ss[e] + qs[e], 0.0)
                    colt_ref[h0 + e:h0 + e + 1, :] = jnp.sum(tm_, axis=0, keepdims=True)
            dgb = dgm.astype(BF16)
            dx_ref[:, cc0:cc0 + SSM_STATE] = _dot(dgb, bg, 1, 0) + dc_st
            dx_ref[:, cb0:cb0 + SSM_STATE] = _dot(dgb, cg, 0, 0) + db_st
        row_io = lax.broadcasted_iota(jnp.int32, (ll, LANES), 0)
        dacs = rowterm - colt_ref[...].T + jnp.where(row_io == ll - 1, dlast, 0.0)
        da = _dot(jnp.logical_not(tril).astype(F32) + jnp.where(
            lax.broadcasted_iota(jnp.int32, (ll, ll), 0) == lax.broadcasted_iota(jnp.int32, (ll, ll), 1), 1.0, 0.0),
            dacs, 1, 0, HI)
        ddt_ref[...] = da * a_neg + ddt_u
        dal_ref[...] += jnp.sum(da * dtv, axis=0, keepdims=True) * a_neg

    rev = lambda c: nc - 1 - c
    row = pl.BlockSpec((ll, D_INNER), lambda c: (rev(c), 0))
    vec = pl.BlockSpec((1, D_INNER), lambda c: (0, 0))
    zblk = lambda j: pl.BlockSpec((ll, 1024), lambda c: (rev(c), j))
    return _call(
        body, name=name, grid=(nc,),
        in_specs=[pl.BlockSpec((ll, XBC), lambda c: (rev(c), 0)), pl.BlockSpec((ll, LANES), lambda c: (rev(c), 0)),
                  pl.BlockSpec((1, LANES), lambda c: (0, 0)),
                  pl.BlockSpec((1, SSM_STATE, D_INNER), lambda c: (rev(c) * every, 0, 0)), row, row, zblk(0), zblk(1), vec, vec],
        out_specs=[pl.BlockSpec((ll, XBC), lambda c: (rev(c), 0)), pl.BlockSpec((ll, LANES), lambda c: (rev(c), 0)),
                   pl.BlockSpec((1, LANES), lambda c: (0, 0)), row, vec, vec],
        out_shape=[jax.ShapeDtypeStruct((t, XBC), F32), jax.ShapeDtypeStruct((t, LANES), F32),
                   jax.ShapeDtypeStruct((1, LANES), F32), jax.ShapeDtypeStruct((t, D_INNER), BF16),
                   jax.ShapeDtypeStruct((1, D_INNER), F32), jax.ShapeDtypeStruct((1, D_INNER), F32)],
        scratch_shapes=[pltpu.VMEM((SSM_STATE, D_INNER), F32), pltpu.VMEM((LANES, ll), F32)],
        args=(xbc, dt, alog, sprev, dy4, y, zx, zx, dexp, nw), sem=("arbitrary",), comm=comm)


def _sum_parts(parts, name):
    nparts, r, c = parts.shape
    tc = _pick(c, (256, 128))

    def body(p_ref, o_ref):
        g = p_ref[0].astype(F32)
        for k in range(1, nparts):
            g = g + p_ref[k].astype(F32)
        o_ref[...] = g

    return pl.pallas_call(
        body, name=name, grid=(c // tc,), in_specs=[pl.BlockSpec((nparts, r, tc), lambda j: (0, 0, j))],
        out_specs=pl.BlockSpec((r, tc), lambda j: (0, j)), out_shape=jax.ShapeDtypeStruct((r, c), F32),
        compiler_params=_cp(("parallel",)),
    )(parts)


def _adamw(parts, w, m, v, name):
    nl, r, c = w.shape
    assert len(parts) == nl
    tr = _pick(r, (256, 128, 64))
    c1 = 1.0 - ADAM_B1 ** ADAM_STEP
    c2 = 1.0 - ADAM_B2 ** ADAM_STEP

    def body(*refs):
        p_refs = refs[:nl]
        w_ref, m_ref, v_ref, g_ref, d_ref, mo_ref, vo_ref = refs[nl:]
        g = None
        for l, p_ref in enumerate(p_refs):
            s = p_ref[0].astype(F32)
            for k in range(1, p_ref.shape[0]):
                s = s + p_ref[k].astype(F32)
            g = s if g is None else jnp.where(pl.program_id(0) == l, s, g)
        mn = ADAM_B1 * m_ref[0] + (1.0 - ADAM_B1) * g
        vn = ADAM_B2 * v_ref[0] + (1.0 - ADAM_B2) * (g * g)
        g_ref[0] = g
        mo_ref[0] = mn
        vo_ref[0] = vn
        d_ref[0] = -ADAM_LR * ((mn / c1) / (jnp.sqrt(vn / c2) + ADAM_EPS) + ADAM_WD * w_ref[0])

    row = pl.BlockSpec((1, tr, c), lambda l, i: (l, i, 0))
    sd = jax.ShapeDtypeStruct((nl, r, c), F32)
    return pl.pallas_call(
        body, name=name, grid=(nl, r // tr),
        in_specs=[pl.BlockSpec((p.shape[0], tr, c), lambda l, i: (0, i, 0)) for p in parts] + [row, row, row],
        out_specs=[row, row, row, row], out_shape=[sd, sd, sd, sd], compiler_params=_cp(("parallel", "parallel")),
    )(*parts, w, m, v)


def _peers():
    mx, my, mc = lax.axis_index("x"), lax.axis_index("y"), lax.axis_index("c")
    me = 4 * mx + 2 * my + mc
    out = []
    for k in range(1, N_DEV):
        px = 1 - mx if k & 4 else mx
        py = 1 - my if k & 2 else my
        pc = 1 - mc if k & 1 else mc
        out.append(((px, py, pc), 4 * px + 2 * py + pc))
    return me, out


class _Comm:
    def __init__(self, arrs, scatters):
        self.arrs, self.scatters, self.n = list(arrs), list(scatters), len(arrs)
        self.specs = [pl.BlockSpec(memory_space=pl.ANY)] * self.n
        self.out_shape = [jax.ShapeDtypeStruct(x.shape if sc else (N_DEV,) + x.shape, x.dtype)
                          for x, sc in zip(self.arrs, self.scatters)]
        np_ = N_DEV - 1
        self.scratch = [pltpu.SemaphoreType.DMA((np_ * self.n,)), pltpu.SemaphoreType.DMA((np_ * self.n,)),
                        pltpu.SemaphoreType.DMA((self.n,))]

    def _copies(self, x_refs, o_refs, sems):
        send_sems, recv_sems, local_sems = sems
        me, peers = _peers()
        np_ = N_DEV - 1
        local, sends, recvs = [], [], []
        for a in range(self.n):
            mine = x_refs[a].at[me] if self.scatters[a] else x_refs[a]
            local.append(pltpu.make_async_copy(mine, o_refs[a].at[me], local_sems.at[a]))
        for k, (dev, idx) in enumerate(peers):
            for a in range(self.n):
                mine = x_refs[a].at[me] if self.scatters[a] else x_refs[a]
                sends.append(pltpu.make_async_remote_copy(
                    src_ref=x_refs[a].at[idx] if self.scatters[a] else x_refs[a], dst_ref=o_refs[a].at[me],
                    send_sem=send_sems.at[a * np_ + k], recv_sem=recv_sems.at[a * np_ + k], device_id=dev, device_id_type=MESH))
                recvs.append(pltpu.make_async_remote_copy(
                    src_ref=mine, dst_ref=o_refs[a].at[idx], send_sem=send_sems.at[a * np_ + k],
                    recv_sem=recv_sems.at[a * np_ + k], device_id=dev, device_id_type=MESH))
        return local, sends, recvs

    def start(self, x_refs, o_refs, sems):
        local, sends, _ = self._copies(x_refs, o_refs, sems)
        for cp in local + sends:
            cp.start()

    def wait(self, x_refs, o_refs, sems):
        local, sends, recvs = self._copies(x_refs, o_refs, sems)
        for cp in recvs:
            cp.wait_recv()
        for cp in sends:
            cp.wait_send()
        for cp in local:
            cp.wait()


class _Gather2(_Comm):
    def __init__(self, arrs):
        super().__init__(arrs, [False] * len(arrs))

    def _plan(self, x_refs, o_refs, sems):
        send_sems, recv_sems, local_sems = sems
        mx, my, mc = lax.axis_index("x"), lax.axis_index("y"), lax.axis_index("c")
        slot = lambda px, py, pc: 4 * px + 2 * py + pc
        sib = (mx, my, 1 - mc)
        chips = [(1 - mx, my), (mx, 1 - my), (1 - mx, 1 - my)]
        np_ = N_DEV - 1
        local, first, passed, arrive_first, arrive_rest = [], [], [], [], []

        def copy(a, k, src, block, to):
            return pltpu.make_async_remote_copy(
                src_ref=src, dst_ref=o_refs[a].at[block], send_sem=send_sems.at[a * np_ + k], recv_sem=recv_sems.at[a * np_ + k],
                device_id=to, device_id_type=MESH)

        for a in range(self.n):
            me = slot(mx, my, mc)
            local.append(pltpu.make_async_copy(x_refs[a], o_refs[a].at[me], local_sems.at[a]))
            first.append(copy(a, 0, x_refs[a], me, sib))
            arrive_rest.append(copy(a, 0, x_refs[a], slot(*sib), sib))
            for j, (cx, cy) in enumerate(chips):
                first.append(copy(a, 1 + j, x_refs[a], me, (cx, cy, mc)))
                arrive_first.append(copy(a, 1 + j, x_refs[a], slot(cx, cy, mc), (cx, cy, mc)))
                passed.append(copy(a, 4 + j, o_refs[a].at[slot(cx, cy, mc)], slot(cx, cy, mc), sib))
                arrive_rest.append(copy(a, 4 + j, x_refs[a], slot(cx, cy, 1 - mc), sib))
        return local, first, passed, arrive_first, arrive_rest

    def start(self, x_refs, o_refs, sems):
        local, first, _, _, _ = self._plan(x_refs, o_refs, sems)
        for cp in local + first:
            cp.start()

    def wait(self, x_refs, o_refs, sems):
        local, first, passed, arrive_first, arrive_rest = self._plan(x_refs, o_refs, sems)
        for arrived, onward in zip(arrive_first, passed):
            arrived.wait_recv()
            onward.start()
        for cp in arrive_rest:
            cp.wait_recv()
        for cp in first + passed:
            cp.wait_send()
        for cp in local:
            cp.wait()


def _call(body, *, name, grid, in_specs, out_specs, out_shape, args, scratch_shapes=(), sem=None, comm=None):
    if comm is None:
        outs = pl.pallas_call(
            body, name=name, grid=grid, in_specs=list(in_specs), out_specs=list(out_specs), out_shape=list(out_shape),
            scratch_shapes=list(scratch_shapes), compiler_params=_cp(sem),
        )(*args)
        return list(outs), []
    n_in, n_out, nc = len(in_specs), len(out_specs), comm.n
    nsteps = 1
    for g in grid:
        nsteps *= g

    def carrier(*refs):
        ins, cin = refs[:n_in], refs[n_in:n_in + nc]
        outs, cout = refs[n_in + nc:n_in + nc + n_out], refs[n_in + nc + n_out:n_in + 2 * nc + n_out]
        rest = refs[n_in + 2 * nc + n_out:]
        scratch, sems = rest[:len(rest) - 3], rest[len(rest) - 3:]
        if nsteps == 1:
            comm.start(cin, cout, sems)
            body(*ins, *outs, *scratch)
            comm.wait(cin, cout, sems)
            return
        step = 0
        for d, g in enumerate(grid):
            step = step * g + pl.program_id(d)

        @pl.when(step == 0)
        def _():
            comm.start(cin, cout, sems)

        body(*ins, *outs, *scratch)

        @pl.when(step == nsteps - 1)
        def _():
            comm.wait(cin, cout, sems)

    outs = pl.pallas_call(
        carrier, name=name, grid=grid, in_specs=list(in_specs) + comm.specs, out_specs=list(out_specs) + comm.specs,
        out_shape=list(out_shape) + comm.out_shape, scratch_shapes=list(scratch_shapes) + comm.scratch,
        compiler_params=_cp(("arbitrary",) * len(grid) if grid else None),
    )(*args, *comm.arrs)
    return list(outs[:n_out]), list(outs[n_out:])


def _exchange(comm, name):
    return _call(lambda *refs: None, name=name, grid=(), in_specs=[], out_specs=[], out_shape=[], args=[], comm=comm)[1]


def _pack(arrs, dtype, lead=()):
    nl = len(lead)
    flat = jnp.concatenate([a.astype(dtype).reshape(lead + (-1,)) for a in arrs], axis=nl)
    n = flat.shape[-1]
    rows = -(-n // (LANES * 8)) * 8
    flat = jnp.pad(flat, [(0, 0)] * nl + [(0, rows * LANES - n)])
    return flat.reshape(lead + (rows, LANES))


def _unpack(flat, shapes, lead=()):
    nl = len(lead)
    flat = flat.reshape(lead + (-1,))
    out, o = [], 0
    for s in shapes:
        n = 1
        for d in s:
            n *= d
        out.append(lax.slice_in_dim(flat, o, o + n, axis=nl).reshape(lead + tuple(s)))
        o += n
    return out


def _join(g, ax):
    return jnp.concatenate([g[d] for d in range(N_DEV)], axis=ax)


def _split(full, ax):
    n = full.shape[ax] // N_DEV
    return jnp.stack([lax.slice_in_dim(full, d * n, (d + 1) * n, axis=ax) for d in range(N_DEV)])


_WEIGHTS = ['norm_mix', 'norm_ffn', 'attn_w_in', 'attn_w_out', 'relpos_table', 'q_norm_a', 'k_norm_a', 'q_norm_b',
            'k_norm_b', 'sinks', 'ssm_w_in', 'ssm_conv_w', 'ssm_conv_b', 'ssm_dt_bias', 'ssm_a_log', 'ssm_d', 'ssm_norm',
            'ssm_w_out', 'ffn_w_in', 'ffn_conv_w', 'ffn_conv_b', 'ffn_w_out']
_SHARD_AX = {'attn_w_in': 2, 'attn_w_out': 1, 'ssm_w_in': 2, 'ssm_conv_w': 2, 'ssm_conv_b': 1, 'ssm_norm': 1,
             'ssm_w_out': 1, 'ffn_w_in': 2, 'ffn_conv_w': 2, 'ffn_w_out': 1}
_BIG = ['attn_w_in', 'attn_w_out', 'ssm_w_in', 'ssm_w_out', 'ffn_w_in', 'ffn_w_out']
_SMALL = ['ssm_conv_w', 'ssm_conv_b', 'ssm_norm', 'ffn_conv_w']
_AX2 = {n: _SHARD_AX[n] - 1 for n in _BIG}
_REPL = [n for n in _WEIGHTS if n not in _SHARD_AX]


def _rows8(w):
    return jnp.pad(w, ((0, 8 - w.shape[0]), (0, 0)))


def _lanes128(v):
    return jnp.pad(v, (0, LANES - v.shape[0])).reshape(1, LANES)


def _band_mask(n_prev, pad):
    cq = jnp.arange(TQ)[:, None] // CHUNK
    ck = jnp.arange(pad + TQ)[None, :] // CHUNK
    return (ck >= cq) & (ck <= cq + n_prev)


def _ffn_fwd(xin, g, w_in_t, w8, cb, tag):
    gu, h, a, gc = _ffn_in_mid(xin, g, w_in_t, w8, cb, f"mm_ffn_in{tag}")
    return a, (h, gu, a, gc)


def _ffn_bwd(dx, dxb, xin, g, w_in_t, w8, w_out, saved, tag, scatter_of=None, also=()):
    h, gu, a, gc = saved
    dw_out = _mm_tn(a, dxb, f"mm_ffn_dwout{tag}")
    (dgu, dw8, dcb), got_out = _ffn_mid_bwd(gu, gc, dxb, w_out, w8, f"ffn_mid_bwd{tag}",
                                            comm=scatter_of([dw_out, *also]) if scatter_of else None)
    dw_in_t = _mm_tn(dgu, h, f"mm_ffn_dwin{tag}")
    if scatter_of is None:
        dxp, dxpb, dg = _mm_rms_bwd([(dgu, w_in_t, 0)], xin, g, dx, f"mm_ffn_dh{tag}")
        return dxp, dxpb, dg, dw_in_t, dw8[:3], dcb, dw_out
    dxp, dxpb, dg, got_in = _mm_rms_bwd([(dgu, w_in_t, 0)], xin, g, dx, f"mm_ffn_dh{tag}", comm=scatter_of([dw_in_t]))
    return dxp, dxpb, dg, got_in[0], dw8[:3], dcb, got_out


def kernel(x, norm_mix, norm_ffn, attn_w_in, attn_w_out, relpos_table, q_norm_a, k_norm_a, q_norm_b, k_norm_b, sinks, ssm_w_in, ssm_conv_w, ssm_conv_b, ssm_dt_bias, ssm_a_log, ssm_d, ssm_norm, ssm_w_out, ffn_w_in, ffn_conv_w, ffn_conv_b, ffn_w_out, loss_target, m_norm_mix, m_norm_ffn, m_attn_w_in, m_attn_w_out, m_relpos_table, m_q_norm_a, m_k_norm_a, m_q_norm_b, m_k_norm_b, m_sinks, m_ssm_w_in, m_ssm_conv_w, m_ssm_conv_b, m_ssm_dt_bias, m_ssm_a_log, m_ssm_d, m_ssm_norm, m_ssm_w_out, m_ffn_w_in, m_ffn_conv_w, m_ffn_conv_b, m_ffn_w_out, v_norm_mix, v_norm_ffn, v_attn_w_in, v_attn_w_out, v_relpos_table, v_q_norm_a, v_k_norm_a, v_q_norm_b, v_k_norm_b, v_sinks, v_ssm_w_in, v_ssm_conv_w, v_ssm_conv_b, v_ssm_dt_bias, v_ssm_a_log, v_ssm_d, v_ssm_norm, v_ssm_w_out, v_ffn_w_in, v_ffn_conv_w, v_ffn_conv_b, v_ffn_w_out):
    w = dict(norm_mix=norm_mix, norm_ffn=norm_ffn, attn_w_in=attn_w_in, attn_w_out=attn_w_out, relpos_table=relpos_table,
             q_norm_a=q_norm_a, k_norm_a=k_norm_a, q_norm_b=q_norm_b, k_norm_b=k_norm_b, sinks=sinks, ssm_w_in=ssm_w_in,
             ssm_conv_w=ssm_conv_w, ssm_conv_b=ssm_conv_b, ssm_dt_bias=ssm_dt_bias, ssm_a_log=ssm_a_log, ssm_d=ssm_d,
             ssm_norm=ssm_norm, ssm_w_out=ssm_w_out, ffn_w_in=ffn_w_in, ffn_conv_w=ffn_conv_w, ffn_conv_b=ffn_conv_b,
             ffn_w_out=ffn_w_out)
    mom = dict(norm_mix=m_norm_mix, norm_ffn=m_norm_ffn, attn_w_in=m_attn_w_in, attn_w_out=m_attn_w_out,
               relpos_table=m_relpos_table, q_norm_a=m_q_norm_a, k_norm_a=m_k_norm_a, q_norm_b=m_q_norm_b,
               k_norm_b=m_k_norm_b, sinks=m_sinks, ssm_w_in=m_ssm_w_in, ssm_conv_w=m_ssm_conv_w, ssm_conv_b=m_ssm_conv_b,
               ssm_dt_bias=m_ssm_dt_bias, ssm_a_log=m_ssm_a_log, ssm_d=m_ssm_d, ssm_norm=m_ssm_norm, ssm_w_out=m_ssm_w_out,
               ffn_w_in=m_ffn_w_in, ffn_conv_w=m_ffn_conv_w, ffn_conv_b=m_ffn_conv_b, ffn_w_out=m_ffn_w_out)
    var = dict(norm_mix=v_norm_mix, norm_ffn=v_norm_ffn, attn_w_in=v_attn_w_in, attn_w_out=v_attn_w_out,
               relpos_table=v_relpos_table, q_norm_a=v_q_norm_a, k_norm_a=v_k_norm_a, q_norm_b=v_q_norm_b,
               k_norm_b=v_k_norm_b, sinks=v_sinks, ssm_w_in=v_ssm_w_in, ssm_conv_w=v_ssm_conv_w, ssm_conv_b=v_ssm_conv_b,
               ssm_dt_bias=v_ssm_dt_bias, ssm_a_log=v_ssm_a_log, ssm_d=v_ssm_d, ssm_norm=v_ssm_norm, ssm_w_out=v_ssm_w_out,
               ffn_w_in=v_ffn_w_in, ffn_conv_w=v_ffn_conv_w, ffn_conv_b=v_ffn_conv_b, ffn_w_out=v_ffn_w_out)

    def piece(n, l):
        return (w[n][l].T if _AX2[n] == 1 else w[n][l]).astype(BF16)

    def gather_of(names_layers):
        return _Gather2([piece(n, l) for n, l in names_layers])

    def joined(got):
        return [g.reshape(-1, D_MODEL) for g in got]

    first = [('attn_w_in', 0), ('attn_w_out', 0)]
    got = _exchange(_Gather2([piece(n, l) for n, l in first] + [_pack([w[n] for n in _SMALL], F32)]), "gather_attn")
    wt_attn_in, w_attn_out = joined(got[:2])
    full = {}
    for n, g in zip(_SMALL, _unpack(got[2], [w[n].shape for n in _SMALL], lead=(N_DEV,))):
        full[n] = _join(g, _SHARD_AX[n])
    ssm_cw8 = _rows8(full['ssm_conv_w'][0])
    ssm_cb = full['ssm_conv_b']
    ssm_nw = full['ssm_norm']
    ffn_cw8 = [_rows8(full['ffn_conv_w'][l]) for l in range(2)]
    ffn_cb = [ffn_conv_b[l:l + 1] for l in range(2)]

    x0 = x[0]
    target = loss_target[0]
    t = x0.shape[0]

    g_mix0, g_mix1 = norm_mix[0:1], norm_mix[1:2]
    g_ffn0, g_ffn1 = norm_ffn[0:1], norm_ffn[1:2]
    proj, h0 = _rms_mm(x0, g_mix0, wt_attn_in, ATTN_PROJ, "mm_attn_in", F32)
    hn_w = jnp.concatenate([jnp.tile(v, (1, 2)) for v in (q_norm_a, k_norm_a, q_norm_b, k_norm_b)], axis=0)
    qa, kpa, vpa, qb, kpb, vpb = _headnorm_fwd(proj, hn_w, "headnorm")
    table = jnp.pad(relpos_table[0], ((0, 0), (0, REL_W - (2 * MAX_REL + 1))))
    bias_a = jnp.where(_band_mask(A_PREV, PAD_A)[None], jnp.transpose(_relpos_fwd(table, "relpos_bias"), (1, 0, 2)), NEG)
    rel_b = jnp.arange(TQ)[:, None] - (jnp.arange(PAD_B + TQ)[None, :] - PAD_B)
    slopes = 2.0 ** (-8.0 * jnp.arange(1, N_HEADS + 1, dtype=F32) / N_HEADS)
    bias_b = jnp.where(_band_mask(B_PREV, PAD_B)[None], -slopes[:, None, None] * jnp.abs(rel_b).astype(F32)[None], NEG)
    no_sinks = jnp.full((N_HEADS,), NEG, F32)
    ffn0_w, ssm_w, ffn1_w = [('ffn_w_in', 0), ('ffn_w_out', 0)], [('ssm_w_in', 0), ('ssm_w_out', 0)], [('ffn_w_in', 1), ('ffn_w_out', 1)]
    oa, stats_a, got = _attn_fwd(qa, kpa, vpa, bias_a, no_sinks, PAD_A, "attn_a", comm=gather_of(ffn0_w + ssm_w))
    wt_ffn_in0, w_ffn_out0, wt_ssm_in, w_ssm_out = joined(got)
    ob, stats_b, _ = _attn_fwd(qb, kpb, vpb, bias_b, sinks[0], PAD_B, "attn_b")
    wt_ssm_dt = jnp.pad(wt_ssm_in[ZX:], ((0, LANES - SSM_HEADS), (0, 0)))
    x1 = _mm_pair(oa, ob, w_attn_out, x0, "mm_attn_out")
    a0, ffn0_saved = _ffn_fwd(x1, g_ffn0, wt_ffn_in0, ffn_cw8[0], ffn_cb[0], "0")
    x2 = _mm(a0, w_ffn_out0, "mm_ffn_out0", res=x1)

    dt_bias = _lanes128(ssm_dt_bias[0])
    alog = _lanes128(ssm_a_log[0])
    dexp = jnp.repeat(ssm_d[0], HEAD_DIM).reshape(1, D_INNER)
    zx, h2, xbc, conv_pre, dtraw, dt = _ssm_in_pre(x2, g_mix1, wt_ssm_in, wt_ssm_dt, ssm_cw8, ssm_cb, dt_bias, "mm_ssm_in")
    (y, sprev, y4), got = _ssd_fwd(xbc, dt, alog, zx, dexp, ssm_nw, "ssd_fwd", comm=gather_of(ffn1_w))
    wt_ffn_in1, w_ffn_out1 = joined(got)
    x3 = _mm(y4, w_ssm_out, "mm_ssm_out", res=x2)
    a1, ffn1_saved = _ffn_fwd(x3, g_ffn1, wt_ffn_in1, ffn_cw8[1], ffn_cb[1], "1")

    dx4, dx4b, sq = _mm_loss(a1, w_ffn_out1, x3, target, "mm_ffn_out1_loss")
    loss = lax.psum(0.5 * jnp.sum(sq) / D_MODEL, ("x", "y", "c"))

    grads = {}

    def scatter_of(grads_2d):
        return _Comm([g.reshape(N_DEV, -1, D_MODEL) for g in grads_2d], [True] * len(grads_2d))

    dx3, dx3b, dg_ffn1, dwtin1, dcw1, dcb1, dwout1 = _ffn_bwd(
        dx4, dx4b, x3, g_ffn1, wt_ffn_in1, ffn_cw8[1], w_ffn_out1, ffn1_saved, "1")

    dy4 = _mm(dx3b, w_ssm_out, "mm_ssm_dy", trans_b=True)
    dw_ssm_out = _mm_tn(y4, dx3b, "mm_ssm_dwout")
    (dxbc, ddt, dalog, dz, dd_lane, dnw), parts_ffn1 = _ssd_bwd(
        xbc, dt, alog, sprev, dy4, y, zx, dexp, ssm_nw, "ssd_bwd", comm=scatter_of([dwtin1, dwout1, dw_ssm_out]))
    dxr, dcw_s, dcb_s = _ssm_pre_bwd(zx, conv_pre, dxbc, ssm_cw8, "ssm_pre_bwd")
    ddtraw, ddtb = _dt_bwd(dtraw, dt_bias, ddt, "ssm_dt_bwd")
    dwt_ssm_in = jnp.concatenate([
        _mm_tn(dz, h2, "mm_ssm_dwin_z"), _mm_tn(dxr, h2, "mm_ssm_dwin_x"),
        _mm_tn(ddtraw, h2, "mm_ssm_dwin_dt")[:SSM_HEADS]], axis=0)
    dx2, dx2b, dg_mix1 = _mm_rms_bwd([(dz, wt_ssm_in, 0), (dxr, wt_ssm_in, D_INNER), (ddtraw, wt_ssm_dt, 0)],
                                     x2, g_mix1, dx3, "mm_ssm_dh")
    grads['ssm_conv_w'] = dcw_s[:4][None]
    grads['ssm_conv_b'] = dcb_s
    grads['ssm_norm'] = dnw
    grads['ssm_dt_bias'] = ddtb[:, :SSM_HEADS]
    grads['ssm_a_log'] = dalog[:, :SSM_HEADS]
    grads['ssm_d'] = jnp.sum(dd_lane.reshape(SSM_HEADS, HEAD_DIM), axis=1)[None]

    dx1, dx1b, dg_ffn0, parts_in0, dcw0, dcb0, (parts_out0, parts_ssm_in) = _ffn_bwd(
        dx2, dx2b, x1, g_ffn0, wt_ffn_in0, ffn_cw8[0], w_ffn_out0, ffn0_saved, "0", scatter_of=scatter_of, also=[dwt_ssm_in])
    parts_ffn0 = [parts_in0, parts_out0]
    grads['ffn_conv_w'] = jnp.stack([dcw0, dcw1])
    grads['ffn_conv_b'] = jnp.concatenate([dcb0, dcb1], axis=0)
    grads['norm_ffn'] = jnp.concatenate([dg_ffn0, dg_ffn1], axis=0)

    do = _mm(dx1b, w_attn_out, "mm_attn_do", out_dtype=BF16, trans_b=True)
    dw_attn_out = jnp.concatenate([_mm_tn(oa, dx1b, "mm_attn_dwout_a"), _mm_tn(ob, dx1b, "mm_attn_dwout_b")], axis=0)
    (dqa, dkpa, dvpa, dbias_a, _), parts_attn_out = _attn_bwd(
        qa, kpa, vpa, bias_a, no_sinks, do, stats_a, oa, 0, PAD_A, "attn_a_bwd", comm=scatter_of([dw_attn_out]))
    (dqb, dkpb, dvpb, _, dsink), _ = _attn_bwd(qb, kpb, vpb, bias_b, sinks[0], do, stats_b, ob, 4, PAD_B, "attn_b_bwd")
    grads['relpos_table'] = _relpos_bwd(jnp.transpose(dbias_a, (1, 0, 2)), "relpos_bwd")[None, :, :2 * MAX_REL + 1]
    grads['sinks'] = dsink[:, :2, 0].reshape(1, N_HEADS)
    dproj, dhn = _headnorm_bwd(proj, hn_w, dqa, dkpa, dvpa, dqb, dkpb, dvpb, "headnorm_bwd")
    dhn = dhn[:, :HEAD_DIM] + dhn[:, HEAD_DIM:]
    for k, n in enumerate(('q_norm_a', 'k_norm_a', 'q_norm_b', 'k_norm_b')):
        grads[n] = dhn[k:k + 1]
    dwt_attn_in = _mm_tn(dproj, h0, "mm_attn_dwin")
    dx0, _, dg_mix0, parts_attn_in = _mm_rms_bwd([(dproj, wt_attn_in, 0)], x0, g_mix0, dx1, "mm_attn_dh",
                                                 comm=scatter_of([dwt_attn_in]))
    grads['norm_mix'] = jnp.concatenate([dg_mix0, dg_mix1], axis=0)

    def summed_t(parts, name):
        return _sum_parts(parts, name).T[None]

    sm_shapes = [w[n].shape for n in _SMALL]
    rp_shapes = [w[n].shape for n in _REPL]
    recv = _exchange(_Comm(
        [_pack([_split(grads[n], _SHARD_AX[n]) for n in _SMALL], F32, lead=(N_DEV,)), _pack([grads[n] for n in _REPL], F32)],
        [True, False]), "exchange_small")
    big_parts = {
        'attn_w_in': [summed_t(parts_attn_in[0], "sum_attn_w_in")], 'attn_w_out': [parts_attn_out[0]],
        'ssm_w_in': [summed_t(parts_ssm_in, "sum_ssm_w_in")], 'ssm_w_out': [parts_ffn1[2]],
        'ffn_w_in': [summed_t(parts_ffn0[0], "sum_ffn_w_in0"), summed_t(parts_ffn1[0], "sum_ffn_w_in1")],
        'ffn_w_out': [parts_ffn0[1], parts_ffn1[1]],
    }
    res = [{}, {}, {}, {}]
    for n in _BIG:
        for kind, a in enumerate(_adamw(big_parts[n], w[n], mom[n], var[n], f"adamw_{n}")):
            res[kind][n] = a
    for names, shapes, parts in ((_SMALL, sm_shapes, recv[0]), (_REPL, rp_shapes, recv[1])):
        outs = _adamw([parts], _pack([w[n] for n in names], F32)[None], _pack([mom[n] for n in names], F32)[None],
                      _pack([var[n] for n in names], F32)[None], "adamw_" + ("small" if names is _SMALL else "replicated"))
        for kind, flat in enumerate(outs):
            for n, a in zip(names, _unpack(flat[0], shapes)):
                res[kind][n] = a
    return (loss, dx0[None], *[res[0][n] for n in _WEIGHTS], *[res[1][n] for n in _WEIGHTS],
            *[res[2][n] for n in _WEIGHTS], *[res[3][n] for n in _WEIGHTS])
```

```python
import jax
import jax.numpy as jnp
from jax import lax
from jax.experimental import pallas as pl
from jax.experimental.pallas import tpu as pltpu

F32 = jnp.float32
BF16 = jnp.bfloat16
HI = lax.Precision.HIGHEST
MESH = pl.DeviceIdType.MESH
NEG = -1e30

N_DEV = 8
D_MODEL = 1024
EPS = 1e-6
CHUNK = 64
HEAD_DIM = 64
N_HEADS = 8
A_PREV = 8
B_PREV = 2
MAX_REL = 256
TQ = 2 * CHUNK
ATT_SUB = 32
PAD_A = A_PREV * CHUNK
PAD_B = B_PREV * CHUNK
REL_W = PAD_A + TQ
D_ATT = N_HEADS * HEAD_DIM
COL_QA, COL_KA, COL_VA, COL_QB = 0, D_ATT, 2 * D_ATT, 3 * D_ATT
COL_KB, COL_VB = 4 * D_ATT, 4 * D_ATT + 2 * HEAD_DIM
ATTN_PROJ = COL_VB + 2 * HEAD_DIM
D_INNER = 2048
SSM_HEADS = 32
SSM_GROUPS = 4
SSM_STATE = 128
XBC = D_INNER + 2 * SSM_GROUPS * SSM_STATE
ZX = D_INNER + XBC
D_FF = 2816
SSD_L = 128
SSD_L_BWD = 2 * SSD_L
LANES = 128
VMEM_LIMIT = 56 << 20

ADAM_LR, ADAM_B1, ADAM_B2, ADAM_EPS, ADAM_WD, ADAM_STEP = 0.001, 0.9, 0.999, 1e-08, 0.01, 10


def _cp(sem=None):
    return pltpu.CompilerParams(dimension_semantics=sem, vmem_limit_bytes=VMEM_LIMIT)


def _dot(a, b, ca=1, cb=0, prec=None):
    return lax.dot_general(a, b, (((ca,), (cb,)), ((), ())), preferred_element_type=F32, precision=prec)


def _pick(n, cands):
    for c in cands:
        if n % c == 0:
            return c
    return n


def _lo_mask():
    return lax.broadcasted_iota(jnp.int32, (1, LANES), 1) < HEAD_DIM


_TN_CHUNKS = (1408, 1536, 1152, 1024, 512, 256, 128)


TN_MAX_ROWS = 3072
MM_WIDE = 2304


def _mm_tn(a, b, name):
    kdim, m = a.shape
    n = b.shape[1]
    assert b.shape[0] == kdim, (a.shape, b.shape)
    mb = m if m <= TN_MAX_ROWS else m // 2
    tn = _pick(n, _TN_CHUNKS)
    tk = _pick(kdim, (512, 256, 128))
    nk = kdim // tk

    def body(a_ref, b_ref, o_ref, acc):
        k = pl.program_id(1)

        @pl.when(k == 0)
        def _():
            acc[...] = jnp.zeros_like(acc)

        av = a_ref[...]
        for c in range(0, n, tn):
            acc[:, c:c + tn] += _dot(av, b_ref[:, c:c + tn], 0, 0)

        @pl.when(k == nk - 1)
        def _():
            o_ref[...] = acc[...].astype(BF16)

    return pl.pallas_call(
        body, name=name, grid=(m // mb, nk),
        in_specs=[pl.BlockSpec((tk, mb), lambda j, k: (k, j)), pl.BlockSpec((tk, n), lambda j, k: (k, 0))],
        out_specs=pl.BlockSpec((mb, n), lambda j, k: (j, 0)), out_shape=jax.ShapeDtypeStruct((m, n), BF16),
        scratch_shapes=[pltpu.VMEM((mb, n), F32)], compiler_params=_cp(("parallel", "arbitrary")),
    )(a, b)


def _mm(a, b, name, out_dtype=F32, res=None, trans_b=False):
    m, kdim = a.shape
    n = b.shape[0] if trans_b else b.shape[1]
    assert (b.shape[1] if trans_b else b.shape[0]) == kdim, (a.shape, b.shape)
    tn = _pick(n, _TN_CHUNKS)
    tm = _pick(m, (256, 128) if n > MM_WIDE else (512, 256, 128))

    def body(*refs):
        if res is None:
            a_ref, b_ref, o_ref = refs
        else:
            a_ref, b_ref, r_ref, o_ref = refs
        av = a_ref[...]
        for c in range(0, n, tn):
            r = _dot(av, b_ref[c:c + tn, :], 1, 1) if trans_b else _dot(av, b_ref[:, c:c + tn], 1, 0)
            if res is not None:
                r = r + r_ref[:, c:c + tn]
            o_ref[:, c:c + tn] = r.astype(out_dtype)

    in_specs = [pl.BlockSpec((tm, kdim), lambda i: (i, 0)), pl.BlockSpec(b.shape, lambda i: (0, 0))]
    args = [a, b]
    if res is not None:
        in_specs.append(pl.BlockSpec((tm, n), lambda i: (i, 0)))
        args.append(res)
    return pl.pallas_call(
        body, name=name, grid=(m // tm,), in_specs=in_specs, out_specs=pl.BlockSpec((tm, n), lambda i: (i, 0)),
        out_shape=jax.ShapeDtypeStruct((m, n), out_dtype), compiler_params=_cp(("parallel",)),
    )(*args)


def _mm_pair(a1, a2, b, res, name):
    m, k1 = a1.shape
    k2 = a2.shape[1]
    n = b.shape[1]
    assert b.shape[0] == k1 + k2
    tm = _pick(m, (512, 256, 128))

    def body(a1_ref, a2_ref, b_ref, r_ref, o_ref):
        o_ref[...] = _dot(a1_ref[...], b_ref[:k1, :], 1, 0) + _dot(a2_ref[...], b_ref[k1:, :], 1, 0) + r_ref[...]

    row = pl.BlockSpec((tm, n), lambda i: (i, 0))
    return pl.pallas_call(
        body, name=name, grid=(m // tm,),
        in_specs=[pl.BlockSpec((tm, k1), lambda i: (i, 0)), pl.BlockSpec((tm, k2), lambda i: (i, 0)),
                  pl.BlockSpec(b.shape, lambda i: (0, 0)), row],
        out_specs=row, out_shape=jax.ShapeDtypeStruct((m, n), F32), compiler_params=_cp(("parallel",)),
    )(a1, a2, b, res)


def _rms_mm(x, g, bt, n, name, out_dtype):
    t, d = x.shape
    tn = _pick(n, _TN_CHUNKS)
    tm = _pick(t, (256, 128))

    def body(x_ref, g_ref, b_ref, o_ref, h_ref):
        xv = x_ref[...]
        r = lax.rsqrt(jnp.mean(xv * xv, axis=-1, keepdims=True) + EPS)
        h = (xv * r * g_ref[...]).astype(BF16)
        h_ref[...] = h
        for c in range(0, n, tn):
            o_ref[:, c:c + tn] = _dot(h, b_ref[c:c + tn, :], 1, 1).astype(out_dtype)

    row = pl.BlockSpec((tm, d), lambda i: (i, 0))
    return pl.pallas_call(
        body, name=name, grid=(t // tm,),
        in_specs=[row, pl.BlockSpec((1, d), lambda i: (0, 0)), pl.BlockSpec(bt.shape, lambda i: (0, 0))],
        out_specs=[pl.BlockSpec((tm, n), lambda i: (i, 0)), row],
        out_shape=[jax.ShapeDtypeStruct((t, n), out_dtype), jax.ShapeDtypeStruct((t, d), BF16)],
        compiler_params=_cp(("parallel",)),
    )(x, g, bt)


def _mm_rms_bwd(terms, x, g, dres, name, comm=None):
    t, d = x.shape
    tm = _pick(t, (256, 128))
    weights = []
    for _, b, _ in terms:
        if not any(b is wgt for wgt in weights):
            weights.append(b)
    which = [next(k for k, wgt in enumerate(weights) if wgt is b) for _, b, _ in terms]
    na, nw = len(terms), len(weights)

    def body(*refs):
        a_refs, w_refs = refs[:na], refs[na:na + nw]
        x_ref, g_ref, dr_ref, dx_ref, dxb_ref, dg_ref = refs[na + nw:]
        dhv = None
        for (a, _, row), a_ref, k in zip(terms, a_refs, which):
            part = _dot(a_ref[...], w_refs[k][row:row + a.shape[1], :], 1, 0)
            dhv = part if dhv is None else dhv + part
        xv = x_ref[...]
        r = lax.rsqrt(jnp.mean(xv * xv, axis=-1, keepdims=True) + EPS)
        xh = xv * r
        dxh = dhv * g_ref[...]
        dx = dr_ref[...] + r * (dxh - xh * jnp.mean(dxh * xh, axis=-1, keepdims=True))
        dx_ref[...] = dx
        dxb_ref[...] = dx.astype(BF16)

        @pl.when(pl.program_id(0) == 0)
        def _():
            dg_ref[...] = jnp.zeros_like(dg_ref)

        dg_ref[...] += jnp.sum(dhv * xh, axis=0, keepdims=True)

    row = pl.BlockSpec((tm, d), lambda i: (i, 0))
    vec = pl.BlockSpec((1, d), lambda i: (0, 0))
    in_specs = ([pl.BlockSpec((tm, a.shape[1]), lambda i: (i, 0)) for a, _, _ in terms]
                + [pl.BlockSpec(wgt.shape, lambda i: (0, 0)) for wgt in weights])
    outs, got = _call(
        body, name=name, grid=(t // tm,), in_specs=in_specs + [row, vec, row], out_specs=[row, row, vec],
        out_shape=[jax.ShapeDtypeStruct((t, d), F32), jax.ShapeDtypeStruct((t, d), BF16), jax.ShapeDtypeStruct((1, d), F32)],
        args=(*[a for a, _, _ in terms], *weights, x, g, dres), sem=("arbitrary",), comm=comm)
    return (*outs, got) if comm is not None else tuple(outs)


def _mm_loss(a, b, res, target, name):
    t, kdim = a.shape
    d = b.shape[1]
    tm = _pick(t, (512, 256, 128))

    def body(a_ref, b_ref, r_ref, t_ref, dy_ref, dyb_ref, acc_ref):
        @pl.when(pl.program_id(0) == 0)
        def _():
            acc_ref[...] = jnp.zeros_like(acc_ref)

        err = _dot(a_ref[...], b_ref[...], 1, 0) + r_ref[...] - t_ref[...]
        dy = err * (1.0 / d)
        dy_ref[...] = dy
        dyb_ref[...] = dy.astype(BF16)
        acc_ref[...] += jnp.sum(err * err, axis=0, keepdims=True)

    row = pl.BlockSpec((tm, d), lambda i: (i, 0))
    vec = pl.BlockSpec((1, d), lambda i: (0, 0))
    return pl.pallas_call(
        body, name=name, grid=(t // tm,),
        in_specs=[pl.BlockSpec((tm, kdim), lambda i: (i, 0)), pl.BlockSpec((kdim, d), lambda i: (0, 0)), row, row],
        out_specs=[row, row, vec],
        out_shape=[jax.ShapeDtypeStruct((t, d), F32), jax.ShapeDtypeStruct((t, d), BF16), jax.ShapeDtypeStruct((1, d), F32)],
        compiler_params=_cp(("arbitrary",)),
    )(a, b, res, target)


def _head_sums(v):
    ri = lax.broadcasted_iota(jnp.int32, (LANES, LANES), 0) // HEAD_DIM
    ci = lax.broadcasted_iota(jnp.int32, (LANES, LANES), 1) // HEAD_DIM
    ones = (ri == ci).astype(BF16)
    hi = v.astype(BF16)
    lo_part = (v - hi.astype(F32)).astype(BF16)
    return _dot(hi, ones, 1, 0) + _dot(lo_part, ones, 1, 0)


def _head_rms(xs):
    r = lax.rsqrt(_head_sums(xs * xs) * (1.0 / HEAD_DIM) + EPS)
    return xs * r, r


def _head_rms_bwd(xs, w, dy):
    xh, r = _head_rms(xs)
    dxh = dy * w
    mm = _head_sums(dxh * xh) * (1.0 / HEAD_DIM)
    return r * (dxh - xh * mm), dy * xh


_QSCALE = HEAD_DIM ** -0.5


def _headnorm_fwd(proj, ws, name):
    t = proj.shape[0]
    tm = TQ
    lead = PAD_A // tm
    leadb = PAD_B // tm

    def body(p_ref, w_ref, qa_ref, ka_ref, va_ref, qb_ref, kb_ref, vb_ref):
        data = pl.program_id(0) >= lead
        lo = _lo_mask()

        def put(ref, c, val):
            ref[:, c:c + val.shape[1]] = jnp.where(data, val, 0.0).astype(BF16)

        def per_query_head(slab):
            other = pltpu.roll(slab, HEAD_DIM, 1)
            e0, e1 = jnp.where(lo, slab, other), jnp.where(lo, other, slab)
            return jnp.concatenate([e0, e0, e1, e1], axis=1)

        for s in range(D_ATT // LANES):
            c = LANES * s
            xh, _ = _head_rms(p_ref[:, COL_QA + c:COL_QA + c + LANES])
            qa_ref[:, c:c + LANES] = (xh * w_ref[0:1, :] * _QSCALE).astype(BF16)
            xh, _ = _head_rms(p_ref[:, COL_KA + c:COL_KA + c + LANES])
            put(ka_ref, c, xh * w_ref[1:2, :])
            xh, _ = _head_rms(p_ref[:, COL_QB + c:COL_QB + c + LANES])
            qb_ref[:, c:c + LANES] = (xh * w_ref[2:3, :] * _QSCALE).astype(BF16)
        put(va_ref, 0, p_ref[:, COL_VA:COL_VA + D_ATT])
        xh, _ = _head_rms(p_ref[:, COL_KB:COL_KB + LANES])
        put(kb_ref, 0, per_query_head(xh * w_ref[3:4, :]))
        put(vb_ref, 0, per_query_head(p_ref[:, COL_VB:COL_VB + LANES]))

    src = lambda i: jnp.maximum(i - lead, 0)
    wide = pl.BlockSpec((tm, D_ATT), lambda i: (src(i), 0))
    pad_a = pl.BlockSpec((tm, D_ATT), lambda i: (i, 0))
    pad_b = pl.BlockSpec((tm, D_ATT), lambda i: (jnp.maximum(i - lead + leadb, 0), 0))
    sd = lambda rows: jax.ShapeDtypeStruct((rows, D_ATT), BF16)
    return pl.pallas_call(
        body, name=name, grid=(t // tm + lead,),
        in_specs=[pl.BlockSpec((tm, ATTN_PROJ), lambda i: (src(i), 0)), pl.BlockSpec((4, LANES), lambda i: (0, 0))],
        out_specs=[wide, pad_a, pad_a, wide, pad_b, pad_b],
        out_shape=[sd(t), sd(t + PAD_A), sd(t + PAD_A), sd(t), sd(t + PAD_B), sd(t + PAD_B)],
        compiler_params=_cp(("arbitrary",)),
    )(proj, ws)


def _headnorm_bwd(proj, ws, dqa, dkpa, dvpa, dqb, dkpb, dvpb, name):
    t = proj.shape[0]
    tm = TQ
    offa, offb = PAD_A // tm, PAD_B // tm

    def body(p_ref, w_ref, dqa_ref, dka_ref, dva_ref, dqb_ref, dkb_ref, dvb_ref, dp_ref, dw_ref):
        i = pl.program_id(0)
        lo = _lo_mask()

        @pl.when(i == 0)
        def _():
            dw_ref[...] = jnp.zeros_like(dw_ref)

        acc = [jnp.zeros((1, LANES), F32) for _ in range(4)]
        for s in range(D_ATT // LANES):
            c = LANES * s
            dx, dwl = _head_rms_bwd(p_ref[:, COL_QA + c:COL_QA + c + LANES], w_ref[0:1, :], dqa_ref[:, c:c + LANES] * _QSCALE)
            dp_ref[:, COL_QA + c:COL_QA + c + LANES] = dx.astype(BF16)
            acc[0] += jnp.sum(dwl, axis=0, keepdims=True)
            dx, dwl = _head_rms_bwd(p_ref[:, COL_KA + c:COL_KA + c + LANES], w_ref[1:2, :], dka_ref[:, c:c + LANES])
            dp_ref[:, COL_KA + c:COL_KA + c + LANES] = dx.astype(BF16)
            acc[1] += jnp.sum(dwl, axis=0, keepdims=True)
            dx, dwl = _head_rms_bwd(p_ref[:, COL_QB + c:COL_QB + c + LANES], w_ref[2:3, :], dqb_ref[:, c:c + LANES] * _QSCALE)
            dp_ref[:, COL_QB + c:COL_QB + c + LANES] = dx.astype(BF16)
            acc[2] += jnp.sum(dwl, axis=0, keepdims=True)
        dp_ref[:, COL_VA:COL_VA + D_ATT] = dva_ref[...].astype(BF16)

        def group_sum(ref):
            s0 = ref[:, 0:LANES] + ref[:, LANES:2 * LANES]
            s1 = ref[:, 2 * LANES:3 * LANES] + ref[:, 3 * LANES:4 * LANES]
            s0 = s0 + pltpu.roll(s0, HEAD_DIM, 1)
            s1 = s1 + pltpu.roll(s1, HEAD_DIM, 1)
            return jnp.where(lo, s0, s1)

        dx, dwl = _head_rms_bwd(p_ref[:, COL_KB:COL_KB + LANES], w_ref[3:4, :], group_sum(dkb_ref))
        dp_ref[:, COL_KB:COL_KB + LANES] = dx.astype(BF16)
        acc[3] += jnp.sum(dwl, axis=0, keepdims=True)
        dp_ref[:, COL_VB:COL_VB + LANES] = group_sum(dvb_ref).astype(BF16)
        for n in range(4):
            dw_ref[n:n + 1, :] += acc[n]

    wide = pl.BlockSpec((tm, D_ATT), lambda i: (i, 0))
    pa = pl.BlockSpec((tm, D_ATT), lambda i: (i + offa, 0))
    pb = pl.BlockSpec((tm, D_ATT), lambda i: (i + offb, 0))
    whole = pl.BlockSpec((tm, ATTN_PROJ), lambda i: (i, 0))
    return pl.pallas_call(
        body, name=name, grid=(t // tm,),
        in_specs=[whole, pl.BlockSpec((4, LANES), lambda i: (0, 0)), wide, pa, pa, wide, pb, pb],
        out_specs=[whole, pl.BlockSpec((4, LANES), lambda i: (0, 0))],
        out_shape=[jax.ShapeDtypeStruct((t, ATTN_PROJ), BF16), jax.ShapeDtypeStruct((4, LANES), F32)],
        compiler_params=_cp(("arbitrary",)),
    )(proj, ws, dqa, dkpa, dvpa, dqb, dkpb, dvpb)


ROLL_W = 1024


def _rel_onehot():
    r_io = lax.broadcasted_iota(jnp.int32, (REL_W, ROLL_W), 0)
    m_io = lax.broadcasted_iota(jnp.int32, (REL_W, ROLL_W), 1)
    return (r_io == jnp.clip(REL_W - 1 - m_io, -MAX_REL, MAX_REL) + MAX_REL).astype(F32)


def _relpos_fwd(table, name):
    def body(t_ref, o_ref):
        rr = _dot(t_ref[...], _rel_onehot(), 1, 0, HI)

        def step(q, c):
            o_ref[q] = pltpu.roll(rr, (ROLL_W - (TQ - 1) + q) % ROLL_W, 1)[:, :REL_W]
            return c

        lax.fori_loop(0, TQ, step, 0)

    return pl.pallas_call(
        body, name=name, out_shape=jax.ShapeDtypeStruct((TQ, N_HEADS, REL_W), F32),
        in_specs=[pl.BlockSpec(memory_space=pltpu.VMEM)], out_specs=pl.BlockSpec(memory_space=pltpu.VMEM),
        compiler_params=_cp(),
    )(table)


def _relpos_bwd(dbias_t, name):
    def body(d_ref, o_ref):
        def step(q, acc):
            row = jnp.concatenate([d_ref[q], jnp.zeros((N_HEADS, ROLL_W - REL_W), F32)], axis=1)
            return acc + pltpu.roll(row, TQ - 1 - q, 1)

        drr = lax.fori_loop(0, TQ, step, jnp.zeros((N_HEADS, ROLL_W), F32))
        o_ref[...] = _dot(drr, _rel_onehot(), 1, 1, HI)

    return pl.pallas_call(
        body, name=name, out_shape=jax.ShapeDtypeStruct((N_HEADS, REL_W), F32),
        in_specs=[pl.BlockSpec(memory_space=pltpu.VMEM)], out_specs=pl.BlockSpec(memory_space=pltpu.VMEM),
        compiler_params=_cp(),
    )(dbias_t)


def _attn_scores(qe, kw, bias, kvalid):
    return jnp.where(kvalid, _dot(qe, kw, 1, 1) + bias, NEG)


def _stat_cols(stats, e):
    return stats[:, 64 * e:64 * e + 1], stats[:, 64 * e + 32:64 * e + 33]


def _attn_fwd(q, kp, vp, bias, sinks, pad, name, comm=None):
    t, hd = q.shape
    w = pad + TQ

    def body(sink_ref, q_ref, k_ref, v_ref, b_ref, o_ref, st_ref):
        hp, i = pl.program_id(0), pl.program_id(1)
        lo = _lo_mask()
        lane = lax.broadcasted_iota(jnp.int32, (1, LANES), 1)
        for j in range(ATT_SUB):
            start = pl.multiple_of((i * ATT_SUB + j) * TQ, TQ)
            qv = q_ref[TQ * j:TQ * (j + 1), :]
            kw = k_ref[pl.ds(start, w), :]
            vw = v_ref[pl.ds(start, w), :]
            kvalid = (start + lax.broadcasted_iota(jnp.int32, (1, w), 1)) >= pad
            outs, ms, ls = [], [], []
            for e in range(2):
                sel = lo if e == 0 else jnp.logical_not(lo)
                qe = jnp.where(sel, qv, jnp.zeros_like(qv))
                snk = sink_ref[2 * hp + e]
                s = _attn_scores(qe, kw, b_ref[e], kvalid)
                m = jnp.maximum(jnp.max(s, axis=-1, keepdims=True), snk)
                acc = _dot(jnp.exp(s - m).astype(BF16), jnp.where(sel, vw, jnp.ones_like(vw)), 1, 0)
                denom = acc[:, 64 * (1 - e):64 * (1 - e) + 1] + jnp.exp(snk - m)
                outs.append(acc * (1.0 / denom))
                ms.append(m)
                ls.append(denom)
            o_ref[TQ * j:TQ * (j + 1), :] = jnp.where(lo, outs[0], outs[1]).astype(BF16)
            st_ref[TQ * j:TQ * (j + 1), :] = jnp.where(lane < 32, ms[0], jnp.where(lane < 64, ls[0],
                                                                                 jnp.where(lane < 96, ms[1], ls[1])))

    full = pl.BlockSpec((t + pad, LANES), lambda h, i: (0, h))
    tile = pl.BlockSpec((ATT_SUB * TQ, LANES), lambda h, i: (i, h))
    (o, stats), got = _call(
        body, name=name, grid=(hd // LANES, t // (ATT_SUB * TQ)),
        in_specs=[pl.BlockSpec(memory_space=pltpu.SMEM), tile, full, full, pl.BlockSpec((2, TQ, w), lambda h, i: (h, 0, 0))],
        out_specs=[tile, tile], out_shape=[jax.ShapeDtypeStruct((t, hd), BF16), jax.ShapeDtypeStruct((t, hd), F32)],
        args=(sinks, q, kp, vp, bias), sem=("parallel", "arbitrary"), comm=comm)
    return o, stats, got


def _attn_bwd(q, kp, vp, bias, sinks, do, stats, o, col_off, pad, name, comm=None):
    t, hd = q.shape
    w = pad + TQ
    nhp = hd // LANES

    def body(sink_ref, q_ref, k_ref, v_ref, b_ref, do_ref, st_ref, o_ref, dq_ref, dk_ref, dv_ref, db_ref, ds_ref):
        hp, i = pl.program_id(0), pl.program_id(1)

        @pl.when(i == 0)
        def _():
            dk_ref[...] = jnp.zeros_like(dk_ref)
            dv_ref[...] = jnp.zeros_like(dv_ref)
            db_ref[...] = jnp.zeros_like(db_ref)
            ds_ref[...] = jnp.zeros_like(ds_ref)

        lo = _lo_mask()
        row8 = lax.broadcasted_iota(jnp.int32, (8, LANES), 0)
        dbias = [None, None]
        dsink = jnp.zeros((8, LANES), F32)
        for j in range(ATT_SUB):
            start = pl.multiple_of((i * ATT_SUB + j) * TQ, TQ)
            qv = q_ref[TQ * j:TQ * (j + 1), :]
            dov = do_ref[TQ * j:TQ * (j + 1), :]
            kw = k_ref[pl.ds(start, w), :]
            vw = v_ref[pl.ds(start, w), :]
            kvalid = (start + lax.broadcasted_iota(jnp.int32, (1, w), 1)) >= pad
            stats = st_ref[TQ * j:TQ * (j + 1), :]
            od = dov.astype(F32) * o_ref[TQ * j:TQ * (j + 1), :].astype(F32)
            dqs, dkw, dvw = [], None, None
            for e in range(2):
                sel = lo if e == 0 else jnp.logical_not(lo)
                qe = jnp.where(sel, qv, jnp.zeros_like(qv))
                doe = jnp.where(sel, dov, jnp.zeros_like(dov))
                m, denom = _stat_cols(stats, e)
                inv = 1.0 / denom
                p = jnp.exp(_attn_scores(qe, kw, b_ref[e], kvalid) - m) * inv
                psink = jnp.exp(sink_ref[2 * hp + e] - m) * inv
                dp = _dot(doe, vw, 1, 1)
                delta = jnp.sum(jnp.where(sel, od, 0.0), axis=-1, keepdims=True)
                ds = p * (dp - delta)
                dbias[e] = ds if dbias[e] is None else dbias[e] + ds
                dsink = dsink + jnp.where(row8 == e, jnp.sum(-psink * delta, axis=0, keepdims=True), 0.0)
                dsb = ds.astype(BF16)
                dqs.append(_dot(dsb, kw, 1, 0))
                dk_e = _dot(dsb, qe, 0, 0)
                dv_e = _dot(p.astype(BF16), doe, 0, 0)
                dkw = dk_e if dkw is None else dkw + dk_e
                dvw = dv_e if dvw is None else dvw + dv_e
            dq_ref[TQ * j:TQ * (j + 1), :] = jnp.where(lo, dqs[0], dqs[1])
            dk_ref[pl.ds(start, w), :] += dkw
            dv_ref[pl.ds(start, w), :] += dvw
        for e in range(2):
            db_ref[e] += dbias[e]
        ds_ref[0] += dsink

    full = pl.BlockSpec((t + pad, LANES), lambda h, i: (0, h))
    tile = pl.BlockSpec((ATT_SUB * TQ, LANES), lambda h, i: (i, h))
    btile = pl.BlockSpec((2, TQ, w), lambda h, i: (h, 0, 0))
    return _call(
        body, name=name, grid=(nhp, t // (ATT_SUB * TQ)),
        in_specs=[pl.BlockSpec(memory_space=pltpu.SMEM), tile, full, full, btile,
                  pl.BlockSpec((ATT_SUB * TQ, LANES), lambda h, i: (i, h + col_off)), tile, tile],
        out_specs=[tile, full, full, btile, pl.BlockSpec((1, 8, LANES), lambda h, i: (h, 0, 0))],
        out_shape=[jax.ShapeDtypeStruct((t, hd), F32), jax.ShapeDtypeStruct((t + pad, hd), F32),
                   jax.ShapeDtypeStruct((t + pad, hd), F32), jax.ShapeDtypeStruct((N_HEADS, TQ, w), F32),
                   jax.ShapeDtypeStruct((nhp, 8, LANES), F32)],
        args=(sinks, q, kp, vp, bias, do, stats, o), sem=("parallel", "arbitrary"), comm=comm)


def _conv_apply(taps, w_ref, ktaps):
    out = taps[0] * w_ref[ktaps - 1:ktaps, :]
    for s in range(1, ktaps):
        out = out + taps[s] * w_ref[ktaps - 1 - s:ktaps - s, :]
    return out


def _sigmoid(x):
    return jax.nn.sigmoid(x)


def _silu_grad(x):
    sg = _sigmoid(x)
    return x * sg, sg * (1.0 + x * (1.0 - sg))


FFN_HALO = 16
FFN_BT = 256
FFN_BC = 1408


def _ffn_in_mid(x, g, wt, w8, b, name):
    t, d = x.shape
    f = D_FF
    tm = FFN_BT

    def body(x_ref, g_ref, b_ref, w_ref, cb_ref, gu_ref, h_ref, a_ref, gc_ref, halo_ref):
        @pl.when(pl.program_id(0) == 0)
        def _():
            halo_ref[...] = jnp.zeros_like(halo_ref)

        xv = x_ref[...]
        r = lax.rsqrt(jnp.mean(xv * xv, axis=-1, keepdims=True) + EPS)
        h = (xv * r * g_ref[...]).astype(BF16)
        h_ref[...] = h
        for c in range(0, f, FFN_BC):
            cs = slice(c, c + FFN_BC)
            gate = _dot(h, b_ref[c:c + FFN_BC, :], 1, 1).astype(BF16)
            up = _dot(h, b_ref[f + c:f + c + FFN_BC, :], 1, 1).astype(BF16)
            gu_ref[:, cs] = gate
            gu_ref[:, f + c:f + c + FFN_BC] = up
            gf = gate.astype(F32)
            ext = jnp.concatenate([halo_ref[:, cs], gf], axis=0)
            gc = (cb_ref[:, cs] + gf * w_ref[2:3, cs] + pltpu.roll(ext, 1, 0)[8:] * w_ref[1:2, cs]
                  + pltpu.roll(ext, 2, 0)[8:] * w_ref[0:1, cs])
            a_ref[:, cs] = (gc * _sigmoid(gc) * up.astype(F32)).astype(BF16)
            gc_ref[:, cs] = gc.astype(BF16)
            halo_ref[:, cs] = gf[tm - 8:]

    row = pl.BlockSpec((tm, d), lambda i: (i, 0))
    row_f = pl.BlockSpec((tm, f), lambda i: (i, 0))
    return pl.pallas_call(
        body, name=name, grid=(t // tm,),
        in_specs=[row, pl.BlockSpec((1, d), lambda i: (0, 0)), pl.BlockSpec((2 * f, d), lambda i: (0, 0)),
                  pl.BlockSpec((8, f), lambda i: (0, 0)), pl.BlockSpec((1, f), lambda i: (0, 0))],
        out_specs=[pl.BlockSpec((tm, 2 * f), lambda i: (i, 0)), row, row_f, row_f],
        out_shape=[jax.ShapeDtypeStruct((t, 2 * f), BF16), jax.ShapeDtypeStruct((t, d), BF16), jax.ShapeDtypeStruct((t, f), BF16),
                   jax.ShapeDtypeStruct((t, f), BF16)],
        scratch_shapes=[pltpu.VMEM((8, f), F32)], compiler_params=_cp(("arbitrary",)),
    )(x, g, wt, w8, b)


def _ffn_mid_bwd(gu, gc, dxb, w_out, w8, name, comm=None):
    t, d = dxb.shape
    f = D_FF
    tm, hr = FFN_BT, FFN_HALO
    nt = t // tm
    n = tm + hr

    def body(g_ref, u_ref, un_ref, c_ref, cn_ref, dx_ref, dxn_ref, wo_ref, w_ref, dgu_ref, dw_ref, db_ref):
        i = pl.program_id(0)
        last = i == nt - 1

        @pl.when(i == 0)
        def _():
            dw_ref[...] = jnp.zeros_like(dw_ref)
            db_ref[...] = jnp.zeros_like(db_ref)

        dxe = jnp.concatenate([dx_ref[...], dxn_ref[...]], axis=0)
        row = lax.broadcasted_iota(jnp.int32, (n, 1), 0)
        keep = (row < tm) | jnp.logical_not(last)
        for c in range(0, f, FFN_BC):
            cs = slice(c, c + FFN_BC)
            act, dact = _silu_grad(jnp.concatenate([c_ref[:, cs], cn_ref[:, cs]], axis=0).astype(F32))
            da = _dot(dxe, wo_ref[cs, :], 1, 1)
            up = jnp.concatenate([u_ref[:, cs], un_ref[:, cs]], axis=0).astype(F32)
            dgc = jnp.where(keep, da * up * dact, 0.0)
            nxt = [dgc[:tm], pltpu.roll(dgc, n - 1, 0)[:tm], pltpu.roll(dgc, n - 2, 0)[:tm]]
            dgu_ref[:, f + c:f + c + FFN_BC] = (da[:tm] * act[:tm]).astype(BF16)
            dgu_ref[:, cs] = (nxt[0] * w_ref[2:3, cs] + nxt[1] * w_ref[1:2, cs] + nxt[2] * w_ref[0:1, cs]).astype(BF16)
            gate = g_ref[:, cs].astype(F32)
            db_ref[:, cs] += jnp.sum(nxt[0], axis=0, keepdims=True)
            for s in range(3):
                dw_ref[2 - s:3 - s, cs] += jnp.sum(nxt[s] * gate, axis=0, keepdims=True)

    r = tm // hr
    nxt_blk = lambda i: jnp.minimum((i + 1) * r, t // hr - 1)
    row_f = pl.BlockSpec((tm, f), lambda i: (i, 0))
    halo_f = pl.BlockSpec((hr, f), lambda i: (nxt_blk(i), 0))
    return _call(
        body, name=name, grid=(nt,),
        in_specs=[row_f, pl.BlockSpec((tm, f), lambda i: (i, 1)), pl.BlockSpec((hr, f), lambda i: (nxt_blk(i), 1)),
                  row_f, halo_f,
                  pl.BlockSpec((tm, d), lambda i: (i, 0)), pl.BlockSpec((hr, d), lambda i: (nxt_blk(i), 0)),
                  pl.BlockSpec((f, d), lambda i: (0, 0)), pl.BlockSpec((8, f), lambda i: (0, 0))],
        out_specs=[pl.BlockSpec((tm, 2 * f), lambda i: (i, 0)), pl.BlockSpec((8, f), lambda i: (0, 0)),
                   pl.BlockSpec((1, f), lambda i: (0, 0))],
        out_shape=[jax.ShapeDtypeStruct((t, 2 * f), BF16), jax.ShapeDtypeStruct((8, f), F32), jax.ShapeDtypeStruct((1, f), F32)],
        args=(gu, gu, gu, gc, gc, dxb, dxb, w_out, w8), sem=("arbitrary",), comm=comm)


PRE_TM = 256
PRE_TC = 1024


def _softplus_heads(v):
    sp = jnp.maximum(v, 0.0) + jnp.log(1.0 + jnp.exp(-jnp.abs(v)))
    return jnp.where(lax.broadcasted_iota(jnp.int32, (1, LANES), 1) < SSM_HEADS, sp, 0.0)


def _ssm_in_pre(x, g, wt, wt_dt, w8, b, dt_bias, name):
    t, d = x.shape
    tm, tc = PRE_TM, PRE_TC

    def body(x_ref, g_ref, b_ref, bdt_ref, w_ref, cb_ref, db_ref, zx_ref, h_ref, o_ref, c_ref, dtr_ref, dt_ref, halo_ref):
        @pl.when(pl.program_id(0) == 0)
        def _():
            halo_ref[...] = jnp.zeros_like(halo_ref)

        xv = x_ref[...]
        r = lax.rsqrt(jnp.mean(xv * xv, axis=-1, keepdims=True) + EPS)
        h = (xv * r * g_ref[...]).astype(BF16)
        h_ref[...] = h
        dtr = _dot(h, bdt_ref[...], 1, 1)
        dtr_ref[...] = dtr
        dt_ref[...] = _softplus_heads(dtr + db_ref[...])
        for c in range(0, ZX, tc):
            v = _dot(h, b_ref[c:c + tc, :], 1, 1)
            zx_ref[:, c:c + tc] = v
            if c >= D_INNER:
                cs = slice(c - D_INNER, c - D_INNER + tc)
                ext = jnp.concatenate([halo_ref[:, cs], v], axis=0)
                conv = cb_ref[:, cs] + v * w_ref[3:4, cs]
                for s in (1, 2, 3):
                    conv = conv + pltpu.roll(ext, s, 0)[8:] * w_ref[3 - s:4 - s, cs]
                o_ref[:, cs] = conv * _sigmoid(conv)
                c_ref[:, cs] = conv.astype(BF16)
                halo_ref[:, cs] = v[tm - 8:]

    row = pl.BlockSpec((tm, d), lambda i: (i, 0))
    row_x = pl.BlockSpec((tm, XBC), lambda i: (i, 0))
    row_h = pl.BlockSpec((tm, LANES), lambda i: (i, 0))
    return pl.pallas_call(
        body, name=name, grid=(t // tm,),
        in_specs=[row, pl.BlockSpec((1, d), lambda i: (0, 0)), pl.BlockSpec(wt.shape, lambda i: (0, 0)),
                  pl.BlockSpec(wt_dt.shape, lambda i: (0, 0)),
                  pl.BlockSpec((8, XBC), lambda i: (0, 0)), pl.BlockSpec((1, XBC), lambda i: (0, 0)),
                  pl.BlockSpec((1, LANES), lambda i: (0, 0))],
        out_specs=[pl.BlockSpec((tm, ZX), lambda i: (i, 0)), row, row_x, row_x, row_h, row_h],
        out_shape=[jax.ShapeDtypeStruct((t, ZX), F32), jax.ShapeDtypeStruct((t, d), BF16), jax.ShapeDtypeStruct((t, XBC), F32),
                   jax.ShapeDtypeStruct((t, XBC), BF16), jax.ShapeDtypeStruct((t, LANES), F32),
                   jax.ShapeDtypeStruct((t, LANES), F32)],
        scratch_shapes=[pltpu.VMEM((8, XBC), F32)], compiler_params=_cp(("arbitrary",)),
    )(x, g, wt, wt_dt, w8, b, dt_bias)


PRE_HALO = 16


def _ssm_pre_bwd(zx, conv, dxbc, w8, name):
    t = zx.shape[0]
    tm, tc, hr = PRE_TM, PRE_TC, PRE_HALO
    off = D_INNER // tc
    nt = t // tm
    n = tm + hr

    def body(x_ref, c_ref, cn_ref, d_ref, dn_ref, w_ref, o_ref, dw_ref, db_ref):
        i = pl.program_id(1)
        last = i == nt - 1

        @pl.when(i == 0)
        def _():
            dw_ref[...] = jnp.zeros_like(dw_ref)
            db_ref[...] = jnp.zeros_like(db_ref)

        _, dact = _silu_grad(jnp.concatenate([c_ref[...], cn_ref[...]], axis=0).astype(F32))
        row = lax.broadcasted_iota(jnp.int32, (n, 1), 0)
        dc = jnp.where((row < tm) | jnp.logical_not(last), jnp.concatenate([d_ref[...], dn_ref[...]], axis=0) * dact, 0.0)
        nxt = [dc[:tm]] + [pltpu.roll(dc, n - s, 0)[:tm] for s in (1, 2, 3)]
        o_ref[...] = _conv_apply(nxt, w_ref, 4).astype(BF16)
        xv = x_ref[...]
        db_ref[...] += jnp.sum(nxt[0], axis=0, keepdims=True)
        for s in range(4):
            dw_ref[3 - s:4 - s, :] += jnp.sum(nxt[s] * xv, axis=0, keepdims=True)

    nxt_blk = lambda i: jnp.minimum((i + 1) * (tm // hr), t // hr - 1)
    tile = pl.BlockSpec((tm, tc), lambda j, i: (i, j))
    halo = pl.BlockSpec((hr, tc), lambda j, i: (nxt_blk(i), j))
    return pl.pallas_call(
        body, name=name, grid=(XBC // tc, nt),
        in_specs=[pl.BlockSpec((tm, tc), lambda j, i: (i, j + off)), tile, halo, tile, halo,
                  pl.BlockSpec((8, tc), lambda j, i: (0, j))],
        out_specs=[tile, pl.BlockSpec((8, tc), lambda j, i: (0, j)), pl.BlockSpec((1, tc), lambda j, i: (0, j))],
        out_shape=[jax.ShapeDtypeStruct((t, XBC), BF16), jax.ShapeDtypeStruct((8, XBC), F32),
                   jax.ShapeDtypeStruct((1, XBC), F32)],
        compiler_params=_cp(("parallel", "arbitrary")),
    )(zx, conv, conv, dxbc, dxbc, w8)


def _head_lanes():
    return lax.broadcasted_iota(jnp.int32, (1, LANES), 1) < SSM_HEADS


def _dt_bwd(dtraw, bias, ddt, name):
    t = dtraw.shape[0]
    tm = _pick(t, (1024, 512, 256, 128))

    def body(x_ref, b_ref, d_ref, o_ref, db_ref):
        @pl.when(pl.program_id(0) == 0)
        def _():
            db_ref[...] = jnp.zeros_like(db_ref)

        g = jnp.where(_head_lanes(), d_ref[...] * _sigmoid(x_ref[...] + b_ref[...]), 0.0)
        o_ref[...] = g.astype(BF16)
        db_ref[...] += jnp.sum(g, axis=0, keepdims=True)

    row = pl.BlockSpec((tm, LANES), lambda i: (i, 0))
    vec = pl.BlockSpec((1, LANES), lambda i: (0, 0))
    return pl.pallas_call(
        body, name=name, grid=(t // tm,), in_specs=[row, vec, row], out_specs=[row, vec],
        out_shape=[jax.ShapeDtypeStruct((t, LANES), BF16), jax.ShapeDtypeStruct((1, LANES), F32)],
        compiler_params=_cp(("arbitrary",)),
    )(dtraw, bias, ddt)


GROUP_W = D_INNER // SSM_GROUPS


def _ssd_common(dt, alog):
    ll = dt.shape[0]
    a_neg = -jnp.exp(alog)
    a = dt * a_neg
    ri = lax.broadcasted_iota(jnp.int32, (ll, ll), 0)
    ci = lax.broadcasted_iota(jnp.int32, (ll, ll), 1)
    tril = ri >= ci
    acs = _dot(tril.astype(F32), a, 1, 0, HI)
    return a_neg, tril, acs, acs.T


def _pair_terms(acs, acs_t, dt, h0, lo):
    ll = acs.shape[0]
    cols = [acs[:, h0 + e:h0 + e + 1] for e in range(2)]
    rows = [acs_t[h0 + e:h0 + e + 1, :] for e in range(2)]
    dtc = [dt[:, h0 + e:h0 + e + 1] for e in range(2)]
    lasts = [c[ll - 1:ll, :] for c in cols]
    dtx = jnp.where(lo, dtc[0], dtc[1])
    eac = jnp.where(lo, jnp.exp(cols[0]), jnp.exp(cols[1]))
    fdec = jnp.where(lo, jnp.exp(lasts[0] - cols[0]), jnp.exp(lasts[1] - cols[1]))
    elast = jnp.where(lo, jnp.exp(lasts[0]), jnp.exp(lasts[1]))
    return cols, rows, dtx, eac, fdec, elast


def _decay(col, row, tril):
    return jnp.where(tril, jnp.exp(jnp.minimum(col - row, 0.0)), 0.0)


def _two_heads_rows(v, lo):
    z = jnp.zeros_like(v)
    return jnp.concatenate([jnp.where(lo, v, z), jnp.where(lo, z, v)], axis=0)


def _two_heads_cols(ms):
    return jnp.concatenate(ms, axis=1)


def _z_group(z_refs, g):
    return z_refs[g // 2][:, GROUP_W * (g % 2):GROUP_W * (g % 2 + 1)]


def _ssd_fwd(xbc, dt, alog, zx, dexp, nw, name, comm=None):
    t = xbc.shape[0]
    ll = SSD_L
    nc = t // ll

    def body(x_ref, dt_ref, al_ref, z0_ref, z1_ref, d_ref, w_ref, y_ref, sp_ref, y4_ref, st_ref):
        @pl.when(pl.program_id(0) == 0)
        def _():
            st_ref[...] = jnp.zeros_like(st_ref)

        dtv = dt_ref[...]
        _, tril, acs, acs_t = _ssd_common(dtv, al_ref[...])
        lo = _lo_mask()
        sp_ref[0] = st_ref[...]
        for g in range(SSM_GROUPS):
            bg = x_ref[:, D_INNER + SSM_STATE * g:D_INNER + SSM_STATE * (g + 1)].astype(BF16)
            cg = x_ref[:, D_INNER + 512 + SSM_STATE * g:D_INNER + 512 + SSM_STATE * (g + 1)].astype(BF16)
            gm = _dot(cg, bg, 1, 1)
            g0 = GROUP_W * g
            terms = [_pair_terms(acs, acs_t, dtv, 8 * g + 2 * pp, lo) for pp in range(4)]
            dtx, eac, fdec, elast = [jnp.concatenate([tt[k] for tt in terms], axis=1) for k in (2, 3, 4, 5)]
            xg = x_ref[:, g0:g0 + GROUP_W]
            ug = (xg * dtx).astype(BF16)
            sg = st_ref[:, g0:g0 + GROUP_W]
            yst = _dot(cg, sg.astype(BF16), 1, 0) * eac
            st_ref[:, g0:g0 + GROUP_W] = sg * elast + _dot(bg, (xg * (fdec * dtx)).astype(BF16), 0, 0)
            ys = []
            for pp in range(4):
                cols, rows = terms[pp][0], terms[pp][1]
                sl = slice(LANES * pp, LANES * (pp + 1))
                y_in = _dot(_two_heads_cols([(gm * _decay(cols[e], rows[e], tril)).astype(BF16) for e in range(2)]),
                            _two_heads_rows(ug[:, sl], lo), 1, 0)
                ys.append(y_in + yst[:, sl])
            yg = jnp.concatenate(ys, axis=1)
            y_ref[:, g0:g0 + GROUP_W] = yg
            zg = _z_group((z0_ref, z1_ref), g)
            y3 = (yg + d_ref[:, g0:g0 + GROUP_W] * xg) * (zg * _sigmoid(zg))
            r = lax.rsqrt(jnp.mean(y3 * y3, axis=-1, keepdims=True) + EPS)
            y4_ref[:, g0:g0 + GROUP_W] = (y3 * r * w_ref[:, g0:g0 + GROUP_W]).astype(BF16)

    zblk = lambda j: pl.BlockSpec((ll, 1024), lambda c: (c, j))
    vec = pl.BlockSpec((1, D_INNER), lambda c: (0, 0))
    row = pl.BlockSpec((ll, D_INNER), lambda c: (c, 0))
    return _call(
        body, name=name, grid=(nc,),
        in_specs=[pl.BlockSpec((ll, XBC), lambda c: (c, 0)), pl.BlockSpec((ll, LANES), lambda c: (c, 0)),
                  pl.BlockSpec((1, LANES), lambda c: (0, 0)), zblk(0), zblk(1), vec, vec],
        out_specs=[row, pl.BlockSpec((1, SSM_STATE, D_INNER), lambda c: (c, 0, 0)), row],
        out_shape=[jax.ShapeDtypeStruct((t, D_INNER), F32), jax.ShapeDtypeStruct((nc, SSM_STATE, D_INNER), F32),
                   jax.ShapeDtypeStruct((t, D_INNER), BF16)],
        scratch_shapes=[pltpu.VMEM((SSM_STATE, D_INNER), F32)],
        args=(xbc, dt, alog, zx, zx, dexp, nw), sem=("arbitrary",), comm=comm)


def _ssd_bwd(xbc, dt, alog, sprev, dy4, y, zx, dexp, nw, name, comm=None):
    t = xbc.shape[0]
    ll = SSD_L_BWD if t % SSD_L_BWD == 0 else SSD_L
    nc = t // ll
    every = ll // SSD_L

    def body(x_ref, dt_ref, al_ref, sp_ref, g4_ref, y_ref, z0_ref, z1_ref, d_ref, w_ref,
             dx_ref, ddt_ref, dal_ref, dz_ref, dd_ref, dnw_ref, ds_ref, colt_ref):
        @pl.when(pl.program_id(0) == 0)
        def _():
            ds_ref[...] = jnp.zeros_like(ds_ref)
            dal_ref[...] = jnp.zeros_like(dal_ref)
            dd_ref[...] = jnp.zeros_like(dd_ref)
            dnw_ref[...] = jnp.zeros_like(dnw_ref)

        dtv = dt_ref[...]
        a_neg, tril, acs, acs_t = _ssd_common(dtv, al_ref[...])
        lo = _lo_mask()
        hi = jnp.logical_not(lo)
        lane = lax.broadcasted_iota(jnp.int32, (1, LANES), 1)
        colt_ref[...] = jnp.zeros_like(colt_ref)
        rowterm = jnp.zeros((ll, LANES), F32)
        ddt_u = jnp.zeros((ll, LANES), F32)
        dlast = jnp.zeros((1, LANES), F32)

        def halves(v):
            return (jnp.sum(jnp.where(lo, v, 0.0), axis=-1, keepdims=True),
                    jnp.sum(jnp.where(hi, v, 0.0), axis=-1, keepdims=True))

        for g in range(SSM_GROUPS):
            cb0 = D_INNER + SSM_STATE * g
            cc0 = D_INNER + 512 + SSM_STATE * g
            bg = x_ref[:, cb0:cb0 + SSM_STATE].astype(BF16)
            cg = x_ref[:, cc0:cc0 + SSM_STATE].astype(BF16)
            gm = _dot(cg, bg, 1, 1)
            g0 = GROUP_W * g
            terms = [_pair_terms(acs, acs_t, dtv, 8 * g + 2 * pp, lo) for pp in range(4)]
            dtx, eac, fdec, elast = [jnp.concatenate([tt[k] for tt in terms], axis=1) for k in (2, 3, 4, 5)]
            xg = x_ref[:, g0:g0 + GROUP_W]
            u32 = xg * dtx
            ug = u32.astype(BF16)
            zg = _z_group((z0_ref, z1_ref), g)
            dg = d_ref[:, g0:g0 + GROUP_W]
            act, dact = _silu_grad(zg)
            y2 = y_ref[:, g0:g0 + GROUP_W] + dg * xg
            y3 = y2 * act
            rn = lax.rsqrt(jnp.mean(y3 * y3, axis=-1, keepdims=True) + EPS)
            y3n = y3 * rn
            gv = g4_ref[:, g0:g0 + GROUP_W]
            dyn = gv * w_ref[:, g0:g0 + GROUP_W]
            dy3 = rn * (dyn - y3n * jnp.mean(dyn * y3n, axis=-1, keepdims=True))
            dyg = dy3 * act
            dskip = dyg * dg
            dz_ref[:, g0:g0 + GROUP_W] = (dy3 * y2 * dact).astype(BF16)
            dd_ref[:, g0:g0 + GROUP_W] += jnp.sum(dyg * xg, axis=0, keepdims=True)
            dnw_ref[:, g0:g0 + GROUP_W] += jnp.sum(gv * y3n, axis=0, keepdims=True)
            dyb = dyg.astype(BF16)
            spg = sp_ref[0, :, g0:g0 + GROUP_W]
            spb = spg.astype(BF16)
            dsg = ds_ref[:, g0:g0 + GROUP_W]
            dsb = dsg.astype(BF16)
            du_st = _dot(bg, dsb, 1, 0) * fdec
            yst = _dot(cg, spb, 1, 0) * eac
            dye = (dyg * eac).astype(BF16)
            dc_st = _dot(dye, spb, 1, 1)
            db_st = _dot((xg * (fdec * dtx)).astype(BF16), dsb, 1, 1)
            ds_ref[:, g0:g0 + GROUP_W] = dsg * elast + _dot(cg, dye, 0, 0)
            qst_el = du_st * u32
            rq_el = dyg * yst - qst_el
            q_row = jnp.sum(qst_el, axis=0, keepdims=True)
            s_row = jnp.sum(dsg * spg, axis=0, keepdims=True)
            dgm = jnp.zeros((ll, ll), F32)
            for pp in range(4):
                h0 = 8 * g + 2 * pp
                cols, rows = terms[pp][0], terms[pp][1]
                sl = slice(LANES * pp, LANES * (pp + 1))
                decs = [_decay(cols[e], rows[e], tril) for e in range(2)]
                wms = [gm * d for d in decs]
                dum2 = _dot(dyb[:, sl], _two_heads_rows(ug[:, sl], lo), 1, 1)
                du = _dot(jnp.concatenate([wm.astype(BF16) for wm in wms], axis=0),
                          _two_heads_rows(dyb[:, sl], lo), 0, 0) + du_st[:, sl]
                dx_ref[:, g0 + LANES * pp:g0 + LANES * (pp + 1)] = du * dtx[:, sl] + dskip[:, sl]
                ddtu = halves(du * xg[:, sl])
                rq = halves(rq_el[:, sl])
                qs = halves(q_row[:, sl])
                ss = halves(s_row[:, sl])
                for e in range(2):
                    dum = dum2[:, ll * e:ll * (e + 1)]
                    dgm = dgm + dum * decs[e]
                    tm_ = dum * wms[e]
                    oh = lane == (h0 + e)
                    rowterm = rowterm + jnp.where(oh, jnp.sum(tm_, axis=1, keepdims=True) + rq[e], 0.0)
                    ddt_u = ddt_u + jnp.where(oh, ddtu[e], 0.0)
                    dlast = dlast + jnp.where(oh, jnp.exp(cols[e][ll - 1:ll, :]) * ss[e] + qs[e], 0.0)
                    colt_ref[h0 + e:h0 + e + 1, :] = jnp.sum(tm_, axis=0, keepdims=True)
            dgb = dgm.astype(BF16)
            dx_ref[:, cc0:cc0 + SSM_STATE] = _dot(dgb, bg, 1, 0) + dc_st
            dx_ref[:, cb0:cb0 + SSM_STATE] = _dot(dgb, cg, 0, 0) + db_st
        row_io = lax.broadcasted_iota(jnp.int32, (ll, LANES), 0)
        dacs = rowterm - colt_ref[...].T + jnp.where(row_io == ll - 1, dlast, 0.0)
        da = _dot(jnp.logical_not(tril).astype(F32) + jnp.where(
            lax.broadcasted_iota(jnp.int32, (ll, ll), 0) == lax.broadcasted_iota(jnp.int32, (ll, ll), 1), 1.0, 0.0),
            dacs, 1, 0, HI)
        ddt_ref[...] = da * a_neg + ddt_u
        dal_ref[...] += jnp.sum(da * dtv, axis=0, keepdims=True) * a_neg

    rev = lambda c: nc - 1 - c
    row = pl.BlockSpec((ll, D_INNER), lambda c: (rev(c), 0))
    vec = pl.BlockSpec((1, D_INNER), lambda c: (0, 0))
    zblk = lambda j: pl.BlockSpec((ll, 1024), lambda c: (rev(c), j))
    return _call(
        body, name=name, grid=(nc,),
        in_specs=[pl.BlockSpec((ll, XBC), lambda c: (rev(c), 0)), pl.BlockSpec((ll, LANES), lambda c: (rev(c), 0)),
                  pl.BlockSpec((1, LANES), lambda c: (0, 0)),
                  pl.BlockSpec((1, SSM_STATE, D_INNER), lambda c: (rev(c) * every, 0, 0)), row, row, zblk(0), zblk(1), vec, vec],
        out_specs=[pl.BlockSpec((ll, XBC), lambda c: (rev(c), 0)), pl.BlockSpec((ll, LANES), lambda c: (rev(c), 0)),
                   pl.BlockSpec((1, LANES), lambda c: (0, 0)), row, vec, vec],
        out_shape=[jax.ShapeDtypeStruct((t, XBC), F32), jax.ShapeDtypeStruct((t, LANES), F32),
                   jax.ShapeDtypeStruct((1, LANES), F32), jax.ShapeDtypeStruct((t, D_INNER), BF16),
                   jax.ShapeDtypeStruct((1, D_INNER), F32), jax.ShapeDtypeStruct((1, D_INNER), F32)],
        scratch_shapes=[pltpu.VMEM((SSM_STATE, D_INNER), F32), pltpu.VMEM((LANES, ll), F32)],
        args=(xbc, dt, alog, sprev, dy4, y, zx, zx, dexp, nw), sem=("arbitrary",), comm=comm)


def _sum_parts(parts, name):
    nparts, r, c = parts.shape
    tc = _pick(c, (256, 128))

    def body(p_ref, o_ref):
        g = p_ref[0].astype(F32)
        for k in range(1, nparts):
            g = g + p_ref[k].astype(F32)
        o_ref[...] = g

    return pl.pallas_call(
        body, name=name, grid=(c // tc,), in_specs=[pl.BlockSpec((nparts, r, tc), lambda j: (0, 0, j))],
        out_specs=pl.BlockSpec((r, tc), lambda j: (0, j)), out_shape=jax.ShapeDtypeStruct((r, c), F32),
        compiler_params=_cp(("parallel",)),
    )(parts)


def _adamw(parts, w, m, v, name):
    nl, r, c = w.shape
    assert len(parts) == nl
    tr = _pick(r, (256, 128, 64))
    c1 = 1.0 - ADAM_B1 ** ADAM_STEP
    c2 = 1.0 - ADAM_B2 ** ADAM_STEP

    def body(*refs):
        p_refs = refs[:nl]
        w_ref, m_ref, v_ref, g_ref, d_ref, mo_ref, vo_ref = refs[nl:]
        g = None
        for l, p_ref in enumerate(p_refs):
            s = p_ref[0].astype(F32)
            for k in range(1, p_ref.shape[0]):
                s = s + p_ref[k].astype(F32)
            g = s if g is None else jnp.where(pl.program_id(0) == l, s, g)
        mn = ADAM_B1 * m_ref[0] + (1.0 - ADAM_B1) * g
        vn = ADAM_B2 * v_ref[0] + (1.0 - ADAM_B2) * (g * g)
        g_ref[0] = g
        mo_ref[0] = mn
        vo_ref[0] = vn
        d_ref[0] = -ADAM_LR * ((mn / c1) / (jnp.sqrt(vn / c2) + ADAM_EPS) + ADAM_WD * w_ref[0])

    row = pl.BlockSpec((1, tr, c), lambda l, i: (l, i, 0))
    sd = jax.ShapeDtypeStruct((nl, r, c), F32)
    return pl.pallas_call(
        body, name=name, grid=(nl, r // tr),
        in_specs=[pl.BlockSpec((p.shape[0], tr, c), lambda l, i: (0, i, 0)) for p in parts] + [row, row, row],
        out_specs=[row, row, row, row], out_shape=[sd, sd, sd, sd], compiler_params=_cp(("parallel", "parallel")),
    )(*parts, w, m, v)


def _peers():
    mx, my, mc = lax.axis_index("x"), lax.axis_index("y"), lax.axis_index("c")
    me = 4 * mx + 2 * my + mc
    out = []
    for k in range(1, N_DEV):
        px = 1 - mx if k & 4 else mx
        py = 1 - my if k & 2 else my
        pc = 1 - mc if k & 1 else mc
        out.append(((px, py, pc), 4 * px + 2 * py + pc))
    return me, out


class _Comm:
    def __init__(self, arrs, scatters):
        self.arrs, self.scatters, self.n = list(arrs), list(scatters), len(arrs)
        self.specs = [pl.BlockSpec(memory_space=pl.ANY)] * self.n
        self.out_shape = [jax.ShapeDtypeStruct(x.shape if sc else (N_DEV,) + x.shape, x.dtype)
                          for x, sc in zip(self.arrs, self.scatters)]
        np_ = N_DEV - 1
        self.scratch = [pltpu.SemaphoreType.DMA((np_ * self.n,)), pltpu.SemaphoreType.DMA((np_ * self.n,)),
                        pltpu.SemaphoreType.DMA((self.n,))]

    def _copies(self, x_refs, o_refs, sems):
        send_sems, recv_sems, local_sems = sems
        me, peers = _peers()
        np_ = N_DEV - 1
        local, sends, recvs = [], [], []
        for a in range(self.n):
            mine = x_refs[a].at[me] if self.scatters[a] else x_refs[a]
            local.append(pltpu.make_async_copy(mine, o_refs[a].at[me], local_sems.at[a]))
        for k, (dev, idx) in enumerate(peers):
            for a in range(self.n):
                mine = x_refs[a].at[me] if self.scatters[a] else x_refs[a]
                sends.append(pltpu.make_async_remote_copy(
                    src_ref=x_refs[a].at[idx] if self.scatters[a] else x_refs[a], dst_ref=o_refs[a].at[me],
                    send_sem=send_sems.at[a * np_ + k], recv_sem=recv_sems.at[a * np_ + k], device_id=dev, device_id_type=MESH))
                recvs.append(pltpu.make_async_remote_copy(
                    src_ref=mine, dst_ref=o_refs[a].at[idx], send_sem=send_sems.at[a * np_ + k],
                    recv_sem=recv_sems.at[a * np_ + k], device_id=dev, device_id_type=MESH))
        return local, sends, recvs

    def start(self, x_refs, o_refs, sems):
        local, sends, _ = self._copies(x_refs, o_refs, sems)
        for cp in local + sends:
            cp.start()

    def wait(self, x_refs, o_refs, sems):
        local, sends, recvs = self._copies(x_refs, o_refs, sems)
        for cp in recvs:
            cp.wait_recv()
        for cp in sends:
            cp.wait_send()
        for cp in local:
            cp.wait()


class _Gather2(_Comm):
    def __init__(self, arrs):
        super().__init__(arrs, [False] * len(arrs))

    def _plan(self, x_refs, o_refs, sems):
        send_sems, recv_sems, local_sems = sems
        mx, my, mc = lax.axis_index("x"), lax.axis_index("y"), lax.axis_index("c")
        slot = lambda px, py, pc: 4 * px + 2 * py + pc
        sib = (mx, my, 1 - mc)
        chips = [(1 - mx, my), (mx, 1 - my), (1 - mx, 1 - my)]
        np_ = N_DEV - 1
        local, first, passed, arrive_first, arrive_rest = [], [], [], [], []

        def copy(a, k, src, block, to):
            return pltpu.make_async_remote_copy(
                src_ref=src, dst_ref=o_refs[a].at[block], send_sem=send_sems.at[a * np_ + k], recv_sem=recv_sems.at[a * np_ + k],
                device_id=to, device_id_type=MESH)

        for a in range(self.n):
            me = slot(mx, my, mc)
            local.append(pltpu.make_async_copy(x_refs[a], o_refs[a].at[me], local_sems.at[a]))
            first.append(copy(a, 0, x_refs[a], me, sib))
            arrive_rest.append(copy(a, 0, x_refs[a], slot(*sib), sib))
            for j, (cx, cy) in enumerate(chips):
                first.append(copy(a, 1 + j, x_refs[a], me, (cx, cy, mc)))
                arrive_first.append(copy(a, 1 + j, x_refs[a], slot(cx, cy, mc), (cx, cy, mc)))
                passed.append(copy(a, 4 + j, o_refs[a].at[slot(cx, cy, mc)], slot(cx, cy, mc), sib))
                arrive_rest.append(copy(a, 4 + j, x_refs[a], slot(cx, cy, 1 - mc), sib))
        return local, first, passed, arrive_first, arrive_rest

    def start(self, x_refs, o_refs, sems):
        local, first, _, _, _ = self._plan(x_refs, o_refs, sems)
        for cp in local + first:
            cp.start()

    def wait(self, x_refs, o_refs, sems):
        local, first, passed, arrive_first, arrive_rest = self._plan(x_refs, o_refs, sems)
        for arrived, onward in zip(arrive_first, passed):
            arrived.wait_recv()
            onward.start()
        for cp in arrive_rest:
            cp.wait_recv()
        for cp in first + passed:
            cp.wait_send()
        for cp in local:
            cp.wait()


def _call(body, *, name, grid, in_specs, out_specs, out_shape, args, scratch_shapes=(), sem=None, comm=None):
    if comm is None:
        outs = pl.pallas_call(
            body, name=name, grid=grid, in_specs=list(in_specs), out_specs=list(out_specs), out_shape=list(out_shape),
            scratch_shapes=list(scratch_shapes), compiler_params=_cp(sem),
        )(*args)
        return list(outs), []
    n_in, n_out, nc = len(in_specs), len(out_specs), comm.n
    nsteps = 1
    for g in grid:
        nsteps *= g

    def carrier(*refs):
        ins, cin = refs[:n_in], refs[n_in:n_in + nc]
        outs, cout = refs[n_in + nc:n_in + nc + n_out], refs[n_in + nc + n_out:n_in + 2 * nc + n_out]
        rest = refs[n_in + 2 * nc + n_out:]
        scratch, sems = rest[:len(rest) - 3], rest[len(rest) - 3:]
        if nsteps == 1:
            comm.start(cin, cout, sems)
            body(*ins, *outs, *scratch)
            comm.wait(cin, cout, sems)
            return
        step = 0
        for d, g in enumerate(grid):
            step = step * g + pl.program_id(d)

        @pl.when(step == 0)
        def _():
            comm.start(cin, cout, sems)

        body(*ins, *outs, *scratch)

        @pl.when(step == nsteps - 1)
        def _():
            comm.wait(cin, cout, sems)

    outs = pl.pallas_call(
        carrier, name=name, grid=grid, in_specs=list(in_specs) + comm.specs, out_specs=list(out_specs) + comm.specs,
        out_shape=list(out_shape) + comm.out_shape, scratch_shapes=list(scratch_shapes) + comm.scratch,
        compiler_params=_cp(("arbitrary",) * len(grid) if grid else None),
    )(*args, *comm.arrs)
    return list(outs[:n_out]), list(outs[n_out:])


def _exchange(comm, name):
    return _call(lambda *refs: None, name=name, grid=(), in_specs=[], out_specs=[], out_shape=[], args=[], comm=comm)[1]


def _pack(arrs, dtype, lead=()):
    nl = len(lead)
    flat = jnp.concatenate([a.astype(dtype).reshape(lead + (-1,)) for a in arrs], axis=nl)
    n = flat.shape[-1]
    rows = -(-n // (LANES * 8)) * 8
    flat = jnp.pad(flat, [(0, 0)] * nl + [(0, rows * LANES - n)])
    return flat.reshape(lead + (rows, LANES))


def _unpack(flat, shapes, lead=()):
    nl = len(lead)
    flat = flat.reshape(lead + (-1,))
    out, o = [], 0
    for s in shapes:
        n = 1
        for d in s:
            n *= d
        out.append(lax.slice_in_dim(flat, o, o + n, axis=nl).reshape(lead + tuple(s)))
        o += n
    return out


def _join(g, ax):
    return jnp.concatenate([g[d] for d in range(N_DEV)], axis=ax)


def _split(full, ax):
    n = full.shape[ax] // N_DEV
    return jnp.stack([lax.slice_in_dim(full, d * n, (d + 1) * n, axis=ax) for d in range(N_DEV)])


_WEIGHTS = ['norm_mix', 'norm_ffn', 'attn_w_in', 'attn_w_out', 'relpos_table', 'q_norm_a', 'k_norm_a', 'q_norm_b',
            'k_norm_b', 'sinks', 'ssm_w_in', 'ssm_conv_w', 'ssm_conv_b', 'ssm_dt_bias', 'ssm_a_log', 'ssm_d', 'ssm_norm',
            'ssm_w_out', 'ffn_w_in', 'ffn_conv_w', 'ffn_conv_b', 'ffn_w_out']
_SHARD_AX = {'attn_w_in': 2, 'attn_w_out': 1, 'ssm_w_in': 2, 'ssm_conv_w': 2, 'ssm_conv_b': 1, 'ssm_norm': 1,
             'ssm_w_out': 1, 'ffn_w_in': 2, 'ffn_conv_w': 2, 'ffn_w_out': 1}
_BIG = ['attn_w_in', 'attn_w_out', 'ssm_w_in', 'ssm_w_out', 'ffn_w_in', 'ffn_w_out']
_SMALL = ['ssm_conv_w', 'ssm_conv_b', 'ssm_norm', 'ffn_conv_w']
_AX2 = {n: _SHARD_AX[n] - 1 for n in _BIG}
_REPL = [n for n in _WEIGHTS if n not in _SHARD_AX]


def _rows8(w):
    return jnp.pad(w, ((0, 8 - w.shape[0]), (0, 0)))


def _lanes128(v):
    return jnp.pad(v, (0, LANES - v.shape[0])).reshape(1, LANES)


def _band_mask(n_prev, pad):
    cq = jnp.arange(TQ)[:, None] // CHUNK
    ck = jnp.arange(pad + TQ)[None, :] // CHUNK
    return (ck >= cq) & (ck <= cq + n_prev)


def _ffn_fwd(xin, g, w_in_t, w8, cb, tag):
    gu, h, a, gc = _ffn_in_mid(xin, g, w_in_t, w8, cb, f"mm_ffn_in{tag}")
    return a, (h, gu, a, gc)


def _ffn_bwd(dx, dxb, xin, g, w_in_t, w8, w_out, saved, tag, scatter_of=None, also=()):
    h, gu, a, gc = saved
    dw_out = _mm_tn(a, dxb, f"mm_ffn_dwout{tag}")
    (dgu, dw8, dcb), got_out = _ffn_mid_bwd(gu, gc, dxb, w_out, w8, f"ffn_mid_bwd{tag}",
                                            comm=scatter_of([dw_out, *also]) if scatter_of else None)
    dw_in_t = _mm_tn(dgu, h, f"mm_ffn_dwin{tag}")
    dxp, dxpb, dg = _mm_rms_bwd([(dgu, w_in_t, 0)], xin, g, dx, f"mm_ffn_dh{tag}")
    return dxp, dxpb, dg, dw_in_t, dw8[:3], dcb, got_out if scatter_of else dw_out


def kernel(x, norm_mix, norm_ffn, attn_w_in, attn_w_out, relpos_table, q_norm_a, k_norm_a, q_norm_b, k_norm_b, sinks, ssm_w_in, ssm_conv_w, ssm_conv_b, ssm_dt_bias, ssm_a_log, ssm_d, ssm_norm, ssm_w_out, ffn_w_in, ffn_conv_w, ffn_conv_b, ffn_w_out, loss_target, m_norm_mix, m_norm_ffn, m_attn_w_in, m_attn_w_out, m_relpos_table, m_q_norm_a, m_k_norm_a, m_q_norm_b, m_k_norm_b, m_sinks, m_ssm_w_in, m_ssm_conv_w, m_ssm_conv_b, m_ssm_dt_bias, m_ssm_a_log, m_ssm_d, m_ssm_norm, m_ssm_w_out, m_ffn_w_in, m_ffn_conv_w, m_ffn_conv_b, m_ffn_w_out, v_norm_mix, v_norm_ffn, v_attn_w_in, v_attn_w_out, v_relpos_table, v_q_norm_a, v_k_norm_a, v_q_norm_b, v_k_norm_b, v_sinks, v_ssm_w_in, v_ssm_conv_w, v_ssm_conv_b, v_ssm_dt_bias, v_ssm_a_log, v_ssm_d, v_ssm_norm, v_ssm_w_out, v_ffn_w_in, v_ffn_conv_w, v_ffn_conv_b, v_ffn_w_out):
    w = dict(norm_mix=norm_mix, norm_ffn=norm_ffn, attn_w_in=attn_w_in, attn_w_out=attn_w_out, relpos_table=relpos_table,
             q_norm_a=q_norm_a, k_norm_a=k_norm_a, q_norm_b=q_norm_b, k_norm_b=k_norm_b, sinks=sinks, ssm_w_in=ssm_w_in,
             ssm_conv_w=ssm_conv_w, ssm_conv_b=ssm_conv_b, ssm_dt_bias=ssm_dt_bias, ssm_a_log=ssm_a_log, ssm_d=ssm_d,
             ssm_norm=ssm_norm, ssm_w_out=ssm_w_out, ffn_w_in=ffn_w_in, ffn_conv_w=ffn_conv_w, ffn_conv_b=ffn_conv_b,
             ffn_w_out=ffn_w_out)
    mom = dict(norm_mix=m_norm_mix, norm_ffn=m_norm_ffn, attn_w_in=m_attn_w_in, attn_w_out=m_attn_w_out,
               relpos_table=m_relpos_table, q_norm_a=m_q_norm_a, k_norm_a=m_k_norm_a, q_norm_b=m_q_norm_b,
               k_norm_b=m_k_norm_b, sinks=m_sinks, ssm_w_in=m_ssm_w_in, ssm_conv_w=m_ssm_conv_w, ssm_conv_b=m_ssm_conv_b,
               ssm_dt_bias=m_ssm_dt_bias, ssm_a_log=m_ssm_a_log, ssm_d=m_ssm_d, ssm_norm=m_ssm_norm, ssm_w_out=m_ssm_w_out,
               ffn_w_in=m_ffn_w_in, ffn_conv_w=m_ffn_conv_w, ffn_conv_b=m_ffn_conv_b, ffn_w_out=m_ffn_w_out)
    var = dict(norm_mix=v_norm_mix, norm_ffn=v_norm_ffn, attn_w_in=v_attn_w_in, attn_w_out=v_attn_w_out,
               relpos_table=v_relpos_table, q_norm_a=v_q_norm_a, k_norm_a=v_k_norm_a, q_norm_b=v_q_norm_b,
               k_norm_b=v_k_norm_b, sinks=v_sinks, ssm_w_in=v_ssm_w_in, ssm_conv_w=v_ssm_conv_w, ssm_conv_b=v_ssm_conv_b,
               ssm_dt_bias=v_ssm_dt_bias, ssm_a_log=v_ssm_a_log, ssm_d=v_ssm_d, ssm_norm=v_ssm_norm, ssm_w_out=v_ssm_w_out,
               ffn_w_in=v_ffn_w_in, ffn_conv_w=v_ffn_conv_w, ffn_conv_b=v_ffn_conv_b, ffn_w_out=v_ffn_w_out)

    def piece(n, l):
        return (w[n][l].T if _AX2[n] == 1 else w[n][l]).astype(BF16)

    def gather_of(names_layers):
        return _Gather2([piece(n, l) for n, l in names_layers])

    def joined(got):
        return [g.reshape(-1, D_MODEL) for g in got]

    first = [('attn_w_in', 0), ('attn_w_out', 0)]
    got = _exchange(_Gather2([piece(n, l) for n, l in first] + [_pack([w[n] for n in _SMALL], F32)]), "gather_attn")
    wt_attn_in, w_attn_out = joined(got[:2])
    full = {}
    for n, g in zip(_SMALL, _unpack(got[2], [w[n].shape for n in _SMALL], lead=(N_DEV,))):
        full[n] = _join(g, _SHARD_AX[n])
    ssm_cw8 = _rows8(full['ssm_conv_w'][0])
    ssm_cb = full['ssm_conv_b']
    ssm_nw = full['ssm_norm']
    ffn_cw8 = [_rows8(full['ffn_conv_w'][l]) for l in range(2)]
    ffn_cb = [ffn_conv_b[l:l + 1] for l in range(2)]

    x0 = x[0]
    target = loss_target[0]
    t = x0.shape[0]

    g_mix0, g_mix1 = norm_mix[0:1], norm_mix[1:2]
    g_ffn0, g_ffn1 = norm_ffn[0:1], norm_ffn[1:2]
    proj, h0 = _rms_mm(x0, g_mix0, wt_attn_in, ATTN_PROJ, "mm_attn_in", F32)
    hn_w = jnp.concatenate([jnp.tile(v, (1, 2)) for v in (q_norm_a, k_norm_a, q_norm_b, k_norm_b)], axis=0)
    qa, kpa, vpa, qb, kpb, vpb = _headnorm_fwd(proj, hn_w, "headnorm")
    table = jnp.pad(relpos_table[0], ((0, 0), (0, REL_W - (2 * MAX_REL + 1))))
    bias_a = jnp.where(_band_mask(A_PREV, PAD_A)[None], jnp.transpose(_relpos_fwd(table, "relpos_bias"), (1, 0, 2)), NEG)
    rel_b = jnp.arange(TQ)[:, None] - (jnp.arange(PAD_B + TQ)[None, :] - PAD_B)
    slopes = 2.0 ** (-8.0 * jnp.arange(1, N_HEADS + 1, dtype=F32) / N_HEADS)
    bias_b = jnp.where(_band_mask(B_PREV, PAD_B)[None], -slopes[:, None, None] * jnp.abs(rel_b).astype(F32)[None], NEG)
    no_sinks = jnp.full((N_HEADS,), NEG, F32)
    ffn0_w, ssm_w, ffn1_w = [('ffn_w_in', 0), ('ffn_w_out', 0)], [('ssm_w_in', 0), ('ssm_w_out', 0)], [('ffn_w_in', 1), ('ffn_w_out', 1)]
    oa, stats_a, got = _attn_fwd(qa, kpa, vpa, bias_a, no_sinks, PAD_A, "attn_a", comm=gather_of(ffn0_w + ssm_w))
    wt_ffn_in0, w_ffn_out0, wt_ssm_in, w_ssm_out = joined(got)
    ob, stats_b, _ = _attn_fwd(qb, kpb, vpb, bias_b, sinks[0], PAD_B, "attn_b")
    wt_ssm_dt = jnp.pad(wt_ssm_in[ZX:], ((0, LANES - SSM_HEADS), (0, 0)))
    x1 = _mm_pair(oa, ob, w_attn_out, x0, "mm_attn_out")
    a0, ffn0_saved = _ffn_fwd(x1, g_ffn0, wt_ffn_in0, ffn_cw8[0], ffn_cb[0], "0")
    x2 = _mm(a0, w_ffn_out0, "mm_ffn_out0", res=x1)

    dt_bias = _lanes128(ssm_dt_bias[0])
    alog = _lanes128(ssm_a_log[0])
    dexp = jnp.repeat(ssm_d[0], HEAD_DIM).reshape(1, D_INNER)
    zx, h2, xbc, conv_pre, dtraw, dt = _ssm_in_pre(x2, g_mix1, wt_ssm_in, wt_ssm_dt, ssm_cw8, ssm_cb, dt_bias, "mm_ssm_in")
    (y, sprev, y4), got = _ssd_fwd(xbc, dt, alog, zx, dexp, ssm_nw, "ssd_fwd", comm=gather_of(ffn1_w))
    wt_ffn_in1, w_ffn_out1 = joined(got)
    x3 = _mm(y4, w_ssm_out, "mm_ssm_out", res=x2)
    a1, ffn1_saved = _ffn_fwd(x3, g_ffn1, wt_ffn_in1, ffn_cw8[1], ffn_cb[1], "1")

    dx4, dx4b, sq = _mm_loss(a1, w_ffn_out1, x3, target, "mm_ffn_out1_loss")
    loss = lax.psum(0.5 * jnp.sum(sq) / D_MODEL, ("x", "y", "c"))

    grads = {}

    def scatter_of(grads_2d):
        return _Comm([g.reshape(N_DEV, -1, D_MODEL) for g in grads_2d], [True] * len(grads_2d))

    dx3, dx3b, dg_ffn1, dwtin1, dcw1, dcb1, dwout1 = _ffn_bwd(
        dx4, dx4b, x3, g_ffn1, wt_ffn_in1, ffn_cw8[1], w_ffn_out1, ffn1_saved, "1")

    dy4 = _mm(dx3b, w_ssm_out, "mm_ssm_dy", trans_b=True)
    dw_ssm_out = _mm_tn(y4, dx3b, "mm_ssm_dwout")
    (dxbc, ddt, dalog, dz, dd_lane, dnw), parts_ffn1 = _ssd_bwd(
        xbc, dt, alog, sprev, dy4, y, zx, dexp, ssm_nw, "ssd_bwd", comm=scatter_of([dwtin1, dwout1, dw_ssm_out]))
    dxr, dcw_s, dcb_s = _ssm_pre_bwd(zx, conv_pre, dxbc, ssm_cw8, "ssm_pre_bwd")
    ddtraw, ddtb = _dt_bwd(dtraw, dt_bias, ddt, "ssm_dt_bwd")
    dwt_ssm_in = jnp.concatenate([
        _mm_tn(dz, h2, "mm_ssm_dwin_z"), _mm_tn(dxr, h2, "mm_ssm_dwin_x"),
        _mm_tn(ddtraw, h2, "mm_ssm_dwin_dt")[:SSM_HEADS]], axis=0)
    dx2, dx2b, dg_mix1 = _mm_rms_bwd([(dz, wt_ssm_in, 0), (dxr, wt_ssm_in, D_INNER), (ddtraw, wt_ssm_dt, 0)],
                                     x2, g_mix1, dx3, "mm_ssm_dh")
    grads['ssm_conv_w'] = dcw_s[:4][None]
    grads['ssm_conv_b'] = dcb_s
    grads['ssm_norm'] = dnw
    grads['ssm_dt_bias'] = ddtb[:, :SSM_HEADS]
    grads['ssm_a_log'] = dalog[:, :SSM_HEADS]
    grads['ssm_d'] = jnp.sum(dd_lane.reshape(SSM_HEADS, HEAD_DIM), axis=1)[None]

    dx1, dx1b, dg_ffn0, dwtin0, dcw0, dcb0, (parts_out0, parts_ssm_in) = _ffn_bwd(
        dx2, dx2b, x1, g_ffn0, wt_ffn_in0, ffn_cw8[0], w_ffn_out0, ffn0_saved, "0", scatter_of=scatter_of, also=[dwt_ssm_in])
    grads['ffn_conv_w'] = jnp.stack([dcw0, dcw1])
    grads['ffn_conv_b'] = jnp.concatenate([dcb0, dcb1], axis=0)
    grads['norm_ffn'] = jnp.concatenate([dg_ffn0, dg_ffn1], axis=0)

    do = _mm(dx1b, w_attn_out, "mm_attn_do", out_dtype=BF16, trans_b=True)
    dw_attn_out = jnp.concatenate([_mm_tn(oa, dx1b, "mm_attn_dwout_a"), _mm_tn(ob, dx1b, "mm_attn_dwout_b")], axis=0)
    (dqa, dkpa, dvpa, dbias_a, _), parts_attn_out = _attn_bwd(
        qa, kpa, vpa, bias_a, no_sinks, do, stats_a, oa, 0, PAD_A, "attn_a_bwd", comm=scatter_of([dw_attn_out, dwtin0]))
    parts_ffn0 = [parts_attn_out[1], parts_out0]
    (dqb, dkpb, dvpb, _, dsink), _ = _attn_bwd(qb, kpb, vpb, bias_b, sinks[0], do, stats_b, ob, 4, PAD_B, "attn_b_bwd")
    grads['relpos_table'] = _relpos_bwd(jnp.transpose(dbias_a, (1, 0, 2)), "relpos_bwd")[None, :, :2 * MAX_REL + 1]
    grads['sinks'] = dsink[:, :2, 0].reshape(1, N_HEADS)
    dproj, dhn = _headnorm_bwd(proj, hn_w, dqa, dkpa, dvpa, dqb, dkpb, dvpb, "headnorm_bwd")
    dhn = dhn[:, :HEAD_DIM] + dhn[:, HEAD_DIM:]
    for k, n in enumerate(('q_norm_a', 'k_norm_a', 'q_norm_b', 'k_norm_b')):
        grads[n] = dhn[k:k + 1]
    dwt_attn_in = _mm_tn(dproj, h0, "mm_attn_dwin")
    dx0, _, dg_mix0, parts_attn_in = _mm_rms_bwd([(dproj, wt_attn_in, 0)], x0, g_mix0, dx1, "mm_attn_dh",
                                                 comm=scatter_of([dwt_attn_in]))
    grads['norm_mix'] = jnp.concatenate([dg_mix0, dg_mix1], axis=0)

    def summed_t(parts, name):
        return _sum_parts(parts, name).T[None]

    sm_shapes = [w[n].shape for n in _SMALL]
    rp_shapes = [w[n].shape for n in _REPL]
    recv = _exchange(_Comm(
        [_pack([_split(grads[n], _SHARD_AX[n]) for n in _SMALL], F32, lead=(N_DEV,)), _pack([grads[n] for n in _REPL], F32)],
        [True, False]), "exchange_small")
    big_parts = {
        'attn_w_in': [summed_t(parts_attn_in[0], "sum_attn_w_in")], 'attn_w_out': [parts_attn_out[0]],
        'ssm_w_in': [summed_t(parts_ssm_in, "sum_ssm_w_in")], 'ssm_w_out': [parts_ffn1[2]],
        'ffn_w_in': [summed_t(parts_ffn0[0], "sum_ffn_w_in0"), summed_t(parts_ffn1[0], "sum_ffn_w_in1")],
        'ffn_w_out': [parts_ffn0[1], parts_ffn1[1]],
    }
    res = [{}, {}, {}, {}]
    for n in _BIG:
        for kind, a in enumerate(_adamw(big_parts[n], w[n], mom[n], var[n], f"adamw_{n}")):
            res[kind][n] = a
    for names, shapes, parts in ((_SMALL, sm_shapes, recv[0]), (_REPL, rp_shapes, recv[1])):
        outs = _adamw([parts], _pack([w[n] for n in names], F32)[None], _pack([mom[n] for n in names], F32)[None],
                      _pack([var[n] for n in names], F32)[None], "adamw_" + ("small" if names is _SMALL else "replicated"))
        for kind, flat in enumerate(outs):
            for n, a in zip(names, _unpack(flat[0], shapes)):
                res[kind][n] = a
    return (loss, dx0[None], *[res[0][n] for n in _WEIGHTS], *[res[1][n] for n in _WEIGHTS],
            *[res[2][n] for n in _WEIGHTS], *[res[3][n] for n in _WEIGHTS])
```
